```python
import jax, jax.numpy as jnp
from jax import lax
import numpy as np

D_MODEL = 1024
BATCH = 8
SEQ = 4096
DEPTH = 4

N_MIXERS = 2
N_LAYERS_A = (DEPTH + 1) // 2
N_LAYERS_B = DEPTH // 2
CHUNK = 128
GMLP_WIDTH = 2 * D_MODEL
GMLP_GROUPS = 8
GMLP_GROUP_DIM = GMLP_WIDTH // GMLP_GROUPS
HEAD_DIM = 64
N_Q_HEADS = D_MODEL // HEAD_DIM
N_KV_HEADS = 4
GQA_GROUP = N_Q_HEADS // N_KV_HEADS
WINDOW = 128
ATTN_BLOCK = 128
ROPE_DIM = HEAD_DIM // 4
ROPE_THETA = 500000.0
Q_WIDTH = N_Q_HEADS * HEAD_DIM
KV_WIDTH = N_KV_HEADS * HEAD_DIM
QKV_WIDTH = Q_WIDTH + 2 * KV_WIDTH
FFN_HIDDEN = -(-(8 * D_MODEL) // (3 * 256)) * 256
RMS_EPS = 1e-6
LN_EPS = 1e-5
NEG_INF = -1e30

kernel_name = "hybrid_gmlp_swa_sink_sandwich"


def rmsnorm(x, g):
    xf = x.astype(jnp.float32)
    y = xf * lax.rsqrt(jnp.mean(xf * xf, axis=-1, keepdims=True) + RMS_EPS)
    return y.astype(x.dtype) * g


def layernorm(x, g, b):
    xf = x.astype(jnp.float32)
    mu = jnp.mean(xf, axis=-1, keepdims=True)
    var = jnp.mean(jnp.square(xf - mu), axis=-1, keepdims=True)
    return ((xf - mu) * lax.rsqrt(var + LN_EPS)).astype(x.dtype) * g + b


def rope_tables(positions):
    inv_freq = ROPE_THETA ** (-jnp.arange(0, ROPE_DIM, 2, dtype=jnp.float32) / ROPE_DIM)
    ang = positions.astype(jnp.float32)[..., None] * inv_freq
    return jnp.cos(ang)[:, :, None, :], jnp.sin(ang)[:, :, None, :]


def apply_partial_rope(x, cos, sin):
    half = ROPE_DIM // 2
    cos = cos.astype(x.dtype)
    sin = sin.astype(x.dtype)
    x1, x2, rest = x[..., :half], x[..., half:ROPE_DIM], x[..., ROPE_DIM:]
    return jnp.concatenate([x1 * cos - x2 * sin, x2 * cos + x1 * sin, rest], axis=-1)


def gmlp_mixer(h, w_in, b_in, ln_g, ln_b, w_s, b_s, w_out):
    B, S, _ = h.shape
    nc = S // CHUNK
    z = jax.nn.gelu(h @ w_in + b_in, approximate=False)
    u, v = jnp.split(z, 2, axis=-1)
    v = layernorm(v, ln_g, ln_b)
    v = v.reshape(B, nc, CHUNK, GMLP_GROUPS, GMLP_GROUP_DIM)
    causal = jnp.tril(jnp.ones((CHUNK, CHUNK), dtype=bool))
    w = jnp.where(causal[None], w_s, 0.0)
    sv = jnp.einsum("gts,bnsgc->bntgc", w, v) + b_s.T[None, None, :, :, None]
    gated = u * sv.reshape(B, S, GMLP_WIDTH)
    return gated @ w_out


def swa_sink_mixer(h, cos, sin, w_qkv, b_qkv, sinks, w_o):
    B, S, _ = h.shape
    nb = S // ATTN_BLOCK
    qkv = h @ w_qkv + b_qkv
    q = qkv[..., :Q_WIDTH].reshape(B, S, N_Q_HEADS, HEAD_DIM)
    k = qkv[..., Q_WIDTH:Q_WIDTH + KV_WIDTH].reshape(B, S, N_KV_HEADS, HEAD_DIM)
    v = qkv[..., Q_WIDTH + KV_WIDTH:].reshape(B, S, N_KV_HEADS, HEAD_DIM)
    q = apply_partial_rope(q, cos, sin) * (HEAD_DIM ** -0.5)
    k = apply_partial_rope(k, cos, sin)
    qb = q.reshape(B, nb, ATTN_BLOCK, N_KV_HEADS, GQA_GROUP, HEAD_DIM)
    kb = k.reshape(B, nb, ATTN_BLOCK, N_KV_HEADS, HEAD_DIM)
    vb = v.reshape(B, nb, ATTN_BLOCK, N_KV_HEADS, HEAD_DIM)
    kk = jnp.concatenate([jnp.concatenate([jnp.zeros_like(kb[:, :1]), kb[:, :-1]], axis=1), kb], axis=2)
    vv = jnp.concatenate([jnp.concatenate([jnp.zeros_like(vb[:, :1]), vb[:, :-1]], axis=1), vb], axis=2)
    s = jnp.einsum("bnqkgd,bnskd->bnkgqs", qb, kk).astype(jnp.float32)
    qi = jnp.arange(ATTN_BLOCK)[:, None]
    sj = jnp.arange(2 * ATTN_BLOCK)[None, :]
    diff = ATTN_BLOCK + qi - sj
    band = (diff >= 0) & (diff < WINDOW)
    exists = (jnp.arange(nb)[:, None, None] > 0) | (sj >= ATTN_BLOCK)[None]
    valid = band[None] & exists
    s = jnp.where(valid[None, :, None, None], s, NEG_INF)
    sink = sinks.astype(jnp.float32).reshape(N_KV_HEADS, GQA_GROUP)[None, None, :, :, None, None]
    m = jnp.maximum(jnp.max(s, axis=-1, keepdims=True), sink)
    p = jnp.exp(s - m)
    denom = jnp.sum(p, axis=-1, keepdims=True) + jnp.exp(sink - m)
    p = (p / denom).astype(vv.dtype)
    o = jnp.einsum("bnkgqs,bnskd->bnqkgd", p, vv).reshape(B, S, Q_WIDTH)
    return o @ w_o


def swiglu_ffn(h, w_gu, w_down):
    g, up = jnp.split(h @ w_gu, 2, axis=-1)
    return (jax.nn.silu(g) * up) @ w_down


def _fwd_setup_inputs(seed: int = 0) -> dict:
    key = jax.random.key(seed)
    ks = jax.random.split(key, 20)
    f32 = jnp.float32
    nrm = lambda k, shape, scale: jax.random.normal(k, shape, f32) * scale
    x = jax.random.normal(ks[0], (BATCH, SEQ, D_MODEL), f32)
    positions = jnp.broadcast_to(jnp.arange(SEQ, dtype=jnp.int32)[None, :], (BATCH, SEQ))
    gain = lambda k: 1.0 + nrm(k, (DEPTH, D_MODEL), 0.1)
    return {
        "x": x,
        "positions": positions,
        "pre_mix_g": gain(ks[1]),
        "post_mix_g": gain(ks[2]),
        "pre_ffn_g": gain(ks[3]),
        "post_ffn_g": gain(ks[4]),
        "a_w_in": nrm(ks[5], (N_LAYERS_A, D_MODEL, 2 * GMLP_WIDTH), D_MODEL ** -0.5),
        "a_b_in": nrm(ks[6], (N_LAYERS_A, 2 * GMLP_WIDTH), 0.01),
        "a_ln_g": 1.0 + nrm(ks[7], (N_LAYERS_A, GMLP_WIDTH), 0.1),
        "a_ln_b": nrm(ks[8], (N_LAYERS_A, GMLP_WIDTH), 0.01),
        "a_w_s": nrm(ks[9], (N_LAYERS_A, GMLP_GROUPS, CHUNK, CHUNK), 0.5 * CHUNK ** -0.5),
        "a_b_s": 1.0 + nrm(ks[10], (N_LAYERS_A, GMLP_GROUPS, CHUNK), 0.1),
        "a_w_out": nrm(ks[11], (N_LAYERS_A, GMLP_WIDTH, D_MODEL), GMLP_WIDTH ** -0.5),
        "b_w_qkv": nrm(ks[12], (N_LAYERS_B, D_MODEL, QKV_WIDTH), D_MODEL ** -0.5),
        "b_b_qkv": nrm(ks[13], (N_LAYERS_B, QKV_WIDTH), 0.01),
        "b_sinks": nrm(ks[14], (N_LAYERS_B, N_Q_HEADS), 1.0),
        "b_w_o": nrm(ks[15], (N_LAYERS_B, Q_WIDTH, D_MODEL), Q_WIDTH ** -0.5),
        "ffn_w_gu": nrm(ks[16], (DEPTH, D_MODEL, 2 * FFN_HIDDEN), D_MODEL ** -0.5),
        "ffn_w_down": nrm(ks[17], (DEPTH, FFN_HIDDEN, D_MODEL), FFN_HIDDEN ** -0.5),
    }


def _fwd_reference(x, positions, pre_mix_g, post_mix_g, pre_ffn_g, post_ffn_g,
              a_w_in, a_b_in, a_ln_g, a_ln_b, a_w_s, a_b_s, a_w_out,
              b_w_qkv, b_b_qkv, b_sinks, b_w_o, ffn_w_gu, ffn_w_down):
    cos, sin = rope_tables(positions)
    h = x
    for i in range(DEPTH):
        j = i // N_MIXERS
        hn = rmsnorm(h, pre_mix_g[i])
        if i % N_MIXERS == 0:
            mix = gmlp_mixer(hn, a_w_in[j], a_b_in[j], a_ln_g[j], a_ln_b[j],
                             a_w_s[j], a_b_s[j], a_w_out[j])
        else:
            mix = swa_sink_mixer(hn, cos, sin, b_w_qkv[j], b_b_qkv[j], b_sinks[j], b_w_o[j])
        h = h + rmsnorm(mix, post_mix_g[i])
        f = swiglu_ffn(rmsnorm(h, pre_ffn_g[i]), ffn_w_gu[i], ffn_w_down[i])
        h = h + rmsnorm(f, post_ffn_g[i])
    return h


import jax as _jax
import jax.numpy as _jnp

TWIN_FORMAT = 'train_step'
FWD_PARAMS = ['x', 'positions', 'pre_mix_g', 'post_mix_g', 'pre_ffn_g', 'post_ffn_g', 'a_w_in', 'a_b_in', 'a_ln_g', 'a_ln_b', 'a_w_s', 'a_b_s', 'a_w_out', 'b_w_qkv', 'b_b_qkv', 'b_sinks', 'b_w_o', 'ffn_w_gu', 'ffn_w_down']
TWIN_WEIGHTS = ['pre_mix_g', 'post_mix_g', 'pre_ffn_g', 'post_ffn_g', 'a_w_in', 'a_b_in', 'a_ln_g', 'a_ln_b', 'a_w_s', 'a_b_s', 'a_w_out', 'b_w_qkv', 'b_b_qkv', 'b_sinks', 'b_w_o', 'ffn_w_gu', 'ffn_w_down']
TWIN_DIFF_INPUT = 'x'
TWIN_INPUTS = ['x', 'positions', 'pre_mix_g', 'post_mix_g', 'pre_ffn_g', 'post_ffn_g', 'a_w_in', 'a_b_in', 'a_ln_g', 'a_ln_b', 'a_w_s', 'a_b_s', 'a_w_out', 'b_w_qkv', 'b_b_qkv', 'b_sinks', 'b_w_o', 'ffn_w_gu', 'ffn_w_down', 'loss_target', 'm_pre_mix_g', 'm_post_mix_g', 'm_pre_ffn_g', 'm_post_ffn_g', 'm_a_w_in', 'm_a_b_in', 'm_a_ln_g', 'm_a_ln_b', 'm_a_w_s', 'm_a_b_s', 'm_a_w_out', 'm_b_w_qkv', 'm_b_b_qkv', 'm_b_sinks', 'm_b_w_o', 'm_ffn_w_gu', 'm_ffn_w_down', 'v_pre_mix_g', 'v_post_mix_g', 'v_pre_ffn_g', 'v_post_ffn_g', 'v_a_w_in', 'v_a_b_in', 'v_a_ln_g', 'v_a_ln_b', 'v_a_w_s', 'v_a_b_s', 'v_a_w_out', 'v_b_w_qkv', 'v_b_b_qkv', 'v_b_sinks', 'v_b_w_o', 'v_ffn_w_gu', 'v_ffn_w_down']
TWIN_OUTPUTS = ['loss', 'grad_x', 'grad_pre_mix_g', 'grad_post_mix_g', 'grad_pre_ffn_g', 'grad_post_ffn_g', 'grad_a_w_in', 'grad_a_b_in', 'grad_a_ln_g', 'grad_a_ln_b', 'grad_a_w_s', 'grad_a_b_s', 'grad_a_w_out', 'grad_b_w_qkv', 'grad_b_b_qkv', 'grad_b_sinks', 'grad_b_w_o', 'grad_ffn_w_gu', 'grad_ffn_w_down', 'delta_pre_mix_g', 'delta_post_mix_g', 'delta_pre_ffn_g', 'delta_post_ffn_g', 'delta_a_w_in', 'delta_a_b_in', 'delta_a_ln_g', 'delta_a_ln_b', 'delta_a_w_s', 'delta_a_b_s', 'delta_a_w_out', 'delta_b_w_qkv', 'delta_b_b_qkv', 'delta_b_sinks', 'delta_b_w_o', 'delta_ffn_w_gu', 'delta_ffn_w_down', 'new_m_pre_mix_g', 'new_m_post_mix_g', 'new_m_pre_ffn_g', 'new_m_post_ffn_g', 'new_m_a_w_in', 'new_m_a_b_in', 'new_m_a_ln_g', 'new_m_a_ln_b', 'new_m_a_w_s', 'new_m_a_b_s', 'new_m_a_w_out', 'new_m_b_w_qkv', 'new_m_b_b_qkv', 'new_m_b_sinks', 'new_m_b_w_o', 'new_m_ffn_w_gu', 'new_m_ffn_w_down', 'new_v_pre_mix_g', 'new_v_post_mix_g', 'new_v_pre_ffn_g', 'new_v_post_ffn_g', 'new_v_a_w_in', 'new_v_a_b_in', 'new_v_a_ln_g', 'new_v_a_ln_b', 'new_v_a_w_s', 'new_v_a_b_s', 'new_v_a_w_out', 'new_v_b_w_qkv', 'new_v_b_b_qkv', 'new_v_b_sinks', 'new_v_b_w_o', 'new_v_ffn_w_gu', 'new_v_ffn_w_down']
TWIN_LEAF_KINDS = {'loss': 'loss', 'grad_x': 'grad_x', 'grad_pre_mix_g': 'grad_w', 'grad_post_mix_g': 'grad_w', 'grad_pre_ffn_g': 'grad_w', 'grad_post_ffn_g': 'grad_w', 'grad_a_w_in': 'grad_w', 'grad_a_b_in': 'grad_w', 'grad_a_ln_g': 'grad_w', 'grad_a_ln_b': 'grad_w', 'grad_a_w_s': 'grad_w', 'grad_a_b_s': 'grad_w', 'grad_a_w_out': 'grad_w', 'grad_b_w_qkv': 'grad_w', 'grad_b_b_qkv': 'grad_w', 'grad_b_sinks': 'grad_w', 'grad_b_w_o': 'grad_w', 'grad_ffn_w_gu': 'grad_w', 'grad_ffn_w_down': 'grad_w', 'delta_pre_mix_g': 'delta_w', 'delta_post_mix_g': 'delta_w', 'delta_pre_ffn_g': 'delta_w', 'delta_post_ffn_g': 'delta_w', 'delta_a_w_in': 'delta_w', 'delta_a_b_in': 'delta_w', 'delta_a_ln_g': 'delta_w', 'delta_a_ln_b': 'delta_w', 'delta_a_w_s': 'delta_w', 'delta_a_b_s': 'delta_w', 'delta_a_w_out': 'delta_w', 'delta_b_w_qkv': 'delta_w', 'delta_b_b_qkv': 'delta_w', 'delta_b_sinks': 'delta_w', 'delta_b_w_o': 'delta_w', 'delta_ffn_w_gu': 'delta_w', 'delta_ffn_w_down': 'delta_w', 'new_m_pre_mix_g': 'new_m', 'new_m_post_mix_g': 'new_m', 'new_m_pre_ffn_g': 'new_m', 'new_m_post_ffn_g': 'new_m', 'new_m_a_w_in': 'new_m', 'new_m_a_b_in': 'new_m', 'new_m_a_ln_g': 'new_m', 'new_m_a_ln_b': 'new_m', 'new_m_a_w_s': 'new_m', 'new_m_a_b_s': 'new_m', 'new_m_a_w_out': 'new_m', 'new_m_b_w_qkv': 'new_m', 'new_m_b_b_qkv': 'new_m', 'new_m_b_sinks': 'new_m', 'new_m_b_w_o': 'new_m', 'new_m_ffn_w_gu': 'new_m', 'new_m_ffn_w_down': 'new_m', 'new_v_pre_mix_g': 'new_v', 'new_v_post_mix_g': 'new_v', 'new_v_pre_ffn_g': 'new_v', 'new_v_post_ffn_g': 'new_v', 'new_v_a_w_in': 'new_v', 'new_v_a_b_in': 'new_v', 'new_v_a_ln_g': 'new_v', 'new_v_a_ln_b': 'new_v', 'new_v_a_w_s': 'new_v', 'new_v_a_b_s': 'new_v', 'new_v_a_w_out': 'new_v', 'new_v_b_w_qkv': 'new_v', 'new_v_b_b_qkv': 'new_v', 'new_v_b_sinks': 'new_v', 'new_v_b_w_o': 'new_v', 'new_v_ffn_w_gu': 'new_v', 'new_v_ffn_w_down': 'new_v'}


def _forward(args):
    return _fwd_reference(*[args[k] for k in FWD_PARAMS])


def _output_shape():
    out = _jax.eval_shape(lambda: _forward(_fwd_setup_inputs(0)))
    return out.shape, out.dtype

N_MICROBATCH = 1
ADAM_LR = 0.001
ADAM_B1 = 0.9
ADAM_B2 = 0.999
ADAM_EPS = 1e-08
ADAM_WD = 0.01
ADAM_STEP = 10
PER_EXAMPLE_BATCH_AXIS = {'x': 0, 'positions': 0, 'loss_target': 0}
SHARED_INPUTS = []
_WEIGHT_DTYPES = {'pre_mix_g': _jnp.float32, 'post_mix_g': _jnp.float32, 'pre_ffn_g': _jnp.float32, 'post_ffn_g': _jnp.float32, 'a_w_in': _jnp.float32, 'a_b_in': _jnp.float32, 'a_ln_g': _jnp.float32, 'a_ln_b': _jnp.float32, 'a_w_s': _jnp.float32, 'a_b_s': _jnp.float32, 'a_w_out': _jnp.float32, 'b_w_qkv': _jnp.float32, 'b_b_qkv': _jnp.float32, 'b_sinks': _jnp.float32, 'b_w_o': _jnp.float32, 'ffn_w_gu': _jnp.float32, 'ffn_w_down': _jnp.float32}
MOMENT_SCALE = {'pre_mix_g': 3.239153e+01, 'post_mix_g': 6.147424e+01, 'pre_ffn_g': 2.094263e+01, 'post_ffn_g': 3.629049e+01, 'a_w_in': 7.532515e+00, 'a_b_in': 3.961732e+01, 'a_ln_g': 3.983726e-01, 'a_ln_b': 1.180465e+00, 'a_w_s': 1.138735e+00, 'a_b_s': 2.631888e+00, 'a_w_out': 4.640568e+01, 'b_w_qkv': 3.583123e+01, 'b_b_qkv': 1.187841e+02, 'b_sinks': 1.613573e+00, 'b_w_o': 4.419769e+01, 'ffn_w_gu': 9.163334e+00, 'ffn_w_down': 1.666140e+01}


def _to_microbatches(a, axis):
    t = _jnp.moveaxis(a, axis, 0)
    t = t.reshape((N_MICROBATCH, t.shape[0] // N_MICROBATCH) + t.shape[1:])
    return _jnp.moveaxis(t, 1, axis + 1)


def setup_inputs(seed: int = 0) -> dict:
    inp = _fwd_setup_inputs(seed)
    key = _jax.random.fold_in(_jax.random.key(seed), 7919)
    shape, _ = _output_shape()
    out = dict(inp)
    out["loss_target"] = _jax.random.normal(_jax.random.fold_in(key, 0), shape, _jnp.float32)
    for i, name in enumerate(TWIN_WEIGHTS):
        w = inp[name].astype(_jnp.float32)
        if MOMENT_SCALE is None:
            s = _jnp.sqrt(_jnp.mean(_jnp.square(w)) + 1e-30)
        else:
            s = MOMENT_SCALE[name]
        km, kv = _jax.random.split(_jax.random.fold_in(key, i + 1))
        out[name] = w
        out["m_" + name] = s * _jax.random.normal(km, w.shape, _jnp.float32)
        out["v_" + name] = (s * s) * _jax.random.uniform(kv, w.shape, _jnp.float32, 0.5, 1.5)
    if N_MICROBATCH > 1:
        for name, axis in PER_EXAMPLE_BATCH_AXIS.items():
            out[name] = _to_microbatches(out[name], axis)
    return {'x': out['x'], 'positions': out['positions'], 'pre_mix_g': out['pre_mix_g'], 'post_mix_g': out['post_mix_g'], 'pre_ffn_g': out['pre_ffn_g'], 'post_ffn_g': out['post_ffn_g'], 'a_w_in': out['a_w_in'], 'a_b_in': out['a_b_in'], 'a_ln_g': out['a_ln_g'], 'a_ln_b': out['a_ln_b'], 'a_w_s': out['a_w_s'], 'a_b_s': out['a_b_s'], 'a_w_out': out['a_w_out'], 'b_w_qkv': out['b_w_qkv'], 'b_b_qkv': out['b_b_qkv'], 'b_sinks': out['b_sinks'], 'b_w_o': out['b_w_o'], 'ffn_w_gu': out['ffn_w_gu'], 'ffn_w_down': out['ffn_w_down'], 'loss_target': out['loss_target'], 'm_pre_mix_g': out['m_pre_mix_g'], 'm_post_mix_g': out['m_post_mix_g'], 'm_pre_ffn_g': out['m_pre_ffn_g'], 'm_post_ffn_g': out['m_post_ffn_g'], 'm_a_w_in': out['m_a_w_in'], 'm_a_b_in': out['m_a_b_in'], 'm_a_ln_g': out['m_a_ln_g'], 'm_a_ln_b': out['m_a_ln_b'], 'm_a_w_s': out['m_a_w_s'], 'm_a_b_s': out['m_a_b_s'], 'm_a_w_out': out['m_a_w_out'], 'm_b_w_qkv': out['m_b_w_qkv'], 'm_b_b_qkv': out['m_b_b_qkv'], 'm_b_sinks': out['m_b_sinks'], 'm_b_w_o': out['m_b_w_o'], 'm_ffn_w_gu': out['m_ffn_w_gu'], 'm_ffn_w_down': out['m_ffn_w_down'], 'v_pre_mix_g': out['v_pre_mix_g'], 'v_post_mix_g': out['v_post_mix_g'], 'v_pre_ffn_g': out['v_pre_ffn_g'], 'v_post_ffn_g': out['v_post_ffn_g'], 'v_a_w_in': out['v_a_w_in'], 'v_a_b_in': out['v_a_b_in'], 'v_a_ln_g': out['v_a_ln_g'], 'v_a_ln_b': out['v_a_ln_b'], 'v_a_w_s': out['v_a_w_s'], 'v_a_b_s': out['v_a_b_s'], 'v_a_w_out': out['v_a_w_out'], 'v_b_w_qkv': out['v_b_w_qkv'], 'v_b_b_qkv': out['v_b_b_qkv'], 'v_b_sinks': out['v_b_sinks'], 'v_b_w_o': out['v_b_w_o'], 'v_ffn_w_gu': out['v_ffn_w_gu'], 'v_ffn_w_down': out['v_ffn_w_down']}


def _loss(weights, diff, rest, loss_target):
    with _jax.named_scope("forward"):
        args = {**rest, TWIN_DIFF_INPUT: diff, **{k: w.astype(_WEIGHT_DTYPES[k]) for k, w in weights.items()}}
        y = _forward(args)
    with _jax.named_scope("loss_head"):
        err = _jnp.square(y.astype(_jnp.float32) - loss_target)
        return 0.5 * _jnp.sum(_jnp.mean(err, axis=-1)) if err.ndim else 0.5 * err


def _adamw(w, g, m, v):
    m = ADAM_B1 * m + (1.0 - ADAM_B1) * g
    v = ADAM_B2 * v + (1.0 - ADAM_B2) * _jnp.square(g)
    m_hat = m / (1.0 - ADAM_B1 ** ADAM_STEP)
    v_hat = v / (1.0 - ADAM_B2 ** ADAM_STEP)
    delta = -ADAM_LR * (m_hat / (_jnp.sqrt(v_hat) + ADAM_EPS) + ADAM_WD * w)
    return delta, m, v


def reference(x, positions, pre_mix_g, post_mix_g, pre_ffn_g, post_ffn_g, a_w_in, a_b_in, a_ln_g, a_ln_b, a_w_s, a_b_s, a_w_out, b_w_qkv, b_b_qkv, b_sinks, b_w_o, ffn_w_gu, ffn_w_down, loss_target, m_pre_mix_g, m_post_mix_g, m_pre_ffn_g, m_post_ffn_g, m_a_w_in, m_a_b_in, m_a_ln_g, m_a_ln_b, m_a_w_s, m_a_b_s, m_a_w_out, m_b_w_qkv, m_b_b_qkv, m_b_sinks, m_b_w_o, m_ffn_w_gu, m_ffn_w_down, v_pre_mix_g, v_post_mix_g, v_pre_ffn_g, v_post_ffn_g, v_a_w_in, v_a_b_in, v_a_ln_g, v_a_ln_b, v_a_w_s, v_a_b_s, v_a_w_out, v_b_w_qkv, v_b_b_qkv, v_b_sinks, v_b_w_o, v_ffn_w_gu, v_ffn_w_down):
    given = dict(x=x, positions=positions, pre_mix_g=pre_mix_g, post_mix_g=post_mix_g, pre_ffn_g=pre_ffn_g, post_ffn_g=post_ffn_g, a_w_in=a_w_in, a_b_in=a_b_in, a_ln_g=a_ln_g, a_ln_b=a_ln_b, a_w_s=a_w_s, a_b_s=a_b_s, a_w_out=a_w_out, b_w_qkv=b_w_qkv, b_b_qkv=b_b_qkv, b_sinks=b_sinks, b_w_o=b_w_o, ffn_w_gu=ffn_w_gu, ffn_w_down=ffn_w_down, loss_target=loss_target, m_pre_mix_g=m_pre_mix_g, m_post_mix_g=m_post_mix_g, m_pre_ffn_g=m_pre_ffn_g, m_post_ffn_g=m_post_ffn_g, m_a_w_in=m_a_w_in, m_a_b_in=m_a_b_in, m_a_ln_g=m_a_ln_g, m_a_ln_b=m_a_ln_b, m_a_w_s=m_a_w_s, m_a_b_s=m_a_b_s, m_a_w_out=m_a_w_out, m_b_w_qkv=m_b_w_qkv, m_b_b_qkv=m_b_b_qkv, m_b_sinks=m_b_sinks, m_b_w_o=m_b_w_o, m_ffn_w_gu=m_ffn_w_gu, m_ffn_w_down=m_ffn_w_down, v_pre_mix_g=v_pre_mix_g, v_post_mix_g=v_post_mix_g, v_pre_ffn_g=v_pre_ffn_g, v_post_ffn_g=v_post_ffn_g, v_a_w_in=v_a_w_in, v_a_b_in=v_a_b_in, v_a_ln_g=v_a_ln_g, v_a_ln_b=v_a_ln_b, v_a_w_s=v_a_w_s, v_a_b_s=v_a_b_s, v_a_w_out=v_a_w_out, v_b_w_qkv=v_b_w_qkv, v_b_b_qkv=v_b_b_qkv, v_b_sinks=v_b_sinks, v_b_w_o=v_b_w_o, v_ffn_w_gu=v_ffn_w_gu, v_ffn_w_down=v_ffn_w_down)
    weights = {n: given[n] for n in TWIN_WEIGHTS}
    shared = {n: given[n] for n in SHARED_INPUTS}
    per_example = {n: given[n] for n in ['x', 'positions']}
    grad_fn = _jax.value_and_grad(_loss, argnums=(0, 1))

    def one_microbatch(ex, loss_target):
        ex = dict(ex)
        diff = ex.pop(TWIN_DIFF_INPUT)
        return grad_fn(weights, diff, {**shared, **ex}, loss_target)

    if N_MICROBATCH == 1:
        loss, (grad_w, grad_x) = one_microbatch(per_example, given["loss_target"])
    else:
        def body(carry, xs):
            loss_sum, grad_sum = carry
            l_k, (gw_k, gx_k) = one_microbatch(xs[0], xs[1])
            with _jax.named_scope("update"):
                return (loss_sum + l_k, _jax.tree.map(_jnp.add, grad_sum, gw_k)), gx_k

        init = (_jnp.zeros((), _jnp.float32), _jax.tree.map(_jnp.zeros_like, weights))
        (loss, grad_w), grad_x = _jax.lax.scan(body, init, (per_example, given["loss_target"]))
    with _jax.named_scope("update"):
        delta_w, new_m, new_v = {}, {}, {}
        for n in TWIN_WEIGHTS:
            delta_w[n], new_m[n], new_v[n] = _adamw(weights[n], grad_w[n], given["m_" + n], given["v_" + n])
    return (loss, grad_x, *[grad_w[n] for n in TWIN_WEIGHTS], *[delta_w[n] for n in TWIN_WEIGHTS],
            *[new_m[n] for n in TWIN_WEIGHTS], *[new_v[n] for n in TWIN_WEIGHTS])
```

```python
import functools
import math

import jax
import jax.numpy as jnp
import numpy as np
from jax import lax
from jax.experimental import pallas as pl
from jax.experimental.pallas import tpu as pltpu

F32 = jnp.float32
BF16 = jnp.bfloat16
MESH = pl.DeviceIdType.MESH

HEAD_DIM = 64
N_KV_HEADS = 4
ROPE_DIM = 16
ROPE_THETA = 500000.0
CHUNK = 128
GMLP_GROUPS = 8
RMS_EPS = 1e-6
LN_EPS = 1e-5
NEG_INF = -1e30
ADAM_LR = 0.001
ADAM_B1 = 0.9
ADAM_B2 = 0.999
ADAM_EPS = 1e-08
ADAM_WD = 0.01
ADAM_STEP = 10

N_CHIPS = 4
LANES = 128
VMEM_CAP = 56 * 1024 * 1024


def _vmem(est_bytes):
    return int(min(max(int(est_bytes * 1.3) + (4 << 20), 32 << 20), VMEM_CAP))


def _pick(n, cands):
    for c in cands:
        if c <= n and n % c == 0:
            return c
    return n


def _nbytes(shape, dtype):
    return int(np.prod(shape)) * jnp.dtype(dtype).itemsize


def _matmul(a, b, *, mode, out_dtype, name, a_l=None, b_l=None, bias=None, into=None, o_l=None,
            tp=None, tq=None, tr=None):
    a2 = a.shape[-2:]
    b2 = b.shape[-2:]
    if mode == "nn":
        (P, R), (R2, Q) = a2, b2
    elif mode == "nt":
        (P, R), (Q, R2) = a2, b2
    else:
        (R, P), (R2, Q) = a2, b2
    assert R == R2, (mode, a.shape, b.shape)
    tp = tp or _pick(P, (1024, 1408, 512, 384, 256, 128))
    tq = tq or _pick(Q, (1024, 1408, 768, 512, 384, 256, 128))
    tr = tr or _pick(R, (2048, 1408, 1024, 512, 256, 128))
    assert P % tp == 0 and Q % tq == 0 and R % tr == 0
    nk = R // tr
    dims = {"nn": (((1,), (0,)), ((), ())), "nt": (((1,), (1,)), ((), ())), "tn": (((0,), (0,)), ((), ()))}[mode]

    def lead(l, blk, idx):
        if l is None:
            return pl.BlockSpec(blk, idx)
        return pl.BlockSpec((None,) + blk, lambda i, j, k: (l,) + idx(i, j, k))

    if mode == "nn":
        a_spec = lead(a_l, (tp, tr), lambda i, j, k: (i, k))
        b_spec = lead(b_l, (tr, tq), lambda i, j, k: (k, j))
    elif mode == "nt":
        a_spec = lead(a_l, (tp, tr), lambda i, j, k: (i, k))
        b_spec = lead(b_l, (tq, tr), lambda i, j, k: (j, k))
    else:
        a_spec = lead(a_l, (tr, tp), lambda i, j, k: (k, i))
        b_spec = lead(b_l, (tr, tq), lambda i, j, k: (k, j))
    in_specs = [a_spec, b_spec]
    args = [a, b]
    if bias is not None:
        in_specs.append(pl.BlockSpec((1, tq), lambda i, j, k: (0, j)))
        args.append(bias)
    aliases = {}
    if into is not None:
        in_specs.append(pl.BlockSpec(memory_space=pl.ANY))
        args.append(into)
        aliases = {len(args) - 1: 0}
        out_shape = jax.ShapeDtypeStruct(into.shape, into.dtype)
        out_dtype = into.dtype
        out_spec = pl.BlockSpec((None, tp, tq), lambda i, j, k: (o_l, i, j))
    else:
        out_shape = jax.ShapeDtypeStruct((P, Q), out_dtype)
        out_spec = pl.BlockSpec((tp, tq), lambda i, j, k: (i, j))
    has_bias = bias is not None
    has_into = into is not None

    def body(*refs):
        a_ref, b_ref = refs[0], refs[1]
        pos = 2
        bias_ref = None
        if has_bias:
            bias_ref = refs[pos]
            pos += 1
        if has_into:
            pos += 1
        o_ref = refs[pos]
        acc_ref = refs[pos + 1] if nk > 1 else None
        part = lax.dot_general(a_ref[...], b_ref[...], dims, preferred_element_type=F32)

        def finish(acc):
            if has_bias:
                acc = acc + bias_ref[...]
            o_ref[...] = acc.astype(out_dtype)

        if nk == 1:
            finish(part)
        else:
            k = pl.program_id(2)

            @pl.when(k == 0)
            def _():
                acc_ref[...] = part

            @pl.when(k > 0)
            def _():
                acc_ref[...] += part

            @pl.when(k == nk - 1)
            def _():
                finish(acc_ref[...])

    est = 2 * (_nbytes((tp, tr), a.dtype) + _nbytes((tr, tq), b.dtype) + _nbytes((tp, tq), out_dtype)) + 3 * tp * tq * 4
    return pl.pallas_call(
        body, name=name, out_shape=out_shape,
        grid=(P // tp, Q // tq, nk),
        in_specs=in_specs, out_specs=out_spec,
        scratch_shapes=[pltpu.VMEM((tp, tq), F32)] if nk > 1 else [],
        input_output_aliases=aliases,
        compiler_params=pltpu.CompilerParams(
            dimension_semantics=("parallel", "parallel", "arbitrary"), vmem_limit_bytes=_vmem(est)),
    )(*args)


def _row_call(body, ins, outs, *, name, rows, tr, acc_outs=(), est=0):
    in_specs = []
    for arr, kind in ins:
        if kind == "row":
            in_specs.append(pl.BlockSpec((tr, arr.shape[1]), lambda i: (i, 0)))
        else:
            nd = arr.ndim
            in_specs.append(pl.BlockSpec(arr.shape, lambda i, nd=nd: (0,) * nd))
    out_shapes = [jax.ShapeDtypeStruct(s, d) for s, d in outs] + [jax.ShapeDtypeStruct(s, d) for s, d in acc_outs]
    out_specs = [pl.BlockSpec((tr, s[1]), lambda i: (i, 0)) for s, _ in outs]
    out_specs += [pl.BlockSpec(s, lambda i, nd=len(s): (0,) * nd) for s, _ in acc_outs]
    res = pl.pallas_call(
        body, name=name, out_shape=out_shapes, grid=(rows // tr,), in_specs=in_specs, out_specs=out_specs,
        compiler_params=pltpu.CompilerParams(dimension_semantics=("arbitrary",), vmem_limit_bytes=_vmem(est)),
    )(*[a for a, _ in ins])
    return res


def _rms_fwd(x, g, *, out_dtype, name):
    T, D = x.shape
    tr = _pick(T, (512, 256, 128))

    def body(x_ref, g_ref, o_ref):
        xv = x_ref[...]
        r = lax.rsqrt(jnp.mean(xv * xv, axis=-1, keepdims=True) + RMS_EPS)
        o_ref[...] = (xv * r * g_ref[...]).astype(out_dtype)

    return _row_call(body, [(x, "row"), (g, "full")], [((T, D), out_dtype)], name=name, rows=T, tr=tr,
                     est=8 * tr * D * 4)[0]


def _rms_res(h, y, g, *, name):
    T, D = h.shape
    tr = _pick(T, (512, 256, 128))

    def body(h_ref, y_ref, g_ref, o_ref):
        yv = y_ref[...]
        r = lax.rsqrt(jnp.mean(yv * yv, axis=-1, keepdims=True) + RMS_EPS)
        o_ref[...] = h_ref[...] + yv * r * g_ref[...]

    return _row_call(body, [(h, "row"), (y, "row"), (g, "full")], [((T, D), F32)], name=name, rows=T, tr=tr,
                     est=10 * tr * D * 4)[0]


def _rms_bwd(x, g, dy, dres, *, out_dtype, name):
    T, D = x.shape
    tr = _pick(T, (512, 256, 128))
    has_res = dres is not None

    def body(*refs):
        if has_res:
            x_ref, g_ref, dy_ref, dr_ref, dx_ref, dg_ref = refs
        else:
            x_ref, g_ref, dy_ref, dx_ref, dg_ref = refs
        xv = x_ref[...]
        r = lax.rsqrt(jnp.mean(xv * xv, axis=-1, keepdims=True) + RMS_EPS)
        xhat = xv * r
        dyv = dy_ref[...].astype(F32)
        dxn = dyv * g_ref[...]
        dx = r * (dxn - xhat * jnp.mean(dxn * xhat, axis=-1, keepdims=True))
        if has_res:
            dx = dx + dr_ref[...]
        dx_ref[...] = dx.astype(out_dtype)
        part = jnp.sum(dyv * xhat, axis=0, keepdims=True)

        @pl.when(pl.program_id(0) == 0)
        def _():
            dg_ref[...] = part

        @pl.when(pl.program_id(0) > 0)
        def _():
            dg_ref[...] += part

    ins = [(x, "row"), (g, "full"), (dy, "row")] + ([(dres, "row")] if has_res else [])
    dx, dg = _row_call(body, ins, [((T, D), out_dtype)], name=name, rows=T, tr=tr, acc_outs=[((1, D), F32)],
                       est=12 * tr * D * 4)
    return dx, dg


def _silu_fwd(gu, *, name):
    T, W2 = gu.shape
    W = W2 // 2
    tr = _pick(T, (256, 128))

    def body(gu_ref, o_ref):
        gv = gu_ref[:, :W]
        uv = gu_ref[:, W:]
        o_ref[...] = (gv * jax.nn.sigmoid(gv) * uv).astype(BF16)

    return _row_call(body, [(gu, "row")], [((T, W), BF16)], name=name, rows=T, tr=tr, est=6 * tr * W2 * 4)[0]


def _silu_bwd(gu, dact, *, name):
    T, W2 = gu.shape
    W = W2 // 2
    tr = _pick(T, (256, 128))

    def body(gu_ref, da_ref, o_ref):
        gv = gu_ref[:, :W]
        uv = gu_ref[:, W:]
        da = da_ref[...].astype(F32)
        sg = jax.nn.sigmoid(gv)
        silu = gv * sg
        o_ref[:, :W] = (da * uv * (sg + silu * (1.0 - sg))).astype(BF16)
        o_ref[:, W:] = (da * silu).astype(BF16)

    return _row_call(body, [(gu, "row"), (dact, "row")], [((T, W2), BF16)], name=name, rows=T, tr=tr,
                     est=8 * tr * W2 * 4)[0]


def _loss_and_grad(y, target, *, name):
    T, D = y.shape
    tr = _pick(T, (512, 256, 128))

    def body(y_ref, t_ref, dy_ref, l_ref):
        e = y_ref[...] - t_ref[...]
        dy_ref[...] = e * (1.0 / D)
        part = jnp.sum(jnp.sum(e * e, axis=1, keepdims=True), axis=0, keepdims=True) * (0.5 / D)

        @pl.when(pl.program_id(0) == 0)
        def _():
            l_ref[...] = part

        @pl.when(pl.program_id(0) > 0)
        def _():
            l_ref[...] += part

    dy, l = _row_call(body, [(y, "row"), (target, "row")], [((T, D), F32)], name=name, rows=T, tr=tr,
                      acc_outs=[((1, 1), F32)], est=8 * tr * D * 4)
    return dy, l


_SQRT_HALF = 0.7071067811865476
_INV_SQRT_2PI = 0.3989422804014327


def _gelu_parts(x):
    cdf = 0.5 * (1.0 + lax.erf(x * _SQRT_HALF))
    return cdf


def _sgu_common(pre, lng, lnb, W):
    cdf = _gelu_parts(pre)
    z = pre * cdf
    u = z[:, :W]
    v = z[:, W:]
    mu = jnp.mean(v, axis=-1, keepdims=True)
    vc = v - mu
    var = jnp.mean(vc * vc, axis=-1, keepdims=True)
    rstd = lax.rsqrt(var + LN_EPS)
    vhat = vc * rstd
    vn = vhat * lng + lnb
    return cdf, u, vhat, rstd, vn


def _causal_mask():
    t = lax.broadcasted_iota(jnp.int32, (CHUNK, CHUNK), 0)
    s = lax.broadcasted_iota(jnp.int32, (CHUNK, CHUNK), 1)
    return t >= s


def _sgu_fwd(pre, lng, lnb, ws, bsT, *, name):
    T, W2 = pre.shape
    W = W2 // 2
    G = ws.shape[0]
    gd = W // G

    def body(pre_ref, lng_ref, lnb_ref, ws_ref, bs_ref, o_ref):
        _, u, _, _, vn = _sgu_common(pre_ref[...], lng_ref[...], lnb_ref[...], W)
        vnb = vn.astype(BF16)
        causal = _causal_mask()
        for g in range(G):
            w = jnp.where(causal, ws_ref[g], 0.0).astype(BF16)
            sv = jnp.dot(w, vnb[:, g * gd:(g + 1) * gd], preferred_element_type=F32) + bs_ref[:, g:g + 1]
            o_ref[:, g * gd:(g + 1) * gd] = (u[:, g * gd:(g + 1) * gd] * sv).astype(BF16)

    return pl.pallas_call(
        body, name=name, out_shape=jax.ShapeDtypeStruct((T, W), BF16), grid=(T // CHUNK,),
        in_specs=[pl.BlockSpec((CHUNK, W2), lambda i: (i, 0)),
                  pl.BlockSpec((1, W), lambda i: (0, 0)), pl.BlockSpec((1, W), lambda i: (0, 0)),
                  pl.BlockSpec(ws.shape, lambda i: (0, 0, 0)), pl.BlockSpec(bsT.shape, lambda i: (0, 0))],
        out_specs=pl.BlockSpec((CHUNK, W), lambda i: (i, 0)),
        compiler_params=pltpu.CompilerParams(dimension_semantics=("arbitrary",),
                                             vmem_limit_bytes=_vmem(12 * CHUNK * W2 * 4)),
    )(pre, lng, lnb, ws, bsT)


def _sgu_bwd(pre, dgated, lng, lnb, ws, bsT, *, name):
    T, W2 = pre.shape
    W = W2 // 2
    G = ws.shape[0]
    gd = W // G

    def body(pre_ref, dgt_ref, lng_ref, lnb_ref, ws_ref, bs_ref,
             dpre_ref, dws_ref, dbs_ref, dlng_ref, dlnb_ref, dbin_ref):
        first = pl.program_id(0) == 0

        @pl.when(first)
        def _():
            dws_ref[...] = jnp.zeros_like(dws_ref)
            dbs_ref[...] = jnp.zeros_like(dbs_ref)
            dlng_ref[...] = jnp.zeros_like(dlng_ref)
            dlnb_ref[...] = jnp.zeros_like(dlnb_ref)
            dbin_ref[...] = jnp.zeros_like(dbin_ref)

        pre_v = pre_ref[...]
        lng_v = lng_ref[...]
        cdf, u, vhat, rstd, vn = _sgu_common(pre_v, lng_v, lnb_ref[...], W)
        vnb = vn.astype(BF16)
        dgt = dgt_ref[...].astype(F32)
        causal = _causal_mask()
        du_parts, dvn_parts = [], []
        for g in range(G):
            sl = slice(g * gd, (g + 1) * gd)
            w = jnp.where(causal, ws_ref[g], 0.0).astype(BF16)
            sv = jnp.dot(w, vnb[:, sl], preferred_element_type=F32) + bs_ref[:, g:g + 1]
            dgt_g = dgt[:, sl]
            du_parts.append(dgt_g * sv)
            dsv = dgt_g * u[:, sl]
            dsvb = dsv.astype(BF16)
            dvn_parts.append(lax.dot_general(w, dsvb, (((0,), (0,)), ((), ())), preferred_element_type=F32))
            dw = lax.dot_general(dsvb, vnb[:, sl], (((1,), (1,)), ((), ())), preferred_element_type=F32)
            dws_ref[g] += jnp.where(causal, dw, 0.0)
            dbs_ref[:, g:g + 1] += jnp.sum(dsv, axis=1, keepdims=True)
        du = jnp.concatenate(du_parts, axis=1)
        dvn = jnp.concatenate(dvn_parts, axis=1)
        dlng_ref[...] += jnp.sum(dvn * vhat, axis=0, keepdims=True)
        dlnb_ref[...] += jnp.sum(dvn, axis=0, keepdims=True)
        dvh = dvn * lng_v
        dv = rstd * (dvh - jnp.mean(dvh, axis=-1, keepdims=True)
                     - vhat * jnp.mean(dvh * vhat, axis=-1, keepdims=True))
        dz = jnp.concatenate([du, dv], axis=1)
        dgelu = cdf + pre_v * jnp.exp(-0.5 * pre_v * pre_v) * _INV_SQRT_2PI
        dpre = dz * dgelu
        dbin_ref[...] += jnp.sum(dpre, axis=0, keepdims=True)
        dpre_ref[...] = dpre.astype(BF16)

    full = lambda shape: pl.BlockSpec(shape, lambda i, nd=len(shape): (0,) * nd)
    return pl.pallas_call(
        body, name=name,
        out_shape=[jax.ShapeDtypeStruct((T, W2), BF16), jax.ShapeDtypeStruct(ws.shape, F32),
                   jax.ShapeDtypeStruct(bsT.shape, F32), jax.ShapeDtypeStruct((1, W), F32),
                   jax.ShapeDtypeStruct((1, W), F32), jax.ShapeDtypeStruct((1, W2), F32)],
        grid=(T // CHUNK,),
        in_specs=[pl.BlockSpec((CHUNK, W2), lambda i: (i, 0)), pl.BlockSpec((CHUNK, W), lambda i: (i, 0)),
                  full((1, W)), full((1, W)), full(ws.shape), full(bsT.shape)],
        out_specs=[pl.BlockSpec((CHUNK, W2), lambda i: (i, 0)), full(ws.shape), full(bsT.shape),
                   full((1, W)), full((1, W)), full((1, W2))],
        compiler_params=pltpu.CompilerParams(dimension_semantics=("arbitrary",),
                                             vmem_limit_bytes=_vmem(24 * CHUNK * W2 * 4)),
    )(pre, dgated, lng, lnb, ws, bsT)


def _rope_tables(positions):
    half = ROPE_DIM // 2
    inv_freq = ROPE_THETA ** (-jnp.arange(0, ROPE_DIM, 2, dtype=F32) / ROPE_DIM)
    ang = positions.astype(F32).reshape(-1, 1) * inv_freq
    cos, sin = jnp.cos(ang), jnp.sin(ang)
    T = ang.shape[0]
    rest = HEAD_DIM - ROPE_DIM
    c64 = jnp.concatenate([cos, cos, jnp.ones((T, rest), F32)], axis=1)
    s64 = jnp.concatenate([-sin, sin, jnp.zeros((T, rest), F32)], axis=1)
    del half
    return jnp.tile(c64, (1, LANES // HEAD_DIM)), jnp.tile(s64, (1, LANES // HEAD_DIM))


def _swap8(x):
    W = x.shape[1]
    half = ROPE_DIM // 2
    lane = lax.broadcasted_iota(jnp.int32, x.shape, 1) % HEAD_DIM
    return jnp.where(lane < half, pltpu.roll(x, W - half, axis=1),
                     jnp.where(lane < ROPE_DIM, pltpu.roll(x, half, axis=1), 0.0))


def _wide(tab, W):
    return jnp.concatenate([tab] * (W // LANES), axis=1) if W > LANES else tab


def _rope_fwd(qkv, ctab, stab, *, q_width, kv_width, name):
    T = qkv.shape[0]
    tr = _pick(T, (256, 128))
    scale = HEAD_DIM ** -0.5

    def body(x_ref, c_ref, s_ref, q_ref, k_ref, v_ref):
        c = c_ref[...]
        s = s_ref[...]
        q = x_ref[:, :q_width]
        k = x_ref[:, q_width:q_width + kv_width]
        q_ref[...] = ((q * _wide(c, q_width) + _swap8(q) * _wide(s, q_width)) * scale).astype(BF16)
        k_ref[...] = (k * _wide(c, kv_width) + _swap8(k) * _wide(s, kv_width)).astype(BF16)
        v_ref[...] = x_ref[:, q_width + kv_width:].astype(BF16)

    return _row_call(body, [(qkv, "row"), (ctab, "row"), (stab, "row")],
                     [((T, q_width), BF16), ((T, kv_width), BF16), ((T, kv_width), BF16)],
                     name=name, rows=T, tr=tr, est=10 * tr * qkv.shape[1] * 4)


def _attn_masks(n):
    qi = lax.broadcasted_iota(jnp.int32, (CHUNK, CHUNK), 0)
    sj = lax.broadcasted_iota(jnp.int32, (CHUNK, CHUNK), 1)
    return (sj > qi) & (n > 0), sj <= qi


_NT = (((1,), (1,)), ((), ()))
_TN = (((0,), (0,)), ((), ()))


def _attn_probs(q, kp, kc, sink, prev_ok, cur_ok):
    sp = jnp.where(prev_ok, lax.dot_general(q, kp, _NT, preferred_element_type=F32), NEG_INF)
    sc = jnp.where(cur_ok, lax.dot_general(q, kc, _NT, preferred_element_type=F32), NEG_INF)
    m = jnp.maximum(jnp.maximum(jnp.max(sp, axis=1, keepdims=True), jnp.max(sc, axis=1, keepdims=True)), sink)
    pp = jnp.exp(sp - m)
    pc = jnp.exp(sc - m)
    ps = jnp.exp(sink - m)
    inv = 1.0 / (jnp.sum(pp, axis=1, keepdims=True) + jnp.sum(pc, axis=1, keepdims=True) + ps)
    return pp * inv, pc * inv, ps * inv


def _kv_specs(width, nb):
    prev = pl.BlockSpec((CHUNK, width), lambda n: (jnp.maximum(n - 1, 0), 0))
    cur = pl.BlockSpec((CHUNK, width), lambda n: (n, 0))
    return prev, cur


def _attn_fwd(qr, kr, vr, sinks, *, name):
    T, QW = qr.shape
    KW = kr.shape[1]
    HQ, HK = QW // HEAD_DIM, KW // HEAD_DIM
    grp = HQ // HK
    nb = T // CHUNK

    def body(q_ref, kp_ref, kc_ref, vp_ref, vc_ref, s_ref, o_ref):
        prev_ok, cur_ok = _attn_masks(pl.program_id(0))
        for h in range(HQ):
            qs = slice(h * HEAD_DIM, (h + 1) * HEAD_DIM)
            ks = slice((h // grp) * HEAD_DIM, (h // grp + 1) * HEAD_DIM)
            pp, pc, _ = _attn_probs(q_ref[:, qs], kp_ref[:, ks], kc_ref[:, ks], s_ref[0, h], prev_ok, cur_ok)
            o = (jnp.dot(pp.astype(BF16), vp_ref[:, ks], preferred_element_type=F32)
                 + jnp.dot(pc.astype(BF16), vc_ref[:, ks], preferred_element_type=F32))
            o_ref[:, qs] = o.astype(BF16)

    kp, kc = _kv_specs(KW, nb)
    return pl.pallas_call(
        body, name=name, out_shape=jax.ShapeDtypeStruct((T, QW), BF16), grid=(nb,),
        in_specs=[pl.BlockSpec((CHUNK, QW), lambda n: (n, 0)), kp, kc, kp, kc,
                  pl.BlockSpec(memory_space=pltpu.SMEM)],
        out_specs=pl.BlockSpec((CHUNK, QW), lambda n: (n, 0)),
        compiler_params=pltpu.CompilerParams(dimension_semantics=("arbitrary",), vmem_limit_bytes=_vmem(8 << 20)),
    )(qr, kr, kr, vr, vr, sinks)


def _attn_bwd(qr, kr, vr, sinks, do, *, name):
    T, QW = qr.shape
    KW = kr.shape[1]
    HQ, HK = QW // HEAD_DIM, KW // HEAD_DIM
    grp = HQ // HK
    nb = T // CHUNK

    def body(q_ref, kp_ref, kc_ref, vp_ref, vc_ref, s_ref, do_ref,
             dq_ref, dkp_ref, dkc_ref, dvp_ref, dvc_ref, ds_ref):
        n = pl.program_id(0)
        prev_ok, cur_ok = _attn_masks(n)
        lane = lax.broadcasted_iota(jnp.int32, (1, LANES), 1)
        dsink = jnp.zeros((1, LANES), F32)
        for kh in range(HK):
            ks = slice(kh * HEAD_DIM, (kh + 1) * HEAD_DIM)
            kp, kc, vp, vc = kp_ref[:, ks], kc_ref[:, ks], vp_ref[:, ks], vc_ref[:, ks]
            dkp = jnp.zeros((CHUNK, HEAD_DIM), F32)
            dkc = jnp.zeros((CHUNK, HEAD_DIM), F32)
            dvp = jnp.zeros((CHUNK, HEAD_DIM), F32)
            dvc = jnp.zeros((CHUNK, HEAD_DIM), F32)
            for h in range(kh * grp, (kh + 1) * grp):
                qs = slice(h * HEAD_DIM, (h + 1) * HEAD_DIM)
                q = q_ref[:, qs]
                doh = do_ref[:, qs]
                pp, pc, ps = _attn_probs(q, kp, kc, s_ref[0, h], prev_ok, cur_ok)
                dpp = lax.dot_general(doh, vp, _NT, preferred_element_type=F32)
                dpc = lax.dot_general(doh, vc, _NT, preferred_element_type=F32)
                delta = jnp.sum(pp * dpp, axis=1, keepdims=True) + jnp.sum(pc * dpc, axis=1, keepdims=True)
                dsp = (pp * (dpp - delta)).astype(BF16)
                dsc = (pc * (dpc - delta)).astype(BF16)
                dsink = dsink + jnp.where(lane == h, -jnp.sum(ps * delta, axis=0, keepdims=True), 0.0)
                dvp = dvp + lax.dot_general(pp.astype(BF16), doh, _TN, preferred_element_type=F32)
                dvc = dvc + lax.dot_general(pc.astype(BF16), doh, _TN, preferred_element_type=F32)
                dq_ref[:, qs] = (jnp.dot(dsp, kp, preferred_element_type=F32)
                                 + jnp.dot(dsc, kc, preferred_element_type=F32))
                dkp = dkp + lax.dot_general(dsp, q, _TN, preferred_element_type=F32)
                dkc = dkc + lax.dot_general(dsc, q, _TN, preferred_element_type=F32)
            dkp_ref[:, ks] = dkp
            dkc_ref[:, ks] = dkc
            dvp_ref[:, ks] = dvp
            dvc_ref[:, ks] = dvc

        @pl.when(n == 0)
        def _():
            ds_ref[...] = dsink

        @pl.when(n > 0)
        def _():
            ds_ref[...] += dsink

    kp, kc = _kv_specs(KW, nb)
    qspec = pl.BlockSpec((CHUNK, QW), lambda n: (n, 0))
    kout = pl.BlockSpec((CHUNK, KW), lambda n: (n, 0))
    return pl.pallas_call(
        body, name=name,
        out_shape=[jax.ShapeDtypeStruct((T, QW), F32)] + [jax.ShapeDtypeStruct((T, KW), F32)] * 4
        + [jax.ShapeDtypeStruct((1, LANES), F32)],
        grid=(nb,),
        in_specs=[qspec, kp, kc, kp, kc, pl.BlockSpec(memory_space=pltpu.SMEM), qspec],
        out_specs=[qspec, kout, kout, kout, kout, pl.BlockSpec((1, LANES), lambda n: (0, 0))],
        compiler_params=pltpu.CompilerParams(dimension_semantics=("arbitrary",), vmem_limit_bytes=_vmem(12 << 20)),
    )(qr, kr, kr, vr, vr, sinks, do)


def _rope_bwd(dq, dkp, dkc, dvp, dvc, ctab, stab, *, name):
    T, QW = dq.shape
    KW = dkp.shape[1]
    nb = T // CHUNK
    scale = HEAD_DIM ** -0.5
    width = QW + 2 * KW

    def body(dq_ref, dkc_ref, dkn_ref, dvc_ref, dvn_ref, c_ref, s_ref, o_ref, db_ref):
        n = pl.program_id(0)
        c = c_ref[...]
        s = s_ref[...]
        has_next = (n < nb - 1).astype(F32)
        dqv = dq_ref[...]
        dk = dkc_ref[...] + has_next * dkn_ref[...]
        dv = dvc_ref[...] + has_next * dvn_ref[...]
        dq_pre = (dqv * _wide(c, QW) + _swap8(dqv * _wide(s, QW))) * scale
        dk_pre = dk * _wide(c, KW) + _swap8(dk * _wide(s, KW))
        o_ref[:, :QW] = dq_pre.astype(BF16)
        o_ref[:, QW:QW + KW] = dk_pre.astype(BF16)
        o_ref[:, QW + KW:] = dv.astype(BF16)
        part = jnp.concatenate([jnp.sum(dq_pre, axis=0, keepdims=True), jnp.sum(dk_pre, axis=0, keepdims=True),
                                jnp.sum(dv, axis=0, keepdims=True)], axis=1)

        @pl.when(n == 0)
        def _():
            db_ref[...] = part

        @pl.when(n > 0)
        def _():
            db_ref[...] += part

    cur = lambda w: pl.BlockSpec((CHUNK, w), lambda n: (n, 0))
    nxt = lambda w: pl.BlockSpec((CHUNK, w), lambda n: (jnp.minimum(n + 1, nb - 1), 0))
    return pl.pallas_call(
        body, name=name,
        out_shape=[jax.ShapeDtypeStruct((T, width), BF16), jax.ShapeDtypeStruct((1, width), F32)],
        grid=(nb,),
        in_specs=[cur(QW), cur(KW), nxt(KW), cur(KW), nxt(KW), cur(LANES), cur(LANES)],
        out_specs=[cur(width), pl.BlockSpec((1, width), lambda n: (0, 0))],
        compiler_params=pltpu.CompilerParams(dimension_semantics=("arbitrary",), vmem_limit_bytes=_vmem(8 << 20)),
    )(dq, dkc, dkp, dvc, dvp, ctab, stab)


def _cast_bf16(w, *, name):
    L, K, N = w.shape
    tk = _pick(K, (512, 352, 256, 128))

    def body(w_ref, o_ref):
        o_ref[...] = w_ref[...].astype(BF16)

    spec = pl.BlockSpec((None, tk, N), lambda l, i: (l, i, 0))
    return pl.pallas_call(
        body, name=name, out_shape=jax.ShapeDtypeStruct(w.shape, BF16), grid=(L, K // tk),
        in_specs=[spec], out_specs=spec,
        compiler_params=pltpu.CompilerParams(dimension_semantics=("parallel", "parallel"),
                                             vmem_limit_bytes=_vmem(4 * tk * N * 6)),
    )(w)


def _adamw_math(w, g, m, v):
    m = ADAM_B1 * m + (1.0 - ADAM_B1) * g
    v = ADAM_B2 * v + (1.0 - ADAM_B2) * (g * g)
    m_hat = m / (1.0 - ADAM_B1 ** ADAM_STEP)
    v_hat = v / (1.0 - ADAM_B2 ** ADAM_STEP)
    delta = -ADAM_LR * (m_hat / (jnp.sqrt(v_hat) + ADAM_EPS) + ADAM_WD * w)
    return delta, m, v


def _adamw_big(w, g, m, v, *, name):
    L, K, N = w.shape
    tk = _pick(K, (256, 176, 128))

    def body(w_ref, g_ref, m_ref, v_ref, d_ref, mo_ref, vo_ref):
        d, mn, vn = _adamw_math(w_ref[...], g_ref[...], m_ref[...], v_ref[...])
        d_ref[...] = d
        mo_ref[...] = mn
        vo_ref[...] = vn

    spec = pl.BlockSpec((None, tk, N), lambda l, i: (l, i, 0))
    sd = jax.ShapeDtypeStruct(w.shape, F32)
    return pl.pallas_call(
        body, name=name, out_shape=[sd, sd, sd], grid=(L, K // tk),
        in_specs=[spec] * 4, out_specs=[spec] * 3,
        compiler_params=pltpu.CompilerParams(dimension_semantics=("parallel", "parallel"),
                                             vmem_limit_bytes=_vmem(2 * 7 * tk * N * 4 + 6 * tk * N * 4)),
    )(w, g, m, v)


def _adamw_small(w, g, m, v, *, name):
    def body(w_ref, g_ref, m_ref, v_ref, d_ref, mo_ref, vo_ref):
        d, mn, vn = _adamw_math(w_ref[...], g_ref[...], m_ref[...], v_ref[...])
        d_ref[...] = d
        mo_ref[...] = mn
        vo_ref[...] = vn

    sd = jax.ShapeDtypeStruct(w.shape, F32)
    return pl.pallas_call(body, name=name, out_shape=[sd, sd, sd])(w, g, m, v)


def _my_place():
    return lax.axis_index("x"), lax.axis_index("y"), lax.axis_index("c")


def _shard_region(ref, axis, chip, layer0, n_layers, size):
    if axis == 1:
        assert size % 16 == 0
        return ref.at[pl.ds(layer0, n_layers), pl.ds(pl.multiple_of(chip * size, 16), size), :]
    assert size % LANES == 0
    return ref.at[pl.ds(layer0, n_layers), :, pl.ds(pl.multiple_of(chip * size, LANES), size)]


def _allgather_weights(shards, axes):
    n = len(shards)
    fulls = []
    for s, ax in zip(shards, axes):
        shp = list(s.shape)
        shp[ax] *= N_CHIPS
        fulls.append(jax.ShapeDtypeStruct(tuple(shp), s.dtype))

    def body(*refs):
        ins, outs = refs[:n], refs[n:2 * n]
        send_sems, recv_sems, fsend_sems, frecv_sems, loc_sems = refs[2 * n:]
        x, y, c = _my_place()
        me_chip = 2 * x + y
        peers = [(1 - x, y), (x, 1 - y), (1 - x, 1 - y)]

        def region(w, chip, half):
            L = ins[w].shape[0]
            lh = L // 2
            return _shard_region(outs[w], axes[w], chip, half * lh, lh, ins[w].shape[axes[w]])

        local = []
        for w in range(n):
            L = ins[w].shape[0]
            cp = pltpu.make_async_copy(ins[w], _shard_region(outs[w], axes[w], me_chip, 0, L, ins[w].shape[axes[w]]),
                                       loc_sems.at[w])
            cp.start()
            local.append(cp)
        sends = []
        for j, (px, py) in enumerate(peers):
            for w in range(n):
                lh = ins[w].shape[0] // 2
                cp = pltpu.make_async_remote_copy(
                    src_ref=ins[w].at[pl.ds(c * lh, lh)], dst_ref=region(w, me_chip, c),
                    send_sem=send_sems.at[j * n + w], recv_sem=recv_sems.at[j * n + w],
                    device_id=(px, py, c), device_id_type=MESH)
                cp.start()
                sends.append(cp)
        fwds = []
        for j, (px, py) in enumerate(peers):
            pchip = 2 * px + py
            for w in range(n):
                got = region(w, pchip, c)
                pltpu.make_async_remote_copy(
                    src_ref=got, dst_ref=got, send_sem=send_sems.at[j * n + w], recv_sem=recv_sems.at[j * n + w],
                    device_id=(px, py, c), device_id_type=MESH).wait_recv()
                cp = pltpu.make_async_remote_copy(
                    src_ref=got, dst_ref=got, send_sem=fsend_sems.at[j * n + w], recv_sem=frecv_sems.at[j * n + w],
                    device_id=(x, y, 1 - c), device_id_type=MESH)
                cp.start()
                fwds.append(cp)
        for j, (px, py) in enumerate(peers):
            pchip = 2 * px + py
            for w in range(n):
                got = region(w, pchip, 1 - c)
                pltpu.make_async_remote_copy(
                    src_ref=got, dst_ref=got, send_sem=fsend_sems.at[j * n + w], recv_sem=frecv_sems.at[j * n + w],
                    device_id=(x, y, 1 - c), device_id_type=MESH).wait_recv()
        for cp in sends + fwds:
            cp.wait_send()
        for cp in local:
            cp.wait()

    any_spec = pl.BlockSpec(memory_space=pl.ANY)
    return pl.pallas_call(
        body, name="allgather_weights", out_shape=fulls,
        in_specs=[any_spec] * n, out_specs=[any_spec] * n,
        scratch_shapes=[pltpu.SemaphoreType.DMA((3 * n,))] * 4 + [pltpu.SemaphoreType.DMA((n,))],
    )(*shards)


def _sibling_exchange_halves(grads):
    n = len(grads)
    outs = [jax.ShapeDtypeStruct((g.shape[0] // 2,) + g.shape[1:], g.dtype) for g in grads]

    def body(*refs):
        ins, rs = refs[:n], refs[n:2 * n]
        send_sems, recv_sems = refs[2 * n:]
        x, y, c = _my_place()
        cps = []
        for w in range(n):
            lh = ins[w].shape[0] // 2
            cp = pltpu.make_async_remote_copy(
                src_ref=ins[w].at[pl.ds((1 - c) * lh, lh)], dst_ref=rs[w],
                send_sem=send_sems.at[w], recv_sem=recv_sems.at[w],
                device_id=(x, y, 1 - c), device_id_type=MESH)
            cp.start()
            cps.append(cp)
        for cp in cps:
            cp.wait()

    any_spec = pl.BlockSpec(memory_space=pl.ANY)
    return pl.pallas_call(
        body, name="grad_sibling_exchange", out_shape=outs,
        in_specs=[any_spec] * n, out_specs=[any_spec] * n,
        scratch_shapes=[pltpu.SemaphoreType.DMA((n,))] * 2,
    )(*grads)


def _chip_sum(g, r, c_arr, *, name):
    L, K, N = g.shape
    lh = L // 2
    tk = _pick(K, (256, 128))

    def body(c_ref, g_ref, r_ref, f_ref, b_ref):
        s = g_ref[...] + r_ref[...]
        f_ref[...] = s
        b_ref[...] = s.astype(BF16)

    half = pl.BlockSpec((None, tk, N), lambda l, i, c_ref: (l, i, 0))
    grid_spec = pltpu.PrefetchScalarGridSpec(
        num_scalar_prefetch=1, grid=(lh, K // tk),
        in_specs=[pl.BlockSpec((None, tk, N), lambda l, i, c_ref: (c_ref[0] * lh + l, i, 0)), half],
        out_specs=[half, half])
    return pl.pallas_call(
        body, name=name,
        out_shape=[jax.ShapeDtypeStruct((lh, K, N), F32), jax.ShapeDtypeStruct((lh, K, N), BF16)],
        grid_spec=grid_spec,
        compiler_params=pltpu.CompilerParams(dimension_semantics=("arbitrary", "arbitrary"),
                                             vmem_limit_bytes=_vmem(2 * tk * N * 14)),
    )(c_arr, g, r)


def _chip_exchange(sums_bf16, axes):
    n = len(sums_bf16)
    outs = []
    for s, ax in zip(sums_bf16, axes):
        shp = list(s.shape)
        shp[ax] //= N_CHIPS
        outs.append(jax.ShapeDtypeStruct((3,) + tuple(shp), s.dtype))

    def body(*refs):
        ins, rs = refs[:n], refs[n:2 * n]
        send_sems, recv_sems = refs[2 * n:]
        x, y, c = _my_place()
        peers = [(1 - x, y), (x, 1 - y), (1 - x, 1 - y)]
        cps = []
        for j, (px, py) in enumerate(peers):
            for w in range(n):
                lh = ins[w].shape[0]
                size = ins[w].shape[axes[w]] // N_CHIPS
                cp = pltpu.make_async_remote_copy(
                    src_ref=_shard_region(ins[w], axes[w], 2 * px + py, 0, lh, size), dst_ref=rs[w].at[j],
                    send_sem=send_sems.at[j * n + w], recv_sem=recv_sems.at[j * n + w],
                    device_id=(px, py, c), device_id_type=MESH)
                cp.start()
                cps.append(cp)
        for cp in cps:
            cp.wait()

    any_spec = pl.BlockSpec(memory_space=pl.ANY)
    return pl.pallas_call(
        body, name="grad_chip_exchange", out_shape=outs,
        in_specs=[any_spec] * n, out_specs=[any_spec] * n,
        scratch_shapes=[pltpu.SemaphoreType.DMA((3 * n,))] * 2,
    )(*sums_bf16)


def _final_sum(own, recv, axis, place, *, name):
    lh, K, N = own.shape
    _, _, Ks, Ns = recv.shape
    tk = _pick(Ks, (256, 176, 128))
    nk = Ks // tk

    def body(p_ref, o_ref, r_ref, out_ref):
        out_ref[...] = ((o_ref[...] + r_ref[0].astype(F32)) + r_ref[1].astype(F32)) + r_ref[2].astype(F32)

    if axis == 1:
        own_spec = pl.BlockSpec((None, tk, Ns), lambda l, i, p: (l, p[0] * nk + i, 0))
    else:
        own_spec = pl.BlockSpec((None, tk, Ns), lambda l, i, p: (l, i, p[0]))
    grid_spec = pltpu.PrefetchScalarGridSpec(
        num_scalar_prefetch=1, grid=(lh, nk),
        in_specs=[own_spec, pl.BlockSpec((3, None, tk, Ns), lambda l, i, p: (0, l, i, 0))],
        out_specs=pl.BlockSpec((None, tk, Ns), lambda l, i, p: (p[1] * lh + l, i, 0)))
    return pl.pallas_call(
        body, name=name, out_shape=jax.ShapeDtypeStruct((2 * lh, Ks, Ns), F32), grid_spec=grid_spec,
        compiler_params=pltpu.CompilerParams(dimension_semantics=("arbitrary", "arbitrary"),
                                             vmem_limit_bytes=_vmem(2 * tk * Ns * 14 + 4 * tk * Ns * 4)),
    )(place, own, recv)


def _sibling_fill(gshards):
    n = len(gshards)

    def body(*refs):
        ins, outs = refs[:n], refs[n:2 * n]
        send_sems, recv_sems = refs[2 * n:]
        x, y, c = _my_place()
        cps = []
        for w in range(n):
            lh = ins[w].shape[0] // 2
            cp = pltpu.make_async_remote_copy(
                src_ref=outs[w].at[pl.ds(c * lh, lh)], dst_ref=outs[w].at[pl.ds(c * lh, lh)],
                send_sem=send_sems.at[w], recv_sem=recv_sems.at[w],
                device_id=(x, y, 1 - c), device_id_type=MESH)
            cp.start()
            cps.append(cp)
        for cp in cps:
            cp.wait()

    any_spec = pl.BlockSpec(memory_space=pl.ANY)
    return pl.pallas_call(
        body, name="grad_sibling_fill", out_shape=[jax.ShapeDtypeStruct(g.shape, g.dtype) for g in gshards],
        in_specs=[any_spec] * n, out_specs=[any_spec] * n,
        input_output_aliases={w: w for w in range(n)},
        scratch_shapes=[pltpu.SemaphoreType.DMA((n,))] * 2,
    )(*gshards)


def _allreduce_small(p):
    def body(p_ref, o_ref, r0, r1, r2, send_sems, recv_sems):
        x, y, c = _my_place()
        o_ref[...] = p_ref[...]
        for s, (peer, rbuf) in enumerate([((x, y, 1 - c), r0), ((1 - x, y, c), r1), ((x, 1 - y, c), r2)]):
            cp = pltpu.make_async_remote_copy(src_ref=o_ref, dst_ref=rbuf, send_sem=send_sems.at[s],
                                              recv_sem=recv_sems.at[s], device_id=peer, device_id_type=MESH)
            cp.start()
            cp.wait()
            o_ref[...] = o_ref[...] + rbuf[...]

    vm = pl.BlockSpec(memory_space=pltpu.VMEM)
    return pl.pallas_call(
        body, name="allreduce_small", out_shape=jax.ShapeDtypeStruct(p.shape, F32),
        in_specs=[vm], out_specs=vm,
        scratch_shapes=[pltpu.VMEM(p.shape, F32)] * 3 + [pltpu.SemaphoreType.DMA((3,))] * 2,
        compiler_params=pltpu.CompilerParams(vmem_limit_bytes=_vmem(6 * _nbytes(p.shape, F32))),
    )(p)


def _pack_rows(parts):
    rows, metas = [], []
    for a in parts:
        flat = a.reshape(-1)
        nrow = -(-flat.shape[0] // LANES)
        nrow = -(-nrow // 8) * 8
        flat = jnp.pad(flat, (0, nrow * LANES - flat.shape[0]))
        rows.append(flat.reshape(nrow, LANES))
        metas.append((a.shape, nrow))
    return jnp.concatenate(rows, axis=0), metas


def _unpack_rows(packed, metas):
    out, r0 = [], 0
    for shape, nrow in metas:
        size = int(np.prod(shape))
        out.append(packed[r0:r0 + nrow].reshape(-1)[:size].reshape(shape))
        r0 += nrow
    return out


def kernel(x, positions, pre_mix_g, post_mix_g, pre_ffn_g, post_ffn_g, a_w_in, a_b_in, a_ln_g, a_ln_b, a_w_s, a_b_s, a_w_out, b_w_qkv, b_b_qkv, b_sinks, b_w_o, ffn_w_gu, ffn_w_down, loss_target, m_pre_mix_g, m_post_mix_g, m_pre_ffn_g, m_post_ffn_g, m_a_w_in, m_a_b_in, m_a_ln_g, m_a_ln_b, m_a_w_s, m_a_b_s, m_a_w_out, m_b_w_qkv, m_b_b_qkv, m_b_sinks, m_b_w_o, m_ffn_w_gu, m_ffn_w_down, v_pre_mix_g, v_post_mix_g, v_pre_ffn_g, v_post_ffn_g, v_a_w_in, v_a_b_in, v_a_ln_g, v_a_ln_b, v_a_w_s, v_a_b_s, v_a_w_out, v_b_w_qkv, v_b_b_qkv, v_b_sinks, v_b_w_o, v_ffn_w_gu, v_ffn_w_down):
    depth, D = pre_mix_g.shape
    T = x.shape[1]
    xi, yi, ci = _my_place()
    chip = 2 * xi + yi
    place = jnp.stack([chip, ci]).astype(jnp.int32)
    c_arr = jnp.reshape(ci, (1,)).astype(jnp.int32)

    big_names = ["a_w_in", "a_w_out", "b_w_qkv", "b_w_o", "ffn_w_gu", "ffn_w_down"]
    big = [a_w_in, a_w_out, b_w_qkv, b_w_o, ffn_w_gu, ffn_w_down]
    big_m = [m_a_w_in, m_a_w_out, m_b_w_qkv, m_b_w_o, m_ffn_w_gu, m_ffn_w_down]
    big_v = [v_a_w_in, v_a_w_out, v_b_w_qkv, v_b_w_o, v_ffn_w_gu, v_ffn_w_down]
    axes = [2, 1, 2, 1, 2, 1]
    shards_bf16 = [_cast_bf16(w, name="cast_" + nm) for w, nm in zip(big, big_names)]
    W_in, W_out, W_qkv, W_o, W_gu, W_down = _allgather_weights(shards_bf16, axes)

    h = x[0]
    target = loss_target[0]
    ctab, stab = _rope_tables(positions[0])
    q_width = W_o.shape[1]
    kv_width = N_KV_HEADS * HEAD_DIM
    row = lambda a, i: a[i:i + 1]

    nq = b_b_qkv.shape[1]
    bq_full = jnp.zeros((b_b_qkv.shape[0], N_CHIPS * nq), F32)
    bq_full = lax.dynamic_update_slice(bq_full, jnp.where(ci == 0, b_b_qkv, 0.0), (0, chip * nq))
    bq_packed, bq_meta = _pack_rows([bq_full])
    b_qkv_full = _unpack_rows(_allreduce_small(bq_packed), bq_meta)[0]

    saved = []
    for i in range(depth):
        j = i // 2
        s = {"h": h}
        hn = _rms_fwd(h, row(pre_mix_g, i), out_dtype=BF16, name=f"rms_pre_mix_{i}")
        s["hn"] = hn
        if i % 2 == 0:
            pre = _matmul(hn, W_in, mode="nn", b_l=j, bias=row(a_b_in, j), out_dtype=F32, name=f"gmlp_in_{i}")
            gated = _sgu_fwd(pre, row(a_ln_g, j), row(a_ln_b, j), a_w_s[j], a_b_s[j].T, name=f"sgu_fwd_{i}")
            mix = _matmul(gated, W_out, mode="nn", b_l=j, out_dtype=F32, name=f"gmlp_out_{i}")
            s.update(pre=pre, gated=gated)
        else:
            qkv = _matmul(hn, W_qkv, mode="nn", b_l=j, bias=row(b_qkv_full, j), out_dtype=F32, name=f"attn_qkv_{i}")
            qr, kr, vr = _rope_fwd(qkv, ctab, stab, q_width=q_width, kv_width=kv_width, name=f"rope_fwd_{i}")
            o = _attn_fwd(qr, kr, vr, row(b_sinks, j), name=f"attn_fwd_{i}")
            mix = _matmul(o, W_o, mode="nn", b_l=j, out_dtype=F32, name=f"attn_o_{i}")
            s.update(qr=qr, kr=kr, vr=vr, o=o)
        s["mix"] = mix
        h1 = _rms_res(h, mix, row(post_mix_g, i), name=f"rms_post_mix_{i}")
        s["h1"] = h1
        fn = _rms_fwd(h1, row(pre_ffn_g, i), out_dtype=BF16, name=f"rms_pre_ffn_{i}")
        gu = _matmul(fn, W_gu, mode="nn", b_l=i, out_dtype=F32, name=f"ffn_gu_{i}")
        act = _silu_fwd(gu, name=f"silu_fwd_{i}")
        f = _matmul(act, W_down, mode="nn", b_l=i, out_dtype=F32, name=f"ffn_down_{i}")
        h = _rms_res(h1, f, row(post_ffn_g, i), name=f"rms_post_ffn_{i}")
        s.update(fn=fn, gu=gu, act=act, f=f)
        saved.append(s)

    dh, loss_part = _loss_and_grad(h, target, name="loss")
    loss = lax.psum(loss_part[0, 0], ("x", "y", "c"))

    G = {nm: lax.empty(w.shape, F32) for nm, w in
         zip(big_names, [W_in, W_out, W_qkv, W_o, W_gu, W_down])}
    small = {}
    g_pre_mix, g_post_mix, g_pre_ffn, g_post_ffn = [None] * depth, [None] * depth, [None] * depth, [None] * depth
    for i in reversed(range(depth)):
        j = i // 2
        s = saved[i]
        df, g_post_ffn[i] = _rms_bwd(s["f"], row(post_ffn_g, i), dh, None, out_dtype=BF16, name=f"rms_post_ffn_bwd_{i}")
        G["ffn_w_down"] = _matmul(s["act"], df, mode="tn", into=G["ffn_w_down"], o_l=i, out_dtype=F32, name=f"ffn_down_dw_{i}")
        dact = _matmul(df, W_down, mode="nt", b_l=i, out_dtype=F32, name=f"ffn_down_dx_{i}")
        dgu = _silu_bwd(s["gu"], dact, name=f"silu_bwd_{i}")
        G["ffn_w_gu"] = _matmul(s["fn"], dgu, mode="tn", into=G["ffn_w_gu"], o_l=i, out_dtype=F32, name=f"ffn_gu_dw_{i}")
        dfn = _matmul(dgu, W_gu, mode="nt", b_l=i, out_dtype=F32, name=f"ffn_gu_dx_{i}")
        dh1, g_pre_ffn[i] = _rms_bwd(s["h1"], row(pre_ffn_g, i), dfn, dh, out_dtype=F32, name=f"rms_pre_ffn_bwd_{i}")
        dmix, g_post_mix[i] = _rms_bwd(s["mix"], row(post_mix_g, i), dh1, None, out_dtype=BF16, name=f"rms_post_mix_bwd_{i}")
        if i % 2 == 0:
            G["a_w_out"] = _matmul(s["gated"], dmix, mode="tn", into=G["a_w_out"], o_l=j, out_dtype=F32, name=f"gmlp_out_dw_{i}")
            dgated = _matmul(dmix, W_out, mode="nt", b_l=j, out_dtype=F32, name=f"gmlp_out_dx_{i}")
            dpre, dws, dbsT, dlng, dlnb, dbin = _sgu_bwd(s["pre"], dgated, row(a_ln_g, j), row(a_ln_b, j),
                                                         a_w_s[j], a_b_s[j].T, name=f"sgu_bwd_{i}")
            small[("a_w_s", j)] = dws
            small[("a_b_s", j)] = dbsT.T
            small[("a_ln_g", j)] = dlng
            small[("a_ln_b", j)] = dlnb
            small[("a_b_in", j)] = dbin
            G["a_w_in"] = _matmul(s["hn"], dpre, mode="tn", into=G["a_w_in"], o_l=j, out_dtype=F32, name=f"gmlp_in_dw_{i}")
            dhn = _matmul(dpre, W_in, mode="nt", b_l=j, out_dtype=F32, name=f"gmlp_in_dx_{i}")
        else:
            G["b_w_o"] = _matmul(s["o"], dmix, mode="tn", into=G["b_w_o"], o_l=j, out_dtype=F32, name=f"attn_o_dw_{i}")
            do = _matmul(dmix, W_o, mode="nt", b_l=j, out_dtype=BF16, name=f"attn_o_dx_{i}")
            dq, dkp, dkc, dvp, dvc, dsk = _attn_bwd(s["qr"], s["kr"], s["vr"], row(b_sinks, j), do, name=f"attn_bwd_{i}")
            dqkv, dbq = _rope_bwd(dq, dkp, dkc, dvp, dvc, ctab, stab, name=f"rope_bwd_{i}")
            small[("b_sinks", j)] = dsk[:, :b_sinks.shape[1]]
            small[("b_b_qkv", j)] = dbq
            G["b_w_qkv"] = _matmul(s["hn"], dqkv, mode="tn", into=G["b_w_qkv"], o_l=j, out_dtype=F32, name=f"attn_qkv_dw_{i}")
            dhn = _matmul(dqkv, W_qkv, mode="nt", b_l=j, out_dtype=F32, name=f"attn_qkv_dx_{i}")
        dh, g_pre_mix[i] = _rms_bwd(s["h"], row(pre_mix_g, i), dhn, dh1, out_dtype=F32, name=f"rms_pre_mix_bwd_{i}")
    grad_x = dh[None]

    g_list = [G[nm] for nm in big_names]
    sib = _sibling_exchange_halves(g_list)
    sums = [_chip_sum(g, r, c_arr, name="chip_sum_" + nm) for g, r, nm in zip(g_list, sib, big_names)]
    recv = _chip_exchange([sb for _, sb in sums], axes)
    halves = [_final_sum(sf, r, ax, place, name="final_sum_" + nm)
              for (sf, _), r, ax, nm in zip(sums, recv, axes, big_names)]
    big_g = _sibling_fill(halves)

    n_a, n_b = a_b_in.shape[0], b_sinks.shape[0]
    stack = lambda key, n: jnp.concatenate([small[(key, j)] for j in range(n)], axis=0)
    small_parts = [
        jnp.concatenate(g_pre_mix, axis=0), jnp.concatenate(g_post_mix, axis=0),
        jnp.concatenate(g_pre_ffn, axis=0), jnp.concatenate(g_post_ffn, axis=0),
        stack("a_b_in", n_a), stack("a_ln_g", n_a), stack("a_ln_b", n_a),
        jnp.stack([small[("a_w_s", j)] for j in range(n_a)]), jnp.stack([small[("a_b_s", j)] for j in range(n_a)]),
        stack("b_b_qkv", n_b), stack("b_sinks", n_b),
    ]
    packed, metas = _pack_rows(small_parts)
    red = _unpack_rows(_allreduce_small(packed), metas)
    (gr_pre_mix, gr_post_mix, gr_pre_ffn, gr_post_ffn, gr_b_in, gr_ln_g, gr_ln_b, gr_w_s, gr_b_s,
     gr_b_qkv_full, gr_sinks) = red
    gr_b_qkv = lax.dynamic_slice(gr_b_qkv_full, (0, chip * nq), (gr_b_qkv_full.shape[0], nq))

    grads = {"pre_mix_g": gr_pre_mix, "post_mix_g": gr_post_mix, "pre_ffn_g": gr_pre_ffn, "post_ffn_g": gr_post_ffn,
             "a_b_in": gr_b_in, "a_ln_g": gr_ln_g, "a_ln_b": gr_ln_b, "a_w_s": gr_w_s, "a_b_s": gr_b_s,
             "b_b_qkv": gr_b_qkv, "b_sinks": gr_sinks}
    for nm, g in zip(big_names, big_g):
        grads[nm] = g
    weights = {"pre_mix_g": (pre_mix_g, m_pre_mix_g, v_pre_mix_g), "post_mix_g": (post_mix_g, m_post_mix_g, v_post_mix_g),
               "pre_ffn_g": (pre_ffn_g, m_pre_ffn_g, v_pre_ffn_g), "post_ffn_g": (post_ffn_g, m_post_ffn_g, v_post_ffn_g),
               "a_b_in": (a_b_in, m_a_b_in, v_a_b_in), "a_ln_g": (a_ln_g, m_a_ln_g, v_a_ln_g),
               "a_ln_b": (a_ln_b, m_a_ln_b, v_a_ln_b), "a_w_s": (a_w_s, m_a_w_s, v_a_w_s), "a_b_s": (a_b_s, m_a_b_s, v_a_b_s),
               "b_b_qkv": (b_b_qkv, m_b_b_qkv, v_b_b_qkv), "b_sinks": (b_sinks, m_b_sinks, v_b_sinks)}
    for nm, w, m, v in zip(big_names, big, big_m, big_v):
        weights[nm] = (w, m, v)
    order = ["pre_mix_g", "post_mix_g", "pre_ffn_g", "post_ffn_g", "a_w_in", "a_b_in", "a_ln_g", "a_ln_b", "a_w_s",
             "a_b_s", "a_w_out", "b_w_qkv", "b_b_qkv", "b_sinks", "b_w_o", "ffn_w_gu", "ffn_w_down"]
    deltas, new_m, new_v = {}, {}, {}
    for nm in order:
        w, m, v = weights[nm]
        fn_ = _adamw_big if nm in big_names else _adamw_small
        deltas[nm], new_m[nm], new_v[nm] = fn_(w, grads[nm], m, v, name="adamw_" + nm)
    return (loss, grad_x, *[grads[nm] for nm in order], *[deltas[nm] for nm in order],
            *[new_m[nm] for nm in order], *[new_v[nm] for nm in order])
```

```python
import functools
import math

import jax
import jax.numpy as jnp
import numpy as np
from jax import lax
from jax.experimental import pallas as pl
from jax.experimental.pallas import tpu as pltpu

F32 = jnp.float32
BF16 = jnp.bfloat16
MESH = pl.DeviceIdType.MESH

HEAD_DIM = 64
N_KV_HEADS = 4
ROPE_DIM = 16
ROPE_THETA = 500000.0
CHUNK = 128
GMLP_GROUPS = 8
RMS_EPS = 1e-6
LN_EPS = 1e-5
NEG_INF = -1e30
ADAM_LR = 0.001
ADAM_B1 = 0.9
ADAM_B2 = 0.999
ADAM_EPS = 1e-08
ADAM_WD = 0.01
ADAM_STEP = 10

N_CHIPS = 4
LANES = 128
VMEM_CAP = 58 * 1024 * 1024


def _vmem(est_bytes):
    assert est_bytes < VMEM_CAP
    return VMEM_CAP


def _pick(n, cands):
    for c in cands:
        if c <= n and n % c == 0:
            return c
    return n


def _nbytes(shape, dtype):
    return int(np.prod(shape)) * jnp.dtype(dtype).itemsize


def _matmul(a, b, *, mode, out_dtype, name, a_l=None, b_l=None, bias=None, into=None, o_l=None,
            q_off=0, b_r_off=0, tp=None, tq=None, tr=None):
    a2 = a.shape[-2:]
    b2 = b.shape[-2:]
    if mode == "nn":
        (P, R), (R2, Q) = a2, b2
    elif mode == "nt":
        (P, R), (Q, R2) = a2, b2
    else:
        (R, P), (R2, Q) = a2, b2
    assert R == R2 or (mode == "nt" and R2 % R == 0), (mode, a.shape, b.shape)
    tp = tp or _pick(P, (1024, 1408, 512, 384, 256, 128))
    tq = tq or _pick(Q, (1024, 1408, 768, 512, 384, 256, 128))
    tr = tr or _pick(R, (2048, 1408, 1024, 512, 256, 128))
    assert P % tp == 0 and Q % tq == 0 and R % tr == 0
    nk = R // tr
    dims = {"nn": (((1,), (0,)), ((), ())), "nt": (((1,), (1,)), ((), ())), "tn": (((0,), (0,)), ((), ()))}[mode]

    def lead(l, blk, idx):
        if l is None:
            return pl.BlockSpec(blk, idx)
        return pl.BlockSpec((None,) + blk, lambda i, j, k: (l,) + idx(i, j, k))

    if mode == "nn":
        a_spec = lead(a_l, (tp, tr), lambda i, j, k: (i, k))
        b_spec = lead(b_l, (tr, tq), lambda i, j, k: (k, j))
    elif mode == "nt":
        a_spec = lead(a_l, (tp, tr), lambda i, j, k: (i, k))
        b_spec = lead(b_l, (tq, tr), lambda i, j, k: (j, k + b_r_off))
    else:
        a_spec = lead(a_l, (tr, tp), lambda i, j, k: (k, i))
        b_spec = lead(b_l, (tr, tq), lambda i, j, k: (k, j))
    in_specs = [a_spec, b_spec]
    args = [a, b]
    if bias is not None:
        if bias.shape[0] == 1:
            in_specs.append(pl.BlockSpec((1, tq), lambda i, j, k: (0, j)))
        else:
            in_specs.append(pl.BlockSpec((tp, tq), lambda i, j, k: (i, j)))
        args.append(bias)
    aliases = {}
    if into is not None:
        in_specs.append(pl.BlockSpec(memory_space=pl.ANY))
        args.append(into)
        aliases = {len(args) - 1: 0}
        out_shape = jax.ShapeDtypeStruct(into.shape, into.dtype)
        out_dtype = into.dtype
        out_spec = pl.BlockSpec((None, tp, tq), lambda i, j, k: (o_l, i, j + q_off))
    else:
        out_shape = jax.ShapeDtypeStruct((P, Q), out_dtype)
        out_spec = pl.BlockSpec((tp, tq), lambda i, j, k: (i, j))
    has_bias = bias is not None
    has_into = into is not None

    def body(*refs):
        a_ref, b_ref = refs[0], refs[1]
        pos = 2
        bias_ref = None
        if has_bias:
            bias_ref = refs[pos]
            pos += 1
        if has_into:
            pos += 1
        o_ref = refs[pos]
        acc_ref = refs[pos + 1] if nk > 1 else None
        part = lax.dot_general(a_ref[...], b_ref[...], dims, preferred_element_type=F32)

        def finish(acc):
            if has_bias:
                acc = acc + bias_ref[...]
            o_ref[...] = acc.astype(out_dtype)

        if nk == 1:
            finish(part)
        else:
            k = pl.program_id(2)

            @pl.when(k == 0)
            def _():
                acc_ref[...] = part

            @pl.when(k > 0)
            def _():
                acc_ref[...] += part

            @pl.when(k == nk - 1)
            def _():
                finish(acc_ref[...])

    est = 2 * (_nbytes((tp, tr), a.dtype) + _nbytes((tr, tq), b.dtype) + _nbytes((tp, tq), out_dtype)) + 3 * tp * tq * 4
    return pl.pallas_call(
        body, name=name, out_shape=out_shape,
        grid=(P // tp, Q // tq, nk),
        in_specs=in_specs, out_specs=out_spec,
        scratch_shapes=[pltpu.VMEM((tp, tq), F32)] if nk > 1 else [],
        input_output_aliases=aliases,
        compiler_params=pltpu.CompilerParams(
            dimension_semantics=("parallel", "parallel", "arbitrary"), vmem_limit_bytes=_vmem(est)),
    )(*args)


def _row_call(body, ins, outs, *, name, rows, tr, acc_outs=(), est=0):
    in_specs = []
    for arr, kind in ins:
        if kind == "row":
            in_specs.append(pl.BlockSpec((tr, arr.shape[1]), lambda i: (i, 0)))
        else:
            nd = arr.ndim
            in_specs.append(pl.BlockSpec(arr.shape, lambda i, nd=nd: (0,) * nd))
    out_shapes = [jax.ShapeDtypeStruct(s, d) for s, d in outs] + [jax.ShapeDtypeStruct(s, d) for s, d in acc_outs]
    out_specs = [pl.BlockSpec((tr, s[1]), lambda i: (i, 0)) for s, _ in outs]
    out_specs += [pl.BlockSpec(s, lambda i, nd=len(s): (0,) * nd) for s, _ in acc_outs]
    res = pl.pallas_call(
        body, name=name, out_shape=out_shapes, grid=(rows // tr,), in_specs=in_specs, out_specs=out_specs,
        compiler_params=pltpu.CompilerParams(dimension_semantics=("arbitrary",), vmem_limit_bytes=_vmem(est)),
    )(*[a for a, _ in ins])
    return res


def _rms_fwd(x, g, *, out_dtype, name):
    T, D = x.shape
    tr = _pick(T, (512, 256, 128))

    def body(x_ref, g_ref, o_ref):
        xv = x_ref[...]
        r = lax.rsqrt(jnp.mean(xv * xv, axis=-1, keepdims=True) + RMS_EPS)
        o_ref[...] = (xv * r * g_ref[...]).astype(out_dtype)

    return _row_call(body, [(x, "row"), (g, "full")], [((T, D), out_dtype)], name=name, rows=T, tr=tr,
                     est=8 * tr * D * 4)[0]


def _rms_res(h, y, g, *, name):
    T, D = h.shape
    tr = _pick(T, (512, 256, 128))

    def body(h_ref, y_ref, g_ref, o_ref):
        yv = y_ref[...]
        r = lax.rsqrt(jnp.mean(yv * yv, axis=-1, keepdims=True) + RMS_EPS)
        o_ref[...] = h_ref[...] + yv * r * g_ref[...]

    return _row_call(body, [(h, "row"), (y, "row"), (g, "full")], [((T, D), F32)], name=name, rows=T, tr=tr,
                     est=10 * tr * D * 4)[0]


def _rms_bwd(x, g, dy, dres, *, out_dtype, name):
    T, D = x.shape
    tr = _pick(T, (512, 256, 128))
    has_res = dres is not None

    def body(*refs):
        if has_res:
            x_ref, g_ref, dy_ref, dr_ref, dx_ref, dg_ref = refs
        else:
            x_ref, g_ref, dy_ref, dx_ref, dg_ref = refs
        xv = x_ref[...]
        r = lax.rsqrt(jnp.mean(xv * xv, axis=-1, keepdims=True) + RMS_EPS)
        xhat = xv * r
        dyv = dy_ref[...].astype(F32)
        dxn = dyv * g_ref[...]
        dx = r * (dxn - xhat * jnp.mean(dxn * xhat, axis=-1, keepdims=True))
        if has_res:
            dx = dx + dr_ref[...]
        dx_ref[...] = dx.astype(out_dtype)
        part = jnp.sum(dyv * xhat, axis=0, keepdims=True)

        @pl.when(pl.program_id(0) == 0)
        def _():
            dg_ref[...] = part

        @pl.when(pl.program_id(0) > 0)
        def _():
            dg_ref[...] += part

    ins = [(x, "row"), (g, "full"), (dy, "row")] + ([(dres, "row")] if has_res else [])
    dx, dg = _row_call(body, ins, [((T, D), out_dtype)], name=name, rows=T, tr=tr, acc_outs=[((1, D), F32)],
                       est=12 * tr * D * 4)
    return dx, dg


def _ffn_up(fn, w_gu, l, *, name):
    T, D = fn.shape
    H = w_gu.shape[2] // 2
    tp = _pick(T, (512, 256, 128))
    tq = _pick(H, (1408, 768, 512, 256, 128))
    nj = H // tq

    def body(a_ref, wg_ref, wu_ref, g_ref, u_ref, act_ref):
        a = a_ref[...]
        g = jnp.dot(a, wg_ref[...], preferred_element_type=F32)
        u = jnp.dot(a, wu_ref[...], preferred_element_type=F32)
        g_ref[...] = g
        u_ref[...] = u
        act_ref[...] = (g * jax.nn.sigmoid(g) * u).astype(BF16)

    tile = pl.BlockSpec((tp, tq), lambda j, i: (i, j))
    est = 2 * (tp * D * 2 + 2 * D * tq * 2 + 2 * tp * tq * 4 + tp * tq * 2) + 4 * tp * tq * 4
    return pl.pallas_call(
        body, name=name,
        out_shape=[jax.ShapeDtypeStruct((T, H), F32), jax.ShapeDtypeStruct((T, H), F32),
                   jax.ShapeDtypeStruct((T, H), BF16)],
        grid=(nj, T // tp),
        in_specs=[pl.BlockSpec((tp, D), lambda j, i: (i, 0)),
                  pl.BlockSpec((None, D, tq), lambda j, i: (l, 0, j)),
                  pl.BlockSpec((None, D, tq), lambda j, i: (l, 0, j + nj))],
        out_specs=[tile, tile, tile],
        compiler_params=pltpu.CompilerParams(dimension_semantics=("parallel", "parallel"),
                                             vmem_limit_bytes=_vmem(est)),
    )(fn, w_gu, w_gu)


def _ffn_down_dx(df, w_down, l, g, u, *, name):
    T, D = df.shape
    H = w_down.shape[1]
    tp = _pick(T, (512, 256, 128))
    tq = _pick(H, (1408, 768, 512, 256, 128))

    def body(a_ref, w_ref, g_ref, u_ref, dg_ref, du_ref):
        da = lax.dot_general(a_ref[...], w_ref[...], (((1,), (1,)), ((), ())), preferred_element_type=F32)
        gv = g_ref[...]
        sg = jax.nn.sigmoid(gv)
        silu = gv * sg
        dg_ref[...] = (da * u_ref[...] * (sg + silu * (1.0 - sg))).astype(BF16)
        du_ref[...] = (da * silu).astype(BF16)

    tile = pl.BlockSpec((tp, tq), lambda j, i: (i, j))
    est = 2 * (tp * D * 2 + tq * D * 2 + 2 * tp * tq * 4 + 2 * tp * tq * 2) + 5 * tp * tq * 4
    return pl.pallas_call(
        body, name=name,
        out_shape=[jax.ShapeDtypeStruct((T, H), BF16), jax.ShapeDtypeStruct((T, H), BF16)],
        grid=(H // tq, T // tp),
        in_specs=[pl.BlockSpec((tp, D), lambda j, i: (i, 0)),
                  pl.BlockSpec((None, tq, D), lambda j, i: (l, j, 0)), tile, tile],
        out_specs=[tile, tile],
        compiler_params=pltpu.CompilerParams(dimension_semantics=("parallel", "parallel"),
                                             vmem_limit_bytes=_vmem(est)),
    )(df, w_down, g, u)


def _loss_and_grad(y, target, *, name):
    T, D = y.shape
    tr = _pick(T, (512, 256, 128))

    def body(y_ref, t_ref, dy_ref, l_ref):
        e = y_ref[...] - t_ref[...]
        dy_ref[...] = e * (1.0 / D)
        part = jnp.sum(jnp.sum(e * e, axis=1, keepdims=True), axis=0, keepdims=True) * (0.5 / D)

        @pl.when(pl.program_id(0) == 0)
        def _():
            l_ref[...] = part

        @pl.when(pl.program_id(0) > 0)
        def _():
            l_ref[...] += part

    dy, l = _row_call(body, [(y, "row"), (target, "row")], [((T, D), F32)], name=name, rows=T, tr=tr,
                      acc_outs=[((1, 1), F32)], est=8 * tr * D * 4)
    return dy, l


_SQRT_HALF = 0.7071067811865476
_INV_SQRT_2PI = 0.3989422804014327


def _gelu_parts(x):
    cdf = 0.5 * (1.0 + lax.erf(x * _SQRT_HALF))
    return cdf


def _sgu_common(pre, lng, lnb, W):
    cdf = _gelu_parts(pre)
    z = pre * cdf
    u = z[:, :W]
    v = z[:, W:]
    mu = jnp.mean(v, axis=-1, keepdims=True)
    vc = v - mu
    var = jnp.mean(vc * vc, axis=-1, keepdims=True)
    rstd = lax.rsqrt(var + LN_EPS)
    vhat = vc * rstd
    vn = vhat * lng + lnb
    return cdf, u, vhat, rstd, vn


def _causal_mask():
    t = lax.broadcasted_iota(jnp.int32, (CHUNK, CHUNK), 0)
    s = lax.broadcasted_iota(jnp.int32, (CHUNK, CHUNK), 1)
    return t >= s


def _sgu_fwd(pre, lng, lnb, ws, bsT, *, name):
    T, W2 = pre.shape
    W = W2 // 2
    G = ws.shape[0]
    gd = W // G

    def body(pre_ref, lng_ref, lnb_ref, ws_ref, bs_ref, o_ref):
        _, u, _, _, vn = _sgu_common(pre_ref[...], lng_ref[...], lnb_ref[...], W)
        vnb = vn.astype(BF16)
        causal = _causal_mask()
        for g in range(G):
            w = jnp.where(causal, ws_ref[g], 0.0).astype(BF16)
            sv = jnp.dot(w, vnb[:, g * gd:(g + 1) * gd], preferred_element_type=F32) + bs_ref[:, g:g + 1]
            o_ref[:, g * gd:(g + 1) * gd] = (u[:, g * gd:(g + 1) * gd] * sv).astype(BF16)

    return pl.pallas_call(
        body, name=name, out_shape=jax.ShapeDtypeStruct((T, W), BF16), grid=(T // CHUNK,),
        in_specs=[pl.BlockSpec((CHUNK, W2), lambda i: (i, 0)),
                  pl.BlockSpec((1, W), lambda i: (0, 0)), pl.BlockSpec((1, W), lambda i: (0, 0)),
                  pl.BlockSpec(ws.shape, lambda i: (0, 0, 0)), pl.BlockSpec(bsT.shape, lambda i: (0, 0))],
        out_specs=pl.BlockSpec((CHUNK, W), lambda i: (i, 0)),
        compiler_params=pltpu.CompilerParams(dimension_semantics=("arbitrary",),
                                             vmem_limit_bytes=_vmem(12 * CHUNK * W2 * 4)),
    )(pre, lng, lnb, ws, bsT)


def _sgu_bwd(pre, dgated, lng, lnb, ws, bsT, *, name):
    T, W2 = pre.shape
    W = W2 // 2
    G = ws.shape[0]
    gd = W // G

    def body(pre_ref, dgt_ref, lng_ref, lnb_ref, ws_ref, bs_ref,
             dpre_ref, dws_ref, dbs_ref, dlng_ref, dlnb_ref, dbin_ref):
        first = pl.program_id(0) == 0

        @pl.when(first)
        def _():
            dws_ref[...] = jnp.zeros_like(dws_ref)
            dbs_ref[...] = jnp.zeros_like(dbs_ref)
            dlng_ref[...] = jnp.zeros_like(dlng_ref)
            dlnb_ref[...] = jnp.zeros_like(dlnb_ref)
            dbin_ref[...] = jnp.zeros_like(dbin_ref)

        pre_v = pre_ref[...]
        lng_v = lng_ref[...]
        cdf, u, vhat, rstd, vn = _sgu_common(pre_v, lng_v, lnb_ref[...], W)
        vnb = vn.astype(BF16)
        dgt = dgt_ref[...].astype(F32)
        causal = _causal_mask()
        du_parts, dvn_parts = [], []
        for g in range(G):
            sl = slice(g * gd, (g + 1) * gd)
            w = jnp.where(causal, ws_ref[g], 0.0).astype(BF16)
            sv = jnp.dot(w, vnb[:, sl], preferred_element_type=F32) + bs_ref[:, g:g + 1]
            dgt_g = dgt[:, sl]
            du_parts.append(dgt_g * sv)
            dsv = dgt_g * u[:, sl]
            dsvb = dsv.astype(BF16)
            dvn_parts.append(lax.dot_general(w, dsvb, (((0,), (0,)), ((), ())), preferred_element_type=F32))
            dw = lax.dot_general(dsvb, vnb[:, sl], (((1,), (1,)), ((), ())), preferred_element_type=F32)
            dws_ref[g] += jnp.where(causal, dw, 0.0)
            dbs_ref[:, g:g + 1] += jnp.sum(dsv, axis=1, keepdims=True)
        du = jnp.concatenate(du_parts, axis=1)
        dvn = jnp.concatenate(dvn_parts, axis=1)
        dlng_ref[...] += jnp.sum(dvn * vhat, axis=0, keepdims=True)
        dlnb_ref[...] += jnp.sum(dvn, axis=0, keepdims=True)
        dvh = dvn * lng_v
        dv = rstd * (dvh - jnp.mean(dvh, axis=-1, keepdims=True)
                     - vhat * jnp.mean(dvh * vhat, axis=-1, keepdims=True))
        dz = jnp.concatenate([du, dv], axis=1)
        dgelu = cdf + pre_v * jnp.exp(-0.5 * pre_v * pre_v) * _INV_SQRT_2PI
        dpre = dz * dgelu
        dbin_ref[...] += jnp.sum(dpre, axis=0, keepdims=True)
        dpre_ref[...] = dpre.astype(BF16)

    full = lambda shape: pl.BlockSpec(shape, lambda i, nd=len(shape): (0,) * nd)
    return pl.pallas_call(
        body, name=name,
        out_shape=[jax.ShapeDtypeStruct((T, W2), BF16), jax.ShapeDtypeStruct(ws.shape, F32),
                   jax.ShapeDtypeStruct(bsT.shape, F32), jax.ShapeDtypeStruct((1, W), F32),
                   jax.ShapeDtypeStruct((1, W), F32), jax.ShapeDtypeStruct((1, W2), F32)],
        grid=(T // CHUNK,),
        in_specs=[pl.BlockSpec((CHUNK, W2), lambda i: (i, 0)), pl.BlockSpec((CHUNK, W), lambda i: (i, 0)),
                  full((1, W)), full((1, W)), full(ws.shape), full(bsT.shape)],
        out_specs=[pl.BlockSpec((CHUNK, W2), lambda i: (i, 0)), full(ws.shape), full(bsT.shape),
                   full((1, W)), full((1, W)), full((1, W2))],
        compiler_params=pltpu.CompilerParams(dimension_semantics=("arbitrary",),
                                             vmem_limit_bytes=_vmem(24 * CHUNK * W2 * 4)),
    )(pre, dgated, lng, lnb, ws, bsT)


def _rope_tables(positions):
    half = ROPE_DIM // 2
    inv_freq = ROPE_THETA ** (-jnp.arange(0, ROPE_DIM, 2, dtype=F32) / ROPE_DIM)
    ang = positions.astype(F32).reshape(-1, 1) * inv_freq
    cos, sin = jnp.cos(ang), jnp.sin(ang)
    T = ang.shape[0]
    rest = HEAD_DIM - ROPE_DIM
    c64 = jnp.concatenate([cos, cos, jnp.ones((T, rest), F32)], axis=1)
    s64 = jnp.concatenate([-sin, sin, jnp.zeros((T, rest), F32)], axis=1)
    del half
    return jnp.tile(c64, (1, LANES // HEAD_DIM)), jnp.tile(s64, (1, LANES // HEAD_DIM))


def _swap8(x):
    W = x.shape[1]
    half = ROPE_DIM // 2
    lane = lax.broadcasted_iota(jnp.int32, x.shape, 1) % HEAD_DIM
    return jnp.where(lane < half, pltpu.roll(x, W - half, axis=1),
                     jnp.where(lane < ROPE_DIM, pltpu.roll(x, half, axis=1), 0.0))


def _wide(tab, W):
    return jnp.concatenate([tab] * (W // LANES), axis=1) if W > LANES else tab


def _rope_fwd(qkv, ctab, stab, *, q_width, kv_width, name):
    T = qkv.shape[0]
    tr = _pick(T, (256, 128))
    scale = HEAD_DIM ** -0.5

    def body(x_ref, c_ref, s_ref, q_ref, k_ref, v_ref):
        c = c_ref[...]
        s = s_ref[...]
        q = x_ref[:, :q_width]
        k = x_ref[:, q_width:q_width + kv_width]
        q_ref[...] = ((q * _wide(c, q_width) + _swap8(q) * _wide(s, q_width)) * scale).astype(BF16)
        k_ref[...] = (k * _wide(c, kv_width) + _swap8(k) * _wide(s, kv_width)).astype(BF16)
        v_ref[...] = x_ref[:, q_width + kv_width:].astype(BF16)

    return _row_call(body, [(qkv, "row"), (ctab, "row"), (stab, "row")],
                     [((T, q_width), BF16), ((T, kv_width), BF16), ((T, kv_width), BF16)],
                     name=name, rows=T, tr=tr, est=10 * tr * qkv.shape[1] * 4)


_NT = (((1,), (1,)), ((), ()))
_TN = (((0,), (0,)), ((), ()))


def _group_rows(ref, heads):
    return jnp.concatenate([ref[:, h * HEAD_DIM:(h + 1) * HEAD_DIM] for h in heads], axis=0)


def _attn_group_probs(q, kk, sinks, n, grp):
    rows = grp * CHUNK
    s = lax.dot_general(q, kk, _NT, preferred_element_type=F32)
    qi = lax.broadcasted_iota(jnp.int32, (rows, 2 * CHUNK), 0) & (CHUNK - 1)
    sj = lax.broadcasted_iota(jnp.int32, (rows, 2 * CHUNK), 1)
    valid = ((sj < CHUNK) & (sj > qi) & (n > 0)) | ((sj >= CHUNK) & (sj - CHUNK <= qi))
    s = jnp.where(valid, s, NEG_INF)
    r = lax.broadcasted_iota(jnp.int32, (rows, 1), 0)
    sink = jnp.full((rows, 1), sinks[grp - 1], F32)
    for g in range(grp - 2, -1, -1):
        sink = jnp.where(r < (g + 1) * CHUNK, sinks[g], sink)
    m = jnp.maximum(jnp.max(s, axis=1, keepdims=True), sink)
    p = jnp.exp(s - m)
    ps = jnp.exp(sink - m)
    inv = 1.0 / (jnp.sum(p, axis=1, keepdims=True) + ps)
    return p * inv, ps * inv


def _kv_specs(width, nb):
    prev = pl.BlockSpec((CHUNK, width), lambda n: (jnp.maximum(n - 1, 0), 0))
    cur = pl.BlockSpec((CHUNK, width), lambda n: (n, 0))
    return prev, cur


def _attn_fwd(qr, kr, vr, sinks, *, name):
    T, QW = qr.shape
    KW = kr.shape[1]
    HQ, HK = QW // HEAD_DIM, KW // HEAD_DIM
    grp = HQ // HK
    nb = T // CHUNK

    def body(q_ref, kp_ref, kc_ref, vp_ref, vc_ref, s_ref, o_ref):
        n = pl.program_id(0)
        for kh in range(HK):
            ks = slice(kh * HEAD_DIM, (kh + 1) * HEAD_DIM)
            heads = list(range(kh * grp, (kh + 1) * grp))
            q = _group_rows(q_ref, heads)
            kk = jnp.concatenate([kp_ref[:, ks], kc_ref[:, ks]], axis=0)
            vv = jnp.concatenate([vp_ref[:, ks], vc_ref[:, ks]], axis=0)
            p, _ = _attn_group_probs(q, kk, [s_ref[0, h] for h in heads], n, grp)
            o = jnp.dot(p.astype(BF16), vv, preferred_element_type=F32).astype(BF16)
            for g, h in enumerate(heads):
                o_ref[:, h * HEAD_DIM:(h + 1) * HEAD_DIM] = o[g * CHUNK:(g + 1) * CHUNK]

    kp, kc = _kv_specs(KW, nb)
    return pl.pallas_call(
        body, name=name, out_shape=jax.ShapeDtypeStruct((T, QW), BF16), grid=(nb,),
        in_specs=[pl.BlockSpec((CHUNK, QW), lambda n: (n, 0)), kp, kc, kp, kc,
                  pl.BlockSpec(memory_space=pltpu.SMEM)],
        out_specs=pl.BlockSpec((CHUNK, QW), lambda n: (n, 0)),
        compiler_params=pltpu.CompilerParams(dimension_semantics=("arbitrary",), vmem_limit_bytes=_vmem(8 << 20)),
    )(qr, kr, kr, vr, vr, sinks)


def _attn_bwd(qr, kr, vr, sinks, do, *, name):
    T, QW = qr.shape
    KW = kr.shape[1]
    HQ, HK = QW // HEAD_DIM, KW // HEAD_DIM
    grp = HQ // HK
    nb = T // CHUNK

    def body(q_ref, kp_ref, kc_ref, vp_ref, vc_ref, s_ref, do_ref,
             dq_ref, dkp_ref, dkc_ref, dvp_ref, dvc_ref, ds_ref):
        n = pl.program_id(0)
        lane = lax.broadcasted_iota(jnp.int32, (1, LANES), 1)
        dsink = jnp.zeros((1, LANES), F32)
        for kh in range(HK):
            ks = slice(kh * HEAD_DIM, (kh + 1) * HEAD_DIM)
            heads = list(range(kh * grp, (kh + 1) * grp))
            q = _group_rows(q_ref, heads)
            doh = _group_rows(do_ref, heads)
            kk = jnp.concatenate([kp_ref[:, ks], kc_ref[:, ks]], axis=0)
            vv = jnp.concatenate([vp_ref[:, ks], vc_ref[:, ks]], axis=0)
            p, ps = _attn_group_probs(q, kk, [s_ref[0, h] for h in heads], n, grp)
            dp = lax.dot_general(doh, vv, _NT, preferred_element_type=F32)
            delta = jnp.sum(p * dp, axis=1, keepdims=True)
            ds = (p * (dp - delta)).astype(BF16)
            dv = lax.dot_general(p.astype(BF16), doh, _TN, preferred_element_type=F32)
            dk = lax.dot_general(ds, q, _TN, preferred_element_type=F32)
            dq = jnp.dot(ds, kk, preferred_element_type=F32)
            psd = ps * delta
            for g, h in enumerate(heads):
                dq_ref[:, h * HEAD_DIM:(h + 1) * HEAD_DIM] = dq[g * CHUNK:(g + 1) * CHUNK]
                dsink = dsink + jnp.where(
                    lane == h, -jnp.sum(psd[g * CHUNK:(g + 1) * CHUNK], axis=0, keepdims=True), 0.0)
            dkp_ref[:, ks] = dk[:CHUNK]
            dkc_ref[:, ks] = dk[CHUNK:]
            dvp_ref[:, ks] = dv[:CHUNK]
            dvc_ref[:, ks] = dv[CHUNK:]

        @pl.when(n == 0)
        def _():
            ds_ref[...] = dsink

        @pl.when(n > 0)
        def _():
            ds_ref[...] += dsink

    kp, kc = _kv_specs(KW, nb)
    qspec = pl.BlockSpec((CHUNK, QW), lambda n: (n, 0))
    kout = pl.BlockSpec((CHUNK, KW), lambda n: (n, 0))
    return pl.pallas_call(
        body, name=name,
        out_shape=[jax.ShapeDtypeStruct((T, QW), F32)] + [jax.ShapeDtypeStruct((T, KW), F32)] * 4
        + [jax.ShapeDtypeStruct((1, LANES), F32)],
        grid=(nb,),
        in_specs=[qspec, kp, kc, kp, kc, pl.BlockSpec(memory_space=pltpu.SMEM), qspec],
        out_specs=[qspec, kout, kout, kout, kout, pl.BlockSpec((1, LANES), lambda n: (0, 0))],
        compiler_params=pltpu.CompilerParams(dimension_semantics=("arbitrary",), vmem_limit_bytes=_vmem(12 << 20)),
    )(qr, kr, kr, vr, vr, sinks, do)


def _rope_bwd(dq, dkp, dkc, dvp, dvc, ctab, stab, *, name):
    T, QW = dq.shape
    KW = dkp.shape[1]
    nb = T // CHUNK
    scale = HEAD_DIM ** -0.5
    width = QW + 2 * KW

    def body(dq_ref, dkc_ref, dkn_ref, dvc_ref, dvn_ref, c_ref, s_ref, o_ref, db_ref):
        n = pl.program_id(0)
        c = c_ref[...]
        s = s_ref[...]
        has_next = (n < nb - 1).astype(F32)
        dqv = dq_ref[...]
        dk = dkc_ref[...] + has_next * dkn_ref[...]
        dv = dvc_ref[...] + has_next * dvn_ref[...]
        dq_pre = (dqv * _wide(c, QW) + _swap8(dqv * _wide(s, QW))) * scale
        dk_pre = dk * _wide(c, KW) + _swap8(dk * _wide(s, KW))
        o_ref[:, :QW] = dq_pre.astype(BF16)
        o_ref[:, QW:QW + KW] = dk_pre.astype(BF16)
        o_ref[:, QW + KW:] = dv.astype(BF16)
        part = jnp.concatenate([jnp.sum(dq_pre, axis=0, keepdims=True), jnp.sum(dk_pre, axis=0, keepdims=True),
                                jnp.sum(dv, axis=0, keepdims=True)], axis=1)

        @pl.when(n == 0)
        def _():
            db_ref[...] = part

        @pl.when(n > 0)
        def _():
            db_ref[...] += part

    cur = lambda w: pl.BlockSpec((CHUNK, w), lambda n: (n, 0))
    nxt = lambda w: pl.BlockSpec((CHUNK, w), lambda n: (jnp.minimum(n + 1, nb - 1), 0))
    return pl.pallas_call(
        body, name=name,
        out_shape=[jax.ShapeDtypeStruct((T, width), BF16), jax.ShapeDtypeStruct((1, width), F32)],
        grid=(nb,),
        in_specs=[cur(QW), cur(KW), nxt(KW), cur(KW), nxt(KW), cur(LANES), cur(LANES)],
        out_specs=[cur(width), pl.BlockSpec((1, width), lambda n: (0, 0))],
        compiler_params=pltpu.CompilerParams(dimension_semantics=("arbitrary",), vmem_limit_bytes=_vmem(8 << 20)),
    )(dq, dkc, dkp, dvc, dvp, ctab, stab)


def _cast_bf16(w, *, name):
    L, K, N = w.shape
    tk = _pick(K, (512, 352, 256, 128))

    def body(w_ref, o_ref):
        o_ref[...] = w_ref[...].astype(BF16)

    spec = pl.BlockSpec((None, tk, N), lambda l, i: (l, i, 0))
    return pl.pallas_call(
        body, name=name, out_shape=jax.ShapeDtypeStruct(w.shape, BF16), grid=(L, K // tk),
        in_specs=[spec], out_specs=spec,
        compiler_params=pltpu.CompilerParams(dimension_semantics=("parallel", "parallel"),
                                             vmem_limit_bytes=_vmem(4 * tk * N * 6)),
    )(w)


def _adamw_math(w, g, m, v):
    m = ADAM_B1 * m + (1.0 - ADAM_B1) * g
    v = ADAM_B2 * v + (1.0 - ADAM_B2) * (g * g)
    m_hat = m / (1.0 - ADAM_B1 ** ADAM_STEP)
    v_hat = v / (1.0 - ADAM_B2 ** ADAM_STEP)
    delta = -ADAM_LR * (m_hat / (jnp.sqrt(v_hat) + ADAM_EPS) + ADAM_WD * w)
    return delta, m, v


def _adamw_big(w, g, m, v, *, name):
    L, K, N = w.shape
    tk = _pick(K, (256, 176, 128))

    def body(w_ref, g_ref, m_ref, v_ref, d_ref, mo_ref, vo_ref):
        d, mn, vn = _adamw_math(w_ref[...], g_ref[...], m_ref[...], v_ref[...])
        d_ref[...] = d
        mo_ref[...] = mn
        vo_ref[...] = vn

    spec = pl.BlockSpec((None, tk, N), lambda l, i: (l, i, 0))
    sd = jax.ShapeDtypeStruct(w.shape, F32)
    return pl.pallas_call(
        body, name=name, out_shape=[sd, sd, sd], grid=(L, K // tk),
        in_specs=[spec] * 4, out_specs=[spec] * 3,
        compiler_params=pltpu.CompilerParams(dimension_semantics=("parallel", "parallel"),
                                             vmem_limit_bytes=_vmem(2 * 7 * tk * N * 4 + 6 * tk * N * 4)),
    )(w, g, m, v)


def _adamw_small(w, g, m, v, *, name):
    def body(w_ref, g_ref, m_ref, v_ref, d_ref, mo_ref, vo_ref):
        d, mn, vn = _adamw_math(w_ref[...], g_ref[...], m_ref[...], v_ref[...])
        d_ref[...] = d
        mo_ref[...] = mn
        vo_ref[...] = vn

    sd = jax.ShapeDtypeStruct(w.shape, F32)
    return pl.pallas_call(body, name=name, out_shape=[sd, sd, sd])(w, g, m, v)


def _my_place():
    return lax.axis_index("x"), lax.axis_index("y"), lax.axis_index("c")


def _shard_region(ref, axis, chip, layer0, n_layers, size):
    if axis == 1:
        assert size % 16 == 0
        return ref.at[pl.ds(layer0, n_layers), pl.ds(pl.multiple_of(chip * size, 16), size), :]
    assert size % LANES == 0
    return ref.at[pl.ds(layer0, n_layers), :, pl.ds(pl.multiple_of(chip * size, LANES), size)]


def _allgather_weights(shards, axes):
    n = len(shards)
    fulls = []
    for s, ax in zip(shards, axes):
        shp = list(s.shape)
        shp[ax] *= N_CHIPS
        fulls.append(jax.ShapeDtypeStruct(tuple(shp), s.dtype))

    def body(*refs):
        ins, outs = refs[:n], refs[n:2 * n]
        send_sems, recv_sems, fsend_sems, frecv_sems, loc_sems = refs[2 * n:]
        x, y, c = _my_place()
        me_chip = 2 * x + y
        peers = [(1 - x, y), (x, 1 - y), (1 - x, 1 - y)]

        def region(w, chip, half):
            L = ins[w].shape[0]
            lh = L // 2
            return _shard_region(outs[w], axes[w], chip, half * lh, lh, ins[w].shape[axes[w]])

        local = []
        for w in range(n):
            L = ins[w].shape[0]
            cp = pltpu.make_async_copy(ins[w], _shard_region(outs[w], axes[w], me_chip, 0, L, ins[w].shape[axes[w]]),
                                       loc_sems.at[w])
            cp.start()
            local.append(cp)
        sends = []
        for j, (px, py) in enumerate(peers):
            for w in range(n):
                lh = ins[w].shape[0] // 2
                cp = pltpu.make_async_remote_copy(
                    src_ref=ins[w].at[pl.ds(c * lh, lh)], dst_ref=region(w, me_chip, c),
                    send_sem=send_sems.at[j * n + w], recv_sem=recv_sems.at[j * n + w],
                    device_id=(px, py, c), device_id_type=MESH)
                cp.start()
                sends.append(cp)
        fwds = []
        for j, (px, py) in enumerate(peers):
            pchip = 2 * px + py
            for w in range(n):
                got = region(w, pchip, c)
                pltpu.make_async_remote_copy(
                    src_ref=got, dst_ref=got, send_sem=send_sems.at[j * n + w], recv_sem=recv_sems.at[j * n + w],
                    device_id=(px, py, c), device_id_type=MESH).wait_recv()
                cp = pltpu.make_async_remote_copy(
                    src_ref=got, dst_ref=got, send_sem=fsend_sems.at[j * n + w], recv_sem=frecv_sems.at[j * n + w],
                    device_id=(x, y, 1 - c), device_id_type=MESH)
                cp.start()
                fwds.append(cp)
        for j, (px, py) in enumerate(peers):
            pchip = 2 * px + py
            for w in range(n):
                got = region(w, pchip, 1 - c)
                pltpu.make_async_remote_copy(
                    src_ref=got, dst_ref=got, send_sem=fsend_sems.at[j * n + w], recv_sem=frecv_sems.at[j * n + w],
                    device_id=(x, y, 1 - c), device_id_type=MESH).wait_recv()
        for cp in sends + fwds:
            cp.wait_send()
        for cp in local:
            cp.wait()

    any_spec = pl.BlockSpec(memory_space=pl.ANY)
    return pl.pallas_call(
        body, name="allgather_weights", out_shape=fulls,
        in_specs=[any_spec] * n, out_specs=[any_spec] * n,
        scratch_shapes=[pltpu.SemaphoreType.DMA((3 * n,))] * 4 + [pltpu.SemaphoreType.DMA((n,))],
    )(*shards)


def _sibling_exchange_halves(grads):
    n = len(grads)
    outs = [jax.ShapeDtypeStruct((g.shape[0] // 2,) + g.shape[1:], g.dtype) for g in grads]

    def body(*refs):
        ins, rs = refs[:n], refs[n:2 * n]
        send_sems, recv_sems = refs[2 * n:]
        x, y, c = _my_place()
        cps = []
        for w in range(n):
            lh = ins[w].shape[0] // 2
            cp = pltpu.make_async_remote_copy(
                src_ref=ins[w].at[pl.ds((1 - c) * lh, lh)], dst_ref=rs[w],
                send_sem=send_sems.at[w], recv_sem=recv_sems.at[w],
                device_id=(x, y, 1 - c), device_id_type=MESH)
            cp.start()
            cps.append(cp)
        for cp in cps:
            cp.wait()

    any_spec = pl.BlockSpec(memory_space=pl.ANY)
    return pl.pallas_call(
        body, name="grad_sibling_exchange", out_shape=outs,
        in_specs=[any_spec] * n, out_specs=[any_spec] * n,
        scratch_shapes=[pltpu.SemaphoreType.DMA((n,))] * 2,
    )(*grads)


def _chip_sum(g, r, c_arr, *, name):
    L, K, N = g.shape
    lh = L // 2
    tk = _pick(K, (256, 128))

    def body(c_ref, g_ref, r_ref, f_ref, b_ref):
        s = g_ref[...] + r_ref[...]
        f_ref[...] = s
        b_ref[...] = s.astype(BF16)

    half = pl.BlockSpec((None, tk, N), lambda l, i, c_ref: (l, i, 0))
    grid_spec = pltpu.PrefetchScalarGridSpec(
        num_scalar_prefetch=1, grid=(lh, K // tk),
        in_specs=[pl.BlockSpec((None, tk, N), lambda l, i, c_ref: (c_ref[0] * lh + l, i, 0)), half],
        out_specs=[half, half])
    return pl.pallas_call(
        body, name=name,
        out_shape=[jax.ShapeDtypeStruct((lh, K, N), F32), jax.ShapeDtypeStruct((lh, K, N), BF16)],
        grid_spec=grid_spec,
        compiler_params=pltpu.CompilerParams(dimension_semantics=("arbitrary", "arbitrary"),
                                             vmem_limit_bytes=_vmem(2 * tk * N * 14)),
    )(c_arr, g, r)


def _chip_exchange(sums_bf16, axes):
    n = len(sums_bf16)
    outs = []
    for s, ax in zip(sums_bf16, axes):
        shp = list(s.shape)
        shp[ax] //= N_CHIPS
        outs.append(jax.ShapeDtypeStruct((3,) + tuple(shp), s.dtype))

    def body(*refs):
        ins, rs = refs[:n], refs[n:2 * n]
        send_sems, recv_sems = refs[2 * n:]
        x, y, c = _my_place()
        peers = [(1 - x, y), (x, 1 - y), (1 - x, 1 - y)]
        cps = []
        for j, (px, py) in enumerate(peers):
            for w in range(n):
                lh = ins[w].shape[0]
                size = ins[w].shape[axes[w]] // N_CHIPS
                cp = pltpu.make_async_remote_copy(
                    src_ref=_shard_region(ins[w], axes[w], 2 * px + py, 0, lh, size), dst_ref=rs[w].at[j],
                    send_sem=send_sems.at[j * n + w], recv_sem=recv_sems.at[j * n + w],
                    device_id=(px, py, c), device_id_type=MESH)
                cp.start()
                cps.append(cp)
        for cp in cps:
            cp.wait()

    any_spec = pl.BlockSpec(memory_space=pl.ANY)
    return pl.pallas_call(
        body, name="grad_chip_exchange", out_shape=outs,
        in_specs=[any_spec] * n, out_specs=[any_spec] * n,
        scratch_shapes=[pltpu.SemaphoreType.DMA((3 * n,))] * 2,
    )(*sums_bf16)


def _final_sum(own, recv, axis, place, *, name):
    lh, K, N = own.shape
    _, _, Ks, Ns = recv.shape
    tk = _pick(Ks, (256, 176, 128))
    nk = Ks // tk

    def body(p_ref, o_ref, r_ref, out_ref):
        out_ref[...] = ((o_ref[...] + r_ref[0].astype(F32)) + r_ref[1].astype(F32)) + r_ref[2].astype(F32)

    if axis == 1:
        own_spec = pl.BlockSpec((None, tk, Ns), lambda l, i, p: (l, p[0] * nk + i, 0))
    else:
        own_spec = pl.BlockSpec((None, tk, Ns), lambda l, i, p: (l, i, p[0]))
    grid_spec = pltpu.PrefetchScalarGridSpec(
        num_scalar_prefetch=1, grid=(lh, nk),
        in_specs=[own_spec, pl.BlockSpec((3, None, tk, Ns), lambda l, i, p: (0, l, i, 0))],
        out_specs=pl.BlockSpec((None, tk, Ns), lambda l, i, p: (p[1] * lh + l, i, 0)))
    return pl.pallas_call(
        body, name=name, out_shape=jax.ShapeDtypeStruct((2 * lh, Ks, Ns), F32), grid_spec=grid_spec,
        compiler_params=pltpu.CompilerParams(dimension_semantics=("arbitrary", "arbitrary"),
                                             vmem_limit_bytes=_vmem(2 * tk * Ns * 14 + 4 * tk * Ns * 4)),
    )(place, own, recv)


def _sibling_fill(gshards):
    n = len(gshards)

    def body(*refs):
        ins, outs = refs[:n], refs[n:2 * n]
        send_sems, recv_sems = refs[2 * n:]
        x, y, c = _my_place()
        cps = []
        for w in range(n):
            lh = ins[w].shape[0] // 2
            cp = pltpu.make_async_remote_copy(
                src_ref=outs[w].at[pl.ds(c * lh, lh)], dst_ref=outs[w].at[pl.ds(c * lh, lh)],
                send_sem=send_sems.at[w], recv_sem=recv_sems.at[w],
                device_id=(x, y, 1 - c), device_id_type=MESH)
            cp.start()
            cps.append(cp)
        for cp in cps:
            cp.wait()

    any_spec = pl.BlockSpec(memory_space=pl.ANY)
    return pl.pallas_call(
        body, name="grad_sibling_fill", out_shape=[jax.ShapeDtypeStruct(g.shape, g.dtype) for g in gshards],
        in_specs=[any_spec] * n, out_specs=[any_spec] * n,
        input_output_aliases={w: w for w in range(n)},
        scratch_shapes=[pltpu.SemaphoreType.DMA((n,))] * 2,
    )(*gshards)


def _allreduce_small(p):
    def body(p_ref, o_ref, r0, r1, r2, send_sems, recv_sems):
        x, y, c = _my_place()
        o_ref[...] = p_ref[...]
        for s, (peer, rbuf) in enumerate([((x, y, 1 - c), r0), ((1 - x, y, c), r1), ((x, 1 - y, c), r2)]):
            cp = pltpu.make_async_remote_copy(src_ref=o_ref, dst_ref=rbuf, send_sem=send_sems.at[s],
                                              recv_sem=recv_sems.at[s], device_id=peer, device_id_type=MESH)
            cp.start()
            cp.wait()
            o_ref[...] = o_ref[...] + rbuf[...]

    vm = pl.BlockSpec(memory_space=pltpu.VMEM)
    return pl.pallas_call(
        body, name="allreduce_small", out_shape=jax.ShapeDtypeStruct(p.shape, F32),
        in_specs=[vm], out_specs=vm,
        scratch_shapes=[pltpu.VMEM(p.shape, F32)] * 3 + [pltpu.SemaphoreType.DMA((3,))] * 2,
        compiler_params=pltpu.CompilerParams(vmem_limit_bytes=_vmem(6 * _nbytes(p.shape, F32))),
    )(p)


def _pack_rows(parts):
    rows, metas = [], []
    for a in parts:
        flat = a.reshape(-1)
        nrow = -(-flat.shape[0] // LANES)
        nrow = -(-nrow // 8) * 8
        flat = jnp.pad(flat, (0, nrow * LANES - flat.shape[0]))
        rows.append(flat.reshape(nrow, LANES))
        metas.append((a.shape, nrow))
    return jnp.concatenate(rows, axis=0), metas


def _unpack_rows(packed, metas):
    out, r0 = [], 0
    for shape, nrow in metas:
        size = int(np.prod(shape))
        out.append(packed[r0:r0 + nrow].reshape(-1)[:size].reshape(shape))
        r0 += nrow
    return out


def kernel(x, positions, pre_mix_g, post_mix_g, pre_ffn_g, post_ffn_g, a_w_in, a_b_in, a_ln_g, a_ln_b, a_w_s, a_b_s, a_w_out, b_w_qkv, b_b_qkv, b_sinks, b_w_o, ffn_w_gu, ffn_w_down, loss_target, m_pre_mix_g, m_post_mix_g, m_pre_ffn_g, m_post_ffn_g, m_a_w_in, m_a_b_in, m_a_ln_g, m_a_ln_b, m_a_w_s, m_a_b_s, m_a_w_out, m_b_w_qkv, m_b_b_qkv, m_b_sinks, m_b_w_o, m_ffn_w_gu, m_ffn_w_down, v_pre_mix_g, v_post_mix_g, v_pre_ffn_g, v_post_ffn_g, v_a_w_in, v_a_b_in, v_a_ln_g, v_a_ln_b, v_a_w_s, v_a_b_s, v_a_w_out, v_b_w_qkv, v_b_b_qkv, v_b_sinks, v_b_w_o, v_ffn_w_gu, v_ffn_w_down):
    depth, D = pre_mix_g.shape
    T = x.shape[1]
    xi, yi, ci = _my_place()
    chip = 2 * xi + yi
    place = jnp.stack([chip, ci]).astype(jnp.int32)
    c_arr = jnp.reshape(ci, (1,)).astype(jnp.int32)

    big_names = ["a_w_in", "a_w_out", "b_w_qkv", "b_w_o", "ffn_w_gu", "ffn_w_down"]
    big = [a_w_in, a_w_out, b_w_qkv, b_w_o, ffn_w_gu, ffn_w_down]
    big_m = [m_a_w_in, m_a_w_out, m_b_w_qkv, m_b_w_o, m_ffn_w_gu, m_ffn_w_down]
    big_v = [v_a_w_in, v_a_w_out, v_b_w_qkv, v_b_w_o, v_ffn_w_gu, v_ffn_w_down]
    axes = [2, 1, 2, 1, 2, 1]
    shards_bf16 = [_cast_bf16(w, name="cast_" + nm) for w, nm in zip(big, big_names)]
    W_in, W_out, W_qkv, W_o, W_gu, W_down = _allgather_weights(shards_bf16, axes)

    h = x[0]
    target = loss_target[0]
    ctab, stab = _rope_tables(positions[0])
    q_width = W_o.shape[1]
    kv_width = N_KV_HEADS * HEAD_DIM
    row = lambda a, i: a[i:i + 1]

    nq = b_b_qkv.shape[1]
    bq_full = jnp.zeros((b_b_qkv.shape[0], N_CHIPS * nq), F32)
    bq_full = lax.dynamic_update_slice(bq_full, jnp.where(ci == 0, b_b_qkv, 0.0), (0, chip * nq))
    bq_packed, bq_meta = _pack_rows([bq_full])
    b_qkv_full = _unpack_rows(_allreduce_small(bq_packed), bq_meta)[0]

    saved = []
    for i in range(depth):
        j = i // 2
        s = {"h": h}
        hn = _rms_fwd(h, row(pre_mix_g, i), out_dtype=BF16, name=f"rms_pre_mix_{i}")
        s["hn"] = hn
        if i % 2 == 0:
            pre = _matmul(hn, W_in, mode="nn", b_l=j, bias=row(a_b_in, j), out_dtype=F32, name=f"gmlp_in_{i}")
            gated = _sgu_fwd(pre, row(a_ln_g, j), row(a_ln_b, j), a_w_s[j], a_b_s[j].T, name=f"sgu_fwd_{i}")
            mix = _matmul(gated, W_out, mode="nn", b_l=j, out_dtype=F32, name=f"gmlp_out_{i}")
            s.update(pre=pre, gated=gated)
        else:
            qkv = _matmul(hn, W_qkv, mode="nn", b_l=j, bias=row(b_qkv_full, j), out_dtype=F32, name=f"attn_qkv_{i}")
            qr, kr, vr = _rope_fwd(qkv, ctab, stab, q_width=q_width, kv_width=kv_width, name=f"rope_fwd_{i}")
            o = _attn_fwd(qr, kr, vr, row(b_sinks, j), name=f"attn_fwd_{i}")
            mix = _matmul(o, W_o, mode="nn", b_l=j, out_dtype=F32, name=f"attn_o_{i}")
            s.update(qr=qr, kr=kr, vr=vr, o=o)
        s["mix"] = mix
        h1 = _rms_res(h, mix, row(post_mix_g, i), name=f"rms_post_mix_{i}")
        s["h1"] = h1
        fn = _rms_fwd(h1, row(pre_ffn_g, i), out_dtype=BF16, name=f"rms_pre_ffn_{i}")
        g_pre, u_pre, act = _ffn_up(fn, W_gu, i, name=f"ffn_up_{i}")
        f = _matmul(act, W_down, mode="nn", b_l=i, out_dtype=F32, name=f"ffn_down_{i}")
        h = _rms_res(h1, f, row(post_ffn_g, i), name=f"rms_post_ffn_{i}")
        s.update(fn=fn, g_pre=g_pre, u_pre=u_pre, act=act, f=f)
        saved.append(s)

    dh, loss_part = _loss_and_grad(h, target, name="loss")
    loss = lax.psum(loss_part[0, 0], ("x", "y", "c"))

    G = {nm: lax.empty(w.shape, F32) for nm, w in
         zip(big_names, [W_in, W_out, W_qkv, W_o, W_gu, W_down])}
    small = {}
    g_pre_mix, g_post_mix, g_pre_ffn, g_post_ffn = [None] * depth, [None] * depth, [None] * depth, [None] * depth
    for i in reversed(range(depth)):
        j = i // 2
        s = saved[i]
        df, g_post_ffn[i] = _rms_bwd(s["f"], row(post_ffn_g, i), dh, None, out_dtype=BF16, name=f"rms_post_ffn_bwd_{i}")
        G["ffn_w_down"] = _matmul(s["act"], df, mode="tn", into=G["ffn_w_down"], o_l=i, out_dtype=F32, name=f"ffn_down_dw_{i}")
        dg_, du_ = _ffn_down_dx(df, W_down, i, s["g_pre"], s["u_pre"], name=f"ffn_down_dx_{i}")
        hid = dg_.shape[1]
        tq_w = _pick(hid, (1408, 768, 512, 256, 128))
        G["ffn_w_gu"] = _matmul(s["fn"], dg_, mode="tn", into=G["ffn_w_gu"], o_l=i, tq=tq_w, out_dtype=F32,
                                name=f"ffn_g_dw_{i}")
        G["ffn_w_gu"] = _matmul(s["fn"], du_, mode="tn", into=G["ffn_w_gu"], o_l=i, tq=tq_w, q_off=hid // tq_w,
                                out_dtype=F32, name=f"ffn_u_dw_{i}")
        tr_x = _pick(hid, (1408, 768, 512, 256, 128))
        dfn_g = _matmul(dg_, W_gu, mode="nt", b_l=i, tr=tr_x, out_dtype=F32, name=f"ffn_g_dx_{i}")
        dfn = _matmul(du_, W_gu, mode="nt", b_l=i, tr=tr_x, b_r_off=hid // tr_x, bias=dfn_g, out_dtype=F32,
                      name=f"ffn_u_dx_{i}")
        dh1, g_pre_ffn[i] = _rms_bwd(s["h1"], row(pre_ffn_g, i), dfn, dh, out_dtype=F32, name=f"rms_pre_ffn_bwd_{i}")
        dmix, g_post_mix[i] = _rms_bwd(s["mix"], row(post_mix_g, i), dh1, None, out_dtype=BF16, name=f"rms_post_mix_bwd_{i}")
        if i % 2 == 0:
            G["a_w_out"] = _matmul(s["gated"], dmix, mode="tn", into=G["a_w_out"], o_l=j, out_dtype=F32, name=f"gmlp_out_dw_{i}")
            dgated = _matmul(dmix, W_out, mode="nt", b_l=j, out_dtype=F32, name=f"gmlp_out_dx_{i}")
            dpre, dws, dbsT, dlng, dlnb, dbin = _sgu_bwd(s["pre"], dgated, row(a_ln_g, j), row(a_ln_b, j),
                                                         a_w_s[j], a_b_s[j].T, name=f"sgu_bwd_{i}")
            small[("a_w_s", j)] = dws
            small[("a_b_s", j)] = dbsT.T
            small[("a_ln_g", j)] = dlng
            small[("a_ln_b", j)] = dlnb
            small[("a_b_in", j)] = dbin
            G["a_w_in"] = _matmul(s["hn"], dpre, mode="tn", into=G["a_w_in"], o_l=j, out_dtype=F32, name=f"gmlp_in_dw_{i}")
            dhn = _matmul(dpre, W_in, mode="nt", b_l=j, out_dtype=F32, name=f"gmlp_in_dx_{i}")
        else:
            G["b_w_o"] = _matmul(s["o"], dmix, mode="tn", into=G["b_w_o"], o_l=j, out_dtype=F32, name=f"attn_o_dw_{i}")
            do = _matmul(dmix, W_o, mode="nt", b_l=j, out_dtype=BF16, name=f"attn_o_dx_{i}")
            dq, dkp, dkc, dvp, dvc, dsk = _attn_bwd(s["qr"], s["kr"], s["vr"], row(b_sinks, j), do, name=f"attn_bwd_{i}")
            dqkv, dbq = _rope_bwd(dq, dkp, dkc, dvp, dvc, ctab, stab, name=f"rope_bwd_{i}")
            small[("b_sinks", j)] = dsk[:, :b_sinks.shape[1]]
            small[("b_b_qkv", j)] = dbq
            G["b_w_qkv"] = _matmul(s["hn"], dqkv, mode="tn", into=G["b_w_qkv"], o_l=j, out_dtype=F32, name=f"attn_qkv_dw_{i}")
            dhn = _matmul(dqkv, W_qkv, mode="nt", b_l=j, out_dtype=F32, name=f"attn_qkv_dx_{i}")
        dh, g_pre_mix[i] = _rms_bwd(s["h"], row(pre_mix_g, i), dhn, dh1, out_dtype=F32, name=f"rms_pre_mix_bwd_{i}")
    grad_x = dh[None]

    g_list = [G[nm] for nm in big_names]
    sib = _sibling_exchange_halves(g_list)
    sums = [_chip_sum(g, r, c_arr, name="chip_sum_" + nm) for g, r, nm in zip(g_list, sib, big_names)]
    recv = _chip_exchange([sb for _, sb in sums], axes)
    halves = [_final_sum(sf, r, ax, place, name="final_sum_" + nm)
              for (sf, _), r, ax, nm in zip(sums, recv, axes, big_names)]
    big_g = _sibling_fill(halves)

    n_a, n_b = a_b_in.shape[0], b_sinks.shape[0]
    stack = lambda key, n: jnp.concatenate([small[(key, j)] for j in range(n)], axis=0)
    small_parts = [
        jnp.concatenate(g_pre_mix, axis=0), jnp.concatenate(g_post_mix, axis=0),
        jnp.concatenate(g_pre_ffn, axis=0), jnp.concatenate(g_post_ffn, axis=0),
        stack("a_b_in", n_a), stack("a_ln_g", n_a), stack("a_ln_b", n_a),
        jnp.stack([small[("a_w_s", j)] for j in range(n_a)]), jnp.stack([small[("a_b_s", j)] for j in range(n_a)]),
        stack("b_b_qkv", n_b), stack("b_sinks", n_b),
    ]
    packed, metas = _pack_rows(small_parts)
    red = _unpack_rows(_allreduce_small(packed), metas)
    (gr_pre_mix, gr_post_mix, gr_pre_ffn, gr_post_ffn, gr_b_in, gr_ln_g, gr_ln_b, gr_w_s, gr_b_s,
     gr_b_qkv_full, gr_sinks) = red
    gr_b_qkv = lax.dynamic_slice(gr_b_qkv_full, (0, chip * nq), (gr_b_qkv_full.shape[0], nq))

    grads = {"pre_mix_g": gr_pre_mix, "post_mix_g": gr_post_mix, "pre_ffn_g": gr_pre_ffn, "post_ffn_g": gr_post_ffn,
             "a_b_in": gr_b_in, "a_ln_g": gr_ln_g, "a_ln_b": gr_ln_b, "a_w_s": gr_w_s, "a_b_s": gr_b_s,
             "b_b_qkv": gr_b_qkv, "b_sinks": gr_sinks}
    for nm, g in zip(big_names, big_g):
        grads[nm] = g
    weights = {"pre_mix_g": (pre_mix_g, m_pre_mix_g, v_pre_mix_g), "post_mix_g": (post_mix_g, m_post_mix_g, v_post_mix_g),
               "pre_ffn_g": (pre_ffn_g, m_pre_ffn_g, v_pre_ffn_g), "post_ffn_g": (post_ffn_g, m_post_ffn_g, v_post_ffn_g),
               "a_b_in": (a_b_in, m_a_b_in, v_a_b_in), "a_ln_g": (a_ln_g, m_a_ln_g, v_a_ln_g),
               "a_ln_b": (a_ln_b, m_a_ln_b, v_a_ln_b), "a_w_s": (a_w_s, m_a_w_s, v_a_w_s), "a_b_s": (a_b_s, m_a_b_s, v_a_b_s),
               "b_b_qkv": (b_b_qkv, m_b_b_qkv, v_b_b_qkv), "b_sinks": (b_sinks, m_b_sinks, v_b_sinks)}
    for nm, w, m, v in zip(big_names, big, big_m, big_v):
        weights[nm] = (w, m, v)
    order = ["pre_mix_g", "post_mix_g", "pre_ffn_g", "post_ffn_g", "a_w_in", "a_b_in", "a_ln_g", "a_ln_b", "a_w_s",
             "a_b_s", "a_w_out", "b_w_qkv", "b_b_qkv", "b_sinks", "b_w_o", "ffn_w_gu", "ffn_w_down"]
    deltas, new_m, new_v = {}, {}, {}
    for nm in order:
        w, m, v = weights[nm]
        fn_ = _adamw_big if nm in big_names else _adamw_small
        deltas[nm], new_m[nm], new_v[nm] = fn_(w, grads[nm], m, v, name="adamw_" + nm)
    return (loss, grad_x, *[grads[nm] for nm in order], *[deltas[nm] for nm in order],
            *[new_m[nm] for nm in order], *[new_v[nm] for nm in order])
```

```python
import functools
import math

import jax
import jax.numpy as jnp
import numpy as np
from jax import lax
from jax.experimental import pallas as pl
from jax.experimental.pallas import tpu as pltpu

F32 = jnp.float32
BF16 = jnp.bfloat16
MESH = pl.DeviceIdType.MESH

HEAD_DIM = 64
N_KV_HEADS = 4
ROPE_DIM = 16
ROPE_THETA = 500000.0
CHUNK = 128
GMLP_GROUPS = 8
RMS_EPS = 1e-6
LN_EPS = 1e-5
NEG_INF = -1e30
ADAM_LR = 0.001
ADAM_B1 = 0.9
ADAM_B2 = 0.999
ADAM_EPS = 1e-08
ADAM_WD = 0.01
ADAM_STEP = 10

N_CHIPS = 4
LANES = 128
VMEM_CAP = 58 * 1024 * 1024


def _vmem(est_bytes):
    assert est_bytes < VMEM_CAP
    return VMEM_CAP


def _pick(n, cands):
    for c in cands:
        if c <= n and n % c == 0:
            return c
    return n


def _nbytes(shape, dtype):
    return int(np.prod(shape)) * jnp.dtype(dtype).itemsize


def _matmul(a, b, *, mode, out_dtype, name, a_l=None, b_l=None, bias=None, into=None, o_l=None,
            q_off=0, b_r_off=0, tp=None, tq=None, tr=None):
    a2 = a.shape[-2:]
    b2 = b.shape[-2:]
    if mode == "nn":
        (P, R), (R2, Q) = a2, b2
    elif mode == "nt":
        (P, R), (Q, R2) = a2, b2
    else:
        (R, P), (R2, Q) = a2, b2
    assert R == R2 or (mode == "nt" and R2 % R == 0), (mode, a.shape, b.shape)
    tp = tp or _pick(P, (1024, 1408, 512, 384, 256, 128))
    tq = tq or _pick(Q, (1024, 1408, 768, 512, 384, 256, 128))
    tr = tr or _pick(R, (2048, 1408, 1024, 512, 256, 128))
    assert P % tp == 0 and Q % tq == 0 and R % tr == 0
    nk = R // tr
    dims = {"nn": (((1,), (0,)), ((), ())), "nt": (((1,), (1,)), ((), ())), "tn": (((0,), (0,)), ((), ()))}[mode]

    def lead(l, blk, idx):
        if l is None:
            return pl.BlockSpec(blk, idx)
        return pl.BlockSpec((None,) + blk, lambda i, j, k: (l,) + idx(i, j, k))

    if mode == "nn":
        a_spec = lead(a_l, (tp, tr), lambda i, j, k: (i, k))
        b_spec = lead(b_l, (tr, tq), lambda i, j, k: (k, j))
    elif mode == "nt":
        a_spec = lead(a_l, (tp, tr), lambda i, j, k: (i, k))
        b_spec = lead(b_l, (tq, tr), lambda i, j, k: (j, k + b_r_off))
    else:
        a_spec = lead(a_l, (tr, tp), lambda i, j, k: (k, i))
        b_spec = lead(b_l, (tr, tq), lambda i, j, k: (k, j))
    in_specs = [a_spec, b_spec]
    args = [a, b]
    if bias is not None:
        if bias.shape[0] == 1:
            in_specs.append(pl.BlockSpec((1, tq), lambda i, j, k: (0, j)))
        else:
            in_specs.append(pl.BlockSpec((tp, tq), lambda i, j, k: (i, j)))
        args.append(bias)
    aliases = {}
    if into is not None:
        in_specs.append(pl.BlockSpec(memory_space=pl.ANY))
        args.append(into)
        aliases = {len(args) - 1: 0}
        out_shape = jax.ShapeDtypeStruct(into.shape, into.dtype)
        out_dtype = into.dtype
        if o_l is None:
            out_spec = pl.BlockSpec((tp, tq), lambda i, j, k: (i, j + q_off))
        else:
            out_spec = pl.BlockSpec((None, tp, tq), lambda i, j, k: (o_l, i, j + q_off))
    else:
        out_shape = jax.ShapeDtypeStruct((P, Q), out_dtype)
        out_spec = pl.BlockSpec((tp, tq), lambda i, j, k: (i, j))
    has_bias = bias is not None
    has_into = into is not None

    def body(*refs):
        a_ref, b_ref = refs[0], refs[1]
        pos = 2
        bias_ref = None
        if has_bias:
            bias_ref = refs[pos]
            pos += 1
        if has_into:
            pos += 1
        o_ref = refs[pos]
        acc_ref = refs[pos + 1] if nk > 1 else None
        part = lax.dot_general(a_ref[...], b_ref[...], dims, preferred_element_type=F32)

        def finish(acc):
            if has_bias:
                acc = acc + bias_ref[...]
            o_ref[...] = acc.astype(out_dtype)

        if nk == 1:
            finish(part)
        else:
            k = pl.program_id(2)

            @pl.when(k == 0)
            def _():
                acc_ref[...] = part

            @pl.when(k > 0)
            def _():
                acc_ref[...] += part

            @pl.when(k == nk - 1)
            def _():
                finish(acc_ref[...])

    est = 2 * (_nbytes((tp, tr), a.dtype) + _nbytes((tr, tq), b.dtype) + _nbytes((tp, tq), out_dtype)) + 3 * tp * tq * 4
    return pl.pallas_call(
        body, name=name, out_shape=out_shape,
        grid=(P // tp, Q // tq, nk),
        in_specs=in_specs, out_specs=out_spec,
        scratch_shapes=[pltpu.VMEM((tp, tq), F32)] if nk > 1 else [],
        input_output_aliases=aliases,
        compiler_params=pltpu.CompilerParams(
            dimension_semantics=("parallel", "parallel", "arbitrary"), vmem_limit_bytes=_vmem(est)),
    )(*args)


def _row_call(body, ins, outs, *, name, rows, tr, acc_outs=(), est=0):
    in_specs = []
    for arr, kind in ins:
        if kind == "row":
            in_specs.append(pl.BlockSpec((tr, arr.shape[1]), lambda i: (i, 0)))
        else:
            nd = arr.ndim
            in_specs.append(pl.BlockSpec(arr.shape, lambda i, nd=nd: (0,) * nd))
    out_shapes = [jax.ShapeDtypeStruct(s, d) for s, d in outs] + [jax.ShapeDtypeStruct(s, d) for s, d in acc_outs]
    out_specs = [pl.BlockSpec((tr, s[1]), lambda i: (i, 0)) for s, _ in outs]
    out_specs += [pl.BlockSpec(s, lambda i, nd=len(s): (0,) * nd) for s, _ in acc_outs]
    res = pl.pallas_call(
        body, name=name, out_shape=out_shapes, grid=(rows // tr,), in_specs=in_specs, out_specs=out_specs,
        compiler_params=pltpu.CompilerParams(dimension_semantics=("arbitrary",), vmem_limit_bytes=_vmem(est)),
    )(*[a for a, _ in ins])
    return res


def _rms_fwd(x, g, *, out_dtype, name):
    T, D = x.shape
    tr = _pick(T, (512, 256, 128))

    def body(x_ref, g_ref, o_ref):
        xv = x_ref[...]
        r = lax.rsqrt(jnp.mean(xv * xv, axis=-1, keepdims=True) + RMS_EPS)
        o_ref[...] = (xv * r * g_ref[...]).astype(out_dtype)

    return _row_call(body, [(x, "row"), (g, "full")], [((T, D), out_dtype)], name=name, rows=T, tr=tr,
                     est=8 * tr * D * 4)[0]


def _rms_res(h, y, g, *, name):
    T, D = h.shape
    tr = _pick(T, (512, 256, 128))

    def body(h_ref, y_ref, g_ref, o_ref):
        yv = y_ref[...]
        r = lax.rsqrt(jnp.mean(yv * yv, axis=-1, keepdims=True) + RMS_EPS)
        o_ref[...] = h_ref[...] + yv * r * g_ref[...]

    return _row_call(body, [(h, "row"), (y, "row"), (g, "full")], [((T, D), F32)], name=name, rows=T, tr=tr,
                     est=10 * tr * D * 4)[0]


def _rms_bwd(x, g, dy, dres, *, out_dtype, name):
    T, D = x.shape
    tr = _pick(T, (512, 256, 128))
    has_res = dres is not None

    def body(*refs):
        if has_res:
            x_ref, g_ref, dy_ref, dr_ref, dx_ref, dg_ref = refs
        else:
            x_ref, g_ref, dy_ref, dx_ref, dg_ref = refs
        xv = x_ref[...]
        r = lax.rsqrt(jnp.mean(xv * xv, axis=-1, keepdims=True) + RMS_EPS)
        xhat = xv * r
        dyv = dy_ref[...].astype(F32)
        dxn = dyv * g_ref[...]
        dx = r * (dxn - xhat * jnp.mean(dxn * xhat, axis=-1, keepdims=True))
        if has_res:
            dx = dx + dr_ref[...]
        dx_ref[...] = dx.astype(out_dtype)
        part = jnp.sum(dyv * xhat, axis=0, keepdims=True)

        @pl.when(pl.program_id(0) == 0)
        def _():
            dg_ref[...] = part

        @pl.when(pl.program_id(0) > 0)
        def _():
            dg_ref[...] += part

    ins = [(x, "row"), (g, "full"), (dy, "row")] + ([(dres, "row")] if has_res else [])
    dx, dg = _row_call(body, ins, [((T, D), out_dtype)], name=name, rows=T, tr=tr, acc_outs=[((1, D), F32)],
                       est=12 * tr * D * 4)
    return dx, dg


def _ffn_up(fn, w_gu, l, *, name):
    T, D = fn.shape
    H = w_gu.shape[2] // 2
    tp = _pick(T, (512, 256, 128))
    tq = _pick(H, (1408, 768, 512, 256, 128))
    nj = H // tq

    def body(a_ref, wg_ref, wu_ref, g_ref, u_ref, act_ref):
        a = a_ref[...]
        g = jnp.dot(a, wg_ref[...], preferred_element_type=F32)
        u = jnp.dot(a, wu_ref[...], preferred_element_type=F32)
        g_ref[...] = g
        u_ref[...] = u
        act_ref[...] = (g * jax.nn.sigmoid(g) * u).astype(BF16)

    tile = pl.BlockSpec((tp, tq), lambda j, i: (i, j))
    est = 2 * (tp * D * 2 + 2 * D * tq * 2 + 2 * tp * tq * 4 + tp * tq * 2) + 4 * tp * tq * 4
    return pl.pallas_call(
        body, name=name,
        out_shape=[jax.ShapeDtypeStruct((T, H), F32), jax.ShapeDtypeStruct((T, H), F32),
                   jax.ShapeDtypeStruct((T, H), BF16)],
        grid=(nj, T // tp),
        in_specs=[pl.BlockSpec((tp, D), lambda j, i: (i, 0)),
                  pl.BlockSpec((None, D, tq), lambda j, i: (l, 0, j)),
                  pl.BlockSpec((None, D, tq), lambda j, i: (l, 0, j + nj))],
        out_specs=[tile, tile, tile],
        compiler_params=pltpu.CompilerParams(dimension_semantics=("parallel", "parallel"),
                                             vmem_limit_bytes=_vmem(est)),
    )(fn, w_gu, w_gu)


def _ffn_down_dx(df, w_down, l, g, u, *, name):
    T, D = df.shape
    H = w_down.shape[1]
    tp = _pick(T, (512, 256, 128))
    tq = _pick(H, (1408, 768, 512, 256, 128))

    def body(a_ref, w_ref, g_ref, u_ref, dg_ref, du_ref):
        da = lax.dot_general(a_ref[...], w_ref[...], (((1,), (1,)), ((), ())), preferred_element_type=F32)
        gv = g_ref[...]
        sg = jax.nn.sigmoid(gv)
        silu = gv * sg
        dg_ref[...] = (da * u_ref[...] * (sg + silu * (1.0 - sg))).astype(BF16)
        du_ref[...] = (da * silu).astype(BF16)

    tile = pl.BlockSpec((tp, tq), lambda j, i: (i, j))
    est = 2 * (tp * D * 2 + tq * D * 2 + 2 * tp * tq * 4 + 2 * tp * tq * 2) + 5 * tp * tq * 4
    return pl.pallas_call(
        body, name=name,
        out_shape=[jax.ShapeDtypeStruct((T, H), BF16), jax.ShapeDtypeStruct((T, H), BF16)],
        grid=(H // tq, T // tp),
        in_specs=[pl.BlockSpec((tp, D), lambda j, i: (i, 0)),
                  pl.BlockSpec((None, tq, D), lambda j, i: (l, j, 0)), tile, tile],
        out_specs=[tile, tile],
        compiler_params=pltpu.CompilerParams(dimension_semantics=("parallel", "parallel"),
                                             vmem_limit_bytes=_vmem(est)),
    )(df, w_down, g, u)


def _loss_and_grad(y, target, *, name):
    T, D = y.shape
    tr = _pick(T, (512, 256, 128))

    def body(y_ref, t_ref, dy_ref, l_ref):
        e = y_ref[...] - t_ref[...]
        dy_ref[...] = e * (1.0 / D)
        part = jnp.sum(jnp.sum(e * e, axis=1, keepdims=True), axis=0, keepdims=True) * (0.5 / D)

        @pl.when(pl.program_id(0) == 0)
        def _():
            l_ref[...] = part

        @pl.when(pl.program_id(0) > 0)
        def _():
            l_ref[...] += part

    dy, l = _row_call(body, [(y, "row"), (target, "row")], [((T, D), F32)], name=name, rows=T, tr=tr,
                      acc_outs=[((1, 1), F32)], est=8 * tr * D * 4)
    return dy, l


_SQRT_HALF = 0.7071067811865476
_INV_SQRT_2PI = 0.3989422804014327


def _gelu_parts(x):
    cdf = 0.5 * (1.0 + lax.erf(x * _SQRT_HALF))
    return cdf


def _sgu_common(pre, lng, lnb, W):
    cdf = _gelu_parts(pre)
    z = pre * cdf
    u = z[:, :W]
    v = z[:, W:]
    mu = jnp.mean(v, axis=-1, keepdims=True)
    vc = v - mu
    var = jnp.mean(vc * vc, axis=-1, keepdims=True)
    rstd = lax.rsqrt(var + LN_EPS)
    vhat = vc * rstd
    vn = vhat * lng + lnb
    return cdf, u, vhat, rstd, vn


def _causal_mask():
    t = lax.broadcasted_iota(jnp.int32, (CHUNK, CHUNK), 0)
    s = lax.broadcasted_iota(jnp.int32, (CHUNK, CHUNK), 1)
    return t >= s


def _sgu_fwd(pre, lng, lnb, ws, bsT, *, name):
    T, W2 = pre.shape
    W = W2 // 2
    G = ws.shape[0]
    gd = W // G

    def body(pre_ref, lng_ref, lnb_ref, ws_ref, bs_ref, o_ref):
        _, u, _, _, vn = _sgu_common(pre_ref[...], lng_ref[...], lnb_ref[...], W)
        vnb = vn.astype(BF16)
        causal = _causal_mask()
        for g in range(G):
            w = jnp.where(causal, ws_ref[g], 0.0).astype(BF16)
            sv = jnp.dot(w, vnb[:, g * gd:(g + 1) * gd], preferred_element_type=F32) + bs_ref[:, g:g + 1]
            o_ref[:, g * gd:(g + 1) * gd] = (u[:, g * gd:(g + 1) * gd] * sv).astype(BF16)

    return pl.pallas_call(
        body, name=name, out_shape=jax.ShapeDtypeStruct((T, W), BF16), grid=(T // CHUNK,),
        in_specs=[pl.BlockSpec((CHUNK, W2), lambda i: (i, 0)),
                  pl.BlockSpec((1, W), lambda i: (0, 0)), pl.BlockSpec((1, W), lambda i: (0, 0)),
                  pl.BlockSpec(ws.shape, lambda i: (0, 0, 0)), pl.BlockSpec(bsT.shape, lambda i: (0, 0))],
        out_specs=pl.BlockSpec((CHUNK, W), lambda i: (i, 0)),
        compiler_params=pltpu.CompilerParams(dimension_semantics=("arbitrary",),
                                             vmem_limit_bytes=_vmem(12 * CHUNK * W2 * 4)),
    )(pre, lng, lnb, ws, bsT)


def _sgu_bwd(pre, dgated, lng, lnb, ws, bsT, *, name):
    T, W2 = pre.shape
    W = W2 // 2
    G = ws.shape[0]
    gd = W // G

    def body(pre_ref, dgt_ref, lng_ref, lnb_ref, ws_ref, bs_ref,
             dpre_ref, dws_ref, dbs_ref, dlng_ref, dlnb_ref, dbin_ref):
        first = pl.program_id(0) == 0

        @pl.when(first)
        def _():
            dws_ref[...] = jnp.zeros_like(dws_ref)
            dbs_ref[...] = jnp.zeros_like(dbs_ref)
            dlng_ref[...] = jnp.zeros_like(dlng_ref)
            dlnb_ref[...] = jnp.zeros_like(dlnb_ref)
            dbin_ref[...] = jnp.zeros_like(dbin_ref)

        pre_v = pre_ref[...]
        lng_v = lng_ref[...]
        cdf, u, vhat, rstd, vn = _sgu_common(pre_v, lng_v, lnb_ref[...], W)
        vnb = vn.astype(BF16)
        dgt = dgt_ref[...].astype(F32)
        causal = _causal_mask()
        du_parts, dvn_parts = [], []
        for g in range(G):
            sl = slice(g * gd, (g + 1) * gd)
            w = jnp.where(causal, ws_ref[g], 0.0).astype(BF16)
            sv = jnp.dot(w, vnb[:, sl], preferred_element_type=F32) + bs_ref[:, g:g + 1]
            dgt_g = dgt[:, sl]
            du_parts.append(dgt_g * sv)
            dsv = dgt_g * u[:, sl]
            dsvb = dsv.astype(BF16)
            dvn_parts.append(lax.dot_general(w, dsvb, (((0,), (0,)), ((), ())), preferred_element_type=F32))
            dw = lax.dot_general(dsvb, vnb[:, sl], (((1,), (1,)), ((), ())), preferred_element_type=F32)
            dws_ref[g] += jnp.where(causal, dw, 0.0)
            dbs_ref[:, g:g + 1] += jnp.sum(dsv, axis=1, keepdims=True)
        du = jnp.concatenate(du_parts, axis=1)
        dvn = jnp.concatenate(dvn_parts, axis=1)
        dlng_ref[...] += jnp.sum(dvn * vhat, axis=0, keepdims=True)
        dlnb_ref[...] += jnp.sum(dvn, axis=0, keepdims=True)
        dvh = dvn * lng_v
        dv = rstd * (dvh - jnp.mean(dvh, axis=-1, keepdims=True)
                     - vhat * jnp.mean(dvh * vhat, axis=-1, keepdims=True))
        dz = jnp.concatenate([du, dv], axis=1)
        dgelu = cdf + pre_v * jnp.exp(-0.5 * pre_v * pre_v) * _INV_SQRT_2PI
        dpre = dz * dgelu
        dbin_ref[...] += jnp.sum(dpre, axis=0, keepdims=True)
        dpre_ref[...] = dpre.astype(BF16)

    full = lambda shape: pl.BlockSpec(shape, lambda i, nd=len(shape): (0,) * nd)
    return pl.pallas_call(
        body, name=name,
        out_shape=[jax.ShapeDtypeStruct((T, W2), BF16), jax.ShapeDtypeStruct(ws.shape, F32),
                   jax.ShapeDtypeStruct(bsT.shape, F32), jax.ShapeDtypeStruct((1, W), F32),
                   jax.ShapeDtypeStruct((1, W), F32), jax.ShapeDtypeStruct((1, W2), F32)],
        grid=(T // CHUNK,),
        in_specs=[pl.BlockSpec((CHUNK, W2), lambda i: (i, 0)), pl.BlockSpec((CHUNK, W), lambda i: (i, 0)),
                  full((1, W)), full((1, W)), full(ws.shape), full(bsT.shape)],
        out_specs=[pl.BlockSpec((CHUNK, W2), lambda i: (i, 0)), full(ws.shape), full(bsT.shape),
                   full((1, W)), full((1, W)), full((1, W2))],
        compiler_params=pltpu.CompilerParams(dimension_semantics=("arbitrary",),
                                             vmem_limit_bytes=_vmem(24 * CHUNK * W2 * 4)),
    )(pre, dgated, lng, lnb, ws, bsT)


def _rope_tables(positions):
    half = ROPE_DIM // 2
    inv_freq = ROPE_THETA ** (-jnp.arange(0, ROPE_DIM, 2, dtype=F32) / ROPE_DIM)
    ang = positions.astype(F32).reshape(-1, 1) * inv_freq
    cos, sin = jnp.cos(ang), jnp.sin(ang)
    T = ang.shape[0]
    rest = HEAD_DIM - ROPE_DIM
    c64 = jnp.concatenate([cos, cos, jnp.ones((T, rest), F32)], axis=1)
    s64 = jnp.concatenate([-sin, sin, jnp.zeros((T, rest), F32)], axis=1)
    del half
    return jnp.tile(c64, (1, LANES // HEAD_DIM)), jnp.tile(s64, (1, LANES // HEAD_DIM))


def _swap8(x):
    W = x.shape[1]
    half = ROPE_DIM // 2
    lane = lax.broadcasted_iota(jnp.int32, x.shape, 1) % HEAD_DIM
    return jnp.where(lane < half, pltpu.roll(x, W - half, axis=1),
                     jnp.where(lane < ROPE_DIM, pltpu.roll(x, half, axis=1), 0.0))


def _wide(tab, W):
    return jnp.concatenate([tab] * (W // LANES), axis=1) if W > LANES else tab


def _rope_fwd(qkv, ctab, stab, *, q_width, kv_width, name):
    T = qkv.shape[0]
    tr = _pick(T, (256, 128))
    scale = HEAD_DIM ** -0.5

    def body(x_ref, c_ref, s_ref, q_ref, k_ref, v_ref):
        c = c_ref[...]
        s = s_ref[...]
        q = x_ref[:, :q_width]
        k = x_ref[:, q_width:q_width + kv_width]
        q_ref[...] = ((q * _wide(c, q_width) + _swap8(q) * _wide(s, q_width)) * scale).astype(BF16)
        k_ref[...] = (k * _wide(c, kv_width) + _swap8(k) * _wide(s, kv_width)).astype(BF16)
        v_ref[...] = x_ref[:, q_width + kv_width:].astype(BF16)

    return _row_call(body, [(qkv, "row"), (ctab, "row"), (stab, "row")],
                     [((T, q_width), BF16), ((T, kv_width), BF16), ((T, kv_width), BF16)],
                     name=name, rows=T, tr=tr, est=10 * tr * qkv.shape[1] * 4)


_NT = (((1,), (1,)), ((), ()))
_TN = (((0,), (0,)), ((), ()))


def _group_rows(ref, heads):
    return jnp.concatenate([ref[:, h * HEAD_DIM:(h + 1) * HEAD_DIM] for h in heads], axis=0)


def _attn_group_probs(q, kk, sinks, n, grp):
    rows = grp * CHUNK
    s = lax.dot_general(q, kk, _NT, preferred_element_type=F32)
    qi = lax.broadcasted_iota(jnp.int32, (rows, 2 * CHUNK), 0) & (CHUNK - 1)
    sj = lax.broadcasted_iota(jnp.int32, (rows, 2 * CHUNK), 1)
    valid = ((sj < CHUNK) & (sj > qi) & (n > 0)) | ((sj >= CHUNK) & (sj - CHUNK <= qi))
    s = jnp.where(valid, s, NEG_INF)
    r = lax.broadcasted_iota(jnp.int32, (rows, 1), 0)
    sink = jnp.full((rows, 1), sinks[grp - 1], F32)
    for g in range(grp - 2, -1, -1):
        sink = jnp.where(r < (g + 1) * CHUNK, sinks[g], sink)
    m = jnp.maximum(jnp.max(s, axis=1, keepdims=True), sink)
    p = jnp.exp(s - m)
    ps = jnp.exp(sink - m)
    inv = 1.0 / (jnp.sum(p, axis=1, keepdims=True) + ps)
    return p * inv, ps * inv


def _kv_specs(width, nb):
    prev = pl.BlockSpec((CHUNK, width), lambda n: (jnp.maximum(n - 1, 0), 0))
    cur = pl.BlockSpec((CHUNK, width), lambda n: (n, 0))
    return prev, cur


def _attn_fwd(qr, kr, vr, sinks, *, name):
    T, QW = qr.shape
    KW = kr.shape[1]
    HQ, HK = QW // HEAD_DIM, KW // HEAD_DIM
    grp = HQ // HK
    nb = T // CHUNK

    def body(q_ref, kp_ref, kc_ref, vp_ref, vc_ref, s_ref, o_ref):
        n = pl.program_id(0)
        for kh in range(HK):
            ks = slice(kh * HEAD_DIM, (kh + 1) * HEAD_DIM)
            heads = list(range(kh * grp, (kh + 1) * grp))
            q = _group_rows(q_ref, heads)
            kk = jnp.concatenate([kp_ref[:, ks], kc_ref[:, ks]], axis=0)
            vv = jnp.concatenate([vp_ref[:, ks], vc_ref[:, ks]], axis=0)
            p, _ = _attn_group_probs(q, kk, [s_ref[0, h] for h in heads], n, grp)
            o = jnp.dot(p.astype(BF16), vv, preferred_element_type=F32).astype(BF16)
            for g, h in enumerate(heads):
                o_ref[:, h * HEAD_DIM:(h + 1) * HEAD_DIM] = o[g * CHUNK:(g + 1) * CHUNK]

    kp, kc = _kv_specs(KW, nb)
    return pl.pallas_call(
        body, name=name, out_shape=jax.ShapeDtypeStruct((T, QW), BF16), grid=(nb,),
        in_specs=[pl.BlockSpec((CHUNK, QW), lambda n: (n, 0)), kp, kc, kp, kc,
                  pl.BlockSpec(memory_space=pltpu.SMEM)],
        out_specs=pl.BlockSpec((CHUNK, QW), lambda n: (n, 0)),
        compiler_params=pltpu.CompilerParams(dimension_semantics=("arbitrary",), vmem_limit_bytes=_vmem(8 << 20)),
    )(qr, kr, kr, vr, vr, sinks)


def _attn_bwd(qr, kr, vr, sinks, do, *, name):
    T, QW = qr.shape
    KW = kr.shape[1]
    HQ, HK = QW // HEAD_DIM, KW // HEAD_DIM
    grp = HQ // HK
    nb = T // CHUNK

    def body(q_ref, kp_ref, kc_ref, vp_ref, vc_ref, s_ref, do_ref,
             dq_ref, dkp_ref, dkc_ref, dvp_ref, dvc_ref, ds_ref):
        n = pl.program_id(0)
        lane = lax.broadcasted_iota(jnp.int32, (1, LANES), 1)
        dsink = jnp.zeros((1, LANES), F32)
        for kh in range(HK):
            ks = slice(kh * HEAD_DIM, (kh + 1) * HEAD_DIM)
            heads = list(range(kh * grp, (kh + 1) * grp))
            q = _group_rows(q_ref, heads)
            doh = _group_rows(do_ref, heads)
            kk = jnp.concatenate([kp_ref[:, ks], kc_ref[:, ks]], axis=0)
            vv = jnp.concatenate([vp_ref[:, ks], vc_ref[:, ks]], axis=0)
            p, ps = _attn_group_probs(q, kk, [s_ref[0, h] for h in heads], n, grp)
            dp = lax.dot_general(doh, vv, _NT, preferred_element_type=F32)
            delta = jnp.sum(p * dp, axis=1, keepdims=True)
            ds = (p * (dp - delta)).astype(BF16)
            dv = lax.dot_general(p.astype(BF16), doh, _TN, preferred_element_type=F32)
            dk = lax.dot_general(ds, q, _TN, preferred_element_type=F32)
            dq = jnp.dot(ds, kk, preferred_element_type=F32)
            psd = ps * delta
            for g, h in enumerate(heads):
                dq_ref[:, h * HEAD_DIM:(h + 1) * HEAD_DIM] = dq[g * CHUNK:(g + 1) * CHUNK]
                dsink = dsink + jnp.where(
                    lane == h, -jnp.sum(psd[g * CHUNK:(g + 1) * CHUNK], axis=0, keepdims=True), 0.0)
            dkp_ref[:, ks] = dk[:CHUNK]
            dkc_ref[:, ks] = dk[CHUNK:]
            dvp_ref[:, ks] = dv[:CHUNK]
            dvc_ref[:, ks] = dv[CHUNK:]

        @pl.when(n == 0)
        def _():
            ds_ref[...] = dsink

        @pl.when(n > 0)
        def _():
            ds_ref[...] += dsink

    kp, kc = _kv_specs(KW, nb)
    qspec = pl.BlockSpec((CHUNK, QW), lambda n: (n, 0))
    kout = pl.BlockSpec((CHUNK, KW), lambda n: (n, 0))
    return pl.pallas_call(
        body, name=name,
        out_shape=[jax.ShapeDtypeStruct((T, QW), F32)] + [jax.ShapeDtypeStruct((T, KW), F32)] * 4
        + [jax.ShapeDtypeStruct((1, LANES), F32)],
        grid=(nb,),
        in_specs=[qspec, kp, kc, kp, kc, pl.BlockSpec(memory_space=pltpu.SMEM), qspec],
        out_specs=[qspec, kout, kout, kout, kout, pl.BlockSpec((1, LANES), lambda n: (0, 0))],
        compiler_params=pltpu.CompilerParams(dimension_semantics=("arbitrary",), vmem_limit_bytes=_vmem(12 << 20)),
    )(qr, kr, kr, vr, vr, sinks, do)


def _rope_bwd(dq, dkp, dkc, dvp, dvc, ctab, stab, *, name):
    T, QW = dq.shape
    KW = dkp.shape[1]
    nb = T // CHUNK
    scale = HEAD_DIM ** -0.5
    width = QW + 2 * KW

    def body(dq_ref, dkc_ref, dkn_ref, dvc_ref, dvn_ref, c_ref, s_ref, o_ref, db_ref):
        n = pl.program_id(0)
        c = c_ref[...]
        s = s_ref[...]
        has_next = (n < nb - 1).astype(F32)
        dqv = dq_ref[...]
        dk = dkc_ref[...] + has_next * dkn_ref[...]
        dv = dvc_ref[...] + has_next * dvn_ref[...]
        dq_pre = (dqv * _wide(c, QW) + _swap8(dqv * _wide(s, QW))) * scale
        dk_pre = dk * _wide(c, KW) + _swap8(dk * _wide(s, KW))
        o_ref[:, :QW] = dq_pre.astype(BF16)
        o_ref[:, QW:QW + KW] = dk_pre.astype(BF16)
        o_ref[:, QW + KW:] = dv.astype(BF16)
        part = jnp.concatenate([jnp.sum(dq_pre, axis=0, keepdims=True), jnp.sum(dk_pre, axis=0, keepdims=True),
                                jnp.sum(dv, axis=0, keepdims=True)], axis=1)

        @pl.when(n == 0)
        def _():
            db_ref[...] = part

        @pl.when(n > 0)
        def _():
            db_ref[...] += part

    cur = lambda w: pl.BlockSpec((CHUNK, w), lambda n: (n, 0))
    nxt = lambda w: pl.BlockSpec((CHUNK, w), lambda n: (jnp.minimum(n + 1, nb - 1), 0))
    return pl.pallas_call(
        body, name=name,
        out_shape=[jax.ShapeDtypeStruct((T, width), BF16), jax.ShapeDtypeStruct((1, width), F32)],
        grid=(nb,),
        in_specs=[cur(QW), cur(KW), nxt(KW), cur(KW), nxt(KW), cur(LANES), cur(LANES)],
        out_specs=[cur(width), pl.BlockSpec((1, width), lambda n: (0, 0))],
        compiler_params=pltpu.CompilerParams(dimension_semantics=("arbitrary",), vmem_limit_bytes=_vmem(8 << 20)),
    )(dq, dkc, dkp, dvc, dvp, ctab, stab)


def _cast_block(w, l, axis, chip_arr, *, name):
    _, Ks, Ns = w.shape
    tk = _pick(Ks, (512, 352, 256, 128))
    nk = Ks // tk
    full = (Ks * N_CHIPS, Ns) if axis == 0 else (Ks, Ns * N_CHIPS)

    def body(p_ref, w_ref, o_ref):
        o_ref[...] = w_ref[...].astype(BF16)

    if axis == 0:
        out_spec = pl.BlockSpec((tk, Ns), lambda i, p: (p[0] * nk + i, 0))
    else:
        out_spec = pl.BlockSpec((tk, Ns), lambda i, p: (i, p[0]))
    grid_spec = pltpu.PrefetchScalarGridSpec(
        num_scalar_prefetch=1, grid=(nk,),
        in_specs=[pl.BlockSpec((None, tk, Ns), lambda i, p: (l, i, 0))], out_specs=out_spec)
    return pl.pallas_call(
        body, name=name, out_shape=jax.ShapeDtypeStruct(full, BF16), grid_spec=grid_spec,
        compiler_params=pltpu.CompilerParams(dimension_semantics=("arbitrary",),
                                             vmem_limit_bytes=_vmem(4 * tk * Ns * 6)),
    )(chip_arr, w)


def _adamw_math(w, g, m, v):
    m = ADAM_B1 * m + (1.0 - ADAM_B1) * g
    v = ADAM_B2 * v + (1.0 - ADAM_B2) * (g * g)
    m_hat = m / (1.0 - ADAM_B1 ** ADAM_STEP)
    v_hat = v / (1.0 - ADAM_B2 ** ADAM_STEP)
    delta = -ADAM_LR * (m_hat / (jnp.sqrt(v_hat) + ADAM_EPS) + ADAM_WD * w)
    return delta, m, v


def _adamw_layer(w, m, v, g, l, outs, *, name):
    _, K, N = w.shape
    tk = _pick(K, (256, 176, 128))

    def body(w_ref, m_ref, v_ref, g_ref, _g, _d, _m, _v, go_ref, d_ref, mo_ref, vo_ref):
        gv = g_ref[...]
        d, mn, vn = _adamw_math(w_ref[...], gv, m_ref[...], v_ref[...])
        go_ref[...] = gv
        d_ref[...] = d
        mo_ref[...] = mn
        vo_ref[...] = vn

    layer = pl.BlockSpec((None, tk, N), lambda i: (l, i, 0))
    any_spec = pl.BlockSpec(memory_space=pl.ANY)
    sd = jax.ShapeDtypeStruct(w.shape, F32)
    return pl.pallas_call(
        body, name=name, out_shape=[sd, sd, sd, sd], grid=(K // tk,),
        in_specs=[layer, layer, layer, pl.BlockSpec((tk, N), lambda i: (i, 0))] + [any_spec] * 4,
        out_specs=[layer] * 4, input_output_aliases={4: 0, 5: 1, 6: 2, 7: 3},
        compiler_params=pltpu.CompilerParams(dimension_semantics=("arbitrary",),
                                             vmem_limit_bytes=_vmem(2 * 8 * tk * N * 4 + 6 * tk * N * 4)),
    )(w, m, v, g, *outs)


def _adamw_small(w, g, m, v, *, name):
    def body(w_ref, g_ref, m_ref, v_ref, d_ref, mo_ref, vo_ref):
        d, mn, vn = _adamw_math(w_ref[...], g_ref[...], m_ref[...], v_ref[...])
        d_ref[...] = d
        mo_ref[...] = mn
        vo_ref[...] = vn

    sd = jax.ShapeDtypeStruct(w.shape, F32)
    return pl.pallas_call(body, name=name, out_shape=[sd, sd, sd])(w, g, m, v)


def _my_place():
    return lax.axis_index("x"), lax.axis_index("y"), lax.axis_index("c")


def _peer_chips(x, y):
    return [(1 - x, y), (x, 1 - y), (1 - x, 1 - y)]


_HBM = pl.BlockSpec(memory_space=pltpu.HBM)
_SEM = pl.BlockSpec(memory_space=pltpu.SEMAPHORE)
_EFFECT = pltpu.SideEffectType.DATAFLOW_SIDE_EFFECTING


def _split_start(name, bufs, n_copies, make_copies):
    nb = len(bufs)

    def body(*refs):
        send_sems, recv_sems = refs[nb], refs[nb + 1]
        token = refs[2 * nb + 2]
        sends, _ = make_copies(refs[:nb], send_sems, recv_sems)
        for cp in sends:
            cp.start()
        token[...] = jnp.zeros_like(token)

    res = pl.pallas_call(
        body, name=name,
        out_shape=(pltpu.SemaphoreType.DMA((n_copies,)), pltpu.SemaphoreType.DMA((n_copies,)),
                   *[pltpu.HBM(b.shape, b.dtype) for b in bufs], jax.ShapeDtypeStruct((8, LANES), F32)),
        in_specs=[_HBM] * nb,
        out_specs=(_SEM, _SEM, *[_HBM] * nb, pl.BlockSpec(memory_space=pltpu.VMEM)),
        input_output_aliases={k: 2 + k for k in range(nb)},
        compiler_params=pltpu.CompilerParams(has_side_effects=_EFFECT),
    )(*[pltpu.with_memory_space_constraint(b, pltpu.HBM) for b in bufs])
    return res[0], res[1], list(res[2:2 + nb]), res[2 + nb]


def _split_wait(name, bufs, sems, make_copies, after):
    nb = len(bufs)

    def body(*refs):
        send_sems, recv_sems = refs[nb], refs[nb + 1]
        sends, recvs = make_copies(refs[:nb], send_sems, recv_sems)
        for cp in sends:
            cp.wait_send()
        for cp in recvs:
            cp.wait_recv()

    res = pl.pallas_call(
        body, name=name,
        out_shape=tuple(pltpu.HBM(b.shape, b.dtype) for b in bufs),
        in_specs=[_HBM] * nb + [_SEM, _SEM, pl.BlockSpec(memory_space=pl.ANY)],
        out_specs=tuple([_HBM] * nb),
        input_output_aliases={k: k for k in range(nb)},
        compiler_params=pltpu.CompilerParams(has_side_effects=_EFFECT),
    )(*bufs, sems[0], sems[1], after)
    return list(res)


def _remote(src, dst, send_sems, recv_sems, k, target):
    return pltpu.make_async_remote_copy(src_ref=src, dst_ref=dst, send_sem=send_sems.at[k],
                                        recv_sem=recv_sems.at[k], device_id=target, device_id_type=MESH)


def _ag_region(ref, axis, chip, half):
    K, N = ref.shape
    if axis == 0:
        hs = K // N_CHIPS // 2
        assert hs % 16 == 0
        return ref.at[pl.ds(pl.multiple_of((2 * chip + half) * hs, 16), hs), :]
    ns, hk = N // N_CHIPS, K // 2
    assert ns % LANES == 0 and hk % 16 == 0
    return ref.at[pl.ds(pl.multiple_of(half * hk, 16), hk), pl.ds(pl.multiple_of(chip * ns, LANES), ns)]


def _ag_copies(stage, axes):
    n = len(axes)

    def make(bufs, send_sems, recv_sems):
        x, y, c = _my_place()
        me = 2 * x + y
        sends, recvs = [], []
        for j, (px, py) in enumerate(_peer_chips(x, y)):
            other = 2 * px + py
            for w in range(n):
                k = j * n + w
                if stage == 1:
                    src, target = _ag_region(bufs[w], axes[w], me, c), (px, py, c)
                    land = _ag_region(bufs[w], axes[w], other, c)
                else:
                    src, target = _ag_region(bufs[w], axes[w], other, c), (x, y, 1 - c)
                    land = _ag_region(bufs[w], axes[w], other, 1 - c)
                sends.append(_remote(src, src, send_sems, recv_sems, k, target))
                recvs.append(_remote(land, land, send_sems, recv_sems, k, target))
        return sends, recvs

    return make


def _half_shape(shape, axis):
    K, N = shape
    return (K, N // 2) if axis == 0 else (K // 2, N)


def _core_half(ref, axis, half):
    K, N = ref.shape
    if axis == 0:
        return ref.at[:, pl.ds(pl.multiple_of(half * (N // 2), LANES), N // 2)]
    return ref.at[pl.ds(pl.multiple_of(half * (K // 2), 16), K // 2), :]


def _chip_block(ref, axis, chip):
    K, N = ref.shape
    if axis == 0:
        return ref.at[pl.ds(pl.multiple_of(chip * (K // N_CHIPS), 16), K // N_CHIPS), :]
    return ref.at[:, pl.ds(pl.multiple_of(chip * (N // N_CHIPS), LANES), N // N_CHIPS)]


def _rs_sibling_copies(axes):
    n = len(axes)

    def make(bufs, send_sems, recv_sems):
        x, y, c = _my_place()
        sends = [_remote(_core_half(bufs[w], axes[w], 1 - c), bufs[n + w], send_sems, recv_sems, w, (x, y, 1 - c))
                 for w in range(n)]
        recvs = [_remote(bufs[n + w], bufs[n + w], send_sems, recv_sems, w, (x, y, 1 - c)) for w in range(n)]
        return sends, recvs

    return make


def _rs_chip_copies(axes):
    n = len(axes)

    def make(bufs, send_sems, recv_sems):
        x, y, c = _my_place()
        sends, recvs = [], []
        for j, (px, py) in enumerate(_peer_chips(x, y)):
            for w in range(n):
                k = j * n + w
                sends.append(_remote(_chip_block(bufs[w], axes[w], 2 * px + py), bufs[n + w].at[j],
                                     send_sems, recv_sems, k, (px, py, c)))
                recvs.append(_remote(bufs[n + w].at[j], bufs[n + w].at[j], send_sems, recv_sems, k, (px, py, c)))
        return sends, recvs

    return make


def _rs_fill_copies(axes):
    n = len(axes)

    def make(bufs, send_sems, recv_sems):
        x, y, c = _my_place()
        sends = [_remote(_core_half(bufs[w], axes[w], c), _core_half(bufs[w], axes[w], c),
                         send_sems, recv_sems, w, (x, y, 1 - c)) for w in range(n)]
        recvs = [_remote(_core_half(bufs[w], axes[w], 1 - c), _core_half(bufs[w], axes[w], 1 - c),
                         send_sems, recv_sems, w, (x, y, 1 - c)) for w in range(n)]
        return sends, recvs

    return make


def _chip_sum(g, r, axis, place, *, name):
    hk, hn = r.shape
    tk = 128 if hn > 4096 else _pick(hk, (256, 128))
    nk = hk // tk

    def body(p_ref, g_ref, r_ref, f_ref, b_ref):
        s = g_ref[...] + r_ref[...]
        f_ref[...] = s
        b_ref[...] = s.astype(BF16)

    half = pl.BlockSpec((tk, hn), lambda i, p: (i, 0))
    if axis == 0:
        g_spec = pl.BlockSpec((tk, hn), lambda i, p: (i, p[1]))
    else:
        g_spec = pl.BlockSpec((tk, hn), lambda i, p: (p[1] * nk + i, 0))
    grid_spec = pltpu.PrefetchScalarGridSpec(num_scalar_prefetch=1, grid=(nk,), in_specs=[g_spec, half],
                                             out_specs=[half, half])
    return pl.pallas_call(
        body, name=name,
        out_shape=[jax.ShapeDtypeStruct(r.shape, F32), jax.ShapeDtypeStruct(r.shape, BF16)],
        grid_spec=grid_spec,
        compiler_params=pltpu.CompilerParams(dimension_semantics=("arbitrary",),
                                             vmem_limit_bytes=_vmem(2 * tk * hn * 14)),
    )(place, g, r)


def _final_sum(own, recv, axis, place, *, name):
    _, bk, bn = recv.shape
    tk = _pick(bk, (256, 176, 128))
    nk = bk // tk

    def body(p_ref, o_ref, r_ref, out_ref):
        out_ref[...] = ((o_ref[...] + r_ref[0].astype(F32)) + r_ref[1].astype(F32)) + r_ref[2].astype(F32)

    if axis == 0:
        own_spec = pl.BlockSpec((tk, bn), lambda i, p: (p[0] * nk + i, 0))
        out_shape, out_spec = (bk, 2 * bn), pl.BlockSpec((tk, bn), lambda i, p: (i, p[1]))
    else:
        own_spec = pl.BlockSpec((tk, bn), lambda i, p: (i, p[0]))
        out_shape, out_spec = (2 * bk, bn), pl.BlockSpec((tk, bn), lambda i, p: (p[1] * nk + i, 0))
    grid_spec = pltpu.PrefetchScalarGridSpec(
        num_scalar_prefetch=1, grid=(nk,),
        in_specs=[own_spec, pl.BlockSpec((3, tk, bn), lambda i, p: (0, i, 0))], out_specs=out_spec)
    return pl.pallas_call(
        body, name=name, out_shape=jax.ShapeDtypeStruct(out_shape, F32), grid_spec=grid_spec,
        compiler_params=pltpu.CompilerParams(dimension_semantics=("arbitrary",),
                                             vmem_limit_bytes=_vmem(2 * tk * bn * 14 + 4 * tk * bn * 4)),
    )(place, own, recv)


def _allreduce_small(p):
    def body(p_ref, o_ref, r0, r1, r2, send_sems, recv_sems):
        x, y, c = _my_place()
        o_ref[...] = p_ref[...]
        for s, (peer, rbuf) in enumerate([((x, y, 1 - c), r0), ((1 - x, y, c), r1), ((x, 1 - y, c), r2)]):
            cp = pltpu.make_async_remote_copy(src_ref=o_ref, dst_ref=rbuf, send_sem=send_sems.at[s],
                                              recv_sem=recv_sems.at[s], device_id=peer, device_id_type=MESH)
            cp.start()
            cp.wait()
            o_ref[...] = o_ref[...] + rbuf[...]

    vm = pl.BlockSpec(memory_space=pltpu.VMEM)
    return pl.pallas_call(
        body, name="allreduce_small", out_shape=jax.ShapeDtypeStruct(p.shape, F32),
        in_specs=[vm], out_specs=vm,
        scratch_shapes=[pltpu.VMEM(p.shape, F32)] * 3 + [pltpu.SemaphoreType.DMA((3,))] * 2,
        compiler_params=pltpu.CompilerParams(vmem_limit_bytes=_vmem(6 * _nbytes(p.shape, F32))),
    )(p)


def _pack_rows(parts):
    rows, metas = [], []
    for a in parts:
        flat = a.reshape(-1)
        nrow = -(-flat.shape[0] // LANES)
        nrow = -(-nrow // 8) * 8
        flat = jnp.pad(flat, (0, nrow * LANES - flat.shape[0]))
        rows.append(flat.reshape(nrow, LANES))
        metas.append((a.shape, nrow))
    return jnp.concatenate(rows, axis=0), metas


def _unpack_rows(packed, metas):
    out, r0 = [], 0
    for shape, nrow in metas:
        size = int(np.prod(shape))
        out.append(packed[r0:r0 + nrow].reshape(-1)[:size].reshape(shape))
        r0 += nrow
    return out


def kernel(x, positions, pre_mix_g, post_mix_g, pre_ffn_g, post_ffn_g, a_w_in, a_b_in, a_ln_g, a_ln_b, a_w_s, a_b_s, a_w_out, b_w_qkv, b_b_qkv, b_sinks, b_w_o, ffn_w_gu, ffn_w_down, loss_target, m_pre_mix_g, m_post_mix_g, m_pre_ffn_g, m_post_ffn_g, m_a_w_in, m_a_b_in, m_a_ln_g, m_a_ln_b, m_a_w_s, m_a_b_s, m_a_w_out, m_b_w_qkv, m_b_b_qkv, m_b_sinks, m_b_w_o, m_ffn_w_gu, m_ffn_w_down, v_pre_mix_g, v_post_mix_g, v_pre_ffn_g, v_post_ffn_g, v_a_w_in, v_a_b_in, v_a_ln_g, v_a_ln_b, v_a_w_s, v_a_b_s, v_a_w_out, v_b_w_qkv, v_b_b_qkv, v_b_sinks, v_b_w_o, v_ffn_w_gu, v_ffn_w_down):
    depth, D = pre_mix_g.shape
    xi, yi, ci = _my_place()
    chip = 2 * xi + yi
    place = jnp.stack([chip, ci]).astype(jnp.int32)

    stacked = {"a_w_in": (a_w_in, m_a_w_in, v_a_w_in), "a_w_out": (a_w_out, m_a_w_out, v_a_w_out),
               "b_w_qkv": (b_w_qkv, m_b_w_qkv, v_b_w_qkv), "b_w_o": (b_w_o, m_b_w_o, v_b_w_o),
               "ffn_w_gu": (ffn_w_gu, m_ffn_w_gu, v_ffn_w_gu), "ffn_w_down": (ffn_w_down, m_ffn_w_down, v_ffn_w_down)}
    cut = {"a_w_in": 1, "a_w_out": 0, "b_w_qkv": 1, "b_w_o": 0, "ffn_w_gu": 1, "ffn_w_down": 0}

    def layer_keys(i):
        mix = [("a_w_in", i // 2), ("a_w_out", i // 2)] if i % 2 == 0 else [("b_w_qkv", i // 2), ("b_w_o", i // 2)]
        return mix + [("ffn_w_gu", i), ("ffn_w_down", i)]

    def dep(a, toks):
        for t in toks:
            a = a + t[:1, :1]
        return a

    W = {}
    for i in range(depth):
        for nm, l in layer_keys(i):
            W[(nm, l)] = _cast_block(stacked[nm][0], l, cut[nm], place, name=f"cast_{nm}_{l}")

    def ag_start(i, stage):
        keys = layer_keys(i)
        ss, rs, bufs, tok = _split_start(f"ag{stage}_start_{i}", [W[k] for k in keys], 3 * len(keys),
                                         _ag_copies(stage, [cut[nm] for nm, _ in keys]))
        W.update(zip(keys, bufs))
        return (ss, rs), tok

    def ag_wait(i, stage, sems, after):
        keys = layer_keys(i)
        bufs = _split_wait(f"ag{stage}_wait_{i}", [W[k] for k in keys], sems,
                           _ag_copies(stage, [cut[nm] for nm, _ in keys]), after)
        W.update(zip(keys, bufs))

    for stage in (1, 2):
        sems, tok = ag_start(0, stage)
        ag_wait(0, stage, sems, tok)

    h = x[0]
    target = loss_target[0]
    ctab, stab = _rope_tables(positions[0])
    q_width = W[("b_w_o", 0)].shape[0]
    kv_width = N_KV_HEADS * HEAD_DIM
    row = lambda a, i: a[i:i + 1]

    nq = b_b_qkv.shape[1]
    bq_full = jnp.zeros((b_b_qkv.shape[0], N_CHIPS * nq), F32)
    bq_full = lax.dynamic_update_slice(bq_full, jnp.where(ci == 0, b_b_qkv, 0.0), (0, chip * nq))
    bq_packed, bq_meta = _pack_rows([bq_full])
    b_qkv_full = _unpack_rows(_allreduce_small(bq_packed), bq_meta)[0]

    saved = []
    for i in range(depth):
        j = i // 2
        s = {"h": h}
        toks = []
        if i + 1 < depth:
            sems, tok = ag_start(i + 1, 1)
            toks = [tok]
        hn = _rms_fwd(h, dep(row(pre_mix_g, i), toks), out_dtype=BF16, name=f"rms_pre_mix_{i}")
        s["hn"] = hn
        if i % 2 == 0:
            pre = _matmul(hn, W[("a_w_in", j)], mode="nn", bias=row(a_b_in, j), out_dtype=F32, name=f"gmlp_in_{i}")
            gated = _sgu_fwd(pre, row(a_ln_g, j), row(a_ln_b, j), a_w_s[j], a_b_s[j].T, name=f"sgu_fwd_{i}")
            mix = _matmul(gated, W[("a_w_out", j)], mode="nn", out_dtype=F32, name=f"gmlp_out_{i}")
            s.update(pre=pre, gated=gated)
        else:
            qkv = _matmul(hn, W[("b_w_qkv", j)], mode="nn", bias=row(b_qkv_full, j), out_dtype=F32,
                          name=f"attn_qkv_{i}")
            qr, kr, vr = _rope_fwd(qkv, ctab, stab, q_width=q_width, kv_width=kv_width, name=f"rope_fwd_{i}")
            o = _attn_fwd(qr, kr, vr, row(b_sinks, j), name=f"attn_fwd_{i}")
            mix = _matmul(o, W[("b_w_o", j)], mode="nn", out_dtype=F32, name=f"attn_o_{i}")
            s.update(qr=qr, kr=kr, vr=vr, o=o)
        s["mix"] = mix
        h1 = _rms_res(h, mix, row(post_mix_g, i), name=f"rms_post_mix_{i}")
        s["h1"] = h1
        fn = _rms_fwd(h1, row(pre_ffn_g, i), out_dtype=BF16, name=f"rms_pre_ffn_{i}")
        g_pre, u_pre, act = _ffn_up(fn, W[("ffn_w_gu", i)][None], 0, name=f"ffn_up_{i}")
        f = _matmul(act, W[("ffn_w_down", i)], mode="nn", out_dtype=F32, name=f"ffn_down_{i}")
        toks = []
        if i + 1 < depth:
            ag_wait(i + 1, 1, sems, f)
            sems, tok = ag_start(i + 1, 2)
            toks = [tok]
        h = _rms_res(h1, f, dep(row(post_ffn_g, i), toks), name=f"rms_post_ffn_{i}")
        if i + 1 < depth:
            ag_wait(i + 1, 2, sems, h)
        s.update(fn=fn, g_pre=g_pre, u_pre=u_pre, act=act, f=f)
        saved.append(s)

    dh, loss_part = _loss_and_grad(h, target, name="loss")
    loss = lax.psum(loss_part[0, 0], ("x", "y", "c"))

    big_out = {nm: tuple(lax.empty(w.shape, F32) for _ in range(4)) for nm, (w, _, _) in stacked.items()}

    def reduce_layer(i, grads):
        keys = layer_keys(i)
        axes = [cut[nm] for nm, _ in keys]
        n = len(keys)
        lands = [lax.empty(_half_shape(g.shape, ax), F32) for g, ax in zip(grads, axes)]
        ss, rs, bufs, tok = _split_start(f"rs_sibling_start_{i}", list(grads) + lands, n, _rs_sibling_copies(axes))
        after = yield tok
        bufs = _split_wait(f"rs_sibling_wait_{i}", bufs, (ss, rs), _rs_sibling_copies(axes), after)
        sums = [_chip_sum(bufs[w], bufs[n + w], axes[w], place, name=f"chip_sum_{keys[w][0]}_{keys[w][1]}")
                for w in range(n)]
        lands = []
        for (sf, _), ax in zip(sums, axes):
            hk, hn = sf.shape
            lands.append(lax.empty((3, hk // N_CHIPS, hn) if ax == 0 else (3, hk, hn // N_CHIPS), BF16))
        ss, rs, bufs, tok = _split_start(f"rs_chip_start_{i}", [sb for _, sb in sums] + lands, 3 * n,
                                         _rs_chip_copies(axes))
        after = yield tok
        bufs = _split_wait(f"rs_chip_wait_{i}", bufs, (ss, rs), _rs_chip_copies(axes), after)
        blocks = [_final_sum(sums[w][0], bufs[n + w], axes[w], place, name=f"final_sum_{keys[w][0]}_{keys[w][1]}")
                  for w in range(n)]
        ss, rs, bufs, tok = _split_start(f"rs_fill_start_{i}", blocks, n, _rs_fill_copies(axes))
        after = yield tok
        blocks = _split_wait(f"rs_fill_wait_{i}", bufs, (ss, rs), _rs_fill_copies(axes), after)
        for (nm, l), g in zip(keys, blocks):
            w, m, v = stacked[nm]
            big_out[nm] = tuple(_adamw_layer(w, m, v, g, l, big_out[nm], name=f"adamw_{nm}_{l}"))
        yield None

    reducing = []

    def advance(after):
        toks = []
        for gen in list(reducing):
            tok = gen.send(after)
            if tok is None:
                reducing.remove(gen)
            else:
                toks.append(tok)
        return toks

    small = {}
    g_pre_mix, g_post_mix, g_pre_ffn, g_post_ffn = [None] * depth, [None] * depth, [None] * depth, [None] * depth
    toks = []
    for i in reversed(range(depth)):
        j = i // 2
        s = saved[i]
        df, g_post_ffn[i] = _rms_bwd(s["f"], dep(row(post_ffn_g, i), toks), dh, None, out_dtype=BF16,
                                     name=f"rms_post_ffn_bwd_{i}")
        g_down = _matmul(s["act"], df, mode="tn", out_dtype=F32, name=f"ffn_down_dw_{i}")
        dg_, du_ = _ffn_down_dx(df, W[("ffn_w_down", i)][None], 0, s["g_pre"], s["u_pre"], name=f"ffn_down_dx_{i}")
        hid = dg_.shape[1]
        tile = _pick(hid, (1408, 768, 512, 256, 128))
        w_gu = W[("ffn_w_gu", i)]
        g_gu = lax.empty(w_gu.shape, F32)
        g_gu = _matmul(s["fn"], dg_, mode="tn", into=g_gu, tq=tile, out_dtype=F32, name=f"ffn_g_dw_{i}")
        g_gu = _matmul(s["fn"], du_, mode="tn", into=g_gu, tq=tile, q_off=hid // tile, out_dtype=F32,
                       name=f"ffn_u_dw_{i}")
        dfn_g = _matmul(dg_, w_gu, mode="nt", tr=tile, out_dtype=F32, name=f"ffn_g_dx_{i}")
        dfn = _matmul(du_, w_gu, mode="nt", tr=tile, b_r_off=hid // tile, bias=dfn_g, out_dtype=F32,
                      name=f"ffn_u_dx_{i}")
        toks = advance(dfn)
        dh1, g_pre_ffn[i] = _rms_bwd(s["h1"], dep(row(pre_ffn_g, i), toks), dfn, dh, out_dtype=F32,
                                     name=f"rms_pre_ffn_bwd_{i}")
        dmix, g_post_mix[i] = _rms_bwd(s["mix"], row(post_mix_g, i), dh1, None, out_dtype=BF16,
                                       name=f"rms_post_mix_bwd_{i}")
        if i % 2 == 0:
            g_out = _matmul(s["gated"], dmix, mode="tn", out_dtype=F32, name=f"gmlp_out_dw_{i}")
            dgated = _matmul(dmix, W[("a_w_out", j)], mode="nt", out_dtype=F32, name=f"gmlp_out_dx_{i}")
            dpre, dws, dbsT, dlng, dlnb, dbin = _sgu_bwd(s["pre"], dgated, row(a_ln_g, j), row(a_ln_b, j),
                                                         a_w_s[j], a_b_s[j].T, name=f"sgu_bwd_{i}")
            small[("a_w_s", j)] = dws
            small[("a_b_s", j)] = dbsT.T
            small[("a_ln_g", j)] = dlng
            small[("a_ln_b", j)] = dlnb
            small[("a_b_in", j)] = dbin
            g_in = _matmul(s["hn"], dpre, mode="tn", out_dtype=F32, name=f"gmlp_in_dw_{i}")
            dhn = _matmul(dpre, W[("a_w_in", j)], mode="nt", out_dtype=F32, name=f"gmlp_in_dx_{i}")
        else:
            g_out = _matmul(s["o"], dmix, mode="tn", out_dtype=F32, name=f"attn_o_dw_{i}")
            do = _matmul(dmix, W[("b_w_o", j)], mode="nt", out_dtype=BF16, name=f"attn_o_dx_{i}")
            dq, dkp, dkc, dvp, dvc, dsk = _attn_bwd(s["qr"], s["kr"], s["vr"], row(b_sinks, j), do,
                                                    name=f"attn_bwd_{i}")
            dqkv, dbq = _rope_bwd(dq, dkp, dkc, dvp, dvc, ctab, stab, name=f"rope_bwd_{i}")
            small[("b_sinks", j)] = dsk[:, :b_sinks.shape[1]]
            small[("b_b_qkv", j)] = dbq
            g_in = _matmul(s["hn"], dqkv, mode="tn", out_dtype=F32, name=f"attn_qkv_dw_{i}")
            dhn = _matmul(dqkv, W[("b_w_qkv", j)], mode="nt", out_dtype=F32, name=f"attn_qkv_dx_{i}")
        toks = advance(dhn)
        dh, g_pre_mix[i] = _rms_bwd(s["h"], dep(row(pre_mix_g, i), toks), dhn, dh1, out_dtype=F32,
                                    name=f"rms_pre_mix_bwd_{i}")
        gen = reduce_layer(i, [g_in, g_out, g_gu, g_down])
        toks = [next(gen)] + advance(dh)
        reducing.append(gen)
    grad_x = dh[None]
    while reducing:
        advance(dh)

    n_a, n_b = a_b_in.shape[0], b_sinks.shape[0]
    stack = lambda key, n: jnp.concatenate([small[(key, j)] for j in range(n)], axis=0)
    small_parts = [
        jnp.concatenate(g_pre_mix, axis=0), jnp.concatenate(g_post_mix, axis=0),
        jnp.concatenate(g_pre_ffn, axis=0), jnp.concatenate(g_post_ffn, axis=0),
        stack("a_b_in", n_a), stack("a_ln_g", n_a), stack("a_ln_b", n_a),
        jnp.stack([small[("a_w_s", j)] for j in range(n_a)]), jnp.stack([small[("a_b_s", j)] for j in range(n_a)]),
        stack("b_b_qkv", n_b), stack("b_sinks", n_b),
    ]
    packed, metas = _pack_rows(small_parts)
    red = _unpack_rows(_allreduce_small(packed), metas)
    (gr_pre_mix, gr_post_mix, gr_pre_ffn, gr_post_ffn, gr_b_in, gr_ln_g, gr_ln_b, gr_w_s, gr_b_s,
     gr_b_qkv_full, gr_sinks) = red
    gr_b_qkv = lax.dynamic_slice(gr_b_qkv_full, (0, chip * nq), (gr_b_qkv_full.shape[0], nq))

    grads = {"pre_mix_g": gr_pre_mix, "post_mix_g": gr_post_mix, "pre_ffn_g": gr_pre_ffn, "post_ffn_g": gr_post_ffn,
             "a_b_in": gr_b_in, "a_ln_g": gr_ln_g, "a_ln_b": gr_ln_b, "a_w_s": gr_w_s, "a_b_s": gr_b_s,
             "b_b_qkv": gr_b_qkv, "b_sinks": gr_sinks}
    weights = {"pre_mix_g": (pre_mix_g, m_pre_mix_g, v_pre_mix_g), "post_mix_g": (post_mix_g, m_post_mix_g, v_post_mix_g),
               "pre_ffn_g": (pre_ffn_g, m_pre_ffn_g, v_pre_ffn_g), "post_ffn_g": (post_ffn_g, m_post_ffn_g, v_post_ffn_g),
               "a_b_in": (a_b_in, m_a_b_in, v_a_b_in), "a_ln_g": (a_ln_g, m_a_ln_g, v_a_ln_g),
               "a_ln_b": (a_ln_b, m_a_ln_b, v_a_ln_b), "a_w_s": (a_w_s, m_a_w_s, v_a_w_s), "a_b_s": (a_b_s, m_a_b_s, v_a_b_s),
               "b_b_qkv": (b_b_qkv, m_b_b_qkv, v_b_b_qkv), "b_sinks": (b_sinks, m_b_sinks, v_b_sinks)}
    order = ["pre_mix_g", "post_mix_g", "pre_ffn_g", "post_ffn_g", "a_w_in", "a_b_in", "a_ln_g", "a_ln_b", "a_w_s",
             "a_b_s", "a_w_out", "b_w_qkv", "b_b_qkv", "b_sinks", "b_w_o", "ffn_w_gu", "ffn_w_down"]
    deltas, new_m, new_v = {}, {}, {}
    for nm in order:
        if nm in big_out:
            grads[nm], deltas[nm], new_m[nm], new_v[nm] = big_out[nm]
        else:
            w, m, v = weights[nm]
            deltas[nm], new_m[nm], new_v[nm] = _adamw_small(w, grads[nm], m, v, name="adamw_" + nm)
    return (loss, grad_x, *[grads[nm] for nm in order], *[deltas[nm] for nm in order],
            *[new_m[nm] for nm in order], *[new_v[nm] for nm in order])
```

```python
import functools
import math

import jax
import jax.numpy as jnp
import numpy as np
from jax import lax
from jax.experimental import pallas as pl
from jax.experimental.pallas import tpu as pltpu

F32 = jnp.float32
BF16 = jnp.bfloat16
MESH = pl.DeviceIdType.MESH

HEAD_DIM = 64
N_KV_HEADS = 4
ROPE_DIM = 16
ROPE_THETA = 500000.0
CHUNK = 128
GMLP_GROUPS = 8
RMS_EPS = 1e-6
LN_EPS = 1e-5
NEG_INF = -1e30
ADAM_LR = 0.001
ADAM_B1 = 0.9
ADAM_B2 = 0.999
ADAM_EPS = 1e-08
ADAM_WD = 0.01
ADAM_STEP = 10

N_CHIPS = 4
LANES = 128
VMEM_CAP = 58 * 1024 * 1024


def _vmem(est_bytes):
    assert est_bytes < VMEM_CAP
    return VMEM_CAP


def _pick(n, cands):
    for c in cands:
        if c <= n and n % c == 0:
            return c
    return n


def _nbytes(shape, dtype):
    return int(np.prod(shape)) * jnp.dtype(dtype).itemsize


def _matmul(a, b, *, mode, out_dtype, name, a_l=None, b_l=None, bias=None, into=None, o_l=None,
            q_off=0, b_r_off=0, tp=None, tq=None, tr=None):
    a2 = a.shape[-2:]
    b2 = b.shape[-2:]
    if mode == "nn":
        (P, R), (R2, Q) = a2, b2
    elif mode == "nt":
        (P, R), (Q, R2) = a2, b2
    else:
        (R, P), (R2, Q) = a2, b2
    assert R == R2 or (mode == "nt" and R2 % R == 0), (mode, a.shape, b.shape)
    tp = tp or _pick(P, (1024, 1408, 512, 384, 256, 128))
    tq = tq or _pick(Q, (1024, 1408, 768, 512, 384, 256, 128))
    tr = tr or _pick(R, (2048, 1408, 1024, 512, 256, 128))
    assert P % tp == 0 and Q % tq == 0 and R % tr == 0
    nk = R // tr
    dims = {"nn": (((1,), (0,)), ((), ())), "nt": (((1,), (1,)), ((), ())), "tn": (((0,), (0,)), ((), ()))}[mode]

    def lead(l, blk, idx):
        if l is None:
            return pl.BlockSpec(blk, idx)
        return pl.BlockSpec((None,) + blk, lambda i, j, k: (l,) + idx(i, j, k))

    if mode == "nn":
        a_spec = lead(a_l, (tp, tr), lambda i, j, k: (i, k))
        b_spec = lead(b_l, (tr, tq), lambda i, j, k: (k, j))
    elif mode == "nt":
        a_spec = lead(a_l, (tp, tr), lambda i, j, k: (i, k))
        b_spec = lead(b_l, (tq, tr), lambda i, j, k: (j, k + b_r_off))
    else:
        a_spec = lead(a_l, (tr, tp), lambda i, j, k: (k, i))
        b_spec = lead(b_l, (tr, tq), lambda i, j, k: (k, j))
    in_specs = [a_spec, b_spec]
    args = [a, b]
    if bias is not None:
        if bias.shape[0] == 1:
            in_specs.append(pl.BlockSpec((1, tq), lambda i, j, k: (0, j)))
        else:
            in_specs.append(pl.BlockSpec((tp, tq), lambda i, j, k: (i, j)))
        args.append(bias)
    aliases = {}
    if into is not None:
        in_specs.append(pl.BlockSpec(memory_space=pl.ANY))
        args.append(into)
        aliases = {len(args) - 1: 0}
        out_shape = jax.ShapeDtypeStruct(into.shape, into.dtype)
        out_dtype = into.dtype
        if o_l is None:
            out_spec = pl.BlockSpec((tp, tq), lambda i, j, k: (i, j + q_off))
        else:
            out_spec = pl.BlockSpec((None, tp, tq), lambda i, j, k: (o_l, i, j + q_off))
    else:
        out_shape = jax.ShapeDtypeStruct((P, Q), out_dtype)
        out_spec = pl.BlockSpec((tp, tq), lambda i, j, k: (i, j))
    has_bias = bias is not None
    has_into = into is not None

    def body(*refs):
        a_ref, b_ref = refs[0], refs[1]
        pos = 2
        bias_ref = None
        if has_bias:
            bias_ref = refs[pos]
            pos += 1
        if has_into:
            pos += 1
        o_ref = refs[pos]
        acc_ref = refs[pos + 1] if nk > 1 else None
        part = lax.dot_general(a_ref[...], b_ref[...], dims, preferred_element_type=F32)

        def finish(acc):
            if has_bias:
                acc = acc + bias_ref[...]
            o_ref[...] = acc.astype(out_dtype)

        if nk == 1:
            finish(part)
        else:
            k = pl.program_id(2)

            @pl.when(k == 0)
            def _():
                acc_ref[...] = part

            @pl.when(k > 0)
            def _():
                acc_ref[...] += part

            @pl.when(k == nk - 1)
            def _():
                finish(acc_ref[...])

    est = 2 * (_nbytes((tp, tr), a.dtype) + _nbytes((tr, tq), b.dtype) + _nbytes((tp, tq), out_dtype)) + 3 * tp * tq * 4
    return pl.pallas_call(
        body, name=name, out_shape=out_shape,
        grid=(P // tp, Q // tq, nk),
        in_specs=in_specs, out_specs=out_spec,
        scratch_shapes=[pltpu.VMEM((tp, tq), F32)] if nk > 1 else [],
        input_output_aliases=aliases,
        compiler_params=pltpu.CompilerParams(
            dimension_semantics=("parallel", "parallel", "arbitrary"), vmem_limit_bytes=_vmem(est)),
    )(*args)


def _row_call(body, ins, outs, *, name, rows, tr, acc_outs=(), est=0):
    in_specs = []
    for arr, kind in ins:
        if kind == "row":
            in_specs.append(pl.BlockSpec((tr, arr.shape[1]), lambda i: (i, 0)))
        else:
            nd = arr.ndim
            in_specs.append(pl.BlockSpec(arr.shape, lambda i, nd=nd: (0,) * nd))
    out_shapes = [jax.ShapeDtypeStruct(s, d) for s, d in outs] + [jax.ShapeDtypeStruct(s, d) for s, d in acc_outs]
    out_specs = [pl.BlockSpec((tr, s[1]), lambda i: (i, 0)) for s, _ in outs]
    out_specs += [pl.BlockSpec(s, lambda i, nd=len(s): (0,) * nd) for s, _ in acc_outs]
    res = pl.pallas_call(
        body, name=name, out_shape=out_shapes, grid=(rows // tr,), in_specs=in_specs, out_specs=out_specs,
        compiler_params=pltpu.CompilerParams(dimension_semantics=("arbitrary",), vmem_limit_bytes=_vmem(est)),
    )(*[a for a, _ in ins])
    return res


def _rms_fwd(x, g, *, out_dtype, name):
    T, D = x.shape
    tr = _pick(T, (512, 256, 128))

    def body(x_ref, g_ref, o_ref):
        xv = x_ref[...]
        r = lax.rsqrt(jnp.mean(xv * xv, axis=-1, keepdims=True) + RMS_EPS)
        o_ref[...] = (xv * r * g_ref[...]).astype(out_dtype)

    return _row_call(body, [(x, "row"), (g, "full")], [((T, D), out_dtype)], name=name, rows=T, tr=tr,
                     est=8 * tr * D * 4)[0]


def _rms_res(h, y, g, *, name):
    T, D = h.shape
    tr = _pick(T, (512, 256, 128))

    def body(h_ref, y_ref, g_ref, o_ref):
        yv = y_ref[...]
        r = lax.rsqrt(jnp.mean(yv * yv, axis=-1, keepdims=True) + RMS_EPS)
        o_ref[...] = h_ref[...] + yv * r * g_ref[...]

    return _row_call(body, [(h, "row"), (y, "row"), (g, "full")], [((T, D), F32)], name=name, rows=T, tr=tr,
                     est=10 * tr * D * 4)[0]


def _rms_bwd(x, g, dy, dres, *, out_dtype, name):
    T, D = x.shape
    tr = _pick(T, (512, 256, 128))
    has_res = dres is not None

    def body(*refs):
        if has_res:
            x_ref, g_ref, dy_ref, dr_ref, dx_ref, dg_ref = refs
        else:
            x_ref, g_ref, dy_ref, dx_ref, dg_ref = refs
        xv = x_ref[...]
        r = lax.rsqrt(jnp.mean(xv * xv, axis=-1, keepdims=True) + RMS_EPS)
        xhat = xv * r
        dyv = dy_ref[...].astype(F32)
        dxn = dyv * g_ref[...]
        dx = r * (dxn - xhat * jnp.mean(dxn * xhat, axis=-1, keepdims=True))
        if has_res:
            dx = dx + dr_ref[...]
        dx_ref[...] = dx.astype(out_dtype)
        part = jnp.sum(dyv * xhat, axis=0, keepdims=True)

        @pl.when(pl.program_id(0) == 0)
        def _():
            dg_ref[...] = part

        @pl.when(pl.program_id(0) > 0)
        def _():
            dg_ref[...] += part

    ins = [(x, "row"), (g, "full"), (dy, "row")] + ([(dres, "row")] if has_res else [])
    dx, dg = _row_call(body, ins, [((T, D), out_dtype)], name=name, rows=T, tr=tr, acc_outs=[((1, D), F32)],
                       est=12 * tr * D * 4)
    return dx, dg


def _ffn_up(fn, w_gu, l, *, name):
    T, D = fn.shape
    H = w_gu.shape[2] // 2
    tp = _pick(T, (512, 256, 128))
    tq = _pick(H, (1408, 768, 512, 256, 128))
    nj = H // tq

    def body(a_ref, wg_ref, wu_ref, g_ref, u_ref, act_ref):
        a = a_ref[...]
        g = jnp.dot(a, wg_ref[...], preferred_element_type=F32)
        u = jnp.dot(a, wu_ref[...], preferred_element_type=F32)
        g_ref[...] = g
        u_ref[...] = u
        act_ref[...] = (g * jax.nn.sigmoid(g) * u).astype(BF16)

    tile = pl.BlockSpec((tp, tq), lambda j, i: (i, j))
    est = 2 * (tp * D * 2 + 2 * D * tq * 2 + 2 * tp * tq * 4 + tp * tq * 2) + 4 * tp * tq * 4
    return pl.pallas_call(
        body, name=name,
        out_shape=[jax.ShapeDtypeStruct((T, H), F32), jax.ShapeDtypeStruct((T, H), F32),
                   jax.ShapeDtypeStruct((T, H), BF16)],
        grid=(nj, T // tp),
        in_specs=[pl.BlockSpec((tp, D), lambda j, i: (i, 0)),
                  pl.BlockSpec((None, D, tq), lambda j, i: (l, 0, j)),
                  pl.BlockSpec((None, D, tq), lambda j, i: (l, 0, j + nj))],
        out_specs=[tile, tile, tile],
        compiler_params=pltpu.CompilerParams(dimension_semantics=("parallel", "parallel"),
                                             vmem_limit_bytes=_vmem(est)),
    )(fn, w_gu, w_gu)


def _ffn_down_dx(df, w_down, l, g, u, *, name):
    T, D = df.shape
    H = w_down.shape[1]
    tp = _pick(T, (512, 256, 128))
    tq = _pick(H, (1408, 768, 512, 256, 128))

    def body(a_ref, w_ref, g_ref, u_ref, dg_ref, du_ref):
        da = lax.dot_general(a_ref[...], w_ref[...], (((1,), (1,)), ((), ())), preferred_element_type=F32)
        gv = g_ref[...]
        sg = jax.nn.sigmoid(gv)
        silu = gv * sg
        dg_ref[...] = (da * u_ref[...] * (sg + silu * (1.0 - sg))).astype(BF16)
        du_ref[...] = (da * silu).astype(BF16)

    tile = pl.BlockSpec((tp, tq), lambda j, i: (i, j))
    est = 2 * (tp * D * 2 + tq * D * 2 + 2 * tp * tq * 4 + 2 * tp * tq * 2) + 5 * tp * tq * 4
    return pl.pallas_call(
        body, name=name,
        out_shape=[jax.ShapeDtypeStruct((T, H), BF16), jax.ShapeDtypeStruct((T, H), BF16)],
        grid=(H // tq, T // tp),
        in_specs=[pl.BlockSpec((tp, D), lambda j, i: (i, 0)),
                  pl.BlockSpec((None, tq, D), lambda j, i: (l, j, 0)), tile, tile],
        out_specs=[tile, tile],
        compiler_params=pltpu.CompilerParams(dimension_semantics=("parallel", "parallel"),
                                             vmem_limit_bytes=_vmem(est)),
    )(df, w_down, g, u)


def _loss_and_grad(y, target, *, name):
    T, D = y.shape
    tr = _pick(T, (512, 256, 128))

    def body(y_ref, t_ref, dy_ref, l_ref):
        e = y_ref[...] - t_ref[...]
        dy_ref[...] = e * (1.0 / D)
        part = jnp.sum(jnp.sum(e * e, axis=1, keepdims=True), axis=0, keepdims=True) * (0.5 / D)

        @pl.when(pl.program_id(0) == 0)
        def _():
            l_ref[...] = part

        @pl.when(pl.program_id(0) > 0)
        def _():
            l_ref[...] += part

    dy, l = _row_call(body, [(y, "row"), (target, "row")], [((T, D), F32)], name=name, rows=T, tr=tr,
                      acc_outs=[((1, 1), F32)], est=8 * tr * D * 4)
    return dy, l


_SQRT_HALF = 0.7071067811865476
_INV_SQRT_2PI = 0.3989422804014327


def _gelu_parts(x):
    cdf = 0.5 * (1.0 + lax.erf(x * _SQRT_HALF))
    return cdf


def _sgu_common(pre, lng, lnb, W):
    cdf = _gelu_parts(pre)
    z = pre * cdf
    u = z[:, :W]
    v = z[:, W:]
    mu = jnp.mean(v, axis=-1, keepdims=True)
    vc = v - mu
    var = jnp.mean(vc * vc, axis=-1, keepdims=True)
    rstd = lax.rsqrt(var + LN_EPS)
    vhat = vc * rstd
    vn = vhat * lng + lnb
    return cdf, u, vhat, rstd, vn


def _causal_mask():
    t = lax.broadcasted_iota(jnp.int32, (CHUNK, CHUNK), 0)
    s = lax.broadcasted_iota(jnp.int32, (CHUNK, CHUNK), 1)
    return t >= s


def _sgu_fwd(pre, lng, lnb, ws, bsT, *, name):
    T, W2 = pre.shape
    W = W2 // 2
    G = ws.shape[0]
    gd = W // G

    def body(pre_ref, lng_ref, lnb_ref, ws_ref, bs_ref, o_ref):
        _, u, _, _, vn = _sgu_common(pre_ref[...], lng_ref[...], lnb_ref[...], W)
        vnb = vn.astype(BF16)
        causal = _causal_mask()
        for g in range(G):
            w = jnp.where(causal, ws_ref[g], 0.0).astype(BF16)
            sv = jnp.dot(w, vnb[:, g * gd:(g + 1) * gd], preferred_element_type=F32) + bs_ref[:, g:g + 1]
            o_ref[:, g * gd:(g + 1) * gd] = (u[:, g * gd:(g + 1) * gd] * sv).astype(BF16)

    return pl.pallas_call(
        body, name=name, out_shape=jax.ShapeDtypeStruct((T, W), BF16), grid=(T // CHUNK,),
        in_specs=[pl.BlockSpec((CHUNK, W2), lambda i: (i, 0)),
                  pl.BlockSpec((1, W), lambda i: (0, 0)), pl.BlockSpec((1, W), lambda i: (0, 0)),
                  pl.BlockSpec(ws.shape, lambda i: (0, 0, 0)), pl.BlockSpec(bsT.shape, lambda i: (0, 0))],
        out_specs=pl.BlockSpec((CHUNK, W), lambda i: (i, 0)),
        compiler_params=pltpu.CompilerParams(dimension_semantics=("arbitrary",),
                                             vmem_limit_bytes=_vmem(12 * CHUNK * W2 * 4)),
    )(pre, lng, lnb, ws, bsT)


def _sgu_bwd(pre, dgated, lng, lnb, ws, bsT, *, name):
    T, W2 = pre.shape
    W = W2 // 2
    G = ws.shape[0]
    gd = W // G

    def body(pre_ref, dgt_ref, lng_ref, lnb_ref, ws_ref, bs_ref,
             dpre_ref, dws_ref, dbs_ref, dlng_ref, dlnb_ref, dbin_ref):
        first = pl.program_id(0) == 0

        @pl.when(first)
        def _():
            dws_ref[...] = jnp.zeros_like(dws_ref)
            dbs_ref[...] = jnp.zeros_like(dbs_ref)
            dlng_ref[...] = jnp.zeros_like(dlng_ref)
            dlnb_ref[...] = jnp.zeros_like(dlnb_ref)
            dbin_ref[...] = jnp.zeros_like(dbin_ref)

        pre_v = pre_ref[...]
        lng_v = lng_ref[...]
        cdf, u, vhat, rstd, vn = _sgu_common(pre_v, lng_v, lnb_ref[...], W)
        vnb = vn.astype(BF16)
        dgt = dgt_ref[...].astype(F32)
        causal = _causal_mask()
        du_parts, dvn_parts = [], []
        for g in range(G):
            sl = slice(g * gd, (g + 1) * gd)
            w = jnp.where(causal, ws_ref[g], 0.0).astype(BF16)
            sv = jnp.dot(w, vnb[:, sl], preferred_element_type=F32) + bs_ref[:, g:g + 1]
            dgt_g = dgt[:, sl]
            du_parts.append(dgt_g * sv)
            dsv = dgt_g * u[:, sl]
            dsvb = dsv.astype(BF16)
            dvn_parts.append(lax.dot_general(w, dsvb, (((0,), (0,)), ((), ())), preferred_element_type=F32))
            dw = lax.dot_general(dsvb, vnb[:, sl], (((1,), (1,)), ((), ())), preferred_element_type=F32)
            dws_ref[g] += jnp.where(causal, dw, 0.0)
            dbs_ref[:, g:g + 1] += jnp.sum(dsv, axis=1, keepdims=True)
        du = jnp.concatenate(du_parts, axis=1)
        dvn = jnp.concatenate(dvn_parts, axis=1)
        dlng_ref[...] += jnp.sum(dvn * vhat, axis=0, keepdims=True)
        dlnb_ref[...] += jnp.sum(dvn, axis=0, keepdims=True)
        dvh = dvn * lng_v
        dv = rstd * (dvh - jnp.mean(dvh, axis=-1, keepdims=True)
                     - vhat * jnp.mean(dvh * vhat, axis=-1, keepdims=True))
        dz = jnp.concatenate([du, dv], axis=1)
        dgelu = cdf + pre_v * jnp.exp(-0.5 * pre_v * pre_v) * _INV_SQRT_2PI
        dpre = dz * dgelu
        dbin_ref[...] += jnp.sum(dpre, axis=0, keepdims=True)
        dpre_ref[...] = dpre.astype(BF16)

    full = lambda shape: pl.BlockSpec(shape, lambda i, nd=len(shape): (0,) * nd)
    return pl.pallas_call(
        body, name=name,
        out_shape=[jax.ShapeDtypeStruct((T, W2), BF16), jax.ShapeDtypeStruct(ws.shape, F32),
                   jax.ShapeDtypeStruct(bsT.shape, F32), jax.ShapeDtypeStruct((1, W), F32),
                   jax.ShapeDtypeStruct((1, W), F32), jax.ShapeDtypeStruct((1, W2), F32)],
        grid=(T // CHUNK,),
        in_specs=[pl.BlockSpec((CHUNK, W2), lambda i: (i, 0)), pl.BlockSpec((CHUNK, W), lambda i: (i, 0)),
                  full((1, W)), full((1, W)), full(ws.shape), full(bsT.shape)],
        out_specs=[pl.BlockSpec((CHUNK, W2), lambda i: (i, 0)), full(ws.shape), full(bsT.shape),
                   full((1, W)), full((1, W)), full((1, W2))],
        compiler_params=pltpu.CompilerParams(dimension_semantics=("arbitrary",),
                                             vmem_limit_bytes=_vmem(24 * CHUNK * W2 * 4)),
    )(pre, dgated, lng, lnb, ws, bsT)


def _rope_tables(positions):
    half = ROPE_DIM // 2
    inv_freq = ROPE_THETA ** (-jnp.arange(0, ROPE_DIM, 2, dtype=F32) / ROPE_DIM)
    ang = positions.astype(F32).reshape(-1, 1) * inv_freq
    cos, sin = jnp.cos(ang), jnp.sin(ang)
    T = ang.shape[0]
    rest = HEAD_DIM - ROPE_DIM
    c64 = jnp.concatenate([cos, cos, jnp.ones((T, rest), F32)], axis=1)
    s64 = jnp.concatenate([-sin, sin, jnp.zeros((T, rest), F32)], axis=1)
    del half
    return jnp.tile(c64, (1, LANES // HEAD_DIM)), jnp.tile(s64, (1, LANES // HEAD_DIM))


def _swap8(x):
    W = x.shape[1]
    half = ROPE_DIM // 2
    lane = lax.broadcasted_iota(jnp.int32, x.shape, 1) % HEAD_DIM
    return jnp.where(lane < half, pltpu.roll(x, W - half, axis=1),
                     jnp.where(lane < ROPE_DIM, pltpu.roll(x, half, axis=1), 0.0))


def _wide(tab, W):
    return jnp.concatenate([tab] * (W // LANES), axis=1) if W > LANES else tab


def _rope_fwd(qkv, ctab, stab, *, q_width, kv_width, name):
    T = qkv.shape[0]
    tr = _pick(T, (256, 128))
    scale = HEAD_DIM ** -0.5

    def body(x_ref, c_ref, s_ref, q_ref, k_ref, v_ref):
        c = c_ref[...]
        s = s_ref[...]
        q = x_ref[:, :q_width]
        k = x_ref[:, q_width:q_width + kv_width]
        q_ref[...] = ((q * _wide(c, q_width) + _swap8(q) * _wide(s, q_width)) * scale).astype(BF16)
        k_ref[...] = (k * _wide(c, kv_width) + _swap8(k) * _wide(s, kv_width)).astype(BF16)
        v_ref[...] = x_ref[:, q_width + kv_width:].astype(BF16)

    return _row_call(body, [(qkv, "row"), (ctab, "row"), (stab, "row")],
                     [((T, q_width), BF16), ((T, kv_width), BF16), ((T, kv_width), BF16)],
                     name=name, rows=T, tr=tr, est=10 * tr * qkv.shape[1] * 4)


_NT = (((1,), (1,)), ((), ()))
_TN = (((0,), (0,)), ((), ()))


def _group_rows(ref, heads):
    return jnp.concatenate([ref[:, h * HEAD_DIM:(h + 1) * HEAD_DIM] for h in heads], axis=0)


def _attn_group_probs(q, kk, sinks, n, grp):
    rows = grp * CHUNK
    s = lax.dot_general(q, kk, _NT, preferred_element_type=F32)
    qi = lax.broadcasted_iota(jnp.int32, (rows, 2 * CHUNK), 0) & (CHUNK - 1)
    sj = lax.broadcasted_iota(jnp.int32, (rows, 2 * CHUNK), 1)
    valid = ((sj < CHUNK) & (sj > qi) & (n > 0)) | ((sj >= CHUNK) & (sj - CHUNK <= qi))
    s = jnp.where(valid, s, NEG_INF)
    r = lax.broadcasted_iota(jnp.int32, (rows, 1), 0)
    sink = jnp.full((rows, 1), sinks[grp - 1], F32)
    for g in range(grp - 2, -1, -1):
        sink = jnp.where(r < (g + 1) * CHUNK, sinks[g], sink)
    m = jnp.maximum(jnp.max(s, axis=1, keepdims=True), sink)
    p = jnp.exp(s - m)
    ps = jnp.exp(sink - m)
    inv = 1.0 / (jnp.sum(p, axis=1, keepdims=True) + ps)
    return p * inv, ps * inv


def _kv_specs(width, nb):
    prev = pl.BlockSpec((CHUNK, width), lambda n: (jnp.maximum(n - 1, 0), 0))
    cur = pl.BlockSpec((CHUNK, width), lambda n: (n, 0))
    return prev, cur


def _attn_fwd(qr, kr, vr, sinks, *, name):
    T, QW = qr.shape
    KW = kr.shape[1]
    HQ, HK = QW // HEAD_DIM, KW // HEAD_DIM
    grp = HQ // HK
    nb = T // CHUNK

    def body(q_ref, kp_ref, kc_ref, vp_ref, vc_ref, s_ref, o_ref):
        n = pl.program_id(0)
        for kh in range(HK):
            ks = slice(kh * HEAD_DIM, (kh + 1) * HEAD_DIM)
            heads = list(range(kh * grp, (kh + 1) * grp))
            q = _group_rows(q_ref, heads)
            kk = jnp.concatenate([kp_ref[:, ks], kc_ref[:, ks]], axis=0)
            vv = jnp.concatenate([vp_ref[:, ks], vc_ref[:, ks]], axis=0)
            p, _ = _attn_group_probs(q, kk, [s_ref[0, h] for h in heads], n, grp)
            o = jnp.dot(p.astype(BF16), vv, preferred_element_type=F32).astype(BF16)
            for g, h in enumerate(heads):
                o_ref[:, h * HEAD_DIM:(h + 1) * HEAD_DIM] = o[g * CHUNK:(g + 1) * CHUNK]

    kp, kc = _kv_specs(KW, nb)
    return pl.pallas_call(
        body, name=name, out_shape=jax.ShapeDtypeStruct((T, QW), BF16), grid=(nb,),
        in_specs=[pl.BlockSpec((CHUNK, QW), lambda n: (n, 0)), kp, kc, kp, kc,
                  pl.BlockSpec(memory_space=pltpu.SMEM)],
        out_specs=pl.BlockSpec((CHUNK, QW), lambda n: (n, 0)),
        compiler_params=pltpu.CompilerParams(dimension_semantics=("arbitrary",), vmem_limit_bytes=_vmem(8 << 20)),
    )(qr, kr, kr, vr, vr, sinks)


def _attn_bwd(qr, kr, vr, sinks, do, *, name):
    T, QW = qr.shape
    KW = kr.shape[1]
    HQ, HK = QW // HEAD_DIM, KW // HEAD_DIM
    grp = HQ // HK
    nb = T // CHUNK

    def body(q_ref, kp_ref, kc_ref, vp_ref, vc_ref, s_ref, do_ref,
             dq_ref, dkp_ref, dkc_ref, dvp_ref, dvc_ref, ds_ref):
        n = pl.program_id(0)
        lane = lax.broadcasted_iota(jnp.int32, (1, LANES), 1)
        dsink = jnp.zeros((1, LANES), F32)
        for kh in range(HK):
            ks = slice(kh * HEAD_DIM, (kh + 1) * HEAD_DIM)
            heads = list(range(kh * grp, (kh + 1) * grp))
            q = _group_rows(q_ref, heads)
            doh = _group_rows(do_ref, heads)
            kk = jnp.concatenate([kp_ref[:, ks], kc_ref[:, ks]], axis=0)
            vv = jnp.concatenate([vp_ref[:, ks], vc_ref[:, ks]], axis=0)
            p, ps = _attn_group_probs(q, kk, [s_ref[0, h] for h in heads], n, grp)
            dp = lax.dot_general(doh, vv, _NT, preferred_element_type=F32)
            delta = jnp.sum(p * dp, axis=1, keepdims=True)
            ds = (p * (dp - delta)).astype(BF16)
            dv = lax.dot_general(p.astype(BF16), doh, _TN, preferred_element_type=F32)
            dk = lax.dot_general(ds, q, _TN, preferred_element_type=F32)
            dq = jnp.dot(ds, kk, preferred_element_type=F32)
            psd = ps * delta
            for g, h in enumerate(heads):
                dq_ref[:, h * HEAD_DIM:(h + 1) * HEAD_DIM] = dq[g * CHUNK:(g + 1) * CHUNK]
                dsink = dsink + jnp.where(
                    lane == h, -jnp.sum(psd[g * CHUNK:(g + 1) * CHUNK], axis=0, keepdims=True), 0.0)
            dkp_ref[:, ks] = dk[:CHUNK]
            dkc_ref[:, ks] = dk[CHUNK:]
            dvp_ref[:, ks] = dv[:CHUNK]
            dvc_ref[:, ks] = dv[CHUNK:]

        @pl.when(n == 0)
        def _():
            ds_ref[...] = dsink

        @pl.when(n > 0)
        def _():
            ds_ref[...] += dsink

    kp, kc = _kv_specs(KW, nb)
    qspec = pl.BlockSpec((CHUNK, QW), lambda n: (n, 0))
    kout = pl.BlockSpec((CHUNK, KW), lambda n: (n, 0))
    return pl.pallas_call(
        body, name=name,
        out_shape=[jax.ShapeDtypeStruct((T, QW), F32)] + [jax.ShapeDtypeStruct((T, KW), F32)] * 4
        + [jax.ShapeDtypeStruct((1, LANES), F32)],
        grid=(nb,),
        in_specs=[qspec, kp, kc, kp, kc, pl.BlockSpec(memory_space=pltpu.SMEM), qspec],
        out_specs=[qspec, kout, kout, kout, kout, pl.BlockSpec((1, LANES), lambda n: (0, 0))],
        compiler_params=pltpu.CompilerParams(dimension_semantics=("arbitrary",), vmem_limit_bytes=_vmem(12 << 20)),
    )(qr, kr, kr, vr, vr, sinks, do)


def _rope_bwd(dq, dkp, dkc, dvp, dvc, ctab, stab, *, name):
    T, QW = dq.shape
    KW = dkp.shape[1]
    nb = T // CHUNK
    scale = HEAD_DIM ** -0.5
    width = QW + 2 * KW

    def body(dq_ref, dkc_ref, dkn_ref, dvc_ref, dvn_ref, c_ref, s_ref, o_ref, db_ref):
        n = pl.program_id(0)
        c = c_ref[...]
        s = s_ref[...]
        has_next = (n < nb - 1).astype(F32)
        dqv = dq_ref[...]
        dk = dkc_ref[...] + has_next * dkn_ref[...]
        dv = dvc_ref[...] + has_next * dvn_ref[...]
        dq_pre = (dqv * _wide(c, QW) + _swap8(dqv * _wide(s, QW))) * scale
        dk_pre = dk * _wide(c, KW) + _swap8(dk * _wide(s, KW))
        o_ref[:, :QW] = dq_pre.astype(BF16)
        o_ref[:, QW:QW + KW] = dk_pre.astype(BF16)
        o_ref[:, QW + KW:] = dv.astype(BF16)
        part = jnp.concatenate([jnp.sum(dq_pre, axis=0, keepdims=True), jnp.sum(dk_pre, axis=0, keepdims=True),
                                jnp.sum(dv, axis=0, keepdims=True)], axis=1)

        @pl.when(n == 0)
        def _():
            db_ref[...] = part

        @pl.when(n > 0)
        def _():
            db_ref[...] += part

    cur = lambda w: pl.BlockSpec((CHUNK, w), lambda n: (n, 0))
    nxt = lambda w: pl.BlockSpec((CHUNK, w), lambda n: (jnp.minimum(n + 1, nb - 1), 0))
    return pl.pallas_call(
        body, name=name,
        out_shape=[jax.ShapeDtypeStruct((T, width), BF16), jax.ShapeDtypeStruct((1, width), F32)],
        grid=(nb,),
        in_specs=[cur(QW), cur(KW), nxt(KW), cur(KW), nxt(KW), cur(LANES), cur(LANES)],
        out_specs=[cur(width), pl.BlockSpec((1, width), lambda n: (0, 0))],
        compiler_params=pltpu.CompilerParams(dimension_semantics=("arbitrary",), vmem_limit_bytes=_vmem(8 << 20)),
    )(dq, dkc, dkp, dvc, dvp, ctab, stab)


def _cast_block(w, l, axis, chip_arr, *, name):
    _, Ks, Ns = w.shape
    tk = _pick(Ks, (512, 352, 256, 128))
    nk = Ks // tk
    full = (Ks * N_CHIPS, Ns) if axis == 0 else (Ks, Ns * N_CHIPS)

    def body(p_ref, w_ref, o_ref):
        o_ref[...] = w_ref[...].astype(BF16)

    if axis == 0:
        out_spec = pl.BlockSpec((tk, Ns), lambda i, p: (p[0] * nk + i, 0))
    else:
        out_spec = pl.BlockSpec((tk, Ns), lambda i, p: (i, p[0]))
    grid_spec = pltpu.PrefetchScalarGridSpec(
        num_scalar_prefetch=1, grid=(nk,),
        in_specs=[pl.BlockSpec((None, tk, Ns), lambda i, p: (l, i, 0))], out_specs=out_spec)
    return pl.pallas_call(
        body, name=name, out_shape=jax.ShapeDtypeStruct(full, BF16), grid_spec=grid_spec,
        compiler_params=pltpu.CompilerParams(dimension_semantics=("arbitrary",),
                                             vmem_limit_bytes=_vmem(4 * tk * Ns * 6)),
    )(chip_arr, w)


def _adamw_math(w, g, m, v):
    m = ADAM_B1 * m + (1.0 - ADAM_B1) * g
    v = ADAM_B2 * v + (1.0 - ADAM_B2) * (g * g)
    m_hat = m / (1.0 - ADAM_B1 ** ADAM_STEP)
    v_hat = v / (1.0 - ADAM_B2 ** ADAM_STEP)
    delta = -ADAM_LR * (m_hat / (jnp.sqrt(v_hat) + ADAM_EPS) + ADAM_WD * w)
    return delta, m, v


def _adamw_layer(w, m, v, g, l, outs, *, name):
    _, K, N = w.shape
    tk = _pick(K, (256, 176, 128))

    def body(w_ref, m_ref, v_ref, g_ref, _g, _d, _m, _v, go_ref, d_ref, mo_ref, vo_ref):
        gv = g_ref[...]
        d, mn, vn = _adamw_math(w_ref[...], gv, m_ref[...], v_ref[...])
        go_ref[...] = gv
        d_ref[...] = d
        mo_ref[...] = mn
        vo_ref[...] = vn

    layer = pl.BlockSpec((None, tk, N), lambda i: (l, i, 0))
    any_spec = pl.BlockSpec(memory_space=pl.ANY)
    sd = jax.ShapeDtypeStruct(w.shape, F32)
    return pl.pallas_call(
        body, name=name, out_shape=[sd, sd, sd, sd], grid=(K // tk,),
        in_specs=[layer, layer, layer, pl.BlockSpec((tk, N), lambda i: (i, 0))] + [any_spec] * 4,
        out_specs=[layer] * 4, input_output_aliases={4: 0, 5: 1, 6: 2, 7: 3},
        compiler_params=pltpu.CompilerParams(dimension_semantics=("arbitrary",),
                                             vmem_limit_bytes=_vmem(2 * 8 * tk * N * 4 + 6 * tk * N * 4)),
    )(w, m, v, g, *outs)


def _adamw_small(w, g, m, v, *, name):
    def body(w_ref, g_ref, m_ref, v_ref, d_ref, mo_ref, vo_ref):
        d, mn, vn = _adamw_math(w_ref[...], g_ref[...], m_ref[...], v_ref[...])
        d_ref[...] = d
        mo_ref[...] = mn
        vo_ref[...] = vn

    sd = jax.ShapeDtypeStruct(w.shape, F32)
    return pl.pallas_call(body, name=name, out_shape=[sd, sd, sd])(w, g, m, v)


def _my_place():
    return lax.axis_index("x"), lax.axis_index("y"), lax.axis_index("c")


def _peer_chips(x, y):
    return [(1 - x, y), (x, 1 - y), (1 - x, 1 - y)]


_HBM = pl.BlockSpec(memory_space=pltpu.HBM)
_SEM = pl.BlockSpec(memory_space=pltpu.SEMAPHORE)
_EFFECT = pltpu.SideEffectType.DATAFLOW_SIDE_EFFECTING


def _split_start(name, bufs, n_copies, make_copies, after):
    nb = len(bufs)

    def body(*refs):
        send_sems, recv_sems = refs[nb + 1], refs[nb + 2]
        token = refs[2 * nb + 3]
        sends, _ = make_copies(refs[:nb], send_sems, recv_sems)
        for cp in sends:
            cp.start()
        token[...] = jnp.zeros_like(token)

    res = pl.pallas_call(
        body, name=name,
        out_shape=(pltpu.SemaphoreType.DMA((n_copies,)), pltpu.SemaphoreType.DMA((n_copies,)),
                   *[pltpu.HBM(b.shape, b.dtype) for b in bufs], jax.ShapeDtypeStruct((8, LANES), F32)),
        in_specs=[_HBM] * nb + [pl.BlockSpec(memory_space=pl.ANY)],
        out_specs=(_SEM, _SEM, *[_HBM] * nb, pl.BlockSpec(memory_space=pltpu.VMEM)),
        input_output_aliases={k: 2 + k for k in range(nb)},
        compiler_params=pltpu.CompilerParams(has_side_effects=_EFFECT),
    )(*[pltpu.with_memory_space_constraint(b, pltpu.HBM) for b in bufs], after)
    return res[0], res[1], list(res[2:2 + nb]), res[2 + nb]


def _split_wait(name, bufs, sems, make_copies, after):
    nb = len(bufs)

    def body(*refs):
        send_sems, recv_sems = refs[nb], refs[nb + 1]
        sends, recvs = make_copies(refs[:nb], send_sems, recv_sems)
        for cp in sends:
            cp.wait_send()
        for cp in recvs:
            cp.wait_recv()

    res = pl.pallas_call(
        body, name=name,
        out_shape=tuple(pltpu.HBM(b.shape, b.dtype) for b in bufs),
        in_specs=[_HBM] * nb + [_SEM, _SEM, pl.BlockSpec(memory_space=pl.ANY)],
        out_specs=tuple([_HBM] * nb),
        input_output_aliases={k: k for k in range(nb)},
        compiler_params=pltpu.CompilerParams(has_side_effects=_EFFECT),
    )(*bufs, sems[0], sems[1], after)
    return list(res)


def _remote(src, dst, send_sems, recv_sems, k, target):
    return pltpu.make_async_remote_copy(src_ref=src, dst_ref=dst, send_sem=send_sems.at[k],
                                        recv_sem=recv_sems.at[k], device_id=target, device_id_type=MESH)


def _ag_region(ref, axis, chip, half):
    K, N = ref.shape
    if axis == 0:
        hs = K // N_CHIPS // 2
        assert hs % 16 == 0
        return ref.at[pl.ds(pl.multiple_of((2 * chip + half) * hs, 16), hs), :]
    ns, hk = N // N_CHIPS, K // 2
    assert ns % LANES == 0 and hk % 16 == 0
    return ref.at[pl.ds(pl.multiple_of(half * hk, 16), hk), pl.ds(pl.multiple_of(chip * ns, LANES), ns)]


def _ag_copies(stage, axes):
    n = len(axes)

    def make(bufs, send_sems, recv_sems):
        x, y, c = _my_place()
        me = 2 * x + y
        sends, recvs = [], []
        for j, (px, py) in enumerate(_peer_chips(x, y)):
            other = 2 * px + py
            for w in range(n):
                k = j * n + w
                if stage == 1:
                    src, target = _ag_region(bufs[w], axes[w], me, c), (px, py, c)
                    land = _ag_region(bufs[w], axes[w], other, c)
                else:
                    src, target = _ag_region(bufs[w], axes[w], other, c), (x, y, 1 - c)
                    land = _ag_region(bufs[w], axes[w], other, 1 - c)
                sends.append(_remote(src, src, send_sems, recv_sems, k, target))
                recvs.append(_remote(land, land, send_sems, recv_sems, k, target))
        return sends, recvs

    return make


def _half_shape(shape, axis):
    K, N = shape
    return (K, N // 2) if axis == 0 else (K // 2, N)


def _core_half(ref, axis, half):
    K, N = ref.shape
    if axis == 0:
        return ref.at[:, pl.ds(pl.multiple_of(half * (N // 2), LANES), N // 2)]
    return ref.at[pl.ds(pl.multiple_of(half * (K // 2), 16), K // 2), :]


def _chip_block(ref, axis, chip):
    K, N = ref.shape
    if axis == 0:
        return ref.at[pl.ds(pl.multiple_of(chip * (K // N_CHIPS), 16), K // N_CHIPS), :]
    return ref.at[:, pl.ds(pl.multiple_of(chip * (N // N_CHIPS), LANES), N // N_CHIPS)]


def _rs_sibling_copies(axes):
    n = len(axes)

    def make(bufs, send_sems, recv_sems):
        x, y, c = _my_place()
        sends = [_remote(_core_half(bufs[w], axes[w], 1 - c), bufs[n + w], send_sems, recv_sems, w, (x, y, 1 - c))
                 for w in range(n)]
        recvs = [_remote(bufs[n + w], bufs[n + w], send_sems, recv_sems, w, (x, y, 1 - c)) for w in range(n)]
        return sends, recvs

    return make


def _rs_chip_copies(axes):
    n = len(axes)

    def make(bufs, send_sems, recv_sems):
        x, y, c = _my_place()
        sends, recvs = [], []
        for j, (px, py) in enumerate(_peer_chips(x, y)):
            for w in range(n):
                k = j * n + w
                sends.append(_remote(_chip_block(bufs[w], axes[w], 2 * px + py), bufs[n + w].at[j],
                                     send_sems, recv_sems, k, (px, py, c)))
                recvs.append(_remote(bufs[n + w].at[j], bufs[n + w].at[j], send_sems, recv_sems, k, (px, py, c)))
        return sends, recvs

    return make


def _rs_fill_copies(axes):
    n = len(axes)

    def make(bufs, send_sems, recv_sems):
        x, y, c = _my_place()
        sends = [_remote(_core_half(bufs[w], axes[w], c), _core_half(bufs[w], axes[w], c),
                         send_sems, recv_sems, w, (x, y, 1 - c)) for w in range(n)]
        recvs = [_remote(_core_half(bufs[w], axes[w], 1 - c), _core_half(bufs[w], axes[w], 1 - c),
                         send_sems, recv_sems, w, (x, y, 1 - c)) for w in range(n)]
        return sends, recvs

    return make


def _chip_sum(g, r, axis, place, *, name):
    hk, hn = r.shape
    tk = 128 if hn > 4096 else _pick(hk, (256, 128))
    nk = hk // tk

    def body(p_ref, g_ref, r_ref, f_ref, b_ref):
        s = g_ref[...] + r_ref[...]
        f_ref[...] = s
        b_ref[...] = s.astype(BF16)

    half = pl.BlockSpec((tk, hn), lambda i, p: (i, 0))
    if axis == 0:
        g_spec = pl.BlockSpec((tk, hn), lambda i, p: (i, p[1]))
    else:
        g_spec = pl.BlockSpec((tk, hn), lambda i, p: (p[1] * nk + i, 0))
    grid_spec = pltpu.PrefetchScalarGridSpec(num_scalar_prefetch=1, grid=(nk,), in_specs=[g_spec, half],
                                             out_specs=[half, half])
    return pl.pallas_call(
        body, name=name,
        out_shape=[jax.ShapeDtypeStruct(r.shape, F32), jax.ShapeDtypeStruct(r.shape, BF16)],
        grid_spec=grid_spec,
        compiler_params=pltpu.CompilerParams(dimension_semantics=("arbitrary",),
                                             vmem_limit_bytes=_vmem(2 * tk * hn * 14)),
    )(place, g, r)


def _final_sum(own, recv, axis, place, *, name):
    _, bk, bn = recv.shape
    tk = _pick(bk, (256, 176, 128))
    nk = bk // tk

    def body(p_ref, o_ref, r_ref, out_ref):
        out_ref[...] = ((o_ref[...] + r_ref[0].astype(F32)) + r_ref[1].astype(F32)) + r_ref[2].astype(F32)

    if axis == 0:
        own_spec = pl.BlockSpec((tk, bn), lambda i, p: (p[0] * nk + i, 0))
        out_shape, out_spec = (bk, 2 * bn), pl.BlockSpec((tk, bn), lambda i, p: (i, p[1]))
    else:
        own_spec = pl.BlockSpec((tk, bn), lambda i, p: (i, p[0]))
        out_shape, out_spec = (2 * bk, bn), pl.BlockSpec((tk, bn), lambda i, p: (p[1] * nk + i, 0))
    grid_spec = pltpu.PrefetchScalarGridSpec(
        num_scalar_prefetch=1, grid=(nk,),
        in_specs=[own_spec, pl.BlockSpec((3, tk, bn), lambda i, p: (0, i, 0))], out_specs=out_spec)
    return pl.pallas_call(
        body, name=name, out_shape=jax.ShapeDtypeStruct(out_shape, F32), grid_spec=grid_spec,
        compiler_params=pltpu.CompilerParams(dimension_semantics=("arbitrary",),
                                             vmem_limit_bytes=_vmem(2 * tk * bn * 14 + 4 * tk * bn * 4)),
    )(place, own, recv)


def _allreduce_small(p):
    def body(p_ref, o_ref, r0, r1, r2, send_sems, recv_sems):
        x, y, c = _my_place()
        o_ref[...] = p_ref[...]
        for s, (peer, rbuf) in enumerate([((x, y, 1 - c), r0), ((1 - x, y, c), r1), ((x, 1 - y, c), r2)]):
            cp = pltpu.make_async_remote_copy(src_ref=o_ref, dst_ref=rbuf, send_sem=send_sems.at[s],
                                              recv_sem=recv_sems.at[s], device_id=peer, device_id_type=MESH)
            cp.start()
            cp.wait()
            o_ref[...] = o_ref[...] + rbuf[...]

    vm = pl.BlockSpec(memory_space=pltpu.VMEM)
    return pl.pallas_call(
        body, name="allreduce_small", out_shape=jax.ShapeDtypeStruct(p.shape, F32),
        in_specs=[vm], out_specs=vm,
        scratch_shapes=[pltpu.VMEM(p.shape, F32)] * 3 + [pltpu.SemaphoreType.DMA((3,))] * 2,
        compiler_params=pltpu.CompilerParams(vmem_limit_bytes=_vmem(6 * _nbytes(p.shape, F32))),
    )(p)


def _pack_rows(parts):
    rows, metas = [], []
    for a in parts:
        flat = a.reshape(-1)
        nrow = -(-flat.shape[0] // LANES)
        nrow = -(-nrow // 8) * 8
        flat = jnp.pad(flat, (0, nrow * LANES - flat.shape[0]))
        rows.append(flat.reshape(nrow, LANES))
        metas.append((a.shape, nrow))
    return jnp.concatenate(rows, axis=0), metas


def _unpack_rows(packed, metas):
    out, r0 = [], 0
    for shape, nrow in metas:
        size = int(np.prod(shape))
        out.append(packed[r0:r0 + nrow].reshape(-1)[:size].reshape(shape))
        r0 += nrow
    return out


def kernel(x, positions, pre_mix_g, post_mix_g, pre_ffn_g, post_ffn_g, a_w_in, a_b_in, a_ln_g, a_ln_b, a_w_s, a_b_s, a_w_out, b_w_qkv, b_b_qkv, b_sinks, b_w_o, ffn_w_gu, ffn_w_down, loss_target, m_pre_mix_g, m_post_mix_g, m_pre_ffn_g, m_post_ffn_g, m_a_w_in, m_a_b_in, m_a_ln_g, m_a_ln_b, m_a_w_s, m_a_b_s, m_a_w_out, m_b_w_qkv, m_b_b_qkv, m_b_sinks, m_b_w_o, m_ffn_w_gu, m_ffn_w_down, v_pre_mix_g, v_post_mix_g, v_pre_ffn_g, v_post_ffn_g, v_a_w_in, v_a_b_in, v_a_ln_g, v_a_ln_b, v_a_w_s, v_a_b_s, v_a_w_out, v_b_w_qkv, v_b_b_qkv, v_b_sinks, v_b_w_o, v_ffn_w_gu, v_ffn_w_down):
    depth, D = pre_mix_g.shape
    xi, yi, ci = _my_place()
    chip = 2 * xi + yi
    place = jnp.stack([chip, ci]).astype(jnp.int32)

    stacked = {"a_w_in": (a_w_in, m_a_w_in, v_a_w_in), "a_w_out": (a_w_out, m_a_w_out, v_a_w_out),
               "b_w_qkv": (b_w_qkv, m_b_w_qkv, v_b_w_qkv), "b_w_o": (b_w_o, m_b_w_o, v_b_w_o),
               "ffn_w_gu": (ffn_w_gu, m_ffn_w_gu, v_ffn_w_gu), "ffn_w_down": (ffn_w_down, m_ffn_w_down, v_ffn_w_down)}
    cut = {"a_w_in": 1, "a_w_out": 0, "b_w_qkv": 1, "b_w_o": 0, "ffn_w_gu": 1, "ffn_w_down": 0}

    def layer_keys(i):
        mix = [("a_w_in", i // 2), ("a_w_out", i // 2)] if i % 2 == 0 else [("b_w_qkv", i // 2), ("b_w_o", i // 2)]
        return mix + [("ffn_w_gu", i), ("ffn_w_down", i)]

    def dep(a, toks):
        for t in toks:
            a = a + t[:1, :1]
        return a

    W = {}
    for i in range(depth):
        for nm, l in layer_keys(i):
            W[(nm, l)] = _cast_block(stacked[nm][0], l, cut[nm], place, name=f"cast_{nm}_{l}")

    def gather(tag, keys, after):
        axes = [cut[nm] for nm, _ in keys]
        for stage in (1, 2):
            ss, rs, bufs, tok = _split_start(f"ag{stage}_start_{tag}", [W[k] for k in keys], 3 * len(keys),
                                             _ag_copies(stage, axes), after)
            after = yield tok
            bufs = _split_wait(f"ag{stage}_wait_{tag}", bufs, (ss, rs), _ag_copies(stage, axes), after)
            W.update(zip(keys, bufs))
        yield None

    first = gather("0m", layer_keys(0)[:2], place)
    tok = next(first)
    tok = first.send(tok)
    first.send(tok)

    h = x[0]
    target = loss_target[0]
    ctab, stab = _rope_tables(positions[0])
    q_width = W[("b_w_o", 0)].shape[0]
    kv_width = N_KV_HEADS * HEAD_DIM
    row = lambda a, i: a[i:i + 1]

    nq = b_b_qkv.shape[1]
    bq_full = jnp.zeros((b_b_qkv.shape[0], N_CHIPS * nq), F32)
    bq_full = lax.dynamic_update_slice(bq_full, jnp.where(ci == 0, b_b_qkv, 0.0), (0, chip * nq))
    bq_packed, bq_meta = _pack_rows([bq_full])
    b_qkv_full = _unpack_rows(_allreduce_small(bq_packed), bq_meta)[0]

    saved = []
    for i in range(depth):
        j = i // 2
        s = {"h": h}
        toks = []
        ffn_w = None
        if i == 0:
            ffn_w = gather("0f", layer_keys(0)[2:], W[("a_w_out", 0)])
            toks.append(next(ffn_w))
            nxt = gather("1", layer_keys(1), toks[0])
            toks.append(next(nxt))
        elif i + 1 < depth:
            nxt = gather(str(i + 1), layer_keys(i + 1), h)
            toks.append(next(nxt))
        hn = _rms_fwd(h, dep(row(pre_mix_g, i), toks), out_dtype=BF16, name=f"rms_pre_mix_{i}")
        s["hn"] = hn
        if i % 2 == 0:
            pre = _matmul(hn, W[("a_w_in", j)], mode="nn", bias=row(a_b_in, j), out_dtype=F32, name=f"gmlp_in_{i}")
            gated = _sgu_fwd(pre, row(a_ln_g, j), row(a_ln_b, j), a_w_s[j], a_b_s[j].T, name=f"sgu_fwd_{i}")
            mix = _matmul(gated, W[("a_w_out", j)], mode="nn", out_dtype=F32, name=f"gmlp_out_{i}")
            s.update(pre=pre, gated=gated)
        else:
            qkv = _matmul(hn, W[("b_w_qkv", j)], mode="nn", bias=row(b_qkv_full, j), out_dtype=F32,
                          name=f"attn_qkv_{i}")
            qr, kr, vr = _rope_fwd(qkv, ctab, stab, q_width=q_width, kv_width=kv_width, name=f"rope_fwd_{i}")
            o = _attn_fwd(qr, kr, vr, row(b_sinks, j), name=f"attn_fwd_{i}")
            mix = _matmul(o, W[("b_w_o", j)], mode="nn", out_dtype=F32, name=f"attn_o_{i}")
            s.update(qr=qr, kr=kr, vr=vr, o=o)
        s["mix"] = mix
        toks = [ffn_w.send(mix)] if ffn_w else []
        h1 = _rms_res(h, mix, dep(row(post_mix_g, i), toks), name=f"rms_post_mix_{i}")
        if ffn_w:
            ffn_w.send(h1)
        s["h1"] = h1
        fn = _rms_fwd(h1, row(pre_ffn_g, i), out_dtype=BF16, name=f"rms_pre_ffn_{i}")
        g_pre, u_pre, act = _ffn_up(fn, W[("ffn_w_gu", i)][None], 0, name=f"ffn_up_{i}")
        f = _matmul(act, W[("ffn_w_down", i)], mode="nn", out_dtype=F32, name=f"ffn_down_{i}")
        toks = [nxt.send(f)] if i + 1 < depth else []
        h = _rms_res(h1, f, dep(row(post_ffn_g, i), toks), name=f"rms_post_ffn_{i}")
        if i + 1 < depth:
            nxt.send(h)
        s.update(fn=fn, g_pre=g_pre, u_pre=u_pre, act=act, f=f)
        saved.append(s)

    dh, loss_part = _loss_and_grad(h, target, name="loss")
    loss = lax.psum(loss_part[0, 0], ("x", "y", "c"))

    big_out = {nm: tuple(lax.empty(w.shape, F32) for _ in range(4)) for nm, (w, _, _) in stacked.items()}

    def reduce_group(i, keys, grads):
        axes = [cut[nm] for nm, _ in keys]
        n = len(keys)
        lands = [lax.empty(_half_shape(g.shape, ax), F32) for g, ax in zip(grads, axes)]
        ss, rs, bufs, tok = _split_start(f"rs_sibling_start_{i}", list(grads) + lands, n, _rs_sibling_copies(axes),
                                         place)
        after = yield tok
        bufs = _split_wait(f"rs_sibling_wait_{i}", bufs, (ss, rs), _rs_sibling_copies(axes), after)
        sums = [_chip_sum(bufs[w], bufs[n + w], axes[w], place, name=f"chip_sum_{keys[w][0]}_{keys[w][1]}")
                for w in range(n)]
        lands = []
        for (sf, _), ax in zip(sums, axes):
            hk, hn = sf.shape
            lands.append(lax.empty((3, hk // N_CHIPS, hn) if ax == 0 else (3, hk, hn // N_CHIPS), BF16))
        ss, rs, bufs, tok = _split_start(f"rs_chip_start_{i}", [sb for _, sb in sums] + lands, 3 * n,
                                         _rs_chip_copies(axes), place)
        after = yield tok
        bufs = _split_wait(f"rs_chip_wait_{i}", bufs, (ss, rs), _rs_chip_copies(axes), after)
        blocks = [_final_sum(sums[w][0], bufs[n + w], axes[w], place, name=f"final_sum_{keys[w][0]}_{keys[w][1]}")
                  for w in range(n)]
        ss, rs, bufs, tok = _split_start(f"rs_fill_start_{i}", blocks, n, _rs_fill_copies(axes), place)
        after = yield tok
        blocks = _split_wait(f"rs_fill_wait_{i}", bufs, (ss, rs), _rs_fill_copies(axes), after)
        for (nm, l), g in zip(keys, blocks):
            w, m, v = stacked[nm]
            big_out[nm] = tuple(_adamw_layer(w, m, v, g, l, big_out[nm], name=f"adamw_{nm}_{l}"))
        yield None

    reducing = []

    def advance(after):
        toks = []
        for gen in list(reducing):
            tok = gen.send(after)
            if tok is None:
                reducing.remove(gen)
            else:
                toks.append(tok)
        return toks

    small = {}
    g_pre_mix, g_post_mix, g_pre_ffn, g_post_ffn = [None] * depth, [None] * depth, [None] * depth, [None] * depth
    toks = []
    for i in reversed(range(depth)):
        j = i // 2
        s = saved[i]
        df, g_post_ffn[i] = _rms_bwd(s["f"], dep(row(post_ffn_g, i), toks), dh, None, out_dtype=BF16,
                                     name=f"rms_post_ffn_bwd_{i}")
        g_down = _matmul(s["act"], df, mode="tn", out_dtype=F32, name=f"ffn_down_dw_{i}")
        dg_, du_ = _ffn_down_dx(df, W[("ffn_w_down", i)][None], 0, s["g_pre"], s["u_pre"], name=f"ffn_down_dx_{i}")
        hid = dg_.shape[1]
        tile = _pick(hid, (1408, 768, 512, 256, 128))
        w_gu = W[("ffn_w_gu", i)]
        g_gu = lax.empty(w_gu.shape, F32)
        g_gu = _matmul(s["fn"], dg_, mode="tn", into=g_gu, tq=tile, out_dtype=F32, name=f"ffn_g_dw_{i}")
        g_gu = _matmul(s["fn"], du_, mode="tn", into=g_gu, tq=tile, q_off=hid // tile, out_dtype=F32,
                       name=f"ffn_u_dw_{i}")
        dfn_g = _matmul(dg_, w_gu, mode="nt", tr=tile, out_dtype=F32, name=f"ffn_g_dx_{i}")
        dfn = _matmul(du_, w_gu, mode="nt", tr=tile, b_r_off=hid // tile, bias=dfn_g, out_dtype=F32,
                      name=f"ffn_u_dx_{i}")
        toks = advance(dfn)
        if i == 0:
            gen = reduce_group("0f", layer_keys(0)[2:], [g_gu, g_down])
            toks.append(next(gen))
            reducing.append(gen)
        dh1, g_pre_ffn[i] = _rms_bwd(s["h1"], dep(row(pre_ffn_g, i), toks), dfn, dh, out_dtype=F32,
                                     name=f"rms_pre_ffn_bwd_{i}")
        dmix, g_post_mix[i] = _rms_bwd(s["mix"], row(post_mix_g, i), dh1, None, out_dtype=BF16,
                                       name=f"rms_post_mix_bwd_{i}")
        if i % 2 == 0:
            g_out = _matmul(s["gated"], dmix, mode="tn", out_dtype=F32, name=f"gmlp_out_dw_{i}")
            dgated = _matmul(dmix, W[("a_w_out", j)], mode="nt", out_dtype=F32, name=f"gmlp_out_dx_{i}")
            if i == 0:
                advance(dgated)
            dpre, dws, dbsT, dlng, dlnb, dbin = _sgu_bwd(s["pre"], dgated, row(a_ln_g, j), row(a_ln_b, j),
                                                         a_w_s[j], a_b_s[j].T, name=f"sgu_bwd_{i}")
            small[("a_w_s", j)] = dws
            small[("a_b_s", j)] = dbsT.T
            small[("a_ln_g", j)] = dlng
            small[("a_ln_b", j)] = dlnb
            small[("a_b_in", j)] = dbin
            g_in = _matmul(s["hn"], dpre, mode="tn", out_dtype=F32, name=f"gmlp_in_dw_{i}")
            dhn = _matmul(dpre, W[("a_w_in", j)], mode="nt", out_dtype=F32, name=f"gmlp_in_dx_{i}")
        else:
            g_out = _matmul(s["o"], dmix, mode="tn", out_dtype=F32, name=f"attn_o_dw_{i}")
            do = _matmul(dmix, W[("b_w_o", j)], mode="nt", out_dtype=BF16, name=f"attn_o_dx_{i}")
            dq, dkp, dkc, dvp, dvc, dsk = _attn_bwd(s["qr"], s["kr"], s["vr"], row(b_sinks, j), do,
                                                    name=f"attn_bwd_{i}")
            dqkv, dbq = _rope_bwd(dq, dkp, dkc, dvp, dvc, ctab, stab, name=f"rope_bwd_{i}")
            small[("b_sinks", j)] = dsk[:, :b_sinks.shape[1]]
            small[("b_b_qkv", j)] = dbq
            g_in = _matmul(s["hn"], dqkv, mode="tn", out_dtype=F32, name=f"attn_qkv_dw_{i}")
            dhn = _matmul(dqkv, W[("b_w_qkv", j)], mode="nt", out_dtype=F32, name=f"attn_qkv_dx_{i}")
        toks = advance(dhn)
        dh, g_pre_mix[i] = _rms_bwd(s["h"], dep(row(pre_mix_g, i), toks), dhn, dh1, out_dtype=F32,
                                    name=f"rms_pre_mix_bwd_{i}")
        if i == 0:
            gen = reduce_group("0m", layer_keys(0)[:2], [g_in, g_out])
        else:
            gen = reduce_group(str(i), layer_keys(i), [g_in, g_out, g_gu, g_down])
        toks = [next(gen)] + advance(dh)
        reducing.append(gen)
    grad_x = dh[None]
    while reducing:
        advance(dh)

    n_a, n_b = a_b_in.shape[0], b_sinks.shape[0]
    stack = lambda key, n: jnp.concatenate([small[(key, j)] for j in range(n)], axis=0)
    small_parts = [
        jnp.concatenate(g_pre_mix, axis=0), jnp.concatenate(g_post_mix, axis=0),
        jnp.concatenate(g_pre_ffn, axis=0), jnp.concatenate(g_post_ffn, axis=0),
        stack("a_b_in", n_a), stack("a_ln_g", n_a), stack("a_ln_b", n_a),
        jnp.stack([small[("a_w_s", j)] for j in range(n_a)]), jnp.stack([small[("a_b_s", j)] for j in range(n_a)]),
        stack("b_b_qkv", n_b), stack("b_sinks", n_b),
    ]
    packed, metas = _pack_rows(small_parts)
    red = _unpack_rows(_allreduce_small(packed), metas)
    (gr_pre_mix, gr_post_mix, gr_pre_ffn, gr_post_ffn, gr_b_in, gr_ln_g, gr_ln_b, gr_w_s, gr_b_s,
     gr_b_qkv_full, gr_sinks) = red
    gr_b_qkv = lax.dynamic_slice(gr_b_qkv_full, (0, chip * nq), (gr_b_qkv_full.shape[0], nq))

    grads = {"pre_mix_g": gr_pre_mix, "post_mix_g": gr_post_mix, "pre_ffn_g": gr_pre_ffn, "post_ffn_g": gr_post_ffn,
             "a_b_in": gr_b_in, "a_ln_g": gr_ln_g, "a_ln_b": gr_ln_b, "a_w_s": gr_w_s, "a_b_s": gr_b_s,
             "b_b_qkv": gr_b_qkv, "b_sinks": gr_sinks}
    weights = {"pre_mix_g": (pre_mix_g, m_pre_mix_g, v_pre_mix_g), "post_mix_g": (post_mix_g, m_post_mix_g, v_post_mix_g),
               "pre_ffn_g": (pre_ffn_g, m_pre_ffn_g, v_pre_ffn_g), "post_ffn_g": (post_ffn_g, m_post_ffn_g, v_post_ffn_g),
               "a_b_in": (a_b_in, m_a_b_in, v_a_b_in), "a_ln_g": (a_ln_g, m_a_ln_g, v_a_ln_g),
               "a_ln_b": (a_ln_b, m_a_ln_b, v_a_ln_b), "a_w_s": (a_w_s, m_a_w_s, v_a_w_s), "a_b_s": (a_b_s, m_a_b_s, v_a_b_s),
               "b_b_qkv": (b_b_qkv, m_b_b_qkv, v_b_b_qkv), "b_sinks": (b_sinks, m_b_sinks, v_b_sinks)}
    order = ["pre_mix_g", "post_mix_g", "pre_ffn_g", "post_ffn_g", "a_w_in", "a_b_in", "a_ln_g", "a_ln_b", "a_w_s",
             "a_b_s", "a_w_out", "b_w_qkv", "b_b_qkv", "b_sinks", "b_w_o", "ffn_w_gu", "ffn_w_down"]
    deltas, new_m, new_v = {}, {}, {}
    for nm in order:
        if nm in big_out:
            grads[nm], deltas[nm], new_m[nm], new_v[nm] = big_out[nm]
        else:
            w, m, v = weights[nm]
            deltas[nm], new_m[nm], new_v[nm] = _adamw_small(w, grads[nm], m, v, name="adamw_" + nm)
    return (loss, grad_x, *[grads[nm] for nm in order], *[deltas[nm] for nm in order],
            *[new_m[nm] for nm in order], *[new_v[nm] for nm in order])
```

```python
import functools
import math

import jax
import jax.numpy as jnp
import numpy as np
from jax import lax
from jax.experimental import pallas as pl
from jax.experimental.pallas import tpu as pltpu

F32 = jnp.float32
BF16 = jnp.bfloat16
MESH = pl.DeviceIdType.MESH

HEAD_DIM = 64
N_KV_HEADS = 4
ROPE_DIM = 16
ROPE_THETA = 500000.0
CHUNK = 128
GMLP_GROUPS = 8
RMS_EPS = 1e-6
LN_EPS = 1e-5
NEG_INF = -1e30
ADAM_LR = 0.001
ADAM_B1 = 0.9
ADAM_B2 = 0.999
ADAM_EPS = 1e-08
ADAM_WD = 0.01
ADAM_STEP = 10

N_CHIPS = 4
LANES = 128
VMEM_CAP = 58 * 1024 * 1024


def _vmem(est_bytes):
    assert est_bytes < VMEM_CAP
    return VMEM_CAP


def _pick(n, cands):
    for c in cands:
        if c <= n and n % c == 0:
            return c
    return n


def _nbytes(shape, dtype):
    return int(np.prod(shape)) * jnp.dtype(dtype).itemsize


def _matmul(a, b, *, mode, out_dtype, name, a_l=None, b_l=None, bias=None, into=None, o_l=None,
            q_off=0, b_r_off=0, tp=None, tq=None, tr=None):
    a2 = a.shape[-2:]
    b2 = b.shape[-2:]
    if mode == "nn":
        (P, R), (R2, Q) = a2, b2
    elif mode == "nt":
        (P, R), (Q, R2) = a2, b2
    else:
        (R, P), (R2, Q) = a2, b2
    assert R == R2 or (mode == "nt" and R2 % R == 0), (mode, a.shape, b.shape)
    tp = tp or _pick(P, (1024, 1408, 512, 384, 256, 128))
    tq = tq or _pick(Q, (1024, 1408, 768, 512, 384, 256, 128))
    tr = tr or _pick(R, (2048, 1408, 1024, 512, 256, 128))
    assert P % tp == 0 and Q % tq == 0 and R % tr == 0
    nk = R // tr
    dims = {"nn": (((1,), (0,)), ((), ())), "nt": (((1,), (1,)), ((), ())), "tn": (((0,), (0,)), ((), ()))}[mode]

    def lead(l, blk, idx):
        if l is None:
            return pl.BlockSpec(blk, idx)
        return pl.BlockSpec((None,) + blk, lambda i, j, k: (l,) + idx(i, j, k))

    if mode == "nn":
        a_spec = lead(a_l, (tp, tr), lambda i, j, k: (i, k))
        b_spec = lead(b_l, (tr, tq), lambda i, j, k: (k, j))
    elif mode == "nt":
        a_spec = lead(a_l, (tp, tr), lambda i, j, k: (i, k))
        b_spec = lead(b_l, (tq, tr), lambda i, j, k: (j, k + b_r_off))
    else:
        a_spec = lead(a_l, (tr, tp), lambda i, j, k: (k, i))
        b_spec = lead(b_l, (tr, tq), lambda i, j, k: (k, j))
    in_specs = [a_spec, b_spec]
    args = [a, b]
    if bias is not None:
        if bias.shape[0] == 1:
            in_specs.append(pl.BlockSpec((1, tq), lambda i, j, k: (0, j)))
        else:
            in_specs.append(pl.BlockSpec((tp, tq), lambda i, j, k: (i, j)))
        args.append(bias)
    aliases = {}
    if into is not None:
        in_specs.append(pl.BlockSpec(memory_space=pl.ANY))
        args.append(into)
        aliases = {len(args) - 1: 0}
        out_shape = jax.ShapeDtypeStruct(into.shape, into.dtype)
        out_dtype = into.dtype
        if o_l is None:
            out_spec = pl.BlockSpec((tp, tq), lambda i, j, k: (i, j + q_off))
        else:
            out_spec = pl.BlockSpec((None, tp, tq), lambda i, j, k: (o_l, i, j + q_off))
    else:
        out_shape = jax.ShapeDtypeStruct((P, Q), out_dtype)
        out_spec = pl.BlockSpec((tp, tq), lambda i, j, k: (i, j))
    has_bias = bias is not None
    has_into = into is not None

    def body(*refs):
        a_ref, b_ref = refs[0], refs[1]
        pos = 2
        bias_ref = None
        if has_bias:
            bias_ref = refs[pos]
            pos += 1
        if has_into:
            pos += 1
        o_ref = refs[pos]
        acc_ref = refs[pos + 1] if nk > 1 else None
        part = lax.dot_general(a_ref[...], b_ref[...], dims, preferred_element_type=F32)

        def finish(acc):
            if has_bias:
                acc = acc + bias_ref[...]
            o_ref[...] = acc.astype(out_dtype)

        if nk == 1:
            finish(part)
        else:
            k = pl.program_id(2)

            @pl.when(k == 0)
            def _():
                acc_ref[...] = part

            @pl.when(k > 0)
            def _():
                acc_ref[...] += part

            @pl.when(k == nk - 1)
            def _():
                finish(acc_ref[...])

    est = 2 * (_nbytes((tp, tr), a.dtype) + _nbytes((tr, tq), b.dtype) + _nbytes((tp, tq), out_dtype)) + 3 * tp * tq * 4
    return pl.pallas_call(
        body, name=name, out_shape=out_shape,
        grid=(P // tp, Q // tq, nk),
        in_specs=in_specs, out_specs=out_spec,
        scratch_shapes=[pltpu.VMEM((tp, tq), F32)] if nk > 1 else [],
        input_output_aliases=aliases,
        compiler_params=pltpu.CompilerParams(
            dimension_semantics=("parallel", "parallel", "arbitrary"), vmem_limit_bytes=_vmem(est)),
    )(*args)


def _row_call(body, ins, outs, *, name, rows, tr, acc_outs=(), est=0):
    in_specs = []
    for arr, kind in ins:
        if kind == "row":
            in_specs.append(pl.BlockSpec((tr, arr.shape[1]), lambda i: (i, 0)))
        else:
            nd = arr.ndim
            in_specs.append(pl.BlockSpec(arr.shape, lambda i, nd=nd: (0,) * nd))
    out_shapes = [jax.ShapeDtypeStruct(s, d) for s, d in outs] + [jax.ShapeDtypeStruct(s, d) for s, d in acc_outs]
    out_specs = [pl.BlockSpec((tr, s[1]), lambda i: (i, 0)) for s, _ in outs]
    out_specs += [pl.BlockSpec(s, lambda i, nd=len(s): (0,) * nd) for s, _ in acc_outs]
    res = pl.pallas_call(
        body, name=name, out_shape=out_shapes, grid=(rows // tr,), in_specs=in_specs, out_specs=out_specs,
        compiler_params=pltpu.CompilerParams(dimension_semantics=("arbitrary",), vmem_limit_bytes=_vmem(est)),
    )(*[a for a, _ in ins])
    return res


def _rms_fwd(x, g, *, out_dtype, name):
    T, D = x.shape
    tr = _pick(T, (512, 256, 128))

    def body(x_ref, g_ref, o_ref):
        xv = x_ref[...]
        r = lax.rsqrt(jnp.mean(xv * xv, axis=-1, keepdims=True) + RMS_EPS)
        o_ref[...] = (xv * r * g_ref[...]).astype(out_dtype)

    return _row_call(body, [(x, "row"), (g, "full")], [((T, D), out_dtype)], name=name, rows=T, tr=tr,
                     est=8 * tr * D * 4)[0]


def _rms_res(h, y, g, *, name):
    T, D = h.shape
    tr = _pick(T, (512, 256, 128))

    def body(h_ref, y_ref, g_ref, o_ref):
        yv = y_ref[...]
        r = lax.rsqrt(jnp.mean(yv * yv, axis=-1, keepdims=True) + RMS_EPS)
        o_ref[...] = h_ref[...] + yv * r * g_ref[...]

    return _row_call(body, [(h, "row"), (y, "row"), (g, "full")], [((T, D), F32)], name=name, rows=T, tr=tr,
                     est=10 * tr * D * 4)[0]


def _rms_bwd(x, g, dy, dres, *, out_dtype, name):
    T, D = x.shape
    tr = _pick(T, (512, 256, 128))
    has_res = dres is not None

    def body(*refs):
        if has_res:
            x_ref, g_ref, dy_ref, dr_ref, dx_ref, dg_ref = refs
        else:
            x_ref, g_ref, dy_ref, dx_ref, dg_ref = refs
        xv = x_ref[...]
        r = lax.rsqrt(jnp.mean(xv * xv, axis=-1, keepdims=True) + RMS_EPS)
        xhat = xv * r
        dyv = dy_ref[...].astype(F32)
        dxn = dyv * g_ref[...]
        dx = r * (dxn - xhat * jnp.mean(dxn * xhat, axis=-1, keepdims=True))
        if has_res:
            dx = dx + dr_ref[...]
        dx_ref[...] = dx.astype(out_dtype)
        part = jnp.sum(dyv * xhat, axis=0, keepdims=True)

        @pl.when(pl.program_id(0) == 0)
        def _():
            dg_ref[...] = part

        @pl.when(pl.program_id(0) > 0)
        def _():
            dg_ref[...] += part

    ins = [(x, "row"), (g, "full"), (dy, "row")] + ([(dres, "row")] if has_res else [])
    dx, dg = _row_call(body, ins, [((T, D), out_dtype)], name=name, rows=T, tr=tr, acc_outs=[((1, D), F32)],
                       est=12 * tr * D * 4)
    return dx, dg


def _ffn_up(fn, w_gu, l, *, name):
    T, D = fn.shape
    H = w_gu.shape[2] // 2
    tp = _pick(T, (512, 256, 128))
    tq = _pick(H, (1408, 768, 512, 256, 128))
    nj = H // tq

    def body(a_ref, wg_ref, wu_ref, g_ref, u_ref, act_ref):
        a = a_ref[...]
        g = jnp.dot(a, wg_ref[...], preferred_element_type=F32)
        u = jnp.dot(a, wu_ref[...], preferred_element_type=F32)
        g_ref[...] = g
        u_ref[...] = u
        act_ref[...] = (g * jax.nn.sigmoid(g) * u).astype(BF16)

    tile = pl.BlockSpec((tp, tq), lambda j, i: (i, j))
    est = 2 * (tp * D * 2 + 2 * D * tq * 2 + 2 * tp * tq * 4 + tp * tq * 2) + 4 * tp * tq * 4
    return pl.pallas_call(
        body, name=name,
        out_shape=[jax.ShapeDtypeStruct((T, H), F32), jax.ShapeDtypeStruct((T, H), F32),
                   jax.ShapeDtypeStruct((T, H), BF16)],
        grid=(nj, T // tp),
        in_specs=[pl.BlockSpec((tp, D), lambda j, i: (i, 0)),
                  pl.BlockSpec((None, D, tq), lambda j, i: (l, 0, j)),
                  pl.BlockSpec((None, D, tq), lambda j, i: (l, 0, j + nj))],
        out_specs=[tile, tile, tile],
        compiler_params=pltpu.CompilerParams(dimension_semantics=("parallel", "parallel"),
                                             vmem_limit_bytes=_vmem(est)),
    )(fn, w_gu, w_gu)


def _ffn_down_dx(df, w_down, l, g, u, *, name):
    T, D = df.shape
    H = w_down.shape[1]
    tp = _pick(T, (512, 256, 128))
    tq = _pick(H, (1408, 768, 512, 256, 128))

    def body(a_ref, w_ref, g_ref, u_ref, dg_ref, du_ref):
        da = lax.dot_general(a_ref[...], w_ref[...], (((1,), (1,)), ((), ())), preferred_element_type=F32)
        gv = g_ref[...]
        sg = jax.nn.sigmoid(gv)
        silu = gv * sg
        dg_ref[...] = (da * u_ref[...] * (sg + silu * (1.0 - sg))).astype(BF16)
        du_ref[...] = (da * silu).astype(BF16)

    tile = pl.BlockSpec((tp, tq), lambda j, i: (i, j))
    est = 2 * (tp * D * 2 + tq * D * 2 + 2 * tp * tq * 4 + 2 * tp * tq * 2) + 5 * tp * tq * 4
    return pl.pallas_call(
        body, name=name,
        out_shape=[jax.ShapeDtypeStruct((T, H), BF16), jax.ShapeDtypeStruct((T, H), BF16)],
        grid=(H // tq, T // tp),
        in_specs=[pl.BlockSpec((tp, D), lambda j, i: (i, 0)),
                  pl.BlockSpec((None, tq, D), lambda j, i: (l, j, 0)), tile, tile],
        out_specs=[tile, tile],
        compiler_params=pltpu.CompilerParams(dimension_semantics=("parallel", "parallel"),
                                             vmem_limit_bytes=_vmem(est)),
    )(df, w_down, g, u)


def _loss_and_grad(y, target, *, name):
    T, D = y.shape
    tr = _pick(T, (512, 256, 128))

    def body(y_ref, t_ref, dy_ref, l_ref):
        e = y_ref[...] - t_ref[...]
        dy_ref[...] = e * (1.0 / D)
        part = jnp.sum(jnp.sum(e * e, axis=1, keepdims=True), axis=0, keepdims=True) * (0.5 / D)

        @pl.when(pl.program_id(0) == 0)
        def _():
            l_ref[...] = part

        @pl.when(pl.program_id(0) > 0)
        def _():
            l_ref[...] += part

    dy, l = _row_call(body, [(y, "row"), (target, "row")], [((T, D), F32)], name=name, rows=T, tr=tr,
                      acc_outs=[((1, 1), F32)], est=8 * tr * D * 4)
    return dy, l


_SQRT_HALF = 0.7071067811865476
_INV_SQRT_2PI = 0.3989422804014327


def _gelu_parts(x):
    cdf = 0.5 * (1.0 + lax.erf(x * _SQRT_HALF))
    return cdf


def _sgu_common(pre, lng, lnb, W):
    cdf = _gelu_parts(pre)
    z = pre * cdf
    u = z[:, :W]
    v = z[:, W:]
    mu = jnp.mean(v, axis=-1, keepdims=True)
    vc = v - mu
    var = jnp.mean(vc * vc, axis=-1, keepdims=True)
    rstd = lax.rsqrt(var + LN_EPS)
    vhat = vc * rstd
    vn = vhat * lng + lnb
    return cdf, u, vhat, rstd, vn


def _causal_mask():
    t = lax.broadcasted_iota(jnp.int32, (CHUNK, CHUNK), 0)
    s = lax.broadcasted_iota(jnp.int32, (CHUNK, CHUNK), 1)
    return t >= s


def _sgu_fwd(pre, lng, lnb, ws, bsT, *, name):
    T, W2 = pre.shape
    W = W2 // 2
    G = ws.shape[0]
    gd = W // G

    def body(pre_ref, lng_ref, lnb_ref, ws_ref, bs_ref, o_ref):
        _, u, _, _, vn = _sgu_common(pre_ref[...], lng_ref[...], lnb_ref[...], W)
        vnb = vn.astype(BF16)
        causal = _causal_mask()
        for g in range(G):
            w = jnp.where(causal, ws_ref[g], 0.0).astype(BF16)
            sv = jnp.dot(w, vnb[:, g * gd:(g + 1) * gd], preferred_element_type=F32) + bs_ref[:, g:g + 1]
            o_ref[:, g * gd:(g + 1) * gd] = (u[:, g * gd:(g + 1) * gd] * sv).astype(BF16)

    return pl.pallas_call(
        body, name=name, out_shape=jax.ShapeDtypeStruct((T, W), BF16), grid=(T // CHUNK,),
        in_specs=[pl.BlockSpec((CHUNK, W2), lambda i: (i, 0)),
                  pl.BlockSpec((1, W), lambda i: (0, 0)), pl.BlockSpec((1, W), lambda i: (0, 0)),
                  pl.BlockSpec(ws.shape, lambda i: (0, 0, 0)), pl.BlockSpec(bsT.shape, lambda i: (0, 0))],
        out_specs=pl.BlockSpec((CHUNK, W), lambda i: (i, 0)),
        compiler_params=pltpu.CompilerParams(dimension_semantics=("arbitrary",),
                                             vmem_limit_bytes=_vmem(12 * CHUNK * W2 * 4)),
    )(pre, lng, lnb, ws, bsT)


def _sgu_bwd(pre, dgated, lng, lnb, ws, bsT, *, name):
    T, W2 = pre.shape
    W = W2 // 2
    G = ws.shape[0]
    gd = W // G

    def body(pre_ref, dgt_ref, lng_ref, lnb_ref, ws_ref, bs_ref,
             dpre_ref, dws_ref, dbs_ref, dlng_ref, dlnb_ref, dbin_ref):
        first = pl.program_id(0) == 0

        @pl.when(first)
        def _():
            dws_ref[...] = jnp.zeros_like(dws_ref)
            dbs_ref[...] = jnp.zeros_like(dbs_ref)
            dlng_ref[...] = jnp.zeros_like(dlng_ref)
            dlnb_ref[...] = jnp.zeros_like(dlnb_ref)
            dbin_ref[...] = jnp.zeros_like(dbin_ref)

        pre_v = pre_ref[...]
        lng_v = lng_ref[...]
        cdf, u, vhat, rstd, vn = _sgu_common(pre_v, lng_v, lnb_ref[...], W)
        vnb = vn.astype(BF16)
        dgt = dgt_ref[...].astype(F32)
        causal = _causal_mask()
        du_parts, dvn_parts = [], []
        for g in range(G):
            sl = slice(g * gd, (g + 1) * gd)
            w = jnp.where(causal, ws_ref[g], 0.0).astype(BF16)
            sv = jnp.dot(w, vnb[:, sl], preferred_element_type=F32) + bs_ref[:, g:g + 1]
            dgt_g = dgt[:, sl]
            du_parts.append(dgt_g * sv)
            dsv = dgt_g * u[:, sl]
            dsvb = dsv.astype(BF16)
            dvn_parts.append(lax.dot_general(w, dsvb, (((0,), (0,)), ((), ())), preferred_element_type=F32))
            dw = lax.dot_general(dsvb, vnb[:, sl], (((1,), (1,)), ((), ())), preferred_element_type=F32)
            dws_ref[g] += jnp.where(causal, dw, 0.0)
            dbs_ref[:, g:g + 1] += jnp.sum(dsv, axis=1, keepdims=True)
        du = jnp.concatenate(du_parts, axis=1)
        dvn = jnp.concatenate(dvn_parts, axis=1)
        dlng_ref[...] += jnp.sum(dvn * vhat, axis=0, keepdims=True)
        dlnb_ref[...] += jnp.sum(dvn, axis=0, keepdims=True)
        dvh = dvn * lng_v
        dv = rstd * (dvh - jnp.mean(dvh, axis=-1, keepdims=True)
                     - vhat * jnp.mean(dvh * vhat, axis=-1, keepdims=True))
        dz = jnp.concatenate([du, dv], axis=1)
        dgelu = cdf + pre_v * jnp.exp(-0.5 * pre_v * pre_v) * _INV_SQRT_2PI
        dpre = dz * dgelu
        dbin_ref[...] += jnp.sum(dpre, axis=0, keepdims=True)
        dpre_ref[...] = dpre.astype(BF16)

    full = lambda shape: pl.BlockSpec(shape, lambda i, nd=len(shape): (0,) * nd)
    return pl.pallas_call(
        body, name=name,
        out_shape=[jax.ShapeDtypeStruct((T, W2), BF16), jax.ShapeDtypeStruct(ws.shape, F32),
                   jax.ShapeDtypeStruct(bsT.shape, F32), jax.ShapeDtypeStruct((1, W), F32),
                   jax.ShapeDtypeStruct((1, W), F32), jax.ShapeDtypeStruct((1, W2), F32)],
        grid=(T // CHUNK,),
        in_specs=[pl.BlockSpec((CHUNK, W2), lambda i: (i, 0)), pl.BlockSpec((CHUNK, W), lambda i: (i, 0)),
                  full((1, W)), full((1, W)), full(ws.shape), full(bsT.shape)],
        out_specs=[pl.BlockSpec((CHUNK, W2), lambda i: (i, 0)), full(ws.shape), full(bsT.shape),
                   full((1, W)), full((1, W)), full((1, W2))],
        compiler_params=pltpu.CompilerParams(dimension_semantics=("arbitrary",),
                                             vmem_limit_bytes=_vmem(24 * CHUNK * W2 * 4)),
    )(pre, dgated, lng, lnb, ws, bsT)


def _rope_tables(positions):
    half = ROPE_DIM // 2
    inv_freq = ROPE_THETA ** (-jnp.arange(0, ROPE_DIM, 2, dtype=F32) / ROPE_DIM)
    ang = positions.astype(F32).reshape(-1, 1) * inv_freq
    cos, sin = jnp.cos(ang), jnp.sin(ang)
    T = ang.shape[0]
    rest = HEAD_DIM - ROPE_DIM
    c64 = jnp.concatenate([cos, cos, jnp.ones((T, rest), F32)], axis=1)
    s64 = jnp.concatenate([-sin, sin, jnp.zeros((T, rest), F32)], axis=1)
    del half
    return jnp.tile(c64, (1, LANES // HEAD_DIM)), jnp.tile(s64, (1, LANES // HEAD_DIM))


def _swap8(x):
    W = x.shape[1]
    half = ROPE_DIM // 2
    lane = lax.broadcasted_iota(jnp.int32, x.shape, 1) % HEAD_DIM
    return jnp.where(lane < half, pltpu.roll(x, W - half, axis=1),
                     jnp.where(lane < ROPE_DIM, pltpu.roll(x, half, axis=1), 0.0))


def _wide(tab, W):
    return jnp.concatenate([tab] * (W // LANES), axis=1) if W > LANES else tab


def _rope_fwd(qkv, ctab, stab, *, q_width, kv_width, name):
    T = qkv.shape[0]
    tr = _pick(T, (256, 128))
    scale = HEAD_DIM ** -0.5

    def body(x_ref, c_ref, s_ref, q_ref, k_ref, v_ref):
        c = c_ref[...]
        s = s_ref[...]
        q = x_ref[:, :q_width]
        k = x_ref[:, q_width:q_width + kv_width]
        q_ref[...] = ((q * _wide(c, q_width) + _swap8(q) * _wide(s, q_width)) * scale).astype(BF16)
        k_ref[...] = (k * _wide(c, kv_width) + _swap8(k) * _wide(s, kv_width)).astype(BF16)
        v_ref[...] = x_ref[:, q_width + kv_width:].astype(BF16)

    return _row_call(body, [(qkv, "row"), (ctab, "row"), (stab, "row")],
                     [((T, q_width), BF16), ((T, kv_width), BF16), ((T, kv_width), BF16)],
                     name=name, rows=T, tr=tr, est=10 * tr * qkv.shape[1] * 4)


_NT = (((1,), (1,)), ((), ()))
_TN = (((0,), (0,)), ((), ()))


def _group_rows(ref, heads):
    return jnp.concatenate([ref[:, h * HEAD_DIM:(h + 1) * HEAD_DIM] for h in heads], axis=0)


def _attn_valid(grp):
    qi = np.arange(grp * CHUNK)[:, None] % CHUNK
    sj = np.arange(2 * CHUNK)[None, :]
    cur = (sj >= CHUNK) & (sj - CHUNK <= qi)
    prev = (sj < CHUNK) & (sj > qi)
    return jnp.asarray(np.stack([cur, cur | prev]).astype(np.float32))


def _valid_spec(grp):
    return pl.BlockSpec((None, grp * CHUNK, 2 * CHUNK), lambda n: (jnp.minimum(n, 1), 0, 0))


def _attn_group_probs(q, kk, sinks, valid, grp):
    rows = grp * CHUNK
    s = lax.dot_general(q, kk, _NT, preferred_element_type=F32)
    s = jnp.where(valid, s, NEG_INF)
    r = lax.broadcasted_iota(jnp.int32, (rows, 1), 0)
    sink = jnp.full((rows, 1), sinks[grp - 1], F32)
    for g in range(grp - 2, -1, -1):
        sink = jnp.where(r < (g + 1) * CHUNK, sinks[g], sink)
    m = jnp.maximum(jnp.max(s, axis=1, keepdims=True), sink)
    p = jnp.exp(s - m)
    ps = jnp.exp(sink - m)
    inv = 1.0 / (jnp.sum(p, axis=1, keepdims=True) + ps)
    return p * inv, ps * inv


def _kv_specs(width, nb):
    prev = pl.BlockSpec((CHUNK, width), lambda n: (jnp.maximum(n - 1, 0), 0))
    cur = pl.BlockSpec((CHUNK, width), lambda n: (n, 0))
    return prev, cur


def _attn_fwd(qr, kr, vr, sinks, *, name):
    T, QW = qr.shape
    KW = kr.shape[1]
    HQ, HK = QW // HEAD_DIM, KW // HEAD_DIM
    grp = HQ // HK
    nb = T // CHUNK

    def body(q_ref, kp_ref, kc_ref, vp_ref, vc_ref, s_ref, ok_ref, o_ref):
        valid = ok_ref[...] > 0.5
        for kh in range(HK):
            ks = slice(kh * HEAD_DIM, (kh + 1) * HEAD_DIM)
            heads = list(range(kh * grp, (kh + 1) * grp))
            q = _group_rows(q_ref, heads)
            kk = jnp.concatenate([kp_ref[:, ks], kc_ref[:, ks]], axis=0)
            vv = jnp.concatenate([vp_ref[:, ks], vc_ref[:, ks]], axis=0)
            p, _ = _attn_group_probs(q, kk, [s_ref[0, h] for h in heads], valid, grp)
            o = jnp.dot(p.astype(BF16), vv, preferred_element_type=F32).astype(BF16)
            for g, h in enumerate(heads):
                o_ref[:, h * HEAD_DIM:(h + 1) * HEAD_DIM] = o[g * CHUNK:(g + 1) * CHUNK]

    kp, kc = _kv_specs(KW, nb)
    return pl.pallas_call(
        body, name=name, out_shape=jax.ShapeDtypeStruct((T, QW), BF16), grid=(nb,),
        in_specs=[pl.BlockSpec((CHUNK, QW), lambda n: (n, 0)), kp, kc, kp, kc,
                  pl.BlockSpec(memory_space=pltpu.SMEM), _valid_spec(grp)],
        out_specs=pl.BlockSpec((CHUNK, QW), lambda n: (n, 0)),
        compiler_params=pltpu.CompilerParams(dimension_semantics=("arbitrary",), vmem_limit_bytes=_vmem(8 << 20)),
    )(qr, kr, kr, vr, vr, sinks, _attn_valid(grp))


def _attn_bwd(qr, kr, vr, sinks, do, *, name):
    T, QW = qr.shape
    KW = kr.shape[1]
    HQ, HK = QW // HEAD_DIM, KW // HEAD_DIM
    grp = HQ // HK
    nb = T // CHUNK

    def body(q_ref, kp_ref, kc_ref, vp_ref, vc_ref, s_ref, do_ref, ok_ref,
             dq_ref, dkp_ref, dkc_ref, dvp_ref, dvc_ref, ds_ref):
        n = pl.program_id(0)
        valid = ok_ref[...] > 0.5
        lane = lax.broadcasted_iota(jnp.int32, (1, LANES), 1)
        dsink = jnp.zeros((1, LANES), F32)
        for kh in range(HK):
            ks = slice(kh * HEAD_DIM, (kh + 1) * HEAD_DIM)
            heads = list(range(kh * grp, (kh + 1) * grp))
            q = _group_rows(q_ref, heads)
            doh = _group_rows(do_ref, heads)
            kk = jnp.concatenate([kp_ref[:, ks], kc_ref[:, ks]], axis=0)
            vv = jnp.concatenate([vp_ref[:, ks], vc_ref[:, ks]], axis=0)
            p, ps = _attn_group_probs(q, kk, [s_ref[0, h] for h in heads], valid, grp)
            dp = lax.dot_general(doh, vv, _NT, preferred_element_type=F32)
            delta = jnp.sum(p * dp, axis=1, keepdims=True)
            ds = (p * (dp - delta)).astype(BF16)
            dv = lax.dot_general(p.astype(BF16), doh, _TN, preferred_element_type=F32)
            dk = lax.dot_general(ds, q, _TN, preferred_element_type=F32)
            dq = jnp.dot(ds, kk, preferred_element_type=F32)
            psd = ps * delta
            for g, h in enumerate(heads):
                dq_ref[:, h * HEAD_DIM:(h + 1) * HEAD_DIM] = dq[g * CHUNK:(g + 1) * CHUNK]
                dsink = dsink + jnp.where(
                    lane == h, -jnp.sum(psd[g * CHUNK:(g + 1) * CHUNK], axis=0, keepdims=True), 0.0)
            dkp_ref[:, ks] = dk[:CHUNK]
            dkc_ref[:, ks] = dk[CHUNK:]
            dvp_ref[:, ks] = dv[:CHUNK]
            dvc_ref[:, ks] = dv[CHUNK:]

        @pl.when(n == 0)
        def _():
            ds_ref[...] = dsink

        @pl.when(n > 0)
        def _():
            ds_ref[...] += dsink

    kp, kc = _kv_specs(KW, nb)
    qspec = pl.BlockSpec((CHUNK, QW), lambda n: (n, 0))
    kout = pl.BlockSpec((CHUNK, KW), lambda n: (n, 0))
    return pl.pallas_call(
        body, name=name,
        out_shape=[jax.ShapeDtypeStruct((T, QW), F32)] + [jax.ShapeDtypeStruct((T, KW), F32)] * 4
        + [jax.ShapeDtypeStruct((1, LANES), F32)],
        grid=(nb,),
        in_specs=[qspec, kp, kc, kp, kc, pl.BlockSpec(memory_space=pltpu.SMEM), qspec, _valid_spec(grp)],
        out_specs=[qspec, kout, kout, kout, kout, pl.BlockSpec((1, LANES), lambda n: (0, 0))],
        compiler_params=pltpu.CompilerParams(dimension_semantics=("arbitrary",), vmem_limit_bytes=_vmem(12 << 20)),
    )(qr, kr, kr, vr, vr, sinks, do, _attn_valid(grp))


def _rope_bwd(dq, dkp, dkc, dvp, dvc, ctab, stab, *, name):
    T, QW = dq.shape
    KW = dkp.shape[1]
    nb = T // CHUNK
    scale = HEAD_DIM ** -0.5
    width = QW + 2 * KW

    def body(dq_ref, dkc_ref, dkn_ref, dvc_ref, dvn_ref, c_ref, s_ref, o_ref, db_ref):
        n = pl.program_id(0)
        c = c_ref[...]
        s = s_ref[...]
        has_next = (n < nb - 1).astype(F32)
        dqv = dq_ref[...]
        dk = dkc_ref[...] + has_next * dkn_ref[...]
        dv = dvc_ref[...] + has_next * dvn_ref[...]
        dq_pre = (dqv * _wide(c, QW) + _swap8(dqv * _wide(s, QW))) * scale
        dk_pre = dk * _wide(c, KW) + _swap8(dk * _wide(s, KW))
        o_ref[:, :QW] = dq_pre.astype(BF16)
        o_ref[:, QW:QW + KW] = dk_pre.astype(BF16)
        o_ref[:, QW + KW:] = dv.astype(BF16)
        part = jnp.concatenate([jnp.sum(dq_pre, axis=0, keepdims=True), jnp.sum(dk_pre, axis=0, keepdims=True),
                                jnp.sum(dv, axis=0, keepdims=True)], axis=1)

        @pl.when(n == 0)
        def _():
            db_ref[...] = part

        @pl.when(n > 0)
        def _():
            db_ref[...] += part

    cur = lambda w: pl.BlockSpec((CHUNK, w), lambda n: (n, 0))
    nxt = lambda w: pl.BlockSpec((CHUNK, w), lambda n: (jnp.minimum(n + 1, nb - 1), 0))
    return pl.pallas_call(
        body, name=name,
        out_shape=[jax.ShapeDtypeStruct((T, width), BF16), jax.ShapeDtypeStruct((1, width), F32)],
        grid=(nb,),
        in_specs=[cur(QW), cur(KW), nxt(KW), cur(KW), nxt(KW), cur(LANES), cur(LANES)],
        out_specs=[cur(width), pl.BlockSpec((1, width), lambda n: (0, 0))],
        compiler_params=pltpu.CompilerParams(dimension_semantics=("arbitrary",), vmem_limit_bytes=_vmem(8 << 20)),
    )(dq, dkc, dkp, dvc, dvp, ctab, stab)


def _cast_block(w, l, axis, chip_arr, *, name):
    _, Ks, Ns = w.shape
    tk = _pick(Ks, (512, 352, 256, 128))
    nk = Ks // tk
    full = (Ks * N_CHIPS, Ns) if axis == 0 else (Ks, Ns * N_CHIPS)

    def body(p_ref, w_ref, o_ref):
        o_ref[...] = w_ref[...].astype(BF16)

    if axis == 0:
        out_spec = pl.BlockSpec((tk, Ns), lambda i, p: (p[0] * nk + i, 0))
    else:
        out_spec = pl.BlockSpec((tk, Ns), lambda i, p: (i, p[0]))
    grid_spec = pltpu.PrefetchScalarGridSpec(
        num_scalar_prefetch=1, grid=(nk,),
        in_specs=[pl.BlockSpec((None, tk, Ns), lambda i, p: (l, i, 0))], out_specs=out_spec)
    return pl.pallas_call(
        body, name=name, out_shape=jax.ShapeDtypeStruct(full, BF16), grid_spec=grid_spec,
        compiler_params=pltpu.CompilerParams(dimension_semantics=("arbitrary",),
                                             vmem_limit_bytes=_vmem(4 * tk * Ns * 6)),
    )(chip_arr, w)


def _adamw_math(w, g, m, v):
    m = ADAM_B1 * m + (1.0 - ADAM_B1) * g
    v = ADAM_B2 * v + (1.0 - ADAM_B2) * (g * g)
    m_hat = m / (1.0 - ADAM_B1 ** ADAM_STEP)
    v_hat = v / (1.0 - ADAM_B2 ** ADAM_STEP)
    delta = -ADAM_LR * (m_hat / (jnp.sqrt(v_hat) + ADAM_EPS) + ADAM_WD * w)
    return delta, m, v


def _adamw_layer(w, m, v, g, l, outs, *, name):
    _, K, N = w.shape
    tk = _pick(K, (256, 176, 128))

    def body(w_ref, m_ref, v_ref, g_ref, _g, _d, _m, _v, go_ref, d_ref, mo_ref, vo_ref):
        gv = g_ref[...]
        d, mn, vn = _adamw_math(w_ref[...], gv, m_ref[...], v_ref[...])
        go_ref[...] = gv
        d_ref[...] = d
        mo_ref[...] = mn
        vo_ref[...] = vn

    layer = pl.BlockSpec((None, tk, N), lambda i: (l, i, 0))
    any_spec = pl.BlockSpec(memory_space=pl.ANY)
    sd = jax.ShapeDtypeStruct(w.shape, F32)
    return pl.pallas_call(
        body, name=name, out_shape=[sd, sd, sd, sd], grid=(K // tk,),
        in_specs=[layer, layer, layer, pl.BlockSpec((tk, N), lambda i: (i, 0))] + [any_spec] * 4,
        out_specs=[layer] * 4, input_output_aliases={4: 0, 5: 1, 6: 2, 7: 3},
        compiler_params=pltpu.CompilerParams(dimension_semantics=("arbitrary",),
                                             vmem_limit_bytes=_vmem(2 * 8 * tk * N * 4 + 6 * tk * N * 4)),
    )(w, m, v, g, *outs)


def _adamw_small(w, g, m, v, *, name):
    def body(w_ref, g_ref, m_ref, v_ref, d_ref, mo_ref, vo_ref):
        d, mn, vn = _adamw_math(w_ref[...], g_ref[...], m_ref[...], v_ref[...])
        d_ref[...] = d
        mo_ref[...] = mn
        vo_ref[...] = vn

    sd = jax.ShapeDtypeStruct(w.shape, F32)
    return pl.pallas_call(body, name=name, out_shape=[sd, sd, sd])(w, g, m, v)


def _my_place():
    return lax.axis_index("x"), lax.axis_index("y"), lax.axis_index("c")


def _peer_chips(x, y):
    return [(1 - x, y), (x, 1 - y), (1 - x, 1 - y)]


_HBM = pl.BlockSpec(memory_space=pltpu.HBM)
_SEM = pl.BlockSpec(memory_space=pltpu.SEMAPHORE)
_EFFECT = pltpu.SideEffectType.DATAFLOW_SIDE_EFFECTING


def _split_start(name, bufs, n_copies, make_copies, after):
    nb = len(bufs)

    def body(*refs):
        send_sems, recv_sems = refs[nb + 1], refs[nb + 2]
        token = refs[2 * nb + 3]
        sends, _ = make_copies(refs[:nb], send_sems, recv_sems)
        for cp in sends:
            cp.start()
        token[...] = jnp.zeros_like(token)

    res = pl.pallas_call(
        body, name=name,
        out_shape=(pltpu.SemaphoreType.DMA((n_copies,)), pltpu.SemaphoreType.DMA((n_copies,)),
                   *[pltpu.HBM(b.shape, b.dtype) for b in bufs], jax.ShapeDtypeStruct((8, LANES), F32)),
        in_specs=[_HBM] * nb + [pl.BlockSpec(memory_space=pl.ANY)],
        out_specs=(_SEM, _SEM, *[_HBM] * nb, pl.BlockSpec(memory_space=pltpu.VMEM)),
        input_output_aliases={k: 2 + k for k in range(nb)},
        compiler_params=pltpu.CompilerParams(has_side_effects=_EFFECT),
    )(*[pltpu.with_memory_space_constraint(b, pltpu.HBM) for b in bufs], after)
    return res[0], res[1], list(res[2:2 + nb]), res[2 + nb]


def _split_wait(name, bufs, sems, make_copies, after):
    nb = len(bufs)

    def body(*refs):
        send_sems, recv_sems = refs[nb], refs[nb + 1]
        sends, recvs = make_copies(refs[:nb], send_sems, recv_sems)
        for cp in sends:
            cp.wait_send()
        for cp in recvs:
            cp.wait_recv()

    res = pl.pallas_call(
        body, name=name,
        out_shape=tuple(pltpu.HBM(b.shape, b.dtype) for b in bufs),
        in_specs=[_HBM] * nb + [_SEM, _SEM, pl.BlockSpec(memory_space=pl.ANY)],
        out_specs=tuple([_HBM] * nb),
        input_output_aliases={k: k for k in range(nb)},
        compiler_params=pltpu.CompilerParams(has_side_effects=_EFFECT),
    )(*bufs, sems[0], sems[1], after)
    return list(res)


def _remote(src, dst, send_sems, recv_sems, k, target):
    return pltpu.make_async_remote_copy(src_ref=src, dst_ref=dst, send_sem=send_sems.at[k],
                                        recv_sem=recv_sems.at[k], device_id=target, device_id_type=MESH)


def _ag_region(ref, axis, chip, half):
    K, N = ref.shape
    if axis == 0:
        hs = K // N_CHIPS // 2
        assert hs % 16 == 0
        return ref.at[pl.ds(pl.multiple_of((2 * chip + half) * hs, 16), hs), :]
    ns, hk = N // N_CHIPS, K // 2
    assert ns % LANES == 0 and hk % 16 == 0
    return ref.at[pl.ds(pl.multiple_of(half * hk, 16), hk), pl.ds(pl.multiple_of(chip * ns, LANES), ns)]


def _ag_copies(stage, axes):
    n = len(axes)

    def make(bufs, send_sems, recv_sems):
        x, y, c = _my_place()
        me = 2 * x + y
        sends, recvs = [], []
        for j, (px, py) in enumerate(_peer_chips(x, y)):
            other = 2 * px + py
            for w in range(n):
                k = j * n + w
                if stage == 1:
                    src, target = _ag_region(bufs[w], axes[w], me, c), (px, py, c)
                    land = _ag_region(bufs[w], axes[w], other, c)
                else:
                    src, target = _ag_region(bufs[w], axes[w], other, c), (x, y, 1 - c)
                    land = _ag_region(bufs[w], axes[w], other, 1 - c)
                sends.append(_remote(src, src, send_sems, recv_sems, k, target))
                recvs.append(_remote(land, land, send_sems, recv_sems, k, target))
        return sends, recvs

    return make


def _half_shape(shape, axis):
    K, N = shape
    return (K, N // 2) if axis == 0 else (K // 2, N)


def _core_half(ref, axis, half):
    K, N = ref.shape
    if axis == 0:
        return ref.at[:, pl.ds(pl.multiple_of(half * (N // 2), LANES), N // 2)]
    return ref.at[pl.ds(pl.multiple_of(half * (K // 2), 16), K // 2), :]


def _chip_block(ref, axis, chip):
    K, N = ref.shape
    if axis == 0:
        return ref.at[pl.ds(pl.multiple_of(chip * (K // N_CHIPS), 16), K // N_CHIPS), :]
    return ref.at[:, pl.ds(pl.multiple_of(chip * (N // N_CHIPS), LANES), N // N_CHIPS)]


def _rs_sibling_copies(axes):
    n = len(axes)

    def make(bufs, send_sems, recv_sems):
        x, y, c = _my_place()
        sends = [_remote(_core_half(bufs[w], axes[w], 1 - c), bufs[n + w], send_sems, recv_sems, w, (x, y, 1 - c))
                 for w in range(n)]
        recvs = [_remote(bufs[n + w], bufs[n + w], send_sems, recv_sems, w, (x, y, 1 - c)) for w in range(n)]
        return sends, recvs

    return make


def _rs_chip_copies(axes):
    n = len(axes)

    def make(bufs, send_sems, recv_sems):
        x, y, c = _my_place()
        sends, recvs = [], []
        for j, (px, py) in enumerate(_peer_chips(x, y)):
            for w in range(n):
                k = j * n + w
                sends.append(_remote(_chip_block(bufs[w], axes[w], 2 * px + py), bufs[n + w].at[j],
                                     send_sems, recv_sems, k, (px, py, c)))
                recvs.append(_remote(bufs[n + w].at[j], bufs[n + w].at[j], send_sems, recv_sems, k, (px, py, c)))
        return sends, recvs

    return make


def _rs_fill_copies(axes):
    n = len(axes)

    def make(bufs, send_sems, recv_sems):
        x, y, c = _my_place()
        sends = [_remote(_core_half(bufs[w], axes[w], c), _core_half(bufs[w], axes[w], c),
                         send_sems, recv_sems, w, (x, y, 1 - c)) for w in range(n)]
        recvs = [_remote(_core_half(bufs[w], axes[w], 1 - c), _core_half(bufs[w], axes[w], 1 - c),
                         send_sems, recv_sems, w, (x, y, 1 - c)) for w in range(n)]
        return sends, recvs

    return make


def _chip_sum(g, r, axis, place, *, name):
    hk, hn = r.shape
    tk = 128 if hn > 4096 else _pick(hk, (256, 128))
    nk = hk // tk

    def body(p_ref, g_ref, r_ref, f_ref, b_ref):
        s = g_ref[...] + r_ref[...]
        f_ref[...] = s
        b_ref[...] = s.astype(BF16)

    half = pl.BlockSpec((tk, hn), lambda i, p: (i, 0))
    if axis == 0:
        g_spec = pl.BlockSpec((tk, hn), lambda i, p: (i, p[1]))
    else:
        g_spec = pl.BlockSpec((tk, hn), lambda i, p: (p[1] * nk + i, 0))
    grid_spec = pltpu.PrefetchScalarGridSpec(num_scalar_prefetch=1, grid=(nk,), in_specs=[g_spec, half],
                                             out_specs=[half, half])
    return pl.pallas_call(
        body, name=name,
        out_shape=[jax.ShapeDtypeStruct(r.shape, F32), jax.ShapeDtypeStruct(r.shape, BF16)],
        grid_spec=grid_spec,
        compiler_params=pltpu.CompilerParams(dimension_semantics=("arbitrary",),
                                             vmem_limit_bytes=_vmem(2 * tk * hn * 14)),
    )(place, g, r)


def _final_sum(own, recv, axis, place, *, name):
    _, bk, bn = recv.shape
    tk = _pick(bk, (256, 176, 128))
    nk = bk // tk

    def body(p_ref, o_ref, r_ref, out_ref):
        out_ref[...] = ((o_ref[...] + r_ref[0].astype(F32)) + r_ref[1].astype(F32)) + r_ref[2].astype(F32)

    if axis == 0:
        own_spec = pl.BlockSpec((tk, bn), lambda i, p: (p[0] * nk + i, 0))
        out_shape, out_spec = (bk, 2 * bn), pl.BlockSpec((tk, bn), lambda i, p: (i, p[1]))
    else:
        own_spec = pl.BlockSpec((tk, bn), lambda i, p: (i, p[0]))
        out_shape, out_spec = (2 * bk, bn), pl.BlockSpec((tk, bn), lambda i, p: (p[1] * nk + i, 0))
    grid_spec = pltpu.PrefetchScalarGridSpec(
        num_scalar_prefetch=1, grid=(nk,),
        in_specs=[own_spec, pl.BlockSpec((3, tk, bn), lambda i, p: (0, i, 0))], out_specs=out_spec)
    return pl.pallas_call(
        body, name=name, out_shape=jax.ShapeDtypeStruct(out_shape, F32), grid_spec=grid_spec,
        compiler_params=pltpu.CompilerParams(dimension_semantics=("arbitrary",),
                                             vmem_limit_bytes=_vmem(2 * tk * bn * 14 + 4 * tk * bn * 4)),
    )(place, own, recv)


def _allreduce_small(p):
    def body(p_ref, o_ref, r0, r1, r2, send_sems, recv_sems):
        x, y, c = _my_place()
        o_ref[...] = p_ref[...]
        for s, (peer, rbuf) in enumerate([((x, y, 1 - c), r0), ((1 - x, y, c), r1), ((x, 1 - y, c), r2)]):
            cp = pltpu.make_async_remote_copy(src_ref=o_ref, dst_ref=rbuf, send_sem=send_sems.at[s],
                                              recv_sem=recv_sems.at[s], device_id=peer, device_id_type=MESH)
            cp.start()
            cp.wait()
            o_ref[...] = o_ref[...] + rbuf[...]

    vm = pl.BlockSpec(memory_space=pltpu.VMEM)
    return pl.pallas_call(
        body, name="allreduce_small", out_shape=jax.ShapeDtypeStruct(p.shape, F32),
        in_specs=[vm], out_specs=vm,
        scratch_shapes=[pltpu.VMEM(p.shape, F32)] * 3 + [pltpu.SemaphoreType.DMA((3,))] * 2,
        compiler_params=pltpu.CompilerParams(vmem_limit_bytes=_vmem(6 * _nbytes(p.shape, F32))),
    )(p)


def _pack_rows(parts):
    rows, metas = [], []
    for a in parts:
        flat = a.reshape(-1)
        nrow = -(-flat.shape[0] // LANES)
        nrow = -(-nrow // 8) * 8
        flat = jnp.pad(flat, (0, nrow * LANES - flat.shape[0]))
        rows.append(flat.reshape(nrow, LANES))
        metas.append((a.shape, nrow))
    return jnp.concatenate(rows, axis=0), metas


def _unpack_rows(packed, metas):
    out, r0 = [], 0
    for shape, nrow in metas:
        size = int(np.prod(shape))
        out.append(packed[r0:r0 + nrow].reshape(-1)[:size].reshape(shape))
        r0 += nrow
    return out


def kernel(x, positions, pre_mix_g, post_mix_g, pre_ffn_g, post_ffn_g, a_w_in, a_b_in, a_ln_g, a_ln_b, a_w_s, a_b_s, a_w_out, b_w_qkv, b_b_qkv, b_sinks, b_w_o, ffn_w_gu, ffn_w_down, loss_target, m_pre_mix_g, m_post_mix_g, m_pre_ffn_g, m_post_ffn_g, m_a_w_in, m_a_b_in, m_a_ln_g, m_a_ln_b, m_a_w_s, m_a_b_s, m_a_w_out, m_b_w_qkv, m_b_b_qkv, m_b_sinks, m_b_w_o, m_ffn_w_gu, m_ffn_w_down, v_pre_mix_g, v_post_mix_g, v_pre_ffn_g, v_post_ffn_g, v_a_w_in, v_a_b_in, v_a_ln_g, v_a_ln_b, v_a_w_s, v_a_b_s, v_a_w_out, v_b_w_qkv, v_b_b_qkv, v_b_sinks, v_b_w_o, v_ffn_w_gu, v_ffn_w_down):
    depth, D = pre_mix_g.shape
    xi, yi, ci = _my_place()
    chip = 2 * xi + yi
    place = jnp.stack([chip, ci]).astype(jnp.int32)

    stacked = {"a_w_in": (a_w_in, m_a_w_in, v_a_w_in), "a_w_out": (a_w_out, m_a_w_out, v_a_w_out),
               "b_w_qkv": (b_w_qkv, m_b_w_qkv, v_b_w_qkv), "b_w_o": (b_w_o, m_b_w_o, v_b_w_o),
               "ffn_w_gu": (ffn_w_gu, m_ffn_w_gu, v_ffn_w_gu), "ffn_w_down": (ffn_w_down, m_ffn_w_down, v_ffn_w_down)}
    cut = {"a_w_in": 1, "a_w_out": 0, "b_w_qkv": 1, "b_w_o": 0, "ffn_w_gu": 1, "ffn_w_down": 0}

    def layer_keys(i):
        mix = [("a_w_in", i // 2), ("a_w_out", i // 2)] if i % 2 == 0 else [("b_w_qkv", i // 2), ("b_w_o", i // 2)]
        return mix + [("ffn_w_gu", i), ("ffn_w_down", i)]

    def dep(a, toks):
        for t in toks:
            a = a + t[:1, :1]
        return a

    W = {}
    for i in range(depth):
        for nm, l in layer_keys(i):
            W[(nm, l)] = _cast_block(stacked[nm][0], l, cut[nm], place, name=f"cast_{nm}_{l}")

    def gather(tag, keys, after):
        axes = [cut[nm] for nm, _ in keys]
        for stage in (1, 2):
            ss, rs, bufs, tok = _split_start(f"ag{stage}_start_{tag}", [W[k] for k in keys], 3 * len(keys),
                                             _ag_copies(stage, axes), after)
            after = yield tok
            bufs = _split_wait(f"ag{stage}_wait_{tag}", bufs, (ss, rs), _ag_copies(stage, axes), after)
            W.update(zip(keys, bufs))
        yield None

    nq = b_b_qkv.shape[1]
    bq_full = jnp.zeros((b_b_qkv.shape[0], N_CHIPS * nq), F32)
    bq_full = lax.dynamic_update_slice(bq_full, jnp.where(ci == 0, b_b_qkv, 0.0), (0, chip * nq))
    bq_packed, bq_meta = _pack_rows([bq_full])
    bq_gathered = _allreduce_small(bq_packed)
    b_qkv_full = _unpack_rows(bq_gathered, bq_meta)[0]

    first = gather("0m", layer_keys(0)[:2], bq_gathered)
    tok = next(first)
    tok = first.send(tok)
    first.send(tok)

    h = x[0]
    target = loss_target[0]
    ctab, stab = _rope_tables(positions[0])
    q_width = W[("b_w_o", 0)].shape[0]
    kv_width = N_KV_HEADS * HEAD_DIM
    row = lambda a, i: a[i:i + 1]

    saved = []
    for i in range(depth):
        j = i // 2
        s = {"h": h}
        toks = []
        ffn_w = None
        if i == 0:
            ffn_w = gather("0f", layer_keys(0)[2:], W[("a_w_out", 0)])
            toks.append(next(ffn_w))
            nxt = gather("1", layer_keys(1), toks[0])
            toks.append(next(nxt))
        elif i + 1 < depth:
            nxt = gather(str(i + 1), layer_keys(i + 1), h)
            toks.append(next(nxt))
        hn = _rms_fwd(h, dep(row(pre_mix_g, i), toks), out_dtype=BF16, name=f"rms_pre_mix_{i}")
        s["hn"] = hn
        if i % 2 == 0:
            pre = _matmul(hn, W[("a_w_in", j)], mode="nn", bias=row(a_b_in, j), out_dtype=F32, name=f"gmlp_in_{i}")
            gated = _sgu_fwd(pre, row(a_ln_g, j), row(a_ln_b, j), a_w_s[j], a_b_s[j].T, name=f"sgu_fwd_{i}")
            mix = _matmul(gated, W[("a_w_out", j)], mode="nn", out_dtype=F32, name=f"gmlp_out_{i}")
            s.update(pre=pre, gated=gated)
        else:
            qkv = _matmul(hn, W[("b_w_qkv", j)], mode="nn", bias=row(b_qkv_full, j), out_dtype=F32,
                          name=f"attn_qkv_{i}")
            qr, kr, vr = _rope_fwd(qkv, ctab, stab, q_width=q_width, kv_width=kv_width, name=f"rope_fwd_{i}")
            o = _attn_fwd(qr, kr, vr, row(b_sinks, j), name=f"attn_fwd_{i}")
            mix = _matmul(o, W[("b_w_o", j)], mode="nn", out_dtype=F32, name=f"attn_o_{i}")
            s.update(qr=qr, kr=kr, vr=vr, o=o)
        s["mix"] = mix
        toks = [ffn_w.send(mix)] if ffn_w else []
        h1 = _rms_res(h, mix, dep(row(post_mix_g, i), toks), name=f"rms_post_mix_{i}")
        if ffn_w:
            ffn_w.send(h1)
        s["h1"] = h1
        fn = _rms_fwd(h1, row(pre_ffn_g, i), out_dtype=BF16, name=f"rms_pre_ffn_{i}")
        g_pre, u_pre, act = _ffn_up(fn, W[("ffn_w_gu", i)][None], 0, name=f"ffn_up_{i}")
        f = _matmul(act, W[("ffn_w_down", i)], mode="nn", out_dtype=F32, name=f"ffn_down_{i}")
        toks = [nxt.send(f)] if i + 1 < depth else []
        h = _rms_res(h1, f, dep(row(post_ffn_g, i), toks), name=f"rms_post_ffn_{i}")
        if i + 1 < depth:
            nxt.send(h)
        s.update(fn=fn, g_pre=g_pre, u_pre=u_pre, act=act, f=f)
        saved.append(s)

    dh, loss_part = _loss_and_grad(h, target, name="loss")
    loss = lax.psum(loss_part[0, 0], ("x", "y", "c"))

    big_out = {nm: tuple(lax.empty(w.shape, F32) for _ in range(4)) for nm, (w, _, _) in stacked.items()}

    def reduce_group(i, keys, grads):
        axes = [cut[nm] for nm, _ in keys]
        n = len(keys)
        lands = [lax.empty(_half_shape(g.shape, ax), F32) for g, ax in zip(grads, axes)]
        ss, rs, bufs, tok = _split_start(f"rs_sibling_start_{i}", list(grads) + lands, n, _rs_sibling_copies(axes),
                                         place)
        after = yield tok
        bufs = _split_wait(f"rs_sibling_wait_{i}", bufs, (ss, rs), _rs_sibling_copies(axes), after)
        sums = [_chip_sum(bufs[w], bufs[n + w], axes[w], place, name=f"chip_sum_{keys[w][0]}_{keys[w][1]}")
                for w in range(n)]
        lands = []
        for (sf, _), ax in zip(sums, axes):
            hk, hn = sf.shape
            lands.append(lax.empty((3, hk // N_CHIPS, hn) if ax == 0 else (3, hk, hn // N_CHIPS), BF16))
        ss, rs, bufs, tok = _split_start(f"rs_chip_start_{i}", [sb for _, sb in sums] + lands, 3 * n,
                                         _rs_chip_copies(axes), place)
        after = yield tok
        bufs = _split_wait(f"rs_chip_wait_{i}", bufs, (ss, rs), _rs_chip_copies(axes), after)
        blocks = [_final_sum(sums[w][0], bufs[n + w], axes[w], place, name=f"final_sum_{keys[w][0]}_{keys[w][1]}")
                  for w in range(n)]
        ss, rs, bufs, tok = _split_start(f"rs_fill_start_{i}", blocks, n, _rs_fill_copies(axes), place)
        after = yield tok
        blocks = _split_wait(f"rs_fill_wait_{i}", bufs, (ss, rs), _rs_fill_copies(axes), after)
        for (nm, l), g in zip(keys, blocks):
            w, m, v = stacked[nm]
            big_out[nm] = tuple(_adamw_layer(w, m, v, g, l, big_out[nm], name=f"adamw_{nm}_{l}"))
        yield None

    reducing = []

    def advance(after):
        toks = []
        for gen in list(reducing):
            tok = gen.send(after)
            if tok is None:
                reducing.remove(gen)
            else:
                toks.append(tok)
        return toks

    small = {}
    g_pre_mix, g_post_mix, g_pre_ffn, g_post_ffn = [None] * depth, [None] * depth, [None] * depth, [None] * depth
    toks = []
    for i in reversed(range(depth)):
        j = i // 2
        s = saved[i]
        df, g_post_ffn[i] = _rms_bwd(s["f"], dep(row(post_ffn_g, i), toks), dh, None, out_dtype=BF16,
                                     name=f"rms_post_ffn_bwd_{i}")
        g_down = _matmul(s["act"], df, mode="tn", out_dtype=F32, name=f"ffn_down_dw_{i}")
        dg_, du_ = _ffn_down_dx(df, W[("ffn_w_down", i)][None], 0, s["g_pre"], s["u_pre"], name=f"ffn_down_dx_{i}")
        hid = dg_.shape[1]
        tile = _pick(hid, (1408, 768, 512, 256, 128))
        w_gu = W[("ffn_w_gu", i)]
        g_gu = lax.empty(w_gu.shape, F32)
        g_gu = _matmul(s["fn"], dg_, mode="tn", into=g_gu, tq=tile, out_dtype=F32, name=f"ffn_g_dw_{i}")
        g_gu = _matmul(s["fn"], du_, mode="tn", into=g_gu, tq=tile, q_off=hid // tile, out_dtype=F32,
                       name=f"ffn_u_dw_{i}")
        dfn_g = _matmul(dg_, w_gu, mode="nt", tr=tile, out_dtype=F32, name=f"ffn_g_dx_{i}")
        dfn = _matmul(du_, w_gu, mode="nt", tr=tile, b_r_off=hid // tile, bias=dfn_g, out_dtype=F32,
                      name=f"ffn_u_dx_{i}")
        toks = advance(dfn)
        if i == 0:
            gen = reduce_group("0f", layer_keys(0)[2:], [g_gu, g_down])
            toks.append(next(gen))
            reducing.append(gen)
        dh1, g_pre_ffn[i] = _rms_bwd(s["h1"], dep(row(pre_ffn_g, i), toks), dfn, dh, out_dtype=F32,
                                     name=f"rms_pre_ffn_bwd_{i}")
        dmix, g_post_mix[i] = _rms_bwd(s["mix"], row(post_mix_g, i), dh1, None, out_dtype=BF16,
                                       name=f"rms_post_mix_bwd_{i}")
        if i % 2 == 0:
            g_out = _matmul(s["gated"], dmix, mode="tn", out_dtype=F32, name=f"gmlp_out_dw_{i}")
            dgated = _matmul(dmix, W[("a_w_out", j)], mode="nt", out_dtype=F32, name=f"gmlp_out_dx_{i}")
            if i == 0:
                advance(dgated)
            dpre, dws, dbsT, dlng, dlnb, dbin = _sgu_bwd(s["pre"], dgated, row(a_ln_g, j), row(a_ln_b, j),
                                                         a_w_s[j], a_b_s[j].T, name=f"sgu_bwd_{i}")
            small[("a_w_s", j)] = dws
            small[("a_b_s", j)] = dbsT.T
            small[("a_ln_g", j)] = dlng
            small[("a_ln_b", j)] = dlnb
            small[("a_b_in", j)] = dbin
            g_in = _matmul(s["hn"], dpre, mode="tn", out_dtype=F32, name=f"gmlp_in_dw_{i}")
            dhn = _matmul(dpre, W[("a_w_in", j)], mode="nt", out_dtype=F32, name=f"gmlp_in_dx_{i}")
        else:
            g_out = _matmul(s["o"], dmix, mode="tn", out_dtype=F32, name=f"attn_o_dw_{i}")
            do = _matmul(dmix, W[("b_w_o", j)], mode="nt", out_dtype=BF16, name=f"attn_o_dx_{i}")
            dq, dkp, dkc, dvp, dvc, dsk = _attn_bwd(s["qr"], s["kr"], s["vr"], row(b_sinks, j), do,
                                                    name=f"attn_bwd_{i}")
            dqkv, dbq = _rope_bwd(dq, dkp, dkc, dvp, dvc, ctab, stab, name=f"rope_bwd_{i}")
            small[("b_sinks", j)] = dsk[:, :b_sinks.shape[1]]
            small[("b_b_qkv", j)] = dbq
            g_in = _matmul(s["hn"], dqkv, mode="tn", out_dtype=F32, name=f"attn_qkv_dw_{i}")
            dhn = _matmul(dqkv, W[("b_w_qkv", j)], mode="nt", out_dtype=F32, name=f"attn_qkv_dx_{i}")
        toks = advance(dhn)
        dh, g_pre_mix[i] = _rms_bwd(s["h"], dep(row(pre_mix_g, i), toks), dhn, dh1, out_dtype=F32,
                                    name=f"rms_pre_mix_bwd_{i}")
        if i == 0:
            gen = reduce_group("0m", layer_keys(0)[:2], [g_in, g_out])
        else:
            gen = reduce_group(str(i), layer_keys(i), [g_in, g_out, g_gu, g_down])
        toks = [next(gen)] + advance(dh)
        reducing.append(gen)
    grad_x = dh[None]

    toks = advance(dh)
    n_a, n_b = a_b_in.shape[0], b_sinks.shape[0]
    stack = lambda key, n: jnp.concatenate([small[(key, j)] for j in range(n)], axis=0)
    small_parts = [
        jnp.concatenate(g_pre_mix, axis=0), jnp.concatenate(g_post_mix, axis=0),
        jnp.concatenate(g_pre_ffn, axis=0), jnp.concatenate(g_post_ffn, axis=0),
        stack("a_b_in", n_a), stack("a_ln_g", n_a), stack("a_ln_b", n_a),
        jnp.stack([small[("a_w_s", j)] for j in range(n_a)]), jnp.stack([small[("a_b_s", j)] for j in range(n_a)]),
        stack("b_b_qkv", n_b), stack("b_sinks", n_b),
    ]
    packed, metas = _pack_rows(small_parts)
    reduced = _allreduce_small(dep(packed, toks))
    while reducing:
        advance(reduced)
    red = _unpack_rows(reduced, metas)
    (gr_pre_mix, gr_post_mix, gr_pre_ffn, gr_post_ffn, gr_b_in, gr_ln_g, gr_ln_b, gr_w_s, gr_b_s,
     gr_b_qkv_full, gr_sinks) = red
    gr_b_qkv = lax.dynamic_slice(gr_b_qkv_full, (0, chip * nq), (gr_b_qkv_full.shape[0], nq))

    grads = {"pre_mix_g": gr_pre_mix, "post_mix_g": gr_post_mix, "pre_ffn_g": gr_pre_ffn, "post_ffn_g": gr_post_ffn,
             "a_b_in": gr_b_in, "a_ln_g": gr_ln_g, "a_ln_b": gr_ln_b, "a_w_s": gr_w_s, "a_b_s": gr_b_s,
             "b_b_qkv": gr_b_qkv, "b_sinks": gr_sinks}
    weights = {"pre_mix_g": (pre_mix_g, m_pre_mix_g, v_pre_mix_g), "post_mix_g": (post_mix_g, m_post_mix_g, v_post_mix_g),
               "pre_ffn_g": (pre_ffn_g, m_pre_ffn_g, v_pre_ffn_g), "post_ffn_g": (post_ffn_g, m_post_ffn_g, v_post_ffn_g),
               "a_b_in": (a_b_in, m_a_b_in, v_a_b_in), "a_ln_g": (a_ln_g, m_a_ln_g, v_a_ln_g),
               "a_ln_b": (a_ln_b, m_a_ln_b, v_a_ln_b), "a_w_s": (a_w_s, m_a_w_s, v_a_w_s), "a_b_s": (a_b_s, m_a_b_s, v_a_b_s),
               "b_b_qkv": (b_b_qkv, m_b_b_qkv, v_b_b_qkv), "b_sinks": (b_sinks, m_b_sinks, v_b_sinks)}
    order = ["pre_mix_g", "post_mix_g", "pre_ffn_g", "post_ffn_g", "a_w_in", "a_b_in", "a_ln_g", "a_ln_b", "a_w_s",
             "a_b_s", "a_w_out", "b_w_qkv", "b_b_qkv", "b_sinks", "b_w_o", "ffn_w_gu", "ffn_w_down"]
    deltas, new_m, new_v = {}, {}, {}
    for nm in order:
        if nm in big_out:
            grads[nm], deltas[nm], new_m[nm], new_v[nm] = big_out[nm]
        else:
            w, m, v = weights[nm]
            deltas[nm], new_m[nm], new_v[nm] = _adamw_small(w, grads[nm], m, v, name="adamw_" + nm)
    return (loss, grad_x, *[grads[nm] for nm in order], *[deltas[nm] for nm in order],
            *[new_m[nm] for nm in order], *[new_v[nm] for nm in order])
```

```python
import functools
import math

import jax
import jax.numpy as jnp
import numpy as np
from jax import lax
from jax.experimental import pallas as pl
from jax.experimental.pallas import tpu as pltpu

F32 = jnp.float32
BF16 = jnp.bfloat16
MESH = pl.DeviceIdType.MESH

HEAD_DIM = 64
N_KV_HEADS = 4
ROPE_DIM = 16
ROPE_THETA = 500000.0
CHUNK = 128
GMLP_GROUPS = 8
RMS_EPS = 1e-6
LN_EPS = 1e-5
NEG_INF = -1e30
ADAM_LR = 0.001
ADAM_B1 = 0.9
ADAM_B2 = 0.999
ADAM_EPS = 1e-08
ADAM_WD = 0.01
ADAM_STEP = 10

N_CHIPS = 4
LANES = 128
VMEM_CAP = 58 * 1024 * 1024


def _vmem(est_bytes):
    assert est_bytes < VMEM_CAP
    return VMEM_CAP


def _pick(n, cands):
    for c in cands:
        if c <= n and n % c == 0:
            return c
    return n


def _nbytes(shape, dtype):
    return int(np.prod(shape)) * jnp.dtype(dtype).itemsize


MATMUL_VMEM_BUDGET = 48 * 1024 * 1024


def _halvings(n, unit):
    out, t = [], n
    while t % unit == 0 and t >= unit:
        out.append(t)
        if t % 2:
            break
        t //= 2
    return out


def _matmul_tiles(P, Q, R, a_bytes, b_bytes, o_bytes, full_addend, tp, tq, tr):
    step_us, bytes_per_us = 0.85, 3.2e6
    best = None
    for p in ([tp] if tp else _halvings(P, LANES)):
        for q in ([tq] if tq else _halvings(Q, LANES)):
            for r in ([tr] if tr else _halvings(R, LANES)):
                nk = R // r
                vm = 2 * (p * r * a_bytes + r * q * b_bytes + p * q * o_bytes + (p * q * 4 if full_addend else 0))
                vm += p * q * 4 * (2 if nk > 1 else 1)
                if vm > MATMUL_VMEM_BUDGET:
                    continue
                exposed = (p * r * a_bytes + r * q * b_bytes + p * q * o_bytes) / bytes_per_us
                key = ((P // p) * (Q // q) * nk * step_us + exposed, nk, abs(p - q))
                if best is None or key < best[0]:
                    best = (key, (p, q, r))
    assert best is not None, (P, Q, R)
    return best[1]


def _matmul(a, b, *, mode, out_dtype, name, a_l=None, b_l=None, bias=None, into=None, o_l=None,
            q_off=0, b_r_off=0, tp=None, tq=None, tr=None):
    a2 = a.shape[-2:]
    b2 = b.shape[-2:]
    if mode == "nn":
        (P, R), (R2, Q) = a2, b2
    elif mode == "nt":
        (P, R), (Q, R2) = a2, b2
    else:
        (R, P), (R2, Q) = a2, b2
    assert R == R2 or (mode == "nt" and R2 % R == 0), (mode, a.shape, b.shape)
    o_bytes = jnp.dtype(into.dtype if into is not None else out_dtype).itemsize
    full_addend = bias is not None and bias.shape[0] != 1
    tp, tq, tr = _matmul_tiles(P, Q, R, a.dtype.itemsize, b.dtype.itemsize, o_bytes, full_addend, tp, tq, tr)
    assert P % tp == 0 and Q % tq == 0 and R % tr == 0
    nk = R // tr
    dims = {"nn": (((1,), (0,)), ((), ())), "nt": (((1,), (1,)), ((), ())), "tn": (((0,), (0,)), ((), ()))}[mode]

    def lead(l, blk, idx):
        if l is None:
            return pl.BlockSpec(blk, idx)
        return pl.BlockSpec((None,) + blk, lambda i, j, k: (l,) + idx(i, j, k))

    if mode == "nn":
        a_spec = lead(a_l, (tp, tr), lambda i, j, k: (i, k))
        b_spec = lead(b_l, (tr, tq), lambda i, j, k: (k, j))
    elif mode == "nt":
        a_spec = lead(a_l, (tp, tr), lambda i, j, k: (i, k))
        b_spec = lead(b_l, (tq, tr), lambda i, j, k: (j, k + b_r_off))
    else:
        a_spec = lead(a_l, (tr, tp), lambda i, j, k: (k, i))
        b_spec = lead(b_l, (tr, tq), lambda i, j, k: (k, j))
    in_specs = [a_spec, b_spec]
    args = [a, b]
    if bias is not None:
        if bias.shape[0] == 1:
            in_specs.append(pl.BlockSpec((1, tq), lambda i, j, k: (0, j)))
        else:
            in_specs.append(pl.BlockSpec((tp, tq), lambda i, j, k: (i, j)))
        args.append(bias)
    aliases = {}
    if into is not None:
        in_specs.append(pl.BlockSpec(memory_space=pl.ANY))
        args.append(into)
        aliases = {len(args) - 1: 0}
        out_shape = jax.ShapeDtypeStruct(into.shape, into.dtype)
        out_dtype = into.dtype
        if o_l is None:
            out_spec = pl.BlockSpec((tp, tq), lambda i, j, k: (i, j + q_off))
        else:
            out_spec = pl.BlockSpec((None, tp, tq), lambda i, j, k: (o_l, i, j + q_off))
    else:
        out_shape = jax.ShapeDtypeStruct((P, Q), out_dtype)
        out_spec = pl.BlockSpec((tp, tq), lambda i, j, k: (i, j))
    has_bias = bias is not None
    has_into = into is not None

    def body(*refs):
        a_ref, b_ref = refs[0], refs[1]
        pos = 2
        bias_ref = None
        if has_bias:
            bias_ref = refs[pos]
            pos += 1
        if has_into:
            pos += 1
        o_ref = refs[pos]
        acc_ref = refs[pos + 1] if nk > 1 else None
        part = lax.dot_general(a_ref[...], b_ref[...], dims, preferred_element_type=F32)

        def finish(acc):
            if has_bias:
                acc = acc + bias_ref[...]
            o_ref[...] = acc.astype(out_dtype)

        if nk == 1:
            finish(part)
        else:
            k = pl.program_id(2)

            @pl.when(k == 0)
            def _():
                acc_ref[...] = part

            @pl.when(k > 0)
            def _():
                acc_ref[...] += part

            @pl.when(k == nk - 1)
            def _():
                finish(acc_ref[...])

    est = 2 * (_nbytes((tp, tr), a.dtype) + _nbytes((tr, tq), b.dtype) + _nbytes((tp, tq), out_dtype)) + 3 * tp * tq * 4
    return pl.pallas_call(
        body, name=name, out_shape=out_shape,
        grid=(P // tp, Q // tq, nk),
        in_specs=in_specs, out_specs=out_spec,
        scratch_shapes=[pltpu.VMEM((tp, tq), F32)] if nk > 1 else [],
        input_output_aliases=aliases,
        compiler_params=pltpu.CompilerParams(
            dimension_semantics=("parallel", "parallel", "arbitrary"), vmem_limit_bytes=_vmem(est)),
    )(*args)


def _row_call(body, ins, outs, *, name, rows, tr, acc_outs=(), est=0):
    in_specs = []
    for arr, kind in ins:
        if kind == "row":
            in_specs.append(pl.BlockSpec((tr, arr.shape[1]), lambda i: (i, 0)))
        else:
            nd = arr.ndim
            in_specs.append(pl.BlockSpec(arr.shape, lambda i, nd=nd: (0,) * nd))
    out_shapes = [jax.ShapeDtypeStruct(s, d) for s, d in outs] + [jax.ShapeDtypeStruct(s, d) for s, d in acc_outs]
    out_specs = [pl.BlockSpec((tr, s[1]), lambda i: (i, 0)) for s, _ in outs]
    out_specs += [pl.BlockSpec(s, lambda i, nd=len(s): (0,) * nd) for s, _ in acc_outs]
    res = pl.pallas_call(
        body, name=name, out_shape=out_shapes, grid=(rows // tr,), in_specs=in_specs, out_specs=out_specs,
        compiler_params=pltpu.CompilerParams(dimension_semantics=("arbitrary",), vmem_limit_bytes=_vmem(est)),
    )(*[a for a, _ in ins])
    return res


def _rms_fwd(x, g, *, out_dtype, name):
    T, D = x.shape
    tr = _pick(T, (512, 256, 128))

    def body(x_ref, g_ref, o_ref):
        xv = x_ref[...]
        r = lax.rsqrt(jnp.mean(xv * xv, axis=-1, keepdims=True) + RMS_EPS)
        o_ref[...] = (xv * r * g_ref[...]).astype(out_dtype)

    return _row_call(body, [(x, "row"), (g, "full")], [((T, D), out_dtype)], name=name, rows=T, tr=tr,
                     est=8 * tr * D * 4)[0]


def _rms_res(h, y, g, *, name):
    T, D = h.shape
    tr = _pick(T, (512, 256, 128))

    def body(h_ref, y_ref, g_ref, o_ref):
        yv = y_ref[...]
        r = lax.rsqrt(jnp.mean(yv * yv, axis=-1, keepdims=True) + RMS_EPS)
        o_ref[...] = h_ref[...] + yv * r * g_ref[...]

    return _row_call(body, [(h, "row"), (y, "row"), (g, "full")], [((T, D), F32)], name=name, rows=T, tr=tr,
                     est=10 * tr * D * 4)[0]


def _rms_bwd(x, g, dy, dres, *, out_dtype, name):
    T, D = x.shape
    tr = _pick(T, (512, 256, 128))
    has_res = dres is not None

    def body(*refs):
        if has_res:
            x_ref, g_ref, dy_ref, dr_ref, dx_ref, dg_ref = refs
        else:
            x_ref, g_ref, dy_ref, dx_ref, dg_ref = refs
        xv = x_ref[...]
        r = lax.rsqrt(jnp.mean(xv * xv, axis=-1, keepdims=True) + RMS_EPS)
        xhat = xv * r
        dyv = dy_ref[...].astype(F32)
        dxn = dyv * g_ref[...]
        dx = r * (dxn - xhat * jnp.mean(dxn * xhat, axis=-1, keepdims=True))
        if has_res:
            dx = dx + dr_ref[...]
        dx_ref[...] = dx.astype(out_dtype)
        part = jnp.sum(dyv * xhat, axis=0, keepdims=True)

        @pl.when(pl.program_id(0) == 0)
        def _():
            dg_ref[...] = part

        @pl.when(pl.program_id(0) > 0)
        def _():
            dg_ref[...] += part

    ins = [(x, "row"), (g, "full"), (dy, "row")] + ([(dres, "row")] if has_res else [])
    dx, dg = _row_call(body, ins, [((T, D), out_dtype)], name=name, rows=T, tr=tr, acc_outs=[((1, D), F32)],
                       est=12 * tr * D * 4)
    return dx, dg


def _ffn_up(fn, w_gu, l, *, name):
    T, D = fn.shape
    H = w_gu.shape[2] // 2
    tp = _pick(T, (1024, 512, 256, 128))
    tq = _pick(H, (1408, 768, 512, 256, 128))
    nj = H // tq

    def body(a_ref, wg_ref, wu_ref, g_ref, u_ref, act_ref):
        a = a_ref[...]
        g = jnp.dot(a, wg_ref[...], preferred_element_type=F32)
        u = jnp.dot(a, wu_ref[...], preferred_element_type=F32)
        g_ref[...] = g.astype(BF16)
        u_ref[...] = u.astype(BF16)
        act_ref[...] = (g * jax.nn.sigmoid(g) * u).astype(BF16)

    tile = pl.BlockSpec((tp, tq), lambda j, i: (i, j))
    est = 2 * (tp * D * 2 + 2 * D * tq * 2 + 3 * tp * tq * 2) + 4 * tp * tq * 4
    return pl.pallas_call(
        body, name=name,
        out_shape=[jax.ShapeDtypeStruct((T, H), BF16), jax.ShapeDtypeStruct((T, H), BF16),
                   jax.ShapeDtypeStruct((T, H), BF16)],
        grid=(nj, T // tp),
        in_specs=[pl.BlockSpec((tp, D), lambda j, i: (i, 0)),
                  pl.BlockSpec((None, D, tq), lambda j, i: (l, 0, j)),
                  pl.BlockSpec((None, D, tq), lambda j, i: (l, 0, j + nj))],
        out_specs=[tile, tile, tile],
        compiler_params=pltpu.CompilerParams(dimension_semantics=("parallel", "parallel"),
                                             vmem_limit_bytes=_vmem(est)),
    )(fn, w_gu, w_gu)


def _ffn_down_dx(df, w_down, l, g, u, *, name):
    T, D = df.shape
    H = w_down.shape[1]
    tp = _pick(T, (512, 256, 128))
    tq = _pick(H, (1408, 768, 512, 256, 128))

    def body(a_ref, w_ref, g_ref, u_ref, dg_ref, du_ref):
        da = lax.dot_general(a_ref[...], w_ref[...], (((1,), (1,)), ((), ())), preferred_element_type=F32)
        gv = g_ref[...].astype(F32)
        sg = jax.nn.sigmoid(gv)
        silu = gv * sg
        dg_ref[...] = (da * u_ref[...].astype(F32) * (sg + silu * (1.0 - sg))).astype(BF16)
        du_ref[...] = (da * silu).astype(BF16)

    tile = pl.BlockSpec((tp, tq), lambda j, i: (i, j))
    est = 2 * (tp * D * 2 + tq * D * 2 + 4 * tp * tq * 2) + 5 * tp * tq * 4
    return pl.pallas_call(
        body, name=name,
        out_shape=[jax.ShapeDtypeStruct((T, H), BF16), jax.ShapeDtypeStruct((T, H), BF16)],
        grid=(H // tq, T // tp),
        in_specs=[pl.BlockSpec((tp, D), lambda j, i: (i, 0)),
                  pl.BlockSpec((None, tq, D), lambda j, i: (l, j, 0)), tile, tile],
        out_specs=[tile, tile],
        compiler_params=pltpu.CompilerParams(dimension_semantics=("parallel", "parallel"),
                                             vmem_limit_bytes=_vmem(est)),
    )(df, w_down, g, u)


def _loss_and_grad(y, target, *, name):
    T, D = y.shape
    tr = _pick(T, (512, 256, 128))

    def body(y_ref, t_ref, dy_ref, l_ref):
        e = y_ref[...] - t_ref[...]
        dy_ref[...] = e * (1.0 / D)
        part = jnp.sum(jnp.sum(e * e, axis=1, keepdims=True), axis=0, keepdims=True) * (0.5 / D)

        @pl.when(pl.program_id(0) == 0)
        def _():
            l_ref[...] = part

        @pl.when(pl.program_id(0) > 0)
        def _():
            l_ref[...] += part

    dy, l = _row_call(body, [(y, "row"), (target, "row")], [((T, D), F32)], name=name, rows=T, tr=tr,
                      acc_outs=[((1, 1), F32)], est=8 * tr * D * 4)
    return dy, l


_SQRT_HALF = 0.7071067811865476
_INV_SQRT_2PI = 0.3989422804014327


def _gelu_parts(x):
    cdf = 0.5 * (1.0 + lax.erf(x * _SQRT_HALF))
    return cdf


def _sgu_common(pre, lng, lnb, W):
    cdf = _gelu_parts(pre)
    z = pre * cdf
    u = z[:, :W]
    v = z[:, W:]
    mu = jnp.mean(v, axis=-1, keepdims=True)
    vc = v - mu
    var = jnp.mean(vc * vc, axis=-1, keepdims=True)
    rstd = lax.rsqrt(var + LN_EPS)
    vhat = vc * rstd
    vn = vhat * lng + lnb
    return cdf, u, vhat, rstd, vn


def _causal_mask():
    t = lax.broadcasted_iota(jnp.int32, (CHUNK, CHUNK), 0)
    s = lax.broadcasted_iota(jnp.int32, (CHUNK, CHUNK), 1)
    return t >= s


def _sgu_fwd(pre, lng, lnb, ws, bsT, *, name):
    T, W2 = pre.shape
    W = W2 // 2
    G = ws.shape[0]
    gd = W // G

    def body(pre_ref, lng_ref, lnb_ref, ws_ref, bs_ref, o_ref):
        _, u, _, _, vn = _sgu_common(pre_ref[...], lng_ref[...], lnb_ref[...], W)
        vnb = vn.astype(BF16)
        causal = _causal_mask()
        for g in range(G):
            w = jnp.where(causal, ws_ref[g], 0.0).astype(BF16)
            sv = jnp.dot(w, vnb[:, g * gd:(g + 1) * gd], preferred_element_type=F32) + bs_ref[:, g:g + 1]
            o_ref[:, g * gd:(g + 1) * gd] = (u[:, g * gd:(g + 1) * gd] * sv).astype(BF16)

    return pl.pallas_call(
        body, name=name, out_shape=jax.ShapeDtypeStruct((T, W), BF16), grid=(T // CHUNK,),
        in_specs=[pl.BlockSpec((CHUNK, W2), lambda i: (i, 0)),
                  pl.BlockSpec((1, W), lambda i: (0, 0)), pl.BlockSpec((1, W), lambda i: (0, 0)),
                  pl.BlockSpec(ws.shape, lambda i: (0, 0, 0)), pl.BlockSpec(bsT.shape, lambda i: (0, 0))],
        out_specs=pl.BlockSpec((CHUNK, W), lambda i: (i, 0)),
        compiler_params=pltpu.CompilerParams(dimension_semantics=("arbitrary",),
                                             vmem_limit_bytes=_vmem(12 * CHUNK * W2 * 4)),
    )(pre, lng, lnb, ws, bsT)


def _sgu_bwd(pre, dgated, lng, lnb, ws, bsT, *, name):
    T, W2 = pre.shape
    W = W2 // 2
    G = ws.shape[0]
    gd = W // G

    def body(pre_ref, dgt_ref, lng_ref, lnb_ref, ws_ref, bs_ref,
             dpre_ref, dws_ref, dbs_ref, dlng_ref, dlnb_ref, dbin_ref):
        first = pl.program_id(0) == 0

        @pl.when(first)
        def _():
            dws_ref[...] = jnp.zeros_like(dws_ref)
            dbs_ref[...] = jnp.zeros_like(dbs_ref)
            dlng_ref[...] = jnp.zeros_like(dlng_ref)
            dlnb_ref[...] = jnp.zeros_like(dlnb_ref)
            dbin_ref[...] = jnp.zeros_like(dbin_ref)

        pre_v = pre_ref[...]
        lng_v = lng_ref[...]
        cdf, u, vhat, rstd, vn = _sgu_common(pre_v, lng_v, lnb_ref[...], W)
        vnb = vn.astype(BF16)
        dgt = dgt_ref[...].astype(F32)
        causal = _causal_mask()
        du_parts, dvn_parts = [], []
        for g in range(G):
            sl = slice(g * gd, (g + 1) * gd)
            w = jnp.where(causal, ws_ref[g], 0.0).astype(BF16)
            sv = jnp.dot(w, vnb[:, sl], preferred_element_type=F32) + bs_ref[:, g:g + 1]
            dgt_g = dgt[:, sl]
            du_parts.append(dgt_g * sv)
            dsv = dgt_g * u[:, sl]
            dsvb = dsv.astype(BF16)
            dvn_parts.append(lax.dot_general(w, dsvb, (((0,), (0,)), ((), ())), preferred_element_type=F32))
            dw = lax.dot_general(dsvb, vnb[:, sl], (((1,), (1,)), ((), ())), preferred_element_type=F32)
            dws_ref[g] += jnp.where(causal, dw, 0.0)
            dbs_ref[:, g:g + 1] += jnp.sum(dsv, axis=1, keepdims=True)
        du = jnp.concatenate(du_parts, axis=1)
        dvn = jnp.concatenate(dvn_parts, axis=1)
        dlng_ref[...] += jnp.sum(dvn * vhat, axis=0, keepdims=True)
        dlnb_ref[...] += jnp.sum(dvn, axis=0, keepdims=True)
        dvh = dvn * lng_v
        dv = rstd * (dvh - jnp.mean(dvh, axis=-1, keepdims=True)
                     - vhat * jnp.mean(dvh * vhat, axis=-1, keepdims=True))
        dz = jnp.concatenate([du, dv], axis=1)
        dgelu = cdf + pre_v * jnp.exp(-0.5 * pre_v * pre_v) * _INV_SQRT_2PI
        dpre = dz * dgelu
        dbin_ref[...] += jnp.sum(dpre, axis=0, keepdims=True)
        dpre_ref[...] = dpre.astype(BF16)

    full = lambda shape: pl.BlockSpec(shape, lambda i, nd=len(shape): (0,) * nd)
    return pl.pallas_call(
        body, name=name,
        out_shape=[jax.ShapeDtypeStruct((T, W2), BF16), jax.ShapeDtypeStruct(ws.shape, F32),
                   jax.ShapeDtypeStruct(bsT.shape, F32), jax.ShapeDtypeStruct((1, W), F32),
                   jax.ShapeDtypeStruct((1, W), F32), jax.ShapeDtypeStruct((1, W2), F32)],
        grid=(T // CHUNK,),
        in_specs=[pl.BlockSpec((CHUNK, W2), lambda i: (i, 0)), pl.BlockSpec((CHUNK, W), lambda i: (i, 0)),
                  full((1, W)), full((1, W)), full(ws.shape), full(bsT.shape)],
        out_specs=[pl.BlockSpec((CHUNK, W2), lambda i: (i, 0)), full(ws.shape), full(bsT.shape),
                   full((1, W)), full((1, W)), full((1, W2))],
        compiler_params=pltpu.CompilerParams(dimension_semantics=("arbitrary",),
                                             vmem_limit_bytes=_vmem(24 * CHUNK * W2 * 4)),
    )(pre, dgated, lng, lnb, ws, bsT)


def _rope_tables(positions):
    half = ROPE_DIM // 2
    inv_freq = ROPE_THETA ** (-jnp.arange(0, ROPE_DIM, 2, dtype=F32) / ROPE_DIM)
    ang = positions.astype(F32).reshape(-1, 1) * inv_freq
    cos, sin = jnp.cos(ang), jnp.sin(ang)
    T = ang.shape[0]
    rest = HEAD_DIM - ROPE_DIM
    c64 = jnp.concatenate([cos, cos, jnp.ones((T, rest), F32)], axis=1)
    s64 = jnp.concatenate([-sin, sin, jnp.zeros((T, rest), F32)], axis=1)
    del half
    return jnp.tile(c64, (1, LANES // HEAD_DIM)), jnp.tile(s64, (1, LANES // HEAD_DIM))


def _swap8(x):
    W = x.shape[1]
    half = ROPE_DIM // 2
    lane = lax.broadcasted_iota(jnp.int32, x.shape, 1) % HEAD_DIM
    return jnp.where(lane < half, pltpu.roll(x, W - half, axis=1),
                     jnp.where(lane < ROPE_DIM, pltpu.roll(x, half, axis=1), 0.0))


def _wide(tab, W):
    return jnp.concatenate([tab] * (W // LANES), axis=1) if W > LANES else tab


def _rope_fwd(qkv, ctab, stab, *, q_width, kv_width, name):
    T = qkv.shape[0]
    tr = _pick(T, (256, 128))
    scale = HEAD_DIM ** -0.5

    def body(x_ref, c_ref, s_ref, q_ref, k_ref, v_ref):
        c = c_ref[...]
        s = s_ref[...]
        q = x_ref[:, :q_width]
        k = x_ref[:, q_width:q_width + kv_width]
        q_ref[...] = ((q * _wide(c, q_width) + _swap8(q) * _wide(s, q_width)) * scale).astype(BF16)
        k_ref[...] = (k * _wide(c, kv_width) + _swap8(k) * _wide(s, kv_width)).astype(BF16)
        v_ref[...] = x_ref[:, q_width + kv_width:].astype(BF16)

    return _row_call(body, [(qkv, "row"), (ctab, "row"), (stab, "row")],
                     [((T, q_width), BF16), ((T, kv_width), BF16), ((T, kv_width), BF16)],
                     name=name, rows=T, tr=tr, est=10 * tr * qkv.shape[1] * 4)


_NT = (((1,), (1,)), ((), ()))
_TN = (((0,), (0,)), ((), ()))


def _group_rows(ref, heads):
    return jnp.concatenate([ref[:, h * HEAD_DIM:(h + 1) * HEAD_DIM] for h in heads], axis=0)


def _attn_valid(grp):
    qi = np.arange(grp * CHUNK)[:, None] % CHUNK
    sj = np.arange(2 * CHUNK)[None, :]
    cur = (sj >= CHUNK) & (sj - CHUNK <= qi)
    prev = (sj < CHUNK) & (sj > qi)
    return jnp.asarray(np.stack([cur, cur | prev]).astype(np.float32))


def _valid_spec(grp):
    return pl.BlockSpec((None, grp * CHUNK, 2 * CHUNK), lambda n: (jnp.minimum(n, 1), 0, 0))


def _attn_group_probs(q, kk, sinks, valid, grp):
    rows = grp * CHUNK
    s = lax.dot_general(q, kk, _NT, preferred_element_type=F32)
    s = jnp.where(valid, s, NEG_INF)
    r = lax.broadcasted_iota(jnp.int32, (rows, 1), 0)
    sink = jnp.full((rows, 1), sinks[grp - 1], F32)
    for g in range(grp - 2, -1, -1):
        sink = jnp.where(r < (g + 1) * CHUNK, sinks[g], sink)
    m = jnp.maximum(jnp.max(s, axis=1, keepdims=True), sink)
    p = jnp.exp(s - m)
    ps = jnp.exp(sink - m)
    inv = 1.0 / (jnp.sum(p, axis=1, keepdims=True) + ps)
    return p * inv, ps * inv


def _kv_specs(width, nb):
    prev = pl.BlockSpec((CHUNK, width), lambda n: (jnp.maximum(n - 1, 0), 0))
    cur = pl.BlockSpec((CHUNK, width), lambda n: (n, 0))
    return prev, cur


def _attn_fwd(qr, kr, vr, sinks, *, name):
    T, QW = qr.shape
    KW = kr.shape[1]
    HQ, HK = QW // HEAD_DIM, KW // HEAD_DIM
    grp = HQ // HK
    nb = T // CHUNK

    def body(q_ref, kp_ref, kc_ref, vp_ref, vc_ref, s_ref, ok_ref, o_ref):
        valid = ok_ref[...] > 0.5
        for kh in range(HK):
            ks = slice(kh * HEAD_DIM, (kh + 1) * HEAD_DIM)
            heads = list(range(kh * grp, (kh + 1) * grp))
            q = _group_rows(q_ref, heads)
            kk = jnp.concatenate([kp_ref[:, ks], kc_ref[:, ks]], axis=0)
            vv = jnp.concatenate([vp_ref[:, ks], vc_ref[:, ks]], axis=0)
            p, _ = _attn_group_probs(q, kk, [s_ref[0, h] for h in heads], valid, grp)
            o = jnp.dot(p.astype(BF16), vv, preferred_element_type=F32).astype(BF16)
            for g, h in enumerate(heads):
                o_ref[:, h * HEAD_DIM:(h + 1) * HEAD_DIM] = o[g * CHUNK:(g + 1) * CHUNK]

    kp, kc = _kv_specs(KW, nb)
    return pl.pallas_call(
        body, name=name, out_shape=jax.ShapeDtypeStruct((T, QW), BF16), grid=(nb,),
        in_specs=[pl.BlockSpec((CHUNK, QW), lambda n: (n, 0)), kp, kc, kp, kc,
                  pl.BlockSpec(memory_space=pltpu.SMEM), _valid_spec(grp)],
        out_specs=pl.BlockSpec((CHUNK, QW), lambda n: (n, 0)),
        compiler_params=pltpu.CompilerParams(dimension_semantics=("arbitrary",), vmem_limit_bytes=_vmem(8 << 20)),
    )(qr, kr, kr, vr, vr, sinks, _attn_valid(grp))


def _attn_bwd(qr, kr, vr, sinks, do, *, name):
    T, QW = qr.shape
    KW = kr.shape[1]
    HQ, HK = QW // HEAD_DIM, KW // HEAD_DIM
    grp = HQ // HK
    nb = T // CHUNK

    def body(q_ref, kp_ref, kc_ref, vp_ref, vc_ref, s_ref, do_ref, ok_ref,
             dq_ref, dkp_ref, dkc_ref, dvp_ref, dvc_ref, ds_ref):
        n = pl.program_id(0)
        valid = ok_ref[...] > 0.5
        lane = lax.broadcasted_iota(jnp.int32, (1, LANES), 1)
        dsink = jnp.zeros((1, LANES), F32)
        for kh in range(HK):
            ks = slice(kh * HEAD_DIM, (kh + 1) * HEAD_DIM)
            heads = list(range(kh * grp, (kh + 1) * grp))
            q = _group_rows(q_ref, heads)
            doh = _group_rows(do_ref, heads)
            kk = jnp.concatenate([kp_ref[:, ks], kc_ref[:, ks]], axis=0)
            vv = jnp.concatenate([vp_ref[:, ks], vc_ref[:, ks]], axis=0)
            p, ps = _attn_group_probs(q, kk, [s_ref[0, h] for h in heads], valid, grp)
            dp = lax.dot_general(doh, vv, _NT, preferred_element_type=F32)
            delta = jnp.sum(p * dp, axis=1, keepdims=True)
            ds = (p * (dp - delta)).astype(BF16)
            dv = lax.dot_general(p.astype(BF16), doh, _TN, preferred_element_type=F32)
            dk = lax.dot_general(ds, q, _TN, preferred_element_type=F32)
            dq = jnp.dot(ds, kk, preferred_element_type=F32)
            psd = ps * delta
            for g, h in enumerate(heads):
                dq_ref[:, h * HEAD_DIM:(h + 1) * HEAD_DIM] = dq[g * CHUNK:(g + 1) * CHUNK]
                dsink = dsink + jnp.where(
                    lane == h, -jnp.sum(psd[g * CHUNK:(g + 1) * CHUNK], axis=0, keepdims=True), 0.0)
            dkp_ref[:, ks] = dk[:CHUNK]
            dkc_ref[:, ks] = dk[CHUNK:]
            dvp_ref[:, ks] = dv[:CHUNK]
            dvc_ref[:, ks] = dv[CHUNK:]

        @pl.when(n == 0)
        def _():
            ds_ref[...] = dsink

        @pl.when(n > 0)
        def _():
            ds_ref[...] += dsink

    kp, kc = _kv_specs(KW, nb)
    qspec = pl.BlockSpec((CHUNK, QW), lambda n: (n, 0))
    kout = pl.BlockSpec((CHUNK, KW), lambda n: (n, 0))
    return pl.pallas_call(
        body, name=name,
        out_shape=[jax.ShapeDtypeStruct((T, QW), F32)] + [jax.ShapeDtypeStruct((T, KW), F32)] * 4
        + [jax.ShapeDtypeStruct((1, LANES), F32)],
        grid=(nb,),
        in_specs=[qspec, kp, kc, kp, kc, pl.BlockSpec(memory_space=pltpu.SMEM), qspec, _valid_spec(grp)],
        out_specs=[qspec, kout, kout, kout, kout, pl.BlockSpec((1, LANES), lambda n: (0, 0))],
        compiler_params=pltpu.CompilerParams(dimension_semantics=("arbitrary",), vmem_limit_bytes=_vmem(12 << 20)),
    )(qr, kr, kr, vr, vr, sinks, do, _attn_valid(grp))


def _rope_bwd(dq, dkp, dkc, dvp, dvc, ctab, stab, *, name):
    T, QW = dq.shape
    KW = dkp.shape[1]
    nb = T // CHUNK
    scale = HEAD_DIM ** -0.5
    width = QW + 2 * KW

    def body(dq_ref, dkc_ref, dkn_ref, dvc_ref, dvn_ref, c_ref, s_ref, o_ref, db_ref):
        n = pl.program_id(0)
        c = c_ref[...]
        s = s_ref[...]
        has_next = (n < nb - 1).astype(F32)
        dqv = dq_ref[...]
        dk = dkc_ref[...] + has_next * dkn_ref[...]
        dv = dvc_ref[...] + has_next * dvn_ref[...]
        dq_pre = (dqv * _wide(c, QW) + _swap8(dqv * _wide(s, QW))) * scale
        dk_pre = dk * _wide(c, KW) + _swap8(dk * _wide(s, KW))
        o_ref[:, :QW] = dq_pre.astype(BF16)
        o_ref[:, QW:QW + KW] = dk_pre.astype(BF16)
        o_ref[:, QW + KW:] = dv.astype(BF16)
        part = jnp.concatenate([jnp.sum(dq_pre, axis=0, keepdims=True), jnp.sum(dk_pre, axis=0, keepdims=True),
                                jnp.sum(dv, axis=0, keepdims=True)], axis=1)

        @pl.when(n == 0)
        def _():
            db_ref[...] = part

        @pl.when(n > 0)
        def _():
            db_ref[...] += part

    cur = lambda w: pl.BlockSpec((CHUNK, w), lambda n: (n, 0))
    nxt = lambda w: pl.BlockSpec((CHUNK, w), lambda n: (jnp.minimum(n + 1, nb - 1), 0))
    return pl.pallas_call(
        body, name=name,
        out_shape=[jax.ShapeDtypeStruct((T, width), BF16), jax.ShapeDtypeStruct((1, width), F32)],
        grid=(nb,),
        in_specs=[cur(QW), cur(KW), nxt(KW), cur(KW), nxt(KW), cur(LANES), cur(LANES)],
        out_specs=[cur(width), pl.BlockSpec((1, width), lambda n: (0, 0))],
        compiler_params=pltpu.CompilerParams(dimension_semantics=("arbitrary",), vmem_limit_bytes=_vmem(8 << 20)),
    )(dq, dkc, dkp, dvc, dvp, ctab, stab)


def _cast_block(w, l, axis, chip_arr, *, name):
    _, Ks, Ns = w.shape
    tk = _pick(Ks, (512, 352, 256, 128))
    nk = Ks // tk
    full = (Ks * N_CHIPS, Ns) if axis == 0 else (Ks, Ns * N_CHIPS)

    def body(p_ref, w_ref, o_ref):
        o_ref[...] = w_ref[...].astype(BF16)

    if axis == 0:
        out_spec = pl.BlockSpec((tk, Ns), lambda i, p: (p[0] * nk + i, 0))
    else:
        out_spec = pl.BlockSpec((tk, Ns), lambda i, p: (i, p[0]))
    grid_spec = pltpu.PrefetchScalarGridSpec(
        num_scalar_prefetch=1, grid=(nk,),
        in_specs=[pl.BlockSpec((None, tk, Ns), lambda i, p: (l, i, 0))], out_specs=out_spec)
    return pl.pallas_call(
        body, name=name, out_shape=jax.ShapeDtypeStruct(full, BF16), grid_spec=grid_spec,
        compiler_params=pltpu.CompilerParams(dimension_semantics=("arbitrary",),
                                             vmem_limit_bytes=_vmem(4 * tk * Ns * 6)),
    )(chip_arr, w)


def _adamw_math(w, g, m, v):
    m = ADAM_B1 * m + (1.0 - ADAM_B1) * g
    v = ADAM_B2 * v + (1.0 - ADAM_B2) * (g * g)
    m_hat = m / (1.0 - ADAM_B1 ** ADAM_STEP)
    v_hat = v / (1.0 - ADAM_B2 ** ADAM_STEP)
    delta = -ADAM_LR * (m_hat / (jnp.sqrt(v_hat) + ADAM_EPS) + ADAM_WD * w)
    return delta, m, v


def _adamw_layer(w, m, v, g, l, outs, *, name):
    _, K, N = w.shape
    tk = _pick(K, (256, 176, 128))

    def body(w_ref, m_ref, v_ref, g_ref, _g, _d, _m, _v, go_ref, d_ref, mo_ref, vo_ref):
        gv = g_ref[...]
        d, mn, vn = _adamw_math(w_ref[...], gv, m_ref[...], v_ref[...])
        go_ref[...] = gv
        d_ref[...] = d
        mo_ref[...] = mn
        vo_ref[...] = vn

    layer = pl.BlockSpec((None, tk, N), lambda i: (l, i, 0))
    any_spec = pl.BlockSpec(memory_space=pl.ANY)
    sd = jax.ShapeDtypeStruct(w.shape, F32)
    return pl.pallas_call(
        body, name=name, out_shape=[sd, sd, sd, sd], grid=(K // tk,),
        in_specs=[layer, layer, layer, pl.BlockSpec((tk, N), lambda i: (i, 0))] + [any_spec] * 4,
        out_specs=[layer] * 4, input_output_aliases={4: 0, 5: 1, 6: 2, 7: 3},
        compiler_params=pltpu.CompilerParams(dimension_semantics=("arbitrary",),
                                             vmem_limit_bytes=_vmem(2 * 8 * tk * N * 4 + 6 * tk * N * 4)),
    )(w, m, v, g, *outs)


def _adamw_small(w, g, m, v, *, name):
    def body(w_ref, g_ref, m_ref, v_ref, d_ref, mo_ref, vo_ref):
        d, mn, vn = _adamw_math(w_ref[...], g_ref[...], m_ref[...], v_ref[...])
        d_ref[...] = d
        mo_ref[...] = mn
        vo_ref[...] = vn

    sd = jax.ShapeDtypeStruct(w.shape, F32)
    return pl.pallas_call(body, name=name, out_shape=[sd, sd, sd])(w, g, m, v)


def _my_place():
    return lax.axis_index("x"), lax.axis_index("y"), lax.axis_index("c")


def _peer_chips(x, y):
    return [(1 - x, y), (x, 1 - y), (1 - x, 1 - y)]


_HBM = pl.BlockSpec(memory_space=pltpu.HBM)
_SEM = pl.BlockSpec(memory_space=pltpu.SEMAPHORE)
_EFFECT = pltpu.SideEffectType.DATAFLOW_SIDE_EFFECTING


def _split_start(name, bufs, n_copies, make_copies, after):
    nb = len(bufs)

    def body(*refs):
        send_sems, recv_sems = refs[nb + 1], refs[nb + 2]
        token = refs[2 * nb + 3]
        sends, _ = make_copies(refs[:nb], send_sems, recv_sems)
        for cp in sends:
            cp.start()
        token[...] = jnp.zeros_like(token)

    res = pl.pallas_call(
        body, name=name,
        out_shape=(pltpu.SemaphoreType.DMA((n_copies,)), pltpu.SemaphoreType.DMA((n_copies,)),
                   *[pltpu.HBM(b.shape, b.dtype) for b in bufs], jax.ShapeDtypeStruct((8, LANES), F32)),
        in_specs=[_HBM] * nb + [pl.BlockSpec(memory_space=pl.ANY)],
        out_specs=(_SEM, _SEM, *[_HBM] * nb, pl.BlockSpec(memory_space=pltpu.VMEM)),
        input_output_aliases={k: 2 + k for k in range(nb)},
        compiler_params=pltpu.CompilerParams(has_side_effects=_EFFECT),
    )(*[pltpu.with_memory_space_constraint(b, pltpu.HBM) for b in bufs], after)
    return res[0], res[1], list(res[2:2 + nb]), res[2 + nb]


def _split_wait(name, bufs, sems, make_copies, after):
    nb = len(bufs)

    def body(*refs):
        send_sems, recv_sems = refs[nb], refs[nb + 1]
        sends, recvs = make_copies(refs[:nb], send_sems, recv_sems)
        for cp in sends:
            cp.wait_send()
        for cp in recvs:
            cp.wait_recv()

    res = pl.pallas_call(
        body, name=name,
        out_shape=tuple(pltpu.HBM(b.shape, b.dtype) for b in bufs),
        in_specs=[_HBM] * nb + [_SEM, _SEM, pl.BlockSpec(memory_space=pl.ANY)],
        out_specs=tuple([_HBM] * nb),
        input_output_aliases={k: k for k in range(nb)},
        compiler_params=pltpu.CompilerParams(has_side_effects=_EFFECT),
    )(*bufs, sems[0], sems[1], after)
    return list(res)


def _remote(src, dst, send_sems, recv_sems, k, target):
    return pltpu.make_async_remote_copy(src_ref=src, dst_ref=dst, send_sem=send_sems.at[k],
                                        recv_sem=recv_sems.at[k], device_id=target, device_id_type=MESH)


def _ag_region(ref, axis, chip, half):
    K, N = ref.shape
    if axis == 0:
        hs = K // N_CHIPS // 2
        assert hs % 16 == 0
        return ref.at[pl.ds(pl.multiple_of((2 * chip + half) * hs, 16), hs), :]
    ns, hk = N // N_CHIPS, K // 2
    assert ns % LANES == 0 and hk % 16 == 0
    return ref.at[pl.ds(pl.multiple_of(half * hk, 16), hk), pl.ds(pl.multiple_of(chip * ns, LANES), ns)]


def _ag_copies(stage, axes):
    n = len(axes)

    def make(bufs, send_sems, recv_sems):
        x, y, c = _my_place()
        me = 2 * x + y
        sends, recvs = [], []
        for j, (px, py) in enumerate(_peer_chips(x, y)):
            other = 2 * px + py
            for w in range(n):
                k = j * n + w
                if stage == 1:
                    src, target = _ag_region(bufs[w], axes[w], me, c), (px, py, c)
                    land = _ag_region(bufs[w], axes[w], other, c)
                else:
                    src, target = _ag_region(bufs[w], axes[w], other, c), (x, y, 1 - c)
                    land = _ag_region(bufs[w], axes[w], other, 1 - c)
                sends.append(_remote(src, src, send_sems, recv_sems, k, target))
                recvs.append(_remote(land, land, send_sems, recv_sems, k, target))
        return sends, recvs

    return make


def _half_shape(shape, axis):
    K, N = shape
    return (K, N // 2) if axis == 0 else (K // 2, N)


def _core_half(ref, axis, half):
    K, N = ref.shape
    if axis == 0:
        return ref.at[:, pl.ds(pl.multiple_of(half * (N // 2), LANES), N // 2)]
    return ref.at[pl.ds(pl.multiple_of(half * (K // 2), 16), K // 2), :]


def _chip_block(ref, axis, chip):
    K, N = ref.shape
    if axis == 0:
        return ref.at[pl.ds(pl.multiple_of(chip * (K // N_CHIPS), 16), K // N_CHIPS), :]
    return ref.at[:, pl.ds(pl.multiple_of(chip * (N // N_CHIPS), LANES), N // N_CHIPS)]


def _rs_sibling_copies(axes):
    n = len(axes)

    def make(bufs, send_sems, recv_sems):
        x, y, c = _my_place()
        sends = [_remote(_core_half(bufs[w], axes[w], 1 - c), bufs[n + w], send_sems, recv_sems, w, (x, y, 1 - c))
                 for w in range(n)]
        recvs = [_remote(bufs[n + w], bufs[n + w], send_sems, recv_sems, w, (x, y, 1 - c)) for w in range(n)]
        return sends, recvs

    return make


def _rs_chip_copies(axes):
    n = len(axes)

    def make(bufs, send_sems, recv_sems):
        x, y, c = _my_place()
        sends, recvs = [], []
        for j, (px, py) in enumerate(_peer_chips(x, y)):
            for w in range(n):
                k = j * n + w
                sends.append(_remote(_chip_block(bufs[w], axes[w], 2 * px + py), bufs[n + w].at[j],
                                     send_sems, recv_sems, k, (px, py, c)))
                recvs.append(_remote(bufs[n + w].at[j], bufs[n + w].at[j], send_sems, recv_sems, k, (px, py, c)))
        return sends, recvs

    return make


def _rs_fill_copies(axes):
    n = len(axes)

    def make(bufs, send_sems, recv_sems):
        x, y, c = _my_place()
        sends = [_remote(_core_half(bufs[w], axes[w], c), _core_half(bufs[w], axes[w], c),
                         send_sems, recv_sems, w, (x, y, 1 - c)) for w in range(n)]
        recvs = [_remote(_core_half(bufs[w], axes[w], 1 - c), _core_half(bufs[w], axes[w], 1 - c),
                         send_sems, recv_sems, w, (x, y, 1 - c)) for w in range(n)]
        return sends, recvs

    return make


def _chip_sum(g, r, axis, place, *, name):
    hk, hn = r.shape
    tk = 128 if hn > 4096 else _pick(hk, (256, 128))
    nk = hk // tk

    def body(p_ref, g_ref, r_ref, f_ref, b_ref):
        s = g_ref[...] + r_ref[...]
        f_ref[...] = s
        b_ref[...] = s.astype(BF16)

    half = pl.BlockSpec((tk, hn), lambda i, p: (i, 0))
    if axis == 0:
        g_spec = pl.BlockSpec((tk, hn), lambda i, p: (i, p[1]))
    else:
        g_spec = pl.BlockSpec((tk, hn), lambda i, p: (p[1] * nk + i, 0))
    grid_spec = pltpu.PrefetchScalarGridSpec(num_scalar_prefetch=1, grid=(nk,), in_specs=[g_spec, half],
                                             out_specs=[half, half])
    return pl.pallas_call(
        body, name=name,
        out_shape=[jax.ShapeDtypeStruct(r.shape, F32), jax.ShapeDtypeStruct(r.shape, BF16)],
        grid_spec=grid_spec,
        compiler_params=pltpu.CompilerParams(dimension_semantics=("arbitrary",),
                                             vmem_limit_bytes=_vmem(2 * tk * hn * 14)),
    )(place, g, r)


def _final_sum(own, recv, axis, place, *, name):
    _, bk, bn = recv.shape
    tk = _pick(bk, (256, 176, 128))
    nk = bk // tk

    def body(p_ref, o_ref, r_ref, out_ref):
        out_ref[...] = ((o_ref[...] + r_ref[0].astype(F32)) + r_ref[1].astype(F32)) + r_ref[2].astype(F32)

    if axis == 0:
        own_spec = pl.BlockSpec((tk, bn), lambda i, p: (p[0] * nk + i, 0))
        out_shape, out_spec = (bk, 2 * bn), pl.BlockSpec((tk, bn), lambda i, p: (i, p[1]))
    else:
        own_spec = pl.BlockSpec((tk, bn), lambda i, p: (i, p[0]))
        out_shape, out_spec = (2 * bk, bn), pl.BlockSpec((tk, bn), lambda i, p: (p[1] * nk + i, 0))
    grid_spec = pltpu.PrefetchScalarGridSpec(
        num_scalar_prefetch=1, grid=(nk,),
        in_specs=[own_spec, pl.BlockSpec((3, tk, bn), lambda i, p: (0, i, 0))], out_specs=out_spec)
    return pl.pallas_call(
        body, name=name, out_shape=jax.ShapeDtypeStruct(out_shape, F32), grid_spec=grid_spec,
        compiler_params=pltpu.CompilerParams(dimension_semantics=("arbitrary",),
                                             vmem_limit_bytes=_vmem(2 * tk * bn * 14 + 4 * tk * bn * 4)),
    )(place, own, recv)


def _allreduce_small(p):
    def body(p_ref, o_ref, r0, r1, r2, send_sems, recv_sems):
        x, y, c = _my_place()
        o_ref[...] = p_ref[...]
        for s, (peer, rbuf) in enumerate([((x, y, 1 - c), r0), ((1 - x, y, c), r1), ((x, 1 - y, c), r2)]):
            cp = pltpu.make_async_remote_copy(src_ref=o_ref, dst_ref=rbuf, send_sem=send_sems.at[s],
                                              recv_sem=recv_sems.at[s], device_id=peer, device_id_type=MESH)
            cp.start()
            cp.wait()
            o_ref[...] = o_ref[...] + rbuf[...]

    vm = pl.BlockSpec(memory_space=pltpu.VMEM)
    return pl.pallas_call(
        body, name="allreduce_small", out_shape=jax.ShapeDtypeStruct(p.shape, F32),
        in_specs=[vm], out_specs=vm,
        scratch_shapes=[pltpu.VMEM(p.shape, F32)] * 3 + [pltpu.SemaphoreType.DMA((3,))] * 2,
        compiler_params=pltpu.CompilerParams(vmem_limit_bytes=_vmem(6 * _nbytes(p.shape, F32))),
    )(p)


def _pack_rows(parts):
    rows, metas = [], []
    for a in parts:
        flat = a.reshape(-1)
        nrow = -(-flat.shape[0] // LANES)
        nrow = -(-nrow // 8) * 8
        flat = jnp.pad(flat, (0, nrow * LANES - flat.shape[0]))
        rows.append(flat.reshape(nrow, LANES))
        metas.append((a.shape, nrow))
    return jnp.concatenate(rows, axis=0), metas


def _unpack_rows(packed, metas):
    out, r0 = [], 0
    for shape, nrow in metas:
        size = int(np.prod(shape))
        out.append(packed[r0:r0 + nrow].reshape(-1)[:size].reshape(shape))
        r0 += nrow
    return out


def kernel(x, positions, pre_mix_g, post_mix_g, pre_ffn_g, post_ffn_g, a_w_in, a_b_in, a_ln_g, a_ln_b, a_w_s, a_b_s, a_w_out, b_w_qkv, b_b_qkv, b_sinks, b_w_o, ffn_w_gu, ffn_w_down, loss_target, m_pre_mix_g, m_post_mix_g, m_pre_ffn_g, m_post_ffn_g, m_a_w_in, m_a_b_in, m_a_ln_g, m_a_ln_b, m_a_w_s, m_a_b_s, m_a_w_out, m_b_w_qkv, m_b_b_qkv, m_b_sinks, m_b_w_o, m_ffn_w_gu, m_ffn_w_down, v_pre_mix_g, v_post_mix_g, v_pre_ffn_g, v_post_ffn_g, v_a_w_in, v_a_b_in, v_a_ln_g, v_a_ln_b, v_a_w_s, v_a_b_s, v_a_w_out, v_b_w_qkv, v_b_b_qkv, v_b_sinks, v_b_w_o, v_ffn_w_gu, v_ffn_w_down):
    depth, D = pre_mix_g.shape
    xi, yi, ci = _my_place()
    chip = 2 * xi + yi
    place = jnp.stack([chip, ci]).astype(jnp.int32)

    stacked = {"a_w_in": (a_w_in, m_a_w_in, v_a_w_in), "a_w_out": (a_w_out, m_a_w_out, v_a_w_out),
               "b_w_qkv": (b_w_qkv, m_b_w_qkv, v_b_w_qkv), "b_w_o": (b_w_o, m_b_w_o, v_b_w_o),
               "ffn_w_gu": (ffn_w_gu, m_ffn_w_gu, v_ffn_w_gu), "ffn_w_down": (ffn_w_down, m_ffn_w_down, v_ffn_w_down)}
    cut = {"a_w_in": 1, "a_w_out": 0, "b_w_qkv": 1, "b_w_o": 0, "ffn_w_gu": 1, "ffn_w_down": 0}

    def layer_keys(i):
        mix = [("a_w_in", i // 2), ("a_w_out", i // 2)] if i % 2 == 0 else [("b_w_qkv", i // 2), ("b_w_o", i // 2)]
        return mix + [("ffn_w_gu", i), ("ffn_w_down", i)]

    def dep(a, toks):
        for t in toks:
            a = a + t[:1, :1]
        return a

    W = {}
    for i in range(depth):
        for nm, l in layer_keys(i):
            W[(nm, l)] = _cast_block(stacked[nm][0], l, cut[nm], place, name=f"cast_{nm}_{l}")

    def gather(tag, keys, after):
        axes = [cut[nm] for nm, _ in keys]
        for stage in (1, 2):
            ss, rs, bufs, tok = _split_start(f"ag{stage}_start_{tag}", [W[k] for k in keys], 3 * len(keys),
                                             _ag_copies(stage, axes), after)
            after = yield tok
            bufs = _split_wait(f"ag{stage}_wait_{tag}", bufs, (ss, rs), _ag_copies(stage, axes), after)
            W.update(zip(keys, bufs))
        yield None

    nq = b_b_qkv.shape[1]
    bq_full = jnp.zeros((b_b_qkv.shape[0], N_CHIPS * nq), F32)
    bq_full = lax.dynamic_update_slice(bq_full, jnp.where(ci == 0, b_b_qkv, 0.0), (0, chip * nq))
    bq_packed, bq_meta = _pack_rows([bq_full])
    bq_gathered = _allreduce_small(bq_packed)
    b_qkv_full = _unpack_rows(bq_gathered, bq_meta)[0]

    first = gather("0m", layer_keys(0)[:2], bq_gathered)
    tok = next(first)
    tok = first.send(tok)
    first.send(tok)

    h = x[0]
    target = loss_target[0]
    ctab, stab = _rope_tables(positions[0])
    q_width = W[("b_w_o", 0)].shape[0]
    kv_width = N_KV_HEADS * HEAD_DIM
    row = lambda a, i: a[i:i + 1]

    saved = []
    for i in range(depth):
        j = i // 2
        s = {"h": h}
        toks = []
        ffn_w = None
        if i == 0:
            ffn_w = gather("0f", layer_keys(0)[2:], W[("a_w_out", 0)])
            toks.append(next(ffn_w))
            nxt = gather("1", layer_keys(1), toks[0])
            toks.append(next(nxt))
        elif i + 1 < depth:
            nxt = gather(str(i + 1), layer_keys(i + 1), h)
            toks.append(next(nxt))
        hn = _rms_fwd(h, dep(row(pre_mix_g, i), toks), out_dtype=BF16, name=f"rms_pre_mix_{i}")
        s["hn"] = hn
        if i % 2 == 0:
            pre = _matmul(hn, W[("a_w_in", j)], mode="nn", bias=row(a_b_in, j), out_dtype=F32, name=f"gmlp_in_{i}")
            gated = _sgu_fwd(pre, row(a_ln_g, j), row(a_ln_b, j), a_w_s[j], a_b_s[j].T, name=f"sgu_fwd_{i}")
            mix = _matmul(gated, W[("a_w_out", j)], mode="nn", out_dtype=F32, name=f"gmlp_out_{i}")
            s.update(pre=pre, gated=gated)
        else:
            qkv = _matmul(hn, W[("b_w_qkv", j)], mode="nn", bias=row(b_qkv_full, j), out_dtype=F32,
                          name=f"attn_qkv_{i}")
            qr, kr, vr = _rope_fwd(qkv, ctab, stab, q_width=q_width, kv_width=kv_width, name=f"rope_fwd_{i}")
            o = _attn_fwd(qr, kr, vr, row(b_sinks, j), name=f"attn_fwd_{i}")
            mix = _matmul(o, W[("b_w_o", j)], mode="nn", out_dtype=F32, name=f"attn_o_{i}")
            s.update(qr=qr, kr=kr, vr=vr, o=o)
        s["mix"] = mix
        toks = [ffn_w.send(mix)] if ffn_w else []
        h1 = _rms_res(h, mix, dep(row(post_mix_g, i), toks), name=f"rms_post_mix_{i}")
        if ffn_w:
            ffn_w.send(h1)
        s["h1"] = h1
        fn = _rms_fwd(h1, row(pre_ffn_g, i), out_dtype=BF16, name=f"rms_pre_ffn_{i}")
        g_pre, u_pre, act = _ffn_up(fn, W[("ffn_w_gu", i)][None], 0, name=f"ffn_up_{i}")
        f = _matmul(act, W[("ffn_w_down", i)], mode="nn", out_dtype=F32, name=f"ffn_down_{i}")
        toks = [nxt.send(f)] if i + 1 < depth else []
        h = _rms_res(h1, f, dep(row(post_ffn_g, i), toks), name=f"rms_post_ffn_{i}")
        if i + 1 < depth:
            nxt.send(h)
        s.update(fn=fn, g_pre=g_pre, u_pre=u_pre, act=act, f=f)
        saved.append(s)

    dh, loss_part = _loss_and_grad(h, target, name="loss")
    loss = lax.psum(loss_part[0, 0], ("x", "y", "c"))

    big_out = {nm: tuple(lax.empty(w.shape, F32) for _ in range(4)) for nm, (w, _, _) in stacked.items()}

    def reduce_group(i, keys, grads):
        axes = [cut[nm] for nm, _ in keys]
        n = len(keys)
        lands = [lax.empty(_half_shape(g.shape, ax), F32) for g, ax in zip(grads, axes)]
        ss, rs, bufs, tok = _split_start(f"rs_sibling_start_{i}", list(grads) + lands, n, _rs_sibling_copies(axes),
                                         place)
        after = yield tok
        bufs = _split_wait(f"rs_sibling_wait_{i}", bufs, (ss, rs), _rs_sibling_copies(axes), after)
        sums = [_chip_sum(bufs[w], bufs[n + w], axes[w], place, name=f"chip_sum_{keys[w][0]}_{keys[w][1]}")
                for w in range(n)]
        lands = []
        for (sf, _), ax in zip(sums, axes):
            hk, hn = sf.shape
            lands.append(lax.empty((3, hk // N_CHIPS, hn) if ax == 0 else (3, hk, hn // N_CHIPS), BF16))
        ss, rs, bufs, tok = _split_start(f"rs_chip_start_{i}", [sb for _, sb in sums] + lands, 3 * n,
                                         _rs_chip_copies(axes), place)
        after = yield tok
        bufs = _split_wait(f"rs_chip_wait_{i}", bufs, (ss, rs), _rs_chip_copies(axes), after)
        blocks = [_final_sum(sums[w][0], bufs[n + w], axes[w], place, name=f"final_sum_{keys[w][0]}_{keys[w][1]}")
                  for w in range(n)]
        ss, rs, bufs, tok = _split_start(f"rs_fill_start_{i}", blocks, n, _rs_fill_copies(axes), place)
        after = yield tok
        blocks = _split_wait(f"rs_fill_wait_{i}", bufs, (ss, rs), _rs_fill_copies(axes), after)
        for (nm, l), g in zip(keys, blocks):
            w, m, v = stacked[nm]
            big_out[nm] = tuple(_adamw_layer(w, m, v, g, l, big_out[nm], name=f"adamw_{nm}_{l}"))
        yield None

    reducing = []

    def advance(after):
        toks = []
        for gen in list(reducing):
            tok = gen.send(after)
            if tok is None:
                reducing.remove(gen)
            else:
                toks.append(tok)
        return toks

    small = {}
    g_pre_mix, g_post_mix, g_pre_ffn, g_post_ffn = [None] * depth, [None] * depth, [None] * depth, [None] * depth
    toks = []
    for i in reversed(range(depth)):
        j = i // 2
        s = saved[i]
        df, g_post_ffn[i] = _rms_bwd(s["f"], dep(row(post_ffn_g, i), toks), dh, None, out_dtype=BF16,
                                     name=f"rms_post_ffn_bwd_{i}")
        g_down = _matmul(s["act"], df, mode="tn", out_dtype=F32, name=f"ffn_down_dw_{i}")
        dg_, du_ = _ffn_down_dx(df, W[("ffn_w_down", i)][None], 0, s["g_pre"], s["u_pre"], name=f"ffn_down_dx_{i}")
        hid = dg_.shape[1]
        tile = _pick(hid, (1408, 768, 512, 256, 128))
        w_gu = W[("ffn_w_gu", i)]
        g_gu = lax.empty(w_gu.shape, F32)
        g_gu = _matmul(s["fn"], dg_, mode="tn", into=g_gu, tq=tile, out_dtype=F32, name=f"ffn_g_dw_{i}")
        g_gu = _matmul(s["fn"], du_, mode="tn", into=g_gu, tq=tile, q_off=hid // tile, out_dtype=F32,
                       name=f"ffn_u_dw_{i}")
        dfn_g = _matmul(dg_, w_gu, mode="nt", tr=hid, out_dtype=F32, name=f"ffn_g_dx_{i}")
        dfn = _matmul(du_, w_gu, mode="nt", tr=hid, b_r_off=1, bias=dfn_g, out_dtype=F32, name=f"ffn_u_dx_{i}")
        toks = advance(dfn)
        if i == 0:
            gen = reduce_group("0f", layer_keys(0)[2:], [g_gu, g_down])
            toks.append(next(gen))
            reducing.append(gen)
        dh1, g_pre_ffn[i] = _rms_bwd(s["h1"], dep(row(pre_ffn_g, i), toks), dfn, dh, out_dtype=F32,
                                     name=f"rms_pre_ffn_bwd_{i}")
        dmix, g_post_mix[i] = _rms_bwd(s["mix"], row(post_mix_g, i), dh1, None, out_dtype=BF16,
                                       name=f"rms_post_mix_bwd_{i}")
        if i % 2 == 0:
            g_out = _matmul(s["gated"], dmix, mode="tn", out_dtype=F32, name=f"gmlp_out_dw_{i}")
            dgated = _matmul(dmix, W[("a_w_out", j)], mode="nt", out_dtype=F32, name=f"gmlp_out_dx_{i}")
            if i == 0:
                advance(dgated)
            dpre, dws, dbsT, dlng, dlnb, dbin = _sgu_bwd(s["pre"], dgated, row(a_ln_g, j), row(a_ln_b, j),
                                                         a_w_s[j], a_b_s[j].T, name=f"sgu_bwd_{i}")
            small[("a_w_s", j)] = dws
            small[("a_b_s", j)] = dbsT.T
            small[("a_ln_g", j)] = dlng
            small[("a_ln_b", j)] = dlnb
            small[("a_b_in", j)] = dbin
            g_in = _matmul(s["hn"], dpre, mode="tn", out_dtype=F32, name=f"gmlp_in_dw_{i}")
            dhn = _matmul(dpre, W[("a_w_in", j)], mode="nt", out_dtype=F32, name=f"gmlp_in_dx_{i}")
        else:
            g_out = _matmul(s["o"], dmix, mode="tn", out_dtype=F32, name=f"attn_o_dw_{i}")
            do = _matmul(dmix, W[("b_w_o", j)], mode="nt", out_dtype=BF16, name=f"attn_o_dx_{i}")
            dq, dkp, dkc, dvp, dvc, dsk = _attn_bwd(s["qr"], s["kr"], s["vr"], row(b_sinks, j), do,
                                                    name=f"attn_bwd_{i}")
            dqkv, dbq = _rope_bwd(dq, dkp, dkc, dvp, dvc, ctab, stab, name=f"rope_bwd_{i}")
            small[("b_sinks", j)] = dsk[:, :b_sinks.shape[1]]
            small[("b_b_qkv", j)] = dbq
            g_in = _matmul(s["hn"], dqkv, mode="tn", out_dtype=F32, name=f"attn_qkv_dw_{i}")
            dhn = _matmul(dqkv, W[("b_w_qkv", j)], mode="nt", out_dtype=F32, name=f"attn_qkv_dx_{i}")
        toks = advance(dhn)
        dh, g_pre_mix[i] = _rms_bwd(s["h"], dep(row(pre_mix_g, i), toks), dhn, dh1, out_dtype=F32,
                                    name=f"rms_pre_mix_bwd_{i}")
        if i == 0:
            gen = reduce_group("0m", layer_keys(0)[:2], [g_in, g_out])
        else:
            gen = reduce_group(str(i), layer_keys(i), [g_in, g_out, g_gu, g_down])
        toks = [next(gen)] + advance(dh)
        reducing.append(gen)
    grad_x = dh[None]

    toks = advance(dh)
    n_a, n_b = a_b_in.shape[0], b_sinks.shape[0]
    stack = lambda key, n: jnp.concatenate([small[(key, j)] for j in range(n)], axis=0)
    small_parts = [
        jnp.concatenate(g_pre_mix, axis=0), jnp.concatenate(g_post_mix, axis=0),
        jnp.concatenate(g_pre_ffn, axis=0), jnp.concatenate(g_post_ffn, axis=0),
        stack("a_b_in", n_a), stack("a_ln_g", n_a), stack("a_ln_b", n_a),
        jnp.stack([small[("a_w_s", j)] for j in range(n_a)]), jnp.stack([small[("a_b_s", j)] for j in range(n_a)]),
        stack("b_b_qkv", n_b), stack("b_sinks", n_b),
    ]
    packed, metas = _pack_rows(small_parts)
    reduced = _allreduce_small(dep(packed, toks))
    while reducing:
        advance(reduced)
    red = _unpack_rows(reduced, metas)
    (gr_pre_mix, gr_post_mix, gr_pre_ffn, gr_post_ffn, gr_b_in, gr_ln_g, gr_ln_b, gr_w_s, gr_b_s,
     gr_b_qkv_full, gr_sinks) = red
    gr_b_qkv = lax.dynamic_slice(gr_b_qkv_full, (0, chip * nq), (gr_b_qkv_full.shape[0], nq))

    grads = {"pre_mix_g": gr_pre_mix, "post_mix_g": gr_post_mix, "pre_ffn_g": gr_pre_ffn, "post_ffn_g": gr_post_ffn,
             "a_b_in": gr_b_in, "a_ln_g": gr_ln_g, "a_ln_b": gr_ln_b, "a_w_s": gr_w_s, "a_b_s": gr_b_s,
             "b_b_qkv": gr_b_qkv, "b_sinks": gr_sinks}
    weights = {"pre_mix_g": (pre_mix_g, m_pre_mix_g, v_pre_mix_g), "post_mix_g": (post_mix_g, m_post_mix_g, v_post_mix_g),
               "pre_ffn_g": (pre_ffn_g, m_pre_ffn_g, v_pre_ffn_g), "post_ffn_g": (post_ffn_g, m_post_ffn_g, v_post_ffn_g),
               "a_b_in": (a_b_in, m_a_b_in, v_a_b_in), "a_ln_g": (a_ln_g, m_a_ln_g, v_a_ln_g),
               "a_ln_b": (a_ln_b, m_a_ln_b, v_a_ln_b), "a_w_s": (a_w_s, m_a_w_s, v_a_w_s), "a_b_s": (a_b_s, m_a_b_s, v_a_b_s),
               "b_b_qkv": (b_b_qkv, m_b_b_qkv, v_b_b_qkv), "b_sinks": (b_sinks, m_b_sinks, v_b_sinks)}
    order = ["pre_mix_g", "post_mix_g", "pre_ffn_g", "post_ffn_g", "a_w_in", "a_b_in", "a_ln_g", "a_ln_b", "a_w_s",
             "a_b_s", "a_w_out", "b_w_qkv", "b_b_qkv", "b_sinks", "b_w_o", "ffn_w_gu", "ffn_w_down"]
    deltas, new_m, new_v = {}, {}, {}
    for nm in order:
        if nm in big_out:
            grads[nm], deltas[nm], new_m[nm], new_v[nm] = big_out[nm]
        else:
            w, m, v = weights[nm]
            deltas[nm], new_m[nm], new_v[nm] = _adamw_small(w, grads[nm], m, v, name="adamw_" + nm)
    return (loss, grad_x, *[grads[nm] for nm in order], *[deltas[nm] for nm in order],
            *[new_m[nm] for nm in order], *[new_v[nm] for nm in order])
```

```python
import functools
import math

import jax
import jax.numpy as jnp
import numpy as np
from jax import lax
from jax.experimental import pallas as pl
from jax.experimental.pallas import tpu as pltpu

F32 = jnp.float32
BF16 = jnp.bfloat16
MESH = pl.DeviceIdType.MESH

HEAD_DIM = 64
N_KV_HEADS = 4
ROPE_DIM = 16
ROPE_THETA = 500000.0
CHUNK = 128
GMLP_GROUPS = 8
RMS_EPS = 1e-6
LN_EPS = 1e-5
NEG_INF = -1e30
ADAM_LR = 0.001
ADAM_B1 = 0.9
ADAM_B2 = 0.999
ADAM_EPS = 1e-08
ADAM_WD = 0.01
ADAM_STEP = 10

N_CHIPS = 4
LANES = 128
VMEM_CAP = 58 * 1024 * 1024


def _vmem(est_bytes):
    assert est_bytes < VMEM_CAP
    return VMEM_CAP


def _pick(n, cands):
    for c in cands:
        if c <= n and n % c == 0:
            return c
    return n


def _nbytes(shape, dtype):
    return int(np.prod(shape)) * jnp.dtype(dtype).itemsize


MATMUL_VMEM_BUDGET = 48 * 1024 * 1024


def _halvings(n, unit):
    out, t = [], n
    while t % unit == 0 and t >= unit:
        out.append(t)
        if t % 2:
            break
        t //= 2
    return out


def _matmul_tiles(P, Q, R, a_bytes, b_bytes, o_bytes, full_addend, tp, tq, tr):
    step_us, bytes_per_us = 0.85, 3.2e6
    best = None
    for p in ([tp] if tp else _halvings(P, LANES)):
        for q in ([tq] if tq else _halvings(Q, LANES)):
            for r in ([tr] if tr else _halvings(R, LANES)):
                nk = R // r
                vm = 2 * (p * r * a_bytes + r * q * b_bytes + p * q * o_bytes + (p * q * 4 if full_addend else 0))
                vm += p * q * 4 * (2 if nk > 1 else 1)
                if vm > MATMUL_VMEM_BUDGET:
                    continue
                exposed = (p * r * a_bytes + r * q * b_bytes + p * q * o_bytes) / bytes_per_us
                key = ((P // p) * (Q // q) * nk * step_us + exposed, nk, abs(p - q))
                if best is None or key < best[0]:
                    best = (key, (p, q, r))
    assert best is not None, (P, Q, R)
    return best[1]


def _matmul(a, b, *, mode, out_dtype, name, a_l=None, b_l=None, bias=None, into=None, o_l=None,
            q_off=0, b_r_off=0, tp=None, tq=None, tr=None):
    a2 = a.shape[-2:]
    b2 = b.shape[-2:]
    if mode == "nn":
        (P, R), (R2, Q) = a2, b2
    elif mode == "nt":
        (P, R), (Q, R2) = a2, b2
    else:
        (R, P), (R2, Q) = a2, b2
    assert R == R2 or (mode == "nt" and R2 % R == 0), (mode, a.shape, b.shape)
    o_bytes = jnp.dtype(into.dtype if into is not None else out_dtype).itemsize
    full_addend = bias is not None and bias.shape[0] != 1
    tp, tq, tr = _matmul_tiles(P, Q, R, a.dtype.itemsize, b.dtype.itemsize, o_bytes, full_addend, tp, tq, tr)
    assert P % tp == 0 and Q % tq == 0 and R % tr == 0
    nk = R // tr
    dims = {"nn": (((1,), (0,)), ((), ())), "nt": (((1,), (1,)), ((), ())), "tn": (((0,), (0,)), ((), ()))}[mode]

    def lead(l, blk, idx):
        if l is None:
            return pl.BlockSpec(blk, idx)
        return pl.BlockSpec((None,) + blk, lambda i, j, k: (l,) + idx(i, j, k))

    if mode == "nn":
        a_spec = lead(a_l, (tp, tr), lambda i, j, k: (i, k))
        b_spec = lead(b_l, (tr, tq), lambda i, j, k: (k, j))
    elif mode == "nt":
        a_spec = lead(a_l, (tp, tr), lambda i, j, k: (i, k))
        b_spec = lead(b_l, (tq, tr), lambda i, j, k: (j, k + b_r_off))
    else:
        a_spec = lead(a_l, (tr, tp), lambda i, j, k: (k, i))
        b_spec = lead(b_l, (tr, tq), lambda i, j, k: (k, j))
    in_specs = [a_spec, b_spec]
    args = [a, b]
    if bias is not None:
        if bias.shape[0] == 1:
            in_specs.append(pl.BlockSpec((1, tq), lambda i, j, k: (0, j)))
        else:
            in_specs.append(pl.BlockSpec((tp, tq), lambda i, j, k: (i, j)))
        args.append(bias)
    aliases = {}
    if into is not None:
        in_specs.append(pl.BlockSpec(memory_space=pl.ANY))
        args.append(into)
        aliases = {len(args) - 1: 0}
        out_shape = jax.ShapeDtypeStruct(into.shape, into.dtype)
        out_dtype = into.dtype
        if o_l is None:
            out_spec = pl.BlockSpec((tp, tq), lambda i, j, k: (i, j + q_off))
        else:
            out_spec = pl.BlockSpec((None, tp, tq), lambda i, j, k: (o_l, i, j + q_off))
    else:
        out_shape = jax.ShapeDtypeStruct((P, Q), out_dtype)
        out_spec = pl.BlockSpec((tp, tq), lambda i, j, k: (i, j))
    has_bias = bias is not None
    has_into = into is not None

    def body(*refs):
        a_ref, b_ref = refs[0], refs[1]
        pos = 2
        bias_ref = None
        if has_bias:
            bias_ref = refs[pos]
            pos += 1
        if has_into:
            pos += 1
        o_ref = refs[pos]
        acc_ref = refs[pos + 1] if nk > 1 else None
        part = lax.dot_general(a_ref[...], b_ref[...], dims, preferred_element_type=F32)

        def finish(acc):
            if has_bias:
                acc = acc + bias_ref[...]
            o_ref[...] = acc.astype(out_dtype)

        if nk == 1:
            finish(part)
        else:
            k = pl.program_id(2)

            @pl.when(k == 0)
            def _():
                acc_ref[...] = part

            @pl.when(k > 0)
            def _():
                acc_ref[...] += part

            @pl.when(k == nk - 1)
            def _():
                finish(acc_ref[...])

    est = 2 * (_nbytes((tp, tr), a.dtype) + _nbytes((tr, tq), b.dtype) + _nbytes((tp, tq), out_dtype)) + 3 * tp * tq * 4
    return pl.pallas_call(
        body, name=name, out_shape=out_shape,
        grid=(P // tp, Q // tq, nk),
        in_specs=in_specs, out_specs=out_spec,
        scratch_shapes=[pltpu.VMEM((tp, tq), F32)] if nk > 1 else [],
        input_output_aliases=aliases,
        compiler_params=pltpu.CompilerParams(
            dimension_semantics=("parallel", "parallel", "arbitrary"), vmem_limit_bytes=_vmem(est)),
    )(*args)


def _row_call(body, ins, outs, *, name, rows, tr, acc_outs=(), est=0):
    in_specs = []
    for arr, kind in ins:
        if kind == "row":
            in_specs.append(pl.BlockSpec((tr, arr.shape[1]), lambda i: (i, 0)))
        else:
            nd = arr.ndim
            in_specs.append(pl.BlockSpec(arr.shape, lambda i, nd=nd: (0,) * nd))
    out_shapes = [jax.ShapeDtypeStruct(s, d) for s, d in outs] + [jax.ShapeDtypeStruct(s, d) for s, d in acc_outs]
    out_specs = [pl.BlockSpec((tr, s[1]), lambda i: (i, 0)) for s, _ in outs]
    out_specs += [pl.BlockSpec(s, lambda i, nd=len(s): (0,) * nd) for s, _ in acc_outs]
    res = pl.pallas_call(
        body, name=name, out_shape=out_shapes, grid=(rows // tr,), in_specs=in_specs, out_specs=out_specs,
        compiler_params=pltpu.CompilerParams(dimension_semantics=("arbitrary",), vmem_limit_bytes=_vmem(est)),
    )(*[a for a, _ in ins])
    return res


def _rms_fwd(x, g, *, out_dtype, name):
    T, D = x.shape
    tr = _pick(T, (512, 256, 128))

    def body(x_ref, g_ref, o_ref):
        xv = x_ref[...]
        r = lax.rsqrt(jnp.mean(xv * xv, axis=-1, keepdims=True) + RMS_EPS)
        o_ref[...] = (xv * r * g_ref[...]).astype(out_dtype)

    return _row_call(body, [(x, "row"), (g, "full")], [((T, D), out_dtype)], name=name, rows=T, tr=tr,
                     est=8 * tr * D * 4)[0]


def _rms_res(h, y, g, *, name):
    T, D = h.shape
    tr = _pick(T, (512, 256, 128))

    def body(h_ref, y_ref, g_ref, o_ref):
        yv = y_ref[...]
        r = lax.rsqrt(jnp.mean(yv * yv, axis=-1, keepdims=True) + RMS_EPS)
        o_ref[...] = h_ref[...] + yv * r * g_ref[...]

    return _row_call(body, [(h, "row"), (y, "row"), (g, "full")], [((T, D), F32)], name=name, rows=T, tr=tr,
                     est=10 * tr * D * 4)[0]


def _rms_bwd(x, g, dy, dres, *, out_dtype, name):
    T, D = x.shape
    tr = _pick(T, (512, 256, 128))
    has_res = dres is not None

    def body(*refs):
        if has_res:
            x_ref, g_ref, dy_ref, dr_ref, dx_ref, dg_ref = refs
        else:
            x_ref, g_ref, dy_ref, dx_ref, dg_ref = refs
        xv = x_ref[...]
        r = lax.rsqrt(jnp.mean(xv * xv, axis=-1, keepdims=True) + RMS_EPS)
        xhat = xv * r
        dyv = dy_ref[...].astype(F32)
        dxn = dyv * g_ref[...]
        dx = r * (dxn - xhat * jnp.mean(dxn * xhat, axis=-1, keepdims=True))
        if has_res:
            dx = dx + dr_ref[...]
        dx_ref[...] = dx.astype(out_dtype)
        part = jnp.sum(dyv * xhat, axis=0, keepdims=True)

        @pl.when(pl.program_id(0) == 0)
        def _():
            dg_ref[...] = part

        @pl.when(pl.program_id(0) > 0)
        def _():
            dg_ref[...] += part

    ins = [(x, "row"), (g, "full"), (dy, "row")] + ([(dres, "row")] if has_res else [])
    dx, dg = _row_call(body, ins, [((T, D), out_dtype)], name=name, rows=T, tr=tr, acc_outs=[((1, D), F32)],
                       est=12 * tr * D * 4)
    return dx, dg


def _ffn_up(fn, w_gu, l, *, name):
    T, D = fn.shape
    H = w_gu.shape[2] // 2
    tp = _pick(T, (1024, 512, 256, 128))
    tq = _pick(H, (1408, 768, 512, 256, 128))
    nj = H // tq

    def body(a_ref, wg_ref, wu_ref, g_ref, u_ref, act_ref):
        a = a_ref[...]
        g = jnp.dot(a, wg_ref[...], preferred_element_type=F32)
        u = jnp.dot(a, wu_ref[...], preferred_element_type=F32)
        g_ref[...] = g.astype(BF16)
        u_ref[...] = u.astype(BF16)
        act_ref[...] = (g * jax.nn.sigmoid(g) * u).astype(BF16)

    tile = pl.BlockSpec((tp, tq), lambda j, i: (i, j))
    est = 2 * (tp * D * 2 + 2 * D * tq * 2 + 3 * tp * tq * 2) + 4 * tp * tq * 4
    return pl.pallas_call(
        body, name=name,
        out_shape=[jax.ShapeDtypeStruct((T, H), BF16), jax.ShapeDtypeStruct((T, H), BF16),
                   jax.ShapeDtypeStruct((T, H), BF16)],
        grid=(nj, T // tp),
        in_specs=[pl.BlockSpec((tp, D), lambda j, i: (i, 0)),
                  pl.BlockSpec((None, D, tq), lambda j, i: (l, 0, j)),
                  pl.BlockSpec((None, D, tq), lambda j, i: (l, 0, j + nj))],
        out_specs=[tile, tile, tile],
        compiler_params=pltpu.CompilerParams(dimension_semantics=("parallel", "parallel"),
                                             vmem_limit_bytes=_vmem(est)),
    )(fn, w_gu, w_gu)


def _ffn_down_dx(df, w_down, l, g, u, *, name):
    T, D = df.shape
    H = w_down.shape[1]
    tp = _pick(T, (512, 256, 128))
    tq = _pick(H, (1408, 768, 512, 256, 128))

    def body(a_ref, w_ref, g_ref, u_ref, dg_ref, du_ref):
        da = lax.dot_general(a_ref[...], w_ref[...], (((1,), (1,)), ((), ())), preferred_element_type=F32)
        gv = g_ref[...].astype(F32)
        sg = jax.nn.sigmoid(gv)
        silu = gv * sg
        dg_ref[...] = (da * u_ref[...].astype(F32) * (sg + silu * (1.0 - sg))).astype(BF16)
        du_ref[...] = (da * silu).astype(BF16)

    tile = pl.BlockSpec((tp, tq), lambda j, i: (i, j))
    est = 2 * (tp * D * 2 + tq * D * 2 + 4 * tp * tq * 2) + 5 * tp * tq * 4
    return pl.pallas_call(
        body, name=name,
        out_shape=[jax.ShapeDtypeStruct((T, H), BF16), jax.ShapeDtypeStruct((T, H), BF16)],
        grid=(H // tq, T // tp),
        in_specs=[pl.BlockSpec((tp, D), lambda j, i: (i, 0)),
                  pl.BlockSpec((None, tq, D), lambda j, i: (l, j, 0)), tile, tile],
        out_specs=[tile, tile],
        compiler_params=pltpu.CompilerParams(dimension_semantics=("parallel", "parallel"),
                                             vmem_limit_bytes=_vmem(est)),
    )(df, w_down, g, u)


def _loss_and_grad(y, target, *, name):
    T, D = y.shape
    tr = _pick(T, (512, 256, 128))

    def body(y_ref, t_ref, dy_ref, l_ref):
        e = y_ref[...] - t_ref[...]
        dy_ref[...] = e * (1.0 / D)
        part = jnp.sum(jnp.sum(e * e, axis=1, keepdims=True), axis=0, keepdims=True) * (0.5 / D)

        @pl.when(pl.program_id(0) == 0)
        def _():
            l_ref[...] = part

        @pl.when(pl.program_id(0) > 0)
        def _():
            l_ref[...] += part

    dy, l = _row_call(body, [(y, "row"), (target, "row")], [((T, D), F32)], name=name, rows=T, tr=tr,
                      acc_outs=[((1, 1), F32)], est=8 * tr * D * 4)
    return dy, l


_SQRT_HALF = 0.7071067811865476
_INV_SQRT_2PI = 0.3989422804014327


def _gelu_parts(x):
    cdf = 0.5 * (1.0 + lax.erf(x * _SQRT_HALF))
    return cdf


def _sgu_common(pre, lng, lnb, W):
    cdf = _gelu_parts(pre)
    z = pre * cdf
    u = z[:, :W]
    v = z[:, W:]
    mu = jnp.mean(v, axis=-1, keepdims=True)
    vc = v - mu
    var = jnp.mean(vc * vc, axis=-1, keepdims=True)
    rstd = lax.rsqrt(var + LN_EPS)
    vhat = vc * rstd
    vn = vhat * lng + lnb
    return cdf, u, vhat, rstd, vn


def _causal_mask():
    t = lax.broadcasted_iota(jnp.int32, (CHUNK, CHUNK), 0)
    s = lax.broadcasted_iota(jnp.int32, (CHUNK, CHUNK), 1)
    return t >= s


def _sgu_fwd(pre, lng, lnb, ws, bsT, *, name):
    T, W2 = pre.shape
    W = W2 // 2
    G = ws.shape[0]
    gd = W // G

    def body(pre_ref, lng_ref, lnb_ref, ws_ref, bs_ref, o_ref):
        _, u, _, _, vn = _sgu_common(pre_ref[...], lng_ref[...], lnb_ref[...], W)
        vnb = vn.astype(BF16)
        causal = _causal_mask()
        for g in range(G):
            w = jnp.where(causal, ws_ref[g], 0.0).astype(BF16)
            sv = jnp.dot(w, vnb[:, g * gd:(g + 1) * gd], preferred_element_type=F32) + bs_ref[:, g:g + 1]
            o_ref[:, g * gd:(g + 1) * gd] = (u[:, g * gd:(g + 1) * gd] * sv).astype(BF16)

    return pl.pallas_call(
        body, name=name, out_shape=jax.ShapeDtypeStruct((T, W), BF16), grid=(T // CHUNK,),
        in_specs=[pl.BlockSpec((CHUNK, W2), lambda i: (i, 0)),
                  pl.BlockSpec((1, W), lambda i: (0, 0)), pl.BlockSpec((1, W), lambda i: (0, 0)),
                  pl.BlockSpec(ws.shape, lambda i: (0, 0, 0)), pl.BlockSpec(bsT.shape, lambda i: (0, 0))],
        out_specs=pl.BlockSpec((CHUNK, W), lambda i: (i, 0)),
        compiler_params=pltpu.CompilerParams(dimension_semantics=("arbitrary",),
                                             vmem_limit_bytes=_vmem(12 * CHUNK * W2 * 4)),
    )(pre, lng, lnb, ws, bsT)


def _sgu_bwd(pre, dgated, lng, lnb, ws, bsT, *, name):
    T, W2 = pre.shape
    W = W2 // 2
    G = ws.shape[0]
    gd = W // G

    def body(pre_ref, dgt_ref, lng_ref, lnb_ref, ws_ref, bs_ref,
             dpre_ref, dws_ref, dbs_ref, dlng_ref, dlnb_ref, dbin_ref):
        first = pl.program_id(0) == 0

        @pl.when(first)
        def _():
            dws_ref[...] = jnp.zeros_like(dws_ref)
            dbs_ref[...] = jnp.zeros_like(dbs_ref)
            dlng_ref[...] = jnp.zeros_like(dlng_ref)
            dlnb_ref[...] = jnp.zeros_like(dlnb_ref)
            dbin_ref[...] = jnp.zeros_like(dbin_ref)

        pre_v = pre_ref[...]
        lng_v = lng_ref[...]
        cdf, u, vhat, rstd, vn = _sgu_common(pre_v, lng_v, lnb_ref[...], W)
        vnb = vn.astype(BF16)
        dgt = dgt_ref[...].astype(F32)
        causal = _causal_mask()
        du_parts, dvn_parts = [], []
        for g in range(G):
            sl = slice(g * gd, (g + 1) * gd)
            w = jnp.where(causal, ws_ref[g], 0.0).astype(BF16)
            sv = jnp.dot(w, vnb[:, sl], preferred_element_type=F32) + bs_ref[:, g:g + 1]
            dgt_g = dgt[:, sl]
            du_parts.append(dgt_g * sv)
            dsv = dgt_g * u[:, sl]
            dsvb = dsv.astype(BF16)
            dvn_parts.append(lax.dot_general(w, dsvb, (((0,), (0,)), ((), ())), preferred_element_type=F32))
            dw = lax.dot_general(dsvb, vnb[:, sl], (((1,), (1,)), ((), ())), preferred_element_type=F32)
            dws_ref[g] += jnp.where(causal, dw, 0.0)
            dbs_ref[:, g:g + 1] += jnp.sum(dsv, axis=1, keepdims=True)
        du = jnp.concatenate(du_parts, axis=1)
        dvn = jnp.concatenate(dvn_parts, axis=1)
        dlng_ref[...] += jnp.sum(dvn * vhat, axis=0, keepdims=True)
        dlnb_ref[...] += jnp.sum(dvn, axis=0, keepdims=True)
        dvh = dvn * lng_v
        dv = rstd * (dvh - jnp.mean(dvh, axis=-1, keepdims=True)
                     - vhat * jnp.mean(dvh * vhat, axis=-1, keepdims=True))
        dz = jnp.concatenate([du, dv], axis=1)
        dgelu = cdf + pre_v * jnp.exp(-0.5 * pre_v * pre_v) * _INV_SQRT_2PI
        dpre = dz * dgelu
        dbin_ref[...] += jnp.sum(dpre, axis=0, keepdims=True)
        dpre_ref[...] = dpre.astype(BF16)

    full = lambda shape: pl.BlockSpec(shape, lambda i, nd=len(shape): (0,) * nd)
    return pl.pallas_call(
        body, name=name,
        out_shape=[jax.ShapeDtypeStruct((T, W2), BF16), jax.ShapeDtypeStruct(ws.shape, F32),
                   jax.ShapeDtypeStruct(bsT.shape, F32), jax.ShapeDtypeStruct((1, W), F32),
                   jax.ShapeDtypeStruct((1, W), F32), jax.ShapeDtypeStruct((1, W2), F32)],
        grid=(T // CHUNK,),
        in_specs=[pl.BlockSpec((CHUNK, W2), lambda i: (i, 0)), pl.BlockSpec((CHUNK, W), lambda i: (i, 0)),
                  full((1, W)), full((1, W)), full(ws.shape), full(bsT.shape)],
        out_specs=[pl.BlockSpec((CHUNK, W2), lambda i: (i, 0)), full(ws.shape), full(bsT.shape),
                   full((1, W)), full((1, W)), full((1, W2))],
        compiler_params=pltpu.CompilerParams(dimension_semantics=("arbitrary",),
                                             vmem_limit_bytes=_vmem(24 * CHUNK * W2 * 4)),
    )(pre, dgated, lng, lnb, ws, bsT)


def _rope_tables(positions):
    half = ROPE_DIM // 2
    inv_freq = ROPE_THETA ** (-jnp.arange(0, ROPE_DIM, 2, dtype=F32) / ROPE_DIM)
    ang = positions.astype(F32).reshape(-1, 1) * inv_freq
    cos, sin = jnp.cos(ang), jnp.sin(ang)
    T = ang.shape[0]
    rest = HEAD_DIM - ROPE_DIM
    c64 = jnp.concatenate([cos, cos, jnp.ones((T, rest), F32)], axis=1)
    s64 = jnp.concatenate([-sin, sin, jnp.zeros((T, rest), F32)], axis=1)
    del half
    return jnp.tile(c64, (1, LANES // HEAD_DIM)), jnp.tile(s64, (1, LANES // HEAD_DIM))


def _swap8(x):
    W = x.shape[1]
    half = ROPE_DIM // 2
    lane = lax.broadcasted_iota(jnp.int32, x.shape, 1) % HEAD_DIM
    return jnp.where(lane < half, pltpu.roll(x, W - half, axis=1),
                     jnp.where(lane < ROPE_DIM, pltpu.roll(x, half, axis=1), 0.0))


def _wide(tab, W):
    return jnp.concatenate([tab] * (W // LANES), axis=1) if W > LANES else tab


def _rope_fwd(qkv, ctab, stab, *, q_width, kv_width, name):
    T = qkv.shape[0]
    tr = _pick(T, (256, 128))
    scale = HEAD_DIM ** -0.5

    def body(x_ref, c_ref, s_ref, q_ref, k_ref, v_ref):
        c = c_ref[...]
        s = s_ref[...]
        q = x_ref[:, :q_width]
        k = x_ref[:, q_width:q_width + kv_width]
        q_ref[...] = ((q * _wide(c, q_width) + _swap8(q) * _wide(s, q_width)) * scale).astype(BF16)
        k_ref[...] = (k * _wide(c, kv_width) + _swap8(k) * _wide(s, kv_width)).astype(BF16)
        v_ref[...] = x_ref[:, q_width + kv_width:].astype(BF16)

    return _row_call(body, [(qkv, "row"), (ctab, "row"), (stab, "row")],
                     [((T, q_width), BF16), ((T, kv_width), BF16), ((T, kv_width), BF16)],
                     name=name, rows=T, tr=tr, est=10 * tr * qkv.shape[1] * 4)


_NT = (((1,), (1,)), ((), ()))
_TN = (((0,), (0,)), ((), ()))


def _group_rows(ref, heads):
    return jnp.concatenate([ref[:, h * HEAD_DIM:(h + 1) * HEAD_DIM] for h in heads], axis=0)


def _attn_valid(grp):
    qi = np.arange(grp * CHUNK)[:, None] % CHUNK
    sj = np.arange(2 * CHUNK)[None, :]
    cur = (sj >= CHUNK) & (sj - CHUNK <= qi)
    prev = (sj < CHUNK) & (sj > qi)
    return jnp.asarray(np.stack([cur, cur | prev]).astype(np.float32))


def _valid_spec(grp):
    return pl.BlockSpec((None, grp * CHUNK, 2 * CHUNK), lambda n: (jnp.minimum(n, 1), 0, 0))


def _attn_group_probs(q, kk, sinks, valid, grp):
    rows = grp * CHUNK
    s = lax.dot_general(q, kk, _NT, preferred_element_type=F32)
    s = jnp.where(valid, s, NEG_INF)
    r = lax.broadcasted_iota(jnp.int32, (rows, 1), 0)
    sink = jnp.full((rows, 1), sinks[grp - 1], F32)
    for g in range(grp - 2, -1, -1):
        sink = jnp.where(r < (g + 1) * CHUNK, sinks[g], sink)
    m = jnp.maximum(jnp.max(s, axis=1, keepdims=True), sink)
    p = jnp.exp(s - m)
    ps = jnp.exp(sink - m)
    inv = 1.0 / (jnp.sum(p, axis=1, keepdims=True) + ps)
    return p * inv, ps * inv


def _kv_specs(width, nb):
    prev = pl.BlockSpec((CHUNK, width), lambda n: (jnp.maximum(n - 1, 0), 0))
    cur = pl.BlockSpec((CHUNK, width), lambda n: (n, 0))
    return prev, cur


def _attn_fwd(qr, kr, vr, sinks, *, name):
    T, QW = qr.shape
    KW = kr.shape[1]
    HQ, HK = QW // HEAD_DIM, KW // HEAD_DIM
    grp = HQ // HK
    nb = T // CHUNK

    def body(q_ref, kp_ref, kc_ref, vp_ref, vc_ref, s_ref, ok_ref, o_ref):
        valid = ok_ref[...] > 0.5
        for kh in range(HK):
            ks = slice(kh * HEAD_DIM, (kh + 1) * HEAD_DIM)
            heads = list(range(kh * grp, (kh + 1) * grp))
            q = _group_rows(q_ref, heads)
            kk = jnp.concatenate([kp_ref[:, ks], kc_ref[:, ks]], axis=0)
            vv = jnp.concatenate([vp_ref[:, ks], vc_ref[:, ks]], axis=0)
            p, _ = _attn_group_probs(q, kk, [s_ref[0, h] for h in heads], valid, grp)
            o = jnp.dot(p.astype(BF16), vv, preferred_element_type=F32).astype(BF16)
            for g, h in enumerate(heads):
                o_ref[:, h * HEAD_DIM:(h + 1) * HEAD_DIM] = o[g * CHUNK:(g + 1) * CHUNK]

    kp, kc = _kv_specs(KW, nb)
    return pl.pallas_call(
        body, name=name, out_shape=jax.ShapeDtypeStruct((T, QW), BF16), grid=(nb,),
        in_specs=[pl.BlockSpec((CHUNK, QW), lambda n: (n, 0)), kp, kc, kp, kc,
                  pl.BlockSpec(memory_space=pltpu.SMEM), _valid_spec(grp)],
        out_specs=pl.BlockSpec((CHUNK, QW), lambda n: (n, 0)),
        compiler_params=pltpu.CompilerParams(dimension_semantics=("arbitrary",), vmem_limit_bytes=_vmem(8 << 20)),
    )(qr, kr, kr, vr, vr, sinks, _attn_valid(grp))


def _attn_bwd(qr, kr, vr, sinks, do, *, name):
    T, QW = qr.shape
    KW = kr.shape[1]
    HQ, HK = QW // HEAD_DIM, KW // HEAD_DIM
    grp = HQ // HK
    nb = T // CHUNK

    def body(q_ref, kp_ref, kc_ref, vp_ref, vc_ref, s_ref, do_ref, ok_ref,
             dq_ref, dkp_ref, dkc_ref, dvp_ref, dvc_ref, ds_ref):
        n = pl.program_id(0)
        valid = ok_ref[...] > 0.5
        lane = lax.broadcasted_iota(jnp.int32, (1, LANES), 1)
        dsink = jnp.zeros((1, LANES), F32)
        for kh in range(HK):
            ks = slice(kh * HEAD_DIM, (kh + 1) * HEAD_DIM)
            heads = list(range(kh * grp, (kh + 1) * grp))
            q = _group_rows(q_ref, heads)
            doh = _group_rows(do_ref, heads)
            kk = jnp.concatenate([kp_ref[:, ks], kc_ref[:, ks]], axis=0)
            vv = jnp.concatenate([vp_ref[:, ks], vc_ref[:, ks]], axis=0)
            p, ps = _attn_group_probs(q, kk, [s_ref[0, h] for h in heads], valid, grp)
            dp = lax.dot_general(doh, vv, _NT, preferred_element_type=F32)
            delta = jnp.sum(p * dp, axis=1, keepdims=True)
            ds = (p * (dp - delta)).astype(BF16)
            dv = lax.dot_general(p.astype(BF16), doh, _TN, preferred_element_type=F32)
            dk = lax.dot_general(ds, q, _TN, preferred_element_type=F32)
            dq = jnp.dot(ds, kk, preferred_element_type=F32)
            psd = ps * delta
            for g, h in enumerate(heads):
                dq_ref[:, h * HEAD_DIM:(h + 1) * HEAD_DIM] = dq[g * CHUNK:(g + 1) * CHUNK]
                dsink = dsink + jnp.where(
                    lane == h, -jnp.sum(psd[g * CHUNK:(g + 1) * CHUNK], axis=0, keepdims=True), 0.0)
            dkp_ref[:, ks] = dk[:CHUNK]
            dkc_ref[:, ks] = dk[CHUNK:]
            dvp_ref[:, ks] = dv[:CHUNK]
            dvc_ref[:, ks] = dv[CHUNK:]

        @pl.when(n == 0)
        def _():
            ds_ref[...] = dsink

        @pl.when(n > 0)
        def _():
            ds_ref[...] += dsink

    kp, kc = _kv_specs(KW, nb)
    qspec = pl.BlockSpec((CHUNK, QW), lambda n: (n, 0))
    kout = pl.BlockSpec((CHUNK, KW), lambda n: (n, 0))
    return pl.pallas_call(
        body, name=name,
        out_shape=[jax.ShapeDtypeStruct((T, QW), F32)] + [jax.ShapeDtypeStruct((T, KW), F32)] * 4
        + [jax.ShapeDtypeStruct((1, LANES), F32)],
        grid=(nb,),
        in_specs=[qspec, kp, kc, kp, kc, pl.BlockSpec(memory_space=pltpu.SMEM), qspec, _valid_spec(grp)],
        out_specs=[qspec, kout, kout, kout, kout, pl.BlockSpec((1, LANES), lambda n: (0, 0))],
        compiler_params=pltpu.CompilerParams(dimension_semantics=("arbitrary",), vmem_limit_bytes=_vmem(12 << 20)),
    )(qr, kr, kr, vr, vr, sinks, do, _attn_valid(grp))


def _rope_bwd(dq, dkp, dkc, dvp, dvc, ctab, stab, *, name):
    T, QW = dq.shape
    KW = dkp.shape[1]
    nb = T // CHUNK
    scale = HEAD_DIM ** -0.5
    width = QW + 2 * KW

    def body(dq_ref, dkc_ref, dkn_ref, dvc_ref, dvn_ref, c_ref, s_ref, o_ref, db_ref):
        n = pl.program_id(0)
        c = c_ref[...]
        s = s_ref[...]
        has_next = (n < nb - 1).astype(F32)
        dqv = dq_ref[...]
        dk = dkc_ref[...] + has_next * dkn_ref[...]
        dv = dvc_ref[...] + has_next * dvn_ref[...]
        dq_pre = (dqv * _wide(c, QW) + _swap8(dqv * _wide(s, QW))) * scale
        dk_pre = dk * _wide(c, KW) + _swap8(dk * _wide(s, KW))
        o_ref[:, :QW] = dq_pre.astype(BF16)
        o_ref[:, QW:QW + KW] = dk_pre.astype(BF16)
        o_ref[:, QW + KW:] = dv.astype(BF16)
        part = jnp.concatenate([jnp.sum(dq_pre, axis=0, keepdims=True), jnp.sum(dk_pre, axis=0, keepdims=True),
                                jnp.sum(dv, axis=0, keepdims=True)], axis=1)

        @pl.when(n == 0)
        def _():
            db_ref[...] = part

        @pl.when(n > 0)
        def _():
            db_ref[...] += part

    cur = lambda w: pl.BlockSpec((CHUNK, w), lambda n: (n, 0))
    nxt = lambda w: pl.BlockSpec((CHUNK, w), lambda n: (jnp.minimum(n + 1, nb - 1), 0))
    return pl.pallas_call(
        body, name=name,
        out_shape=[jax.ShapeDtypeStruct((T, width), BF16), jax.ShapeDtypeStruct((1, width), F32)],
        grid=(nb,),
        in_specs=[cur(QW), cur(KW), nxt(KW), cur(KW), nxt(KW), cur(LANES), cur(LANES)],
        out_specs=[cur(width), pl.BlockSpec((1, width), lambda n: (0, 0))],
        compiler_params=pltpu.CompilerParams(dimension_semantics=("arbitrary",), vmem_limit_bytes=_vmem(8 << 20)),
    )(dq, dkc, dkp, dvc, dvp, ctab, stab)


def _cast_block(w, l, axis, chip_arr, *, name):
    _, Ks, Ns = w.shape
    tk = _pick(Ks, (512, 352, 256, 128))
    nk = Ks // tk
    full = (Ks * N_CHIPS, Ns) if axis == 0 else (Ks, Ns * N_CHIPS)

    def body(p_ref, w_ref, o_ref):
        o_ref[...] = w_ref[...].astype(BF16)

    if axis == 0:
        out_spec = pl.BlockSpec((tk, Ns), lambda i, p: (p[0] * nk + i, 0))
    else:
        out_spec = pl.BlockSpec((tk, Ns), lambda i, p: (i, p[0]))
    grid_spec = pltpu.PrefetchScalarGridSpec(
        num_scalar_prefetch=1, grid=(nk,),
        in_specs=[pl.BlockSpec((None, tk, Ns), lambda i, p: (l, i, 0))], out_specs=out_spec)
    return pl.pallas_call(
        body, name=name, out_shape=jax.ShapeDtypeStruct(full, BF16), grid_spec=grid_spec,
        compiler_params=pltpu.CompilerParams(dimension_semantics=("arbitrary",),
                                             vmem_limit_bytes=_vmem(4 * tk * Ns * 6)),
    )(chip_arr, w)


def _adamw_math(w, g, m, v):
    m = ADAM_B1 * m + (1.0 - ADAM_B1) * g
    v = ADAM_B2 * v + (1.0 - ADAM_B2) * (g * g)
    m_hat = m / (1.0 - ADAM_B1 ** ADAM_STEP)
    v_hat = v / (1.0 - ADAM_B2 ** ADAM_STEP)
    delta = -ADAM_LR * (m_hat / (jnp.sqrt(v_hat) + ADAM_EPS) + ADAM_WD * w)
    return delta, m, v


def _adamw_layer(w, m, v, g, l, outs, *, name):
    _, K, N = w.shape
    tk = _pick(K, (256, 176, 128))

    def body(w_ref, m_ref, v_ref, g_ref, _g, _d, _m, _v, go_ref, d_ref, mo_ref, vo_ref):
        gv = g_ref[...]
        d, mn, vn = _adamw_math(w_ref[...], gv, m_ref[...], v_ref[...])
        go_ref[...] = gv
        d_ref[...] = d
        mo_ref[...] = mn
        vo_ref[...] = vn

    layer = pl.BlockSpec((None, tk, N), lambda i: (l, i, 0))
    any_spec = pl.BlockSpec(memory_space=pl.ANY)
    sd = jax.ShapeDtypeStruct(w.shape, F32)
    return pl.pallas_call(
        body, name=name, out_shape=[sd, sd, sd, sd], grid=(K // tk,),
        in_specs=[layer, layer, layer, pl.BlockSpec((tk, N), lambda i: (i, 0))] + [any_spec] * 4,
        out_specs=[layer] * 4, input_output_aliases={4: 0, 5: 1, 6: 2, 7: 3},
        compiler_params=pltpu.CompilerParams(dimension_semantics=("arbitrary",),
                                             vmem_limit_bytes=_vmem(2 * 8 * tk * N * 4 + 6 * tk * N * 4)),
    )(w, m, v, g, *outs)


def _adamw_small(w, g, m, v, *, name):
    def body(w_ref, g_ref, m_ref, v_ref, d_ref, mo_ref, vo_ref):
        d, mn, vn = _adamw_math(w_ref[...], g_ref[...], m_ref[...], v_ref[...])
        d_ref[...] = d
        mo_ref[...] = mn
        vo_ref[...] = vn

    sd = jax.ShapeDtypeStruct(w.shape, F32)
    return pl.pallas_call(body, name=name, out_shape=[sd, sd, sd])(w, g, m, v)


def _my_place():
    return lax.axis_index("x"), lax.axis_index("y"), lax.axis_index("c")


def _peer_chips(x, y):
    return [(1 - x, y), (x, 1 - y), (1 - x, 1 - y)]


_HBM = pl.BlockSpec(memory_space=pltpu.HBM)
_SEM = pl.BlockSpec(memory_space=pltpu.SEMAPHORE)
_EFFECT = pltpu.SideEffectType.DATAFLOW_SIDE_EFFECTING


def _split_start(name, bufs, n_copies, make_copies, after):
    nb = len(bufs)

    def body(*refs):
        send_sems, recv_sems = refs[nb + 1], refs[nb + 2]
        token = refs[2 * nb + 3]
        sends, _ = make_copies(refs[:nb], send_sems, recv_sems)
        for cp in sends:
            cp.start()
        token[...] = jnp.zeros_like(token)

    res = pl.pallas_call(
        body, name=name,
        out_shape=(pltpu.SemaphoreType.DMA((n_copies,)), pltpu.SemaphoreType.DMA((n_copies,)),
                   *[pltpu.HBM(b.shape, b.dtype) for b in bufs], jax.ShapeDtypeStruct((8, LANES), F32)),
        in_specs=[_HBM] * nb + [pl.BlockSpec(memory_space=pl.ANY)],
        out_specs=(_SEM, _SEM, *[_HBM] * nb, pl.BlockSpec(memory_space=pltpu.VMEM)),
        input_output_aliases={k: 2 + k for k in range(nb)},
        compiler_params=pltpu.CompilerParams(has_side_effects=_EFFECT),
    )(*[pltpu.with_memory_space_constraint(b, pltpu.HBM) for b in bufs], after)
    return res[0], res[1], list(res[2:2 + nb]), res[2 + nb]


def _split_wait(name, bufs, sems, make_copies, after):
    nb = len(bufs)

    def body(*refs):
        send_sems, recv_sems = refs[nb], refs[nb + 1]
        sends, recvs = make_copies(refs[:nb], send_sems, recv_sems)
        for cp in sends:
            cp.wait_send()
        for cp in recvs:
            cp.wait_recv()

    res = pl.pallas_call(
        body, name=name,
        out_shape=tuple(pltpu.HBM(b.shape, b.dtype) for b in bufs),
        in_specs=[_HBM] * nb + [_SEM, _SEM, pl.BlockSpec(memory_space=pl.ANY)],
        out_specs=tuple([_HBM] * nb),
        input_output_aliases={k: k for k in range(nb)},
        compiler_params=pltpu.CompilerParams(has_side_effects=_EFFECT),
    )(*bufs, sems[0], sems[1], after)
    return list(res)


def _remote(src, dst, send_sems, recv_sems, k, target):
    return pltpu.make_async_remote_copy(src_ref=src, dst_ref=dst, send_sem=send_sems.at[k],
                                        recv_sem=recv_sems.at[k], device_id=target, device_id_type=MESH)


def _ag_region(ref, axis, chip, half):
    K, N = ref.shape
    if axis == 0:
        hs = K // N_CHIPS // 2
        assert hs % 16 == 0
        return ref.at[pl.ds(pl.multiple_of((2 * chip + half) * hs, 16), hs), :]
    ns, hk = N // N_CHIPS, K // 2
    assert ns % LANES == 0 and hk % 16 == 0
    return ref.at[pl.ds(pl.multiple_of(half * hk, 16), hk), pl.ds(pl.multiple_of(chip * ns, LANES), ns)]


def _ag_copies(stage, axes):
    n = len(axes)

    def make(bufs, send_sems, recv_sems):
        x, y, c = _my_place()
        me = 2 * x + y
        sends, recvs = [], []
        for j, (px, py) in enumerate(_peer_chips(x, y)):
            other = 2 * px + py
            for w in range(n):
                k = j * n + w
                if stage == 1:
                    src, target = _ag_region(bufs[w], axes[w], me, c), (px, py, c)
                    land = _ag_region(bufs[w], axes[w], other, c)
                else:
                    src, target = _ag_region(bufs[w], axes[w], other, c), (x, y, 1 - c)
                    land = _ag_region(bufs[w], axes[w], other, 1 - c)
                sends.append(_remote(src, src, send_sems, recv_sems, k, target))
                recvs.append(_remote(land, land, send_sems, recv_sems, k, target))
        return sends, recvs

    return make


def _half_shape(shape, axis):
    K, N = shape
    return (K, N // 2) if axis == 0 else (K // 2, N)


def _core_half(ref, axis, half):
    K, N = ref.shape
    if axis == 0:
        return ref.at[:, pl.ds(pl.multiple_of(half * (N // 2), LANES), N // 2)]
    return ref.at[pl.ds(pl.multiple_of(half * (K // 2), 16), K // 2), :]


def _chip_block(ref, axis, chip):
    K, N = ref.shape
    if axis == 0:
        return ref.at[pl.ds(pl.multiple_of(chip * (K // N_CHIPS), 16), K // N_CHIPS), :]
    return ref.at[:, pl.ds(pl.multiple_of(chip * (N // N_CHIPS), LANES), N // N_CHIPS)]


def _rs_sibling_copies(axes):
    n = len(axes)

    def make(bufs, send_sems, recv_sems):
        x, y, c = _my_place()
        sends = [_remote(_core_half(bufs[w], axes[w], 1 - c), bufs[n + w], send_sems, recv_sems, w, (x, y, 1 - c))
                 for w in range(n)]
        recvs = [_remote(bufs[n + w], bufs[n + w], send_sems, recv_sems, w, (x, y, 1 - c)) for w in range(n)]
        return sends, recvs

    return make


def _rs_chip_copies(axes):
    n = len(axes)

    def make(bufs, send_sems, recv_sems):
        x, y, c = _my_place()
        sends, recvs = [], []
        for j, (px, py) in enumerate(_peer_chips(x, y)):
            for w in range(n):
                k = j * n + w
                sends.append(_remote(_chip_block(bufs[w], axes[w], 2 * px + py), bufs[n + w].at[j],
                                     send_sems, recv_sems, k, (px, py, c)))
                recvs.append(_remote(bufs[n + w].at[j], bufs[n + w].at[j], send_sems, recv_sems, k, (px, py, c)))
        return sends, recvs

    return make


def _rs_fill_copies(axes):
    n = len(axes)

    def make(bufs, send_sems, recv_sems):
        x, y, c = _my_place()
        sends = [_remote(_core_half(bufs[w], axes[w], c), _core_half(bufs[w], axes[w], c),
                         send_sems, recv_sems, w, (x, y, 1 - c)) for w in range(n)]
        recvs = [_remote(_core_half(bufs[w], axes[w], 1 - c), _core_half(bufs[w], axes[w], 1 - c),
                         send_sems, recv_sems, w, (x, y, 1 - c)) for w in range(n)]
        return sends, recvs

    return make


def _chip_sum(g, r, axis, place, *, name):
    hk, hn = r.shape
    bk, bn = (hk // N_CHIPS, hn) if axis == 0 else (hk, hn // N_CHIPS)
    tk = _pick(bk, (512, 352, 256, 128))
    nk = bk // tk

    def body(p_ref, g_ref, r_ref, b_ref, own_ref):
        s = g_ref[...].astype(F32) + r_ref[...].astype(F32)
        b_ref[...] = s.astype(BF16)

        @pl.when(pl.program_id(1) == p_ref[0])
        def _():
            own_ref[...] = s

    if axis == 0:
        g_spec = pl.BlockSpec((tk, bn), lambda i, j, p: (j * nk + i, p[1]))
        r_spec = pl.BlockSpec((tk, bn), lambda i, j, p: (j * nk + i, 0))
    else:
        g_spec = pl.BlockSpec((tk, bn), lambda i, j, p: (p[1] * nk + i, j))
        r_spec = pl.BlockSpec((tk, bn), lambda i, j, p: (i, j))
    grid_spec = pltpu.PrefetchScalarGridSpec(
        num_scalar_prefetch=1, grid=(nk, N_CHIPS), in_specs=[g_spec, r_spec],
        out_specs=[r_spec, pl.BlockSpec((tk, bn), lambda i, j, p: (i, 0))])
    return pl.pallas_call(
        body, name=name,
        out_shape=[jax.ShapeDtypeStruct(r.shape, BF16), jax.ShapeDtypeStruct((bk, bn), F32)],
        grid_spec=grid_spec,
        compiler_params=pltpu.CompilerParams(dimension_semantics=("arbitrary", "arbitrary"),
                                             vmem_limit_bytes=_vmem(2 * tk * bn * 10 + 3 * tk * bn * 4)),
    )(place, g, r)


def _final_sum(own, recv, axis, place, *, name):
    _, bk, bn = recv.shape
    tk = _pick(bk, (256, 176, 128))
    nk = bk // tk

    def body(p_ref, o_ref, r_ref, out_ref):
        out_ref[...] = ((o_ref[...] + r_ref[0].astype(F32)) + r_ref[1].astype(F32)) + r_ref[2].astype(F32)

    own_spec = pl.BlockSpec((tk, bn), lambda i, p: (i, 0))
    if axis == 0:
        out_shape, out_spec = (bk, 2 * bn), pl.BlockSpec((tk, bn), lambda i, p: (i, p[1]))
    else:
        out_shape, out_spec = (2 * bk, bn), pl.BlockSpec((tk, bn), lambda i, p: (p[1] * nk + i, 0))
    grid_spec = pltpu.PrefetchScalarGridSpec(
        num_scalar_prefetch=1, grid=(nk,),
        in_specs=[own_spec, pl.BlockSpec((3, tk, bn), lambda i, p: (0, i, 0))], out_specs=out_spec)
    return pl.pallas_call(
        body, name=name, out_shape=jax.ShapeDtypeStruct(out_shape, F32), grid_spec=grid_spec,
        compiler_params=pltpu.CompilerParams(dimension_semantics=("arbitrary",),
                                             vmem_limit_bytes=_vmem(2 * tk * bn * 14 + 4 * tk * bn * 4)),
    )(place, own, recv)


def _allreduce_small(p):
    def body(p_ref, o_ref, r0, r1, r2, send_sems, recv_sems):
        x, y, c = _my_place()
        o_ref[...] = p_ref[...]
        for s, (peer, rbuf) in enumerate([((x, y, 1 - c), r0), ((1 - x, y, c), r1), ((x, 1 - y, c), r2)]):
            cp = pltpu.make_async_remote_copy(src_ref=o_ref, dst_ref=rbuf, send_sem=send_sems.at[s],
                                              recv_sem=recv_sems.at[s], device_id=peer, device_id_type=MESH)
            cp.start()
            cp.wait()
            o_ref[...] = o_ref[...] + rbuf[...]

    vm = pl.BlockSpec(memory_space=pltpu.VMEM)
    return pl.pallas_call(
        body, name="allreduce_small", out_shape=jax.ShapeDtypeStruct(p.shape, F32),
        in_specs=[vm], out_specs=vm,
        scratch_shapes=[pltpu.VMEM(p.shape, F32)] * 3 + [pltpu.SemaphoreType.DMA((3,))] * 2,
        compiler_params=pltpu.CompilerParams(vmem_limit_bytes=_vmem(6 * _nbytes(p.shape, F32))),
    )(p)


def _pack_rows(parts):
    rows, metas = [], []
    for a in parts:
        flat = a.reshape(-1)
        nrow = -(-flat.shape[0] // LANES)
        nrow = -(-nrow // 8) * 8
        flat = jnp.pad(flat, (0, nrow * LANES - flat.shape[0]))
        rows.append(flat.reshape(nrow, LANES))
        metas.append((a.shape, nrow))
    return jnp.concatenate(rows, axis=0), metas


def _unpack_rows(packed, metas):
    out, r0 = [], 0
    for shape, nrow in metas:
        size = int(np.prod(shape))
        out.append(packed[r0:r0 + nrow].reshape(-1)[:size].reshape(shape))
        r0 += nrow
    return out


def kernel(x, positions, pre_mix_g, post_mix_g, pre_ffn_g, post_ffn_g, a_w_in, a_b_in, a_ln_g, a_ln_b, a_w_s, a_b_s, a_w_out, b_w_qkv, b_b_qkv, b_sinks, b_w_o, ffn_w_gu, ffn_w_down, loss_target, m_pre_mix_g, m_post_mix_g, m_pre_ffn_g, m_post_ffn_g, m_a_w_in, m_a_b_in, m_a_ln_g, m_a_ln_b, m_a_w_s, m_a_b_s, m_a_w_out, m_b_w_qkv, m_b_b_qkv, m_b_sinks, m_b_w_o, m_ffn_w_gu, m_ffn_w_down, v_pre_mix_g, v_post_mix_g, v_pre_ffn_g, v_post_ffn_g, v_a_w_in, v_a_b_in, v_a_ln_g, v_a_ln_b, v_a_w_s, v_a_b_s, v_a_w_out, v_b_w_qkv, v_b_b_qkv, v_b_sinks, v_b_w_o, v_ffn_w_gu, v_ffn_w_down):
    depth, D = pre_mix_g.shape
    xi, yi, ci = _my_place()
    chip = 2 * xi + yi
    place = jnp.stack([chip, ci]).astype(jnp.int32)

    stacked = {"a_w_in": (a_w_in, m_a_w_in, v_a_w_in), "a_w_out": (a_w_out, m_a_w_out, v_a_w_out),
               "b_w_qkv": (b_w_qkv, m_b_w_qkv, v_b_w_qkv), "b_w_o": (b_w_o, m_b_w_o, v_b_w_o),
               "ffn_w_gu": (ffn_w_gu, m_ffn_w_gu, v_ffn_w_gu), "ffn_w_down": (ffn_w_down, m_ffn_w_down, v_ffn_w_down)}
    cut = {"a_w_in": 1, "a_w_out": 0, "b_w_qkv": 1, "b_w_o": 0, "ffn_w_gu": 1, "ffn_w_down": 0}

    def layer_keys(i):
        mix = [("a_w_in", i // 2), ("a_w_out", i // 2)] if i % 2 == 0 else [("b_w_qkv", i // 2), ("b_w_o", i // 2)]
        return mix + [("ffn_w_gu", i), ("ffn_w_down", i)]

    def dep(a, toks):
        for t in toks:
            a = a + t[:1, :1]
        return a

    W = {}
    for i in range(depth):
        for nm, l in layer_keys(i):
            W[(nm, l)] = _cast_block(stacked[nm][0], l, cut[nm], place, name=f"cast_{nm}_{l}")

    def gather(tag, keys, after):
        axes = [cut[nm] for nm, _ in keys]
        for stage in (1, 2):
            ss, rs, bufs, tok = _split_start(f"ag{stage}_start_{tag}", [W[k] for k in keys], 3 * len(keys),
                                             _ag_copies(stage, axes), after)
            after = yield tok
            bufs = _split_wait(f"ag{stage}_wait_{tag}", bufs, (ss, rs), _ag_copies(stage, axes), after)
            W.update(zip(keys, bufs))
        yield None

    nq = b_b_qkv.shape[1]
    bq_full = jnp.zeros((b_b_qkv.shape[0], N_CHIPS * nq), F32)
    bq_full = lax.dynamic_update_slice(bq_full, jnp.where(ci == 0, b_b_qkv, 0.0), (0, chip * nq))
    bq_packed, bq_meta = _pack_rows([bq_full])
    bq_gathered = _allreduce_small(bq_packed)
    b_qkv_full = _unpack_rows(bq_gathered, bq_meta)[0]

    first = gather("0m", layer_keys(0)[:2], bq_gathered)
    tok = next(first)
    tok = first.send(tok)
    first.send(tok)

    h = x[0]
    target = loss_target[0]
    ctab, stab = _rope_tables(positions[0])
    q_width = W[("b_w_o", 0)].shape[0]
    kv_width = N_KV_HEADS * HEAD_DIM
    row = lambda a, i: a[i:i + 1]

    saved = []
    for i in range(depth):
        j = i // 2
        s = {"h": h}
        toks = []
        ffn_w = None
        if i == 0:
            ffn_w = gather("0f", layer_keys(0)[2:], W[("a_w_out", 0)])
            toks.append(next(ffn_w))
            nxt = gather("1", layer_keys(1), toks[0])
            toks.append(next(nxt))
        elif i + 1 < depth:
            nxt = gather(str(i + 1), layer_keys(i + 1), h)
            toks.append(next(nxt))
        hn = _rms_fwd(h, dep(row(pre_mix_g, i), toks), out_dtype=BF16, name=f"rms_pre_mix_{i}")
        s["hn"] = hn
        if i % 2 == 0:
            pre = _matmul(hn, W[("a_w_in", j)], mode="nn", bias=row(a_b_in, j), out_dtype=F32, name=f"gmlp_in_{i}")
            gated = _sgu_fwd(pre, row(a_ln_g, j), row(a_ln_b, j), a_w_s[j], a_b_s[j].T, name=f"sgu_fwd_{i}")
            mix = _matmul(gated, W[("a_w_out", j)], mode="nn", out_dtype=F32, name=f"gmlp_out_{i}")
            s.update(pre=pre, gated=gated)
        else:
            qkv = _matmul(hn, W[("b_w_qkv", j)], mode="nn", bias=row(b_qkv_full, j), out_dtype=F32,
                          name=f"attn_qkv_{i}")
            qr, kr, vr = _rope_fwd(qkv, ctab, stab, q_width=q_width, kv_width=kv_width, name=f"rope_fwd_{i}")
            o = _attn_fwd(qr, kr, vr, row(b_sinks, j), name=f"attn_fwd_{i}")
            mix = _matmul(o, W[("b_w_o", j)], mode="nn", out_dtype=F32, name=f"attn_o_{i}")
            s.update(qr=qr, kr=kr, vr=vr, o=o)
        s["mix"] = mix
        toks = [ffn_w.send(mix)] if ffn_w else []
        h1 = _rms_res(h, mix, dep(row(post_mix_g, i), toks), name=f"rms_post_mix_{i}")
        if ffn_w:
            ffn_w.send(h1)
        s["h1"] = h1
        fn = _rms_fwd(h1, row(pre_ffn_g, i), out_dtype=BF16, name=f"rms_pre_ffn_{i}")
        g_pre, u_pre, act = _ffn_up(fn, W[("ffn_w_gu", i)][None], 0, name=f"ffn_up_{i}")
        f = _matmul(act, W[("ffn_w_down", i)], mode="nn", out_dtype=F32, name=f"ffn_down_{i}")
        toks = [nxt.send(f)] if i + 1 < depth else []
        h = _rms_res(h1, f, dep(row(post_ffn_g, i), toks), name=f"rms_post_ffn_{i}")
        if i + 1 < depth:
            nxt.send(h)
        s.update(fn=fn, g_pre=g_pre, u_pre=u_pre, act=act, f=f)
        saved.append(s)

    dh, loss_part = _loss_and_grad(h, target, name="loss")
    loss = lax.psum(loss_part[0, 0], ("x", "y", "c"))

    big_out = {nm: tuple(lax.empty(w.shape, F32) for _ in range(4)) for nm, (w, _, _) in stacked.items()}

    def reduce_group(i, keys, grads):
        axes = [cut[nm] for nm, _ in keys]
        n = len(keys)
        lands = [lax.empty(_half_shape(g.shape, ax), BF16) for g, ax in zip(grads, axes)]
        ss, rs, bufs, tok = _split_start(f"rs_sibling_start_{i}", list(grads) + lands, n, _rs_sibling_copies(axes),
                                         place)
        after = yield tok
        bufs = _split_wait(f"rs_sibling_wait_{i}", bufs, (ss, rs), _rs_sibling_copies(axes), after)
        sums = [_chip_sum(bufs[w], bufs[n + w], axes[w], place, name=f"chip_sum_{keys[w][0]}_{keys[w][1]}")
                for w in range(n)]
        lands = [lax.empty((3,) + own.shape, BF16) for _, own in sums]
        ss, rs, bufs, tok = _split_start(f"rs_chip_start_{i}", [sb for sb, _ in sums] + lands, 3 * n,
                                         _rs_chip_copies(axes), place)
        after = yield tok
        bufs = _split_wait(f"rs_chip_wait_{i}", bufs, (ss, rs), _rs_chip_copies(axes), after)
        blocks = [_final_sum(sums[w][1], bufs[n + w], axes[w], place, name=f"final_sum_{keys[w][0]}_{keys[w][1]}")
                  for w in range(n)]
        ss, rs, bufs, tok = _split_start(f"rs_fill_start_{i}", blocks, n, _rs_fill_copies(axes), place)
        after = yield tok
        blocks = _split_wait(f"rs_fill_wait_{i}", bufs, (ss, rs), _rs_fill_copies(axes), after)
        for (nm, l), g in zip(keys, blocks):
            w, m, v = stacked[nm]
            big_out[nm] = tuple(_adamw_layer(w, m, v, g, l, big_out[nm], name=f"adamw_{nm}_{l}"))
        yield None

    reducing = []

    def advance(after):
        toks = []
        for gen in list(reducing):
            tok = gen.send(after)
            if tok is None:
                reducing.remove(gen)
            else:
                toks.append(tok)
        return toks

    small = {}
    g_pre_mix, g_post_mix, g_pre_ffn, g_post_ffn = [None] * depth, [None] * depth, [None] * depth, [None] * depth
    toks = []
    for i in reversed(range(depth)):
        j = i // 2
        s = saved[i]
        df, g_post_ffn[i] = _rms_bwd(s["f"], dep(row(post_ffn_g, i), toks), dh, None, out_dtype=BF16,
                                     name=f"rms_post_ffn_bwd_{i}")
        g_down = _matmul(s["act"], df, mode="tn", out_dtype=BF16, name=f"ffn_down_dw_{i}")
        dg_, du_ = _ffn_down_dx(df, W[("ffn_w_down", i)][None], 0, s["g_pre"], s["u_pre"], name=f"ffn_down_dx_{i}")
        hid = dg_.shape[1]
        tile = _pick(hid, (1408, 768, 512, 256, 128))
        w_gu = W[("ffn_w_gu", i)]
        g_gu = lax.empty(w_gu.shape, BF16)
        g_gu = _matmul(s["fn"], dg_, mode="tn", into=g_gu, tq=tile, out_dtype=BF16, name=f"ffn_g_dw_{i}")
        g_gu = _matmul(s["fn"], du_, mode="tn", into=g_gu, tq=tile, q_off=hid // tile, out_dtype=BF16,
                       name=f"ffn_u_dw_{i}")
        dfn_g = _matmul(dg_, w_gu, mode="nt", tr=hid, out_dtype=F32, name=f"ffn_g_dx_{i}")
        dfn = _matmul(du_, w_gu, mode="nt", tr=hid, b_r_off=1, bias=dfn_g, out_dtype=F32, name=f"ffn_u_dx_{i}")
        toks = advance(dfn)
        if i == 0:
            gen = reduce_group("0f", layer_keys(0)[2:], [g_gu, g_down])
            toks.append(next(gen))
            reducing.append(gen)
        dh1, g_pre_ffn[i] = _rms_bwd(s["h1"], dep(row(pre_ffn_g, i), toks), dfn, dh, out_dtype=F32,
                                     name=f"rms_pre_ffn_bwd_{i}")
        dmix, g_post_mix[i] = _rms_bwd(s["mix"], row(post_mix_g, i), dh1, None, out_dtype=BF16,
                                       name=f"rms_post_mix_bwd_{i}")
        if i % 2 == 0:
            g_out = _matmul(s["gated"], dmix, mode="tn", out_dtype=BF16, name=f"gmlp_out_dw_{i}")
            dgated = _matmul(dmix, W[("a_w_out", j)], mode="nt", out_dtype=F32, name=f"gmlp_out_dx_{i}")
            if i == 0:
                advance(dgated)
            dpre, dws, dbsT, dlng, dlnb, dbin = _sgu_bwd(s["pre"], dgated, row(a_ln_g, j), row(a_ln_b, j),
                                                         a_w_s[j], a_b_s[j].T, name=f"sgu_bwd_{i}")
            small[("a_w_s", j)] = dws
            small[("a_b_s", j)] = dbsT.T
            small[("a_ln_g", j)] = dlng
            small[("a_ln_b", j)] = dlnb
            small[("a_b_in", j)] = dbin
            g_in = _matmul(s["hn"], dpre, mode="tn", out_dtype=BF16, name=f"gmlp_in_dw_{i}")
            dhn = _matmul(dpre, W[("a_w_in", j)], mode="nt", out_dtype=F32, name=f"gmlp_in_dx_{i}")
        else:
            g_out = _matmul(s["o"], dmix, mode="tn", out_dtype=BF16, name=f"attn_o_dw_{i}")
            do = _matmul(dmix, W[("b_w_o", j)], mode="nt", out_dtype=BF16, name=f"attn_o_dx_{i}")
            dq, dkp, dkc, dvp, dvc, dsk = _attn_bwd(s["qr"], s["kr"], s["vr"], row(b_sinks, j), do,
                                                    name=f"attn_bwd_{i}")
            dqkv, dbq = _rope_bwd(dq, dkp, dkc, dvp, dvc, ctab, stab, name=f"rope_bwd_{i}")
            small[("b_sinks", j)] = dsk[:, :b_sinks.shape[1]]
            small[("b_b_qkv", j)] = dbq
            g_in = _matmul(s["hn"], dqkv, mode="tn", out_dtype=BF16, name=f"attn_qkv_dw_{i}")
            dhn = _matmul(dqkv, W[("b_w_qkv", j)], mode="nt", out_dtype=F32, name=f"attn_qkv_dx_{i}")
        toks = advance(dhn)
        dh, g_pre_mix[i] = _rms_bwd(s["h"], dep(row(pre_mix_g, i), toks), dhn, dh1, out_dtype=F32,
                                    name=f"rms_pre_mix_bwd_{i}")
        if i == 0:
            gen = reduce_group("0m", layer_keys(0)[:2], [g_in, g_out])
        else:
            gen = reduce_group(str(i), layer_keys(i), [g_in, g_out, g_gu, g_down])
        toks = [next(gen)] + advance(dh)
        reducing.append(gen)
    grad_x = dh[None]

    toks = advance(dh)
    n_a, n_b = a_b_in.shape[0], b_sinks.shape[0]
    stack = lambda key, n: jnp.concatenate([small[(key, j)] for j in range(n)], axis=0)
    small_parts = [
        jnp.concatenate(g_pre_mix, axis=0), jnp.concatenate(g_post_mix, axis=0),
        jnp.concatenate(g_pre_ffn, axis=0), jnp.concatenate(g_post_ffn, axis=0),
        stack("a_b_in", n_a), stack("a_ln_g", n_a), stack("a_ln_b", n_a),
        jnp.stack([small[("a_w_s", j)] for j in range(n_a)]), jnp.stack([small[("a_b_s", j)] for j in range(n_a)]),
        stack("b_b_qkv", n_b), stack("b_sinks", n_b),
    ]
    packed, metas = _pack_rows(small_parts)
    reduced = _allreduce_small(dep(packed, toks))
    while reducing:
        advance(reduced)
    red = _unpack_rows(reduced, metas)
    (gr_pre_mix, gr_post_mix, gr_pre_ffn, gr_post_ffn, gr_b_in, gr_ln_g, gr_ln_b, gr_w_s, gr_b_s,
     gr_b_qkv_full, gr_sinks) = red
    gr_b_qkv = lax.dynamic_slice(gr_b_qkv_full, (0, chip * nq), (gr_b_qkv_full.shape[0], nq))

    grads = {"pre_mix_g": gr_pre_mix, "post_mix_g": gr_post_mix, "pre_ffn_g": gr_pre_ffn, "post_ffn_g": gr_post_ffn,
             "a_b_in": gr_b_in, "a_ln_g": gr_ln_g, "a_ln_b": gr_ln_b, "a_w_s": gr_w_s, "a_b_s": gr_b_s,
             "b_b_qkv": gr_b_qkv, "b_sinks": gr_sinks}
    weights = {"pre_mix_g": (pre_mix_g, m_pre_mix_g, v_pre_mix_g), "post_mix_g": (post_mix_g, m_post_mix_g, v_post_mix_g),
               "pre_ffn_g": (pre_ffn_g, m_pre_ffn_g, v_pre_ffn_g), "post_ffn_g": (post_ffn_g, m_post_ffn_g, v_post_ffn_g),
               "a_b_in": (a_b_in, m_a_b_in, v_a_b_in), "a_ln_g": (a_ln_g, m_a_ln_g, v_a_ln_g),
               "a_ln_b": (a_ln_b, m_a_ln_b, v_a_ln_b), "a_w_s": (a_w_s, m_a_w_s, v_a_w_s), "a_b_s": (a_b_s, m_a_b_s, v_a_b_s),
               "b_b_qkv": (b_b_qkv, m_b_b_qkv, v_b_b_qkv), "b_sinks": (b_sinks, m_b_sinks, v_b_sinks)}
    order = ["pre_mix_g", "post_mix_g", "pre_ffn_g", "post_ffn_g", "a_w_in", "a_b_in", "a_ln_g", "a_ln_b", "a_w_s",
             "a_b_s", "a_w_out", "b_w_qkv", "b_b_qkv", "b_sinks", "b_w_o", "ffn_w_gu", "ffn_w_down"]
    deltas, new_m, new_v = {}, {}, {}
    for nm in order:
        if nm in big_out:
            grads[nm], deltas[nm], new_m[nm], new_v[nm] = big_out[nm]
        else:
            w, m, v = weights[nm]
            deltas[nm], new_m[nm], new_v[nm] = _adamw_small(w, grads[nm], m, v, name="adamw_" + nm)
    return (loss, grad_x, *[grads[nm] for nm in order], *[deltas[nm] for nm in order],
            *[new_m[nm] for nm in order], *[new_v[nm] for nm in order])
```

```python
import functools
import math

import jax
import jax.numpy as jnp
import numpy as np
from jax import lax
from jax.experimental import pallas as pl
from jax.experimental.pallas import tpu as pltpu

F32 = jnp.float32
BF16 = jnp.bfloat16
MESH = pl.DeviceIdType.MESH

HEAD_DIM = 64
N_KV_HEADS = 4
ROPE_DIM = 16
ROPE_THETA = 500000.0
CHUNK = 128
GMLP_GROUPS = 8
RMS_EPS = 1e-6
LN_EPS = 1e-5
NEG_INF = -1e30
ADAM_LR = 0.001
ADAM_B1 = 0.9
ADAM_B2 = 0.999
ADAM_EPS = 1e-08
ADAM_WD = 0.01
ADAM_STEP = 10

N_CHIPS = 4
LANES = 128
VMEM_CAP = 58 * 1024 * 1024


def _vmem(est_bytes):
    assert est_bytes < VMEM_CAP
    return VMEM_CAP


def _pick(n, cands):
    for c in cands:
        if c <= n and n % c == 0:
            return c
    return n


def _nbytes(shape, dtype):
    return int(np.prod(shape)) * jnp.dtype(dtype).itemsize


MATMUL_VMEM_BUDGET = 48 * 1024 * 1024


def _halvings(n, unit):
    out, t = [], n
    while t % unit == 0 and t >= unit:
        out.append(t)
        if t % 2:
            break
        t //= 2
    return out


def _matmul_tiles(P, Q, R, a_bytes, b_bytes, o_bytes, full_addend, tp, tq, tr):
    step_us, bytes_per_us = 0.85, 3.2e6
    best = None
    for p in ([tp] if tp else _halvings(P, LANES)):
        for q in ([tq] if tq else _halvings(Q, LANES)):
            for r in ([tr] if tr else _halvings(R, LANES)):
                nk = R // r
                vm = 2 * (p * r * a_bytes + r * q * b_bytes + p * q * o_bytes + (p * q * 4 if full_addend else 0))
                vm += p * q * 4 * (2 if nk > 1 else 1)
                if vm > MATMUL_VMEM_BUDGET:
                    continue
                exposed = (p * r * a_bytes + r * q * b_bytes + p * q * o_bytes) / bytes_per_us
                key = ((P // p) * (Q // q) * nk * step_us + exposed, nk, abs(p - q))
                if best is None or key < best[0]:
                    best = (key, (p, q, r))
    assert best is not None, (P, Q, R)
    return best[1]


def _matmul(a, b, *, mode, out_dtype, name, a_l=None, b_l=None, bias=None, into=None, o_l=None,
            q_off=0, b_r_off=0, tp=None, tq=None, tr=None):
    a2 = a.shape[-2:]
    b2 = b.shape[-2:]
    if mode == "nn":
        (P, R), (R2, Q) = a2, b2
    elif mode == "nt":
        (P, R), (Q, R2) = a2, b2
    else:
        (R, P), (R2, Q) = a2, b2
    assert R == R2 or (mode == "nt" and R2 % R == 0), (mode, a.shape, b.shape)
    o_bytes = jnp.dtype(into.dtype if into is not None else out_dtype).itemsize
    full_addend = bias is not None and bias.shape[0] != 1
    tp, tq, tr = _matmul_tiles(P, Q, R, a.dtype.itemsize, b.dtype.itemsize, o_bytes, full_addend, tp, tq, tr)
    assert P % tp == 0 and Q % tq == 0 and R % tr == 0
    nk = R // tr
    dims = {"nn": (((1,), (0,)), ((), ())), "nt": (((1,), (1,)), ((), ())), "tn": (((0,), (0,)), ((), ()))}[mode]

    def lead(l, blk, idx):
        if l is None:
            return pl.BlockSpec(blk, idx)
        return pl.BlockSpec((None,) + blk, lambda i, j, k: (l,) + idx(i, j, k))

    if mode == "nn":
        a_spec = lead(a_l, (tp, tr), lambda i, j, k: (i, k))
        b_spec = lead(b_l, (tr, tq), lambda i, j, k: (k, j))
    elif mode == "nt":
        a_spec = lead(a_l, (tp, tr), lambda i, j, k: (i, k))
        b_spec = lead(b_l, (tq, tr), lambda i, j, k: (j, k + b_r_off))
    else:
        a_spec = lead(a_l, (tr, tp), lambda i, j, k: (k, i))
        b_spec = lead(b_l, (tr, tq), lambda i, j, k: (k, j))
    in_specs = [a_spec, b_spec]
    args = [a, b]
    if bias is not None:
        if bias.shape[0] == 1:
            in_specs.append(pl.BlockSpec((1, tq), lambda i, j, k: (0, j)))
        else:
            in_specs.append(pl.BlockSpec((tp, tq), lambda i, j, k: (i, j)))
        args.append(bias)
    aliases = {}
    if into is not None:
        in_specs.append(pl.BlockSpec(memory_space=pl.ANY))
        args.append(into)
        aliases = {len(args) - 1: 0}
        out_shape = jax.ShapeDtypeStruct(into.shape, into.dtype)
        out_dtype = into.dtype
        if o_l is None:
            out_spec = pl.BlockSpec((tp, tq), lambda i, j, k: (i, j + q_off))
        else:
            out_spec = pl.BlockSpec((None, tp, tq), lambda i, j, k: (o_l, i, j + q_off))
    else:
        out_shape = jax.ShapeDtypeStruct((P, Q), out_dtype)
        out_spec = pl.BlockSpec((tp, tq), lambda i, j, k: (i, j))
    has_bias = bias is not None
    has_into = into is not None

    def body(*refs):
        a_ref, b_ref = refs[0], refs[1]
        pos = 2
        bias_ref = None
        if has_bias:
            bias_ref = refs[pos]
            pos += 1
        if has_into:
            pos += 1
        o_ref = refs[pos]
        acc_ref = refs[pos + 1] if nk > 1 else None
        part = lax.dot_general(a_ref[...], b_ref[...], dims, preferred_element_type=F32)

        def finish(acc):
            if has_bias:
                acc = acc + bias_ref[...]
            o_ref[...] = acc.astype(out_dtype)

        if nk == 1:
            finish(part)
        else:
            k = pl.program_id(2)

            @pl.when(k == 0)
            def _():
                acc_ref[...] = part

            @pl.when(k > 0)
            def _():
                acc_ref[...] += part

            @pl.when(k == nk - 1)
            def _():
                finish(acc_ref[...])

    est = 2 * (_nbytes((tp, tr), a.dtype) + _nbytes((tr, tq), b.dtype) + _nbytes((tp, tq), out_dtype)) + 3 * tp * tq * 4
    return pl.pallas_call(
        body, name=name, out_shape=out_shape,
        grid=(P // tp, Q // tq, nk),
        in_specs=in_specs, out_specs=out_spec,
        scratch_shapes=[pltpu.VMEM((tp, tq), F32)] if nk > 1 else [],
        input_output_aliases=aliases,
        compiler_params=pltpu.CompilerParams(
            dimension_semantics=("parallel", "parallel", "arbitrary"), vmem_limit_bytes=_vmem(est)),
    )(*args)


def _row_call(body, ins, outs, *, name, rows, tr, acc_outs=(), est=0):
    in_specs = []
    for arr, kind in ins:
        if kind == "row":
            in_specs.append(pl.BlockSpec((tr, arr.shape[1]), lambda i: (i, 0)))
        else:
            nd = arr.ndim
            in_specs.append(pl.BlockSpec(arr.shape, lambda i, nd=nd: (0,) * nd))
    out_shapes = [jax.ShapeDtypeStruct(s, d) for s, d in outs] + [jax.ShapeDtypeStruct(s, d) for s, d in acc_outs]
    out_specs = [pl.BlockSpec((tr, s[1]), lambda i: (i, 0)) for s, _ in outs]
    out_specs += [pl.BlockSpec(s, lambda i, nd=len(s): (0,) * nd) for s, _ in acc_outs]
    res = pl.pallas_call(
        body, name=name, out_shape=out_shapes, grid=(rows // tr,), in_specs=in_specs, out_specs=out_specs,
        compiler_params=pltpu.CompilerParams(dimension_semantics=("arbitrary",), vmem_limit_bytes=_vmem(est)),
    )(*[a for a, _ in ins])
    return res


def _rms_fwd(x, g, *, out_dtype, name):
    T, D = x.shape
    tr = _pick(T, (512, 256, 128))

    def body(x_ref, g_ref, o_ref):
        xv = x_ref[...]
        r = lax.rsqrt(jnp.mean(xv * xv, axis=-1, keepdims=True) + RMS_EPS)
        o_ref[...] = (xv * r * g_ref[...]).astype(out_dtype)

    return _row_call(body, [(x, "row"), (g, "full")], [((T, D), out_dtype)], name=name, rows=T, tr=tr,
                     est=8 * tr * D * 4)[0]


def _rms_res(h, y, g, *, name):
    T, D = h.shape
    tr = _pick(T, (512, 256, 128))

    def body(h_ref, y_ref, g_ref, o_ref):
        yv = y_ref[...]
        r = lax.rsqrt(jnp.mean(yv * yv, axis=-1, keepdims=True) + RMS_EPS)
        o_ref[...] = h_ref[...] + yv * r * g_ref[...]

    return _row_call(body, [(h, "row"), (y, "row"), (g, "full")], [((T, D), F32)], name=name, rows=T, tr=tr,
                     est=10 * tr * D * 4)[0]


def _rms_bwd(x, g, dy, dres, *, out_dtype, name):
    T, D = x.shape
    tr = _pick(T, (512, 256, 128))
    has_res = dres is not None

    def body(*refs):
        if has_res:
            x_ref, g_ref, dy_ref, dr_ref, dx_ref, dg_ref = refs
        else:
            x_ref, g_ref, dy_ref, dx_ref, dg_ref = refs
        xv = x_ref[...]
        r = lax.rsqrt(jnp.mean(xv * xv, axis=-1, keepdims=True) + RMS_EPS)
        xhat = xv * r
        dyv = dy_ref[...].astype(F32)
        dxn = dyv * g_ref[...]
        dx = r * (dxn - xhat * jnp.mean(dxn * xhat, axis=-1, keepdims=True))
        if has_res:
            dx = dx + dr_ref[...]
        dx_ref[...] = dx.astype(out_dtype)
        part = jnp.sum(dyv * xhat, axis=0, keepdims=True)

        @pl.when(pl.program_id(0) == 0)
        def _():
            dg_ref[...] = part

        @pl.when(pl.program_id(0) > 0)
        def _():
            dg_ref[...] += part

    ins = [(x, "row"), (g, "full"), (dy, "row")] + ([(dres, "row")] if has_res else [])
    dx, dg = _row_call(body, ins, [((T, D), out_dtype)], name=name, rows=T, tr=tr, acc_outs=[((1, D), F32)],
                       est=12 * tr * D * 4)
    return dx, dg


def _rms_res_norm(h, y, g_res, g_next, *, name):
    T, D = h.shape
    tr = _pick(T, (512, 256, 128))

    def body(h_ref, y_ref, g_ref, gn_ref, o_ref, n_ref):
        yv = y_ref[...]
        r = lax.rsqrt(jnp.mean(yv * yv, axis=-1, keepdims=True) + RMS_EPS)
        h2 = h_ref[...] + yv * r * g_ref[...]
        o_ref[...] = h2
        r2 = lax.rsqrt(jnp.mean(h2 * h2, axis=-1, keepdims=True) + RMS_EPS)
        n_ref[...] = (h2 * r2 * gn_ref[...]).astype(BF16)

    return _row_call(body, [(h, "row"), (y, "row"), (g_res, "full"), (g_next, "full")],
                     [((T, D), F32), ((T, D), BF16)], name=name, rows=T, tr=tr, est=12 * tr * D * 4)


def _rms_bwd_chain(x1, g1, dy1, dres, x2, g2, *, name):
    T, D = x1.shape
    tr = _pick(T, (512, 256, 128))

    def one(xv, gv, dyv):
        r = lax.rsqrt(jnp.mean(xv * xv, axis=-1, keepdims=True) + RMS_EPS)
        xhat = xv * r
        dxn = dyv * gv
        dx = r * (dxn - xhat * jnp.mean(dxn * xhat, axis=-1, keepdims=True))
        return dx, jnp.sum(dyv * xhat, axis=0, keepdims=True)

    def body(x1_ref, g1_ref, dy1_ref, dr_ref, x2_ref, g2_ref, d1_ref, d2_ref, dg1_ref, dg2_ref):
        dx1, p1 = one(x1_ref[...], g1_ref[...], dy1_ref[...].astype(F32))
        d1 = dx1 + dr_ref[...]
        d1_ref[...] = d1
        dx2, p2 = one(x2_ref[...], g2_ref[...], d1)
        d2_ref[...] = dx2.astype(BF16)

        @pl.when(pl.program_id(0) == 0)
        def _():
            dg1_ref[...] = p1
            dg2_ref[...] = p2

        @pl.when(pl.program_id(0) > 0)
        def _():
            dg1_ref[...] += p1
            dg2_ref[...] += p2

    ins = [(x1, "row"), (g1, "full"), (dy1, "row"), (dres, "row"), (x2, "row"), (g2, "full")]
    return _row_call(body, ins, [((T, D), F32), ((T, D), BF16)], name=name, rows=T, tr=tr,
                     acc_outs=[((1, D), F32), ((1, D), F32)], est=20 * tr * D * 4)


def _ffn_up(fn, w_gu, l, *, name):
    T, D = fn.shape
    H = w_gu.shape[2] // 2
    tp = _pick(T, (1024, 512, 256, 128))
    tq = _pick(H, (1408, 768, 512, 256, 128))
    nj = H // tq

    def body(a_ref, wg_ref, wu_ref, g_ref, u_ref, act_ref):
        a = a_ref[...]
        g = jnp.dot(a, wg_ref[...], preferred_element_type=F32)
        u = jnp.dot(a, wu_ref[...], preferred_element_type=F32)
        g_ref[...] = g.astype(BF16)
        u_ref[...] = u.astype(BF16)
        act_ref[...] = (g * jax.nn.sigmoid(g) * u).astype(BF16)

    tile = pl.BlockSpec((tp, tq), lambda j, i: (i, j))
    est = 2 * (tp * D * 2 + 2 * D * tq * 2 + 3 * tp * tq * 2) + 4 * tp * tq * 4
    return pl.pallas_call(
        body, name=name,
        out_shape=[jax.ShapeDtypeStruct((T, H), BF16), jax.ShapeDtypeStruct((T, H), BF16),
                   jax.ShapeDtypeStruct((T, H), BF16)],
        grid=(nj, T // tp),
        in_specs=[pl.BlockSpec((tp, D), lambda j, i: (i, 0)),
                  pl.BlockSpec((None, D, tq), lambda j, i: (l, 0, j)),
                  pl.BlockSpec((None, D, tq), lambda j, i: (l, 0, j + nj))],
        out_specs=[tile, tile, tile],
        compiler_params=pltpu.CompilerParams(dimension_semantics=("parallel", "parallel"),
                                             vmem_limit_bytes=_vmem(est)),
    )(fn, w_gu, w_gu)


def _ffn_down_dx(df, w_down, l, g, u, *, name):
    T, D = df.shape
    H = w_down.shape[1]
    tp = _pick(T, (512, 256, 128))
    tq = _pick(H, (1408, 768, 512, 256, 128))

    def body(a_ref, w_ref, g_ref, u_ref, dg_ref, du_ref):
        da = lax.dot_general(a_ref[...], w_ref[...], (((1,), (1,)), ((), ())), preferred_element_type=F32)
        gv = g_ref[...].astype(F32)
        sg = jax.nn.sigmoid(gv)
        silu = gv * sg
        dg_ref[...] = (da * u_ref[...].astype(F32) * (sg + silu * (1.0 - sg))).astype(BF16)
        du_ref[...] = (da * silu).astype(BF16)

    tile = pl.BlockSpec((tp, tq), lambda j, i: (i, j))
    est = 2 * (tp * D * 2 + tq * D * 2 + 4 * tp * tq * 2) + 5 * tp * tq * 4
    return pl.pallas_call(
        body, name=name,
        out_shape=[jax.ShapeDtypeStruct((T, H), BF16), jax.ShapeDtypeStruct((T, H), BF16)],
        grid=(H // tq, T // tp),
        in_specs=[pl.BlockSpec((tp, D), lambda j, i: (i, 0)),
                  pl.BlockSpec((None, tq, D), lambda j, i: (l, j, 0)), tile, tile],
        out_specs=[tile, tile],
        compiler_params=pltpu.CompilerParams(dimension_semantics=("parallel", "parallel"),
                                             vmem_limit_bytes=_vmem(est)),
    )(df, w_down, g, u)


def _loss_and_grad(y, target, *, name):
    T, D = y.shape
    tr = _pick(T, (512, 256, 128))

    def body(y_ref, t_ref, dy_ref, l_ref):
        e = y_ref[...] - t_ref[...]
        dy_ref[...] = e * (1.0 / D)
        part = jnp.sum(jnp.sum(e * e, axis=1, keepdims=True), axis=0, keepdims=True) * (0.5 / D)

        @pl.when(pl.program_id(0) == 0)
        def _():
            l_ref[...] = part

        @pl.when(pl.program_id(0) > 0)
        def _():
            l_ref[...] += part

    dy, l = _row_call(body, [(y, "row"), (target, "row")], [((T, D), F32)], name=name, rows=T, tr=tr,
                      acc_outs=[((1, 1), F32)], est=8 * tr * D * 4)
    return dy, l


_SQRT_HALF = 0.7071067811865476
_INV_SQRT_2PI = 0.3989422804014327


def _gelu_parts(x):
    cdf = 0.5 * (1.0 + lax.erf(x * _SQRT_HALF))
    return cdf


def _sgu_common(pre, lng, lnb, W):
    cdf = _gelu_parts(pre)
    z = pre * cdf
    u = z[:, :W]
    v = z[:, W:]
    mu = jnp.mean(v, axis=-1, keepdims=True)
    vc = v - mu
    var = jnp.mean(vc * vc, axis=-1, keepdims=True)
    rstd = lax.rsqrt(var + LN_EPS)
    vhat = vc * rstd
    vn = vhat * lng + lnb
    return cdf, u, vhat, rstd, vn


def _causal_mask():
    t = lax.broadcasted_iota(jnp.int32, (CHUNK, CHUNK), 0)
    s = lax.broadcasted_iota(jnp.int32, (CHUNK, CHUNK), 1)
    return t >= s


def _sgu_fwd(pre, lng, lnb, ws, bsT, *, name):
    T, W2 = pre.shape
    W = W2 // 2
    G = ws.shape[0]
    gd = W // G

    def body(pre_ref, lng_ref, lnb_ref, ws_ref, bs_ref, o_ref):
        _, u, _, _, vn = _sgu_common(pre_ref[...], lng_ref[...], lnb_ref[...], W)
        vnb = vn.astype(BF16)
        causal = _causal_mask()
        for g in range(G):
            w = jnp.where(causal, ws_ref[g], 0.0).astype(BF16)
            sv = jnp.dot(w, vnb[:, g * gd:(g + 1) * gd], preferred_element_type=F32) + bs_ref[:, g:g + 1]
            o_ref[:, g * gd:(g + 1) * gd] = (u[:, g * gd:(g + 1) * gd] * sv).astype(BF16)

    return pl.pallas_call(
        body, name=name, out_shape=jax.ShapeDtypeStruct((T, W), BF16), grid=(T // CHUNK,),
        in_specs=[pl.BlockSpec((CHUNK, W2), lambda i: (i, 0)),
                  pl.BlockSpec((1, W), lambda i: (0, 0)), pl.BlockSpec((1, W), lambda i: (0, 0)),
                  pl.BlockSpec(ws.shape, lambda i: (0, 0, 0)), pl.BlockSpec(bsT.shape, lambda i: (0, 0))],
        out_specs=pl.BlockSpec((CHUNK, W), lambda i: (i, 0)),
        compiler_params=pltpu.CompilerParams(dimension_semantics=("arbitrary",),
                                             vmem_limit_bytes=_vmem(12 * CHUNK * W2 * 4)),
    )(pre, lng, lnb, ws, bsT)


def _sgu_bwd(pre, dgated, lng, lnb, ws, bsT, *, name):
    T, W2 = pre.shape
    W = W2 // 2
    G = ws.shape[0]
    gd = W // G

    def body(pre_ref, dgt_ref, lng_ref, lnb_ref, ws_ref, bs_ref,
             dpre_ref, dws_ref, dbs_ref, dlng_ref, dlnb_ref, dbin_ref):
        first = pl.program_id(0) == 0

        @pl.when(first)
        def _():
            dws_ref[...] = jnp.zeros_like(dws_ref)
            dbs_ref[...] = jnp.zeros_like(dbs_ref)
            dlng_ref[...] = jnp.zeros_like(dlng_ref)
            dlnb_ref[...] = jnp.zeros_like(dlnb_ref)
            dbin_ref[...] = jnp.zeros_like(dbin_ref)

        pre_v = pre_ref[...]
        lng_v = lng_ref[...]
        cdf, u, vhat, rstd, vn = _sgu_common(pre_v, lng_v, lnb_ref[...], W)
        vnb = vn.astype(BF16)
        dgt = dgt_ref[...].astype(F32)
        causal = _causal_mask()
        du_parts, dvn_parts = [], []
        for g in range(G):
            sl = slice(g * gd, (g + 1) * gd)
            w = jnp.where(causal, ws_ref[g], 0.0).astype(BF16)
            sv = jnp.dot(w, vnb[:, sl], preferred_element_type=F32) + bs_ref[:, g:g + 1]
            dgt_g = dgt[:, sl]
            du_parts.append(dgt_g * sv)
            dsv = dgt_g * u[:, sl]
            dsvb = dsv.astype(BF16)
            dvn_parts.append(lax.dot_general(w, dsvb, (((0,), (0,)), ((), ())), preferred_element_type=F32))
            dw = lax.dot_general(dsvb, vnb[:, sl], (((1,), (1,)), ((), ())), preferred_element_type=F32)
            dws_ref[g] += jnp.where(causal, dw, 0.0)
            dbs_ref[:, g:g + 1] += jnp.sum(dsv, axis=1, keepdims=True)
        du = jnp.concatenate(du_parts, axis=1)
        dvn = jnp.concatenate(dvn_parts, axis=1)
        dlng_ref[...] += jnp.sum(dvn * vhat, axis=0, keepdims=True)
        dlnb_ref[...] += jnp.sum(dvn, axis=0, keepdims=True)
        dvh = dvn * lng_v
        dv = rstd * (dvh - jnp.mean(dvh, axis=-1, keepdims=True)
                     - vhat * jnp.mean(dvh * vhat, axis=-1, keepdims=True))
        dz = jnp.concatenate([du, dv], axis=1)
        dgelu = cdf + pre_v * jnp.exp(-0.5 * pre_v * pre_v) * _INV_SQRT_2PI
        dpre = dz * dgelu
        dbin_ref[...] += jnp.sum(dpre, axis=0, keepdims=True)
        dpre_ref[...] = dpre.astype(BF16)

    full = lambda shape: pl.BlockSpec(shape, lambda i, nd=len(shape): (0,) * nd)
    return pl.pallas_call(
        body, name=name,
        out_shape=[jax.ShapeDtypeStruct((T, W2), BF16), jax.ShapeDtypeStruct(ws.shape, F32),
                   jax.ShapeDtypeStruct(bsT.shape, F32), jax.ShapeDtypeStruct((1, W), F32),
                   jax.ShapeDtypeStruct((1, W), F32), jax.ShapeDtypeStruct((1, W2), F32)],
        grid=(T // CHUNK,),
        in_specs=[pl.BlockSpec((CHUNK, W2), lambda i: (i, 0)), pl.BlockSpec((CHUNK, W), lambda i: (i, 0)),
                  full((1, W)), full((1, W)), full(ws.shape), full(bsT.shape)],
        out_specs=[pl.BlockSpec((CHUNK, W2), lambda i: (i, 0)), full(ws.shape), full(bsT.shape),
                   full((1, W)), full((1, W)), full((1, W2))],
        compiler_params=pltpu.CompilerParams(dimension_semantics=("arbitrary",),
                                             vmem_limit_bytes=_vmem(24 * CHUNK * W2 * 4)),
    )(pre, dgated, lng, lnb, ws, bsT)


def _rope_tables(positions):
    half = ROPE_DIM // 2
    inv_freq = ROPE_THETA ** (-jnp.arange(0, ROPE_DIM, 2, dtype=F32) / ROPE_DIM)
    ang = positions.astype(F32).reshape(-1, 1) * inv_freq
    cos, sin = jnp.cos(ang), jnp.sin(ang)
    T = ang.shape[0]
    rest = HEAD_DIM - ROPE_DIM
    c64 = jnp.concatenate([cos, cos, jnp.ones((T, rest), F32)], axis=1)
    s64 = jnp.concatenate([-sin, sin, jnp.zeros((T, rest), F32)], axis=1)
    del half
    return jnp.tile(c64, (1, LANES // HEAD_DIM)), jnp.tile(s64, (1, LANES // HEAD_DIM))


def _swap8(x):
    W = x.shape[1]
    half = ROPE_DIM // 2
    lane = lax.broadcasted_iota(jnp.int32, x.shape, 1) % HEAD_DIM
    return jnp.where(lane < half, pltpu.roll(x, W - half, axis=1),
                     jnp.where(lane < ROPE_DIM, pltpu.roll(x, half, axis=1), 0.0))


def _wide(tab, W):
    return jnp.concatenate([tab] * (W // LANES), axis=1) if W > LANES else tab


def _rope_fwd(qkv, ctab, stab, *, q_width, kv_width, name):
    T = qkv.shape[0]
    tr = _pick(T, (256, 128))
    scale = HEAD_DIM ** -0.5

    def body(x_ref, c_ref, s_ref, q_ref, k_ref, v_ref):
        c = c_ref[...]
        s = s_ref[...]
        q = x_ref[:, :q_width]
        k = x_ref[:, q_width:q_width + kv_width]
        q_ref[...] = ((q * _wide(c, q_width) + _swap8(q) * _wide(s, q_width)) * scale).astype(BF16)
        k_ref[...] = (k * _wide(c, kv_width) + _swap8(k) * _wide(s, kv_width)).astype(BF16)
        v_ref[...] = x_ref[:, q_width + kv_width:].astype(BF16)

    return _row_call(body, [(qkv, "row"), (ctab, "row"), (stab, "row")],
                     [((T, q_width), BF16), ((T, kv_width), BF16), ((T, kv_width), BF16)],
                     name=name, rows=T, tr=tr, est=10 * tr * qkv.shape[1] * 4)


_NT = (((1,), (1,)), ((), ()))
_TN = (((0,), (0,)), ((), ()))


def _group_rows(ref, heads):
    return jnp.concatenate([ref[:, h * HEAD_DIM:(h + 1) * HEAD_DIM] for h in heads], axis=0)


def _attn_valid(grp):
    qi = np.arange(grp * CHUNK)[:, None] % CHUNK
    sj = np.arange(2 * CHUNK)[None, :]
    cur = (sj >= CHUNK) & (sj - CHUNK <= qi)
    prev = (sj < CHUNK) & (sj > qi)
    return jnp.asarray(np.stack([cur, cur | prev]).astype(np.float32))


def _valid_spec(grp):
    return pl.BlockSpec((None, grp * CHUNK, 2 * CHUNK), lambda n: (jnp.minimum(n, 1), 0, 0))


def _attn_group_probs(q, kk, sinks, valid, grp):
    rows = grp * CHUNK
    s = lax.dot_general(q, kk, _NT, preferred_element_type=F32)
    s = jnp.where(valid, s, NEG_INF)
    r = lax.broadcasted_iota(jnp.int32, (rows, 1), 0)
    sink = jnp.full((rows, 1), sinks[grp - 1], F32)
    for g in range(grp - 2, -1, -1):
        sink = jnp.where(r < (g + 1) * CHUNK, sinks[g], sink)
    m = jnp.maximum(jnp.max(s, axis=1, keepdims=True), sink)
    p = jnp.exp(s - m)
    ps = jnp.exp(sink - m)
    inv = 1.0 / (jnp.sum(p, axis=1, keepdims=True) + ps)
    return p * inv, ps * inv


def _kv_specs(width, nb):
    prev = pl.BlockSpec((CHUNK, width), lambda n: (jnp.maximum(n - 1, 0), 0))
    cur = pl.BlockSpec((CHUNK, width), lambda n: (n, 0))
    return prev, cur


def _attn_fwd(qr, kr, vr, sinks, *, name):
    T, QW = qr.shape
    KW = kr.shape[1]
    HQ, HK = QW // HEAD_DIM, KW // HEAD_DIM
    grp = HQ // HK
    nb = T // CHUNK

    def body(q_ref, kp_ref, kc_ref, vp_ref, vc_ref, s_ref, ok_ref, o_ref):
        valid = ok_ref[...] > 0.5
        for kh in range(HK):
            ks = slice(kh * HEAD_DIM, (kh + 1) * HEAD_DIM)
            heads = list(range(kh * grp, (kh + 1) * grp))
            q = _group_rows(q_ref, heads)
            kk = jnp.concatenate([kp_ref[:, ks], kc_ref[:, ks]], axis=0)
            vv = jnp.concatenate([vp_ref[:, ks], vc_ref[:, ks]], axis=0)
            p, _ = _attn_group_probs(q, kk, [s_ref[0, h] for h in heads], valid, grp)
            o = jnp.dot(p.astype(BF16), vv, preferred_element_type=F32).astype(BF16)
            for g, h in enumerate(heads):
                o_ref[:, h * HEAD_DIM:(h + 1) * HEAD_DIM] = o[g * CHUNK:(g + 1) * CHUNK]

    kp, kc = _kv_specs(KW, nb)
    return pl.pallas_call(
        body, name=name, out_shape=jax.ShapeDtypeStruct((T, QW), BF16), grid=(nb,),
        in_specs=[pl.BlockSpec((CHUNK, QW), lambda n: (n, 0)), kp, kc, kp, kc,
                  pl.BlockSpec(memory_space=pltpu.SMEM), _valid_spec(grp)],
        out_specs=pl.BlockSpec((CHUNK, QW), lambda n: (n, 0)),
        compiler_params=pltpu.CompilerParams(dimension_semantics=("arbitrary",), vmem_limit_bytes=_vmem(8 << 20)),
    )(qr, kr, kr, vr, vr, sinks, _attn_valid(grp))


def _attn_bwd(qr, kr, vr, sinks, do, *, name):
    T, QW = qr.shape
    KW = kr.shape[1]
    HQ, HK = QW // HEAD_DIM, KW // HEAD_DIM
    grp = HQ // HK
    nb = T // CHUNK

    def body(q_ref, kp_ref, kc_ref, vp_ref, vc_ref, s_ref, do_ref, ok_ref,
             dq_ref, dkp_ref, dkc_ref, dvp_ref, dvc_ref, ds_ref):
        n = pl.program_id(0)
        valid = ok_ref[...] > 0.5
        lane = lax.broadcasted_iota(jnp.int32, (1, LANES), 1)
        dsink = jnp.zeros((1, LANES), F32)
        for kh in range(HK):
            ks = slice(kh * HEAD_DIM, (kh + 1) * HEAD_DIM)
            heads = list(range(kh * grp, (kh + 1) * grp))
            q = _group_rows(q_ref, heads)
            doh = _group_rows(do_ref, heads)
            kk = jnp.concatenate([kp_ref[:, ks], kc_ref[:, ks]], axis=0)
            vv = jnp.concatenate([vp_ref[:, ks], vc_ref[:, ks]], axis=0)
            p, ps = _attn_group_probs(q, kk, [s_ref[0, h] for h in heads], valid, grp)
            dp = lax.dot_general(doh, vv, _NT, preferred_element_type=F32)
            delta = jnp.sum(p * dp, axis=1, keepdims=True)
            ds = (p * (dp - delta)).astype(BF16)
            dv = lax.dot_general(p.astype(BF16), doh, _TN, preferred_element_type=F32)
            dk = lax.dot_general(ds, q, _TN, preferred_element_type=F32)
            dq = jnp.dot(ds, kk, preferred_element_type=F32)
            psd = ps * delta
            for g, h in enumerate(heads):
                dq_ref[:, h * HEAD_DIM:(h + 1) * HEAD_DIM] = dq[g * CHUNK:(g + 1) * CHUNK]
                dsink = dsink + jnp.where(
                    lane == h, -jnp.sum(psd[g * CHUNK:(g + 1) * CHUNK], axis=0, keepdims=True), 0.0)
            dkp_ref[:, ks] = dk[:CHUNK]
            dkc_ref[:, ks] = dk[CHUNK:]
            dvp_ref[:, ks] = dv[:CHUNK]
            dvc_ref[:, ks] = dv[CHUNK:]

        @pl.when(n == 0)
        def _():
            ds_ref[...] = dsink

        @pl.when(n > 0)
        def _():
            ds_ref[...] += dsink

    kp, kc = _kv_specs(KW, nb)
    qspec = pl.BlockSpec((CHUNK, QW), lambda n: (n, 0))
    kout = pl.BlockSpec((CHUNK, KW), lambda n: (n, 0))
    return pl.pallas_call(
        body, name=name,
        out_shape=[jax.ShapeDtypeStruct((T, QW), F32)] + [jax.ShapeDtypeStruct((T, KW), F32)] * 4
        + [jax.ShapeDtypeStruct((1, LANES), F32)],
        grid=(nb,),
        in_specs=[qspec, kp, kc, kp, kc, pl.BlockSpec(memory_space=pltpu.SMEM), qspec, _valid_spec(grp)],
        out_specs=[qspec, kout, kout, kout, kout, pl.BlockSpec((1, LANES), lambda n: (0, 0))],
        compiler_params=pltpu.CompilerParams(dimension_semantics=("arbitrary",), vmem_limit_bytes=_vmem(12 << 20)),
    )(qr, kr, kr, vr, vr, sinks, do, _attn_valid(grp))


def _rope_bwd(dq, dkp, dkc, dvp, dvc, ctab, stab, *, name):
    T, QW = dq.shape
    KW = dkp.shape[1]
    nb = T // CHUNK
    scale = HEAD_DIM ** -0.5
    width = QW + 2 * KW

    def body(dq_ref, dkc_ref, dkn_ref, dvc_ref, dvn_ref, c_ref, s_ref, o_ref, db_ref):
        n = pl.program_id(0)
        c = c_ref[...]
        s = s_ref[...]
        has_next = (n < nb - 1).astype(F32)
        dqv = dq_ref[...]
        dk = dkc_ref[...] + has_next * dkn_ref[...]
        dv = dvc_ref[...] + has_next * dvn_ref[...]
        dq_pre = (dqv * _wide(c, QW) + _swap8(dqv * _wide(s, QW))) * scale
        dk_pre = dk * _wide(c, KW) + _swap8(dk * _wide(s, KW))
        o_ref[:, :QW] = dq_pre.astype(BF16)
        o_ref[:, QW:QW + KW] = dk_pre.astype(BF16)
        o_ref[:, QW + KW:] = dv.astype(BF16)
        part = jnp.concatenate([jnp.sum(dq_pre, axis=0, keepdims=True), jnp.sum(dk_pre, axis=0, keepdims=True),
                                jnp.sum(dv, axis=0, keepdims=True)], axis=1)

        @pl.when(n == 0)
        def _():
            db_ref[...] = part

        @pl.when(n > 0)
        def _():
            db_ref[...] += part

    cur = lambda w: pl.BlockSpec((CHUNK, w), lambda n: (n, 0))
    nxt = lambda w: pl.BlockSpec((CHUNK, w), lambda n: (jnp.minimum(n + 1, nb - 1), 0))
    return pl.pallas_call(
        body, name=name,
        out_shape=[jax.ShapeDtypeStruct((T, width), BF16), jax.ShapeDtypeStruct((1, width), F32)],
        grid=(nb,),
        in_specs=[cur(QW), cur(KW), nxt(KW), cur(KW), nxt(KW), cur(LANES), cur(LANES)],
        out_specs=[cur(width), pl.BlockSpec((1, width), lambda n: (0, 0))],
        compiler_params=pltpu.CompilerParams(dimension_semantics=("arbitrary",), vmem_limit_bytes=_vmem(8 << 20)),
    )(dq, dkc, dkp, dvc, dvp, ctab, stab)


def _cast_block(w, l, axis, chip_arr, *, name):
    _, Ks, Ns = w.shape
    tk = _pick(Ks, (512, 352, 256, 128))
    nk = Ks // tk
    full = (Ks * N_CHIPS, Ns) if axis == 0 else (Ks, Ns * N_CHIPS)

    def body(p_ref, w_ref, o_ref):
        o_ref[...] = w_ref[...].astype(BF16)

    if axis == 0:
        out_spec = pl.BlockSpec((tk, Ns), lambda i, p: (p[0] * nk + i, 0))
    else:
        out_spec = pl.BlockSpec((tk, Ns), lambda i, p: (i, p[0]))
    grid_spec = pltpu.PrefetchScalarGridSpec(
        num_scalar_prefetch=1, grid=(nk,),
        in_specs=[pl.BlockSpec((None, tk, Ns), lambda i, p: (l, i, 0))], out_specs=out_spec)
    return pl.pallas_call(
        body, name=name, out_shape=jax.ShapeDtypeStruct(full, BF16), grid_spec=grid_spec,
        compiler_params=pltpu.CompilerParams(dimension_semantics=("arbitrary",),
                                             vmem_limit_bytes=_vmem(4 * tk * Ns * 6)),
    )(chip_arr, w)


def _adamw_math(w, g, m, v):
    m = ADAM_B1 * m + (1.0 - ADAM_B1) * g
    v = ADAM_B2 * v + (1.0 - ADAM_B2) * (g * g)
    m_hat = m / (1.0 - ADAM_B1 ** ADAM_STEP)
    v_hat = v / (1.0 - ADAM_B2 ** ADAM_STEP)
    delta = -ADAM_LR * (m_hat / (jnp.sqrt(v_hat) + ADAM_EPS) + ADAM_WD * w)
    return delta, m, v


def _adamw_layer(w, m, v, g, l, outs, *, name):
    _, K, N = w.shape
    tk = _pick(K, (256, 176, 128))

    def body(w_ref, m_ref, v_ref, g_ref, _g, _d, _m, _v, go_ref, d_ref, mo_ref, vo_ref):
        gv = g_ref[...]
        d, mn, vn = _adamw_math(w_ref[...], gv, m_ref[...], v_ref[...])
        go_ref[...] = gv
        d_ref[...] = d
        mo_ref[...] = mn
        vo_ref[...] = vn

    layer = pl.BlockSpec((None, tk, N), lambda i: (l, i, 0))
    any_spec = pl.BlockSpec(memory_space=pl.ANY)
    sd = jax.ShapeDtypeStruct(w.shape, F32)
    return pl.pallas_call(
        body, name=name, out_shape=[sd, sd, sd, sd], grid=(K // tk,),
        in_specs=[layer, layer, layer, pl.BlockSpec((tk, N), lambda i: (i, 0))] + [any_spec] * 4,
        out_specs=[layer] * 4, input_output_aliases={4: 0, 5: 1, 6: 2, 7: 3},
        compiler_params=pltpu.CompilerParams(dimension_semantics=("arbitrary",),
                                             vmem_limit_bytes=_vmem(2 * 8 * tk * N * 4 + 6 * tk * N * 4)),
    )(w, m, v, g, *outs)


def _adamw_small(w, g, m, v, *, name):
    def body(w_ref, g_ref, m_ref, v_ref, d_ref, mo_ref, vo_ref):
        d, mn, vn = _adamw_math(w_ref[...], g_ref[...], m_ref[...], v_ref[...])
        d_ref[...] = d
        mo_ref[...] = mn
        vo_ref[...] = vn

    sd = jax.ShapeDtypeStruct(w.shape, F32)
    return pl.pallas_call(body, name=name, out_shape=[sd, sd, sd])(w, g, m, v)


def _my_place():
    return lax.axis_index("x"), lax.axis_index("y"), lax.axis_index("c")


def _peer_chips(x, y):
    return [(1 - x, y), (x, 1 - y), (1 - x, 1 - y)]


_HBM = pl.BlockSpec(memory_space=pltpu.HBM)
_SEM = pl.BlockSpec(memory_space=pltpu.SEMAPHORE)
_EFFECT = pltpu.SideEffectType.DATAFLOW_SIDE_EFFECTING


def _split_start(name, bufs, n_copies, make_copies, after):
    nb = len(bufs)

    def body(*refs):
        send_sems, recv_sems = refs[nb + 1], refs[nb + 2]
        token = refs[2 * nb + 3]
        sends, _ = make_copies(refs[:nb], send_sems, recv_sems)
        for cp in sends:
            cp.start()
        token[...] = jnp.zeros_like(token)

    res = pl.pallas_call(
        body, name=name,
        out_shape=(pltpu.SemaphoreType.DMA((n_copies,)), pltpu.SemaphoreType.DMA((n_copies,)),
                   *[pltpu.HBM(b.shape, b.dtype) for b in bufs], jax.ShapeDtypeStruct((8, LANES), F32)),
        in_specs=[_HBM] * nb + [pl.BlockSpec(memory_space=pl.ANY)],
        out_specs=(_SEM, _SEM, *[_HBM] * nb, pl.BlockSpec(memory_space=pltpu.VMEM)),
        input_output_aliases={k: 2 + k for k in range(nb)},
        compiler_params=pltpu.CompilerParams(has_side_effects=_EFFECT),
    )(*[pltpu.with_memory_space_constraint(b, pltpu.HBM) for b in bufs], after)
    return res[0], res[1], list(res[2:2 + nb]), res[2 + nb]


def _split_wait(name, bufs, sems, make_copies, after):
    nb = len(bufs)

    def body(*refs):
        send_sems, recv_sems = refs[nb], refs[nb + 1]
        sends, recvs = make_copies(refs[:nb], send_sems, recv_sems)
        for cp in sends:
            cp.wait_send()
        for cp in recvs:
            cp.wait_recv()

    res = pl.pallas_call(
        body, name=name,
        out_shape=tuple(pltpu.HBM(b.shape, b.dtype) for b in bufs),
        in_specs=[_HBM] * nb + [_SEM, _SEM, pl.BlockSpec(memory_space=pl.ANY)],
        out_specs=tuple([_HBM] * nb),
        input_output_aliases={k: k for k in range(nb)},
        compiler_params=pltpu.CompilerParams(has_side_effects=_EFFECT),
    )(*bufs, sems[0], sems[1], after)
    return list(res)


def _remote(src, dst, send_sems, recv_sems, k, target):
    return pltpu.make_async_remote_copy(src_ref=src, dst_ref=dst, send_sem=send_sems.at[k],
                                        recv_sem=recv_sems.at[k], device_id=target, device_id_type=MESH)


def _ag_region(ref, axis, chip, half):
    K, N = ref.shape
    if axis == 0:
        hs = K // N_CHIPS // 2
        assert hs % 16 == 0
        return ref.at[pl.ds(pl.multiple_of((2 * chip + half) * hs, 16), hs), :]
    ns, hk = N // N_CHIPS, K // 2
    assert ns % LANES == 0 and hk % 16 == 0
    return ref.at[pl.ds(pl.multiple_of(half * hk, 16), hk), pl.ds(pl.multiple_of(chip * ns, LANES), ns)]


def _ag_copies(stage, axes):
    n = len(axes)

    def make(bufs, send_sems, recv_sems):
        x, y, c = _my_place()
        me = 2 * x + y
        sends, recvs = [], []
        for j, (px, py) in enumerate(_peer_chips(x, y)):
            other = 2 * px + py
            for w in range(n):
                k = j * n + w
                if stage == 1:
                    src, target = _ag_region(bufs[w], axes[w], me, c), (px, py, c)
                    land = _ag_region(bufs[w], axes[w], other, c)
                else:
                    src, target = _ag_region(bufs[w], axes[w], other, c), (x, y, 1 - c)
                    land = _ag_region(bufs[w], axes[w], other, 1 - c)
                sends.append(_remote(src, src, send_sems, recv_sems, k, target))
                recvs.append(_remote(land, land, send_sems, recv_sems, k, target))
        return sends, recvs

    return make


def _half_shape(shape, axis):
    K, N = shape
    return (K, N // 2) if axis == 0 else (K // 2, N)


def _core_half(ref, axis, half):
    K, N = ref.shape
    if axis == 0:
        return ref.at[:, pl.ds(pl.multiple_of(half * (N // 2), LANES), N // 2)]
    return ref.at[pl.ds(pl.multiple_of(half * (K // 2), 16), K // 2), :]


def _chip_block(ref, axis, chip):
    K, N = ref.shape
    if axis == 0:
        return ref.at[pl.ds(pl.multiple_of(chip * (K // N_CHIPS), 16), K // N_CHIPS), :]
    return ref.at[:, pl.ds(pl.multiple_of(chip * (N // N_CHIPS), LANES), N // N_CHIPS)]


def _rs_sibling_copies(axes):
    n = len(axes)

    def make(bufs, send_sems, recv_sems):
        x, y, c = _my_place()
        sends = [_remote(_core_half(bufs[w], axes[w], 1 - c), bufs[n + w], send_sems, recv_sems, w, (x, y, 1 - c))
                 for w in range(n)]
        recvs = [_remote(bufs[n + w], bufs[n + w], send_sems, recv_sems, w, (x, y, 1 - c)) for w in range(n)]
        return sends, recvs

    return make


def _rs_chip_copies(axes):
    n = len(axes)

    def make(bufs, send_sems, recv_sems):
        x, y, c = _my_place()
        sends, recvs = [], []
        for j, (px, py) in enumerate(_peer_chips(x, y)):
            for w in range(n):
                k = j * n + w
                sends.append(_remote(_chip_block(bufs[w], axes[w], 2 * px + py), bufs[n + w].at[j],
                                     send_sems, recv_sems, k, (px, py, c)))
                recvs.append(_remote(bufs[n + w].at[j], bufs[n + w].at[j], send_sems, recv_sems, k, (px, py, c)))
        return sends, recvs

    return make


def _rs_fill_copies(axes):
    n = len(axes)

    def make(bufs, send_sems, recv_sems):
        x, y, c = _my_place()
        sends = [_remote(_core_half(bufs[w], axes[w], c), _core_half(bufs[w], axes[w], c),
                         send_sems, recv_sems, w, (x, y, 1 - c)) for w in range(n)]
        recvs = [_remote(_core_half(bufs[w], axes[w], 1 - c), _core_half(bufs[w], axes[w], 1 - c),
                         send_sems, recv_sems, w, (x, y, 1 - c)) for w in range(n)]
        return sends, recvs

    return make


def _chip_sum(g, r, axis, place, *, name):
    hk, hn = r.shape
    bk, bn = (hk // N_CHIPS, hn) if axis == 0 else (hk, hn // N_CHIPS)
    tk = _pick(bk, (512, 352, 256, 128))
    nk = bk // tk

    def body(p_ref, g_ref, r_ref, b_ref, own_ref):
        s = g_ref[...].astype(F32) + r_ref[...].astype(F32)
        b_ref[...] = s.astype(BF16)

        @pl.when(pl.program_id(1) == p_ref[0])
        def _():
            own_ref[...] = s

    if axis == 0:
        g_spec = pl.BlockSpec((tk, bn), lambda i, j, p: (j * nk + i, p[1]))
        r_spec = pl.BlockSpec((tk, bn), lambda i, j, p: (j * nk + i, 0))
    else:
        g_spec = pl.BlockSpec((tk, bn), lambda i, j, p: (p[1] * nk + i, j))
        r_spec = pl.BlockSpec((tk, bn), lambda i, j, p: (i, j))
    grid_spec = pltpu.PrefetchScalarGridSpec(
        num_scalar_prefetch=1, grid=(nk, N_CHIPS), in_specs=[g_spec, r_spec],
        out_specs=[r_spec, pl.BlockSpec((tk, bn), lambda i, j, p: (i, 0))])
    return pl.pallas_call(
        body, name=name,
        out_shape=[jax.ShapeDtypeStruct(r.shape, BF16), jax.ShapeDtypeStruct((bk, bn), F32)],
        grid_spec=grid_spec,
        compiler_params=pltpu.CompilerParams(dimension_semantics=("arbitrary", "arbitrary"),
                                             vmem_limit_bytes=_vmem(2 * tk * bn * 10 + 3 * tk * bn * 4)),
    )(place, g, r)


def _final_sum(own, recv, axis, place, *, name):
    _, bk, bn = recv.shape
    tk = _pick(bk, (256, 176, 128))
    nk = bk // tk

    def body(p_ref, o_ref, r_ref, out_ref):
        out_ref[...] = ((o_ref[...] + r_ref[0].astype(F32)) + r_ref[1].astype(F32)) + r_ref[2].astype(F32)

    own_spec = pl.BlockSpec((tk, bn), lambda i, p: (i, 0))
    if axis == 0:
        out_shape, out_spec = (bk, 2 * bn), pl.BlockSpec((tk, bn), lambda i, p: (i, p[1]))
    else:
        out_shape, out_spec = (2 * bk, bn), pl.BlockSpec((tk, bn), lambda i, p: (p[1] * nk + i, 0))
    grid_spec = pltpu.PrefetchScalarGridSpec(
        num_scalar_prefetch=1, grid=(nk,),
        in_specs=[own_spec, pl.BlockSpec((3, tk, bn), lambda i, p: (0, i, 0))], out_specs=out_spec)
    return pl.pallas_call(
        body, name=name, out_shape=jax.ShapeDtypeStruct(out_shape, F32), grid_spec=grid_spec,
        compiler_params=pltpu.CompilerParams(dimension_semantics=("arbitrary",),
                                             vmem_limit_bytes=_vmem(2 * tk * bn * 14 + 4 * tk * bn * 4)),
    )(place, own, recv)


def _allreduce_small(p):
    def body(p_ref, o_ref, r0, r1, r2, send_sems, recv_sems):
        x, y, c = _my_place()
        o_ref[...] = p_ref[...]
        for s, (peer, rbuf) in enumerate([((x, y, 1 - c), r0), ((1 - x, y, c), r1), ((x, 1 - y, c), r2)]):
            cp = pltpu.make_async_remote_copy(src_ref=o_ref, dst_ref=rbuf, send_sem=send_sems.at[s],
                                              recv_sem=recv_sems.at[s], device_id=peer, device_id_type=MESH)
            cp.start()
            cp.wait()
            o_ref[...] = o_ref[...] + rbuf[...]

    vm = pl.BlockSpec(memory_space=pltpu.VMEM)
    return pl.pallas_call(
        body, name="allreduce_small", out_shape=jax.ShapeDtypeStruct(p.shape, F32),
        in_specs=[vm], out_specs=vm,
        scratch_shapes=[pltpu.VMEM(p.shape, F32)] * 3 + [pltpu.SemaphoreType.DMA((3,))] * 2,
        compiler_params=pltpu.CompilerParams(vmem_limit_bytes=_vmem(6 * _nbytes(p.shape, F32))),
    )(p)


def _pack_rows(parts):
    rows, metas = [], []
    for a in parts:
        flat = a.reshape(-1)
        nrow = -(-flat.shape[0] // LANES)
        nrow = -(-nrow // 8) * 8
        flat = jnp.pad(flat, (0, nrow * LANES - flat.shape[0]))
        rows.append(flat.reshape(nrow, LANES))
        metas.append((a.shape, nrow))
    return jnp.concatenate(rows, axis=0), metas


def _unpack_rows(packed, metas):
    out, r0 = [], 0
    for shape, nrow in metas:
        size = int(np.prod(shape))
        out.append(packed[r0:r0 + nrow].reshape(-1)[:size].reshape(shape))
        r0 += nrow
    return out


def kernel(x, positions, pre_mix_g, post_mix_g, pre_ffn_g, post_ffn_g, a_w_in, a_b_in, a_ln_g, a_ln_b, a_w_s, a_b_s, a_w_out, b_w_qkv, b_b_qkv, b_sinks, b_w_o, ffn_w_gu, ffn_w_down, loss_target, m_pre_mix_g, m_post_mix_g, m_pre_ffn_g, m_post_ffn_g, m_a_w_in, m_a_b_in, m_a_ln_g, m_a_ln_b, m_a_w_s, m_a_b_s, m_a_w_out, m_b_w_qkv, m_b_b_qkv, m_b_sinks, m_b_w_o, m_ffn_w_gu, m_ffn_w_down, v_pre_mix_g, v_post_mix_g, v_pre_ffn_g, v_post_ffn_g, v_a_w_in, v_a_b_in, v_a_ln_g, v_a_ln_b, v_a_w_s, v_a_b_s, v_a_w_out, v_b_w_qkv, v_b_b_qkv, v_b_sinks, v_b_w_o, v_ffn_w_gu, v_ffn_w_down):
    depth, D = pre_mix_g.shape
    xi, yi, ci = _my_place()
    chip = 2 * xi + yi
    place = jnp.stack([chip, ci]).astype(jnp.int32)

    stacked = {"a_w_in": (a_w_in, m_a_w_in, v_a_w_in), "a_w_out": (a_w_out, m_a_w_out, v_a_w_out),
               "b_w_qkv": (b_w_qkv, m_b_w_qkv, v_b_w_qkv), "b_w_o": (b_w_o, m_b_w_o, v_b_w_o),
               "ffn_w_gu": (ffn_w_gu, m_ffn_w_gu, v_ffn_w_gu), "ffn_w_down": (ffn_w_down, m_ffn_w_down, v_ffn_w_down)}
    cut = {"a_w_in": 1, "a_w_out": 0, "b_w_qkv": 1, "b_w_o": 0, "ffn_w_gu": 1, "ffn_w_down": 0}

    def layer_keys(i):
        mix = [("a_w_in", i // 2), ("a_w_out", i // 2)] if i % 2 == 0 else [("b_w_qkv", i // 2), ("b_w_o", i // 2)]
        return mix + [("ffn_w_gu", i), ("ffn_w_down", i)]

    def dep(a, toks):
        for t in toks:
            a = a + t[:1, :1]
        return a

    W = {}
    for i in range(depth):
        for nm, l in layer_keys(i):
            W[(nm, l)] = _cast_block(stacked[nm][0], l, cut[nm], place, name=f"cast_{nm}_{l}")

    def gather(tag, keys, after):
        axes = [cut[nm] for nm, _ in keys]
        for stage in (1, 2):
            ss, rs, bufs, tok = _split_start(f"ag{stage}_start_{tag}", [W[k] for k in keys], 3 * len(keys),
                                             _ag_copies(stage, axes), after)
            after = yield tok
            bufs = _split_wait(f"ag{stage}_wait_{tag}", bufs, (ss, rs), _ag_copies(stage, axes), after)
            W.update(zip(keys, bufs))
        yield None

    nq = b_b_qkv.shape[1]
    bq_full = jnp.zeros((b_b_qkv.shape[0], N_CHIPS * nq), F32)
    bq_full = lax.dynamic_update_slice(bq_full, jnp.where(ci == 0, b_b_qkv, 0.0), (0, chip * nq))
    bq_packed, bq_meta = _pack_rows([bq_full])
    bq_gathered = _allreduce_small(bq_packed)
    b_qkv_full = _unpack_rows(bq_gathered, bq_meta)[0]

    first = gather("0m", layer_keys(0)[:2], bq_gathered)
    tok = next(first)
    tok = first.send(tok)
    first.send(tok)

    h = x[0]
    target = loss_target[0]
    ctab, stab = _rope_tables(positions[0])
    q_width = W[("b_w_o", 0)].shape[0]
    kv_width = N_KV_HEADS * HEAD_DIM
    row = lambda a, i: a[i:i + 1]

    saved = []
    hn = None
    for i in range(depth):
        j = i // 2
        s = {"h": h}
        ffn_w = None
        if i == 0:
            ffn_w = gather("0f", layer_keys(0)[2:], W[("a_w_out", 0)])
            toks = [next(ffn_w)]
            nxt = gather("1", layer_keys(1), toks[0])
            toks.append(next(nxt))
            hn = _rms_fwd(h, dep(row(pre_mix_g, i), toks), out_dtype=BF16, name=f"rms_pre_mix_{i}")
        elif i + 1 < depth:
            nxt = gather(str(i + 1), layer_keys(i + 1), h)
            next(nxt)
        s["hn"] = hn
        if i % 2 == 0:
            pre = _matmul(hn, W[("a_w_in", j)], mode="nn", bias=row(a_b_in, j), out_dtype=F32, name=f"gmlp_in_{i}")
            gated = _sgu_fwd(pre, row(a_ln_g, j), row(a_ln_b, j), a_w_s[j], a_b_s[j].T, name=f"sgu_fwd_{i}")
            mix = _matmul(gated, W[("a_w_out", j)], mode="nn", out_dtype=F32, name=f"gmlp_out_{i}")
            s.update(pre=pre, gated=gated)
        else:
            qkv = _matmul(hn, W[("b_w_qkv", j)], mode="nn", bias=row(b_qkv_full, j), out_dtype=F32,
                          name=f"attn_qkv_{i}")
            qr, kr, vr = _rope_fwd(qkv, ctab, stab, q_width=q_width, kv_width=kv_width, name=f"rope_fwd_{i}")
            o = _attn_fwd(qr, kr, vr, row(b_sinks, j), name=f"attn_fwd_{i}")
            mix = _matmul(o, W[("b_w_o", j)], mode="nn", out_dtype=F32, name=f"attn_o_{i}")
            s.update(qr=qr, kr=kr, vr=vr, o=o)
        s["mix"] = mix
        toks = [ffn_w.send(mix)] if ffn_w else []
        h1, fn = _rms_res_norm(h, mix, dep(row(post_mix_g, i), toks), row(pre_ffn_g, i), name=f"rms_post_mix_{i}")
        if ffn_w:
            ffn_w.send(h1)
        s["h1"] = h1
        g_pre, u_pre, act = _ffn_up(fn, W[("ffn_w_gu", i)][None], 0, name=f"ffn_up_{i}")
        f = _matmul(act, W[("ffn_w_down", i)], mode="nn", out_dtype=F32, name=f"ffn_down_{i}")
        if i + 1 < depth:
            toks = [nxt.send(f)]
            h, hn = _rms_res_norm(h1, f, dep(row(post_ffn_g, i), toks), row(pre_mix_g, i + 1),
                                  name=f"rms_post_ffn_{i}")
            nxt.send(h)
        else:
            h = _rms_res(h1, f, row(post_ffn_g, i), name=f"rms_post_ffn_{i}")
        s.update(fn=fn, g_pre=g_pre, u_pre=u_pre, act=act, f=f)
        saved.append(s)

    dh, loss_part = _loss_and_grad(h, target, name="loss")
    loss = lax.psum(loss_part[0, 0], ("x", "y", "c"))

    big_out = {nm: tuple(lax.empty(w.shape, F32) for _ in range(4)) for nm, (w, _, _) in stacked.items()}

    def reduce_group(i, keys, grads):
        axes = [cut[nm] for nm, _ in keys]
        n = len(keys)
        lands = [lax.empty(_half_shape(g.shape, ax), BF16) for g, ax in zip(grads, axes)]
        ss, rs, bufs, tok = _split_start(f"rs_sibling_start_{i}", list(grads) + lands, n, _rs_sibling_copies(axes),
                                         place)
        after = yield tok
        bufs = _split_wait(f"rs_sibling_wait_{i}", bufs, (ss, rs), _rs_sibling_copies(axes), after)
        sums = [_chip_sum(bufs[w], bufs[n + w], axes[w], place, name=f"chip_sum_{keys[w][0]}_{keys[w][1]}")
                for w in range(n)]
        lands = [lax.empty((3,) + own.shape, BF16) for _, own in sums]
        ss, rs, bufs, tok = _split_start(f"rs_chip_start_{i}", [sb for sb, _ in sums] + lands, 3 * n,
                                         _rs_chip_copies(axes), place)
        after = yield tok
        bufs = _split_wait(f"rs_chip_wait_{i}", bufs, (ss, rs), _rs_chip_copies(axes), after)
        blocks = [_final_sum(sums[w][1], bufs[n + w], axes[w], place, name=f"final_sum_{keys[w][0]}_{keys[w][1]}")
                  for w in range(n)]
        ss, rs, bufs, tok = _split_start(f"rs_fill_start_{i}", blocks, n, _rs_fill_copies(axes), place)
        after = yield tok
        blocks = _split_wait(f"rs_fill_wait_{i}", bufs, (ss, rs), _rs_fill_copies(axes), after)
        for (nm, l), g in zip(keys, blocks):
            w, m, v = stacked[nm]
            big_out[nm] = tuple(_adamw_layer(w, m, v, g, l, big_out[nm], name=f"adamw_{nm}_{l}"))
        yield None

    reducing = []

    def advance(after):
        toks = []
        for gen in list(reducing):
            tok = gen.send(after)
            if tok is None:
                reducing.remove(gen)
            else:
                toks.append(tok)
        return toks

    small = {}
    g_pre_mix, g_post_mix, g_pre_ffn, g_post_ffn = [None] * depth, [None] * depth, [None] * depth, [None] * depth
    df = None
    for i in reversed(range(depth)):
        j = i // 2
        s = saved[i]
        if df is None:
            df, g_post_ffn[i] = _rms_bwd(s["f"], row(post_ffn_g, i), dh, None, out_dtype=BF16,
                                         name=f"rms_post_ffn_bwd_{i}")
        g_down = _matmul(s["act"], df, mode="tn", out_dtype=BF16, name=f"ffn_down_dw_{i}")
        dg_, du_ = _ffn_down_dx(df, W[("ffn_w_down", i)][None], 0, s["g_pre"], s["u_pre"], name=f"ffn_down_dx_{i}")
        hid = dg_.shape[1]
        tile = _pick(hid, (1408, 768, 512, 256, 128))
        w_gu = W[("ffn_w_gu", i)]
        g_gu = lax.empty(w_gu.shape, BF16)
        g_gu = _matmul(s["fn"], dg_, mode="tn", into=g_gu, tq=tile, out_dtype=BF16, name=f"ffn_g_dw_{i}")
        g_gu = _matmul(s["fn"], du_, mode="tn", into=g_gu, tq=tile, q_off=hid // tile, out_dtype=BF16,
                       name=f"ffn_u_dw_{i}")
        dfn_g = _matmul(dg_, w_gu, mode="nt", tr=hid, out_dtype=F32, name=f"ffn_g_dx_{i}")
        dfn = _matmul(du_, w_gu, mode="nt", tr=hid, b_r_off=1, bias=dfn_g, out_dtype=F32, name=f"ffn_u_dx_{i}")
        toks = advance(dfn)
        if i == 0:
            gen = reduce_group("0f", layer_keys(0)[2:], [g_gu, g_down])
            toks.append(next(gen))
            reducing.append(gen)
        dh1, dmix, g_pre_ffn[i], g_post_mix[i] = _rms_bwd_chain(
            s["h1"], dep(row(pre_ffn_g, i), toks), dfn, dh, s["mix"], row(post_mix_g, i), name=f"rms_ffn_mix_bwd_{i}")
        if i % 2 == 0:
            g_out = _matmul(s["gated"], dmix, mode="tn", out_dtype=BF16, name=f"gmlp_out_dw_{i}")
            dgated = _matmul(dmix, W[("a_w_out", j)], mode="nt", out_dtype=F32, name=f"gmlp_out_dx_{i}")
            if i == 0:
                advance(dgated)
            dpre, dws, dbsT, dlng, dlnb, dbin = _sgu_bwd(s["pre"], dgated, row(a_ln_g, j), row(a_ln_b, j),
                                                         a_w_s[j], a_b_s[j].T, name=f"sgu_bwd_{i}")
            small[("a_w_s", j)] = dws
            small[("a_b_s", j)] = dbsT.T
            small[("a_ln_g", j)] = dlng
            small[("a_ln_b", j)] = dlnb
            small[("a_b_in", j)] = dbin
            g_in = _matmul(s["hn"], dpre, mode="tn", out_dtype=BF16, name=f"gmlp_in_dw_{i}")
            dhn = _matmul(dpre, W[("a_w_in", j)], mode="nt", out_dtype=F32, name=f"gmlp_in_dx_{i}")
        else:
            g_out = _matmul(s["o"], dmix, mode="tn", out_dtype=BF16, name=f"attn_o_dw_{i}")
            do = _matmul(dmix, W[("b_w_o", j)], mode="nt", out_dtype=BF16, name=f"attn_o_dx_{i}")
            dq, dkp, dkc, dvp, dvc, dsk = _attn_bwd(s["qr"], s["kr"], s["vr"], row(b_sinks, j), do,
                                                    name=f"attn_bwd_{i}")
            dqkv, dbq = _rope_bwd(dq, dkp, dkc, dvp, dvc, ctab, stab, name=f"rope_bwd_{i}")
            small[("b_sinks", j)] = dsk[:, :b_sinks.shape[1]]
            small[("b_b_qkv", j)] = dbq
            g_in = _matmul(s["hn"], dqkv, mode="tn", out_dtype=BF16, name=f"attn_qkv_dw_{i}")
            dhn = _matmul(dqkv, W[("b_w_qkv", j)], mode="nt", out_dtype=F32, name=f"attn_qkv_dx_{i}")
        toks = advance(dhn)
        if i > 0:
            dh, df, g_pre_mix[i], g_post_ffn[i - 1] = _rms_bwd_chain(
                s["h"], dep(row(pre_mix_g, i), toks), dhn, dh1, saved[i - 1]["f"], row(post_ffn_g, i - 1),
                name=f"rms_mix_ffn_bwd_{i}")
        else:
            dh, g_pre_mix[i] = _rms_bwd(s["h"], dep(row(pre_mix_g, i), toks), dhn, dh1, out_dtype=F32,
                                        name=f"rms_pre_mix_bwd_{i}")
        if i == 0:
            gen = reduce_group("0m", layer_keys(0)[:2], [g_in, g_out])
        else:
            gen = reduce_group(str(i), layer_keys(i), [g_in, g_out, g_gu, g_down])
        toks = [next(gen)] + advance(dh)
        reducing.append(gen)
    grad_x = dh[None]

    toks = advance(dh)
    n_a, n_b = a_b_in.shape[0], b_sinks.shape[0]
    stack = lambda key, n: jnp.concatenate([small[(key, j)] for j in range(n)], axis=0)
    small_parts = [
        jnp.concatenate(g_pre_mix, axis=0), jnp.concatenate(g_post_mix, axis=0),
        jnp.concatenate(g_pre_ffn, axis=0), jnp.concatenate(g_post_ffn, axis=0),
        stack("a_b_in", n_a), stack("a_ln_g", n_a), stack("a_ln_b", n_a),
        jnp.stack([small[("a_w_s", j)] for j in range(n_a)]), jnp.stack([small[("a_b_s", j)] for j in range(n_a)]),
        stack("b_b_qkv", n_b), stack("b_sinks", n_b),
    ]
    packed, metas = _pack_rows(small_parts)
    reduced = _allreduce_small(dep(packed, toks))
    while reducing:
        advance(reduced)
    red = _unpack_rows(reduced, metas)
    (gr_pre_mix, gr_post_mix, gr_pre_ffn, gr_post_ffn, gr_b_in, gr_ln_g, gr_ln_b, gr_w_s, gr_b_s,
     gr_b_qkv_full, gr_sinks) = red
    gr_b_qkv = lax.dynamic_slice(gr_b_qkv_full, (0, chip * nq), (gr_b_qkv_full.shape[0], nq))

    grads = {"pre_mix_g": gr_pre_mix, "post_mix_g": gr_post_mix, "pre_ffn_g": gr_pre_ffn, "post_ffn_g": gr_post_ffn,
             "a_b_in": gr_b_in, "a_ln_g": gr_ln_g, "a_ln_b": gr_ln_b, "a_w_s": gr_w_s, "a_b_s": gr_b_s,
             "b_b_qkv": gr_b_qkv, "b_sinks": gr_sinks}
    weights = {"pre_mix_g": (pre_mix_g, m_pre_mix_g, v_pre_mix_g), "post_mix_g": (post_mix_g, m_post_mix_g, v_post_mix_g),
               "pre_ffn_g": (pre_ffn_g, m_pre_ffn_g, v_pre_ffn_g), "post_ffn_g": (post_ffn_g, m_post_ffn_g, v_post_ffn_g),
               "a_b_in": (a_b_in, m_a_b_in, v_a_b_in), "a_ln_g": (a_ln_g, m_a_ln_g, v_a_ln_g),
               "a_ln_b": (a_ln_b, m_a_ln_b, v_a_ln_b), "a_w_s": (a_w_s, m_a_w_s, v_a_w_s), "a_b_s": (a_b_s, m_a_b_s, v_a_b_s),
               "b_b_qkv": (b_b_qkv, m_b_b_qkv, v_b_b_qkv), "b_sinks": (b_sinks, m_b_sinks, v_b_sinks)}
    order = ["pre_mix_g", "post_mix_g", "pre_ffn_g", "post_ffn_g", "a_w_in", "a_b_in", "a_ln_g", "a_ln_b", "a_w_s",
             "a_b_s", "a_w_out", "b_w_qkv", "b_b_qkv", "b_sinks", "b_w_o", "ffn_w_gu", "ffn_w_down"]
    deltas, new_m, new_v = {}, {}, {}
    for nm in order:
        if nm in big_out:
            grads[nm], deltas[nm], new_m[nm], new_v[nm] = big_out[nm]
        else:
            w, m, v = weights[nm]
            deltas[nm], new_m[nm], new_v[nm] = _adamw_small(w, grads[nm], m, v, name="adamw_" + nm)
    return (loss, grad_x, *[grads[nm] for nm in order], *[deltas[nm] for nm in order],
            *[new_m[nm] for nm in order], *[new_v[nm] for nm in order])
```

```python
import functools
import math

import jax
import jax.numpy as jnp
import numpy as np
from jax import lax
from jax.experimental import pallas as pl
from jax.experimental.pallas import tpu as pltpu

F32 = jnp.float32
BF16 = jnp.bfloat16
MESH = pl.DeviceIdType.MESH

HEAD_DIM = 64
N_KV_HEADS = 4
ROPE_DIM = 16
ROPE_THETA = 500000.0
CHUNK = 128
GMLP_GROUPS = 8
RMS_EPS = 1e-6
LN_EPS = 1e-5
NEG_INF = -1e30
ADAM_LR = 0.001
ADAM_B1 = 0.9
ADAM_B2 = 0.999
ADAM_EPS = 1e-08
ADAM_WD = 0.01
ADAM_STEP = 10

N_CHIPS = 4
LANES = 128
VMEM_CAP = 58 * 1024 * 1024


def _vmem(est_bytes):
    assert est_bytes < VMEM_CAP
    return VMEM_CAP


def _pick(n, cands):
    for c in cands:
        if c <= n and n % c == 0:
            return c
    return n


def _nbytes(shape, dtype):
    return int(np.prod(shape)) * jnp.dtype(dtype).itemsize


MATMUL_VMEM_BUDGET = 48 * 1024 * 1024


def _halvings(n, unit):
    out, t = [], n
    while t % unit == 0 and t >= unit:
        out.append(t)
        if t % 2:
            break
        t //= 2
    return out


def _matmul_tiles(P, Q, R, a_bytes, b_bytes, o_bytes, full_addend, tp, tq, tr):
    step_us, bytes_per_us = 0.85, 3.2e6
    best = None
    for p in ([tp] if tp else _halvings(P, LANES)):
        for q in ([tq] if tq else _halvings(Q, LANES)):
            for r in ([tr] if tr else _halvings(R, LANES)):
                nk = R // r
                vm = 2 * (p * r * a_bytes + r * q * b_bytes + p * q * o_bytes + (p * q * 4 if full_addend else 0))
                vm += p * q * 4 * (2 if nk > 1 else 1)
                if vm > MATMUL_VMEM_BUDGET:
                    continue
                exposed = (p * r * a_bytes + r * q * b_bytes + p * q * o_bytes) / bytes_per_us
                key = ((P // p) * (Q // q) * nk * step_us + exposed, nk, abs(p - q))
                if best is None or key < best[0]:
                    best = (key, (p, q, r))
    assert best is not None, (P, Q, R)
    return best[1]


def _matmul(a, b, *, mode, out_dtype, name, a_l=None, b_l=None, bias=None, into=None, o_l=None,
            q_off=0, b_r_off=0, tp=None, tq=None, tr=None):
    a2 = a.shape[-2:]
    b2 = b.shape[-2:]
    if mode == "nn":
        (P, R), (R2, Q) = a2, b2
    elif mode == "nt":
        (P, R), (Q, R2) = a2, b2
    else:
        (R, P), (R2, Q) = a2, b2
    assert R == R2 or (mode == "nt" and R2 % R == 0), (mode, a.shape, b.shape)
    o_bytes = jnp.dtype(into.dtype if into is not None else out_dtype).itemsize
    full_addend = bias is not None and bias.shape[0] != 1
    tp, tq, tr = _matmul_tiles(P, Q, R, a.dtype.itemsize, b.dtype.itemsize, o_bytes, full_addend, tp, tq, tr)
    assert P % tp == 0 and Q % tq == 0 and R % tr == 0
    nk = R // tr
    dims = {"nn": (((1,), (0,)), ((), ())), "nt": (((1,), (1,)), ((), ())), "tn": (((0,), (0,)), ((), ()))}[mode]

    def lead(l, blk, idx):
        if l is None:
            return pl.BlockSpec(blk, idx)
        return pl.BlockSpec((None,) + blk, lambda i, j, k: (l,) + idx(i, j, k))

    if mode == "nn":
        a_spec = lead(a_l, (tp, tr), lambda i, j, k: (i, k))
        b_spec = lead(b_l, (tr, tq), lambda i, j, k: (k, j))
    elif mode == "nt":
        a_spec = lead(a_l, (tp, tr), lambda i, j, k: (i, k))
        b_spec = lead(b_l, (tq, tr), lambda i, j, k: (j, k + b_r_off))
    else:
        a_spec = lead(a_l, (tr, tp), lambda i, j, k: (k, i))
        b_spec = lead(b_l, (tr, tq), lambda i, j, k: (k, j))
    in_specs = [a_spec, b_spec]
    args = [a, b]
    if bias is not None:
        if bias.shape[0] == 1:
            in_specs.append(pl.BlockSpec((1, tq), lambda i, j, k: (0, j)))
        else:
            in_specs.append(pl.BlockSpec((tp, tq), lambda i, j, k: (i, j)))
        args.append(bias)
    aliases = {}
    if into is not None:
        in_specs.append(pl.BlockSpec(memory_space=pl.ANY))
        args.append(into)
        aliases = {len(args) - 1: 0}
        out_shape = jax.ShapeDtypeStruct(into.shape, into.dtype)
        out_dtype = into.dtype
        if o_l is None:
            out_spec = pl.BlockSpec((tp, tq), lambda i, j, k: (i, j + q_off))
        else:
            out_spec = pl.BlockSpec((None, tp, tq), lambda i, j, k: (o_l, i, j + q_off))
    else:
        out_shape = jax.ShapeDtypeStruct((P, Q), out_dtype)
        out_spec = pl.BlockSpec((tp, tq), lambda i, j, k: (i, j))
    has_bias = bias is not None
    has_into = into is not None

    def body(*refs):
        a_ref, b_ref = refs[0], refs[1]
        pos = 2
        bias_ref = None
        if has_bias:
            bias_ref = refs[pos]
            pos += 1
        if has_into:
            pos += 1
        o_ref = refs[pos]
        acc_ref = refs[pos + 1] if nk > 1 else None
        part = lax.dot_general(a_ref[...], b_ref[...], dims, preferred_element_type=F32)

        def finish(acc):
            if has_bias:
                acc = acc + bias_ref[...]
            o_ref[...] = acc.astype(out_dtype)

        if nk == 1:
            finish(part)
        else:
            k = pl.program_id(2)

            @pl.when(k == 0)
            def _():
                acc_ref[...] = part

            @pl.when(k > 0)
            def _():
                acc_ref[...] += part

            @pl.when(k == nk - 1)
            def _():
                finish(acc_ref[...])

    est = 2 * (_nbytes((tp, tr), a.dtype) + _nbytes((tr, tq), b.dtype) + _nbytes((tp, tq), out_dtype)) + 3 * tp * tq * 4
    return pl.pallas_call(
        body, name=name, out_shape=out_shape,
        grid=(P // tp, Q // tq, nk),
        in_specs=in_specs, out_specs=out_spec,
        scratch_shapes=[pltpu.VMEM((tp, tq), F32)] if nk > 1 else [],
        input_output_aliases=aliases,
        compiler_params=pltpu.CompilerParams(
            dimension_semantics=("parallel", "parallel", "arbitrary"), vmem_limit_bytes=_vmem(est)),
    )(*args)


def _row_call(body, ins, outs, *, name, rows, tr, acc_outs=(), est=0):
    in_specs = []
    for arr, kind in ins:
        if kind == "row":
            in_specs.append(pl.BlockSpec((tr, arr.shape[1]), lambda i: (i, 0)))
        else:
            nd = arr.ndim
            in_specs.append(pl.BlockSpec(arr.shape, lambda i, nd=nd: (0,) * nd))
    out_shapes = [jax.ShapeDtypeStruct(s, d) for s, d in outs] + [jax.ShapeDtypeStruct(s, d) for s, d in acc_outs]
    out_specs = [pl.BlockSpec((tr, s[1]), lambda i: (i, 0)) for s, _ in outs]
    out_specs += [pl.BlockSpec(s, lambda i, nd=len(s): (0,) * nd) for s, _ in acc_outs]
    res = pl.pallas_call(
        body, name=name, out_shape=out_shapes, grid=(rows // tr,), in_specs=in_specs, out_specs=out_specs,
        compiler_params=pltpu.CompilerParams(dimension_semantics=("arbitrary",), vmem_limit_bytes=_vmem(est)),
    )(*[a for a, _ in ins])
    return res


def _rms_fwd(x, g, *, out_dtype, name):
    T, D = x.shape
    tr = _pick(T, (512, 256, 128))

    def body(x_ref, g_ref, o_ref):
        xv = x_ref[...]
        r = lax.rsqrt(jnp.mean(xv * xv, axis=-1, keepdims=True) + RMS_EPS)
        o_ref[...] = (xv * r * g_ref[...]).astype(out_dtype)

    return _row_call(body, [(x, "row"), (g, "full")], [((T, D), out_dtype)], name=name, rows=T, tr=tr,
                     est=8 * tr * D * 4)[0]


def _rms_res(h, y, g, *, name):
    T, D = h.shape
    tr = _pick(T, (512, 256, 128))

    def body(h_ref, y_ref, g_ref, o_ref):
        yv = y_ref[...]
        r = lax.rsqrt(jnp.mean(yv * yv, axis=-1, keepdims=True) + RMS_EPS)
        o_ref[...] = h_ref[...] + yv * r * g_ref[...]

    return _row_call(body, [(h, "row"), (y, "row"), (g, "full")], [((T, D), F32)], name=name, rows=T, tr=tr,
                     est=10 * tr * D * 4)[0]


def _rms_bwd(x, g, dy, dres, *, out_dtype, name):
    T, D = x.shape
    tr = _pick(T, (512, 256, 128))
    has_res = dres is not None

    def body(*refs):
        if has_res:
            x_ref, g_ref, dy_ref, dr_ref, dx_ref, dg_ref = refs
        else:
            x_ref, g_ref, dy_ref, dx_ref, dg_ref = refs
        xv = x_ref[...]
        r = lax.rsqrt(jnp.mean(xv * xv, axis=-1, keepdims=True) + RMS_EPS)
        xhat = xv * r
        dyv = dy_ref[...].astype(F32)
        dxn = dyv * g_ref[...]
        dx = r * (dxn - xhat * jnp.mean(dxn * xhat, axis=-1, keepdims=True))
        if has_res:
            dx = dx + dr_ref[...]
        dx_ref[...] = dx.astype(out_dtype)
        part = jnp.sum(dyv * xhat, axis=0, keepdims=True)

        @pl.when(pl.program_id(0) == 0)
        def _():
            dg_ref[...] = part

        @pl.when(pl.program_id(0) > 0)
        def _():
            dg_ref[...] += part

    ins = [(x, "row"), (g, "full"), (dy, "row")] + ([(dres, "row")] if has_res else [])
    dx, dg = _row_call(body, ins, [((T, D), out_dtype)], name=name, rows=T, tr=tr, acc_outs=[((1, D), F32)],
                       est=12 * tr * D * 4)
    return dx, dg


def _rms_res_norm(h, y, g_res, g_next, *, name):
    T, D = h.shape
    tr = _pick(T, (512, 256, 128))

    def body(h_ref, y_ref, g_ref, gn_ref, o_ref, n_ref):
        yv = y_ref[...]
        r = lax.rsqrt(jnp.mean(yv * yv, axis=-1, keepdims=True) + RMS_EPS)
        h2 = h_ref[...] + yv * r * g_ref[...]
        o_ref[...] = h2
        r2 = lax.rsqrt(jnp.mean(h2 * h2, axis=-1, keepdims=True) + RMS_EPS)
        n_ref[...] = (h2 * r2 * gn_ref[...]).astype(BF16)

    return _row_call(body, [(h, "row"), (y, "row"), (g_res, "full"), (g_next, "full")],
                     [((T, D), F32), ((T, D), BF16)], name=name, rows=T, tr=tr, est=12 * tr * D * 4)


def _rms_bwd_chain(x1, g1, dy1, dres, x2, g2, *, name):
    T, D = x1.shape
    tr = _pick(T, (512, 256, 128))

    def one(xv, gv, dyv):
        r = lax.rsqrt(jnp.mean(xv * xv, axis=-1, keepdims=True) + RMS_EPS)
        xhat = xv * r
        dxn = dyv * gv
        dx = r * (dxn - xhat * jnp.mean(dxn * xhat, axis=-1, keepdims=True))
        return dx, jnp.sum(dyv * xhat, axis=0, keepdims=True)

    def body(x1_ref, g1_ref, dy1_ref, dr_ref, x2_ref, g2_ref, d1_ref, d2_ref, dg1_ref, dg2_ref):
        dx1, p1 = one(x1_ref[...], g1_ref[...], dy1_ref[...].astype(F32))
        d1 = dx1 + dr_ref[...]
        d1_ref[...] = d1
        dx2, p2 = one(x2_ref[...], g2_ref[...], d1)
        d2_ref[...] = dx2.astype(BF16)

        @pl.when(pl.program_id(0) == 0)
        def _():
            dg1_ref[...] = p1
            dg2_ref[...] = p2

        @pl.when(pl.program_id(0) > 0)
        def _():
            dg1_ref[...] += p1
            dg2_ref[...] += p2

    ins = [(x1, "row"), (g1, "full"), (dy1, "row"), (dres, "row"), (x2, "row"), (g2, "full")]
    return _row_call(body, ins, [((T, D), F32), ((T, D), BF16)], name=name, rows=T, tr=tr,
                     acc_outs=[((1, D), F32), ((1, D), F32)], est=20 * tr * D * 4)


def _ffn_up(fn, w_gu, l, *, name):
    T, D = fn.shape
    H = w_gu.shape[2] // 2
    tp = _pick(T, (1024, 512, 256, 128))
    tq = _pick(H, (1408, 768, 512, 256, 128))
    nj = H // tq

    def body(a_ref, wg_ref, wu_ref, g_ref, u_ref, act_ref):
        a = a_ref[...]
        g = jnp.dot(a, wg_ref[...], preferred_element_type=F32)
        u = jnp.dot(a, wu_ref[...], preferred_element_type=F32)
        g_ref[...] = g.astype(BF16)
        u_ref[...] = u.astype(BF16)
        act_ref[...] = (g * jax.nn.sigmoid(g) * u).astype(BF16)

    tile = pl.BlockSpec((tp, tq), lambda j, i: (i, j))
    est = 2 * (tp * D * 2 + 2 * D * tq * 2 + 3 * tp * tq * 2) + 4 * tp * tq * 4
    return pl.pallas_call(
        body, name=name,
        out_shape=[jax.ShapeDtypeStruct((T, H), BF16), jax.ShapeDtypeStruct((T, H), BF16),
                   jax.ShapeDtypeStruct((T, H), BF16)],
        grid=(nj, T // tp),
        in_specs=[pl.BlockSpec((tp, D), lambda j, i: (i, 0)),
                  pl.BlockSpec((None, D, tq), lambda j, i: (l, 0, j)),
                  pl.BlockSpec((None, D, tq), lambda j, i: (l, 0, j + nj))],
        out_specs=[tile, tile, tile],
        compiler_params=pltpu.CompilerParams(dimension_semantics=("parallel", "parallel"),
                                             vmem_limit_bytes=_vmem(est)),
    )(fn, w_gu, w_gu)


def _ffn_down_dx(df, w_down, l, g, u, *, name):
    T, D = df.shape
    H = w_down.shape[1]
    tp = _pick(T, (512, 256, 128))
    tq = _pick(H, (1408, 768, 512, 256, 128))

    def body(a_ref, w_ref, g_ref, u_ref, dg_ref, du_ref):
        da = lax.dot_general(a_ref[...], w_ref[...], (((1,), (1,)), ((), ())), preferred_element_type=F32)
        gv = g_ref[...].astype(F32)
        sg = jax.nn.sigmoid(gv)
        silu = gv * sg
        dg_ref[...] = (da * u_ref[...].astype(F32) * (sg + silu * (1.0 - sg))).astype(BF16)
        du_ref[...] = (da * silu).astype(BF16)

    tile = pl.BlockSpec((tp, tq), lambda j, i: (i, j))
    est = 2 * (tp * D * 2 + tq * D * 2 + 4 * tp * tq * 2) + 5 * tp * tq * 4
    return pl.pallas_call(
        body, name=name,
        out_shape=[jax.ShapeDtypeStruct((T, H), BF16), jax.ShapeDtypeStruct((T, H), BF16)],
        grid=(H // tq, T // tp),
        in_specs=[pl.BlockSpec((tp, D), lambda j, i: (i, 0)),
                  pl.BlockSpec((None, tq, D), lambda j, i: (l, j, 0)), tile, tile],
        out_specs=[tile, tile],
        compiler_params=pltpu.CompilerParams(dimension_semantics=("parallel", "parallel"),
                                             vmem_limit_bytes=_vmem(est)),
    )(df, w_down, g, u)


def _loss_and_grad(y, target, *, name):
    T, D = y.shape
    tr = _pick(T, (512, 256, 128))

    def body(y_ref, t_ref, dy_ref, l_ref):
        e = y_ref[...] - t_ref[...]
        dy_ref[...] = e * (1.0 / D)
        part = jnp.sum(jnp.sum(e * e, axis=1, keepdims=True), axis=0, keepdims=True) * (0.5 / D)

        @pl.when(pl.program_id(0) == 0)
        def _():
            l_ref[...] = part

        @pl.when(pl.program_id(0) > 0)
        def _():
            l_ref[...] += part

    dy, l = _row_call(body, [(y, "row"), (target, "row")], [((T, D), F32)], name=name, rows=T, tr=tr,
                      acc_outs=[((1, 1), F32)], est=8 * tr * D * 4)
    return dy, l


_SQRT_HALF = 0.7071067811865476
_INV_SQRT_2PI = 0.3989422804014327


def _gelu_parts(x):
    cdf = 0.5 * (1.0 + lax.erf(x * _SQRT_HALF))
    return cdf


def _sgu_common(pre, lng, lnb, W):
    cdf = _gelu_parts(pre)
    z = pre * cdf
    u = z[:, :W]
    v = z[:, W:]
    mu = jnp.mean(v, axis=-1, keepdims=True)
    vc = v - mu
    var = jnp.mean(vc * vc, axis=-1, keepdims=True)
    rstd = lax.rsqrt(var + LN_EPS)
    vhat = vc * rstd
    vn = vhat * lng + lnb
    return cdf, u, vhat, rstd, vn


def _causal_mask():
    t = lax.broadcasted_iota(jnp.int32, (CHUNK, CHUNK), 0)
    s = lax.broadcasted_iota(jnp.int32, (CHUNK, CHUNK), 1)
    return t >= s


def _sgu_fwd(pre, lng, lnb, ws, bsT, *, name):
    T, W2 = pre.shape
    W = W2 // 2
    G = ws.shape[0]
    gd = W // G

    def body(pre_ref, lng_ref, lnb_ref, ws_ref, bs_ref, o_ref):
        _, u, _, _, vn = _sgu_common(pre_ref[...], lng_ref[...], lnb_ref[...], W)
        vnb = vn.astype(BF16)
        causal = _causal_mask()
        for g in range(G):
            w = jnp.where(causal, ws_ref[g], 0.0).astype(BF16)
            sv = jnp.dot(w, vnb[:, g * gd:(g + 1) * gd], preferred_element_type=F32) + bs_ref[:, g:g + 1]
            o_ref[:, g * gd:(g + 1) * gd] = (u[:, g * gd:(g + 1) * gd] * sv).astype(BF16)

    return pl.pallas_call(
        body, name=name, out_shape=jax.ShapeDtypeStruct((T, W), BF16), grid=(T // CHUNK,),
        in_specs=[pl.BlockSpec((CHUNK, W2), lambda i: (i, 0)),
                  pl.BlockSpec((1, W), lambda i: (0, 0)), pl.BlockSpec((1, W), lambda i: (0, 0)),
                  pl.BlockSpec(ws.shape, lambda i: (0, 0, 0)), pl.BlockSpec(bsT.shape, lambda i: (0, 0))],
        out_specs=pl.BlockSpec((CHUNK, W), lambda i: (i, 0)),
        compiler_params=pltpu.CompilerParams(dimension_semantics=("arbitrary",),
                                             vmem_limit_bytes=_vmem(12 * CHUNK * W2 * 4)),
    )(pre, lng, lnb, ws, bsT)


def _sgu_bwd(pre, dgated, lng, lnb, ws, bsT, *, name):
    T, W2 = pre.shape
    W = W2 // 2
    G = ws.shape[0]
    gd = W // G

    def body(pre_ref, dgt_ref, lng_ref, lnb_ref, ws_ref, bs_ref,
             dpre_ref, dws_ref, dbs_ref, dlng_ref, dlnb_ref, dbin_ref):
        first = pl.program_id(0) == 0

        @pl.when(first)
        def _():
            dws_ref[...] = jnp.zeros_like(dws_ref)
            dbs_ref[...] = jnp.zeros_like(dbs_ref)
            dlng_ref[...] = jnp.zeros_like(dlng_ref)
            dlnb_ref[...] = jnp.zeros_like(dlnb_ref)
            dbin_ref[...] = jnp.zeros_like(dbin_ref)

        pre_v = pre_ref[...]
        lng_v = lng_ref[...]
        cdf, u, vhat, rstd, vn = _sgu_common(pre_v, lng_v, lnb_ref[...], W)
        vnb = vn.astype(BF16)
        dgt = dgt_ref[...].astype(F32)
        causal = _causal_mask()
        du_parts, dvn_parts = [], []
        for g in range(G):
            sl = slice(g * gd, (g + 1) * gd)
            w = jnp.where(causal, ws_ref[g], 0.0).astype(BF16)
            sv = jnp.dot(w, vnb[:, sl], preferred_element_type=F32) + bs_ref[:, g:g + 1]
            dgt_g = dgt[:, sl]
            du_parts.append(dgt_g * sv)
            dsv = dgt_g * u[:, sl]
            dsvb = dsv.astype(BF16)
            dvn_parts.append(lax.dot_general(w, dsvb, (((0,), (0,)), ((), ())), preferred_element_type=F32))
            dw = lax.dot_general(dsvb, vnb[:, sl], (((1,), (1,)), ((), ())), preferred_element_type=F32)
            dws_ref[g] += jnp.where(causal, dw, 0.0)
            dbs_ref[:, g:g + 1] += jnp.sum(dsv, axis=1, keepdims=True)
        du = jnp.concatenate(du_parts, axis=1)
        dvn = jnp.concatenate(dvn_parts, axis=1)
        dlng_ref[...] += jnp.sum(dvn * vhat, axis=0, keepdims=True)
        dlnb_ref[...] += jnp.sum(dvn, axis=0, keepdims=True)
        dvh = dvn * lng_v
        dv = rstd * (dvh - jnp.mean(dvh, axis=-1, keepdims=True)
                     - vhat * jnp.mean(dvh * vhat, axis=-1, keepdims=True))
        dz = jnp.concatenate([du, dv], axis=1)
        dgelu = cdf + pre_v * jnp.exp(-0.5 * pre_v * pre_v) * _INV_SQRT_2PI
        dpre = dz * dgelu
        dbin_ref[...] += jnp.sum(dpre, axis=0, keepdims=True)
        dpre_ref[...] = dpre.astype(BF16)

    full = lambda shape: pl.BlockSpec(shape, lambda i, nd=len(shape): (0,) * nd)
    return pl.pallas_call(
        body, name=name,
        out_shape=[jax.ShapeDtypeStruct((T, W2), BF16), jax.ShapeDtypeStruct(ws.shape, F32),
                   jax.ShapeDtypeStruct(bsT.shape, F32), jax.ShapeDtypeStruct((1, W), F32),
                   jax.ShapeDtypeStruct((1, W), F32), jax.ShapeDtypeStruct((1, W2), F32)],
        grid=(T // CHUNK,),
        in_specs=[pl.BlockSpec((CHUNK, W2), lambda i: (i, 0)), pl.BlockSpec((CHUNK, W), lambda i: (i, 0)),
                  full((1, W)), full((1, W)), full(ws.shape), full(bsT.shape)],
        out_specs=[pl.BlockSpec((CHUNK, W2), lambda i: (i, 0)), full(ws.shape), full(bsT.shape),
                   full((1, W)), full((1, W)), full((1, W2))],
        compiler_params=pltpu.CompilerParams(dimension_semantics=("arbitrary",),
                                             vmem_limit_bytes=_vmem(24 * CHUNK * W2 * 4)),
    )(pre, dgated, lng, lnb, ws, bsT)


def _rope_tables(positions):
    half = ROPE_DIM // 2
    inv_freq = ROPE_THETA ** (-jnp.arange(0, ROPE_DIM, 2, dtype=F32) / ROPE_DIM)
    ang = positions.astype(F32).reshape(-1, 1) * inv_freq
    cos, sin = jnp.cos(ang), jnp.sin(ang)
    T = ang.shape[0]
    rest = HEAD_DIM - ROPE_DIM
    c64 = jnp.concatenate([cos, cos, jnp.ones((T, rest), F32)], axis=1)
    s64 = jnp.concatenate([-sin, sin, jnp.zeros((T, rest), F32)], axis=1)
    del half
    return jnp.tile(c64, (1, LANES // HEAD_DIM)), jnp.tile(s64, (1, LANES // HEAD_DIM))


def _swap8(x):
    W = x.shape[1]
    half = ROPE_DIM // 2
    lane = lax.broadcasted_iota(jnp.int32, x.shape, 1) % HEAD_DIM
    return jnp.where(lane < half, pltpu.roll(x, W - half, axis=1),
                     jnp.where(lane < ROPE_DIM, pltpu.roll(x, half, axis=1), 0.0))


def _wide(tab, W):
    return jnp.concatenate([tab] * (W // LANES), axis=1) if W > LANES else tab


def _rope_fwd(qkv, ctab, stab, *, q_width, kv_width, name):
    T = qkv.shape[0]
    tr = _pick(T, (256, 128))
    scale = HEAD_DIM ** -0.5

    def body(x_ref, c_ref, s_ref, q_ref, k_ref, v_ref):
        c = c_ref[...]
        s = s_ref[...]
        q = x_ref[:, :q_width]
        k = x_ref[:, q_width:q_width + kv_width]
        q_ref[...] = ((q * _wide(c, q_width) + _swap8(q) * _wide(s, q_width)) * scale).astype(BF16)
        k_ref[...] = (k * _wide(c, kv_width) + _swap8(k) * _wide(s, kv_width)).astype(BF16)
        v_ref[...] = x_ref[:, q_width + kv_width:].astype(BF16)

    return _row_call(body, [(qkv, "row"), (ctab, "row"), (stab, "row")],
                     [((T, q_width), BF16), ((T, kv_width), BF16), ((T, kv_width), BF16)],
                     name=name, rows=T, tr=tr, est=10 * tr * qkv.shape[1] * 4)


_NT = (((1,), (1,)), ((), ()))
_TN = (((0,), (0,)), ((), ()))


def _group_rows(ref, heads):
    return jnp.concatenate([ref[:, h * HEAD_DIM:(h + 1) * HEAD_DIM] for h in heads], axis=0)


def _attn_valid(grp):
    qi = np.arange(grp * CHUNK)[:, None] % CHUNK
    sj = np.arange(2 * CHUNK)[None, :]
    cur = (sj >= CHUNK) & (sj - CHUNK <= qi)
    prev = (sj < CHUNK) & (sj > qi)
    return jnp.asarray(np.stack([cur, cur | prev]).astype(np.float32))


def _valid_spec(grp):
    return pl.BlockSpec((None, grp * CHUNK, 2 * CHUNK), lambda n: (jnp.minimum(n, 1), 0, 0))


def _attn_group_probs(q, kk, sinks, valid, grp):
    rows = grp * CHUNK
    s = lax.dot_general(q, kk, _NT, preferred_element_type=F32)
    s = jnp.where(valid, s, NEG_INF)
    r = lax.broadcasted_iota(jnp.int32, (rows, 1), 0)
    sink = jnp.full((rows, 1), sinks[grp - 1], F32)
    for g in range(grp - 2, -1, -1):
        sink = jnp.where(r < (g + 1) * CHUNK, sinks[g], sink)
    m = jnp.maximum(jnp.max(s, axis=1, keepdims=True), sink)
    p = jnp.exp(s - m)
    ps = jnp.exp(sink - m)
    inv = 1.0 / (jnp.sum(p, axis=1, keepdims=True) + ps)
    return p * inv, ps * inv


def _kv_specs(width, nb):
    prev = pl.BlockSpec((CHUNK, width), lambda n: (jnp.maximum(n - 1, 0), 0))
    cur = pl.BlockSpec((CHUNK, width), lambda n: (n, 0))
    return prev, cur


def _attn_fwd(qr, kr, vr, sinks, *, name):
    T, QW = qr.shape
    KW = kr.shape[1]
    HQ, HK = QW // HEAD_DIM, KW // HEAD_DIM
    grp = HQ // HK
    nb = T // CHUNK

    def body(q_ref, kp_ref, kc_ref, vp_ref, vc_ref, s_ref, ok_ref, o_ref):
        valid = ok_ref[...] > 0.5
        for kh in range(HK):
            ks = slice(kh * HEAD_DIM, (kh + 1) * HEAD_DIM)
            heads = list(range(kh * grp, (kh + 1) * grp))
            q = _group_rows(q_ref, heads)
            kk = jnp.concatenate([kp_ref[:, ks], kc_ref[:, ks]], axis=0)
            vv = jnp.concatenate([vp_ref[:, ks], vc_ref[:, ks]], axis=0)
            p, _ = _attn_group_probs(q, kk, [s_ref[0, h] for h in heads], valid, grp)
            o = jnp.dot(p.astype(BF16), vv, preferred_element_type=F32).astype(BF16)
            for g, h in enumerate(heads):
                o_ref[:, h * HEAD_DIM:(h + 1) * HEAD_DIM] = o[g * CHUNK:(g + 1) * CHUNK]

    kp, kc = _kv_specs(KW, nb)
    return pl.pallas_call(
        body, name=name, out_shape=jax.ShapeDtypeStruct((T, QW), BF16), grid=(nb,),
        in_specs=[pl.BlockSpec((CHUNK, QW), lambda n: (n, 0)), kp, kc, kp, kc,
                  pl.BlockSpec(memory_space=pltpu.SMEM), _valid_spec(grp)],
        out_specs=pl.BlockSpec((CHUNK, QW), lambda n: (n, 0)),
        compiler_params=pltpu.CompilerParams(dimension_semantics=("arbitrary",), vmem_limit_bytes=_vmem(8 << 20)),
    )(qr, kr, kr, vr, vr, sinks, _attn_valid(grp))


def _attn_bwd(qr, kr, vr, sinks, do, *, name):
    T, QW = qr.shape
    KW = kr.shape[1]
    HQ, HK = QW // HEAD_DIM, KW // HEAD_DIM
    grp = HQ // HK
    nb = T // CHUNK

    def body(q_ref, kp_ref, kc_ref, vp_ref, vc_ref, s_ref, do_ref, ok_ref,
             dq_ref, dkp_ref, dkc_ref, dvp_ref, dvc_ref, ds_ref):
        n = pl.program_id(0)
        valid = ok_ref[...] > 0.5
        lane = lax.broadcasted_iota(jnp.int32, (1, LANES), 1)
        dsink = jnp.zeros((1, LANES), F32)
        for kh in range(HK):
            ks = slice(kh * HEAD_DIM, (kh + 1) * HEAD_DIM)
            heads = list(range(kh * grp, (kh + 1) * grp))
            q = _group_rows(q_ref, heads)
            doh = _group_rows(do_ref, heads)
            kk = jnp.concatenate([kp_ref[:, ks], kc_ref[:, ks]], axis=0)
            vv = jnp.concatenate([vp_ref[:, ks], vc_ref[:, ks]], axis=0)
            p, ps = _attn_group_probs(q, kk, [s_ref[0, h] for h in heads], valid, grp)
            dp = lax.dot_general(doh, vv, _NT, preferred_element_type=F32)
            delta = jnp.sum(p * dp, axis=1, keepdims=True)
            ds = (p * (dp - delta)).astype(BF16)
            dv = lax.dot_general(p.astype(BF16), doh, _TN, preferred_element_type=F32)
            dk = lax.dot_general(ds, q, _TN, preferred_element_type=F32)
            dq = jnp.dot(ds, kk, preferred_element_type=F32)
            psd = ps * delta
            for g, h in enumerate(heads):
                dq_ref[:, h * HEAD_DIM:(h + 1) * HEAD_DIM] = dq[g * CHUNK:(g + 1) * CHUNK]
                dsink = dsink + jnp.where(
                    lane == h, -jnp.sum(psd[g * CHUNK:(g + 1) * CHUNK], axis=0, keepdims=True), 0.0)
            dkp_ref[:, ks] = dk[:CHUNK]
            dkc_ref[:, ks] = dk[CHUNK:]
            dvp_ref[:, ks] = dv[:CHUNK]
            dvc_ref[:, ks] = dv[CHUNK:]

        @pl.when(n == 0)
        def _():
            ds_ref[...] = dsink

        @pl.when(n > 0)
        def _():
            ds_ref[...] += dsink

    kp, kc = _kv_specs(KW, nb)
    qspec = pl.BlockSpec((CHUNK, QW), lambda n: (n, 0))
    kout = pl.BlockSpec((CHUNK, KW), lambda n: (n, 0))
    return pl.pallas_call(
        body, name=name,
        out_shape=[jax.ShapeDtypeStruct((T, QW), F32)] + [jax.ShapeDtypeStruct((T, KW), F32)] * 4
        + [jax.ShapeDtypeStruct((1, LANES), F32)],
        grid=(nb,),
        in_specs=[qspec, kp, kc, kp, kc, pl.BlockSpec(memory_space=pltpu.SMEM), qspec, _valid_spec(grp)],
        out_specs=[qspec, kout, kout, kout, kout, pl.BlockSpec((1, LANES), lambda n: (0, 0))],
        compiler_params=pltpu.CompilerParams(dimension_semantics=("arbitrary",), vmem_limit_bytes=_vmem(12 << 20)),
    )(qr, kr, kr, vr, vr, sinks, do, _attn_valid(grp))


def _rope_bwd(dq, dkp, dkc, dvp, dvc, ctab, stab, *, name):
    T, QW = dq.shape
    KW = dkp.shape[1]
    nb = T // CHUNK
    scale = HEAD_DIM ** -0.5
    width = QW + 2 * KW

    def body(dq_ref, dkc_ref, dkn_ref, dvc_ref, dvn_ref, c_ref, s_ref, o_ref, db_ref):
        n = pl.program_id(0)
        c = c_ref[...]
        s = s_ref[...]
        has_next = (n < nb - 1).astype(F32)
        dqv = dq_ref[...]
        dk = dkc_ref[...] + has_next * dkn_ref[...]
        dv = dvc_ref[...] + has_next * dvn_ref[...]
        dq_pre = (dqv * _wide(c, QW) + _swap8(dqv * _wide(s, QW))) * scale
        dk_pre = dk * _wide(c, KW) + _swap8(dk * _wide(s, KW))
        o_ref[:, :QW] = dq_pre.astype(BF16)
        o_ref[:, QW:QW + KW] = dk_pre.astype(BF16)
        o_ref[:, QW + KW:] = dv.astype(BF16)
        part = jnp.concatenate([jnp.sum(dq_pre, axis=0, keepdims=True), jnp.sum(dk_pre, axis=0, keepdims=True),
                                jnp.sum(dv, axis=0, keepdims=True)], axis=1)

        @pl.when(n == 0)
        def _():
            db_ref[...] = part

        @pl.when(n > 0)
        def _():
            db_ref[...] += part

    cur = lambda w: pl.BlockSpec((CHUNK, w), lambda n: (n, 0))
    nxt = lambda w: pl.BlockSpec((CHUNK, w), lambda n: (jnp.minimum(n + 1, nb - 1), 0))
    return pl.pallas_call(
        body, name=name,
        out_shape=[jax.ShapeDtypeStruct((T, width), BF16), jax.ShapeDtypeStruct((1, width), F32)],
        grid=(nb,),
        in_specs=[cur(QW), cur(KW), nxt(KW), cur(KW), nxt(KW), cur(LANES), cur(LANES)],
        out_specs=[cur(width), pl.BlockSpec((1, width), lambda n: (0, 0))],
        compiler_params=pltpu.CompilerParams(dimension_semantics=("arbitrary",), vmem_limit_bytes=_vmem(8 << 20)),
    )(dq, dkc, dkp, dvc, dvp, ctab, stab)


def _cast_block(w, l, axis, chip_arr, *, name):
    _, Ks, Ns = w.shape
    tk = _pick(Ks, (512, 352, 256, 128))
    nk = Ks // tk
    full = (Ks * N_CHIPS, Ns) if axis == 0 else (Ks, Ns * N_CHIPS)

    def body(p_ref, w_ref, o_ref):
        o_ref[...] = w_ref[...].astype(BF16)

    if axis == 0:
        out_spec = pl.BlockSpec((tk, Ns), lambda i, p: (p[0] * nk + i, 0))
    else:
        out_spec = pl.BlockSpec((tk, Ns), lambda i, p: (i, p[0]))
    grid_spec = pltpu.PrefetchScalarGridSpec(
        num_scalar_prefetch=1, grid=(nk,),
        in_specs=[pl.BlockSpec((None, tk, Ns), lambda i, p: (l, i, 0))], out_specs=out_spec)
    return pl.pallas_call(
        body, name=name, out_shape=jax.ShapeDtypeStruct(full, BF16), grid_spec=grid_spec,
        compiler_params=pltpu.CompilerParams(dimension_semantics=("arbitrary",),
                                             vmem_limit_bytes=_vmem(4 * tk * Ns * 6)),
    )(chip_arr, w)


def _adamw_math(w, g, m, v):
    m = ADAM_B1 * m + (1.0 - ADAM_B1) * g
    v = ADAM_B2 * v + (1.0 - ADAM_B2) * (g * g)
    m_hat = m / (1.0 - ADAM_B1 ** ADAM_STEP)
    v_hat = v / (1.0 - ADAM_B2 ** ADAM_STEP)
    delta = -ADAM_LR * (m_hat / (jnp.sqrt(v_hat) + ADAM_EPS) + ADAM_WD * w)
    return delta, m, v


def _adamw_layer(w, m, v, g, l, outs, *, name):
    _, K, N = w.shape
    tk = _pick(K, (256, 176, 128))

    def body(w_ref, m_ref, v_ref, g_ref, _g, _d, _m, _v, go_ref, d_ref, mo_ref, vo_ref):
        gv = g_ref[...]
        d, mn, vn = _adamw_math(w_ref[...], gv, m_ref[...], v_ref[...])
        go_ref[...] = gv
        d_ref[...] = d
        mo_ref[...] = mn
        vo_ref[...] = vn

    layer = pl.BlockSpec((None, tk, N), lambda i: (l, i, 0))
    any_spec = pl.BlockSpec(memory_space=pl.ANY)
    sd = jax.ShapeDtypeStruct(w.shape, F32)
    return pl.pallas_call(
        body, name=name, out_shape=[sd, sd, sd, sd], grid=(K // tk,),
        in_specs=[layer, layer, layer, pl.BlockSpec((tk, N), lambda i: (i, 0))] + [any_spec] * 4,
        out_specs=[layer] * 4, input_output_aliases={4: 0, 5: 1, 6: 2, 7: 3},
        compiler_params=pltpu.CompilerParams(dimension_semantics=("arbitrary",),
                                             vmem_limit_bytes=_vmem(2 * 8 * tk * N * 4 + 6 * tk * N * 4)),
    )(w, m, v, g, *outs)


def _adamw_small(w, g, m, v, *, name):
    def body(w_ref, g_ref, m_ref, v_ref, d_ref, mo_ref, vo_ref):
        d, mn, vn = _adamw_math(w_ref[...], g_ref[...], m_ref[...], v_ref[...])
        d_ref[...] = d
        mo_ref[...] = mn
        vo_ref[...] = vn

    sd = jax.ShapeDtypeStruct(w.shape, F32)
    return pl.pallas_call(body, name=name, out_shape=[sd, sd, sd])(w, g, m, v)


def _my_place():
    return lax.axis_index("x"), lax.axis_index("y"), lax.axis_index("c")


def _peer_chips(x, y):
    return [(1 - x, y), (x, 1 - y), (1 - x, 1 - y)]


_HBM = pl.BlockSpec(memory_space=pltpu.HBM)
_SEM = pl.BlockSpec(memory_space=pltpu.SEMAPHORE)
_EFFECT = pltpu.SideEffectType.DATAFLOW_SIDE_EFFECTING


def _split_start(name, bufs, n_copies, make_copies, after):
    nb = len(bufs)

    def body(*refs):
        send_sems, recv_sems = refs[nb + 1], refs[nb + 2]
        token = refs[2 * nb + 3]
        sends, _ = make_copies(refs[:nb], send_sems, recv_sems)
        for cp in sends:
            cp.start()
        token[...] = jnp.zeros_like(token)

    res = pl.pallas_call(
        body, name=name,
        out_shape=(pltpu.SemaphoreType.DMA((n_copies,)), pltpu.SemaphoreType.DMA((n_copies,)),
                   *[pltpu.HBM(b.shape, b.dtype) for b in bufs], jax.ShapeDtypeStruct((8, LANES), F32)),
        in_specs=[_HBM] * nb + [pl.BlockSpec(memory_space=pl.ANY)],
        out_specs=(_SEM, _SEM, *[_HBM] * nb, pl.BlockSpec(memory_space=pltpu.VMEM)),
        input_output_aliases={k: 2 + k for k in range(nb)},
        compiler_params=pltpu.CompilerParams(has_side_effects=_EFFECT),
    )(*[pltpu.with_memory_space_constraint(b, pltpu.HBM) for b in bufs], after)
    return res[0], res[1], list(res[2:2 + nb]), res[2 + nb]


def _split_wait(name, bufs, sems, make_copies, after):
    nb = len(bufs)

    def body(*refs):
        send_sems, recv_sems = refs[nb], refs[nb + 1]
        sends, recvs = make_copies(refs[:nb], send_sems, recv_sems)
        for cp in sends:
            cp.wait_send()
        for cp in recvs:
            cp.wait_recv()

    res = pl.pallas_call(
        body, name=name,
        out_shape=tuple(pltpu.HBM(b.shape, b.dtype) for b in bufs),
        in_specs=[_HBM] * nb + [_SEM, _SEM, pl.BlockSpec(memory_space=pl.ANY)],
        out_specs=tuple([_HBM] * nb),
        input_output_aliases={k: k for k in range(nb)},
        compiler_params=pltpu.CompilerParams(has_side_effects=_EFFECT),
    )(*bufs, sems[0], sems[1], after)
    return list(res)


def _remote(src, dst, send_sems, recv_sems, k, target):
    return pltpu.make_async_remote_copy(src_ref=src, dst_ref=dst, send_sem=send_sems.at[k],
                                        recv_sem=recv_sems.at[k], device_id=target, device_id_type=MESH)


def _ag_region(ref, axis, chip, half):
    K, N = ref.shape
    if axis == 0:
        hs = K // N_CHIPS // 2
        assert hs % 16 == 0
        return ref.at[pl.ds(pl.multiple_of((2 * chip + half) * hs, 16), hs), :]
    ns, hk = N // N_CHIPS, K // 2
    assert ns % LANES == 0 and hk % 16 == 0
    return ref.at[pl.ds(pl.multiple_of(half * hk, 16), hk), pl.ds(pl.multiple_of(chip * ns, LANES), ns)]


def _ag_copies(stage, axes):
    n = len(axes)

    def make(bufs, send_sems, recv_sems):
        x, y, c = _my_place()
        me = 2 * x + y
        sends, recvs = [], []
        for j, (px, py) in enumerate(_peer_chips(x, y)):
            other = 2 * px + py
            for w in range(n):
                k = j * n + w
                if stage == 1:
                    src, target = _ag_region(bufs[w], axes[w], me, c), (px, py, c)
                    land = _ag_region(bufs[w], axes[w], other, c)
                else:
                    src, target = _ag_region(bufs[w], axes[w], other, c), (x, y, 1 - c)
                    land = _ag_region(bufs[w], axes[w], other, 1 - c)
                sends.append(_remote(src, src, send_sems, recv_sems, k, target))
                recvs.append(_remote(land, land, send_sems, recv_sems, k, target))
        return sends, recvs

    return make


def _half_shape(shape, axis):
    K, N = shape
    return (K, N // 2) if axis == 0 else (K // 2, N)


def _core_half(ref, axis, half):
    K, N = ref.shape
    if axis == 0:
        return ref.at[:, pl.ds(pl.multiple_of(half * (N // 2), LANES), N // 2)]
    return ref.at[pl.ds(pl.multiple_of(half * (K // 2), 16), K // 2), :]


def _chip_block(ref, axis, chip):
    K, N = ref.shape
    if axis == 0:
        return ref.at[pl.ds(pl.multiple_of(chip * (K // N_CHIPS), 16), K // N_CHIPS), :]
    return ref.at[:, pl.ds(pl.multiple_of(chip * (N // N_CHIPS), LANES), N // N_CHIPS)]


def _rs_sibling_copies(axes):
    n = len(axes)

    def make(bufs, send_sems, recv_sems):
        x, y, c = _my_place()
        sends = [_remote(_core_half(bufs[w], axes[w], 1 - c), bufs[n + w], send_sems, recv_sems, w, (x, y, 1 - c))
                 for w in range(n)]
        recvs = [_remote(bufs[n + w], bufs[n + w], send_sems, recv_sems, w, (x, y, 1 - c)) for w in range(n)]
        return sends, recvs

    return make


def _rs_chip_copies(axes):
    n = len(axes)

    def make(bufs, send_sems, recv_sems):
        x, y, c = _my_place()
        sends, recvs = [], []
        for j, (px, py) in enumerate(_peer_chips(x, y)):
            for w in range(n):
                k = j * n + w
                sends.append(_remote(_chip_block(bufs[w], axes[w], 2 * px + py), bufs[n + w].at[j],
                                     send_sems, recv_sems, k, (px, py, c)))
                recvs.append(_remote(bufs[n + w].at[j], bufs[n + w].at[j], send_sems, recv_sems, k, (px, py, c)))
        return sends, recvs

    return make


def _rs_fill_copies(axes):
    n = len(axes)

    def make(bufs, send_sems, recv_sems):
        x, y, c = _my_place()
        sends = [_remote(_core_half(bufs[w], axes[w], c), _core_half(bufs[w], axes[w], c),
                         send_sems, recv_sems, w, (x, y, 1 - c)) for w in range(n)]
        recvs = [_remote(_core_half(bufs[w], axes[w], 1 - c), _core_half(bufs[w], axes[w], 1 - c),
                         send_sems, recv_sems, w, (x, y, 1 - c)) for w in range(n)]
        return sends, recvs

    return make


def _chip_sum(g, r, axis, place, *, name):
    hk, hn = r.shape
    bk, bn = (hk // N_CHIPS, hn) if axis == 0 else (hk, hn // N_CHIPS)
    tk = _pick(bk, (512, 352, 256, 128))
    nk = bk // tk

    def body(p_ref, g_ref, r_ref, b_ref, own_ref):
        s = g_ref[...].astype(F32) + r_ref[...].astype(F32)
        b_ref[...] = s.astype(BF16)

        @pl.when(pl.program_id(1) == p_ref[0])
        def _():
            own_ref[...] = s

    if axis == 0:
        g_spec = pl.BlockSpec((tk, bn), lambda i, j, p: (j * nk + i, p[1]))
        r_spec = pl.BlockSpec((tk, bn), lambda i, j, p: (j * nk + i, 0))
    else:
        g_spec = pl.BlockSpec((tk, bn), lambda i, j, p: (p[1] * nk + i, j))
        r_spec = pl.BlockSpec((tk, bn), lambda i, j, p: (i, j))
    grid_spec = pltpu.PrefetchScalarGridSpec(
        num_scalar_prefetch=1, grid=(nk, N_CHIPS), in_specs=[g_spec, r_spec],
        out_specs=[r_spec, pl.BlockSpec((tk, bn), lambda i, j, p: (i, 0))])
    return pl.pallas_call(
        body, name=name,
        out_shape=[jax.ShapeDtypeStruct(r.shape, BF16), jax.ShapeDtypeStruct((bk, bn), F32)],
        grid_spec=grid_spec,
        compiler_params=pltpu.CompilerParams(dimension_semantics=("arbitrary", "arbitrary"),
                                             vmem_limit_bytes=_vmem(2 * tk * bn * 10 + 3 * tk * bn * 4)),
    )(place, g, r)


def _final_sum(own, recv, axis, place, *, name):
    _, bk, bn = recv.shape
    tk = _pick(bk, (256, 176, 128))
    nk = bk // tk

    def body(p_ref, o_ref, r_ref, out_ref):
        out_ref[...] = ((o_ref[...] + r_ref[0].astype(F32)) + r_ref[1].astype(F32)) + r_ref[2].astype(F32)

    own_spec = pl.BlockSpec((tk, bn), lambda i, p: (i, 0))
    if axis == 0:
        out_shape, out_spec = (bk, 2 * bn), pl.BlockSpec((tk, bn), lambda i, p: (i, p[1]))
    else:
        out_shape, out_spec = (2 * bk, bn), pl.BlockSpec((tk, bn), lambda i, p: (p[1] * nk + i, 0))
    grid_spec = pltpu.PrefetchScalarGridSpec(
        num_scalar_prefetch=1, grid=(nk,),
        in_specs=[own_spec, pl.BlockSpec((3, tk, bn), lambda i, p: (0, i, 0))], out_specs=out_spec)
    return pl.pallas_call(
        body, name=name, out_shape=jax.ShapeDtypeStruct(out_shape, F32), grid_spec=grid_spec,
        compiler_params=pltpu.CompilerParams(dimension_semantics=("arbitrary",),
                                             vmem_limit_bytes=_vmem(2 * tk * bn * 14 + 4 * tk * bn * 4)),
    )(place, own, recv)


def _allreduce_small(p):
    def body(p_ref, o_ref, r0, r1, r2, send_sems, recv_sems):
        x, y, c = _my_place()
        o_ref[...] = p_ref[...]
        for s, (peer, rbuf) in enumerate([((x, y, 1 - c), r0), ((1 - x, y, c), r1), ((x, 1 - y, c), r2)]):
            cp = pltpu.make_async_remote_copy(src_ref=o_ref, dst_ref=rbuf, send_sem=send_sems.at[s],
                                              recv_sem=recv_sems.at[s], device_id=peer, device_id_type=MESH)
            cp.start()
            cp.wait()
            o_ref[...] = o_ref[...] + rbuf[...]

    vm = pl.BlockSpec(memory_space=pltpu.VMEM)
    return pl.pallas_call(
        body, name="allreduce_small", out_shape=jax.ShapeDtypeStruct(p.shape, F32),
        in_specs=[vm], out_specs=vm,
        scratch_shapes=[pltpu.VMEM(p.shape, F32)] * 3 + [pltpu.SemaphoreType.DMA((3,))] * 2,
        compiler_params=pltpu.CompilerParams(vmem_limit_bytes=_vmem(6 * _nbytes(p.shape, F32))),
    )(p)


def _pack_rows(parts):
    rows, metas = [], []
    for a in parts:
        flat = a.reshape(-1)
        nrow = -(-flat.shape[0] // LANES)
        nrow = -(-nrow // 8) * 8
        flat = jnp.pad(flat, (0, nrow * LANES - flat.shape[0]))
        rows.append(flat.reshape(nrow, LANES))
        metas.append((a.shape, nrow))
    return jnp.concatenate(rows, axis=0), metas


def _unpack_rows(packed, metas):
    out, r0 = [], 0
    for shape, nrow in metas:
        size = int(np.prod(shape))
        out.append(packed[r0:r0 + nrow].reshape(-1)[:size].reshape(shape))
        r0 += nrow
    return out


def kernel(x, positions, pre_mix_g, post_mix_g, pre_ffn_g, post_ffn_g, a_w_in, a_b_in, a_ln_g, a_ln_b, a_w_s, a_b_s, a_w_out, b_w_qkv, b_b_qkv, b_sinks, b_w_o, ffn_w_gu, ffn_w_down, loss_target, m_pre_mix_g, m_post_mix_g, m_pre_ffn_g, m_post_ffn_g, m_a_w_in, m_a_b_in, m_a_ln_g, m_a_ln_b, m_a_w_s, m_a_b_s, m_a_w_out, m_b_w_qkv, m_b_b_qkv, m_b_sinks, m_b_w_o, m_ffn_w_gu, m_ffn_w_down, v_pre_mix_g, v_post_mix_g, v_pre_ffn_g, v_post_ffn_g, v_a_w_in, v_a_b_in, v_a_ln_g, v_a_ln_b, v_a_w_s, v_a_b_s, v_a_w_out, v_b_w_qkv, v_b_b_qkv, v_b_sinks, v_b_w_o, v_ffn_w_gu, v_ffn_w_down):
    depth, D = pre_mix_g.shape
    xi, yi, ci = _my_place()
    chip = 2 * xi + yi
    place = jnp.stack([chip, ci]).astype(jnp.int32)

    stacked = {"a_w_in": (a_w_in, m_a_w_in, v_a_w_in), "a_w_out": (a_w_out, m_a_w_out, v_a_w_out),
               "b_w_qkv": (b_w_qkv, m_b_w_qkv, v_b_w_qkv), "b_w_o": (b_w_o, m_b_w_o, v_b_w_o),
               "ffn_w_gu": (ffn_w_gu, m_ffn_w_gu, v_ffn_w_gu), "ffn_w_down": (ffn_w_down, m_ffn_w_down, v_ffn_w_down)}
    cut = {"a_w_in": 1, "a_w_out": 0, "b_w_qkv": 1, "b_w_o": 0, "ffn_w_gu": 1, "ffn_w_down": 0}

    def layer_keys(i):
        mix = [("a_w_in", i // 2), ("a_w_out", i // 2)] if i % 2 == 0 else [("b_w_qkv", i // 2), ("b_w_o", i // 2)]
        return mix + [("ffn_w_gu", i), ("ffn_w_down", i)]

    def dep(a, toks):
        for t in toks:
            a = a + t[:1, :1]
        return a

    W = {}
    for i in range(depth):
        for nm, l in layer_keys(i):
            W[(nm, l)] = _cast_block(stacked[nm][0], l, cut[nm], place, name=f"cast_{nm}_{l}")

    def gather(tag, keys, after):
        axes = [cut[nm] for nm, _ in keys]
        for stage in (1, 2):
            ss, rs, bufs, tok = _split_start(f"ag{stage}_start_{tag}", [W[k] for k in keys], 3 * len(keys),
                                             _ag_copies(stage, axes), after)
            after = yield tok
            bufs = _split_wait(f"ag{stage}_wait_{tag}", bufs, (ss, rs), _ag_copies(stage, axes), after)
            W.update(zip(keys, bufs))
        yield None

    nq = b_b_qkv.shape[1]
    bq_full = jnp.zeros((b_b_qkv.shape[0], N_CHIPS * nq), F32)
    bq_full = lax.dynamic_update_slice(bq_full, jnp.where(ci == 0, b_b_qkv, 0.0), (0, chip * nq))
    bq_packed, bq_meta = _pack_rows([bq_full])
    bq_gathered = _allreduce_small(bq_packed)
    b_qkv_full = _unpack_rows(bq_gathered, bq_meta)[0]

    first = gather("0m", layer_keys(0)[:2], bq_gathered)
    tok = next(first)
    tok = first.send(tok)
    first.send(tok)

    h = x[0]
    target = loss_target[0]
    ctab, stab = _rope_tables(positions[0])
    q_width = W[("b_w_o", 0)].shape[0]
    kv_width = N_KV_HEADS * HEAD_DIM
    row = lambda a, i: a[i:i + 1]

    saved = []
    hn = None
    for i in range(depth):
        j = i // 2
        s = {"h": h}
        ffn_w = None
        if i == 0:
            ffn_w = gather("0f", layer_keys(0)[2:], W[("a_w_out", 0)])
            toks = [next(ffn_w)]
            nxt = gather("1", layer_keys(1), toks[0])
            toks.append(next(nxt))
            hn = _rms_fwd(h, dep(row(pre_mix_g, i), toks), out_dtype=BF16, name=f"rms_pre_mix_{i}")
        elif i + 1 < depth:
            nxt = gather(str(i + 1), layer_keys(i + 1), h)
            toks = [next(nxt)]
        else:
            toks = []
        s["hn"] = hn
        if i % 2 == 0:
            pre = _matmul(hn, W[("a_w_in", j)], mode="nn", bias=dep(row(a_b_in, j), toks), out_dtype=F32,
                          name=f"gmlp_in_{i}")
            gated = _sgu_fwd(pre, row(a_ln_g, j), row(a_ln_b, j), a_w_s[j], a_b_s[j].T, name=f"sgu_fwd_{i}")
            mix = _matmul(gated, W[("a_w_out", j)], mode="nn", out_dtype=F32, name=f"gmlp_out_{i}")
            s.update(pre=pre, gated=gated)
        else:
            qkv = _matmul(hn, W[("b_w_qkv", j)], mode="nn", bias=dep(row(b_qkv_full, j), toks), out_dtype=F32,
                          name=f"attn_qkv_{i}")
            qr, kr, vr = _rope_fwd(qkv, ctab, stab, q_width=q_width, kv_width=kv_width, name=f"rope_fwd_{i}")
            o = _attn_fwd(qr, kr, vr, row(b_sinks, j), name=f"attn_fwd_{i}")
            mix = _matmul(o, W[("b_w_o", j)], mode="nn", out_dtype=F32, name=f"attn_o_{i}")
            s.update(qr=qr, kr=kr, vr=vr, o=o)
        s["mix"] = mix
        toks = [ffn_w.send(mix)] if ffn_w else []
        h1, fn = _rms_res_norm(h, mix, dep(row(post_mix_g, i), toks), row(pre_ffn_g, i), name=f"rms_post_mix_{i}")
        if ffn_w:
            ffn_w.send(h1)
        s["h1"] = h1
        g_pre, u_pre, act = _ffn_up(fn, W[("ffn_w_gu", i)][None], 0, name=f"ffn_up_{i}")
        f = _matmul(act, W[("ffn_w_down", i)], mode="nn", out_dtype=F32, name=f"ffn_down_{i}")
        if i + 1 < depth:
            toks = [nxt.send(f)]
            h, hn = _rms_res_norm(h1, f, dep(row(post_ffn_g, i), toks), row(pre_mix_g, i + 1),
                                  name=f"rms_post_ffn_{i}")
            nxt.send(h)
        else:
            h = _rms_res(h1, f, row(post_ffn_g, i), name=f"rms_post_ffn_{i}")
        s.update(fn=fn, g_pre=g_pre, u_pre=u_pre, act=act, f=f)
        saved.append(s)

    dh, loss_part = _loss_and_grad(h, target, name="loss")
    loss = lax.psum(loss_part[0, 0], ("x", "y", "c"))

    big_out = {nm: tuple(lax.empty(w.shape, F32) for _ in range(4)) for nm, (w, _, _) in stacked.items()}

    def reduce_group(i, keys, grads):
        axes = [cut[nm] for nm, _ in keys]
        n = len(keys)
        lands = [lax.empty(_half_shape(g.shape, ax), BF16) for g, ax in zip(grads, axes)]
        ss, rs, bufs, tok = _split_start(f"rs_sibling_start_{i}", list(grads) + lands, n, _rs_sibling_copies(axes),
                                         place)
        after = yield tok
        bufs = _split_wait(f"rs_sibling_wait_{i}", bufs, (ss, rs), _rs_sibling_copies(axes), after)
        sums = [_chip_sum(bufs[w], bufs[n + w], axes[w], place, name=f"chip_sum_{keys[w][0]}_{keys[w][1]}")
                for w in range(n)]
        lands = [lax.empty((3,) + own.shape, BF16) for _, own in sums]
        ss, rs, bufs, tok = _split_start(f"rs_chip_start_{i}", [sb for sb, _ in sums] + lands, 3 * n,
                                         _rs_chip_copies(axes), place)
        after = yield tok
        bufs = _split_wait(f"rs_chip_wait_{i}", bufs, (ss, rs), _rs_chip_copies(axes), after)
        blocks = [_final_sum(sums[w][1], bufs[n + w], axes[w], place, name=f"final_sum_{keys[w][0]}_{keys[w][1]}")
                  for w in range(n)]
        ss, rs, bufs, tok = _split_start(f"rs_fill_start_{i}", blocks, n, _rs_fill_copies(axes), place)
        after = yield tok
        blocks = _split_wait(f"rs_fill_wait_{i}", bufs, (ss, rs), _rs_fill_copies(axes), after)
        for (nm, l), g in zip(keys, blocks):
            w, m, v = stacked[nm]
            big_out[nm] = tuple(_adamw_layer(w, m, v, g, l, big_out[nm], name=f"adamw_{nm}_{l}"))
        yield None

    reducing = []

    def advance(after):
        toks = []
        for gen in list(reducing):
            tok = gen.send(after)
            if tok is None:
                reducing.remove(gen)
            else:
                toks.append(tok)
        return toks

    small = {}
    g_pre_mix, g_post_mix, g_pre_ffn, g_post_ffn = [None] * depth, [None] * depth, [None] * depth, [None] * depth
    df = None
    for i in reversed(range(depth)):
        j = i // 2
        s = saved[i]
        if df is None:
            df, g_post_ffn[i] = _rms_bwd(s["f"], row(post_ffn_g, i), dh, None, out_dtype=BF16,
                                         name=f"rms_post_ffn_bwd_{i}")
        g_down = _matmul(s["act"], df, mode="tn", out_dtype=BF16, name=f"ffn_down_dw_{i}")
        dg_, du_ = _ffn_down_dx(df, W[("ffn_w_down", i)][None], 0, s["g_pre"], s["u_pre"], name=f"ffn_down_dx_{i}")
        hid = dg_.shape[1]
        tile = _pick(hid, (1408, 768, 512, 256, 128))
        w_gu = W[("ffn_w_gu", i)]
        g_gu = lax.empty(w_gu.shape, BF16)
        g_gu = _matmul(s["fn"], dg_, mode="tn", into=g_gu, tq=tile, out_dtype=BF16, name=f"ffn_g_dw_{i}")
        g_gu = _matmul(s["fn"], du_, mode="tn", into=g_gu, tq=tile, q_off=hid // tile, out_dtype=BF16,
                       name=f"ffn_u_dw_{i}")
        dfn_g = _matmul(dg_, w_gu, mode="nt", tr=hid, out_dtype=F32, name=f"ffn_g_dx_{i}")
        dfn = _matmul(du_, w_gu, mode="nt", tr=hid, b_r_off=1, bias=dfn_g, out_dtype=F32, name=f"ffn_u_dx_{i}")
        toks = advance(dfn)
        if i == 0:
            gen = reduce_group("0f", layer_keys(0)[2:], [g_gu, g_down])
            toks.append(next(gen))
            reducing.append(gen)
        dh1, dmix, g_pre_ffn[i], g_post_mix[i] = _rms_bwd_chain(
            s["h1"], dep(row(pre_ffn_g, i), toks), dfn, dh, s["mix"], row(post_mix_g, i), name=f"rms_ffn_mix_bwd_{i}")
        if i % 2 == 0:
            g_out = _matmul(s["gated"], dmix, mode="tn", out_dtype=BF16, name=f"gmlp_out_dw_{i}")
            dgated = _matmul(dmix, W[("a_w_out", j)], mode="nt", out_dtype=F32, name=f"gmlp_out_dx_{i}")
            if i == 0:
                advance(dgated)
            dpre, dws, dbsT, dlng, dlnb, dbin = _sgu_bwd(s["pre"], dgated, row(a_ln_g, j), row(a_ln_b, j),
                                                         a_w_s[j], a_b_s[j].T, name=f"sgu_bwd_{i}")
            small[("a_w_s", j)] = dws
            small[("a_b_s", j)] = dbsT.T
            small[("a_ln_g", j)] = dlng
            small[("a_ln_b", j)] = dlnb
            small[("a_b_in", j)] = dbin
            g_in = _matmul(s["hn"], dpre, mode="tn", out_dtype=BF16, name=f"gmlp_in_dw_{i}")
            dhn = _matmul(dpre, W[("a_w_in", j)], mode="nt", out_dtype=F32, name=f"gmlp_in_dx_{i}")
        else:
            g_out = _matmul(s["o"], dmix, mode="tn", out_dtype=BF16, name=f"attn_o_dw_{i}")
            do = _matmul(dmix, W[("b_w_o", j)], mode="nt", out_dtype=BF16, name=f"attn_o_dx_{i}")
            dq, dkp, dkc, dvp, dvc, dsk = _attn_bwd(s["qr"], s["kr"], s["vr"], row(b_sinks, j), do,
                                                    name=f"attn_bwd_{i}")
            dqkv, dbq = _rope_bwd(dq, dkp, dkc, dvp, dvc, ctab, stab, name=f"rope_bwd_{i}")
            small[("b_sinks", j)] = dsk[:, :b_sinks.shape[1]]
            small[("b_b_qkv", j)] = dbq
            g_in = _matmul(s["hn"], dqkv, mode="tn", out_dtype=BF16, name=f"attn_qkv_dw_{i}")
            dhn = _matmul(dqkv, W[("b_w_qkv", j)], mode="nt", out_dtype=F32, name=f"attn_qkv_dx_{i}")
        toks = advance(dhn)
        if i > 0:
            dh, df, g_pre_mix[i], g_post_ffn[i - 1] = _rms_bwd_chain(
                s["h"], dep(row(pre_mix_g, i), toks), dhn, dh1, saved[i - 1]["f"], row(post_ffn_g, i - 1),
                name=f"rms_mix_ffn_bwd_{i}")
        else:
            dh, g_pre_mix[i] = _rms_bwd(s["h"], dep(row(pre_mix_g, i), toks), dhn, dh1, out_dtype=F32,
                                        name=f"rms_pre_mix_bwd_{i}")
        if i == 0:
            gen = reduce_group("0m", layer_keys(0)[:2], [g_in, g_out])
        else:
            gen = reduce_group(str(i), layer_keys(i), [g_in, g_out, g_gu, g_down])
        toks = [next(gen)] + advance(dh)
        reducing.append(gen)
    grad_x = dh[None]

    toks = advance(dh)
    n_a, n_b = a_b_in.shape[0], b_sinks.shape[0]
    stack = lambda key, n: jnp.concatenate([small[(key, j)] for j in range(n)], axis=0)
    small_parts = [
        jnp.concatenate(g_pre_mix, axis=0), jnp.concatenate(g_post_mix, axis=0),
        jnp.concatenate(g_pre_ffn, axis=0), jnp.concatenate(g_post_ffn, axis=0),
        stack("a_b_in", n_a), stack("a_ln_g", n_a), stack("a_ln_b", n_a),
        jnp.stack([small[("a_w_s", j)] for j in range(n_a)]), jnp.stack([small[("a_b_s", j)] for j in range(n_a)]),
        stack("b_b_qkv", n_b), stack("b_sinks", n_b),
    ]
    packed, metas = _pack_rows(small_parts)
    reduced = _allreduce_small(dep(packed, toks))
    while reducing:
        advance(reduced)
    red = _unpack_rows(reduced, metas)
    (gr_pre_mix, gr_post_mix, gr_pre_ffn, gr_post_ffn, gr_b_in, gr_ln_g, gr_ln_b, gr_w_s, gr_b_s,
     gr_b_qkv_full, gr_sinks) = red
    gr_b_qkv = lax.dynamic_slice(gr_b_qkv_full, (0, chip * nq), (gr_b_qkv_full.shape[0], nq))

    grads = {"pre_mix_g": gr_pre_mix, "post_mix_g": gr_post_mix, "pre_ffn_g": gr_pre_ffn, "post_ffn_g": gr_post_ffn,
             "a_b_in": gr_b_in, "a_ln_g": gr_ln_g, "a_ln_b": gr_ln_b, "a_w_s": gr_w_s, "a_b_s": gr_b_s,
             "b_b_qkv": gr_b_qkv, "b_sinks": gr_sinks}
    weights = {"pre_mix_g": (pre_mix_g, m_pre_mix_g, v_pre_mix_g), "post_mix_g": (post_mix_g, m_post_mix_g, v_post_mix_g),
               "pre_ffn_g": (pre_ffn_g, m_pre_ffn_g, v_pre_ffn_g), "post_ffn_g": (post_ffn_g, m_post_ffn_g, v_post_ffn_g),
               "a_b_in": (a_b_in, m_a_b_in, v_a_b_in), "a_ln_g": (a_ln_g, m_a_ln_g, v_a_ln_g),
               "a_ln_b": (a_ln_b, m_a_ln_b, v_a_ln_b), "a_w_s": (a_w_s, m_a_w_s, v_a_w_s), "a_b_s": (a_b_s, m_a_b_s, v_a_b_s),
               "b_b_qkv": (b_b_qkv, m_b_b_qkv, v_b_b_qkv), "b_sinks": (b_sinks, m_b_sinks, v_b_sinks)}
    order = ["pre_mix_g", "post_mix_g", "pre_ffn_g", "post_ffn_g", "a_w_in", "a_b_in", "a_ln_g", "a_ln_b", "a_w_s",
             "a_b_s", "a_w_out", "b_w_qkv", "b_b_qkv", "b_sinks", "b_w_o", "ffn_w_gu", "ffn_w_down"]
    deltas, new_m, new_v = {}, {}, {}
    for nm in order:
        if nm in big_out:
            grads[nm], deltas[nm], new_m[nm], new_v[nm] = big_out[nm]
        else:
            w, m, v = weights[nm]
            deltas[nm], new_m[nm], new_v[nm] = _adamw_small(w, grads[nm], m, v, name="adamw_" + nm)
    return (loss, grad_x, *[grads[nm] for nm in order], *[deltas[nm] for nm in order],
            *[new_m[nm] for nm in order], *[new_v[nm] for nm in order])
```

```python
import functools
import math

import jax
import jax.numpy as jnp
import numpy as np
from jax import lax
from jax.experimental import pallas as pl
from jax.experimental.pallas import tpu as pltpu

F32 = jnp.float32
BF16 = jnp.bfloat16
MESH = pl.DeviceIdType.MESH

HEAD_DIM = 64
N_KV_HEADS = 4
ROPE_DIM = 16
ROPE_THETA = 500000.0
CHUNK = 128
GMLP_GROUPS = 8
RMS_EPS = 1e-6
LN_EPS = 1e-5
NEG_INF = -1e30
ADAM_LR = 0.001
ADAM_B1 = 0.9
ADAM_B2 = 0.999
ADAM_EPS = 1e-08
ADAM_WD = 0.01
ADAM_STEP = 10

N_CHIPS = 4
LANES = 128
VMEM_CAP = 58 * 1024 * 1024


def _vmem(est_bytes):
    assert est_bytes < VMEM_CAP
    return VMEM_CAP


def _pick(n, cands):
    for c in cands:
        if c <= n and n % c == 0:
            return c
    return n


def _nbytes(shape, dtype):
    return int(np.prod(shape)) * jnp.dtype(dtype).itemsize


MATMUL_VMEM_BUDGET = 48 * 1024 * 1024


def _halvings(n, unit):
    out, t = [], n
    while t % unit == 0 and t >= unit:
        out.append(t)
        if t % 2:
            break
        t //= 2
    return out


def _matmul_tiles(P, Q, R, a_bytes, b_bytes, o_bytes, full_addend, tp, tq, tr):
    step_us, bytes_per_us = 0.85, 3.2e6
    best = None
    for p in ([tp] if tp else _halvings(P, LANES)):
        for q in ([tq] if tq else _halvings(Q, LANES)):
            for r in ([tr] if tr else _halvings(R, LANES)):
                nk = R // r
                vm = 2 * (p * r * a_bytes + r * q * b_bytes + p * q * o_bytes + (p * q * 4 if full_addend else 0))
                vm += p * q * 4 * (2 if nk > 1 else 1)
                if vm > MATMUL_VMEM_BUDGET:
                    continue
                exposed = (p * r * a_bytes + r * q * b_bytes + p * q * o_bytes) / bytes_per_us
                key = ((P // p) * (Q // q) * nk * step_us + exposed, nk, abs(p - q))
                if best is None or key < best[0]:
                    best = (key, (p, q, r))
    assert best is not None, (P, Q, R)
    return best[1]


def _matmul(a, b, *, mode, out_dtype, name, a_l=None, b_l=None, bias=None, into=None, o_l=None,
            q_off=0, b_r_off=0, tp=None, tq=None, tr=None):
    a2 = a.shape[-2:]
    b2 = b.shape[-2:]
    if mode == "nn":
        (P, R), (R2, Q) = a2, b2
    elif mode == "nt":
        (P, R), (Q, R2) = a2, b2
    else:
        (R, P), (R2, Q) = a2, b2
    assert R == R2 or (mode == "nt" and R2 % R == 0), (mode, a.shape, b.shape)
    o_bytes = jnp.dtype(into.dtype if into is not None else out_dtype).itemsize
    full_addend = bias is not None and bias.shape[0] != 1
    tp, tq, tr = _matmul_tiles(P, Q, R, a.dtype.itemsize, b.dtype.itemsize, o_bytes, full_addend, tp, tq, tr)
    assert P % tp == 0 and Q % tq == 0 and R % tr == 0
    nk = R // tr
    dims = {"nn": (((1,), (0,)), ((), ())), "nt": (((1,), (1,)), ((), ())), "tn": (((0,), (0,)), ((), ()))}[mode]

    def lead(l, blk, idx):
        if l is None:
            return pl.BlockSpec(blk, idx)
        return pl.BlockSpec((None,) + blk, lambda i, j, k: (l,) + idx(i, j, k))

    if mode == "nn":
        a_spec = lead(a_l, (tp, tr), lambda i, j, k: (i, k))
        b_spec = lead(b_l, (tr, tq), lambda i, j, k: (k, j))
    elif mode == "nt":
        a_spec = lead(a_l, (tp, tr), lambda i, j, k: (i, k))
        b_spec = lead(b_l, (tq, tr), lambda i, j, k: (j, k + b_r_off))
    else:
        a_spec = lead(a_l, (tr, tp), lambda i, j, k: (k, i))
        b_spec = lead(b_l, (tr, tq), lambda i, j, k: (k, j))
    in_specs = [a_spec, b_spec]
    args = [a, b]
    if bias is not None:
        if bias.shape[0] == 1:
            in_specs.append(pl.BlockSpec((1, tq), lambda i, j, k: (0, j)))
        else:
            in_specs.append(pl.BlockSpec((tp, tq), lambda i, j, k: (i, j)))
        args.append(bias)
    aliases = {}
    if into is not None:
        in_specs.append(pl.BlockSpec(memory_space=pl.ANY))
        args.append(into)
        aliases = {len(args) - 1: 0}
        out_shape = jax.ShapeDtypeStruct(into.shape, into.dtype)
        out_dtype = into.dtype
        if o_l is None:
            out_spec = pl.BlockSpec((tp, tq), lambda i, j, k: (i, j + q_off))
        else:
            out_spec = pl.BlockSpec((None, tp, tq), lambda i, j, k: (o_l, i, j + q_off))
    else:
        out_shape = jax.ShapeDtypeStruct((P, Q), out_dtype)
        out_spec = pl.BlockSpec((tp, tq), lambda i, j, k: (i, j))
    has_bias = bias is not None
    has_into = into is not None

    def body(*refs):
        a_ref, b_ref = refs[0], refs[1]
        pos = 2
        bias_ref = None
        if has_bias:
            bias_ref = refs[pos]
            pos += 1
        if has_into:
            pos += 1
        o_ref = refs[pos]
        acc_ref = refs[pos + 1] if nk > 1 else None
        part = lax.dot_general(a_ref[...], b_ref[...], dims, preferred_element_type=F32)

        def finish(acc):
            if has_bias:
                acc = acc + bias_ref[...]
            o_ref[...] = acc.astype(out_dtype)

        if nk == 1:
            finish(part)
        else:
            k = pl.program_id(2)

            @pl.when(k == 0)
            def _():
                acc_ref[...] = part

            @pl.when(k > 0)
            def _():
                acc_ref[...] += part

            @pl.when(k == nk - 1)
            def _():
                finish(acc_ref[...])

    est = 2 * (_nbytes((tp, tr), a.dtype) + _nbytes((tr, tq), b.dtype) + _nbytes((tp, tq), out_dtype)) + 3 * tp * tq * 4
    return pl.pallas_call(
        body, name=name, out_shape=out_shape,
        grid=(P // tp, Q // tq, nk),
        in_specs=in_specs, out_specs=out_spec,
        scratch_shapes=[pltpu.VMEM((tp, tq), F32)] if nk > 1 else [],
        input_output_aliases=aliases,
        compiler_params=pltpu.CompilerParams(
            dimension_semantics=("parallel", "parallel", "arbitrary"), vmem_limit_bytes=_vmem(est)),
    )(*args)


def _row_call(body, ins, outs, *, name, rows, tr, acc_outs=(), est=0):
    in_specs = []
    for arr, kind in ins:
        if kind == "row":
            in_specs.append(pl.BlockSpec((tr, arr.shape[1]), lambda i: (i, 0)))
        else:
            nd = arr.ndim
            in_specs.append(pl.BlockSpec(arr.shape, lambda i, nd=nd: (0,) * nd))
    out_shapes = [jax.ShapeDtypeStruct(s, d) for s, d in outs] + [jax.ShapeDtypeStruct(s, d) for s, d in acc_outs]
    out_specs = [pl.BlockSpec((tr, s[1]), lambda i: (i, 0)) for s, _ in outs]
    out_specs += [pl.BlockSpec(s, lambda i, nd=len(s): (0,) * nd) for s, _ in acc_outs]
    res = pl.pallas_call(
        body, name=name, out_shape=out_shapes, grid=(rows // tr,), in_specs=in_specs, out_specs=out_specs,
        compiler_params=pltpu.CompilerParams(dimension_semantics=("arbitrary",), vmem_limit_bytes=_vmem(est)),
    )(*[a for a, _ in ins])
    return res


def _rms_fwd(x, g, *, out_dtype, name):
    T, D = x.shape
    tr = _pick(T, (512, 256, 128))

    def body(x_ref, g_ref, o_ref):
        xv = x_ref[...]
        r = lax.rsqrt(jnp.mean(xv * xv, axis=-1, keepdims=True) + RMS_EPS)
        o_ref[...] = (xv * r * g_ref[...]).astype(out_dtype)

    return _row_call(body, [(x, "row"), (g, "full")], [((T, D), out_dtype)], name=name, rows=T, tr=tr,
                     est=8 * tr * D * 4)[0]


def _rms_res(h, y, g, *, name):
    T, D = h.shape
    tr = _pick(T, (512, 256, 128))

    def body(h_ref, y_ref, g_ref, o_ref):
        yv = y_ref[...]
        r = lax.rsqrt(jnp.mean(yv * yv, axis=-1, keepdims=True) + RMS_EPS)
        o_ref[...] = h_ref[...] + yv * r * g_ref[...]

    return _row_call(body, [(h, "row"), (y, "row"), (g, "full")], [((T, D), F32)], name=name, rows=T, tr=tr,
                     est=10 * tr * D * 4)[0]


def _rms_bwd(x, g, dy, dres, *, out_dtype, name):
    T, D = x.shape
    tr = _pick(T, (512, 256, 128))
    has_res = dres is not None

    def body(*refs):
        if has_res:
            x_ref, g_ref, dy_ref, dr_ref, dx_ref, dg_ref = refs
        else:
            x_ref, g_ref, dy_ref, dx_ref, dg_ref = refs
        xv = x_ref[...]
        r = lax.rsqrt(jnp.mean(xv * xv, axis=-1, keepdims=True) + RMS_EPS)
        xhat = xv * r
        dyv = dy_ref[...].astype(F32)
        dxn = dyv * g_ref[...]
        dx = r * (dxn - xhat * jnp.mean(dxn * xhat, axis=-1, keepdims=True))
        if has_res:
            dx = dx + dr_ref[...]
        dx_ref[...] = dx.astype(out_dtype)
        part = jnp.sum(dyv * xhat, axis=0, keepdims=True)

        @pl.when(pl.program_id(0) == 0)
        def _():
            dg_ref[...] = part

        @pl.when(pl.program_id(0) > 0)
        def _():
            dg_ref[...] += part

    ins = [(x, "row"), (g, "full"), (dy, "row")] + ([(dres, "row")] if has_res else [])
    dx, dg = _row_call(body, ins, [((T, D), out_dtype)], name=name, rows=T, tr=tr, acc_outs=[((1, D), F32)],
                       est=12 * tr * D * 4)
    return dx, dg


def _rms_res_norm(h, y, g_res, g_next, *, name):
    T, D = h.shape
    tr = _pick(T, (512, 256, 128))

    def body(h_ref, y_ref, g_ref, gn_ref, o_ref, n_ref):
        yv = y_ref[...]
        r = lax.rsqrt(jnp.mean(yv * yv, axis=-1, keepdims=True) + RMS_EPS)
        h2 = h_ref[...] + yv * r * g_ref[...]
        o_ref[...] = h2
        r2 = lax.rsqrt(jnp.mean(h2 * h2, axis=-1, keepdims=True) + RMS_EPS)
        n_ref[...] = (h2 * r2 * gn_ref[...]).astype(BF16)

    return _row_call(body, [(h, "row"), (y, "row"), (g_res, "full"), (g_next, "full")],
                     [((T, D), F32), ((T, D), BF16)], name=name, rows=T, tr=tr, est=12 * tr * D * 4)


def _rms_bwd_chain(x1, g1, dy1, dres, x2, g2, *, name):
    T, D = x1.shape
    tr = _pick(T, (512, 256, 128))

    def one(xv, gv, dyv):
        r = lax.rsqrt(jnp.mean(xv * xv, axis=-1, keepdims=True) + RMS_EPS)
        xhat = xv * r
        dxn = dyv * gv
        dx = r * (dxn - xhat * jnp.mean(dxn * xhat, axis=-1, keepdims=True))
        return dx, jnp.sum(dyv * xhat, axis=0, keepdims=True)

    def body(x1_ref, g1_ref, dy1_ref, dr_ref, x2_ref, g2_ref, d1_ref, d2_ref, dg1_ref, dg2_ref):
        dx1, p1 = one(x1_ref[...], g1_ref[...], dy1_ref[...].astype(F32))
        d1 = dx1 + dr_ref[...]
        d1_ref[...] = d1
        dx2, p2 = one(x2_ref[...], g2_ref[...], d1)
        d2_ref[...] = dx2.astype(BF16)

        @pl.when(pl.program_id(0) == 0)
        def _():
            dg1_ref[...] = p1
            dg2_ref[...] = p2

        @pl.when(pl.program_id(0) > 0)
        def _():
            dg1_ref[...] += p1
            dg2_ref[...] += p2

    ins = [(x1, "row"), (g1, "full"), (dy1, "row"), (dres, "row"), (x2, "row"), (g2, "full")]
    return _row_call(body, ins, [((T, D), F32), ((T, D), BF16)], name=name, rows=T, tr=tr,
                     acc_outs=[((1, D), F32), ((1, D), F32)], est=20 * tr * D * 4)


def _ffn_up(fn, w_gu, l, *, name):
    T, D = fn.shape
    H = w_gu.shape[2] // 2
    tp = _pick(T, (1024, 512, 256, 128))
    tq = _pick(H, (1408, 768, 512, 256, 128))
    nj = H // tq

    def body(a_ref, wg_ref, wu_ref, g_ref, u_ref, act_ref):
        a = a_ref[...]
        g = jnp.dot(a, wg_ref[...], preferred_element_type=F32)
        u = jnp.dot(a, wu_ref[...], preferred_element_type=F32)
        sg = jax.nn.sigmoid(g)
        silu = g * sg
        g_ref[...] = (u * (sg + silu * (1.0 - sg))).astype(BF16)
        u_ref[...] = silu.astype(BF16)
        act_ref[...] = (silu * u).astype(BF16)

    tile = pl.BlockSpec((tp, tq), lambda j, i: (i, j))
    est = 2 * (tp * D * 2 + 2 * D * tq * 2 + 3 * tp * tq * 2) + 4 * tp * tq * 4
    return pl.pallas_call(
        body, name=name,
        out_shape=[jax.ShapeDtypeStruct((T, H), BF16), jax.ShapeDtypeStruct((T, H), BF16),
                   jax.ShapeDtypeStruct((T, H), BF16)],
        grid=(nj, T // tp),
        in_specs=[pl.BlockSpec((tp, D), lambda j, i: (i, 0)),
                  pl.BlockSpec((None, D, tq), lambda j, i: (l, 0, j)),
                  pl.BlockSpec((None, D, tq), lambda j, i: (l, 0, j + nj))],
        out_specs=[tile, tile, tile],
        compiler_params=pltpu.CompilerParams(dimension_semantics=("parallel", "parallel"),
                                             vmem_limit_bytes=_vmem(est)),
    )(fn, w_gu, w_gu)


def _ffn_down_dx(df, w_down, l, g, u, *, name):
    T, D = df.shape
    H = w_down.shape[1]
    tp = _pick(T, (1024, 512, 256, 128))
    tq = _pick(H, (1408, 768, 512, 256, 128))

    def body(a_ref, w_ref, g_ref, u_ref, dg_ref, du_ref):
        da = lax.dot_general(a_ref[...], w_ref[...], (((1,), (1,)), ((), ())), preferred_element_type=F32)
        dg_ref[...] = (da * g_ref[...].astype(F32)).astype(BF16)
        du_ref[...] = (da * u_ref[...].astype(F32)).astype(BF16)

    tile = pl.BlockSpec((tp, tq), lambda j, i: (i, j))
    est = 2 * (tp * D * 2 + tq * D * 2 + 4 * tp * tq * 2) + 3 * tp * tq * 4
    return pl.pallas_call(
        body, name=name,
        out_shape=[jax.ShapeDtypeStruct((T, H), BF16), jax.ShapeDtypeStruct((T, H), BF16)],
        grid=(H // tq, T // tp),
        in_specs=[pl.BlockSpec((tp, D), lambda j, i: (i, 0)),
                  pl.BlockSpec((None, tq, D), lambda j, i: (l, j, 0)), tile, tile],
        out_specs=[tile, tile],
        compiler_params=pltpu.CompilerParams(dimension_semantics=("parallel", "parallel"),
                                             vmem_limit_bytes=_vmem(est)),
    )(df, w_down, g, u)


def _loss_and_grad(y, target, x, g, *, name):
    T, D = y.shape
    tr = _pick(T, (512, 256, 128))

    def body(y_ref, t_ref, x_ref, g_ref, dy_ref, dx_ref, l_ref, dg_ref):
        e = y_ref[...] - t_ref[...]
        dy = e * (1.0 / D)
        dy_ref[...] = dy
        part = jnp.sum(jnp.sum(e * e, axis=1, keepdims=True), axis=0, keepdims=True) * (0.5 / D)
        xv = x_ref[...]
        r = lax.rsqrt(jnp.mean(xv * xv, axis=-1, keepdims=True) + RMS_EPS)
        xhat = xv * r
        dxn = dy * g_ref[...]
        dx_ref[...] = (r * (dxn - xhat * jnp.mean(dxn * xhat, axis=-1, keepdims=True))).astype(BF16)
        dg = jnp.sum(dy * xhat, axis=0, keepdims=True)

        @pl.when(pl.program_id(0) == 0)
        def _():
            l_ref[...] = part
            dg_ref[...] = dg

        @pl.when(pl.program_id(0) > 0)
        def _():
            l_ref[...] += part
            dg_ref[...] += dg

    dy, dx, l, dg = _row_call(body, [(y, "row"), (target, "row"), (x, "row"), (g, "full")],
                              [((T, D), F32), ((T, D), BF16)], name=name, rows=T, tr=tr,
                              acc_outs=[((1, 1), F32), ((1, D), F32)], est=14 * tr * D * 4)
    return dy, dx, l, dg


_SQRT_HALF = 0.7071067811865476
_INV_SQRT_2PI = 0.3989422804014327


def _gelu_parts(x):
    cdf = 0.5 * (1.0 + lax.erf(x * _SQRT_HALF))
    return cdf


def _sgu_common(pre, lng, lnb, W):
    cdf = _gelu_parts(pre)
    z = pre * cdf
    u = z[:, :W]
    v = z[:, W:]
    mu = jnp.mean(v, axis=-1, keepdims=True)
    vc = v - mu
    var = jnp.mean(vc * vc, axis=-1, keepdims=True)
    rstd = lax.rsqrt(var + LN_EPS)
    vhat = vc * rstd
    vn = vhat * lng + lnb
    return cdf, u, vhat, rstd, vn


def _causal_mask():
    t = lax.broadcasted_iota(jnp.int32, (CHUNK, CHUNK), 0)
    s = lax.broadcasted_iota(jnp.int32, (CHUNK, CHUNK), 1)
    return t >= s


def _sgu_fwd(pre, lng, lnb, ws, bsT, *, name):
    T, W2 = pre.shape
    W = W2 // 2
    G = ws.shape[0]
    gd = W // G

    def body(pre_ref, lng_ref, lnb_ref, ws_ref, bs_ref, o_ref):
        _, u, _, _, vn = _sgu_common(pre_ref[...], lng_ref[...], lnb_ref[...], W)
        vnb = vn.astype(BF16)
        causal = _causal_mask()
        for g in range(G):
            w = jnp.where(causal, ws_ref[g], 0.0).astype(BF16)
            sv = jnp.dot(w, vnb[:, g * gd:(g + 1) * gd], preferred_element_type=F32) + bs_ref[:, g:g + 1]
            o_ref[:, g * gd:(g + 1) * gd] = (u[:, g * gd:(g + 1) * gd] * sv).astype(BF16)

    return pl.pallas_call(
        body, name=name, out_shape=jax.ShapeDtypeStruct((T, W), BF16), grid=(T // CHUNK,),
        in_specs=[pl.BlockSpec((CHUNK, W2), lambda i: (i, 0)),
                  pl.BlockSpec((1, W), lambda i: (0, 0)), pl.BlockSpec((1, W), lambda i: (0, 0)),
                  pl.BlockSpec(ws.shape, lambda i: (0, 0, 0)), pl.BlockSpec(bsT.shape, lambda i: (0, 0))],
        out_specs=pl.BlockSpec((CHUNK, W), lambda i: (i, 0)),
        compiler_params=pltpu.CompilerParams(dimension_semantics=("arbitrary",),
                                             vmem_limit_bytes=_vmem(12 * CHUNK * W2 * 4)),
    )(pre, lng, lnb, ws, bsT)


def _sgu_bwd(pre, dgated, lng, lnb, ws, bsT, *, name):
    T, W2 = pre.shape
    W = W2 // 2
    G = ws.shape[0]
    gd = W // G

    def body(pre_ref, dgt_ref, lng_ref, lnb_ref, ws_ref, bs_ref,
             dpre_ref, dws_ref, dbs_ref, dlng_ref, dlnb_ref, dbin_ref):
        first = pl.program_id(0) == 0

        @pl.when(first)
        def _():
            dws_ref[...] = jnp.zeros_like(dws_ref)
            dbs_ref[...] = jnp.zeros_like(dbs_ref)
            dlng_ref[...] = jnp.zeros_like(dlng_ref)
            dlnb_ref[...] = jnp.zeros_like(dlnb_ref)
            dbin_ref[...] = jnp.zeros_like(dbin_ref)

        pre_v = pre_ref[...]
        lng_v = lng_ref[...]
        cdf, u, vhat, rstd, vn = _sgu_common(pre_v, lng_v, lnb_ref[...], W)
        vnb = vn.astype(BF16)
        dgt = dgt_ref[...].astype(F32)
        causal = _causal_mask()
        du_parts, dvn_parts = [], []
        for g in range(G):
            sl = slice(g * gd, (g + 1) * gd)
            w = jnp.where(causal, ws_ref[g], 0.0).astype(BF16)
            sv = jnp.dot(w, vnb[:, sl], preferred_element_type=F32) + bs_ref[:, g:g + 1]
            dgt_g = dgt[:, sl]
            du_parts.append(dgt_g * sv)
            dsv = dgt_g * u[:, sl]
            dsvb = dsv.astype(BF16)
            dvn_parts.append(lax.dot_general(w, dsvb, (((0,), (0,)), ((), ())), preferred_element_type=F32))
            dw = lax.dot_general(dsvb, vnb[:, sl], (((1,), (1,)), ((), ())), preferred_element_type=F32)
            dws_ref[g] += jnp.where(causal, dw, 0.0)
            dbs_ref[:, g:g + 1] += jnp.sum(dsv, axis=1, keepdims=True)
        du = jnp.concatenate(du_parts, axis=1)
        dvn = jnp.concatenate(dvn_parts, axis=1)
        dlng_ref[...] += jnp.sum(dvn * vhat, axis=0, keepdims=True)
        dlnb_ref[...] += jnp.sum(dvn, axis=0, keepdims=True)
        dvh = dvn * lng_v
        dv = rstd * (dvh - jnp.mean(dvh, axis=-1, keepdims=True)
                     - vhat * jnp.mean(dvh * vhat, axis=-1, keepdims=True))
        dz = jnp.concatenate([du, dv], axis=1)
        dgelu = cdf + pre_v * jnp.exp(-0.5 * pre_v * pre_v) * _INV_SQRT_2PI
        dpre = dz * dgelu
        dbin_ref[...] += jnp.sum(dpre, axis=0, keepdims=True)
        dpre_ref[...] = dpre.astype(BF16)

    full = lambda shape: pl.BlockSpec(shape, lambda i, nd=len(shape): (0,) * nd)
    return pl.pallas_call(
        body, name=name,
        out_shape=[jax.ShapeDtypeStruct((T, W2), BF16), jax.ShapeDtypeStruct(ws.shape, F32),
                   jax.ShapeDtypeStruct(bsT.shape, F32), jax.ShapeDtypeStruct((1, W), F32),
                   jax.ShapeDtypeStruct((1, W), F32), jax.ShapeDtypeStruct((1, W2), F32)],
        grid=(T // CHUNK,),
        in_specs=[pl.BlockSpec((CHUNK, W2), lambda i: (i, 0)), pl.BlockSpec((CHUNK, W), lambda i: (i, 0)),
                  full((1, W)), full((1, W)), full(ws.shape), full(bsT.shape)],
        out_specs=[pl.BlockSpec((CHUNK, W2), lambda i: (i, 0)), full(ws.shape), full(bsT.shape),
                   full((1, W)), full((1, W)), full((1, W2))],
        compiler_params=pltpu.CompilerParams(dimension_semantics=("arbitrary",),
                                             vmem_limit_bytes=_vmem(24 * CHUNK * W2 * 4)),
    )(pre, dgated, lng, lnb, ws, bsT)


def _rope_tables(positions):
    half = ROPE_DIM // 2
    inv_freq = ROPE_THETA ** (-jnp.arange(0, ROPE_DIM, 2, dtype=F32) / ROPE_DIM)
    ang = positions.astype(F32).reshape(-1, 1) * inv_freq
    cos, sin = jnp.cos(ang), jnp.sin(ang)
    T = ang.shape[0]
    rest = HEAD_DIM - ROPE_DIM
    c64 = jnp.concatenate([cos, cos, jnp.ones((T, rest), F32)], axis=1)
    s64 = jnp.concatenate([-sin, sin, jnp.zeros((T, rest), F32)], axis=1)
    del half
    return jnp.tile(c64, (1, LANES // HEAD_DIM)), jnp.tile(s64, (1, LANES // HEAD_DIM))


def _swap8(x):
    W = x.shape[1]
    half = ROPE_DIM // 2
    lane = lax.broadcasted_iota(jnp.int32, x.shape, 1) % HEAD_DIM
    return jnp.where(lane < half, pltpu.roll(x, W - half, axis=1),
                     jnp.where(lane < ROPE_DIM, pltpu.roll(x, half, axis=1), 0.0))


def _wide(tab, W):
    return jnp.concatenate([tab] * (W // LANES), axis=1) if W > LANES else tab


def _rope_fwd(qkv, ctab, stab, *, q_width, kv_width, name):
    T = qkv.shape[0]
    tr = _pick(T, (256, 128))
    scale = HEAD_DIM ** -0.5

    def body(x_ref, c_ref, s_ref, q_ref, k_ref, v_ref):
        c = c_ref[...]
        s = s_ref[...]
        q = x_ref[:, :q_width]
        k = x_ref[:, q_width:q_width + kv_width]
        q_ref[...] = ((q * _wide(c, q_width) + _swap8(q) * _wide(s, q_width)) * scale).astype(BF16)
        k_ref[...] = (k * _wide(c, kv_width) + _swap8(k) * _wide(s, kv_width)).astype(BF16)
        v_ref[...] = x_ref[:, q_width + kv_width:].astype(BF16)

    return _row_call(body, [(qkv, "row"), (ctab, "row"), (stab, "row")],
                     [((T, q_width), BF16), ((T, kv_width), BF16), ((T, kv_width), BF16)],
                     name=name, rows=T, tr=tr, est=10 * tr * qkv.shape[1] * 4)


_NT = (((1,), (1,)), ((), ()))
_TN = (((0,), (0,)), ((), ()))


def _group_rows(ref, heads):
    return jnp.concatenate([ref[:, h * HEAD_DIM:(h + 1) * HEAD_DIM] for h in heads], axis=0)


def _attn_valid(grp):
    qi = np.arange(grp * CHUNK)[:, None] % CHUNK
    sj = np.arange(2 * CHUNK)[None, :]
    cur = (sj >= CHUNK) & (sj - CHUNK <= qi)
    prev = (sj < CHUNK) & (sj > qi)
    return jnp.asarray(np.stack([cur, cur | prev]).astype(np.float32))


def _valid_spec(grp):
    return pl.BlockSpec((None, grp * CHUNK, 2 * CHUNK), lambda n: (jnp.minimum(n, 1), 0, 0))


def _attn_group_probs(q, kk, sinks, valid, grp):
    rows = grp * CHUNK
    s = lax.dot_general(q, kk, _NT, preferred_element_type=F32)
    s = jnp.where(valid, s, NEG_INF)
    r = lax.broadcasted_iota(jnp.int32, (rows, 1), 0)
    sink = jnp.full((rows, 1), sinks[grp - 1], F32)
    for g in range(grp - 2, -1, -1):
        sink = jnp.where(r < (g + 1) * CHUNK, sinks[g], sink)
    m = jnp.maximum(jnp.max(s, axis=1, keepdims=True), sink)
    p = jnp.exp(s - m)
    ps = jnp.exp(sink - m)
    inv = 1.0 / (jnp.sum(p, axis=1, keepdims=True) + ps)
    return p * inv, ps * inv


def _kv_specs(width, nb):
    prev = pl.BlockSpec((CHUNK, width), lambda n: (jnp.maximum(n - 1, 0), 0))
    cur = pl.BlockSpec((CHUNK, width), lambda n: (n, 0))
    return prev, cur


def _attn_fwd(qr, kr, vr, sinks, *, name):
    T, QW = qr.shape
    KW = kr.shape[1]
    HQ, HK = QW // HEAD_DIM, KW // HEAD_DIM
    grp = HQ // HK
    nb = T // CHUNK

    def body(q_ref, kp_ref, kc_ref, vp_ref, vc_ref, s_ref, ok_ref, o_ref):
        valid = ok_ref[...] > 0.5
        for kh in range(HK):
            ks = slice(kh * HEAD_DIM, (kh + 1) * HEAD_DIM)
            heads = list(range(kh * grp, (kh + 1) * grp))
            q = _group_rows(q_ref, heads)
            kk = jnp.concatenate([kp_ref[:, ks], kc_ref[:, ks]], axis=0)
            vv = jnp.concatenate([vp_ref[:, ks], vc_ref[:, ks]], axis=0)
            p, _ = _attn_group_probs(q, kk, [s_ref[0, h] for h in heads], valid, grp)
            o = jnp.dot(p.astype(BF16), vv, preferred_element_type=F32).astype(BF16)
            for g, h in enumerate(heads):
                o_ref[:, h * HEAD_DIM:(h + 1) * HEAD_DIM] = o[g * CHUNK:(g + 1) * CHUNK]

    kp, kc = _kv_specs(KW, nb)
    return pl.pallas_call(
        body, name=name, out_shape=jax.ShapeDtypeStruct((T, QW), BF16), grid=(nb,),
        in_specs=[pl.BlockSpec((CHUNK, QW), lambda n: (n, 0)), kp, kc, kp, kc,
                  pl.BlockSpec(memory_space=pltpu.SMEM), _valid_spec(grp)],
        out_specs=pl.BlockSpec((CHUNK, QW), lambda n: (n, 0)),
        compiler_params=pltpu.CompilerParams(dimension_semantics=("arbitrary",), vmem_limit_bytes=_vmem(8 << 20)),
    )(qr, kr, kr, vr, vr, sinks, _attn_valid(grp))


def _attn_bwd(qr, kr, vr, sinks, do, *, name):
    T, QW = qr.shape
    KW = kr.shape[1]
    HQ, HK = QW // HEAD_DIM, KW // HEAD_DIM
    grp = HQ // HK
    nb = T // CHUNK

    def body(q_ref, kp_ref, kc_ref, vp_ref, vc_ref, s_ref, do_ref, ok_ref,
             dq_ref, dkp_ref, dkc_ref, dvp_ref, dvc_ref, ds_ref):
        n = pl.program_id(0)
        valid = ok_ref[...] > 0.5
        lane = lax.broadcasted_iota(jnp.int32, (1, LANES), 1)
        dsink = jnp.zeros((1, LANES), F32)
        for kh in range(HK):
            ks = slice(kh * HEAD_DIM, (kh + 1) * HEAD_DIM)
            heads = list(range(kh * grp, (kh + 1) * grp))
            q = _group_rows(q_ref, heads)
            doh = _group_rows(do_ref, heads)
            kk = jnp.concatenate([kp_ref[:, ks], kc_ref[:, ks]], axis=0)
            vv = jnp.concatenate([vp_ref[:, ks], vc_ref[:, ks]], axis=0)
            p, ps = _attn_group_probs(q, kk, [s_ref[0, h] for h in heads], valid, grp)
            dp = lax.dot_general(doh, vv, _NT, preferred_element_type=F32)
            delta = jnp.sum(p * dp, axis=1, keepdims=True)
            ds = (p * (dp - delta)).astype(BF16)
            dv = lax.dot_general(p.astype(BF16), doh, _TN, preferred_element_type=F32)
            dk = lax.dot_general(ds, q, _TN, preferred_element_type=F32)
            dq = jnp.dot(ds, kk, preferred_element_type=F32)
            psd = ps * delta
            for g, h in enumerate(heads):
                dq_ref[:, h * HEAD_DIM:(h + 1) * HEAD_DIM] = dq[g * CHUNK:(g + 1) * CHUNK]
                dsink = dsink + jnp.where(
                    lane == h, -jnp.sum(psd[g * CHUNK:(g + 1) * CHUNK], axis=0, keepdims=True), 0.0)
            dkp_ref[:, ks] = dk[:CHUNK]
            dkc_ref[:, ks] = dk[CHUNK:]
            dvp_ref[:, ks] = dv[:CHUNK]
            dvc_ref[:, ks] = dv[CHUNK:]

        @pl.when(n == 0)
        def _():
            ds_ref[...] = dsink

        @pl.when(n > 0)
        def _():
            ds_ref[...] += dsink

    kp, kc = _kv_specs(KW, nb)
    qspec = pl.BlockSpec((CHUNK, QW), lambda n: (n, 0))
    kout = pl.BlockSpec((CHUNK, KW), lambda n: (n, 0))
    return pl.pallas_call(
        body, name=name,
        out_shape=[jax.ShapeDtypeStruct((T, QW), F32)] + [jax.ShapeDtypeStruct((T, KW), F32)] * 4
        + [jax.ShapeDtypeStruct((1, LANES), F32)],
        grid=(nb,),
        in_specs=[qspec, kp, kc, kp, kc, pl.BlockSpec(memory_space=pltpu.SMEM), qspec, _valid_spec(grp)],
        out_specs=[qspec, kout, kout, kout, kout, pl.BlockSpec((1, LANES), lambda n: (0, 0))],
        compiler_params=pltpu.CompilerParams(dimension_semantics=("arbitrary",), vmem_limit_bytes=_vmem(12 << 20)),
    )(qr, kr, kr, vr, vr, sinks, do, _attn_valid(grp))


def _rope_bwd(dq, dkp, dkc, dvp, dvc, ctab, stab, *, name):
    T, QW = dq.shape
    KW = dkp.shape[1]
    nb = T // CHUNK
    scale = HEAD_DIM ** -0.5
    width = QW + 2 * KW

    def body(dq_ref, dkc_ref, dkn_ref, dvc_ref, dvn_ref, c_ref, s_ref, o_ref, db_ref):
        n = pl.program_id(0)
        c = c_ref[...]
        s = s_ref[...]
        has_next = (n < nb - 1).astype(F32)
        dqv = dq_ref[...]
        dk = dkc_ref[...] + has_next * dkn_ref[...]
        dv = dvc_ref[...] + has_next * dvn_ref[...]
        dq_pre = (dqv * _wide(c, QW) + _swap8(dqv * _wide(s, QW))) * scale
        dk_pre = dk * _wide(c, KW) + _swap8(dk * _wide(s, KW))
        o_ref[:, :QW] = dq_pre.astype(BF16)
        o_ref[:, QW:QW + KW] = dk_pre.astype(BF16)
        o_ref[:, QW + KW:] = dv.astype(BF16)
        part = jnp.concatenate([jnp.sum(dq_pre, axis=0, keepdims=True), jnp.sum(dk_pre, axis=0, keepdims=True),
                                jnp.sum(dv, axis=0, keepdims=True)], axis=1)

        @pl.when(n == 0)
        def _():
            db_ref[...] = part

        @pl.when(n > 0)
        def _():
            db_ref[...] += part

    cur = lambda w: pl.BlockSpec((CHUNK, w), lambda n: (n, 0))
    nxt = lambda w: pl.BlockSpec((CHUNK, w), lambda n: (jnp.minimum(n + 1, nb - 1), 0))
    return pl.pallas_call(
        body, name=name,
        out_shape=[jax.ShapeDtypeStruct((T, width), BF16), jax.ShapeDtypeStruct((1, width), F32)],
        grid=(nb,),
        in_specs=[cur(QW), cur(KW), nxt(KW), cur(KW), nxt(KW), cur(LANES), cur(LANES)],
        out_specs=[cur(width), pl.BlockSpec((1, width), lambda n: (0, 0))],
        compiler_params=pltpu.CompilerParams(dimension_semantics=("arbitrary",), vmem_limit_bytes=_vmem(8 << 20)),
    )(dq, dkc, dkp, dvc, dvp, ctab, stab)


def _cast_block(w, l, axis, chip_arr, *, name):
    _, Ks, Ns = w.shape
    tk = _pick(Ks, (512, 352, 256, 128))
    nk = Ks // tk
    full = (Ks * N_CHIPS, Ns) if axis == 0 else (Ks, Ns * N_CHIPS)

    def body(p_ref, w_ref, o_ref):
        o_ref[...] = w_ref[...].astype(BF16)

    if axis == 0:
        out_spec = pl.BlockSpec((tk, Ns), lambda i, p: (p[0] * nk + i, 0))
    else:
        out_spec = pl.BlockSpec((tk, Ns), lambda i, p: (i, p[0]))
    grid_spec = pltpu.PrefetchScalarGridSpec(
        num_scalar_prefetch=1, grid=(nk,),
        in_specs=[pl.BlockSpec((None, tk, Ns), lambda i, p: (l, i, 0))], out_specs=out_spec)
    return pl.pallas_call(
        body, name=name, out_shape=jax.ShapeDtypeStruct(full, BF16), grid_spec=grid_spec,
        compiler_params=pltpu.CompilerParams(dimension_semantics=("arbitrary",),
                                             vmem_limit_bytes=_vmem(4 * tk * Ns * 6)),
    )(chip_arr, w)


def _adamw_math(w, g, m, v):
    m = ADAM_B1 * m + (1.0 - ADAM_B1) * g
    v = ADAM_B2 * v + (1.0 - ADAM_B2) * (g * g)
    m_hat = m / (1.0 - ADAM_B1 ** ADAM_STEP)
    v_hat = v / (1.0 - ADAM_B2 ** ADAM_STEP)
    delta = -ADAM_LR * (m_hat / (jnp.sqrt(v_hat) + ADAM_EPS) + ADAM_WD * w)
    return delta, m, v


def _adamw_layer(w, m, v, g, l, outs, *, name):
    _, K, N = w.shape
    tk = _pick(K, (256, 176, 128))

    def body(w_ref, m_ref, v_ref, g_ref, _g, _d, _m, _v, go_ref, d_ref, mo_ref, vo_ref):
        gv = g_ref[...]
        d, mn, vn = _adamw_math(w_ref[...], gv, m_ref[...], v_ref[...])
        go_ref[...] = gv
        d_ref[...] = d
        mo_ref[...] = mn
        vo_ref[...] = vn

    layer = pl.BlockSpec((None, tk, N), lambda i: (l, i, 0))
    any_spec = pl.BlockSpec(memory_space=pl.ANY)
    sd = jax.ShapeDtypeStruct(w.shape, F32)
    return pl.pallas_call(
        body, name=name, out_shape=[sd, sd, sd, sd], grid=(K // tk,),
        in_specs=[layer, layer, layer, pl.BlockSpec((tk, N), lambda i: (i, 0))] + [any_spec] * 4,
        out_specs=[layer] * 4, input_output_aliases={4: 0, 5: 1, 6: 2, 7: 3},
        compiler_params=pltpu.CompilerParams(dimension_semantics=("arbitrary",),
                                             vmem_limit_bytes=_vmem(2 * 8 * tk * N * 4 + 6 * tk * N * 4)),
    )(w, m, v, g, *outs)


def _adamw_small(w, g, m, v, *, name):
    def body(w_ref, g_ref, m_ref, v_ref, d_ref, mo_ref, vo_ref):
        d, mn, vn = _adamw_math(w_ref[...], g_ref[...], m_ref[...], v_ref[...])
        d_ref[...] = d
        mo_ref[...] = mn
        vo_ref[...] = vn

    sd = jax.ShapeDtypeStruct(w.shape, F32)
    return pl.pallas_call(body, name=name, out_shape=[sd, sd, sd])(w, g, m, v)


def _my_place():
    return lax.axis_index("x"), lax.axis_index("y"), lax.axis_index("c")


def _peer_chips(x, y):
    return [(1 - x, y), (x, 1 - y), (1 - x, 1 - y)]


_HBM = pl.BlockSpec(memory_space=pltpu.HBM)
_SEM = pl.BlockSpec(memory_space=pltpu.SEMAPHORE)
_EFFECT = pltpu.SideEffectType.DATAFLOW_SIDE_EFFECTING


def _split_start(name, bufs, n_copies, make_copies, after):
    nb = len(bufs)

    def body(*refs):
        send_sems, recv_sems = refs[nb + 1], refs[nb + 2]
        token = refs[2 * nb + 3]
        sends, _ = make_copies(refs[:nb], send_sems, recv_sems)
        for cp in sends:
            cp.start()
        token[...] = jnp.zeros_like(token)

    res = pl.pallas_call(
        body, name=name,
        out_shape=(pltpu.SemaphoreType.DMA((n_copies,)), pltpu.SemaphoreType.DMA((n_copies,)),
                   *[pltpu.HBM(b.shape, b.dtype) for b in bufs], jax.ShapeDtypeStruct((8, LANES), F32)),
        in_specs=[_HBM] * nb + [pl.BlockSpec(memory_space=pl.ANY)],
        out_specs=(_SEM, _SEM, *[_HBM] * nb, pl.BlockSpec(memory_space=pltpu.VMEM)),
        input_output_aliases={k: 2 + k for k in range(nb)},
        compiler_params=pltpu.CompilerParams(has_side_effects=_EFFECT),
    )(*[pltpu.with_memory_space_constraint(b, pltpu.HBM) for b in bufs], after)
    return res[0], res[1], list(res[2:2 + nb]), res[2 + nb]


def _split_wait(name, bufs, sems, make_copies, after):
    nb = len(bufs)

    def body(*refs):
        send_sems, recv_sems = refs[nb], refs[nb + 1]
        sends, recvs = make_copies(refs[:nb], send_sems, recv_sems)
        for cp in sends:
            cp.wait_send()
        for cp in recvs:
            cp.wait_recv()

    res = pl.pallas_call(
        body, name=name,
        out_shape=tuple(pltpu.HBM(b.shape, b.dtype) for b in bufs),
        in_specs=[_HBM] * nb + [_SEM, _SEM, pl.BlockSpec(memory_space=pl.ANY)],
        out_specs=tuple([_HBM] * nb),
        input_output_aliases={k: k for k in range(nb)},
        compiler_params=pltpu.CompilerParams(has_side_effects=_EFFECT),
    )(*bufs, sems[0], sems[1], after)
    return list(res)


def _remote(src, dst, send_sems, recv_sems, k, target):
    return pltpu.make_async_remote_copy(src_ref=src, dst_ref=dst, send_sem=send_sems.at[k],
                                        recv_sem=recv_sems.at[k], device_id=target, device_id_type=MESH)


def _ag_region(ref, axis, chip, half):
    K, N = ref.shape
    if axis == 0:
        hs = K // N_CHIPS // 2
        assert hs % 16 == 0
        return ref.at[pl.ds(pl.multiple_of((2 * chip + half) * hs, 16), hs), :]
    ns, hk = N // N_CHIPS, K // 2
    assert ns % LANES == 0 and hk % 16 == 0
    return ref.at[pl.ds(pl.multiple_of(half * hk, 16), hk), pl.ds(pl.multiple_of(chip * ns, LANES), ns)]


def _ag_copies(stage, axes):
    n = len(axes)

    def make(bufs, send_sems, recv_sems):
        x, y, c = _my_place()
        me = 2 * x + y
        sends, recvs = [], []
        for j, (px, py) in enumerate(_peer_chips(x, y)):
            other = 2 * px + py
            for w in range(n):
                k = j * n + w
                if stage == 1:
                    src, target = _ag_region(bufs[w], axes[w], me, c), (px, py, c)
                    land = _ag_region(bufs[w], axes[w], other, c)
                else:
                    src, target = _ag_region(bufs[w], axes[w], other, c), (x, y, 1 - c)
                    land = _ag_region(bufs[w], axes[w], other, 1 - c)
                sends.append(_remote(src, src, send_sems, recv_sems, k, target))
                recvs.append(_remote(land, land, send_sems, recv_sems, k, target))
        return sends, recvs

    return make


def _half_shape(shape, axis):
    K, N = shape
    return (K, N // 2) if axis == 0 else (K // 2, N)


def _core_half(ref, axis, half):
    K, N = ref.shape
    if axis == 0:
        return ref.at[:, pl.ds(pl.multiple_of(half * (N // 2), LANES), N // 2)]
    return ref.at[pl.ds(pl.multiple_of(half * (K // 2), 16), K // 2), :]


def _chip_block(ref, axis, chip):
    K, N = ref.shape
    if axis == 0:
        return ref.at[pl.ds(pl.multiple_of(chip * (K // N_CHIPS), 16), K // N_CHIPS), :]
    return ref.at[:, pl.ds(pl.multiple_of(chip * (N // N_CHIPS), LANES), N // N_CHIPS)]


def _rs_sibling_copies(axes):
    n = len(axes)

    def make(bufs, send_sems, recv_sems):
        x, y, c = _my_place()
        sends = [_remote(_core_half(bufs[w], axes[w], 1 - c), bufs[n + w], send_sems, recv_sems, w, (x, y, 1 - c))
                 for w in range(n)]
        recvs = [_remote(bufs[n + w], bufs[n + w], send_sems, recv_sems, w, (x, y, 1 - c)) for w in range(n)]
        return sends, recvs

    return make


def _rs_chip_copies(axes):
    n = len(axes)

    def make(bufs, send_sems, recv_sems):
        x, y, c = _my_place()
        sends, recvs = [], []
        for j, (px, py) in enumerate(_peer_chips(x, y)):
            for w in range(n):
                k = j * n + w
                sends.append(_remote(_chip_block(bufs[w], axes[w], 2 * px + py), bufs[n + w].at[j],
                                     send_sems, recv_sems, k, (px, py, c)))
                recvs.append(_remote(bufs[n + w].at[j], bufs[n + w].at[j], send_sems, recv_sems, k, (px, py, c)))
        return sends, recvs

    return make


def _rs_fill_copies(axes):
    n = len(axes)

    def make(bufs, send_sems, recv_sems):
        x, y, c = _my_place()
        sends = [_remote(_core_half(bufs[w], axes[w], c), _core_half(bufs[w], axes[w], c),
                         send_sems, recv_sems, w, (x, y, 1 - c)) for w in range(n)]
        recvs = [_remote(_core_half(bufs[w], axes[w], 1 - c), _core_half(bufs[w], axes[w], 1 - c),
                         send_sems, recv_sems, w, (x, y, 1 - c)) for w in range(n)]
        return sends, recvs

    return make


def _chip_sum(g, r, axis, place, *, name):
    hk, hn = r.shape
    bk, bn = (hk // N_CHIPS, hn) if axis == 0 else (hk, hn // N_CHIPS)
    tk = _pick(bk, (512, 352, 256, 128))
    nk = bk // tk

    def body(p_ref, g_ref, r_ref, b_ref, own_ref):
        s = g_ref[...].astype(F32) + r_ref[...].astype(F32)
        b_ref[...] = s.astype(BF16)

        @pl.when(pl.program_id(1) == p_ref[0])
        def _():
            own_ref[...] = s

    if axis == 0:
        g_spec = pl.BlockSpec((tk, bn), lambda i, j, p: (j * nk + i, p[1]))
        r_spec = pl.BlockSpec((tk, bn), lambda i, j, p: (j * nk + i, 0))
    else:
        g_spec = pl.BlockSpec((tk, bn), lambda i, j, p: (p[1] * nk + i, j))
        r_spec = pl.BlockSpec((tk, bn), lambda i, j, p: (i, j))
    grid_spec = pltpu.PrefetchScalarGridSpec(
        num_scalar_prefetch=1, grid=(nk, N_CHIPS), in_specs=[g_spec, r_spec],
        out_specs=[r_spec, pl.BlockSpec((tk, bn), lambda i, j, p: (i, 0))])
    return pl.pallas_call(
        body, name=name,
        out_shape=[jax.ShapeDtypeStruct(r.shape, BF16), jax.ShapeDtypeStruct((bk, bn), F32)],
        grid_spec=grid_spec,
        compiler_params=pltpu.CompilerParams(dimension_semantics=("arbitrary", "arbitrary"),
                                             vmem_limit_bytes=_vmem(2 * tk * bn * 10 + 3 * tk * bn * 4)),
    )(place, g, r)


def _final_sum(own, recv, axis, place, *, name):
    _, bk, bn = recv.shape
    tk = _pick(bk, (256, 176, 128))
    nk = bk // tk

    def body(p_ref, o_ref, r_ref, out_ref):
        out_ref[...] = ((o_ref[...] + r_ref[0].astype(F32)) + r_ref[1].astype(F32)) + r_ref[2].astype(F32)

    own_spec = pl.BlockSpec((tk, bn), lambda i, p: (i, 0))
    if axis == 0:
        out_shape, out_spec = (bk, 2 * bn), pl.BlockSpec((tk, bn), lambda i, p: (i, p[1]))
    else:
        out_shape, out_spec = (2 * bk, bn), pl.BlockSpec((tk, bn), lambda i, p: (p[1] * nk + i, 0))
    grid_spec = pltpu.PrefetchScalarGridSpec(
        num_scalar_prefetch=1, grid=(nk,),
        in_specs=[own_spec, pl.BlockSpec((3, tk, bn), lambda i, p: (0, i, 0))], out_specs=out_spec)
    return pl.pallas_call(
        body, name=name, out_shape=jax.ShapeDtypeStruct(out_shape, F32), grid_spec=grid_spec,
        compiler_params=pltpu.CompilerParams(dimension_semantics=("arbitrary",),
                                             vmem_limit_bytes=_vmem(2 * tk * bn * 14 + 4 * tk * bn * 4)),
    )(place, own, recv)


def _allreduce_small(p):
    def body(p_ref, o_ref, r0, r1, r2, send_sems, recv_sems):
        x, y, c = _my_place()
        o_ref[...] = p_ref[...]
        for s, (peer, rbuf) in enumerate([((x, y, 1 - c), r0), ((1 - x, y, c), r1), ((x, 1 - y, c), r2)]):
            cp = pltpu.make_async_remote_copy(src_ref=o_ref, dst_ref=rbuf, send_sem=send_sems.at[s],
                                              recv_sem=recv_sems.at[s], device_id=peer, device_id_type=MESH)
            cp.start()
            cp.wait()
            o_ref[...] = o_ref[...] + rbuf[...]

    vm = pl.BlockSpec(memory_space=pltpu.VMEM)
    return pl.pallas_call(
        body, name="allreduce_small", out_shape=jax.ShapeDtypeStruct(p.shape, F32),
        in_specs=[vm], out_specs=vm,
        scratch_shapes=[pltpu.VMEM(p.shape, F32)] * 3 + [pltpu.SemaphoreType.DMA((3,))] * 2,
        compiler_params=pltpu.CompilerParams(vmem_limit_bytes=_vmem(6 * _nbytes(p.shape, F32))),
    )(p)


def _pack_rows(parts):
    rows, metas = [], []
    for a in parts:
        flat = a.reshape(-1)
        nrow = -(-flat.shape[0] // LANES)
        nrow = -(-nrow // 8) * 8
        flat = jnp.pad(flat, (0, nrow * LANES - flat.shape[0]))
        rows.append(flat.reshape(nrow, LANES))
        metas.append((a.shape, nrow))
    return jnp.concatenate(rows, axis=0), metas


def _unpack_rows(packed, metas):
    out, r0 = [], 0
    for shape, nrow in metas:
        size = int(np.prod(shape))
        out.append(packed[r0:r0 + nrow].reshape(-1)[:size].reshape(shape))
        r0 += nrow
    return out


def kernel(x, positions, pre_mix_g, post_mix_g, pre_ffn_g, post_ffn_g, a_w_in, a_b_in, a_ln_g, a_ln_b, a_w_s, a_b_s, a_w_out, b_w_qkv, b_b_qkv, b_sinks, b_w_o, ffn_w_gu, ffn_w_down, loss_target, m_pre_mix_g, m_post_mix_g, m_pre_ffn_g, m_post_ffn_g, m_a_w_in, m_a_b_in, m_a_ln_g, m_a_ln_b, m_a_w_s, m_a_b_s, m_a_w_out, m_b_w_qkv, m_b_b_qkv, m_b_sinks, m_b_w_o, m_ffn_w_gu, m_ffn_w_down, v_pre_mix_g, v_post_mix_g, v_pre_ffn_g, v_post_ffn_g, v_a_w_in, v_a_b_in, v_a_ln_g, v_a_ln_b, v_a_w_s, v_a_b_s, v_a_w_out, v_b_w_qkv, v_b_b_qkv, v_b_sinks, v_b_w_o, v_ffn_w_gu, v_ffn_w_down):
    depth, D = pre_mix_g.shape
    xi, yi, ci = _my_place()
    chip = 2 * xi + yi
    place = jnp.stack([chip, ci]).astype(jnp.int32)

    stacked = {"a_w_in": (a_w_in, m_a_w_in, v_a_w_in), "a_w_out": (a_w_out, m_a_w_out, v_a_w_out),
               "b_w_qkv": (b_w_qkv, m_b_w_qkv, v_b_w_qkv), "b_w_o": (b_w_o, m_b_w_o, v_b_w_o),
               "ffn_w_gu": (ffn_w_gu, m_ffn_w_gu, v_ffn_w_gu), "ffn_w_down": (ffn_w_down, m_ffn_w_down, v_ffn_w_down)}
    cut = {"a_w_in": 1, "a_w_out": 0, "b_w_qkv": 1, "b_w_o": 0, "ffn_w_gu": 1, "ffn_w_down": 0}

    def layer_keys(i):
        mix = [("a_w_in", i // 2), ("a_w_out", i // 2)] if i % 2 == 0 else [("b_w_qkv", i // 2), ("b_w_o", i // 2)]
        return mix + [("ffn_w_gu", i), ("ffn_w_down", i)]

    def dep(a, toks):
        for t in toks:
            a = a + t[:1, :1]
        return a

    W = {}
    for i in range(depth):
        for nm, l in layer_keys(i):
            W[(nm, l)] = _cast_block(stacked[nm][0], l, cut[nm], place, name=f"cast_{nm}_{l}")

    def gather(tag, keys, after):
        axes = [cut[nm] for nm, _ in keys]
        for stage in (1, 2):
            ss, rs, bufs, tok = _split_start(f"ag{stage}_start_{tag}", [W[k] for k in keys], 3 * len(keys),
                                             _ag_copies(stage, axes), after)
            after = yield tok
            bufs = _split_wait(f"ag{stage}_wait_{tag}", bufs, (ss, rs), _ag_copies(stage, axes), after)
            W.update(zip(keys, bufs))
        yield None

    nq = b_b_qkv.shape[1]
    bq_full = jnp.zeros((b_b_qkv.shape[0], N_CHIPS * nq), F32)
    bq_full = lax.dynamic_update_slice(bq_full, jnp.where(ci == 0, b_b_qkv, 0.0), (0, chip * nq))
    bq_packed, bq_meta = _pack_rows([bq_full])
    bq_gathered = _allreduce_small(bq_packed)
    b_qkv_full = _unpack_rows(bq_gathered, bq_meta)[0]

    first = gather("0m", layer_keys(0)[:2], bq_gathered)
    tok = next(first)
    tok = first.send(tok)
    first.send(tok)

    h = x[0]
    target = loss_target[0]
    ctab, stab = _rope_tables(positions[0])
    q_width = W[("b_w_o", 0)].shape[0]
    kv_width = N_KV_HEADS * HEAD_DIM
    row = lambda a, i: a[i:i + 1]

    saved = []
    hn = None
    for i in range(depth):
        j = i // 2
        s = {"h": h}
        ffn_w = None
        if i == 0:
            ffn_w = gather("0f", layer_keys(0)[2:], W[("a_w_out", 0)])
            toks = [next(ffn_w)]
            nxt = gather("1", layer_keys(1), toks[0])
            toks.append(next(nxt))
            hn = _rms_fwd(h, dep(row(pre_mix_g, i), toks), out_dtype=BF16, name=f"rms_pre_mix_{i}")
        elif i + 1 < depth:
            nxt = gather(str(i + 1), layer_keys(i + 1), h)
            toks = [next(nxt)]
        else:
            toks = []
        s["hn"] = hn
        if i % 2 == 0:
            pre = _matmul(hn, W[("a_w_in", j)], mode="nn", bias=dep(row(a_b_in, j), toks), out_dtype=F32,
                          name=f"gmlp_in_{i}")
            gated = _sgu_fwd(pre, row(a_ln_g, j), row(a_ln_b, j), a_w_s[j], a_b_s[j].T, name=f"sgu_fwd_{i}")
            mix = _matmul(gated, W[("a_w_out", j)], mode="nn", out_dtype=F32, name=f"gmlp_out_{i}")
            s.update(pre=pre, gated=gated)
        else:
            qkv = _matmul(hn, W[("b_w_qkv", j)], mode="nn", bias=dep(row(b_qkv_full, j), toks), out_dtype=F32,
                          name=f"attn_qkv_{i}")
            qr, kr, vr = _rope_fwd(qkv, ctab, stab, q_width=q_width, kv_width=kv_width, name=f"rope_fwd_{i}")
            o = _attn_fwd(qr, kr, vr, row(b_sinks, j), name=f"attn_fwd_{i}")
            mix = _matmul(o, W[("b_w_o", j)], mode="nn", out_dtype=F32, name=f"attn_o_{i}")
            s.update(qr=qr, kr=kr, vr=vr, o=o)
        s["mix"] = mix
        toks = [ffn_w.send(mix)] if ffn_w else []
        h1, fn = _rms_res_norm(h, mix, dep(row(post_mix_g, i), toks), row(pre_ffn_g, i), name=f"rms_post_mix_{i}")
        if ffn_w:
            ffn_w.send(h1)
        s["h1"] = h1
        g_pre, u_pre, act = _ffn_up(fn, W[("ffn_w_gu", i)][None], 0, name=f"ffn_up_{i}")
        f = _matmul(act, W[("ffn_w_down", i)], mode="nn", out_dtype=F32, name=f"ffn_down_{i}")
        if i + 1 < depth:
            toks = [nxt.send(f)]
            h, hn = _rms_res_norm(h1, f, dep(row(post_ffn_g, i), toks), row(pre_mix_g, i + 1),
                                  name=f"rms_post_ffn_{i}")
            nxt.send(h)
        else:
            h = _rms_res(h1, f, row(post_ffn_g, i), name=f"rms_post_ffn_{i}")
        s.update(fn=fn, g_pre=g_pre, u_pre=u_pre, act=act, f=f)
        saved.append(s)

    dh, df, loss_part, g_last = _loss_and_grad(h, target, saved[-1]["f"], row(post_ffn_g, depth - 1), name="loss")
    loss = lax.psum(loss_part[0, 0], ("x", "y", "c"))

    big_out = {nm: tuple(lax.empty(w.shape, F32) for _ in range(4)) for nm, (w, _, _) in stacked.items()}

    def reduce_group(i, keys, grads):
        axes = [cut[nm] for nm, _ in keys]
        n = len(keys)
        lands = [lax.empty(_half_shape(g.shape, ax), BF16) for g, ax in zip(grads, axes)]
        ss, rs, bufs, tok = _split_start(f"rs_sibling_start_{i}", list(grads) + lands, n, _rs_sibling_copies(axes),
                                         place)
        after = yield tok
        bufs = _split_wait(f"rs_sibling_wait_{i}", bufs, (ss, rs), _rs_sibling_copies(axes), after)
        sums = [_chip_sum(bufs[w], bufs[n + w], axes[w], place, name=f"chip_sum_{keys[w][0]}_{keys[w][1]}")
                for w in range(n)]
        lands = [lax.empty((3,) + own.shape, BF16) for _, own in sums]
        ss, rs, bufs, tok = _split_start(f"rs_chip_start_{i}", [sb for sb, _ in sums] + lands, 3 * n,
                                         _rs_chip_copies(axes), place)
        after = yield tok
        bufs = _split_wait(f"rs_chip_wait_{i}", bufs, (ss, rs), _rs_chip_copies(axes), after)
        blocks = [_final_sum(sums[w][1], bufs[n + w], axes[w], place, name=f"final_sum_{keys[w][0]}_{keys[w][1]}")
                  for w in range(n)]
        ss, rs, bufs, tok = _split_start(f"rs_fill_start_{i}", blocks, n, _rs_fill_copies(axes), place)
        after = yield tok
        blocks = _split_wait(f"rs_fill_wait_{i}", bufs, (ss, rs), _rs_fill_copies(axes), after)
        for (nm, l), g in zip(keys, blocks):
            w, m, v = stacked[nm]
            big_out[nm] = tuple(_adamw_layer(w, m, v, g, l, big_out[nm], name=f"adamw_{nm}_{l}"))
        yield None

    reducing = []

    def advance(after):
        toks = []
        for gen in list(reducing):
            tok = gen.send(after)
            if tok is None:
                reducing.remove(gen)
            else:
                toks.append(tok)
        return toks

    small = {}
    g_pre_mix, g_post_mix, g_pre_ffn, g_post_ffn = [None] * depth, [None] * depth, [None] * depth, [None] * depth
    g_post_ffn[depth - 1] = g_last
    for i in reversed(range(depth)):
        j = i // 2
        s = saved[i]
        g_down = _matmul(s["act"], df, mode="tn", out_dtype=BF16, name=f"ffn_down_dw_{i}")
        dg_, du_ = _ffn_down_dx(df, W[("ffn_w_down", i)][None], 0, s["g_pre"], s["u_pre"], name=f"ffn_down_dx_{i}")
        hid = dg_.shape[1]
        tile = _pick(hid, (1408, 768, 512, 256, 128))
        w_gu = W[("ffn_w_gu", i)]
        g_gu = lax.empty(w_gu.shape, BF16)
        g_gu = _matmul(s["fn"], dg_, mode="tn", into=g_gu, tq=tile, out_dtype=BF16, name=f"ffn_g_dw_{i}")
        g_gu = _matmul(s["fn"], du_, mode="tn", into=g_gu, tq=tile, q_off=hid // tile, out_dtype=BF16,
                       name=f"ffn_u_dw_{i}")
        dfn_g = _matmul(dg_, w_gu, mode="nt", tr=hid, out_dtype=F32, name=f"ffn_g_dx_{i}")
        dfn = _matmul(du_, w_gu, mode="nt", tr=hid, b_r_off=1, bias=dfn_g, out_dtype=F32, name=f"ffn_u_dx_{i}")
        toks = advance(dfn)
        if i == 0:
            gen = reduce_group("0f", layer_keys(0)[2:], [g_gu, g_down])
            toks.append(next(gen))
            reducing.append(gen)
        dh1, dmix, g_pre_ffn[i], g_post_mix[i] = _rms_bwd_chain(
            s["h1"], dep(row(pre_ffn_g, i), toks), dfn, dh, s["mix"], row(post_mix_g, i), name=f"rms_ffn_mix_bwd_{i}")
        if i % 2 == 0:
            g_out = _matmul(s["gated"], dmix, mode="tn", out_dtype=BF16, name=f"gmlp_out_dw_{i}")
            dgated = _matmul(dmix, W[("a_w_out", j)], mode="nt", out_dtype=F32, name=f"gmlp_out_dx_{i}")
            if i == 0:
                advance(dgated)
            dpre, dws, dbsT, dlng, dlnb, dbin = _sgu_bwd(s["pre"], dgated, row(a_ln_g, j), row(a_ln_b, j),
                                                         a_w_s[j], a_b_s[j].T, name=f"sgu_bwd_{i}")
            small[("a_w_s", j)] = dws
            small[("a_b_s", j)] = dbsT.T
            small[("a_ln_g", j)] = dlng
            small[("a_ln_b", j)] = dlnb
            small[("a_b_in", j)] = dbin
            g_in = _matmul(s["hn"], dpre, mode="tn", out_dtype=BF16, name=f"gmlp_in_dw_{i}")
            dhn = _matmul(dpre, W[("a_w_in", j)], mode="nt", out_dtype=F32, name=f"gmlp_in_dx_{i}")
        else:
            g_out = _matmul(s["o"], dmix, mode="tn", out_dtype=BF16, name=f"attn_o_dw_{i}")
            do = _matmul(dmix, W[("b_w_o", j)], mode="nt", out_dtype=BF16, name=f"attn_o_dx_{i}")
            dq, dkp, dkc, dvp, dvc, dsk = _attn_bwd(s["qr"], s["kr"], s["vr"], row(b_sinks, j), do,
                                                    name=f"attn_bwd_{i}")
            dqkv, dbq = _rope_bwd(dq, dkp, dkc, dvp, dvc, ctab, stab, name=f"rope_bwd_{i}")
            small[("b_sinks", j)] = dsk[:, :b_sinks.shape[1]]
            small[("b_b_qkv", j)] = dbq
            g_in = _matmul(s["hn"], dqkv, mode="tn", out_dtype=BF16, name=f"attn_qkv_dw_{i}")
            dhn = _matmul(dqkv, W[("b_w_qkv", j)], mode="nt", out_dtype=F32, name=f"attn_qkv_dx_{i}")
        toks = advance(dhn)
        if i > 0:
            dh, df, g_pre_mix[i], g_post_ffn[i - 1] = _rms_bwd_chain(
                s["h"], dep(row(pre_mix_g, i), toks), dhn, dh1, saved[i - 1]["f"], row(post_ffn_g, i - 1),
                name=f"rms_mix_ffn_bwd_{i}")
        else:
            dh, g_pre_mix[i] = _rms_bwd(s["h"], dep(row(pre_mix_g, i), toks), dhn, dh1, out_dtype=F32,
                                        name=f"rms_pre_mix_bwd_{i}")
        if i == 0:
            gen = reduce_group("0m", layer_keys(0)[:2], [g_in, g_out])
        else:
            gen = reduce_group(str(i), layer_keys(i), [g_in, g_out, g_gu, g_down])
        toks = [next(gen)] + advance(dh)
        reducing.append(gen)
    grad_x = dh[None]

    toks = advance(dh)
    n_a, n_b = a_b_in.shape[0], b_sinks.shape[0]
    stack = lambda key, n: jnp.concatenate([small[(key, j)] for j in range(n)], axis=0)
    small_parts = [
        jnp.concatenate(g_pre_mix, axis=0), jnp.concatenate(g_post_mix, axis=0),
        jnp.concatenate(g_pre_ffn, axis=0), jnp.concatenate(g_post_ffn, axis=0),
        stack("a_b_in", n_a), stack("a_ln_g", n_a), stack("a_ln_b", n_a),
        jnp.stack([small[("a_w_s", j)] for j in range(n_a)]), jnp.stack([small[("a_b_s", j)] for j in range(n_a)]),
        stack("b_b_qkv", n_b), stack("b_sinks", n_b),
    ]
    packed, metas = _pack_rows(small_parts)
    reduced = _allreduce_small(dep(packed, toks))
    while reducing:
        advance(reduced)
    red = _unpack_rows(reduced, metas)
    (gr_pre_mix, gr_post_mix, gr_pre_ffn, gr_post_ffn, gr_b_in, gr_ln_g, gr_ln_b, gr_w_s, gr_b_s,
     gr_b_qkv_full, gr_sinks) = red
    gr_b_qkv = lax.dynamic_slice(gr_b_qkv_full, (0, chip * nq), (gr_b_qkv_full.shape[0], nq))

    grads = {"pre_mix_g": gr_pre_mix, "post_mix_g": gr_post_mix, "pre_ffn_g": gr_pre_ffn, "post_ffn_g": gr_post_ffn,
             "a_b_in": gr_b_in, "a_ln_g": gr_ln_g, "a_ln_b": gr_ln_b, "a_w_s": gr_w_s, "a_b_s": gr_b_s,
             "b_b_qkv": gr_b_qkv, "b_sinks": gr_sinks}
    weights = {"pre_mix_g": (pre_mix_g, m_pre_mix_g, v_pre_mix_g), "post_mix_g": (post_mix_g, m_post_mix_g, v_post_mix_g),
               "pre_ffn_g": (pre_ffn_g, m_pre_ffn_g, v_pre_ffn_g), "post_ffn_g": (post_ffn_g, m_post_ffn_g, v_post_ffn_g),
               "a_b_in": (a_b_in, m_a_b_in, v_a_b_in), "a_ln_g": (a_ln_g, m_a_ln_g, v_a_ln_g),
               "a_ln_b": (a_ln_b, m_a_ln_b, v_a_ln_b), "a_w_s": (a_w_s, m_a_w_s, v_a_w_s), "a_b_s": (a_b_s, m_a_b_s, v_a_b_s),
               "b_b_qkv": (b_b_qkv, m_b_b_qkv, v_b_b_qkv), "b_sinks": (b_sinks, m_b_sinks, v_b_sinks)}
    order = ["pre_mix_g", "post_mix_g", "pre_ffn_g", "post_ffn_g", "a_w_in", "a_b_in", "a_ln_g", "a_ln_b", "a_w_s",
             "a_b_s", "a_w_out", "b_w_qkv", "b_b_qkv", "b_sinks", "b_w_o", "ffn_w_gu", "ffn_w_down"]
    deltas, new_m, new_v = {}, {}, {}
    for nm in order:
        if nm in big_out:
            grads[nm], deltas[nm], new_m[nm], new_v[nm] = big_out[nm]
        else:
            w, m, v = weights[nm]
            deltas[nm], new_m[nm], new_v[nm] = _adamw_small(w, grads[nm], m, v, name="adamw_" + nm)
    return (loss, grad_x, *[grads[nm] for nm in order], *[deltas[nm] for nm in order],
            *[new_m[nm] for nm in order], *[new_v[nm] for nm in order])
```

```python
import functools
import math

import jax
import jax.numpy as jnp
import numpy as np
from jax import lax
from jax.experimental import pallas as pl
from jax.experimental.pallas import tpu as pltpu

F32 = jnp.float32
BF16 = jnp.bfloat16
MESH = pl.DeviceIdType.MESH

HEAD_DIM = 64
N_KV_HEADS = 4
ROPE_DIM = 16
ROPE_THETA = 500000.0
CHUNK = 128
GMLP_GROUPS = 8
RMS_EPS = 1e-6
LN_EPS = 1e-5
NEG_INF = -1e30
ADAM_LR = 0.001
ADAM_B1 = 0.9
ADAM_B2 = 0.999
ADAM_EPS = 1e-08
ADAM_WD = 0.01
ADAM_STEP = 10

N_CHIPS = 4
LANES = 128
VMEM_CAP = 58 * 1024 * 1024


def _vmem(est_bytes):
    assert est_bytes < VMEM_CAP
    return VMEM_CAP


def _pick(n, cands):
    for c in cands:
        if c <= n and n % c == 0:
            return c
    return n


def _nbytes(shape, dtype):
    return int(np.prod(shape)) * jnp.dtype(dtype).itemsize


MATMUL_VMEM_BUDGET = 48 * 1024 * 1024


def _halvings(n, unit):
    out, t = [], n
    while t % unit == 0 and t >= unit:
        out.append(t)
        if t % 2:
            break
        t //= 2
    return out


def _matmul_tiles(P, Q, R, a_bytes, b_bytes, o_bytes, full_addend, tp, tq, tr):
    step_us, bytes_per_us = 0.85, 3.2e6
    best = None
    for p in ([tp] if tp else _halvings(P, LANES)):
        for q in ([tq] if tq else _halvings(Q, LANES)):
            for r in ([tr] if tr else _halvings(R, LANES)):
                nk = R // r
                vm = 2 * (p * r * a_bytes + r * q * b_bytes + p * q * o_bytes + (p * q * 4 if full_addend else 0))
                vm += p * q * 4 * (2 if nk > 1 else 1)
                if vm > MATMUL_VMEM_BUDGET:
                    continue
                exposed = (p * r * a_bytes + r * q * b_bytes + p * q * o_bytes) / bytes_per_us
                key = ((P // p) * (Q // q) * nk * step_us + exposed, nk, abs(p - q))
                if best is None or key < best[0]:
                    best = (key, (p, q, r))
    assert best is not None, (P, Q, R)
    return best[1]


def _matmul(a, b, *, mode, out_dtype, name, a_l=None, b_l=None, bias=None, into=None, o_l=None,
            q_off=0, b_r_off=0, tp=None, tq=None, tr=None, after=()):
    a2 = a.shape[-2:]
    b2 = b.shape[-2:]
    if mode == "nn":
        (P, R), (R2, Q) = a2, b2
    elif mode == "nt":
        (P, R), (Q, R2) = a2, b2
    else:
        (R, P), (R2, Q) = a2, b2
    assert R == R2 or (mode == "nt" and R2 % R == 0), (mode, a.shape, b.shape)
    o_bytes = jnp.dtype(into.dtype if into is not None else out_dtype).itemsize
    full_addend = bias is not None and bias.shape[0] != 1
    tp, tq, tr = _matmul_tiles(P, Q, R, a.dtype.itemsize, b.dtype.itemsize, o_bytes, full_addend, tp, tq, tr)
    assert P % tp == 0 and Q % tq == 0 and R % tr == 0
    nk = R // tr
    dims = {"nn": (((1,), (0,)), ((), ())), "nt": (((1,), (1,)), ((), ())), "tn": (((0,), (0,)), ((), ()))}[mode]

    def lead(l, blk, idx):
        if l is None:
            return pl.BlockSpec(blk, idx)
        return pl.BlockSpec((None,) + blk, lambda i, j, k: (l,) + idx(i, j, k))

    if mode == "nn":
        a_spec = lead(a_l, (tp, tr), lambda i, j, k: (i, k))
        b_spec = lead(b_l, (tr, tq), lambda i, j, k: (k, j))
    elif mode == "nt":
        a_spec = lead(a_l, (tp, tr), lambda i, j, k: (i, k))
        b_spec = lead(b_l, (tq, tr), lambda i, j, k: (j, k + b_r_off))
    else:
        a_spec = lead(a_l, (tr, tp), lambda i, j, k: (k, i))
        b_spec = lead(b_l, (tr, tq), lambda i, j, k: (k, j))
    in_specs = [a_spec, b_spec]
    args = [a, b]
    if bias is not None:
        if bias.shape[0] == 1:
            in_specs.append(pl.BlockSpec((1, tq), lambda i, j, k: (0, j)))
        else:
            in_specs.append(pl.BlockSpec((tp, tq), lambda i, j, k: (i, j)))
        args.append(bias)
    aliases = {}
    if into is not None:
        in_specs.append(pl.BlockSpec(memory_space=pl.ANY))
        args.append(into)
        aliases = {len(args) - 1: 0}
        out_shape = jax.ShapeDtypeStruct(into.shape, into.dtype)
        out_dtype = into.dtype
        if o_l is None:
            out_spec = pl.BlockSpec((tp, tq), lambda i, j, k: (i, j + q_off))
        else:
            out_spec = pl.BlockSpec((None, tp, tq), lambda i, j, k: (o_l, i, j + q_off))
    else:
        out_shape = jax.ShapeDtypeStruct((P, Q), out_dtype)
        out_spec = pl.BlockSpec((tp, tq), lambda i, j, k: (i, j))
    n_in = len(args) + len(after)
    in_specs += [pl.BlockSpec(memory_space=pl.ANY)] * len(after)
    args += list(after)
    has_bias = bias is not None
    has_into = into is not None

    def body(*refs):
        a_ref, b_ref = refs[0], refs[1]
        pos = 2
        bias_ref = None
        if has_bias:
            bias_ref = refs[pos]
            pos += 1
        o_ref = refs[n_in]
        acc_ref = refs[n_in + 1] if nk > 1 else None
        part = lax.dot_general(a_ref[...], b_ref[...], dims, preferred_element_type=F32)

        def finish(acc):
            if has_bias:
                acc = acc + bias_ref[...]
            o_ref[...] = acc.astype(out_dtype)

        if nk == 1:
            finish(part)
        else:
            k = pl.program_id(2)

            @pl.when(k == 0)
            def _():
                acc_ref[...] = part

            @pl.when(k > 0)
            def _():
                acc_ref[...] += part

            @pl.when(k == nk - 1)
            def _():
                finish(acc_ref[...])

    est = 2 * (_nbytes((tp, tr), a.dtype) + _nbytes((tr, tq), b.dtype) + _nbytes((tp, tq), out_dtype)) + 3 * tp * tq * 4
    return pl.pallas_call(
        body, name=name, out_shape=out_shape,
        grid=(P // tp, Q // tq, nk),
        in_specs=in_specs, out_specs=out_spec,
        scratch_shapes=[pltpu.VMEM((tp, tq), F32)] if nk > 1 else [],
        input_output_aliases=aliases,
        compiler_params=pltpu.CompilerParams(
            dimension_semantics=("parallel", "parallel", "arbitrary"), vmem_limit_bytes=_vmem(est)),
    )(*args)


def _row_call(body, ins, outs, *, name, rows, tr, acc_outs=(), est=0):
    in_specs = []
    for arr, kind in ins:
        if kind == "row":
            in_specs.append(pl.BlockSpec((tr, arr.shape[1]), lambda i: (i, 0)))
        else:
            nd = arr.ndim
            in_specs.append(pl.BlockSpec(arr.shape, lambda i, nd=nd: (0,) * nd))
    out_shapes = [jax.ShapeDtypeStruct(s, d) for s, d in outs] + [jax.ShapeDtypeStruct(s, d) for s, d in acc_outs]
    out_specs = [pl.BlockSpec((tr, s[1]), lambda i: (i, 0)) for s, _ in outs]
    out_specs += [pl.BlockSpec(s, lambda i, nd=len(s): (0,) * nd) for s, _ in acc_outs]
    res = pl.pallas_call(
        body, name=name, out_shape=out_shapes, grid=(rows // tr,), in_specs=in_specs, out_specs=out_specs,
        compiler_params=pltpu.CompilerParams(dimension_semantics=("arbitrary",), vmem_limit_bytes=_vmem(est)),
    )(*[a for a, _ in ins])
    return res


def _rms_fwd(x, g, *, out_dtype, name):
    T, D = x.shape
    tr = _pick(T, (512, 256, 128))

    def body(x_ref, g_ref, o_ref):
        xv = x_ref[...]
        r = lax.rsqrt(jnp.mean(xv * xv, axis=-1, keepdims=True) + RMS_EPS)
        o_ref[...] = (xv * r * g_ref[...]).astype(out_dtype)

    return _row_call(body, [(x, "row"), (g, "full")], [((T, D), out_dtype)], name=name, rows=T, tr=tr,
                     est=8 * tr * D * 4)[0]


def _rms_res(h, y, g, *, name):
    T, D = h.shape
    tr = _pick(T, (512, 256, 128))

    def body(h_ref, y_ref, g_ref, o_ref):
        yv = y_ref[...]
        r = lax.rsqrt(jnp.mean(yv * yv, axis=-1, keepdims=True) + RMS_EPS)
        o_ref[...] = h_ref[...] + yv * r * g_ref[...]

    return _row_call(body, [(h, "row"), (y, "row"), (g, "full")], [((T, D), F32)], name=name, rows=T, tr=tr,
                     est=10 * tr * D * 4)[0]


def _rms_bwd(x, g, dy, dres, *, out_dtype, name):
    T, D = x.shape
    tr = _pick(T, (512, 256, 128))
    has_res = dres is not None

    def body(*refs):
        if has_res:
            x_ref, g_ref, dy_ref, dr_ref, dx_ref, dg_ref = refs
        else:
            x_ref, g_ref, dy_ref, dx_ref, dg_ref = refs
        xv = x_ref[...]
        r = lax.rsqrt(jnp.mean(xv * xv, axis=-1, keepdims=True) + RMS_EPS)
        xhat = xv * r
        dyv = dy_ref[...].astype(F32)
        dxn = dyv * g_ref[...]
        dx = r * (dxn - xhat * jnp.mean(dxn * xhat, axis=-1, keepdims=True))
        if has_res:
            dx = dx + dr_ref[...]
        dx_ref[...] = dx.astype(out_dtype)
        part = jnp.sum(dyv * xhat, axis=0, keepdims=True)

        @pl.when(pl.program_id(0) == 0)
        def _():
            dg_ref[...] = part

        @pl.when(pl.program_id(0) > 0)
        def _():
            dg_ref[...] += part

    ins = [(x, "row"), (g, "full"), (dy, "row")] + ([(dres, "row")] if has_res else [])
    dx, dg = _row_call(body, ins, [((T, D), out_dtype)], name=name, rows=T, tr=tr, acc_outs=[((1, D), F32)],
                       est=12 * tr * D * 4)
    return dx, dg


def _rms_res_norm(h, y, g_res, g_next, *, name):
    T, D = h.shape
    tr = _pick(T, (512, 256, 128))

    def body(h_ref, y_ref, g_ref, gn_ref, o_ref, n_ref):
        yv = y_ref[...]
        r = lax.rsqrt(jnp.mean(yv * yv, axis=-1, keepdims=True) + RMS_EPS)
        h2 = h_ref[...] + yv * r * g_ref[...]
        o_ref[...] = h2
        r2 = lax.rsqrt(jnp.mean(h2 * h2, axis=-1, keepdims=True) + RMS_EPS)
        n_ref[...] = (h2 * r2 * gn_ref[...]).astype(BF16)

    return _row_call(body, [(h, "row"), (y, "row"), (g_res, "full"), (g_next, "full")],
                     [((T, D), F32), ((T, D), BF16)], name=name, rows=T, tr=tr, est=12 * tr * D * 4)


def _rms_bwd_chain(x1, g1, dy1, dres, x2, g2, *, name):
    T, D = x1.shape
    tr = _pick(T, (512, 256, 128))

    def one(xv, gv, dyv):
        r = lax.rsqrt(jnp.mean(xv * xv, axis=-1, keepdims=True) + RMS_EPS)
        xhat = xv * r
        dxn = dyv * gv
        dx = r * (dxn - xhat * jnp.mean(dxn * xhat, axis=-1, keepdims=True))
        return dx, jnp.sum(dyv * xhat, axis=0, keepdims=True)

    def body(x1_ref, g1_ref, dy1_ref, dr_ref, x2_ref, g2_ref, d1_ref, d2_ref, dg1_ref, dg2_ref):
        dx1, p1 = one(x1_ref[...], g1_ref[...], dy1_ref[...].astype(F32))
        d1 = dx1 + dr_ref[...]
        d1_ref[...] = d1
        dx2, p2 = one(x2_ref[...], g2_ref[...], d1)
        d2_ref[...] = dx2.astype(BF16)

        @pl.when(pl.program_id(0) == 0)
        def _():
            dg1_ref[...] = p1
            dg2_ref[...] = p2

        @pl.when(pl.program_id(0) > 0)
        def _():
            dg1_ref[...] += p1
            dg2_ref[...] += p2

    ins = [(x1, "row"), (g1, "full"), (dy1, "row"), (dres, "row"), (x2, "row"), (g2, "full")]
    return _row_call(body, ins, [((T, D), F32), ((T, D), BF16)], name=name, rows=T, tr=tr,
                     acc_outs=[((1, D), F32), ((1, D), F32)], est=20 * tr * D * 4)


def _ffn_up(fn, w_gu, l, *, name):
    T, D = fn.shape
    H = w_gu.shape[2] // 2
    tp = _pick(T, (1024, 512, 256, 128))
    tq = _pick(H, (1408, 768, 512, 256, 128))
    nj = H // tq

    def body(a_ref, wg_ref, wu_ref, g_ref, u_ref, act_ref):
        a = a_ref[...]
        g = jnp.dot(a, wg_ref[...], preferred_element_type=F32)
        u = jnp.dot(a, wu_ref[...], preferred_element_type=F32)
        sg = jax.nn.sigmoid(g)
        silu = g * sg
        g_ref[...] = (u * (sg + silu * (1.0 - sg))).astype(BF16)
        u_ref[...] = silu.astype(BF16)
        act_ref[...] = (silu * u).astype(BF16)

    tile = pl.BlockSpec((tp, tq), lambda j, i: (i, j))
    est = 2 * (tp * D * 2 + 2 * D * tq * 2 + 3 * tp * tq * 2) + 4 * tp * tq * 4
    return pl.pallas_call(
        body, name=name,
        out_shape=[jax.ShapeDtypeStruct((T, H), BF16), jax.ShapeDtypeStruct((T, H), BF16),
                   jax.ShapeDtypeStruct((T, H), BF16)],
        grid=(nj, T // tp),
        in_specs=[pl.BlockSpec((tp, D), lambda j, i: (i, 0)),
                  pl.BlockSpec((None, D, tq), lambda j, i: (l, 0, j)),
                  pl.BlockSpec((None, D, tq), lambda j, i: (l, 0, j + nj))],
        out_specs=[tile, tile, tile],
        compiler_params=pltpu.CompilerParams(dimension_semantics=("parallel", "parallel"),
                                             vmem_limit_bytes=_vmem(est)),
    )(fn, w_gu, w_gu)


def _ffn_down_dx(df, w_down, l, g, u, *, name):
    T, D = df.shape
    H = w_down.shape[1]
    tp = _pick(T, (1024, 512, 256, 128))
    tq = _pick(H, (1408, 768, 512, 256, 128))

    def body(a_ref, w_ref, g_ref, u_ref, dg_ref, du_ref):
        da = lax.dot_general(a_ref[...], w_ref[...], (((1,), (1,)), ((), ())), preferred_element_type=F32)
        dg_ref[...] = (da * g_ref[...].astype(F32)).astype(BF16)
        du_ref[...] = (da * u_ref[...].astype(F32)).astype(BF16)

    tile = pl.BlockSpec((tp, tq), lambda j, i: (i, j))
    est = 2 * (tp * D * 2 + tq * D * 2 + 4 * tp * tq * 2) + 3 * tp * tq * 4
    return pl.pallas_call(
        body, name=name,
        out_shape=[jax.ShapeDtypeStruct((T, H), BF16), jax.ShapeDtypeStruct((T, H), BF16)],
        grid=(H // tq, T // tp),
        in_specs=[pl.BlockSpec((tp, D), lambda j, i: (i, 0)),
                  pl.BlockSpec((None, tq, D), lambda j, i: (l, j, 0)), tile, tile],
        out_specs=[tile, tile],
        compiler_params=pltpu.CompilerParams(dimension_semantics=("parallel", "parallel"),
                                             vmem_limit_bytes=_vmem(est)),
    )(df, w_down, g, u)


def _loss_and_grad(y, target, x, g, *, name):
    T, D = y.shape
    tr = _pick(T, (512, 256, 128))

    def body(y_ref, t_ref, x_ref, g_ref, dy_ref, dx_ref, l_ref, dg_ref):
        e = y_ref[...] - t_ref[...]
        dy = e * (1.0 / D)
        dy_ref[...] = dy
        part = jnp.sum(jnp.sum(e * e, axis=1, keepdims=True), axis=0, keepdims=True) * (0.5 / D)
        xv = x_ref[...]
        r = lax.rsqrt(jnp.mean(xv * xv, axis=-1, keepdims=True) + RMS_EPS)
        xhat = xv * r
        dxn = dy * g_ref[...]
        dx_ref[...] = (r * (dxn - xhat * jnp.mean(dxn * xhat, axis=-1, keepdims=True))).astype(BF16)
        dg = jnp.sum(dy * xhat, axis=0, keepdims=True)

        @pl.when(pl.program_id(0) == 0)
        def _():
            l_ref[...] = part
            dg_ref[...] = dg

        @pl.when(pl.program_id(0) > 0)
        def _():
            l_ref[...] += part
            dg_ref[...] += dg

    dy, dx, l, dg = _row_call(body, [(y, "row"), (target, "row"), (x, "row"), (g, "full")],
                              [((T, D), F32), ((T, D), BF16)], name=name, rows=T, tr=tr,
                              acc_outs=[((1, 1), F32), ((1, D), F32)], est=14 * tr * D * 4)
    return dy, dx, l, dg


_SQRT_HALF = 0.7071067811865476
_INV_SQRT_2PI = 0.3989422804014327


def _gelu_parts(x):
    cdf = 0.5 * (1.0 + lax.erf(x * _SQRT_HALF))
    return cdf


def _sgu_common(pre, lng, lnb, W):
    cdf = _gelu_parts(pre)
    z = pre * cdf
    u = z[:, :W]
    v = z[:, W:]
    mu = jnp.mean(v, axis=-1, keepdims=True)
    vc = v - mu
    var = jnp.mean(vc * vc, axis=-1, keepdims=True)
    rstd = lax.rsqrt(var + LN_EPS)
    vhat = vc * rstd
    vn = vhat * lng + lnb
    return cdf, u, vhat, rstd, vn


def _causal_mask():
    t = lax.broadcasted_iota(jnp.int32, (CHUNK, CHUNK), 0)
    s = lax.broadcasted_iota(jnp.int32, (CHUNK, CHUNK), 1)
    return t >= s


def _sgu_fwd(pre, lng, lnb, ws, bsT, *, name):
    T, W2 = pre.shape
    W = W2 // 2
    G = ws.shape[0]
    gd = W // G

    def body(pre_ref, lng_ref, lnb_ref, ws_ref, bs_ref, o_ref):
        _, u, _, _, vn = _sgu_common(pre_ref[...], lng_ref[...], lnb_ref[...], W)
        vnb = vn.astype(BF16)
        causal = _causal_mask()
        for g in range(G):
            w = jnp.where(causal, ws_ref[g], 0.0).astype(BF16)
            sv = jnp.dot(w, vnb[:, g * gd:(g + 1) * gd], preferred_element_type=F32) + bs_ref[:, g:g + 1]
            o_ref[:, g * gd:(g + 1) * gd] = (u[:, g * gd:(g + 1) * gd] * sv).astype(BF16)

    return pl.pallas_call(
        body, name=name, out_shape=jax.ShapeDtypeStruct((T, W), BF16), grid=(T // CHUNK,),
        in_specs=[pl.BlockSpec((CHUNK, W2), lambda i: (i, 0)),
                  pl.BlockSpec((1, W), lambda i: (0, 0)), pl.BlockSpec((1, W), lambda i: (0, 0)),
                  pl.BlockSpec(ws.shape, lambda i: (0, 0, 0)), pl.BlockSpec(bsT.shape, lambda i: (0, 0))],
        out_specs=pl.BlockSpec((CHUNK, W), lambda i: (i, 0)),
        compiler_params=pltpu.CompilerParams(dimension_semantics=("arbitrary",),
                                             vmem_limit_bytes=_vmem(12 * CHUNK * W2 * 4)),
    )(pre, lng, lnb, ws, bsT)


def _sgu_bwd(pre, dgated, lng, lnb, ws, bsT, *, name):
    T, W2 = pre.shape
    W = W2 // 2
    G = ws.shape[0]
    gd = W // G

    def body(pre_ref, dgt_ref, lng_ref, lnb_ref, ws_ref, bs_ref,
             dpre_ref, dws_ref, dbs_ref, dlng_ref, dlnb_ref, dbin_ref):
        first = pl.program_id(0) == 0

        @pl.when(first)
        def _():
            dws_ref[...] = jnp.zeros_like(dws_ref)
            dbs_ref[...] = jnp.zeros_like(dbs_ref)
            dlng_ref[...] = jnp.zeros_like(dlng_ref)
            dlnb_ref[...] = jnp.zeros_like(dlnb_ref)
            dbin_ref[...] = jnp.zeros_like(dbin_ref)

        pre_v = pre_ref[...]
        lng_v = lng_ref[...]
        cdf, u, vhat, rstd, vn = _sgu_common(pre_v, lng_v, lnb_ref[...], W)
        vnb = vn.astype(BF16)
        dgt = dgt_ref[...].astype(F32)
        causal = _causal_mask()
        du_parts, dvn_parts = [], []
        for g in range(G):
            sl = slice(g * gd, (g + 1) * gd)
            w = jnp.where(causal, ws_ref[g], 0.0).astype(BF16)
            sv = jnp.dot(w, vnb[:, sl], preferred_element_type=F32) + bs_ref[:, g:g + 1]
            dgt_g = dgt[:, sl]
            du_parts.append(dgt_g * sv)
            dsv = dgt_g * u[:, sl]
            dsvb = dsv.astype(BF16)
            dvn_parts.append(lax.dot_general(w, dsvb, (((0,), (0,)), ((), ())), preferred_element_type=F32))
            dw = lax.dot_general(dsvb, vnb[:, sl], (((1,), (1,)), ((), ())), preferred_element_type=F32)
            dws_ref[g] += jnp.where(causal, dw, 0.0)
            dbs_ref[:, g:g + 1] += jnp.sum(dsv, axis=1, keepdims=True)
        du = jnp.concatenate(du_parts, axis=1)
        dvn = jnp.concatenate(dvn_parts, axis=1)
        dlng_ref[...] += jnp.sum(dvn * vhat, axis=0, keepdims=True)
        dlnb_ref[...] += jnp.sum(dvn, axis=0, keepdims=True)
        dvh = dvn * lng_v
        dv = rstd * (dvh - jnp.mean(dvh, axis=-1, keepdims=True)
                     - vhat * jnp.mean(dvh * vhat, axis=-1, keepdims=True))
        dz = jnp.concatenate([du, dv], axis=1)
        dgelu = cdf + pre_v * jnp.exp(-0.5 * pre_v * pre_v) * _INV_SQRT_2PI
        dpre = dz * dgelu
        dbin_ref[...] += jnp.sum(dpre, axis=0, keepdims=True)
        dpre_ref[...] = dpre.astype(BF16)

    full = lambda shape: pl.BlockSpec(shape, lambda i, nd=len(shape): (0,) * nd)
    return pl.pallas_call(
        body, name=name,
        out_shape=[jax.ShapeDtypeStruct((T, W2), BF16), jax.ShapeDtypeStruct(ws.shape, F32),
                   jax.ShapeDtypeStruct(bsT.shape, F32), jax.ShapeDtypeStruct((1, W), F32),
                   jax.ShapeDtypeStruct((1, W), F32), jax.ShapeDtypeStruct((1, W2), F32)],
        grid=(T // CHUNK,),
        in_specs=[pl.BlockSpec((CHUNK, W2), lambda i: (i, 0)), pl.BlockSpec((CHUNK, W), lambda i: (i, 0)),
                  full((1, W)), full((1, W)), full(ws.shape), full(bsT.shape)],
        out_specs=[pl.BlockSpec((CHUNK, W2), lambda i: (i, 0)), full(ws.shape), full(bsT.shape),
                   full((1, W)), full((1, W)), full((1, W2))],
        compiler_params=pltpu.CompilerParams(dimension_semantics=("arbitrary",),
                                             vmem_limit_bytes=_vmem(24 * CHUNK * W2 * 4)),
    )(pre, dgated, lng, lnb, ws, bsT)


def _rope_tables(positions):
    half = ROPE_DIM // 2
    inv_freq = ROPE_THETA ** (-jnp.arange(0, ROPE_DIM, 2, dtype=F32) / ROPE_DIM)
    ang = positions.astype(F32).reshape(-1, 1) * inv_freq
    cos, sin = jnp.cos(ang), jnp.sin(ang)
    T = ang.shape[0]
    rest = HEAD_DIM - ROPE_DIM
    c64 = jnp.concatenate([cos, cos, jnp.ones((T, rest), F32)], axis=1)
    s64 = jnp.concatenate([-sin, sin, jnp.zeros((T, rest), F32)], axis=1)
    del half
    return jnp.tile(c64, (1, LANES // HEAD_DIM)), jnp.tile(s64, (1, LANES // HEAD_DIM))


def _swap8(x):
    W = x.shape[1]
    half = ROPE_DIM // 2
    lane = lax.broadcasted_iota(jnp.int32, x.shape, 1) % HEAD_DIM
    return jnp.where(lane < half, pltpu.roll(x, W - half, axis=1),
                     jnp.where(lane < ROPE_DIM, pltpu.roll(x, half, axis=1), 0.0))


def _wide(tab, W):
    return jnp.concatenate([tab] * (W // LANES), axis=1) if W > LANES else tab


def _rope_fwd(qkv, ctab, stab, *, q_width, kv_width, name):
    T = qkv.shape[0]
    tr = _pick(T, (256, 128))
    scale = HEAD_DIM ** -0.5

    def body(x_ref, c_ref, s_ref, q_ref, k_ref, v_ref):
        c = c_ref[...]
        s = s_ref[...]
        q = x_ref[:, :q_width]
        k = x_ref[:, q_width:q_width + kv_width]
        q_ref[...] = ((q * _wide(c, q_width) + _swap8(q) * _wide(s, q_width)) * scale).astype(BF16)
        k_ref[...] = (k * _wide(c, kv_width) + _swap8(k) * _wide(s, kv_width)).astype(BF16)
        v_ref[...] = x_ref[:, q_width + kv_width:].astype(BF16)

    return _row_call(body, [(qkv, "row"), (ctab, "row"), (stab, "row")],
                     [((T, q_width), BF16), ((T, kv_width), BF16), ((T, kv_width), BF16)],
                     name=name, rows=T, tr=tr, est=10 * tr * qkv.shape[1] * 4)


_NT = (((1,), (1,)), ((), ()))
_TN = (((0,), (0,)), ((), ()))


def _group_rows(ref, heads):
    return jnp.concatenate([ref[:, h * HEAD_DIM:(h + 1) * HEAD_DIM] for h in heads], axis=0)


def _attn_valid(grp):
    qi = np.arange(grp * CHUNK)[:, None] % CHUNK
    sj = np.arange(2 * CHUNK)[None, :]
    cur = (sj >= CHUNK) & (sj - CHUNK <= qi)
    prev = (sj < CHUNK) & (sj > qi)
    return jnp.asarray(np.stack([cur, cur | prev]).astype(np.float32))


def _valid_spec(grp):
    return pl.BlockSpec((None, grp * CHUNK, 2 * CHUNK), lambda n: (jnp.minimum(n, 1), 0, 0))


def _attn_group_probs(q, kk, sinks, valid, grp):
    rows = grp * CHUNK
    s = lax.dot_general(q, kk, _NT, preferred_element_type=F32)
    s = jnp.where(valid, s, NEG_INF)
    r = lax.broadcasted_iota(jnp.int32, (rows, 1), 0)
    sink = jnp.full((rows, 1), sinks[grp - 1], F32)
    for g in range(grp - 2, -1, -1):
        sink = jnp.where(r < (g + 1) * CHUNK, sinks[g], sink)
    m = jnp.maximum(jnp.max(s, axis=1, keepdims=True), sink)
    p = jnp.exp(s - m)
    ps = jnp.exp(sink - m)
    inv = 1.0 / (jnp.sum(p, axis=1, keepdims=True) + ps)
    return p * inv, ps * inv


def _kv_specs(width, nb):
    prev = pl.BlockSpec((CHUNK, width), lambda n: (jnp.maximum(n - 1, 0), 0))
    cur = pl.BlockSpec((CHUNK, width), lambda n: (n, 0))
    return prev, cur


def _attn_fwd(qr, kr, vr, sinks, *, name):
    T, QW = qr.shape
    KW = kr.shape[1]
    HQ, HK = QW // HEAD_DIM, KW // HEAD_DIM
    grp = HQ // HK
    nb = T // CHUNK

    def body(q_ref, kp_ref, kc_ref, vp_ref, vc_ref, s_ref, ok_ref, o_ref):
        valid = ok_ref[...] > 0.5
        for kh in range(HK):
            ks = slice(kh * HEAD_DIM, (kh + 1) * HEAD_DIM)
            heads = list(range(kh * grp, (kh + 1) * grp))
            q = _group_rows(q_ref, heads)
            kk = jnp.concatenate([kp_ref[:, ks], kc_ref[:, ks]], axis=0)
            vv = jnp.concatenate([vp_ref[:, ks], vc_ref[:, ks]], axis=0)
            p, _ = _attn_group_probs(q, kk, [s_ref[0, h] for h in heads], valid, grp)
            o = jnp.dot(p.astype(BF16), vv, preferred_element_type=F32).astype(BF16)
            for g, h in enumerate(heads):
                o_ref[:, h * HEAD_DIM:(h + 1) * HEAD_DIM] = o[g * CHUNK:(g + 1) * CHUNK]

    kp, kc = _kv_specs(KW, nb)
    return pl.pallas_call(
        body, name=name, out_shape=jax.ShapeDtypeStruct((T, QW), BF16), grid=(nb,),
        in_specs=[pl.BlockSpec((CHUNK, QW), lambda n: (n, 0)), kp, kc, kp, kc,
                  pl.BlockSpec(memory_space=pltpu.SMEM), _valid_spec(grp)],
        out_specs=pl.BlockSpec((CHUNK, QW), lambda n: (n, 0)),
        compiler_params=pltpu.CompilerParams(dimension_semantics=("arbitrary",), vmem_limit_bytes=_vmem(8 << 20)),
    )(qr, kr, kr, vr, vr, sinks, _attn_valid(grp))


def _attn_bwd(qr, kr, vr, sinks, do, *, name):
    T, QW = qr.shape
    KW = kr.shape[1]
    HQ, HK = QW // HEAD_DIM, KW // HEAD_DIM
    grp = HQ // HK
    nb = T // CHUNK

    def body(q_ref, kp_ref, kc_ref, vp_ref, vc_ref, s_ref, do_ref, ok_ref,
             dq_ref, dkp_ref, dkc_ref, dvp_ref, dvc_ref, ds_ref):
        n = pl.program_id(0)
        valid = ok_ref[...] > 0.5
        lane = lax.broadcasted_iota(jnp.int32, (1, LANES), 1)
        dsink = jnp.zeros((1, LANES), F32)
        for kh in range(HK):
            ks = slice(kh * HEAD_DIM, (kh + 1) * HEAD_DIM)
            heads = list(range(kh * grp, (kh + 1) * grp))
            q = _group_rows(q_ref, heads)
            doh = _group_rows(do_ref, heads)
            kk = jnp.concatenate([kp_ref[:, ks], kc_ref[:, ks]], axis=0)
            vv = jnp.concatenate([vp_ref[:, ks], vc_ref[:, ks]], axis=0)
            p, ps = _attn_group_probs(q, kk, [s_ref[0, h] for h in heads], valid, grp)
            dp = lax.dot_general(doh, vv, _NT, preferred_element_type=F32)
            delta = jnp.sum(p * dp, axis=1, keepdims=True)
            ds = (p * (dp - delta)).astype(BF16)
            dv = lax.dot_general(p.astype(BF16), doh, _TN, preferred_element_type=F32)
            dk = lax.dot_general(ds, q, _TN, preferred_element_type=F32)
            dq = jnp.dot(ds, kk, preferred_element_type=F32)
            psd = ps * delta
            for g, h in enumerate(heads):
                dq_ref[:, h * HEAD_DIM:(h + 1) * HEAD_DIM] = dq[g * CHUNK:(g + 1) * CHUNK]
                dsink = dsink + jnp.where(
                    lane == h, -jnp.sum(psd[g * CHUNK:(g + 1) * CHUNK], axis=0, keepdims=True), 0.0)
            dkp_ref[:, ks] = dk[:CHUNK]
            dkc_ref[:, ks] = dk[CHUNK:]
            dvp_ref[:, ks] = dv[:CHUNK]
            dvc_ref[:, ks] = dv[CHUNK:]

        @pl.when(n == 0)
        def _():
            ds_ref[...] = dsink

        @pl.when(n > 0)
        def _():
            ds_ref[...] += dsink

    kp, kc = _kv_specs(KW, nb)
    qspec = pl.BlockSpec((CHUNK, QW), lambda n: (n, 0))
    kout = pl.BlockSpec((CHUNK, KW), lambda n: (n, 0))
    return pl.pallas_call(
        body, name=name,
        out_shape=[jax.ShapeDtypeStruct((T, QW), F32)] + [jax.ShapeDtypeStruct((T, KW), F32)] * 4
        + [jax.ShapeDtypeStruct((1, LANES), F32)],
        grid=(nb,),
        in_specs=[qspec, kp, kc, kp, kc, pl.BlockSpec(memory_space=pltpu.SMEM), qspec, _valid_spec(grp)],
        out_specs=[qspec, kout, kout, kout, kout, pl.BlockSpec((1, LANES), lambda n: (0, 0))],
        compiler_params=pltpu.CompilerParams(dimension_semantics=("arbitrary",), vmem_limit_bytes=_vmem(12 << 20)),
    )(qr, kr, kr, vr, vr, sinks, do, _attn_valid(grp))


def _rope_bwd(dq, dkp, dkc, dvp, dvc, ctab, stab, *, name):
    T, QW = dq.shape
    KW = dkp.shape[1]
    nb = T // CHUNK
    scale = HEAD_DIM ** -0.5
    width = QW + 2 * KW

    def body(dq_ref, dkc_ref, dkn_ref, dvc_ref, dvn_ref, c_ref, s_ref, o_ref, db_ref):
        n = pl.program_id(0)
        c = c_ref[...]
        s = s_ref[...]
        has_next = (n < nb - 1).astype(F32)
        dqv = dq_ref[...]
        dk = dkc_ref[...] + has_next * dkn_ref[...]
        dv = dvc_ref[...] + has_next * dvn_ref[...]
        dq_pre = (dqv * _wide(c, QW) + _swap8(dqv * _wide(s, QW))) * scale
        dk_pre = dk * _wide(c, KW) + _swap8(dk * _wide(s, KW))
        o_ref[:, :QW] = dq_pre.astype(BF16)
        o_ref[:, QW:QW + KW] = dk_pre.astype(BF16)
        o_ref[:, QW + KW:] = dv.astype(BF16)
        part = jnp.concatenate([jnp.sum(dq_pre, axis=0, keepdims=True), jnp.sum(dk_pre, axis=0, keepdims=True),
                                jnp.sum(dv, axis=0, keepdims=True)], axis=1)

        @pl.when(n == 0)
        def _():
            db_ref[...] = part

        @pl.when(n > 0)
        def _():
            db_ref[...] += part

    cur = lambda w: pl.BlockSpec((CHUNK, w), lambda n: (n, 0))
    nxt = lambda w: pl.BlockSpec((CHUNK, w), lambda n: (jnp.minimum(n + 1, nb - 1), 0))
    return pl.pallas_call(
        body, name=name,
        out_shape=[jax.ShapeDtypeStruct((T, width), BF16), jax.ShapeDtypeStruct((1, width), F32)],
        grid=(nb,),
        in_specs=[cur(QW), cur(KW), nxt(KW), cur(KW), nxt(KW), cur(LANES), cur(LANES)],
        out_specs=[cur(width), pl.BlockSpec((1, width), lambda n: (0, 0))],
        compiler_params=pltpu.CompilerParams(dimension_semantics=("arbitrary",), vmem_limit_bytes=_vmem(8 << 20)),
    )(dq, dkc, dkp, dvc, dvp, ctab, stab)


def _cast_block(w, l, axis, chip_arr, *, name):
    _, Ks, Ns = w.shape
    tk = _pick(Ks, (512, 352, 256, 128))
    nk = Ks // tk
    full = (Ks * N_CHIPS, Ns) if axis == 0 else (Ks, Ns * N_CHIPS)

    def body(p_ref, w_ref, o_ref):
        o_ref[...] = w_ref[...].astype(BF16)

    if axis == 0:
        out_spec = pl.BlockSpec((tk, Ns), lambda i, p: (p[0] * nk + i, 0))
    else:
        out_spec = pl.BlockSpec((tk, Ns), lambda i, p: (i, p[0]))
    grid_spec = pltpu.PrefetchScalarGridSpec(
        num_scalar_prefetch=1, grid=(nk,),
        in_specs=[pl.BlockSpec((None, tk, Ns), lambda i, p: (l, i, 0))], out_specs=out_spec)
    return pl.pallas_call(
        body, name=name, out_shape=jax.ShapeDtypeStruct(full, BF16), grid_spec=grid_spec,
        compiler_params=pltpu.CompilerParams(dimension_semantics=("arbitrary",),
                                             vmem_limit_bytes=_vmem(4 * tk * Ns * 6)),
    )(chip_arr, w)


def _adamw_math(w, g, m, v):
    m = ADAM_B1 * m + (1.0 - ADAM_B1) * g
    v = ADAM_B2 * v + (1.0 - ADAM_B2) * (g * g)
    m_hat = m / (1.0 - ADAM_B1 ** ADAM_STEP)
    v_hat = v / (1.0 - ADAM_B2 ** ADAM_STEP)
    delta = -ADAM_LR * (m_hat / (jnp.sqrt(v_hat) + ADAM_EPS) + ADAM_WD * w)
    return delta, m, v


def _adamw_layer(w, m, v, g, l, outs, *, name):
    _, K, N = w.shape
    tk = _pick(K, (256, 176, 128))

    def body(w_ref, m_ref, v_ref, g_ref, _g, _d, _m, _v, go_ref, d_ref, mo_ref, vo_ref):
        gv = g_ref[...]
        d, mn, vn = _adamw_math(w_ref[...], gv, m_ref[...], v_ref[...])
        go_ref[...] = gv
        d_ref[...] = d
        mo_ref[...] = mn
        vo_ref[...] = vn

    layer = pl.BlockSpec((None, tk, N), lambda i: (l, i, 0))
    any_spec = pl.BlockSpec(memory_space=pl.ANY)
    sd = jax.ShapeDtypeStruct(w.shape, F32)
    return pl.pallas_call(
        body, name=name, out_shape=[sd, sd, sd, sd], grid=(K // tk,),
        in_specs=[layer, layer, layer, pl.BlockSpec((tk, N), lambda i: (i, 0))] + [any_spec] * 4,
        out_specs=[layer] * 4, input_output_aliases={4: 0, 5: 1, 6: 2, 7: 3},
        compiler_params=pltpu.CompilerParams(dimension_semantics=("arbitrary",),
                                             vmem_limit_bytes=_vmem(2 * 8 * tk * N * 4 + 6 * tk * N * 4)),
    )(w, m, v, g, *outs)


def _adamw_small(w, g, m, v, *, name):
    def body(w_ref, g_ref, m_ref, v_ref, d_ref, mo_ref, vo_ref):
        d, mn, vn = _adamw_math(w_ref[...], g_ref[...], m_ref[...], v_ref[...])
        d_ref[...] = d
        mo_ref[...] = mn
        vo_ref[...] = vn

    sd = jax.ShapeDtypeStruct(w.shape, F32)
    return pl.pallas_call(body, name=name, out_shape=[sd, sd, sd])(w, g, m, v)


def _my_place():
    return lax.axis_index("x"), lax.axis_index("y"), lax.axis_index("c")


def _peer_chips(x, y):
    return [(1 - x, y), (x, 1 - y), (1 - x, 1 - y)]


_HBM = pl.BlockSpec(memory_space=pltpu.HBM)
_SEM = pl.BlockSpec(memory_space=pltpu.SEMAPHORE)
_EFFECT = pltpu.SideEffectType.DATAFLOW_SIDE_EFFECTING


def _split_start(name, bufs, n_copies, make_copies, after):
    nb = len(bufs)

    def body(*refs):
        send_sems, recv_sems = refs[nb + 1], refs[nb + 2]
        token = refs[2 * nb + 3]
        sends, _ = make_copies(refs[:nb], send_sems, recv_sems)
        for cp in sends:
            cp.start()
        token[...] = jnp.zeros_like(token)

    res = pl.pallas_call(
        body, name=name,
        out_shape=(pltpu.SemaphoreType.DMA((n_copies,)), pltpu.SemaphoreType.DMA((n_copies,)),
                   *[pltpu.HBM(b.shape, b.dtype) for b in bufs], jax.ShapeDtypeStruct((8, LANES), F32)),
        in_specs=[_HBM] * nb + [pl.BlockSpec(memory_space=pl.ANY)],
        out_specs=(_SEM, _SEM, *[_HBM] * nb, pl.BlockSpec(memory_space=pltpu.VMEM)),
        input_output_aliases={k: 2 + k for k in range(nb)},
        compiler_params=pltpu.CompilerParams(has_side_effects=_EFFECT),
    )(*[pltpu.with_memory_space_constraint(b, pltpu.HBM) for b in bufs], after)
    return res[0], res[1], list(res[2:2 + nb]), res[2 + nb]


def _split_wait(name, bufs, sems, make_copies, after):
    nb = len(bufs)

    def body(*refs):
        send_sems, recv_sems = refs[nb], refs[nb + 1]
        sends, recvs = make_copies(refs[:nb], send_sems, recv_sems)
        for cp in sends:
            cp.wait_send()
        for cp in recvs:
            cp.wait_recv()

    res = pl.pallas_call(
        body, name=name,
        out_shape=tuple(pltpu.HBM(b.shape, b.dtype) for b in bufs),
        in_specs=[_HBM] * nb + [_SEM, _SEM, pl.BlockSpec(memory_space=pl.ANY)],
        out_specs=tuple([_HBM] * nb),
        input_output_aliases={k: k for k in range(nb)},
        compiler_params=pltpu.CompilerParams(has_side_effects=_EFFECT),
    )(*bufs, sems[0], sems[1], after)
    return list(res)


def _remote(src, dst, send_sems, recv_sems, k, target):
    return pltpu.make_async_remote_copy(src_ref=src, dst_ref=dst, send_sem=send_sems.at[k],
                                        recv_sem=recv_sems.at[k], device_id=target, device_id_type=MESH)


def _ag_region(ref, axis, chip, half):
    K, N = ref.shape
    if axis == 0:
        hs = K // N_CHIPS // 2
        assert hs % 16 == 0
        return ref.at[pl.ds(pl.multiple_of((2 * chip + half) * hs, 16), hs), :]
    ns, hk = N // N_CHIPS, K // 2
    assert ns % LANES == 0 and hk % 16 == 0
    return ref.at[pl.ds(pl.multiple_of(half * hk, 16), hk), pl.ds(pl.multiple_of(chip * ns, LANES), ns)]


def _ag_copies(stage, axes):
    n = len(axes)

    def make(bufs, send_sems, recv_sems):
        x, y, c = _my_place()
        me = 2 * x + y
        sends, recvs = [], []
        for j, (px, py) in enumerate(_peer_chips(x, y)):
            other = 2 * px + py
            for w in range(n):
                k = j * n + w
                if stage == 1:
                    src, target = _ag_region(bufs[w], axes[w], me, c), (px, py, c)
                    land = _ag_region(bufs[w], axes[w], other, c)
                else:
                    src, target = _ag_region(bufs[w], axes[w], other, c), (x, y, 1 - c)
                    land = _ag_region(bufs[w], axes[w], other, 1 - c)
                sends.append(_remote(src, src, send_sems, recv_sems, k, target))
                recvs.append(_remote(land, land, send_sems, recv_sems, k, target))
        return sends, recvs

    return make


def _half_shape(shape, axis):
    K, N = shape
    return (K, N // 2) if axis == 0 else (K // 2, N)


def _core_half(ref, axis, half):
    K, N = ref.shape
    if axis == 0:
        return ref.at[:, pl.ds(pl.multiple_of(half * (N // 2), LANES), N // 2)]
    return ref.at[pl.ds(pl.multiple_of(half * (K // 2), 16), K // 2), :]


def _chip_block(ref, axis, chip):
    K, N = ref.shape
    if axis == 0:
        return ref.at[pl.ds(pl.multiple_of(chip * (K // N_CHIPS), 16), K // N_CHIPS), :]
    return ref.at[:, pl.ds(pl.multiple_of(chip * (N // N_CHIPS), LANES), N // N_CHIPS)]


def _rs_sibling_copies(axes):
    n = len(axes)

    def make(bufs, send_sems, recv_sems):
        x, y, c = _my_place()
        sends = [_remote(_core_half(bufs[w], axes[w], 1 - c), bufs[n + w], send_sems, recv_sems, w, (x, y, 1 - c))
                 for w in range(n)]
        recvs = [_remote(bufs[n + w], bufs[n + w], send_sems, recv_sems, w, (x, y, 1 - c)) for w in range(n)]
        return sends, recvs

    return make


def _rs_chip_copies(axes):
    n = len(axes)

    def make(bufs, send_sems, recv_sems):
        x, y, c = _my_place()
        sends, recvs = [], []
        for j, (px, py) in enumerate(_peer_chips(x, y)):
            for w in range(n):
                k = j * n + w
                sends.append(_remote(_chip_block(bufs[w], axes[w], 2 * px + py), bufs[n + w].at[j],
                                     send_sems, recv_sems, k, (px, py, c)))
                recvs.append(_remote(bufs[n + w].at[j], bufs[n + w].at[j], send_sems, recv_sems, k, (px, py, c)))
        return sends, recvs

    return make


def _rs_fill_copies(axes):
    n = len(axes)

    def make(bufs, send_sems, recv_sems):
        x, y, c = _my_place()
        sends = [_remote(_core_half(bufs[w], axes[w], c), _core_half(bufs[w], axes[w], c),
                         send_sems, recv_sems, w, (x, y, 1 - c)) for w in range(n)]
        recvs = [_remote(_core_half(bufs[w], axes[w], 1 - c), _core_half(bufs[w], axes[w], 1 - c),
                         send_sems, recv_sems, w, (x, y, 1 - c)) for w in range(n)]
        return sends, recvs

    return make


def _chip_sum(g, r, axis, place, *, name):
    hk, hn = r.shape
    bk, bn = (hk // N_CHIPS, hn) if axis == 0 else (hk, hn // N_CHIPS)
    tk = _pick(bk, (512, 352, 256, 128))
    nk = bk // tk

    def body(p_ref, g_ref, r_ref, b_ref, own_ref):
        s = g_ref[...].astype(F32) + r_ref[...].astype(F32)
        b_ref[...] = s.astype(BF16)

        @pl.when(pl.program_id(1) == p_ref[0])
        def _():
            own_ref[...] = s

    if axis == 0:
        g_spec = pl.BlockSpec((tk, bn), lambda i, j, p: (j * nk + i, p[1]))
        r_spec = pl.BlockSpec((tk, bn), lambda i, j, p: (j * nk + i, 0))
    else:
        g_spec = pl.BlockSpec((tk, bn), lambda i, j, p: (p[1] * nk + i, j))
        r_spec = pl.BlockSpec((tk, bn), lambda i, j, p: (i, j))
    grid_spec = pltpu.PrefetchScalarGridSpec(
        num_scalar_prefetch=1, grid=(nk, N_CHIPS), in_specs=[g_spec, r_spec],
        out_specs=[r_spec, pl.BlockSpec((tk, bn), lambda i, j, p: (i, 0))])
    return pl.pallas_call(
        body, name=name,
        out_shape=[jax.ShapeDtypeStruct(r.shape, BF16), jax.ShapeDtypeStruct((bk, bn), F32)],
        grid_spec=grid_spec,
        compiler_params=pltpu.CompilerParams(dimension_semantics=("arbitrary", "arbitrary"),
                                             vmem_limit_bytes=_vmem(2 * tk * bn * 10 + 3 * tk * bn * 4)),
    )(place, g, r)


def _final_sum(own, recv, axis, place, *, name):
    _, bk, bn = recv.shape
    tk = _pick(bk, (256, 176, 128))
    nk = bk // tk

    def body(p_ref, o_ref, r_ref, out_ref):
        out_ref[...] = ((o_ref[...] + r_ref[0].astype(F32)) + r_ref[1].astype(F32)) + r_ref[2].astype(F32)

    own_spec = pl.BlockSpec((tk, bn), lambda i, p: (i, 0))
    if axis == 0:
        out_shape, out_spec = (bk, 2 * bn), pl.BlockSpec((tk, bn), lambda i, p: (i, p[1]))
    else:
        out_shape, out_spec = (2 * bk, bn), pl.BlockSpec((tk, bn), lambda i, p: (p[1] * nk + i, 0))
    grid_spec = pltpu.PrefetchScalarGridSpec(
        num_scalar_prefetch=1, grid=(nk,),
        in_specs=[own_spec, pl.BlockSpec((3, tk, bn), lambda i, p: (0, i, 0))], out_specs=out_spec)
    return pl.pallas_call(
        body, name=name, out_shape=jax.ShapeDtypeStruct(out_shape, F32), grid_spec=grid_spec,
        compiler_params=pltpu.CompilerParams(dimension_semantics=("arbitrary",),
                                             vmem_limit_bytes=_vmem(2 * tk * bn * 14 + 4 * tk * bn * 4)),
    )(place, own, recv)


def _allreduce_small(p):
    def body(p_ref, o_ref, r0, r1, r2, send_sems, recv_sems):
        x, y, c = _my_place()
        o_ref[...] = p_ref[...]
        for s, (peer, rbuf) in enumerate([((x, y, 1 - c), r0), ((1 - x, y, c), r1), ((x, 1 - y, c), r2)]):
            cp = pltpu.make_async_remote_copy(src_ref=o_ref, dst_ref=rbuf, send_sem=send_sems.at[s],
                                              recv_sem=recv_sems.at[s], device_id=peer, device_id_type=MESH)
            cp.start()
            cp.wait()
            o_ref[...] = o_ref[...] + rbuf[...]

    vm = pl.BlockSpec(memory_space=pltpu.VMEM)
    return pl.pallas_call(
        body, name="allreduce_small", out_shape=jax.ShapeDtypeStruct(p.shape, F32),
        in_specs=[vm], out_specs=vm,
        scratch_shapes=[pltpu.VMEM(p.shape, F32)] * 3 + [pltpu.SemaphoreType.DMA((3,))] * 2,
        compiler_params=pltpu.CompilerParams(vmem_limit_bytes=_vmem(6 * _nbytes(p.shape, F32))),
    )(p)


def _pack_rows(parts):
    rows, metas = [], []
    for a in parts:
        flat = a.reshape(-1)
        nrow = -(-flat.shape[0] // LANES)
        nrow = -(-nrow // 8) * 8
        flat = jnp.pad(flat, (0, nrow * LANES - flat.shape[0]))
        rows.append(flat.reshape(nrow, LANES))
        metas.append((a.shape, nrow))
    return jnp.concatenate(rows, axis=0), metas


def _unpack_rows(packed, metas):
    out, r0 = [], 0
    for shape, nrow in metas:
        size = int(np.prod(shape))
        out.append(packed[r0:r0 + nrow].reshape(-1)[:size].reshape(shape))
        r0 += nrow
    return out


def kernel(x, positions, pre_mix_g, post_mix_g, pre_ffn_g, post_ffn_g, a_w_in, a_b_in, a_ln_g, a_ln_b, a_w_s, a_b_s, a_w_out, b_w_qkv, b_b_qkv, b_sinks, b_w_o, ffn_w_gu, ffn_w_down, loss_target, m_pre_mix_g, m_post_mix_g, m_pre_ffn_g, m_post_ffn_g, m_a_w_in, m_a_b_in, m_a_ln_g, m_a_ln_b, m_a_w_s, m_a_b_s, m_a_w_out, m_b_w_qkv, m_b_b_qkv, m_b_sinks, m_b_w_o, m_ffn_w_gu, m_ffn_w_down, v_pre_mix_g, v_post_mix_g, v_pre_ffn_g, v_post_ffn_g, v_a_w_in, v_a_b_in, v_a_ln_g, v_a_ln_b, v_a_w_s, v_a_b_s, v_a_w_out, v_b_w_qkv, v_b_b_qkv, v_b_sinks, v_b_w_o, v_ffn_w_gu, v_ffn_w_down):
    depth, D = pre_mix_g.shape
    xi, yi, ci = _my_place()
    chip = 2 * xi + yi
    place = jnp.stack([chip, ci]).astype(jnp.int32)

    stacked = {"a_w_in": (a_w_in, m_a_w_in, v_a_w_in), "a_w_out": (a_w_out, m_a_w_out, v_a_w_out),
               "b_w_qkv": (b_w_qkv, m_b_w_qkv, v_b_w_qkv), "b_w_o": (b_w_o, m_b_w_o, v_b_w_o),
               "ffn_w_gu": (ffn_w_gu, m_ffn_w_gu, v_ffn_w_gu), "ffn_w_down": (ffn_w_down, m_ffn_w_down, v_ffn_w_down)}
    cut = {"a_w_in": 1, "a_w_out": 0, "b_w_qkv": 1, "b_w_o": 0, "ffn_w_gu": 1, "ffn_w_down": 0}

    def layer_keys(i):
        mix = [("a_w_in", i // 2), ("a_w_out", i // 2)] if i % 2 == 0 else [("b_w_qkv", i // 2), ("b_w_o", i // 2)]
        return mix + [("ffn_w_gu", i), ("ffn_w_down", i)]

    def dep(a, toks):
        for t in toks:
            a = a + t[:1, :1]
        return a

    W = {}
    for i in range(depth):
        for nm, l in layer_keys(i):
            W[(nm, l)] = _cast_block(stacked[nm][0], l, cut[nm], place, name=f"cast_{nm}_{l}")

    def gather(tag, keys, after):
        axes = [cut[nm] for nm, _ in keys]
        for stage in (1, 2):
            ss, rs, bufs, tok = _split_start(f"ag{stage}_start_{tag}", [W[k] for k in keys], 3 * len(keys),
                                             _ag_copies(stage, axes), after)
            after = yield tok
            bufs = _split_wait(f"ag{stage}_wait_{tag}", bufs, (ss, rs), _ag_copies(stage, axes), after)
            W.update(zip(keys, bufs))
        yield None

    nq = b_b_qkv.shape[1]
    bq_full = jnp.zeros((b_b_qkv.shape[0], N_CHIPS * nq), F32)
    bq_full = lax.dynamic_update_slice(bq_full, jnp.where(ci == 0, b_b_qkv, 0.0), (0, chip * nq))
    bq_packed, bq_meta = _pack_rows([bq_full])
    bq_gathered = _allreduce_small(bq_packed)
    b_qkv_full = _unpack_rows(bq_gathered, bq_meta)[0]

    first = gather("0m", layer_keys(0)[:2], bq_gathered)
    tok = next(first)
    tok = first.send(tok)
    first.send(tok)

    h = x[0]
    target = loss_target[0]
    ctab, stab = _rope_tables(positions[0])
    q_width = W[("b_w_o", 0)].shape[0]
    kv_width = N_KV_HEADS * HEAD_DIM
    row = lambda a, i: a[i:i + 1]

    saved = []
    hn = None
    for i in range(depth):
        j = i // 2
        s = {"h": h}
        ffn_w = None
        if i == 0:
            ffn_w = gather("0f", layer_keys(0)[2:], W[("a_w_out", 0)])
            toks = [next(ffn_w)]
            nxt = gather("1", layer_keys(1), toks[0])
            toks.append(next(nxt))
            hn = _rms_fwd(h, dep(row(pre_mix_g, i), toks), out_dtype=BF16, name=f"rms_pre_mix_{i}")
        elif i + 1 < depth:
            nxt = gather(str(i + 1), layer_keys(i + 1), h)
            toks = [next(nxt)]
        else:
            toks = []
        s["hn"] = hn
        if i % 2 == 0:
            pre = _matmul(hn, W[("a_w_in", j)], mode="nn", bias=dep(row(a_b_in, j), toks), out_dtype=F32,
                          name=f"gmlp_in_{i}")
            gated = _sgu_fwd(pre, row(a_ln_g, j), row(a_ln_b, j), a_w_s[j], a_b_s[j].T, name=f"sgu_fwd_{i}")
            mix = _matmul(gated, W[("a_w_out", j)], mode="nn", out_dtype=F32, name=f"gmlp_out_{i}")
            s.update(pre=pre, gated=gated)
        else:
            qkv = _matmul(hn, W[("b_w_qkv", j)], mode="nn", bias=dep(row(b_qkv_full, j), toks), out_dtype=F32,
                          name=f"attn_qkv_{i}")
            qr, kr, vr = _rope_fwd(qkv, ctab, stab, q_width=q_width, kv_width=kv_width, name=f"rope_fwd_{i}")
            o = _attn_fwd(qr, kr, vr, row(b_sinks, j), name=f"attn_fwd_{i}")
            mix = _matmul(o, W[("b_w_o", j)], mode="nn", out_dtype=F32, name=f"attn_o_{i}")
            s.update(qr=qr, kr=kr, vr=vr, o=o)
        s["mix"] = mix
        toks = [ffn_w.send(mix)] if ffn_w else []
        h1, fn = _rms_res_norm(h, mix, dep(row(post_mix_g, i), toks), row(pre_ffn_g, i), name=f"rms_post_mix_{i}")
        if ffn_w:
            ffn_w.send(h1)
        s["h1"] = h1
        g_pre, u_pre, act = _ffn_up(fn, W[("ffn_w_gu", i)][None], 0, name=f"ffn_up_{i}")
        f = _matmul(act, W[("ffn_w_down", i)], mode="nn", out_dtype=F32, name=f"ffn_down_{i}")
        if i + 1 < depth:
            toks = [nxt.send(f)]
            h, hn = _rms_res_norm(h1, f, dep(row(post_ffn_g, i), toks), row(pre_mix_g, i + 1),
                                  name=f"rms_post_ffn_{i}")
            nxt.send(h)
        else:
            h = _rms_res(h1, f, row(post_ffn_g, i), name=f"rms_post_ffn_{i}")
        s.update(fn=fn, g_pre=g_pre, u_pre=u_pre, act=act, f=f)
        saved.append(s)

    dh, df, loss_part, g_last = _loss_and_grad(h, target, saved[-1]["f"], row(post_ffn_g, depth - 1), name="loss")
    loss = lax.psum(loss_part[0, 0], ("x", "y", "c"))

    big_out = {nm: tuple(lax.empty(w.shape, F32) for _ in range(4)) for nm, (w, _, _) in stacked.items()}

    def reduce_group(i, keys, grads):
        axes = [cut[nm] for nm, _ in keys]
        n = len(keys)
        lands = [lax.empty(_half_shape(g.shape, ax), BF16) for g, ax in zip(grads, axes)]
        ss, rs, bufs, tok = _split_start(f"rs_sibling_start_{i}", list(grads) + lands, n, _rs_sibling_copies(axes),
                                         place)
        after = yield tok
        bufs = _split_wait(f"rs_sibling_wait_{i}", bufs, (ss, rs), _rs_sibling_copies(axes), after)
        sums = [_chip_sum(bufs[w], bufs[n + w], axes[w], place, name=f"chip_sum_{keys[w][0]}_{keys[w][1]}")
                for w in range(n)]
        lands = [lax.empty((3,) + own.shape, BF16) for _, own in sums]
        ss, rs, bufs, tok = _split_start(f"rs_chip_start_{i}", [sb for sb, _ in sums] + lands, 3 * n,
                                         _rs_chip_copies(axes), place)
        after = yield tok
        bufs = _split_wait(f"rs_chip_wait_{i}", bufs, (ss, rs), _rs_chip_copies(axes), after)
        blocks = [_final_sum(sums[w][1], bufs[n + w], axes[w], place, name=f"final_sum_{keys[w][0]}_{keys[w][1]}")
                  for w in range(n)]
        ss, rs, bufs, tok = _split_start(f"rs_fill_start_{i}", blocks, n, _rs_fill_copies(axes), place)
        after = yield tok
        blocks = _split_wait(f"rs_fill_wait_{i}", bufs, (ss, rs), _rs_fill_copies(axes), after)
        for (nm, l), g in zip(keys, blocks):
            w, m, v = stacked[nm]
            big_out[nm] = tuple(_adamw_layer(w, m, v, g, l, big_out[nm], name=f"adamw_{nm}_{l}"))
        yield None

    reducing = []

    def advance(after):
        toks = []
        for gen in list(reducing):
            tok = gen.send(after)
            if tok is None:
                reducing.remove(gen)
            else:
                toks.append(tok)
        return toks

    small = {}
    g_pre_mix, g_post_mix, g_pre_ffn, g_post_ffn = [None] * depth, [None] * depth, [None] * depth, [None] * depth
    g_post_ffn[depth - 1] = g_last
    toks = []
    for i in reversed(range(depth)):
        j = i // 2
        s = saved[i]
        g_down = _matmul(s["act"], df, mode="tn", out_dtype=BF16, after=toks, name=f"ffn_down_dw_{i}")
        dg_, du_ = _ffn_down_dx(df, W[("ffn_w_down", i)][None], 0, s["g_pre"], s["u_pre"], name=f"ffn_down_dx_{i}")
        hid = dg_.shape[1]
        tile = _pick(hid, (1408, 768, 512, 256, 128))
        w_gu = W[("ffn_w_gu", i)]
        g_gu = lax.empty(w_gu.shape, BF16)
        g_gu = _matmul(s["fn"], dg_, mode="tn", into=g_gu, tq=tile, out_dtype=BF16, name=f"ffn_g_dw_{i}")
        g_gu = _matmul(s["fn"], du_, mode="tn", into=g_gu, tq=tile, q_off=hid // tile, out_dtype=BF16,
                       name=f"ffn_u_dw_{i}")
        dfn_g = _matmul(dg_, w_gu, mode="nt", tr=hid, out_dtype=F32, name=f"ffn_g_dx_{i}")
        dfn = _matmul(du_, w_gu, mode="nt", tr=hid, b_r_off=1, bias=dfn_g, out_dtype=F32, name=f"ffn_u_dx_{i}")
        toks = advance(dfn)
        if i == 0:
            gen = reduce_group("0f", layer_keys(0)[2:], [g_gu, g_down])
            toks.append(next(gen))
            reducing.append(gen)
        dh1, dmix, g_pre_ffn[i], g_post_mix[i] = _rms_bwd_chain(
            s["h1"], dep(row(pre_ffn_g, i), toks), dfn, dh, s["mix"], row(post_mix_g, i), name=f"rms_ffn_mix_bwd_{i}")
        if i % 2 == 0:
            g_out = _matmul(s["gated"], dmix, mode="tn", out_dtype=BF16, name=f"gmlp_out_dw_{i}")
            dgated = _matmul(dmix, W[("a_w_out", j)], mode="nt", out_dtype=F32, name=f"gmlp_out_dx_{i}")
            toks = advance(dgated) if i == 0 else []
            dpre, dws, dbsT, dlng, dlnb, dbin = _sgu_bwd(s["pre"], dgated, dep(row(a_ln_g, j), toks), row(a_ln_b, j),
                                                         a_w_s[j], a_b_s[j].T, name=f"sgu_bwd_{i}")
            small[("a_w_s", j)] = dws
            small[("a_b_s", j)] = dbsT.T
            small[("a_ln_g", j)] = dlng
            small[("a_ln_b", j)] = dlnb
            small[("a_b_in", j)] = dbin
            g_in = _matmul(s["hn"], dpre, mode="tn", out_dtype=BF16, name=f"gmlp_in_dw_{i}")
            dhn = _matmul(dpre, W[("a_w_in", j)], mode="nt", out_dtype=F32, name=f"gmlp_in_dx_{i}")
        else:
            g_out = _matmul(s["o"], dmix, mode="tn", out_dtype=BF16, name=f"attn_o_dw_{i}")
            do = _matmul(dmix, W[("b_w_o", j)], mode="nt", out_dtype=BF16, name=f"attn_o_dx_{i}")
            dq, dkp, dkc, dvp, dvc, dsk = _attn_bwd(s["qr"], s["kr"], s["vr"], row(b_sinks, j), do,
                                                    name=f"attn_bwd_{i}")
            dqkv, dbq = _rope_bwd(dq, dkp, dkc, dvp, dvc, ctab, stab, name=f"rope_bwd_{i}")
            small[("b_sinks", j)] = dsk[:, :b_sinks.shape[1]]
            small[("b_b_qkv", j)] = dbq
            g_in = _matmul(s["hn"], dqkv, mode="tn", out_dtype=BF16, name=f"attn_qkv_dw_{i}")
            dhn = _matmul(dqkv, W[("b_w_qkv", j)], mode="nt", out_dtype=F32, name=f"attn_qkv_dx_{i}")
        toks = advance(dhn)
        if i > 0:
            dh, df, g_pre_mix[i], g_post_ffn[i - 1] = _rms_bwd_chain(
                s["h"], dep(row(pre_mix_g, i), toks), dhn, dh1, saved[i - 1]["f"], row(post_ffn_g, i - 1),
                name=f"rms_mix_ffn_bwd_{i}")
        else:
            dh, g_pre_mix[i] = _rms_bwd(s["h"], dep(row(pre_mix_g, i), toks), dhn, dh1, out_dtype=F32,
                                        name=f"rms_pre_mix_bwd_{i}")
        if i == 0:
            gen = reduce_group("0m", layer_keys(0)[:2], [g_in, g_out])
        else:
            gen = reduce_group(str(i), layer_keys(i), [g_in, g_out, g_gu, g_down])
        toks = [next(gen)] + advance(dh)
        reducing.append(gen)
    grad_x = dh[None]

    toks = advance(dh)
    n_a, n_b = a_b_in.shape[0], b_sinks.shape[0]
    stack = lambda key, n: jnp.concatenate([small[(key, j)] for j in range(n)], axis=0)
    small_parts = [
        jnp.concatenate(g_pre_mix, axis=0), jnp.concatenate(g_post_mix, axis=0),
        jnp.concatenate(g_pre_ffn, axis=0), jnp.concatenate(g_post_ffn, axis=0),
        stack("a_b_in", n_a), stack("a_ln_g", n_a), stack("a_ln_b", n_a),
        jnp.stack([small[("a_w_s", j)] for j in range(n_a)]), jnp.stack([small[("a_b_s", j)] for j in range(n_a)]),
        stack("b_b_qkv", n_b), stack("b_sinks", n_b),
    ]
    packed, metas = _pack_rows(small_parts)
    reduced = _allreduce_small(dep(packed, toks))
    while reducing:
        advance(reduced)
    red = _unpack_rows(reduced, metas)
    (gr_pre_mix, gr_post_mix, gr_pre_ffn, gr_post_ffn, gr_b_in, gr_ln_g, gr_ln_b, gr_w_s, gr_b_s,
     gr_b_qkv_full, gr_sinks) = red
    gr_b_qkv = lax.dynamic_slice(gr_b_qkv_full, (0, chip * nq), (gr_b_qkv_full.shape[0], nq))

    grads = {"pre_mix_g": gr_pre_mix, "post_mix_g": gr_post_mix, "pre_ffn_g": gr_pre_ffn, "post_ffn_g": gr_post_ffn,
             "a_b_in": gr_b_in, "a_ln_g": gr_ln_g, "a_ln_b": gr_ln_b, "a_w_s": gr_w_s, "a_b_s": gr_b_s,
             "b_b_qkv": gr_b_qkv, "b_sinks": gr_sinks}
    weights = {"pre_mix_g": (pre_mix_g, m_pre_mix_g, v_pre_mix_g), "post_mix_g": (post_mix_g, m_post_mix_g, v_post_mix_g),
               "pre_ffn_g": (pre_ffn_g, m_pre_ffn_g, v_pre_ffn_g), "post_ffn_g": (post_ffn_g, m_post_ffn_g, v_post_ffn_g),
               "a_b_in": (a_b_in, m_a_b_in, v_a_b_in), "a_ln_g": (a_ln_g, m_a_ln_g, v_a_ln_g),
               "a_ln_b": (a_ln_b, m_a_ln_b, v_a_ln_b), "a_w_s": (a_w_s, m_a_w_s, v_a_w_s), "a_b_s": (a_b_s, m_a_b_s, v_a_b_s),
               "b_b_qkv": (b_b_qkv, m_b_b_qkv, v_b_b_qkv), "b_sinks": (b_sinks, m_b_sinks, v_b_sinks)}
    order = ["pre_mix_g", "post_mix_g", "pre_ffn_g", "post_ffn_g", "a_w_in", "a_b_in", "a_ln_g", "a_ln_b", "a_w_s",
             "a_b_s", "a_w_out", "b_w_qkv", "b_b_qkv", "b_sinks", "b_w_o", "ffn_w_gu", "ffn_w_down"]
    deltas, new_m, new_v = {}, {}, {}
    for nm in order:
        if nm in big_out:
            grads[nm], deltas[nm], new_m[nm], new_v[nm] = big_out[nm]
        else:
            w, m, v = weights[nm]
            deltas[nm], new_m[nm], new_v[nm] = _adamw_small(w, grads[nm], m, v, name="adamw_" + nm)
    return (loss, grad_x, *[grads[nm] for nm in order], *[deltas[nm] for nm in order],
            *[new_m[nm] for nm in order], *[new_v[nm] for nm in order])
```

```python
import functools
import math

import jax
import jax.numpy as jnp
import numpy as np
from jax import lax
from jax.experimental import pallas as pl
from jax.experimental.pallas import tpu as pltpu

F32 = jnp.float32
BF16 = jnp.bfloat16
MESH = pl.DeviceIdType.MESH

HEAD_DIM = 64
N_KV_HEADS = 4
ROPE_DIM = 16
ROPE_THETA = 500000.0
CHUNK = 128
GMLP_GROUPS = 8
RMS_EPS = 1e-6
LN_EPS = 1e-5
NEG_INF = -1e30
ADAM_LR = 0.001
ADAM_B1 = 0.9
ADAM_B2 = 0.999
ADAM_EPS = 1e-08
ADAM_WD = 0.01
ADAM_STEP = 10

N_CHIPS = 4
LANES = 128
VMEM_CAP = 58 * 1024 * 1024


def _vmem(est_bytes):
    assert est_bytes < VMEM_CAP
    return VMEM_CAP


def _pick(n, cands):
    for c in cands:
        if c <= n and n % c == 0:
            return c
    return n


def _nbytes(shape, dtype):
    return int(np.prod(shape)) * jnp.dtype(dtype).itemsize


MATMUL_VMEM_BUDGET = 48 * 1024 * 1024


def _halvings(n, unit):
    out, t = [], n
    while t % unit == 0 and t >= unit:
        out.append(t)
        if t % 2:
            break
        t //= 2
    return out


def _matmul_tiles(P, Q, R, a_bytes, b_bytes, o_bytes, full_addend, tp, tq, tr):
    step_us, bytes_per_us = 0.85, 3.2e6
    best = None
    for p in ([tp] if tp else _halvings(P, LANES)):
        for q in ([tq] if tq else _halvings(Q, LANES)):
            for r in ([tr] if tr else _halvings(R, LANES)):
                nk = R // r
                vm = 2 * (p * r * a_bytes + r * q * b_bytes + p * q * o_bytes + (p * q * 4 if full_addend else 0))
                vm += p * q * 4 * (2 if nk > 1 else 1)
                if vm > MATMUL_VMEM_BUDGET:
                    continue
                exposed = (p * r * a_bytes + r * q * b_bytes + p * q * o_bytes) / bytes_per_us
                key = ((P // p) * (Q // q) * nk * step_us + exposed, nk, abs(p - q))
                if best is None or key < best[0]:
                    best = (key, (p, q, r))
    assert best is not None, (P, Q, R)
    return best[1]


def _matmul(a, b, *, mode, out_dtype, name, a_l=None, b_l=None, bias=None, into=None, o_l=None,
            q_off=0, b_r_off=0, tp=None, tq=None, tr=None, after=()):
    a2 = a.shape[-2:]
    b2 = b.shape[-2:]
    if mode == "nn":
        (P, R), (R2, Q) = a2, b2
    elif mode == "nt":
        (P, R), (Q, R2) = a2, b2
    else:
        (R, P), (R2, Q) = a2, b2
    assert R == R2 or (mode == "nt" and R2 % R == 0), (mode, a.shape, b.shape)
    o_bytes = jnp.dtype(into.dtype if into is not None else out_dtype).itemsize
    full_addend = bias is not None and bias.shape[0] != 1
    tp, tq, tr = _matmul_tiles(P, Q, R, a.dtype.itemsize, b.dtype.itemsize, o_bytes, full_addend, tp, tq, tr)
    assert P % tp == 0 and Q % tq == 0 and R % tr == 0
    nk = R // tr
    dims = {"nn": (((1,), (0,)), ((), ())), "nt": (((1,), (1,)), ((), ())), "tn": (((0,), (0,)), ((), ()))}[mode]

    def lead(l, blk, idx):
        if l is None:
            return pl.BlockSpec(blk, idx)
        return pl.BlockSpec((None,) + blk, lambda i, j, k: (l,) + idx(i, j, k))

    if mode == "nn":
        a_spec = lead(a_l, (tp, tr), lambda i, j, k: (i, k))
        b_spec = lead(b_l, (tr, tq), lambda i, j, k: (k, j))
    elif mode == "nt":
        a_spec = lead(a_l, (tp, tr), lambda i, j, k: (i, k))
        b_spec = lead(b_l, (tq, tr), lambda i, j, k: (j, k + b_r_off))
    else:
        a_spec = lead(a_l, (tr, tp), lambda i, j, k: (k, i))
        b_spec = lead(b_l, (tr, tq), lambda i, j, k: (k, j))
    in_specs = [a_spec, b_spec]
    args = [a, b]
    if bias is not None:
        if bias.shape[0] == 1:
            in_specs.append(pl.BlockSpec((1, tq), lambda i, j, k: (0, j)))
        else:
            in_specs.append(pl.BlockSpec((tp, tq), lambda i, j, k: (i, j)))
        args.append(bias)
    aliases = {}
    if into is not None:
        in_specs.append(pl.BlockSpec(memory_space=pl.ANY))
        args.append(into)
        aliases = {len(args) - 1: 0}
        out_shape = jax.ShapeDtypeStruct(into.shape, into.dtype)
        out_dtype = into.dtype
        if o_l is None:
            out_spec = pl.BlockSpec((tp, tq), lambda i, j, k: (i, j + q_off))
        else:
            out_spec = pl.BlockSpec((None, tp, tq), lambda i, j, k: (o_l, i, j + q_off))
    else:
        out_shape = jax.ShapeDtypeStruct((P, Q), out_dtype)
        out_spec = pl.BlockSpec((tp, tq), lambda i, j, k: (i, j))
    n_in = len(args) + len(after)
    in_specs += [pl.BlockSpec(memory_space=pl.ANY)] * len(after)
    args += list(after)
    has_bias = bias is not None
    has_into = into is not None

    def body(*refs):
        a_ref, b_ref = refs[0], refs[1]
        pos = 2
        bias_ref = None
        if has_bias:
            bias_ref = refs[pos]
            pos += 1
        o_ref = refs[n_in]
        acc_ref = refs[n_in + 1] if nk > 1 else None
        part = lax.dot_general(a_ref[...], b_ref[...], dims, preferred_element_type=F32)

        def finish(acc):
            if has_bias:
                acc = acc + bias_ref[...]
            o_ref[...] = acc.astype(out_dtype)

        if nk == 1:
            finish(part)
        else:
            k = pl.program_id(2)

            @pl.when(k == 0)
            def _():
                acc_ref[...] = part

            @pl.when(k > 0)
            def _():
                acc_ref[...] += part

            @pl.when(k == nk - 1)
            def _():
                finish(acc_ref[...])

    est = 2 * (_nbytes((tp, tr), a.dtype) + _nbytes((tr, tq), b.dtype) + _nbytes((tp, tq), out_dtype)) + 3 * tp * tq * 4
    return pl.pallas_call(
        body, name=name, out_shape=out_shape,
        grid=(P // tp, Q // tq, nk),
        in_specs=in_specs, out_specs=out_spec,
        scratch_shapes=[pltpu.VMEM((tp, tq), F32)] if nk > 1 else [],
        input_output_aliases=aliases,
        compiler_params=pltpu.CompilerParams(
            dimension_semantics=("parallel", "parallel", "arbitrary"), vmem_limit_bytes=_vmem(est)),
    )(*args)


def _row_call(body, ins, outs, *, name, rows, tr, acc_outs=(), est=0):
    in_specs = []
    for arr, kind in ins:
        if kind == "row":
            in_specs.append(pl.BlockSpec((tr, arr.shape[1]), lambda i: (i, 0)))
        else:
            nd = arr.ndim
            in_specs.append(pl.BlockSpec(arr.shape, lambda i, nd=nd: (0,) * nd))
    out_shapes = [jax.ShapeDtypeStruct(s, d) for s, d in outs] + [jax.ShapeDtypeStruct(s, d) for s, d in acc_outs]
    out_specs = [pl.BlockSpec((tr, s[1]), lambda i: (i, 0)) for s, _ in outs]
    out_specs += [pl.BlockSpec(s, lambda i, nd=len(s): (0,) * nd) for s, _ in acc_outs]
    res = pl.pallas_call(
        body, name=name, out_shape=out_shapes, grid=(rows // tr,), in_specs=in_specs, out_specs=out_specs,
        compiler_params=pltpu.CompilerParams(dimension_semantics=("arbitrary",), vmem_limit_bytes=_vmem(est)),
    )(*[a for a, _ in ins])
    return res


def _rms_fwd(x, g, *, out_dtype, name):
    T, D = x.shape
    tr = _pick(T, (512, 256, 128))

    def body(x_ref, g_ref, o_ref):
        xv = x_ref[...]
        r = lax.rsqrt(jnp.mean(xv * xv, axis=-1, keepdims=True) + RMS_EPS)
        o_ref[...] = (xv * r * g_ref[...]).astype(out_dtype)

    return _row_call(body, [(x, "row"), (g, "full")], [((T, D), out_dtype)], name=name, rows=T, tr=tr,
                     est=8 * tr * D * 4)[0]


def _rms_res(h, y, g, *, name):
    T, D = h.shape
    tr = _pick(T, (512, 256, 128))

    def body(h_ref, y_ref, g_ref, o_ref):
        yv = y_ref[...]
        r = lax.rsqrt(jnp.mean(yv * yv, axis=-1, keepdims=True) + RMS_EPS)
        o_ref[...] = h_ref[...] + yv * r * g_ref[...]

    return _row_call(body, [(h, "row"), (y, "row"), (g, "full")], [((T, D), F32)], name=name, rows=T, tr=tr,
                     est=10 * tr * D * 4)[0]


def _rms_bwd(x, g, dy, dres, *, out_dtype, name):
    T, D = x.shape
    tr = _pick(T, (512, 256, 128))
    has_res = dres is not None

    def body(*refs):
        if has_res:
            x_ref, g_ref, dy_ref, dr_ref, dx_ref, dg_ref = refs
        else:
            x_ref, g_ref, dy_ref, dx_ref, dg_ref = refs
        xv = x_ref[...]
        r = lax.rsqrt(jnp.mean(xv * xv, axis=-1, keepdims=True) + RMS_EPS)
        xhat = xv * r
        dyv = dy_ref[...].astype(F32)
        dxn = dyv * g_ref[...]
        dx = r * (dxn - xhat * jnp.mean(dxn * xhat, axis=-1, keepdims=True))
        if has_res:
            dx = dx + dr_ref[...]
        dx_ref[...] = dx.astype(out_dtype)
        part = jnp.sum(dyv * xhat, axis=0, keepdims=True)

        @pl.when(pl.program_id(0) == 0)
        def _():
            dg_ref[...] = part

        @pl.when(pl.program_id(0) > 0)
        def _():
            dg_ref[...] += part

    ins = [(x, "row"), (g, "full"), (dy, "row")] + ([(dres, "row")] if has_res else [])
    dx, dg = _row_call(body, ins, [((T, D), out_dtype)], name=name, rows=T, tr=tr, acc_outs=[((1, D), F32)],
                       est=12 * tr * D * 4)
    return dx, dg


def _rms_res_norm(h, y, g_res, g_next, *, name):
    T, D = h.shape
    tr = _pick(T, (512, 256, 128))

    def body(h_ref, y_ref, g_ref, gn_ref, o_ref, n_ref):
        yv = y_ref[...]
        r = lax.rsqrt(jnp.mean(yv * yv, axis=-1, keepdims=True) + RMS_EPS)
        h2 = h_ref[...] + yv * r * g_ref[...]
        o_ref[...] = h2
        r2 = lax.rsqrt(jnp.mean(h2 * h2, axis=-1, keepdims=True) + RMS_EPS)
        n_ref[...] = (h2 * r2 * gn_ref[...]).astype(BF16)

    return _row_call(body, [(h, "row"), (y, "row"), (g_res, "full"), (g_next, "full")],
                     [((T, D), F32), ((T, D), BF16)], name=name, rows=T, tr=tr, est=12 * tr * D * 4)


def _rms_bwd_chain(x1, g1, dy1, dres, x2, g2, *, name):
    T, D = x1.shape
    tr = _pick(T, (512, 256, 128))

    def one(xv, gv, dyv):
        r = lax.rsqrt(jnp.mean(xv * xv, axis=-1, keepdims=True) + RMS_EPS)
        xhat = xv * r
        dxn = dyv * gv
        dx = r * (dxn - xhat * jnp.mean(dxn * xhat, axis=-1, keepdims=True))
        return dx, jnp.sum(dyv * xhat, axis=0, keepdims=True)

    def body(x1_ref, g1_ref, dy1_ref, dr_ref, x2_ref, g2_ref, d1_ref, d2_ref, dg1_ref, dg2_ref):
        dx1, p1 = one(x1_ref[...], g1_ref[...], dy1_ref[...].astype(F32))
        d1 = dx1 + dr_ref[...]
        d1_ref[...] = d1
        dx2, p2 = one(x2_ref[...], g2_ref[...], d1)
        d2_ref[...] = dx2.astype(BF16)

        @pl.when(pl.program_id(0) == 0)
        def _():
            dg1_ref[...] = p1
            dg2_ref[...] = p2

        @pl.when(pl.program_id(0) > 0)
        def _():
            dg1_ref[...] += p1
            dg2_ref[...] += p2

    ins = [(x1, "row"), (g1, "full"), (dy1, "row"), (dres, "row"), (x2, "row"), (g2, "full")]
    return _row_call(body, ins, [((T, D), F32), ((T, D), BF16)], name=name, rows=T, tr=tr,
                     acc_outs=[((1, D), F32), ((1, D), F32)], est=20 * tr * D * 4)


def _ffn_up(fn, w_gu, l, *, name):
    T, D = fn.shape
    H = w_gu.shape[2] // 2
    tp = _pick(T, (1024, 512, 256, 128))
    tq = _pick(H, (1408, 768, 512, 256, 128))
    nj = H // tq

    def body(a_ref, wg_ref, wu_ref, g_ref, u_ref, act_ref):
        a = a_ref[...]
        g = jnp.dot(a, wg_ref[...], preferred_element_type=F32)
        u = jnp.dot(a, wu_ref[...], preferred_element_type=F32)
        sg = jax.nn.sigmoid(g)
        silu = g * sg
        g_ref[...] = (u * (sg + silu * (1.0 - sg))).astype(BF16)
        u_ref[...] = silu.astype(BF16)
        act_ref[...] = (silu * u).astype(BF16)

    tile = pl.BlockSpec((tp, tq), lambda j, i: (i, j))
    est = 2 * (tp * D * 2 + 2 * D * tq * 2 + 3 * tp * tq * 2) + 4 * tp * tq * 4
    return pl.pallas_call(
        body, name=name,
        out_shape=[jax.ShapeDtypeStruct((T, H), BF16), jax.ShapeDtypeStruct((T, H), BF16),
                   jax.ShapeDtypeStruct((T, H), BF16)],
        grid=(nj, T // tp),
        in_specs=[pl.BlockSpec((tp, D), lambda j, i: (i, 0)),
                  pl.BlockSpec((None, D, tq), lambda j, i: (l, 0, j)),
                  pl.BlockSpec((None, D, tq), lambda j, i: (l, 0, j + nj))],
        out_specs=[tile, tile, tile],
        compiler_params=pltpu.CompilerParams(dimension_semantics=("parallel", "parallel"),
                                             vmem_limit_bytes=_vmem(est)),
    )(fn, w_gu, w_gu)


def _ffn_down_dx(df, w_down, l, g, u, after, *, name):
    T, D = df.shape
    H = w_down.shape[1]
    tp = _pick(T, (1024, 512, 256, 128))
    tq = _pick(H, (1408, 768, 512, 256, 128))

    def body(a_ref, w_ref, g_ref, u_ref, _, dg_ref, du_ref):
        da = lax.dot_general(a_ref[...], w_ref[...], (((1,), (1,)), ((), ())), preferred_element_type=F32)
        dg_ref[...] = (da * g_ref[...].astype(F32)).astype(BF16)
        du_ref[...] = (da * u_ref[...].astype(F32)).astype(BF16)

    tile = pl.BlockSpec((tp, tq), lambda j, i: (i, j))
    est = 2 * (tp * D * 2 + tq * D * 2 + 4 * tp * tq * 2) + 3 * tp * tq * 4
    return pl.pallas_call(
        body, name=name,
        out_shape=[jax.ShapeDtypeStruct((T, H), BF16), jax.ShapeDtypeStruct((T, H), BF16)],
        grid=(H // tq, T // tp),
        in_specs=[pl.BlockSpec((tp, D), lambda j, i: (i, 0)),
                  pl.BlockSpec((None, tq, D), lambda j, i: (l, j, 0)), tile, tile,
                  pl.BlockSpec(memory_space=pl.ANY)],
        out_specs=[tile, tile],
        compiler_params=pltpu.CompilerParams(dimension_semantics=("parallel", "parallel"),
                                             vmem_limit_bytes=_vmem(est)),
    )(df, w_down, g, u, after)


def _loss_and_grad(y, target, x, g, *, name):
    T, D = y.shape
    tr = _pick(T, (512, 256, 128))

    def body(y_ref, t_ref, x_ref, g_ref, dy_ref, dx_ref, l_ref, dg_ref):
        e = y_ref[...] - t_ref[...]
        dy = e * (1.0 / D)
        dy_ref[...] = dy
        part = jnp.sum(jnp.sum(e * e, axis=1, keepdims=True), axis=0, keepdims=True) * (0.5 / D)
        xv = x_ref[...]
        r = lax.rsqrt(jnp.mean(xv * xv, axis=-1, keepdims=True) + RMS_EPS)
        xhat = xv * r
        dxn = dy * g_ref[...]
        dx_ref[...] = (r * (dxn - xhat * jnp.mean(dxn * xhat, axis=-1, keepdims=True))).astype(BF16)
        dg = jnp.sum(dy * xhat, axis=0, keepdims=True)

        @pl.when(pl.program_id(0) == 0)
        def _():
            l_ref[...] = part
            dg_ref[...] = dg

        @pl.when(pl.program_id(0) > 0)
        def _():
            l_ref[...] += part
            dg_ref[...] += dg

    dy, dx, l, dg = _row_call(body, [(y, "row"), (target, "row"), (x, "row"), (g, "full")],
                              [((T, D), F32), ((T, D), BF16)], name=name, rows=T, tr=tr,
                              acc_outs=[((1, 1), F32), ((1, D), F32)], est=14 * tr * D * 4)
    return dy, dx, l, dg


_SQRT_HALF = 0.7071067811865476
_INV_SQRT_2PI = 0.3989422804014327


def _gelu_parts(x):
    cdf = 0.5 * (1.0 + lax.erf(x * _SQRT_HALF))
    return cdf


def _sgu_common(pre, lng, lnb, W):
    cdf = _gelu_parts(pre)
    z = pre * cdf
    u = z[:, :W]
    v = z[:, W:]
    mu = jnp.mean(v, axis=-1, keepdims=True)
    vc = v - mu
    var = jnp.mean(vc * vc, axis=-1, keepdims=True)
    rstd = lax.rsqrt(var + LN_EPS)
    vhat = vc * rstd
    vn = vhat * lng + lnb
    return cdf, u, vhat, rstd, vn


def _causal_mask():
    t = lax.broadcasted_iota(jnp.int32, (CHUNK, CHUNK), 0)
    s = lax.broadcasted_iota(jnp.int32, (CHUNK, CHUNK), 1)
    return t >= s


def _sgu_fwd(pre, lng, lnb, ws, bsT, *, name):
    T, W2 = pre.shape
    W = W2 // 2
    G = ws.shape[0]
    gd = W // G

    def body(pre_ref, lng_ref, lnb_ref, ws_ref, bs_ref, o_ref):
        _, u, _, _, vn = _sgu_common(pre_ref[...], lng_ref[...], lnb_ref[...], W)
        vnb = vn.astype(BF16)
        causal = _causal_mask()
        for g in range(G):
            w = jnp.where(causal, ws_ref[g], 0.0).astype(BF16)
            sv = jnp.dot(w, vnb[:, g * gd:(g + 1) * gd], preferred_element_type=F32) + bs_ref[:, g:g + 1]
            o_ref[:, g * gd:(g + 1) * gd] = (u[:, g * gd:(g + 1) * gd] * sv).astype(BF16)

    return pl.pallas_call(
        body, name=name, out_shape=jax.ShapeDtypeStruct((T, W), BF16), grid=(T // CHUNK,),
        in_specs=[pl.BlockSpec((CHUNK, W2), lambda i: (i, 0)),
                  pl.BlockSpec((1, W), lambda i: (0, 0)), pl.BlockSpec((1, W), lambda i: (0, 0)),
                  pl.BlockSpec(ws.shape, lambda i: (0, 0, 0)), pl.BlockSpec(bsT.shape, lambda i: (0, 0))],
        out_specs=pl.BlockSpec((CHUNK, W), lambda i: (i, 0)),
        compiler_params=pltpu.CompilerParams(dimension_semantics=("arbitrary",),
                                             vmem_limit_bytes=_vmem(12 * CHUNK * W2 * 4)),
    )(pre, lng, lnb, ws, bsT)


def _sgu_bwd(pre, dgated, lng, lnb, ws, bsT, *, name):
    T, W2 = pre.shape
    W = W2 // 2
    G = ws.shape[0]
    gd = W // G

    def body(pre_ref, dgt_ref, lng_ref, lnb_ref, ws_ref, bs_ref,
             dpre_ref, dws_ref, dbs_ref, dlng_ref, dlnb_ref, dbin_ref):
        first = pl.program_id(0) == 0

        @pl.when(first)
        def _():
            dws_ref[...] = jnp.zeros_like(dws_ref)
            dbs_ref[...] = jnp.zeros_like(dbs_ref)
            dlng_ref[...] = jnp.zeros_like(dlng_ref)
            dlnb_ref[...] = jnp.zeros_like(dlnb_ref)
            dbin_ref[...] = jnp.zeros_like(dbin_ref)

        pre_v = pre_ref[...]
        lng_v = lng_ref[...]
        cdf, u, vhat, rstd, vn = _sgu_common(pre_v, lng_v, lnb_ref[...], W)
        vnb = vn.astype(BF16)
        dgt = dgt_ref[...].astype(F32)
        causal = _causal_mask()
        du_parts, dvn_parts = [], []
        for g in range(G):
            sl = slice(g * gd, (g + 1) * gd)
            w = jnp.where(causal, ws_ref[g], 0.0).astype(BF16)
            sv = jnp.dot(w, vnb[:, sl], preferred_element_type=F32) + bs_ref[:, g:g + 1]
            dgt_g = dgt[:, sl]
            du_parts.append(dgt_g * sv)
            dsv = dgt_g * u[:, sl]
            dsvb = dsv.astype(BF16)
            dvn_parts.append(lax.dot_general(w, dsvb, (((0,), (0,)), ((), ())), preferred_element_type=F32))
            dw = lax.dot_general(dsvb, vnb[:, sl], (((1,), (1,)), ((), ())), preferred_element_type=F32)
            dws_ref[g] += jnp.where(causal, dw, 0.0)
            dbs_ref[:, g:g + 1] += jnp.sum(dsv, axis=1, keepdims=True)
        du = jnp.concatenate(du_parts, axis=1)
        dvn = jnp.concatenate(dvn_parts, axis=1)
        dlng_ref[...] += jnp.sum(dvn * vhat, axis=0, keepdims=True)
        dlnb_ref[...] += jnp.sum(dvn, axis=0, keepdims=True)
        dvh = dvn * lng_v
        dv = rstd * (dvh - jnp.mean(dvh, axis=-1, keepdims=True)
                     - vhat * jnp.mean(dvh * vhat, axis=-1, keepdims=True))
        dz = jnp.concatenate([du, dv], axis=1)
        dgelu = cdf + pre_v * jnp.exp(-0.5 * pre_v * pre_v) * _INV_SQRT_2PI
        dpre = dz * dgelu
        dbin_ref[...] += jnp.sum(dpre, axis=0, keepdims=True)
        dpre_ref[...] = dpre.astype(BF16)

    full = lambda shape: pl.BlockSpec(shape, lambda i, nd=len(shape): (0,) * nd)
    return pl.pallas_call(
        body, name=name,
        out_shape=[jax.ShapeDtypeStruct((T, W2), BF16), jax.ShapeDtypeStruct(ws.shape, F32),
                   jax.ShapeDtypeStruct(bsT.shape, F32), jax.ShapeDtypeStruct((1, W), F32),
                   jax.ShapeDtypeStruct((1, W), F32), jax.ShapeDtypeStruct((1, W2), F32)],
        grid=(T // CHUNK,),
        in_specs=[pl.BlockSpec((CHUNK, W2), lambda i: (i, 0)), pl.BlockSpec((CHUNK, W), lambda i: (i, 0)),
                  full((1, W)), full((1, W)), full(ws.shape), full(bsT.shape)],
        out_specs=[pl.BlockSpec((CHUNK, W2), lambda i: (i, 0)), full(ws.shape), full(bsT.shape),
                   full((1, W)), full((1, W)), full((1, W2))],
        compiler_params=pltpu.CompilerParams(dimension_semantics=("arbitrary",),
                                             vmem_limit_bytes=_vmem(24 * CHUNK * W2 * 4)),
    )(pre, dgated, lng, lnb, ws, bsT)


def _rope_tables(positions):
    half = ROPE_DIM // 2
    inv_freq = ROPE_THETA ** (-jnp.arange(0, ROPE_DIM, 2, dtype=F32) / ROPE_DIM)
    ang = positions.astype(F32).reshape(-1, 1) * inv_freq
    cos, sin = jnp.cos(ang), jnp.sin(ang)
    T = ang.shape[0]
    rest = HEAD_DIM - ROPE_DIM
    c64 = jnp.concatenate([cos, cos, jnp.ones((T, rest), F32)], axis=1)
    s64 = jnp.concatenate([-sin, sin, jnp.zeros((T, rest), F32)], axis=1)
    del half
    return jnp.tile(c64, (1, LANES // HEAD_DIM)), jnp.tile(s64, (1, LANES // HEAD_DIM))


def _swap8(x):
    W = x.shape[1]
    half = ROPE_DIM // 2
    lane = lax.broadcasted_iota(jnp.int32, x.shape, 1) % HEAD_DIM
    return jnp.where(lane < half, pltpu.roll(x, W - half, axis=1),
                     jnp.where(lane < ROPE_DIM, pltpu.roll(x, half, axis=1), 0.0))


def _wide(tab, W):
    return jnp.concatenate([tab] * (W // LANES), axis=1) if W > LANES else tab


def _rope_fwd(qkv, ctab, stab, *, q_width, kv_width, name):
    T = qkv.shape[0]
    tr = _pick(T, (256, 128))
    scale = HEAD_DIM ** -0.5

    def body(x_ref, c_ref, s_ref, q_ref, k_ref, v_ref):
        c = c_ref[...]
        s = s_ref[...]
        q = x_ref[:, :q_width]
        k = x_ref[:, q_width:q_width + kv_width]
        q_ref[...] = ((q * _wide(c, q_width) + _swap8(q) * _wide(s, q_width)) * scale).astype(BF16)
        k_ref[...] = (k * _wide(c, kv_width) + _swap8(k) * _wide(s, kv_width)).astype(BF16)
        v_ref[...] = x_ref[:, q_width + kv_width:].astype(BF16)

    return _row_call(body, [(qkv, "row"), (ctab, "row"), (stab, "row")],
                     [((T, q_width), BF16), ((T, kv_width), BF16), ((T, kv_width), BF16)],
                     name=name, rows=T, tr=tr, est=10 * tr * qkv.shape[1] * 4)


_NT = (((1,), (1,)), ((), ()))
_TN = (((0,), (0,)), ((), ()))


def _group_rows(ref, heads):
    return jnp.concatenate([ref[:, h * HEAD_DIM:(h + 1) * HEAD_DIM] for h in heads], axis=0)


def _attn_valid(grp):
    qi = np.arange(grp * CHUNK)[:, None] % CHUNK
    sj = np.arange(2 * CHUNK)[None, :]
    cur = (sj >= CHUNK) & (sj - CHUNK <= qi)
    prev = (sj < CHUNK) & (sj > qi)
    return jnp.asarray(np.stack([cur, cur | prev]).astype(np.float32))


def _valid_spec(grp):
    return pl.BlockSpec((None, grp * CHUNK, 2 * CHUNK), lambda n: (jnp.minimum(n, 1), 0, 0))


def _attn_group_probs(q, kk, sinks, valid, grp):
    rows = grp * CHUNK
    s = lax.dot_general(q, kk, _NT, preferred_element_type=F32)
    s = jnp.where(valid, s, NEG_INF)
    r = lax.broadcasted_iota(jnp.int32, (rows, 1), 0)
    sink = jnp.full((rows, 1), sinks[grp - 1], F32)
    for g in range(grp - 2, -1, -1):
        sink = jnp.where(r < (g + 1) * CHUNK, sinks[g], sink)
    m = jnp.maximum(jnp.max(s, axis=1, keepdims=True), sink)
    p = jnp.exp(s - m)
    ps = jnp.exp(sink - m)
    inv = 1.0 / (jnp.sum(p, axis=1, keepdims=True) + ps)
    return p * inv, ps * inv


def _kv_specs(width, nb):
    prev = pl.BlockSpec((CHUNK, width), lambda n: (jnp.maximum(n - 1, 0), 0))
    cur = pl.BlockSpec((CHUNK, width), lambda n: (n, 0))
    return prev, cur


def _attn_fwd(qr, kr, vr, sinks, *, name):
    T, QW = qr.shape
    KW = kr.shape[1]
    HQ, HK = QW // HEAD_DIM, KW // HEAD_DIM
    grp = HQ // HK
    nb = T // CHUNK

    def body(q_ref, kp_ref, kc_ref, vp_ref, vc_ref, s_ref, ok_ref, o_ref):
        valid = ok_ref[...] > 0.5
        for kh in range(HK):
            ks = slice(kh * HEAD_DIM, (kh + 1) * HEAD_DIM)
            heads = list(range(kh * grp, (kh + 1) * grp))
            q = _group_rows(q_ref, heads)
            kk = jnp.concatenate([kp_ref[:, ks], kc_ref[:, ks]], axis=0)
            vv = jnp.concatenate([vp_ref[:, ks], vc_ref[:, ks]], axis=0)
            p, _ = _attn_group_probs(q, kk, [s_ref[0, h] for h in heads], valid, grp)
            o = jnp.dot(p.astype(BF16), vv, preferred_element_type=F32).astype(BF16)
            for g, h in enumerate(heads):
                o_ref[:, h * HEAD_DIM:(h + 1) * HEAD_DIM] = o[g * CHUNK:(g + 1) * CHUNK]

    kp, kc = _kv_specs(KW, nb)
    return pl.pallas_call(
        body, name=name, out_shape=jax.ShapeDtypeStruct((T, QW), BF16), grid=(nb,),
        in_specs=[pl.BlockSpec((CHUNK, QW), lambda n: (n, 0)), kp, kc, kp, kc,
                  pl.BlockSpec(memory_space=pltpu.SMEM), _valid_spec(grp)],
        out_specs=pl.BlockSpec((CHUNK, QW), lambda n: (n, 0)),
        compiler_params=pltpu.CompilerParams(dimension_semantics=("arbitrary",), vmem_limit_bytes=_vmem(8 << 20)),
    )(qr, kr, kr, vr, vr, sinks, _attn_valid(grp))


def _attn_bwd(qr, kr, vr, sinks, do, *, name):
    T, QW = qr.shape
    KW = kr.shape[1]
    HQ, HK = QW // HEAD_DIM, KW // HEAD_DIM
    grp = HQ // HK
    nb = T // CHUNK

    def body(q_ref, kp_ref, kc_ref, vp_ref, vc_ref, s_ref, do_ref, ok_ref,
             dq_ref, dkp_ref, dkc_ref, dvp_ref, dvc_ref, ds_ref):
        n = pl.program_id(0)
        valid = ok_ref[...] > 0.5
        lane = lax.broadcasted_iota(jnp.int32, (1, LANES), 1)
        dsink = jnp.zeros((1, LANES), F32)
        for kh in range(HK):
            ks = slice(kh * HEAD_DIM, (kh + 1) * HEAD_DIM)
            heads = list(range(kh * grp, (kh + 1) * grp))
            q = _group_rows(q_ref, heads)
            doh = _group_rows(do_ref, heads)
            kk = jnp.concatenate([kp_ref[:, ks], kc_ref[:, ks]], axis=0)
            vv = jnp.concatenate([vp_ref[:, ks], vc_ref[:, ks]], axis=0)
            p, ps = _attn_group_probs(q, kk, [s_ref[0, h] for h in heads], valid, grp)
            dp = lax.dot_general(doh, vv, _NT, preferred_element_type=F32)
            delta = jnp.sum(p * dp, axis=1, keepdims=True)
            ds = (p * (dp - delta)).astype(BF16)
            dv = lax.dot_general(p.astype(BF16), doh, _TN, preferred_element_type=F32)
            dk = lax.dot_general(ds, q, _TN, preferred_element_type=F32)
            dq = jnp.dot(ds, kk, preferred_element_type=F32)
            psd = ps * delta
            for g, h in enumerate(heads):
                dq_ref[:, h * HEAD_DIM:(h + 1) * HEAD_DIM] = dq[g * CHUNK:(g + 1) * CHUNK]
                dsink = dsink + jnp.where(
                    lane == h, -jnp.sum(psd[g * CHUNK:(g + 1) * CHUNK], axis=0, keepdims=True), 0.0)
            dkp_ref[:, ks] = dk[:CHUNK]
            dkc_ref[:, ks] = dk[CHUNK:]
            dvp_ref[:, ks] = dv[:CHUNK]
            dvc_ref[:, ks] = dv[CHUNK:]

        @pl.when(n == 0)
        def _():
            ds_ref[...] = dsink

        @pl.when(n > 0)
        def _():
            ds_ref[...] += dsink

    kp, kc = _kv_specs(KW, nb)
    qspec = pl.BlockSpec((CHUNK, QW), lambda n: (n, 0))
    kout = pl.BlockSpec((CHUNK, KW), lambda n: (n, 0))
    return pl.pallas_call(
        body, name=name,
        out_shape=[jax.ShapeDtypeStruct((T, QW), F32)] + [jax.ShapeDtypeStruct((T, KW), F32)] * 4
        + [jax.ShapeDtypeStruct((1, LANES), F32)],
        grid=(nb,),
        in_specs=[qspec, kp, kc, kp, kc, pl.BlockSpec(memory_space=pltpu.SMEM), qspec, _valid_spec(grp)],
        out_specs=[qspec, kout, kout, kout, kout, pl.BlockSpec((1, LANES), lambda n: (0, 0))],
        compiler_params=pltpu.CompilerParams(dimension_semantics=("arbitrary",), vmem_limit_bytes=_vmem(12 << 20)),
    )(qr, kr, kr, vr, vr, sinks, do, _attn_valid(grp))


def _rope_bwd(dq, dkp, dkc, dvp, dvc, ctab, stab, *, name):
    T, QW = dq.shape
    KW = dkp.shape[1]
    nb = T // CHUNK
    scale = HEAD_DIM ** -0.5
    width = QW + 2 * KW

    def body(dq_ref, dkc_ref, dkn_ref, dvc_ref, dvn_ref, c_ref, s_ref, o_ref, db_ref):
        n = pl.program_id(0)
        c = c_ref[...]
        s = s_ref[...]
        has_next = (n < nb - 1).astype(F32)
        dqv = dq_ref[...]
        dk = dkc_ref[...] + has_next * dkn_ref[...]
        dv = dvc_ref[...] + has_next * dvn_ref[...]
        dq_pre = (dqv * _wide(c, QW) + _swap8(dqv * _wide(s, QW))) * scale
        dk_pre = dk * _wide(c, KW) + _swap8(dk * _wide(s, KW))
        o_ref[:, :QW] = dq_pre.astype(BF16)
        o_ref[:, QW:QW + KW] = dk_pre.astype(BF16)
        o_ref[:, QW + KW:] = dv.astype(BF16)
        part = jnp.concatenate([jnp.sum(dq_pre, axis=0, keepdims=True), jnp.sum(dk_pre, axis=0, keepdims=True),
                                jnp.sum(dv, axis=0, keepdims=True)], axis=1)

        @pl.when(n == 0)
        def _():
            db_ref[...] = part

        @pl.when(n > 0)
        def _():
            db_ref[...] += part

    cur = lambda w: pl.BlockSpec((CHUNK, w), lambda n: (n, 0))
    nxt = lambda w: pl.BlockSpec((CHUNK, w), lambda n: (jnp.minimum(n + 1, nb - 1), 0))
    return pl.pallas_call(
        body, name=name,
        out_shape=[jax.ShapeDtypeStruct((T, width), BF16), jax.ShapeDtypeStruct((1, width), F32)],
        grid=(nb,),
        in_specs=[cur(QW), cur(KW), nxt(KW), cur(KW), nxt(KW), cur(LANES), cur(LANES)],
        out_specs=[cur(width), pl.BlockSpec((1, width), lambda n: (0, 0))],
        compiler_params=pltpu.CompilerParams(dimension_semantics=("arbitrary",), vmem_limit_bytes=_vmem(8 << 20)),
    )(dq, dkc, dkp, dvc, dvp, ctab, stab)


def _cast_block(w, l, axis, chip_arr, *, name):
    _, Ks, Ns = w.shape
    tk = _pick(Ks, (512, 352, 256, 128))
    nk = Ks // tk
    full = (Ks * N_CHIPS, Ns) if axis == 0 else (Ks, Ns * N_CHIPS)

    def body(p_ref, w_ref, o_ref):
        o_ref[...] = w_ref[...].astype(BF16)

    if axis == 0:
        out_spec = pl.BlockSpec((tk, Ns), lambda i, p: (p[0] * nk + i, 0))
    else:
        out_spec = pl.BlockSpec((tk, Ns), lambda i, p: (i, p[0]))
    grid_spec = pltpu.PrefetchScalarGridSpec(
        num_scalar_prefetch=1, grid=(nk,),
        in_specs=[pl.BlockSpec((None, tk, Ns), lambda i, p: (l, i, 0))], out_specs=out_spec)
    return pl.pallas_call(
        body, name=name, out_shape=jax.ShapeDtypeStruct(full, BF16), grid_spec=grid_spec,
        compiler_params=pltpu.CompilerParams(dimension_semantics=("arbitrary",),
                                             vmem_limit_bytes=_vmem(4 * tk * Ns * 6)),
    )(chip_arr, w)


def _adamw_math(w, g, m, v):
    m = ADAM_B1 * m + (1.0 - ADAM_B1) * g
    v = ADAM_B2 * v + (1.0 - ADAM_B2) * (g * g)
    m_hat = m / (1.0 - ADAM_B1 ** ADAM_STEP)
    v_hat = v / (1.0 - ADAM_B2 ** ADAM_STEP)
    delta = -ADAM_LR * (m_hat / (jnp.sqrt(v_hat) + ADAM_EPS) + ADAM_WD * w)
    return delta, m, v


def _adamw_layer(w, m, v, g, l, outs, *, name):
    _, K, N = w.shape
    tk = _pick(K, (256, 176, 128))

    def body(w_ref, m_ref, v_ref, g_ref, _g, _d, _m, _v, go_ref, d_ref, mo_ref, vo_ref):
        gv = g_ref[...]
        d, mn, vn = _adamw_math(w_ref[...], gv, m_ref[...], v_ref[...])
        go_ref[...] = gv
        d_ref[...] = d
        mo_ref[...] = mn
        vo_ref[...] = vn

    layer = pl.BlockSpec((None, tk, N), lambda i: (l, i, 0))
    any_spec = pl.BlockSpec(memory_space=pl.ANY)
    sd = jax.ShapeDtypeStruct(w.shape, F32)
    return pl.pallas_call(
        body, name=name, out_shape=[sd, sd, sd, sd], grid=(K // tk,),
        in_specs=[layer, layer, layer, pl.BlockSpec((tk, N), lambda i: (i, 0))] + [any_spec] * 4,
        out_specs=[layer] * 4, input_output_aliases={4: 0, 5: 1, 6: 2, 7: 3},
        compiler_params=pltpu.CompilerParams(dimension_semantics=("arbitrary",),
                                             vmem_limit_bytes=_vmem(2 * 8 * tk * N * 4 + 6 * tk * N * 4)),
    )(w, m, v, g, *outs)


def _adamw_small(w, g, m, v, *, name):
    def body(w_ref, g_ref, m_ref, v_ref, d_ref, mo_ref, vo_ref):
        d, mn, vn = _adamw_math(w_ref[...], g_ref[...], m_ref[...], v_ref[...])
        d_ref[...] = d
        mo_ref[...] = mn
        vo_ref[...] = vn

    sd = jax.ShapeDtypeStruct(w.shape, F32)
    return pl.pallas_call(body, name=name, out_shape=[sd, sd, sd])(w, g, m, v)


def _my_place():
    return lax.axis_index("x"), lax.axis_index("y"), lax.axis_index("c")


def _peer_chips(x, y):
    return [(1 - x, y), (x, 1 - y), (1 - x, 1 - y)]


_HBM = pl.BlockSpec(memory_space=pltpu.HBM)
_SEM = pl.BlockSpec(memory_space=pltpu.SEMAPHORE)
_EFFECT = pltpu.SideEffectType.DATAFLOW_SIDE_EFFECTING


def _split_start(name, bufs, n_copies, make_copies, after):
    nb = len(bufs)

    def body(*refs):
        send_sems, recv_sems = refs[nb + 1], refs[nb + 2]
        token = refs[2 * nb + 3]
        sends, _ = make_copies(refs[:nb], send_sems, recv_sems)
        for cp in sends:
            cp.start()
        token[...] = jnp.zeros_like(token)

    res = pl.pallas_call(
        body, name=name,
        out_shape=(pltpu.SemaphoreType.DMA((n_copies,)), pltpu.SemaphoreType.DMA((n_copies,)),
                   *[pltpu.HBM(b.shape, b.dtype) for b in bufs], jax.ShapeDtypeStruct((8, LANES), F32)),
        in_specs=[_HBM] * nb + [pl.BlockSpec(memory_space=pl.ANY)],
        out_specs=(_SEM, _SEM, *[_HBM] * nb, pl.BlockSpec(memory_space=pltpu.VMEM)),
        input_output_aliases={k: 2 + k for k in range(nb)},
        compiler_params=pltpu.CompilerParams(has_side_effects=_EFFECT),
    )(*[pltpu.with_memory_space_constraint(b, pltpu.HBM) for b in bufs], after)
    return res[0], res[1], list(res[2:2 + nb]), res[2 + nb]


def _split_wait(name, bufs, sems, make_copies, after):
    nb = len(bufs)

    def body(*refs):
        send_sems, recv_sems = refs[nb], refs[nb + 1]
        sends, recvs = make_copies(refs[:nb], send_sems, recv_sems)
        for cp in sends:
            cp.wait_send()
        for cp in recvs:
            cp.wait_recv()

    res = pl.pallas_call(
        body, name=name,
        out_shape=tuple(pltpu.HBM(b.shape, b.dtype) for b in bufs),
        in_specs=[_HBM] * nb + [_SEM, _SEM, pl.BlockSpec(memory_space=pl.ANY)],
        out_specs=tuple([_HBM] * nb),
        input_output_aliases={k: k for k in range(nb)},
        compiler_params=pltpu.CompilerParams(has_side_effects=_EFFECT),
    )(*bufs, sems[0], sems[1], after)
    return list(res)


def _remote(src, dst, send_sems, recv_sems, k, target):
    return pltpu.make_async_remote_copy(src_ref=src, dst_ref=dst, send_sem=send_sems.at[k],
                                        recv_sem=recv_sems.at[k], device_id=target, device_id_type=MESH)


def _ag_region(ref, axis, chip, half):
    K, N = ref.shape
    if axis == 0:
        hs = K // N_CHIPS // 2
        assert hs % 16 == 0
        return ref.at[pl.ds(pl.multiple_of((2 * chip + half) * hs, 16), hs), :]
    ns, hk = N // N_CHIPS, K // 2
    assert ns % LANES == 0 and hk % 16 == 0
    return ref.at[pl.ds(pl.multiple_of(half * hk, 16), hk), pl.ds(pl.multiple_of(chip * ns, LANES), ns)]


def _ag_copies(stage, axes):
    n = len(axes)

    def make(bufs, send_sems, recv_sems):
        x, y, c = _my_place()
        me = 2 * x + y
        sends, recvs = [], []
        for j, (px, py) in enumerate(_peer_chips(x, y)):
            other = 2 * px + py
            for w in range(n):
                k = j * n + w
                if stage == 1:
                    src, target = _ag_region(bufs[w], axes[w], me, c), (px, py, c)
                    land = _ag_region(bufs[w], axes[w], other, c)
                else:
                    src, target = _ag_region(bufs[w], axes[w], other, c), (x, y, 1 - c)
                    land = _ag_region(bufs[w], axes[w], other, 1 - c)
                sends.append(_remote(src, src, send_sems, recv_sems, k, target))
                recvs.append(_remote(land, land, send_sems, recv_sems, k, target))
        return sends, recvs

    return make


def _half_shape(shape, axis):
    K, N = shape
    return (K, N // 2) if axis == 0 else (K // 2, N)


def _core_half(ref, axis, half):
    K, N = ref.shape
    if axis == 0:
        return ref.at[:, pl.ds(pl.multiple_of(half * (N // 2), LANES), N // 2)]
    return ref.at[pl.ds(pl.multiple_of(half * (K // 2), 16), K // 2), :]


def _chip_block(ref, axis, chip):
    K, N = ref.shape
    if axis == 0:
        return ref.at[pl.ds(pl.multiple_of(chip * (K // N_CHIPS), 16), K // N_CHIPS), :]
    return ref.at[:, pl.ds(pl.multiple_of(chip * (N // N_CHIPS), LANES), N // N_CHIPS)]


def _rs_sibling_copies(axes):
    n = len(axes)

    def make(bufs, send_sems, recv_sems):
        x, y, c = _my_place()
        sends = [_remote(_core_half(bufs[w], axes[w], 1 - c), bufs[n + w], send_sems, recv_sems, w, (x, y, 1 - c))
                 for w in range(n)]
        recvs = [_remote(bufs[n + w], bufs[n + w], send_sems, recv_sems, w, (x, y, 1 - c)) for w in range(n)]
        return sends, recvs

    return make


def _rs_chip_copies(axes):
    n = len(axes)

    def make(bufs, send_sems, recv_sems):
        x, y, c = _my_place()
        sends, recvs = [], []
        for j, (px, py) in enumerate(_peer_chips(x, y)):
            for w in range(n):
                k = j * n + w
                sends.append(_remote(_chip_block(bufs[w], axes[w], 2 * px + py), bufs[n + w].at[j],
                                     send_sems, recv_sems, k, (px, py, c)))
                recvs.append(_remote(bufs[n + w].at[j], bufs[n + w].at[j], send_sems, recv_sems, k, (px, py, c)))
        return sends, recvs

    return make


def _rs_fill_copies(axes):
    n = len(axes)

    def make(bufs, send_sems, recv_sems):
        x, y, c = _my_place()
        sends = [_remote(_core_half(bufs[w], axes[w], c), _core_half(bufs[w], axes[w], c),
                         send_sems, recv_sems, w, (x, y, 1 - c)) for w in range(n)]
        recvs = [_remote(_core_half(bufs[w], axes[w], 1 - c), _core_half(bufs[w], axes[w], 1 - c),
                         send_sems, recv_sems, w, (x, y, 1 - c)) for w in range(n)]
        return sends, recvs

    return make


def _chip_sum(g, r, axis, place, *, name):
    hk, hn = r.shape
    bk, bn = (hk // N_CHIPS, hn) if axis == 0 else (hk, hn // N_CHIPS)
    tk = _pick(bk, (512, 352, 256, 128))
    nk = bk // tk

    def body(p_ref, g_ref, r_ref, b_ref, own_ref):
        s = g_ref[...].astype(F32) + r_ref[...].astype(F32)
        b_ref[...] = s.astype(BF16)

        @pl.when(pl.program_id(1) == p_ref[0])
        def _():
            own_ref[...] = s

    if axis == 0:
        g_spec = pl.BlockSpec((tk, bn), lambda i, j, p: (j * nk + i, p[1]))
        r_spec = pl.BlockSpec((tk, bn), lambda i, j, p: (j * nk + i, 0))
    else:
        g_spec = pl.BlockSpec((tk, bn), lambda i, j, p: (p[1] * nk + i, j))
        r_spec = pl.BlockSpec((tk, bn), lambda i, j, p: (i, j))
    grid_spec = pltpu.PrefetchScalarGridSpec(
        num_scalar_prefetch=1, grid=(nk, N_CHIPS), in_specs=[g_spec, r_spec],
        out_specs=[r_spec, pl.BlockSpec((tk, bn), lambda i, j, p: (i, 0))])
    return pl.pallas_call(
        body, name=name,
        out_shape=[jax.ShapeDtypeStruct(r.shape, BF16), jax.ShapeDtypeStruct((bk, bn), F32)],
        grid_spec=grid_spec,
        compiler_params=pltpu.CompilerParams(dimension_semantics=("arbitrary", "arbitrary"),
                                             vmem_limit_bytes=_vmem(2 * tk * bn * 10 + 3 * tk * bn * 4)),
    )(place, g, r)


def _final_sum(own, recv, axis, place, *, name):
    _, bk, bn = recv.shape
    tk = _pick(bk, (256, 176, 128))
    nk = bk // tk

    def body(p_ref, o_ref, r_ref, out_ref):
        out_ref[...] = ((o_ref[...] + r_ref[0].astype(F32)) + r_ref[1].astype(F32)) + r_ref[2].astype(F32)

    own_spec = pl.BlockSpec((tk, bn), lambda i, p: (i, 0))
    if axis == 0:
        out_shape, out_spec = (bk, 2 * bn), pl.BlockSpec((tk, bn), lambda i, p: (i, p[1]))
    else:
        out_shape, out_spec = (2 * bk, bn), pl.BlockSpec((tk, bn), lambda i, p: (p[1] * nk + i, 0))
    grid_spec = pltpu.PrefetchScalarGridSpec(
        num_scalar_prefetch=1, grid=(nk,),
        in_specs=[own_spec, pl.BlockSpec((3, tk, bn), lambda i, p: (0, i, 0))], out_specs=out_spec)
    return pl.pallas_call(
        body, name=name, out_shape=jax.ShapeDtypeStruct(out_shape, F32), grid_spec=grid_spec,
        compiler_params=pltpu.CompilerParams(dimension_semantics=("arbitrary",),
                                             vmem_limit_bytes=_vmem(2 * tk * bn * 14 + 4 * tk * bn * 4)),
    )(place, own, recv)


def _allreduce_small(p):
    def body(p_ref, o_ref, r0, r1, r2, send_sems, recv_sems):
        x, y, c = _my_place()
        o_ref[...] = p_ref[...]
        for s, (peer, rbuf) in enumerate([((x, y, 1 - c), r0), ((1 - x, y, c), r1), ((x, 1 - y, c), r2)]):
            cp = pltpu.make_async_remote_copy(src_ref=o_ref, dst_ref=rbuf, send_sem=send_sems.at[s],
                                              recv_sem=recv_sems.at[s], device_id=peer, device_id_type=MESH)
            cp.start()
            cp.wait()
            o_ref[...] = o_ref[...] + rbuf[...]

    vm = pl.BlockSpec(memory_space=pltpu.VMEM)
    return pl.pallas_call(
        body, name="allreduce_small", out_shape=jax.ShapeDtypeStruct(p.shape, F32),
        in_specs=[vm], out_specs=vm,
        scratch_shapes=[pltpu.VMEM(p.shape, F32)] * 3 + [pltpu.SemaphoreType.DMA((3,))] * 2,
        compiler_params=pltpu.CompilerParams(vmem_limit_bytes=_vmem(6 * _nbytes(p.shape, F32))),
    )(p)


def _pack_rows(parts):
    rows, metas = [], []
    for a in parts:
        flat = a.reshape(-1)
        nrow = -(-flat.shape[0] // LANES)
        nrow = -(-nrow // 8) * 8
        flat = jnp.pad(flat, (0, nrow * LANES - flat.shape[0]))
        rows.append(flat.reshape(nrow, LANES))
        metas.append((a.shape, nrow))
    return jnp.concatenate(rows, axis=0), metas


def _unpack_rows(packed, metas):
    out, r0 = [], 0
    for shape, nrow in metas:
        size = int(np.prod(shape))
        out.append(packed[r0:r0 + nrow].reshape(-1)[:size].reshape(shape))
        r0 += nrow
    return out


def kernel(x, positions, pre_mix_g, post_mix_g, pre_ffn_g, post_ffn_g, a_w_in, a_b_in, a_ln_g, a_ln_b, a_w_s, a_b_s, a_w_out, b_w_qkv, b_b_qkv, b_sinks, b_w_o, ffn_w_gu, ffn_w_down, loss_target, m_pre_mix_g, m_post_mix_g, m_pre_ffn_g, m_post_ffn_g, m_a_w_in, m_a_b_in, m_a_ln_g, m_a_ln_b, m_a_w_s, m_a_b_s, m_a_w_out, m_b_w_qkv, m_b_b_qkv, m_b_sinks, m_b_w_o, m_ffn_w_gu, m_ffn_w_down, v_pre_mix_g, v_post_mix_g, v_pre_ffn_g, v_post_ffn_g, v_a_w_in, v_a_b_in, v_a_ln_g, v_a_ln_b, v_a_w_s, v_a_b_s, v_a_w_out, v_b_w_qkv, v_b_b_qkv, v_b_sinks, v_b_w_o, v_ffn_w_gu, v_ffn_w_down):
    depth, D = pre_mix_g.shape
    xi, yi, ci = _my_place()
    chip = 2 * xi + yi
    place = jnp.stack([chip, ci]).astype(jnp.int32)

    stacked = {"a_w_in": (a_w_in, m_a_w_in, v_a_w_in), "a_w_out": (a_w_out, m_a_w_out, v_a_w_out),
               "b_w_qkv": (b_w_qkv, m_b_w_qkv, v_b_w_qkv), "b_w_o": (b_w_o, m_b_w_o, v_b_w_o),
               "ffn_w_gu": (ffn_w_gu, m_ffn_w_gu, v_ffn_w_gu), "ffn_w_down": (ffn_w_down, m_ffn_w_down, v_ffn_w_down)}
    cut = {"a_w_in": 1, "a_w_out": 0, "b_w_qkv": 1, "b_w_o": 0, "ffn_w_gu": 1, "ffn_w_down": 0}

    def layer_keys(i):
        mix = [("a_w_in", i // 2), ("a_w_out", i // 2)] if i % 2 == 0 else [("b_w_qkv", i // 2), ("b_w_o", i // 2)]
        return mix + [("ffn_w_gu", i), ("ffn_w_down", i)]

    def dep(a, toks):
        for t in toks:
            a = a + t[:1, :1]
        return a

    W = {}
    for i in range(depth):
        for nm, l in layer_keys(i):
            W[(nm, l)] = _cast_block(stacked[nm][0], l, cut[nm], place, name=f"cast_{nm}_{l}")

    def gather(tag, keys, after):
        axes = [cut[nm] for nm, _ in keys]
        for stage in (1, 2):
            ss, rs, bufs, tok = _split_start(f"ag{stage}_start_{tag}", [W[k] for k in keys], 3 * len(keys),
                                             _ag_copies(stage, axes), after)
            after = yield tok
            bufs = _split_wait(f"ag{stage}_wait_{tag}", bufs, (ss, rs), _ag_copies(stage, axes), after)
            W.update(zip(keys, bufs))
        yield None

    nq = b_b_qkv.shape[1]
    bq_full = jnp.zeros((b_b_qkv.shape[0], N_CHIPS * nq), F32)
    bq_full = lax.dynamic_update_slice(bq_full, jnp.where(ci == 0, b_b_qkv, 0.0), (0, chip * nq))
    bq_packed, bq_meta = _pack_rows([bq_full])
    bq_gathered = _allreduce_small(bq_packed)
    b_qkv_full = _unpack_rows(bq_gathered, bq_meta)[0]

    first = gather("0m", layer_keys(0)[:2], bq_gathered)
    tok = next(first)
    tok = first.send(tok)
    first.send(tok)

    h = x[0]
    target = loss_target[0]
    ctab, stab = _rope_tables(positions[0])
    q_width = W[("b_w_o", 0)].shape[0]
    kv_width = N_KV_HEADS * HEAD_DIM
    row = lambda a, i: a[i:i + 1]

    saved = []
    hn = None
    for i in range(depth):
        j = i // 2
        s = {"h": h}
        ffn_w = None
        if i == 0:
            ffn_w = gather("0f", layer_keys(0)[2:], W[("a_w_out", 0)])
            toks = [next(ffn_w)]
            nxt = gather("1", layer_keys(1), toks[0])
            toks.append(next(nxt))
            hn = _rms_fwd(h, dep(row(pre_mix_g, i), toks), out_dtype=BF16, name=f"rms_pre_mix_{i}")
        elif i + 1 < depth:
            nxt = gather(str(i + 1), layer_keys(i + 1), h)
            toks = [next(nxt)]
        else:
            toks = []
        s["hn"] = hn
        if i % 2 == 0:
            pre = _matmul(hn, W[("a_w_in", j)], mode="nn", bias=dep(row(a_b_in, j), toks), out_dtype=F32,
                          name=f"gmlp_in_{i}")
            gated = _sgu_fwd(pre, row(a_ln_g, j), row(a_ln_b, j), a_w_s[j], a_b_s[j].T, name=f"sgu_fwd_{i}")
            mix = _matmul(gated, W[("a_w_out", j)], mode="nn", out_dtype=F32, name=f"gmlp_out_{i}")
            s.update(pre=pre, gated=gated)
        else:
            qkv = _matmul(hn, W[("b_w_qkv", j)], mode="nn", bias=dep(row(b_qkv_full, j), toks), out_dtype=F32,
                          name=f"attn_qkv_{i}")
            qr, kr, vr = _rope_fwd(qkv, ctab, stab, q_width=q_width, kv_width=kv_width, name=f"rope_fwd_{i}")
            o = _attn_fwd(qr, kr, vr, row(b_sinks, j), name=f"attn_fwd_{i}")
            mix = _matmul(o, W[("b_w_o", j)], mode="nn", out_dtype=F32, name=f"attn_o_{i}")
            s.update(qr=qr, kr=kr, vr=vr, o=o)
        s["mix"] = mix
        toks = [ffn_w.send(mix)] if ffn_w else []
        h1, fn = _rms_res_norm(h, mix, dep(row(post_mix_g, i), toks), row(pre_ffn_g, i), name=f"rms_post_mix_{i}")
        if ffn_w:
            ffn_w.send(h1)
        s["h1"] = h1
        g_pre, u_pre, act = _ffn_up(fn, W[("ffn_w_gu", i)][None], 0, name=f"ffn_up_{i}")
        f = _matmul(act, W[("ffn_w_down", i)], mode="nn", out_dtype=F32, name=f"ffn_down_{i}")
        if i + 1 < depth:
            toks = [nxt.send(f)]
            h, hn = _rms_res_norm(h1, f, dep(row(post_ffn_g, i), toks), row(pre_mix_g, i + 1),
                                  name=f"rms_post_ffn_{i}")
            nxt.send(h)
        else:
            h = _rms_res(h1, f, row(post_ffn_g, i), name=f"rms_post_ffn_{i}")
        s.update(fn=fn, g_pre=g_pre, u_pre=u_pre, act=act, f=f)
        saved.append(s)

    dh, df, loss_part, g_last = _loss_and_grad(h, target, saved[-1]["f"], row(post_ffn_g, depth - 1), name="loss")
    loss = lax.psum(loss_part[0, 0], ("x", "y", "c"))

    big_out = {nm: tuple(lax.empty(w.shape, F32) for _ in range(4)) for nm, (w, _, _) in stacked.items()}

    def reduce_group(i, keys, grads):
        axes = [cut[nm] for nm, _ in keys]
        n = len(keys)
        lands = [lax.empty(_half_shape(g.shape, ax), BF16) for g, ax in zip(grads, axes)]
        ss, rs, bufs, tok = _split_start(f"rs_sibling_start_{i}", list(grads) + lands, n, _rs_sibling_copies(axes),
                                         place)
        after = yield tok
        bufs = _split_wait(f"rs_sibling_wait_{i}", bufs, (ss, rs), _rs_sibling_copies(axes), after)
        sums = [_chip_sum(bufs[w], bufs[n + w], axes[w], place, name=f"chip_sum_{keys[w][0]}_{keys[w][1]}")
                for w in range(n)]
        lands = [lax.empty((3,) + own.shape, BF16) for _, own in sums]
        ss, rs, bufs, tok = _split_start(f"rs_chip_start_{i}", [sb for sb, _ in sums] + lands, 3 * n,
                                         _rs_chip_copies(axes), place)
        after = yield tok
        bufs = _split_wait(f"rs_chip_wait_{i}", bufs, (ss, rs), _rs_chip_copies(axes), after)
        blocks = [_final_sum(sums[w][1], bufs[n + w], axes[w], place, name=f"final_sum_{keys[w][0]}_{keys[w][1]}")
                  for w in range(n)]
        ss, rs, bufs, tok = _split_start(f"rs_fill_start_{i}", blocks, n, _rs_fill_copies(axes), place)
        after = yield tok
        blocks = _split_wait(f"rs_fill_wait_{i}", bufs, (ss, rs), _rs_fill_copies(axes), after)
        for (nm, l), g in zip(keys, blocks):
            w, m, v = stacked[nm]
            big_out[nm] = tuple(_adamw_layer(w, m, v, g, l, big_out[nm], name=f"adamw_{nm}_{l}"))
        yield None

    reducing = []

    def advance(after):
        toks = []
        for gen in list(reducing):
            tok = gen.send(after)
            if tok is None:
                reducing.remove(gen)
            else:
                toks.append(tok)
        return toks

    small = {}
    g_pre_mix, g_post_mix, g_pre_ffn, g_post_ffn = [None] * depth, [None] * depth, [None] * depth, [None] * depth
    g_post_ffn[depth - 1] = g_last
    toks = []
    for i in reversed(range(depth)):
        j = i // 2
        s = saved[i]
        g_down = _matmul(s["act"], df, mode="tn", out_dtype=BF16, after=toks, name=f"ffn_down_dw_{i}")
        dg_, du_ = _ffn_down_dx(df, W[("ffn_w_down", i)][None], 0, s["g_pre"], s["u_pre"], g_down,
                                name=f"ffn_down_dx_{i}")
        hid = dg_.shape[1]
        tile = _pick(hid, (1408, 768, 512, 256, 128))
        w_gu = W[("ffn_w_gu", i)]
        g_gu = lax.empty(w_gu.shape, BF16)
        g_gu = _matmul(s["fn"], dg_, mode="tn", into=g_gu, tq=tile, out_dtype=BF16, name=f"ffn_g_dw_{i}")
        g_gu = _matmul(s["fn"], du_, mode="tn", into=g_gu, tq=tile, q_off=hid // tile, out_dtype=BF16,
                       name=f"ffn_u_dw_{i}")
        dfn_g = _matmul(dg_, w_gu, mode="nt", tr=hid, out_dtype=F32, after=[g_gu], name=f"ffn_g_dx_{i}")
        dfn = _matmul(du_, w_gu, mode="nt", tr=hid, b_r_off=1, bias=dfn_g, out_dtype=F32, name=f"ffn_u_dx_{i}")
        toks = advance(dfn)
        if i == 0:
            gen = reduce_group("0f", layer_keys(0)[2:], [g_gu, g_down])
            toks.append(next(gen))
            reducing.append(gen)
        dh1, dmix, g_pre_ffn[i], g_post_mix[i] = _rms_bwd_chain(
            s["h1"], dep(row(pre_ffn_g, i), toks), dfn, dh, s["mix"], row(post_mix_g, i), name=f"rms_ffn_mix_bwd_{i}")
        if i % 2 == 0:
            g_out = _matmul(s["gated"], dmix, mode="tn", out_dtype=BF16, name=f"gmlp_out_dw_{i}")
            dgated = _matmul(dmix, W[("a_w_out", j)], mode="nt", out_dtype=F32, after=[g_out],
                             name=f"gmlp_out_dx_{i}")
            toks = advance(dgated) if i == 0 else []
            dpre, dws, dbsT, dlng, dlnb, dbin = _sgu_bwd(s["pre"], dgated, dep(row(a_ln_g, j), toks), row(a_ln_b, j),
                                                         a_w_s[j], a_b_s[j].T, name=f"sgu_bwd_{i}")
            small[("a_w_s", j)] = dws
            small[("a_b_s", j)] = dbsT.T
            small[("a_ln_g", j)] = dlng
            small[("a_ln_b", j)] = dlnb
            small[("a_b_in", j)] = dbin
            g_in = _matmul(s["hn"], dpre, mode="tn", out_dtype=BF16, name=f"gmlp_in_dw_{i}")
            dhn = _matmul(dpre, W[("a_w_in", j)], mode="nt", out_dtype=F32, after=[g_in], name=f"gmlp_in_dx_{i}")
        else:
            g_out = _matmul(s["o"], dmix, mode="tn", out_dtype=BF16, name=f"attn_o_dw_{i}")
            do = _matmul(dmix, W[("b_w_o", j)], mode="nt", out_dtype=BF16, after=[g_out], name=f"attn_o_dx_{i}")
            dq, dkp, dkc, dvp, dvc, dsk = _attn_bwd(s["qr"], s["kr"], s["vr"], row(b_sinks, j), do,
                                                    name=f"attn_bwd_{i}")
            dqkv, dbq = _rope_bwd(dq, dkp, dkc, dvp, dvc, ctab, stab, name=f"rope_bwd_{i}")
            small[("b_sinks", j)] = dsk[:, :b_sinks.shape[1]]
            small[("b_b_qkv", j)] = dbq
            g_in = _matmul(s["hn"], dqkv, mode="tn", out_dtype=BF16, name=f"attn_qkv_dw_{i}")
            dhn = _matmul(dqkv, W[("b_w_qkv", j)], mode="nt", out_dtype=F32, after=[g_in], name=f"attn_qkv_dx_{i}")
        toks = advance(dhn)
        if i > 0:
            dh, df, g_pre_mix[i], g_post_ffn[i - 1] = _rms_bwd_chain(
                s["h"], dep(row(pre_mix_g, i), toks), dhn, dh1, saved[i - 1]["f"], row(post_ffn_g, i - 1),
                name=f"rms_mix_ffn_bwd_{i}")
        else:
            dh, g_pre_mix[i] = _rms_bwd(s["h"], dep(row(pre_mix_g, i), toks), dhn, dh1, out_dtype=F32,
                                        name=f"rms_pre_mix_bwd_{i}")
        if i == 0:
            gen = reduce_group("0m", layer_keys(0)[:2], [g_in, g_out])
        else:
            gen = reduce_group(str(i), layer_keys(i), [g_in, g_out, g_gu, g_down])
        toks = [next(gen)] + advance(dh)
        reducing.append(gen)
    grad_x = dh[None]

    toks = advance(dh)
    n_a, n_b = a_b_in.shape[0], b_sinks.shape[0]
    stack = lambda key, n: jnp.concatenate([small[(key, j)] for j in range(n)], axis=0)
    small_parts = [
        jnp.concatenate(g_pre_mix, axis=0), jnp.concatenate(g_post_mix, axis=0),
        jnp.concatenate(g_pre_ffn, axis=0), jnp.concatenate(g_post_ffn, axis=0),
        stack("a_b_in", n_a), stack("a_ln_g", n_a), stack("a_ln_b", n_a),
        jnp.stack([small[("a_w_s", j)] for j in range(n_a)]), jnp.stack([small[("a_b_s", j)] for j in range(n_a)]),
        stack("b_b_qkv", n_b), stack("b_sinks", n_b),
    ]
    packed, metas = _pack_rows(small_parts)
    reduced = _allreduce_small(dep(packed, toks))
    while reducing:
        advance(reduced)
    red = _unpack_rows(reduced, metas)
    (gr_pre_mix, gr_post_mix, gr_pre_ffn, gr_post_ffn, gr_b_in, gr_ln_g, gr_ln_b, gr_w_s, gr_b_s,
     gr_b_qkv_full, gr_sinks) = red
    gr_b_qkv = lax.dynamic_slice(gr_b_qkv_full, (0, chip * nq), (gr_b_qkv_full.shape[0], nq))

    grads = {"pre_mix_g": gr_pre_mix, "post_mix_g": gr_post_mix, "pre_ffn_g": gr_pre_ffn, "post_ffn_g": gr_post_ffn,
             "a_b_in": gr_b_in, "a_ln_g": gr_ln_g, "a_ln_b": gr_ln_b, "a_w_s": gr_w_s, "a_b_s": gr_b_s,
             "b_b_qkv": gr_b_qkv, "b_sinks": gr_sinks}
    weights = {"pre_mix_g": (pre_mix_g, m_pre_mix_g, v_pre_mix_g), "post_mix_g": (post_mix_g, m_post_mix_g, v_post_mix_g),
               "pre_ffn_g": (pre_ffn_g, m_pre_ffn_g, v_pre_ffn_g), "post_ffn_g": (post_ffn_g, m_post_ffn_g, v_post_ffn_g),
               "a_b_in": (a_b_in, m_a_b_in, v_a_b_in), "a_ln_g": (a_ln_g, m_a_ln_g, v_a_ln_g),
               "a_ln_b": (a_ln_b, m_a_ln_b, v_a_ln_b), "a_w_s": (a_w_s, m_a_w_s, v_a_w_s), "a_b_s": (a_b_s, m_a_b_s, v_a_b_s),
               "b_b_qkv": (b_b_qkv, m_b_b_qkv, v_b_b_qkv), "b_sinks": (b_sinks, m_b_sinks, v_b_sinks)}
    order = ["pre_mix_g", "post_mix_g", "pre_ffn_g", "post_ffn_g", "a_w_in", "a_b_in", "a_ln_g", "a_ln_b", "a_w_s",
             "a_b_s", "a_w_out", "b_w_qkv", "b_b_qkv", "b_sinks", "b_w_o", "ffn_w_gu", "ffn_w_down"]
    deltas, new_m, new_v = {}, {}, {}
    for nm in order:
        if nm in big_out:
            grads[nm], deltas[nm], new_m[nm], new_v[nm] = big_out[nm]
        else:
            w, m, v = weights[nm]
            deltas[nm], new_m[nm], new_v[nm] = _adamw_small(w, grads[nm], m, v, name="adamw_" + nm)
    return (loss, grad_x, *[grads[nm] for nm in order], *[deltas[nm] for nm in order],
            *[new_m[nm] for nm in order], *[new_v[nm] for nm in order])
```

```python
import functools
import math

import jax
import jax.numpy as jnp
import numpy as np
from jax import lax
from jax.experimental import pallas as pl
from jax.experimental.pallas import tpu as pltpu

F32 = jnp.float32
BF16 = jnp.bfloat16
MESH = pl.DeviceIdType.MESH

HEAD_DIM = 64
N_KV_HEADS = 4
ROPE_DIM = 16
ROPE_THETA = 500000.0
CHUNK = 128
GMLP_GROUPS = 8
RMS_EPS = 1e-6
LN_EPS = 1e-5
NEG_INF = -1e30
ADAM_LR = 0.001
ADAM_B1 = 0.9
ADAM_B2 = 0.999
ADAM_EPS = 1e-08
ADAM_WD = 0.01
ADAM_STEP = 10

N_CHIPS = 4
LANES = 128
VMEM_CAP = 58 * 1024 * 1024


def _vmem(est_bytes):
    assert est_bytes < VMEM_CAP
    return VMEM_CAP


def _pick(n, cands):
    for c in cands:
        if c <= n and n % c == 0:
            return c
    return n


def _nbytes(shape, dtype):
    return int(np.prod(shape)) * jnp.dtype(dtype).itemsize


MATMUL_VMEM_BUDGET = 48 * 1024 * 1024


def _halvings(n, unit):
    out, t = [], n
    while t % unit == 0 and t >= unit:
        out.append(t)
        if t % 2:
            break
        t //= 2
    return out


def _matmul_tiles(P, Q, R, a_bytes, b_bytes, o_bytes, full_addend, tp, tq, tr):
    step_us, bytes_per_us = 0.85, 3.2e6
    best = None
    for p in ([tp] if tp else _halvings(P, LANES)):
        for q in ([tq] if tq else _halvings(Q, LANES)):
            for r in ([tr] if tr else _halvings(R, LANES)):
                nk = R // r
                vm = 2 * (p * r * a_bytes + r * q * b_bytes + p * q * o_bytes + (p * q * 4 if full_addend else 0))
                vm += p * q * 4 * (2 if nk > 1 else 1)
                if vm > MATMUL_VMEM_BUDGET:
                    continue
                exposed = (p * r * a_bytes + r * q * b_bytes + p * q * o_bytes) / bytes_per_us
                key = ((P // p) * (Q // q) * nk * step_us + exposed, nk, abs(p - q))
                if best is None or key < best[0]:
                    best = (key, (p, q, r))
    assert best is not None, (P, Q, R)
    return best[1]


def _matmul(a, b, *, mode, out_dtype, name, a_l=None, b_l=None, bias=None, into=None, o_l=None,
            q_off=0, b_r_off=0, tp=None, tq=None, tr=None, after=()):
    a2 = a.shape[-2:]
    b2 = b.shape[-2:]
    if mode == "nn":
        (P, R), (R2, Q) = a2, b2
    elif mode == "nt":
        (P, R), (Q, R2) = a2, b2
    else:
        (R, P), (R2, Q) = a2, b2
    assert R == R2 or (mode == "nt" and R2 % R == 0), (mode, a.shape, b.shape)
    o_bytes = jnp.dtype(into.dtype if into is not None else out_dtype).itemsize
    full_addend = bias is not None and bias.shape[0] != 1
    tp, tq, tr = _matmul_tiles(P, Q, R, a.dtype.itemsize, b.dtype.itemsize, o_bytes, full_addend, tp, tq, tr)
    assert P % tp == 0 and Q % tq == 0 and R % tr == 0
    nk = R // tr
    dims = {"nn": (((1,), (0,)), ((), ())), "nt": (((1,), (1,)), ((), ())), "tn": (((0,), (0,)), ((), ()))}[mode]

    def lead(l, blk, idx):
        if l is None:
            return pl.BlockSpec(blk, idx)
        return pl.BlockSpec((None,) + blk, lambda i, j, k: (l,) + idx(i, j, k))

    if mode == "nn":
        a_spec = lead(a_l, (tp, tr), lambda i, j, k: (i, k))
        b_spec = lead(b_l, (tr, tq), lambda i, j, k: (k, j))
    elif mode == "nt":
        a_spec = lead(a_l, (tp, tr), lambda i, j, k: (i, k))
        b_spec = lead(b_l, (tq, tr), lambda i, j, k: (j, k + b_r_off))
    else:
        a_spec = lead(a_l, (tr, tp), lambda i, j, k: (k, i))
        b_spec = lead(b_l, (tr, tq), lambda i, j, k: (k, j))
    in_specs = [a_spec, b_spec]
    args = [a, b]
    if bias is not None:
        if bias.shape[0] == 1:
            in_specs.append(pl.BlockSpec((1, tq), lambda i, j, k: (0, j)))
        else:
            in_specs.append(pl.BlockSpec((tp, tq), lambda i, j, k: (i, j)))
        args.append(bias)
    aliases = {}
    if into is not None:
        in_specs.append(pl.BlockSpec(memory_space=pl.ANY))
        args.append(into)
        aliases = {len(args) - 1: 0}
        out_shape = jax.ShapeDtypeStruct(into.shape, into.dtype)
        out_dtype = into.dtype
        if o_l is None:
            out_spec = pl.BlockSpec((tp, tq), lambda i, j, k: (i, j + q_off))
        else:
            out_spec = pl.BlockSpec((None, tp, tq), lambda i, j, k: (o_l, i, j + q_off))
    else:
        out_shape = jax.ShapeDtypeStruct((P, Q), out_dtype)
        out_spec = pl.BlockSpec((tp, tq), lambda i, j, k: (i, j))
    n_in = len(args) + len(after)
    in_specs += [pl.BlockSpec(memory_space=pl.ANY)] * len(after)
    args += list(after)
    has_bias = bias is not None
    has_into = into is not None

    def body(*refs):
        a_ref, b_ref = refs[0], refs[1]
        pos = 2
        bias_ref = None
        if has_bias:
            bias_ref = refs[pos]
            pos += 1
        o_ref = refs[n_in]
        acc_ref = refs[n_in + 1] if nk > 1 else None
        part = lax.dot_general(a_ref[...], b_ref[...], dims, preferred_element_type=F32)

        def finish(acc):
            if has_bias:
                acc = acc + bias_ref[...]
            o_ref[...] = acc.astype(out_dtype)

        if nk == 1:
            finish(part)
        else:
            k = pl.program_id(2)

            @pl.when(k == 0)
            def _():
                acc_ref[...] = part

            @pl.when(k > 0)
            def _():
                acc_ref[...] += part

            @pl.when(k == nk - 1)
            def _():
                finish(acc_ref[...])

    est = 2 * (_nbytes((tp, tr), a.dtype) + _nbytes((tr, tq), b.dtype) + _nbytes((tp, tq), out_dtype)) + 3 * tp * tq * 4
    return pl.pallas_call(
        body, name=name, out_shape=out_shape,
        grid=(P // tp, Q // tq, nk),
        in_specs=in_specs, out_specs=out_spec,
        scratch_shapes=[pltpu.VMEM((tp, tq), F32)] if nk > 1 else [],
        input_output_aliases=aliases,
        compiler_params=pltpu.CompilerParams(
            dimension_semantics=("parallel", "parallel", "arbitrary"), vmem_limit_bytes=_vmem(est)),
    )(*args)


def _row_call(body, ins, outs, *, name, rows, tr, acc_outs=(), est=0):
    in_specs = []
    for arr, kind in ins:
        if kind == "row":
            in_specs.append(pl.BlockSpec((tr, arr.shape[1]), lambda i: (i, 0)))
        else:
            nd = arr.ndim
            in_specs.append(pl.BlockSpec(arr.shape, lambda i, nd=nd: (0,) * nd))
    out_shapes = [jax.ShapeDtypeStruct(s, d) for s, d in outs] + [jax.ShapeDtypeStruct(s, d) for s, d in acc_outs]
    out_specs = [pl.BlockSpec((tr, s[1]), lambda i: (i, 0)) for s, _ in outs]
    out_specs += [pl.BlockSpec(s, lambda i, nd=len(s): (0,) * nd) for s, _ in acc_outs]
    res = pl.pallas_call(
        body, name=name, out_shape=out_shapes, grid=(rows // tr,), in_specs=in_specs, out_specs=out_specs,
        compiler_params=pltpu.CompilerParams(dimension_semantics=("arbitrary",), vmem_limit_bytes=_vmem(est)),
    )(*[a for a, _ in ins])
    return res


def _rms_fwd(x, g, *, out_dtype, name):
    T, D = x.shape
    tr = _pick(T, (512, 256, 128))

    def body(x_ref, g_ref, o_ref):
        xv = x_ref[...]
        r = lax.rsqrt(jnp.mean(xv * xv, axis=-1, keepdims=True) + RMS_EPS)
        o_ref[...] = (xv * r * g_ref[...]).astype(out_dtype)

    return _row_call(body, [(x, "row"), (g, "full")], [((T, D), out_dtype)], name=name, rows=T, tr=tr,
                     est=8 * tr * D * 4)[0]


def _rms_res(h, y, g, *, name):
    T, D = h.shape
    tr = _pick(T, (512, 256, 128))

    def body(h_ref, y_ref, g_ref, o_ref):
        yv = y_ref[...]
        r = lax.rsqrt(jnp.mean(yv * yv, axis=-1, keepdims=True) + RMS_EPS)
        o_ref[...] = h_ref[...] + yv * r * g_ref[...]

    return _row_call(body, [(h, "row"), (y, "row"), (g, "full")], [((T, D), F32)], name=name, rows=T, tr=tr,
                     est=10 * tr * D * 4)[0]


def _rms_bwd(x, g, dy, dres, *, out_dtype, name):
    T, D = x.shape
    tr = _pick(T, (512, 256, 128))
    has_res = dres is not None

    def body(*refs):
        if has_res:
            x_ref, g_ref, dy_ref, dr_ref, dx_ref, dg_ref = refs
        else:
            x_ref, g_ref, dy_ref, dx_ref, dg_ref = refs
        xv = x_ref[...]
        r = lax.rsqrt(jnp.mean(xv * xv, axis=-1, keepdims=True) + RMS_EPS)
        xhat = xv * r
        dyv = dy_ref[...].astype(F32)
        dxn = dyv * g_ref[...]
        dx = r * (dxn - xhat * jnp.mean(dxn * xhat, axis=-1, keepdims=True))
        if has_res:
            dx = dx + dr_ref[...]
        dx_ref[...] = dx.astype(out_dtype)
        part = jnp.sum(dyv * xhat, axis=0, keepdims=True)

        @pl.when(pl.program_id(0) == 0)
        def _():
            dg_ref[...] = part

        @pl.when(pl.program_id(0) > 0)
        def _():
            dg_ref[...] += part

    ins = [(x, "row"), (g, "full"), (dy, "row")] + ([(dres, "row")] if has_res else [])
    dx, dg = _row_call(body, ins, [((T, D), out_dtype)], name=name, rows=T, tr=tr, acc_outs=[((1, D), F32)],
                       est=12 * tr * D * 4)
    return dx, dg


def _rms_res_norm(h, y, g_res, g_next, *, name):
    T, D = h.shape
    tr = _pick(T, (512, 256, 128))

    def body(h_ref, y_ref, g_ref, gn_ref, o_ref, n_ref):
        yv = y_ref[...]
        r = lax.rsqrt(jnp.mean(yv * yv, axis=-1, keepdims=True) + RMS_EPS)
        h2 = h_ref[...] + yv * r * g_ref[...]
        o_ref[...] = h2
        r2 = lax.rsqrt(jnp.mean(h2 * h2, axis=-1, keepdims=True) + RMS_EPS)
        n_ref[...] = (h2 * r2 * gn_ref[...]).astype(BF16)

    return _row_call(body, [(h, "row"), (y, "row"), (g_res, "full"), (g_next, "full")],
                     [((T, D), F32), ((T, D), BF16)], name=name, rows=T, tr=tr, est=12 * tr * D * 4)


def _rms_bwd_chain(x1, g1, dy1, dres, x2, g2, *, name):
    T, D = x1.shape
    tr = _pick(T, (512, 256, 128))

    def one(xv, gv, dyv):
        r = lax.rsqrt(jnp.mean(xv * xv, axis=-1, keepdims=True) + RMS_EPS)
        xhat = xv * r
        dxn = dyv * gv
        dx = r * (dxn - xhat * jnp.mean(dxn * xhat, axis=-1, keepdims=True))
        return dx, jnp.sum(dyv * xhat, axis=0, keepdims=True)

    def body(x1_ref, g1_ref, dy1_ref, dr_ref, x2_ref, g2_ref, d1_ref, d2_ref, dg1_ref, dg2_ref):
        dx1, p1 = one(x1_ref[...], g1_ref[...], dy1_ref[...].astype(F32))
        d1 = dx1 + dr_ref[...]
        d1_ref[...] = d1
        dx2, p2 = one(x2_ref[...], g2_ref[...], d1)
        d2_ref[...] = dx2.astype(BF16)

        @pl.when(pl.program_id(0) == 0)
        def _():
            dg1_ref[...] = p1
            dg2_ref[...] = p2

        @pl.when(pl.program_id(0) > 0)
        def _():
            dg1_ref[...] += p1
            dg2_ref[...] += p2

    ins = [(x1, "row"), (g1, "full"), (dy1, "row"), (dres, "row"), (x2, "row"), (g2, "full")]
    return _row_call(body, ins, [((T, D), F32), ((T, D), BF16)], name=name, rows=T, tr=tr,
                     acc_outs=[((1, D), F32), ((1, D), F32)], est=20 * tr * D * 4)


def _ffn_up(fn, w_gu, l, *, name):
    T, D = fn.shape
    H = w_gu.shape[2] // 2
    tp = _pick(T, (1024, 512, 256, 128))
    tq = _pick(H, (1408, 768, 512, 256, 128))
    nj = H // tq

    def body(a_ref, wg_ref, wu_ref, g_ref, u_ref, act_ref):
        a = a_ref[...]
        g = jnp.dot(a, wg_ref[...], preferred_element_type=F32)
        u = jnp.dot(a, wu_ref[...], preferred_element_type=F32)
        sg = jax.nn.sigmoid(g)
        silu = g * sg
        g_ref[...] = (u * (sg + silu * (1.0 - sg))).astype(BF16)
        u_ref[...] = silu.astype(BF16)
        act_ref[...] = (silu * u).astype(BF16)

    tile = pl.BlockSpec((tp, tq), lambda j, i: (i, j))
    est = 2 * (tp * D * 2 + 2 * D * tq * 2 + 3 * tp * tq * 2) + 4 * tp * tq * 4
    return pl.pallas_call(
        body, name=name,
        out_shape=[jax.ShapeDtypeStruct((T, H), BF16), jax.ShapeDtypeStruct((T, H), BF16),
                   jax.ShapeDtypeStruct((T, H), BF16)],
        grid=(nj, T // tp),
        in_specs=[pl.BlockSpec((tp, D), lambda j, i: (i, 0)),
                  pl.BlockSpec((None, D, tq), lambda j, i: (l, 0, j)),
                  pl.BlockSpec((None, D, tq), lambda j, i: (l, 0, j + nj))],
        out_specs=[tile, tile, tile],
        compiler_params=pltpu.CompilerParams(dimension_semantics=("parallel", "parallel"),
                                             vmem_limit_bytes=_vmem(est)),
    )(fn, w_gu, w_gu)


def _ffn_down_dx(df, w_down, l, g, u, after, *, name):
    T, D = df.shape
    H = w_down.shape[1]
    tp = _pick(T, (1024, 512, 256, 128))
    tq = _pick(H, (1408, 768, 512, 256, 128))

    def body(a_ref, w_ref, g_ref, u_ref, _, dg_ref, du_ref):
        da = lax.dot_general(a_ref[...], w_ref[...], (((1,), (1,)), ((), ())), preferred_element_type=F32)
        dg_ref[...] = (da * g_ref[...].astype(F32)).astype(BF16)
        du_ref[...] = (da * u_ref[...].astype(F32)).astype(BF16)

    tile = pl.BlockSpec((tp, tq), lambda j, i: (i, j))
    est = 2 * (tp * D * 2 + tq * D * 2 + 4 * tp * tq * 2) + 3 * tp * tq * 4
    return pl.pallas_call(
        body, name=name,
        out_shape=[jax.ShapeDtypeStruct((T, H), BF16), jax.ShapeDtypeStruct((T, H), BF16)],
        grid=(H // tq, T // tp),
        in_specs=[pl.BlockSpec((tp, D), lambda j, i: (i, 0)),
                  pl.BlockSpec((None, tq, D), lambda j, i: (l, j, 0)), tile, tile,
                  pl.BlockSpec(memory_space=pl.ANY)],
        out_specs=[tile, tile],
        compiler_params=pltpu.CompilerParams(dimension_semantics=("parallel", "parallel"),
                                             vmem_limit_bytes=_vmem(est)),
    )(df, w_down, g, u, after)


def _loss_and_grad(y, target, x, g, *, name):
    T, D = y.shape
    tr = _pick(T, (512, 256, 128))

    def body(y_ref, t_ref, x_ref, g_ref, dy_ref, dx_ref, l_ref, dg_ref):
        e = y_ref[...] - t_ref[...]
        dy = e * (1.0 / D)
        dy_ref[...] = dy
        part = jnp.sum(jnp.sum(e * e, axis=1, keepdims=True), axis=0, keepdims=True) * (0.5 / D)
        xv = x_ref[...]
        r = lax.rsqrt(jnp.mean(xv * xv, axis=-1, keepdims=True) + RMS_EPS)
        xhat = xv * r
        dxn = dy * g_ref[...]
        dx_ref[...] = (r * (dxn - xhat * jnp.mean(dxn * xhat, axis=-1, keepdims=True))).astype(BF16)
        dg = jnp.sum(dy * xhat, axis=0, keepdims=True)

        @pl.when(pl.program_id(0) == 0)
        def _():
            l_ref[...] = part
            dg_ref[...] = dg

        @pl.when(pl.program_id(0) > 0)
        def _():
            l_ref[...] += part
            dg_ref[...] += dg

    dy, dx, l, dg = _row_call(body, [(y, "row"), (target, "row"), (x, "row"), (g, "full")],
                              [((T, D), F32), ((T, D), BF16)], name=name, rows=T, tr=tr,
                              acc_outs=[((1, 1), F32), ((1, D), F32)], est=14 * tr * D * 4)
    return dy, dx, l, dg


_SQRT_HALF = 0.7071067811865476
_INV_SQRT_2PI = 0.3989422804014327


def _gelu_parts(x):
    cdf = 0.5 * (1.0 + lax.erf(x * _SQRT_HALF))
    return cdf


def _sgu_common(pre, lng, lnb, W):
    cdf = _gelu_parts(pre)
    z = pre * cdf
    u = z[:, :W]
    v = z[:, W:]
    mu = jnp.mean(v, axis=-1, keepdims=True)
    vc = v - mu
    var = jnp.mean(vc * vc, axis=-1, keepdims=True)
    rstd = lax.rsqrt(var + LN_EPS)
    vhat = vc * rstd
    vn = vhat * lng + lnb
    return cdf, u, vhat, rstd, vn


def _causal_mask():
    t = lax.broadcasted_iota(jnp.int32, (CHUNK, CHUNK), 0)
    s = lax.broadcasted_iota(jnp.int32, (CHUNK, CHUNK), 1)
    return t >= s


def _sgu_fwd(pre, lng, lnb, ws, bsT, *, name):
    T, W2 = pre.shape
    W = W2 // 2
    G = ws.shape[0]
    gd = W // G

    def body(pre_ref, lng_ref, lnb_ref, ws_ref, bs_ref, o_ref):
        _, u, _, _, vn = _sgu_common(pre_ref[...], lng_ref[...], lnb_ref[...], W)
        vnb = vn.astype(BF16)
        causal = _causal_mask()
        for g in range(G):
            w = jnp.where(causal, ws_ref[g], 0.0).astype(BF16)
            sv = jnp.dot(w, vnb[:, g * gd:(g + 1) * gd], preferred_element_type=F32) + bs_ref[:, g:g + 1]
            o_ref[:, g * gd:(g + 1) * gd] = (u[:, g * gd:(g + 1) * gd] * sv).astype(BF16)

    return pl.pallas_call(
        body, name=name, out_shape=jax.ShapeDtypeStruct((T, W), BF16), grid=(T // CHUNK,),
        in_specs=[pl.BlockSpec((CHUNK, W2), lambda i: (i, 0)),
                  pl.BlockSpec((1, W), lambda i: (0, 0)), pl.BlockSpec((1, W), lambda i: (0, 0)),
                  pl.BlockSpec(ws.shape, lambda i: (0, 0, 0)), pl.BlockSpec(bsT.shape, lambda i: (0, 0))],
        out_specs=pl.BlockSpec((CHUNK, W), lambda i: (i, 0)),
        compiler_params=pltpu.CompilerParams(dimension_semantics=("arbitrary",),
                                             vmem_limit_bytes=_vmem(12 * CHUNK * W2 * 4)),
    )(pre, lng, lnb, ws, bsT)


def _sgu_bwd(pre, dgated, lng, lnb, ws, bsT, *, name):
    T, W2 = pre.shape
    W = W2 // 2
    G = ws.shape[0]
    gd = W // G

    def body(pre_ref, dgt_ref, lng_ref, lnb_ref, ws_ref, bs_ref,
             dpre_ref, dws_ref, dbs_ref, dlng_ref, dlnb_ref, dbin_ref):
        first = pl.program_id(0) == 0

        @pl.when(first)
        def _():
            dws_ref[...] = jnp.zeros_like(dws_ref)
            dbs_ref[...] = jnp.zeros_like(dbs_ref)
            dlng_ref[...] = jnp.zeros_like(dlng_ref)
            dlnb_ref[...] = jnp.zeros_like(dlnb_ref)
            dbin_ref[...] = jnp.zeros_like(dbin_ref)

        pre_v = pre_ref[...]
        lng_v = lng_ref[...]
        cdf, u, vhat, rstd, vn = _sgu_common(pre_v, lng_v, lnb_ref[...], W)
        vnb = vn.astype(BF16)
        dgt = dgt_ref[...].astype(F32)
        causal = _causal_mask()
        du_parts, dvn_parts = [], []
        for g in range(G):
            sl = slice(g * gd, (g + 1) * gd)
            w = jnp.where(causal, ws_ref[g], 0.0).astype(BF16)
            sv = jnp.dot(w, vnb[:, sl], preferred_element_type=F32) + bs_ref[:, g:g + 1]
            dgt_g = dgt[:, sl]
            du_parts.append(dgt_g * sv)
            dsv = dgt_g * u[:, sl]
            dsvb = dsv.astype(BF16)
            dvn_parts.append(lax.dot_general(w, dsvb, (((0,), (0,)), ((), ())), preferred_element_type=F32))
            dw = lax.dot_general(dsvb, vnb[:, sl], (((1,), (1,)), ((), ())), preferred_element_type=F32)
            dws_ref[g] += jnp.where(causal, dw, 0.0)
            dbs_ref[:, g:g + 1] += jnp.sum(dsv, axis=1, keepdims=True)
        du = jnp.concatenate(du_parts, axis=1)
        dvn = jnp.concatenate(dvn_parts, axis=1)
        dlng_ref[...] += jnp.sum(dvn * vhat, axis=0, keepdims=True)
        dlnb_ref[...] += jnp.sum(dvn, axis=0, keepdims=True)
        dvh = dvn * lng_v
        dv = rstd * (dvh - jnp.mean(dvh, axis=-1, keepdims=True)
                     - vhat * jnp.mean(dvh * vhat, axis=-1, keepdims=True))
        dz = jnp.concatenate([du, dv], axis=1)
        dgelu = cdf + pre_v * jnp.exp(-0.5 * pre_v * pre_v) * _INV_SQRT_2PI
        dpre = dz * dgelu
        dbin_ref[...] += jnp.sum(dpre, axis=0, keepdims=True)
        dpre_ref[...] = dpre.astype(BF16)

    full = lambda shape: pl.BlockSpec(shape, lambda i, nd=len(shape): (0,) * nd)
    return pl.pallas_call(
        body, name=name,
        out_shape=[jax.ShapeDtypeStruct((T, W2), BF16), jax.ShapeDtypeStruct(ws.shape, F32),
                   jax.ShapeDtypeStruct(bsT.shape, F32), jax.ShapeDtypeStruct((1, W), F32),
                   jax.ShapeDtypeStruct((1, W), F32), jax.ShapeDtypeStruct((1, W2), F32)],
        grid=(T // CHUNK,),
        in_specs=[pl.BlockSpec((CHUNK, W2), lambda i: (i, 0)), pl.BlockSpec((CHUNK, W), lambda i: (i, 0)),
                  full((1, W)), full((1, W)), full(ws.shape), full(bsT.shape)],
        out_specs=[pl.BlockSpec((CHUNK, W2), lambda i: (i, 0)), full(ws.shape), full(bsT.shape),
                   full((1, W)), full((1, W)), full((1, W2))],
        compiler_params=pltpu.CompilerParams(dimension_semantics=("arbitrary",),
                                             vmem_limit_bytes=_vmem(24 * CHUNK * W2 * 4)),
    )(pre, dgated, lng, lnb, ws, bsT)


def _rope_tables(positions):
    half = ROPE_DIM // 2
    inv_freq = ROPE_THETA ** (-jnp.arange(0, ROPE_DIM, 2, dtype=F32) / ROPE_DIM)
    ang = positions.astype(F32).reshape(-1, 1) * inv_freq
    cos, sin = jnp.cos(ang), jnp.sin(ang)
    T = ang.shape[0]
    rest = HEAD_DIM - ROPE_DIM
    c64 = jnp.concatenate([cos, cos, jnp.ones((T, rest), F32)], axis=1)
    s64 = jnp.concatenate([-sin, sin, jnp.zeros((T, rest), F32)], axis=1)
    del half
    return jnp.tile(c64, (1, LANES // HEAD_DIM)), jnp.tile(s64, (1, LANES // HEAD_DIM))


def _swap8(x):
    W = x.shape[1]
    half = ROPE_DIM // 2
    lane = lax.broadcasted_iota(jnp.int32, x.shape, 1) % HEAD_DIM
    return jnp.where(lane < half, pltpu.roll(x, W - half, axis=1),
                     jnp.where(lane < ROPE_DIM, pltpu.roll(x, half, axis=1), 0.0))


def _wide(tab, W):
    return jnp.concatenate([tab] * (W // LANES), axis=1) if W > LANES else tab


def _rope_fwd(qkv, ctab, stab, *, q_width, kv_width, name):
    T = qkv.shape[0]
    tr = _pick(T, (256, 128))
    scale = HEAD_DIM ** -0.5

    def body(x_ref, c_ref, s_ref, q_ref, k_ref, v_ref):
        c = c_ref[...]
        s = s_ref[...]
        q = x_ref[:, :q_width]
        k = x_ref[:, q_width:q_width + kv_width]
        q_ref[...] = ((q * _wide(c, q_width) + _swap8(q) * _wide(s, q_width)) * scale).astype(BF16)
        k_ref[...] = (k * _wide(c, kv_width) + _swap8(k) * _wide(s, kv_width)).astype(BF16)
        v_ref[...] = x_ref[:, q_width + kv_width:].astype(BF16)

    return _row_call(body, [(qkv, "row"), (ctab, "row"), (stab, "row")],
                     [((T, q_width), BF16), ((T, kv_width), BF16), ((T, kv_width), BF16)],
                     name=name, rows=T, tr=tr, est=10 * tr * qkv.shape[1] * 4)


_NT = (((1,), (1,)), ((), ()))
_TN = (((0,), (0,)), ((), ()))


def _group_rows(ref, heads):
    return jnp.concatenate([ref[:, h * HEAD_DIM:(h + 1) * HEAD_DIM] for h in heads], axis=0)


def _attn_valid(grp):
    qi = np.arange(grp * CHUNK)[:, None] % CHUNK
    sj = np.arange(2 * CHUNK)[None, :]
    cur = (sj >= CHUNK) & (sj - CHUNK <= qi)
    prev = (sj < CHUNK) & (sj > qi)
    return jnp.asarray(np.stack([cur, cur | prev]).astype(np.float32))


def _valid_spec(grp):
    return pl.BlockSpec((None, grp * CHUNK, 2 * CHUNK), lambda n: (jnp.minimum(n, 1), 0, 0))


def _attn_group_probs(q, kk, sinks, valid, grp):
    rows = grp * CHUNK
    s = lax.dot_general(q, kk, _NT, preferred_element_type=F32)
    s = jnp.where(valid, s, NEG_INF)
    r = lax.broadcasted_iota(jnp.int32, (rows, 1), 0)
    sink = jnp.full((rows, 1), sinks[grp - 1], F32)
    for g in range(grp - 2, -1, -1):
        sink = jnp.where(r < (g + 1) * CHUNK, sinks[g], sink)
    m = jnp.maximum(jnp.max(s, axis=1, keepdims=True), sink)
    p = jnp.exp(s - m)
    ps = jnp.exp(sink - m)
    inv = 1.0 / (jnp.sum(p, axis=1, keepdims=True) + ps)
    return p * inv, ps * inv


def _kv_specs(width, nb):
    prev = pl.BlockSpec((CHUNK, width), lambda n: (jnp.maximum(n - 1, 0), 0))
    cur = pl.BlockSpec((CHUNK, width), lambda n: (n, 0))
    return prev, cur


def _attn_fwd(qr, kr, vr, sinks, *, name):
    T, QW = qr.shape
    KW = kr.shape[1]
    HQ, HK = QW // HEAD_DIM, KW // HEAD_DIM
    grp = HQ // HK
    nb = T // CHUNK

    def body(q_ref, kp_ref, kc_ref, vp_ref, vc_ref, s_ref, ok_ref, o_ref):
        valid = ok_ref[...] > 0.5
        for kh in range(HK):
            ks = slice(kh * HEAD_DIM, (kh + 1) * HEAD_DIM)
            heads = list(range(kh * grp, (kh + 1) * grp))
            q = _group_rows(q_ref, heads)
            kk = jnp.concatenate([kp_ref[:, ks], kc_ref[:, ks]], axis=0)
            vv = jnp.concatenate([vp_ref[:, ks], vc_ref[:, ks]], axis=0)
            p, _ = _attn_group_probs(q, kk, [s_ref[0, h] for h in heads], valid, grp)
            o = jnp.dot(p.astype(BF16), vv, preferred_element_type=F32).astype(BF16)
            for g, h in enumerate(heads):
                o_ref[:, h * HEAD_DIM:(h + 1) * HEAD_DIM] = o[g * CHUNK:(g + 1) * CHUNK]

    kp, kc = _kv_specs(KW, nb)
    return pl.pallas_call(
        body, name=name, out_shape=jax.ShapeDtypeStruct((T, QW), BF16), grid=(nb,),
        in_specs=[pl.BlockSpec((CHUNK, QW), lambda n: (n, 0)), kp, kc, kp, kc,
                  pl.BlockSpec(memory_space=pltpu.SMEM), _valid_spec(grp)],
        out_specs=pl.BlockSpec((CHUNK, QW), lambda n: (n, 0)),
        compiler_params=pltpu.CompilerParams(dimension_semantics=("arbitrary",), vmem_limit_bytes=_vmem(8 << 20)),
    )(qr, kr, kr, vr, vr, sinks, _attn_valid(grp))


def _attn_bwd(qr, kr, vr, sinks, do, *, name):
    T, QW = qr.shape
    KW = kr.shape[1]
    HQ, HK = QW // HEAD_DIM, KW // HEAD_DIM
    grp = HQ // HK
    nb = T // CHUNK

    def body(q_ref, kp_ref, kc_ref, vp_ref, vc_ref, s_ref, do_ref, ok_ref,
             dq_ref, dkp_ref, dkc_ref, dvp_ref, dvc_ref, ds_ref):
        n = pl.program_id(0)
        valid = ok_ref[...] > 0.5
        lane = lax.broadcasted_iota(jnp.int32, (1, LANES), 1)
        dsink = jnp.zeros((1, LANES), F32)
        for kh in range(HK):
            ks = slice(kh * HEAD_DIM, (kh + 1) * HEAD_DIM)
            heads = list(range(kh * grp, (kh + 1) * grp))
            q = _group_rows(q_ref, heads)
            doh = _group_rows(do_ref, heads)
            kk = jnp.concatenate([kp_ref[:, ks], kc_ref[:, ks]], axis=0)
            vv = jnp.concatenate([vp_ref[:, ks], vc_ref[:, ks]], axis=0)
            p, ps = _attn_group_probs(q, kk, [s_ref[0, h] for h in heads], valid, grp)
            dp = lax.dot_general(doh, vv, _NT, preferred_element_type=F32)
            delta = jnp.sum(p * dp, axis=1, keepdims=True)
            ds = (p * (dp - delta)).astype(BF16)
            dv = lax.dot_general(p.astype(BF16), doh, _TN, preferred_element_type=F32)
            dk = lax.dot_general(ds, q, _TN, preferred_element_type=F32)
            dq = jnp.dot(ds, kk, preferred_element_type=F32)
            psd = ps * delta
            for g, h in enumerate(heads):
                dq_ref[:, h * HEAD_DIM:(h + 1) * HEAD_DIM] = dq[g * CHUNK:(g + 1) * CHUNK]
                dsink = dsink + jnp.where(
                    lane == h, -jnp.sum(psd[g * CHUNK:(g + 1) * CHUNK], axis=0, keepdims=True), 0.0)
            dkp_ref[:, ks] = dk[:CHUNK]
            dkc_ref[:, ks] = dk[CHUNK:]
            dvp_ref[:, ks] = dv[:CHUNK]
            dvc_ref[:, ks] = dv[CHUNK:]

        @pl.when(n == 0)
        def _():
            ds_ref[...] = dsink

        @pl.when(n > 0)
        def _():
            ds_ref[...] += dsink

    kp, kc = _kv_specs(KW, nb)
    qspec = pl.BlockSpec((CHUNK, QW), lambda n: (n, 0))
    kout = pl.BlockSpec((CHUNK, KW), lambda n: (n, 0))
    return pl.pallas_call(
        body, name=name,
        out_shape=[jax.ShapeDtypeStruct((T, QW), F32)] + [jax.ShapeDtypeStruct((T, KW), F32)] * 4
        + [jax.ShapeDtypeStruct((1, LANES), F32)],
        grid=(nb,),
        in_specs=[qspec, kp, kc, kp, kc, pl.BlockSpec(memory_space=pltpu.SMEM), qspec, _valid_spec(grp)],
        out_specs=[qspec, kout, kout, kout, kout, pl.BlockSpec((1, LANES), lambda n: (0, 0))],
        compiler_params=pltpu.CompilerParams(dimension_semantics=("arbitrary",), vmem_limit_bytes=_vmem(12 << 20)),
    )(qr, kr, kr, vr, vr, sinks, do, _attn_valid(grp))


def _rope_bwd(dq, dkp, dkc, dvp, dvc, ctab, stab, *, name):
    T, QW = dq.shape
    KW = dkp.shape[1]
    nb = T // CHUNK
    scale = HEAD_DIM ** -0.5
    width = QW + 2 * KW

    def body(dq_ref, dkc_ref, dkn_ref, dvc_ref, dvn_ref, c_ref, s_ref, o_ref, db_ref):
        n = pl.program_id(0)
        c = c_ref[...]
        s = s_ref[...]
        has_next = (n < nb - 1).astype(F32)
        dqv = dq_ref[...]
        dk = dkc_ref[...] + has_next * dkn_ref[...]
        dv = dvc_ref[...] + has_next * dvn_ref[...]
        dq_pre = (dqv * _wide(c, QW) + _swap8(dqv * _wide(s, QW))) * scale
        dk_pre = dk * _wide(c, KW) + _swap8(dk * _wide(s, KW))
        o_ref[:, :QW] = dq_pre.astype(BF16)
        o_ref[:, QW:QW + KW] = dk_pre.astype(BF16)
        o_ref[:, QW + KW:] = dv.astype(BF16)
        part = jnp.concatenate([jnp.sum(dq_pre, axis=0, keepdims=True), jnp.sum(dk_pre, axis=0, keepdims=True),
                                jnp.sum(dv, axis=0, keepdims=True)], axis=1)

        @pl.when(n == 0)
        def _():
            db_ref[...] = part

        @pl.when(n > 0)
        def _():
            db_ref[...] += part

    cur = lambda w: pl.BlockSpec((CHUNK, w), lambda n: (n, 0))
    nxt = lambda w: pl.BlockSpec((CHUNK, w), lambda n: (jnp.minimum(n + 1, nb - 1), 0))
    return pl.pallas_call(
        body, name=name,
        out_shape=[jax.ShapeDtypeStruct((T, width), BF16), jax.ShapeDtypeStruct((1, width), F32)],
        grid=(nb,),
        in_specs=[cur(QW), cur(KW), nxt(KW), cur(KW), nxt(KW), cur(LANES), cur(LANES)],
        out_specs=[cur(width), pl.BlockSpec((1, width), lambda n: (0, 0))],
        compiler_params=pltpu.CompilerParams(dimension_semantics=("arbitrary",), vmem_limit_bytes=_vmem(8 << 20)),
    )(dq, dkc, dkp, dvc, dvp, ctab, stab)


def _cast_block(w, l, axis, chip_arr, *, name):
    _, Ks, Ns = w.shape
    tk = _pick(Ks, (512, 352, 256, 128))
    nk = Ks // tk
    full = (Ks * N_CHIPS, Ns) if axis == 0 else (Ks, Ns * N_CHIPS)

    def body(p_ref, w_ref, o_ref):
        o_ref[...] = w_ref[...].astype(BF16)

    if axis == 0:
        out_spec = pl.BlockSpec((tk, Ns), lambda i, p: (p[0] * nk + i, 0))
    else:
        out_spec = pl.BlockSpec((tk, Ns), lambda i, p: (i, p[0]))
    grid_spec = pltpu.PrefetchScalarGridSpec(
        num_scalar_prefetch=1, grid=(nk,),
        in_specs=[pl.BlockSpec((None, tk, Ns), lambda i, p: (l, i, 0))], out_specs=out_spec)
    return pl.pallas_call(
        body, name=name, out_shape=jax.ShapeDtypeStruct(full, BF16), grid_spec=grid_spec,
        compiler_params=pltpu.CompilerParams(dimension_semantics=("arbitrary",),
                                             vmem_limit_bytes=_vmem(4 * tk * Ns * 6)),
    )(chip_arr, w)


def _adamw_math(w, g, m, v):
    m = ADAM_B1 * m + (1.0 - ADAM_B1) * g
    v = ADAM_B2 * v + (1.0 - ADAM_B2) * (g * g)
    m_hat = m / (1.0 - ADAM_B1 ** ADAM_STEP)
    v_hat = v / (1.0 - ADAM_B2 ** ADAM_STEP)
    delta = -ADAM_LR * (m_hat / (jnp.sqrt(v_hat) + ADAM_EPS) + ADAM_WD * w)
    return delta, m, v


def _adamw_layer(w, m, v, g, l, outs, *, name):
    _, K, N = w.shape
    tk = _pick(K, (256, 176, 128))

    def body(w_ref, m_ref, v_ref, g_ref, _g, _d, _m, _v, go_ref, d_ref, mo_ref, vo_ref):
        gv = g_ref[...]
        d, mn, vn = _adamw_math(w_ref[...], gv, m_ref[...], v_ref[...])
        go_ref[...] = gv
        d_ref[...] = d
        mo_ref[...] = mn
        vo_ref[...] = vn

    layer = pl.BlockSpec((None, tk, N), lambda i: (l, i, 0))
    any_spec = pl.BlockSpec(memory_space=pl.ANY)
    sd = jax.ShapeDtypeStruct(w.shape, F32)
    return pl.pallas_call(
        body, name=name, out_shape=[sd, sd, sd, sd], grid=(K // tk,),
        in_specs=[layer, layer, layer, pl.BlockSpec((tk, N), lambda i: (i, 0))] + [any_spec] * 4,
        out_specs=[layer] * 4, input_output_aliases={4: 0, 5: 1, 6: 2, 7: 3},
        compiler_params=pltpu.CompilerParams(dimension_semantics=("arbitrary",),
                                             vmem_limit_bytes=_vmem(2 * 8 * tk * N * 4 + 6 * tk * N * 4)),
    )(w, m, v, g, *outs)


def _adamw_small(w, g, m, v, *, name):
    def body(w_ref, g_ref, m_ref, v_ref, d_ref, mo_ref, vo_ref):
        d, mn, vn = _adamw_math(w_ref[...], g_ref[...], m_ref[...], v_ref[...])
        d_ref[...] = d
        mo_ref[...] = mn
        vo_ref[...] = vn

    sd = jax.ShapeDtypeStruct(w.shape, F32)
    return pl.pallas_call(body, name=name, out_shape=[sd, sd, sd])(w, g, m, v)


def _my_place():
    return lax.axis_index("x"), lax.axis_index("y"), lax.axis_index("c")


def _peer_chips(x, y):
    return [(1 - x, y), (x, 1 - y), (1 - x, 1 - y)]


_HBM = pl.BlockSpec(memory_space=pltpu.HBM)
_SEM = pl.BlockSpec(memory_space=pltpu.SEMAPHORE)
_EFFECT = pltpu.SideEffectType.DATAFLOW_SIDE_EFFECTING


def _split_start(name, bufs, n_copies, make_copies, after):
    nb = len(bufs)

    def body(*refs):
        send_sems, recv_sems = refs[nb + 1], refs[nb + 2]
        token = refs[2 * nb + 3]
        sends, _ = make_copies(refs[:nb], send_sems, recv_sems)
        for cp in sends:
            cp.start()
        token[...] = jnp.zeros_like(token)

    res = pl.pallas_call(
        body, name=name,
        out_shape=(pltpu.SemaphoreType.DMA((n_copies,)), pltpu.SemaphoreType.DMA((n_copies,)),
                   *[pltpu.HBM(b.shape, b.dtype) for b in bufs], jax.ShapeDtypeStruct((8, LANES), F32)),
        in_specs=[_HBM] * nb + [pl.BlockSpec(memory_space=pl.ANY)],
        out_specs=(_SEM, _SEM, *[_HBM] * nb, pl.BlockSpec(memory_space=pltpu.VMEM)),
        input_output_aliases={k: 2 + k for k in range(nb)},
        compiler_params=pltpu.CompilerParams(has_side_effects=_EFFECT),
    )(*[pltpu.with_memory_space_constraint(b, pltpu.HBM) for b in bufs],
      after[0] if isinstance(after, (list, tuple)) else after)
    return res[0], res[1], list(res[2:2 + nb]), res[2 + nb]


def _split_wait(name, bufs, sems, make_copies, after):
    nb = len(bufs)
    after = list(after) if isinstance(after, (list, tuple)) else [after]

    def body(*refs):
        send_sems, recv_sems = refs[nb], refs[nb + 1]
        sends, recvs = make_copies(refs[:nb], send_sems, recv_sems)
        for cp in sends:
            cp.wait_send()
        for cp in recvs:
            cp.wait_recv()

    res = pl.pallas_call(
        body, name=name,
        out_shape=tuple(pltpu.HBM(b.shape, b.dtype) for b in bufs),
        in_specs=[_HBM] * nb + [_SEM, _SEM] + [pl.BlockSpec(memory_space=pl.ANY)] * len(after),
        out_specs=tuple([_HBM] * nb),
        input_output_aliases={k: k for k in range(nb)},
        compiler_params=pltpu.CompilerParams(has_side_effects=_EFFECT),
    )(*bufs, sems[0], sems[1], *after)
    return list(res)


def _remote(src, dst, send_sems, recv_sems, k, target):
    return pltpu.make_async_remote_copy(src_ref=src, dst_ref=dst, send_sem=send_sems.at[k],
                                        recv_sem=recv_sems.at[k], device_id=target, device_id_type=MESH)


def _ag_region(ref, axis, chip, half):
    K, N = ref.shape
    if axis == 0:
        hs = K // N_CHIPS // 2
        assert hs % 16 == 0
        return ref.at[pl.ds(pl.multiple_of((2 * chip + half) * hs, 16), hs), :]
    ns, hk = N // N_CHIPS, K // 2
    assert ns % LANES == 0 and hk % 16 == 0
    return ref.at[pl.ds(pl.multiple_of(half * hk, 16), hk), pl.ds(pl.multiple_of(chip * ns, LANES), ns)]


def _ag_copies(stage, axes):
    n = len(axes)

    def make(bufs, send_sems, recv_sems):
        x, y, c = _my_place()
        me = 2 * x + y
        sends, recvs = [], []
        for j, (px, py) in enumerate(_peer_chips(x, y)):
            other = 2 * px + py
            for w in range(n):
                k = j * n + w
                if stage == 1:
                    src, target = _ag_region(bufs[w], axes[w], me, c), (px, py, c)
                    land = _ag_region(bufs[w], axes[w], other, c)
                else:
                    src, target = _ag_region(bufs[w], axes[w], other, c), (x, y, 1 - c)
                    land = _ag_region(bufs[w], axes[w], other, 1 - c)
                sends.append(_remote(src, src, send_sems, recv_sems, k, target))
                recvs.append(_remote(land, land, send_sems, recv_sems, k, target))
        return sends, recvs

    return make


def _half_shape(shape, axis):
    K, N = shape
    return (K, N // 2) if axis == 0 else (K // 2, N)


def _core_half(ref, axis, half):
    K, N = ref.shape
    if axis == 0:
        return ref.at[:, pl.ds(pl.multiple_of(half * (N // 2), LANES), N // 2)]
    return ref.at[pl.ds(pl.multiple_of(half * (K // 2), 16), K // 2), :]


def _chip_block(ref, axis, chip):
    K, N = ref.shape
    if axis == 0:
        return ref.at[pl.ds(pl.multiple_of(chip * (K // N_CHIPS), 16), K // N_CHIPS), :]
    return ref.at[:, pl.ds(pl.multiple_of(chip * (N // N_CHIPS), LANES), N // N_CHIPS)]


def _rs_sibling_copies(axes):
    n = len(axes)

    def make(bufs, send_sems, recv_sems):
        x, y, c = _my_place()
        sends = [_remote(_core_half(bufs[w], axes[w], 1 - c), bufs[n + w], send_sems, recv_sems, w, (x, y, 1 - c))
                 for w in range(n)]
        recvs = [_remote(bufs[n + w], bufs[n + w], send_sems, recv_sems, w, (x, y, 1 - c)) for w in range(n)]
        return sends, recvs

    return make


def _rs_chip_copies(axes):
    n = len(axes)

    def make(bufs, send_sems, recv_sems):
        x, y, c = _my_place()
        sends, recvs = [], []
        for j, (px, py) in enumerate(_peer_chips(x, y)):
            for w in range(n):
                k = j * n + w
                sends.append(_remote(_chip_block(bufs[w], axes[w], 2 * px + py), bufs[n + w].at[j],
                                     send_sems, recv_sems, k, (px, py, c)))
                recvs.append(_remote(bufs[n + w].at[j], bufs[n + w].at[j], send_sems, recv_sems, k, (px, py, c)))
        return sends, recvs

    return make


def _rs_fill_copies(axes):
    n = len(axes)

    def make(bufs, send_sems, recv_sems):
        x, y, c = _my_place()
        sends = [_remote(_core_half(bufs[w], axes[w], c), _core_half(bufs[w], axes[w], c),
                         send_sems, recv_sems, w, (x, y, 1 - c)) for w in range(n)]
        recvs = [_remote(_core_half(bufs[w], axes[w], 1 - c), _core_half(bufs[w], axes[w], 1 - c),
                         send_sems, recv_sems, w, (x, y, 1 - c)) for w in range(n)]
        return sends, recvs

    return make


def _chip_sum(g, r, axis, place, *, name):
    hk, hn = r.shape
    bk, bn = (hk // N_CHIPS, hn) if axis == 0 else (hk, hn // N_CHIPS)
    tk = _pick(bk, (512, 352, 256, 128))
    nk = bk // tk

    def body(p_ref, g_ref, r_ref, b_ref, own_ref):
        s = g_ref[...].astype(F32) + r_ref[...].astype(F32)
        b_ref[...] = s.astype(BF16)

        @pl.when(pl.program_id(1) == p_ref[0])
        def _():
            own_ref[...] = s

    if axis == 0:
        g_spec = pl.BlockSpec((tk, bn), lambda i, j, p: (j * nk + i, p[1]))
        r_spec = pl.BlockSpec((tk, bn), lambda i, j, p: (j * nk + i, 0))
    else:
        g_spec = pl.BlockSpec((tk, bn), lambda i, j, p: (p[1] * nk + i, j))
        r_spec = pl.BlockSpec((tk, bn), lambda i, j, p: (i, j))
    grid_spec = pltpu.PrefetchScalarGridSpec(
        num_scalar_prefetch=1, grid=(nk, N_CHIPS), in_specs=[g_spec, r_spec],
        out_specs=[r_spec, pl.BlockSpec((tk, bn), lambda i, j, p: (i, 0))])
    return pl.pallas_call(
        body, name=name,
        out_shape=[jax.ShapeDtypeStruct(r.shape, BF16), jax.ShapeDtypeStruct((bk, bn), F32)],
        grid_spec=grid_spec,
        compiler_params=pltpu.CompilerParams(dimension_semantics=("arbitrary", "arbitrary"),
                                             vmem_limit_bytes=_vmem(2 * tk * bn * 10 + 3 * tk * bn * 4)),
    )(place, g, r)


def _final_sum(own, recv, axis, place, *, name):
    _, bk, bn = recv.shape
    tk = _pick(bk, (256, 176, 128))
    nk = bk // tk

    def body(p_ref, o_ref, r_ref, out_ref):
        out_ref[...] = ((o_ref[...] + r_ref[0].astype(F32)) + r_ref[1].astype(F32)) + r_ref[2].astype(F32)

    own_spec = pl.BlockSpec((tk, bn), lambda i, p: (i, 0))
    if axis == 0:
        out_shape, out_spec = (bk, 2 * bn), pl.BlockSpec((tk, bn), lambda i, p: (i, p[1]))
    else:
        out_shape, out_spec = (2 * bk, bn), pl.BlockSpec((tk, bn), lambda i, p: (p[1] * nk + i, 0))
    grid_spec = pltpu.PrefetchScalarGridSpec(
        num_scalar_prefetch=1, grid=(nk,),
        in_specs=[own_spec, pl.BlockSpec((3, tk, bn), lambda i, p: (0, i, 0))], out_specs=out_spec)
    return pl.pallas_call(
        body, name=name, out_shape=jax.ShapeDtypeStruct(out_shape, F32), grid_spec=grid_spec,
        compiler_params=pltpu.CompilerParams(dimension_semantics=("arbitrary",),
                                             vmem_limit_bytes=_vmem(2 * tk * bn * 14 + 4 * tk * bn * 4)),
    )(place, own, recv)


def _allreduce_small(p, after=()):
    n_after = len(after)

    def body(*refs):
        p_ref = refs[0]
        o_ref, r0, r1, r2, send_sems, recv_sems = refs[1 + n_after:]
        x, y, c = _my_place()
        o_ref[...] = p_ref[...]
        for s, (peer, rbuf) in enumerate([((x, y, 1 - c), r0), ((1 - x, y, c), r1), ((x, 1 - y, c), r2)]):
            cp = pltpu.make_async_remote_copy(src_ref=o_ref, dst_ref=rbuf, send_sem=send_sems.at[s],
                                              recv_sem=recv_sems.at[s], device_id=peer, device_id_type=MESH)
            cp.start()
            cp.wait()
            o_ref[...] = o_ref[...] + rbuf[...]

    vm = pl.BlockSpec(memory_space=pltpu.VMEM)
    return pl.pallas_call(
        body, name="allreduce_small", out_shape=jax.ShapeDtypeStruct(p.shape, F32),
        in_specs=[vm] + [pl.BlockSpec(memory_space=pl.ANY)] * n_after, out_specs=vm,
        scratch_shapes=[pltpu.VMEM(p.shape, F32)] * 3 + [pltpu.SemaphoreType.DMA((3,))] * 2,
        compiler_params=pltpu.CompilerParams(vmem_limit_bytes=_vmem(6 * _nbytes(p.shape, F32))),
    )(p, *after)


def _pack_rows(parts):
    rows, metas = [], []
    for a in parts:
        flat = a.reshape(-1)
        nrow = -(-flat.shape[0] // LANES)
        nrow = -(-nrow // 8) * 8
        flat = jnp.pad(flat, (0, nrow * LANES - flat.shape[0]))
        rows.append(flat.reshape(nrow, LANES))
        metas.append((a.shape, nrow))
    return jnp.concatenate(rows, axis=0), metas


def _unpack_rows(packed, metas):
    out, r0 = [], 0
    for shape, nrow in metas:
        size = int(np.prod(shape))
        out.append(packed[r0:r0 + nrow].reshape(-1)[:size].reshape(shape))
        r0 += nrow
    return out


def kernel(x, positions, pre_mix_g, post_mix_g, pre_ffn_g, post_ffn_g, a_w_in, a_b_in, a_ln_g, a_ln_b, a_w_s, a_b_s, a_w_out, b_w_qkv, b_b_qkv, b_sinks, b_w_o, ffn_w_gu, ffn_w_down, loss_target, m_pre_mix_g, m_post_mix_g, m_pre_ffn_g, m_post_ffn_g, m_a_w_in, m_a_b_in, m_a_ln_g, m_a_ln_b, m_a_w_s, m_a_b_s, m_a_w_out, m_b_w_qkv, m_b_b_qkv, m_b_sinks, m_b_w_o, m_ffn_w_gu, m_ffn_w_down, v_pre_mix_g, v_post_mix_g, v_pre_ffn_g, v_post_ffn_g, v_a_w_in, v_a_b_in, v_a_ln_g, v_a_ln_b, v_a_w_s, v_a_b_s, v_a_w_out, v_b_w_qkv, v_b_b_qkv, v_b_sinks, v_b_w_o, v_ffn_w_gu, v_ffn_w_down):
    depth, D = pre_mix_g.shape
    xi, yi, ci = _my_place()
    chip = 2 * xi + yi
    place = jnp.stack([chip, ci]).astype(jnp.int32)

    stacked = {"a_w_in": (a_w_in, m_a_w_in, v_a_w_in), "a_w_out": (a_w_out, m_a_w_out, v_a_w_out),
               "b_w_qkv": (b_w_qkv, m_b_w_qkv, v_b_w_qkv), "b_w_o": (b_w_o, m_b_w_o, v_b_w_o),
               "ffn_w_gu": (ffn_w_gu, m_ffn_w_gu, v_ffn_w_gu), "ffn_w_down": (ffn_w_down, m_ffn_w_down, v_ffn_w_down)}
    cut = {"a_w_in": 1, "a_w_out": 0, "b_w_qkv": 1, "b_w_o": 0, "ffn_w_gu": 1, "ffn_w_down": 0}

    def layer_keys(i):
        mix = [("a_w_in", i // 2), ("a_w_out", i // 2)] if i % 2 == 0 else [("b_w_qkv", i // 2), ("b_w_o", i // 2)]
        return mix + [("ffn_w_gu", i), ("ffn_w_down", i)]

    def dep(a, toks):
        for t in toks:
            a = a + t[:1, :1]
        return a

    W = {}
    for i in range(depth):
        for nm, l in layer_keys(i):
            W[(nm, l)] = _cast_block(stacked[nm][0], l, cut[nm], place, name=f"cast_{nm}_{l}")

    def gather(tag, keys, after):
        axes = [cut[nm] for nm, _ in keys]
        for stage in (1, 2):
            ss, rs, bufs, tok = _split_start(f"ag{stage}_start_{tag}", [W[k] for k in keys], 3 * len(keys),
                                             _ag_copies(stage, axes), after)
            after = yield tok
            bufs = _split_wait(f"ag{stage}_wait_{tag}", bufs, (ss, rs), _ag_copies(stage, axes), after)
            W.update(zip(keys, bufs))
        yield None

    nq = b_b_qkv.shape[1]
    bq_full = jnp.zeros((b_b_qkv.shape[0], N_CHIPS * nq), F32)
    bq_full = lax.dynamic_update_slice(bq_full, jnp.where(ci == 0, b_b_qkv, 0.0), (0, chip * nq))
    bq_packed, bq_meta = _pack_rows([bq_full])
    bq_gathered = _allreduce_small(bq_packed)
    b_qkv_full = _unpack_rows(bq_gathered, bq_meta)[0]

    first = gather("0m", layer_keys(0)[:2], bq_gathered)
    tok = next(first)
    tok = first.send([tok] + [W[k] for k in layer_keys(0)[2:] + layer_keys(1)])
    first.send(tok)

    h = x[0]
    target = loss_target[0]
    ctab, stab = _rope_tables(positions[0])
    q_width = W[("b_w_o", 0)].shape[0]
    kv_width = N_KV_HEADS * HEAD_DIM
    row = lambda a, i: a[i:i + 1]

    saved = []
    hn = None
    for i in range(depth):
        j = i // 2
        s = {"h": h}
        ffn_w = None
        if i == 0:
            ffn_w = gather("0f", layer_keys(0)[2:], W[("a_w_out", 0)])
            toks = [next(ffn_w)]
            nxt = gather("1", layer_keys(1), toks[0])
            toks.append(next(nxt))
            hn = _rms_fwd(h, dep(row(pre_mix_g, i), toks), out_dtype=BF16, name=f"rms_pre_mix_{i}")
        elif i + 1 < depth:
            nxt = gather(str(i + 1), layer_keys(i + 1), h)
            toks = [next(nxt)]
        else:
            toks = []
        s["hn"] = hn
        if i % 2 == 0:
            pre = _matmul(hn, W[("a_w_in", j)], mode="nn", bias=dep(row(a_b_in, j), toks), out_dtype=F32,
                          name=f"gmlp_in_{i}")
            gated = _sgu_fwd(pre, row(a_ln_g, j), row(a_ln_b, j), a_w_s[j], a_b_s[j].T, name=f"sgu_fwd_{i}")
            mix = _matmul(gated, W[("a_w_out", j)], mode="nn", out_dtype=F32, name=f"gmlp_out_{i}")
            s.update(pre=pre, gated=gated)
        else:
            qkv = _matmul(hn, W[("b_w_qkv", j)], mode="nn", bias=dep(row(b_qkv_full, j), toks), out_dtype=F32,
                          name=f"attn_qkv_{i}")
            qr, kr, vr = _rope_fwd(qkv, ctab, stab, q_width=q_width, kv_width=kv_width, name=f"rope_fwd_{i}")
            o = _attn_fwd(qr, kr, vr, row(b_sinks, j), name=f"attn_fwd_{i}")
            mix = _matmul(o, W[("b_w_o", j)], mode="nn", out_dtype=F32, name=f"attn_o_{i}")
            s.update(qr=qr, kr=kr, vr=vr, o=o)
        s["mix"] = mix
        toks = [ffn_w.send(mix)] if ffn_w else []
        h1, fn = _rms_res_norm(h, mix, dep(row(post_mix_g, i), toks), row(pre_ffn_g, i), name=f"rms_post_mix_{i}")
        if ffn_w:
            ffn_w.send(h1)
        s["h1"] = h1
        g_pre, u_pre, act = _ffn_up(fn, W[("ffn_w_gu", i)][None], 0, name=f"ffn_up_{i}")
        f = _matmul(act, W[("ffn_w_down", i)], mode="nn", out_dtype=F32, name=f"ffn_down_{i}")
        if i + 1 < depth:
            toks = [nxt.send(f)]
            h, hn = _rms_res_norm(h1, f, dep(row(post_ffn_g, i), toks), row(pre_mix_g, i + 1),
                                  name=f"rms_post_ffn_{i}")
            nxt.send(h)
        else:
            h = _rms_res(h1, f, row(post_ffn_g, i), name=f"rms_post_ffn_{i}")
        s.update(fn=fn, g_pre=g_pre, u_pre=u_pre, act=act, f=f)
        saved.append(s)

    dh, df, loss_part, g_last = _loss_and_grad(h, target, saved[-1]["f"], row(post_ffn_g, depth - 1), name="loss")
    loss = lax.psum(loss_part[0, 0], ("x", "y", "c"))

    big_out = {nm: tuple(lax.empty(w.shape, F32) for _ in range(4)) for nm, (w, _, _) in stacked.items()}

    def reduce_group(i, keys, grads):
        axes = [cut[nm] for nm, _ in keys]
        n = len(keys)
        lands = [lax.empty(_half_shape(g.shape, ax), BF16) for g, ax in zip(grads, axes)]
        ss, rs, bufs, tok = _split_start(f"rs_sibling_start_{i}", list(grads) + lands, n, _rs_sibling_copies(axes),
                                         place)
        after = yield tok
        bufs = _split_wait(f"rs_sibling_wait_{i}", bufs, (ss, rs), _rs_sibling_copies(axes), after)
        sums = [_chip_sum(bufs[w], bufs[n + w], axes[w], place, name=f"chip_sum_{keys[w][0]}_{keys[w][1]}")
                for w in range(n)]
        lands = [lax.empty((3,) + own.shape, BF16) for _, own in sums]
        ss, rs, bufs, tok = _split_start(f"rs_chip_start_{i}", [sb for sb, _ in sums] + lands, 3 * n,
                                         _rs_chip_copies(axes), place)
        after = yield tok
        bufs = _split_wait(f"rs_chip_wait_{i}", bufs, (ss, rs), _rs_chip_copies(axes), after)
        blocks = [_final_sum(sums[w][1], bufs[n + w], axes[w], place, name=f"final_sum_{keys[w][0]}_{keys[w][1]}")
                  for w in range(n)]
        ss, rs, bufs, tok = _split_start(f"rs_fill_start_{i}", blocks, n, _rs_fill_copies(axes), place)
        after = yield tok
        blocks = _split_wait(f"rs_fill_wait_{i}", bufs, (ss, rs), _rs_fill_copies(axes), after)
        for (nm, l), g in zip(keys, blocks):
            w, m, v = stacked[nm]
            big_out[nm] = tuple(_adamw_layer(w, m, v, g, l, big_out[nm], name=f"adamw_{nm}_{l}"))
        yield None

    reducing = []

    def advance(after):
        toks = []
        for gen in list(reducing):
            tok = gen.send(after)
            if tok is None:
                reducing.remove(gen)
            else:
                toks.append(tok)
        return toks

    small = {}
    g_pre_mix, g_post_mix, g_pre_ffn, g_post_ffn = [None] * depth, [None] * depth, [None] * depth, [None] * depth
    g_post_ffn[depth - 1] = g_last
    toks = []
    for i in reversed(range(depth)):
        j = i // 2
        s = saved[i]
        g_down = _matmul(s["act"], df, mode="tn", out_dtype=BF16, after=toks, name=f"ffn_down_dw_{i}")
        dg_, du_ = _ffn_down_dx(df, W[("ffn_w_down", i)][None], 0, s["g_pre"], s["u_pre"], g_down,
                                name=f"ffn_down_dx_{i}")
        hid = dg_.shape[1]
        tile = _pick(hid, (1408, 768, 512, 256, 128))
        w_gu = W[("ffn_w_gu", i)]
        g_gu = lax.empty(w_gu.shape, BF16)
        g_gu = _matmul(s["fn"], dg_, mode="tn", into=g_gu, tq=tile, out_dtype=BF16, name=f"ffn_g_dw_{i}")
        g_gu = _matmul(s["fn"], du_, mode="tn", into=g_gu, tq=tile, q_off=hid // tile, out_dtype=BF16,
                       name=f"ffn_u_dw_{i}")
        dfn_g = _matmul(dg_, w_gu, mode="nt", tr=hid, out_dtype=F32, after=[g_gu], name=f"ffn_g_dx_{i}")
        dfn = _matmul(du_, w_gu, mode="nt", tr=hid, b_r_off=1, bias=dfn_g, out_dtype=F32, name=f"ffn_u_dx_{i}")
        toks = advance(dfn)
        if i == 0:
            gen = reduce_group("0f", layer_keys(0)[2:], [g_gu, g_down])
            toks.append(next(gen))
            reducing.append(gen)
        dh1, dmix, g_pre_ffn[i], g_post_mix[i] = _rms_bwd_chain(
            s["h1"], dep(row(pre_ffn_g, i), toks), dfn, dh, s["mix"], row(post_mix_g, i), name=f"rms_ffn_mix_bwd_{i}")
        if i % 2 == 0:
            g_out = _matmul(s["gated"], dmix, mode="tn", out_dtype=BF16, name=f"gmlp_out_dw_{i}")
            dgated = _matmul(dmix, W[("a_w_out", j)], mode="nt", out_dtype=BF16, after=[g_out],
                             name=f"gmlp_out_dx_{i}")
            toks = advance(dgated) if i == 0 else []
            dpre, dws, dbsT, dlng, dlnb, dbin = _sgu_bwd(s["pre"], dgated, dep(row(a_ln_g, j), toks), row(a_ln_b, j),
                                                         a_w_s[j], a_b_s[j].T, name=f"sgu_bwd_{i}")
            small[("a_w_s", j)] = dws
            small[("a_b_s", j)] = dbsT.T
            small[("a_ln_g", j)] = dlng
            small[("a_ln_b", j)] = dlnb
            small[("a_b_in", j)] = dbin
            g_in = _matmul(s["hn"], dpre, mode="tn", out_dtype=BF16, name=f"gmlp_in_dw_{i}")
            dhn = _matmul(dpre, W[("a_w_in", j)], mode="nt", out_dtype=F32, after=[g_in], name=f"gmlp_in_dx_{i}")
        else:
            g_out = _matmul(s["o"], dmix, mode="tn", out_dtype=BF16, name=f"attn_o_dw_{i}")
            do = _matmul(dmix, W[("b_w_o", j)], mode="nt", out_dtype=BF16, after=[g_out], name=f"attn_o_dx_{i}")
            dq, dkp, dkc, dvp, dvc, dsk = _attn_bwd(s["qr"], s["kr"], s["vr"], row(b_sinks, j), do,
                                                    name=f"attn_bwd_{i}")
            dqkv, dbq = _rope_bwd(dq, dkp, dkc, dvp, dvc, ctab, stab, name=f"rope_bwd_{i}")
            small[("b_sinks", j)] = dsk[:, :b_sinks.shape[1]]
            small[("b_b_qkv", j)] = dbq
            g_in = _matmul(s["hn"], dqkv, mode="tn", out_dtype=BF16, name=f"attn_qkv_dw_{i}")
            dhn = _matmul(dqkv, W[("b_w_qkv", j)], mode="nt", out_dtype=F32, after=[g_in], name=f"attn_qkv_dx_{i}")
        toks = advance(dhn)
        if i > 0:
            dh, df, g_pre_mix[i], g_post_ffn[i - 1] = _rms_bwd_chain(
                s["h"], dep(row(pre_mix_g, i), toks), dhn, dh1, saved[i - 1]["f"], row(post_ffn_g, i - 1),
                name=f"rms_mix_ffn_bwd_{i}")
            gen = reduce_group(str(i), layer_keys(i), [g_in, g_out, g_gu, g_down])
            toks = [next(gen)] + advance(dh)
            reducing.append(gen)
        else:
            gen = reduce_group("0m", layer_keys(0)[:2], [g_in, g_out])
            toks.append(next(gen))
            dh, g_pre_mix[i] = _rms_bwd(s["h"], dep(row(pre_mix_g, i), toks), dhn, dh1, out_dtype=F32,
                                        name=f"rms_pre_mix_bwd_{i}")
            toks = advance(dh)
            reducing.append(gen)
    grad_x = dh[None]

    toks = toks + advance(dh)
    ready = [big_out[nm][1] for nm in big_out]
    n_a, n_b = a_b_in.shape[0], b_sinks.shape[0]
    stack = lambda key, n: jnp.concatenate([small[(key, j)] for j in range(n)], axis=0)
    small_parts = [
        jnp.concatenate(g_pre_mix, axis=0), jnp.concatenate(g_post_mix, axis=0),
        jnp.concatenate(g_pre_ffn, axis=0), jnp.concatenate(g_post_ffn, axis=0),
        stack("a_b_in", n_a), stack("a_ln_g", n_a), stack("a_ln_b", n_a),
        jnp.stack([small[("a_w_s", j)] for j in range(n_a)]), jnp.stack([small[("a_b_s", j)] for j in range(n_a)]),
        stack("b_b_qkv", n_b), stack("b_sinks", n_b),
    ]
    packed, metas = _pack_rows(small_parts)
    reduced = _allreduce_small(dep(packed, toks), after=ready)
    while reducing:
        advance(reduced)
    red = _unpack_rows(reduced, metas)
    (gr_pre_mix, gr_post_mix, gr_pre_ffn, gr_post_ffn, gr_b_in, gr_ln_g, gr_ln_b, gr_w_s, gr_b_s,
     gr_b_qkv_full, gr_sinks) = red
    gr_b_qkv = lax.dynamic_slice(gr_b_qkv_full, (0, chip * nq), (gr_b_qkv_full.shape[0], nq))

    grads = {"pre_mix_g": gr_pre_mix, "post_mix_g": gr_post_mix, "pre_ffn_g": gr_pre_ffn, "post_ffn_g": gr_post_ffn,
             "a_b_in": gr_b_in, "a_ln_g": gr_ln_g, "a_ln_b": gr_ln_b, "a_w_s": gr_w_s, "a_b_s": gr_b_s,
             "b_b_qkv": gr_b_qkv, "b_sinks": gr_sinks}
    weights = {"pre_mix_g": (pre_mix_g, m_pre_mix_g, v_pre_mix_g), "post_mix_g": (post_mix_g, m_post_mix_g, v_post_mix_g),
               "pre_ffn_g": (pre_ffn_g, m_pre_ffn_g, v_pre_ffn_g), "post_ffn_g": (post_ffn_g, m_post_ffn_g, v_post_ffn_g),
               "a_b_in": (a_b_in, m_a_b_in, v_a_b_in), "a_ln_g": (a_ln_g, m_a_ln_g, v_a_ln_g),
               "a_ln_b": (a_ln_b, m_a_ln_b, v_a_ln_b), "a_w_s": (a_w_s, m_a_w_s, v_a_w_s), "a_b_s": (a_b_s, m_a_b_s, v_a_b_s),
               "b_b_qkv": (b_b_qkv, m_b_b_qkv, v_b_b_qkv), "b_sinks": (b_sinks, m_b_sinks, v_b_sinks)}
    order = ["pre_mix_g", "post_mix_g", "pre_ffn_g", "post_ffn_g", "a_w_in", "a_b_in", "a_ln_g", "a_ln_b", "a_w_s",
             "a_b_s", "a_w_out", "b_w_qkv", "b_b_qkv", "b_sinks", "b_w_o", "ffn_w_gu", "ffn_w_down"]
    deltas, new_m, new_v = {}, {}, {}
    for nm in order:
        if nm in big_out:
            grads[nm], deltas[nm], new_m[nm], new_v[nm] = big_out[nm]
        else:
            w, m, v = weights[nm]
            deltas[nm], new_m[nm], new_v[nm] = _adamw_small(w, grads[nm], m, v, name="adamw_" + nm)
    return (loss, grad_x, *[grads[nm] for nm in order], *[deltas[nm] for nm in order],
            *[new_m[nm] for nm in order], *[new_v[nm] for nm in order])
```

```python
import functools
import math

import jax
import jax.numpy as jnp
import numpy as np
from jax import lax
from jax.experimental import pallas as pl
from jax.experimental.pallas import tpu as pltpu

F32 = jnp.float32
BF16 = jnp.bfloat16
MESH = pl.DeviceIdType.MESH

HEAD_DIM = 64
N_KV_HEADS = 4
ROPE_DIM = 16
ROPE_THETA = 500000.0
CHUNK = 128
GMLP_GROUPS = 8
RMS_EPS = 1e-6
LN_EPS = 1e-5
NEG_INF = -1e30
ADAM_LR = 0.001
ADAM_B1 = 0.9
ADAM_B2 = 0.999
ADAM_EPS = 1e-08
ADAM_WD = 0.01
ADAM_STEP = 10

N_CHIPS = 4
LANES = 128
VMEM_CAP = 58 * 1024 * 1024


def _vmem(est_bytes):
    assert est_bytes < VMEM_CAP
    return VMEM_CAP


def _pick(n, cands):
    for c in cands:
        if c <= n and n % c == 0:
            return c
    return n


def _nbytes(shape, dtype):
    return int(np.prod(shape)) * jnp.dtype(dtype).itemsize


MATMUL_VMEM_BUDGET = 48 * 1024 * 1024


def _halvings(n, unit):
    out, t = [], n
    while t % unit == 0 and t >= unit:
        out.append(t)
        if t % 2:
            break
        t //= 2
    return out


def _matmul_tiles(P, Q, R, a_bytes, b_bytes, o_bytes, full_addend, tp, tq, tr):
    step_us, bytes_per_us = 0.85, 3.2e6
    best = None
    for p in ([tp] if tp else _halvings(P, LANES)):
        for q in ([tq] if tq else _halvings(Q, LANES)):
            for r in ([tr] if tr else _halvings(R, LANES)):
                nk = R // r
                vm = 2 * (p * r * a_bytes + r * q * b_bytes + p * q * o_bytes + (p * q * 4 if full_addend else 0))
                vm += p * q * 4 * (2 if nk > 1 else 1)
                if vm > MATMUL_VMEM_BUDGET:
                    continue
                exposed = (p * r * a_bytes + r * q * b_bytes + p * q * o_bytes) / bytes_per_us
                key = ((P // p) * (Q // q) * nk * step_us + exposed, nk, abs(p - q))
                if best is None or key < best[0]:
                    best = (key, (p, q, r))
    assert best is not None, (P, Q, R)
    return best[1]


def _matmul(a, b, *, mode, out_dtype, name, a_l=None, b_l=None, bias=None, into=None, o_l=None,
            q_off=0, b_r_off=0, tp=None, tq=None, tr=None, after=()):
    a2 = a.shape[-2:]
    b2 = b.shape[-2:]
    if mode == "nn":
        (P, R), (R2, Q) = a2, b2
    elif mode == "nt":
        (P, R), (Q, R2) = a2, b2
    else:
        (R, P), (R2, Q) = a2, b2
    assert R == R2 or (mode == "nt" and R2 % R == 0), (mode, a.shape, b.shape)
    o_bytes = jnp.dtype(into.dtype if into is not None else out_dtype).itemsize
    full_addend = bias is not None and bias.shape[0] != 1
    tp, tq, tr = _matmul_tiles(P, Q, R, a.dtype.itemsize, b.dtype.itemsize, o_bytes, full_addend, tp, tq, tr)
    assert P % tp == 0 and Q % tq == 0 and R % tr == 0
    nk = R // tr
    dims = {"nn": (((1,), (0,)), ((), ())), "nt": (((1,), (1,)), ((), ())), "tn": (((0,), (0,)), ((), ()))}[mode]

    def lead(l, blk, idx):
        if l is None:
            return pl.BlockSpec(blk, idx)
        return pl.BlockSpec((None,) + blk, lambda i, j, k: (l,) + idx(i, j, k))

    if mode == "nn":
        a_spec = lead(a_l, (tp, tr), lambda i, j, k: (i, k))
        b_spec = lead(b_l, (tr, tq), lambda i, j, k: (k, j))
    elif mode == "nt":
        a_spec = lead(a_l, (tp, tr), lambda i, j, k: (i, k))
        b_spec = lead(b_l, (tq, tr), lambda i, j, k: (j, k + b_r_off))
    else:
        a_spec = lead(a_l, (tr, tp), lambda i, j, k: (k, i))
        b_spec = lead(b_l, (tr, tq), lambda i, j, k: (k, j))
    in_specs = [a_spec, b_spec]
    args = [a, b]
    if bias is not None:
        if bias.shape[0] == 1:
            in_specs.append(pl.BlockSpec((1, tq), lambda i, j, k: (0, j)))
        else:
            in_specs.append(pl.BlockSpec((tp, tq), lambda i, j, k: (i, j)))
        args.append(bias)
    aliases = {}
    if into is not None:
        in_specs.append(pl.BlockSpec(memory_space=pl.ANY))
        args.append(into)
        aliases = {len(args) - 1: 0}
        out_shape = jax.ShapeDtypeStruct(into.shape, into.dtype)
        out_dtype = into.dtype
        if o_l is None:
            out_spec = pl.BlockSpec((tp, tq), lambda i, j, k: (i, j + q_off))
        else:
            out_spec = pl.BlockSpec((None, tp, tq), lambda i, j, k: (o_l, i, j + q_off))
    else:
        out_shape = jax.ShapeDtypeStruct((P, Q), out_dtype)
        out_spec = pl.BlockSpec((tp, tq), lambda i, j, k: (i, j))
    n_in = len(args) + len(after)
    in_specs += [pl.BlockSpec(memory_space=pl.ANY)] * len(after)
    args += list(after)
    has_bias = bias is not None
    has_into = into is not None

    def body(*refs):
        a_ref, b_ref = refs[0], refs[1]
        pos = 2
        bias_ref = None
        if has_bias:
            bias_ref = refs[pos]
            pos += 1
        o_ref = refs[n_in]
        acc_ref = refs[n_in + 1] if nk > 1 else None
        part = lax.dot_general(a_ref[...], b_ref[...], dims, preferred_element_type=F32)

        def finish(acc):
            if has_bias:
                acc = acc + bias_ref[...]
            o_ref[...] = acc.astype(out_dtype)

        if nk == 1:
            finish(part)
        else:
            k = pl.program_id(2)

            @pl.when(k == 0)
            def _():
                acc_ref[...] = part

            @pl.when(k > 0)
            def _():
                acc_ref[...] += part

            @pl.when(k == nk - 1)
            def _():
                finish(acc_ref[...])

    est = 2 * (_nbytes((tp, tr), a.dtype) + _nbytes((tr, tq), b.dtype) + _nbytes((tp, tq), out_dtype)) + 3 * tp * tq * 4
    return pl.pallas_call(
        body, name=name, out_shape=out_shape,
        grid=(P // tp, Q // tq, nk),
        in_specs=in_specs, out_specs=out_spec,
        scratch_shapes=[pltpu.VMEM((tp, tq), F32)] if nk > 1 else [],
        input_output_aliases=aliases,
        compiler_params=pltpu.CompilerParams(
            dimension_semantics=("parallel", "parallel", "arbitrary"), vmem_limit_bytes=_vmem(est)),
    )(*args)


def _row_call(body, ins, outs, *, name, rows, tr, acc_outs=(), est=0):
    in_specs = []
    for arr, kind in ins:
        if kind == "row":
            in_specs.append(pl.BlockSpec((tr, arr.shape[1]), lambda i: (i, 0)))
        else:
            nd = arr.ndim
            in_specs.append(pl.BlockSpec(arr.shape, lambda i, nd=nd: (0,) * nd))
    out_shapes = [jax.ShapeDtypeStruct(s, d) for s, d in outs] + [jax.ShapeDtypeStruct(s, d) for s, d in acc_outs]
    out_specs = [pl.BlockSpec((tr, s[1]), lambda i: (i, 0)) for s, _ in outs]
    out_specs += [pl.BlockSpec(s, lambda i, nd=len(s): (0,) * nd) for s, _ in acc_outs]
    res = pl.pallas_call(
        body, name=name, out_shape=out_shapes, grid=(rows // tr,), in_specs=in_specs, out_specs=out_specs,
        compiler_params=pltpu.CompilerParams(dimension_semantics=("arbitrary",), vmem_limit_bytes=_vmem(est)),
    )(*[a for a, _ in ins])
    return res


def _rms_fwd(x, g, *, out_dtype, name):
    T, D = x.shape
    tr = _pick(T, (512, 256, 128))

    def body(x_ref, g_ref, o_ref):
        xv = x_ref[...]
        r = lax.rsqrt(jnp.mean(xv * xv, axis=-1, keepdims=True) + RMS_EPS)
        o_ref[...] = (xv * r * g_ref[...]).astype(out_dtype)

    return _row_call(body, [(x, "row"), (g, "full")], [((T, D), out_dtype)], name=name, rows=T, tr=tr,
                     est=8 * tr * D * 4)[0]


def _rms_res(h, y, g, *, name):
    T, D = h.shape
    tr = _pick(T, (512, 256, 128))

    def body(h_ref, y_ref, g_ref, o_ref):
        yv = y_ref[...]
        r = lax.rsqrt(jnp.mean(yv * yv, axis=-1, keepdims=True) + RMS_EPS)
        o_ref[...] = h_ref[...] + yv * r * g_ref[...]

    return _row_call(body, [(h, "row"), (y, "row"), (g, "full")], [((T, D), F32)], name=name, rows=T, tr=tr,
                     est=10 * tr * D * 4)[0]


def _rms_bwd(x, g, dy, dres, *, out_dtype, name):
    T, D = x.shape
    tr = _pick(T, (512, 256, 128))
    has_res = dres is not None

    def body(*refs):
        if has_res:
            x_ref, g_ref, dy_ref, dr_ref, dx_ref, dg_ref = refs
        else:
            x_ref, g_ref, dy_ref, dx_ref, dg_ref = refs
        xv = x_ref[...]
        r = lax.rsqrt(jnp.mean(xv * xv, axis=-1, keepdims=True) + RMS_EPS)
        xhat = xv * r
        dyv = dy_ref[...].astype(F32)
        dxn = dyv * g_ref[...]
        dx = r * (dxn - xhat * jnp.mean(dxn * xhat, axis=-1, keepdims=True))
        if has_res:
            dx = dx + dr_ref[...]
        dx_ref[...] = dx.astype(out_dtype)
        part = jnp.sum(dyv * xhat, axis=0, keepdims=True)

        @pl.when(pl.program_id(0) == 0)
        def _():
            dg_ref[...] = part

        @pl.when(pl.program_id(0) > 0)
        def _():
            dg_ref[...] += part

    ins = [(x, "row"), (g, "full"), (dy, "row")] + ([(dres, "row")] if has_res else [])
    dx, dg = _row_call(body, ins, [((T, D), out_dtype)], name=name, rows=T, tr=tr, acc_outs=[((1, D), F32)],
                       est=12 * tr * D * 4)
    return dx, dg


def _rms_res_norm(h, y, g_res, g_next, *, name):
    T, D = h.shape
    tr = _pick(T, (512, 256, 128))

    def body(h_ref, y_ref, g_ref, gn_ref, o_ref, n_ref):
        yv = y_ref[...]
        r = lax.rsqrt(jnp.mean(yv * yv, axis=-1, keepdims=True) + RMS_EPS)
        h2 = h_ref[...] + yv * r * g_ref[...]
        o_ref[...] = h2
        r2 = lax.rsqrt(jnp.mean(h2 * h2, axis=-1, keepdims=True) + RMS_EPS)
        n_ref[...] = (h2 * r2 * gn_ref[...]).astype(BF16)

    return _row_call(body, [(h, "row"), (y, "row"), (g_res, "full"), (g_next, "full")],
                     [((T, D), F32), ((T, D), BF16)], name=name, rows=T, tr=tr, est=12 * tr * D * 4)


def _rms_bwd_chain(x1, g1, dy1, dres, x2, g2, *, name):
    T, D = x1.shape
    tr = _pick(T, (512, 256, 128))

    def one(xv, gv, dyv):
        r = lax.rsqrt(jnp.mean(xv * xv, axis=-1, keepdims=True) + RMS_EPS)
        xhat = xv * r
        dxn = dyv * gv
        dx = r * (dxn - xhat * jnp.mean(dxn * xhat, axis=-1, keepdims=True))
        return dx, jnp.sum(dyv * xhat, axis=0, keepdims=True)

    def body(x1_ref, g1_ref, dy1_ref, dr_ref, x2_ref, g2_ref, d1_ref, d2_ref, dg1_ref, dg2_ref):
        dx1, p1 = one(x1_ref[...], g1_ref[...], dy1_ref[...].astype(F32))
        d1 = dx1 + dr_ref[...]
        d1_ref[...] = d1
        dx2, p2 = one(x2_ref[...], g2_ref[...], d1)
        d2_ref[...] = dx2.astype(BF16)

        @pl.when(pl.program_id(0) == 0)
        def _():
            dg1_ref[...] = p1
            dg2_ref[...] = p2

        @pl.when(pl.program_id(0) > 0)
        def _():
            dg1_ref[...] += p1
            dg2_ref[...] += p2

    ins = [(x1, "row"), (g1, "full"), (dy1, "row"), (dres, "row"), (x2, "row"), (g2, "full")]
    return _row_call(body, ins, [((T, D), F32), ((T, D), BF16)], name=name, rows=T, tr=tr,
                     acc_outs=[((1, D), F32), ((1, D), F32)], est=20 * tr * D * 4)


def _ffn_up(fn, w_gu, l, *, name):
    T, D = fn.shape
    H = w_gu.shape[2] // 2
    tp = _pick(T, (1024, 512, 256, 128))
    tq = _pick(H, (1408, 768, 512, 256, 128))
    nj = H // tq

    def body(a_ref, wg_ref, wu_ref, g_ref, u_ref, act_ref):
        a = a_ref[...]
        g = jnp.dot(a, wg_ref[...], preferred_element_type=F32)
        u = jnp.dot(a, wu_ref[...], preferred_element_type=F32)
        sg = jax.nn.sigmoid(g)
        silu = g * sg
        g_ref[...] = (u * (sg + silu * (1.0 - sg))).astype(BF16)
        u_ref[...] = silu.astype(BF16)
        act_ref[...] = (silu * u).astype(BF16)

    tile = pl.BlockSpec((tp, tq), lambda j, i: (i, j))
    est = 2 * (tp * D * 2 + 2 * D * tq * 2 + 3 * tp * tq * 2) + 4 * tp * tq * 4
    return pl.pallas_call(
        body, name=name,
        out_shape=[jax.ShapeDtypeStruct((T, H), BF16), jax.ShapeDtypeStruct((T, H), BF16),
                   jax.ShapeDtypeStruct((T, H), BF16)],
        grid=(nj, T // tp),
        in_specs=[pl.BlockSpec((tp, D), lambda j, i: (i, 0)),
                  pl.BlockSpec((None, D, tq), lambda j, i: (l, 0, j)),
                  pl.BlockSpec((None, D, tq), lambda j, i: (l, 0, j + nj))],
        out_specs=[tile, tile, tile],
        compiler_params=pltpu.CompilerParams(dimension_semantics=("parallel", "parallel"),
                                             vmem_limit_bytes=_vmem(est)),
    )(fn, w_gu, w_gu)


def _ffn_down_dx(df, w_down, l, g, u, after, *, name):
    T, D = df.shape
    H = w_down.shape[1]
    tp = _pick(T, (1024, 512, 256, 128))
    tq = _pick(H, (1408, 768, 512, 256, 128))

    def body(a_ref, w_ref, g_ref, u_ref, _, dg_ref, du_ref):
        da = lax.dot_general(a_ref[...], w_ref[...], (((1,), (1,)), ((), ())), preferred_element_type=F32)
        dg_ref[...] = (da * g_ref[...].astype(F32)).astype(BF16)
        du_ref[...] = (da * u_ref[...].astype(F32)).astype(BF16)

    tile = pl.BlockSpec((tp, tq), lambda j, i: (i, j))
    est = 2 * (tp * D * 2 + tq * D * 2 + 4 * tp * tq * 2) + 3 * tp * tq * 4
    return pl.pallas_call(
        body, name=name,
        out_shape=[jax.ShapeDtypeStruct((T, H), BF16), jax.ShapeDtypeStruct((T, H), BF16)],
        grid=(H // tq, T // tp),
        in_specs=[pl.BlockSpec((tp, D), lambda j, i: (i, 0)),
                  pl.BlockSpec((None, tq, D), lambda j, i: (l, j, 0)), tile, tile,
                  pl.BlockSpec(memory_space=pl.ANY)],
        out_specs=[tile, tile],
        compiler_params=pltpu.CompilerParams(dimension_semantics=("parallel", "parallel"),
                                             vmem_limit_bytes=_vmem(est)),
    )(df, w_down, g, u, after)


def _loss_and_grad(y, target, x, g, *, name):
    T, D = y.shape
    tr = _pick(T, (512, 256, 128))

    def body(y_ref, t_ref, x_ref, g_ref, dy_ref, dx_ref, l_ref, dg_ref):
        e = y_ref[...] - t_ref[...]
        dy = e * (1.0 / D)
        dy_ref[...] = dy
        part = jnp.sum(jnp.sum(e * e, axis=1, keepdims=True), axis=0, keepdims=True) * (0.5 / D)
        xv = x_ref[...]
        r = lax.rsqrt(jnp.mean(xv * xv, axis=-1, keepdims=True) + RMS_EPS)
        xhat = xv * r
        dxn = dy * g_ref[...]
        dx_ref[...] = (r * (dxn - xhat * jnp.mean(dxn * xhat, axis=-1, keepdims=True))).astype(BF16)
        dg = jnp.sum(dy * xhat, axis=0, keepdims=True)

        @pl.when(pl.program_id(0) == 0)
        def _():
            l_ref[...] = part
            dg_ref[...] = dg

        @pl.when(pl.program_id(0) > 0)
        def _():
            l_ref[...] += part
            dg_ref[...] += dg

    dy, dx, l, dg = _row_call(body, [(y, "row"), (target, "row"), (x, "row"), (g, "full")],
                              [((T, D), F32), ((T, D), BF16)], name=name, rows=T, tr=tr,
                              acc_outs=[((1, 1), F32), ((1, D), F32)], est=14 * tr * D * 4)
    return dy, dx, l, dg


_SQRT_HALF = 0.7071067811865476
_INV_SQRT_2PI = 0.3989422804014327


def _gelu_parts(x):
    cdf = 0.5 * (1.0 + lax.erf(x * _SQRT_HALF))
    return cdf


def _sgu_common(pre, lng, lnb, W):
    cdf = _gelu_parts(pre)
    z = pre * cdf
    u = z[:, :W]
    v = z[:, W:]
    mu = jnp.mean(v, axis=-1, keepdims=True)
    vc = v - mu
    var = jnp.mean(vc * vc, axis=-1, keepdims=True)
    rstd = lax.rsqrt(var + LN_EPS)
    vhat = vc * rstd
    vn = vhat * lng + lnb
    return cdf, u, vhat, rstd, vn


def _causal_mask():
    t = lax.broadcasted_iota(jnp.int32, (CHUNK, CHUNK), 0)
    s = lax.broadcasted_iota(jnp.int32, (CHUNK, CHUNK), 1)
    return t >= s


def _sgu_fwd(pre, lng, lnb, ws, bsT, *, name):
    T, W2 = pre.shape
    W = W2 // 2
    G = ws.shape[0]
    gd = W // G

    def body(pre_ref, lng_ref, lnb_ref, ws_ref, bs_ref, o_ref):
        _, u, _, _, vn = _sgu_common(pre_ref[...], lng_ref[...], lnb_ref[...], W)
        vnb = vn.astype(BF16)
        causal = _causal_mask()
        for g in range(G):
            w = jnp.where(causal, ws_ref[g], 0.0).astype(BF16)
            sv = jnp.dot(w, vnb[:, g * gd:(g + 1) * gd], preferred_element_type=F32) + bs_ref[:, g:g + 1]
            o_ref[:, g * gd:(g + 1) * gd] = (u[:, g * gd:(g + 1) * gd] * sv).astype(BF16)

    return pl.pallas_call(
        body, name=name, out_shape=jax.ShapeDtypeStruct((T, W), BF16), grid=(T // CHUNK,),
        in_specs=[pl.BlockSpec((CHUNK, W2), lambda i: (i, 0)),
                  pl.BlockSpec((1, W), lambda i: (0, 0)), pl.BlockSpec((1, W), lambda i: (0, 0)),
                  pl.BlockSpec(ws.shape, lambda i: (0, 0, 0)), pl.BlockSpec(bsT.shape, lambda i: (0, 0))],
        out_specs=pl.BlockSpec((CHUNK, W), lambda i: (i, 0)),
        compiler_params=pltpu.CompilerParams(dimension_semantics=("arbitrary",),
                                             vmem_limit_bytes=_vmem(12 * CHUNK * W2 * 4)),
    )(pre, lng, lnb, ws, bsT)


def _sgu_bwd(pre, dgated, lng, lnb, ws, bsT, *, name):
    T, W2 = pre.shape
    W = W2 // 2
    G = ws.shape[0]
    gd = W // G

    def body(pre_ref, dgt_ref, lng_ref, lnb_ref, ws_ref, bs_ref,
             dpre_ref, dws_ref, dbs_ref, dlng_ref, dlnb_ref, dbin_ref):
        first = pl.program_id(0) == 0

        @pl.when(first)
        def _():
            dws_ref[...] = jnp.zeros_like(dws_ref)
            dbs_ref[...] = jnp.zeros_like(dbs_ref)
            dlng_ref[...] = jnp.zeros_like(dlng_ref)
            dlnb_ref[...] = jnp.zeros_like(dlnb_ref)
            dbin_ref[...] = jnp.zeros_like(dbin_ref)

        pre_v = pre_ref[...]
        lng_v = lng_ref[...]
        cdf, u, vhat, rstd, vn = _sgu_common(pre_v, lng_v, lnb_ref[...], W)
        vnb = vn.astype(BF16)
        dgt = dgt_ref[...].astype(F32)
        causal = _causal_mask()
        du_parts, dvn_parts = [], []
        for g in range(G):
            sl = slice(g * gd, (g + 1) * gd)
            w = jnp.where(causal, ws_ref[g], 0.0).astype(BF16)
            sv = jnp.dot(w, vnb[:, sl], preferred_element_type=F32) + bs_ref[:, g:g + 1]
            dgt_g = dgt[:, sl]
            du_parts.append(dgt_g * sv)
            dsv = dgt_g * u[:, sl]
            dsvb = dsv.astype(BF16)
            dvn_parts.append(lax.dot_general(w, dsvb, (((0,), (0,)), ((), ())), preferred_element_type=F32))
            dw = lax.dot_general(dsvb, vnb[:, sl], (((1,), (1,)), ((), ())), preferred_element_type=F32)
            dws_ref[g] += jnp.where(causal, dw, 0.0)
            dbs_ref[:, g:g + 1] += jnp.sum(dsv, axis=1, keepdims=True)
        du = jnp.concatenate(du_parts, axis=1)
        dvn = jnp.concatenate(dvn_parts, axis=1)
        dlng_ref[...] += jnp.sum(dvn * vhat, axis=0, keepdims=True)
        dlnb_ref[...] += jnp.sum(dvn, axis=0, keepdims=True)
        dvh = dvn * lng_v
        dv = rstd * (dvh - jnp.mean(dvh, axis=-1, keepdims=True)
                     - vhat * jnp.mean(dvh * vhat, axis=-1, keepdims=True))
        dz = jnp.concatenate([du, dv], axis=1)
        dgelu = cdf + pre_v * jnp.exp(-0.5 * pre_v * pre_v) * _INV_SQRT_2PI
        dpre = dz * dgelu
        dbin_ref[...] += jnp.sum(dpre, axis=0, keepdims=True)
        dpre_ref[...] = dpre.astype(BF16)

    full = lambda shape: pl.BlockSpec(shape, lambda i, nd=len(shape): (0,) * nd)
    return pl.pallas_call(
        body, name=name,
        out_shape=[jax.ShapeDtypeStruct((T, W2), BF16), jax.ShapeDtypeStruct(ws.shape, F32),
                   jax.ShapeDtypeStruct(bsT.shape, F32), jax.ShapeDtypeStruct((1, W), F32),
                   jax.ShapeDtypeStruct((1, W), F32), jax.ShapeDtypeStruct((1, W2), F32)],
        grid=(T // CHUNK,),
        in_specs=[pl.BlockSpec((CHUNK, W2), lambda i: (i, 0)), pl.BlockSpec((CHUNK, W), lambda i: (i, 0)),
                  full((1, W)), full((1, W)), full(ws.shape), full(bsT.shape)],
        out_specs=[pl.BlockSpec((CHUNK, W2), lambda i: (i, 0)), full(ws.shape), full(bsT.shape),
                   full((1, W)), full((1, W)), full((1, W2))],
        compiler_params=pltpu.CompilerParams(dimension_semantics=("arbitrary",),
                                             vmem_limit_bytes=_vmem(24 * CHUNK * W2 * 4)),
    )(pre, dgated, lng, lnb, ws, bsT)


def _rope_tables(positions):
    half = ROPE_DIM // 2
    inv_freq = ROPE_THETA ** (-jnp.arange(0, ROPE_DIM, 2, dtype=F32) / ROPE_DIM)
    ang = positions.astype(F32).reshape(-1, 1) * inv_freq
    cos, sin = jnp.cos(ang), jnp.sin(ang)
    T = ang.shape[0]
    rest = HEAD_DIM - ROPE_DIM
    c64 = jnp.concatenate([cos, cos, jnp.ones((T, rest), F32)], axis=1)
    s64 = jnp.concatenate([-sin, sin, jnp.zeros((T, rest), F32)], axis=1)
    del half
    return jnp.tile(c64, (1, LANES // HEAD_DIM)), jnp.tile(s64, (1, LANES // HEAD_DIM))


def _swap8(x):
    W = x.shape[1]
    half = ROPE_DIM // 2
    lane = lax.broadcasted_iota(jnp.int32, x.shape, 1) % HEAD_DIM
    return jnp.where(lane < half, pltpu.roll(x, W - half, axis=1),
                     jnp.where(lane < ROPE_DIM, pltpu.roll(x, half, axis=1), 0.0))


def _wide(tab, W):
    return jnp.concatenate([tab] * (W // LANES), axis=1) if W > LANES else tab


def _rope_fwd(qkv, ctab, stab, *, q_width, kv_width, name):
    T = qkv.shape[0]
    tr = _pick(T, (256, 128))
    scale = HEAD_DIM ** -0.5

    def body(x_ref, c_ref, s_ref, q_ref, k_ref, v_ref):
        c = c_ref[...]
        s = s_ref[...]
        q = x_ref[:, :q_width]
        k = x_ref[:, q_width:q_width + kv_width]
        q_ref[...] = ((q * _wide(c, q_width) + _swap8(q) * _wide(s, q_width)) * scale).astype(BF16)
        k_ref[...] = (k * _wide(c, kv_width) + _swap8(k) * _wide(s, kv_width)).astype(BF16)
        v_ref[...] = x_ref[:, q_width + kv_width:].astype(BF16)

    return _row_call(body, [(qkv, "row"), (ctab, "row"), (stab, "row")],
                     [((T, q_width), BF16), ((T, kv_width), BF16), ((T, kv_width), BF16)],
                     name=name, rows=T, tr=tr, est=10 * tr * qkv.shape[1] * 4)


_NT = (((1,), (1,)), ((), ()))
_TN = (((0,), (0,)), ((), ()))


def _group_rows(ref, heads):
    return jnp.concatenate([ref[:, h * HEAD_DIM:(h + 1) * HEAD_DIM] for h in heads], axis=0)


def _attn_valid(grp):
    qi = np.arange(grp * CHUNK)[:, None] % CHUNK
    sj = np.arange(2 * CHUNK)[None, :]
    cur = (sj >= CHUNK) & (sj - CHUNK <= qi)
    prev = (sj < CHUNK) & (sj > qi)
    return jnp.asarray(np.stack([cur, cur | prev]).astype(np.float32))


def _valid_spec(grp):
    return pl.BlockSpec((None, grp * CHUNK, 2 * CHUNK), lambda n: (jnp.minimum(n, 1), 0, 0))


def _attn_group_probs(q, kk, sinks, valid, grp):
    rows = grp * CHUNK
    s = lax.dot_general(q, kk, _NT, preferred_element_type=F32)
    s = jnp.where(valid, s, NEG_INF)
    r = lax.broadcasted_iota(jnp.int32, (rows, 1), 0)
    sink = jnp.full((rows, 1), sinks[grp - 1], F32)
    for g in range(grp - 2, -1, -1):
        sink = jnp.where(r < (g + 1) * CHUNK, sinks[g], sink)
    m = jnp.maximum(jnp.max(s, axis=1, keepdims=True), sink)
    p = jnp.exp(s - m)
    ps = jnp.exp(sink - m)
    inv = 1.0 / (jnp.sum(p, axis=1, keepdims=True) + ps)
    return p * inv, ps * inv


def _kv_specs(width, nb):
    prev = pl.BlockSpec((CHUNK, width), lambda n: (jnp.maximum(n - 1, 0), 0))
    cur = pl.BlockSpec((CHUNK, width), lambda n: (n, 0))
    return prev, cur


def _attn_fwd(qr, kr, vr, sinks, *, name):
    T, QW = qr.shape
    KW = kr.shape[1]
    HQ, HK = QW // HEAD_DIM, KW // HEAD_DIM
    grp = HQ // HK
    nb = T // CHUNK

    def body(q_ref, kp_ref, kc_ref, vp_ref, vc_ref, s_ref, ok_ref, o_ref):
        valid = ok_ref[...] > 0.5
        for kh in range(HK):
            ks = slice(kh * HEAD_DIM, (kh + 1) * HEAD_DIM)
            heads = list(range(kh * grp, (kh + 1) * grp))
            q = _group_rows(q_ref, heads)
            kk = jnp.concatenate([kp_ref[:, ks], kc_ref[:, ks]], axis=0)
            vv = jnp.concatenate([vp_ref[:, ks], vc_ref[:, ks]], axis=0)
            p, _ = _attn_group_probs(q, kk, [s_ref[0, h] for h in heads], valid, grp)
            o = jnp.dot(p.astype(BF16), vv, preferred_element_type=F32).astype(BF16)
            for g, h in enumerate(heads):
                o_ref[:, h * HEAD_DIM:(h + 1) * HEAD_DIM] = o[g * CHUNK:(g + 1) * CHUNK]

    kp, kc = _kv_specs(KW, nb)
    return pl.pallas_call(
        body, name=name, out_shape=jax.ShapeDtypeStruct((T, QW), BF16), grid=(nb,),
        in_specs=[pl.BlockSpec((CHUNK, QW), lambda n: (n, 0)), kp, kc, kp, kc,
                  pl.BlockSpec(memory_space=pltpu.SMEM), _valid_spec(grp)],
        out_specs=pl.BlockSpec((CHUNK, QW), lambda n: (n, 0)),
        compiler_params=pltpu.CompilerParams(dimension_semantics=("arbitrary",), vmem_limit_bytes=_vmem(8 << 20)),
    )(qr, kr, kr, vr, vr, sinks, _attn_valid(grp))


def _attn_bwd(qr, kr, vr, sinks, do, *, name):
    T, QW = qr.shape
    KW = kr.shape[1]
    HQ, HK = QW // HEAD_DIM, KW // HEAD_DIM
    grp = HQ // HK
    nb = T // CHUNK

    def body(q_ref, kp_ref, kc_ref, vp_ref, vc_ref, s_ref, do_ref, ok_ref,
             dq_ref, dkp_ref, dkc_ref, dvp_ref, dvc_ref, ds_ref):
        n = pl.program_id(0)
        valid = ok_ref[...] > 0.5
        lane = lax.broadcasted_iota(jnp.int32, (1, LANES), 1)
        dsink = jnp.zeros((1, LANES), F32)
        for kh in range(HK):
            ks = slice(kh * HEAD_DIM, (kh + 1) * HEAD_DIM)
            heads = list(range(kh * grp, (kh + 1) * grp))
            q = _group_rows(q_ref, heads)
            doh = _group_rows(do_ref, heads)
            kk = jnp.concatenate([kp_ref[:, ks], kc_ref[:, ks]], axis=0)
            vv = jnp.concatenate([vp_ref[:, ks], vc_ref[:, ks]], axis=0)
            p, ps = _attn_group_probs(q, kk, [s_ref[0, h] for h in heads], valid, grp)
            dp = lax.dot_general(doh, vv, _NT, preferred_element_type=F32)
            delta = jnp.sum(p * dp, axis=1, keepdims=True)
            ds = (p * (dp - delta)).astype(BF16)
            dv = lax.dot_general(p.astype(BF16), doh, _TN, preferred_element_type=F32)
            dk = lax.dot_general(ds, q, _TN, preferred_element_type=F32)
            dq = jnp.dot(ds, kk, preferred_element_type=F32)
            psd = ps * delta
            for g, h in enumerate(heads):
                dq_ref[:, h * HEAD_DIM:(h + 1) * HEAD_DIM] = dq[g * CHUNK:(g + 1) * CHUNK]
                dsink = dsink + jnp.where(
                    lane == h, -jnp.sum(psd[g * CHUNK:(g + 1) * CHUNK], axis=0, keepdims=True), 0.0)
            dkp_ref[:, ks] = dk[:CHUNK]
            dkc_ref[:, ks] = dk[CHUNK:]
            dvp_ref[:, ks] = dv[:CHUNK]
            dvc_ref[:, ks] = dv[CHUNK:]

        @pl.when(n == 0)
        def _():
            ds_ref[...] = dsink

        @pl.when(n > 0)
        def _():
            ds_ref[...] += dsink

    kp, kc = _kv_specs(KW, nb)
    qspec = pl.BlockSpec((CHUNK, QW), lambda n: (n, 0))
    kout = pl.BlockSpec((CHUNK, KW), lambda n: (n, 0))
    return pl.pallas_call(
        body, name=name,
        out_shape=[jax.ShapeDtypeStruct((T, QW), F32)] + [jax.ShapeDtypeStruct((T, KW), F32)] * 4
        + [jax.ShapeDtypeStruct((1, LANES), F32)],
        grid=(nb,),
        in_specs=[qspec, kp, kc, kp, kc, pl.BlockSpec(memory_space=pltpu.SMEM), qspec, _valid_spec(grp)],
        out_specs=[qspec, kout, kout, kout, kout, pl.BlockSpec((1, LANES), lambda n: (0, 0))],
        compiler_params=pltpu.CompilerParams(dimension_semantics=("arbitrary",), vmem_limit_bytes=_vmem(12 << 20)),
    )(qr, kr, kr, vr, vr, sinks, do, _attn_valid(grp))


def _rope_bwd(dq, dkp, dkc, dvp, dvc, ctab, stab, *, name):
    T, QW = dq.shape
    KW = dkp.shape[1]
    nb = T // CHUNK
    scale = HEAD_DIM ** -0.5
    width = QW + 2 * KW

    def body(dq_ref, dkc_ref, dkn_ref, dvc_ref, dvn_ref, c_ref, s_ref, o_ref, db_ref):
        n = pl.program_id(0)
        c = c_ref[...]
        s = s_ref[...]
        has_next = (n < nb - 1).astype(F32)
        dqv = dq_ref[...]
        dk = dkc_ref[...] + has_next * dkn_ref[...]
        dv = dvc_ref[...] + has_next * dvn_ref[...]
        dq_pre = (dqv * _wide(c, QW) + _swap8(dqv * _wide(s, QW))) * scale
        dk_pre = dk * _wide(c, KW) + _swap8(dk * _wide(s, KW))
        o_ref[:, :QW] = dq_pre.astype(BF16)
        o_ref[:, QW:QW + KW] = dk_pre.astype(BF16)
        o_ref[:, QW + KW:] = dv.astype(BF16)
        part = jnp.concatenate([jnp.sum(dq_pre, axis=0, keepdims=True), jnp.sum(dk_pre, axis=0, keepdims=True),
                                jnp.sum(dv, axis=0, keepdims=True)], axis=1)

        @pl.when(n == 0)
        def _():
            db_ref[...] = part

        @pl.when(n > 0)
        def _():
            db_ref[...] += part

    cur = lambda w: pl.BlockSpec((CHUNK, w), lambda n: (n, 0))
    nxt = lambda w: pl.BlockSpec((CHUNK, w), lambda n: (jnp.minimum(n + 1, nb - 1), 0))
    return pl.pallas_call(
        body, name=name,
        out_shape=[jax.ShapeDtypeStruct((T, width), BF16), jax.ShapeDtypeStruct((1, width), F32)],
        grid=(nb,),
        in_specs=[cur(QW), cur(KW), nxt(KW), cur(KW), nxt(KW), cur(LANES), cur(LANES)],
        out_specs=[cur(width), pl.BlockSpec((1, width), lambda n: (0, 0))],
        compiler_params=pltpu.CompilerParams(dimension_semantics=("arbitrary",), vmem_limit_bytes=_vmem(8 << 20)),
    )(dq, dkc, dkp, dvc, dvp, ctab, stab)


def _cast_block(w, l, axis, chip_arr, *, name):
    _, Ks, Ns = w.shape
    tk = _pick(Ks, (512, 352, 256, 128))
    nk = Ks // tk
    full = (Ks * N_CHIPS, Ns) if axis == 0 else (Ks, Ns * N_CHIPS)

    def body(p_ref, w_ref, o_ref):
        o_ref[...] = w_ref[...].astype(BF16)

    if axis == 0:
        out_spec = pl.BlockSpec((tk, Ns), lambda i, p: (p[0] * nk + i, 0))
    else:
        out_spec = pl.BlockSpec((tk, Ns), lambda i, p: (i, p[0]))
    grid_spec = pltpu.PrefetchScalarGridSpec(
        num_scalar_prefetch=1, grid=(nk,),
        in_specs=[pl.BlockSpec((None, tk, Ns), lambda i, p: (l, i, 0))], out_specs=out_spec)
    return pl.pallas_call(
        body, name=name, out_shape=jax.ShapeDtypeStruct(full, BF16), grid_spec=grid_spec,
        compiler_params=pltpu.CompilerParams(dimension_semantics=("arbitrary",),
                                             vmem_limit_bytes=_vmem(4 * tk * Ns * 6)),
    )(chip_arr, w)


def _adamw_math(w, g, m, v):
    m = ADAM_B1 * m + (1.0 - ADAM_B1) * g
    v = ADAM_B2 * v + (1.0 - ADAM_B2) * (g * g)
    m_hat = m / (1.0 - ADAM_B1 ** ADAM_STEP)
    v_hat = v / (1.0 - ADAM_B2 ** ADAM_STEP)
    delta = -ADAM_LR * (m_hat / (jnp.sqrt(v_hat) + ADAM_EPS) + ADAM_WD * w)
    return delta, m, v


def _adamw_layer(w, m, v, g, l, outs, *, name):
    _, K, N = w.shape
    tk = _pick(K, (256, 176, 128))

    def body(w_ref, m_ref, v_ref, g_ref, _g, _d, _m, _v, go_ref, d_ref, mo_ref, vo_ref):
        gv = g_ref[...]
        d, mn, vn = _adamw_math(w_ref[...], gv, m_ref[...], v_ref[...])
        go_ref[...] = gv
        d_ref[...] = d
        mo_ref[...] = mn
        vo_ref[...] = vn

    layer = pl.BlockSpec((None, tk, N), lambda i: (l, i, 0))
    any_spec = pl.BlockSpec(memory_space=pl.ANY)
    sd = jax.ShapeDtypeStruct(w.shape, F32)
    return pl.pallas_call(
        body, name=name, out_shape=[sd, sd, sd, sd], grid=(K // tk,),
        in_specs=[layer, layer, layer, pl.BlockSpec((tk, N), lambda i: (i, 0))] + [any_spec] * 4,
        out_specs=[layer] * 4, input_output_aliases={4: 0, 5: 1, 6: 2, 7: 3},
        compiler_params=pltpu.CompilerParams(dimension_semantics=("arbitrary",),
                                             vmem_limit_bytes=_vmem(2 * 8 * tk * N * 4 + 6 * tk * N * 4)),
    )(w, m, v, g, *outs)


def _adamw_small(w, g, m, v, *, name):
    def body(w_ref, g_ref, m_ref, v_ref, d_ref, mo_ref, vo_ref):
        d, mn, vn = _adamw_math(w_ref[...], g_ref[...], m_ref[...], v_ref[...])
        d_ref[...] = d
        mo_ref[...] = mn
        vo_ref[...] = vn

    sd = jax.ShapeDtypeStruct(w.shape, F32)
    return pl.pallas_call(body, name=name, out_shape=[sd, sd, sd])(w, g, m, v)


def _my_place():
    return lax.axis_index("x"), lax.axis_index("y"), lax.axis_index("c")


def _peer_chips(x, y):
    return [(1 - x, y), (x, 1 - y), (1 - x, 1 - y)]


_HBM = pl.BlockSpec(memory_space=pltpu.HBM)
_SEM = pl.BlockSpec(memory_space=pltpu.SEMAPHORE)
_EFFECT = pltpu.SideEffectType.DATAFLOW_SIDE_EFFECTING


def _split_start(name, bufs, n_copies, make_copies, after):
    nb = len(bufs)

    def body(*refs):
        send_sems, recv_sems = refs[nb + 1], refs[nb + 2]
        token = refs[2 * nb + 3]
        sends, _ = make_copies(refs[:nb], send_sems, recv_sems)
        for cp in sends:
            cp.start()
        token[...] = jnp.zeros_like(token)

    res = pl.pallas_call(
        body, name=name,
        out_shape=(pltpu.SemaphoreType.DMA((n_copies,)), pltpu.SemaphoreType.DMA((n_copies,)),
                   *[pltpu.HBM(b.shape, b.dtype) for b in bufs], jax.ShapeDtypeStruct((8, LANES), F32)),
        in_specs=[_HBM] * nb + [pl.BlockSpec(memory_space=pl.ANY)],
        out_specs=(_SEM, _SEM, *[_HBM] * nb, pl.BlockSpec(memory_space=pltpu.VMEM)),
        input_output_aliases={k: 2 + k for k in range(nb)},
        compiler_params=pltpu.CompilerParams(has_side_effects=_EFFECT),
    )(*[pltpu.with_memory_space_constraint(b, pltpu.HBM) for b in bufs],
      after[0] if isinstance(after, (list, tuple)) else after)
    return res[0], res[1], list(res[2:2 + nb]), res[2 + nb]


def _split_wait(name, bufs, sems, make_copies, after):
    nb = len(bufs)
    after = list(after) if isinstance(after, (list, tuple)) else [after]

    def body(*refs):
        send_sems, recv_sems = refs[nb], refs[nb + 1]
        sends, recvs = make_copies(refs[:nb], send_sems, recv_sems)
        for cp in sends:
            cp.wait_send()
        for cp in recvs:
            cp.wait_recv()

    res = pl.pallas_call(
        body, name=name,
        out_shape=tuple(pltpu.HBM(b.shape, b.dtype) for b in bufs),
        in_specs=[_HBM] * nb + [_SEM, _SEM] + [pl.BlockSpec(memory_space=pl.ANY)] * len(after),
        out_specs=tuple([_HBM] * nb),
        input_output_aliases={k: k for k in range(nb)},
        compiler_params=pltpu.CompilerParams(has_side_effects=_EFFECT),
    )(*bufs, sems[0], sems[1], *after)
    return list(res)


def _remote(src, dst, send_sems, recv_sems, k, target):
    return pltpu.make_async_remote_copy(src_ref=src, dst_ref=dst, send_sem=send_sems.at[k],
                                        recv_sem=recv_sems.at[k], device_id=target, device_id_type=MESH)


def _ag_region(ref, axis, chip, half):
    K, N = ref.shape
    if axis == 0:
        hs = K // N_CHIPS // 2
        assert hs % 16 == 0
        return ref.at[pl.ds(pl.multiple_of((2 * chip + half) * hs, 16), hs), :]
    ns, hk = N // N_CHIPS, K // 2
    assert ns % LANES == 0 and hk % 16 == 0
    return ref.at[pl.ds(pl.multiple_of(half * hk, 16), hk), pl.ds(pl.multiple_of(chip * ns, LANES), ns)]


def _ag_copies(stage, axes):
    n = len(axes)

    def make(bufs, send_sems, recv_sems):
        x, y, c = _my_place()
        me = 2 * x + y
        sends, recvs = [], []
        for j, (px, py) in enumerate(_peer_chips(x, y)):
            other = 2 * px + py
            for w in range(n):
                k = j * n + w
                if stage == 1:
                    src, target = _ag_region(bufs[w], axes[w], me, c), (px, py, c)
                    land = _ag_region(bufs[w], axes[w], other, c)
                else:
                    src, target = _ag_region(bufs[w], axes[w], other, c), (x, y, 1 - c)
                    land = _ag_region(bufs[w], axes[w], other, 1 - c)
                sends.append(_remote(src, src, send_sems, recv_sems, k, target))
                recvs.append(_remote(land, land, send_sems, recv_sems, k, target))
        return sends, recvs

    return make


def _half_shape(shape, axis):
    K, N = shape
    return (K, N // 2) if axis == 0 else (K // 2, N)


def _core_half(ref, axis, half):
    K, N = ref.shape
    if axis == 0:
        return ref.at[:, pl.ds(pl.multiple_of(half * (N // 2), LANES), N // 2)]
    return ref.at[pl.ds(pl.multiple_of(half * (K // 2), 16), K // 2), :]


def _chip_block(ref, axis, chip):
    K, N = ref.shape
    if axis == 0:
        return ref.at[pl.ds(pl.multiple_of(chip * (K // N_CHIPS), 16), K // N_CHIPS), :]
    return ref.at[:, pl.ds(pl.multiple_of(chip * (N // N_CHIPS), LANES), N // N_CHIPS)]


def _rs_sibling_copies(axes):
    n = len(axes)

    def make(bufs, send_sems, recv_sems):
        x, y, c = _my_place()
        sends = [_remote(_core_half(bufs[w], axes[w], 1 - c), bufs[n + w], send_sems, recv_sems, w, (x, y, 1 - c))
                 for w in range(n)]
        recvs = [_remote(bufs[n + w], bufs[n + w], send_sems, recv_sems, w, (x, y, 1 - c)) for w in range(n)]
        return sends, recvs

    return make


def _rs_chip_copies(axes):
    n = len(axes)

    def make(bufs, send_sems, recv_sems):
        x, y, c = _my_place()
        sends, recvs = [], []
        for j, (px, py) in enumerate(_peer_chips(x, y)):
            for w in range(n):
                k = j * n + w
                sends.append(_remote(_chip_block(bufs[w], axes[w], 2 * px + py), bufs[n + w].at[j],
                                     send_sems, recv_sems, k, (px, py, c)))
                recvs.append(_remote(bufs[n + w].at[j], bufs[n + w].at[j], send_sems, recv_sems, k, (px, py, c)))
        return sends, recvs

    return make


def _rs_fill_copies(axes):
    n = len(axes)

    def make(bufs, send_sems, recv_sems):
        x, y, c = _my_place()
        sends = [_remote(_core_half(bufs[w], axes[w], c), _core_half(bufs[w], axes[w], c),
                         send_sems, recv_sems, w, (x, y, 1 - c)) for w in range(n)]
        recvs = [_remote(_core_half(bufs[w], axes[w], 1 - c), _core_half(bufs[w], axes[w], 1 - c),
                         send_sems, recv_sems, w, (x, y, 1 - c)) for w in range(n)]
        return sends, recvs

    return make


def _chip_sum(g, r, axis, place, *, name):
    hk, hn = r.shape
    bk, bn = (hk // N_CHIPS, hn) if axis == 0 else (hk, hn // N_CHIPS)
    tk = _pick(bk, (512, 352, 256, 128))
    nk = bk // tk

    def body(p_ref, g_ref, r_ref, b_ref, own_ref):
        s = g_ref[...].astype(F32) + r_ref[...].astype(F32)
        b_ref[...] = s.astype(BF16)

        @pl.when(pl.program_id(1) == p_ref[0])
        def _():
            own_ref[...] = s

    if axis == 0:
        g_spec = pl.BlockSpec((tk, bn), lambda i, j, p: (j * nk + i, p[1]))
        r_spec = pl.BlockSpec((tk, bn), lambda i, j, p: (j * nk + i, 0))
    else:
        g_spec = pl.BlockSpec((tk, bn), lambda i, j, p: (p[1] * nk + i, j))
        r_spec = pl.BlockSpec((tk, bn), lambda i, j, p: (i, j))
    grid_spec = pltpu.PrefetchScalarGridSpec(
        num_scalar_prefetch=1, grid=(nk, N_CHIPS), in_specs=[g_spec, r_spec],
        out_specs=[r_spec, pl.BlockSpec((tk, bn), lambda i, j, p: (i, 0))])
    return pl.pallas_call(
        body, name=name,
        out_shape=[jax.ShapeDtypeStruct(r.shape, BF16), jax.ShapeDtypeStruct((bk, bn), F32)],
        grid_spec=grid_spec,
        compiler_params=pltpu.CompilerParams(dimension_semantics=("arbitrary", "arbitrary"),
                                             vmem_limit_bytes=_vmem(2 * tk * bn * 10 + 3 * tk * bn * 4)),
    )(place, g, r)


def _final_sum(own, recv, axis, place, *, name):
    _, bk, bn = recv.shape
    tk = _pick(bk, (256, 176, 128))
    nk = bk // tk

    def body(p_ref, o_ref, r_ref, out_ref):
        out_ref[...] = ((o_ref[...] + r_ref[0].astype(F32)) + r_ref[1].astype(F32)) + r_ref[2].astype(F32)

    own_spec = pl.BlockSpec((tk, bn), lambda i, p: (i, 0))
    if axis == 0:
        out_shape, out_spec = (bk, 2 * bn), pl.BlockSpec((tk, bn), lambda i, p: (i, p[1]))
    else:
        out_shape, out_spec = (2 * bk, bn), pl.BlockSpec((tk, bn), lambda i, p: (p[1] * nk + i, 0))
    grid_spec = pltpu.PrefetchScalarGridSpec(
        num_scalar_prefetch=1, grid=(nk,),
        in_specs=[own_spec, pl.BlockSpec((3, tk, bn), lambda i, p: (0, i, 0))], out_specs=out_spec)
    return pl.pallas_call(
        body, name=name, out_shape=jax.ShapeDtypeStruct(out_shape, F32), grid_spec=grid_spec,
        compiler_params=pltpu.CompilerParams(dimension_semantics=("arbitrary",),
                                             vmem_limit_bytes=_vmem(2 * tk * bn * 14 + 4 * tk * bn * 4)),
    )(place, own, recv)


def _allreduce_small(p, after=()):
    n_after = len(after)

    def body(*refs):
        p_ref = refs[0]
        o_ref, r0, r1, r2, send_sems, recv_sems = refs[1 + n_after:]
        x, y, c = _my_place()
        o_ref[...] = p_ref[...]
        for s, (peer, rbuf) in enumerate([((x, y, 1 - c), r0), ((1 - x, y, c), r1), ((x, 1 - y, c), r2)]):
            cp = pltpu.make_async_remote_copy(src_ref=o_ref, dst_ref=rbuf, send_sem=send_sems.at[s],
                                              recv_sem=recv_sems.at[s], device_id=peer, device_id_type=MESH)
            cp.start()
            cp.wait()
            o_ref[...] = o_ref[...] + rbuf[...]

    vm = pl.BlockSpec(memory_space=pltpu.VMEM)
    return pl.pallas_call(
        body, name="allreduce_small", out_shape=jax.ShapeDtypeStruct(p.shape, F32),
        in_specs=[vm] + [pl.BlockSpec(memory_space=pl.ANY)] * n_after, out_specs=vm,
        scratch_shapes=[pltpu.VMEM(p.shape, F32)] * 3 + [pltpu.SemaphoreType.DMA((3,))] * 2,
        compiler_params=pltpu.CompilerParams(vmem_limit_bytes=_vmem(6 * _nbytes(p.shape, F32))),
    )(p, *after)


def _pack_rows(parts):
    rows, metas = [], []
    for a in parts:
        flat = a.reshape(-1)
        nrow = -(-flat.shape[0] // LANES)
        nrow = -(-nrow // 8) * 8
        flat = jnp.pad(flat, (0, nrow * LANES - flat.shape[0]))
        rows.append(flat.reshape(nrow, LANES))
        metas.append((a.shape, nrow))
    return jnp.concatenate(rows, axis=0), metas


def _unpack_rows(packed, metas):
    out, r0 = [], 0
    for shape, nrow in metas:
        size = int(np.prod(shape))
        out.append(packed[r0:r0 + nrow].reshape(-1)[:size].reshape(shape))
        r0 += nrow
    return out


def kernel(x, positions, pre_mix_g, post_mix_g, pre_ffn_g, post_ffn_g, a_w_in, a_b_in, a_ln_g, a_ln_b, a_w_s, a_b_s, a_w_out, b_w_qkv, b_b_qkv, b_sinks, b_w_o, ffn_w_gu, ffn_w_down, loss_target, m_pre_mix_g, m_post_mix_g, m_pre_ffn_g, m_post_ffn_g, m_a_w_in, m_a_b_in, m_a_ln_g, m_a_ln_b, m_a_w_s, m_a_b_s, m_a_w_out, m_b_w_qkv, m_b_b_qkv, m_b_sinks, m_b_w_o, m_ffn_w_gu, m_ffn_w_down, v_pre_mix_g, v_post_mix_g, v_pre_ffn_g, v_post_ffn_g, v_a_w_in, v_a_b_in, v_a_ln_g, v_a_ln_b, v_a_w_s, v_a_b_s, v_a_w_out, v_b_w_qkv, v_b_b_qkv, v_b_sinks, v_b_w_o, v_ffn_w_gu, v_ffn_w_down):
    depth, D = pre_mix_g.shape
    xi, yi, ci = _my_place()
    chip = 2 * xi + yi
    place = jnp.stack([chip, ci]).astype(jnp.int32)

    stacked = {"a_w_in": (a_w_in, m_a_w_in, v_a_w_in), "a_w_out": (a_w_out, m_a_w_out, v_a_w_out),
               "b_w_qkv": (b_w_qkv, m_b_w_qkv, v_b_w_qkv), "b_w_o": (b_w_o, m_b_w_o, v_b_w_o),
               "ffn_w_gu": (ffn_w_gu, m_ffn_w_gu, v_ffn_w_gu), "ffn_w_down": (ffn_w_down, m_ffn_w_down, v_ffn_w_down)}
    cut = {"a_w_in": 1, "a_w_out": 0, "b_w_qkv": 1, "b_w_o": 0, "ffn_w_gu": 1, "ffn_w_down": 0}

    def layer_keys(i):
        mix = [("a_w_in", i // 2), ("a_w_out", i // 2)] if i % 2 == 0 else [("b_w_qkv", i // 2), ("b_w_o", i // 2)]
        return mix + [("ffn_w_gu", i), ("ffn_w_down", i)]

    def dep(a, toks):
        for t in toks:
            a = a + t[:1, :1]
        return a

    W = {}
    for i in range(depth):
        for nm, l in layer_keys(i):
            W[(nm, l)] = _cast_block(stacked[nm][0], l, cut[nm], place, name=f"cast_{nm}_{l}")

    def gather(tag, keys, after):
        axes = [cut[nm] for nm, _ in keys]
        for stage in (1, 2):
            ss, rs, bufs, tok = _split_start(f"ag{stage}_start_{tag}", [W[k] for k in keys], 3 * len(keys),
                                             _ag_copies(stage, axes), after)
            after = yield tok
            bufs = _split_wait(f"ag{stage}_wait_{tag}", bufs, (ss, rs), _ag_copies(stage, axes), after)
            W.update(zip(keys, bufs))
        yield None

    nq = b_b_qkv.shape[1]
    bq_full = jnp.zeros((b_b_qkv.shape[0], N_CHIPS * nq), F32)
    bq_full = lax.dynamic_update_slice(bq_full, jnp.where(ci == 0, b_b_qkv, 0.0), (0, chip * nq))
    bq_packed, bq_meta = _pack_rows([bq_full])
    bq_gathered = _allreduce_small(bq_packed)
    b_qkv_full = _unpack_rows(bq_gathered, bq_meta)[0]

    first = gather("0m", layer_keys(0)[:2], bq_gathered)
    tok = next(first)
    tok = first.send([tok] + [W[k] for k in layer_keys(0)[2:] + layer_keys(1)])
    first.send(tok)

    h = x[0]
    target = loss_target[0]
    ctab, stab = _rope_tables(positions[0])
    q_width = W[("b_w_o", 0)].shape[0]
    kv_width = N_KV_HEADS * HEAD_DIM
    row = lambda a, i: a[i:i + 1]

    saved = []
    hn = None
    for i in range(depth):
        j = i // 2
        s = {"h": h}
        ffn_w = None
        if i == 0:
            ffn_w = gather("0f", layer_keys(0)[2:], W[("a_w_out", 0)])
            toks = [next(ffn_w)]
            nxt = gather("1", layer_keys(1), toks[0])
            toks.append(next(nxt))
            hn = _rms_fwd(h, dep(row(pre_mix_g, i), toks), out_dtype=BF16, name=f"rms_pre_mix_{i}")
        elif i + 1 < depth:
            nxt = gather(str(i + 1), layer_keys(i + 1), h)
            toks = [next(nxt)]
        else:
            toks = []
        s["hn"] = hn
        if i % 2 == 0:
            pre = _matmul(hn, W[("a_w_in", j)], mode="nn", bias=dep(row(a_b_in, j), toks), out_dtype=F32,
                          name=f"gmlp_in_{i}")
            gated = _sgu_fwd(pre, row(a_ln_g, j), row(a_ln_b, j), a_w_s[j], a_b_s[j].T, name=f"sgu_fwd_{i}")
            mix = _matmul(gated, W[("a_w_out", j)], mode="nn", out_dtype=F32, name=f"gmlp_out_{i}")
            s.update(pre=pre, gated=gated)
        else:
            qkv = _matmul(hn, W[("b_w_qkv", j)], mode="nn", bias=dep(row(b_qkv_full, j), toks), out_dtype=F32,
                          name=f"attn_qkv_{i}")
            qr, kr, vr = _rope_fwd(qkv, ctab, stab, q_width=q_width, kv_width=kv_width, name=f"rope_fwd_{i}")
            o = _attn_fwd(qr, kr, vr, row(b_sinks, j), name=f"attn_fwd_{i}")
            mix = _matmul(o, W[("b_w_o", j)], mode="nn", out_dtype=F32, name=f"attn_o_{i}")
            s.update(qr=qr, kr=kr, vr=vr, o=o)
        s["mix"] = mix
        toks = [ffn_w.send(mix)] if ffn_w else []
        h1, fn = _rms_res_norm(h, mix, dep(row(post_mix_g, i), toks), row(pre_ffn_g, i), name=f"rms_post_mix_{i}")
        if ffn_w:
            ffn_w.send(h1)
        s["h1"] = h1
        g_pre, u_pre, act = _ffn_up(fn, W[("ffn_w_gu", i)][None], 0, name=f"ffn_up_{i}")
        f = _matmul(act, W[("ffn_w_down", i)], mode="nn", out_dtype=F32, name=f"ffn_down_{i}")
        if i + 1 < depth:
            toks = [nxt.send(f)]
            h, hn = _rms_res_norm(h1, f, dep(row(post_ffn_g, i), toks), row(pre_mix_g, i + 1),
                                  name=f"rms_post_ffn_{i}")
            nxt.send(h)
        else:
            h = _rms_res(h1, f, row(post_ffn_g, i), name=f"rms_post_ffn_{i}")
        s.update(fn=fn, g_pre=g_pre, u_pre=u_pre, act=act, f=f)
        saved.append(s)

    dh, df, loss_part, g_last = _loss_and_grad(h, target, saved[-1]["f"], row(post_ffn_g, depth - 1), name="loss")
    loss = lax.psum(loss_part[0, 0], ("x", "y", "c"))

    big_out = {nm: tuple(lax.empty(w.shape, F32) for _ in range(4)) for nm, (w, _, _) in stacked.items()}

    def reduce_group(i, keys, grads):
        axes = [cut[nm] for nm, _ in keys]
        n = len(keys)
        lands = [lax.empty(_half_shape(g.shape, ax), BF16) for g, ax in zip(grads, axes)]
        ss, rs, bufs, tok = _split_start(f"rs_sibling_start_{i}", list(grads) + lands, n, _rs_sibling_copies(axes),
                                         place)
        after = yield tok
        bufs = _split_wait(f"rs_sibling_wait_{i}", bufs, (ss, rs), _rs_sibling_copies(axes), after)
        sums = [_chip_sum(bufs[w], bufs[n + w], axes[w], place, name=f"chip_sum_{keys[w][0]}_{keys[w][1]}")
                for w in range(n)]
        lands = [lax.empty((3,) + own.shape, BF16) for _, own in sums]
        ss, rs, bufs, tok = _split_start(f"rs_chip_start_{i}", [sb for sb, _ in sums] + lands, 3 * n,
                                         _rs_chip_copies(axes), place)
        after = yield tok
        bufs = _split_wait(f"rs_chip_wait_{i}", bufs, (ss, rs), _rs_chip_copies(axes), after)
        blocks = [_final_sum(sums[w][1], bufs[n + w], axes[w], place, name=f"final_sum_{keys[w][0]}_{keys[w][1]}")
                  for w in range(n)]
        ss, rs, bufs, tok = _split_start(f"rs_fill_start_{i}", blocks, n, _rs_fill_copies(axes), place)
        after = yield tok
        blocks = _split_wait(f"rs_fill_wait_{i}", bufs, (ss, rs), _rs_fill_copies(axes), after)
        for (nm, l), g in zip(keys, blocks):
            w, m, v = stacked[nm]
            big_out[nm] = tuple(_adamw_layer(w, m, v, g, l, big_out[nm], name=f"adamw_{nm}_{l}"))
        yield None

    reducing = []

    def advance(after):
        toks = []
        for gen in list(reducing):
            tok = gen.send(after)
            if tok is None:
                reducing.remove(gen)
            else:
                toks.append(tok)
        return toks

    small = {}
    g_pre_mix, g_post_mix, g_pre_ffn, g_post_ffn = [None] * depth, [None] * depth, [None] * depth, [None] * depth
    g_post_ffn[depth - 1] = g_last
    toks = []
    early = []
    for i in reversed(range(depth)):
        j = i // 2
        s = saved[i]
        g_down = _matmul(s["act"], df, mode="tn", out_dtype=BF16, after=toks, name=f"ffn_down_dw_{i}")
        dg_, du_ = _ffn_down_dx(df, W[("ffn_w_down", i)][None], 0, s["g_pre"], s["u_pre"], g_down,
                                name=f"ffn_down_dx_{i}")
        hid = dg_.shape[1]
        tile = _pick(hid, (1408, 768, 512, 256, 128))
        w_gu = W[("ffn_w_gu", i)]
        g_gu = lax.empty(w_gu.shape, BF16)
        g_gu = _matmul(s["fn"], dg_, mode="tn", into=g_gu, tq=tile, out_dtype=BF16, name=f"ffn_g_dw_{i}")
        g_gu = _matmul(s["fn"], du_, mode="tn", into=g_gu, tq=tile, q_off=hid // tile, out_dtype=BF16,
                       name=f"ffn_u_dw_{i}")
        dfn_g = _matmul(dg_, w_gu, mode="nt", tr=hid, out_dtype=F32, after=[g_gu], name=f"ffn_g_dx_{i}")
        dfn = _matmul(du_, w_gu, mode="nt", tr=hid, b_r_off=1, bias=dfn_g, out_dtype=F32, name=f"ffn_u_dx_{i}")
        toks = advance(dfn)
        if i == 0:
            gen = reduce_group("0f", layer_keys(0)[2:], [g_gu, g_down])
            toks.append(next(gen))
            reducing.append(gen)
        dh1, dmix, g_pre_ffn[i], g_post_mix[i] = _rms_bwd_chain(
            s["h1"], dep(row(pre_ffn_g, i), toks), dfn, dh, s["mix"], row(post_mix_g, i), name=f"rms_ffn_mix_bwd_{i}")
        if i % 2 == 0:
            g_out = _matmul(s["gated"], dmix, mode="tn", out_dtype=BF16, name=f"gmlp_out_dw_{i}")
            dgated = _matmul(dmix, W[("a_w_out", j)], mode="nt", out_dtype=BF16, after=[g_out],
                             name=f"gmlp_out_dx_{i}")
            toks = advance(dgated) if i == 0 else []
            dpre, dws, dbsT, dlng, dlnb, dbin = _sgu_bwd(s["pre"], dgated, dep(row(a_ln_g, j), toks), row(a_ln_b, j),
                                                         a_w_s[j], a_b_s[j].T, name=f"sgu_bwd_{i}")
            small[("a_w_s", j)] = dws
            small[("a_b_s", j)] = dbsT.T
            small[("a_ln_g", j)] = dlng
            small[("a_ln_b", j)] = dlnb
            small[("a_b_in", j)] = dbin
            g_in = _matmul(s["hn"], dpre, mode="tn", out_dtype=BF16, name=f"gmlp_in_dw_{i}")
            if i == 0:
                last = reduce_group("0m", layer_keys(0)[:2], [g_in, g_out])
                early = [next(last)]
            dhn = _matmul(dpre, W[("a_w_in", j)], mode="nt", out_dtype=F32, after=[g_in] + early,
                          name=f"gmlp_in_dx_{i}")
        else:
            g_out = _matmul(s["o"], dmix, mode="tn", out_dtype=BF16, name=f"attn_o_dw_{i}")
            do = _matmul(dmix, W[("b_w_o", j)], mode="nt", out_dtype=BF16, after=[g_out], name=f"attn_o_dx_{i}")
            dq, dkp, dkc, dvp, dvc, dsk = _attn_bwd(s["qr"], s["kr"], s["vr"], row(b_sinks, j), do,
                                                    name=f"attn_bwd_{i}")
            dqkv, dbq = _rope_bwd(dq, dkp, dkc, dvp, dvc, ctab, stab, name=f"rope_bwd_{i}")
            small[("b_sinks", j)] = dsk[:, :b_sinks.shape[1]]
            small[("b_b_qkv", j)] = dbq
            g_in = _matmul(s["hn"], dqkv, mode="tn", out_dtype=BF16, name=f"attn_qkv_dw_{i}")
            if i == 0:
                last = reduce_group("0m", layer_keys(0)[:2], [g_in, g_out])
                early = [next(last)]
            dhn = _matmul(dqkv, W[("b_w_qkv", j)], mode="nt", out_dtype=F32, after=[g_in] + early,
                          name=f"attn_qkv_dx_{i}")
        toks = advance(dhn)
        if i > 0:
            dh, df, g_pre_mix[i], g_post_ffn[i - 1] = _rms_bwd_chain(
                s["h"], dep(row(pre_mix_g, i), toks), dhn, dh1, saved[i - 1]["f"], row(post_ffn_g, i - 1),
                name=f"rms_mix_ffn_bwd_{i}")
            gen = reduce_group(str(i), layer_keys(i), [g_in, g_out, g_gu, g_down])
            toks = [next(gen)] + advance(dh)
            reducing.append(gen)
        else:
            toks.append(last.send(dhn))
            dh, g_pre_mix[i] = _rms_bwd(s["h"], dep(row(pre_mix_g, i), toks), dhn, dh1, out_dtype=F32,
                                        name=f"rms_pre_mix_bwd_{i}")
            advance(dh)
    grad_x = dh[None]
    assert not reducing

    ready = [big_out[nm][1] for nm in big_out]
    n_a, n_b = a_b_in.shape[0], b_sinks.shape[0]
    stack = lambda key, n: jnp.concatenate([small[(key, j)] for j in range(n)], axis=0)
    small_parts = [
        jnp.concatenate(g_pre_mix, axis=0), jnp.concatenate(g_post_mix, axis=0),
        jnp.concatenate(g_pre_ffn, axis=0), jnp.concatenate(g_post_ffn, axis=0),
        stack("a_b_in", n_a), stack("a_ln_g", n_a), stack("a_ln_b", n_a),
        jnp.stack([small[("a_w_s", j)] for j in range(n_a)]), jnp.stack([small[("a_b_s", j)] for j in range(n_a)]),
        stack("b_b_qkv", n_b), stack("b_sinks", n_b),
    ]
    packed, metas = _pack_rows(small_parts)
    reduced = _allreduce_small(packed, after=ready + [dh])
    while last.send(reduced) is not None:
        pass
    red = _unpack_rows(reduced, metas)
    (gr_pre_mix, gr_post_mix, gr_pre_ffn, gr_post_ffn, gr_b_in, gr_ln_g, gr_ln_b, gr_w_s, gr_b_s,
     gr_b_qkv_full, gr_sinks) = red
    gr_b_qkv = lax.dynamic_slice(gr_b_qkv_full, (0, chip * nq), (gr_b_qkv_full.shape[0], nq))

    grads = {"pre_mix_g": gr_pre_mix, "post_mix_g": gr_post_mix, "pre_ffn_g": gr_pre_ffn, "post_ffn_g": gr_post_ffn,
             "a_b_in": gr_b_in, "a_ln_g": gr_ln_g, "a_ln_b": gr_ln_b, "a_w_s": gr_w_s, "a_b_s": gr_b_s,
             "b_b_qkv": gr_b_qkv, "b_sinks": gr_sinks}
    weights = {"pre_mix_g": (pre_mix_g, m_pre_mix_g, v_pre_mix_g), "post_mix_g": (post_mix_g, m_post_mix_g, v_post_mix_g),
               "pre_ffn_g": (pre_ffn_g, m_pre_ffn_g, v_pre_ffn_g), "post_ffn_g": (post_ffn_g, m_post_ffn_g, v_post_ffn_g),
               "a_b_in": (a_b_in, m_a_b_in, v_a_b_in), "a_ln_g": (a_ln_g, m_a_ln_g, v_a_ln_g),
               "a_ln_b": (a_ln_b, m_a_ln_b, v_a_ln_b), "a_w_s": (a_w_s, m_a_w_s, v_a_w_s), "a_b_s": (a_b_s, m_a_b_s, v_a_b_s),
               "b_b_qkv": (b_b_qkv, m_b_b_qkv, v_b_b_qkv), "b_sinks": (b_sinks, m_b_sinks, v_b_sinks)}
    order = ["pre_mix_g", "post_mix_g", "pre_ffn_g", "post_ffn_g", "a_w_in", "a_b_in", "a_ln_g", "a_ln_b", "a_w_s",
             "a_b_s", "a_w_out", "b_w_qkv", "b_b_qkv", "b_sinks", "b_w_o", "ffn_w_gu", "ffn_w_down"]
    deltas, new_m, new_v = {}, {}, {}
    for nm in order:
        if nm in big_out:
            grads[nm], deltas[nm], new_m[nm], new_v[nm] = big_out[nm]
        else:
            w, m, v = weights[nm]
            deltas[nm], new_m[nm], new_v[nm] = _adamw_small(w, grads[nm], m, v, name="adamw_" + nm)
    return (loss, grad_x, *[grads[nm] for nm in order], *[deltas[nm] for nm in order],
            *[new_m[nm] for nm in order], *[new_v[nm] for nm in order])
```

```python
import functools
import math

import jax
import jax.numpy as jnp
import numpy as np
from jax import lax
from jax.experimental import pallas as pl
from jax.experimental.pallas import tpu as pltpu

F32 = jnp.float32
BF16 = jnp.bfloat16
MESH = pl.DeviceIdType.MESH

HEAD_DIM = 64
N_KV_HEADS = 4
ROPE_DIM = 16
ROPE_THETA = 500000.0
CHUNK = 128
GMLP_GROUPS = 8
RMS_EPS = 1e-6
LN_EPS = 1e-5
NEG_INF = -1e30
ADAM_LR = 0.001
ADAM_B1 = 0.9
ADAM_B2 = 0.999
ADAM_EPS = 1e-08
ADAM_WD = 0.01
ADAM_STEP = 10

N_CHIPS = 4
LANES = 128
VMEM_CAP = 58 * 1024 * 1024


def _vmem(est_bytes):
    assert est_bytes < VMEM_CAP
    return VMEM_CAP


def _pick(n, cands):
    for c in cands:
        if c <= n and n % c == 0:
            return c
    return n


def _nbytes(shape, dtype):
    return int(np.prod(shape)) * jnp.dtype(dtype).itemsize


MATMUL_VMEM_BUDGET = 48 * 1024 * 1024


def _halvings(n, unit):
    out, t = [], n
    while t % unit == 0 and t >= unit:
        out.append(t)
        if t % 2:
            break
        t //= 2
    return out


def _matmul_tiles(P, Q, R, a_bytes, b_bytes, o_bytes, full_addend, tp, tq, tr):
    step_us, bytes_per_us = 0.85, 3.2e6
    best = None
    for p in ([tp] if tp else _halvings(P, LANES)):
        for q in ([tq] if tq else _halvings(Q, LANES)):
            for r in ([tr] if tr else _halvings(R, LANES)):
                nk = R // r
                vm = 2 * (p * r * a_bytes + r * q * b_bytes + p * q * o_bytes + (p * q * 4 if full_addend else 0))
                vm += p * q * 4 * (2 if nk > 1 else 1)
                if vm > MATMUL_VMEM_BUDGET:
                    continue
                exposed = (p * r * a_bytes + r * q * b_bytes + p * q * o_bytes) / bytes_per_us
                key = ((P // p) * (Q // q) * nk * step_us + exposed, nk, abs(p - q))
                if best is None or key < best[0]:
                    best = (key, (p, q, r))
    assert best is not None, (P, Q, R)
    return best[1]


def _matmul(a, b, *, mode, out_dtype, name, a_l=None, b_l=None, bias=None, into=None, o_l=None,
            q_off=0, b_r_off=0, tp=None, tq=None, tr=None, after=()):
    a2 = a.shape[-2:]
    b2 = b.shape[-2:]
    if mode == "nn":
        (P, R), (R2, Q) = a2, b2
    elif mode == "nt":
        (P, R), (Q, R2) = a2, b2
    else:
        (R, P), (R2, Q) = a2, b2
    assert R == R2 or (mode == "nt" and R2 % R == 0), (mode, a.shape, b.shape)
    o_bytes = jnp.dtype(into.dtype if into is not None else out_dtype).itemsize
    full_addend = bias is not None and bias.shape[0] != 1
    tp, tq, tr = _matmul_tiles(P, Q, R, a.dtype.itemsize, b.dtype.itemsize, o_bytes, full_addend, tp, tq, tr)
    assert P % tp == 0 and Q % tq == 0 and R % tr == 0
    nk = R // tr
    dims = {"nn": (((1,), (0,)), ((), ())), "nt": (((1,), (1,)), ((), ())), "tn": (((0,), (0,)), ((), ()))}[mode]

    def lead(l, blk, idx):
        if l is None:
            return pl.BlockSpec(blk, idx)
        return pl.BlockSpec((None,) + blk, lambda i, j, k: (l,) + idx(i, j, k))

    if mode == "nn":
        a_spec = lead(a_l, (tp, tr), lambda i, j, k: (i, k))
        b_spec = lead(b_l, (tr, tq), lambda i, j, k: (k, j))
    elif mode == "nt":
        a_spec = lead(a_l, (tp, tr), lambda i, j, k: (i, k))
        b_spec = lead(b_l, (tq, tr), lambda i, j, k: (j, k + b_r_off))
    else:
        a_spec = lead(a_l, (tr, tp), lambda i, j, k: (k, i))
        b_spec = lead(b_l, (tr, tq), lambda i, j, k: (k, j))
    in_specs = [a_spec, b_spec]
    args = [a, b]
    if bias is not None:
        if bias.shape[0] == 1:
            in_specs.append(pl.BlockSpec((1, tq), lambda i, j, k: (0, j)))
        else:
            in_specs.append(pl.BlockSpec((tp, tq), lambda i, j, k: (i, j)))
        args.append(bias)
    aliases = {}
    if into is not None:
        in_specs.append(pl.BlockSpec(memory_space=pl.ANY))
        args.append(into)
        aliases = {len(args) - 1: 0}
        out_shape = jax.ShapeDtypeStruct(into.shape, into.dtype)
        out_dtype = into.dtype
        if o_l is None:
            out_spec = pl.BlockSpec((tp, tq), lambda i, j, k: (i, j + q_off))
        else:
            out_spec = pl.BlockSpec((None, tp, tq), lambda i, j, k: (o_l, i, j + q_off))
    else:
        out_shape = jax.ShapeDtypeStruct((P, Q), out_dtype)
        out_spec = pl.BlockSpec((tp, tq), lambda i, j, k: (i, j))
    n_in = len(args) + len(after)
    in_specs += [pl.BlockSpec(memory_space=pl.ANY)] * len(after)
    args += list(after)
    has_bias = bias is not None
    has_into = into is not None

    def body(*refs):
        a_ref, b_ref = refs[0], refs[1]
        pos = 2
        bias_ref = None
        if has_bias:
            bias_ref = refs[pos]
            pos += 1
        o_ref = refs[n_in]
        acc_ref = refs[n_in + 1] if nk > 1 else None
        part = lax.dot_general(a_ref[...], b_ref[...], dims, preferred_element_type=F32)

        def finish(acc):
            if has_bias:
                acc = acc + bias_ref[...]
            o_ref[...] = acc.astype(out_dtype)

        if nk == 1:
            finish(part)
        else:
            k = pl.program_id(2)

            @pl.when(k == 0)
            def _():
                acc_ref[...] = part

            @pl.when(k > 0)
            def _():
                acc_ref[...] += part

            @pl.when(k == nk - 1)
            def _():
                finish(acc_ref[...])

    est = 2 * (_nbytes((tp, tr), a.dtype) + _nbytes((tr, tq), b.dtype) + _nbytes((tp, tq), out_dtype)) + 3 * tp * tq * 4
    return pl.pallas_call(
        body, name=name, out_shape=out_shape,
        grid=(P // tp, Q // tq, nk),
        in_specs=in_specs, out_specs=out_spec,
        scratch_shapes=[pltpu.VMEM((tp, tq), F32)] if nk > 1 else [],
        input_output_aliases=aliases,
        compiler_params=pltpu.CompilerParams(
            dimension_semantics=("parallel", "parallel", "arbitrary"), vmem_limit_bytes=_vmem(est)),
    )(*args)


def _row_call(body, ins, outs, *, name, rows, tr, acc_outs=(), est=0):
    in_specs = []
    for arr, kind in ins:
        if kind == "row":
            in_specs.append(pl.BlockSpec((tr, arr.shape[1]), lambda i: (i, 0)))
        else:
            nd = arr.ndim
            in_specs.append(pl.BlockSpec(arr.shape, lambda i, nd=nd: (0,) * nd))
    out_shapes = [jax.ShapeDtypeStruct(s, d) for s, d in outs] + [jax.ShapeDtypeStruct(s, d) for s, d in acc_outs]
    out_specs = [pl.BlockSpec((tr, s[1]), lambda i: (i, 0)) for s, _ in outs]
    out_specs += [pl.BlockSpec(s, lambda i, nd=len(s): (0,) * nd) for s, _ in acc_outs]
    res = pl.pallas_call(
        body, name=name, out_shape=out_shapes, grid=(rows // tr,), in_specs=in_specs, out_specs=out_specs,
        compiler_params=pltpu.CompilerParams(dimension_semantics=("arbitrary",), vmem_limit_bytes=_vmem(est)),
    )(*[a for a, _ in ins])
    return res


def _rms_fwd(x, g, *, out_dtype, name):
    T, D = x.shape
    tr = _pick(T, (512, 256, 128))

    def body(x_ref, g_ref, o_ref):
        xv = x_ref[...]
        r = lax.rsqrt(jnp.mean(xv * xv, axis=-1, keepdims=True) + RMS_EPS)
        o_ref[...] = (xv * r * g_ref[...]).astype(out_dtype)

    return _row_call(body, [(x, "row"), (g, "full")], [((T, D), out_dtype)], name=name, rows=T, tr=tr,
                     est=8 * tr * D * 4)[0]


def _rms_res(h, y, g, *, name):
    T, D = h.shape
    tr = _pick(T, (512, 256, 128))

    def body(h_ref, y_ref, g_ref, o_ref):
        yv = y_ref[...]
        r = lax.rsqrt(jnp.mean(yv * yv, axis=-1, keepdims=True) + RMS_EPS)
        o_ref[...] = h_ref[...] + yv * r * g_ref[...]

    return _row_call(body, [(h, "row"), (y, "row"), (g, "full")], [((T, D), F32)], name=name, rows=T, tr=tr,
                     est=10 * tr * D * 4)[0]


def _rms_bwd(x, g, dy, dres, *, out_dtype, name):
    T, D = x.shape
    tr = _pick(T, (512, 256, 128))
    has_res = dres is not None

    def body(*refs):
        if has_res:
            x_ref, g_ref, dy_ref, dr_ref, dx_ref, dg_ref = refs
        else:
            x_ref, g_ref, dy_ref, dx_ref, dg_ref = refs
        xv = x_ref[...]
        r = lax.rsqrt(jnp.mean(xv * xv, axis=-1, keepdims=True) + RMS_EPS)
        xhat = xv * r
        dyv = dy_ref[...].astype(F32)
        dxn = dyv * g_ref[...]
        dx = r * (dxn - xhat * jnp.mean(dxn * xhat, axis=-1, keepdims=True))
        if has_res:
            dx = dx + dr_ref[...]
        dx_ref[...] = dx.astype(out_dtype)
        part = jnp.sum(dyv * xhat, axis=0, keepdims=True)

        @pl.when(pl.program_id(0) == 0)
        def _():
            dg_ref[...] = part

        @pl.when(pl.program_id(0) > 0)
        def _():
            dg_ref[...] += part

    ins = [(x, "row"), (g, "full"), (dy, "row")] + ([(dres, "row")] if has_res else [])
    dx, dg = _row_call(body, ins, [((T, D), out_dtype)], name=name, rows=T, tr=tr, acc_outs=[((1, D), F32)],
                       est=12 * tr * D * 4)
    return dx, dg


def _rms_res_norm(h, y, g_res, g_next, *, name):
    T, D = h.shape
    tr = _pick(T, (512, 256, 128))

    def body(h_ref, y_ref, g_ref, gn_ref, o_ref, n_ref):
        yv = y_ref[...]
        r = lax.rsqrt(jnp.mean(yv * yv, axis=-1, keepdims=True) + RMS_EPS)
        h2 = h_ref[...] + yv * r * g_ref[...]
        o_ref[...] = h2
        r2 = lax.rsqrt(jnp.mean(h2 * h2, axis=-1, keepdims=True) + RMS_EPS)
        n_ref[...] = (h2 * r2 * gn_ref[...]).astype(BF16)

    return _row_call(body, [(h, "row"), (y, "row"), (g_res, "full"), (g_next, "full")],
                     [((T, D), F32), ((T, D), BF16)], name=name, rows=T, tr=tr, est=12 * tr * D * 4)


def _rms_bwd_chain(x1, g1, dy1, dres, x2, g2, *, name):
    T, D = x1.shape
    tr = _pick(T, (512, 256, 128))

    def one(xv, gv, dyv):
        r = lax.rsqrt(jnp.mean(xv * xv, axis=-1, keepdims=True) + RMS_EPS)
        xhat = xv * r
        dxn = dyv * gv
        dx = r * (dxn - xhat * jnp.mean(dxn * xhat, axis=-1, keepdims=True))
        return dx, jnp.sum(dyv * xhat, axis=0, keepdims=True)

    def body(x1_ref, g1_ref, dy1_ref, dr_ref, x2_ref, g2_ref, d1_ref, d2_ref, dg1_ref, dg2_ref):
        dx1, p1 = one(x1_ref[...], g1_ref[...], dy1_ref[...].astype(F32))
        d1 = dx1 + dr_ref[...]
        d1_ref[...] = d1
        dx2, p2 = one(x2_ref[...], g2_ref[...], d1)
        d2_ref[...] = dx2.astype(BF16)

        @pl.when(pl.program_id(0) == 0)
        def _():
            dg1_ref[...] = p1
            dg2_ref[...] = p2

        @pl.when(pl.program_id(0) > 0)
        def _():
            dg1_ref[...] += p1
            dg2_ref[...] += p2

    ins = [(x1, "row"), (g1, "full"), (dy1, "row"), (dres, "row"), (x2, "row"), (g2, "full")]
    return _row_call(body, ins, [((T, D), F32), ((T, D), BF16)], name=name, rows=T, tr=tr,
                     acc_outs=[((1, D), F32), ((1, D), F32)], est=20 * tr * D * 4)


def _ffn_up(fn, w_gu, l, *, name):
    T, D = fn.shape
    H = w_gu.shape[2] // 2
    tp = _pick(T, (1024, 512, 256, 128))
    tq = _pick(H, (1408, 768, 512, 256, 128))
    nj = H // tq

    def body(a_ref, wg_ref, wu_ref, g_ref, u_ref, act_ref):
        a = a_ref[...]
        g = jnp.dot(a, wg_ref[...], preferred_element_type=F32)
        u = jnp.dot(a, wu_ref[...], preferred_element_type=F32)
        sg = jax.nn.sigmoid(g)
        silu = g * sg
        g_ref[...] = (u * (sg + silu * (1.0 - sg))).astype(BF16)
        u_ref[...] = silu.astype(BF16)
        act_ref[...] = (silu * u).astype(BF16)

    tile = pl.BlockSpec((tp, tq), lambda j, i: (i, j))
    est = 2 * (tp * D * 2 + 2 * D * tq * 2 + 3 * tp * tq * 2) + 4 * tp * tq * 4
    return pl.pallas_call(
        body, name=name,
        out_shape=[jax.ShapeDtypeStruct((T, H), BF16), jax.ShapeDtypeStruct((T, H), BF16),
                   jax.ShapeDtypeStruct((T, H), BF16)],
        grid=(nj, T // tp),
        in_specs=[pl.BlockSpec((tp, D), lambda j, i: (i, 0)),
                  pl.BlockSpec((None, D, tq), lambda j, i: (l, 0, j)),
                  pl.BlockSpec((None, D, tq), lambda j, i: (l, 0, j + nj))],
        out_specs=[tile, tile, tile],
        compiler_params=pltpu.CompilerParams(dimension_semantics=("parallel", "parallel"),
                                             vmem_limit_bytes=_vmem(est)),
    )(fn, w_gu, w_gu)


def _ffn_down_dx(df, w_down, l, g, u, after, *, name):
    T, D = df.shape
    H = w_down.shape[1]
    tp = _pick(T, (1024, 512, 256, 128))
    tq = _pick(H, (1408, 768, 512, 256, 128))

    def body(a_ref, w_ref, g_ref, u_ref, _, dg_ref, du_ref):
        da = lax.dot_general(a_ref[...], w_ref[...], (((1,), (1,)), ((), ())), preferred_element_type=F32)
        dg_ref[...] = (da * g_ref[...].astype(F32)).astype(BF16)
        du_ref[...] = (da * u_ref[...].astype(F32)).astype(BF16)

    tile = pl.BlockSpec((tp, tq), lambda j, i: (i, j))
    est = 2 * (tp * D * 2 + tq * D * 2 + 4 * tp * tq * 2) + 3 * tp * tq * 4
    return pl.pallas_call(
        body, name=name,
        out_shape=[jax.ShapeDtypeStruct((T, H), BF16), jax.ShapeDtypeStruct((T, H), BF16)],
        grid=(H // tq, T // tp),
        in_specs=[pl.BlockSpec((tp, D), lambda j, i: (i, 0)),
                  pl.BlockSpec((None, tq, D), lambda j, i: (l, j, 0)), tile, tile,
                  pl.BlockSpec(memory_space=pl.ANY)],
        out_specs=[tile, tile],
        compiler_params=pltpu.CompilerParams(dimension_semantics=("parallel", "parallel"),
                                             vmem_limit_bytes=_vmem(est)),
    )(df, w_down, g, u, after)


def _loss_and_grad(y, target, x, g, *, name):
    T, D = y.shape
    tr = _pick(T, (512, 256, 128))

    def body(y_ref, t_ref, x_ref, g_ref, dy_ref, dx_ref, l_ref, dg_ref):
        e = y_ref[...] - t_ref[...]
        dy = e * (1.0 / D)
        dy_ref[...] = dy
        part = jnp.sum(jnp.sum(e * e, axis=1, keepdims=True), axis=0, keepdims=True) * (0.5 / D)
        xv = x_ref[...]
        r = lax.rsqrt(jnp.mean(xv * xv, axis=-1, keepdims=True) + RMS_EPS)
        xhat = xv * r
        dxn = dy * g_ref[...]
        dx_ref[...] = (r * (dxn - xhat * jnp.mean(dxn * xhat, axis=-1, keepdims=True))).astype(BF16)
        dg = jnp.sum(dy * xhat, axis=0, keepdims=True)

        @pl.when(pl.program_id(0) == 0)
        def _():
            l_ref[...] = part
            dg_ref[...] = dg

        @pl.when(pl.program_id(0) > 0)
        def _():
            l_ref[...] += part
            dg_ref[...] += dg

    dy, dx, l, dg = _row_call(body, [(y, "row"), (target, "row"), (x, "row"), (g, "full")],
                              [((T, D), F32), ((T, D), BF16)], name=name, rows=T, tr=tr,
                              acc_outs=[((1, 1), F32), ((1, D), F32)], est=14 * tr * D * 4)
    return dy, dx, l, dg


_SQRT_HALF = 0.7071067811865476
_INV_SQRT_2PI = 0.3989422804014327


def _gelu_parts(x):
    cdf = 0.5 * (1.0 + lax.erf(x * _SQRT_HALF))
    return cdf


def _sgu_common(pre, lng, lnb, W):
    cdf = _gelu_parts(pre)
    z = pre * cdf
    u = z[:, :W]
    v = z[:, W:]
    mu = jnp.mean(v, axis=-1, keepdims=True)
    vc = v - mu
    var = jnp.mean(vc * vc, axis=-1, keepdims=True)
    rstd = lax.rsqrt(var + LN_EPS)
    vhat = vc * rstd
    vn = vhat * lng + lnb
    return cdf, u, vhat, rstd, vn


def _causal_mask():
    t = lax.broadcasted_iota(jnp.int32, (CHUNK, CHUNK), 0)
    s = lax.broadcasted_iota(jnp.int32, (CHUNK, CHUNK), 1)
    return t >= s


def _sgu_fwd(pre, lng, lnb, ws, bsT, *, name):
    T, W2 = pre.shape
    W = W2 // 2
    G = ws.shape[0]
    gd = W // G

    def body(pre_ref, lng_ref, lnb_ref, ws_ref, bs_ref, o_ref):
        _, u, _, _, vn = _sgu_common(pre_ref[...], lng_ref[...], lnb_ref[...], W)
        vnb = vn.astype(BF16)
        causal = _causal_mask()
        for g in range(G):
            w = jnp.where(causal, ws_ref[g], 0.0).astype(BF16)
            sv = jnp.dot(w, vnb[:, g * gd:(g + 1) * gd], preferred_element_type=F32) + bs_ref[:, g:g + 1]
            o_ref[:, g * gd:(g + 1) * gd] = (u[:, g * gd:(g + 1) * gd] * sv).astype(BF16)

    return pl.pallas_call(
        body, name=name, out_shape=jax.ShapeDtypeStruct((T, W), BF16), grid=(T // CHUNK,),
        in_specs=[pl.BlockSpec((CHUNK, W2), lambda i: (i, 0)),
                  pl.BlockSpec((1, W), lambda i: (0, 0)), pl.BlockSpec((1, W), lambda i: (0, 0)),
                  pl.BlockSpec(ws.shape, lambda i: (0, 0, 0)), pl.BlockSpec(bsT.shape, lambda i: (0, 0))],
        out_specs=pl.BlockSpec((CHUNK, W), lambda i: (i, 0)),
        compiler_params=pltpu.CompilerParams(dimension_semantics=("arbitrary",),
                                             vmem_limit_bytes=_vmem(12 * CHUNK * W2 * 4)),
    )(pre, lng, lnb, ws, bsT)


def _sgu_bwd(pre, dgated, lng, lnb, ws, bsT, *, name):
    T, W2 = pre.shape
    W = W2 // 2
    G = ws.shape[0]
    gd = W // G

    def body(pre_ref, dgt_ref, lng_ref, lnb_ref, ws_ref, bs_ref,
             dpre_ref, dws_ref, dbs_ref, dlng_ref, dlnb_ref, dbin_ref):
        first = pl.program_id(0) == 0

        @pl.when(first)
        def _():
            dws_ref[...] = jnp.zeros_like(dws_ref)
            dbs_ref[...] = jnp.zeros_like(dbs_ref)
            dlng_ref[...] = jnp.zeros_like(dlng_ref)
            dlnb_ref[...] = jnp.zeros_like(dlnb_ref)
            dbin_ref[...] = jnp.zeros_like(dbin_ref)

        pre_v = pre_ref[...]
        lng_v = lng_ref[...]
        cdf, u, vhat, rstd, vn = _sgu_common(pre_v, lng_v, lnb_ref[...], W)
        vnb = vn.astype(BF16)
        dgt = dgt_ref[...].astype(F32)
        causal = _causal_mask()
        du_parts, dvn_parts = [], []
        for g in range(G):
            sl = slice(g * gd, (g + 1) * gd)
            w = jnp.where(causal, ws_ref[g], 0.0).astype(BF16)
            sv = jnp.dot(w, vnb[:, sl], preferred_element_type=F32) + bs_ref[:, g:g + 1]
            dgt_g = dgt[:, sl]
            du_parts.append(dgt_g * sv)
            dsv = dgt_g * u[:, sl]
            dsvb = dsv.astype(BF16)
            dvn_parts.append(lax.dot_general(w, dsvb, (((0,), (0,)), ((), ())), preferred_element_type=F32))
            dw = lax.dot_general(dsvb, vnb[:, sl], (((1,), (1,)), ((), ())), preferred_element_type=F32)
            dws_ref[g] += jnp.where(causal, dw, 0.0)
            dbs_ref[:, g:g + 1] += jnp.sum(dsv, axis=1, keepdims=True)
        du = jnp.concatenate(du_parts, axis=1)
        dvn = jnp.concatenate(dvn_parts, axis=1)
        dlng_ref[...] += jnp.sum(dvn * vhat, axis=0, keepdims=True)
        dlnb_ref[...] += jnp.sum(dvn, axis=0, keepdims=True)
        dvh = dvn * lng_v
        dv = rstd * (dvh - jnp.mean(dvh, axis=-1, keepdims=True)
                     - vhat * jnp.mean(dvh * vhat, axis=-1, keepdims=True))
        dz = jnp.concatenate([du, dv], axis=1)
        dgelu = cdf + pre_v * jnp.exp(-0.5 * pre_v * pre_v) * _INV_SQRT_2PI
        dpre = dz * dgelu
        dbin_ref[...] += jnp.sum(dpre, axis=0, keepdims=True)
        dpre_ref[...] = dpre.astype(BF16)

    full = lambda shape: pl.BlockSpec(shape, lambda i, nd=len(shape): (0,) * nd)
    return pl.pallas_call(
        body, name=name,
        out_shape=[jax.ShapeDtypeStruct((T, W2), BF16), jax.ShapeDtypeStruct(ws.shape, F32),
                   jax.ShapeDtypeStruct(bsT.shape, F32), jax.ShapeDtypeStruct((1, W), F32),
                   jax.ShapeDtypeStruct((1, W), F32), jax.ShapeDtypeStruct((1, W2), F32)],
        grid=(T // CHUNK,),
        in_specs=[pl.BlockSpec((CHUNK, W2), lambda i: (i, 0)), pl.BlockSpec((CHUNK, W), lambda i: (i, 0)),
                  full((1, W)), full((1, W)), full(ws.shape), full(bsT.shape)],
        out_specs=[pl.BlockSpec((CHUNK, W2), lambda i: (i, 0)), full(ws.shape), full(bsT.shape),
                   full((1, W)), full((1, W)), full((1, W2))],
        compiler_params=pltpu.CompilerParams(dimension_semantics=("arbitrary",),
                                             vmem_limit_bytes=_vmem(24 * CHUNK * W2 * 4)),
    )(pre, dgated, lng, lnb, ws, bsT)


def _rope_tables(positions):
    half = ROPE_DIM // 2
    inv_freq = ROPE_THETA ** (-jnp.arange(0, ROPE_DIM, 2, dtype=F32) / ROPE_DIM)
    ang = positions.astype(F32).reshape(-1, 1) * inv_freq
    cos, sin = jnp.cos(ang), jnp.sin(ang)
    T = ang.shape[0]
    rest = HEAD_DIM - ROPE_DIM
    c64 = jnp.concatenate([cos, cos, jnp.ones((T, rest), F32)], axis=1)
    s64 = jnp.concatenate([-sin, sin, jnp.zeros((T, rest), F32)], axis=1)
    del half
    return jnp.tile(c64, (1, LANES // HEAD_DIM)), jnp.tile(s64, (1, LANES // HEAD_DIM))


def _swap8(x):
    W = x.shape[1]
    half = ROPE_DIM // 2
    lane = lax.broadcasted_iota(jnp.int32, x.shape, 1) % HEAD_DIM
    return jnp.where(lane < half, pltpu.roll(x, W - half, axis=1),
                     jnp.where(lane < ROPE_DIM, pltpu.roll(x, half, axis=1), 0.0))


def _wide(tab, W):
    return jnp.concatenate([tab] * (W // LANES), axis=1) if W > LANES else tab


def _rope_fwd(qkv, ctab, stab, *, q_width, kv_width, name):
    T = qkv.shape[0]
    tr = _pick(T, (256, 128))
    scale = HEAD_DIM ** -0.5

    def body(x_ref, c_ref, s_ref, q_ref, k_ref, v_ref):
        c = c_ref[...]
        s = s_ref[...]
        q = x_ref[:, :q_width]
        k = x_ref[:, q_width:q_width + kv_width]
        q_ref[...] = ((q * _wide(c, q_width) + _swap8(q) * _wide(s, q_width)) * scale).astype(BF16)
        k_ref[...] = (k * _wide(c, kv_width) + _swap8(k) * _wide(s, kv_width)).astype(BF16)
        v_ref[...] = x_ref[:, q_width + kv_width:].astype(BF16)

    return _row_call(body, [(qkv, "row"), (ctab, "row"), (stab, "row")],
                     [((T, q_width), BF16), ((T, kv_width), BF16), ((T, kv_width), BF16)],
                     name=name, rows=T, tr=tr, est=10 * tr * qkv.shape[1] * 4)


_NT = (((1,), (1,)), ((), ()))
_TN = (((0,), (0,)), ((), ()))


def _group_rows(ref, heads):
    return jnp.concatenate([ref[:, h * HEAD_DIM:(h + 1) * HEAD_DIM] for h in heads], axis=0)


def _attn_valid(grp):
    qi = np.arange(grp * CHUNK)[:, None] % CHUNK
    sj = np.arange(2 * CHUNK)[None, :]
    cur = (sj >= CHUNK) & (sj - CHUNK <= qi)
    prev = (sj < CHUNK) & (sj > qi)
    return jnp.asarray(np.stack([cur, cur | prev]).astype(np.float32))


def _valid_spec(grp):
    return pl.BlockSpec((None, grp * CHUNK, 2 * CHUNK), lambda n: (jnp.minimum(n, 1), 0, 0))


def _attn_group_probs(q, kk, sinks, valid, grp):
    rows = grp * CHUNK
    s = lax.dot_general(q, kk, _NT, preferred_element_type=F32)
    s = jnp.where(valid, s, NEG_INF)
    r = lax.broadcasted_iota(jnp.int32, (rows, 1), 0)
    sink = jnp.full((rows, 1), sinks[grp - 1], F32)
    for g in range(grp - 2, -1, -1):
        sink = jnp.where(r < (g + 1) * CHUNK, sinks[g], sink)
    m = jnp.maximum(jnp.max(s, axis=1, keepdims=True), sink)
    p = jnp.exp(s - m)
    ps = jnp.exp(sink - m)
    inv = 1.0 / (jnp.sum(p, axis=1, keepdims=True) + ps)
    return p * inv, ps * inv


def _kv_specs(width, nb):
    prev = pl.BlockSpec((CHUNK, width), lambda n: (jnp.maximum(n - 1, 0), 0))
    cur = pl.BlockSpec((CHUNK, width), lambda n: (n, 0))
    return prev, cur


def _attn_fwd(qr, kr, vr, sinks, *, name):
    T, QW = qr.shape
    KW = kr.shape[1]
    HQ, HK = QW // HEAD_DIM, KW // HEAD_DIM
    grp = HQ // HK
    nb = T // CHUNK

    def body(q_ref, kp_ref, kc_ref, vp_ref, vc_ref, s_ref, ok_ref, o_ref):
        valid = ok_ref[...] > 0.5
        for kh in range(HK):
            ks = slice(kh * HEAD_DIM, (kh + 1) * HEAD_DIM)
            heads = list(range(kh * grp, (kh + 1) * grp))
            q = _group_rows(q_ref, heads)
            kk = jnp.concatenate([kp_ref[:, ks], kc_ref[:, ks]], axis=0)
            vv = jnp.concatenate([vp_ref[:, ks], vc_ref[:, ks]], axis=0)
            p, _ = _attn_group_probs(q, kk, [s_ref[0, h] for h in heads], valid, grp)
            o = jnp.dot(p.astype(BF16), vv, preferred_element_type=F32).astype(BF16)
            for g, h in enumerate(heads):
                o_ref[:, h * HEAD_DIM:(h + 1) * HEAD_DIM] = o[g * CHUNK:(g + 1) * CHUNK]

    kp, kc = _kv_specs(KW, nb)
    return pl.pallas_call(
        body, name=name, out_shape=jax.ShapeDtypeStruct((T, QW), BF16), grid=(nb,),
        in_specs=[pl.BlockSpec((CHUNK, QW), lambda n: (n, 0)), kp, kc, kp, kc,
                  pl.BlockSpec(memory_space=pltpu.SMEM), _valid_spec(grp)],
        out_specs=pl.BlockSpec((CHUNK, QW), lambda n: (n, 0)),
        compiler_params=pltpu.CompilerParams(dimension_semantics=("arbitrary",), vmem_limit_bytes=_vmem(8 << 20)),
    )(qr, kr, kr, vr, vr, sinks, _attn_valid(grp))


def _attn_bwd(qr, kr, vr, sinks, do, *, name):
    T, QW = qr.shape
    KW = kr.shape[1]
    HQ, HK = QW // HEAD_DIM, KW // HEAD_DIM
    grp = HQ // HK
    nb = T // CHUNK

    def body(q_ref, kp_ref, kc_ref, vp_ref, vc_ref, s_ref, do_ref, ok_ref,
             dq_ref, dkp_ref, dkc_ref, dvp_ref, dvc_ref, ds_ref):
        n = pl.program_id(0)
        valid = ok_ref[...] > 0.5
        lane = lax.broadcasted_iota(jnp.int32, (1, LANES), 1)
        dsink = jnp.zeros((1, LANES), F32)
        for kh in range(HK):
            ks = slice(kh * HEAD_DIM, (kh + 1) * HEAD_DIM)
            heads = list(range(kh * grp, (kh + 1) * grp))
            q = _group_rows(q_ref, heads)
            doh = _group_rows(do_ref, heads)
            kk = jnp.concatenate([kp_ref[:, ks], kc_ref[:, ks]], axis=0)
            vv = jnp.concatenate([vp_ref[:, ks], vc_ref[:, ks]], axis=0)
            p, ps = _attn_group_probs(q, kk, [s_ref[0, h] for h in heads], valid, grp)
            dp = lax.dot_general(doh, vv, _NT, preferred_element_type=F32)
            delta = jnp.sum(p * dp, axis=1, keepdims=True)
            ds = (p * (dp - delta)).astype(BF16)
            dv = lax.dot_general(p.astype(BF16), doh, _TN, preferred_element_type=F32)
            dk = lax.dot_general(ds, q, _TN, preferred_element_type=F32)
            dq = jnp.dot(ds, kk, preferred_element_type=F32)
            psd = ps * delta
            for g, h in enumerate(heads):
                dq_ref[:, h * HEAD_DIM:(h + 1) * HEAD_DIM] = dq[g * CHUNK:(g + 1) * CHUNK]
                dsink = dsink + jnp.where(
                    lane == h, -jnp.sum(psd[g * CHUNK:(g + 1) * CHUNK], axis=0, keepdims=True), 0.0)
            dkp_ref[:, ks] = dk[:CHUNK]
            dkc_ref[:, ks] = dk[CHUNK:]
            dvp_ref[:, ks] = dv[:CHUNK]
            dvc_ref[:, ks] = dv[CHUNK:]

        @pl.when(n == 0)
        def _():
            ds_ref[...] = dsink

        @pl.when(n > 0)
        def _():
            ds_ref[...] += dsink

    kp, kc = _kv_specs(KW, nb)
    qspec = pl.BlockSpec((CHUNK, QW), lambda n: (n, 0))
    kout = pl.BlockSpec((CHUNK, KW), lambda n: (n, 0))
    return pl.pallas_call(
        body, name=name,
        out_shape=[jax.ShapeDtypeStruct((T, QW), F32)] + [jax.ShapeDtypeStruct((T, KW), F32)] * 4
        + [jax.ShapeDtypeStruct((1, LANES), F32)],
        grid=(nb,),
        in_specs=[qspec, kp, kc, kp, kc, pl.BlockSpec(memory_space=pltpu.SMEM), qspec, _valid_spec(grp)],
        out_specs=[qspec, kout, kout, kout, kout, pl.BlockSpec((1, LANES), lambda n: (0, 0))],
        compiler_params=pltpu.CompilerParams(dimension_semantics=("arbitrary",), vmem_limit_bytes=_vmem(12 << 20)),
    )(qr, kr, kr, vr, vr, sinks, do, _attn_valid(grp))


def _rope_bwd(dq, dkp, dkc, dvp, dvc, ctab, stab, *, name):
    T, QW = dq.shape
    KW = dkp.shape[1]
    nb = T // CHUNK
    scale = HEAD_DIM ** -0.5
    width = QW + 2 * KW

    def body(dq_ref, dkc_ref, dkn_ref, dvc_ref, dvn_ref, c_ref, s_ref, o_ref, db_ref):
        n = pl.program_id(0)
        c = c_ref[...]
        s = s_ref[...]
        has_next = (n < nb - 1).astype(F32)
        dqv = dq_ref[...]
        dk = dkc_ref[...] + has_next * dkn_ref[...]
        dv = dvc_ref[...] + has_next * dvn_ref[...]
        dq_pre = (dqv * _wide(c, QW) + _swap8(dqv * _wide(s, QW))) * scale
        dk_pre = dk * _wide(c, KW) + _swap8(dk * _wide(s, KW))
        o_ref[:, :QW] = dq_pre.astype(BF16)
        o_ref[:, QW:QW + KW] = dk_pre.astype(BF16)
        o_ref[:, QW + KW:] = dv.astype(BF16)
        part = jnp.concatenate([jnp.sum(dq_pre, axis=0, keepdims=True), jnp.sum(dk_pre, axis=0, keepdims=True),
                                jnp.sum(dv, axis=0, keepdims=True)], axis=1)

        @pl.when(n == 0)
        def _():
            db_ref[...] = part

        @pl.when(n > 0)
        def _():
            db_ref[...] += part

    cur = lambda w: pl.BlockSpec((CHUNK, w), lambda n: (n, 0))
    nxt = lambda w: pl.BlockSpec((CHUNK, w), lambda n: (jnp.minimum(n + 1, nb - 1), 0))
    return pl.pallas_call(
        body, name=name,
        out_shape=[jax.ShapeDtypeStruct((T, width), BF16), jax.ShapeDtypeStruct((1, width), F32)],
        grid=(nb,),
        in_specs=[cur(QW), cur(KW), nxt(KW), cur(KW), nxt(KW), cur(LANES), cur(LANES)],
        out_specs=[cur(width), pl.BlockSpec((1, width), lambda n: (0, 0))],
        compiler_params=pltpu.CompilerParams(dimension_semantics=("arbitrary",), vmem_limit_bytes=_vmem(8 << 20)),
    )(dq, dkc, dkp, dvc, dvp, ctab, stab)


def _cast_block(w, l, axis, chip_arr, *, name):
    _, Ks, Ns = w.shape
    tk = _pick(Ks, (512, 352, 256, 128))
    nk = Ks // tk
    full = (Ks * N_CHIPS, Ns) if axis == 0 else (Ks, Ns * N_CHIPS)

    def body(p_ref, w_ref, o_ref):
        o_ref[...] = w_ref[...].astype(BF16)

    if axis == 0:
        out_spec = pl.BlockSpec((tk, Ns), lambda i, p: (p[0] * nk + i, 0))
    else:
        out_spec = pl.BlockSpec((tk, Ns), lambda i, p: (i, p[0]))
    grid_spec = pltpu.PrefetchScalarGridSpec(
        num_scalar_prefetch=1, grid=(nk,),
        in_specs=[pl.BlockSpec((None, tk, Ns), lambda i, p: (l, i, 0))], out_specs=out_spec)
    return pl.pallas_call(
        body, name=name, out_shape=jax.ShapeDtypeStruct(full, BF16), grid_spec=grid_spec,
        compiler_params=pltpu.CompilerParams(dimension_semantics=("arbitrary",),
                                             vmem_limit_bytes=_vmem(4 * tk * Ns * 6)),
    )(chip_arr, w)


def _adamw_math(w, g, m, v):
    m = ADAM_B1 * m + (1.0 - ADAM_B1) * g
    v = ADAM_B2 * v + (1.0 - ADAM_B2) * (g * g)
    m_hat = m / (1.0 - ADAM_B1 ** ADAM_STEP)
    v_hat = v / (1.0 - ADAM_B2 ** ADAM_STEP)
    delta = -ADAM_LR * (m_hat / (jnp.sqrt(v_hat) + ADAM_EPS) + ADAM_WD * w)
    return delta, m, v


def _adamw_layer(w, m, v, g, l, outs, *, name):
    _, K, N = w.shape
    tk = _pick(K, (256, 176, 128))

    def body(w_ref, m_ref, v_ref, g_ref, _g, _d, _m, _v, go_ref, d_ref, mo_ref, vo_ref):
        gv = g_ref[...]
        d, mn, vn = _adamw_math(w_ref[...], gv, m_ref[...], v_ref[...])
        go_ref[...] = gv
        d_ref[...] = d
        mo_ref[...] = mn
        vo_ref[...] = vn

    layer = pl.BlockSpec((None, tk, N), lambda i: (l, i, 0))
    any_spec = pl.BlockSpec(memory_space=pl.ANY)
    sd = jax.ShapeDtypeStruct(w.shape, F32)
    return pl.pallas_call(
        body, name=name, out_shape=[sd, sd, sd, sd], grid=(K // tk,),
        in_specs=[layer, layer, layer, pl.BlockSpec((tk, N), lambda i: (i, 0))] + [any_spec] * 4,
        out_specs=[layer] * 4, input_output_aliases={4: 0, 5: 1, 6: 2, 7: 3},
        compiler_params=pltpu.CompilerParams(dimension_semantics=("arbitrary",),
                                             vmem_limit_bytes=_vmem(2 * 8 * tk * N * 4 + 6 * tk * N * 4)),
    )(w, m, v, g, *outs)


def _adamw_small(w, g, m, v, *, name):
    def body(w_ref, g_ref, m_ref, v_ref, d_ref, mo_ref, vo_ref):
        d, mn, vn = _adamw_math(w_ref[...], g_ref[...], m_ref[...], v_ref[...])
        d_ref[...] = d
        mo_ref[...] = mn
        vo_ref[...] = vn

    sd = jax.ShapeDtypeStruct(w.shape, F32)
    return pl.pallas_call(body, name=name, out_shape=[sd, sd, sd])(w, g, m, v)


def _my_place():
    return lax.axis_index("x"), lax.axis_index("y"), lax.axis_index("c")


def _peer_chips(x, y):
    return [(1 - x, y), (x, 1 - y), (1 - x, 1 - y)]


_HBM = pl.BlockSpec(memory_space=pltpu.HBM)
_SEM = pl.BlockSpec(memory_space=pltpu.SEMAPHORE)
_EFFECT = pltpu.SideEffectType.DATAFLOW_SIDE_EFFECTING


def _split_start(name, bufs, n_copies, make_copies, after):
    nb = len(bufs)

    def body(*refs):
        send_sems, recv_sems = refs[nb + 1], refs[nb + 2]
        token = refs[2 * nb + 3]
        sends, _ = make_copies(refs[:nb], send_sems, recv_sems)
        for cp in sends:
            cp.start()
        token[...] = jnp.zeros_like(token)

    res = pl.pallas_call(
        body, name=name,
        out_shape=(pltpu.SemaphoreType.DMA((n_copies,)), pltpu.SemaphoreType.DMA((n_copies,)),
                   *[pltpu.HBM(b.shape, b.dtype) for b in bufs], jax.ShapeDtypeStruct((8, LANES), F32)),
        in_specs=[_HBM] * nb + [pl.BlockSpec(memory_space=pl.ANY)],
        out_specs=(_SEM, _SEM, *[_HBM] * nb, pl.BlockSpec(memory_space=pltpu.VMEM)),
        input_output_aliases={k: 2 + k for k in range(nb)},
        compiler_params=pltpu.CompilerParams(has_side_effects=_EFFECT),
    )(*[pltpu.with_memory_space_constraint(b, pltpu.HBM) for b in bufs],
      after[0] if isinstance(after, (list, tuple)) else after)
    return res[0], res[1], list(res[2:2 + nb]), res[2 + nb]


def _split_wait(name, bufs, sems, make_copies, after):
    nb = len(bufs)
    after = list(after) if isinstance(after, (list, tuple)) else [after]

    def body(*refs):
        send_sems, recv_sems = refs[nb], refs[nb + 1]
        sends, recvs = make_copies(refs[:nb], send_sems, recv_sems)
        for cp in sends:
            cp.wait_send()
        for cp in recvs:
            cp.wait_recv()

    res = pl.pallas_call(
        body, name=name,
        out_shape=tuple(pltpu.HBM(b.shape, b.dtype) for b in bufs),
        in_specs=[_HBM] * nb + [_SEM, _SEM] + [pl.BlockSpec(memory_space=pl.ANY)] * len(after),
        out_specs=tuple([_HBM] * nb),
        input_output_aliases={k: k for k in range(nb)},
        compiler_params=pltpu.CompilerParams(has_side_effects=_EFFECT),
    )(*bufs, sems[0], sems[1], *after)
    return list(res)


def _remote(src, dst, send_sems, recv_sems, k, target):
    return pltpu.make_async_remote_copy(src_ref=src, dst_ref=dst, send_sem=send_sems.at[k],
                                        recv_sem=recv_sems.at[k], device_id=target, device_id_type=MESH)


def _ag_region(ref, axis, chip, half):
    K, N = ref.shape
    if axis == 0:
        hs = K // N_CHIPS // 2
        assert hs % 16 == 0
        return ref.at[pl.ds(pl.multiple_of((2 * chip + half) * hs, 16), hs), :]
    ns, hk = N // N_CHIPS, K // 2
    assert ns % LANES == 0 and hk % 16 == 0
    return ref.at[pl.ds(pl.multiple_of(half * hk, 16), hk), pl.ds(pl.multiple_of(chip * ns, LANES), ns)]


def _ag_copies(stage, axes):
    n = len(axes)

    def make(bufs, send_sems, recv_sems):
        x, y, c = _my_place()
        me = 2 * x + y
        sends, recvs = [], []
        for j, (px, py) in enumerate(_peer_chips(x, y)):
            other = 2 * px + py
            for w in range(n):
                k = j * n + w
                if stage == 1:
                    src, target = _ag_region(bufs[w], axes[w], me, c), (px, py, c)
                    land = _ag_region(bufs[w], axes[w], other, c)
                else:
                    src, target = _ag_region(bufs[w], axes[w], other, c), (x, y, 1 - c)
                    land = _ag_region(bufs[w], axes[w], other, 1 - c)
                sends.append(_remote(src, src, send_sems, recv_sems, k, target))
                recvs.append(_remote(land, land, send_sems, recv_sems, k, target))
        return sends, recvs

    return make


def _half_shape(shape, axis):
    K, N = shape
    return (K, N // 2) if axis == 0 else (K // 2, N)


def _core_half(ref, axis, half):
    K, N = ref.shape
    if axis == 0:
        return ref.at[:, pl.ds(pl.multiple_of(half * (N // 2), LANES), N // 2)]
    return ref.at[pl.ds(pl.multiple_of(half * (K // 2), 16), K // 2), :]


def _chip_block(ref, axis, chip):
    K, N = ref.shape
    if axis == 0:
        return ref.at[pl.ds(pl.multiple_of(chip * (K // N_CHIPS), 16), K // N_CHIPS), :]
    return ref.at[:, pl.ds(pl.multiple_of(chip * (N // N_CHIPS), LANES), N // N_CHIPS)]


def _rs_sibling_copies(axes):
    n = len(axes)

    def make(bufs, send_sems, recv_sems):
        x, y, c = _my_place()
        sends = [_remote(_core_half(bufs[w], axes[w], 1 - c), bufs[n + w], send_sems, recv_sems, w, (x, y, 1 - c))
                 for w in range(n)]
        recvs = [_remote(bufs[n + w], bufs[n + w], send_sems, recv_sems, w, (x, y, 1 - c)) for w in range(n)]
        return sends, recvs

    return make


def _rs_chip_copies(axes):
    n = len(axes)

    def make(bufs, send_sems, recv_sems):
        x, y, c = _my_place()
        sends, recvs = [], []
        for j, (px, py) in enumerate(_peer_chips(x, y)):
            for w in range(n):
                k = j * n + w
                sends.append(_remote(_chip_block(bufs[w], axes[w], 2 * px + py), bufs[n + w].at[j],
                                     send_sems, recv_sems, k, (px, py, c)))
                recvs.append(_remote(bufs[n + w].at[j], bufs[n + w].at[j], send_sems, recv_sems, k, (px, py, c)))
        return sends, recvs

    return make


def _rs_fill_copies(axes):
    n = len(axes)

    def make(bufs, send_sems, recv_sems):
        x, y, c = _my_place()
        sends = [_remote(_core_half(bufs[w], axes[w], c), _core_half(bufs[w], axes[w], c),
                         send_sems, recv_sems, w, (x, y, 1 - c)) for w in range(n)]
        recvs = [_remote(_core_half(bufs[w], axes[w], 1 - c), _core_half(bufs[w], axes[w], 1 - c),
                         send_sems, recv_sems, w, (x, y, 1 - c)) for w in range(n)]
        return sends, recvs

    return make


def _chip_sum(g, r, axis, place, *, name):
    hk, hn = r.shape
    bk, bn = (hk // N_CHIPS, hn) if axis == 0 else (hk, hn // N_CHIPS)
    tk = _pick(bk, (512, 352, 256, 128))
    nk = bk // tk

    def body(p_ref, g_ref, r_ref, b_ref, own_ref):
        s = g_ref[...].astype(F32) + r_ref[...].astype(F32)
        b_ref[...] = s.astype(BF16)

        @pl.when(pl.program_id(1) == p_ref[0])
        def _():
            own_ref[...] = s

    if axis == 0:
        g_spec = pl.BlockSpec((tk, bn), lambda i, j, p: (j * nk + i, p[1]))
        r_spec = pl.BlockSpec((tk, bn), lambda i, j, p: (j * nk + i, 0))
    else:
        g_spec = pl.BlockSpec((tk, bn), lambda i, j, p: (p[1] * nk + i, j))
        r_spec = pl.BlockSpec((tk, bn), lambda i, j, p: (i, j))
    grid_spec = pltpu.PrefetchScalarGridSpec(
        num_scalar_prefetch=1, grid=(nk, N_CHIPS), in_specs=[g_spec, r_spec],
        out_specs=[r_spec, pl.BlockSpec((tk, bn), lambda i, j, p: (i, 0))])
    return pl.pallas_call(
        body, name=name,
        out_shape=[jax.ShapeDtypeStruct(r.shape, BF16), jax.ShapeDtypeStruct((bk, bn), F32)],
        grid_spec=grid_spec,
        compiler_params=pltpu.CompilerParams(dimension_semantics=("arbitrary", "arbitrary"),
                                             vmem_limit_bytes=_vmem(2 * tk * bn * 10 + 3 * tk * bn * 4)),
    )(place, g, r)


def _final_sum(own, recv, axis, place, *, name):
    _, bk, bn = recv.shape
    tk = _pick(bk, (256, 176, 128))
    nk = bk // tk

    def body(p_ref, o_ref, r_ref, out_ref):
        out_ref[...] = ((o_ref[...] + r_ref[0].astype(F32)) + r_ref[1].astype(F32)) + r_ref[2].astype(F32)

    own_spec = pl.BlockSpec((tk, bn), lambda i, p: (i, 0))
    if axis == 0:
        out_shape, out_spec = (bk, 2 * bn), pl.BlockSpec((tk, bn), lambda i, p: (i, p[1]))
    else:
        out_shape, out_spec = (2 * bk, bn), pl.BlockSpec((tk, bn), lambda i, p: (p[1] * nk + i, 0))
    grid_spec = pltpu.PrefetchScalarGridSpec(
        num_scalar_prefetch=1, grid=(nk,),
        in_specs=[own_spec, pl.BlockSpec((3, tk, bn), lambda i, p: (0, i, 0))], out_specs=out_spec)
    return pl.pallas_call(
        body, name=name, out_shape=jax.ShapeDtypeStruct(out_shape, F32), grid_spec=grid_spec,
        compiler_params=pltpu.CompilerParams(dimension_semantics=("arbitrary",),
                                             vmem_limit_bytes=_vmem(2 * tk * bn * 14 + 4 * tk * bn * 4)),
    )(place, own, recv)


def _allreduce_small(p, after=()):
    n_after = len(after)

    def body(*refs):
        p_ref = refs[0]
        o_ref, r0, r1, r2, send_sems, recv_sems = refs[1 + n_after:]
        x, y, c = _my_place()
        o_ref[...] = p_ref[...]
        for s, (peer, rbuf) in enumerate([((x, y, 1 - c), r0), ((1 - x, y, c), r1), ((x, 1 - y, c), r2)]):
            cp = pltpu.make_async_remote_copy(src_ref=o_ref, dst_ref=rbuf, send_sem=send_sems.at[s],
                                              recv_sem=recv_sems.at[s], device_id=peer, device_id_type=MESH)
            cp.start()
            cp.wait()
            o_ref[...] = o_ref[...] + rbuf[...]

    vm = pl.BlockSpec(memory_space=pltpu.VMEM)
    return pl.pallas_call(
        body, name="allreduce_small", out_shape=jax.ShapeDtypeStruct(p.shape, F32),
        in_specs=[vm] + [pl.BlockSpec(memory_space=pl.ANY)] * n_after, out_specs=vm,
        scratch_shapes=[pltpu.VMEM(p.shape, F32)] * 3 + [pltpu.SemaphoreType.DMA((3,))] * 2,
        compiler_params=pltpu.CompilerParams(vmem_limit_bytes=_vmem(6 * _nbytes(p.shape, F32))),
    )(p, *after)


def _pack_rows(parts):
    rows, metas = [], []
    for a in parts:
        flat = a.reshape(-1)
        nrow = -(-flat.shape[0] // LANES)
        nrow = -(-nrow // 8) * 8
        flat = jnp.pad(flat, (0, nrow * LANES - flat.shape[0]))
        rows.append(flat.reshape(nrow, LANES))
        metas.append((a.shape, nrow))
    return jnp.concatenate(rows, axis=0), metas


def _unpack_rows(packed, metas):
    out, r0 = [], 0
    for shape, nrow in metas:
        size = int(np.prod(shape))
        out.append(packed[r0:r0 + nrow].reshape(-1)[:size].reshape(shape))
        r0 += nrow
    return out


def kernel(x, positions, pre_mix_g, post_mix_g, pre_ffn_g, post_ffn_g, a_w_in, a_b_in, a_ln_g, a_ln_b, a_w_s, a_b_s, a_w_out, b_w_qkv, b_b_qkv, b_sinks, b_w_o, ffn_w_gu, ffn_w_down, loss_target, m_pre_mix_g, m_post_mix_g, m_pre_ffn_g, m_post_ffn_g, m_a_w_in, m_a_b_in, m_a_ln_g, m_a_ln_b, m_a_w_s, m_a_b_s, m_a_w_out, m_b_w_qkv, m_b_b_qkv, m_b_sinks, m_b_w_o, m_ffn_w_gu, m_ffn_w_down, v_pre_mix_g, v_post_mix_g, v_pre_ffn_g, v_post_ffn_g, v_a_w_in, v_a_b_in, v_a_ln_g, v_a_ln_b, v_a_w_s, v_a_b_s, v_a_w_out, v_b_w_qkv, v_b_b_qkv, v_b_sinks, v_b_w_o, v_ffn_w_gu, v_ffn_w_down):
    depth, D = pre_mix_g.shape
    xi, yi, ci = _my_place()
    chip = 2 * xi + yi
    place = jnp.stack([chip, ci]).astype(jnp.int32)

    stacked = {"a_w_in": (a_w_in, m_a_w_in, v_a_w_in), "a_w_out": (a_w_out, m_a_w_out, v_a_w_out),
               "b_w_qkv": (b_w_qkv, m_b_w_qkv, v_b_w_qkv), "b_w_o": (b_w_o, m_b_w_o, v_b_w_o),
               "ffn_w_gu": (ffn_w_gu, m_ffn_w_gu, v_ffn_w_gu), "ffn_w_down": (ffn_w_down, m_ffn_w_down, v_ffn_w_down)}
    cut = {"a_w_in": 1, "a_w_out": 0, "b_w_qkv": 1, "b_w_o": 0, "ffn_w_gu": 1, "ffn_w_down": 0}

    def layer_keys(i):
        mix = [("a_w_in", i // 2), ("a_w_out", i // 2)] if i % 2 == 0 else [("b_w_qkv", i // 2), ("b_w_o", i // 2)]
        return mix + [("ffn_w_gu", i), ("ffn_w_down", i)]

    def dep(a, toks):
        for t in toks:
            a = a + t[:1, :1]
        return a

    W = {}
    for i in range(depth):
        for nm, l in layer_keys(i):
            W[(nm, l)] = _cast_block(stacked[nm][0], l, cut[nm], place, name=f"cast_{nm}_{l}")

    def gather(tag, keys, after):
        axes = [cut[nm] for nm, _ in keys]
        for stage in (1, 2):
            ss, rs, bufs, tok = _split_start(f"ag{stage}_start_{tag}", [W[k] for k in keys], 3 * len(keys),
                                             _ag_copies(stage, axes), after)
            after = yield tok
            bufs = _split_wait(f"ag{stage}_wait_{tag}", bufs, (ss, rs), _ag_copies(stage, axes), after)
            W.update(zip(keys, bufs))
        yield None

    nq = b_b_qkv.shape[1]
    bq_full = jnp.zeros((b_b_qkv.shape[0], N_CHIPS * nq), F32)
    bq_full = lax.dynamic_update_slice(bq_full, jnp.where(ci == 0, b_b_qkv, 0.0), (0, chip * nq))
    bq_packed, bq_meta = _pack_rows([bq_full])
    bq_gathered = _allreduce_small(bq_packed)
    b_qkv_full = _unpack_rows(bq_gathered, bq_meta)[0]

    first = gather("0m", layer_keys(0)[:2], bq_gathered)
    tok = next(first)
    tok = first.send([tok] + [W[k] for i in range(depth) for k in layer_keys(i)[2 if i == 0 else 0:]])
    first.send(tok)

    h = x[0]
    target = loss_target[0]
    ctab, stab = _rope_tables(positions[0])
    q_width = W[("b_w_o", 0)].shape[0]
    kv_width = N_KV_HEADS * HEAD_DIM
    row = lambda a, i: a[i:i + 1]

    saved = []
    hn = None
    for i in range(depth):
        j = i // 2
        s = {"h": h}
        ffn_w = None
        if i == 0:
            ffn_w = gather("0f", layer_keys(0)[2:], W[("a_w_out", 0)])
            toks = [next(ffn_w)]
            nxt = gather("1", layer_keys(1), toks[0])
            toks.append(next(nxt))
            hn = _rms_fwd(h, dep(row(pre_mix_g, i), toks), out_dtype=BF16, name=f"rms_pre_mix_{i}")
        elif i + 1 < depth:
            nxt = gather(str(i + 1), layer_keys(i + 1), h)
            toks = [next(nxt)]
        else:
            toks = []
        s["hn"] = hn
        if i % 2 == 0:
            pre = _matmul(hn, W[("a_w_in", j)], mode="nn", bias=dep(row(a_b_in, j), toks), out_dtype=F32,
                          name=f"gmlp_in_{i}")
            gated = _sgu_fwd(pre, row(a_ln_g, j), row(a_ln_b, j), a_w_s[j], a_b_s[j].T, name=f"sgu_fwd_{i}")
            mix = _matmul(gated, W[("a_w_out", j)], mode="nn", out_dtype=F32, name=f"gmlp_out_{i}")
            s.update(pre=pre, gated=gated)
        else:
            qkv = _matmul(hn, W[("b_w_qkv", j)], mode="nn", bias=dep(row(b_qkv_full, j), toks), out_dtype=F32,
                          name=f"attn_qkv_{i}")
            qr, kr, vr = _rope_fwd(qkv, ctab, stab, q_width=q_width, kv_width=kv_width, name=f"rope_fwd_{i}")
            o = _attn_fwd(qr, kr, vr, row(b_sinks, j), name=f"attn_fwd_{i}")
            mix = _matmul(o, W[("b_w_o", j)], mode="nn", out_dtype=F32, name=f"attn_o_{i}")
            s.update(qr=qr, kr=kr, vr=vr, o=o)
        s["mix"] = mix
        toks = [ffn_w.send(mix)] if ffn_w else []
        h1, fn = _rms_res_norm(h, mix, dep(row(post_mix_g, i), toks), row(pre_ffn_g, i), name=f"rms_post_mix_{i}")
        if ffn_w:
            ffn_w.send(h1)
        s["h1"] = h1
        g_pre, u_pre, act = _ffn_up(fn, W[("ffn_w_gu", i)][None], 0, name=f"ffn_up_{i}")
        f = _matmul(act, W[("ffn_w_down", i)], mode="nn", out_dtype=F32, name=f"ffn_down_{i}")
        if i + 1 < depth:
            toks = [nxt.send(f)]
            h, hn = _rms_res_norm(h1, f, dep(row(post_ffn_g, i), toks), row(pre_mix_g, i + 1),
                                  name=f"rms_post_ffn_{i}")
            nxt.send(h)
        else:
            h = _rms_res(h1, f, row(post_ffn_g, i), name=f"rms_post_ffn_{i}")
        s.update(fn=fn, g_pre=g_pre, u_pre=u_pre, act=act, f=f)
        saved.append(s)

    dh, df, loss_part, g_last = _loss_and_grad(h, target, saved[-1]["f"], row(post_ffn_g, depth - 1), name="loss")

    big_out = {nm: tuple(lax.empty(w.shape, F32) for _ in range(4)) for nm, (w, _, _) in stacked.items()}

    def reduce_group(i, keys, grads):
        axes = [cut[nm] for nm, _ in keys]
        n = len(keys)
        lands = [lax.empty(_half_shape(g.shape, ax), BF16) for g, ax in zip(grads, axes)]
        ss, rs, bufs, tok = _split_start(f"rs_sibling_start_{i}", list(grads) + lands, n, _rs_sibling_copies(axes),
                                         place)
        after = yield tok
        bufs = _split_wait(f"rs_sibling_wait_{i}", bufs, (ss, rs), _rs_sibling_copies(axes), after)
        sums = [_chip_sum(bufs[w], bufs[n + w], axes[w], place, name=f"chip_sum_{keys[w][0]}_{keys[w][1]}")
                for w in range(n)]
        lands = [lax.empty((3,) + own.shape, BF16) for _, own in sums]
        ss, rs, bufs, tok = _split_start(f"rs_chip_start_{i}", [sb for sb, _ in sums] + lands, 3 * n,
                                         _rs_chip_copies(axes), place)
        after = yield tok
        bufs = _split_wait(f"rs_chip_wait_{i}", bufs, (ss, rs), _rs_chip_copies(axes), after)
        blocks = [_final_sum(sums[w][1], bufs[n + w], axes[w], place, name=f"final_sum_{keys[w][0]}_{keys[w][1]}")
                  for w in range(n)]
        ss, rs, bufs, tok = _split_start(f"rs_fill_start_{i}", blocks, n, _rs_fill_copies(axes), place)
        after = yield tok
        blocks = _split_wait(f"rs_fill_wait_{i}", bufs, (ss, rs), _rs_fill_copies(axes), after)
        for (nm, l), g in zip(keys, blocks):
            w, m, v = stacked[nm]
            big_out[nm] = tuple(_adamw_layer(w, m, v, g, l, big_out[nm], name=f"adamw_{nm}_{l}"))
        yield None

    reducing = []

    def advance(after, newest_only=False):
        toks = []
        for gen in (reducing[-1:] if newest_only else list(reducing)):
            tok = gen.send(after)
            if tok is None:
                reducing.remove(gen)
            else:
                toks.append(tok)
        return toks

    small = {}
    g_pre_mix, g_post_mix, g_pre_ffn, g_post_ffn = [None] * depth, [None] * depth, [None] * depth, [None] * depth
    g_post_ffn[depth - 1] = g_last
    toks = []
    early = []
    for i in reversed(range(depth)):
        j = i // 2
        s = saved[i]
        g_down = _matmul(s["act"], df, mode="tn", out_dtype=BF16, after=toks, name=f"ffn_down_dw_{i}")
        dg_, du_ = _ffn_down_dx(df, W[("ffn_w_down", i)][None], 0, s["g_pre"], s["u_pre"], g_down,
                                name=f"ffn_down_dx_{i}")
        hid = dg_.shape[1]
        tile = _pick(hid, (1408, 768, 512, 256, 128))
        w_gu = W[("ffn_w_gu", i)]
        g_gu = lax.empty(w_gu.shape, BF16)
        g_gu = _matmul(s["fn"], dg_, mode="tn", into=g_gu, tq=tile, out_dtype=BF16, name=f"ffn_g_dw_{i}")
        g_gu = _matmul(s["fn"], du_, mode="tn", into=g_gu, tq=tile, q_off=hid // tile, out_dtype=BF16,
                       name=f"ffn_u_dw_{i}")
        dfn_g = _matmul(dg_, w_gu, mode="nt", tr=hid, out_dtype=F32, after=[g_gu], name=f"ffn_g_dx_{i}")
        dfn = _matmul(du_, w_gu, mode="nt", tr=hid, b_r_off=1, bias=dfn_g, out_dtype=F32, name=f"ffn_u_dx_{i}")
        toks = advance(dfn)
        if i == 0:
            gen = reduce_group("0f", layer_keys(0)[2:], [g_gu, g_down])
            toks.append(next(gen))
            reducing.append(gen)
        dh1, dmix, g_pre_ffn[i], g_post_mix[i] = _rms_bwd_chain(
            s["h1"], dep(row(pre_ffn_g, i), toks), dfn, dh, s["mix"], row(post_mix_g, i), name=f"rms_ffn_mix_bwd_{i}")
        if i % 2 == 0:
            g_out = _matmul(s["gated"], dmix, mode="tn", out_dtype=BF16, name=f"gmlp_out_dw_{i}")
            dgated = _matmul(dmix, W[("a_w_out", j)], mode="nt", out_dtype=BF16, after=[g_out],
                             name=f"gmlp_out_dx_{i}")
            toks = advance(dgated, newest_only=True) if i == 0 else []
            dpre, dws, dbsT, dlng, dlnb, dbin = _sgu_bwd(s["pre"], dgated, dep(row(a_ln_g, j), toks), row(a_ln_b, j),
                                                         a_w_s[j], a_b_s[j].T, name=f"sgu_bwd_{i}")
            small[("a_w_s", j)] = dws
            small[("a_b_s", j)] = dbsT.T
            small[("a_ln_g", j)] = dlng
            small[("a_ln_b", j)] = dlnb
            small[("a_b_in", j)] = dbin
            g_in = _matmul(s["hn"], dpre, mode="tn", out_dtype=BF16, name=f"gmlp_in_dw_{i}")
            if i == 0:
                last = reduce_group("0m", layer_keys(0)[:2], [g_in, g_out])
                early = [next(last)]
            dhn = _matmul(dpre, W[("a_w_in", j)], mode="nt", out_dtype=F32, after=[g_in] + early,
                          name=f"gmlp_in_dx_{i}")
        else:
            g_out = _matmul(s["o"], dmix, mode="tn", out_dtype=BF16, name=f"attn_o_dw_{i}")
            do = _matmul(dmix, W[("b_w_o", j)], mode="nt", out_dtype=BF16, after=[g_out], name=f"attn_o_dx_{i}")
            dq, dkp, dkc, dvp, dvc, dsk = _attn_bwd(s["qr"], s["kr"], s["vr"], row(b_sinks, j), do,
                                                    name=f"attn_bwd_{i}")
            dqkv, dbq = _rope_bwd(dq, dkp, dkc, dvp, dvc, ctab, stab, name=f"rope_bwd_{i}")
            small[("b_sinks", j)] = dsk[:, :b_sinks.shape[1]]
            small[("b_b_qkv", j)] = dbq
            g_in = _matmul(s["hn"], dqkv, mode="tn", out_dtype=BF16, name=f"attn_qkv_dw_{i}")
            if i == 0:
                last = reduce_group("0m", layer_keys(0)[:2], [g_in, g_out])
                early = [next(last)]
            dhn = _matmul(dqkv, W[("b_w_qkv", j)], mode="nt", out_dtype=F32, after=[g_in] + early,
                          name=f"attn_qkv_dx_{i}")
        toks = advance(dhn)
        if i > 0:
            dh, df, g_pre_mix[i], g_post_ffn[i - 1] = _rms_bwd_chain(
                s["h"], dep(row(pre_mix_g, i), toks), dhn, dh1, saved[i - 1]["f"], row(post_ffn_g, i - 1),
                name=f"rms_mix_ffn_bwd_{i}")
            gen = reduce_group(str(i), layer_keys(i), [g_in, g_out, g_gu, g_down])
            toks = [next(gen)] + advance(dh)
            reducing.append(gen)
        else:
            toks.append(last.send(dhn))
            dh, g_pre_mix[i] = _rms_bwd(s["h"], dep(row(pre_mix_g, i), toks), dhn, dh1, out_dtype=F32,
                                        name=f"rms_pre_mix_bwd_{i}")
            advance(dh)
    grad_x = dh[None]
    assert not reducing

    ready = [big_out[nm][1] for nm in big_out]
    n_a, n_b = a_b_in.shape[0], b_sinks.shape[0]
    stack = lambda key, n: jnp.concatenate([small[(key, j)] for j in range(n)], axis=0)
    small_parts = [
        jnp.concatenate(g_pre_mix, axis=0), jnp.concatenate(g_post_mix, axis=0),
        jnp.concatenate(g_pre_ffn, axis=0), jnp.concatenate(g_post_ffn, axis=0),
        stack("a_b_in", n_a), stack("a_ln_g", n_a), stack("a_ln_b", n_a),
        jnp.stack([small[("a_w_s", j)] for j in range(n_a)]), jnp.stack([small[("a_b_s", j)] for j in range(n_a)]),
        stack("b_b_qkv", n_b), stack("b_sinks", n_b), loss_part,
    ]
    packed, metas = _pack_rows(small_parts)
    reduced = _allreduce_small(packed, after=ready + [dh])
    while last.send(reduced) is not None:
        pass
    red = _unpack_rows(reduced, metas)
    (gr_pre_mix, gr_post_mix, gr_pre_ffn, gr_post_ffn, gr_b_in, gr_ln_g, gr_ln_b, gr_w_s, gr_b_s,
     gr_b_qkv_full, gr_sinks, loss_sum) = red
    loss = loss_sum[0, 0]
    gr_b_qkv = lax.dynamic_slice(gr_b_qkv_full, (0, chip * nq), (gr_b_qkv_full.shape[0], nq))

    grads = {"pre_mix_g": gr_pre_mix, "post_mix_g": gr_post_mix, "pre_ffn_g": gr_pre_ffn, "post_ffn_g": gr_post_ffn,
             "a_b_in": gr_b_in, "a_ln_g": gr_ln_g, "a_ln_b": gr_ln_b, "a_w_s": gr_w_s, "a_b_s": gr_b_s,
             "b_b_qkv": gr_b_qkv, "b_sinks": gr_sinks}
    weights = {"pre_mix_g": (pre_mix_g, m_pre_mix_g, v_pre_mix_g), "post_mix_g": (post_mix_g, m_post_mix_g, v_post_mix_g),
               "pre_ffn_g": (pre_ffn_g, m_pre_ffn_g, v_pre_ffn_g), "post_ffn_g": (post_ffn_g, m_post_ffn_g, v_post_ffn_g),
               "a_b_in": (a_b_in, m_a_b_in, v_a_b_in), "a_ln_g": (a_ln_g, m_a_ln_g, v_a_ln_g),
               "a_ln_b": (a_ln_b, m_a_ln_b, v_a_ln_b), "a_w_s": (a_w_s, m_a_w_s, v_a_w_s), "a_b_s": (a_b_s, m_a_b_s, v_a_b_s),
               "b_b_qkv": (b_b_qkv, m_b_b_qkv, v_b_b_qkv), "b_sinks": (b_sinks, m_b_sinks, v_b_sinks)}
    order = ["pre_mix_g", "post_mix_g", "pre_ffn_g", "post_ffn_g", "a_w_in", "a_b_in", "a_ln_g", "a_ln_b", "a_w_s",
             "a_b_s", "a_w_out", "b_w_qkv", "b_b_qkv", "b_sinks", "b_w_o", "ffn_w_gu", "ffn_w_down"]
    deltas, new_m, new_v = {}, {}, {}
    for nm in order:
        if nm in big_out:
            grads[nm], deltas[nm], new_m[nm], new_v[nm] = big_out[nm]
        else:
            w, m, v = weights[nm]
            deltas[nm], new_m[nm], new_v[nm] = _adamw_small(w, grads[nm], m, v, name="adamw_" + nm)
    return (loss, grad_x, *[grads[nm] for nm in order], *[deltas[nm] for nm in order],
            *[new_m[nm] for nm in order], *[new_v[nm] for nm in order])
```

```python
import functools
import math

import jax
import jax.numpy as jnp
import numpy as np
from jax import lax
from jax.experimental import pallas as pl
from jax.experimental.pallas import tpu as pltpu

F32 = jnp.float32
BF16 = jnp.bfloat16
MESH = pl.DeviceIdType.MESH

HEAD_DIM = 64
N_KV_HEADS = 4
ROPE_DIM = 16
ROPE_THETA = 500000.0
CHUNK = 128
GMLP_GROUPS = 8
RMS_EPS = 1e-6
LN_EPS = 1e-5
NEG_INF = -1e30
ADAM_LR = 0.001
ADAM_B1 = 0.9
ADAM_B2 = 0.999
ADAM_EPS = 1e-08
ADAM_WD = 0.01
ADAM_STEP = 10

N_CHIPS = 4
LANES = 128
VMEM_CAP = 58 * 1024 * 1024


def _vmem(est_bytes):
    assert est_bytes < VMEM_CAP
    return VMEM_CAP


def _pick(n, cands):
    for c in cands:
        if c <= n and n % c == 0:
            return c
    return n


def _nbytes(shape, dtype):
    return int(np.prod(shape)) * jnp.dtype(dtype).itemsize


MATMUL_VMEM_BUDGET = 48 * 1024 * 1024


def _halvings(n, unit):
    out, t = [], n
    while t % unit == 0 and t >= unit:
        out.append(t)
        if t % 2:
            break
        t //= 2
    return out


def _matmul_tiles(P, Q, R, a_bytes, b_bytes, o_bytes, full_addend, tp, tq, tr):
    step_us, bytes_per_us = 0.85, 3.2e6
    best = None
    for p in ([tp] if tp else _halvings(P, LANES)):
        for q in ([tq] if tq else _halvings(Q, LANES)):
            for r in ([tr] if tr else _halvings(R, LANES)):
                nk = R // r
                vm = 2 * (p * r * a_bytes + r * q * b_bytes + p * q * o_bytes + (p * q * 4 if full_addend else 0))
                vm += p * q * 4 * (2 if nk > 1 else 1)
                if vm > MATMUL_VMEM_BUDGET:
                    continue
                exposed = (p * r * a_bytes + r * q * b_bytes + p * q * o_bytes) / bytes_per_us
                key = ((P // p) * (Q // q) * nk * step_us + exposed, nk, abs(p - q))
                if best is None or key < best[0]:
                    best = (key, (p, q, r))
    assert best is not None, (P, Q, R)
    return best[1]


def _matmul(a, b, *, mode, out_dtype, name, a_l=None, b_l=None, bias=None, into=None, o_l=None,
            q_off=0, b_r_off=0, tp=None, tq=None, tr=None, after=()):
    a2 = a.shape[-2:]
    b2 = b.shape[-2:]
    if mode == "nn":
        (P, R), (R2, Q) = a2, b2
    elif mode == "nt":
        (P, R), (Q, R2) = a2, b2
    else:
        (R, P), (R2, Q) = a2, b2
    assert R == R2 or (mode == "nt" and R2 % R == 0), (mode, a.shape, b.shape)
    o_bytes = jnp.dtype(into.dtype if into is not None else out_dtype).itemsize
    full_addend = bias is not None and bias.shape[0] != 1
    tp, tq, tr = _matmul_tiles(P, Q, R, a.dtype.itemsize, b.dtype.itemsize, o_bytes, full_addend, tp, tq, tr)
    assert P % tp == 0 and Q % tq == 0 and R % tr == 0
    nk = R // tr
    dims = {"nn": (((1,), (0,)), ((), ())), "nt": (((1,), (1,)), ((), ())), "tn": (((0,), (0,)), ((), ()))}[mode]

    def lead(l, blk, idx):
        if l is None:
            return pl.BlockSpec(blk, idx)
        return pl.BlockSpec((None,) + blk, lambda i, j, k: (l,) + idx(i, j, k))

    if mode == "nn":
        a_spec = lead(a_l, (tp, tr), lambda i, j, k: (i, k))
        b_spec = lead(b_l, (tr, tq), lambda i, j, k: (k, j))
    elif mode == "nt":
        a_spec = lead(a_l, (tp, tr), lambda i, j, k: (i, k))
        b_spec = lead(b_l, (tq, tr), lambda i, j, k: (j, k + b_r_off))
    else:
        a_spec = lead(a_l, (tr, tp), lambda i, j, k: (k, i))
        b_spec = lead(b_l, (tr, tq), lambda i, j, k: (k, j))
    in_specs = [a_spec, b_spec]
    args = [a, b]
    if bias is not None:
        if bias.shape[0] == 1:
            in_specs.append(pl.BlockSpec((1, tq), lambda i, j, k: (0, j)))
        else:
            in_specs.append(pl.BlockSpec((tp, tq), lambda i, j, k: (i, j)))
        args.append(bias)
    aliases = {}
    if into is not None:
        in_specs.append(pl.BlockSpec(memory_space=pl.ANY))
        args.append(into)
        aliases = {len(args) - 1: 0}
        out_shape = jax.ShapeDtypeStruct(into.shape, into.dtype)
        out_dtype = into.dtype
        if o_l is None:
            out_spec = pl.BlockSpec((tp, tq), lambda i, j, k: (i, j + q_off))
        else:
            out_spec = pl.BlockSpec((None, tp, tq), lambda i, j, k: (o_l, i, j + q_off))
    else:
        out_shape = jax.ShapeDtypeStruct((P, Q), out_dtype)
        out_spec = pl.BlockSpec((tp, tq), lambda i, j, k: (i, j))
    n_in = len(args) + len(after)
    in_specs += [pl.BlockSpec(memory_space=pl.ANY)] * len(after)
    args += list(after)
    has_bias = bias is not None
    has_into = into is not None

    def body(*refs):
        a_ref, b_ref = refs[0], refs[1]
        pos = 2
        bias_ref = None
        if has_bias:
            bias_ref = refs[pos]
            pos += 1
        o_ref = refs[n_in]
        acc_ref = refs[n_in + 1] if nk > 1 else None
        part = lax.dot_general(a_ref[...], b_ref[...], dims, preferred_element_type=F32)

        def finish(acc):
            if has_bias:
                acc = acc + bias_ref[...]
            o_ref[...] = acc.astype(out_dtype)

        if nk == 1:
            finish(part)
        else:
            k = pl.program_id(2)

            @pl.when(k == 0)
            def _():
                acc_ref[...] = part

            @pl.when(k > 0)
            def _():
                acc_ref[...] += part

            @pl.when(k == nk - 1)
            def _():
                finish(acc_ref[...])

    est = 2 * (_nbytes((tp, tr), a.dtype) + _nbytes((tr, tq), b.dtype) + _nbytes((tp, tq), out_dtype)) + 3 * tp * tq * 4
    return pl.pallas_call(
        body, name=name, out_shape=out_shape,
        grid=(P // tp, Q // tq, nk),
        in_specs=in_specs, out_specs=out_spec,
        scratch_shapes=[pltpu.VMEM((tp, tq), F32)] if nk > 1 else [],
        input_output_aliases=aliases,
        compiler_params=pltpu.CompilerParams(
            dimension_semantics=("parallel", "parallel", "arbitrary"), vmem_limit_bytes=_vmem(est)),
    )(*args)


def _row_call(body, ins, outs, *, name, rows, tr, acc_outs=(), est=0, after=()):
    in_specs, args = [], []
    for arr, kind in ins:
        if kind == "row":
            in_specs.append(pl.BlockSpec((tr, arr.shape[1]), lambda i: (i, 0)))
        elif isinstance(arr, tuple):
            arr, layer = arr
            in_specs.append(pl.BlockSpec((None,) + arr.shape[1:], lambda i, layer=layer: (layer, 0, 0)))
        else:
            nd = arr.ndim
            in_specs.append(pl.BlockSpec(arr.shape, lambda i, nd=nd: (0,) * nd))
        args.append(arr)
    n_ins = len(args)
    in_specs += [pl.BlockSpec(memory_space=pl.ANY)] * len(after)
    args += list(after)

    def kernel_fn(*refs):
        body(*refs[:n_ins], *refs[n_ins + len(after):])

    out_shapes = [jax.ShapeDtypeStruct(s, d) for s, d in outs] + [jax.ShapeDtypeStruct(s, d) for s, d in acc_outs]
    out_specs = [pl.BlockSpec((tr, s[1]), lambda i: (i, 0)) for s, _ in outs]
    out_specs += [pl.BlockSpec(s, lambda i, nd=len(s): (0,) * nd) for s, _ in acc_outs]
    res = pl.pallas_call(
        kernel_fn, name=name, out_shape=out_shapes, grid=(rows // tr,), in_specs=in_specs, out_specs=out_specs,
        compiler_params=pltpu.CompilerParams(dimension_semantics=("arbitrary",), vmem_limit_bytes=_vmem(est)),
    )(*args)
    return res


def _rms_fwd(x, g, *, out_dtype, name, after=()):
    T, D = x.shape
    tr = _pick(T, (512, 256, 128))

    def body(x_ref, g_ref, o_ref):
        xv = x_ref[...]
        r = lax.rsqrt(jnp.mean(xv * xv, axis=-1, keepdims=True) + RMS_EPS)
        o_ref[...] = (xv * r * g_ref[...]).astype(out_dtype)

    return _row_call(body, [(x, "row"), (g, "full")], [((T, D), out_dtype)], name=name, rows=T, tr=tr,
                     est=8 * tr * D * 4, after=after)[0]


def _rms_res(h, y, g, *, name):
    T, D = h.shape
    tr = _pick(T, (512, 256, 128))

    def body(h_ref, y_ref, g_ref, o_ref):
        yv = y_ref[...]
        r = lax.rsqrt(jnp.mean(yv * yv, axis=-1, keepdims=True) + RMS_EPS)
        o_ref[...] = h_ref[...] + yv * r * g_ref[...]

    return _row_call(body, [(h, "row"), (y, "row"), (g, "full")], [((T, D), F32)], name=name, rows=T, tr=tr,
                     est=10 * tr * D * 4)[0]


def _rms_bwd(x, g, dy, dres, *, out_dtype, name, after=()):
    T, D = x.shape
    tr = _pick(T, (512, 256, 128))
    has_res = dres is not None

    def body(*refs):
        if has_res:
            x_ref, g_ref, dy_ref, dr_ref, dx_ref, dg_ref = refs
        else:
            x_ref, g_ref, dy_ref, dx_ref, dg_ref = refs
        xv = x_ref[...]
        r = lax.rsqrt(jnp.mean(xv * xv, axis=-1, keepdims=True) + RMS_EPS)
        xhat = xv * r
        dyv = dy_ref[...].astype(F32)
        dxn = dyv * g_ref[...]
        dx = r * (dxn - xhat * jnp.mean(dxn * xhat, axis=-1, keepdims=True))
        if has_res:
            dx = dx + dr_ref[...]
        dx_ref[...] = dx.astype(out_dtype)
        part = jnp.sum(dyv * xhat, axis=0, keepdims=True)

        @pl.when(pl.program_id(0) == 0)
        def _():
            dg_ref[...] = part

        @pl.when(pl.program_id(0) > 0)
        def _():
            dg_ref[...] += part

    ins = [(x, "row"), (g, "full"), (dy, "row")] + ([(dres, "row")] if has_res else [])
    dx, dg = _row_call(body, ins, [((T, D), out_dtype)], name=name, rows=T, tr=tr, acc_outs=[((1, D), F32)],
                       est=12 * tr * D * 4, after=after)
    return dx, dg


def _rms_res_norm(h, y, g_res, g_next, *, name, after=()):
    T, D = h.shape
    tr = _pick(T, (512, 256, 128))

    def body(h_ref, y_ref, g_ref, gn_ref, o_ref, n_ref):
        yv = y_ref[...]
        r = lax.rsqrt(jnp.mean(yv * yv, axis=-1, keepdims=True) + RMS_EPS)
        h2 = h_ref[...] + yv * r * g_ref[...]
        o_ref[...] = h2
        r2 = lax.rsqrt(jnp.mean(h2 * h2, axis=-1, keepdims=True) + RMS_EPS)
        n_ref[...] = (h2 * r2 * gn_ref[...]).astype(BF16)

    return _row_call(body, [(h, "row"), (y, "row"), (g_res, "full"), (g_next, "full")],
                     [((T, D), F32), ((T, D), BF16)], name=name, rows=T, tr=tr, est=12 * tr * D * 4, after=after)


def _rms_bwd_chain(x1, g1, dy1, dres, x2, g2, *, name, after=()):
    T, D = x1.shape
    tr = _pick(T, (512, 256, 128))

    def one(xv, gv, dyv):
        r = lax.rsqrt(jnp.mean(xv * xv, axis=-1, keepdims=True) + RMS_EPS)
        xhat = xv * r
        dxn = dyv * gv
        dx = r * (dxn - xhat * jnp.mean(dxn * xhat, axis=-1, keepdims=True))
        return dx, jnp.sum(dyv * xhat, axis=0, keepdims=True)

    def body(x1_ref, g1_ref, dy1_ref, dr_ref, x2_ref, g2_ref, d1_ref, d2_ref, dg1_ref, dg2_ref):
        dx1, p1 = one(x1_ref[...], g1_ref[...], dy1_ref[...].astype(F32))
        d1 = dx1 + dr_ref[...]
        d1_ref[...] = d1
        dx2, p2 = one(x2_ref[...], g2_ref[...], d1)
        d2_ref[...] = dx2.astype(BF16)

        @pl.when(pl.program_id(0) == 0)
        def _():
            dg1_ref[...] = p1
            dg2_ref[...] = p2

        @pl.when(pl.program_id(0) > 0)
        def _():
            dg1_ref[...] += p1
            dg2_ref[...] += p2

    ins = [(x1, "row"), (g1, "full"), (dy1, "row"), (dres, "row"), (x2, "row"), (g2, "full")]
    return _row_call(body, ins, [((T, D), F32), ((T, D), BF16)], name=name, rows=T, tr=tr,
                     acc_outs=[((1, D), F32), ((1, D), F32)], est=20 * tr * D * 4, after=after)


def _ffn_up(fn, w_gu, l, *, name):
    T, D = fn.shape
    H = w_gu.shape[2] // 2
    tp = _pick(T, (1024, 512, 256, 128))
    tq = _pick(H, (1408, 768, 512, 256, 128))
    nj = H // tq

    def body(a_ref, wg_ref, wu_ref, g_ref, u_ref, act_ref):
        a = a_ref[...]
        g = jnp.dot(a, wg_ref[...], preferred_element_type=F32)
        u = jnp.dot(a, wu_ref[...], preferred_element_type=F32)
        sg = jax.nn.sigmoid(g)
        silu = g * sg
        g_ref[...] = (u * (sg + silu * (1.0 - sg))).astype(BF16)
        u_ref[...] = silu.astype(BF16)
        act_ref[...] = (silu * u).astype(BF16)

    tile = pl.BlockSpec((tp, tq), lambda j, i: (i, j))
    est = 2 * (tp * D * 2 + 2 * D * tq * 2 + 3 * tp * tq * 2) + 4 * tp * tq * 4
    return pl.pallas_call(
        body, name=name,
        out_shape=[jax.ShapeDtypeStruct((T, H), BF16), jax.ShapeDtypeStruct((T, H), BF16),
                   jax.ShapeDtypeStruct((T, H), BF16)],
        grid=(nj, T // tp),
        in_specs=[pl.BlockSpec((tp, D), lambda j, i: (i, 0)),
                  pl.BlockSpec((None, D, tq), lambda j, i: (l, 0, j)),
                  pl.BlockSpec((None, D, tq), lambda j, i: (l, 0, j + nj))],
        out_specs=[tile, tile, tile],
        compiler_params=pltpu.CompilerParams(dimension_semantics=("parallel", "parallel"),
                                             vmem_limit_bytes=_vmem(est)),
    )(fn, w_gu, w_gu)


def _ffn_down_dx(df, w_down, l, g, u, after, *, name):
    T, D = df.shape
    H = w_down.shape[1]
    tp = _pick(T, (1024, 512, 256, 128))
    tq = _pick(H, (1408, 768, 512, 256, 128))

    def body(a_ref, w_ref, g_ref, u_ref, _, dg_ref, du_ref):
        da = lax.dot_general(a_ref[...], w_ref[...], (((1,), (1,)), ((), ())), preferred_element_type=F32)
        dg_ref[...] = (da * g_ref[...].astype(F32)).astype(BF16)
        du_ref[...] = (da * u_ref[...].astype(F32)).astype(BF16)

    tile = pl.BlockSpec((tp, tq), lambda j, i: (i, j))
    est = 2 * (tp * D * 2 + tq * D * 2 + 4 * tp * tq * 2) + 3 * tp * tq * 4
    return pl.pallas_call(
        body, name=name,
        out_shape=[jax.ShapeDtypeStruct((T, H), BF16), jax.ShapeDtypeStruct((T, H), BF16)],
        grid=(H // tq, T // tp),
        in_specs=[pl.BlockSpec((tp, D), lambda j, i: (i, 0)),
                  pl.BlockSpec((None, tq, D), lambda j, i: (l, j, 0)), tile, tile,
                  pl.BlockSpec(memory_space=pl.ANY)],
        out_specs=[tile, tile],
        compiler_params=pltpu.CompilerParams(dimension_semantics=("parallel", "parallel"),
                                             vmem_limit_bytes=_vmem(est)),
    )(df, w_down, g, u, after)


def _loss_and_grad(y, target, x, g, *, name):
    T, D = y.shape
    tr = _pick(T, (512, 256, 128))

    def body(y_ref, t_ref, x_ref, g_ref, dy_ref, dx_ref, l_ref, dg_ref):
        e = y_ref[...] - t_ref[...]
        dy = e * (1.0 / D)
        dy_ref[...] = dy
        part = jnp.sum(jnp.sum(e * e, axis=1, keepdims=True), axis=0, keepdims=True) * (0.5 / D)
        xv = x_ref[...]
        r = lax.rsqrt(jnp.mean(xv * xv, axis=-1, keepdims=True) + RMS_EPS)
        xhat = xv * r
        dxn = dy * g_ref[...]
        dx_ref[...] = (r * (dxn - xhat * jnp.mean(dxn * xhat, axis=-1, keepdims=True))).astype(BF16)
        dg = jnp.sum(dy * xhat, axis=0, keepdims=True)

        @pl.when(pl.program_id(0) == 0)
        def _():
            l_ref[...] = part
            dg_ref[...] = dg

        @pl.when(pl.program_id(0) > 0)
        def _():
            l_ref[...] += part
            dg_ref[...] += dg

    dy, dx, l, dg = _row_call(body, [(y, "row"), (target, "row"), (x, "row"), (g, "full")],
                              [((T, D), F32), ((T, D), BF16)], name=name, rows=T, tr=tr,
                              acc_outs=[((1, 1), F32), ((1, D), F32)], est=14 * tr * D * 4)
    return dy, dx, l, dg


_SQRT_HALF = 0.7071067811865476
_INV_SQRT_2PI = 0.3989422804014327


def _gelu_parts(x):
    cdf = 0.5 * (1.0 + lax.erf(x * _SQRT_HALF))
    return cdf


def _sgu_common(pre, lng, lnb, W):
    cdf = _gelu_parts(pre)
    z = pre * cdf
    u = z[:, :W]
    v = z[:, W:]
    mu = jnp.mean(v, axis=-1, keepdims=True)
    vc = v - mu
    var = jnp.mean(vc * vc, axis=-1, keepdims=True)
    rstd = lax.rsqrt(var + LN_EPS)
    vhat = vc * rstd
    vn = vhat * lng + lnb
    return cdf, u, vhat, rstd, vn


def _causal_mask():
    t = lax.broadcasted_iota(jnp.int32, (CHUNK, CHUNK), 0)
    s = lax.broadcasted_iota(jnp.int32, (CHUNK, CHUNK), 1)
    return t >= s


def _sgu_fwd(pre, lng, lnb, ws, bsT, *, name):
    T, W2 = pre.shape
    W = W2 // 2
    G = ws.shape[0]
    gd = W // G

    def body(pre_ref, lng_ref, lnb_ref, ws_ref, bs_ref, o_ref):
        _, u, _, _, vn = _sgu_common(pre_ref[...], lng_ref[...], lnb_ref[...], W)
        vnb = vn.astype(BF16)
        causal = _causal_mask()
        for g in range(G):
            w = jnp.where(causal, ws_ref[g], 0.0).astype(BF16)
            sv = jnp.dot(w, vnb[:, g * gd:(g + 1) * gd], preferred_element_type=F32) + bs_ref[:, g:g + 1]
            o_ref[:, g * gd:(g + 1) * gd] = (u[:, g * gd:(g + 1) * gd] * sv).astype(BF16)

    return pl.pallas_call(
        body, name=name, out_shape=jax.ShapeDtypeStruct((T, W), BF16), grid=(T // CHUNK,),
        in_specs=[pl.BlockSpec((CHUNK, W2), lambda i: (i, 0)),
                  pl.BlockSpec((1, W), lambda i: (0, 0)), pl.BlockSpec((1, W), lambda i: (0, 0)),
                  pl.BlockSpec(ws.shape, lambda i: (0, 0, 0)), pl.BlockSpec(bsT.shape, lambda i: (0, 0))],
        out_specs=pl.BlockSpec((CHUNK, W), lambda i: (i, 0)),
        compiler_params=pltpu.CompilerParams(dimension_semantics=("arbitrary",),
                                             vmem_limit_bytes=_vmem(12 * CHUNK * W2 * 4)),
    )(pre, lng, lnb, ws, bsT)


def _sgu_bwd(pre, dgated, lng, lnb, ws, bsT, *, name):
    T, W2 = pre.shape
    W = W2 // 2
    G = ws.shape[0]
    gd = W // G

    def body(pre_ref, dgt_ref, lng_ref, lnb_ref, ws_ref, bs_ref,
             dpre_ref, dws_ref, dbs_ref, dlng_ref, dlnb_ref, dbin_ref):
        first = pl.program_id(0) == 0

        @pl.when(first)
        def _():
            dws_ref[...] = jnp.zeros_like(dws_ref)
            dbs_ref[...] = jnp.zeros_like(dbs_ref)
            dlng_ref[...] = jnp.zeros_like(dlng_ref)
            dlnb_ref[...] = jnp.zeros_like(dlnb_ref)
            dbin_ref[...] = jnp.zeros_like(dbin_ref)

        pre_v = pre_ref[...]
        lng_v = lng_ref[...]
        cdf, u, vhat, rstd, vn = _sgu_common(pre_v, lng_v, lnb_ref[...], W)
        vnb = vn.astype(BF16)
        dgt = dgt_ref[...].astype(F32)
        causal = _causal_mask()
        du_parts, dvn_parts = [], []
        for g in range(G):
            sl = slice(g * gd, (g + 1) * gd)
            w = jnp.where(causal, ws_ref[g], 0.0).astype(BF16)
            sv = jnp.dot(w, vnb[:, sl], preferred_element_type=F32) + bs_ref[:, g:g + 1]
            dgt_g = dgt[:, sl]
            du_parts.append(dgt_g * sv)
            dsv = dgt_g * u[:, sl]
            dsvb = dsv.astype(BF16)
            dvn_parts.append(lax.dot_general(w, dsvb, (((0,), (0,)), ((), ())), preferred_element_type=F32))
            dw = lax.dot_general(dsvb, vnb[:, sl], (((1,), (1,)), ((), ())), preferred_element_type=F32)
            dws_ref[g] += jnp.where(causal, dw, 0.0)
            dbs_ref[:, g:g + 1] += jnp.sum(dsv, axis=1, keepdims=True)
        du = jnp.concatenate(du_parts, axis=1)
        dvn = jnp.concatenate(dvn_parts, axis=1)
        dlng_ref[...] += jnp.sum(dvn * vhat, axis=0, keepdims=True)
        dlnb_ref[...] += jnp.sum(dvn, axis=0, keepdims=True)
        dvh = dvn * lng_v
        dv = rstd * (dvh - jnp.mean(dvh, axis=-1, keepdims=True)
                     - vhat * jnp.mean(dvh * vhat, axis=-1, keepdims=True))
        dz = jnp.concatenate([du, dv], axis=1)
        dgelu = cdf + pre_v * jnp.exp(-0.5 * pre_v * pre_v) * _INV_SQRT_2PI
        dpre = dz * dgelu
        dbin_ref[...] += jnp.sum(dpre, axis=0, keepdims=True)
        dpre_ref[...] = dpre.astype(BF16)

    full = lambda shape: pl.BlockSpec(shape, lambda i, nd=len(shape): (0,) * nd)
    return pl.pallas_call(
        body, name=name,
        out_shape=[jax.ShapeDtypeStruct((T, W2), BF16), jax.ShapeDtypeStruct(ws.shape, F32),
                   jax.ShapeDtypeStruct(bsT.shape, F32), jax.ShapeDtypeStruct((1, W), F32),
                   jax.ShapeDtypeStruct((1, W), F32), jax.ShapeDtypeStruct((1, W2), F32)],
        grid=(T // CHUNK,),
        in_specs=[pl.BlockSpec((CHUNK, W2), lambda i: (i, 0)), pl.BlockSpec((CHUNK, W), lambda i: (i, 0)),
                  full((1, W)), full((1, W)), full(ws.shape), full(bsT.shape)],
        out_specs=[pl.BlockSpec((CHUNK, W2), lambda i: (i, 0)), full(ws.shape), full(bsT.shape),
                   full((1, W)), full((1, W)), full((1, W2))],
        compiler_params=pltpu.CompilerParams(dimension_semantics=("arbitrary",),
                                             vmem_limit_bytes=_vmem(24 * CHUNK * W2 * 4)),
    )(pre, dgated, lng, lnb, ws, bsT)


def _rope_tables(positions):
    half = ROPE_DIM // 2
    inv_freq = ROPE_THETA ** (-jnp.arange(0, ROPE_DIM, 2, dtype=F32) / ROPE_DIM)
    ang = positions.astype(F32).reshape(-1, 1) * inv_freq
    cos, sin = jnp.cos(ang), jnp.sin(ang)
    T = ang.shape[0]
    rest = HEAD_DIM - ROPE_DIM
    c64 = jnp.concatenate([cos, cos, jnp.ones((T, rest), F32)], axis=1)
    s64 = jnp.concatenate([-sin, sin, jnp.zeros((T, rest), F32)], axis=1)
    del half
    return jnp.tile(c64, (1, LANES // HEAD_DIM)), jnp.tile(s64, (1, LANES // HEAD_DIM))


def _swap8(x):
    W = x.shape[1]
    half = ROPE_DIM // 2
    lane = lax.broadcasted_iota(jnp.int32, x.shape, 1) % HEAD_DIM
    return jnp.where(lane < half, pltpu.roll(x, W - half, axis=1),
                     jnp.where(lane < ROPE_DIM, pltpu.roll(x, half, axis=1), 0.0))


def _wide(tab, W):
    return jnp.concatenate([tab] * (W // LANES), axis=1) if W > LANES else tab


def _rope_fwd(qkv, ctab, stab, *, q_width, kv_width, name):
    T = qkv.shape[0]
    tr = _pick(T, (256, 128))
    scale = HEAD_DIM ** -0.5

    def body(x_ref, c_ref, s_ref, q_ref, k_ref, v_ref):
        c = c_ref[...]
        s = s_ref[...]
        q = x_ref[:, :q_width]
        k = x_ref[:, q_width:q_width + kv_width]
        q_ref[...] = ((q * _wide(c, q_width) + _swap8(q) * _wide(s, q_width)) * scale).astype(BF16)
        k_ref[...] = (k * _wide(c, kv_width) + _swap8(k) * _wide(s, kv_width)).astype(BF16)
        v_ref[...] = x_ref[:, q_width + kv_width:].astype(BF16)

    return _row_call(body, [(qkv, "row"), (ctab, "row"), (stab, "row")],
                     [((T, q_width), BF16), ((T, kv_width), BF16), ((T, kv_width), BF16)],
                     name=name, rows=T, tr=tr, est=10 * tr * qkv.shape[1] * 4)


_NT = (((1,), (1,)), ((), ()))
_TN = (((0,), (0,)), ((), ()))


def _group_rows(ref, heads):
    return jnp.concatenate([ref[:, h * HEAD_DIM:(h + 1) * HEAD_DIM] for h in heads], axis=0)


def _attn_valid(grp):
    qi = np.arange(grp * CHUNK)[:, None] % CHUNK
    sj = np.arange(2 * CHUNK)[None, :]
    cur = (sj >= CHUNK) & (sj - CHUNK <= qi)
    prev = (sj < CHUNK) & (sj > qi)
    return jnp.asarray(np.stack([cur, cur | prev]).astype(np.float32))


def _valid_spec(grp):
    return pl.BlockSpec((None, grp * CHUNK, 2 * CHUNK), lambda n: (jnp.minimum(n, 1), 0, 0))


def _attn_group_probs(q, kk, sinks, valid, grp):
    rows = grp * CHUNK
    s = lax.dot_general(q, kk, _NT, preferred_element_type=F32)
    s = jnp.where(valid, s, NEG_INF)
    r = lax.broadcasted_iota(jnp.int32, (rows, 1), 0)
    sink = jnp.full((rows, 1), sinks[grp - 1], F32)
    for g in range(grp - 2, -1, -1):
        sink = jnp.where(r < (g + 1) * CHUNK, sinks[g], sink)
    m = jnp.maximum(jnp.max(s, axis=1, keepdims=True), sink)
    p = jnp.exp(s - m)
    ps = jnp.exp(sink - m)
    inv = 1.0 / (jnp.sum(p, axis=1, keepdims=True) + ps)
    return p * inv, ps * inv


def _kv_specs(width, nb):
    prev = pl.BlockSpec((CHUNK, width), lambda n: (jnp.maximum(n - 1, 0), 0))
    cur = pl.BlockSpec((CHUNK, width), lambda n: (n, 0))
    return prev, cur


def _attn_fwd(qr, kr, vr, sinks, *, name):
    T, QW = qr.shape
    KW = kr.shape[1]
    HQ, HK = QW // HEAD_DIM, KW // HEAD_DIM
    grp = HQ // HK
    nb = T // CHUNK

    def body(q_ref, kp_ref, kc_ref, vp_ref, vc_ref, s_ref, ok_ref, o_ref):
        valid = ok_ref[...] > 0.5
        for kh in range(HK):
            ks = slice(kh * HEAD_DIM, (kh + 1) * HEAD_DIM)
            heads = list(range(kh * grp, (kh + 1) * grp))
            q = _group_rows(q_ref, heads)
            kk = jnp.concatenate([kp_ref[:, ks], kc_ref[:, ks]], axis=0)
            vv = jnp.concatenate([vp_ref[:, ks], vc_ref[:, ks]], axis=0)
            p, _ = _attn_group_probs(q, kk, [s_ref[0, h] for h in heads], valid, grp)
            o = jnp.dot(p.astype(BF16), vv, preferred_element_type=F32).astype(BF16)
            for g, h in enumerate(heads):
                o_ref[:, h * HEAD_DIM:(h + 1) * HEAD_DIM] = o[g * CHUNK:(g + 1) * CHUNK]

    kp, kc = _kv_specs(KW, nb)
    return pl.pallas_call(
        body, name=name, out_shape=jax.ShapeDtypeStruct((T, QW), BF16), grid=(nb,),
        in_specs=[pl.BlockSpec((CHUNK, QW), lambda n: (n, 0)), kp, kc, kp, kc,
                  pl.BlockSpec(memory_space=pltpu.SMEM), _valid_spec(grp)],
        out_specs=pl.BlockSpec((CHUNK, QW), lambda n: (n, 0)),
        compiler_params=pltpu.CompilerParams(dimension_semantics=("arbitrary",), vmem_limit_bytes=_vmem(8 << 20)),
    )(qr, kr, kr, vr, vr, sinks, _attn_valid(grp))


def _attn_bwd(qr, kr, vr, sinks, do, *, name):
    T, QW = qr.shape
    KW = kr.shape[1]
    HQ, HK = QW // HEAD_DIM, KW // HEAD_DIM
    grp = HQ // HK
    nb = T // CHUNK

    def body(q_ref, kp_ref, kc_ref, vp_ref, vc_ref, s_ref, do_ref, ok_ref,
             dq_ref, dkp_ref, dkc_ref, dvp_ref, dvc_ref, ds_ref):
        n = pl.program_id(0)
        valid = ok_ref[...] > 0.5
        lane = lax.broadcasted_iota(jnp.int32, (1, LANES), 1)
        dsink = jnp.zeros((1, LANES), F32)
        for kh in range(HK):
            ks = slice(kh * HEAD_DIM, (kh + 1) * HEAD_DIM)
            heads = list(range(kh * grp, (kh + 1) * grp))
            q = _group_rows(q_ref, heads)
            doh = _group_rows(do_ref, heads)
            kk = jnp.concatenate([kp_ref[:, ks], kc_ref[:, ks]], axis=0)
            vv = jnp.concatenate([vp_ref[:, ks], vc_ref[:, ks]], axis=0)
            p, ps = _attn_group_probs(q, kk, [s_ref[0, h] for h in heads], valid, grp)
            dp = lax.dot_general(doh, vv, _NT, preferred_element_type=F32)
            delta = jnp.sum(p * dp, axis=1, keepdims=True)
            ds = (p * (dp - delta)).astype(BF16)
            dv = lax.dot_general(p.astype(BF16), doh, _TN, preferred_element_type=F32)
            dk = lax.dot_general(ds, q, _TN, preferred_element_type=F32)
            dq = jnp.dot(ds, kk, preferred_element_type=F32)
            psd = ps * delta
            for g, h in enumerate(heads):
                dq_ref[:, h * HEAD_DIM:(h + 1) * HEAD_DIM] = dq[g * CHUNK:(g + 1) * CHUNK]
                dsink = dsink + jnp.where(
                    lane == h, -jnp.sum(psd[g * CHUNK:(g + 1) * CHUNK], axis=0, keepdims=True), 0.0)
            dkp_ref[:, ks] = dk[:CHUNK]
            dkc_ref[:, ks] = dk[CHUNK:]
            dvp_ref[:, ks] = dv[:CHUNK]
            dvc_ref[:, ks] = dv[CHUNK:]

        @pl.when(n == 0)
        def _():
            ds_ref[...] = dsink

        @pl.when(n > 0)
        def _():
            ds_ref[...] += dsink

    kp, kc = _kv_specs(KW, nb)
    qspec = pl.BlockSpec((CHUNK, QW), lambda n: (n, 0))
    kout = pl.BlockSpec((CHUNK, KW), lambda n: (n, 0))
    return pl.pallas_call(
        body, name=name,
        out_shape=[jax.ShapeDtypeStruct((T, QW), F32)] + [jax.ShapeDtypeStruct((T, KW), F32)] * 4
        + [jax.ShapeDtypeStruct((1, LANES), F32)],
        grid=(nb,),
        in_specs=[qspec, kp, kc, kp, kc, pl.BlockSpec(memory_space=pltpu.SMEM), qspec, _valid_spec(grp)],
        out_specs=[qspec, kout, kout, kout, kout, pl.BlockSpec((1, LANES), lambda n: (0, 0))],
        compiler_params=pltpu.CompilerParams(dimension_semantics=("arbitrary",), vmem_limit_bytes=_vmem(12 << 20)),
    )(qr, kr, kr, vr, vr, sinks, do, _attn_valid(grp))


def _rope_bwd(dq, dkp, dkc, dvp, dvc, ctab, stab, *, name):
    T, QW = dq.shape
    KW = dkp.shape[1]
    nb = T // CHUNK
    scale = HEAD_DIM ** -0.5
    width = QW + 2 * KW

    def body(dq_ref, dkc_ref, dkn_ref, dvc_ref, dvn_ref, c_ref, s_ref, o_ref, db_ref):
        n = pl.program_id(0)
        c = c_ref[...]
        s = s_ref[...]
        has_next = (n < nb - 1).astype(F32)
        dqv = dq_ref[...]
        dk = dkc_ref[...] + has_next * dkn_ref[...]
        dv = dvc_ref[...] + has_next * dvn_ref[...]
        dq_pre = (dqv * _wide(c, QW) + _swap8(dqv * _wide(s, QW))) * scale
        dk_pre = dk * _wide(c, KW) + _swap8(dk * _wide(s, KW))
        o_ref[:, :QW] = dq_pre.astype(BF16)
        o_ref[:, QW:QW + KW] = dk_pre.astype(BF16)
        o_ref[:, QW + KW:] = dv.astype(BF16)
        part = jnp.concatenate([jnp.sum(dq_pre, axis=0, keepdims=True), jnp.sum(dk_pre, axis=0, keepdims=True),
                                jnp.sum(dv, axis=0, keepdims=True)], axis=1)

        @pl.when(n == 0)
        def _():
            db_ref[...] = part

        @pl.when(n > 0)
        def _():
            db_ref[...] += part

    cur = lambda w: pl.BlockSpec((CHUNK, w), lambda n: (n, 0))
    nxt = lambda w: pl.BlockSpec((CHUNK, w), lambda n: (jnp.minimum(n + 1, nb - 1), 0))
    return pl.pallas_call(
        body, name=name,
        out_shape=[jax.ShapeDtypeStruct((T, width), BF16), jax.ShapeDtypeStruct((1, width), F32)],
        grid=(nb,),
        in_specs=[cur(QW), cur(KW), nxt(KW), cur(KW), nxt(KW), cur(LANES), cur(LANES)],
        out_specs=[cur(width), pl.BlockSpec((1, width), lambda n: (0, 0))],
        compiler_params=pltpu.CompilerParams(dimension_semantics=("arbitrary",), vmem_limit_bytes=_vmem(8 << 20)),
    )(dq, dkc, dkp, dvc, dvp, ctab, stab)


def _cast_block(w, l, axis, chip_arr, *, name):
    _, Ks, Ns = w.shape
    tk = _pick(Ks, (512, 352, 256, 128))
    nk = Ks // tk
    full = (Ks * N_CHIPS, Ns) if axis == 0 else (Ks, Ns * N_CHIPS)

    def body(p_ref, w_ref, o_ref):
        o_ref[...] = w_ref[...].astype(BF16)

    if axis == 0:
        out_spec = pl.BlockSpec((tk, Ns), lambda i, p: (p[0] * nk + i, 0))
    else:
        out_spec = pl.BlockSpec((tk, Ns), lambda i, p: (i, p[0]))
    grid_spec = pltpu.PrefetchScalarGridSpec(
        num_scalar_prefetch=1, grid=(nk,),
        in_specs=[pl.BlockSpec((None, tk, Ns), lambda i, p: (l, i, 0))], out_specs=out_spec)
    return pl.pallas_call(
        body, name=name, out_shape=jax.ShapeDtypeStruct(full, BF16), grid_spec=grid_spec,
        compiler_params=pltpu.CompilerParams(dimension_semantics=("arbitrary",),
                                             vmem_limit_bytes=_vmem(4 * tk * Ns * 6)),
    )(chip_arr, w)


def _adamw_math(w, g, m, v):
    m = ADAM_B1 * m + (1.0 - ADAM_B1) * g
    v = ADAM_B2 * v + (1.0 - ADAM_B2) * (g * g)
    m_hat = m / (1.0 - ADAM_B1 ** ADAM_STEP)
    v_hat = v / (1.0 - ADAM_B2 ** ADAM_STEP)
    delta = -ADAM_LR * (m_hat / (jnp.sqrt(v_hat) + ADAM_EPS) + ADAM_WD * w)
    return delta, m, v


def _adamw_layer(w, m, v, g, l, outs, *, name):
    _, K, N = w.shape
    tk = _pick(K, (256, 176, 128))

    def body(w_ref, m_ref, v_ref, g_ref, _g, _d, _m, _v, go_ref, d_ref, mo_ref, vo_ref):
        gv = g_ref[...]
        d, mn, vn = _adamw_math(w_ref[...], gv, m_ref[...], v_ref[...])
        go_ref[...] = gv
        d_ref[...] = d
        mo_ref[...] = mn
        vo_ref[...] = vn

    layer = pl.BlockSpec((None, tk, N), lambda i: (l, i, 0))
    any_spec = pl.BlockSpec(memory_space=pl.ANY)
    sd = jax.ShapeDtypeStruct(w.shape, F32)
    return pl.pallas_call(
        body, name=name, out_shape=[sd, sd, sd, sd], grid=(K // tk,),
        in_specs=[layer, layer, layer, pl.BlockSpec((tk, N), lambda i: (i, 0))] + [any_spec] * 4,
        out_specs=[layer] * 4, input_output_aliases={4: 0, 5: 1, 6: 2, 7: 3},
        compiler_params=pltpu.CompilerParams(dimension_semantics=("arbitrary",),
                                             vmem_limit_bytes=_vmem(2 * 8 * tk * N * 4 + 6 * tk * N * 4)),
    )(w, m, v, g, *outs)


def _adamw_small(w, g, m, v, *, name):
    def body(w_ref, g_ref, m_ref, v_ref, d_ref, mo_ref, vo_ref):
        d, mn, vn = _adamw_math(w_ref[...], g_ref[...], m_ref[...], v_ref[...])
        d_ref[...] = d
        mo_ref[...] = mn
        vo_ref[...] = vn

    sd = jax.ShapeDtypeStruct(w.shape, F32)
    return pl.pallas_call(body, name=name, out_shape=[sd, sd, sd])(w, g, m, v)


def _my_place():
    return lax.axis_index("x"), lax.axis_index("y"), lax.axis_index("c")


def _peer_chips(x, y):
    return [(1 - x, y), (x, 1 - y), (1 - x, 1 - y)]


_HBM = pl.BlockSpec(memory_space=pltpu.HBM)
_SEM = pl.BlockSpec(memory_space=pltpu.SEMAPHORE)
_EFFECT = pltpu.SideEffectType.DATAFLOW_SIDE_EFFECTING


def _split_start(name, bufs, n_copies, make_copies, after):
    nb = len(bufs)

    def body(*refs):
        send_sems, recv_sems = refs[nb + 1], refs[nb + 2]
        token = refs[2 * nb + 3]
        sends, _ = make_copies(refs[:nb], send_sems, recv_sems)
        for cp in sends:
            cp.start()
        token[...] = jnp.zeros_like(token)

    res = pl.pallas_call(
        body, name=name,
        out_shape=(pltpu.SemaphoreType.DMA((n_copies,)), pltpu.SemaphoreType.DMA((n_copies,)),
                   *[pltpu.HBM(b.shape, b.dtype) for b in bufs], jax.ShapeDtypeStruct((8, LANES), F32)),
        in_specs=[_HBM] * nb + [pl.BlockSpec(memory_space=pl.ANY)],
        out_specs=(_SEM, _SEM, *[_HBM] * nb, pl.BlockSpec(memory_space=pltpu.VMEM)),
        input_output_aliases={k: 2 + k for k in range(nb)},
        compiler_params=pltpu.CompilerParams(has_side_effects=_EFFECT),
    )(*[pltpu.with_memory_space_constraint(b, pltpu.HBM) for b in bufs],
      after[0] if isinstance(after, (list, tuple)) else after)
    return res[0], res[1], list(res[2:2 + nb]), res[2 + nb]


def _split_wait(name, bufs, sems, make_copies, after):
    nb = len(bufs)
    after = list(after) if isinstance(after, (list, tuple)) else [after]

    def body(*refs):
        send_sems, recv_sems = refs[nb], refs[nb + 1]
        sends, recvs = make_copies(refs[:nb], send_sems, recv_sems)
        for cp in sends:
            cp.wait_send()
        for cp in recvs:
            cp.wait_recv()

    res = pl.pallas_call(
        body, name=name,
        out_shape=tuple(pltpu.HBM(b.shape, b.dtype) for b in bufs),
        in_specs=[_HBM] * nb + [_SEM, _SEM] + [pl.BlockSpec(memory_space=pl.ANY)] * len(after),
        out_specs=tuple([_HBM] * nb),
        input_output_aliases={k: k for k in range(nb)},
        compiler_params=pltpu.CompilerParams(has_side_effects=_EFFECT),
    )(*bufs, sems[0], sems[1], *after)
    return list(res)


def _remote(src, dst, send_sems, recv_sems, k, target):
    return pltpu.make_async_remote_copy(src_ref=src, dst_ref=dst, send_sem=send_sems.at[k],
                                        recv_sem=recv_sems.at[k], device_id=target, device_id_type=MESH)


def _ag_region(ref, axis, chip, half):
    K, N = ref.shape
    if axis == 0:
        hs = K // N_CHIPS // 2
        assert hs % 16 == 0
        return ref.at[pl.ds(pl.multiple_of((2 * chip + half) * hs, 16), hs), :]
    ns, hk = N // N_CHIPS, K // 2
    assert ns % LANES == 0 and hk % 16 == 0
    return ref.at[pl.ds(pl.multiple_of(half * hk, 16), hk), pl.ds(pl.multiple_of(chip * ns, LANES), ns)]


def _ag_copies(stage, axes):
    n = len(axes)

    def make(bufs, send_sems, recv_sems):
        x, y, c = _my_place()
        me = 2 * x + y
        sends, recvs = [], []
        for j, (px, py) in enumerate(_peer_chips(x, y)):
            other = 2 * px + py
            for w in range(n):
                k = j * n + w
                if stage == 1:
                    src, target = _ag_region(bufs[w], axes[w], me, c), (px, py, c)
                    land = _ag_region(bufs[w], axes[w], other, c)
                else:
                    src, target = _ag_region(bufs[w], axes[w], other, c), (x, y, 1 - c)
                    land = _ag_region(bufs[w], axes[w], other, 1 - c)
                sends.append(_remote(src, src, send_sems, recv_sems, k, target))
                recvs.append(_remote(land, land, send_sems, recv_sems, k, target))
        return sends, recvs

    return make


def _half_shape(shape, axis):
    K, N = shape
    return (K, N // 2) if axis == 0 else (K // 2, N)


def _core_half(ref, axis, half):
    K, N = ref.shape
    if axis == 0:
        return ref.at[:, pl.ds(pl.multiple_of(half * (N // 2), LANES), N // 2)]
    return ref.at[pl.ds(pl.multiple_of(half * (K // 2), 16), K // 2), :]


def _chip_block(ref, axis, chip):
    K, N = ref.shape
    if axis == 0:
        return ref.at[pl.ds(pl.multiple_of(chip * (K // N_CHIPS), 16), K // N_CHIPS), :]
    return ref.at[:, pl.ds(pl.multiple_of(chip * (N // N_CHIPS), LANES), N // N_CHIPS)]


def _rs_sibling_copies(axes):
    n = len(axes)

    def make(bufs, send_sems, recv_sems):
        x, y, c = _my_place()
        sends = [_remote(_core_half(bufs[w], axes[w], 1 - c), bufs[n + w], send_sems, recv_sems, w, (x, y, 1 - c))
                 for w in range(n)]
        recvs = [_remote(bufs[n + w], bufs[n + w], send_sems, recv_sems, w, (x, y, 1 - c)) for w in range(n)]
        return sends, recvs

    return make


def _rs_chip_copies(axes):
    n = len(axes)

    def make(bufs, send_sems, recv_sems):
        x, y, c = _my_place()
        sends, recvs = [], []
        for j, (px, py) in enumerate(_peer_chips(x, y)):
            for w in range(n):
                k = j * n + w
                sends.append(_remote(_chip_block(bufs[w], axes[w], 2 * px + py), bufs[n + w].at[j],
                                     send_sems, recv_sems, k, (px, py, c)))
                recvs.append(_remote(bufs[n + w].at[j], bufs[n + w].at[j], send_sems, recv_sems, k, (px, py, c)))
        return sends, recvs

    return make


def _rs_fill_copies(axes):
    n = len(axes)

    def make(bufs, send_sems, recv_sems):
        x, y, c = _my_place()
        sends = [_remote(_core_half(bufs[w], axes[w], c), _core_half(bufs[w], axes[w], c),
                         send_sems, recv_sems, w, (x, y, 1 - c)) for w in range(n)]
        recvs = [_remote(_core_half(bufs[w], axes[w], 1 - c), _core_half(bufs[w], axes[w], 1 - c),
                         send_sems, recv_sems, w, (x, y, 1 - c)) for w in range(n)]
        return sends, recvs

    return make


def _chip_sum(g, r, axis, place, *, name):
    hk, hn = r.shape
    bk, bn = (hk // N_CHIPS, hn) if axis == 0 else (hk, hn // N_CHIPS)
    tk = _pick(bk, (512, 352, 256, 128))
    nk = bk // tk

    def body(p_ref, g_ref, r_ref, b_ref, own_ref):
        s = g_ref[...].astype(F32) + r_ref[...].astype(F32)
        b_ref[...] = s.astype(BF16)

        @pl.when(pl.program_id(1) == p_ref[0])
        def _():
            own_ref[...] = s

    if axis == 0:
        g_spec = pl.BlockSpec((tk, bn), lambda i, j, p: (j * nk + i, p[1]))
        r_spec = pl.BlockSpec((tk, bn), lambda i, j, p: (j * nk + i, 0))
    else:
        g_spec = pl.BlockSpec((tk, bn), lambda i, j, p: (p[1] * nk + i, j))
        r_spec = pl.BlockSpec((tk, bn), lambda i, j, p: (i, j))
    grid_spec = pltpu.PrefetchScalarGridSpec(
        num_scalar_prefetch=1, grid=(nk, N_CHIPS), in_specs=[g_spec, r_spec],
        out_specs=[r_spec, pl.BlockSpec((tk, bn), lambda i, j, p: (i, 0))])
    return pl.pallas_call(
        body, name=name,
        out_shape=[jax.ShapeDtypeStruct(r.shape, BF16), jax.ShapeDtypeStruct((bk, bn), F32)],
        grid_spec=grid_spec,
        compiler_params=pltpu.CompilerParams(dimension_semantics=("arbitrary", "arbitrary"),
                                             vmem_limit_bytes=_vmem(2 * tk * bn * 10 + 3 * tk * bn * 4)),
    )(place, g, r)


def _final_sum(own, recv, axis, place, *, name):
    _, bk, bn = recv.shape
    tk = _pick(bk, (256, 176, 128))
    nk = bk // tk

    def body(p_ref, o_ref, r_ref, out_ref):
        out_ref[...] = ((o_ref[...] + r_ref[0].astype(F32)) + r_ref[1].astype(F32)) + r_ref[2].astype(F32)

    own_spec = pl.BlockSpec((tk, bn), lambda i, p: (i, 0))
    if axis == 0:
        out_shape, out_spec = (bk, 2 * bn), pl.BlockSpec((tk, bn), lambda i, p: (i, p[1]))
    else:
        out_shape, out_spec = (2 * bk, bn), pl.BlockSpec((tk, bn), lambda i, p: (p[1] * nk + i, 0))
    grid_spec = pltpu.PrefetchScalarGridSpec(
        num_scalar_prefetch=1, grid=(nk,),
        in_specs=[own_spec, pl.BlockSpec((3, tk, bn), lambda i, p: (0, i, 0))], out_specs=out_spec)
    return pl.pallas_call(
        body, name=name, out_shape=jax.ShapeDtypeStruct(out_shape, F32), grid_spec=grid_spec,
        compiler_params=pltpu.CompilerParams(dimension_semantics=("arbitrary",),
                                             vmem_limit_bytes=_vmem(2 * tk * bn * 14 + 4 * tk * bn * 4)),
    )(place, own, recv)


def _allreduce_small(p, after=()):
    n_after = len(after)

    def body(*refs):
        p_ref = refs[0]
        o_ref, r0, r1, r2, send_sems, recv_sems = refs[1 + n_after:]
        x, y, c = _my_place()
        o_ref[...] = p_ref[...]
        for s, (peer, rbuf) in enumerate([((x, y, 1 - c), r0), ((1 - x, y, c), r1), ((x, 1 - y, c), r2)]):
            cp = pltpu.make_async_remote_copy(src_ref=o_ref, dst_ref=rbuf, send_sem=send_sems.at[s],
                                              recv_sem=recv_sems.at[s], device_id=peer, device_id_type=MESH)
            cp.start()
            cp.wait()
            o_ref[...] = o_ref[...] + rbuf[...]

    vm = pl.BlockSpec(memory_space=pltpu.VMEM)
    return pl.pallas_call(
        body, name="allreduce_small", out_shape=jax.ShapeDtypeStruct(p.shape, F32),
        in_specs=[vm] + [pl.BlockSpec(memory_space=pl.ANY)] * n_after, out_specs=vm,
        scratch_shapes=[pltpu.VMEM(p.shape, F32)] * 3 + [pltpu.SemaphoreType.DMA((3,))] * 2,
        compiler_params=pltpu.CompilerParams(vmem_limit_bytes=_vmem(6 * _nbytes(p.shape, F32))),
    )(p, *after)


def _pack_rows(parts):
    rows, metas = [], []
    for a in parts:
        flat = a.reshape(-1)
        nrow = -(-flat.shape[0] // LANES)
        nrow = -(-nrow // 8) * 8
        flat = jnp.pad(flat, (0, nrow * LANES - flat.shape[0]))
        rows.append(flat.reshape(nrow, LANES))
        metas.append((a.shape, nrow))
    return jnp.concatenate(rows, axis=0), metas


def _unpack_rows(packed, metas):
    out, r0 = [], 0
    for shape, nrow in metas:
        size = int(np.prod(shape))
        out.append(packed[r0:r0 + nrow].reshape(-1)[:size].reshape(shape))
        r0 += nrow
    return out


def kernel(x, positions, pre_mix_g, post_mix_g, pre_ffn_g, post_ffn_g, a_w_in, a_b_in, a_ln_g, a_ln_b, a_w_s, a_b_s, a_w_out, b_w_qkv, b_b_qkv, b_sinks, b_w_o, ffn_w_gu, ffn_w_down, loss_target, m_pre_mix_g, m_post_mix_g, m_pre_ffn_g, m_post_ffn_g, m_a_w_in, m_a_b_in, m_a_ln_g, m_a_ln_b, m_a_w_s, m_a_b_s, m_a_w_out, m_b_w_qkv, m_b_b_qkv, m_b_sinks, m_b_w_o, m_ffn_w_gu, m_ffn_w_down, v_pre_mix_g, v_post_mix_g, v_pre_ffn_g, v_post_ffn_g, v_a_w_in, v_a_b_in, v_a_ln_g, v_a_ln_b, v_a_w_s, v_a_b_s, v_a_w_out, v_b_w_qkv, v_b_b_qkv, v_b_sinks, v_b_w_o, v_ffn_w_gu, v_ffn_w_down):
    depth, D = pre_mix_g.shape
    xi, yi, ci = _my_place()
    chip = 2 * xi + yi
    place = jnp.stack([chip, ci]).astype(jnp.int32)

    stacked = {"a_w_in": (a_w_in, m_a_w_in, v_a_w_in), "a_w_out": (a_w_out, m_a_w_out, v_a_w_out),
               "b_w_qkv": (b_w_qkv, m_b_w_qkv, v_b_w_qkv), "b_w_o": (b_w_o, m_b_w_o, v_b_w_o),
               "ffn_w_gu": (ffn_w_gu, m_ffn_w_gu, v_ffn_w_gu), "ffn_w_down": (ffn_w_down, m_ffn_w_down, v_ffn_w_down)}
    cut = {"a_w_in": 1, "a_w_out": 0, "b_w_qkv": 1, "b_w_o": 0, "ffn_w_gu": 1, "ffn_w_down": 0}

    def layer_keys(i):
        mix = [("a_w_in", i // 2), ("a_w_out", i // 2)] if i % 2 == 0 else [("b_w_qkv", i // 2), ("b_w_o", i // 2)]
        return mix + [("ffn_w_gu", i), ("ffn_w_down", i)]

    def dep(a, toks):
        for t in toks:
            a = a + t[:1, :1]
        return a

    W = {}
    for i in range(depth):
        for nm, l in layer_keys(i):
            W[(nm, l)] = _cast_block(stacked[nm][0], l, cut[nm], place, name=f"cast_{nm}_{l}")

    def gather(tag, keys, after):
        axes = [cut[nm] for nm, _ in keys]
        for stage in (1, 2):
            ss, rs, bufs, tok = _split_start(f"ag{stage}_start_{tag}", [W[k] for k in keys], 3 * len(keys),
                                             _ag_copies(stage, axes), after)
            after = yield tok
            bufs = _split_wait(f"ag{stage}_wait_{tag}", bufs, (ss, rs), _ag_copies(stage, axes), after)
            W.update(zip(keys, bufs))
        yield None

    nq = b_b_qkv.shape[1]
    bq_full = jnp.zeros((b_b_qkv.shape[0], N_CHIPS * nq), F32)
    bq_full = lax.dynamic_update_slice(bq_full, jnp.where(ci == 0, b_b_qkv, 0.0), (0, chip * nq))
    bq_packed, bq_meta = _pack_rows([bq_full])
    bq_gathered = _allreduce_small(bq_packed)
    b_qkv_full = _unpack_rows(bq_gathered, bq_meta)[0]

    first = gather("0m", layer_keys(0)[:2], bq_gathered)
    tok = next(first)
    tok = first.send([tok] + [W[k] for i in range(depth) for k in layer_keys(i)[2 if i == 0 else 0:]])
    first.send(tok)

    h = x[0]
    target = loss_target[0]
    ctab, stab = _rope_tables(positions[0])
    q_width = W[("b_w_o", 0)].shape[0]
    kv_width = N_KV_HEADS * HEAD_DIM
    row = lambda a, i: a[i:i + 1]
    gains = {"pre_mix": pre_mix_g[:, None], "post_mix": post_mix_g[:, None], "pre_ffn": pre_ffn_g[:, None],
             "post_ffn": post_ffn_g[:, None]}
    gain = lambda which, i: (gains[which], i)

    saved = []
    hn = None
    for i in range(depth):
        j = i // 2
        s = {"h": h}
        ffn_w = None
        if i == 0:
            ffn_w = gather("0f", layer_keys(0)[2:], W[("a_w_out", 0)])
            toks = [next(ffn_w)]
            nxt = gather("1", layer_keys(1), toks[0])
            toks.append(next(nxt))
            hn = _rms_fwd(h, gain("pre_mix", i), out_dtype=BF16, after=toks, name=f"rms_pre_mix_{i}")
        elif i + 1 < depth:
            nxt = gather(str(i + 1), layer_keys(i + 1), h)
            toks = [next(nxt)]
        else:
            toks = []
        s["hn"] = hn
        if i % 2 == 0:
            pre = _matmul(hn, W[("a_w_in", j)], mode="nn", bias=row(a_b_in, j), out_dtype=F32, after=toks,
                          name=f"gmlp_in_{i}")
            gated = _sgu_fwd(pre, row(a_ln_g, j), row(a_ln_b, j), a_w_s[j], a_b_s[j].T, name=f"sgu_fwd_{i}")
            mix = _matmul(gated, W[("a_w_out", j)], mode="nn", out_dtype=F32, name=f"gmlp_out_{i}")
            s.update(pre=pre, gated=gated)
        else:
            qkv = _matmul(hn, W[("b_w_qkv", j)], mode="nn", bias=row(b_qkv_full, j), out_dtype=F32, after=toks,
                          name=f"attn_qkv_{i}")
            qr, kr, vr = _rope_fwd(qkv, ctab, stab, q_width=q_width, kv_width=kv_width, name=f"rope_fwd_{i}")
            o = _attn_fwd(qr, kr, vr, row(b_sinks, j), name=f"attn_fwd_{i}")
            mix = _matmul(o, W[("b_w_o", j)], mode="nn", out_dtype=F32, name=f"attn_o_{i}")
            s.update(qr=qr, kr=kr, vr=vr, o=o)
        s["mix"] = mix
        toks = [ffn_w.send(mix)] if ffn_w else []
        h1, fn = _rms_res_norm(h, mix, gain("post_mix", i), gain("pre_ffn", i), after=toks, name=f"rms_post_mix_{i}")
        if ffn_w:
            ffn_w.send(h1)
        s["h1"] = h1
        g_pre, u_pre, act = _ffn_up(fn, W[("ffn_w_gu", i)][None], 0, name=f"ffn_up_{i}")
        f = _matmul(act, W[("ffn_w_down", i)], mode="nn", out_dtype=F32, name=f"ffn_down_{i}")
        if i + 1 < depth:
            toks = [nxt.send(f)]
            h, hn = _rms_res_norm(h1, f, gain("post_ffn", i), gain("pre_mix", i + 1), after=toks,
                                  name=f"rms_post_ffn_{i}")
            nxt.send(h)
        else:
            h = _rms_res(h1, f, gain("post_ffn", i), name=f"rms_post_ffn_{i}")
        s.update(fn=fn, g_pre=g_pre, u_pre=u_pre, act=act, f=f)
        saved.append(s)

    dh, df, loss_part, g_last = _loss_and_grad(h, target, saved[-1]["f"], gain("post_ffn", depth - 1), name="loss")

    big_out = {nm: tuple(lax.empty(w.shape, F32) for _ in range(4)) for nm, (w, _, _) in stacked.items()}

    def reduce_group(i, keys, grads):
        axes = [cut[nm] for nm, _ in keys]
        n = len(keys)
        lands = [lax.empty(_half_shape(g.shape, ax), BF16) for g, ax in zip(grads, axes)]
        ss, rs, bufs, tok = _split_start(f"rs_sibling_start_{i}", list(grads) + lands, n, _rs_sibling_copies(axes),
                                         place)
        after = yield tok
        bufs = _split_wait(f"rs_sibling_wait_{i}", bufs, (ss, rs), _rs_sibling_copies(axes), after)
        sums = [_chip_sum(bufs[w], bufs[n + w], axes[w], place, name=f"chip_sum_{keys[w][0]}_{keys[w][1]}")
                for w in range(n)]
        lands = [lax.empty((3,) + own.shape, BF16) for _, own in sums]
        ss, rs, bufs, tok = _split_start(f"rs_chip_start_{i}", [sb for sb, _ in sums] + lands, 3 * n,
                                         _rs_chip_copies(axes), place)
        after = yield tok
        bufs = _split_wait(f"rs_chip_wait_{i}", bufs, (ss, rs), _rs_chip_copies(axes), after)
        blocks = [_final_sum(sums[w][1], bufs[n + w], axes[w], place, name=f"final_sum_{keys[w][0]}_{keys[w][1]}")
                  for w in range(n)]
        ss, rs, bufs, tok = _split_start(f"rs_fill_start_{i}", blocks, n, _rs_fill_copies(axes), place)
        after = yield tok
        blocks = _split_wait(f"rs_fill_wait_{i}", bufs, (ss, rs), _rs_fill_copies(axes), after)
        for (nm, l), g in zip(keys, blocks):
            w, m, v = stacked[nm]
            big_out[nm] = tuple(_adamw_layer(w, m, v, g, l, big_out[nm], name=f"adamw_{nm}_{l}"))
        yield None

    reducing = []

    def advance(after, newest_only=False):
        toks = []
        for gen in (reducing[-1:] if newest_only else list(reducing)):
            tok = gen.send(after)
            if tok is None:
                reducing.remove(gen)
            else:
                toks.append(tok)
        return toks

    small = {}
    g_pre_mix, g_post_mix, g_pre_ffn, g_post_ffn = [None] * depth, [None] * depth, [None] * depth, [None] * depth
    g_post_ffn[depth - 1] = g_last
    toks = []
    early = []
    for i in reversed(range(depth)):
        j = i // 2
        s = saved[i]
        g_down = _matmul(s["act"], df, mode="tn", out_dtype=BF16, after=toks, name=f"ffn_down_dw_{i}")
        dg_, du_ = _ffn_down_dx(df, W[("ffn_w_down", i)][None], 0, s["g_pre"], s["u_pre"], g_down,
                                name=f"ffn_down_dx_{i}")
        hid = dg_.shape[1]
        tile = _pick(hid, (1408, 768, 512, 256, 128))
        w_gu = W[("ffn_w_gu", i)]
        g_gu = lax.empty(w_gu.shape, BF16)
        g_gu = _matmul(s["fn"], dg_, mode="tn", into=g_gu, tq=tile, out_dtype=BF16, name=f"ffn_g_dw_{i}")
        g_gu = _matmul(s["fn"], du_, mode="tn", into=g_gu, tq=tile, q_off=hid // tile, out_dtype=BF16,
                       name=f"ffn_u_dw_{i}")
        dfn_g = _matmul(dg_, w_gu, mode="nt", tr=hid, out_dtype=F32, after=[g_gu], name=f"ffn_g_dx_{i}")
        dfn = _matmul(du_, w_gu, mode="nt", tr=hid, b_r_off=1, bias=dfn_g, out_dtype=F32, name=f"ffn_u_dx_{i}")
        toks = advance(dfn)
        if i == 0:
            gen = reduce_group("0f", layer_keys(0)[2:], [g_gu, g_down])
            toks.append(next(gen))
            reducing.append(gen)
        dh1, dmix, g_pre_ffn[i], g_post_mix[i] = _rms_bwd_chain(
            s["h1"], gain("pre_ffn", i), dfn, dh, s["mix"], gain("post_mix", i), after=toks,
            name=f"rms_ffn_mix_bwd_{i}")
        if i % 2 == 0:
            g_out = _matmul(s["gated"], dmix, mode="tn", out_dtype=BF16, name=f"gmlp_out_dw_{i}")
            dgated = _matmul(dmix, W[("a_w_out", j)], mode="nt", out_dtype=BF16, after=[g_out],
                             name=f"gmlp_out_dx_{i}")
            toks = advance(dgated, newest_only=True) if i == 0 else []
            dpre, dws, dbsT, dlng, dlnb, dbin = _sgu_bwd(s["pre"], dgated, dep(row(a_ln_g, j), toks), row(a_ln_b, j),
                                                         a_w_s[j], a_b_s[j].T, name=f"sgu_bwd_{i}")
            small[("a_w_s", j)] = dws
            small[("a_b_s", j)] = dbsT.T
            small[("a_ln_g", j)] = dlng
            small[("a_ln_b", j)] = dlnb
            small[("a_b_in", j)] = dbin
            g_in = _matmul(s["hn"], dpre, mode="tn", out_dtype=BF16, name=f"gmlp_in_dw_{i}")
            if i == 0:
                last = reduce_group("0m", layer_keys(0)[:2], [g_in, g_out])
                early = [next(last)]
            dhn = _matmul(dpre, W[("a_w_in", j)], mode="nt", out_dtype=F32, after=[g_in] + early,
                          name=f"gmlp_in_dx_{i}")
        else:
            g_out = _matmul(s["o"], dmix, mode="tn", out_dtype=BF16, name=f"attn_o_dw_{i}")
            do = _matmul(dmix, W[("b_w_o", j)], mode="nt", out_dtype=BF16, after=[g_out], name=f"attn_o_dx_{i}")
            dq, dkp, dkc, dvp, dvc, dsk = _attn_bwd(s["qr"], s["kr"], s["vr"], row(b_sinks, j), do,
                                                    name=f"attn_bwd_{i}")
            dqkv, dbq = _rope_bwd(dq, dkp, dkc, dvp, dvc, ctab, stab, name=f"rope_bwd_{i}")
            small[("b_sinks", j)] = dsk[:, :b_sinks.shape[1]]
            small[("b_b_qkv", j)] = dbq
            g_in = _matmul(s["hn"], dqkv, mode="tn", out_dtype=BF16, name=f"attn_qkv_dw_{i}")
            if i == 0:
                last = reduce_group("0m", layer_keys(0)[:2], [g_in, g_out])
                early = [next(last)]
            dhn = _matmul(dqkv, W[("b_w_qkv", j)], mode="nt", out_dtype=F32, after=[g_in] + early,
                          name=f"attn_qkv_dx_{i}")
        toks = advance(dhn)
        if i > 0:
            dh, df, g_pre_mix[i], g_post_ffn[i - 1] = _rms_bwd_chain(
                s["h"], gain("pre_mix", i), dhn, dh1, saved[i - 1]["f"], gain("post_ffn", i - 1), after=toks,
                name=f"rms_mix_ffn_bwd_{i}")
            gen = reduce_group(str(i), layer_keys(i), [g_in, g_out, g_gu, g_down])
            toks = [next(gen)] + advance(dh)
            reducing.append(gen)
        else:
            toks.append(last.send(dhn))
            dh, g_pre_mix[i] = _rms_bwd(s["h"], gain("pre_mix", i), dhn, dh1, out_dtype=F32, after=toks,
                                        name=f"rms_pre_mix_bwd_{i}")
            advance(dh)
    grad_x = dh[None]
    assert not reducing

    ready = [big_out[nm][1] for nm in big_out]
    n_a, n_b = a_b_in.shape[0], b_sinks.shape[0]
    stack = lambda key, n: jnp.concatenate([small[(key, j)] for j in range(n)], axis=0)
    small_parts = [
        jnp.concatenate(g_pre_mix, axis=0), jnp.concatenate(g_post_mix, axis=0),
        jnp.concatenate(g_pre_ffn, axis=0), jnp.concatenate(g_post_ffn, axis=0),
        stack("a_b_in", n_a), stack("a_ln_g", n_a), stack("a_ln_b", n_a),
        jnp.stack([small[("a_w_s", j)] for j in range(n_a)]), jnp.stack([small[("a_b_s", j)] for j in range(n_a)]),
        stack("b_b_qkv", n_b), stack("b_sinks", n_b), loss_part,
    ]
    packed, metas = _pack_rows(small_parts)
    reduced = _allreduce_small(packed, after=ready + [dh])
    while last.send(reduced) is not None:
        pass
    red = _unpack_rows(reduced, metas)
    (gr_pre_mix, gr_post_mix, gr_pre_ffn, gr_post_ffn, gr_b_in, gr_ln_g, gr_ln_b, gr_w_s, gr_b_s,
     gr_b_qkv_full, gr_sinks, loss_sum) = red
    loss = loss_sum[0, 0]
    gr_b_qkv = lax.dynamic_slice(gr_b_qkv_full, (0, chip * nq), (gr_b_qkv_full.shape[0], nq))

    grads = {"pre_mix_g": gr_pre_mix, "post_mix_g": gr_post_mix, "pre_ffn_g": gr_pre_ffn, "post_ffn_g": gr_post_ffn,
             "a_b_in": gr_b_in, "a_ln_g": gr_ln_g, "a_ln_b": gr_ln_b, "a_w_s": gr_w_s, "a_b_s": gr_b_s,
             "b_b_qkv": gr_b_qkv, "b_sinks": gr_sinks}
    weights = {"pre_mix_g": (pre_mix_g, m_pre_mix_g, v_pre_mix_g), "post_mix_g": (post_mix_g, m_post_mix_g, v_post_mix_g),
               "pre_ffn_g": (pre_ffn_g, m_pre_ffn_g, v_pre_ffn_g), "post_ffn_g": (post_ffn_g, m_post_ffn_g, v_post_ffn_g),
               "a_b_in": (a_b_in, m_a_b_in, v_a_b_in), "a_ln_g": (a_ln_g, m_a_ln_g, v_a_ln_g),
               "a_ln_b": (a_ln_b, m_a_ln_b, v_a_ln_b), "a_w_s": (a_w_s, m_a_w_s, v_a_w_s), "a_b_s": (a_b_s, m_a_b_s, v_a_b_s),
               "b_b_qkv": (b_b_qkv, m_b_b_qkv, v_b_b_qkv), "b_sinks": (b_sinks, m_b_sinks, v_b_sinks)}
    order = ["pre_mix_g", "post_mix_g", "pre_ffn_g", "post_ffn_g", "a_w_in", "a_b_in", "a_ln_g", "a_ln_b", "a_w_s",
             "a_b_s", "a_w_out", "b_w_qkv", "b_b_qkv", "b_sinks", "b_w_o", "ffn_w_gu", "ffn_w_down"]
    deltas, new_m, new_v = {}, {}, {}
    for nm in order:
        if nm in big_out:
            grads[nm], deltas[nm], new_m[nm], new_v[nm] = big_out[nm]
        else:
            w, m, v = weights[nm]
            deltas[nm], new_m[nm], new_v[nm] = _adamw_small(w, grads[nm], m, v, name="adamw_" + nm)
    return (loss, grad_x, *[grads[nm] for nm in order], *[deltas[nm] for nm in order],
            *[new_m[nm] for nm in order], *[new_v[nm] for nm in order])
```

```python
import functools
import math

import jax
import jax.numpy as jnp
import numpy as np
from jax import lax
from jax.experimental import pallas as pl
from jax.experimental.pallas import tpu as pltpu

F32 = jnp.float32
BF16 = jnp.bfloat16
MESH = pl.DeviceIdType.MESH

HEAD_DIM = 64
N_KV_HEADS = 4
ROPE_DIM = 16
ROPE_THETA = 500000.0
CHUNK = 128
GMLP_GROUPS = 8
RMS_EPS = 1e-6
LN_EPS = 1e-5
NEG_INF = -1e30
ADAM_LR = 0.001
ADAM_B1 = 0.9
ADAM_B2 = 0.999
ADAM_EPS = 1e-08
ADAM_WD = 0.01
ADAM_STEP = 10

N_CHIPS = 4
LANES = 128
VMEM_CAP = 58 * 1024 * 1024


def _vmem(est_bytes):
    assert est_bytes < VMEM_CAP
    return VMEM_CAP


def _pick(n, cands):
    for c in cands:
        if c <= n and n % c == 0:
            return c
    return n


def _nbytes(shape, dtype):
    return int(np.prod(shape)) * jnp.dtype(dtype).itemsize


MATMUL_VMEM_BUDGET = 48 * 1024 * 1024


def _halvings(n, unit):
    out, t = [], n
    while t % unit == 0 and t >= unit:
        out.append(t)
        if t % 2:
            break
        t //= 2
    return out


def _matmul_tiles(P, Q, R, a_bytes, b_bytes, o_bytes, full_addend, tp, tq, tr):
    step_us, bytes_per_us = 0.85, 3.2e6
    best = None
    for p in ([tp] if tp else _halvings(P, LANES)):
        for q in ([tq] if tq else _halvings(Q, LANES)):
            for r in ([tr] if tr else _halvings(R, LANES)):
                nk = R // r
                vm = 2 * (p * r * a_bytes + r * q * b_bytes + p * q * o_bytes + (p * q * 4 if full_addend else 0))
                vm += p * q * 4 * (2 if nk > 1 else 1)
                if vm > MATMUL_VMEM_BUDGET:
                    continue
                exposed = (p * r * a_bytes + r * q * b_bytes + p * q * o_bytes) / bytes_per_us
                key = ((P // p) * (Q // q) * nk * step_us + exposed, nk, abs(p - q))
                if best is None or key < best[0]:
                    best = (key, (p, q, r))
    assert best is not None, (P, Q, R)
    return best[1]


def _matmul(a, b, *, mode, out_dtype, name, a_l=None, b_l=None, bias=None, into=None, o_l=None,
            q_off=0, b_r_off=0, tp=None, tq=None, tr=None, after=()):
    a2 = a.shape[-2:]
    b2 = b.shape[-2:]
    if mode == "nn":
        (P, R), (R2, Q) = a2, b2
    elif mode == "nt":
        (P, R), (Q, R2) = a2, b2
    else:
        (R, P), (R2, Q) = a2, b2
    assert R == R2 or (mode == "nt" and R2 % R == 0), (mode, a.shape, b.shape)
    o_bytes = jnp.dtype(into.dtype if into is not None else out_dtype).itemsize
    full_addend = bias is not None and bias.shape[0] != 1
    tp, tq, tr = _matmul_tiles(P, Q, R, a.dtype.itemsize, b.dtype.itemsize, o_bytes, full_addend, tp, tq, tr)
    assert P % tp == 0 and Q % tq == 0 and R % tr == 0
    nk = R // tr
    dims = {"nn": (((1,), (0,)), ((), ())), "nt": (((1,), (1,)), ((), ())), "tn": (((0,), (0,)), ((), ()))}[mode]

    def lead(l, blk, idx):
        if l is None:
            return pl.BlockSpec(blk, idx)
        return pl.BlockSpec((None,) + blk, lambda i, j, k: (l,) + idx(i, j, k))

    if mode == "nn":
        a_spec = lead(a_l, (tp, tr), lambda i, j, k: (i, k))
        b_spec = lead(b_l, (tr, tq), lambda i, j, k: (k, j))
    elif mode == "nt":
        a_spec = lead(a_l, (tp, tr), lambda i, j, k: (i, k))
        b_spec = lead(b_l, (tq, tr), lambda i, j, k: (j, k + b_r_off))
    else:
        a_spec = lead(a_l, (tr, tp), lambda i, j, k: (k, i))
        b_spec = lead(b_l, (tr, tq), lambda i, j, k: (k, j))
    in_specs = [a_spec, b_spec]
    args = [a, b]
    if bias is not None:
        if bias.shape[0] == 1:
            in_specs.append(pl.BlockSpec((1, tq), lambda i, j, k: (0, j)))
        else:
            in_specs.append(pl.BlockSpec((tp, tq), lambda i, j, k: (i, j)))
        args.append(bias)
    aliases = {}
    if into is not None:
        in_specs.append(pl.BlockSpec(memory_space=pl.ANY))
        args.append(into)
        aliases = {len(args) - 1: 0}
        out_shape = jax.ShapeDtypeStruct(into.shape, into.dtype)
        out_dtype = into.dtype
        if o_l is None:
            out_spec = pl.BlockSpec((tp, tq), lambda i, j, k: (i, j + q_off))
        else:
            out_spec = pl.BlockSpec((None, tp, tq), lambda i, j, k: (o_l, i, j + q_off))
    else:
        out_shape = jax.ShapeDtypeStruct((P, Q), out_dtype)
        out_spec = pl.BlockSpec((tp, tq), lambda i, j, k: (i, j))
    n_in = len(args) + len(after)
    in_specs += [pl.BlockSpec(memory_space=pl.ANY)] * len(after)
    args += list(after)
    has_bias = bias is not None
    has_into = into is not None

    def body(*refs):
        a_ref, b_ref = refs[0], refs[1]
        pos = 2
        bias_ref = None
        if has_bias:
            bias_ref = refs[pos]
            pos += 1
        o_ref = refs[n_in]
        acc_ref = refs[n_in + 1] if nk > 1 else None
        part = lax.dot_general(a_ref[...], b_ref[...], dims, preferred_element_type=F32)

        def finish(acc):
            if has_bias:
                acc = acc + bias_ref[...]
            o_ref[...] = acc.astype(out_dtype)

        if nk == 1:
            finish(part)
        else:
            k = pl.program_id(2)

            @pl.when(k == 0)
            def _():
                acc_ref[...] = part

            @pl.when(k > 0)
            def _():
                acc_ref[...] += part

            @pl.when(k == nk - 1)
            def _():
                finish(acc_ref[...])

    est = 2 * (_nbytes((tp, tr), a.dtype) + _nbytes((tr, tq), b.dtype) + _nbytes((tp, tq), out_dtype)) + 3 * tp * tq * 4
    return pl.pallas_call(
        body, name=name, out_shape=out_shape,
        grid=(P // tp, Q // tq, nk),
        in_specs=in_specs, out_specs=out_spec,
        scratch_shapes=[pltpu.VMEM((tp, tq), F32)] if nk > 1 else [],
        input_output_aliases=aliases,
        compiler_params=pltpu.CompilerParams(
            dimension_semantics=("parallel", "parallel", "arbitrary"), vmem_limit_bytes=_vmem(est)),
    )(*args)


def _matmul_pair(a, b, pair, *, mode, out_dtype, name, after=()):
    if mode == "tn":
        (R, P), (R2, Q) = a.shape, b.shape
        assert R == R2 and pair.shape == b.shape
        tp, tq, tr = _matmul_tiles(P, Q, R, a.dtype.itemsize, 2 * b.dtype.itemsize,
                                   jnp.dtype(out_dtype).itemsize, False, None, None, None)
        nq, nk = Q // tq, R // tr
        grid, nk_total = (P // tp, 2 * nq, nk), nk
        a_spec = pl.BlockSpec((tr, tp), lambda i, j, k: (k, i))
        b_spec = pl.BlockSpec((tr, tq), lambda i, j, k: (jnp.where(j < nq, k, nk - 1), jnp.minimum(j, nq - 1)))
        p_spec = pl.BlockSpec((tr, tq), lambda i, j, k: (jnp.where(j >= nq, k, 0), jnp.maximum(j - nq, 0)))
        out_shape = (P, 2 * Q)
        dims = (((0,), (0,)), ((), ()))
    else:
        assert mode == "nt"
        (P, R), (Q, R2) = a.shape, b.shape
        assert R2 == 2 * R and pair.shape == a.shape
        tp, tq, tr = _matmul_tiles(P, Q, R, 2 * a.dtype.itemsize, b.dtype.itemsize,
                                   jnp.dtype(out_dtype).itemsize, False, None, None, None)
        nk = R // tr
        grid, nk_total = (P // tp, Q // tq, 2 * nk), 2 * nk
        a_spec = pl.BlockSpec((tp, tr), lambda i, j, k: (i, jnp.minimum(k, nk - 1)))
        p_spec = pl.BlockSpec((tp, tr), lambda i, j, k: (i, jnp.maximum(k - nk, 0)))
        b_spec = pl.BlockSpec((tq, tr), lambda i, j, k: (j, k))
        out_shape = (P, Q)
        dims = (((1,), (1,)), ((), ()))
    n_after = len(after)

    def body(a_ref, b_ref, p_ref, *rest):
        o_ref = rest[n_after]
        acc_ref = rest[n_after + 1] if nk_total > 1 else None
        j, k = pl.program_id(1), pl.program_id(2)

        def step(l_ref, r_ref):
            part = lax.dot_general(l_ref[...], r_ref[...], dims, preferred_element_type=F32)
            if nk_total == 1:
                o_ref[...] = part.astype(out_dtype)
                return

            @pl.when(k == 0)
            def _():
                acc_ref[...] = part

            @pl.when(k > 0)
            def _():
                acc_ref[...] += part

            @pl.when(k == nk_total - 1)
            def _():
                o_ref[...] = acc_ref[...].astype(out_dtype)

        first = (j < nq) if mode == "tn" else (k < nk)

        @pl.when(first)
        def _():
            step(a_ref, b_ref)

        @pl.when(jnp.logical_not(first))
        def _():
            step(a_ref if mode == "tn" else p_ref, p_ref if mode == "tn" else b_ref)

    n_a, n_b = (1, 2) if mode == "tn" else (2, 1)
    est = (2 * (n_a * _nbytes((tp, tr), a.dtype) + n_b * _nbytes((tr, tq), b.dtype) + _nbytes((tp, tq), out_dtype))
           + 2 * tp * tq * 4)
    return pl.pallas_call(
        body, name=name, out_shape=jax.ShapeDtypeStruct(out_shape, out_dtype), grid=grid,
        in_specs=[a_spec, b_spec, p_spec] + [pl.BlockSpec(memory_space=pl.ANY)] * n_after,
        out_specs=pl.BlockSpec((tp, tq), lambda i, j, k: (i, j)),
        scratch_shapes=[pltpu.VMEM((tp, tq), F32)] if nk_total > 1 else [],
        compiler_params=pltpu.CompilerParams(
            dimension_semantics=("parallel", "parallel", "arbitrary"), vmem_limit_bytes=_vmem(est)),
    )(a, b, pair, *after)


def _row_call(body, ins, outs, *, name, rows, tr, acc_outs=(), est=0, after=()):
    in_specs, args = [], []
    for arr, kind in ins:
        if kind == "row":
            in_specs.append(pl.BlockSpec((tr, arr.shape[1]), lambda i: (i, 0)))
        elif isinstance(arr, tuple):
            arr, layer = arr
            in_specs.append(pl.BlockSpec((None,) + arr.shape[1:], lambda i, layer=layer: (layer, 0, 0)))
        else:
            nd = arr.ndim
            in_specs.append(pl.BlockSpec(arr.shape, lambda i, nd=nd: (0,) * nd))
        args.append(arr)
    n_ins = len(args)
    in_specs += [pl.BlockSpec(memory_space=pl.ANY)] * len(after)
    args += list(after)

    def kernel_fn(*refs):
        body(*refs[:n_ins], *refs[n_ins + len(after):])

    out_shapes = [jax.ShapeDtypeStruct(s, d) for s, d in outs] + [jax.ShapeDtypeStruct(s, d) for s, d in acc_outs]
    out_specs = [pl.BlockSpec((tr, s[1]), lambda i: (i, 0)) for s, _ in outs]
    out_specs += [pl.BlockSpec(s, lambda i, nd=len(s): (0,) * nd) for s, _ in acc_outs]
    res = pl.pallas_call(
        kernel_fn, name=name, out_shape=out_shapes, grid=(rows // tr,), in_specs=in_specs, out_specs=out_specs,
        compiler_params=pltpu.CompilerParams(dimension_semantics=("arbitrary",), vmem_limit_bytes=_vmem(est)),
    )(*args)
    return res


def _rms_fwd(x, g, *, out_dtype, name, after=()):
    T, D = x.shape
    tr = _pick(T, (512, 256, 128))

    def body(x_ref, g_ref, o_ref):
        xv = x_ref[...]
        r = lax.rsqrt(jnp.mean(xv * xv, axis=-1, keepdims=True) + RMS_EPS)
        o_ref[...] = (xv * r * g_ref[...]).astype(out_dtype)

    return _row_call(body, [(x, "row"), (g, "full")], [((T, D), out_dtype)], name=name, rows=T, tr=tr,
                     est=8 * tr * D * 4, after=after)[0]


def _rms_res(h, y, g, *, name):
    T, D = h.shape
    tr = _pick(T, (512, 256, 128))

    def body(h_ref, y_ref, g_ref, o_ref):
        yv = y_ref[...]
        r = lax.rsqrt(jnp.mean(yv * yv, axis=-1, keepdims=True) + RMS_EPS)
        o_ref[...] = h_ref[...] + yv * r * g_ref[...]

    return _row_call(body, [(h, "row"), (y, "row"), (g, "full")], [((T, D), F32)], name=name, rows=T, tr=tr,
                     est=10 * tr * D * 4)[0]


def _rms_bwd(x, g, dy, dres, *, out_dtype, name, after=()):
    T, D = x.shape
    tr = _pick(T, (512, 256, 128))
    has_res = dres is not None

    def body(*refs):
        if has_res:
            x_ref, g_ref, dy_ref, dr_ref, dx_ref, dg_ref = refs
        else:
            x_ref, g_ref, dy_ref, dx_ref, dg_ref = refs
        xv = x_ref[...]
        r = lax.rsqrt(jnp.mean(xv * xv, axis=-1, keepdims=True) + RMS_EPS)
        xhat = xv * r
        dyv = dy_ref[...].astype(F32)
        dxn = dyv * g_ref[...]
        dx = r * (dxn - xhat * jnp.mean(dxn * xhat, axis=-1, keepdims=True))
        if has_res:
            dx = dx + dr_ref[...]
        dx_ref[...] = dx.astype(out_dtype)
        part = jnp.sum(dyv * xhat, axis=0, keepdims=True)

        @pl.when(pl.program_id(0) == 0)
        def _():
            dg_ref[...] = part

        @pl.when(pl.program_id(0) > 0)
        def _():
            dg_ref[...] += part

    ins = [(x, "row"), (g, "full"), (dy, "row")] + ([(dres, "row")] if has_res else [])
    dx, dg = _row_call(body, ins, [((T, D), out_dtype)], name=name, rows=T, tr=tr, acc_outs=[((1, D), F32)],
                       est=12 * tr * D * 4, after=after)
    return dx, dg


def _rms_res_norm(h, y, g_res, g_next, *, name, after=()):
    T, D = h.shape
    tr = _pick(T, (512, 256, 128))

    def body(h_ref, y_ref, g_ref, gn_ref, o_ref, n_ref):
        yv = y_ref[...]
        r = lax.rsqrt(jnp.mean(yv * yv, axis=-1, keepdims=True) + RMS_EPS)
        h2 = h_ref[...] + yv * r * g_ref[...]
        o_ref[...] = h2
        r2 = lax.rsqrt(jnp.mean(h2 * h2, axis=-1, keepdims=True) + RMS_EPS)
        n_ref[...] = (h2 * r2 * gn_ref[...]).astype(BF16)

    return _row_call(body, [(h, "row"), (y, "row"), (g_res, "full"), (g_next, "full")],
                     [((T, D), F32), ((T, D), BF16)], name=name, rows=T, tr=tr, est=12 * tr * D * 4, after=after)


def _rms_bwd_chain(x1, g1, dy1, dres, x2, g2, *, name, after=()):
    T, D = x1.shape
    tr = _pick(T, (512, 256, 128))

    def one(xv, gv, dyv):
        r = lax.rsqrt(jnp.mean(xv * xv, axis=-1, keepdims=True) + RMS_EPS)
        xhat = xv * r
        dxn = dyv * gv
        dx = r * (dxn - xhat * jnp.mean(dxn * xhat, axis=-1, keepdims=True))
        return dx, jnp.sum(dyv * xhat, axis=0, keepdims=True)

    def body(x1_ref, g1_ref, dy1_ref, dr_ref, x2_ref, g2_ref, d1_ref, d2_ref, dg1_ref, dg2_ref):
        dx1, p1 = one(x1_ref[...], g1_ref[...], dy1_ref[...].astype(F32))
        d1 = dx1 + dr_ref[...]
        d1_ref[...] = d1
        dx2, p2 = one(x2_ref[...], g2_ref[...], d1)
        d2_ref[...] = dx2.astype(BF16)

        @pl.when(pl.program_id(0) == 0)
        def _():
            dg1_ref[...] = p1
            dg2_ref[...] = p2

        @pl.when(pl.program_id(0) > 0)
        def _():
            dg1_ref[...] += p1
            dg2_ref[...] += p2

    ins = [(x1, "row"), (g1, "full"), (dy1, "row"), (dres, "row"), (x2, "row"), (g2, "full")]
    return _row_call(body, ins, [((T, D), F32), ((T, D), BF16)], name=name, rows=T, tr=tr,
                     acc_outs=[((1, D), F32), ((1, D), F32)], est=20 * tr * D * 4, after=after)


def _ffn_up(fn, w_gu, l, *, name):
    T, D = fn.shape
    H = w_gu.shape[2] // 2
    tp = _pick(T, (1024, 512, 256, 128))
    tq = _pick(H, (1408, 768, 512, 256, 128))
    nj = H // tq

    def body(a_ref, wg_ref, wu_ref, g_ref, u_ref, act_ref):
        a = a_ref[...]
        g = jnp.dot(a, wg_ref[...], preferred_element_type=F32)
        u = jnp.dot(a, wu_ref[...], preferred_element_type=F32)
        sg = jax.nn.sigmoid(g)
        silu = g * sg
        g_ref[...] = (u * (sg + silu * (1.0 - sg))).astype(BF16)
        u_ref[...] = silu.astype(BF16)
        act_ref[...] = (silu * u).astype(BF16)

    tile = pl.BlockSpec((tp, tq), lambda j, i: (i, j))
    est = 2 * (tp * D * 2 + 2 * D * tq * 2 + 3 * tp * tq * 2) + 4 * tp * tq * 4
    return pl.pallas_call(
        body, name=name,
        out_shape=[jax.ShapeDtypeStruct((T, H), BF16), jax.ShapeDtypeStruct((T, H), BF16),
                   jax.ShapeDtypeStruct((T, H), BF16)],
        grid=(nj, T // tp),
        in_specs=[pl.BlockSpec((tp, D), lambda j, i: (i, 0)),
                  pl.BlockSpec((None, D, tq), lambda j, i: (l, 0, j)),
                  pl.BlockSpec((None, D, tq), lambda j, i: (l, 0, j + nj))],
        out_specs=[tile, tile, tile],
        compiler_params=pltpu.CompilerParams(dimension_semantics=("parallel", "parallel"),
                                             vmem_limit_bytes=_vmem(est)),
    )(fn, w_gu, w_gu)


def _ffn_down_dx(df, w_down, l, g, u, after, *, name):
    T, D = df.shape
    H = w_down.shape[1]
    tp = _pick(T, (1024, 512, 256, 128))
    tq = _pick(H, (1408, 768, 512, 256, 128))

    def body(a_ref, w_ref, g_ref, u_ref, _, dg_ref, du_ref):
        da = lax.dot_general(a_ref[...], w_ref[...], (((1,), (1,)), ((), ())), preferred_element_type=F32)
        dg_ref[...] = (da * g_ref[...].astype(F32)).astype(BF16)
        du_ref[...] = (da * u_ref[...].astype(F32)).astype(BF16)

    tile = pl.BlockSpec((tp, tq), lambda j, i: (i, j))
    est = 2 * (tp * D * 2 + tq * D * 2 + 4 * tp * tq * 2) + 3 * tp * tq * 4
    return pl.pallas_call(
        body, name=name,
        out_shape=[jax.ShapeDtypeStruct((T, H), BF16), jax.ShapeDtypeStruct((T, H), BF16)],
        grid=(H // tq, T // tp),
        in_specs=[pl.BlockSpec((tp, D), lambda j, i: (i, 0)),
                  pl.BlockSpec((None, tq, D), lambda j, i: (l, j, 0)), tile, tile,
                  pl.BlockSpec(memory_space=pl.ANY)],
        out_specs=[tile, tile],
        compiler_params=pltpu.CompilerParams(dimension_semantics=("parallel", "parallel"),
                                             vmem_limit_bytes=_vmem(est)),
    )(df, w_down, g, u, after)


def _loss_and_grad(y, target, x, g, *, name):
    T, D = y.shape
    tr = _pick(T, (512, 256, 128))

    def body(y_ref, t_ref, x_ref, g_ref, dy_ref, dx_ref, l_ref, dg_ref):
        e = y_ref[...] - t_ref[...]
        dy = e * (1.0 / D)
        dy_ref[...] = dy
        part = jnp.sum(jnp.sum(e * e, axis=1, keepdims=True), axis=0, keepdims=True) * (0.5 / D)
        xv = x_ref[...]
        r = lax.rsqrt(jnp.mean(xv * xv, axis=-1, keepdims=True) + RMS_EPS)
        xhat = xv * r
        dxn = dy * g_ref[...]
        dx_ref[...] = (r * (dxn - xhat * jnp.mean(dxn * xhat, axis=-1, keepdims=True))).astype(BF16)
        dg = jnp.sum(dy * xhat, axis=0, keepdims=True)

        @pl.when(pl.program_id(0) == 0)
        def _():
            l_ref[...] = part
            dg_ref[...] = dg

        @pl.when(pl.program_id(0) > 0)
        def _():
            l_ref[...] += part
            dg_ref[...] += dg

    dy, dx, l, dg = _row_call(body, [(y, "row"), (target, "row"), (x, "row"), (g, "full")],
                              [((T, D), F32), ((T, D), BF16)], name=name, rows=T, tr=tr,
                              acc_outs=[((1, 1), F32), ((1, D), F32)], est=14 * tr * D * 4)
    return dy, dx, l, dg


_SQRT_HALF = 0.7071067811865476
_INV_SQRT_2PI = 0.3989422804014327


def _gelu_parts(x):
    cdf = 0.5 * (1.0 + lax.erf(x * _SQRT_HALF))
    return cdf


def _sgu_common(pre, lng, lnb, W):
    cdf = _gelu_parts(pre)
    z = pre * cdf
    u = z[:, :W]
    v = z[:, W:]
    mu = jnp.mean(v, axis=-1, keepdims=True)
    vc = v - mu
    var = jnp.mean(vc * vc, axis=-1, keepdims=True)
    rstd = lax.rsqrt(var + LN_EPS)
    vhat = vc * rstd
    vn = vhat * lng + lnb
    return cdf, u, vhat, rstd, vn


def _causal_mask():
    t = lax.broadcasted_iota(jnp.int32, (CHUNK, CHUNK), 0)
    s = lax.broadcasted_iota(jnp.int32, (CHUNK, CHUNK), 1)
    return t >= s


def _sgu_fwd(pre, lng, lnb, ws, bsT, *, name):
    T, W2 = pre.shape
    W = W2 // 2
    G = ws.shape[0]
    gd = W // G

    def body(pre_ref, lng_ref, lnb_ref, ws_ref, bs_ref, o_ref):
        _, u, _, _, vn = _sgu_common(pre_ref[...], lng_ref[...], lnb_ref[...], W)
        vnb = vn.astype(BF16)
        causal = _causal_mask()
        for g in range(G):
            w = jnp.where(causal, ws_ref[g], 0.0).astype(BF16)
            sv = jnp.dot(w, vnb[:, g * gd:(g + 1) * gd], preferred_element_type=F32) + bs_ref[:, g:g + 1]
            o_ref[:, g * gd:(g + 1) * gd] = (u[:, g * gd:(g + 1) * gd] * sv).astype(BF16)

    return pl.pallas_call(
        body, name=name, out_shape=jax.ShapeDtypeStruct((T, W), BF16), grid=(T // CHUNK,),
        in_specs=[pl.BlockSpec((CHUNK, W2), lambda i: (i, 0)),
                  pl.BlockSpec((1, W), lambda i: (0, 0)), pl.BlockSpec((1, W), lambda i: (0, 0)),
                  pl.BlockSpec(ws.shape, lambda i: (0, 0, 0)), pl.BlockSpec(bsT.shape, lambda i: (0, 0))],
        out_specs=pl.BlockSpec((CHUNK, W), lambda i: (i, 0)),
        compiler_params=pltpu.CompilerParams(dimension_semantics=("arbitrary",),
                                             vmem_limit_bytes=_vmem(12 * CHUNK * W2 * 4)),
    )(pre, lng, lnb, ws, bsT)


def _sgu_bwd(pre, dgated, lng, lnb, ws, bsT, *, name):
    T, W2 = pre.shape
    W = W2 // 2
    G = ws.shape[0]
    gd = W // G

    def body(pre_ref, dgt_ref, lng_ref, lnb_ref, ws_ref, bs_ref,
             dpre_ref, dws_ref, dbs_ref, dlng_ref, dlnb_ref, dbin_ref):
        first = pl.program_id(0) == 0

        @pl.when(first)
        def _():
            dws_ref[...] = jnp.zeros_like(dws_ref)
            dbs_ref[...] = jnp.zeros_like(dbs_ref)
            dlng_ref[...] = jnp.zeros_like(dlng_ref)
            dlnb_ref[...] = jnp.zeros_like(dlnb_ref)
            dbin_ref[...] = jnp.zeros_like(dbin_ref)

        pre_v = pre_ref[...]
        lng_v = lng_ref[...]
        cdf, u, vhat, rstd, vn = _sgu_common(pre_v, lng_v, lnb_ref[...], W)
        vnb = vn.astype(BF16)
        dgt = dgt_ref[...].astype(F32)
        causal = _causal_mask()
        du_parts, dvn_parts = [], []
        for g in range(G):
            sl = slice(g * gd, (g + 1) * gd)
            w = jnp.where(causal, ws_ref[g], 0.0).astype(BF16)
            sv = jnp.dot(w, vnb[:, sl], preferred_element_type=F32) + bs_ref[:, g:g + 1]
            dgt_g = dgt[:, sl]
            du_parts.append(dgt_g * sv)
            dsv = dgt_g * u[:, sl]
            dsvb = dsv.astype(BF16)
            dvn_parts.append(lax.dot_general(w, dsvb, (((0,), (0,)), ((), ())), preferred_element_type=F32))
            dw = lax.dot_general(dsvb, vnb[:, sl], (((1,), (1,)), ((), ())), preferred_element_type=F32)
            dws_ref[g] += jnp.where(causal, dw, 0.0)
            dbs_ref[:, g:g + 1] += jnp.sum(dsv, axis=1, keepdims=True)
        du = jnp.concatenate(du_parts, axis=1)
        dvn = jnp.concatenate(dvn_parts, axis=1)
        dlng_ref[...] += jnp.sum(dvn * vhat, axis=0, keepdims=True)
        dlnb_ref[...] += jnp.sum(dvn, axis=0, keepdims=True)
        dvh = dvn * lng_v
        dv = rstd * (dvh - jnp.mean(dvh, axis=-1, keepdims=True)
                     - vhat * jnp.mean(dvh * vhat, axis=-1, keepdims=True))
        dz = jnp.concatenate([du, dv], axis=1)
        dgelu = cdf + pre_v * jnp.exp(-0.5 * pre_v * pre_v) * _INV_SQRT_2PI
        dpre = dz * dgelu
        dbin_ref[...] += jnp.sum(dpre, axis=0, keepdims=True)
        dpre_ref[...] = dpre.astype(BF16)

    full = lambda shape: pl.BlockSpec(shape, lambda i, nd=len(shape): (0,) * nd)
    return pl.pallas_call(
        body, name=name,
        out_shape=[jax.ShapeDtypeStruct((T, W2), BF16), jax.ShapeDtypeStruct(ws.shape, F32),
                   jax.ShapeDtypeStruct(bsT.shape, F32), jax.ShapeDtypeStruct((1, W), F32),
                   jax.ShapeDtypeStruct((1, W), F32), jax.ShapeDtypeStruct((1, W2), F32)],
        grid=(T // CHUNK,),
        in_specs=[pl.BlockSpec((CHUNK, W2), lambda i: (i, 0)), pl.BlockSpec((CHUNK, W), lambda i: (i, 0)),
                  full((1, W)), full((1, W)), full(ws.shape), full(bsT.shape)],
        out_specs=[pl.BlockSpec((CHUNK, W2), lambda i: (i, 0)), full(ws.shape), full(bsT.shape),
                   full((1, W)), full((1, W)), full((1, W2))],
        compiler_params=pltpu.CompilerParams(dimension_semantics=("arbitrary",),
                                             vmem_limit_bytes=_vmem(24 * CHUNK * W2 * 4)),
    )(pre, dgated, lng, lnb, ws, bsT)


def _rope_tables(positions):
    half = ROPE_DIM // 2
    inv_freq = ROPE_THETA ** (-jnp.arange(0, ROPE_DIM, 2, dtype=F32) / ROPE_DIM)
    ang = positions.astype(F32).reshape(-1, 1) * inv_freq
    cos, sin = jnp.cos(ang), jnp.sin(ang)
    T = ang.shape[0]
    rest = HEAD_DIM - ROPE_DIM
    c64 = jnp.concatenate([cos, cos, jnp.ones((T, rest), F32)], axis=1)
    s64 = jnp.concatenate([-sin, sin, jnp.zeros((T, rest), F32)], axis=1)
    del half
    return jnp.tile(c64, (1, LANES // HEAD_DIM)), jnp.tile(s64, (1, LANES // HEAD_DIM))


def _swap8(x):
    W = x.shape[1]
    half = ROPE_DIM // 2
    lane = lax.broadcasted_iota(jnp.int32, x.shape, 1) % HEAD_DIM
    return jnp.where(lane < half, pltpu.roll(x, W - half, axis=1),
                     jnp.where(lane < ROPE_DIM, pltpu.roll(x, half, axis=1), 0.0))


def _wide(tab, W):
    return jnp.concatenate([tab] * (W // LANES), axis=1) if W > LANES else tab


def _rope_fwd(qkv, ctab, stab, *, q_width, kv_width, name):
    T = qkv.shape[0]
    tr = _pick(T, (256, 128))
    scale = HEAD_DIM ** -0.5

    def body(x_ref, c_ref, s_ref, q_ref, k_ref, v_ref):
        c = c_ref[...]
        s = s_ref[...]
        q = x_ref[:, :q_width]
        k = x_ref[:, q_width:q_width + kv_width]
        q_ref[...] = ((q * _wide(c, q_width) + _swap8(q) * _wide(s, q_width)) * scale).astype(BF16)
        k_ref[...] = (k * _wide(c, kv_width) + _swap8(k) * _wide(s, kv_width)).astype(BF16)
        v_ref[...] = x_ref[:, q_width + kv_width:].astype(BF16)

    return _row_call(body, [(qkv, "row"), (ctab, "row"), (stab, "row")],
                     [((T, q_width), BF16), ((T, kv_width), BF16), ((T, kv_width), BF16)],
                     name=name, rows=T, tr=tr, est=10 * tr * qkv.shape[1] * 4)


_NT = (((1,), (1,)), ((), ()))
_TN = (((0,), (0,)), ((), ()))


def _group_rows(ref, heads):
    return jnp.concatenate([ref[:, h * HEAD_DIM:(h + 1) * HEAD_DIM] for h in heads], axis=0)


def _attn_valid(grp):
    qi = np.arange(grp * CHUNK)[:, None] % CHUNK
    sj = np.arange(2 * CHUNK)[None, :]
    cur = (sj >= CHUNK) & (sj - CHUNK <= qi)
    prev = (sj < CHUNK) & (sj > qi)
    return jnp.asarray(np.stack([cur, cur | prev]).astype(np.float32))


def _valid_spec(grp):
    return pl.BlockSpec((None, grp * CHUNK, 2 * CHUNK), lambda n: (jnp.minimum(n, 1), 0, 0))


def _attn_group_probs(q, kk, sinks, valid, grp):
    rows = grp * CHUNK
    s = lax.dot_general(q, kk, _NT, preferred_element_type=F32)
    s = jnp.where(valid, s, NEG_INF)
    r = lax.broadcasted_iota(jnp.int32, (rows, 1), 0)
    sink = jnp.full((rows, 1), sinks[grp - 1], F32)
    for g in range(grp - 2, -1, -1):
        sink = jnp.where(r < (g + 1) * CHUNK, sinks[g], sink)
    m = jnp.maximum(jnp.max(s, axis=1, keepdims=True), sink)
    p = jnp.exp(s - m)
    ps = jnp.exp(sink - m)
    inv = 1.0 / (jnp.sum(p, axis=1, keepdims=True) + ps)
    return p * inv, ps * inv


def _kv_specs(width, nb):
    prev = pl.BlockSpec((CHUNK, width), lambda n: (jnp.maximum(n - 1, 0), 0))
    cur = pl.BlockSpec((CHUNK, width), lambda n: (n, 0))
    return prev, cur


def _attn_fwd(qr, kr, vr, sinks, *, name):
    T, QW = qr.shape
    KW = kr.shape[1]
    HQ, HK = QW // HEAD_DIM, KW // HEAD_DIM
    grp = HQ // HK
    nb = T // CHUNK

    def body(q_ref, kp_ref, kc_ref, vp_ref, vc_ref, s_ref, ok_ref, o_ref):
        valid = ok_ref[...] > 0.5
        for kh in range(HK):
            ks = slice(kh * HEAD_DIM, (kh + 1) * HEAD_DIM)
            heads = list(range(kh * grp, (kh + 1) * grp))
            q = _group_rows(q_ref, heads)
            kk = jnp.concatenate([kp_ref[:, ks], kc_ref[:, ks]], axis=0)
            vv = jnp.concatenate([vp_ref[:, ks], vc_ref[:, ks]], axis=0)
            p, _ = _attn_group_probs(q, kk, [s_ref[0, h] for h in heads], valid, grp)
            o = jnp.dot(p.astype(BF16), vv, preferred_element_type=F32).astype(BF16)
            for g, h in enumerate(heads):
                o_ref[:, h * HEAD_DIM:(h + 1) * HEAD_DIM] = o[g * CHUNK:(g + 1) * CHUNK]

    kp, kc = _kv_specs(KW, nb)
    return pl.pallas_call(
        body, name=name, out_shape=jax.ShapeDtypeStruct((T, QW), BF16), grid=(nb,),
        in_specs=[pl.BlockSpec((CHUNK, QW), lambda n: (n, 0)), kp, kc, kp, kc,
                  pl.BlockSpec(memory_space=pltpu.SMEM), _valid_spec(grp)],
        out_specs=pl.BlockSpec((CHUNK, QW), lambda n: (n, 0)),
        compiler_params=pltpu.CompilerParams(dimension_semantics=("arbitrary",), vmem_limit_bytes=_vmem(8 << 20)),
    )(qr, kr, kr, vr, vr, sinks, _attn_valid(grp))


def _attn_bwd(qr, kr, vr, sinks, do, *, name):
    T, QW = qr.shape
    KW = kr.shape[1]
    HQ, HK = QW // HEAD_DIM, KW // HEAD_DIM
    grp = HQ // HK
    nb = T // CHUNK

    def body(q_ref, kp_ref, kc_ref, vp_ref, vc_ref, s_ref, do_ref, ok_ref,
             dq_ref, dkp_ref, dkc_ref, dvp_ref, dvc_ref, ds_ref):
        n = pl.program_id(0)
        valid = ok_ref[...] > 0.5
        lane = lax.broadcasted_iota(jnp.int32, (1, LANES), 1)
        dsink = jnp.zeros((1, LANES), F32)
        for kh in range(HK):
            ks = slice(kh * HEAD_DIM, (kh + 1) * HEAD_DIM)
            heads = list(range(kh * grp, (kh + 1) * grp))
            q = _group_rows(q_ref, heads)
            doh = _group_rows(do_ref, heads)
            kk = jnp.concatenate([kp_ref[:, ks], kc_ref[:, ks]], axis=0)
            vv = jnp.concatenate([vp_ref[:, ks], vc_ref[:, ks]], axis=0)
            p, ps = _attn_group_probs(q, kk, [s_ref[0, h] for h in heads], valid, grp)
            dp = lax.dot_general(doh, vv, _NT, preferred_element_type=F32)
            delta = jnp.sum(p * dp, axis=1, keepdims=True)
            ds = (p * (dp - delta)).astype(BF16)
            dv = lax.dot_general(p.astype(BF16), doh, _TN, preferred_element_type=F32)
            dk = lax.dot_general(ds, q, _TN, preferred_element_type=F32)
            dq = jnp.dot(ds, kk, preferred_element_type=F32)
            psd = ps * delta
            for g, h in enumerate(heads):
                dq_ref[:, h * HEAD_DIM:(h + 1) * HEAD_DIM] = dq[g * CHUNK:(g + 1) * CHUNK]
                dsink = dsink + jnp.where(
                    lane == h, -jnp.sum(psd[g * CHUNK:(g + 1) * CHUNK], axis=0, keepdims=True), 0.0)
            dkp_ref[:, ks] = dk[:CHUNK]
            dkc_ref[:, ks] = dk[CHUNK:]
            dvp_ref[:, ks] = dv[:CHUNK]
            dvc_ref[:, ks] = dv[CHUNK:]

        @pl.when(n == 0)
        def _():
            ds_ref[...] = dsink

        @pl.when(n > 0)
        def _():
            ds_ref[...] += dsink

    kp, kc = _kv_specs(KW, nb)
    qspec = pl.BlockSpec((CHUNK, QW), lambda n: (n, 0))
    kout = pl.BlockSpec((CHUNK, KW), lambda n: (n, 0))
    return pl.pallas_call(
        body, name=name,
        out_shape=[jax.ShapeDtypeStruct((T, QW), F32)] + [jax.ShapeDtypeStruct((T, KW), F32)] * 4
        + [jax.ShapeDtypeStruct((1, LANES), F32)],
        grid=(nb,),
        in_specs=[qspec, kp, kc, kp, kc, pl.BlockSpec(memory_space=pltpu.SMEM), qspec, _valid_spec(grp)],
        out_specs=[qspec, kout, kout, kout, kout, pl.BlockSpec((1, LANES), lambda n: (0, 0))],
        compiler_params=pltpu.CompilerParams(dimension_semantics=("arbitrary",), vmem_limit_bytes=_vmem(12 << 20)),
    )(qr, kr, kr, vr, vr, sinks, do, _attn_valid(grp))


def _rope_bwd(dq, dkp, dkc, dvp, dvc, ctab, stab, *, name):
    T, QW = dq.shape
    KW = dkp.shape[1]
    nb = T // CHUNK
    scale = HEAD_DIM ** -0.5
    width = QW + 2 * KW

    def body(dq_ref, dkc_ref, dkn_ref, dvc_ref, dvn_ref, c_ref, s_ref, o_ref, db_ref):
        n = pl.program_id(0)
        c = c_ref[...]
        s = s_ref[...]
        has_next = (n < nb - 1).astype(F32)
        dqv = dq_ref[...]
        dk = dkc_ref[...] + has_next * dkn_ref[...]
        dv = dvc_ref[...] + has_next * dvn_ref[...]
        dq_pre = (dqv * _wide(c, QW) + _swap8(dqv * _wide(s, QW))) * scale
        dk_pre = dk * _wide(c, KW) + _swap8(dk * _wide(s, KW))
        o_ref[:, :QW] = dq_pre.astype(BF16)
        o_ref[:, QW:QW + KW] = dk_pre.astype(BF16)
        o_ref[:, QW + KW:] = dv.astype(BF16)
        part = jnp.concatenate([jnp.sum(dq_pre, axis=0, keepdims=True), jnp.sum(dk_pre, axis=0, keepdims=True),
                                jnp.sum(dv, axis=0, keepdims=True)], axis=1)

        @pl.when(n == 0)
        def _():
            db_ref[...] = part

        @pl.when(n > 0)
        def _():
            db_ref[...] += part

    cur = lambda w: pl.BlockSpec((CHUNK, w), lambda n: (n, 0))
    nxt = lambda w: pl.BlockSpec((CHUNK, w), lambda n: (jnp.minimum(n + 1, nb - 1), 0))
    return pl.pallas_call(
        body, name=name,
        out_shape=[jax.ShapeDtypeStruct((T, width), BF16), jax.ShapeDtypeStruct((1, width), F32)],
        grid=(nb,),
        in_specs=[cur(QW), cur(KW), nxt(KW), cur(KW), nxt(KW), cur(LANES), cur(LANES)],
        out_specs=[cur(width), pl.BlockSpec((1, width), lambda n: (0, 0))],
        compiler_params=pltpu.CompilerParams(dimension_semantics=("arbitrary",), vmem_limit_bytes=_vmem(8 << 20)),
    )(dq, dkc, dkp, dvc, dvp, ctab, stab)


def _cast_block(w, l, axis, chip_arr, *, name):
    _, Ks, Ns = w.shape
    tk = _pick(Ks, (512, 352, 256, 128))
    nk = Ks // tk
    full = (Ks * N_CHIPS, Ns) if axis == 0 else (Ks, Ns * N_CHIPS)

    def body(p_ref, w_ref, o_ref):
        o_ref[...] = w_ref[...].astype(BF16)

    if axis == 0:
        out_spec = pl.BlockSpec((tk, Ns), lambda i, p: (p[0] * nk + i, 0))
    else:
        out_spec = pl.BlockSpec((tk, Ns), lambda i, p: (i, p[0]))
    grid_spec = pltpu.PrefetchScalarGridSpec(
        num_scalar_prefetch=1, grid=(nk,),
        in_specs=[pl.BlockSpec((None, tk, Ns), lambda i, p: (l, i, 0))], out_specs=out_spec)
    return pl.pallas_call(
        body, name=name, out_shape=jax.ShapeDtypeStruct(full, BF16), grid_spec=grid_spec,
        compiler_params=pltpu.CompilerParams(dimension_semantics=("arbitrary",),
                                             vmem_limit_bytes=_vmem(4 * tk * Ns * 6)),
    )(chip_arr, w)


def _adamw_math(w, g, m, v):
    m = ADAM_B1 * m + (1.0 - ADAM_B1) * g
    v = ADAM_B2 * v + (1.0 - ADAM_B2) * (g * g)
    m_hat = m / (1.0 - ADAM_B1 ** ADAM_STEP)
    v_hat = v / (1.0 - ADAM_B2 ** ADAM_STEP)
    delta = -ADAM_LR * (m_hat / (jnp.sqrt(v_hat) + ADAM_EPS) + ADAM_WD * w)
    return delta, m, v


def _adamw_layer(w, m, v, g, l, outs, *, name):
    _, K, N = w.shape
    tk = _pick(K, (512, 352, 256, 128)) if N <= 1024 else _pick(K, (256, 176, 128))

    def body(w_ref, m_ref, v_ref, g_ref, _g, _d, _m, _v, go_ref, d_ref, mo_ref, vo_ref):
        gv = g_ref[...]
        d, mn, vn = _adamw_math(w_ref[...], gv, m_ref[...], v_ref[...])
        go_ref[...] = gv
        d_ref[...] = d
        mo_ref[...] = mn
        vo_ref[...] = vn

    layer = pl.BlockSpec((None, tk, N), lambda i: (l, i, 0))
    any_spec = pl.BlockSpec(memory_space=pl.ANY)
    sd = jax.ShapeDtypeStruct(w.shape, F32)
    return pl.pallas_call(
        body, name=name, out_shape=[sd, sd, sd, sd], grid=(K // tk,),
        in_specs=[layer, layer, layer, pl.BlockSpec((tk, N), lambda i: (i, 0))] + [any_spec] * 4,
        out_specs=[layer] * 4, input_output_aliases={4: 0, 5: 1, 6: 2, 7: 3},
        compiler_params=pltpu.CompilerParams(dimension_semantics=("arbitrary",),
                                             vmem_limit_bytes=_vmem(2 * 8 * tk * N * 4 + 6 * tk * N * 4)),
    )(w, m, v, g, *outs)


def _adamw_small(w, g, m, v, *, name):
    def body(w_ref, g_ref, m_ref, v_ref, d_ref, mo_ref, vo_ref):
        d, mn, vn = _adamw_math(w_ref[...], g_ref[...], m_ref[...], v_ref[...])
        d_ref[...] = d
        mo_ref[...] = mn
        vo_ref[...] = vn

    sd = jax.ShapeDtypeStruct(w.shape, F32)
    return pl.pallas_call(body, name=name, out_shape=[sd, sd, sd])(w, g, m, v)


def _my_place():
    return lax.axis_index("x"), lax.axis_index("y"), lax.axis_index("c")


def _peer_chips(x, y):
    return [(1 - x, y), (x, 1 - y), (1 - x, 1 - y)]


_HBM = pl.BlockSpec(memory_space=pltpu.HBM)
_SEM = pl.BlockSpec(memory_space=pltpu.SEMAPHORE)
_EFFECT = pltpu.SideEffectType.DATAFLOW_SIDE_EFFECTING


def _split_start(name, bufs, n_copies, make_copies, after):
    nb = len(bufs)

    def body(*refs):
        send_sems, recv_sems = refs[nb + 1], refs[nb + 2]
        token = refs[2 * nb + 3]
        sends, _ = make_copies(refs[:nb], send_sems, recv_sems)
        for cp in sends:
            cp.start()
        token[...] = jnp.zeros_like(token)

    res = pl.pallas_call(
        body, name=name,
        out_shape=(pltpu.SemaphoreType.DMA((n_copies,)), pltpu.SemaphoreType.DMA((n_copies,)),
                   *[pltpu.HBM(b.shape, b.dtype) for b in bufs], jax.ShapeDtypeStruct((8, LANES), F32)),
        in_specs=[_HBM] * nb + [pl.BlockSpec(memory_space=pl.ANY)],
        out_specs=(_SEM, _SEM, *[_HBM] * nb, pl.BlockSpec(memory_space=pltpu.VMEM)),
        input_output_aliases={k: 2 + k for k in range(nb)},
        compiler_params=pltpu.CompilerParams(has_side_effects=_EFFECT),
    )(*[pltpu.with_memory_space_constraint(b, pltpu.HBM) for b in bufs],
      after[0] if isinstance(after, (list, tuple)) else after)
    return res[0], res[1], list(res[2:2 + nb]), res[2 + nb]


def _split_wait(name, bufs, sems, make_copies, after):
    nb = len(bufs)
    after = list(after) if isinstance(after, (list, tuple)) else [after]

    def body(*refs):
        send_sems, recv_sems = refs[nb], refs[nb + 1]
        sends, recvs = make_copies(refs[:nb], send_sems, recv_sems)
        for cp in sends:
            cp.wait_send()
        for cp in recvs:
            cp.wait_recv()

    res = pl.pallas_call(
        body, name=name,
        out_shape=tuple(pltpu.HBM(b.shape, b.dtype) for b in bufs),
        in_specs=[_HBM] * nb + [_SEM, _SEM] + [pl.BlockSpec(memory_space=pl.ANY)] * len(after),
        out_specs=tuple([_HBM] * nb),
        input_output_aliases={k: k for k in range(nb)},
        compiler_params=pltpu.CompilerParams(has_side_effects=_EFFECT),
    )(*bufs, sems[0], sems[1], *after)
    return list(res)


def _remote(src, dst, send_sems, recv_sems, k, target):
    return pltpu.make_async_remote_copy(src_ref=src, dst_ref=dst, send_sem=send_sems.at[k],
                                        recv_sem=recv_sems.at[k], device_id=target, device_id_type=MESH)


def _ag_region(ref, axis, chip, half):
    K, N = ref.shape
    if axis == 0:
        hs = K // N_CHIPS // 2
        assert hs % 16 == 0
        return ref.at[pl.ds(pl.multiple_of((2 * chip + half) * hs, 16), hs), :]
    ns, hk = N // N_CHIPS, K // 2
    assert ns % LANES == 0 and hk % 16 == 0
    return ref.at[pl.ds(pl.multiple_of(half * hk, 16), hk), pl.ds(pl.multiple_of(chip * ns, LANES), ns)]


def _ag_copies(stage, axes):
    n = len(axes)

    def make(bufs, send_sems, recv_sems):
        x, y, c = _my_place()
        me = 2 * x + y
        sends, recvs = [], []
        for j, (px, py) in enumerate(_peer_chips(x, y)):
            other = 2 * px + py
            for w in range(n):
                k = j * n + w
                if stage == 1:
                    src, target = _ag_region(bufs[w], axes[w], me, c), (px, py, c)
                    land = _ag_region(bufs[w], axes[w], other, c)
                else:
                    src, target = _ag_region(bufs[w], axes[w], other, c), (x, y, 1 - c)
                    land = _ag_region(bufs[w], axes[w], other, 1 - c)
                sends.append(_remote(src, src, send_sems, recv_sems, k, target))
                recvs.append(_remote(land, land, send_sems, recv_sems, k, target))
        return sends, recvs

    return make


def _half_shape(shape, axis):
    K, N = shape
    return (K, N // 2) if axis == 0 else (K // 2, N)


def _core_half(ref, axis, half):
    K, N = ref.shape
    if axis == 0:
        return ref.at[:, pl.ds(pl.multiple_of(half * (N // 2), LANES), N // 2)]
    return ref.at[pl.ds(pl.multiple_of(half * (K // 2), 16), K // 2), :]


def _chip_block(ref, axis, chip):
    K, N = ref.shape
    if axis == 0:
        return ref.at[pl.ds(pl.multiple_of(chip * (K // N_CHIPS), 16), K // N_CHIPS), :]
    return ref.at[:, pl.ds(pl.multiple_of(chip * (N // N_CHIPS), LANES), N // N_CHIPS)]


def _rs_sibling_copies(axes):
    n = len(axes)

    def make(bufs, send_sems, recv_sems):
        x, y, c = _my_place()
        sends = [_remote(_core_half(bufs[w], axes[w], 1 - c), bufs[n + w], send_sems, recv_sems, w, (x, y, 1 - c))
                 for w in range(n)]
        recvs = [_remote(bufs[n + w], bufs[n + w], send_sems, recv_sems, w, (x, y, 1 - c)) for w in range(n)]
        return sends, recvs

    return make


def _rs_chip_copies(axes):
    n = len(axes)

    def make(bufs, send_sems, recv_sems):
        x, y, c = _my_place()
        sends, recvs = [], []
        for j, (px, py) in enumerate(_peer_chips(x, y)):
            for w in range(n):
                k = j * n + w
                sends.append(_remote(_chip_block(bufs[w], axes[w], 2 * px + py), bufs[n + w].at[j],
                                     send_sems, recv_sems, k, (px, py, c)))
                recvs.append(_remote(bufs[n + w].at[j], bufs[n + w].at[j], send_sems, recv_sems, k, (px, py, c)))
        return sends, recvs

    return make


def _rs_fill_copies(axes):
    n = len(axes)

    def make(bufs, send_sems, recv_sems):
        x, y, c = _my_place()
        sends = [_remote(_core_half(bufs[w], axes[w], c), _core_half(bufs[w], axes[w], c),
                         send_sems, recv_sems, w, (x, y, 1 - c)) for w in range(n)]
        recvs = [_remote(_core_half(bufs[w], axes[w], 1 - c), _core_half(bufs[w], axes[w], 1 - c),
                         send_sems, recv_sems, w, (x, y, 1 - c)) for w in range(n)]
        return sends, recvs

    return make


def _chip_sum(g, r, axis, place, *, name):
    hk, hn = r.shape
    bk, bn = (hk // N_CHIPS, hn) if axis == 0 else (hk, hn // N_CHIPS)
    tk = _pick(bk, (512, 352, 256, 128))
    nk = bk // tk

    def body(p_ref, g_ref, r_ref, b_ref, own_ref):
        s = g_ref[...].astype(F32) + r_ref[...].astype(F32)
        b_ref[...] = s.astype(BF16)

        @pl.when(pl.program_id(1) == p_ref[0])
        def _():
            own_ref[...] = s

    if axis == 0:
        g_spec = pl.BlockSpec((tk, bn), lambda i, j, p: (j * nk + i, p[1]))
        r_spec = pl.BlockSpec((tk, bn), lambda i, j, p: (j * nk + i, 0))
    else:
        g_spec = pl.BlockSpec((tk, bn), lambda i, j, p: (p[1] * nk + i, j))
        r_spec = pl.BlockSpec((tk, bn), lambda i, j, p: (i, j))
    grid_spec = pltpu.PrefetchScalarGridSpec(
        num_scalar_prefetch=1, grid=(nk, N_CHIPS), in_specs=[g_spec, r_spec],
        out_specs=[r_spec, pl.BlockSpec((tk, bn), lambda i, j, p: (i, 0))])
    return pl.pallas_call(
        body, name=name,
        out_shape=[jax.ShapeDtypeStruct(r.shape, BF16), jax.ShapeDtypeStruct((bk, bn), F32)],
        grid_spec=grid_spec,
        compiler_params=pltpu.CompilerParams(dimension_semantics=("arbitrary", "arbitrary"),
                                             vmem_limit_bytes=_vmem(2 * tk * bn * 10 + 3 * tk * bn * 4)),
    )(place, g, r)


def _final_sum(own, recv, axis, place, *, name):
    _, bk, bn = recv.shape
    tk = _pick(bk, (256, 176, 128))
    nk = bk // tk

    def body(p_ref, o_ref, r_ref, out_ref):
        out_ref[...] = ((o_ref[...] + r_ref[0].astype(F32)) + r_ref[1].astype(F32)) + r_ref[2].astype(F32)

    own_spec = pl.BlockSpec((tk, bn), lambda i, p: (i, 0))
    if axis == 0:
        out_shape, out_spec = (bk, 2 * bn), pl.BlockSpec((tk, bn), lambda i, p: (i, p[1]))
    else:
        out_shape, out_spec = (2 * bk, bn), pl.BlockSpec((tk, bn), lambda i, p: (p[1] * nk + i, 0))
    grid_spec = pltpu.PrefetchScalarGridSpec(
        num_scalar_prefetch=1, grid=(nk,),
        in_specs=[own_spec, pl.BlockSpec((3, tk, bn), lambda i, p: (0, i, 0))], out_specs=out_spec)
    return pl.pallas_call(
        body, name=name, out_shape=jax.ShapeDtypeStruct(out_shape, F32), grid_spec=grid_spec,
        compiler_params=pltpu.CompilerParams(dimension_semantics=("arbitrary",),
                                             vmem_limit_bytes=_vmem(2 * tk * bn * 14 + 4 * tk * bn * 4)),
    )(place, own, recv)


def _allreduce_small(p, after=()):
    n_after = len(after)

    def body(*refs):
        p_ref = refs[0]
        o_ref, r0, r1, r2, send_sems, recv_sems = refs[1 + n_after:]
        x, y, c = _my_place()
        o_ref[...] = p_ref[...]
        for s, (peer, rbuf) in enumerate([((x, y, 1 - c), r0), ((1 - x, y, c), r1), ((x, 1 - y, c), r2)]):
            cp = pltpu.make_async_remote_copy(src_ref=o_ref, dst_ref=rbuf, send_sem=send_sems.at[s],
                                              recv_sem=recv_sems.at[s], device_id=peer, device_id_type=MESH)
            cp.start()
            cp.wait()
            o_ref[...] = o_ref[...] + rbuf[...]

    vm = pl.BlockSpec(memory_space=pltpu.VMEM)
    return pl.pallas_call(
        body, name="allreduce_small", out_shape=jax.ShapeDtypeStruct(p.shape, F32),
        in_specs=[vm] + [pl.BlockSpec(memory_space=pl.ANY)] * n_after, out_specs=vm,
        scratch_shapes=[pltpu.VMEM(p.shape, F32)] * 3 + [pltpu.SemaphoreType.DMA((3,))] * 2,
        compiler_params=pltpu.CompilerParams(vmem_limit_bytes=_vmem(6 * _nbytes(p.shape, F32))),
    )(p, *after)


def _pack_rows(parts):
    rows, metas = [], []
    for a in parts:
        flat = a.reshape(-1)
        nrow = -(-flat.shape[0] // LANES)
        nrow = -(-nrow // 8) * 8
        flat = jnp.pad(flat, (0, nrow * LANES - flat.shape[0]))
        rows.append(flat.reshape(nrow, LANES))
        metas.append((a.shape, nrow))
    return jnp.concatenate(rows, axis=0), metas


def _unpack_rows(packed, metas):
    out, r0 = [], 0
    for shape, nrow in metas:
        size = int(np.prod(shape))
        out.append(packed[r0:r0 + nrow].reshape(-1)[:size].reshape(shape))
        r0 += nrow
    return out


def kernel(x, positions, pre_mix_g, post_mix_g, pre_ffn_g, post_ffn_g, a_w_in, a_b_in, a_ln_g, a_ln_b, a_w_s, a_b_s, a_w_out, b_w_qkv, b_b_qkv, b_sinks, b_w_o, ffn_w_gu, ffn_w_down, loss_target, m_pre_mix_g, m_post_mix_g, m_pre_ffn_g, m_post_ffn_g, m_a_w_in, m_a_b_in, m_a_ln_g, m_a_ln_b, m_a_w_s, m_a_b_s, m_a_w_out, m_b_w_qkv, m_b_b_qkv, m_b_sinks, m_b_w_o, m_ffn_w_gu, m_ffn_w_down, v_pre_mix_g, v_post_mix_g, v_pre_ffn_g, v_post_ffn_g, v_a_w_in, v_a_b_in, v_a_ln_g, v_a_ln_b, v_a_w_s, v_a_b_s, v_a_w_out, v_b_w_qkv, v_b_b_qkv, v_b_sinks, v_b_w_o, v_ffn_w_gu, v_ffn_w_down):
    depth, D = pre_mix_g.shape
    xi, yi, ci = _my_place()
    chip = 2 * xi + yi
    place = jnp.stack([chip, ci]).astype(jnp.int32)

    stacked = {"a_w_in": (a_w_in, m_a_w_in, v_a_w_in), "a_w_out": (a_w_out, m_a_w_out, v_a_w_out),
               "b_w_qkv": (b_w_qkv, m_b_w_qkv, v_b_w_qkv), "b_w_o": (b_w_o, m_b_w_o, v_b_w_o),
               "ffn_w_gu": (ffn_w_gu, m_ffn_w_gu, v_ffn_w_gu), "ffn_w_down": (ffn_w_down, m_ffn_w_down, v_ffn_w_down)}
    cut = {"a_w_in": 1, "a_w_out": 0, "b_w_qkv": 1, "b_w_o": 0, "ffn_w_gu": 1, "ffn_w_down": 0}

    def layer_keys(i):
        mix = [("a_w_in", i // 2), ("a_w_out", i // 2)] if i % 2 == 0 else [("b_w_qkv", i // 2), ("b_w_o", i // 2)]
        return mix + [("ffn_w_gu", i), ("ffn_w_down", i)]

    def dep(a, toks):
        for t in toks:
            a = a + t[:1, :1]
        return a

    W = {}
    for i in range(depth):
        for nm, l in layer_keys(i):
            W[(nm, l)] = _cast_block(stacked[nm][0], l, cut[nm], place, name=f"cast_{nm}_{l}")

    def gather(tag, keys, after):
        axes = [cut[nm] for nm, _ in keys]
        for stage in (1, 2):
            ss, rs, bufs, tok = _split_start(f"ag{stage}_start_{tag}", [W[k] for k in keys], 3 * len(keys),
                                             _ag_copies(stage, axes), after)
            after = yield tok
            bufs = _split_wait(f"ag{stage}_wait_{tag}", bufs, (ss, rs), _ag_copies(stage, axes), after)
            W.update(zip(keys, bufs))
        yield None

    nq = b_b_qkv.shape[1]
    bq_full = jnp.zeros((b_b_qkv.shape[0], N_CHIPS * nq), F32)
    bq_full = lax.dynamic_update_slice(bq_full, jnp.where(ci == 0, b_b_qkv, 0.0), (0, chip * nq))
    bq_packed, bq_meta = _pack_rows([bq_full])
    bq_gathered = _allreduce_small(bq_packed)
    b_qkv_full = _unpack_rows(bq_gathered, bq_meta)[0]

    first = gather("0m", layer_keys(0)[:2], bq_gathered)
    tok = next(first)
    tok = first.send([tok] + [W[k] for i in range(depth) for k in layer_keys(i)[2 if i == 0 else 0:]])
    first.send(tok)

    h = x[0]
    target = loss_target[0]
    ctab, stab = _rope_tables(positions[0])
    q_width = W[("b_w_o", 0)].shape[0]
    kv_width = N_KV_HEADS * HEAD_DIM
    row = lambda a, i: a[i:i + 1]
    gains = {"pre_mix": pre_mix_g[:, None], "post_mix": post_mix_g[:, None], "pre_ffn": pre_ffn_g[:, None],
             "post_ffn": post_ffn_g[:, None]}
    gain = lambda which, i: (gains[which], i)

    saved = []
    hn = None
    for i in range(depth):
        j = i // 2
        s = {"h": h}
        ffn_w = None
        if i == 0:
            ffn_w = gather("0f", layer_keys(0)[2:], W[("a_w_out", 0)])
            toks = [next(ffn_w)]
            nxt = gather("1", layer_keys(1), toks[0])
            toks.append(next(nxt))
            hn = _rms_fwd(h, gain("pre_mix", i), out_dtype=BF16, after=toks, name=f"rms_pre_mix_{i}")
        elif i + 1 < depth:
            nxt = gather(str(i + 1), layer_keys(i + 1), h)
            toks = [next(nxt)]
        else:
            toks = []
        s["hn"] = hn
        if i % 2 == 0:
            pre = _matmul(hn, W[("a_w_in", j)], mode="nn", bias=row(a_b_in, j), out_dtype=F32, after=toks,
                          name=f"gmlp_in_{i}")
            gated = _sgu_fwd(pre, row(a_ln_g, j), row(a_ln_b, j), a_w_s[j], a_b_s[j].T, name=f"sgu_fwd_{i}")
            mix = _matmul(gated, W[("a_w_out", j)], mode="nn", out_dtype=F32, name=f"gmlp_out_{i}")
            s.update(pre=pre, gated=gated)
        else:
            qkv = _matmul(hn, W[("b_w_qkv", j)], mode="nn", bias=row(b_qkv_full, j), out_dtype=F32, after=toks,
                          name=f"attn_qkv_{i}")
            qr, kr, vr = _rope_fwd(qkv, ctab, stab, q_width=q_width, kv_width=kv_width, name=f"rope_fwd_{i}")
            o = _attn_fwd(qr, kr, vr, row(b_sinks, j), name=f"attn_fwd_{i}")
            mix = _matmul(o, W[("b_w_o", j)], mode="nn", out_dtype=F32, name=f"attn_o_{i}")
            s.update(qr=qr, kr=kr, vr=vr, o=o)
        s["mix"] = mix
        toks = [ffn_w.send(mix)] if ffn_w else []
        h1, fn = _rms_res_norm(h, mix, gain("post_mix", i), gain("pre_ffn", i), after=toks, name=f"rms_post_mix_{i}")
        if ffn_w:
            ffn_w.send(h1)
        s["h1"] = h1
        g_pre, u_pre, act = _ffn_up(fn, W[("ffn_w_gu", i)][None], 0, name=f"ffn_up_{i}")
        f = _matmul(act, W[("ffn_w_down", i)], mode="nn", out_dtype=F32, name=f"ffn_down_{i}")
        if i + 1 < depth:
            toks = [nxt.send(f)]
            h, hn = _rms_res_norm(h1, f, gain("post_ffn", i), gain("pre_mix", i + 1), after=toks,
                                  name=f"rms_post_ffn_{i}")
            nxt.send(h)
        else:
            h = _rms_res(h1, f, gain("post_ffn", i), name=f"rms_post_ffn_{i}")
        s.update(fn=fn, g_pre=g_pre, u_pre=u_pre, act=act, f=f)
        saved.append(s)

    dh, df, loss_part, g_last = _loss_and_grad(h, target, saved[-1]["f"], gain("post_ffn", depth - 1), name="loss")

    big_out = {nm: tuple(lax.empty(w.shape, F32) for _ in range(4)) for nm, (w, _, _) in stacked.items()}

    def reduce_group(i, keys, grads):
        axes = [cut[nm] for nm, _ in keys]
        n = len(keys)
        lands = [lax.empty(_half_shape(g.shape, ax), BF16) for g, ax in zip(grads, axes)]
        ss, rs, bufs, tok = _split_start(f"rs_sibling_start_{i}", list(grads) + lands, n, _rs_sibling_copies(axes),
                                         place)
        after = yield tok
        bufs = _split_wait(f"rs_sibling_wait_{i}", bufs, (ss, rs), _rs_sibling_copies(axes), after)
        sums = [_chip_sum(bufs[w], bufs[n + w], axes[w], place, name=f"chip_sum_{keys[w][0]}_{keys[w][1]}")
                for w in range(n)]
        lands = [lax.empty((3,) + own.shape, BF16) for _, own in sums]
        ss, rs, bufs, tok = _split_start(f"rs_chip_start_{i}", [sb for sb, _ in sums] + lands, 3 * n,
                                         _rs_chip_copies(axes), place)
        after = yield tok
        bufs = _split_wait(f"rs_chip_wait_{i}", bufs, (ss, rs), _rs_chip_copies(axes), after)
        blocks = [_final_sum(sums[w][1], bufs[n + w], axes[w], place, name=f"final_sum_{keys[w][0]}_{keys[w][1]}")
                  for w in range(n)]
        ss, rs, bufs, tok = _split_start(f"rs_fill_start_{i}", blocks, n, _rs_fill_copies(axes), place)
        after = yield tok
        blocks = _split_wait(f"rs_fill_wait_{i}", bufs, (ss, rs), _rs_fill_copies(axes), after)
        for (nm, l), g in zip(keys, blocks):
            w, m, v = stacked[nm]
            big_out[nm] = tuple(_adamw_layer(w, m, v, g, l, big_out[nm], name=f"adamw_{nm}_{l}"))
        yield None

    reducing = []

    def advance(after, newest_only=False):
        toks = []
        for gen in (reducing[-1:] if newest_only else list(reducing)):
            tok = gen.send(after)
            if tok is None:
                reducing.remove(gen)
            else:
                toks.append(tok)
        return toks

    small = {}
    g_pre_mix, g_post_mix, g_pre_ffn, g_post_ffn = [None] * depth, [None] * depth, [None] * depth, [None] * depth
    g_post_ffn[depth - 1] = g_last
    toks = []
    early = []
    for i in reversed(range(depth)):
        j = i // 2
        s = saved[i]
        g_down = _matmul(s["act"], df, mode="tn", out_dtype=BF16, after=toks, name=f"ffn_down_dw_{i}")
        dg_, du_ = _ffn_down_dx(df, W[("ffn_w_down", i)][None], 0, s["g_pre"], s["u_pre"], g_down,
                                name=f"ffn_down_dx_{i}")
        g_gu = _matmul_pair(s["fn"], dg_, du_, mode="tn", out_dtype=BF16, name=f"ffn_gu_dw_{i}")
        dfn = _matmul_pair(dg_, W[("ffn_w_gu", i)], du_, mode="nt", out_dtype=F32, after=[g_gu],
                           name=f"ffn_gu_dx_{i}")
        toks = advance(dfn)
        if i == 0:
            gen = reduce_group("0f", layer_keys(0)[2:], [g_gu, g_down])
            toks.append(next(gen))
            reducing.append(gen)
        dh1, dmix, g_pre_ffn[i], g_post_mix[i] = _rms_bwd_chain(
            s["h1"], gain("pre_ffn", i), dfn, dh, s["mix"], gain("post_mix", i), after=toks,
            name=f"rms_ffn_mix_bwd_{i}")
        if i % 2 == 0:
            g_out = _matmul(s["gated"], dmix, mode="tn", out_dtype=BF16, name=f"gmlp_out_dw_{i}")
            dgated = _matmul(dmix, W[("a_w_out", j)], mode="nt", out_dtype=BF16, after=[g_out],
                             name=f"gmlp_out_dx_{i}")
            toks = advance(dgated, newest_only=True) if i == 0 else []
            dpre, dws, dbsT, dlng, dlnb, dbin = _sgu_bwd(s["pre"], dgated, dep(row(a_ln_g, j), toks), row(a_ln_b, j),
                                                         a_w_s[j], a_b_s[j].T, name=f"sgu_bwd_{i}")
            small[("a_w_s", j)] = dws
            small[("a_b_s", j)] = dbsT.T
            small[("a_ln_g", j)] = dlng
            small[("a_ln_b", j)] = dlnb
            small[("a_b_in", j)] = dbin
            g_in = _matmul(s["hn"], dpre, mode="tn", out_dtype=BF16, name=f"gmlp_in_dw_{i}")
            if i == 0:
                last = reduce_group("0m", layer_keys(0)[:2], [g_in, g_out])
                early = [next(last)]
            dhn = _matmul(dpre, W[("a_w_in", j)], mode="nt", out_dtype=F32, after=[g_in] + early,
                          name=f"gmlp_in_dx_{i}")
        else:
            g_out = _matmul(s["o"], dmix, mode="tn", out_dtype=BF16, name=f"attn_o_dw_{i}")
            do = _matmul(dmix, W[("b_w_o", j)], mode="nt", out_dtype=BF16, after=[g_out], name=f"attn_o_dx_{i}")
            dq, dkp, dkc, dvp, dvc, dsk = _attn_bwd(s["qr"], s["kr"], s["vr"], row(b_sinks, j), do,
                                                    name=f"attn_bwd_{i}")
            dqkv, dbq = _rope_bwd(dq, dkp, dkc, dvp, dvc, ctab, stab, name=f"rope_bwd_{i}")
            small[("b_sinks", j)] = dsk[:, :b_sinks.shape[1]]
            small[("b_b_qkv", j)] = dbq
            g_in = _matmul(s["hn"], dqkv, mode="tn", out_dtype=BF16, name=f"attn_qkv_dw_{i}")
            if i == 0:
                last = reduce_group("0m", layer_keys(0)[:2], [g_in, g_out])
                early = [next(last)]
            dhn = _matmul(dqkv, W[("b_w_qkv", j)], mode="nt", out_dtype=F32, after=[g_in] + early,
                          name=f"attn_qkv_dx_{i}")
        toks = advance(dhn)
        if i > 0:
            dh, df, g_pre_mix[i], g_post_ffn[i - 1] = _rms_bwd_chain(
                s["h"], gain("pre_mix", i), dhn, dh1, saved[i - 1]["f"], gain("post_ffn", i - 1), after=toks,
                name=f"rms_mix_ffn_bwd_{i}")
            gen = reduce_group(str(i), layer_keys(i), [g_in, g_out, g_gu, g_down])
            toks = [next(gen)] + advance(dh)
            reducing.append(gen)
        else:
            toks.append(last.send(dhn))
            dh, g_pre_mix[i] = _rms_bwd(s["h"], gain("pre_mix", i), dhn, dh1, out_dtype=F32, after=toks,
                                        name=f"rms_pre_mix_bwd_{i}")
            advance(dh)
    grad_x = dh[None]
    assert not reducing

    ready = [big_out[nm][1] for nm in big_out]
    n_a, n_b = a_b_in.shape[0], b_sinks.shape[0]
    stack = lambda key, n: jnp.concatenate([small[(key, j)] for j in range(n)], axis=0)
    small_parts = [
        jnp.concatenate(g_pre_mix, axis=0), jnp.concatenate(g_post_mix, axis=0),
        jnp.concatenate(g_pre_ffn, axis=0), jnp.concatenate(g_post_ffn, axis=0),
        stack("a_b_in", n_a), stack("a_ln_g", n_a), stack("a_ln_b", n_a),
        jnp.stack([small[("a_w_s", j)] for j in range(n_a)]), jnp.stack([small[("a_b_s", j)] for j in range(n_a)]),
        stack("b_b_qkv", n_b), stack("b_sinks", n_b), loss_part,
    ]
    packed, metas = _pack_rows(small_parts)
    reduced = _allreduce_small(packed, after=ready + [dh])
    while last.send(reduced) is not None:
        pass
    red = _unpack_rows(reduced, metas)
    (gr_pre_mix, gr_post_mix, gr_pre_ffn, gr_post_ffn, gr_b_in, gr_ln_g, gr_ln_b, gr_w_s, gr_b_s,
     gr_b_qkv_full, gr_sinks, loss_sum) = red
    loss = loss_sum[0, 0]
    gr_b_qkv = lax.dynamic_slice(gr_b_qkv_full, (0, chip * nq), (gr_b_qkv_full.shape[0], nq))

    grads = {"pre_mix_g": gr_pre_mix, "post_mix_g": gr_post_mix, "pre_ffn_g": gr_pre_ffn, "post_ffn_g": gr_post_ffn,
             "a_b_in": gr_b_in, "a_ln_g": gr_ln_g, "a_ln_b": gr_ln_b, "a_w_s": gr_w_s, "a_b_s": gr_b_s,
             "b_b_qkv": gr_b_qkv, "b_sinks": gr_sinks}
    weights = {"pre_mix_g": (pre_mix_g, m_pre_mix_g, v_pre_mix_g), "post_mix_g": (post_mix_g, m_post_mix_g, v_post_mix_g),
               "pre_ffn_g": (pre_ffn_g, m_pre_ffn_g, v_pre_ffn_g), "post_ffn_g": (post_ffn_g, m_post_ffn_g, v_post_ffn_g),
               "a_b_in": (a_b_in, m_a_b_in, v_a_b_in), "a_ln_g": (a_ln_g, m_a_ln_g, v_a_ln_g),
               "a_ln_b": (a_ln_b, m_a_ln_b, v_a_ln_b), "a_w_s": (a_w_s, m_a_w_s, v_a_w_s), "a_b_s": (a_b_s, m_a_b_s, v_a_b_s),
               "b_b_qkv": (b_b_qkv, m_b_b_qkv, v_b_b_qkv), "b_sinks": (b_sinks, m_b_sinks, v_b_sinks)}
    order = ["pre_mix_g", "post_mix_g", "pre_ffn_g", "post_ffn_g", "a_w_in", "a_b_in", "a_ln_g", "a_ln_b", "a_w_s",
             "a_b_s", "a_w_out", "b_w_qkv", "b_b_qkv", "b_sinks", "b_w_o", "ffn_w_gu", "ffn_w_down"]
    deltas, new_m, new_v = {}, {}, {}
    for nm in order:
        if nm in big_out:
            grads[nm], deltas[nm], new_m[nm], new_v[nm] = big_out[nm]
        else:
            w, m, v = weights[nm]
            deltas[nm], new_m[nm], new_v[nm] = _adamw_small(w, grads[nm], m, v, name="adamw_" + nm)
    return (loss, grad_x, *[grads[nm] for nm in order], *[deltas[nm] for nm in order],
            *[new_m[nm] for nm in order], *[new_v[nm] for nm in order])
```

```python
import functools
import math

import jax
import jax.numpy as jnp
import numpy as np
from jax import lax
from jax.experimental import pallas as pl
from jax.experimental.pallas import tpu as pltpu

F32 = jnp.float32
BF16 = jnp.bfloat16
MESH = pl.DeviceIdType.MESH

HEAD_DIM = 64
N_KV_HEADS = 4
ROPE_DIM = 16
ROPE_THETA = 500000.0
CHUNK = 128
GMLP_GROUPS = 8
RMS_EPS = 1e-6
LN_EPS = 1e-5
NEG_INF = -1e30
ADAM_LR = 0.001
ADAM_B1 = 0.9
ADAM_B2 = 0.999
ADAM_EPS = 1e-08
ADAM_WD = 0.01
ADAM_STEP = 10

N_CHIPS = 4
LANES = 128
VMEM_CAP = 58 * 1024 * 1024


def _vmem(est_bytes):
    assert est_bytes < VMEM_CAP
    return VMEM_CAP


def _pick(n, cands):
    for c in cands:
        if c <= n and n % c == 0:
            return c
    return n


def _nbytes(shape, dtype):
    return int(np.prod(shape)) * jnp.dtype(dtype).itemsize


MATMUL_VMEM_BUDGET = 48 * 1024 * 1024
MXU_COLS = 256


def _halvings(n, unit):
    out, t = [], n
    while t % unit == 0 and t >= unit:
        out.append(t)
        if t % 2:
            break
        t //= 2
    return out


def _matmul_tiles(P, Q, R, a_bytes, b_bytes, o_bytes, full_addend, tp, tq, tr):
    step_us, bytes_per_us, flops_per_us = 0.85, 3.2e6, 9.0e8
    best = None
    for p in ([tp] if tp else _halvings(P, LANES)):
        for q in ([tq] if tq else _halvings(Q, LANES)):
            for r in ([tr] if tr else _halvings(R, LANES)):
                nk = R // r
                vm = 2 * (p * r * a_bytes + r * q * b_bytes + p * q * o_bytes + (p * q * 4 if full_addend else 0))
                vm += p * q * 4 * (2 if nk > 1 else 1)
                if vm > MATMUL_VMEM_BUDGET:
                    continue
                exposed = (p * r * a_bytes + r * q * b_bytes + p * q * o_bytes) / bytes_per_us
                mxu_us = 2.0 * P * R * (Q // q) * (-(-q // MXU_COLS) * MXU_COLS) / flops_per_us
                key = ((P // p) * (Q // q) * nk * step_us + exposed + mxu_us, nk, abs(p - q))
                if best is None or key < best[0]:
                    best = (key, (p, q, r))
    assert best is not None, (P, Q, R)
    return best[1]


def _matmul(a, b, *, mode, out_dtype, name, bias=None, after=()):
    if mode == "nn":
        (P, R), (R2, Q) = a.shape, b.shape
    elif mode == "nt":
        (P, R), (Q, R2) = a.shape, b.shape
    else:
        (R, P), (R2, Q) = a.shape, b.shape
    assert R == R2, (mode, a.shape, b.shape)
    tp, tq, tr = _matmul_tiles(P, Q, R, a.dtype.itemsize, b.dtype.itemsize, jnp.dtype(out_dtype).itemsize, False,
                               None, None, None)
    nk = R // tr
    dims = {"nn": (((1,), (0,)), ((), ())), "nt": (((1,), (1,)), ((), ())), "tn": (((0,), (0,)), ((), ()))}[mode]
    if mode == "nn":
        a_spec = pl.BlockSpec((tp, tr), lambda i, j, k: (i, k))
        b_spec = pl.BlockSpec((tr, tq), lambda i, j, k: (k, j))
    elif mode == "nt":
        a_spec = pl.BlockSpec((tp, tr), lambda i, j, k: (i, k))
        b_spec = pl.BlockSpec((tq, tr), lambda i, j, k: (j, k))
    else:
        a_spec = pl.BlockSpec((tr, tp), lambda i, j, k: (k, i))
        b_spec = pl.BlockSpec((tr, tq), lambda i, j, k: (k, j))
    in_specs = [a_spec, b_spec]
    args = [a, b]
    has_bias = bias is not None
    if has_bias:
        in_specs.append(pl.BlockSpec((1, tq), lambda i, j, k: (0, j)))
        args.append(bias)
    out_shape = jax.ShapeDtypeStruct((P, Q), out_dtype)
    out_spec = pl.BlockSpec((tp, tq), lambda i, j, k: (i, j))
    n_in = len(args) + len(after)
    in_specs += [pl.BlockSpec(memory_space=pl.ANY)] * len(after)
    args += list(after)

    def body(*refs):
        a_ref, b_ref = refs[0], refs[1]
        bias_ref = refs[2] if has_bias else None
        o_ref = refs[n_in]
        acc_ref = refs[n_in + 1] if nk > 1 else None
        part = lax.dot_general(a_ref[...], b_ref[...], dims, preferred_element_type=F32)

        def finish(acc):
            if has_bias:
                acc = acc + bias_ref[...]
            o_ref[...] = acc.astype(out_dtype)

        if nk == 1:
            finish(part)
        else:
            k = pl.program_id(2)

            @pl.when(k == 0)
            def _():
                acc_ref[...] = part

            @pl.when(k > 0)
            def _():
                acc_ref[...] += part

            @pl.when(k == nk - 1)
            def _():
                finish(acc_ref[...])

    est = 2 * (_nbytes((tp, tr), a.dtype) + _nbytes((tr, tq), b.dtype) + _nbytes((tp, tq), out_dtype)) + 3 * tp * tq * 4
    return pl.pallas_call(
        body, name=name, out_shape=out_shape,
        grid=(P // tp, Q // tq, nk),
        in_specs=in_specs, out_specs=out_spec,
        scratch_shapes=[pltpu.VMEM((tp, tq), F32)] if nk > 1 else [],
        compiler_params=pltpu.CompilerParams(
            dimension_semantics=("parallel", "parallel", "arbitrary"), vmem_limit_bytes=_vmem(est)),
    )(*args)


def _matmul_pair(a, b, pair, *, mode, out_dtype, name, after=()):
    if mode == "tn":
        (R, P), (R2, Q) = a.shape, b.shape
        assert R == R2 and pair.shape == b.shape
        tp, tq, tr = _matmul_tiles(P, Q, R, a.dtype.itemsize, 2 * b.dtype.itemsize,
                                   jnp.dtype(out_dtype).itemsize, False, None, None, None)
        nq, nk = Q // tq, R // tr
        grid, nk_total = (P // tp, 2 * nq, nk), nk
        a_spec = pl.BlockSpec((tr, tp), lambda i, j, k: (k, i))
        b_spec = pl.BlockSpec((tr, tq), lambda i, j, k: (jnp.where(j < nq, k, nk - 1), jnp.minimum(j, nq - 1)))
        p_spec = pl.BlockSpec((tr, tq), lambda i, j, k: (jnp.where(j >= nq, k, 0), jnp.maximum(j - nq, 0)))
        out_shape = (P, 2 * Q)
        dims = (((0,), (0,)), ((), ()))
    else:
        assert mode == "nt"
        (P, R), (Q, R2) = a.shape, b.shape
        assert R2 == 2 * R and pair.shape == a.shape
        tp, tq, tr = _matmul_tiles(P, Q, R, 2 * a.dtype.itemsize, b.dtype.itemsize,
                                   jnp.dtype(out_dtype).itemsize, False, None, None, None)
        nk = R // tr
        grid, nk_total = (P // tp, Q // tq, 2 * nk), 2 * nk
        a_spec = pl.BlockSpec((tp, tr), lambda i, j, k: (i, jnp.minimum(k, nk - 1)))
        p_spec = pl.BlockSpec((tp, tr), lambda i, j, k: (i, jnp.maximum(k - nk, 0)))
        b_spec = pl.BlockSpec((tq, tr), lambda i, j, k: (j, k))
        out_shape = (P, Q)
        dims = (((1,), (1,)), ((), ()))
    n_after = len(after)

    def body(a_ref, b_ref, p_ref, *rest):
        o_ref = rest[n_after]
        acc_ref = rest[n_after + 1] if nk_total > 1 else None
        j, k = pl.program_id(1), pl.program_id(2)

        def step(l_ref, r_ref):
            part = lax.dot_general(l_ref[...], r_ref[...], dims, preferred_element_type=F32)
            if nk_total == 1:
                o_ref[...] = part.astype(out_dtype)
                return

            @pl.when(k == 0)
            def _():
                acc_ref[...] = part

            @pl.when(k > 0)
            def _():
                acc_ref[...] += part

            @pl.when(k == nk_total - 1)
            def _():
                o_ref[...] = acc_ref[...].astype(out_dtype)

        first = (j < nq) if mode == "tn" else (k < nk)

        @pl.when(first)
        def _():
            step(a_ref, b_ref)

        @pl.when(jnp.logical_not(first))
        def _():
            step(a_ref if mode == "tn" else p_ref, p_ref if mode == "tn" else b_ref)

    n_a, n_b = (1, 2) if mode == "tn" else (2, 1)
    est = (2 * (n_a * _nbytes((tp, tr), a.dtype) + n_b * _nbytes((tr, tq), b.dtype) + _nbytes((tp, tq), out_dtype))
           + 2 * tp * tq * 4)
    return pl.pallas_call(
        body, name=name, out_shape=jax.ShapeDtypeStruct(out_shape, out_dtype), grid=grid,
        in_specs=[a_spec, b_spec, p_spec] + [pl.BlockSpec(memory_space=pl.ANY)] * n_after,
        out_specs=pl.BlockSpec((tp, tq), lambda i, j, k: (i, j)),
        scratch_shapes=[pltpu.VMEM((tp, tq), F32)] if nk_total > 1 else [],
        compiler_params=pltpu.CompilerParams(
            dimension_semantics=("parallel", "parallel", "arbitrary"), vmem_limit_bytes=_vmem(est)),
    )(a, b, pair, *after)


def _row_call(body, ins, outs, *, name, rows, tr, acc_outs=(), est=0, after=()):
    in_specs, args = [], []
    for arr, kind in ins:
        if kind == "row":
            in_specs.append(pl.BlockSpec((tr, arr.shape[1]), lambda i: (i, 0)))
        elif isinstance(arr, tuple):
            arr, layer = arr
            in_specs.append(pl.BlockSpec((None,) + arr.shape[1:], lambda i, layer=layer: (layer, 0, 0)))
        else:
            nd = arr.ndim
            in_specs.append(pl.BlockSpec(arr.shape, lambda i, nd=nd: (0,) * nd))
        args.append(arr)
    n_ins = len(args)
    in_specs += [pl.BlockSpec(memory_space=pl.ANY)] * len(after)
    args += list(after)

    def kernel_fn(*refs):
        body(*refs[:n_ins], *refs[n_ins + len(after):])

    out_shapes = [jax.ShapeDtypeStruct(s, d) for s, d in outs] + [jax.ShapeDtypeStruct(s, d) for s, d in acc_outs]
    out_specs = [pl.BlockSpec((tr, s[1]), lambda i: (i, 0)) for s, _ in outs]
    out_specs += [pl.BlockSpec(s, lambda i, nd=len(s): (0,) * nd) for s, _ in acc_outs]
    res = pl.pallas_call(
        kernel_fn, name=name, out_shape=out_shapes, grid=(rows // tr,), in_specs=in_specs, out_specs=out_specs,
        compiler_params=pltpu.CompilerParams(dimension_semantics=("arbitrary",), vmem_limit_bytes=_vmem(est)),
    )(*args)
    return res


def _rms_fwd(x, g, *, out_dtype, name, after=()):
    T, D = x.shape
    tr = _pick(T, (512, 256, 128))

    def body(x_ref, g_ref, o_ref):
        xv = x_ref[...]
        r = lax.rsqrt(jnp.mean(xv * xv, axis=-1, keepdims=True) + RMS_EPS)
        o_ref[...] = (xv * r * g_ref[...]).astype(out_dtype)

    return _row_call(body, [(x, "row"), (g, "full")], [((T, D), out_dtype)], name=name, rows=T, tr=tr,
                     est=8 * tr * D * 4, after=after)[0]


def _rms_res(h, y, g, *, name):
    T, D = h.shape
    tr = _pick(T, (512, 256, 128))

    def body(h_ref, y_ref, g_ref, o_ref):
        yv = y_ref[...]
        r = lax.rsqrt(jnp.mean(yv * yv, axis=-1, keepdims=True) + RMS_EPS)
        o_ref[...] = h_ref[...] + yv * r * g_ref[...]

    return _row_call(body, [(h, "row"), (y, "row"), (g, "full")], [((T, D), F32)], name=name, rows=T, tr=tr,
                     est=10 * tr * D * 4)[0]


def _rms_bwd(x, g, dy, dres, *, out_dtype, name, after=()):
    T, D = x.shape
    tr = _pick(T, (512, 256, 128))
    has_res = dres is not None

    def body(*refs):
        if has_res:
            x_ref, g_ref, dy_ref, dr_ref, dx_ref, dg_ref = refs
        else:
            x_ref, g_ref, dy_ref, dx_ref, dg_ref = refs
        xv = x_ref[...]
        r = lax.rsqrt(jnp.mean(xv * xv, axis=-1, keepdims=True) + RMS_EPS)
        xhat = xv * r
        dyv = dy_ref[...].astype(F32)
        dxn = dyv * g_ref[...]
        dx = r * (dxn - xhat * jnp.mean(dxn * xhat, axis=-1, keepdims=True))
        if has_res:
            dx = dx + dr_ref[...]
        dx_ref[...] = dx.astype(out_dtype)
        part = jnp.sum(dyv * xhat, axis=0, keepdims=True)

        @pl.when(pl.program_id(0) == 0)
        def _():
            dg_ref[...] = part

        @pl.when(pl.program_id(0) > 0)
        def _():
            dg_ref[...] += part

    ins = [(x, "row"), (g, "full"), (dy, "row")] + ([(dres, "row")] if has_res else [])
    dx, dg = _row_call(body, ins, [((T, D), out_dtype)], name=name, rows=T, tr=tr, acc_outs=[((1, D), F32)],
                       est=12 * tr * D * 4, after=after)
    return dx, dg


def _rms_res_norm(h, y, g_res, g_next, *, name, after=()):
    T, D = h.shape
    tr = _pick(T, (512, 256, 128))

    def body(h_ref, y_ref, g_ref, gn_ref, o_ref, n_ref):
        yv = y_ref[...]
        r = lax.rsqrt(jnp.mean(yv * yv, axis=-1, keepdims=True) + RMS_EPS)
        h2 = h_ref[...] + yv * r * g_ref[...]
        o_ref[...] = h2
        r2 = lax.rsqrt(jnp.mean(h2 * h2, axis=-1, keepdims=True) + RMS_EPS)
        n_ref[...] = (h2 * r2 * gn_ref[...]).astype(BF16)

    return _row_call(body, [(h, "row"), (y, "row"), (g_res, "full"), (g_next, "full")],
                     [((T, D), F32), ((T, D), BF16)], name=name, rows=T, tr=tr, est=12 * tr * D * 4, after=after)


def _rms_bwd_chain(x1, g1, dy1, dres, x2, g2, *, name, after=()):
    T, D = x1.shape
    tr = _pick(T, (512, 256, 128))

    def one(xv, gv, dyv):
        r = lax.rsqrt(jnp.mean(xv * xv, axis=-1, keepdims=True) + RMS_EPS)
        xhat = xv * r
        dxn = dyv * gv
        dx = r * (dxn - xhat * jnp.mean(dxn * xhat, axis=-1, keepdims=True))
        return dx, jnp.sum(dyv * xhat, axis=0, keepdims=True)

    def body(x1_ref, g1_ref, dy1_ref, dr_ref, x2_ref, g2_ref, d1_ref, d2_ref, dg1_ref, dg2_ref):
        dx1, p1 = one(x1_ref[...], g1_ref[...], dy1_ref[...].astype(F32))
        d1 = dx1 + dr_ref[...]
        d1_ref[...] = d1
        dx2, p2 = one(x2_ref[...], g2_ref[...], d1)
        d2_ref[...] = dx2.astype(BF16)

        @pl.when(pl.program_id(0) == 0)
        def _():
            dg1_ref[...] = p1
            dg2_ref[...] = p2

        @pl.when(pl.program_id(0) > 0)
        def _():
            dg1_ref[...] += p1
            dg2_ref[...] += p2

    ins = [(x1, "row"), (g1, "full"), (dy1, "row"), (dres, "row"), (x2, "row"), (g2, "full")]
    return _row_call(body, ins, [((T, D), F32), ((T, D), BF16)], name=name, rows=T, tr=tr,
                     acc_outs=[((1, D), F32), ((1, D), F32)], est=20 * tr * D * 4, after=after)


def _ffn_up(fn, w_gu, l, *, name):
    T, D = fn.shape
    H = w_gu.shape[2] // 2
    tp = _pick(T, (256, 128))
    tq = H
    nj = H // tq

    def body(a_ref, wg_ref, wu_ref, g_ref, u_ref, act_ref):
        a = a_ref[...]
        g = jnp.dot(a, wg_ref[...], preferred_element_type=F32)
        u = jnp.dot(a, wu_ref[...], preferred_element_type=F32)
        sg = jax.nn.sigmoid(g)
        silu = g * sg
        g_ref[...] = (u * (sg + silu * (1.0 - sg))).astype(BF16)
        u_ref[...] = silu.astype(BF16)
        act_ref[...] = (silu * u).astype(BF16)

    tile = pl.BlockSpec((tp, tq), lambda j, i: (i, j))
    est = 2 * (tp * D * 2 + 2 * D * tq * 2 + 3 * tp * tq * 2) + 4 * tp * tq * 4
    return pl.pallas_call(
        body, name=name,
        out_shape=[jax.ShapeDtypeStruct((T, H), BF16), jax.ShapeDtypeStruct((T, H), BF16),
                   jax.ShapeDtypeStruct((T, H), BF16)],
        grid=(nj, T // tp),
        in_specs=[pl.BlockSpec((tp, D), lambda j, i: (i, 0)),
                  pl.BlockSpec((None, D, tq), lambda j, i: (l, 0, j)),
                  pl.BlockSpec((None, D, tq), lambda j, i: (l, 0, j + nj))],
        out_specs=[tile, tile, tile],
        compiler_params=pltpu.CompilerParams(dimension_semantics=("parallel", "parallel"),
                                             vmem_limit_bytes=_vmem(est)),
    )(fn, w_gu, w_gu)


def _ffn_down_dx(df, w_down, l, g, u, after, *, name):
    T, D = df.shape
    H = w_down.shape[1]
    tp = _pick(T, (512, 256, 128))
    tq = H

    def body(a_ref, w_ref, g_ref, u_ref, _, dg_ref, du_ref):
        da = lax.dot_general(a_ref[...], w_ref[...], (((1,), (1,)), ((), ())), preferred_element_type=F32)
        dg_ref[...] = (da * g_ref[...].astype(F32)).astype(BF16)
        du_ref[...] = (da * u_ref[...].astype(F32)).astype(BF16)

    tile = pl.BlockSpec((tp, tq), lambda j, i: (i, j))
    est = 2 * (tp * D * 2 + tq * D * 2 + 4 * tp * tq * 2) + 3 * tp * tq * 4
    return pl.pallas_call(
        body, name=name,
        out_shape=[jax.ShapeDtypeStruct((T, H), BF16), jax.ShapeDtypeStruct((T, H), BF16)],
        grid=(H // tq, T // tp),
        in_specs=[pl.BlockSpec((tp, D), lambda j, i: (i, 0)),
                  pl.BlockSpec((None, tq, D), lambda j, i: (l, j, 0)), tile, tile,
                  pl.BlockSpec(memory_space=pl.ANY)],
        out_specs=[tile, tile],
        compiler_params=pltpu.CompilerParams(dimension_semantics=("parallel", "parallel"),
                                             vmem_limit_bytes=_vmem(est)),
    )(df, w_down, g, u, after)


def _loss_and_grad(y, target, x, g, *, name):
    T, D = y.shape
    tr = _pick(T, (512, 256, 128))

    def body(y_ref, t_ref, x_ref, g_ref, dy_ref, dx_ref, l_ref, dg_ref):
        e = y_ref[...] - t_ref[...]
        dy = e * (1.0 / D)
        dy_ref[...] = dy
        part = jnp.sum(jnp.sum(e * e, axis=1, keepdims=True), axis=0, keepdims=True) * (0.5 / D)
        xv = x_ref[...]
        r = lax.rsqrt(jnp.mean(xv * xv, axis=-1, keepdims=True) + RMS_EPS)
        xhat = xv * r
        dxn = dy * g_ref[...]
        dx_ref[...] = (r * (dxn - xhat * jnp.mean(dxn * xhat, axis=-1, keepdims=True))).astype(BF16)
        dg = jnp.sum(dy * xhat, axis=0, keepdims=True)

        @pl.when(pl.program_id(0) == 0)
        def _():
            l_ref[...] = part
            dg_ref[...] = dg

        @pl.when(pl.program_id(0) > 0)
        def _():
            l_ref[...] += part
            dg_ref[...] += dg

    dy, dx, l, dg = _row_call(body, [(y, "row"), (target, "row"), (x, "row"), (g, "full")],
                              [((T, D), F32), ((T, D), BF16)], name=name, rows=T, tr=tr,
                              acc_outs=[((1, 1), F32), ((1, D), F32)], est=14 * tr * D * 4)
    return dy, dx, l, dg


_SQRT_HALF = 0.7071067811865476
_INV_SQRT_2PI = 0.3989422804014327


def _gelu_parts(x):
    cdf = 0.5 * (1.0 + lax.erf(x * _SQRT_HALF))
    return cdf


def _sgu_common(pre, lng, lnb, W):
    cdf = _gelu_parts(pre)
    z = pre * cdf
    u = z[:, :W]
    v = z[:, W:]
    mu = jnp.mean(v, axis=-1, keepdims=True)
    vc = v - mu
    var = jnp.mean(vc * vc, axis=-1, keepdims=True)
    rstd = lax.rsqrt(var + LN_EPS)
    vhat = vc * rstd
    vn = vhat * lng + lnb
    return cdf, u, vhat, rstd, vn


def _causal_mask():
    t = lax.broadcasted_iota(jnp.int32, (CHUNK, CHUNK), 0)
    s = lax.broadcasted_iota(jnp.int32, (CHUNK, CHUNK), 1)
    return t >= s


def _sgu_fwd(pre, lng, lnb, ws, bsT, *, name):
    T, W2 = pre.shape
    W = W2 // 2
    G = ws.shape[0]
    gd = W // G

    def body(pre_ref, lng_ref, lnb_ref, ws_ref, bs_ref, o_ref):
        _, u, _, _, vn = _sgu_common(pre_ref[...], lng_ref[...], lnb_ref[...], W)
        vnb = vn.astype(BF16)
        causal = _causal_mask()
        for g in range(G):
            w = jnp.where(causal, ws_ref[g], 0.0).astype(BF16)
            sv = jnp.dot(w, vnb[:, g * gd:(g + 1) * gd], preferred_element_type=F32) + bs_ref[:, g:g + 1]
            o_ref[:, g * gd:(g + 1) * gd] = (u[:, g * gd:(g + 1) * gd] * sv).astype(BF16)

    return pl.pallas_call(
        body, name=name, out_shape=jax.ShapeDtypeStruct((T, W), BF16), grid=(T // CHUNK,),
        in_specs=[pl.BlockSpec((CHUNK, W2), lambda i: (i, 0)),
                  pl.BlockSpec((1, W), lambda i: (0, 0)), pl.BlockSpec((1, W), lambda i: (0, 0)),
                  pl.BlockSpec(ws.shape, lambda i: (0, 0, 0)), pl.BlockSpec(bsT.shape, lambda i: (0, 0))],
        out_specs=pl.BlockSpec((CHUNK, W), lambda i: (i, 0)),
        compiler_params=pltpu.CompilerParams(dimension_semantics=("arbitrary",),
                                             vmem_limit_bytes=_vmem(12 * CHUNK * W2 * 4)),
    )(pre, lng, lnb, ws, bsT)


def _sgu_bwd(pre, dgated, lng, lnb, ws, bsT, *, name):
    T, W2 = pre.shape
    W = W2 // 2
    G = ws.shape[0]
    gd = W // G

    def body(pre_ref, dgt_ref, lng_ref, lnb_ref, ws_ref, bs_ref,
             dpre_ref, dws_ref, dbs_ref, dlng_ref, dlnb_ref, dbin_ref):
        first = pl.program_id(0) == 0

        @pl.when(first)
        def _():
            dws_ref[...] = jnp.zeros_like(dws_ref)
            dbs_ref[...] = jnp.zeros_like(dbs_ref)
            dlng_ref[...] = jnp.zeros_like(dlng_ref)
            dlnb_ref[...] = jnp.zeros_like(dlnb_ref)
            dbin_ref[...] = jnp.zeros_like(dbin_ref)

        pre_v = pre_ref[...]
        lng_v = lng_ref[...]
        cdf, u, vhat, rstd, vn = _sgu_common(pre_v, lng_v, lnb_ref[...], W)
        vnb = vn.astype(BF16)
        dgt = dgt_ref[...].astype(F32)
        causal = _causal_mask()
        du_parts, dvn_parts = [], []
        for g in range(G):
            sl = slice(g * gd, (g + 1) * gd)
            w = jnp.where(causal, ws_ref[g], 0.0).astype(BF16)
            sv = jnp.dot(w, vnb[:, sl], preferred_element_type=F32) + bs_ref[:, g:g + 1]
            dgt_g = dgt[:, sl]
            du_parts.append(dgt_g * sv)
            dsv = dgt_g * u[:, sl]
            dsvb = dsv.astype(BF16)
            dvn_parts.append(lax.dot_general(w, dsvb, (((0,), (0,)), ((), ())), preferred_element_type=F32))
            dw = lax.dot_general(dsvb, vnb[:, sl], (((1,), (1,)), ((), ())), preferred_element_type=F32)
            dws_ref[g] += jnp.where(causal, dw, 0.0)
            dbs_ref[:, g:g + 1] += jnp.sum(dsv, axis=1, keepdims=True)
        du = jnp.concatenate(du_parts, axis=1)
        dvn = jnp.concatenate(dvn_parts, axis=1)
        dlng_ref[...] += jnp.sum(dvn * vhat, axis=0, keepdims=True)
        dlnb_ref[...] += jnp.sum(dvn, axis=0, keepdims=True)
        dvh = dvn * lng_v
        dv = rstd * (dvh - jnp.mean(dvh, axis=-1, keepdims=True)
                     - vhat * jnp.mean(dvh * vhat, axis=-1, keepdims=True))
        dz = jnp.concatenate([du, dv], axis=1)
        dgelu = cdf + pre_v * jnp.exp(-0.5 * pre_v * pre_v) * _INV_SQRT_2PI
        dpre = dz * dgelu
        dbin_ref[...] += jnp.sum(dpre, axis=0, keepdims=True)
        dpre_ref[...] = dpre.astype(BF16)

    full = lambda shape: pl.BlockSpec(shape, lambda i, nd=len(shape): (0,) * nd)
    return pl.pallas_call(
        body, name=name,
        out_shape=[jax.ShapeDtypeStruct((T, W2), BF16), jax.ShapeDtypeStruct(ws.shape, F32),
                   jax.ShapeDtypeStruct(bsT.shape, F32), jax.ShapeDtypeStruct((1, W), F32),
                   jax.ShapeDtypeStruct((1, W), F32), jax.ShapeDtypeStruct((1, W2), F32)],
        grid=(T // CHUNK,),
        in_specs=[pl.BlockSpec((CHUNK, W2), lambda i: (i, 0)), pl.BlockSpec((CHUNK, W), lambda i: (i, 0)),
                  full((1, W)), full((1, W)), full(ws.shape), full(bsT.shape)],
        out_specs=[pl.BlockSpec((CHUNK, W2), lambda i: (i, 0)), full(ws.shape), full(bsT.shape),
                   full((1, W)), full((1, W)), full((1, W2))],
        compiler_params=pltpu.CompilerParams(dimension_semantics=("arbitrary",),
                                             vmem_limit_bytes=_vmem(24 * CHUNK * W2 * 4)),
    )(pre, dgated, lng, lnb, ws, bsT)


def _rope_tables(positions):
    half = ROPE_DIM // 2
    inv_freq = ROPE_THETA ** (-jnp.arange(0, ROPE_DIM, 2, dtype=F32) / ROPE_DIM)
    ang = positions.astype(F32).reshape(-1, 1) * inv_freq
    cos, sin = jnp.cos(ang), jnp.sin(ang)
    T = ang.shape[0]
    rest = HEAD_DIM - ROPE_DIM
    c64 = jnp.concatenate([cos, cos, jnp.ones((T, rest), F32)], axis=1)
    s64 = jnp.concatenate([-sin, sin, jnp.zeros((T, rest), F32)], axis=1)
    del half
    return jnp.tile(c64, (1, LANES // HEAD_DIM)), jnp.tile(s64, (1, LANES // HEAD_DIM))


def _swap8(x):
    W = x.shape[1]
    half = ROPE_DIM // 2
    lane = lax.broadcasted_iota(jnp.int32, x.shape, 1) % HEAD_DIM
    return jnp.where(lane < half, pltpu.roll(x, W - half, axis=1),
                     jnp.where(lane < ROPE_DIM, pltpu.roll(x, half, axis=1), 0.0))


def _wide(tab, W):
    return jnp.concatenate([tab] * (W // LANES), axis=1) if W > LANES else tab


def _rope_fwd(qkv, ctab, stab, *, q_width, kv_width, name):
    T = qkv.shape[0]
    tr = _pick(T, (256, 128))
    scale = HEAD_DIM ** -0.5

    def body(x_ref, c_ref, s_ref, q_ref, k_ref, v_ref):
        c = c_ref[...]
        s = s_ref[...]
        q = x_ref[:, :q_width]
        k = x_ref[:, q_width:q_width + kv_width]
        q_ref[...] = ((q * _wide(c, q_width) + _swap8(q) * _wide(s, q_width)) * scale).astype(BF16)
        k_ref[...] = (k * _wide(c, kv_width) + _swap8(k) * _wide(s, kv_width)).astype(BF16)
        v_ref[...] = x_ref[:, q_width + kv_width:].astype(BF16)

    return _row_call(body, [(qkv, "row"), (ctab, "row"), (stab, "row")],
                     [((T, q_width), BF16), ((T, kv_width), BF16), ((T, kv_width), BF16)],
                     name=name, rows=T, tr=tr, est=10 * tr * qkv.shape[1] * 4)


_NT = (((1,), (1,)), ((), ()))
_TN = (((0,), (0,)), ((), ()))


def _group_rows(ref, heads):
    return jnp.concatenate([ref[:, h * HEAD_DIM:(h + 1) * HEAD_DIM] for h in heads], axis=0)


def _attn_valid(grp):
    qi = np.arange(grp * CHUNK)[:, None] % CHUNK
    sj = np.arange(2 * CHUNK)[None, :]
    cur = (sj >= CHUNK) & (sj - CHUNK <= qi)
    prev = (sj < CHUNK) & (sj > qi)
    return jnp.asarray(np.stack([cur, cur | prev]).astype(np.float32))


def _valid_spec(grp):
    return pl.BlockSpec((None, grp * CHUNK, 2 * CHUNK), lambda n: (jnp.minimum(n, 1), 0, 0))


def _attn_group_probs(q, kk, sinks, valid, grp):
    rows = grp * CHUNK
    s = lax.dot_general(q, kk, _NT, preferred_element_type=F32)
    s = jnp.where(valid, s, NEG_INF)
    r = lax.broadcasted_iota(jnp.int32, (rows, 1), 0)
    sink = jnp.full((rows, 1), sinks[grp - 1], F32)
    for g in range(grp - 2, -1, -1):
        sink = jnp.where(r < (g + 1) * CHUNK, sinks[g], sink)
    m = jnp.maximum(jnp.max(s, axis=1, keepdims=True), sink)
    p = jnp.exp(s - m)
    ps = jnp.exp(sink - m)
    inv = 1.0 / (jnp.sum(p, axis=1, keepdims=True) + ps)
    return p * inv, ps * inv


def _kv_specs(width, nb):
    prev = pl.BlockSpec((CHUNK, width), lambda n: (jnp.maximum(n - 1, 0), 0))
    cur = pl.BlockSpec((CHUNK, width), lambda n: (n, 0))
    return prev, cur


def _attn_fwd(qr, kr, vr, sinks, *, name):
    T, QW = qr.shape
    KW = kr.shape[1]
    HQ, HK = QW // HEAD_DIM, KW // HEAD_DIM
    grp = HQ // HK
    nb = T // CHUNK

    def body(q_ref, kp_ref, kc_ref, vp_ref, vc_ref, s_ref, ok_ref, o_ref):
        valid = ok_ref[...] > 0.5
        for kh in range(HK):
            ks = slice(kh * HEAD_DIM, (kh + 1) * HEAD_DIM)
            heads = list(range(kh * grp, (kh + 1) * grp))
            q = _group_rows(q_ref, heads)
            kk = jnp.concatenate([kp_ref[:, ks], kc_ref[:, ks]], axis=0)
            vv = jnp.concatenate([vp_ref[:, ks], vc_ref[:, ks]], axis=0)
            p, _ = _attn_group_probs(q, kk, [s_ref[0, h] for h in heads], valid, grp)
            o = jnp.dot(p.astype(BF16), vv, preferred_element_type=F32).astype(BF16)
            for g, h in enumerate(heads):
                o_ref[:, h * HEAD_DIM:(h + 1) * HEAD_DIM] = o[g * CHUNK:(g + 1) * CHUNK]

    kp, kc = _kv_specs(KW, nb)
    return pl.pallas_call(
        body, name=name, out_shape=jax.ShapeDtypeStruct((T, QW), BF16), grid=(nb,),
        in_specs=[pl.BlockSpec((CHUNK, QW), lambda n: (n, 0)), kp, kc, kp, kc,
                  pl.BlockSpec(memory_space=pltpu.SMEM), _valid_spec(grp)],
        out_specs=pl.BlockSpec((CHUNK, QW), lambda n: (n, 0)),
        compiler_params=pltpu.CompilerParams(dimension_semantics=("arbitrary",), vmem_limit_bytes=_vmem(8 << 20)),
    )(qr, kr, kr, vr, vr, sinks, _attn_valid(grp))


def _attn_bwd(qr, kr, vr, sinks, do, *, name):
    T, QW = qr.shape
    KW = kr.shape[1]
    HQ, HK = QW // HEAD_DIM, KW // HEAD_DIM
    grp = HQ // HK
    nb = T // CHUNK

    def body(q_ref, kp_ref, kc_ref, vp_ref, vc_ref, s_ref, do_ref, ok_ref,
             dq_ref, dkp_ref, dkc_ref, dvp_ref, dvc_ref, ds_ref):
        n = pl.program_id(0)
        valid = ok_ref[...] > 0.5
        lane = lax.broadcasted_iota(jnp.int32, (1, LANES), 1)
        dsink = jnp.zeros((1, LANES), F32)
        for kh in range(HK):
            ks = slice(kh * HEAD_DIM, (kh + 1) * HEAD_DIM)
            heads = list(range(kh * grp, (kh + 1) * grp))
            q = _group_rows(q_ref, heads)
            doh = _group_rows(do_ref, heads)
            kk = jnp.concatenate([kp_ref[:, ks], kc_ref[:, ks]], axis=0)
            vv = jnp.concatenate([vp_ref[:, ks], vc_ref[:, ks]], axis=0)
            p, ps = _attn_group_probs(q, kk, [s_ref[0, h] for h in heads], valid, grp)
            dp = lax.dot_general(doh, vv, _NT, preferred_element_type=F32)
            delta = jnp.sum(p * dp, axis=1, keepdims=True)
            ds = (p * (dp - delta)).astype(BF16)
            dv = lax.dot_general(p.astype(BF16), doh, _TN, preferred_element_type=F32)
            dk = lax.dot_general(ds, q, _TN, preferred_element_type=F32)
            dq = jnp.dot(ds, kk, preferred_element_type=F32)
            psd = ps * delta
            for g, h in enumerate(heads):
                dq_ref[:, h * HEAD_DIM:(h + 1) * HEAD_DIM] = dq[g * CHUNK:(g + 1) * CHUNK]
                dsink = dsink + jnp.where(
                    lane == h, -jnp.sum(psd[g * CHUNK:(g + 1) * CHUNK], axis=0, keepdims=True), 0.0)
            dkp_ref[:, ks] = dk[:CHUNK]
            dkc_ref[:, ks] = dk[CHUNK:]
            dvp_ref[:, ks] = dv[:CHUNK]
            dvc_ref[:, ks] = dv[CHUNK:]

        @pl.when(n == 0)
        def _():
            ds_ref[...] = dsink

        @pl.when(n > 0)
        def _():
            ds_ref[...] += dsink

    kp, kc = _kv_specs(KW, nb)
    qspec = pl.BlockSpec((CHUNK, QW), lambda n: (n, 0))
    kout = pl.BlockSpec((CHUNK, KW), lambda n: (n, 0))
    return pl.pallas_call(
        body, name=name,
        out_shape=[jax.ShapeDtypeStruct((T, QW), F32)] + [jax.ShapeDtypeStruct((T, KW), F32)] * 4
        + [jax.ShapeDtypeStruct((1, LANES), F32)],
        grid=(nb,),
        in_specs=[qspec, kp, kc, kp, kc, pl.BlockSpec(memory_space=pltpu.SMEM), qspec, _valid_spec(grp)],
        out_specs=[qspec, kout, kout, kout, kout, pl.BlockSpec((1, LANES), lambda n: (0, 0))],
        compiler_params=pltpu.CompilerParams(dimension_semantics=("arbitrary",), vmem_limit_bytes=_vmem(12 << 20)),
    )(qr, kr, kr, vr, vr, sinks, do, _attn_valid(grp))


def _rope_bwd(dq, dkp, dkc, dvp, dvc, ctab, stab, *, name):
    T, QW = dq.shape
    KW = dkp.shape[1]
    nb = T // CHUNK
    scale = HEAD_DIM ** -0.5
    width = QW + 2 * KW

    def body(dq_ref, dkc_ref, dkn_ref, dvc_ref, dvn_ref, c_ref, s_ref, o_ref, db_ref):
        n = pl.program_id(0)
        c = c_ref[...]
        s = s_ref[...]
        has_next = (n < nb - 1).astype(F32)
        dqv = dq_ref[...]
        dk = dkc_ref[...] + has_next * dkn_ref[...]
        dv = dvc_ref[...] + has_next * dvn_ref[...]
        dq_pre = (dqv * _wide(c, QW) + _swap8(dqv * _wide(s, QW))) * scale
        dk_pre = dk * _wide(c, KW) + _swap8(dk * _wide(s, KW))
        o_ref[:, :QW] = dq_pre.astype(BF16)
        o_ref[:, QW:QW + KW] = dk_pre.astype(BF16)
        o_ref[:, QW + KW:] = dv.astype(BF16)
        part = jnp.concatenate([jnp.sum(dq_pre, axis=0, keepdims=True), jnp.sum(dk_pre, axis=0, keepdims=True),
                                jnp.sum(dv, axis=0, keepdims=True)], axis=1)

        @pl.when(n == 0)
        def _():
            db_ref[...] = part

        @pl.when(n > 0)
        def _():
            db_ref[...] += part

    cur = lambda w: pl.BlockSpec((CHUNK, w), lambda n: (n, 0))
    nxt = lambda w: pl.BlockSpec((CHUNK, w), lambda n: (jnp.minimum(n + 1, nb - 1), 0))
    return pl.pallas_call(
        body, name=name,
        out_shape=[jax.ShapeDtypeStruct((T, width), BF16), jax.ShapeDtypeStruct((1, width), F32)],
        grid=(nb,),
        in_specs=[cur(QW), cur(KW), nxt(KW), cur(KW), nxt(KW), cur(LANES), cur(LANES)],
        out_specs=[cur(width), pl.BlockSpec((1, width), lambda n: (0, 0))],
        compiler_params=pltpu.CompilerParams(dimension_semantics=("arbitrary",), vmem_limit_bytes=_vmem(8 << 20)),
    )(dq, dkc, dkp, dvc, dvp, ctab, stab)


def _cast_block(w, l, axis, chip_arr, *, name):
    _, Ks, Ns = w.shape
    tk = _pick(Ks, (512, 352, 256, 128))
    nk = Ks // tk
    full = (Ks * N_CHIPS, Ns) if axis == 0 else (Ks, Ns * N_CHIPS)

    def body(p_ref, w_ref, o_ref):
        o_ref[...] = w_ref[...].astype(BF16)

    if axis == 0:
        out_spec = pl.BlockSpec((tk, Ns), lambda i, p: (p[0] * nk + i, 0))
    else:
        out_spec = pl.BlockSpec((tk, Ns), lambda i, p: (i, p[0]))
    grid_spec = pltpu.PrefetchScalarGridSpec(
        num_scalar_prefetch=1, grid=(nk,),
        in_specs=[pl.BlockSpec((None, tk, Ns), lambda i, p: (l, i, 0))], out_specs=out_spec)
    return pl.pallas_call(
        body, name=name, out_shape=jax.ShapeDtypeStruct(full, BF16), grid_spec=grid_spec,
        compiler_params=pltpu.CompilerParams(dimension_semantics=("arbitrary",),
                                             vmem_limit_bytes=_vmem(4 * tk * Ns * 6)),
    )(chip_arr, w)


def _adamw_math(w, g, m, v):
    m = ADAM_B1 * m + (1.0 - ADAM_B1) * g
    v = ADAM_B2 * v + (1.0 - ADAM_B2) * (g * g)
    m_hat = m / (1.0 - ADAM_B1 ** ADAM_STEP)
    v_hat = v / (1.0 - ADAM_B2 ** ADAM_STEP)
    delta = -ADAM_LR * (m_hat / (jnp.sqrt(v_hat) + ADAM_EPS) + ADAM_WD * w)
    return delta, m, v


def _adamw_layer(w, m, v, g, l, outs, *, name):
    _, K, N = w.shape
    tk = _pick(K, (512, 352, 256, 128)) if N <= 1024 else _pick(K, (256, 176, 128))

    def body(w_ref, m_ref, v_ref, g_ref, _g, _d, _m, _v, go_ref, d_ref, mo_ref, vo_ref):
        gv = g_ref[...]
        d, mn, vn = _adamw_math(w_ref[...], gv, m_ref[...], v_ref[...])
        go_ref[...] = gv
        d_ref[...] = d
        mo_ref[...] = mn
        vo_ref[...] = vn

    layer = pl.BlockSpec((None, tk, N), lambda i: (l, i, 0))
    any_spec = pl.BlockSpec(memory_space=pl.ANY)
    sd = jax.ShapeDtypeStruct(w.shape, F32)
    return pl.pallas_call(
        body, name=name, out_shape=[sd, sd, sd, sd], grid=(K // tk,),
        in_specs=[layer, layer, layer, pl.BlockSpec((tk, N), lambda i: (i, 0))] + [any_spec] * 4,
        out_specs=[layer] * 4, input_output_aliases={4: 0, 5: 1, 6: 2, 7: 3},
        compiler_params=pltpu.CompilerParams(dimension_semantics=("arbitrary",),
                                             vmem_limit_bytes=_vmem(2 * 8 * tk * N * 4 + 6 * tk * N * 4)),
    )(w, m, v, g, *outs)


def _adamw_small(w, g, m, v, *, name):
    def body(w_ref, g_ref, m_ref, v_ref, d_ref, mo_ref, vo_ref):
        d, mn, vn = _adamw_math(w_ref[...], g_ref[...], m_ref[...], v_ref[...])
        d_ref[...] = d
        mo_ref[...] = mn
        vo_ref[...] = vn

    sd = jax.ShapeDtypeStruct(w.shape, F32)
    return pl.pallas_call(body, name=name, out_shape=[sd, sd, sd])(w, g, m, v)


def _my_place():
    return lax.axis_index("x"), lax.axis_index("y"), lax.axis_index("c")


def _peer_chips(x, y):
    return [(1 - x, y), (x, 1 - y), (1 - x, 1 - y)]


_HBM = pl.BlockSpec(memory_space=pltpu.HBM)
_SEM = pl.BlockSpec(memory_space=pltpu.SEMAPHORE)
_EFFECT = pltpu.SideEffectType.DATAFLOW_SIDE_EFFECTING


def _split_start(name, bufs, n_copies, make_copies, after):
    nb = len(bufs)

    def body(*refs):
        send_sems, recv_sems = refs[nb + 1], refs[nb + 2]
        token = refs[2 * nb + 3]
        sends, _ = make_copies(refs[:nb], send_sems, recv_sems)
        for cp in sends:
            cp.start()
        token[...] = jnp.zeros_like(token)

    res = pl.pallas_call(
        body, name=name,
        out_shape=(pltpu.SemaphoreType.DMA((n_copies,)), pltpu.SemaphoreType.DMA((n_copies,)),
                   *[pltpu.HBM(b.shape, b.dtype) for b in bufs], jax.ShapeDtypeStruct((8, LANES), F32)),
        in_specs=[_HBM] * nb + [pl.BlockSpec(memory_space=pl.ANY)],
        out_specs=(_SEM, _SEM, *[_HBM] * nb, pl.BlockSpec(memory_space=pltpu.VMEM)),
        input_output_aliases={k: 2 + k for k in range(nb)},
        compiler_params=pltpu.CompilerParams(has_side_effects=_EFFECT),
    )(*[pltpu.with_memory_space_constraint(b, pltpu.HBM) for b in bufs],
      after[0] if isinstance(after, (list, tuple)) else after)
    return res[0], res[1], list(res[2:2 + nb]), res[2 + nb]


def _split_wait(name, bufs, sems, make_copies, after):
    nb = len(bufs)
    after = list(after) if isinstance(after, (list, tuple)) else [after]

    def body(*refs):
        send_sems, recv_sems = refs[nb], refs[nb + 1]
        sends, recvs = make_copies(refs[:nb], send_sems, recv_sems)
        for cp in sends:
            cp.wait_send()
        for cp in recvs:
            cp.wait_recv()

    res = pl.pallas_call(
        body, name=name,
        out_shape=tuple(pltpu.HBM(b.shape, b.dtype) for b in bufs),
        in_specs=[_HBM] * nb + [_SEM, _SEM] + [pl.BlockSpec(memory_space=pl.ANY)] * len(after),
        out_specs=tuple([_HBM] * nb),
        input_output_aliases={k: k for k in range(nb)},
        compiler_params=pltpu.CompilerParams(has_side_effects=_EFFECT),
    )(*bufs, sems[0], sems[1], *after)
    return list(res)


def _remote(src, dst, send_sems, recv_sems, k, target):
    return pltpu.make_async_remote_copy(src_ref=src, dst_ref=dst, send_sem=send_sems.at[k],
                                        recv_sem=recv_sems.at[k], device_id=target, device_id_type=MESH)


def _ag_region(ref, axis, chip, half):
    K, N = ref.shape
    if axis == 0:
        hs = K // N_CHIPS // 2
        assert hs % 16 == 0
        return ref.at[pl.ds(pl.multiple_of((2 * chip + half) * hs, 16), hs), :]
    ns, hk = N // N_CHIPS, K // 2
    assert ns % LANES == 0 and hk % 16 == 0
    return ref.at[pl.ds(pl.multiple_of(half * hk, 16), hk), pl.ds(pl.multiple_of(chip * ns, LANES), ns)]


def _ag_copies(stage, axes):
    n = len(axes)

    def make(bufs, send_sems, recv_sems):
        x, y, c = _my_place()
        me = 2 * x + y
        sends, recvs = [], []
        for j, (px, py) in enumerate(_peer_chips(x, y)):
            other = 2 * px + py
            for w in range(n):
                k = j * n + w
                if stage == 1:
                    src, target = _ag_region(bufs[w], axes[w], me, c), (px, py, c)
                    land = _ag_region(bufs[w], axes[w], other, c)
                else:
                    src, target = _ag_region(bufs[w], axes[w], other, c), (x, y, 1 - c)
                    land = _ag_region(bufs[w], axes[w], other, 1 - c)
                sends.append(_remote(src, src, send_sems, recv_sems, k, target))
                recvs.append(_remote(land, land, send_sems, recv_sems, k, target))
        return sends, recvs

    return make


def _half_shape(shape, axis):
    K, N = shape
    return (K, N // 2) if axis == 0 else (K // 2, N)


def _core_half(ref, axis, half):
    K, N = ref.shape
    if axis == 0:
        return ref.at[:, pl.ds(pl.multiple_of(half * (N // 2), LANES), N // 2)]
    return ref.at[pl.ds(pl.multiple_of(half * (K // 2), 16), K // 2), :]


def _chip_block(ref, axis, chip):
    K, N = ref.shape
    if axis == 0:
        return ref.at[pl.ds(pl.multiple_of(chip * (K // N_CHIPS), 16), K // N_CHIPS), :]
    return ref.at[:, pl.ds(pl.multiple_of(chip * (N // N_CHIPS), LANES), N // N_CHIPS)]


def _rs_sibling_copies(axes):
    n = len(axes)

    def make(bufs, send_sems, recv_sems):
        x, y, c = _my_place()
        sends = [_remote(_core_half(bufs[w], axes[w], 1 - c), bufs[n + w], send_sems, recv_sems, w, (x, y, 1 - c))
                 for w in range(n)]
        recvs = [_remote(bufs[n + w], bufs[n + w], send_sems, recv_sems, w, (x, y, 1 - c)) for w in range(n)]
        return sends, recvs

    return make


def _rs_chip_copies(axes):
    n = len(axes)

    def make(bufs, send_sems, recv_sems):
        x, y, c = _my_place()
        sends, recvs = [], []
        for j, (px, py) in enumerate(_peer_chips(x, y)):
            for w in range(n):
                k = j * n + w
                sends.append(_remote(_chip_block(bufs[w], axes[w], 2 * px + py), bufs[n + w].at[j],
                                     send_sems, recv_sems, k, (px, py, c)))
                recvs.append(_remote(bufs[n + w].at[j], bufs[n + w].at[j], send_sems, recv_sems, k, (px, py, c)))
        return sends, recvs

    return make


def _rs_fill_copies(axes):
    n = len(axes)

    def make(bufs, send_sems, recv_sems):
        x, y, c = _my_place()
        sends = [_remote(_core_half(bufs[w], axes[w], c), _core_half(bufs[w], axes[w], c),
                         send_sems, recv_sems, w, (x, y, 1 - c)) for w in range(n)]
        recvs = [_remote(_core_half(bufs[w], axes[w], 1 - c), _core_half(bufs[w], axes[w], 1 - c),
                         send_sems, recv_sems, w, (x, y, 1 - c)) for w in range(n)]
        return sends, recvs

    return make


def _chip_sum(g, r, axis, place, *, name):
    hk, hn = r.shape
    bk, bn = (hk // N_CHIPS, hn) if axis == 0 else (hk, hn // N_CHIPS)
    tk = _pick(bk, (512, 352, 256, 128))
    nk = bk // tk

    def body(p_ref, g_ref, r_ref, b_ref, own_ref):
        s = g_ref[...].astype(F32) + r_ref[...].astype(F32)
        b_ref[...] = s.astype(BF16)

        @pl.when(pl.program_id(1) == p_ref[0])
        def _():
            own_ref[...] = s

    if axis == 0:
        g_spec = pl.BlockSpec((tk, bn), lambda i, j, p: (j * nk + i, p[1]))
        r_spec = pl.BlockSpec((tk, bn), lambda i, j, p: (j * nk + i, 0))
    else:
        g_spec = pl.BlockSpec((tk, bn), lambda i, j, p: (p[1] * nk + i, j))
        r_spec = pl.BlockSpec((tk, bn), lambda i, j, p: (i, j))
    grid_spec = pltpu.PrefetchScalarGridSpec(
        num_scalar_prefetch=1, grid=(nk, N_CHIPS), in_specs=[g_spec, r_spec],
        out_specs=[r_spec, pl.BlockSpec((tk, bn), lambda i, j, p: (i, 0))])
    return pl.pallas_call(
        body, name=name,
        out_shape=[jax.ShapeDtypeStruct(r.shape, BF16), jax.ShapeDtypeStruct((bk, bn), F32)],
        grid_spec=grid_spec,
        compiler_params=pltpu.CompilerParams(dimension_semantics=("arbitrary", "arbitrary"),
                                             vmem_limit_bytes=_vmem(2 * tk * bn * 10 + 3 * tk * bn * 4)),
    )(place, g, r)


def _final_sum(own, recv, axis, place, *, name):
    _, bk, bn = recv.shape
    tk = _pick(bk, (256, 176, 128))
    nk = bk // tk

    def body(p_ref, o_ref, r_ref, out_ref):
        out_ref[...] = ((o_ref[...] + r_ref[0].astype(F32)) + r_ref[1].astype(F32)) + r_ref[2].astype(F32)

    own_spec = pl.BlockSpec((tk, bn), lambda i, p: (i, 0))
    if axis == 0:
        out_shape, out_spec = (bk, 2 * bn), pl.BlockSpec((tk, bn), lambda i, p: (i, p[1]))
    else:
        out_shape, out_spec = (2 * bk, bn), pl.BlockSpec((tk, bn), lambda i, p: (p[1] * nk + i, 0))
    grid_spec = pltpu.PrefetchScalarGridSpec(
        num_scalar_prefetch=1, grid=(nk,),
        in_specs=[own_spec, pl.BlockSpec((3, tk, bn), lambda i, p: (0, i, 0))], out_specs=out_spec)
    return pl.pallas_call(
        body, name=name, out_shape=jax.ShapeDtypeStruct(out_shape, F32), grid_spec=grid_spec,
        compiler_params=pltpu.CompilerParams(dimension_semantics=("arbitrary",),
                                             vmem_limit_bytes=_vmem(2 * tk * bn * 14 + 4 * tk * bn * 4)),
    )(place, own, recv)


def _allreduce_small(p, after=()):
    n_after = len(after)

    def body(*refs):
        p_ref = refs[0]
        o_ref, r0, r1, r2, send_sems, recv_sems = refs[1 + n_after:]
        x, y, c = _my_place()
        o_ref[...] = p_ref[...]
        for s, (peer, rbuf) in enumerate([((x, y, 1 - c), r0), ((1 - x, y, c), r1), ((x, 1 - y, c), r2)]):
            cp = pltpu.make_async_remote_copy(src_ref=o_ref, dst_ref=rbuf, send_sem=send_sems.at[s],
                                              recv_sem=recv_sems.at[s], device_id=peer, device_id_type=MESH)
            cp.start()
            cp.wait()
            o_ref[...] = o_ref[...] + rbuf[...]

    vm = pl.BlockSpec(memory_space=pltpu.VMEM)
    return pl.pallas_call(
        body, name="allreduce_small", out_shape=jax.ShapeDtypeStruct(p.shape, F32),
        in_specs=[vm] + [pl.BlockSpec(memory_space=pl.ANY)] * n_after, out_specs=vm,
        scratch_shapes=[pltpu.VMEM(p.shape, F32)] * 3 + [pltpu.SemaphoreType.DMA((3,))] * 2,
        compiler_params=pltpu.CompilerParams(vmem_limit_bytes=_vmem(6 * _nbytes(p.shape, F32))),
    )(p, *after)


def _pack_rows(parts):
    rows, metas = [], []
    for a in parts:
        flat = a.reshape(-1)
        nrow = -(-flat.shape[0] // LANES)
        nrow = -(-nrow // 8) * 8
        flat = jnp.pad(flat, (0, nrow * LANES - flat.shape[0]))
        rows.append(flat.reshape(nrow, LANES))
        metas.append((a.shape, nrow))
    return jnp.concatenate(rows, axis=0), metas


def _unpack_rows(packed, metas):
    out, r0 = [], 0
    for shape, nrow in metas:
        size = int(np.prod(shape))
        out.append(packed[r0:r0 + nrow].reshape(-1)[:size].reshape(shape))
        r0 += nrow
    return out


def kernel(x, positions, pre_mix_g, post_mix_g, pre_ffn_g, post_ffn_g, a_w_in, a_b_in, a_ln_g, a_ln_b, a_w_s, a_b_s, a_w_out, b_w_qkv, b_b_qkv, b_sinks, b_w_o, ffn_w_gu, ffn_w_down, loss_target, m_pre_mix_g, m_post_mix_g, m_pre_ffn_g, m_post_ffn_g, m_a_w_in, m_a_b_in, m_a_ln_g, m_a_ln_b, m_a_w_s, m_a_b_s, m_a_w_out, m_b_w_qkv, m_b_b_qkv, m_b_sinks, m_b_w_o, m_ffn_w_gu, m_ffn_w_down, v_pre_mix_g, v_post_mix_g, v_pre_ffn_g, v_post_ffn_g, v_a_w_in, v_a_b_in, v_a_ln_g, v_a_ln_b, v_a_w_s, v_a_b_s, v_a_w_out, v_b_w_qkv, v_b_b_qkv, v_b_sinks, v_b_w_o, v_ffn_w_gu, v_ffn_w_down):
    depth, D = pre_mix_g.shape
    xi, yi, ci = _my_place()
    chip = 2 * xi + yi
    place = jnp.stack([chip, ci]).astype(jnp.int32)

    stacked = {"a_w_in": (a_w_in, m_a_w_in, v_a_w_in), "a_w_out": (a_w_out, m_a_w_out, v_a_w_out),
               "b_w_qkv": (b_w_qkv, m_b_w_qkv, v_b_w_qkv), "b_w_o": (b_w_o, m_b_w_o, v_b_w_o),
               "ffn_w_gu": (ffn_w_gu, m_ffn_w_gu, v_ffn_w_gu), "ffn_w_down": (ffn_w_down, m_ffn_w_down, v_ffn_w_down)}
    cut = {"a_w_in": 1, "a_w_out": 0, "b_w_qkv": 1, "b_w_o": 0, "ffn_w_gu": 1, "ffn_w_down": 0}

    def layer_keys(i):
        mix = [("a_w_in", i // 2), ("a_w_out", i // 2)] if i % 2 == 0 else [("b_w_qkv", i // 2), ("b_w_o", i // 2)]
        return mix + [("ffn_w_gu", i), ("ffn_w_down", i)]

    def dep(a, toks):
        for t in toks:
            a = a + t[:1, :1]
        return a

    W = {}
    for i in range(depth):
        for nm, l in layer_keys(i):
            W[(nm, l)] = _cast_block(stacked[nm][0], l, cut[nm], place, name=f"cast_{nm}_{l}")

    def gather(tag, keys, after):
        axes = [cut[nm] for nm, _ in keys]
        for stage in (1, 2):
            ss, rs, bufs, tok = _split_start(f"ag{stage}_start_{tag}", [W[k] for k in keys], 3 * len(keys),
                                             _ag_copies(stage, axes), after)
            after = yield tok
            bufs = _split_wait(f"ag{stage}_wait_{tag}", bufs, (ss, rs), _ag_copies(stage, axes), after)
            W.update(zip(keys, bufs))
        yield None

    nq = b_b_qkv.shape[1]
    bq_full = jnp.zeros((b_b_qkv.shape[0], N_CHIPS * nq), F32)
    bq_full = lax.dynamic_update_slice(bq_full, jnp.where(ci == 0, b_b_qkv, 0.0), (0, chip * nq))
    bq_packed, bq_meta = _pack_rows([bq_full])
    bq_gathered = _allreduce_small(bq_packed)
    b_qkv_full = _unpack_rows(bq_gathered, bq_meta)[0]

    first = gather("0m", layer_keys(0)[:2], bq_gathered)
    tok = next(first)
    tok = first.send([tok] + [W[k] for i in range(depth) for k in layer_keys(i)[2 if i == 0 else 0:]])
    first.send(tok)

    h = x[0]
    target = loss_target[0]
    ctab, stab = _rope_tables(positions[0])
    q_width = W[("b_w_o", 0)].shape[0]
    kv_width = N_KV_HEADS * HEAD_DIM
    row = lambda a, i: a[i:i + 1]
    gains = {"pre_mix": pre_mix_g[:, None], "post_mix": post_mix_g[:, None], "pre_ffn": pre_ffn_g[:, None],
             "post_ffn": post_ffn_g[:, None]}
    gain = lambda which, i: (gains[which], i)

    saved = []
    hn = None
    for i in range(depth):
        j = i // 2
        s = {"h": h}
        ffn_w = None
        if i == 0:
            ffn_w = gather("0f", layer_keys(0)[2:], W[("a_w_out", 0)])
            toks = [next(ffn_w)]
            nxt = gather("1", layer_keys(1), toks[0])
            toks.append(next(nxt))
            hn = _rms_fwd(h, gain("pre_mix", i), out_dtype=BF16, after=toks, name=f"rms_pre_mix_{i}")
        elif i + 1 < depth:
            nxt = gather(str(i + 1), layer_keys(i + 1), h)
            toks = [next(nxt)]
        else:
            toks = []
        s["hn"] = hn
        if i % 2 == 0:
            pre = _matmul(hn, W[("a_w_in", j)], mode="nn", bias=row(a_b_in, j), out_dtype=F32, after=toks,
                          name=f"gmlp_in_{i}")
            gated = _sgu_fwd(pre, row(a_ln_g, j), row(a_ln_b, j), a_w_s[j], a_b_s[j].T, name=f"sgu_fwd_{i}")
            mix = _matmul(gated, W[("a_w_out", j)], mode="nn", out_dtype=F32, name=f"gmlp_out_{i}")
            s.update(pre=pre, gated=gated)
        else:
            qkv = _matmul(hn, W[("b_w_qkv", j)], mode="nn", bias=row(b_qkv_full, j), out_dtype=F32, after=toks,
                          name=f"attn_qkv_{i}")
            qr, kr, vr = _rope_fwd(qkv, ctab, stab, q_width=q_width, kv_width=kv_width, name=f"rope_fwd_{i}")
            o = _attn_fwd(qr, kr, vr, row(b_sinks, j), name=f"attn_fwd_{i}")
            mix = _matmul(o, W[("b_w_o", j)], mode="nn", out_dtype=F32, name=f"attn_o_{i}")
            s.update(qr=qr, kr=kr, vr=vr, o=o)
        s["mix"] = mix
        toks = [ffn_w.send(mix)] if ffn_w else []
        h1, fn = _rms_res_norm(h, mix, gain("post_mix", i), gain("pre_ffn", i), after=toks, name=f"rms_post_mix_{i}")
        if ffn_w:
            ffn_w.send(h1)
        s["h1"] = h1
        g_pre, u_pre, act = _ffn_up(fn, W[("ffn_w_gu", i)][None], 0, name=f"ffn_up_{i}")
        f = _matmul(act, W[("ffn_w_down", i)], mode="nn", out_dtype=F32, name=f"ffn_down_{i}")
        if i + 1 < depth:
            toks = [nxt.send(f)]
            h, hn = _rms_res_norm(h1, f, gain("post_ffn", i), gain("pre_mix", i + 1), after=toks,
                                  name=f"rms_post_ffn_{i}")
            nxt.send(h)
        else:
            h = _rms_res(h1, f, gain("post_ffn", i), name=f"rms_post_ffn_{i}")
        s.update(fn=fn, g_pre=g_pre, u_pre=u_pre, act=act, f=f)
        saved.append(s)

    dh, df, loss_part, g_last = _loss_and_grad(h, target, saved[-1]["f"], gain("post_ffn", depth - 1), name="loss")

    big_out = {nm: tuple(lax.empty(w.shape, F32) for _ in range(4)) for nm, (w, _, _) in stacked.items()}

    def reduce_group(i, keys, grads):
        axes = [cut[nm] for nm, _ in keys]
        n = len(keys)
        lands = [lax.empty(_half_shape(g.shape, ax), BF16) for g, ax in zip(grads, axes)]
        ss, rs, bufs, tok = _split_start(f"rs_sibling_start_{i}", list(grads) + lands, n, _rs_sibling_copies(axes),
                                         place)
        after = yield tok
        bufs = _split_wait(f"rs_sibling_wait_{i}", bufs, (ss, rs), _rs_sibling_copies(axes), after)
        sums = [_chip_sum(bufs[w], bufs[n + w], axes[w], place, name=f"chip_sum_{keys[w][0]}_{keys[w][1]}")
                for w in range(n)]
        lands = [lax.empty((3,) + own.shape, BF16) for _, own in sums]
        ss, rs, bufs, tok = _split_start(f"rs_chip_start_{i}", [sb for sb, _ in sums] + lands, 3 * n,
                                         _rs_chip_copies(axes), place)
        after = yield tok
        bufs = _split_wait(f"rs_chip_wait_{i}", bufs, (ss, rs), _rs_chip_copies(axes), after)
        blocks = [_final_sum(sums[w][1], bufs[n + w], axes[w], place, name=f"final_sum_{keys[w][0]}_{keys[w][1]}")
                  for w in range(n)]
        ss, rs, bufs, tok = _split_start(f"rs_fill_start_{i}", blocks, n, _rs_fill_copies(axes), place)
        after = yield tok
        blocks = _split_wait(f"rs_fill_wait_{i}", bufs, (ss, rs), _rs_fill_copies(axes), after)
        for (nm, l), g in zip(keys, blocks):
            w, m, v = stacked[nm]
            big_out[nm] = tuple(_adamw_layer(w, m, v, g, l, big_out[nm], name=f"adamw_{nm}_{l}"))
        yield None

    reducing = []

    def advance(after, newest_only=False):
        toks = []
        for gen in (reducing[-1:] if newest_only else list(reducing)):
            tok = gen.send(after)
            if tok is None:
                reducing.remove(gen)
            else:
                toks.append(tok)
        return toks

    small = {}
    g_pre_mix, g_post_mix, g_pre_ffn, g_post_ffn = [None] * depth, [None] * depth, [None] * depth, [None] * depth
    g_post_ffn[depth - 1] = g_last
    toks = []
    early = []
    for i in reversed(range(depth)):
        j = i // 2
        s = saved[i]
        g_down = _matmul(s["act"], df, mode="tn", out_dtype=BF16, after=toks, name=f"ffn_down_dw_{i}")
        dg_, du_ = _ffn_down_dx(df, W[("ffn_w_down", i)][None], 0, s["g_pre"], s["u_pre"], g_down,
                                name=f"ffn_down_dx_{i}")
        g_gu = _matmul_pair(s["fn"], dg_, du_, mode="tn", out_dtype=BF16, name=f"ffn_gu_dw_{i}")
        dfn = _matmul_pair(dg_, W[("ffn_w_gu", i)], du_, mode="nt", out_dtype=F32, after=[g_gu],
                           name=f"ffn_gu_dx_{i}")
        toks = advance(dfn)
        if i == 0:
            gen = reduce_group("0f", layer_keys(0)[2:], [g_gu, g_down])
            toks.append(next(gen))
            reducing.append(gen)
        dh1, dmix, g_pre_ffn[i], g_post_mix[i] = _rms_bwd_chain(
            s["h1"], gain("pre_ffn", i), dfn, dh, s["mix"], gain("post_mix", i), after=toks,
            name=f"rms_ffn_mix_bwd_{i}")
        if i % 2 == 0:
            g_out = _matmul(s["gated"], dmix, mode="tn", out_dtype=BF16, name=f"gmlp_out_dw_{i}")
            dgated = _matmul(dmix, W[("a_w_out", j)], mode="nt", out_dtype=BF16, after=[g_out],
                             name=f"gmlp_out_dx_{i}")
            toks = advance(dgated, newest_only=True) if i == 0 else []
            dpre, dws, dbsT, dlng, dlnb, dbin = _sgu_bwd(s["pre"], dgated, dep(row(a_ln_g, j), toks), row(a_ln_b, j),
                                                         a_w_s[j], a_b_s[j].T, name=f"sgu_bwd_{i}")
            small[("a_w_s", j)] = dws
            small[("a_b_s", j)] = dbsT.T
            small[("a_ln_g", j)] = dlng
            small[("a_ln_b", j)] = dlnb
            small[("a_b_in", j)] = dbin
            g_in = _matmul(s["hn"], dpre, mode="tn", out_dtype=BF16, name=f"gmlp_in_dw_{i}")
            if i == 0:
                last = reduce_group("0m", layer_keys(0)[:2], [g_in, g_out])
                early = [next(last)]
            dhn = _matmul(dpre, W[("a_w_in", j)], mode="nt", out_dtype=F32, after=[g_in] + early,
                          name=f"gmlp_in_dx_{i}")
        else:
            g_out = _matmul(s["o"], dmix, mode="tn", out_dtype=BF16, name=f"attn_o_dw_{i}")
            do = _matmul(dmix, W[("b_w_o", j)], mode="nt", out_dtype=BF16, after=[g_out], name=f"attn_o_dx_{i}")
            dq, dkp, dkc, dvp, dvc, dsk = _attn_bwd(s["qr"], s["kr"], s["vr"], row(b_sinks, j), do,
                                                    name=f"attn_bwd_{i}")
            dqkv, dbq = _rope_bwd(dq, dkp, dkc, dvp, dvc, ctab, stab, name=f"rope_bwd_{i}")
            small[("b_sinks", j)] = dsk[:, :b_sinks.shape[1]]
            small[("b_b_qkv", j)] = dbq
            g_in = _matmul(s["hn"], dqkv, mode="tn", out_dtype=BF16, name=f"attn_qkv_dw_{i}")
            if i == 0:
                last = reduce_group("0m", layer_keys(0)[:2], [g_in, g_out])
                early = [next(last)]
            dhn = _matmul(dqkv, W[("b_w_qkv", j)], mode="nt", out_dtype=F32, after=[g_in] + early,
                          name=f"attn_qkv_dx_{i}")
        toks = advance(dhn)
        if i > 0:
            dh, df, g_pre_mix[i], g_post_ffn[i - 1] = _rms_bwd_chain(
                s["h"], gain("pre_mix", i), dhn, dh1, saved[i - 1]["f"], gain("post_ffn", i - 1), after=toks,
                name=f"rms_mix_ffn_bwd_{i}")
            gen = reduce_group(str(i), layer_keys(i), [g_in, g_out, g_gu, g_down])
            toks = [next(gen)] + advance(dh)
            reducing.append(gen)
        else:
            toks.append(last.send(dhn))
            dh, g_pre_mix[i] = _rms_bwd(s["h"], gain("pre_mix", i), dhn, dh1, out_dtype=F32, after=toks,
                                        name=f"rms_pre_mix_bwd_{i}")
            advance(dh)
    grad_x = dh[None]
    assert not reducing

    ready = [big_out[nm][1] for nm in big_out]
    n_a, n_b = a_b_in.shape[0], b_sinks.shape[0]
    stack = lambda key, n: jnp.concatenate([small[(key, j)] for j in range(n)], axis=0)
    small_parts = [
        jnp.concatenate(g_pre_mix, axis=0), jnp.concatenate(g_post_mix, axis=0),
        jnp.concatenate(g_pre_ffn, axis=0), jnp.concatenate(g_post_ffn, axis=0),
        stack("a_b_in", n_a), stack("a_ln_g", n_a), stack("a_ln_b", n_a),
        jnp.stack([small[("a_w_s", j)] for j in range(n_a)]), jnp.stack([small[("a_b_s", j)] for j in range(n_a)]),
        stack("b_b_qkv", n_b), stack("b_sinks", n_b), loss_part,
    ]
    packed, metas = _pack_rows(small_parts)
    reduced = _allreduce_small(packed, after=ready + [dh])
    while last.send(reduced) is not None:
        pass
    red = _unpack_rows(reduced, metas)
    (gr_pre_mix, gr_post_mix, gr_pre_ffn, gr_post_ffn, gr_b_in, gr_ln_g, gr_ln_b, gr_w_s, gr_b_s,
     gr_b_qkv_full, gr_sinks, loss_sum) = red
    loss = loss_sum[0, 0]
    gr_b_qkv = lax.dynamic_slice(gr_b_qkv_full, (0, chip * nq), (gr_b_qkv_full.shape[0], nq))

    grads = {"pre_mix_g": gr_pre_mix, "post_mix_g": gr_post_mix, "pre_ffn_g": gr_pre_ffn, "post_ffn_g": gr_post_ffn,
             "a_b_in": gr_b_in, "a_ln_g": gr_ln_g, "a_ln_b": gr_ln_b, "a_w_s": gr_w_s, "a_b_s": gr_b_s,
             "b_b_qkv": gr_b_qkv, "b_sinks": gr_sinks}
    weights = {"pre_mix_g": (pre_mix_g, m_pre_mix_g, v_pre_mix_g), "post_mix_g": (post_mix_g, m_post_mix_g, v_post_mix_g),
               "pre_ffn_g": (pre_ffn_g, m_pre_ffn_g, v_pre_ffn_g), "post_ffn_g": (post_ffn_g, m_post_ffn_g, v_post_ffn_g),
               "a_b_in": (a_b_in, m_a_b_in, v_a_b_in), "a_ln_g": (a_ln_g, m_a_ln_g, v_a_ln_g),
               "a_ln_b": (a_ln_b, m_a_ln_b, v_a_ln_b), "a_w_s": (a_w_s, m_a_w_s, v_a_w_s), "a_b_s": (a_b_s, m_a_b_s, v_a_b_s),
               "b_b_qkv": (b_b_qkv, m_b_b_qkv, v_b_b_qkv), "b_sinks": (b_sinks, m_b_sinks, v_b_sinks)}
    order = ["pre_mix_g", "post_mix_g", "pre_ffn_g", "post_ffn_g", "a_w_in", "a_b_in", "a_ln_g", "a_ln_b", "a_w_s",
             "a_b_s", "a_w_out", "b_w_qkv", "b_b_qkv", "b_sinks", "b_w_o", "ffn_w_gu", "ffn_w_down"]
    deltas, new_m, new_v = {}, {}, {}
    for nm in order:
        if nm in big_out:
            grads[nm], deltas[nm], new_m[nm], new_v[nm] = big_out[nm]
        else:
            w, m, v = weights[nm]
            deltas[nm], new_m[nm], new_v[nm] = _adamw_small(w, grads[nm], m, v, name="adamw_" + nm)
    return (loss, grad_x, *[grads[nm] for nm in order], *[deltas[nm] for nm in order],
            *[new_m[nm] for nm in order], *[new_v[nm] for nm in order])
```

```python
import functools
import math

import jax
import jax.numpy as jnp
import numpy as np
from jax import lax
from jax.experimental import pallas as pl
from jax.experimental.pallas import tpu as pltpu

F32 = jnp.float32
BF16 = jnp.bfloat16
MESH = pl.DeviceIdType.MESH

HEAD_DIM = 64
N_KV_HEADS = 4
ROPE_DIM = 16
ROPE_THETA = 500000.0
CHUNK = 128
GMLP_GROUPS = 8
RMS_EPS = 1e-6
LN_EPS = 1e-5
NEG_INF = -1e30
ADAM_LR = 0.001
ADAM_B1 = 0.9
ADAM_B2 = 0.999
ADAM_EPS = 1e-08
ADAM_WD = 0.01
ADAM_STEP = 10

N_CHIPS = 4
LANES = 128
VMEM_CAP = 58 * 1024 * 1024


def _vmem(est_bytes):
    assert est_bytes < VMEM_CAP
    return VMEM_CAP


def _pick(n, cands):
    for c in cands:
        if c <= n and n % c == 0:
            return c
    return n


def _nbytes(shape, dtype):
    return int(np.prod(shape)) * jnp.dtype(dtype).itemsize


MATMUL_VMEM_BUDGET = 48 * 1024 * 1024
MXU_COLS = 256


def _halvings(n, unit):
    out, t = [], n
    while t % unit == 0 and t >= unit:
        out.append(t)
        if t % 2:
            break
        t //= 2
    return out


def _matmul_tiles(P, Q, R, a_bytes, b_bytes, o_bytes, full_addend, tp, tq, tr, repeat=1):
    step_us, bytes_per_us, flops_per_us = 0.85, 3.2e6, 9.0e8
    best = None
    for p in ([tp] if tp else _halvings(P, LANES)):
        for q in ([tq] if tq else _halvings(Q, LANES)):
            for r in ([tr] if tr else _halvings(R, LANES)):
                nk = R // r
                vm = 2 * (p * r * a_bytes + r * q * b_bytes + p * q * o_bytes + (p * q * 4 if full_addend else 0))
                vm += p * q * 4 * (2 if nk > 1 else 1)
                if vm > MATMUL_VMEM_BUDGET:
                    continue
                exposed = (p * r * a_bytes + r * q * b_bytes + p * q * o_bytes) / bytes_per_us
                mxu_us = repeat * 2.0 * P * R * (Q // q) * (-(-q // MXU_COLS) * MXU_COLS) / flops_per_us
                key = (repeat * (P // p) * (Q // q) * nk * step_us + exposed + mxu_us, nk, abs(p - q))
                if best is None or key < best[0]:
                    best = (key, (p, q, r))
    assert best is not None, (P, Q, R)
    return best[1]


def _matmul(a, b, *, mode, out_dtype, name, bias=None, after=()):
    if mode == "nn":
        (P, R), (R2, Q) = a.shape, b.shape
    elif mode == "nt":
        (P, R), (Q, R2) = a.shape, b.shape
    else:
        (R, P), (R2, Q) = a.shape, b.shape
    assert R == R2, (mode, a.shape, b.shape)
    tp, tq, tr = _matmul_tiles(P, Q, R, a.dtype.itemsize, b.dtype.itemsize, jnp.dtype(out_dtype).itemsize, False,
                               None, None, None)
    nk = R // tr
    dims = {"nn": (((1,), (0,)), ((), ())), "nt": (((1,), (1,)), ((), ())), "tn": (((0,), (0,)), ((), ()))}[mode]
    if mode == "nn":
        a_spec = pl.BlockSpec((tp, tr), lambda i, j, k: (i, k))
        b_spec = pl.BlockSpec((tr, tq), lambda i, j, k: (k, j))
    elif mode == "nt":
        a_spec = pl.BlockSpec((tp, tr), lambda i, j, k: (i, k))
        b_spec = pl.BlockSpec((tq, tr), lambda i, j, k: (j, k))
    else:
        a_spec = pl.BlockSpec((tr, tp), lambda i, j, k: (k, i))
        b_spec = pl.BlockSpec((tr, tq), lambda i, j, k: (k, j))
    in_specs = [a_spec, b_spec]
    args = [a, b]
    has_bias = bias is not None
    if has_bias:
        in_specs.append(pl.BlockSpec((1, tq), lambda i, j, k: (0, j)))
        args.append(bias)
    out_shape = jax.ShapeDtypeStruct((P, Q), out_dtype)
    out_spec = pl.BlockSpec((tp, tq), lambda i, j, k: (i, j))
    n_in = len(args) + len(after)
    in_specs += [pl.BlockSpec(memory_space=pl.ANY)] * len(after)
    args += list(after)

    def body(*refs):
        a_ref, b_ref = refs[0], refs[1]
        bias_ref = refs[2] if has_bias else None
        o_ref = refs[n_in]
        acc_ref = refs[n_in + 1] if nk > 1 else None
        part = lax.dot_general(a_ref[...], b_ref[...], dims, preferred_element_type=F32)

        def finish(acc):
            if has_bias:
                acc = acc + bias_ref[...]
            o_ref[...] = acc.astype(out_dtype)

        if nk == 1:
            finish(part)
        else:
            k = pl.program_id(2)

            @pl.when(k == 0)
            def _():
                acc_ref[...] = part

            @pl.when(k > 0)
            def _():
                acc_ref[...] += part

            @pl.when(k == nk - 1)
            def _():
                finish(acc_ref[...])

    est = 2 * (_nbytes((tp, tr), a.dtype) + _nbytes((tr, tq), b.dtype) + _nbytes((tp, tq), out_dtype)) + 3 * tp * tq * 4
    return pl.pallas_call(
        body, name=name, out_shape=out_shape,
        grid=(P // tp, Q // tq, nk),
        in_specs=in_specs, out_specs=out_spec,
        scratch_shapes=[pltpu.VMEM((tp, tq), F32)] if nk > 1 else [],
        compiler_params=pltpu.CompilerParams(
            dimension_semantics=("parallel", "parallel", "arbitrary"), vmem_limit_bytes=_vmem(est)),
    )(*args)


def _matmul_pair(a, b, pair, *, mode, out_dtype, name, after=()):
    if mode == "tn":
        (R, P), (R2, Q) = a.shape, b.shape
        assert R == R2 and pair.shape == b.shape
        tp, tq, tr = _matmul_tiles(P, Q, R, a.dtype.itemsize, 2 * b.dtype.itemsize,
                                   jnp.dtype(out_dtype).itemsize, False, None, None, None, repeat=2)
        nq, nk = Q // tq, R // tr
        grid, nk_total = (P // tp, 2 * nq, nk), nk
        a_spec = pl.BlockSpec((tr, tp), lambda i, j, k: (k, i))
        b_spec = pl.BlockSpec((tr, tq), lambda i, j, k: (jnp.where(j < nq, k, nk - 1), jnp.minimum(j, nq - 1)))
        p_spec = pl.BlockSpec((tr, tq), lambda i, j, k: (jnp.where(j >= nq, k, 0), jnp.maximum(j - nq, 0)))
        out_shape = (P, 2 * Q)
        dims = (((0,), (0,)), ((), ()))
    else:
        assert mode == "nt"
        (P, R), (Q, R2) = a.shape, b.shape
        assert R2 == 2 * R and pair.shape == a.shape
        tp, tq, tr = _matmul_tiles(P, Q, R, 2 * a.dtype.itemsize, b.dtype.itemsize,
                                   jnp.dtype(out_dtype).itemsize, False, None, None, None, repeat=2)
        nk = R // tr
        grid, nk_total = (P // tp, Q // tq, 2 * nk), 2 * nk
        a_spec = pl.BlockSpec((tp, tr), lambda i, j, k: (i, jnp.minimum(k, nk - 1)))
        p_spec = pl.BlockSpec((tp, tr), lambda i, j, k: (i, jnp.maximum(k - nk, 0)))
        b_spec = pl.BlockSpec((tq, tr), lambda i, j, k: (j, k))
        out_shape = (P, Q)
        dims = (((1,), (1,)), ((), ()))
    n_after = len(after)

    def body(a_ref, b_ref, p_ref, *rest):
        o_ref = rest[n_after]
        acc_ref = rest[n_after + 1] if nk_total > 1 else None
        j, k = pl.program_id(1), pl.program_id(2)

        def step(l_ref, r_ref):
            part = lax.dot_general(l_ref[...], r_ref[...], dims, preferred_element_type=F32)
            if nk_total == 1:
                o_ref[...] = part.astype(out_dtype)
                return

            @pl.when(k == 0)
            def _():
                acc_ref[...] = part

            @pl.when(k > 0)
            def _():
                acc_ref[...] += part

            @pl.when(k == nk_total - 1)
            def _():
                o_ref[...] = acc_ref[...].astype(out_dtype)

        first = (j < nq) if mode == "tn" else (k < nk)

        @pl.when(first)
        def _():
            step(a_ref, b_ref)

        @pl.when(jnp.logical_not(first))
        def _():
            step(a_ref if mode == "tn" else p_ref, p_ref if mode == "tn" else b_ref)

    n_a, n_b = (1, 2) if mode == "tn" else (2, 1)
    est = (2 * (n_a * _nbytes((tp, tr), a.dtype) + n_b * _nbytes((tr, tq), b.dtype) + _nbytes((tp, tq), out_dtype))
           + 2 * tp * tq * 4)
    return pl.pallas_call(
        body, name=name, out_shape=jax.ShapeDtypeStruct(out_shape, out_dtype), grid=grid,
        in_specs=[a_spec, b_spec, p_spec] + [pl.BlockSpec(memory_space=pl.ANY)] * n_after,
        out_specs=pl.BlockSpec((tp, tq), lambda i, j, k: (i, j)),
        scratch_shapes=[pltpu.VMEM((tp, tq), F32)] if nk_total > 1 else [],
        compiler_params=pltpu.CompilerParams(
            dimension_semantics=("parallel", "parallel", "arbitrary"), vmem_limit_bytes=_vmem(est)),
    )(a, b, pair, *after)


def _row_call(body, ins, outs, *, name, rows, tr, acc_outs=(), est=0, after=()):
    in_specs, args = [], []
    for arr, kind in ins:
        if kind == "row":
            in_specs.append(pl.BlockSpec((tr, arr.shape[1]), lambda i: (i, 0)))
        elif isinstance(arr, tuple):
            arr, layer = arr
            in_specs.append(pl.BlockSpec((None,) + arr.shape[1:], lambda i, layer=layer: (layer, 0, 0)))
        else:
            nd = arr.ndim
            in_specs.append(pl.BlockSpec(arr.shape, lambda i, nd=nd: (0,) * nd))
        args.append(arr)
    n_ins = len(args)
    in_specs += [pl.BlockSpec(memory_space=pl.ANY)] * len(after)
    args += list(after)

    def kernel_fn(*refs):
        body(*refs[:n_ins], *refs[n_ins + len(after):])

    out_shapes = [jax.ShapeDtypeStruct(s, d) for s, d in outs] + [jax.ShapeDtypeStruct(s, d) for s, d in acc_outs]
    out_specs = [pl.BlockSpec((tr, s[1]), lambda i: (i, 0)) for s, _ in outs]
    out_specs += [pl.BlockSpec(s, lambda i, nd=len(s): (0,) * nd) for s, _ in acc_outs]
    res = pl.pallas_call(
        kernel_fn, name=name, out_shape=out_shapes, grid=(rows // tr,), in_specs=in_specs, out_specs=out_specs,
        compiler_params=pltpu.CompilerParams(dimension_semantics=("arbitrary",), vmem_limit_bytes=_vmem(est)),
    )(*args)
    return res


def _rms_fwd(x, g, *, out_dtype, name, after=()):
    T, D = x.shape
    tr = _pick(T, (512, 256, 128))

    def body(x_ref, g_ref, o_ref):
        xv = x_ref[...]
        r = lax.rsqrt(jnp.mean(xv * xv, axis=-1, keepdims=True) + RMS_EPS)
        o_ref[...] = (xv * r * g_ref[...]).astype(out_dtype)

    return _row_call(body, [(x, "row"), (g, "full")], [((T, D), out_dtype)], name=name, rows=T, tr=tr,
                     est=8 * tr * D * 4, after=after)[0]


def _rms_res(h, y, g, *, name):
    T, D = h.shape
    tr = _pick(T, (512, 256, 128))

    def body(h_ref, y_ref, g_ref, o_ref):
        yv = y_ref[...]
        r = lax.rsqrt(jnp.mean(yv * yv, axis=-1, keepdims=True) + RMS_EPS)
        o_ref[...] = h_ref[...] + yv * r * g_ref[...]

    return _row_call(body, [(h, "row"), (y, "row"), (g, "full")], [((T, D), F32)], name=name, rows=T, tr=tr,
                     est=10 * tr * D * 4)[0]


def _rms_bwd(x, g, dy, dres, *, out_dtype, name, after=()):
    T, D = x.shape
    tr = _pick(T, (512, 256, 128))
    has_res = dres is not None

    def body(*refs):
        if has_res:
            x_ref, g_ref, dy_ref, dr_ref, dx_ref, dg_ref = refs
        else:
            x_ref, g_ref, dy_ref, dx_ref, dg_ref = refs
        xv = x_ref[...]
        r = lax.rsqrt(jnp.mean(xv * xv, axis=-1, keepdims=True) + RMS_EPS)
        xhat = xv * r
        dyv = dy_ref[...].astype(F32)
        dxn = dyv * g_ref[...]
        dx = r * (dxn - xhat * jnp.mean(dxn * xhat, axis=-1, keepdims=True))
        if has_res:
            dx = dx + dr_ref[...]
        dx_ref[...] = dx.astype(out_dtype)
        part = jnp.sum(dyv * xhat, axis=0, keepdims=True)

        @pl.when(pl.program_id(0) == 0)
        def _():
            dg_ref[...] = part

        @pl.when(pl.program_id(0) > 0)
        def _():
            dg_ref[...] += part

    ins = [(x, "row"), (g, "full"), (dy, "row")] + ([(dres, "row")] if has_res else [])
    dx, dg = _row_call(body, ins, [((T, D), out_dtype)], name=name, rows=T, tr=tr, acc_outs=[((1, D), F32)],
                       est=12 * tr * D * 4, after=after)
    return dx, dg


def _rms_res_norm(h, y, g_res, g_next, *, name, after=()):
    T, D = h.shape
    tr = _pick(T, (512, 256, 128))

    def body(h_ref, y_ref, g_ref, gn_ref, o_ref, n_ref):
        yv = y_ref[...]
        r = lax.rsqrt(jnp.mean(yv * yv, axis=-1, keepdims=True) + RMS_EPS)
        h2 = h_ref[...] + yv * r * g_ref[...]
        o_ref[...] = h2
        r2 = lax.rsqrt(jnp.mean(h2 * h2, axis=-1, keepdims=True) + RMS_EPS)
        n_ref[...] = (h2 * r2 * gn_ref[...]).astype(BF16)

    return _row_call(body, [(h, "row"), (y, "row"), (g_res, "full"), (g_next, "full")],
                     [((T, D), F32), ((T, D), BF16)], name=name, rows=T, tr=tr, est=12 * tr * D * 4, after=after)


def _rms_bwd_chain(x1, g1, dy1, dres, x2, g2, *, name, after=()):
    T, D = x1.shape
    tr = _pick(T, (512, 256, 128))

    def one(xv, gv, dyv):
        r = lax.rsqrt(jnp.mean(xv * xv, axis=-1, keepdims=True) + RMS_EPS)
        xhat = xv * r
        dxn = dyv * gv
        dx = r * (dxn - xhat * jnp.mean(dxn * xhat, axis=-1, keepdims=True))
        return dx, jnp.sum(dyv * xhat, axis=0, keepdims=True)

    def body(x1_ref, g1_ref, dy1_ref, dr_ref, x2_ref, g2_ref, d1_ref, d2_ref, dg1_ref, dg2_ref):
        dx1, p1 = one(x1_ref[...], g1_ref[...], dy1_ref[...].astype(F32))
        d1 = dx1 + dr_ref[...]
        d1_ref[...] = d1
        dx2, p2 = one(x2_ref[...], g2_ref[...], d1)
        d2_ref[...] = dx2.astype(BF16)

        @pl.when(pl.program_id(0) == 0)
        def _():
            dg1_ref[...] = p1
            dg2_ref[...] = p2

        @pl.when(pl.program_id(0) > 0)
        def _():
            dg1_ref[...] += p1
            dg2_ref[...] += p2

    ins = [(x1, "row"), (g1, "full"), (dy1, "row"), (dres, "row"), (x2, "row"), (g2, "full")]
    return _row_call(body, ins, [((T, D), F32), ((T, D), BF16)], name=name, rows=T, tr=tr,
                     acc_outs=[((1, D), F32), ((1, D), F32)], est=20 * tr * D * 4, after=after)


def _ffn_up(fn, w_gu, l, *, name):
    T, D = fn.shape
    H = w_gu.shape[2] // 2
    tp = _pick(T, (256, 128))
    tq = H
    nj = H // tq

    def body(a_ref, wg_ref, wu_ref, g_ref, u_ref, act_ref):
        a = a_ref[...]
        g = jnp.dot(a, wg_ref[...], preferred_element_type=F32)
        u = jnp.dot(a, wu_ref[...], preferred_element_type=F32)
        sg = jax.nn.sigmoid(g)
        silu = g * sg
        g_ref[...] = (u * (sg + silu * (1.0 - sg))).astype(BF16)
        u_ref[...] = silu.astype(BF16)
        act_ref[...] = (silu * u).astype(BF16)

    tile = pl.BlockSpec((tp, tq), lambda j, i: (i, j))
    est = 2 * (tp * D * 2 + 2 * D * tq * 2 + 3 * tp * tq * 2) + 4 * tp * tq * 4
    return pl.pallas_call(
        body, name=name,
        out_shape=[jax.ShapeDtypeStruct((T, H), BF16), jax.ShapeDtypeStruct((T, H), BF16),
                   jax.ShapeDtypeStruct((T, H), BF16)],
        grid=(nj, T // tp),
        in_specs=[pl.BlockSpec((tp, D), lambda j, i: (i, 0)),
                  pl.BlockSpec((None, D, tq), lambda j, i: (l, 0, j)),
                  pl.BlockSpec((None, D, tq), lambda j, i: (l, 0, j + nj))],
        out_specs=[tile, tile, tile],
        compiler_params=pltpu.CompilerParams(dimension_semantics=("parallel", "parallel"),
                                             vmem_limit_bytes=_vmem(est)),
    )(fn, w_gu, w_gu)


def _ffn_down_dx(df, w_down, l, g, u, after, *, name):
    T, D = df.shape
    H = w_down.shape[1]
    tp = _pick(T, (512, 256, 128))
    tq = H

    def body(a_ref, w_ref, g_ref, u_ref, _, dg_ref, du_ref):
        da = lax.dot_general(a_ref[...], w_ref[...], (((1,), (1,)), ((), ())), preferred_element_type=F32)
        dg_ref[...] = (da * g_ref[...].astype(F32)).astype(BF16)
        du_ref[...] = (da * u_ref[...].astype(F32)).astype(BF16)

    tile = pl.BlockSpec((tp, tq), lambda j, i: (i, j))
    est = 2 * (tp * D * 2 + tq * D * 2 + 4 * tp * tq * 2) + 3 * tp * tq * 4
    return pl.pallas_call(
        body, name=name,
        out_shape=[jax.ShapeDtypeStruct((T, H), BF16), jax.ShapeDtypeStruct((T, H), BF16)],
        grid=(H // tq, T // tp),
        in_specs=[pl.BlockSpec((tp, D), lambda j, i: (i, 0)),
                  pl.BlockSpec((None, tq, D), lambda j, i: (l, j, 0)), tile, tile,
                  pl.BlockSpec(memory_space=pl.ANY)],
        out_specs=[tile, tile],
        compiler_params=pltpu.CompilerParams(dimension_semantics=("parallel", "parallel"),
                                             vmem_limit_bytes=_vmem(est)),
    )(df, w_down, g, u, after)


def _loss_and_grad(y, target, x, g, *, name):
    T, D = y.shape
    tr = _pick(T, (512, 256, 128))

    def body(y_ref, t_ref, x_ref, g_ref, dy_ref, dx_ref, l_ref, dg_ref):
        e = y_ref[...] - t_ref[...]
        dy = e * (1.0 / D)
        dy_ref[...] = dy
        part = jnp.sum(jnp.sum(e * e, axis=1, keepdims=True), axis=0, keepdims=True) * (0.5 / D)
        xv = x_ref[...]
        r = lax.rsqrt(jnp.mean(xv * xv, axis=-1, keepdims=True) + RMS_EPS)
        xhat = xv * r
        dxn = dy * g_ref[...]
        dx_ref[...] = (r * (dxn - xhat * jnp.mean(dxn * xhat, axis=-1, keepdims=True))).astype(BF16)
        dg = jnp.sum(dy * xhat, axis=0, keepdims=True)

        @pl.when(pl.program_id(0) == 0)
        def _():
            l_ref[...] = part
            dg_ref[...] = dg

        @pl.when(pl.program_id(0) > 0)
        def _():
            l_ref[...] += part
            dg_ref[...] += dg

    dy, dx, l, dg = _row_call(body, [(y, "row"), (target, "row"), (x, "row"), (g, "full")],
                              [((T, D), F32), ((T, D), BF16)], name=name, rows=T, tr=tr,
                              acc_outs=[((1, 1), F32), ((1, D), F32)], est=14 * tr * D * 4)
    return dy, dx, l, dg


_SQRT_HALF = 0.7071067811865476
_INV_SQRT_2PI = 0.3989422804014327


def _gelu_parts(x):
    cdf = 0.5 * (1.0 + lax.erf(x * _SQRT_HALF))
    return cdf


def _sgu_common(pre, lng, lnb, W):
    cdf = _gelu_parts(pre)
    z = pre * cdf
    u = z[:, :W]
    v = z[:, W:]
    mu = jnp.mean(v, axis=-1, keepdims=True)
    vc = v - mu
    var = jnp.mean(vc * vc, axis=-1, keepdims=True)
    rstd = lax.rsqrt(var + LN_EPS)
    vhat = vc * rstd
    vn = vhat * lng + lnb
    return cdf, u, vhat, rstd, vn


def _causal_mask():
    t = lax.broadcasted_iota(jnp.int32, (CHUNK, CHUNK), 0)
    s = lax.broadcasted_iota(jnp.int32, (CHUNK, CHUNK), 1)
    return t >= s


def _sgu_fwd(pre, lng, lnb, ws, bsT, *, name):
    T, W2 = pre.shape
    W = W2 // 2
    G = ws.shape[0]
    gd = W // G

    def body(pre_ref, lng_ref, lnb_ref, ws_ref, bs_ref, o_ref):
        _, u, _, _, vn = _sgu_common(pre_ref[...], lng_ref[...], lnb_ref[...], W)
        vnb = vn.astype(BF16)
        causal = _causal_mask()
        for g in range(G):
            w = jnp.where(causal, ws_ref[g], 0.0).astype(BF16)
            sv = jnp.dot(w, vnb[:, g * gd:(g + 1) * gd], preferred_element_type=F32) + bs_ref[:, g:g + 1]
            o_ref[:, g * gd:(g + 1) * gd] = (u[:, g * gd:(g + 1) * gd] * sv).astype(BF16)

    return pl.pallas_call(
        body, name=name, out_shape=jax.ShapeDtypeStruct((T, W), BF16), grid=(T // CHUNK,),
        in_specs=[pl.BlockSpec((CHUNK, W2), lambda i: (i, 0)),
                  pl.BlockSpec((1, W), lambda i: (0, 0)), pl.BlockSpec((1, W), lambda i: (0, 0)),
                  pl.BlockSpec(ws.shape, lambda i: (0, 0, 0)), pl.BlockSpec(bsT.shape, lambda i: (0, 0))],
        out_specs=pl.BlockSpec((CHUNK, W), lambda i: (i, 0)),
        compiler_params=pltpu.CompilerParams(dimension_semantics=("arbitrary",),
                                             vmem_limit_bytes=_vmem(12 * CHUNK * W2 * 4)),
    )(pre, lng, lnb, ws, bsT)


def _sgu_bwd(pre, dgated, lng, lnb, ws, bsT, *, name):
    T, W2 = pre.shape
    W = W2 // 2
    G = ws.shape[0]
    gd = W // G

    def body(pre_ref, dgt_ref, lng_ref, lnb_ref, ws_ref, bs_ref,
             dpre_ref, dws_ref, dbs_ref, dlng_ref, dlnb_ref, dbin_ref):
        first = pl.program_id(0) == 0

        @pl.when(first)
        def _():
            dws_ref[...] = jnp.zeros_like(dws_ref)
            dbs_ref[...] = jnp.zeros_like(dbs_ref)
            dlng_ref[...] = jnp.zeros_like(dlng_ref)
            dlnb_ref[...] = jnp.zeros_like(dlnb_ref)
            dbin_ref[...] = jnp.zeros_like(dbin_ref)

        pre_v = pre_ref[...]
        lng_v = lng_ref[...]
        cdf, u, vhat, rstd, vn = _sgu_common(pre_v, lng_v, lnb_ref[...], W)
        vnb = vn.astype(BF16)
        dgt = dgt_ref[...].astype(F32)
        causal = _causal_mask()
        du_parts, dvn_parts = [], []
        for g in range(G):
            sl = slice(g * gd, (g + 1) * gd)
            w = jnp.where(causal, ws_ref[g], 0.0).astype(BF16)
            sv = jnp.dot(w, vnb[:, sl], preferred_element_type=F32) + bs_ref[:, g:g + 1]
            dgt_g = dgt[:, sl]
            du_parts.append(dgt_g * sv)
            dsv = dgt_g * u[:, sl]
            dsvb = dsv.astype(BF16)
            dvn_parts.append(lax.dot_general(w, dsvb, (((0,), (0,)), ((), ())), preferred_element_type=F32))
            dw = lax.dot_general(dsvb, vnb[:, sl], (((1,), (1,)), ((), ())), preferred_element_type=F32)
            dws_ref[g] += jnp.where(causal, dw, 0.0)
            dbs_ref[:, g:g + 1] += jnp.sum(dsv, axis=1, keepdims=True)
        du = jnp.concatenate(du_parts, axis=1)
        dvn = jnp.concatenate(dvn_parts, axis=1)
        dlng_ref[...] += jnp.sum(dvn * vhat, axis=0, keepdims=True)
        dlnb_ref[...] += jnp.sum(dvn, axis=0, keepdims=True)
        dvh = dvn * lng_v
        dv = rstd * (dvh - jnp.mean(dvh, axis=-1, keepdims=True)
                     - vhat * jnp.mean(dvh * vhat, axis=-1, keepdims=True))
        dz = jnp.concatenate([du, dv], axis=1)
        dgelu = cdf + pre_v * jnp.exp(-0.5 * pre_v * pre_v) * _INV_SQRT_2PI
        dpre = dz * dgelu
        dbin_ref[...] += jnp.sum(dpre, axis=0, keepdims=True)
        dpre_ref[...] = dpre.astype(BF16)

    full = lambda shape: pl.BlockSpec(shape, lambda i, nd=len(shape): (0,) * nd)
    return pl.pallas_call(
        body, name=name,
        out_shape=[jax.ShapeDtypeStruct((T, W2), BF16), jax.ShapeDtypeStruct(ws.shape, F32),
                   jax.ShapeDtypeStruct(bsT.shape, F32), jax.ShapeDtypeStruct((1, W), F32),
                   jax.ShapeDtypeStruct((1, W), F32), jax.ShapeDtypeStruct((1, W2), F32)],
        grid=(T // CHUNK,),
        in_specs=[pl.BlockSpec((CHUNK, W2), lambda i: (i, 0)), pl.BlockSpec((CHUNK, W), lambda i: (i, 0)),
                  full((1, W)), full((1, W)), full(ws.shape), full(bsT.shape)],
        out_specs=[pl.BlockSpec((CHUNK, W2), lambda i: (i, 0)), full(ws.shape), full(bsT.shape),
                   full((1, W)), full((1, W)), full((1, W2))],
        compiler_params=pltpu.CompilerParams(dimension_semantics=("arbitrary",),
                                             vmem_limit_bytes=_vmem(24 * CHUNK * W2 * 4)),
    )(pre, dgated, lng, lnb, ws, bsT)


def _rope_tables(positions):
    half = ROPE_DIM // 2
    inv_freq = ROPE_THETA ** (-jnp.arange(0, ROPE_DIM, 2, dtype=F32) / ROPE_DIM)
    ang = positions.astype(F32).reshape(-1, 1) * inv_freq
    cos, sin = jnp.cos(ang), jnp.sin(ang)
    T = ang.shape[0]
    rest = HEAD_DIM - ROPE_DIM
    c64 = jnp.concatenate([cos, cos, jnp.ones((T, rest), F32)], axis=1)
    s64 = jnp.concatenate([-sin, sin, jnp.zeros((T, rest), F32)], axis=1)
    del half
    return jnp.tile(c64, (1, LANES // HEAD_DIM)), jnp.tile(s64, (1, LANES // HEAD_DIM))


def _swap8(x):
    W = x.shape[1]
    half = ROPE_DIM // 2
    lane = lax.broadcasted_iota(jnp.int32, x.shape, 1) % HEAD_DIM
    return jnp.where(lane < half, pltpu.roll(x, W - half, axis=1),
                     jnp.where(lane < ROPE_DIM, pltpu.roll(x, half, axis=1), 0.0))


def _wide(tab, W):
    return jnp.concatenate([tab] * (W // LANES), axis=1) if W > LANES else tab


def _rope_fwd(qkv, ctab, stab, *, q_width, kv_width, name):
    T = qkv.shape[0]
    tr = _pick(T, (256, 128))
    scale = HEAD_DIM ** -0.5

    def body(x_ref, c_ref, s_ref, q_ref, k_ref, v_ref):
        c = c_ref[...]
        s = s_ref[...]
        q = x_ref[:, :q_width]
        k = x_ref[:, q_width:q_width + kv_width]
        q_ref[...] = ((q * _wide(c, q_width) + _swap8(q) * _wide(s, q_width)) * scale).astype(BF16)
        k_ref[...] = (k * _wide(c, kv_width) + _swap8(k) * _wide(s, kv_width)).astype(BF16)
        v_ref[...] = x_ref[:, q_width + kv_width:].astype(BF16)

    return _row_call(body, [(qkv, "row"), (ctab, "row"), (stab, "row")],
                     [((T, q_width), BF16), ((T, kv_width), BF16), ((T, kv_width), BF16)],
                     name=name, rows=T, tr=tr, est=10 * tr * qkv.shape[1] * 4)


_NT = (((1,), (1,)), ((), ()))
_TN = (((0,), (0,)), ((), ()))


def _group_rows(ref, heads):
    return jnp.concatenate([ref[:, h * HEAD_DIM:(h + 1) * HEAD_DIM] for h in heads], axis=0)


def _attn_valid(grp):
    qi = np.arange(grp * CHUNK)[:, None] % CHUNK
    sj = np.arange(2 * CHUNK)[None, :]
    cur = (sj >= CHUNK) & (sj - CHUNK <= qi)
    prev = (sj < CHUNK) & (sj > qi)
    return jnp.asarray(np.stack([cur, cur | prev]).astype(np.float32))


def _valid_spec(grp):
    return pl.BlockSpec((None, grp * CHUNK, 2 * CHUNK), lambda n: (jnp.minimum(n, 1), 0, 0))


def _attn_group_probs(q, kk, sinks, valid, grp):
    rows = grp * CHUNK
    s = lax.dot_general(q, kk, _NT, preferred_element_type=F32)
    s = jnp.where(valid, s, NEG_INF)
    r = lax.broadcasted_iota(jnp.int32, (rows, 1), 0)
    sink = jnp.full((rows, 1), sinks[grp - 1], F32)
    for g in range(grp - 2, -1, -1):
        sink = jnp.where(r < (g + 1) * CHUNK, sinks[g], sink)
    m = jnp.maximum(jnp.max(s, axis=1, keepdims=True), sink)
    p = jnp.exp(s - m)
    ps = jnp.exp(sink - m)
    inv = 1.0 / (jnp.sum(p, axis=1, keepdims=True) + ps)
    return p * inv, ps * inv


def _kv_specs(width, nb):
    prev = pl.BlockSpec((CHUNK, width), lambda n: (jnp.maximum(n - 1, 0), 0))
    cur = pl.BlockSpec((CHUNK, width), lambda n: (n, 0))
    return prev, cur


def _attn_fwd(qr, kr, vr, sinks, *, name):
    T, QW = qr.shape
    KW = kr.shape[1]
    HQ, HK = QW // HEAD_DIM, KW // HEAD_DIM
    grp = HQ // HK
    nb = T // CHUNK

    def body(q_ref, kp_ref, kc_ref, vp_ref, vc_ref, s_ref, ok_ref, o_ref):
        valid = ok_ref[...] > 0.5
        for kh in range(HK):
            ks = slice(kh * HEAD_DIM, (kh + 1) * HEAD_DIM)
            heads = list(range(kh * grp, (kh + 1) * grp))
            q = _group_rows(q_ref, heads)
            kk = jnp.concatenate([kp_ref[:, ks], kc_ref[:, ks]], axis=0)
            vv = jnp.concatenate([vp_ref[:, ks], vc_ref[:, ks]], axis=0)
            p, _ = _attn_group_probs(q, kk, [s_ref[0, h] for h in heads], valid, grp)
            o = jnp.dot(p.astype(BF16), vv, preferred_element_type=F32).astype(BF16)
            for g, h in enumerate(heads):
                o_ref[:, h * HEAD_DIM:(h + 1) * HEAD_DIM] = o[g * CHUNK:(g + 1) * CHUNK]

    kp, kc = _kv_specs(KW, nb)
    return pl.pallas_call(
        body, name=name, out_shape=jax.ShapeDtypeStruct((T, QW), BF16), grid=(nb,),
        in_specs=[pl.BlockSpec((CHUNK, QW), lambda n: (n, 0)), kp, kc, kp, kc,
                  pl.BlockSpec(memory_space=pltpu.SMEM), _valid_spec(grp)],
        out_specs=pl.BlockSpec((CHUNK, QW), lambda n: (n, 0)),
        compiler_params=pltpu.CompilerParams(dimension_semantics=("arbitrary",), vmem_limit_bytes=_vmem(8 << 20)),
    )(qr, kr, kr, vr, vr, sinks, _attn_valid(grp))


def _attn_bwd(qr, kr, vr, sinks, do, *, name):
    T, QW = qr.shape
    KW = kr.shape[1]
    HQ, HK = QW // HEAD_DIM, KW // HEAD_DIM
    grp = HQ // HK
    nb = T // CHUNK

    def body(q_ref, kp_ref, kc_ref, vp_ref, vc_ref, s_ref, do_ref, ok_ref,
             dq_ref, dkp_ref, dkc_ref, dvp_ref, dvc_ref, ds_ref):
        n = pl.program_id(0)
        valid = ok_ref[...] > 0.5
        lane = lax.broadcasted_iota(jnp.int32, (1, LANES), 1)
        dsink = jnp.zeros((1, LANES), F32)
        for kh in range(HK):
            ks = slice(kh * HEAD_DIM, (kh + 1) * HEAD_DIM)
            heads = list(range(kh * grp, (kh + 1) * grp))
            q = _group_rows(q_ref, heads)
            doh = _group_rows(do_ref, heads)
            kk = jnp.concatenate([kp_ref[:, ks], kc_ref[:, ks]], axis=0)
            vv = jnp.concatenate([vp_ref[:, ks], vc_ref[:, ks]], axis=0)
            p, ps = _attn_group_probs(q, kk, [s_ref[0, h] for h in heads], valid, grp)
            dp = lax.dot_general(doh, vv, _NT, preferred_element_type=F32)
            delta = jnp.sum(p * dp, axis=1, keepdims=True)
            ds = (p * (dp - delta)).astype(BF16)
            dv = lax.dot_general(p.astype(BF16), doh, _TN, preferred_element_type=F32)
            dk = lax.dot_general(ds, q, _TN, preferred_element_type=F32)
            dq = jnp.dot(ds, kk, preferred_element_type=F32)
            psd = ps * delta
            for g, h in enumerate(heads):
                dq_ref[:, h * HEAD_DIM:(h + 1) * HEAD_DIM] = dq[g * CHUNK:(g + 1) * CHUNK]
                dsink = dsink + jnp.where(
                    lane == h, -jnp.sum(psd[g * CHUNK:(g + 1) * CHUNK], axis=0, keepdims=True), 0.0)
            dkp_ref[:, ks] = dk[:CHUNK]
            dkc_ref[:, ks] = dk[CHUNK:]
            dvp_ref[:, ks] = dv[:CHUNK]
            dvc_ref[:, ks] = dv[CHUNK:]

        @pl.when(n == 0)
        def _():
            ds_ref[...] = dsink

        @pl.when(n > 0)
        def _():
            ds_ref[...] += dsink

    kp, kc = _kv_specs(KW, nb)
    qspec = pl.BlockSpec((CHUNK, QW), lambda n: (n, 0))
    kout = pl.BlockSpec((CHUNK, KW), lambda n: (n, 0))
    return pl.pallas_call(
        body, name=name,
        out_shape=[jax.ShapeDtypeStruct((T, QW), F32)] + [jax.ShapeDtypeStruct((T, KW), F32)] * 4
        + [jax.ShapeDtypeStruct((1, LANES), F32)],
        grid=(nb,),
        in_specs=[qspec, kp, kc, kp, kc, pl.BlockSpec(memory_space=pltpu.SMEM), qspec, _valid_spec(grp)],
        out_specs=[qspec, kout, kout, kout, kout, pl.BlockSpec((1, LANES), lambda n: (0, 0))],
        compiler_params=pltpu.CompilerParams(dimension_semantics=("arbitrary",), vmem_limit_bytes=_vmem(12 << 20)),
    )(qr, kr, kr, vr, vr, sinks, do, _attn_valid(grp))


def _rope_bwd(dq, dkp, dkc, dvp, dvc, ctab, stab, *, name):
    T, QW = dq.shape
    KW = dkp.shape[1]
    nb = T // CHUNK
    scale = HEAD_DIM ** -0.5
    width = QW + 2 * KW

    def body(dq_ref, dkc_ref, dkn_ref, dvc_ref, dvn_ref, c_ref, s_ref, o_ref, db_ref):
        n = pl.program_id(0)
        c = c_ref[...]
        s = s_ref[...]
        has_next = (n < nb - 1).astype(F32)
        dqv = dq_ref[...]
        dk = dkc_ref[...] + has_next * dkn_ref[...]
        dv = dvc_ref[...] + has_next * dvn_ref[...]
        dq_pre = (dqv * _wide(c, QW) + _swap8(dqv * _wide(s, QW))) * scale
        dk_pre = dk * _wide(c, KW) + _swap8(dk * _wide(s, KW))
        o_ref[:, :QW] = dq_pre.astype(BF16)
        o_ref[:, QW:QW + KW] = dk_pre.astype(BF16)
        o_ref[:, QW + KW:] = dv.astype(BF16)
        part = jnp.concatenate([jnp.sum(dq_pre, axis=0, keepdims=True), jnp.sum(dk_pre, axis=0, keepdims=True),
                                jnp.sum(dv, axis=0, keepdims=True)], axis=1)

        @pl.when(n == 0)
        def _():
            db_ref[...] = part

        @pl.when(n > 0)
        def _():
            db_ref[...] += part

    cur = lambda w: pl.BlockSpec((CHUNK, w), lambda n: (n, 0))
    nxt = lambda w: pl.BlockSpec((CHUNK, w), lambda n: (jnp.minimum(n + 1, nb - 1), 0))
    return pl.pallas_call(
        body, name=name,
        out_shape=[jax.ShapeDtypeStruct((T, width), BF16), jax.ShapeDtypeStruct((1, width), F32)],
        grid=(nb,),
        in_specs=[cur(QW), cur(KW), nxt(KW), cur(KW), nxt(KW), cur(LANES), cur(LANES)],
        out_specs=[cur(width), pl.BlockSpec((1, width), lambda n: (0, 0))],
        compiler_params=pltpu.CompilerParams(dimension_semantics=("arbitrary",), vmem_limit_bytes=_vmem(8 << 20)),
    )(dq, dkc, dkp, dvc, dvp, ctab, stab)


def _cast_block(w, l, axis, chip_arr, *, name):
    _, Ks, Ns = w.shape
    tk = _pick(Ks, (512, 352, 256, 128))
    nk = Ks // tk
    full = (Ks * N_CHIPS, Ns) if axis == 0 else (Ks, Ns * N_CHIPS)

    def body(p_ref, w_ref, o_ref):
        o_ref[...] = w_ref[...].astype(BF16)

    if axis == 0:
        out_spec = pl.BlockSpec((tk, Ns), lambda i, p: (p[0] * nk + i, 0))
    else:
        out_spec = pl.BlockSpec((tk, Ns), lambda i, p: (i, p[0]))
    grid_spec = pltpu.PrefetchScalarGridSpec(
        num_scalar_prefetch=1, grid=(nk,),
        in_specs=[pl.BlockSpec((None, tk, Ns), lambda i, p: (l, i, 0))], out_specs=out_spec)
    return pl.pallas_call(
        body, name=name, out_shape=jax.ShapeDtypeStruct(full, BF16), grid_spec=grid_spec,
        compiler_params=pltpu.CompilerParams(dimension_semantics=("arbitrary",),
                                             vmem_limit_bytes=_vmem(4 * tk * Ns * 6)),
    )(chip_arr, w)


def _adamw_math(w, g, m, v):
    m = ADAM_B1 * m + (1.0 - ADAM_B1) * g
    v = ADAM_B2 * v + (1.0 - ADAM_B2) * (g * g)
    m_hat = m / (1.0 - ADAM_B1 ** ADAM_STEP)
    v_hat = v / (1.0 - ADAM_B2 ** ADAM_STEP)
    delta = -ADAM_LR * (m_hat / (jnp.sqrt(v_hat) + ADAM_EPS) + ADAM_WD * w)
    return delta, m, v


def _adamw_layer(w, m, v, g, l, outs, *, name):
    _, K, N = w.shape
    tk = _pick(K, (512, 352, 256, 128)) if N <= 1024 else _pick(K, (256, 176, 128))

    def body(w_ref, m_ref, v_ref, g_ref, _g, _d, _m, _v, go_ref, d_ref, mo_ref, vo_ref):
        gv = g_ref[...]
        d, mn, vn = _adamw_math(w_ref[...], gv, m_ref[...], v_ref[...])
        go_ref[...] = gv
        d_ref[...] = d
        mo_ref[...] = mn
        vo_ref[...] = vn

    layer = pl.BlockSpec((None, tk, N), lambda i: (l, i, 0))
    any_spec = pl.BlockSpec(memory_space=pl.ANY)
    sd = jax.ShapeDtypeStruct(w.shape, F32)
    return pl.pallas_call(
        body, name=name, out_shape=[sd, sd, sd, sd], grid=(K // tk,),
        in_specs=[layer, layer, layer, pl.BlockSpec((tk, N), lambda i: (i, 0))] + [any_spec] * 4,
        out_specs=[layer] * 4, input_output_aliases={4: 0, 5: 1, 6: 2, 7: 3},
        compiler_params=pltpu.CompilerParams(dimension_semantics=("arbitrary",),
                                             vmem_limit_bytes=_vmem(2 * 8 * tk * N * 4 + 6 * tk * N * 4)),
    )(w, m, v, g, *outs)


def _adamw_small(w, g, m, v, *, name):
    def body(w_ref, g_ref, m_ref, v_ref, d_ref, mo_ref, vo_ref):
        d, mn, vn = _adamw_math(w_ref[...], g_ref[...], m_ref[...], v_ref[...])
        d_ref[...] = d
        mo_ref[...] = mn
        vo_ref[...] = vn

    sd = jax.ShapeDtypeStruct(w.shape, F32)
    return pl.pallas_call(body, name=name, out_shape=[sd, sd, sd])(w, g, m, v)


def _my_place():
    return lax.axis_index("x"), lax.axis_index("y"), lax.axis_index("c")


def _peer_chips(x, y):
    return [(1 - x, y), (x, 1 - y), (1 - x, 1 - y)]


_HBM = pl.BlockSpec(memory_space=pltpu.HBM)
_SEM = pl.BlockSpec(memory_space=pltpu.SEMAPHORE)
_EFFECT = pltpu.SideEffectType.DATAFLOW_SIDE_EFFECTING


def _split_start(name, bufs, n_copies, make_copies, after):
    nb = len(bufs)

    def body(*refs):
        send_sems, recv_sems = refs[nb + 1], refs[nb + 2]
        token = refs[2 * nb + 3]
        sends, _ = make_copies(refs[:nb], send_sems, recv_sems)
        for cp in sends:
            cp.start()
        token[...] = jnp.zeros_like(token)

    res = pl.pallas_call(
        body, name=name,
        out_shape=(pltpu.SemaphoreType.DMA((n_copies,)), pltpu.SemaphoreType.DMA((n_copies,)),
                   *[pltpu.HBM(b.shape, b.dtype) for b in bufs], jax.ShapeDtypeStruct((8, LANES), F32)),
        in_specs=[_HBM] * nb + [pl.BlockSpec(memory_space=pl.ANY)],
        out_specs=(_SEM, _SEM, *[_HBM] * nb, pl.BlockSpec(memory_space=pltpu.VMEM)),
        input_output_aliases={k: 2 + k for k in range(nb)},
        compiler_params=pltpu.CompilerParams(has_side_effects=_EFFECT),
    )(*[pltpu.with_memory_space_constraint(b, pltpu.HBM) for b in bufs],
      after[0] if isinstance(after, (list, tuple)) else after)
    return res[0], res[1], list(res[2:2 + nb]), res[2 + nb]


def _split_wait(name, bufs, sems, make_copies, after):
    nb = len(bufs)
    after = list(after) if isinstance(after, (list, tuple)) else [after]

    def body(*refs):
        send_sems, recv_sems = refs[nb], refs[nb + 1]
        sends, recvs = make_copies(refs[:nb], send_sems, recv_sems)
        for cp in sends:
            cp.wait_send()
        for cp in recvs:
            cp.wait_recv()

    res = pl.pallas_call(
        body, name=name,
        out_shape=tuple(pltpu.HBM(b.shape, b.dtype) for b in bufs),
        in_specs=[_HBM] * nb + [_SEM, _SEM] + [pl.BlockSpec(memory_space=pl.ANY)] * len(after),
        out_specs=tuple([_HBM] * nb),
        input_output_aliases={k: k for k in range(nb)},
        compiler_params=pltpu.CompilerParams(has_side_effects=_EFFECT),
    )(*bufs, sems[0], sems[1], *after)
    return list(res)


def _remote(src, dst, send_sems, recv_sems, k, target):
    return pltpu.make_async_remote_copy(src_ref=src, dst_ref=dst, send_sem=send_sems.at[k],
                                        recv_sem=recv_sems.at[k], device_id=target, device_id_type=MESH)


def _ag_region(ref, axis, chip, half):
    K, N = ref.shape
    if axis == 0:
        hs = K // N_CHIPS // 2
        assert hs % 16 == 0
        return ref.at[pl.ds(pl.multiple_of((2 * chip + half) * hs, 16), hs), :]
    ns, hk = N // N_CHIPS, K // 2
    assert ns % LANES == 0 and hk % 16 == 0
    return ref.at[pl.ds(pl.multiple_of(half * hk, 16), hk), pl.ds(pl.multiple_of(chip * ns, LANES), ns)]


def _ag_copies(stage, axes):
    n = len(axes)

    def make(bufs, send_sems, recv_sems):
        x, y, c = _my_place()
        me = 2 * x + y
        sends, recvs = [], []
        for j, (px, py) in enumerate(_peer_chips(x, y)):
            other = 2 * px + py
            for w in range(n):
                k = j * n + w
                if stage == 1:
                    src, target = _ag_region(bufs[w], axes[w], me, c), (px, py, c)
                    land = _ag_region(bufs[w], axes[w], other, c)
                else:
                    src, target = _ag_region(bufs[w], axes[w], other, c), (x, y, 1 - c)
                    land = _ag_region(bufs[w], axes[w], other, 1 - c)
                sends.append(_remote(src, src, send_sems, recv_sems, k, target))
                recvs.append(_remote(land, land, send_sems, recv_sems, k, target))
        return sends, recvs

    return make


def _half_shape(shape, axis):
    K, N = shape
    return (K, N // 2) if axis == 0 else (K // 2, N)


def _core_half(ref, axis, half):
    K, N = ref.shape
    if axis == 0:
        return ref.at[:, pl.ds(pl.multiple_of(half * (N // 2), LANES), N // 2)]
    return ref.at[pl.ds(pl.multiple_of(half * (K // 2), 16), K // 2), :]


def _chip_block(ref, axis, chip):
    K, N = ref.shape
    if axis == 0:
        return ref.at[pl.ds(pl.multiple_of(chip * (K // N_CHIPS), 16), K // N_CHIPS), :]
    return ref.at[:, pl.ds(pl.multiple_of(chip * (N // N_CHIPS), LANES), N // N_CHIPS)]


def _rs_sibling_copies(axes):
    n = len(axes)

    def make(bufs, send_sems, recv_sems):
        x, y, c = _my_place()
        sends = [_remote(_core_half(bufs[w], axes[w], 1 - c), bufs[n + w], send_sems, recv_sems, w, (x, y, 1 - c))
                 for w in range(n)]
        recvs = [_remote(bufs[n + w], bufs[n + w], send_sems, recv_sems, w, (x, y, 1 - c)) for w in range(n)]
        return sends, recvs

    return make


def _rs_chip_copies(axes):
    n = len(axes)

    def make(bufs, send_sems, recv_sems):
        x, y, c = _my_place()
        sends, recvs = [], []
        for j, (px, py) in enumerate(_peer_chips(x, y)):
            for w in range(n):
                k = j * n + w
                sends.append(_remote(_chip_block(bufs[w], axes[w], 2 * px + py), bufs[n + w].at[j],
                                     send_sems, recv_sems, k, (px, py, c)))
                recvs.append(_remote(bufs[n + w].at[j], bufs[n + w].at[j], send_sems, recv_sems, k, (px, py, c)))
        return sends, recvs

    return make


def _rs_fill_copies(axes):
    n = len(axes)

    def make(bufs, send_sems, recv_sems):
        x, y, c = _my_place()
        sends = [_remote(_core_half(bufs[w], axes[w], c), _core_half(bufs[w], axes[w], c),
                         send_sems, recv_sems, w, (x, y, 1 - c)) for w in range(n)]
        recvs = [_remote(_core_half(bufs[w], axes[w], 1 - c), _core_half(bufs[w], axes[w], 1 - c),
                         send_sems, recv_sems, w, (x, y, 1 - c)) for w in range(n)]
        return sends, recvs

    return make


def _chip_sum(g, r, axis, place, *, name):
    hk, hn = r.shape
    bk, bn = (hk // N_CHIPS, hn) if axis == 0 else (hk, hn // N_CHIPS)
    tk = _pick(bk, (512, 352, 256, 128))
    nk = bk // tk

    def body(p_ref, g_ref, r_ref, b_ref, own_ref):
        s = g_ref[...].astype(F32) + r_ref[...].astype(F32)
        b_ref[...] = s.astype(BF16)

        @pl.when(pl.program_id(1) == p_ref[0])
        def _():
            own_ref[...] = s

    if axis == 0:
        g_spec = pl.BlockSpec((tk, bn), lambda i, j, p: (j * nk + i, p[1]))
        r_spec = pl.BlockSpec((tk, bn), lambda i, j, p: (j * nk + i, 0))
    else:
        g_spec = pl.BlockSpec((tk, bn), lambda i, j, p: (p[1] * nk + i, j))
        r_spec = pl.BlockSpec((tk, bn), lambda i, j, p: (i, j))
    grid_spec = pltpu.PrefetchScalarGridSpec(
        num_scalar_prefetch=1, grid=(nk, N_CHIPS), in_specs=[g_spec, r_spec],
        out_specs=[r_spec, pl.BlockSpec((tk, bn), lambda i, j, p: (i, 0))])
    return pl.pallas_call(
        body, name=name,
        out_shape=[jax.ShapeDtypeStruct(r.shape, BF16), jax.ShapeDtypeStruct((bk, bn), F32)],
        grid_spec=grid_spec,
        compiler_params=pltpu.CompilerParams(dimension_semantics=("arbitrary", "arbitrary"),
                                             vmem_limit_bytes=_vmem(2 * tk * bn * 10 + 3 * tk * bn * 4)),
    )(place, g, r)


def _final_sum(own, recv, axis, place, *, name):
    _, bk, bn = recv.shape
    tk = _pick(bk, (256, 176, 128))
    nk = bk // tk

    def body(p_ref, o_ref, r_ref, out_ref):
        out_ref[...] = ((o_ref[...] + r_ref[0].astype(F32)) + r_ref[1].astype(F32)) + r_ref[2].astype(F32)

    own_spec = pl.BlockSpec((tk, bn), lambda i, p: (i, 0))
    if axis == 0:
        out_shape, out_spec = (bk, 2 * bn), pl.BlockSpec((tk, bn), lambda i, p: (i, p[1]))
    else:
        out_shape, out_spec = (2 * bk, bn), pl.BlockSpec((tk, bn), lambda i, p: (p[1] * nk + i, 0))
    grid_spec = pltpu.PrefetchScalarGridSpec(
        num_scalar_prefetch=1, grid=(nk,),
        in_specs=[own_spec, pl.BlockSpec((3, tk, bn), lambda i, p: (0, i, 0))], out_specs=out_spec)
    return pl.pallas_call(
        body, name=name, out_shape=jax.ShapeDtypeStruct(out_shape, F32), grid_spec=grid_spec,
        compiler_params=pltpu.CompilerParams(dimension_semantics=("arbitrary",),
                                             vmem_limit_bytes=_vmem(2 * tk * bn * 14 + 4 * tk * bn * 4)),
    )(place, own, recv)


def _allreduce_small(p, after=()):
    n_after = len(after)

    def body(*refs):
        p_ref = refs[0]
        o_ref, r0, r1, r2, send_sems, recv_sems = refs[1 + n_after:]
        x, y, c = _my_place()
        o_ref[...] = p_ref[...]
        for s, (peer, rbuf) in enumerate([((x, y, 1 - c), r0), ((1 - x, y, c), r1), ((x, 1 - y, c), r2)]):
            cp = pltpu.make_async_remote_copy(src_ref=o_ref, dst_ref=rbuf, send_sem=send_sems.at[s],
                                              recv_sem=recv_sems.at[s], device_id=peer, device_id_type=MESH)
            cp.start()
            cp.wait()
            o_ref[...] = o_ref[...] + rbuf[...]

    vm = pl.BlockSpec(memory_space=pltpu.VMEM)
    return pl.pallas_call(
        body, name="allreduce_small", out_shape=jax.ShapeDtypeStruct(p.shape, F32),
        in_specs=[vm] + [pl.BlockSpec(memory_space=pl.ANY)] * n_after, out_specs=vm,
        scratch_shapes=[pltpu.VMEM(p.shape, F32)] * 3 + [pltpu.SemaphoreType.DMA((3,))] * 2,
        compiler_params=pltpu.CompilerParams(vmem_limit_bytes=_vmem(6 * _nbytes(p.shape, F32))),
    )(p, *after)


def _pack_rows(parts):
    rows, metas = [], []
    for a in parts:
        flat = a.reshape(-1)
        nrow = -(-flat.shape[0] // LANES)
        nrow = -(-nrow // 8) * 8
        flat = jnp.pad(flat, (0, nrow * LANES - flat.shape[0]))
        rows.append(flat.reshape(nrow, LANES))
        metas.append((a.shape, nrow))
    return jnp.concatenate(rows, axis=0), metas


def _unpack_rows(packed, metas):
    out, r0 = [], 0
    for shape, nrow in metas:
        size = int(np.prod(shape))
        out.append(packed[r0:r0 + nrow].reshape(-1)[:size].reshape(shape))
        r0 += nrow
    return out


def kernel(x, positions, pre_mix_g, post_mix_g, pre_ffn_g, post_ffn_g, a_w_in, a_b_in, a_ln_g, a_ln_b, a_w_s, a_b_s, a_w_out, b_w_qkv, b_b_qkv, b_sinks, b_w_o, ffn_w_gu, ffn_w_down, loss_target, m_pre_mix_g, m_post_mix_g, m_pre_ffn_g, m_post_ffn_g, m_a_w_in, m_a_b_in, m_a_ln_g, m_a_ln_b, m_a_w_s, m_a_b_s, m_a_w_out, m_b_w_qkv, m_b_b_qkv, m_b_sinks, m_b_w_o, m_ffn_w_gu, m_ffn_w_down, v_pre_mix_g, v_post_mix_g, v_pre_ffn_g, v_post_ffn_g, v_a_w_in, v_a_b_in, v_a_ln_g, v_a_ln_b, v_a_w_s, v_a_b_s, v_a_w_out, v_b_w_qkv, v_b_b_qkv, v_b_sinks, v_b_w_o, v_ffn_w_gu, v_ffn_w_down):
    depth, D = pre_mix_g.shape
    xi, yi, ci = _my_place()
    chip = 2 * xi + yi
    place = jnp.stack([chip, ci]).astype(jnp.int32)

    stacked = {"a_w_in": (a_w_in, m_a_w_in, v_a_w_in), "a_w_out": (a_w_out, m_a_w_out, v_a_w_out),
               "b_w_qkv": (b_w_qkv, m_b_w_qkv, v_b_w_qkv), "b_w_o": (b_w_o, m_b_w_o, v_b_w_o),
               "ffn_w_gu": (ffn_w_gu, m_ffn_w_gu, v_ffn_w_gu), "ffn_w_down": (ffn_w_down, m_ffn_w_down, v_ffn_w_down)}
    cut = {"a_w_in": 1, "a_w_out": 0, "b_w_qkv": 1, "b_w_o": 0, "ffn_w_gu": 1, "ffn_w_down": 0}

    def layer_keys(i):
        mix = [("a_w_in", i // 2), ("a_w_out", i // 2)] if i % 2 == 0 else [("b_w_qkv", i // 2), ("b_w_o", i // 2)]
        return mix + [("ffn_w_gu", i), ("ffn_w_down", i)]

    def dep(a, toks):
        for t in toks:
            a = a + t[:1, :1]
        return a

    W = {}
    for i in range(depth):
        for nm, l in layer_keys(i):
            W[(nm, l)] = _cast_block(stacked[nm][0], l, cut[nm], place, name=f"cast_{nm}_{l}")

    def gather(tag, keys, after):
        axes = [cut[nm] for nm, _ in keys]
        for stage in (1, 2):
            ss, rs, bufs, tok = _split_start(f"ag{stage}_start_{tag}", [W[k] for k in keys], 3 * len(keys),
                                             _ag_copies(stage, axes), after)
            after = yield tok
            bufs = _split_wait(f"ag{stage}_wait_{tag}", bufs, (ss, rs), _ag_copies(stage, axes), after)
            W.update(zip(keys, bufs))
        yield None

    nq = b_b_qkv.shape[1]
    bq_full = jnp.zeros((b_b_qkv.shape[0], N_CHIPS * nq), F32)
    bq_full = lax.dynamic_update_slice(bq_full, jnp.where(ci == 0, b_b_qkv, 0.0), (0, chip * nq))
    bq_packed, bq_meta = _pack_rows([bq_full])
    bq_gathered = _allreduce_small(bq_packed)
    b_qkv_full = _unpack_rows(bq_gathered, bq_meta)[0]

    first = gather("0m", layer_keys(0)[:2], bq_gathered)
    tok = next(first)
    tok = first.send([tok] + [W[k] for i in range(depth) for k in layer_keys(i)[2 if i == 0 else 0:]])
    first.send(tok)

    h = x[0]
    target = loss_target[0]
    ctab, stab = _rope_tables(positions[0])
    q_width = W[("b_w_o", 0)].shape[0]
    kv_width = N_KV_HEADS * HEAD_DIM
    row = lambda a, i: a[i:i + 1]
    gains = {"pre_mix": pre_mix_g[:, None], "post_mix": post_mix_g[:, None], "pre_ffn": pre_ffn_g[:, None],
             "post_ffn": post_ffn_g[:, None]}
    gain = lambda which, i: (gains[which], i)

    saved = []
    hn = None
    for i in range(depth):
        j = i // 2
        s = {"h": h}
        ffn_w = None
        if i == 0:
            ffn_w = gather("0f", layer_keys(0)[2:], W[("a_w_out", 0)])
            toks = [next(ffn_w)]
            nxt = gather("1", layer_keys(1), toks[0])
            toks.append(next(nxt))
            hn = _rms_fwd(h, gain("pre_mix", i), out_dtype=BF16, after=toks, name=f"rms_pre_mix_{i}")
        elif i + 1 < depth:
            nxt = gather(str(i + 1), layer_keys(i + 1), h)
            toks = [next(nxt)]
        else:
            toks = []
        s["hn"] = hn
        if i % 2 == 0:
            pre = _matmul(hn, W[("a_w_in", j)], mode="nn", bias=row(a_b_in, j), out_dtype=F32, after=toks,
                          name=f"gmlp_in_{i}")
            gated = _sgu_fwd(pre, row(a_ln_g, j), row(a_ln_b, j), a_w_s[j], a_b_s[j].T, name=f"sgu_fwd_{i}")
            mix = _matmul(gated, W[("a_w_out", j)], mode="nn", out_dtype=F32, name=f"gmlp_out_{i}")
            s.update(pre=pre, gated=gated)
        else:
            qkv = _matmul(hn, W[("b_w_qkv", j)], mode="nn", bias=row(b_qkv_full, j), out_dtype=F32, after=toks,
                          name=f"attn_qkv_{i}")
            qr, kr, vr = _rope_fwd(qkv, ctab, stab, q_width=q_width, kv_width=kv_width, name=f"rope_fwd_{i}")
            o = _attn_fwd(qr, kr, vr, row(b_sinks, j), name=f"attn_fwd_{i}")
            mix = _matmul(o, W[("b_w_o", j)], mode="nn", out_dtype=F32, name=f"attn_o_{i}")
            s.update(qr=qr, kr=kr, vr=vr, o=o)
        s["mix"] = mix
        toks = [ffn_w.send(mix)] if ffn_w else []
        h1, fn = _rms_res_norm(h, mix, gain("post_mix", i), gain("pre_ffn", i), after=toks, name=f"rms_post_mix_{i}")
        if ffn_w:
            ffn_w.send(h1)
        s["h1"] = h1
        g_pre, u_pre, act = _ffn_up(fn, W[("ffn_w_gu", i)][None], 0, name=f"ffn_up_{i}")
        f = _matmul(act, W[("ffn_w_down", i)], mode="nn", out_dtype=F32, name=f"ffn_down_{i}")
        if i + 1 < depth:
            toks = [nxt.send(f)]
            h, hn = _rms_res_norm(h1, f, gain("post_ffn", i), gain("pre_mix", i + 1), after=toks,
                                  name=f"rms_post_ffn_{i}")
            nxt.send(h)
        else:
            h = _rms_res(h1, f, gain("post_ffn", i), name=f"rms_post_ffn_{i}")
        s.update(fn=fn, g_pre=g_pre, u_pre=u_pre, act=act, f=f)
        saved.append(s)

    dh, df, loss_part, g_last = _loss_and_grad(h, target, saved[-1]["f"], gain("post_ffn", depth - 1), name="loss")

    big_out = {nm: tuple(lax.empty(w.shape, F32) for _ in range(4)) for nm, (w, _, _) in stacked.items()}

    def reduce_group(i, keys, grads):
        axes = [cut[nm] for nm, _ in keys]
        n = len(keys)
        lands = [lax.empty(_half_shape(g.shape, ax), BF16) for g, ax in zip(grads, axes)]
        ss, rs, bufs, tok = _split_start(f"rs_sibling_start_{i}", list(grads) + lands, n, _rs_sibling_copies(axes),
                                         place)
        after = yield tok
        bufs = _split_wait(f"rs_sibling_wait_{i}", bufs, (ss, rs), _rs_sibling_copies(axes), after)
        sums = [_chip_sum(bufs[w], bufs[n + w], axes[w], place, name=f"chip_sum_{keys[w][0]}_{keys[w][1]}")
                for w in range(n)]
        lands = [lax.empty((3,) + own.shape, BF16) for _, own in sums]
        ss, rs, bufs, tok = _split_start(f"rs_chip_start_{i}", [sb for sb, _ in sums] + lands, 3 * n,
                                         _rs_chip_copies(axes), place)
        after = yield tok
        bufs = _split_wait(f"rs_chip_wait_{i}", bufs, (ss, rs), _rs_chip_copies(axes), after)
        blocks = [_final_sum(sums[w][1], bufs[n + w], axes[w], place, name=f"final_sum_{keys[w][0]}_{keys[w][1]}")
                  for w in range(n)]
        ss, rs, bufs, tok = _split_start(f"rs_fill_start_{i}", blocks, n, _rs_fill_copies(axes), place)
        after = yield tok
        blocks = _split_wait(f"rs_fill_wait_{i}", bufs, (ss, rs), _rs_fill_copies(axes), after)
        for (nm, l), g in zip(keys, blocks):
            w, m, v = stacked[nm]
            big_out[nm] = tuple(_adamw_layer(w, m, v, g, l, big_out[nm], name=f"adamw_{nm}_{l}"))
        yield None

    reducing = []

    def advance(after, newest_only=False):
        toks = []
        for gen in (reducing[-1:] if newest_only else list(reducing)):
            tok = gen.send(after)
            if tok is None:
                reducing.remove(gen)
            else:
                toks.append(tok)
        return toks

    small = {}
    g_pre_mix, g_post_mix, g_pre_ffn, g_post_ffn = [None] * depth, [None] * depth, [None] * depth, [None] * depth
    g_post_ffn[depth - 1] = g_last
    toks = []
    early = []
    for i in reversed(range(depth)):
        j = i // 2
        s = saved[i]
        g_down = _matmul(s["act"], df, mode="tn", out_dtype=BF16, after=toks, name=f"ffn_down_dw_{i}")
        dg_, du_ = _ffn_down_dx(df, W[("ffn_w_down", i)][None], 0, s["g_pre"], s["u_pre"], g_down,
                                name=f"ffn_down_dx_{i}")
        g_gu = _matmul_pair(s["fn"], dg_, du_, mode="tn", out_dtype=BF16, name=f"ffn_gu_dw_{i}")
        dfn = _matmul_pair(dg_, W[("ffn_w_gu", i)], du_, mode="nt", out_dtype=F32, after=[g_gu],
                           name=f"ffn_gu_dx_{i}")
        toks = advance(dfn)
        if i == 0:
            gen = reduce_group("0f", layer_keys(0)[2:], [g_gu, g_down])
            toks.append(next(gen))
            reducing.append(gen)
        dh1, dmix, g_pre_ffn[i], g_post_mix[i] = _rms_bwd_chain(
            s["h1"], gain("pre_ffn", i), dfn, dh, s["mix"], gain("post_mix", i), after=toks,
            name=f"rms_ffn_mix_bwd_{i}")
        if i % 2 == 0:
            g_out = _matmul(s["gated"], dmix, mode="tn", out_dtype=BF16, name=f"gmlp_out_dw_{i}")
            dgated = _matmul(dmix, W[("a_w_out", j)], mode="nt", out_dtype=BF16, after=[g_out],
                             name=f"gmlp_out_dx_{i}")
            toks = advance(dgated, newest_only=True) if i == 0 else []
            dpre, dws, dbsT, dlng, dlnb, dbin = _sgu_bwd(s["pre"], dgated, dep(row(a_ln_g, j), toks), row(a_ln_b, j),
                                                         a_w_s[j], a_b_s[j].T, name=f"sgu_bwd_{i}")
            small[("a_w_s", j)] = dws
            small[("a_b_s", j)] = dbsT.T
            small[("a_ln_g", j)] = dlng
            small[("a_ln_b", j)] = dlnb
            small[("a_b_in", j)] = dbin
            g_in = _matmul(s["hn"], dpre, mode="tn", out_dtype=BF16, name=f"gmlp_in_dw_{i}")
            if i == 0:
                last = reduce_group("0m", layer_keys(0)[:2], [g_in, g_out])
                early = [next(last)]
            dhn = _matmul(dpre, W[("a_w_in", j)], mode="nt", out_dtype=F32, after=[g_in] + early,
                          name=f"gmlp_in_dx_{i}")
        else:
            g_out = _matmul(s["o"], dmix, mode="tn", out_dtype=BF16, name=f"attn_o_dw_{i}")
            do = _matmul(dmix, W[("b_w_o", j)], mode="nt", out_dtype=BF16, after=[g_out], name=f"attn_o_dx_{i}")
            dq, dkp, dkc, dvp, dvc, dsk = _attn_bwd(s["qr"], s["kr"], s["vr"], row(b_sinks, j), do,
                                                    name=f"attn_bwd_{i}")
            dqkv, dbq = _rope_bwd(dq, dkp, dkc, dvp, dvc, ctab, stab, name=f"rope_bwd_{i}")
            small[("b_sinks", j)] = dsk[:, :b_sinks.shape[1]]
            small[("b_b_qkv", j)] = dbq
            g_in = _matmul(s["hn"], dqkv, mode="tn", out_dtype=BF16, name=f"attn_qkv_dw_{i}")
            if i == 0:
                last = reduce_group("0m", layer_keys(0)[:2], [g_in, g_out])
                early = [next(last)]
            dhn = _matmul(dqkv, W[("b_w_qkv", j)], mode="nt", out_dtype=F32, after=[g_in] + early,
                          name=f"attn_qkv_dx_{i}")
        toks = advance(dhn)
        if i > 0:
            dh, df, g_pre_mix[i], g_post_ffn[i - 1] = _rms_bwd_chain(
                s["h"], gain("pre_mix", i), dhn, dh1, saved[i - 1]["f"], gain("post_ffn", i - 1), after=toks,
                name=f"rms_mix_ffn_bwd_{i}")
            gen = reduce_group(str(i), layer_keys(i), [g_in, g_out, g_gu, g_down])
            toks = [next(gen)] + advance(dh)
            reducing.append(gen)
        else:
            toks.append(last.send(dhn))
            dh, g_pre_mix[i] = _rms_bwd(s["h"], gain("pre_mix", i), dhn, dh1, out_dtype=F32, after=toks,
                                        name=f"rms_pre_mix_bwd_{i}")
            advance(dh)
    grad_x = dh[None]
    assert not reducing

    ready = [big_out[nm][1] for nm in big_out]
    n_a, n_b = a_b_in.shape[0], b_sinks.shape[0]
    stack = lambda key, n: jnp.concatenate([small[(key, j)] for j in range(n)], axis=0)
    small_parts = [
        jnp.concatenate(g_pre_mix, axis=0), jnp.concatenate(g_post_mix, axis=0),
        jnp.concatenate(g_pre_ffn, axis=0), jnp.concatenate(g_post_ffn, axis=0),
        stack("a_b_in", n_a), stack("a_ln_g", n_a), stack("a_ln_b", n_a),
        jnp.stack([small[("a_w_s", j)] for j in range(n_a)]), jnp.stack([small[("a_b_s", j)] for j in range(n_a)]),
        stack("b_b_qkv", n_b), stack("b_sinks", n_b), loss_part,
    ]
    packed, metas = _pack_rows(small_parts)
    reduced = _allreduce_small(packed, after=ready + [dh])
    while last.send(reduced) is not None:
        pass
    red = _unpack_rows(reduced, metas)
    (gr_pre_mix, gr_post_mix, gr_pre_ffn, gr_post_ffn, gr_b_in, gr_ln_g, gr_ln_b, gr_w_s, gr_b_s,
     gr_b_qkv_full, gr_sinks, loss_sum) = red
    loss = loss_sum[0, 0]
    gr_b_qkv = lax.dynamic_slice(gr_b_qkv_full, (0, chip * nq), (gr_b_qkv_full.shape[0], nq))

    grads = {"pre_mix_g": gr_pre_mix, "post_mix_g": gr_post_mix, "pre_ffn_g": gr_pre_ffn, "post_ffn_g": gr_post_ffn,
             "a_b_in": gr_b_in, "a_ln_g": gr_ln_g, "a_ln_b": gr_ln_b, "a_w_s": gr_w_s, "a_b_s": gr_b_s,
             "b_b_qkv": gr_b_qkv, "b_sinks": gr_sinks}
    weights = {"pre_mix_g": (pre_mix_g, m_pre_mix_g, v_pre_mix_g), "post_mix_g": (post_mix_g, m_post_mix_g, v_post_mix_g),
               "pre_ffn_g": (pre_ffn_g, m_pre_ffn_g, v_pre_ffn_g), "post_ffn_g": (post_ffn_g, m_post_ffn_g, v_post_ffn_g),
               "a_b_in": (a_b_in, m_a_b_in, v_a_b_in), "a_ln_g": (a_ln_g, m_a_ln_g, v_a_ln_g),
               "a_ln_b": (a_ln_b, m_a_ln_b, v_a_ln_b), "a_w_s": (a_w_s, m_a_w_s, v_a_w_s), "a_b_s": (a_b_s, m_a_b_s, v_a_b_s),
               "b_b_qkv": (b_b_qkv, m_b_b_qkv, v_b_b_qkv), "b_sinks": (b_sinks, m_b_sinks, v_b_sinks)}
    order = ["pre_mix_g", "post_mix_g", "pre_ffn_g", "post_ffn_g", "a_w_in", "a_b_in", "a_ln_g", "a_ln_b", "a_w_s",
             "a_b_s", "a_w_out", "b_w_qkv", "b_b_qkv", "b_sinks", "b_w_o", "ffn_w_gu", "ffn_w_down"]
    deltas, new_m, new_v = {}, {}, {}
    for nm in order:
        if nm in big_out:
            grads[nm], deltas[nm], new_m[nm], new_v[nm] = big_out[nm]
        else:
            w, m, v = weights[nm]
            deltas[nm], new_m[nm], new_v[nm] = _adamw_small(w, grads[nm], m, v, name="adamw_" + nm)
    return (loss, grad_x, *[grads[nm] for nm in order], *[deltas[nm] for nm in order],
            *[new_m[nm] for nm in order], *[new_v[nm] for nm in order])
```

```python
import functools
import math

import jax
import jax.numpy as jnp
import numpy as np
from jax import lax
from jax.experimental import pallas as pl
from jax.experimental.pallas import tpu as pltpu

F32 = jnp.float32
BF16 = jnp.bfloat16
MESH = pl.DeviceIdType.MESH

HEAD_DIM = 64
N_KV_HEADS = 4
ROPE_DIM = 16
ROPE_THETA = 500000.0
CHUNK = 128
GMLP_GROUPS = 8
RMS_EPS = 1e-6
LN_EPS = 1e-5
NEG_INF = -1e30
ADAM_LR = 0.001
ADAM_B1 = 0.9
ADAM_B2 = 0.999
ADAM_EPS = 1e-08
ADAM_WD = 0.01
ADAM_STEP = 10

N_CHIPS = 4
LANES = 128
VMEM_CAP = 58 * 1024 * 1024


def _vmem(est_bytes):
    assert est_bytes < VMEM_CAP
    return VMEM_CAP


def _pick(n, cands):
    for c in cands:
        if c <= n and n % c == 0:
            return c
    return n


def _nbytes(shape, dtype):
    return int(np.prod(shape)) * jnp.dtype(dtype).itemsize


MATMUL_VMEM_BUDGET = 48 * 1024 * 1024
MXU_COLS = 256


def _halvings(n, unit):
    out, t = [], n
    while t % unit == 0 and t >= unit:
        out.append(t)
        if t % 2:
            break
        t //= 2
    return out


def _matmul_tiles(P, Q, R, a_bytes, b_bytes, o_bytes, full_addend, tp, tq, tr, repeat=1):
    step_us, bytes_per_us, flops_per_us = 0.85, 3.2e6, 9.0e8
    best = None
    for p in ([tp] if tp else _halvings(P, LANES)):
        for q in ([tq] if tq else _halvings(Q, LANES)):
            for r in ([tr] if tr else _halvings(R, LANES)):
                nk = R // r
                vm = 2 * (p * r * a_bytes + r * q * b_bytes + p * q * o_bytes + (p * q * 4 if full_addend else 0))
                vm += p * q * 4 * (2 if nk > 1 else 1)
                if vm > MATMUL_VMEM_BUDGET:
                    continue
                exposed = (p * r * a_bytes + r * q * b_bytes + p * q * o_bytes) / bytes_per_us
                mxu_us = repeat * 2.0 * P * R * (Q // q) * (-(-q // MXU_COLS) * MXU_COLS) / flops_per_us
                key = (repeat * (P // p) * (Q // q) * nk * step_us + exposed + mxu_us, nk, abs(p - q))
                if best is None or key < best[0]:
                    best = (key, (p, q, r))
    assert best is not None, (P, Q, R)
    return best[1]


def _matmul(a, b, *, mode, out_dtype, name, bias=None, after=()):
    if mode == "nn":
        (P, R), (R2, Q) = a.shape, b.shape
    elif mode == "nt":
        (P, R), (Q, R2) = a.shape, b.shape
    else:
        (R, P), (R2, Q) = a.shape, b.shape
    assert R == R2, (mode, a.shape, b.shape)
    tp, tq, tr = _matmul_tiles(P, Q, R, a.dtype.itemsize, b.dtype.itemsize, jnp.dtype(out_dtype).itemsize, False,
                               None, None, None)
    nk = R // tr
    dims = {"nn": (((1,), (0,)), ((), ())), "nt": (((1,), (1,)), ((), ())), "tn": (((0,), (0,)), ((), ()))}[mode]
    if mode == "nn":
        a_spec = pl.BlockSpec((tp, tr), lambda i, j, k: (i, k))
        b_spec = pl.BlockSpec((tr, tq), lambda i, j, k: (k, j))
    elif mode == "nt":
        a_spec = pl.BlockSpec((tp, tr), lambda i, j, k: (i, k))
        b_spec = pl.BlockSpec((tq, tr), lambda i, j, k: (j, k))
    else:
        a_spec = pl.BlockSpec((tr, tp), lambda i, j, k: (k, i))
        b_spec = pl.BlockSpec((tr, tq), lambda i, j, k: (k, j))
    in_specs = [a_spec, b_spec]
    args = [a, b]
    has_bias = bias is not None
    if has_bias:
        in_specs.append(pl.BlockSpec((1, tq), lambda i, j, k: (0, j)))
        args.append(bias)
    out_shape = jax.ShapeDtypeStruct((P, Q), out_dtype)
    out_spec = pl.BlockSpec((tp, tq), lambda i, j, k: (i, j))
    n_in = len(args) + len(after)
    in_specs += [pl.BlockSpec(memory_space=pl.ANY)] * len(after)
    args += list(after)

    def body(*refs):
        a_ref, b_ref = refs[0], refs[1]
        bias_ref = refs[2] if has_bias else None
        o_ref = refs[n_in]
        acc_ref = refs[n_in + 1] if nk > 1 else None
        part = lax.dot_general(a_ref[...], b_ref[...], dims, preferred_element_type=F32)

        def finish(acc):
            if has_bias:
                acc = acc + bias_ref[...]
            o_ref[...] = acc.astype(out_dtype)

        if nk == 1:
            finish(part)
        else:
            k = pl.program_id(2)

            @pl.when(k == 0)
            def _():
                acc_ref[...] = part

            @pl.when(k > 0)
            def _():
                acc_ref[...] += part

            @pl.when(k == nk - 1)
            def _():
                finish(acc_ref[...])

    est = 2 * (_nbytes((tp, tr), a.dtype) + _nbytes((tr, tq), b.dtype) + _nbytes((tp, tq), out_dtype)) + 3 * tp * tq * 4
    return pl.pallas_call(
        body, name=name, out_shape=out_shape,
        grid=(P // tp, Q // tq, nk),
        in_specs=in_specs, out_specs=out_spec,
        scratch_shapes=[pltpu.VMEM((tp, tq), F32)] if nk > 1 else [],
        compiler_params=pltpu.CompilerParams(
            dimension_semantics=("parallel", "parallel", "arbitrary"), vmem_limit_bytes=_vmem(est)),
    )(*args)


def _matmul_pair(a, b, pair, *, mode, out_dtype, name, after=()):
    if mode == "tn":
        (R, P), (R2, Q) = a.shape, b.shape
        assert R == R2 and pair.shape == b.shape
        tp, tq, tr = _matmul_tiles(P, Q, R, a.dtype.itemsize, 2 * b.dtype.itemsize,
                                   jnp.dtype(out_dtype).itemsize, False, None, None, None, repeat=2)
        nq, nk = Q // tq, R // tr
        grid, nk_total = (P // tp, 2 * nq, nk), nk
        a_spec = pl.BlockSpec((tr, tp), lambda i, j, k: (k, i))
        b_spec = pl.BlockSpec((tr, tq), lambda i, j, k: (jnp.where(j < nq, k, nk - 1), jnp.minimum(j, nq - 1)))
        p_spec = pl.BlockSpec((tr, tq), lambda i, j, k: (jnp.where(j >= nq, k, 0), jnp.maximum(j - nq, 0)))
        out_shape = (P, 2 * Q)
        dims = (((0,), (0,)), ((), ()))
    else:
        assert mode == "nt"
        (P, R), (Q, R2) = a.shape, b.shape
        assert R2 == 2 * R and pair.shape == a.shape
        tp, tq, tr = _matmul_tiles(P, Q, R, 2 * a.dtype.itemsize, b.dtype.itemsize,
                                   jnp.dtype(out_dtype).itemsize, False, None, None, None, repeat=2)
        nk = R // tr
        grid, nk_total = (P // tp, Q // tq, 2 * nk), 2 * nk
        a_spec = pl.BlockSpec((tp, tr), lambda i, j, k: (i, jnp.minimum(k, nk - 1)))
        p_spec = pl.BlockSpec((tp, tr), lambda i, j, k: (i, jnp.maximum(k - nk, 0)))
        b_spec = pl.BlockSpec((tq, tr), lambda i, j, k: (j, k))
        out_shape = (P, Q)
        dims = (((1,), (1,)), ((), ()))
    n_after = len(after)

    def body(a_ref, b_ref, p_ref, *rest):
        o_ref = rest[n_after]
        acc_ref = rest[n_after + 1] if nk_total > 1 else None
        j, k = pl.program_id(1), pl.program_id(2)

        def step(l_ref, r_ref):
            part = lax.dot_general(l_ref[...], r_ref[...], dims, preferred_element_type=F32)
            if nk_total == 1:
                o_ref[...] = part.astype(out_dtype)
                return

            @pl.when(k == 0)
            def _():
                acc_ref[...] = part

            @pl.when(k > 0)
            def _():
                acc_ref[...] += part

            @pl.when(k == nk_total - 1)
            def _():
                o_ref[...] = acc_ref[...].astype(out_dtype)

        first = (j < nq) if mode == "tn" else (k < nk)

        @pl.when(first)
        def _():
            step(a_ref, b_ref)

        @pl.when(jnp.logical_not(first))
        def _():
            step(a_ref if mode == "tn" else p_ref, p_ref if mode == "tn" else b_ref)

    n_a, n_b = (1, 2) if mode == "tn" else (2, 1)
    est = (2 * (n_a * _nbytes((tp, tr), a.dtype) + n_b * _nbytes((tr, tq), b.dtype) + _nbytes((tp, tq), out_dtype))
           + 2 * tp * tq * 4)
    return pl.pallas_call(
        body, name=name, out_shape=jax.ShapeDtypeStruct(out_shape, out_dtype), grid=grid,
        in_specs=[a_spec, b_spec, p_spec] + [pl.BlockSpec(memory_space=pl.ANY)] * n_after,
        out_specs=pl.BlockSpec((tp, tq), lambda i, j, k: (i, j)),
        scratch_shapes=[pltpu.VMEM((tp, tq), F32)] if nk_total > 1 else [],
        compiler_params=pltpu.CompilerParams(
            dimension_semantics=("parallel", "parallel", "arbitrary"), vmem_limit_bytes=_vmem(est)),
    )(a, b, pair, *after)


def _row_call(body, ins, outs, *, name, rows, tr, acc_outs=(), est=0, after=()):
    in_specs, args = [], []
    for arr, kind in ins:
        if kind == "row":
            in_specs.append(pl.BlockSpec((tr, arr.shape[1]), lambda i: (i, 0)))
        elif isinstance(arr, tuple):
            arr, layer = arr
            in_specs.append(pl.BlockSpec((None,) + arr.shape[1:], lambda i, layer=layer: (layer, 0, 0)))
        else:
            nd = arr.ndim
            in_specs.append(pl.BlockSpec(arr.shape, lambda i, nd=nd: (0,) * nd))
        args.append(arr)
    n_ins = len(args)
    in_specs += [pl.BlockSpec(memory_space=pl.ANY)] * len(after)
    args += list(after)

    def kernel_fn(*refs):
        body(*refs[:n_ins], *refs[n_ins + len(after):])

    out_shapes = [jax.ShapeDtypeStruct(s, d) for s, d in outs] + [jax.ShapeDtypeStruct(s, d) for s, d in acc_outs]
    out_specs = [pl.BlockSpec((tr, s[1]), lambda i: (i, 0)) for s, _ in outs]
    out_specs += [pl.BlockSpec(s, lambda i, nd=len(s): (0,) * nd) for s, _ in acc_outs]
    res = pl.pallas_call(
        kernel_fn, name=name, out_shape=out_shapes, grid=(rows // tr,), in_specs=in_specs, out_specs=out_specs,
        compiler_params=pltpu.CompilerParams(dimension_semantics=("arbitrary",), vmem_limit_bytes=_vmem(est)),
    )(*args)
    return res


def _rms_fwd(x, g, *, out_dtype, name, after=()):
    T, D = x.shape
    tr = _pick(T, (512, 256, 128))

    def body(x_ref, g_ref, o_ref):
        xv = x_ref[...]
        r = lax.rsqrt(jnp.mean(xv * xv, axis=-1, keepdims=True) + RMS_EPS)
        o_ref[...] = (xv * r * g_ref[...]).astype(out_dtype)

    return _row_call(body, [(x, "row"), (g, "full")], [((T, D), out_dtype)], name=name, rows=T, tr=tr,
                     est=8 * tr * D * 4, after=after)[0]


def _rms_res(h, y, g, *, name):
    T, D = h.shape
    tr = _pick(T, (512, 256, 128))

    def body(h_ref, y_ref, g_ref, o_ref):
        yv = y_ref[...]
        r = lax.rsqrt(jnp.mean(yv * yv, axis=-1, keepdims=True) + RMS_EPS)
        o_ref[...] = h_ref[...] + yv * r * g_ref[...]

    return _row_call(body, [(h, "row"), (y, "row"), (g, "full")], [((T, D), F32)], name=name, rows=T, tr=tr,
                     est=10 * tr * D * 4)[0]


def _rms_bwd(x, g, dy, dres, *, out_dtype, name, after=()):
    T, D = x.shape
    tr = _pick(T, (512, 256, 128))
    has_res = dres is not None

    def body(*refs):
        if has_res:
            x_ref, g_ref, dy_ref, dr_ref, dx_ref, dg_ref = refs
        else:
            x_ref, g_ref, dy_ref, dx_ref, dg_ref = refs
        xv = x_ref[...]
        r = lax.rsqrt(jnp.mean(xv * xv, axis=-1, keepdims=True) + RMS_EPS)
        xhat = xv * r
        dyv = dy_ref[...].astype(F32)
        dxn = dyv * g_ref[...]
        dx = r * (dxn - xhat * jnp.mean(dxn * xhat, axis=-1, keepdims=True))
        if has_res:
            dx = dx + dr_ref[...]
        dx_ref[...] = dx.astype(out_dtype)
        part = jnp.sum(dyv * xhat, axis=0, keepdims=True)

        @pl.when(pl.program_id(0) == 0)
        def _():
            dg_ref[...] = part

        @pl.when(pl.program_id(0) > 0)
        def _():
            dg_ref[...] += part

    ins = [(x, "row"), (g, "full"), (dy, "row")] + ([(dres, "row")] if has_res else [])
    dx, dg = _row_call(body, ins, [((T, D), out_dtype)], name=name, rows=T, tr=tr, acc_outs=[((1, D), F32)],
                       est=12 * tr * D * 4, after=after)
    return dx, dg


def _rms_res_norm(h, y, g_res, g_next, *, name, after=()):
    T, D = h.shape
    tr = _pick(T, (512, 256, 128))

    def body(h_ref, y_ref, g_ref, gn_ref, o_ref, n_ref):
        yv = y_ref[...]
        r = lax.rsqrt(jnp.mean(yv * yv, axis=-1, keepdims=True) + RMS_EPS)
        h2 = h_ref[...] + yv * r * g_ref[...]
        o_ref[...] = h2
        r2 = lax.rsqrt(jnp.mean(h2 * h2, axis=-1, keepdims=True) + RMS_EPS)
        n_ref[...] = (h2 * r2 * gn_ref[...]).astype(BF16)

    return _row_call(body, [(h, "row"), (y, "row"), (g_res, "full"), (g_next, "full")],
                     [((T, D), F32), ((T, D), BF16)], name=name, rows=T, tr=tr, est=12 * tr * D * 4, after=after)


def _rms_bwd_chain(x1, g1, dy1, dres, x2, g2, *, name, after=()):
    T, D = x1.shape
    tr = _pick(T, (512, 256, 128))

    def one(xv, gv, dyv):
        r = lax.rsqrt(jnp.mean(xv * xv, axis=-1, keepdims=True) + RMS_EPS)
        xhat = xv * r
        dxn = dyv * gv
        dx = r * (dxn - xhat * jnp.mean(dxn * xhat, axis=-1, keepdims=True))
        return dx, jnp.sum(dyv * xhat, axis=0, keepdims=True)

    def body(x1_ref, g1_ref, dy1_ref, dr_ref, x2_ref, g2_ref, d1_ref, d2_ref, dg1_ref, dg2_ref):
        dx1, p1 = one(x1_ref[...], g1_ref[...], dy1_ref[...].astype(F32))
        d1 = dx1 + dr_ref[...]
        d1_ref[...] = d1
        dx2, p2 = one(x2_ref[...], g2_ref[...], d1)
        d2_ref[...] = dx2.astype(BF16)

        @pl.when(pl.program_id(0) == 0)
        def _():
            dg1_ref[...] = p1
            dg2_ref[...] = p2

        @pl.when(pl.program_id(0) > 0)
        def _():
            dg1_ref[...] += p1
            dg2_ref[...] += p2

    ins = [(x1, "row"), (g1, "full"), (dy1, "row"), (dres, "row"), (x2, "row"), (g2, "full")]
    return _row_call(body, ins, [((T, D), F32), ((T, D), BF16)], name=name, rows=T, tr=tr,
                     acc_outs=[((1, D), F32), ((1, D), F32)], est=20 * tr * D * 4, after=after)


def _ffn_up(fn, w_gu, l, *, name):
    T, D = fn.shape
    H = w_gu.shape[2] // 2
    tp = _pick(T, (256, 128))
    tq = H
    nj = H // tq

    def body(a_ref, wg_ref, wu_ref, g_ref, u_ref, act_ref):
        a = a_ref[...]
        g = jnp.dot(a, wg_ref[...], preferred_element_type=F32)
        u = jnp.dot(a, wu_ref[...], preferred_element_type=F32)
        sg = jax.nn.sigmoid(g)
        silu = g * sg
        g_ref[...] = (u * (sg + silu * (1.0 - sg))).astype(BF16)
        u_ref[...] = silu.astype(BF16)
        act_ref[...] = (silu * u).astype(BF16)

    tile = pl.BlockSpec((tp, tq), lambda j, i: (i, j))
    est = 2 * (tp * D * 2 + 2 * D * tq * 2 + 3 * tp * tq * 2) + 4 * tp * tq * 4
    return pl.pallas_call(
        body, name=name,
        out_shape=[jax.ShapeDtypeStruct((T, H), BF16), jax.ShapeDtypeStruct((T, H), BF16),
                   jax.ShapeDtypeStruct((T, H), BF16)],
        grid=(nj, T // tp),
        in_specs=[pl.BlockSpec((tp, D), lambda j, i: (i, 0)),
                  pl.BlockSpec((None, D, tq), lambda j, i: (l, 0, j)),
                  pl.BlockSpec((None, D, tq), lambda j, i: (l, 0, j + nj))],
        out_specs=[tile, tile, tile],
        compiler_params=pltpu.CompilerParams(dimension_semantics=("parallel", "parallel"),
                                             vmem_limit_bytes=_vmem(est)),
    )(fn, w_gu, w_gu)


def _ffn_down_dx(df, w_down, l, g, u, after, *, name):
    T, D = df.shape
    H = w_down.shape[1]
    tp = _pick(T, (512, 256, 128))
    tq = H

    def body(a_ref, w_ref, g_ref, u_ref, _, dg_ref, du_ref):
        da = lax.dot_general(a_ref[...], w_ref[...], (((1,), (1,)), ((), ())), preferred_element_type=F32)
        dg_ref[...] = (da * g_ref[...].astype(F32)).astype(BF16)
        du_ref[...] = (da * u_ref[...].astype(F32)).astype(BF16)

    tile = pl.BlockSpec((tp, tq), lambda j, i: (i, j))
    est = 2 * (tp * D * 2 + tq * D * 2 + 4 * tp * tq * 2) + 3 * tp * tq * 4
    return pl.pallas_call(
        body, name=name,
        out_shape=[jax.ShapeDtypeStruct((T, H), BF16), jax.ShapeDtypeStruct((T, H), BF16)],
        grid=(H // tq, T // tp),
        in_specs=[pl.BlockSpec((tp, D), lambda j, i: (i, 0)),
                  pl.BlockSpec((None, tq, D), lambda j, i: (l, j, 0)), tile, tile,
                  pl.BlockSpec(memory_space=pl.ANY)],
        out_specs=[tile, tile],
        compiler_params=pltpu.CompilerParams(dimension_semantics=("parallel", "parallel"),
                                             vmem_limit_bytes=_vmem(est)),
    )(df, w_down, g, u, after)


def _loss_and_grad(y, target, x, g, *, name):
    T, D = y.shape
    tr = _pick(T, (512, 256, 128))

    def body(y_ref, t_ref, x_ref, g_ref, dy_ref, dx_ref, l_ref, dg_ref):
        e = y_ref[...] - t_ref[...]
        dy = e * (1.0 / D)
        dy_ref[...] = dy
        part = jnp.sum(jnp.sum(e * e, axis=1, keepdims=True), axis=0, keepdims=True) * (0.5 / D)
        xv = x_ref[...]
        r = lax.rsqrt(jnp.mean(xv * xv, axis=-1, keepdims=True) + RMS_EPS)
        xhat = xv * r
        dxn = dy * g_ref[...]
        dx_ref[...] = (r * (dxn - xhat * jnp.mean(dxn * xhat, axis=-1, keepdims=True))).astype(BF16)
        dg = jnp.sum(dy * xhat, axis=0, keepdims=True)

        @pl.when(pl.program_id(0) == 0)
        def _():
            l_ref[...] = part
            dg_ref[...] = dg

        @pl.when(pl.program_id(0) > 0)
        def _():
            l_ref[...] += part
            dg_ref[...] += dg

    dy, dx, l, dg = _row_call(body, [(y, "row"), (target, "row"), (x, "row"), (g, "full")],
                              [((T, D), F32), ((T, D), BF16)], name=name, rows=T, tr=tr,
                              acc_outs=[((1, 1), F32), ((1, D), F32)], est=14 * tr * D * 4)
    return dy, dx, l, dg


_SQRT_HALF = 0.7071067811865476
_INV_SQRT_2PI = 0.3989422804014327


def _gelu_parts(x):
    cdf = 0.5 * (1.0 + lax.erf(x * _SQRT_HALF))
    return cdf


def _sgu_common(pre, lng, lnb, W):
    cdf = _gelu_parts(pre)
    z = pre * cdf
    u = z[:, :W]
    v = z[:, W:]
    mu = jnp.mean(v, axis=-1, keepdims=True)
    vc = v - mu
    var = jnp.mean(vc * vc, axis=-1, keepdims=True)
    rstd = lax.rsqrt(var + LN_EPS)
    vhat = vc * rstd
    vn = vhat * lng + lnb
    return cdf, u, vhat, rstd, vn


def _causal_mask():
    t = lax.broadcasted_iota(jnp.int32, (CHUNK, CHUNK), 0)
    s = lax.broadcasted_iota(jnp.int32, (CHUNK, CHUNK), 1)
    return t >= s


def _sgu_fwd(pre, lng, lnb, ws, bsT, *, name):
    T, W2 = pre.shape
    W = W2 // 2
    G = ws.shape[0]
    gd = W // G

    def body(pre_ref, lng_ref, lnb_ref, ws_ref, bs_ref, o_ref):
        _, u, _, _, vn = _sgu_common(pre_ref[...], lng_ref[...], lnb_ref[...], W)
        vnb = vn.astype(BF16)
        causal = _causal_mask()
        for g in range(G):
            w = jnp.where(causal, ws_ref[g], 0.0).astype(BF16)
            sv = jnp.dot(w, vnb[:, g * gd:(g + 1) * gd], preferred_element_type=F32) + bs_ref[:, g:g + 1]
            o_ref[:, g * gd:(g + 1) * gd] = (u[:, g * gd:(g + 1) * gd] * sv).astype(BF16)

    return pl.pallas_call(
        body, name=name, out_shape=jax.ShapeDtypeStruct((T, W), BF16), grid=(T // CHUNK,),
        in_specs=[pl.BlockSpec((CHUNK, W2), lambda i: (i, 0)),
                  pl.BlockSpec((1, W), lambda i: (0, 0)), pl.BlockSpec((1, W), lambda i: (0, 0)),
                  pl.BlockSpec(ws.shape, lambda i: (0, 0, 0)), pl.BlockSpec(bsT.shape, lambda i: (0, 0))],
        out_specs=pl.BlockSpec((CHUNK, W), lambda i: (i, 0)),
        compiler_params=pltpu.CompilerParams(dimension_semantics=("arbitrary",),
                                             vmem_limit_bytes=_vmem(12 * CHUNK * W2 * 4)),
    )(pre, lng, lnb, ws, bsT)


def _sgu_bwd(pre, dgated, lng, lnb, ws, bsT, *, name):
    T, W2 = pre.shape
    W = W2 // 2
    G = ws.shape[0]
    gd = W // G

    def body(pre_ref, dgt_ref, lng_ref, lnb_ref, ws_ref, bs_ref,
             dpre_ref, dws_ref, dbs_ref, dlng_ref, dlnb_ref, dbin_ref):
        first = pl.program_id(0) == 0

        @pl.when(first)
        def _():
            dws_ref[...] = jnp.zeros_like(dws_ref)
            dbs_ref[...] = jnp.zeros_like(dbs_ref)
            dlng_ref[...] = jnp.zeros_like(dlng_ref)
            dlnb_ref[...] = jnp.zeros_like(dlnb_ref)
            dbin_ref[...] = jnp.zeros_like(dbin_ref)

        pre_v = pre_ref[...]
        lng_v = lng_ref[...]
        cdf, u, vhat, rstd, vn = _sgu_common(pre_v, lng_v, lnb_ref[...], W)
        vnb = vn.astype(BF16)
        dgt = dgt_ref[...].astype(F32)
        causal = _causal_mask()
        du_parts, dvn_parts = [], []
        for g in range(G):
            sl = slice(g * gd, (g + 1) * gd)
            w = jnp.where(causal, ws_ref[g], 0.0).astype(BF16)
            sv = jnp.dot(w, vnb[:, sl], preferred_element_type=F32) + bs_ref[:, g:g + 1]
            dgt_g = dgt[:, sl]
            du_parts.append(dgt_g * sv)
            dsv = dgt_g * u[:, sl]
            dsvb = dsv.astype(BF16)
            dvn_parts.append(lax.dot_general(w, dsvb, (((0,), (0,)), ((), ())), preferred_element_type=F32))
            dw = lax.dot_general(dsvb, vnb[:, sl], (((1,), (1,)), ((), ())), preferred_element_type=F32)
            dws_ref[g] += jnp.where(causal, dw, 0.0)
            dbs_ref[:, g:g + 1] += jnp.sum(dsv, axis=1, keepdims=True)
        du = jnp.concatenate(du_parts, axis=1)
        dvn = jnp.concatenate(dvn_parts, axis=1)
        dlng_ref[...] += jnp.sum(dvn * vhat, axis=0, keepdims=True)
        dlnb_ref[...] += jnp.sum(dvn, axis=0, keepdims=True)
        dvh = dvn * lng_v
        dv = rstd * (dvh - jnp.mean(dvh, axis=-1, keepdims=True)
                     - vhat * jnp.mean(dvh * vhat, axis=-1, keepdims=True))
        dz = jnp.concatenate([du, dv], axis=1)
        dgelu = cdf + pre_v * jnp.exp(-0.5 * pre_v * pre_v) * _INV_SQRT_2PI
        dpre = dz * dgelu
        dbin_ref[...] += jnp.sum(dpre, axis=0, keepdims=True)
        dpre_ref[...] = dpre.astype(BF16)

    full = lambda shape: pl.BlockSpec(shape, lambda i, nd=len(shape): (0,) * nd)
    return pl.pallas_call(
        body, name=name,
        out_shape=[jax.ShapeDtypeStruct((T, W2), BF16), jax.ShapeDtypeStruct(ws.shape, F32),
                   jax.ShapeDtypeStruct(bsT.shape, F32), jax.ShapeDtypeStruct((1, W), F32),
                   jax.ShapeDtypeStruct((1, W), F32), jax.ShapeDtypeStruct((1, W2), F32)],
        grid=(T // CHUNK,),
        in_specs=[pl.BlockSpec((CHUNK, W2), lambda i: (i, 0)), pl.BlockSpec((CHUNK, W), lambda i: (i, 0)),
                  full((1, W)), full((1, W)), full(ws.shape), full(bsT.shape)],
        out_specs=[pl.BlockSpec((CHUNK, W2), lambda i: (i, 0)), full(ws.shape), full(bsT.shape),
                   full((1, W)), full((1, W)), full((1, W2))],
        compiler_params=pltpu.CompilerParams(dimension_semantics=("arbitrary",),
                                             vmem_limit_bytes=_vmem(24 * CHUNK * W2 * 4)),
    )(pre, dgated, lng, lnb, ws, bsT)


def _rope_tables(positions):
    half = ROPE_DIM // 2
    inv_freq = ROPE_THETA ** (-jnp.arange(0, ROPE_DIM, 2, dtype=F32) / ROPE_DIM)
    ang = positions.astype(F32).reshape(-1, 1) * inv_freq
    cos, sin = jnp.cos(ang), jnp.sin(ang)
    T = ang.shape[0]
    rest = HEAD_DIM - ROPE_DIM
    c64 = jnp.concatenate([cos, cos, jnp.ones((T, rest), F32)], axis=1)
    s64 = jnp.concatenate([-sin, sin, jnp.zeros((T, rest), F32)], axis=1)
    del half
    return jnp.tile(c64, (1, LANES // HEAD_DIM)), jnp.tile(s64, (1, LANES // HEAD_DIM))


def _swap8(x):
    W = x.shape[1]
    half = ROPE_DIM // 2
    lane = lax.broadcasted_iota(jnp.int32, x.shape, 1) % HEAD_DIM
    return jnp.where(lane < half, pltpu.roll(x, W - half, axis=1),
                     jnp.where(lane < ROPE_DIM, pltpu.roll(x, half, axis=1), 0.0))


def _wide(tab, W):
    return jnp.concatenate([tab] * (W // LANES), axis=1) if W > LANES else tab


def _rope_fwd(qkv, ctab, stab, *, q_width, kv_width, name):
    T = qkv.shape[0]
    tr = _pick(T, (256, 128))
    scale = HEAD_DIM ** -0.5

    def body(x_ref, c_ref, s_ref, q_ref, k_ref, v_ref):
        c = c_ref[...]
        s = s_ref[...]
        q = x_ref[:, :q_width]
        k = x_ref[:, q_width:q_width + kv_width]
        q_ref[...] = ((q * _wide(c, q_width) + _swap8(q) * _wide(s, q_width)) * scale).astype(BF16)
        k_ref[...] = (k * _wide(c, kv_width) + _swap8(k) * _wide(s, kv_width)).astype(BF16)
        v_ref[...] = x_ref[:, q_width + kv_width:].astype(BF16)

    return _row_call(body, [(qkv, "row"), (ctab, "row"), (stab, "row")],
                     [((T, q_width), BF16), ((T, kv_width), BF16), ((T, kv_width), BF16)],
                     name=name, rows=T, tr=tr, est=10 * tr * qkv.shape[1] * 4)


_NT = (((1,), (1,)), ((), ()))
_TN = (((0,), (0,)), ((), ()))


def _group_rows(ref, heads):
    return jnp.concatenate([ref[:, h * HEAD_DIM:(h + 1) * HEAD_DIM] for h in heads], axis=0)


def _attn_valid(grp):
    qi = np.arange(grp * CHUNK)[:, None] % CHUNK
    sj = np.arange(2 * CHUNK)[None, :]
    cur = (sj >= CHUNK) & (sj - CHUNK <= qi)
    prev = (sj < CHUNK) & (sj > qi)
    return jnp.asarray(np.stack([cur, cur | prev]).astype(np.float32))


def _valid_spec(grp):
    return pl.BlockSpec((None, grp * CHUNK, 2 * CHUNK), lambda n: (jnp.minimum(n, 1), 0, 0))


def _attn_group_probs(q, kk, sinks, valid, grp):
    rows = grp * CHUNK
    s = lax.dot_general(q, kk, _NT, preferred_element_type=F32)
    s = jnp.where(valid, s, NEG_INF)
    r = lax.broadcasted_iota(jnp.int32, (rows, 1), 0)
    sink = jnp.full((rows, 1), sinks[grp - 1], F32)
    for g in range(grp - 2, -1, -1):
        sink = jnp.where(r < (g + 1) * CHUNK, sinks[g], sink)
    m = jnp.maximum(jnp.max(s, axis=1, keepdims=True), sink)
    p = jnp.exp(s - m)
    ps = jnp.exp(sink - m)
    inv = 1.0 / (jnp.sum(p, axis=1, keepdims=True) + ps)
    return p * inv, ps * inv


def _kv_specs(width, nb):
    prev = pl.BlockSpec((CHUNK, width), lambda n: (jnp.maximum(n - 1, 0), 0))
    cur = pl.BlockSpec((CHUNK, width), lambda n: (n, 0))
    return prev, cur


def _attn_fwd(qr, kr, vr, sinks, *, name):
    T, QW = qr.shape
    KW = kr.shape[1]
    HQ, HK = QW // HEAD_DIM, KW // HEAD_DIM
    grp = HQ // HK
    nb = T // CHUNK

    def body(q_ref, kp_ref, kc_ref, vp_ref, vc_ref, s_ref, ok_ref, o_ref):
        valid = ok_ref[...] > 0.5
        for kh in range(HK):
            ks = slice(kh * HEAD_DIM, (kh + 1) * HEAD_DIM)
            heads = list(range(kh * grp, (kh + 1) * grp))
            q = _group_rows(q_ref, heads)
            kk = jnp.concatenate([kp_ref[:, ks], kc_ref[:, ks]], axis=0)
            vv = jnp.concatenate([vp_ref[:, ks], vc_ref[:, ks]], axis=0)
            p, _ = _attn_group_probs(q, kk, [s_ref[0, h] for h in heads], valid, grp)
            o = jnp.dot(p.astype(BF16), vv, preferred_element_type=F32).astype(BF16)
            for g, h in enumerate(heads):
                o_ref[:, h * HEAD_DIM:(h + 1) * HEAD_DIM] = o[g * CHUNK:(g + 1) * CHUNK]

    kp, kc = _kv_specs(KW, nb)
    return pl.pallas_call(
        body, name=name, out_shape=jax.ShapeDtypeStruct((T, QW), BF16), grid=(nb,),
        in_specs=[pl.BlockSpec((CHUNK, QW), lambda n: (n, 0)), kp, kc, kp, kc,
                  pl.BlockSpec(memory_space=pltpu.SMEM), _valid_spec(grp)],
        out_specs=pl.BlockSpec((CHUNK, QW), lambda n: (n, 0)),
        compiler_params=pltpu.CompilerParams(dimension_semantics=("arbitrary",), vmem_limit_bytes=_vmem(8 << 20)),
    )(qr, kr, kr, vr, vr, sinks, _attn_valid(grp))


def _attn_bwd(qr, kr, vr, sinks, do, *, name):
    T, QW = qr.shape
    KW = kr.shape[1]
    HQ, HK = QW // HEAD_DIM, KW // HEAD_DIM
    grp = HQ // HK
    nb = T // CHUNK

    def body(q_ref, kp_ref, kc_ref, vp_ref, vc_ref, s_ref, do_ref, ok_ref,
             dq_ref, dkp_ref, dkc_ref, dvp_ref, dvc_ref, ds_ref):
        n = pl.program_id(0)
        valid = ok_ref[...] > 0.5
        lane = lax.broadcasted_iota(jnp.int32, (1, LANES), 1)
        dsink = jnp.zeros((1, LANES), F32)
        for kh in range(HK):
            ks = slice(kh * HEAD_DIM, (kh + 1) * HEAD_DIM)
            heads = list(range(kh * grp, (kh + 1) * grp))
            q = _group_rows(q_ref, heads)
            doh = _group_rows(do_ref, heads)
            kk = jnp.concatenate([kp_ref[:, ks], kc_ref[:, ks]], axis=0)
            vv = jnp.concatenate([vp_ref[:, ks], vc_ref[:, ks]], axis=0)
            p, ps = _attn_group_probs(q, kk, [s_ref[0, h] for h in heads], valid, grp)
            dp = lax.dot_general(doh, vv, _NT, preferred_element_type=F32)
            delta = jnp.sum(p * dp, axis=1, keepdims=True)
            ds = (p * (dp - delta)).astype(BF16)
            dv = lax.dot_general(p.astype(BF16), doh, _TN, preferred_element_type=F32)
            dk = lax.dot_general(ds, q, _TN, preferred_element_type=F32)
            dq = jnp.dot(ds, kk, preferred_element_type=F32)
            psd = ps * delta
            for g, h in enumerate(heads):
                dq_ref[:, h * HEAD_DIM:(h + 1) * HEAD_DIM] = dq[g * CHUNK:(g + 1) * CHUNK]
                dsink = dsink + jnp.where(
                    lane == h, -jnp.sum(psd[g * CHUNK:(g + 1) * CHUNK], axis=0, keepdims=True), 0.0)
            dkp_ref[:, ks] = dk[:CHUNK]
            dkc_ref[:, ks] = dk[CHUNK:]
            dvp_ref[:, ks] = dv[:CHUNK]
            dvc_ref[:, ks] = dv[CHUNK:]

        @pl.when(n == 0)
        def _():
            ds_ref[...] = dsink

        @pl.when(n > 0)
        def _():
            ds_ref[...] += dsink

    kp, kc = _kv_specs(KW, nb)
    qspec = pl.BlockSpec((CHUNK, QW), lambda n: (n, 0))
    kout = pl.BlockSpec((CHUNK, KW), lambda n: (n, 0))
    return pl.pallas_call(
        body, name=name,
        out_shape=[jax.ShapeDtypeStruct((T, QW), F32)] + [jax.ShapeDtypeStruct((T, KW), F32)] * 4
        + [jax.ShapeDtypeStruct((1, LANES), F32)],
        grid=(nb,),
        in_specs=[qspec, kp, kc, kp, kc, pl.BlockSpec(memory_space=pltpu.SMEM), qspec, _valid_spec(grp)],
        out_specs=[qspec, kout, kout, kout, kout, pl.BlockSpec((1, LANES), lambda n: (0, 0))],
        compiler_params=pltpu.CompilerParams(dimension_semantics=("arbitrary",), vmem_limit_bytes=_vmem(12 << 20)),
    )(qr, kr, kr, vr, vr, sinks, do, _attn_valid(grp))


def _rope_bwd(dq, dkp, dkc, dvp, dvc, ctab, stab, *, name):
    T, QW = dq.shape
    KW = dkp.shape[1]
    nb = T // CHUNK
    scale = HEAD_DIM ** -0.5
    width = QW + 2 * KW

    def body(dq_ref, dkc_ref, dkn_ref, dvc_ref, dvn_ref, c_ref, s_ref, o_ref, db_ref):
        n = pl.program_id(0)
        c = c_ref[...]
        s = s_ref[...]
        has_next = (n < nb - 1).astype(F32)
        dqv = dq_ref[...]
        dk = dkc_ref[...] + has_next * dkn_ref[...]
        dv = dvc_ref[...] + has_next * dvn_ref[...]
        dq_pre = (dqv * _wide(c, QW) + _swap8(dqv * _wide(s, QW))) * scale
        dk_pre = dk * _wide(c, KW) + _swap8(dk * _wide(s, KW))
        o_ref[:, :QW] = dq_pre.astype(BF16)
        o_ref[:, QW:QW + KW] = dk_pre.astype(BF16)
        o_ref[:, QW + KW:] = dv.astype(BF16)
        part = jnp.concatenate([jnp.sum(dq_pre, axis=0, keepdims=True), jnp.sum(dk_pre, axis=0, keepdims=True),
                                jnp.sum(dv, axis=0, keepdims=True)], axis=1)

        @pl.when(n == 0)
        def _():
            db_ref[...] = part

        @pl.when(n > 0)
        def _():
            db_ref[...] += part

    cur = lambda w: pl.BlockSpec((CHUNK, w), lambda n: (n, 0))
    nxt = lambda w: pl.BlockSpec((CHUNK, w), lambda n: (jnp.minimum(n + 1, nb - 1), 0))
    return pl.pallas_call(
        body, name=name,
        out_shape=[jax.ShapeDtypeStruct((T, width), BF16), jax.ShapeDtypeStruct((1, width), F32)],
        grid=(nb,),
        in_specs=[cur(QW), cur(KW), nxt(KW), cur(KW), nxt(KW), cur(LANES), cur(LANES)],
        out_specs=[cur(width), pl.BlockSpec((1, width), lambda n: (0, 0))],
        compiler_params=pltpu.CompilerParams(dimension_semantics=("arbitrary",), vmem_limit_bytes=_vmem(8 << 20)),
    )(dq, dkc, dkp, dvc, dvp, ctab, stab)


def _cast_block(w, l, axis, chip_arr, *, name):
    _, Ks, Ns = w.shape
    tk = _pick(Ks, (512, 352, 256, 128))
    nk = Ks // tk
    full = (Ks * N_CHIPS, Ns) if axis == 0 else (Ks, Ns * N_CHIPS)

    def body(p_ref, w_ref, o_ref):
        o_ref[...] = w_ref[...].astype(BF16)

    if axis == 0:
        out_spec = pl.BlockSpec((tk, Ns), lambda i, p: (p[0] * nk + i, 0))
    else:
        out_spec = pl.BlockSpec((tk, Ns), lambda i, p: (i, p[0]))
    grid_spec = pltpu.PrefetchScalarGridSpec(
        num_scalar_prefetch=1, grid=(nk,),
        in_specs=[pl.BlockSpec((None, tk, Ns), lambda i, p: (l, i, 0))], out_specs=out_spec)
    return pl.pallas_call(
        body, name=name, out_shape=jax.ShapeDtypeStruct(full, BF16), grid_spec=grid_spec,
        compiler_params=pltpu.CompilerParams(dimension_semantics=("arbitrary",),
                                             vmem_limit_bytes=_vmem(4 * tk * Ns * 6)),
    )(chip_arr, w)


def _adamw_math(w, g, m, v):
    m = ADAM_B1 * m + (1.0 - ADAM_B1) * g
    v = ADAM_B2 * v + (1.0 - ADAM_B2) * (g * g)
    m_hat = m / (1.0 - ADAM_B1 ** ADAM_STEP)
    v_hat = v / (1.0 - ADAM_B2 ** ADAM_STEP)
    delta = -ADAM_LR * (m_hat / (jnp.sqrt(v_hat) + ADAM_EPS) + ADAM_WD * w)
    return delta, m, v


def _adamw_layer(w, m, v, g, l, outs, *, name, after=()):
    _, K, N = w.shape
    tk = _pick(K, (512, 352, 256, 128)) if N <= 1024 else _pick(K, (256, 176, 128))
    n_after = len(after)

    def body(w_ref, m_ref, v_ref, g_ref, *rest):
        go_ref, d_ref, mo_ref, vo_ref = rest[4 + n_after:]
        gv = g_ref[...]
        d, mn, vn = _adamw_math(w_ref[...], gv, m_ref[...], v_ref[...])
        go_ref[...] = gv
        d_ref[...] = d
        mo_ref[...] = mn
        vo_ref[...] = vn

    layer = pl.BlockSpec((None, tk, N), lambda i: (l, i, 0))
    any_spec = pl.BlockSpec(memory_space=pl.ANY)
    sd = jax.ShapeDtypeStruct(w.shape, F32)
    return pl.pallas_call(
        body, name=name, out_shape=[sd, sd, sd, sd], grid=(K // tk,),
        in_specs=[layer, layer, layer, pl.BlockSpec((tk, N), lambda i: (i, 0))] + [any_spec] * (4 + n_after),
        out_specs=[layer] * 4, input_output_aliases={4: 0, 5: 1, 6: 2, 7: 3},
        compiler_params=pltpu.CompilerParams(dimension_semantics=("arbitrary",),
                                             vmem_limit_bytes=_vmem(2 * 8 * tk * N * 4 + 6 * tk * N * 4)),
    )(w, m, v, g, *outs, *after)


def _adamw_small(w, g, m, v, *, name):
    def body(w_ref, g_ref, m_ref, v_ref, d_ref, mo_ref, vo_ref):
        d, mn, vn = _adamw_math(w_ref[...], g_ref[...], m_ref[...], v_ref[...])
        d_ref[...] = d
        mo_ref[...] = mn
        vo_ref[...] = vn

    sd = jax.ShapeDtypeStruct(w.shape, F32)
    return pl.pallas_call(body, name=name, out_shape=[sd, sd, sd])(w, g, m, v)


def _my_place():
    return lax.axis_index("x"), lax.axis_index("y"), lax.axis_index("c")


def _peer_chips(x, y):
    return [(1 - x, y), (x, 1 - y), (1 - x, 1 - y)]


_HBM = pl.BlockSpec(memory_space=pltpu.HBM)
_SEM = pl.BlockSpec(memory_space=pltpu.SEMAPHORE)
_EFFECT = pltpu.SideEffectType.DATAFLOW_SIDE_EFFECTING


def _split_start(name, bufs, n_copies, make_copies, after):
    nb = len(bufs)

    def body(*refs):
        send_sems, recv_sems = refs[nb + 1], refs[nb + 2]
        token = refs[2 * nb + 3]
        sends, _ = make_copies(refs[:nb], send_sems, recv_sems)
        for cp in sends:
            cp.start()
        token[...] = jnp.zeros_like(token)

    res = pl.pallas_call(
        body, name=name,
        out_shape=(pltpu.SemaphoreType.DMA((n_copies,)), pltpu.SemaphoreType.DMA((n_copies,)),
                   *[pltpu.HBM(b.shape, b.dtype) for b in bufs], jax.ShapeDtypeStruct((8, LANES), F32)),
        in_specs=[_HBM] * nb + [pl.BlockSpec(memory_space=pl.ANY)],
        out_specs=(_SEM, _SEM, *[_HBM] * nb, pl.BlockSpec(memory_space=pltpu.VMEM)),
        input_output_aliases={k: 2 + k for k in range(nb)},
        compiler_params=pltpu.CompilerParams(has_side_effects=_EFFECT),
    )(*[pltpu.with_memory_space_constraint(b, pltpu.HBM) for b in bufs],
      after[0] if isinstance(after, (list, tuple)) else after)
    return res[0], res[1], list(res[2:2 + nb]), res[2 + nb]


def _split_wait(name, bufs, sems, make_copies, after):
    nb = len(bufs)
    after = list(after) if isinstance(after, (list, tuple)) else [after]

    def body(*refs):
        send_sems, recv_sems = refs[nb], refs[nb + 1]
        sends, recvs = make_copies(refs[:nb], send_sems, recv_sems)
        for cp in sends:
            cp.wait_send()
        for cp in recvs:
            cp.wait_recv()

    res = pl.pallas_call(
        body, name=name,
        out_shape=tuple(pltpu.HBM(b.shape, b.dtype) for b in bufs),
        in_specs=[_HBM] * nb + [_SEM, _SEM] + [pl.BlockSpec(memory_space=pl.ANY)] * len(after),
        out_specs=tuple([_HBM] * nb),
        input_output_aliases={k: k for k in range(nb)},
        compiler_params=pltpu.CompilerParams(has_side_effects=_EFFECT),
    )(*bufs, sems[0], sems[1], *after)
    return list(res)


def _remote(src, dst, send_sems, recv_sems, k, target):
    return pltpu.make_async_remote_copy(src_ref=src, dst_ref=dst, send_sem=send_sems.at[k],
                                        recv_sem=recv_sems.at[k], device_id=target, device_id_type=MESH)


def _ag_region(ref, axis, chip, half):
    K, N = ref.shape
    if axis == 0:
        hs = K // N_CHIPS // 2
        assert hs % 16 == 0
        return ref.at[pl.ds(pl.multiple_of((2 * chip + half) * hs, 16), hs), :]
    ns, hk = N // N_CHIPS, K // 2
    assert ns % LANES == 0 and hk % 16 == 0
    return ref.at[pl.ds(pl.multiple_of(half * hk, 16), hk), pl.ds(pl.multiple_of(chip * ns, LANES), ns)]


def _ag_copies(stage, axes):
    n = len(axes)

    def make(bufs, send_sems, recv_sems):
        x, y, c = _my_place()
        me = 2 * x + y
        sends, recvs = [], []
        for j, (px, py) in enumerate(_peer_chips(x, y)):
            other = 2 * px + py
            for w in range(n):
                k = j * n + w
                if stage == 1:
                    src, target = _ag_region(bufs[w], axes[w], me, c), (px, py, c)
                    land = _ag_region(bufs[w], axes[w], other, c)
                else:
                    src, target = _ag_region(bufs[w], axes[w], other, c), (x, y, 1 - c)
                    land = _ag_region(bufs[w], axes[w], other, 1 - c)
                sends.append(_remote(src, src, send_sems, recv_sems, k, target))
                recvs.append(_remote(land, land, send_sems, recv_sems, k, target))
        return sends, recvs

    return make


def _half_shape(shape, axis):
    K, N = shape
    return (K, N // 2) if axis == 0 else (K // 2, N)


def _core_half(ref, axis, half):
    K, N = ref.shape
    if axis == 0:
        return ref.at[:, pl.ds(pl.multiple_of(half * (N // 2), LANES), N // 2)]
    return ref.at[pl.ds(pl.multiple_of(half * (K // 2), 16), K // 2), :]


def _chip_block(ref, axis, chip):
    K, N = ref.shape
    if axis == 0:
        return ref.at[pl.ds(pl.multiple_of(chip * (K // N_CHIPS), 16), K // N_CHIPS), :]
    return ref.at[:, pl.ds(pl.multiple_of(chip * (N // N_CHIPS), LANES), N // N_CHIPS)]


def _rs_sibling_copies(axes):
    n = len(axes)

    def make(bufs, send_sems, recv_sems):
        x, y, c = _my_place()
        sends = [_remote(_core_half(bufs[w], axes[w], 1 - c), bufs[n + w], send_sems, recv_sems, w, (x, y, 1 - c))
                 for w in range(n)]
        recvs = [_remote(bufs[n + w], bufs[n + w], send_sems, recv_sems, w, (x, y, 1 - c)) for w in range(n)]
        return sends, recvs

    return make


def _rs_chip_copies(axes):
    n = len(axes)

    def make(bufs, send_sems, recv_sems):
        x, y, c = _my_place()
        sends, recvs = [], []
        for j, (px, py) in enumerate(_peer_chips(x, y)):
            for w in range(n):
                k = j * n + w
                sends.append(_remote(_chip_block(bufs[w], axes[w], 2 * px + py), bufs[n + w].at[j],
                                     send_sems, recv_sems, k, (px, py, c)))
                recvs.append(_remote(bufs[n + w].at[j], bufs[n + w].at[j], send_sems, recv_sems, k, (px, py, c)))
        return sends, recvs

    return make


def _rs_fill_copies(axes):
    n = len(axes)

    def make(bufs, send_sems, recv_sems):
        x, y, c = _my_place()
        sends = [_remote(_core_half(bufs[w], axes[w], c), _core_half(bufs[w], axes[w], c),
                         send_sems, recv_sems, w, (x, y, 1 - c)) for w in range(n)]
        recvs = [_remote(_core_half(bufs[w], axes[w], 1 - c), _core_half(bufs[w], axes[w], 1 - c),
                         send_sems, recv_sems, w, (x, y, 1 - c)) for w in range(n)]
        return sends, recvs

    return make


def _chip_sum(g, r, axis, place, *, name):
    hk, hn = r.shape
    bk, bn = (hk // N_CHIPS, hn) if axis == 0 else (hk, hn // N_CHIPS)
    tk = _pick(bk, (512, 352, 256, 128))
    nk = bk // tk

    def body(p_ref, g_ref, r_ref, b_ref, own_ref):
        s = g_ref[...].astype(F32) + r_ref[...].astype(F32)
        b_ref[...] = s.astype(BF16)

        @pl.when(pl.program_id(1) == p_ref[0])
        def _():
            own_ref[...] = s

    if axis == 0:
        g_spec = pl.BlockSpec((tk, bn), lambda i, j, p: (j * nk + i, p[1]))
        r_spec = pl.BlockSpec((tk, bn), lambda i, j, p: (j * nk + i, 0))
    else:
        g_spec = pl.BlockSpec((tk, bn), lambda i, j, p: (p[1] * nk + i, j))
        r_spec = pl.BlockSpec((tk, bn), lambda i, j, p: (i, j))
    grid_spec = pltpu.PrefetchScalarGridSpec(
        num_scalar_prefetch=1, grid=(nk, N_CHIPS), in_specs=[g_spec, r_spec],
        out_specs=[r_spec, pl.BlockSpec((tk, bn), lambda i, j, p: (i, 0))])
    return pl.pallas_call(
        body, name=name,
        out_shape=[jax.ShapeDtypeStruct(r.shape, BF16), jax.ShapeDtypeStruct((bk, bn), F32)],
        grid_spec=grid_spec,
        compiler_params=pltpu.CompilerParams(dimension_semantics=("arbitrary", "arbitrary"),
                                             vmem_limit_bytes=_vmem(2 * tk * bn * 10 + 3 * tk * bn * 4)),
    )(place, g, r)


def _final_sum(own, recv, axis, place, *, name):
    _, bk, bn = recv.shape
    tk = _pick(bk, (256, 176, 128))
    nk = bk // tk

    def body(p_ref, o_ref, r_ref, out_ref):
        out_ref[...] = ((o_ref[...] + r_ref[0].astype(F32)) + r_ref[1].astype(F32)) + r_ref[2].astype(F32)

    own_spec = pl.BlockSpec((tk, bn), lambda i, p: (i, 0))
    if axis == 0:
        out_shape, out_spec = (bk, 2 * bn), pl.BlockSpec((tk, bn), lambda i, p: (i, p[1]))
    else:
        out_shape, out_spec = (2 * bk, bn), pl.BlockSpec((tk, bn), lambda i, p: (p[1] * nk + i, 0))
    grid_spec = pltpu.PrefetchScalarGridSpec(
        num_scalar_prefetch=1, grid=(nk,),
        in_specs=[own_spec, pl.BlockSpec((3, tk, bn), lambda i, p: (0, i, 0))], out_specs=out_spec)
    return pl.pallas_call(
        body, name=name, out_shape=jax.ShapeDtypeStruct(out_shape, F32), grid_spec=grid_spec,
        compiler_params=pltpu.CompilerParams(dimension_semantics=("arbitrary",),
                                             vmem_limit_bytes=_vmem(2 * tk * bn * 14 + 4 * tk * bn * 4)),
    )(place, own, recv)


def _allreduce_small(p, after=()):
    n_after = len(after)

    def body(*refs):
        p_ref = refs[0]
        o_ref, r0, r1, r2, send_sems, recv_sems = refs[1 + n_after:]
        x, y, c = _my_place()
        o_ref[...] = p_ref[...]
        for s, (peer, rbuf) in enumerate([((x, y, 1 - c), r0), ((1 - x, y, c), r1), ((x, 1 - y, c), r2)]):
            cp = pltpu.make_async_remote_copy(src_ref=o_ref, dst_ref=rbuf, send_sem=send_sems.at[s],
                                              recv_sem=recv_sems.at[s], device_id=peer, device_id_type=MESH)
            cp.start()
            cp.wait()
            o_ref[...] = o_ref[...] + rbuf[...]

    vm = pl.BlockSpec(memory_space=pltpu.VMEM)
    return pl.pallas_call(
        body, name="allreduce_small", out_shape=jax.ShapeDtypeStruct(p.shape, F32),
        in_specs=[vm] + [pl.BlockSpec(memory_space=pl.ANY)] * n_after, out_specs=vm,
        scratch_shapes=[pltpu.VMEM(p.shape, F32)] * 3 + [pltpu.SemaphoreType.DMA((3,))] * 2,
        compiler_params=pltpu.CompilerParams(vmem_limit_bytes=_vmem(6 * _nbytes(p.shape, F32))),
    )(p, *after)


_BUTTERFLY = (lambda x, y, c: (x, y, 1 - c), lambda x, y, c: (1 - x, y, c), lambda x, y, c: (x, 1 - y, c))


def _swap_copies(stage):
    def make(bufs, send_sems, recv_sems):
        target = _BUTTERFLY[stage](*_my_place())
        return ([_remote(bufs[0], bufs[1], send_sems, recv_sems, 0, target)],
                [_remote(bufs[1], bufs[1], send_sems, recv_sems, 0, target)])

    return make


def _add(a, b, *, name):
    def body(a_ref, b_ref, o_ref):
        o_ref[...] = a_ref[...] + b_ref[...]

    return pl.pallas_call(body, name=name, out_shape=jax.ShapeDtypeStruct(a.shape, a.dtype))(a, b)


def _pack_rows(parts):
    rows, metas = [], []
    for a in parts:
        flat = a.reshape(-1)
        nrow = -(-flat.shape[0] // LANES)
        nrow = -(-nrow // 8) * 8
        flat = jnp.pad(flat, (0, nrow * LANES - flat.shape[0]))
        rows.append(flat.reshape(nrow, LANES))
        metas.append((a.shape, nrow))
    return jnp.concatenate(rows, axis=0), metas


def _unpack_rows(packed, metas):
    out, r0 = [], 0
    for shape, nrow in metas:
        size = int(np.prod(shape))
        out.append(packed[r0:r0 + nrow].reshape(-1)[:size].reshape(shape))
        r0 += nrow
    return out


def kernel(x, positions, pre_mix_g, post_mix_g, pre_ffn_g, post_ffn_g, a_w_in, a_b_in, a_ln_g, a_ln_b, a_w_s, a_b_s, a_w_out, b_w_qkv, b_b_qkv, b_sinks, b_w_o, ffn_w_gu, ffn_w_down, loss_target, m_pre_mix_g, m_post_mix_g, m_pre_ffn_g, m_post_ffn_g, m_a_w_in, m_a_b_in, m_a_ln_g, m_a_ln_b, m_a_w_s, m_a_b_s, m_a_w_out, m_b_w_qkv, m_b_b_qkv, m_b_sinks, m_b_w_o, m_ffn_w_gu, m_ffn_w_down, v_pre_mix_g, v_post_mix_g, v_pre_ffn_g, v_post_ffn_g, v_a_w_in, v_a_b_in, v_a_ln_g, v_a_ln_b, v_a_w_s, v_a_b_s, v_a_w_out, v_b_w_qkv, v_b_b_qkv, v_b_sinks, v_b_w_o, v_ffn_w_gu, v_ffn_w_down):
    depth, D = pre_mix_g.shape
    xi, yi, ci = _my_place()
    chip = 2 * xi + yi
    place = jnp.stack([chip, ci]).astype(jnp.int32)

    stacked = {"a_w_in": (a_w_in, m_a_w_in, v_a_w_in), "a_w_out": (a_w_out, m_a_w_out, v_a_w_out),
               "b_w_qkv": (b_w_qkv, m_b_w_qkv, v_b_w_qkv), "b_w_o": (b_w_o, m_b_w_o, v_b_w_o),
               "ffn_w_gu": (ffn_w_gu, m_ffn_w_gu, v_ffn_w_gu), "ffn_w_down": (ffn_w_down, m_ffn_w_down, v_ffn_w_down)}
    cut = {"a_w_in": 1, "a_w_out": 0, "b_w_qkv": 1, "b_w_o": 0, "ffn_w_gu": 1, "ffn_w_down": 0}

    def layer_keys(i):
        mix = [("a_w_in", i // 2), ("a_w_out", i // 2)] if i % 2 == 0 else [("b_w_qkv", i // 2), ("b_w_o", i // 2)]
        return mix + [("ffn_w_gu", i), ("ffn_w_down", i)]

    def dep(a, toks):
        for t in toks:
            a = a + t[:1, :1]
        return a

    W = {}
    for i in range(depth):
        for nm, l in layer_keys(i):
            W[(nm, l)] = _cast_block(stacked[nm][0], l, cut[nm], place, name=f"cast_{nm}_{l}")

    def gather(tag, keys, after):
        axes = [cut[nm] for nm, _ in keys]
        for stage in (1, 2):
            ss, rs, bufs, tok = _split_start(f"ag{stage}_start_{tag}", [W[k] for k in keys], 3 * len(keys),
                                             _ag_copies(stage, axes), after)
            after = yield tok
            bufs = _split_wait(f"ag{stage}_wait_{tag}", bufs, (ss, rs), _ag_copies(stage, axes), after)
            W.update(zip(keys, bufs))
        yield None

    nq = b_b_qkv.shape[1]
    bq_full = jnp.zeros((b_b_qkv.shape[0], N_CHIPS * nq), F32)
    bq_full = lax.dynamic_update_slice(bq_full, jnp.where(ci == 0, b_b_qkv, 0.0), (0, chip * nq))
    bq_packed, bq_meta = _pack_rows([bq_full])
    bq_gathered = _allreduce_small(bq_packed)
    b_qkv_full = _unpack_rows(bq_gathered, bq_meta)[0]

    first = gather("0m", layer_keys(0)[:2], bq_gathered)
    tok = next(first)
    tok = first.send([tok] + [W[k] for i in range(depth) for k in layer_keys(i)[2 if i == 0 else 0:]])
    first.send(tok)

    h = x[0]
    target = loss_target[0]
    ctab, stab = _rope_tables(positions[0])
    q_width = W[("b_w_o", 0)].shape[0]
    kv_width = N_KV_HEADS * HEAD_DIM
    row = lambda a, i: a[i:i + 1]
    gains = {"pre_mix": pre_mix_g[:, None], "post_mix": post_mix_g[:, None], "pre_ffn": pre_ffn_g[:, None],
             "post_ffn": post_ffn_g[:, None]}
    gain = lambda which, i: (gains[which], i)

    saved = []
    hn = None
    for i in range(depth):
        j = i // 2
        s = {"h": h}
        ffn_w = None
        if i == 0:
            ffn_w = gather("0f", layer_keys(0)[2:], W[("a_w_out", 0)])
            toks = [next(ffn_w)]
            nxt = gather("1", layer_keys(1), toks[0])
            toks.append(next(nxt))
            hn = _rms_fwd(h, gain("pre_mix", i), out_dtype=BF16, after=toks, name=f"rms_pre_mix_{i}")
        elif i + 1 < depth:
            nxt = gather(str(i + 1), layer_keys(i + 1), h)
            toks = [next(nxt)]
        else:
            toks = []
        s["hn"] = hn
        if i % 2 == 0:
            pre = _matmul(hn, W[("a_w_in", j)], mode="nn", bias=row(a_b_in, j), out_dtype=F32, after=toks,
                          name=f"gmlp_in_{i}")
            gated = _sgu_fwd(pre, row(a_ln_g, j), row(a_ln_b, j), a_w_s[j], a_b_s[j].T, name=f"sgu_fwd_{i}")
            mix = _matmul(gated, W[("a_w_out", j)], mode="nn", out_dtype=F32, name=f"gmlp_out_{i}")
            s.update(pre=pre, gated=gated)
        else:
            qkv = _matmul(hn, W[("b_w_qkv", j)], mode="nn", bias=row(b_qkv_full, j), out_dtype=F32, after=toks,
                          name=f"attn_qkv_{i}")
            qr, kr, vr = _rope_fwd(qkv, ctab, stab, q_width=q_width, kv_width=kv_width, name=f"rope_fwd_{i}")
            o = _attn_fwd(qr, kr, vr, row(b_sinks, j), name=f"attn_fwd_{i}")
            mix = _matmul(o, W[("b_w_o", j)], mode="nn", out_dtype=F32, name=f"attn_o_{i}")
            s.update(qr=qr, kr=kr, vr=vr, o=o)
        s["mix"] = mix
        toks = [ffn_w.send(mix)] if ffn_w else []
        h1, fn = _rms_res_norm(h, mix, gain("post_mix", i), gain("pre_ffn", i), after=toks, name=f"rms_post_mix_{i}")
        if ffn_w:
            ffn_w.send(h1)
        s["h1"] = h1
        g_pre, u_pre, act = _ffn_up(fn, W[("ffn_w_gu", i)][None], 0, name=f"ffn_up_{i}")
        f = _matmul(act, W[("ffn_w_down", i)], mode="nn", out_dtype=F32, name=f"ffn_down_{i}")
        if i + 1 < depth:
            toks = [nxt.send(f)]
            h, hn = _rms_res_norm(h1, f, gain("post_ffn", i), gain("pre_mix", i + 1), after=toks,
                                  name=f"rms_post_ffn_{i}")
            nxt.send(h)
        else:
            h = _rms_res(h1, f, gain("post_ffn", i), name=f"rms_post_ffn_{i}")
        s.update(fn=fn, g_pre=g_pre, u_pre=u_pre, act=act, f=f)
        saved.append(s)

    dh, df, loss_part, g_last = _loss_and_grad(h, target, saved[-1]["f"], gain("post_ffn", depth - 1), name="loss")

    big_out = {nm: tuple(lax.empty(w.shape, F32) for _ in range(4)) for nm, (w, _, _) in stacked.items()}

    def reduce_group(i, keys, grads):
        axes = [cut[nm] for nm, _ in keys]
        n = len(keys)
        lands = [lax.empty(_half_shape(g.shape, ax), BF16) for g, ax in zip(grads, axes)]
        ss, rs, bufs, tok = _split_start(f"rs_sibling_start_{i}", list(grads) + lands, n, _rs_sibling_copies(axes),
                                         place)
        after = yield tok
        bufs = _split_wait(f"rs_sibling_wait_{i}", bufs, (ss, rs), _rs_sibling_copies(axes), after)
        sums = [_chip_sum(bufs[w], bufs[n + w], axes[w], place, name=f"chip_sum_{keys[w][0]}_{keys[w][1]}")
                for w in range(n)]
        lands = [lax.empty((3,) + own.shape, BF16) for _, own in sums]
        ss, rs, bufs, tok = _split_start(f"rs_chip_start_{i}", [sb for sb, _ in sums] + lands, 3 * n,
                                         _rs_chip_copies(axes), place)
        after = yield tok
        bufs = _split_wait(f"rs_chip_wait_{i}", bufs, (ss, rs), _rs_chip_copies(axes), after)
        blocks = [_final_sum(sums[w][1], bufs[n + w], axes[w], place, name=f"final_sum_{keys[w][0]}_{keys[w][1]}")
                  for w in range(n)]
        ss, rs, bufs, tok = _split_start(f"rs_fill_start_{i}", blocks, n, _rs_fill_copies(axes), place)
        after = yield tok
        blocks = _split_wait(f"rs_fill_wait_{i}", bufs, (ss, rs), _rs_fill_copies(axes), after)
        updates.extend(zip(keys, blocks))
        yield None

    updates = []

    def adamw(items, after):
        done = []
        for (nm, l), g in items:
            w, m, v = stacked[nm]
            big_out[nm] = tuple(_adamw_layer(w, m, v, g, l, big_out[nm], after=after, name=f"adamw_{nm}_{l}"))
            done.append(big_out[nm][1])
        return done

    reducing = []

    def advance(after, newest_only=False):
        toks = []
        for gen in (reducing[-1:] if newest_only else list(reducing)):
            tok = gen.send(after)
            if tok is None:
                reducing.remove(gen)
            else:
                toks.append(tok)
        return toks

    small = {}
    g_pre_mix, g_post_mix, g_pre_ffn, g_post_ffn = [None] * depth, [None] * depth, [None] * depth, [None] * depth
    g_post_ffn[depth - 1] = g_last
    toks = []
    early = []
    for i in reversed(range(depth)):
        j = i // 2
        s = saved[i]
        g_down = _matmul(s["act"], df, mode="tn", out_dtype=BF16, after=toks, name=f"ffn_down_dw_{i}")
        dg_, du_ = _ffn_down_dx(df, W[("ffn_w_down", i)][None], 0, s["g_pre"], s["u_pre"], g_down,
                                name=f"ffn_down_dx_{i}")
        g_gu = _matmul_pair(s["fn"], dg_, du_, mode="tn", out_dtype=BF16, name=f"ffn_gu_dw_{i}")
        dfn = _matmul_pair(dg_, W[("ffn_w_gu", i)], du_, mode="nt", out_dtype=F32, after=[g_gu],
                           name=f"ffn_gu_dx_{i}")
        toks = advance(dfn)
        if i == 0:
            gen = reduce_group("0f", layer_keys(0)[2:], [g_gu, g_down])
            toks.append(next(gen))
            reducing.append(gen)
        dh1, dmix, g_pre_ffn[i], g_post_mix[i] = _rms_bwd_chain(
            s["h1"], gain("pre_ffn", i), dfn, dh, s["mix"], gain("post_mix", i), after=toks,
            name=f"rms_ffn_mix_bwd_{i}")
        if i % 2 == 0:
            g_out = _matmul(s["gated"], dmix, mode="tn", out_dtype=BF16, name=f"gmlp_out_dw_{i}")
            dgated = _matmul(dmix, W[("a_w_out", j)], mode="nt", out_dtype=BF16, after=[g_out],
                             name=f"gmlp_out_dx_{i}")
            toks = advance(dgated, newest_only=True) if i == 0 else []
            dpre, dws, dbsT, dlng, dlnb, dbin = _sgu_bwd(s["pre"], dgated, dep(row(a_ln_g, j), toks), row(a_ln_b, j),
                                                         a_w_s[j], a_b_s[j].T, name=f"sgu_bwd_{i}")
            small[("a_w_s", j)] = dws
            small[("a_b_s", j)] = dbsT.T
            small[("a_ln_g", j)] = dlng
            small[("a_ln_b", j)] = dlnb
            small[("a_b_in", j)] = dbin
            g_in = _matmul(s["hn"], dpre, mode="tn", out_dtype=BF16, name=f"gmlp_in_dw_{i}")
            if i == 0:
                last = reduce_group("0m", layer_keys(0)[:2], [g_in, g_out])
                early = [next(last)]
            dhn = _matmul(dpre, W[("a_w_in", j)], mode="nt", out_dtype=F32, after=[g_in] + early,
                          name=f"gmlp_in_dx_{i}")
        else:
            g_out = _matmul(s["o"], dmix, mode="tn", out_dtype=BF16, name=f"attn_o_dw_{i}")
            do = _matmul(dmix, W[("b_w_o", j)], mode="nt", out_dtype=BF16, after=[g_out], name=f"attn_o_dx_{i}")
            dq, dkp, dkc, dvp, dvc, dsk = _attn_bwd(s["qr"], s["kr"], s["vr"], row(b_sinks, j), do,
                                                    name=f"attn_bwd_{i}")
            dqkv, dbq = _rope_bwd(dq, dkp, dkc, dvp, dvc, ctab, stab, name=f"rope_bwd_{i}")
            small[("b_sinks", j)] = dsk[:, :b_sinks.shape[1]]
            small[("b_b_qkv", j)] = dbq
            g_in = _matmul(s["hn"], dqkv, mode="tn", out_dtype=BF16, name=f"attn_qkv_dw_{i}")
            if i == 0:
                last = reduce_group("0m", layer_keys(0)[:2], [g_in, g_out])
                early = [next(last)]
            dhn = _matmul(dqkv, W[("b_w_qkv", j)], mode="nt", out_dtype=F32, after=[g_in] + early,
                          name=f"attn_qkv_dx_{i}")
        toks = advance(dhn)
        if i > 0:
            dh, df, g_pre_mix[i], g_post_ffn[i - 1] = _rms_bwd_chain(
                s["h"], gain("pre_mix", i), dhn, dh1, saved[i - 1]["f"], gain("post_ffn", i - 1), after=toks,
                name=f"rms_mix_ffn_bwd_{i}")
            gen = reduce_group(str(i), layer_keys(i), [g_in, g_out, g_gu, g_down])
            toks = [next(gen)] + advance(dh)
            reducing.append(gen)
        else:
            toks.append(last.send(dhn))
            dh, g_pre_mix[i] = _rms_bwd(s["h"], gain("pre_mix", i), dhn, dh1, out_dtype=F32, after=toks,
                                        name=f"rms_pre_mix_bwd_{i}")
            advance(dh)
    grad_x = dh[None]
    assert not reducing

    n_a, n_b = a_b_in.shape[0], b_sinks.shape[0]
    stack = lambda key, n: jnp.concatenate([small[(key, j)] for j in range(n)], axis=0)
    small_parts = [
        jnp.concatenate(g_pre_mix, axis=0), jnp.concatenate(g_post_mix, axis=0),
        jnp.concatenate(g_pre_ffn, axis=0), jnp.concatenate(g_post_ffn, axis=0),
        stack("a_b_in", n_a), stack("a_ln_g", n_a), stack("a_ln_b", n_a),
        jnp.stack([small[("a_w_s", j)] for j in range(n_a)]), jnp.stack([small[("a_b_s", j)] for j in range(n_a)]),
        stack("b_b_qkv", n_b), stack("b_sinks", n_b), loss_part,
    ]
    packed, metas = _pack_rows(small_parts)
    reduced = packed
    for stage in range(len(_BUTTERFLY)):
        ss, rs, bufs, tok = _split_start(f"butterfly_start_{stage}", [reduced, lax.empty(packed.shape, F32)], 1,
                                         _swap_copies(stage), place)
        done = adamw(updates[stage::len(_BUTTERFLY)], [tok])
        bufs = _split_wait(f"butterfly_wait_{stage}", bufs, (ss, rs), _swap_copies(stage), done or [tok])
        reduced = _add(bufs[0], bufs[1], name=f"butterfly_add_{stage}")
    updates = []
    while last.send(reduced) is not None:
        pass
    adamw(updates, [])
    red = _unpack_rows(reduced, metas)
    (gr_pre_mix, gr_post_mix, gr_pre_ffn, gr_post_ffn, gr_b_in, gr_ln_g, gr_ln_b, gr_w_s, gr_b_s,
     gr_b_qkv_full, gr_sinks, loss_sum) = red
    loss = loss_sum[0, 0]
    gr_b_qkv = lax.dynamic_slice(gr_b_qkv_full, (0, chip * nq), (gr_b_qkv_full.shape[0], nq))

    grads = {"pre_mix_g": gr_pre_mix, "post_mix_g": gr_post_mix, "pre_ffn_g": gr_pre_ffn, "post_ffn_g": gr_post_ffn,
             "a_b_in": gr_b_in, "a_ln_g": gr_ln_g, "a_ln_b": gr_ln_b, "a_w_s": gr_w_s, "a_b_s": gr_b_s,
             "b_b_qkv": gr_b_qkv, "b_sinks": gr_sinks}
    weights = {"pre_mix_g": (pre_mix_g, m_pre_mix_g, v_pre_mix_g), "post_mix_g": (post_mix_g, m_post_mix_g, v_post_mix_g),
               "pre_ffn_g": (pre_ffn_g, m_pre_ffn_g, v_pre_ffn_g), "post_ffn_g": (post_ffn_g, m_post_ffn_g, v_post_ffn_g),
               "a_b_in": (a_b_in, m_a_b_in, v_a_b_in), "a_ln_g": (a_ln_g, m_a_ln_g, v_a_ln_g),
               "a_ln_b": (a_ln_b, m_a_ln_b, v_a_ln_b), "a_w_s": (a_w_s, m_a_w_s, v_a_w_s), "a_b_s": (a_b_s, m_a_b_s, v_a_b_s),
               "b_b_qkv": (b_b_qkv, m_b_b_qkv, v_b_b_qkv), "b_sinks": (b_sinks, m_b_sinks, v_b_sinks)}
    order = ["pre_mix_g", "post_mix_g", "pre_ffn_g", "post_ffn_g", "a_w_in", "a_b_in", "a_ln_g", "a_ln_b", "a_w_s",
             "a_b_s", "a_w_out", "b_w_qkv", "b_b_qkv", "b_sinks", "b_w_o", "ffn_w_gu", "ffn_w_down"]
    deltas, new_m, new_v = {}, {}, {}
    for nm in order:
        if nm in big_out:
            grads[nm], deltas[nm], new_m[nm], new_v[nm] = big_out[nm]
        else:
            w, m, v = weights[nm]
            deltas[nm], new_m[nm], new_v[nm] = _adamw_small(w, grads[nm], m, v, name="adamw_" + nm)
    return (loss, grad_x, *[grads[nm] for nm in order], *[deltas[nm] for nm in order],
            *[new_m[nm] for nm in order], *[new_v[nm] for nm in order])
```

```python
import functools
import math

import jax
import jax.numpy as jnp
import numpy as np
from jax import lax
from jax.experimental import pallas as pl
from jax.experimental.pallas import tpu as pltpu

F32 = jnp.float32
BF16 = jnp.bfloat16
MESH = pl.DeviceIdType.MESH

HEAD_DIM = 64
N_KV_HEADS = 4
ROPE_DIM = 16
ROPE_THETA = 500000.0
CHUNK = 128
GMLP_GROUPS = 8
RMS_EPS = 1e-6
LN_EPS = 1e-5
NEG_INF = -1e30
ADAM_LR = 0.001
ADAM_B1 = 0.9
ADAM_B2 = 0.999
ADAM_EPS = 1e-08
ADAM_WD = 0.01
ADAM_STEP = 10

N_CHIPS = 4
LANES = 128
VMEM_CAP = 58 * 1024 * 1024


def _vmem(est_bytes):
    assert est_bytes < VMEM_CAP
    return VMEM_CAP


def _pick(n, cands):
    for c in cands:
        if c <= n and n % c == 0:
            return c
    return n


def _nbytes(shape, dtype):
    return int(np.prod(shape)) * jnp.dtype(dtype).itemsize


MATMUL_VMEM_BUDGET = 48 * 1024 * 1024
MXU_COLS = 256


def _halvings(n, unit):
    out, t = [], n
    while t % unit == 0 and t >= unit:
        out.append(t)
        if t % 2:
            break
        t //= 2
    return out


def _matmul_tiles(P, Q, R, a_bytes, b_bytes, o_bytes, full_addend, tp, tq, tr, repeat=1):
    step_us, bytes_per_us, flops_per_us = 0.85, 3.2e6, 9.0e8
    best = None
    for p in ([tp] if tp else _halvings(P, LANES)):
        for q in ([tq] if tq else _halvings(Q, LANES)):
            for r in ([tr] if tr else _halvings(R, LANES)):
                nk = R // r
                vm = 2 * (p * r * a_bytes + r * q * b_bytes + p * q * o_bytes + (p * q * 4 if full_addend else 0))
                vm += p * q * 4 * (2 if nk > 1 else 1)
                if vm > MATMUL_VMEM_BUDGET:
                    continue
                exposed = (p * r * a_bytes + r * q * b_bytes + p * q * o_bytes) / bytes_per_us
                mxu_us = repeat * 2.0 * P * R * (Q // q) * (-(-q // MXU_COLS) * MXU_COLS) / flops_per_us
                key = (repeat * (P // p) * (Q // q) * nk * step_us + exposed + mxu_us, nk, abs(p - q))
                if best is None or key < best[0]:
                    best = (key, (p, q, r))
    assert best is not None, (P, Q, R)
    return best[1]


def _matmul(a, b, *, mode, out_dtype, name, bias=None, after=()):
    if mode == "nn":
        (P, R), (R2, Q) = a.shape, b.shape
    elif mode == "nt":
        (P, R), (Q, R2) = a.shape, b.shape
    else:
        (R, P), (R2, Q) = a.shape, b.shape
    assert R == R2, (mode, a.shape, b.shape)
    tp, tq, tr = _matmul_tiles(P, Q, R, a.dtype.itemsize, b.dtype.itemsize, jnp.dtype(out_dtype).itemsize, False,
                               None, None, None)
    nk = R // tr
    dims = {"nn": (((1,), (0,)), ((), ())), "nt": (((1,), (1,)), ((), ())), "tn": (((0,), (0,)), ((), ()))}[mode]
    if mode == "nn":
        a_spec = pl.BlockSpec((tp, tr), lambda i, j, k: (i, k))
        b_spec = pl.BlockSpec((tr, tq), lambda i, j, k: (k, j))
    elif mode == "nt":
        a_spec = pl.BlockSpec((tp, tr), lambda i, j, k: (i, k))
        b_spec = pl.BlockSpec((tq, tr), lambda i, j, k: (j, k))
    else:
        a_spec = pl.BlockSpec((tr, tp), lambda i, j, k: (k, i))
        b_spec = pl.BlockSpec((tr, tq), lambda i, j, k: (k, j))
    in_specs = [a_spec, b_spec]
    args = [a, b]
    has_bias = bias is not None
    if has_bias:
        in_specs.append(pl.BlockSpec((1, tq), lambda i, j, k: (0, j)))
        args.append(bias)
    out_shape = jax.ShapeDtypeStruct((P, Q), out_dtype)
    out_spec = pl.BlockSpec((tp, tq), lambda i, j, k: (i, j))
    n_in = len(args) + len(after)
    in_specs += [pl.BlockSpec(memory_space=pl.ANY)] * len(after)
    args += list(after)

    def body(*refs):
        a_ref, b_ref = refs[0], refs[1]
        bias_ref = refs[2] if has_bias else None
        o_ref = refs[n_in]
        acc_ref = refs[n_in + 1] if nk > 1 else None
        part = lax.dot_general(a_ref[...], b_ref[...], dims, preferred_element_type=F32)

        def finish(acc):
            if has_bias:
                acc = acc + bias_ref[...]
            o_ref[...] = acc.astype(out_dtype)

        if nk == 1:
            finish(part)
        else:
            k = pl.program_id(2)

            @pl.when(k == 0)
            def _():
                acc_ref[...] = part

            @pl.when(k > 0)
            def _():
                acc_ref[...] += part

            @pl.when(k == nk - 1)
            def _():
                finish(acc_ref[...])

    est = 2 * (_nbytes((tp, tr), a.dtype) + _nbytes((tr, tq), b.dtype) + _nbytes((tp, tq), out_dtype)) + 3 * tp * tq * 4
    return pl.pallas_call(
        body, name=name, out_shape=out_shape,
        grid=(P // tp, Q // tq, nk),
        in_specs=in_specs, out_specs=out_spec,
        scratch_shapes=[pltpu.VMEM((tp, tq), F32)] if nk > 1 else [],
        compiler_params=pltpu.CompilerParams(
            dimension_semantics=("parallel", "parallel", "arbitrary"), vmem_limit_bytes=_vmem(est)),
    )(*args)


def _matmul_pair(a, b, pair, *, mode, out_dtype, name, after=()):
    if mode == "tn":
        (R, P), (R2, Q) = a.shape, b.shape
        assert R == R2 and pair.shape == b.shape
        tp, tq, tr = _matmul_tiles(P, Q, R, a.dtype.itemsize, 2 * b.dtype.itemsize,
                                   jnp.dtype(out_dtype).itemsize, False, None, None, None, repeat=2)
        nq, nk = Q // tq, R // tr
        grid, nk_total = (P // tp, 2 * nq, nk), nk
        a_spec = pl.BlockSpec((tr, tp), lambda i, j, k: (k, i))
        b_spec = pl.BlockSpec((tr, tq), lambda i, j, k: (jnp.where(j < nq, k, nk - 1), jnp.minimum(j, nq - 1)))
        p_spec = pl.BlockSpec((tr, tq), lambda i, j, k: (jnp.where(j >= nq, k, 0), jnp.maximum(j - nq, 0)))
        out_shape = (P, 2 * Q)
        dims = (((0,), (0,)), ((), ()))
    else:
        assert mode == "nt"
        (P, R), (Q, R2) = a.shape, b.shape
        assert R2 == 2 * R and pair.shape == a.shape
        tp, tq, tr = _matmul_tiles(P, Q, R, 2 * a.dtype.itemsize, b.dtype.itemsize,
                                   jnp.dtype(out_dtype).itemsize, False, None, None, None, repeat=2)
        nk = R // tr
        grid, nk_total = (P // tp, Q // tq, 2 * nk), 2 * nk
        a_spec = pl.BlockSpec((tp, tr), lambda i, j, k: (i, jnp.minimum(k, nk - 1)))
        p_spec = pl.BlockSpec((tp, tr), lambda i, j, k: (i, jnp.maximum(k - nk, 0)))
        b_spec = pl.BlockSpec((tq, tr), lambda i, j, k: (j, k))
        out_shape = (P, Q)
        dims = (((1,), (1,)), ((), ()))
    n_after = len(after)

    def body(a_ref, b_ref, p_ref, *rest):
        o_ref = rest[n_after]
        acc_ref = rest[n_after + 1] if nk_total > 1 else None
        j, k = pl.program_id(1), pl.program_id(2)

        def step(l_ref, r_ref):
            part = lax.dot_general(l_ref[...], r_ref[...], dims, preferred_element_type=F32)
            if nk_total == 1:
                o_ref[...] = part.astype(out_dtype)
                return

            @pl.when(k == 0)
            def _():
                acc_ref[...] = part

            @pl.when(k > 0)
            def _():
                acc_ref[...] += part

            @pl.when(k == nk_total - 1)
            def _():
                o_ref[...] = acc_ref[...].astype(out_dtype)

        first = (j < nq) if mode == "tn" else (k < nk)

        @pl.when(first)
        def _():
            step(a_ref, b_ref)

        @pl.when(jnp.logical_not(first))
        def _():
            step(a_ref if mode == "tn" else p_ref, p_ref if mode == "tn" else b_ref)

    n_a, n_b = (1, 2) if mode == "tn" else (2, 1)
    est = (2 * (n_a * _nbytes((tp, tr), a.dtype) + n_b * _nbytes((tr, tq), b.dtype) + _nbytes((tp, tq), out_dtype))
           + 2 * tp * tq * 4)
    return pl.pallas_call(
        body, name=name, out_shape=jax.ShapeDtypeStruct(out_shape, out_dtype), grid=grid,
        in_specs=[a_spec, b_spec, p_spec] + [pl.BlockSpec(memory_space=pl.ANY)] * n_after,
        out_specs=pl.BlockSpec((tp, tq), lambda i, j, k: (i, j)),
        scratch_shapes=[pltpu.VMEM((tp, tq), F32)] if nk_total > 1 else [],
        compiler_params=pltpu.CompilerParams(
            dimension_semantics=("parallel", "parallel", "arbitrary"), vmem_limit_bytes=_vmem(est)),
    )(a, b, pair, *after)


def _row_call(body, ins, outs, *, name, rows, tr, acc_outs=(), est=0, after=()):
    in_specs, args = [], []
    for arr, kind in ins:
        if kind == "row":
            in_specs.append(pl.BlockSpec((tr, arr.shape[1]), lambda i: (i, 0)))
        elif isinstance(arr, tuple):
            arr, layer = arr
            in_specs.append(pl.BlockSpec((None,) + arr.shape[1:], lambda i, layer=layer: (layer, 0, 0)))
        else:
            nd = arr.ndim
            in_specs.append(pl.BlockSpec(arr.shape, lambda i, nd=nd: (0,) * nd))
        args.append(arr)
    n_ins = len(args)
    in_specs += [pl.BlockSpec(memory_space=pl.ANY)] * len(after)
    args += list(after)

    def kernel_fn(*refs):
        body(*refs[:n_ins], *refs[n_ins + len(after):])

    out_shapes = [jax.ShapeDtypeStruct(s, d) for s, d in outs] + [jax.ShapeDtypeStruct(s, d) for s, d in acc_outs]
    out_specs = [pl.BlockSpec((tr, s[1]), lambda i: (i, 0)) for s, _ in outs]
    out_specs += [pl.BlockSpec(s, lambda i, nd=len(s): (0,) * nd) for s, _ in acc_outs]
    res = pl.pallas_call(
        kernel_fn, name=name, out_shape=out_shapes, grid=(rows // tr,), in_specs=in_specs, out_specs=out_specs,
        compiler_params=pltpu.CompilerParams(dimension_semantics=("arbitrary",), vmem_limit_bytes=_vmem(est)),
    )(*args)
    return res


def _rms_fwd(x, g, *, out_dtype, name, after=()):
    T, D = x.shape
    tr = _pick(T, (512, 256, 128))

    def body(x_ref, g_ref, o_ref):
        xv = x_ref[...]
        r = lax.rsqrt(jnp.mean(xv * xv, axis=-1, keepdims=True) + RMS_EPS)
        o_ref[...] = (xv * r * g_ref[...]).astype(out_dtype)

    return _row_call(body, [(x, "row"), (g, "full")], [((T, D), out_dtype)], name=name, rows=T, tr=tr,
                     est=8 * tr * D * 4, after=after)[0]


def _rms_res(h, y, g, *, name):
    T, D = h.shape
    tr = _pick(T, (512, 256, 128))

    def body(h_ref, y_ref, g_ref, o_ref):
        yv = y_ref[...]
        r = lax.rsqrt(jnp.mean(yv * yv, axis=-1, keepdims=True) + RMS_EPS)
        o_ref[...] = h_ref[...] + yv * r * g_ref[...]

    return _row_call(body, [(h, "row"), (y, "row"), (g, "full")], [((T, D), F32)], name=name, rows=T, tr=tr,
                     est=10 * tr * D * 4)[0]


def _rms_bwd(x, g, dy, dres, *, out_dtype, name, after=()):
    T, D = x.shape
    tr = _pick(T, (512, 256, 128))
    has_res = dres is not None

    def body(*refs):
        if has_res:
            x_ref, g_ref, dy_ref, dr_ref, dx_ref, dg_ref = refs
        else:
            x_ref, g_ref, dy_ref, dx_ref, dg_ref = refs
        xv = x_ref[...]
        r = lax.rsqrt(jnp.mean(xv * xv, axis=-1, keepdims=True) + RMS_EPS)
        xhat = xv * r
        dyv = dy_ref[...].astype(F32)
        dxn = dyv * g_ref[...]
        dx = r * (dxn - xhat * jnp.mean(dxn * xhat, axis=-1, keepdims=True))
        if has_res:
            dx = dx + dr_ref[...]
        dx_ref[...] = dx.astype(out_dtype)
        part = jnp.sum(dyv * xhat, axis=0, keepdims=True)

        @pl.when(pl.program_id(0) == 0)
        def _():
            dg_ref[...] = part

        @pl.when(pl.program_id(0) > 0)
        def _():
            dg_ref[...] += part

    ins = [(x, "row"), (g, "full"), (dy, "row")] + ([(dres, "row")] if has_res else [])
    dx, dg = _row_call(body, ins, [((T, D), out_dtype)], name=name, rows=T, tr=tr, acc_outs=[((1, D), F32)],
                       est=12 * tr * D * 4, after=after)
    return dx, dg


def _rms_res_norm(h, y, g_res, g_next, *, name, after=()):
    T, D = h.shape
    tr = _pick(T, (512, 256, 128))

    def body(h_ref, y_ref, g_ref, gn_ref, o_ref, n_ref):
        yv = y_ref[...]
        r = lax.rsqrt(jnp.mean(yv * yv, axis=-1, keepdims=True) + RMS_EPS)
        h2 = h_ref[...] + yv * r * g_ref[...]
        o_ref[...] = h2
        r2 = lax.rsqrt(jnp.mean(h2 * h2, axis=-1, keepdims=True) + RMS_EPS)
        n_ref[...] = (h2 * r2 * gn_ref[...]).astype(BF16)

    return _row_call(body, [(h, "row"), (y, "row"), (g_res, "full"), (g_next, "full")],
                     [((T, D), F32), ((T, D), BF16)], name=name, rows=T, tr=tr, est=12 * tr * D * 4, after=after)


def _rms_bwd_chain(x1, g1, dy1, dres, x2, g2, *, name, after=()):
    T, D = x1.shape
    tr = _pick(T, (512, 256, 128))

    def one(xv, gv, dyv):
        r = lax.rsqrt(jnp.mean(xv * xv, axis=-1, keepdims=True) + RMS_EPS)
        xhat = xv * r
        dxn = dyv * gv
        dx = r * (dxn - xhat * jnp.mean(dxn * xhat, axis=-1, keepdims=True))
        return dx, jnp.sum(dyv * xhat, axis=0, keepdims=True)

    def body(x1_ref, g1_ref, dy1_ref, dr_ref, x2_ref, g2_ref, d1_ref, d2_ref, dg1_ref, dg2_ref):
        dx1, p1 = one(x1_ref[...], g1_ref[...], dy1_ref[...].astype(F32))
        d1 = dx1 + dr_ref[...]
        d1_ref[...] = d1
        dx2, p2 = one(x2_ref[...], g2_ref[...], d1)
        d2_ref[...] = dx2.astype(BF16)

        @pl.when(pl.program_id(0) == 0)
        def _():
            dg1_ref[...] = p1
            dg2_ref[...] = p2

        @pl.when(pl.program_id(0) > 0)
        def _():
            dg1_ref[...] += p1
            dg2_ref[...] += p2

    ins = [(x1, "row"), (g1, "full"), (dy1, "row"), (dres, "row"), (x2, "row"), (g2, "full")]
    return _row_call(body, ins, [((T, D), F32), ((T, D), BF16)], name=name, rows=T, tr=tr,
                     acc_outs=[((1, D), F32), ((1, D), F32)], est=20 * tr * D * 4, after=after)


def _ffn_up(fn, w_gu, l, *, name):
    T, D = fn.shape
    H = w_gu.shape[2] // 2
    tp = _pick(T, (256, 128))
    tq = H
    nj = H // tq

    def body(a_ref, wg_ref, wu_ref, g_ref, u_ref, act_ref):
        a = a_ref[...]
        g = jnp.dot(a, wg_ref[...], preferred_element_type=F32)
        u = jnp.dot(a, wu_ref[...], preferred_element_type=F32)
        sg = jax.nn.sigmoid(g)
        silu = g * sg
        g_ref[...] = (u * (sg + silu * (1.0 - sg))).astype(BF16)
        u_ref[...] = silu.astype(BF16)
        act_ref[...] = (silu * u).astype(BF16)

    tile = pl.BlockSpec((tp, tq), lambda j, i: (i, j))
    est = 2 * (tp * D * 2 + 2 * D * tq * 2 + 3 * tp * tq * 2) + 4 * tp * tq * 4
    return pl.pallas_call(
        body, name=name,
        out_shape=[jax.ShapeDtypeStruct((T, H), BF16), jax.ShapeDtypeStruct((T, H), BF16),
                   jax.ShapeDtypeStruct((T, H), BF16)],
        grid=(nj, T // tp),
        in_specs=[pl.BlockSpec((tp, D), lambda j, i: (i, 0)),
                  pl.BlockSpec((None, D, tq), lambda j, i: (l, 0, j)),
                  pl.BlockSpec((None, D, tq), lambda j, i: (l, 0, j + nj))],
        out_specs=[tile, tile, tile],
        compiler_params=pltpu.CompilerParams(dimension_semantics=("parallel", "parallel"),
                                             vmem_limit_bytes=_vmem(est)),
    )(fn, w_gu, w_gu)


def _ffn_down_dx(df, w_down, l, g, u, after, *, name):
    T, D = df.shape
    H = w_down.shape[1]
    tp = _pick(T, (512, 256, 128))
    tq = H

    def body(a_ref, w_ref, g_ref, u_ref, _, dg_ref, du_ref):
        da = lax.dot_general(a_ref[...], w_ref[...], (((1,), (1,)), ((), ())), preferred_element_type=F32)
        dg_ref[...] = (da * g_ref[...].astype(F32)).astype(BF16)
        du_ref[...] = (da * u_ref[...].astype(F32)).astype(BF16)

    tile = pl.BlockSpec((tp, tq), lambda j, i: (i, j))
    est = 2 * (tp * D * 2 + tq * D * 2 + 4 * tp * tq * 2) + 3 * tp * tq * 4
    return pl.pallas_call(
        body, name=name,
        out_shape=[jax.ShapeDtypeStruct((T, H), BF16), jax.ShapeDtypeStruct((T, H), BF16)],
        grid=(H // tq, T // tp),
        in_specs=[pl.BlockSpec((tp, D), lambda j, i: (i, 0)),
                  pl.BlockSpec((None, tq, D), lambda j, i: (l, j, 0)), tile, tile,
                  pl.BlockSpec(memory_space=pl.ANY)],
        out_specs=[tile, tile],
        compiler_params=pltpu.CompilerParams(dimension_semantics=("parallel", "parallel"),
                                             vmem_limit_bytes=_vmem(est)),
    )(df, w_down, g, u, after)


def _loss_and_grad(y, target, x, g, *, name):
    T, D = y.shape
    tr = _pick(T, (512, 256, 128))

    def body(y_ref, t_ref, x_ref, g_ref, dy_ref, dx_ref, l_ref, dg_ref):
        e = y_ref[...] - t_ref[...]
        dy = e * (1.0 / D)
        dy_ref[...] = dy
        part = jnp.sum(jnp.sum(e * e, axis=1, keepdims=True), axis=0, keepdims=True) * (0.5 / D)
        xv = x_ref[...]
        r = lax.rsqrt(jnp.mean(xv * xv, axis=-1, keepdims=True) + RMS_EPS)
        xhat = xv * r
        dxn = dy * g_ref[...]
        dx_ref[...] = (r * (dxn - xhat * jnp.mean(dxn * xhat, axis=-1, keepdims=True))).astype(BF16)
        dg = jnp.sum(dy * xhat, axis=0, keepdims=True)

        @pl.when(pl.program_id(0) == 0)
        def _():
            l_ref[...] = part
            dg_ref[...] = dg

        @pl.when(pl.program_id(0) > 0)
        def _():
            l_ref[...] += part
            dg_ref[...] += dg

    dy, dx, l, dg = _row_call(body, [(y, "row"), (target, "row"), (x, "row"), (g, "full")],
                              [((T, D), F32), ((T, D), BF16)], name=name, rows=T, tr=tr,
                              acc_outs=[((1, 1), F32), ((1, D), F32)], est=14 * tr * D * 4)
    return dy, dx, l, dg


_SQRT_HALF = 0.7071067811865476
_INV_SQRT_2PI = 0.3989422804014327


def _gelu_parts(x):
    cdf = 0.5 * (1.0 + lax.erf(x * _SQRT_HALF))
    return cdf


def _sgu_common(pre, lng, lnb, W):
    cdf = _gelu_parts(pre)
    z = pre * cdf
    u = z[:, :W]
    v = z[:, W:]
    mu = jnp.mean(v, axis=-1, keepdims=True)
    vc = v - mu
    var = jnp.mean(vc * vc, axis=-1, keepdims=True)
    rstd = lax.rsqrt(var + LN_EPS)
    vhat = vc * rstd
    vn = vhat * lng + lnb
    return cdf, u, vhat, rstd, vn


def _causal_mask():
    t = lax.broadcasted_iota(jnp.int32, (CHUNK, CHUNK), 0)
    s = lax.broadcasted_iota(jnp.int32, (CHUNK, CHUNK), 1)
    return t >= s


def _sgu_fwd(pre, lng, lnb, ws, bsT, *, name):
    T, W2 = pre.shape
    W = W2 // 2
    G = ws.shape[0]
    gd = W // G

    def body(pre_ref, lng_ref, lnb_ref, ws_ref, bs_ref, o_ref):
        _, u, _, _, vn = _sgu_common(pre_ref[...], lng_ref[...], lnb_ref[...], W)
        vnb = vn.astype(BF16)
        causal = _causal_mask()
        for g in range(G):
            w = jnp.where(causal, ws_ref[g], 0.0).astype(BF16)
            sv = jnp.dot(w, vnb[:, g * gd:(g + 1) * gd], preferred_element_type=F32) + bs_ref[:, g:g + 1]
            o_ref[:, g * gd:(g + 1) * gd] = (u[:, g * gd:(g + 1) * gd] * sv).astype(BF16)

    return pl.pallas_call(
        body, name=name, out_shape=jax.ShapeDtypeStruct((T, W), BF16), grid=(T // CHUNK,),
        in_specs=[pl.BlockSpec((CHUNK, W2), lambda i: (i, 0)),
                  pl.BlockSpec((1, W), lambda i: (0, 0)), pl.BlockSpec((1, W), lambda i: (0, 0)),
                  pl.BlockSpec(ws.shape, lambda i: (0, 0, 0)), pl.BlockSpec(bsT.shape, lambda i: (0, 0))],
        out_specs=pl.BlockSpec((CHUNK, W), lambda i: (i, 0)),
        compiler_params=pltpu.CompilerParams(dimension_semantics=("arbitrary",),
                                             vmem_limit_bytes=_vmem(12 * CHUNK * W2 * 4)),
    )(pre, lng, lnb, ws, bsT)


def _sgu_bwd(pre, dgated, lng, lnb, ws, bsT, *, name):
    T, W2 = pre.shape
    W = W2 // 2
    G = ws.shape[0]
    gd = W // G

    def body(pre_ref, dgt_ref, lng_ref, lnb_ref, ws_ref, bs_ref,
             dpre_ref, dws_ref, dbs_ref, dlng_ref, dlnb_ref, dbin_ref):
        first = pl.program_id(0) == 0

        @pl.when(first)
        def _():
            dws_ref[...] = jnp.zeros_like(dws_ref)
            dbs_ref[...] = jnp.zeros_like(dbs_ref)
            dlng_ref[...] = jnp.zeros_like(dlng_ref)
            dlnb_ref[...] = jnp.zeros_like(dlnb_ref)
            dbin_ref[...] = jnp.zeros_like(dbin_ref)

        pre_v = pre_ref[...]
        lng_v = lng_ref[...]
        cdf, u, vhat, rstd, vn = _sgu_common(pre_v, lng_v, lnb_ref[...], W)
        vnb = vn.astype(BF16)
        dgt = dgt_ref[...].astype(F32)
        causal = _causal_mask()
        du_parts, dvn_parts = [], []
        for g in range(G):
            sl = slice(g * gd, (g + 1) * gd)
            w = jnp.where(causal, ws_ref[g], 0.0).astype(BF16)
            sv = jnp.dot(w, vnb[:, sl], preferred_element_type=F32) + bs_ref[:, g:g + 1]
            dgt_g = dgt[:, sl]
            du_parts.append(dgt_g * sv)
            dsv = dgt_g * u[:, sl]
            dsvb = dsv.astype(BF16)
            dvn_parts.append(lax.dot_general(w, dsvb, (((0,), (0,)), ((), ())), preferred_element_type=F32))
            dw = lax.dot_general(dsvb, vnb[:, sl], (((1,), (1,)), ((), ())), preferred_element_type=F32)
            dws_ref[g] += jnp.where(causal, dw, 0.0)
            dbs_ref[:, g:g + 1] += jnp.sum(dsv, axis=1, keepdims=True)
        du = jnp.concatenate(du_parts, axis=1)
        dvn = jnp.concatenate(dvn_parts, axis=1)
        dlng_ref[...] += jnp.sum(dvn * vhat, axis=0, keepdims=True)
        dlnb_ref[...] += jnp.sum(dvn, axis=0, keepdims=True)
        dvh = dvn * lng_v
        dv = rstd * (dvh - jnp.mean(dvh, axis=-1, keepdims=True)
                     - vhat * jnp.mean(dvh * vhat, axis=-1, keepdims=True))
        dz = jnp.concatenate([du, dv], axis=1)
        dgelu = cdf + pre_v * jnp.exp(-0.5 * pre_v * pre_v) * _INV_SQRT_2PI
        dpre = dz * dgelu
        dbin_ref[...] += jnp.sum(dpre, axis=0, keepdims=True)
        dpre_ref[...] = dpre.astype(BF16)

    full = lambda shape: pl.BlockSpec(shape, lambda i, nd=len(shape): (0,) * nd)
    return pl.pallas_call(
        body, name=name,
        out_shape=[jax.ShapeDtypeStruct((T, W2), BF16), jax.ShapeDtypeStruct(ws.shape, F32),
                   jax.ShapeDtypeStruct(bsT.shape, F32), jax.ShapeDtypeStruct((1, W), F32),
                   jax.ShapeDtypeStruct((1, W), F32), jax.ShapeDtypeStruct((1, W2), F32)],
        grid=(T // CHUNK,),
        in_specs=[pl.BlockSpec((CHUNK, W2), lambda i: (i, 0)), pl.BlockSpec((CHUNK, W), lambda i: (i, 0)),
                  full((1, W)), full((1, W)), full(ws.shape), full(bsT.shape)],
        out_specs=[pl.BlockSpec((CHUNK, W2), lambda i: (i, 0)), full(ws.shape), full(bsT.shape),
                   full((1, W)), full((1, W)), full((1, W2))],
        compiler_params=pltpu.CompilerParams(dimension_semantics=("arbitrary",),
                                             vmem_limit_bytes=_vmem(24 * CHUNK * W2 * 4)),
    )(pre, dgated, lng, lnb, ws, bsT)


def _rope_tables(positions):
    half = ROPE_DIM // 2
    inv_freq = ROPE_THETA ** (-jnp.arange(0, ROPE_DIM, 2, dtype=F32) / ROPE_DIM)
    ang = positions.astype(F32).reshape(-1, 1) * inv_freq
    cos, sin = jnp.cos(ang), jnp.sin(ang)
    T = ang.shape[0]
    rest = HEAD_DIM - ROPE_DIM
    c64 = jnp.concatenate([cos, cos, jnp.ones((T, rest), F32)], axis=1)
    s64 = jnp.concatenate([-sin, sin, jnp.zeros((T, rest), F32)], axis=1)
    del half
    return jnp.tile(c64, (1, LANES // HEAD_DIM)), jnp.tile(s64, (1, LANES // HEAD_DIM))


def _swap8(x):
    W = x.shape[1]
    half = ROPE_DIM // 2
    lane = lax.broadcasted_iota(jnp.int32, x.shape, 1) % HEAD_DIM
    return jnp.where(lane < half, pltpu.roll(x, W - half, axis=1),
                     jnp.where(lane < ROPE_DIM, pltpu.roll(x, half, axis=1), 0.0))


def _wide(tab, W):
    return jnp.concatenate([tab] * (W // LANES), axis=1) if W > LANES else tab


def _rope_fwd(qkv, ctab, stab, *, q_width, kv_width, name):
    T = qkv.shape[0]
    tr = _pick(T, (256, 128))
    scale = HEAD_DIM ** -0.5

    def body(x_ref, c_ref, s_ref, q_ref, k_ref, v_ref):
        c = c_ref[...]
        s = s_ref[...]
        q = x_ref[:, :q_width]
        k = x_ref[:, q_width:q_width + kv_width]
        q_ref[...] = ((q * _wide(c, q_width) + _swap8(q) * _wide(s, q_width)) * scale).astype(BF16)
        k_ref[...] = (k * _wide(c, kv_width) + _swap8(k) * _wide(s, kv_width)).astype(BF16)
        v_ref[...] = x_ref[:, q_width + kv_width:].astype(BF16)

    return _row_call(body, [(qkv, "row"), (ctab, "row"), (stab, "row")],
                     [((T, q_width), BF16), ((T, kv_width), BF16), ((T, kv_width), BF16)],
                     name=name, rows=T, tr=tr, est=10 * tr * qkv.shape[1] * 4)


_NT = (((1,), (1,)), ((), ()))
_TN = (((0,), (0,)), ((), ()))


def _group_rows(ref, heads):
    return jnp.concatenate([ref[:, h * HEAD_DIM:(h + 1) * HEAD_DIM] for h in heads], axis=0)


def _attn_valid(grp):
    qi = np.arange(grp * CHUNK)[:, None] % CHUNK
    sj = np.arange(2 * CHUNK)[None, :]
    cur = (sj >= CHUNK) & (sj - CHUNK <= qi)
    prev = (sj < CHUNK) & (sj > qi)
    return jnp.asarray(np.stack([cur, cur | prev]).astype(np.float32))


def _valid_spec(grp):
    return pl.BlockSpec((None, grp * CHUNK, 2 * CHUNK), lambda n: (jnp.minimum(n, 1), 0, 0))


def _attn_group_probs(q, kk, sinks, valid, grp):
    rows = grp * CHUNK
    s = lax.dot_general(q, kk, _NT, preferred_element_type=F32)
    s = jnp.where(valid, s, NEG_INF)
    r = lax.broadcasted_iota(jnp.int32, (rows, 1), 0)
    sink = jnp.full((rows, 1), sinks[grp - 1], F32)
    for g in range(grp - 2, -1, -1):
        sink = jnp.where(r < (g + 1) * CHUNK, sinks[g], sink)
    m = jnp.maximum(jnp.max(s, axis=1, keepdims=True), sink)
    p = jnp.exp(s - m)
    ps = jnp.exp(sink - m)
    inv = 1.0 / (jnp.sum(p, axis=1, keepdims=True) + ps)
    return p * inv, ps * inv


def _kv_specs(width, nb):
    prev = pl.BlockSpec((CHUNK, width), lambda n: (jnp.maximum(n - 1, 0), 0))
    cur = pl.BlockSpec((CHUNK, width), lambda n: (n, 0))
    return prev, cur


def _attn_fwd(qr, kr, vr, sinks, *, name):
    T, QW = qr.shape
    KW = kr.shape[1]
    HQ, HK = QW // HEAD_DIM, KW // HEAD_DIM
    grp = HQ // HK
    nb = T // CHUNK

    def body(q_ref, kp_ref, kc_ref, vp_ref, vc_ref, s_ref, ok_ref, o_ref):
        valid = ok_ref[...] > 0.5
        for kh in range(HK):
            ks = slice(kh * HEAD_DIM, (kh + 1) * HEAD_DIM)
            heads = list(range(kh * grp, (kh + 1) * grp))
            q = _group_rows(q_ref, heads)
            kk = jnp.concatenate([kp_ref[:, ks], kc_ref[:, ks]], axis=0)
            vv = jnp.concatenate([vp_ref[:, ks], vc_ref[:, ks]], axis=0)
            p, _ = _attn_group_probs(q, kk, [s_ref[0, h] for h in heads], valid, grp)
            o = jnp.dot(p.astype(BF16), vv, preferred_element_type=F32).astype(BF16)
            for g, h in enumerate(heads):
                o_ref[:, h * HEAD_DIM:(h + 1) * HEAD_DIM] = o[g * CHUNK:(g + 1) * CHUNK]

    kp, kc = _kv_specs(KW, nb)
    return pl.pallas_call(
        body, name=name, out_shape=jax.ShapeDtypeStruct((T, QW), BF16), grid=(nb,),
        in_specs=[pl.BlockSpec((CHUNK, QW), lambda n: (n, 0)), kp, kc, kp, kc,
                  pl.BlockSpec(memory_space=pltpu.SMEM), _valid_spec(grp)],
        out_specs=pl.BlockSpec((CHUNK, QW), lambda n: (n, 0)),
        compiler_params=pltpu.CompilerParams(dimension_semantics=("arbitrary",), vmem_limit_bytes=_vmem(8 << 20)),
    )(qr, kr, kr, vr, vr, sinks, _attn_valid(grp))


def _attn_bwd(qr, kr, vr, sinks, do, *, name):
    T, QW = qr.shape
    KW = kr.shape[1]
    HQ, HK = QW // HEAD_DIM, KW // HEAD_DIM
    grp = HQ // HK
    nb = T // CHUNK

    def body(q_ref, kp_ref, kc_ref, vp_ref, vc_ref, s_ref, do_ref, ok_ref,
             dq_ref, dkp_ref, dkc_ref, dvp_ref, dvc_ref, ds_ref):
        n = pl.program_id(0)
        valid = ok_ref[...] > 0.5
        lane = lax.broadcasted_iota(jnp.int32, (1, LANES), 1)
        dsink = jnp.zeros((1, LANES), F32)
        for kh in range(HK):
            ks = slice(kh * HEAD_DIM, (kh + 1) * HEAD_DIM)
            heads = list(range(kh * grp, (kh + 1) * grp))
            q = _group_rows(q_ref, heads)
            doh = _group_rows(do_ref, heads)
            kk = jnp.concatenate([kp_ref[:, ks], kc_ref[:, ks]], axis=0)
            vv = jnp.concatenate([vp_ref[:, ks], vc_ref[:, ks]], axis=0)
            p, ps = _attn_group_probs(q, kk, [s_ref[0, h] for h in heads], valid, grp)
            dp = lax.dot_general(doh, vv, _NT, preferred_element_type=F32)
            delta = jnp.sum(p * dp, axis=1, keepdims=True)
            ds = (p * (dp - delta)).astype(BF16)
            dv = lax.dot_general(p.astype(BF16), doh, _TN, preferred_element_type=F32)
            dk = lax.dot_general(ds, q, _TN, preferred_element_type=F32)
            dq = jnp.dot(ds, kk, preferred_element_type=F32)
            psd = ps * delta
            for g, h in enumerate(heads):
                dq_ref[:, h * HEAD_DIM:(h + 1) * HEAD_DIM] = dq[g * CHUNK:(g + 1) * CHUNK]
                dsink = dsink + jnp.where(
                    lane == h, -jnp.sum(psd[g * CHUNK:(g + 1) * CHUNK], axis=0, keepdims=True), 0.0)
            dkp_ref[:, ks] = dk[:CHUNK]
            dkc_ref[:, ks] = dk[CHUNK:]
            dvp_ref[:, ks] = dv[:CHUNK]
            dvc_ref[:, ks] = dv[CHUNK:]

        @pl.when(n == 0)
        def _():
            ds_ref[...] = dsink

        @pl.when(n > 0)
        def _():
            ds_ref[...] += dsink

    kp, kc = _kv_specs(KW, nb)
    qspec = pl.BlockSpec((CHUNK, QW), lambda n: (n, 0))
    kout = pl.BlockSpec((CHUNK, KW), lambda n: (n, 0))
    return pl.pallas_call(
        body, name=name,
        out_shape=[jax.ShapeDtypeStruct((T, QW), F32)] + [jax.ShapeDtypeStruct((T, KW), F32)] * 4
        + [jax.ShapeDtypeStruct((1, LANES), F32)],
        grid=(nb,),
        in_specs=[qspec, kp, kc, kp, kc, pl.BlockSpec(memory_space=pltpu.SMEM), qspec, _valid_spec(grp)],
        out_specs=[qspec, kout, kout, kout, kout, pl.BlockSpec((1, LANES), lambda n: (0, 0))],
        compiler_params=pltpu.CompilerParams(dimension_semantics=("arbitrary",), vmem_limit_bytes=_vmem(12 << 20)),
    )(qr, kr, kr, vr, vr, sinks, do, _attn_valid(grp))


def _rope_bwd(dq, dkp, dkc, dvp, dvc, ctab, stab, *, name):
    T, QW = dq.shape
    KW = dkp.shape[1]
    nb = T // CHUNK
    scale = HEAD_DIM ** -0.5
    width = QW + 2 * KW

    def body(dq_ref, dkc_ref, dkn_ref, dvc_ref, dvn_ref, c_ref, s_ref, o_ref, db_ref):
        n = pl.program_id(0)
        c = c_ref[...]
        s = s_ref[...]
        has_next = (n < nb - 1).astype(F32)
        dqv = dq_ref[...]
        dk = dkc_ref[...] + has_next * dkn_ref[...]
        dv = dvc_ref[...] + has_next * dvn_ref[...]
        dq_pre = (dqv * _wide(c, QW) + _swap8(dqv * _wide(s, QW))) * scale
        dk_pre = dk * _wide(c, KW) + _swap8(dk * _wide(s, KW))
        o_ref[:, :QW] = dq_pre.astype(BF16)
        o_ref[:, QW:QW + KW] = dk_pre.astype(BF16)
        o_ref[:, QW + KW:] = dv.astype(BF16)
        part = jnp.concatenate([jnp.sum(dq_pre, axis=0, keepdims=True), jnp.sum(dk_pre, axis=0, keepdims=True),
                                jnp.sum(dv, axis=0, keepdims=True)], axis=1)

        @pl.when(n == 0)
        def _():
            db_ref[...] = part

        @pl.when(n > 0)
        def _():
            db_ref[...] += part

    cur = lambda w: pl.BlockSpec((CHUNK, w), lambda n: (n, 0))
    nxt = lambda w: pl.BlockSpec((CHUNK, w), lambda n: (jnp.minimum(n + 1, nb - 1), 0))
    return pl.pallas_call(
        body, name=name,
        out_shape=[jax.ShapeDtypeStruct((T, width), BF16), jax.ShapeDtypeStruct((1, width), F32)],
        grid=(nb,),
        in_specs=[cur(QW), cur(KW), nxt(KW), cur(KW), nxt(KW), cur(LANES), cur(LANES)],
        out_specs=[cur(width), pl.BlockSpec((1, width), lambda n: (0, 0))],
        compiler_params=pltpu.CompilerParams(dimension_semantics=("arbitrary",), vmem_limit_bytes=_vmem(8 << 20)),
    )(dq, dkc, dkp, dvc, dvp, ctab, stab)


def _cast_block(w, l, axis, chip_arr, *, name):
    _, Ks, Ns = w.shape
    tk = _pick(Ks, (512, 352, 256, 128))
    nk = Ks // tk
    full = (Ks * N_CHIPS, Ns) if axis == 0 else (Ks, Ns * N_CHIPS)

    def body(p_ref, w_ref, o_ref):
        o_ref[...] = w_ref[...].astype(BF16)

    if axis == 0:
        out_spec = pl.BlockSpec((tk, Ns), lambda i, p: (p[0] * nk + i, 0))
    else:
        out_spec = pl.BlockSpec((tk, Ns), lambda i, p: (i, p[0]))
    grid_spec = pltpu.PrefetchScalarGridSpec(
        num_scalar_prefetch=1, grid=(nk,),
        in_specs=[pl.BlockSpec((None, tk, Ns), lambda i, p: (l, i, 0))], out_specs=out_spec)
    return pl.pallas_call(
        body, name=name, out_shape=jax.ShapeDtypeStruct(full, BF16), grid_spec=grid_spec,
        compiler_params=pltpu.CompilerParams(dimension_semantics=("arbitrary",),
                                             vmem_limit_bytes=_vmem(4 * tk * Ns * 6)),
    )(chip_arr, w)


def _adamw_math(w, g, m, v):
    m = ADAM_B1 * m + (1.0 - ADAM_B1) * g
    v = ADAM_B2 * v + (1.0 - ADAM_B2) * (g * g)
    m_hat = m / (1.0 - ADAM_B1 ** ADAM_STEP)
    v_hat = v / (1.0 - ADAM_B2 ** ADAM_STEP)
    delta = -ADAM_LR * (m_hat / (jnp.sqrt(v_hat) + ADAM_EPS) + ADAM_WD * w)
    return delta, m, v


def _adamw_layer(w, m, v, g, l, outs, *, name, after=()):
    _, K, N = w.shape
    tk = _pick(K, (512, 352, 256, 128)) if N <= 1024 else _pick(K, (256, 176, 128))
    n_after = len(after)

    def body(w_ref, m_ref, v_ref, g_ref, *rest):
        go_ref, d_ref, mo_ref, vo_ref = rest[4 + n_after:]
        gv = g_ref[...]
        d, mn, vn = _adamw_math(w_ref[...], gv, m_ref[...], v_ref[...])
        go_ref[...] = gv
        d_ref[...] = d
        mo_ref[...] = mn
        vo_ref[...] = vn

    layer = pl.BlockSpec((None, tk, N), lambda i: (l, i, 0))
    any_spec = pl.BlockSpec(memory_space=pl.ANY)
    sd = jax.ShapeDtypeStruct(w.shape, F32)
    return pl.pallas_call(
        body, name=name, out_shape=[sd, sd, sd, sd], grid=(K // tk,),
        in_specs=[layer, layer, layer, pl.BlockSpec((tk, N), lambda i: (i, 0))] + [any_spec] * (4 + n_after),
        out_specs=[layer] * 4, input_output_aliases={4: 0, 5: 1, 6: 2, 7: 3},
        compiler_params=pltpu.CompilerParams(dimension_semantics=("arbitrary",),
                                             vmem_limit_bytes=_vmem(2 * 8 * tk * N * 4 + 6 * tk * N * 4)),
    )(w, m, v, g, *outs, *after)


def _adamw_small(params, *, name):
    n = len(params)

    def body(*refs):
        ins, outs = refs[:4 * n], refs[4 * n:]
        for k in range(n):
            w_ref, g_ref, m_ref, v_ref = ins[4 * k:4 * k + 4]
            d, mn, vn = _adamw_math(w_ref[...], g_ref[...], m_ref[...], v_ref[...])
            outs[3 * k][...] = d
            outs[3 * k + 1][...] = mn
            outs[3 * k + 2][...] = vn

    out_shape = [jax.ShapeDtypeStruct(p[0].shape, F32) for p in params for _ in range(3)]
    res = pl.pallas_call(body, name=name, out_shape=out_shape)(*[a for p in params for a in p])
    return [tuple(res[3 * k:3 * k + 3]) for k in range(n)]


def _my_place():
    return lax.axis_index("x"), lax.axis_index("y"), lax.axis_index("c")


def _peer_chips(x, y):
    return [(1 - x, y), (x, 1 - y), (1 - x, 1 - y)]


_HBM = pl.BlockSpec(memory_space=pltpu.HBM)
_SEM = pl.BlockSpec(memory_space=pltpu.SEMAPHORE)
_EFFECT = pltpu.SideEffectType.DATAFLOW_SIDE_EFFECTING


def _split_start(name, bufs, n_copies, make_copies, after):
    nb = len(bufs)

    def body(*refs):
        send_sems, recv_sems = refs[nb + 1], refs[nb + 2]
        token = refs[2 * nb + 3]
        sends, _ = make_copies(refs[:nb], send_sems, recv_sems)
        for cp in sends:
            cp.start()
        token[...] = jnp.zeros_like(token)

    res = pl.pallas_call(
        body, name=name,
        out_shape=(pltpu.SemaphoreType.DMA((n_copies,)), pltpu.SemaphoreType.DMA((n_copies,)),
                   *[pltpu.HBM(b.shape, b.dtype) for b in bufs], jax.ShapeDtypeStruct((8, LANES), F32)),
        in_specs=[_HBM] * nb + [pl.BlockSpec(memory_space=pl.ANY)],
        out_specs=(_SEM, _SEM, *[_HBM] * nb, pl.BlockSpec(memory_space=pltpu.VMEM)),
        input_output_aliases={k: 2 + k for k in range(nb)},
        compiler_params=pltpu.CompilerParams(has_side_effects=_EFFECT),
    )(*[pltpu.with_memory_space_constraint(b, pltpu.HBM) for b in bufs],
      after[0] if isinstance(after, (list, tuple)) else after)
    return res[0], res[1], list(res[2:2 + nb]), res[2 + nb]


def _split_wait(name, bufs, sems, make_copies, after):
    nb = len(bufs)
    after = list(after) if isinstance(after, (list, tuple)) else [after]

    def body(*refs):
        send_sems, recv_sems = refs[nb], refs[nb + 1]
        sends, recvs = make_copies(refs[:nb], send_sems, recv_sems)
        for cp in sends:
            cp.wait_send()
        for cp in recvs:
            cp.wait_recv()

    res = pl.pallas_call(
        body, name=name,
        out_shape=tuple(pltpu.HBM(b.shape, b.dtype) for b in bufs),
        in_specs=[_HBM] * nb + [_SEM, _SEM] + [pl.BlockSpec(memory_space=pl.ANY)] * len(after),
        out_specs=tuple([_HBM] * nb),
        input_output_aliases={k: k for k in range(nb)},
        compiler_params=pltpu.CompilerParams(has_side_effects=_EFFECT),
    )(*bufs, sems[0], sems[1], *after)
    return list(res)


def _remote(src, dst, send_sems, recv_sems, k, target):
    return pltpu.make_async_remote_copy(src_ref=src, dst_ref=dst, send_sem=send_sems.at[k],
                                        recv_sem=recv_sems.at[k], device_id=target, device_id_type=MESH)


def _ag_region(ref, axis, chip, half):
    K, N = ref.shape
    if axis == 0:
        hs = K // N_CHIPS // 2
        assert hs % 16 == 0
        return ref.at[pl.ds(pl.multiple_of((2 * chip + half) * hs, 16), hs), :]
    ns, hk = N // N_CHIPS, K // 2
    assert ns % LANES == 0 and hk % 16 == 0
    return ref.at[pl.ds(pl.multiple_of(half * hk, 16), hk), pl.ds(pl.multiple_of(chip * ns, LANES), ns)]


def _ag_copies(stage, axes):
    n = len(axes)

    def make(bufs, send_sems, recv_sems):
        x, y, c = _my_place()
        me = 2 * x + y
        sends, recvs = [], []
        for j, (px, py) in enumerate(_peer_chips(x, y)):
            other = 2 * px + py
            for w in range(n):
                k = j * n + w
                if stage == 1:
                    src, target = _ag_region(bufs[w], axes[w], me, c), (px, py, c)
                    land = _ag_region(bufs[w], axes[w], other, c)
                else:
                    src, target = _ag_region(bufs[w], axes[w], other, c), (x, y, 1 - c)
                    land = _ag_region(bufs[w], axes[w], other, 1 - c)
                sends.append(_remote(src, src, send_sems, recv_sems, k, target))
                recvs.append(_remote(land, land, send_sems, recv_sems, k, target))
        return sends, recvs

    return make


def _half_shape(shape, axis):
    K, N = shape
    return (K, N // 2) if axis == 0 else (K // 2, N)


def _core_half(ref, axis, half):
    K, N = ref.shape
    if axis == 0:
        return ref.at[:, pl.ds(pl.multiple_of(half * (N // 2), LANES), N // 2)]
    return ref.at[pl.ds(pl.multiple_of(half * (K // 2), 16), K // 2), :]


def _chip_block(ref, axis, chip):
    K, N = ref.shape
    if axis == 0:
        return ref.at[pl.ds(pl.multiple_of(chip * (K // N_CHIPS), 16), K // N_CHIPS), :]
    return ref.at[:, pl.ds(pl.multiple_of(chip * (N // N_CHIPS), LANES), N // N_CHIPS)]


def _rs_sibling_copies(axes):
    n = len(axes)

    def make(bufs, send_sems, recv_sems):
        x, y, c = _my_place()
        sends = [_remote(_core_half(bufs[w], axes[w], 1 - c), bufs[n + w], send_sems, recv_sems, w, (x, y, 1 - c))
                 for w in range(n)]
        recvs = [_remote(bufs[n + w], bufs[n + w], send_sems, recv_sems, w, (x, y, 1 - c)) for w in range(n)]
        return sends, recvs

    return make


def _rs_chip_copies(axes):
    n = len(axes)

    def make(bufs, send_sems, recv_sems):
        x, y, c = _my_place()
        sends, recvs = [], []
        for j, (px, py) in enumerate(_peer_chips(x, y)):
            for w in range(n):
                k = j * n + w
                sends.append(_remote(_chip_block(bufs[w], axes[w], 2 * px + py), bufs[n + w].at[j],
                                     send_sems, recv_sems, k, (px, py, c)))
                recvs.append(_remote(bufs[n + w].at[j], bufs[n + w].at[j], send_sems, recv_sems, k, (px, py, c)))
        return sends, recvs

    return make


def _rs_fill_copies(axes):
    n = len(axes)

    def make(bufs, send_sems, recv_sems):
        x, y, c = _my_place()
        sends = [_remote(_core_half(bufs[w], axes[w], c), _core_half(bufs[w], axes[w], c),
                         send_sems, recv_sems, w, (x, y, 1 - c)) for w in range(n)]
        recvs = [_remote(_core_half(bufs[w], axes[w], 1 - c), _core_half(bufs[w], axes[w], 1 - c),
                         send_sems, recv_sems, w, (x, y, 1 - c)) for w in range(n)]
        return sends, recvs

    return make


def _chip_sum(g, r, axis, place, *, name):
    hk, hn = r.shape
    bk, bn = (hk // N_CHIPS, hn) if axis == 0 else (hk, hn // N_CHIPS)
    tk = _pick(bk, (512, 352, 256, 128))
    nk = bk // tk

    def body(p_ref, g_ref, r_ref, b_ref, own_ref):
        s = g_ref[...].astype(F32) + r_ref[...].astype(F32)
        b_ref[...] = s.astype(BF16)

        @pl.when(pl.program_id(1) == p_ref[0])
        def _():
            own_ref[...] = s

    if axis == 0:
        g_spec = pl.BlockSpec((tk, bn), lambda i, j, p: (j * nk + i, p[1]))
        r_spec = pl.BlockSpec((tk, bn), lambda i, j, p: (j * nk + i, 0))
    else:
        g_spec = pl.BlockSpec((tk, bn), lambda i, j, p: (p[1] * nk + i, j))
        r_spec = pl.BlockSpec((tk, bn), lambda i, j, p: (i, j))
    grid_spec = pltpu.PrefetchScalarGridSpec(
        num_scalar_prefetch=1, grid=(nk, N_CHIPS), in_specs=[g_spec, r_spec],
        out_specs=[r_spec, pl.BlockSpec((tk, bn), lambda i, j, p: (i, 0))])
    return pl.pallas_call(
        body, name=name,
        out_shape=[jax.ShapeDtypeStruct(r.shape, BF16), jax.ShapeDtypeStruct((bk, bn), F32)],
        grid_spec=grid_spec,
        compiler_params=pltpu.CompilerParams(dimension_semantics=("arbitrary", "arbitrary"),
                                             vmem_limit_bytes=_vmem(2 * tk * bn * 10 + 3 * tk * bn * 4)),
    )(place, g, r)


def _final_sum(own, recv, axis, place, *, name):
    _, bk, bn = recv.shape
    tk = _pick(bk, (256, 176, 128))
    nk = bk // tk

    def body(p_ref, o_ref, r_ref, out_ref):
        out_ref[...] = ((o_ref[...] + r_ref[0].astype(F32)) + r_ref[1].astype(F32)) + r_ref[2].astype(F32)

    own_spec = pl.BlockSpec((tk, bn), lambda i, p: (i, 0))
    if axis == 0:
        out_shape, out_spec = (bk, 2 * bn), pl.BlockSpec((tk, bn), lambda i, p: (i, p[1]))
    else:
        out_shape, out_spec = (2 * bk, bn), pl.BlockSpec((tk, bn), lambda i, p: (p[1] * nk + i, 0))
    grid_spec = pltpu.PrefetchScalarGridSpec(
        num_scalar_prefetch=1, grid=(nk,),
        in_specs=[own_spec, pl.BlockSpec((3, tk, bn), lambda i, p: (0, i, 0))], out_specs=out_spec)
    return pl.pallas_call(
        body, name=name, out_shape=jax.ShapeDtypeStruct(out_shape, F32), grid_spec=grid_spec,
        compiler_params=pltpu.CompilerParams(dimension_semantics=("arbitrary",),
                                             vmem_limit_bytes=_vmem(2 * tk * bn * 14 + 4 * tk * bn * 4)),
    )(place, own, recv)


def _allreduce_small(p, after=()):
    n_after = len(after)

    def body(*refs):
        p_ref = refs[0]
        o_ref, r0, r1, r2, send_sems, recv_sems = refs[1 + n_after:]
        x, y, c = _my_place()
        o_ref[...] = p_ref[...]
        for s, (peer, rbuf) in enumerate([((x, y, 1 - c), r0), ((1 - x, y, c), r1), ((x, 1 - y, c), r2)]):
            cp = pltpu.make_async_remote_copy(src_ref=o_ref, dst_ref=rbuf, send_sem=send_sems.at[s],
                                              recv_sem=recv_sems.at[s], device_id=peer, device_id_type=MESH)
            cp.start()
            cp.wait()
            o_ref[...] = o_ref[...] + rbuf[...]

    vm = pl.BlockSpec(memory_space=pltpu.VMEM)
    return pl.pallas_call(
        body, name="allreduce_small", out_shape=jax.ShapeDtypeStruct(p.shape, F32),
        in_specs=[vm] + [pl.BlockSpec(memory_space=pl.ANY)] * n_after, out_specs=vm,
        scratch_shapes=[pltpu.VMEM(p.shape, F32)] * 3 + [pltpu.SemaphoreType.DMA((3,))] * 2,
        compiler_params=pltpu.CompilerParams(vmem_limit_bytes=_vmem(6 * _nbytes(p.shape, F32))),
    )(p, *after)


_BUTTERFLY = (lambda x, y, c: (x, y, 1 - c), lambda x, y, c: (1 - x, y, c), lambda x, y, c: (x, 1 - y, c))


def _swap_copies(stage):
    def make(bufs, send_sems, recv_sems):
        target = _BUTTERFLY[stage](*_my_place())
        return ([_remote(bufs[0], bufs[1], send_sems, recv_sems, 0, target)],
                [_remote(bufs[1], bufs[1], send_sems, recv_sems, 0, target)])

    return make


def _add(a, b, *, name):
    def body(a_ref, b_ref, o_ref):
        o_ref[...] = a_ref[...] + b_ref[...]

    return pl.pallas_call(body, name=name, out_shape=jax.ShapeDtypeStruct(a.shape, a.dtype))(a, b)


def _pack_rows(parts):
    rows, metas = [], []
    for a in parts:
        flat = a.reshape(-1)
        nrow = -(-flat.shape[0] // LANES)
        nrow = -(-nrow // 8) * 8
        flat = jnp.pad(flat, (0, nrow * LANES - flat.shape[0]))
        rows.append(flat.reshape(nrow, LANES))
        metas.append((a.shape, nrow))
    return jnp.concatenate(rows, axis=0), metas


def _unpack_rows(packed, metas):
    out, r0 = [], 0
    for shape, nrow in metas:
        size = int(np.prod(shape))
        out.append(packed[r0:r0 + nrow].reshape(-1)[:size].reshape(shape))
        r0 += nrow
    return out


def kernel(x, positions, pre_mix_g, post_mix_g, pre_ffn_g, post_ffn_g, a_w_in, a_b_in, a_ln_g, a_ln_b, a_w_s, a_b_s, a_w_out, b_w_qkv, b_b_qkv, b_sinks, b_w_o, ffn_w_gu, ffn_w_down, loss_target, m_pre_mix_g, m_post_mix_g, m_pre_ffn_g, m_post_ffn_g, m_a_w_in, m_a_b_in, m_a_ln_g, m_a_ln_b, m_a_w_s, m_a_b_s, m_a_w_out, m_b_w_qkv, m_b_b_qkv, m_b_sinks, m_b_w_o, m_ffn_w_gu, m_ffn_w_down, v_pre_mix_g, v_post_mix_g, v_pre_ffn_g, v_post_ffn_g, v_a_w_in, v_a_b_in, v_a_ln_g, v_a_ln_b, v_a_w_s, v_a_b_s, v_a_w_out, v_b_w_qkv, v_b_b_qkv, v_b_sinks, v_b_w_o, v_ffn_w_gu, v_ffn_w_down):
    depth, D = pre_mix_g.shape
    xi, yi, ci = _my_place()
    chip = 2 * xi + yi
    place = jnp.stack([chip, ci]).astype(jnp.int32)

    stacked = {"a_w_in": (a_w_in, m_a_w_in, v_a_w_in), "a_w_out": (a_w_out, m_a_w_out, v_a_w_out),
               "b_w_qkv": (b_w_qkv, m_b_w_qkv, v_b_w_qkv), "b_w_o": (b_w_o, m_b_w_o, v_b_w_o),
               "ffn_w_gu": (ffn_w_gu, m_ffn_w_gu, v_ffn_w_gu), "ffn_w_down": (ffn_w_down, m_ffn_w_down, v_ffn_w_down)}
    cut = {"a_w_in": 1, "a_w_out": 0, "b_w_qkv": 1, "b_w_o": 0, "ffn_w_gu": 1, "ffn_w_down": 0}

    def layer_keys(i):
        mix = [("a_w_in", i // 2), ("a_w_out", i // 2)] if i % 2 == 0 else [("b_w_qkv", i // 2), ("b_w_o", i // 2)]
        return mix + [("ffn_w_gu", i), ("ffn_w_down", i)]

    def dep(a, toks):
        for t in toks:
            a = a + t[:1, :1]
        return a

    W = {}
    for i in range(depth):
        for nm, l in layer_keys(i):
            W[(nm, l)] = _cast_block(stacked[nm][0], l, cut[nm], place, name=f"cast_{nm}_{l}")

    def gather(tag, keys, after):
        axes = [cut[nm] for nm, _ in keys]
        for stage in (1, 2):
            ss, rs, bufs, tok = _split_start(f"ag{stage}_start_{tag}", [W[k] for k in keys], 3 * len(keys),
                                             _ag_copies(stage, axes), after)
            after = yield tok
            bufs = _split_wait(f"ag{stage}_wait_{tag}", bufs, (ss, rs), _ag_copies(stage, axes), after)
            W.update(zip(keys, bufs))
        yield None

    nq = b_b_qkv.shape[1]
    bq_full = jnp.zeros((b_b_qkv.shape[0], N_CHIPS * nq), F32)
    bq_full = lax.dynamic_update_slice(bq_full, jnp.where(ci == 0, b_b_qkv, 0.0), (0, chip * nq))
    bq_packed, bq_meta = _pack_rows([bq_full])
    bq_gathered = _allreduce_small(bq_packed)
    b_qkv_full = _unpack_rows(bq_gathered, bq_meta)[0]

    first = gather("0m", layer_keys(0)[:2], bq_gathered)
    tok = next(first)
    tok = first.send([tok] + [W[k] for i in range(depth) for k in layer_keys(i)[2 if i == 0 else 0:]])
    first.send(tok)

    h = x[0]
    target = loss_target[0]
    ctab, stab = _rope_tables(positions[0])
    q_width = W[("b_w_o", 0)].shape[0]
    kv_width = N_KV_HEADS * HEAD_DIM
    row = lambda a, i: a[i:i + 1]
    gains = {"pre_mix": pre_mix_g[:, None], "post_mix": post_mix_g[:, None], "pre_ffn": pre_ffn_g[:, None],
             "post_ffn": post_ffn_g[:, None]}
    gain = lambda which, i: (gains[which], i)

    saved = []
    hn = None
    for i in range(depth):
        j = i // 2
        s = {"h": h}
        ffn_w = None
        if i == 0:
            ffn_w = gather("0f", layer_keys(0)[2:], W[("a_w_out", 0)])
            toks = [next(ffn_w)]
            nxt = gather("1", layer_keys(1), toks[0])
            toks.append(next(nxt))
            hn = _rms_fwd(h, gain("pre_mix", i), out_dtype=BF16, after=toks, name=f"rms_pre_mix_{i}")
        elif i + 1 < depth:
            nxt = gather(str(i + 1), layer_keys(i + 1), h)
            toks = [next(nxt)]
        else:
            toks = []
        s["hn"] = hn
        if i % 2 == 0:
            pre = _matmul(hn, W[("a_w_in", j)], mode="nn", bias=row(a_b_in, j), out_dtype=F32, after=toks,
                          name=f"gmlp_in_{i}")
            gated = _sgu_fwd(pre, row(a_ln_g, j), row(a_ln_b, j), a_w_s[j], a_b_s[j].T, name=f"sgu_fwd_{i}")
            mix = _matmul(gated, W[("a_w_out", j)], mode="nn", out_dtype=F32, name=f"gmlp_out_{i}")
            s.update(pre=pre, gated=gated)
        else:
            qkv = _matmul(hn, W[("b_w_qkv", j)], mode="nn", bias=row(b_qkv_full, j), out_dtype=F32, after=toks,
                          name=f"attn_qkv_{i}")
            qr, kr, vr = _rope_fwd(qkv, ctab, stab, q_width=q_width, kv_width=kv_width, name=f"rope_fwd_{i}")
            o = _attn_fwd(qr, kr, vr, row(b_sinks, j), name=f"attn_fwd_{i}")
            mix = _matmul(o, W[("b_w_o", j)], mode="nn", out_dtype=F32, name=f"attn_o_{i}")
            s.update(qr=qr, kr=kr, vr=vr, o=o)
        s["mix"] = mix
        toks = [ffn_w.send(mix)] if ffn_w else []
        h1, fn = _rms_res_norm(h, mix, gain("post_mix", i), gain("pre_ffn", i), after=toks, name=f"rms_post_mix_{i}")
        if ffn_w:
            ffn_w.send(h1)
        s["h1"] = h1
        g_pre, u_pre, act = _ffn_up(fn, W[("ffn_w_gu", i)][None], 0, name=f"ffn_up_{i}")
        f = _matmul(act, W[("ffn_w_down", i)], mode="nn", out_dtype=F32, name=f"ffn_down_{i}")
        if i + 1 < depth:
            toks = [nxt.send(f)]
            h, hn = _rms_res_norm(h1, f, gain("post_ffn", i), gain("pre_mix", i + 1), after=toks,
                                  name=f"rms_post_ffn_{i}")
            nxt.send(h)
        else:
            h = _rms_res(h1, f, gain("post_ffn", i), name=f"rms_post_ffn_{i}")
        s.update(fn=fn, g_pre=g_pre, u_pre=u_pre, act=act, f=f)
        saved.append(s)

    dh, df, loss_part, g_last = _loss_and_grad(h, target, saved[-1]["f"], gain("post_ffn", depth - 1), name="loss")

    big_out = {nm: tuple(lax.empty(w.shape, F32) for _ in range(4)) for nm, (w, _, _) in stacked.items()}

    def reduce_group(i, keys, grads):
        axes = [cut[nm] for nm, _ in keys]
        n = len(keys)
        lands = [lax.empty(_half_shape(g.shape, ax), BF16) for g, ax in zip(grads, axes)]
        ss, rs, bufs, tok = _split_start(f"rs_sibling_start_{i}", list(grads) + lands, n, _rs_sibling_copies(axes),
                                         place)
        after = yield tok
        bufs = _split_wait(f"rs_sibling_wait_{i}", bufs, (ss, rs), _rs_sibling_copies(axes), after)
        sums = [_chip_sum(bufs[w], bufs[n + w], axes[w], place, name=f"chip_sum_{keys[w][0]}_{keys[w][1]}")
                for w in range(n)]
        lands = [lax.empty((3,) + own.shape, BF16) for _, own in sums]
        ss, rs, bufs, tok = _split_start(f"rs_chip_start_{i}", [sb for sb, _ in sums] + lands, 3 * n,
                                         _rs_chip_copies(axes), place)
        after = yield tok
        bufs = _split_wait(f"rs_chip_wait_{i}", bufs, (ss, rs), _rs_chip_copies(axes), after)
        blocks = [_final_sum(sums[w][1], bufs[n + w], axes[w], place, name=f"final_sum_{keys[w][0]}_{keys[w][1]}")
                  for w in range(n)]
        ss, rs, bufs, tok = _split_start(f"rs_fill_start_{i}", blocks, n, _rs_fill_copies(axes), place)
        after = yield tok
        blocks = _split_wait(f"rs_fill_wait_{i}", bufs, (ss, rs), _rs_fill_copies(axes), after)
        updates.extend(zip(keys, blocks))
        yield None

    updates = []

    def adamw(items, after):
        done = []
        for (nm, l), g in items:
            w, m, v = stacked[nm]
            big_out[nm] = tuple(_adamw_layer(w, m, v, g, l, big_out[nm], after=after, name=f"adamw_{nm}_{l}"))
            done.append(big_out[nm][1])
        return done

    reducing = []

    def advance(after, newest_only=False):
        toks = []
        for gen in (reducing[-1:] if newest_only else list(reducing)):
            tok = gen.send(after)
            if tok is None:
                reducing.remove(gen)
            else:
                toks.append(tok)
        return toks

    small = {}
    g_pre_mix, g_post_mix, g_pre_ffn, g_post_ffn = [None] * depth, [None] * depth, [None] * depth, [None] * depth
    g_post_ffn[depth - 1] = g_last
    toks = []
    early = []
    for i in reversed(range(depth)):
        j = i // 2
        s = saved[i]
        g_down = _matmul(s["act"], df, mode="tn", out_dtype=BF16, after=toks, name=f"ffn_down_dw_{i}")
        dg_, du_ = _ffn_down_dx(df, W[("ffn_w_down", i)][None], 0, s["g_pre"], s["u_pre"], g_down,
                                name=f"ffn_down_dx_{i}")
        g_gu = _matmul_pair(s["fn"], dg_, du_, mode="tn", out_dtype=BF16, name=f"ffn_gu_dw_{i}")
        dfn = _matmul_pair(dg_, W[("ffn_w_gu", i)], du_, mode="nt", out_dtype=F32, after=[g_gu],
                           name=f"ffn_gu_dx_{i}")
        toks = advance(dfn)
        if i == 0:
            gen = reduce_group("0f", layer_keys(0)[2:], [g_gu, g_down])
            toks.append(next(gen))
            reducing.append(gen)
        dh1, dmix, g_pre_ffn[i], g_post_mix[i] = _rms_bwd_chain(
            s["h1"], gain("pre_ffn", i), dfn, dh, s["mix"], gain("post_mix", i), after=toks,
            name=f"rms_ffn_mix_bwd_{i}")
        if i % 2 == 0:
            g_out = _matmul(s["gated"], dmix, mode="tn", out_dtype=BF16, name=f"gmlp_out_dw_{i}")
            dgated = _matmul(dmix, W[("a_w_out", j)], mode="nt", out_dtype=BF16, after=[g_out],
                             name=f"gmlp_out_dx_{i}")
            toks = advance(dgated, newest_only=True) if i == 0 else []
            dpre, dws, dbsT, dlng, dlnb, dbin = _sgu_bwd(s["pre"], dgated, dep(row(a_ln_g, j), toks), row(a_ln_b, j),
                                                         a_w_s[j], a_b_s[j].T, name=f"sgu_bwd_{i}")
            small[("a_w_s", j)] = dws
            small[("a_b_s", j)] = dbsT.T
            small[("a_ln_g", j)] = dlng
            small[("a_ln_b", j)] = dlnb
            small[("a_b_in", j)] = dbin
            g_in = _matmul(s["hn"], dpre, mode="tn", out_dtype=BF16, name=f"gmlp_in_dw_{i}")
            if i == 0:
                last = reduce_group("0m", layer_keys(0)[:2], [g_in, g_out])
                early = [next(last)]
            dhn = _matmul(dpre, W[("a_w_in", j)], mode="nt", out_dtype=F32, after=[g_in] + early,
                          name=f"gmlp_in_dx_{i}")
        else:
            g_out = _matmul(s["o"], dmix, mode="tn", out_dtype=BF16, name=f"attn_o_dw_{i}")
            do = _matmul(dmix, W[("b_w_o", j)], mode="nt", out_dtype=BF16, after=[g_out], name=f"attn_o_dx_{i}")
            dq, dkp, dkc, dvp, dvc, dsk = _attn_bwd(s["qr"], s["kr"], s["vr"], row(b_sinks, j), do,
                                                    name=f"attn_bwd_{i}")
            dqkv, dbq = _rope_bwd(dq, dkp, dkc, dvp, dvc, ctab, stab, name=f"rope_bwd_{i}")
            small[("b_sinks", j)] = dsk[:, :b_sinks.shape[1]]
            small[("b_b_qkv", j)] = dbq
            g_in = _matmul(s["hn"], dqkv, mode="tn", out_dtype=BF16, name=f"attn_qkv_dw_{i}")
            if i == 0:
                last = reduce_group("0m", layer_keys(0)[:2], [g_in, g_out])
                early = [next(last)]
            dhn = _matmul(dqkv, W[("b_w_qkv", j)], mode="nt", out_dtype=F32, after=[g_in] + early,
                          name=f"attn_qkv_dx_{i}")
        toks = advance(dhn)
        if i > 0:
            dh, df, g_pre_mix[i], g_post_ffn[i - 1] = _rms_bwd_chain(
                s["h"], gain("pre_mix", i), dhn, dh1, saved[i - 1]["f"], gain("post_ffn", i - 1), after=toks,
                name=f"rms_mix_ffn_bwd_{i}")
            gen = reduce_group(str(i), layer_keys(i), [g_in, g_out, g_gu, g_down])
            toks = [next(gen)] + advance(dh)
            reducing.append(gen)
        else:
            toks.append(last.send(dhn))
            dh, g_pre_mix[i] = _rms_bwd(s["h"], gain("pre_mix", i), dhn, dh1, out_dtype=F32, after=toks,
                                        name=f"rms_pre_mix_bwd_{i}")
            advance(dh)
    grad_x = dh[None]
    assert not reducing

    n_a, n_b = a_b_in.shape[0], b_sinks.shape[0]
    stack = lambda key, n: jnp.concatenate([small[(key, j)] for j in range(n)], axis=0)
    small_parts = [
        jnp.concatenate(g_pre_mix, axis=0), jnp.concatenate(g_post_mix, axis=0),
        jnp.concatenate(g_pre_ffn, axis=0), jnp.concatenate(g_post_ffn, axis=0),
        stack("a_b_in", n_a), stack("a_ln_g", n_a), stack("a_ln_b", n_a),
        jnp.stack([small[("a_w_s", j)] for j in range(n_a)]), jnp.stack([small[("a_b_s", j)] for j in range(n_a)]),
        stack("b_b_qkv", n_b), stack("b_sinks", n_b), loss_part,
    ]
    packed, metas = _pack_rows(small_parts)
    reduced = packed
    for stage in range(len(_BUTTERFLY)):
        ss, rs, bufs, tok = _split_start(f"butterfly_start_{stage}", [reduced, lax.empty(packed.shape, F32)], 1,
                                         _swap_copies(stage), place)
        done = adamw(updates[stage::len(_BUTTERFLY)], [tok])
        bufs = _split_wait(f"butterfly_wait_{stage}", bufs, (ss, rs), _swap_copies(stage), done or [tok])
        reduced = _add(bufs[0], bufs[1], name=f"butterfly_add_{stage}")
    updates = []
    while last.send(reduced) is not None:
        pass
    adamw(updates, [])
    red = _unpack_rows(reduced, metas)
    (gr_pre_mix, gr_post_mix, gr_pre_ffn, gr_post_ffn, gr_b_in, gr_ln_g, gr_ln_b, gr_w_s, gr_b_s,
     gr_b_qkv_full, gr_sinks, loss_sum) = red
    loss = loss_sum[0, 0]
    gr_b_qkv = lax.dynamic_slice(gr_b_qkv_full, (0, chip * nq), (gr_b_qkv_full.shape[0], nq))

    grads = {"pre_mix_g": gr_pre_mix, "post_mix_g": gr_post_mix, "pre_ffn_g": gr_pre_ffn, "post_ffn_g": gr_post_ffn,
             "a_b_in": gr_b_in, "a_ln_g": gr_ln_g, "a_ln_b": gr_ln_b, "a_w_s": gr_w_s, "a_b_s": gr_b_s,
             "b_b_qkv": gr_b_qkv, "b_sinks": gr_sinks}
    weights = {"pre_mix_g": (pre_mix_g, m_pre_mix_g, v_pre_mix_g), "post_mix_g": (post_mix_g, m_post_mix_g, v_post_mix_g),
               "pre_ffn_g": (pre_ffn_g, m_pre_ffn_g, v_pre_ffn_g), "post_ffn_g": (post_ffn_g, m_post_ffn_g, v_post_ffn_g),
               "a_b_in": (a_b_in, m_a_b_in, v_a_b_in), "a_ln_g": (a_ln_g, m_a_ln_g, v_a_ln_g),
               "a_ln_b": (a_ln_b, m_a_ln_b, v_a_ln_b), "a_w_s": (a_w_s, m_a_w_s, v_a_w_s), "a_b_s": (a_b_s, m_a_b_s, v_a_b_s),
               "b_b_qkv": (b_b_qkv, m_b_b_qkv, v_b_b_qkv), "b_sinks": (b_sinks, m_b_sinks, v_b_sinks)}
    order = ["pre_mix_g", "post_mix_g", "pre_ffn_g", "post_ffn_g", "a_w_in", "a_b_in", "a_ln_g", "a_ln_b", "a_w_s",
             "a_b_s", "a_w_out", "b_w_qkv", "b_b_qkv", "b_sinks", "b_w_o", "ffn_w_gu", "ffn_w_down"]
    deltas, new_m, new_v = {}, {}, {}
    small_names = [nm for nm in order if nm not in big_out]
    small_res = _adamw_small([(weights[nm][0], grads[nm], weights[nm][1], weights[nm][2]) for nm in small_names],
                             name="adamw_small")
    for nm, res in zip(small_names, small_res):
        deltas[nm], new_m[nm], new_v[nm] = res
    for nm in big_out:
        grads[nm], deltas[nm], new_m[nm], new_v[nm] = big_out[nm]
    return (loss, grad_x, *[grads[nm] for nm in order], *[deltas[nm] for nm in order],
            *[new_m[nm] for nm in order], *[new_v[nm] for nm in order])
```

```python
import functools
import math

import jax
import jax.numpy as jnp
import numpy as np
from jax import lax
from jax.experimental import pallas as pl
from jax.experimental.pallas import tpu as pltpu

F32 = jnp.float32
BF16 = jnp.bfloat16
MESH = pl.DeviceIdType.MESH

HEAD_DIM = 64
N_KV_HEADS = 4
ROPE_DIM = 16
ROPE_THETA = 500000.0
CHUNK = 128
GMLP_GROUPS = 8
RMS_EPS = 1e-6
LN_EPS = 1e-5
NEG_INF = -1e30
ADAM_LR = 0.001
ADAM_B1 = 0.9
ADAM_B2 = 0.999
ADAM_EPS = 1e-08
ADAM_WD = 0.01
ADAM_STEP = 10

N_CHIPS = 4
LANES = 128
VMEM_CAP = 58 * 1024 * 1024


def _vmem(est_bytes):
    assert est_bytes < VMEM_CAP
    return VMEM_CAP


def _pick(n, cands):
    for c in cands:
        if c <= n and n % c == 0:
            return c
    return n


def _nbytes(shape, dtype):
    return int(np.prod(shape)) * jnp.dtype(dtype).itemsize


MATMUL_VMEM_BUDGET = 48 * 1024 * 1024
MXU_COLS = 256


def _halvings(n, unit):
    out, t = [], n
    while t % unit == 0 and t >= unit:
        out.append(t)
        if t % 2:
            break
        t //= 2
    return out


def _matmul_tiles(P, Q, R, a_bytes, b_bytes, o_bytes, full_addend, tp, tq, tr, repeat=1):
    step_us, bytes_per_us, flops_per_us = 0.85, 3.2e6, 9.0e8
    best = None
    for p in ([tp] if tp else _halvings(P, LANES)):
        for q in ([tq] if tq else _halvings(Q, LANES)):
            for r in ([tr] if tr else _halvings(R, LANES)):
                nk = R // r
                vm = 2 * (p * r * a_bytes + r * q * b_bytes + p * q * o_bytes + (p * q * 4 if full_addend else 0))
                vm += p * q * 4 * (2 if nk > 1 else 1)
                if vm > MATMUL_VMEM_BUDGET:
                    continue
                exposed = (p * r * a_bytes + r * q * b_bytes + p * q * o_bytes) / bytes_per_us
                mxu_us = repeat * 2.0 * P * R * (Q // q) * (-(-q // MXU_COLS) * MXU_COLS) / flops_per_us
                key = (repeat * (P // p) * (Q // q) * nk * step_us + exposed + mxu_us, nk, abs(p - q))
                if best is None or key < best[0]:
                    best = (key, (p, q, r))
    assert best is not None, (P, Q, R)
    return best[1]


def _matmul(a, b, *, mode, out_dtype, name, bias=None, after=()):
    if mode == "nn":
        (P, R), (R2, Q) = a.shape, b.shape
    elif mode == "nt":
        (P, R), (Q, R2) = a.shape, b.shape
    else:
        (R, P), (R2, Q) = a.shape, b.shape
    assert R == R2, (mode, a.shape, b.shape)
    tp, tq, tr = _matmul_tiles(P, Q, R, a.dtype.itemsize, b.dtype.itemsize, jnp.dtype(out_dtype).itemsize, False,
                               None, None, None)
    nk = R // tr
    dims = {"nn": (((1,), (0,)), ((), ())), "nt": (((1,), (1,)), ((), ())), "tn": (((0,), (0,)), ((), ()))}[mode]
    if mode == "nn":
        a_spec = pl.BlockSpec((tp, tr), lambda i, j, k: (i, k))
        b_spec = pl.BlockSpec((tr, tq), lambda i, j, k: (k, j))
    elif mode == "nt":
        a_spec = pl.BlockSpec((tp, tr), lambda i, j, k: (i, k))
        b_spec = pl.BlockSpec((tq, tr), lambda i, j, k: (j, k))
    else:
        a_spec = pl.BlockSpec((tr, tp), lambda i, j, k: (k, i))
        b_spec = pl.BlockSpec((tr, tq), lambda i, j, k: (k, j))
    in_specs = [a_spec, b_spec]
    args = [a, b]
    has_bias = bias is not None
    if has_bias:
        in_specs.append(pl.BlockSpec((1, tq), lambda i, j, k: (0, j)))
        args.append(bias)
    out_shape = jax.ShapeDtypeStruct((P, Q), out_dtype)
    out_spec = pl.BlockSpec((tp, tq), lambda i, j, k: (i, j))
    n_in = len(args) + len(after)
    in_specs += [pl.BlockSpec(memory_space=pl.ANY)] * len(after)
    args += list(after)

    def body(*refs):
        a_ref, b_ref = refs[0], refs[1]
        bias_ref = refs[2] if has_bias else None
        o_ref = refs[n_in]
        acc_ref = refs[n_in + 1] if nk > 1 else None
        part = lax.dot_general(a_ref[...], b_ref[...], dims, preferred_element_type=F32)

        def finish(acc):
            if has_bias:
                acc = acc + bias_ref[...]
            o_ref[...] = acc.astype(out_dtype)

        if nk == 1:
            finish(part)
        else:
            k = pl.program_id(2)

            @pl.when(k == 0)
            def _():
                acc_ref[...] = part

            @pl.when(k > 0)
            def _():
                acc_ref[...] += part

            @pl.when(k == nk - 1)
            def _():
                finish(acc_ref[...])

    est = 2 * (_nbytes((tp, tr), a.dtype) + _nbytes((tr, tq), b.dtype) + _nbytes((tp, tq), out_dtype)) + 3 * tp * tq * 4
    return pl.pallas_call(
        body, name=name, out_shape=out_shape,
        grid=(P // tp, Q // tq, nk),
        in_specs=in_specs, out_specs=out_spec,
        scratch_shapes=[pltpu.VMEM((tp, tq), F32)] if nk > 1 else [],
        compiler_params=pltpu.CompilerParams(
            dimension_semantics=("parallel", "parallel", "arbitrary"), vmem_limit_bytes=_vmem(est)),
    )(*args)


def _matmul_pair(a, b, pair, *, mode, out_dtype, name, after=()):
    if mode == "tn":
        (R, P), (R2, Q) = a.shape, b.shape
        assert R == R2 and pair.shape == b.shape
        tp, tq, tr = _matmul_tiles(P, Q, R, a.dtype.itemsize, 2 * b.dtype.itemsize,
                                   jnp.dtype(out_dtype).itemsize, False, None, None, None, repeat=2)
        nq, nk = Q // tq, R // tr
        grid, nk_total = (P // tp, 2 * nq, nk), nk
        a_spec = pl.BlockSpec((tr, tp), lambda i, j, k: (k, i))
        b_spec = pl.BlockSpec((tr, tq), lambda i, j, k: (jnp.where(j < nq, k, nk - 1), jnp.minimum(j, nq - 1)))
        p_spec = pl.BlockSpec((tr, tq), lambda i, j, k: (jnp.where(j >= nq, k, 0), jnp.maximum(j - nq, 0)))
        out_shape = (P, 2 * Q)
        dims = (((0,), (0,)), ((), ()))
    else:
        assert mode == "nt"
        (P, R), (Q, R2) = a.shape, b.shape
        assert R2 == 2 * R and pair.shape == a.shape
        tp, tq, tr = _matmul_tiles(P, Q, R, 2 * a.dtype.itemsize, b.dtype.itemsize,
                                   jnp.dtype(out_dtype).itemsize, False, None, None, None, repeat=2)
        nk = R // tr
        grid, nk_total = (P // tp, Q // tq, 2 * nk), 2 * nk
        a_spec = pl.BlockSpec((tp, tr), lambda i, j, k: (i, jnp.minimum(k, nk - 1)))
        p_spec = pl.BlockSpec((tp, tr), lambda i, j, k: (i, jnp.maximum(k - nk, 0)))
        b_spec = pl.BlockSpec((tq, tr), lambda i, j, k: (j, k))
        out_shape = (P, Q)
        dims = (((1,), (1,)), ((), ()))
    n_after = len(after)

    def body(a_ref, b_ref, p_ref, *rest):
        o_ref = rest[n_after]
        acc_ref = rest[n_after + 1] if nk_total > 1 else None
        j, k = pl.program_id(1), pl.program_id(2)

        def step(l_ref, r_ref):
            part = lax.dot_general(l_ref[...], r_ref[...], dims, preferred_element_type=F32)
            if nk_total == 1:
                o_ref[...] = part.astype(out_dtype)
                return

            @pl.when(k == 0)
            def _():
                acc_ref[...] = part

            @pl.when(k > 0)
            def _():
                acc_ref[...] += part

            @pl.when(k == nk_total - 1)
            def _():
                o_ref[...] = acc_ref[...].astype(out_dtype)

        first = (j < nq) if mode == "tn" else (k < nk)

        @pl.when(first)
        def _():
            step(a_ref, b_ref)

        @pl.when(jnp.logical_not(first))
        def _():
            step(a_ref if mode == "tn" else p_ref, p_ref if mode == "tn" else b_ref)

    n_a, n_b = (1, 2) if mode == "tn" else (2, 1)
    est = (2 * (n_a * _nbytes((tp, tr), a.dtype) + n_b * _nbytes((tr, tq), b.dtype) + _nbytes((tp, tq), out_dtype))
           + 2 * tp * tq * 4)
    return pl.pallas_call(
        body, name=name, out_shape=jax.ShapeDtypeStruct(out_shape, out_dtype), grid=grid,
        in_specs=[a_spec, b_spec, p_spec] + [pl.BlockSpec(memory_space=pl.ANY)] * n_after,
        out_specs=pl.BlockSpec((tp, tq), lambda i, j, k: (i, j)),
        scratch_shapes=[pltpu.VMEM((tp, tq), F32)] if nk_total > 1 else [],
        compiler_params=pltpu.CompilerParams(
            dimension_semantics=("parallel", "parallel", "arbitrary"), vmem_limit_bytes=_vmem(est)),
    )(a, b, pair, *after)


def _row_call(body, ins, outs, *, name, rows, tr, acc_outs=(), est=0, after=()):
    in_specs, args = [], []
    for arr, kind in ins:
        if kind == "row":
            in_specs.append(pl.BlockSpec((tr, arr.shape[1]), lambda i: (i, 0)))
        elif isinstance(arr, tuple):
            arr, layer = arr
            in_specs.append(pl.BlockSpec((None,) + arr.shape[1:], lambda i, layer=layer: (layer, 0, 0)))
        else:
            nd = arr.ndim
            in_specs.append(pl.BlockSpec(arr.shape, lambda i, nd=nd: (0,) * nd))
        args.append(arr)
    n_ins = len(args)
    in_specs += [pl.BlockSpec(memory_space=pl.ANY)] * len(after)
    args += list(after)

    def kernel_fn(*refs):
        body(*refs[:n_ins], *refs[n_ins + len(after):])

    out_shapes = [jax.ShapeDtypeStruct(s, d) for s, d in outs] + [jax.ShapeDtypeStruct(s, d) for s, d in acc_outs]
    out_specs = [pl.BlockSpec((tr, s[1]), lambda i: (i, 0)) for s, _ in outs]
    out_specs += [pl.BlockSpec(s, lambda i, nd=len(s): (0,) * nd) for s, _ in acc_outs]
    res = pl.pallas_call(
        kernel_fn, name=name, out_shape=out_shapes, grid=(rows // tr,), in_specs=in_specs, out_specs=out_specs,
        compiler_params=pltpu.CompilerParams(dimension_semantics=("arbitrary",), vmem_limit_bytes=_vmem(est)),
    )(*args)
    return res


def _rms_fwd(x, g, *, out_dtype, name, after=()):
    T, D = x.shape
    tr = _pick(T, (512, 256, 128))

    def body(x_ref, g_ref, o_ref):
        xv = x_ref[...]
        r = lax.rsqrt(jnp.mean(xv * xv, axis=-1, keepdims=True) + RMS_EPS)
        o_ref[...] = (xv * r * g_ref[...]).astype(out_dtype)

    return _row_call(body, [(x, "row"), (g, "full")], [((T, D), out_dtype)], name=name, rows=T, tr=tr,
                     est=8 * tr * D * 4, after=after)[0]


def _rms_res(h, y, g, *, name):
    T, D = h.shape
    tr = _pick(T, (512, 256, 128))

    def body(h_ref, y_ref, g_ref, o_ref):
        yv = y_ref[...]
        r = lax.rsqrt(jnp.mean(yv * yv, axis=-1, keepdims=True) + RMS_EPS)
        o_ref[...] = h_ref[...] + yv * r * g_ref[...]

    return _row_call(body, [(h, "row"), (y, "row"), (g, "full")], [((T, D), F32)], name=name, rows=T, tr=tr,
                     est=10 * tr * D * 4)[0]


def _rms_bwd(x, g, dy, dres, *, out_dtype, name, after=()):
    T, D = x.shape
    tr = _pick(T, (512, 256, 128))
    has_res = dres is not None

    def body(*refs):
        if has_res:
            x_ref, g_ref, dy_ref, dr_ref, dx_ref, dg_ref = refs
        else:
            x_ref, g_ref, dy_ref, dx_ref, dg_ref = refs
        xv = x_ref[...]
        r = lax.rsqrt(jnp.mean(xv * xv, axis=-1, keepdims=True) + RMS_EPS)
        xhat = xv * r
        dyv = dy_ref[...].astype(F32)
        dxn = dyv * g_ref[...]
        dx = r * (dxn - xhat * jnp.mean(dxn * xhat, axis=-1, keepdims=True))
        if has_res:
            dx = dx + dr_ref[...]
        dx_ref[...] = dx.astype(out_dtype)
        part = jnp.sum(dyv * xhat, axis=0, keepdims=True)

        @pl.when(pl.program_id(0) == 0)
        def _():
            dg_ref[...] = part

        @pl.when(pl.program_id(0) > 0)
        def _():
            dg_ref[...] += part

    ins = [(x, "row"), (g, "full"), (dy, "row")] + ([(dres, "row")] if has_res else [])
    dx, dg = _row_call(body, ins, [((T, D), out_dtype)], name=name, rows=T, tr=tr, acc_outs=[((1, D), F32)],
                       est=12 * tr * D * 4, after=after)
    return dx, dg


def _rms_res_norm(h, y, g_res, g_next, *, name, after=()):
    T, D = h.shape
    tr = _pick(T, (512, 256, 128))

    def body(h_ref, y_ref, g_ref, gn_ref, o_ref, n_ref):
        yv = y_ref[...]
        r = lax.rsqrt(jnp.mean(yv * yv, axis=-1, keepdims=True) + RMS_EPS)
        h2 = h_ref[...] + yv * r * g_ref[...]
        o_ref[...] = h2
        r2 = lax.rsqrt(jnp.mean(h2 * h2, axis=-1, keepdims=True) + RMS_EPS)
        n_ref[...] = (h2 * r2 * gn_ref[...]).astype(BF16)

    return _row_call(body, [(h, "row"), (y, "row"), (g_res, "full"), (g_next, "full")],
                     [((T, D), F32), ((T, D), BF16)], name=name, rows=T, tr=tr, est=12 * tr * D * 4, after=after)


def _rms_bwd_chain(x1, g1, dy1, dres, x2, g2, *, name, after=()):
    T, D = x1.shape
    tr = _pick(T, (512, 256, 128))

    def one(xv, gv, dyv):
        r = lax.rsqrt(jnp.mean(xv * xv, axis=-1, keepdims=True) + RMS_EPS)
        xhat = xv * r
        dxn = dyv * gv
        dx = r * (dxn - xhat * jnp.mean(dxn * xhat, axis=-1, keepdims=True))
        return dx, jnp.sum(dyv * xhat, axis=0, keepdims=True)

    def body(x1_ref, g1_ref, dy1_ref, dr_ref, x2_ref, g2_ref, d1_ref, d2_ref, dg1_ref, dg2_ref):
        dx1, p1 = one(x1_ref[...], g1_ref[...], dy1_ref[...].astype(F32))
        d1 = dx1 + dr_ref[...]
        d1_ref[...] = d1
        dx2, p2 = one(x2_ref[...], g2_ref[...], d1)
        d2_ref[...] = dx2.astype(BF16)

        @pl.when(pl.program_id(0) == 0)
        def _():
            dg1_ref[...] = p1
            dg2_ref[...] = p2

        @pl.when(pl.program_id(0) > 0)
        def _():
            dg1_ref[...] += p1
            dg2_ref[...] += p2

    ins = [(x1, "row"), (g1, "full"), (dy1, "row"), (dres, "row"), (x2, "row"), (g2, "full")]
    return _row_call(body, ins, [((T, D), F32), ((T, D), BF16)], name=name, rows=T, tr=tr,
                     acc_outs=[((1, D), F32), ((1, D), F32)], est=20 * tr * D * 4, after=after)


def _ffn_up(fn, w_gu, l, *, name):
    T, D = fn.shape
    H = w_gu.shape[2] // 2
    tp = _pick(T, (256, 128))
    tq = H
    nj = H // tq

    def body(a_ref, wg_ref, wu_ref, g_ref, u_ref, act_ref):
        a = a_ref[...]
        g = jnp.dot(a, wg_ref[...], preferred_element_type=F32)
        u = jnp.dot(a, wu_ref[...], preferred_element_type=F32)
        sg = jax.nn.sigmoid(g)
        silu = g * sg
        g_ref[...] = (u * (sg + silu * (1.0 - sg))).astype(BF16)
        u_ref[...] = silu.astype(BF16)
        act_ref[...] = (silu * u).astype(BF16)

    tile = pl.BlockSpec((tp, tq), lambda j, i: (i, j))
    est = 2 * (tp * D * 2 + 2 * D * tq * 2 + 3 * tp * tq * 2) + 4 * tp * tq * 4
    return pl.pallas_call(
        body, name=name,
        out_shape=[jax.ShapeDtypeStruct((T, H), BF16), jax.ShapeDtypeStruct((T, H), BF16),
                   jax.ShapeDtypeStruct((T, H), BF16)],
        grid=(nj, T // tp),
        in_specs=[pl.BlockSpec((tp, D), lambda j, i: (i, 0)),
                  pl.BlockSpec((None, D, tq), lambda j, i: (l, 0, j)),
                  pl.BlockSpec((None, D, tq), lambda j, i: (l, 0, j + nj))],
        out_specs=[tile, tile, tile],
        compiler_params=pltpu.CompilerParams(dimension_semantics=("parallel", "parallel"),
                                             vmem_limit_bytes=_vmem(est)),
    )(fn, w_gu, w_gu)


def _ffn_down_dx(df, w_down, l, g, u, after, *, name):
    T, D = df.shape
    H = w_down.shape[1]
    tp = _pick(T, (512, 256, 128))
    tq = H

    def body(a_ref, w_ref, g_ref, u_ref, _, dg_ref, du_ref):
        da = lax.dot_general(a_ref[...], w_ref[...], (((1,), (1,)), ((), ())), preferred_element_type=F32)
        dg_ref[...] = (da * g_ref[...].astype(F32)).astype(BF16)
        du_ref[...] = (da * u_ref[...].astype(F32)).astype(BF16)

    tile = pl.BlockSpec((tp, tq), lambda j, i: (i, j))
    est = 2 * (tp * D * 2 + tq * D * 2 + 4 * tp * tq * 2) + 3 * tp * tq * 4
    return pl.pallas_call(
        body, name=name,
        out_shape=[jax.ShapeDtypeStruct((T, H), BF16), jax.ShapeDtypeStruct((T, H), BF16)],
        grid=(H // tq, T // tp),
        in_specs=[pl.BlockSpec((tp, D), lambda j, i: (i, 0)),
                  pl.BlockSpec((None, tq, D), lambda j, i: (l, j, 0)), tile, tile,
                  pl.BlockSpec(memory_space=pl.ANY)],
        out_specs=[tile, tile],
        compiler_params=pltpu.CompilerParams(dimension_semantics=("parallel", "parallel"),
                                             vmem_limit_bytes=_vmem(est)),
    )(df, w_down, g, u, after)


def _loss_and_grad(y, target, x, g, *, name):
    T, D = y.shape
    tr = _pick(T, (512, 256, 128))

    def body(y_ref, t_ref, x_ref, g_ref, dy_ref, dx_ref, l_ref, dg_ref):
        e = y_ref[...] - t_ref[...]
        dy = e * (1.0 / D)
        dy_ref[...] = dy
        part = jnp.sum(jnp.sum(e * e, axis=1, keepdims=True), axis=0, keepdims=True) * (0.5 / D)
        xv = x_ref[...]
        r = lax.rsqrt(jnp.mean(xv * xv, axis=-1, keepdims=True) + RMS_EPS)
        xhat = xv * r
        dxn = dy * g_ref[...]
        dx_ref[...] = (r * (dxn - xhat * jnp.mean(dxn * xhat, axis=-1, keepdims=True))).astype(BF16)
        dg = jnp.sum(dy * xhat, axis=0, keepdims=True)

        @pl.when(pl.program_id(0) == 0)
        def _():
            l_ref[...] = part
            dg_ref[...] = dg

        @pl.when(pl.program_id(0) > 0)
        def _():
            l_ref[...] += part
            dg_ref[...] += dg

    dy, dx, l, dg = _row_call(body, [(y, "row"), (target, "row"), (x, "row"), (g, "full")],
                              [((T, D), F32), ((T, D), BF16)], name=name, rows=T, tr=tr,
                              acc_outs=[((1, 1), F32), ((1, D), F32)], est=14 * tr * D * 4)
    return dy, dx, l, dg


_SQRT_HALF = 0.7071067811865476
_INV_SQRT_2PI = 0.3989422804014327


def _gelu_parts(x):
    cdf = 0.5 * (1.0 + lax.erf(x * _SQRT_HALF))
    return cdf


def _sgu_common(pre, lng, lnb, W):
    cdf = _gelu_parts(pre)
    z = pre * cdf
    u = z[:, :W]
    v = z[:, W:]
    mu = jnp.mean(v, axis=-1, keepdims=True)
    vc = v - mu
    var = jnp.mean(vc * vc, axis=-1, keepdims=True)
    rstd = lax.rsqrt(var + LN_EPS)
    vhat = vc * rstd
    vn = vhat * lng + lnb
    return cdf, u, vhat, rstd, vn


def _causal_mask():
    t = lax.broadcasted_iota(jnp.int32, (CHUNK, CHUNK), 0)
    s = lax.broadcasted_iota(jnp.int32, (CHUNK, CHUNK), 1)
    return t >= s


def _sgu_fwd(pre, lng, lnb, ws, bsT, *, name):
    T, W2 = pre.shape
    W = W2 // 2
    G = ws.shape[0]
    gd = W // G

    def body(pre_ref, lng_ref, lnb_ref, ws_ref, bs_ref, o_ref):
        _, u, _, _, vn = _sgu_common(pre_ref[...], lng_ref[...], lnb_ref[...], W)
        vnb = vn.astype(BF16)
        causal = _causal_mask()
        for g in range(G):
            w = jnp.where(causal, ws_ref[g], 0.0).astype(BF16)
            sv = jnp.dot(w, vnb[:, g * gd:(g + 1) * gd], preferred_element_type=F32) + bs_ref[:, g:g + 1]
            o_ref[:, g * gd:(g + 1) * gd] = (u[:, g * gd:(g + 1) * gd] * sv).astype(BF16)

    return pl.pallas_call(
        body, name=name, out_shape=jax.ShapeDtypeStruct((T, W), BF16), grid=(T // CHUNK,),
        in_specs=[pl.BlockSpec((CHUNK, W2), lambda i: (i, 0)),
                  pl.BlockSpec((1, W), lambda i: (0, 0)), pl.BlockSpec((1, W), lambda i: (0, 0)),
                  pl.BlockSpec(ws.shape, lambda i: (0, 0, 0)), pl.BlockSpec(bsT.shape, lambda i: (0, 0))],
        out_specs=pl.BlockSpec((CHUNK, W), lambda i: (i, 0)),
        compiler_params=pltpu.CompilerParams(dimension_semantics=("arbitrary",),
                                             vmem_limit_bytes=_vmem(12 * CHUNK * W2 * 4)),
    )(pre, lng, lnb, ws, bsT)


def _sgu_bwd(pre, dgated, lng, lnb, ws, bsT, *, name):
    T, W2 = pre.shape
    W = W2 // 2
    G = ws.shape[0]
    gd = W // G

    def body(pre_ref, dgt_ref, lng_ref, lnb_ref, ws_ref, bs_ref,
             dpre_ref, dws_ref, dbs_ref, dlng_ref, dlnb_ref, dbin_ref):
        first = pl.program_id(0) == 0

        @pl.when(first)
        def _():
            dws_ref[...] = jnp.zeros_like(dws_ref)
            dbs_ref[...] = jnp.zeros_like(dbs_ref)
            dlng_ref[...] = jnp.zeros_like(dlng_ref)
            dlnb_ref[...] = jnp.zeros_like(dlnb_ref)
            dbin_ref[...] = jnp.zeros_like(dbin_ref)

        pre_v = pre_ref[...]
        lng_v = lng_ref[...]
        cdf, u, vhat, rstd, vn = _sgu_common(pre_v, lng_v, lnb_ref[...], W)
        vnb = vn.astype(BF16)
        dgt = dgt_ref[...].astype(F32)
        causal = _causal_mask()
        du_parts, dvn_parts = [], []
        for g in range(G):
            sl = slice(g * gd, (g + 1) * gd)
            w = jnp.where(causal, ws_ref[g], 0.0).astype(BF16)
            sv = jnp.dot(w, vnb[:, sl], preferred_element_type=F32) + bs_ref[:, g:g + 1]
            dgt_g = dgt[:, sl]
            du_parts.append(dgt_g * sv)
            dsv = dgt_g * u[:, sl]
            dsvb = dsv.astype(BF16)
            dvn_parts.append(lax.dot_general(w, dsvb, (((0,), (0,)), ((), ())), preferred_element_type=F32))
            dw = lax.dot_general(dsvb, vnb[:, sl], (((1,), (1,)), ((), ())), preferred_element_type=F32)
            dws_ref[g] += jnp.where(causal, dw, 0.0)
            dbs_ref[:, g:g + 1] += jnp.sum(dsv, axis=1, keepdims=True)
        du = jnp.concatenate(du_parts, axis=1)
        dvn = jnp.concatenate(dvn_parts, axis=1)
        dlng_ref[...] += jnp.sum(dvn * vhat, axis=0, keepdims=True)
        dlnb_ref[...] += jnp.sum(dvn, axis=0, keepdims=True)
        dvh = dvn * lng_v
        dv = rstd * (dvh - jnp.mean(dvh, axis=-1, keepdims=True)
                     - vhat * jnp.mean(dvh * vhat, axis=-1, keepdims=True))
        dz = jnp.concatenate([du, dv], axis=1)
        dgelu = cdf + pre_v * jnp.exp(-0.5 * pre_v * pre_v) * _INV_SQRT_2PI
        dpre = dz * dgelu
        dbin_ref[...] += jnp.sum(dpre, axis=0, keepdims=True)
        dpre_ref[...] = dpre.astype(BF16)

    full = lambda shape: pl.BlockSpec(shape, lambda i, nd=len(shape): (0,) * nd)
    return pl.pallas_call(
        body, name=name,
        out_shape=[jax.ShapeDtypeStruct((T, W2), BF16), jax.ShapeDtypeStruct(ws.shape, F32),
                   jax.ShapeDtypeStruct(bsT.shape, F32), jax.ShapeDtypeStruct((1, W), F32),
                   jax.ShapeDtypeStruct((1, W), F32), jax.ShapeDtypeStruct((1, W2), F32)],
        grid=(T // CHUNK,),
        in_specs=[pl.BlockSpec((CHUNK, W2), lambda i: (i, 0)), pl.BlockSpec((CHUNK, W), lambda i: (i, 0)),
                  full((1, W)), full((1, W)), full(ws.shape), full(bsT.shape)],
        out_specs=[pl.BlockSpec((CHUNK, W2), lambda i: (i, 0)), full(ws.shape), full(bsT.shape),
                   full((1, W)), full((1, W)), full((1, W2))],
        compiler_params=pltpu.CompilerParams(dimension_semantics=("arbitrary",),
                                             vmem_limit_bytes=_vmem(24 * CHUNK * W2 * 4)),
    )(pre, dgated, lng, lnb, ws, bsT)


def _rope_tables(positions):
    half = ROPE_DIM // 2
    inv_freq = ROPE_THETA ** (-jnp.arange(0, ROPE_DIM, 2, dtype=F32) / ROPE_DIM)
    ang = positions.astype(F32).reshape(-1, 1) * inv_freq
    cos, sin = jnp.cos(ang), jnp.sin(ang)
    T = ang.shape[0]
    rest = HEAD_DIM - ROPE_DIM
    c64 = jnp.concatenate([cos, cos, jnp.ones((T, rest), F32)], axis=1)
    s64 = jnp.concatenate([-sin, sin, jnp.zeros((T, rest), F32)], axis=1)
    del half
    return jnp.tile(c64, (1, LANES // HEAD_DIM)), jnp.tile(s64, (1, LANES // HEAD_DIM))


def _swap8(x):
    W = x.shape[1]
    half = ROPE_DIM // 2
    lane = lax.broadcasted_iota(jnp.int32, x.shape, 1) % HEAD_DIM
    return jnp.where(lane < half, pltpu.roll(x, W - half, axis=1),
                     jnp.where(lane < ROPE_DIM, pltpu.roll(x, half, axis=1), 0.0))


def _wide(tab, W):
    return jnp.concatenate([tab] * (W // LANES), axis=1) if W > LANES else tab


def _rope_fwd(qkv, ctab, stab, *, q_width, kv_width, name):
    T = qkv.shape[0]
    tr = _pick(T, (256, 128))
    scale = HEAD_DIM ** -0.5

    def body(x_ref, c_ref, s_ref, q_ref, k_ref, v_ref):
        c = c_ref[...]
        s = s_ref[...]
        q = x_ref[:, :q_width]
        k = x_ref[:, q_width:q_width + kv_width]
        q_ref[...] = ((q * _wide(c, q_width) + _swap8(q) * _wide(s, q_width)) * scale).astype(BF16)
        k_ref[...] = (k * _wide(c, kv_width) + _swap8(k) * _wide(s, kv_width)).astype(BF16)
        v_ref[...] = x_ref[:, q_width + kv_width:].astype(BF16)

    return _row_call(body, [(qkv, "row"), (ctab, "row"), (stab, "row")],
                     [((T, q_width), BF16), ((T, kv_width), BF16), ((T, kv_width), BF16)],
                     name=name, rows=T, tr=tr, est=10 * tr * qkv.shape[1] * 4)


_NT = (((1,), (1,)), ((), ()))
_TN = (((0,), (0,)), ((), ()))


def _group_rows(ref, heads):
    return jnp.concatenate([ref[:, h * HEAD_DIM:(h + 1) * HEAD_DIM] for h in heads], axis=0)


def _attn_valid(grp):
    qi = np.arange(grp * CHUNK)[:, None] % CHUNK
    sj = np.arange(2 * CHUNK)[None, :]
    cur = (sj >= CHUNK) & (sj - CHUNK <= qi)
    prev = (sj < CHUNK) & (sj > qi)
    return jnp.asarray(np.stack([cur, cur | prev]).astype(np.float32))


def _valid_spec(grp):
    return pl.BlockSpec((None, grp * CHUNK, 2 * CHUNK), lambda n: (jnp.minimum(n, 1), 0, 0))


def _attn_group_probs(q, kk, sinks, valid, grp):
    rows = grp * CHUNK
    s = lax.dot_general(q, kk, _NT, preferred_element_type=F32)
    s = jnp.where(valid, s, NEG_INF)
    r = lax.broadcasted_iota(jnp.int32, (rows, 1), 0)
    sink = jnp.full((rows, 1), sinks[grp - 1], F32)
    for g in range(grp - 2, -1, -1):
        sink = jnp.where(r < (g + 1) * CHUNK, sinks[g], sink)
    m = jnp.maximum(jnp.max(s, axis=1, keepdims=True), sink)
    p = jnp.exp(s - m)
    ps = jnp.exp(sink - m)
    inv = 1.0 / (jnp.sum(p, axis=1, keepdims=True) + ps)
    return p * inv, ps * inv


def _kv_specs(width, nb):
    prev = pl.BlockSpec((CHUNK, width), lambda n: (jnp.maximum(n - 1, 0), 0))
    cur = pl.BlockSpec((CHUNK, width), lambda n: (n, 0))
    return prev, cur


def _attn_fwd(qr, kr, vr, sinks, *, name):
    T, QW = qr.shape
    KW = kr.shape[1]
    HQ, HK = QW // HEAD_DIM, KW // HEAD_DIM
    grp = HQ // HK
    nb = T // CHUNK

    def body(q_ref, kp_ref, kc_ref, vp_ref, vc_ref, s_ref, ok_ref, o_ref):
        valid = ok_ref[...] > 0.5
        for kh in range(HK):
            ks = slice(kh * HEAD_DIM, (kh + 1) * HEAD_DIM)
            heads = list(range(kh * grp, (kh + 1) * grp))
            q = _group_rows(q_ref, heads)
            kk = jnp.concatenate([kp_ref[:, ks], kc_ref[:, ks]], axis=0)
            vv = jnp.concatenate([vp_ref[:, ks], vc_ref[:, ks]], axis=0)
            p, _ = _attn_group_probs(q, kk, [s_ref[0, h] for h in heads], valid, grp)
            o = jnp.dot(p.astype(BF16), vv, preferred_element_type=F32).astype(BF16)
            for g, h in enumerate(heads):
                o_ref[:, h * HEAD_DIM:(h + 1) * HEAD_DIM] = o[g * CHUNK:(g + 1) * CHUNK]

    kp, kc = _kv_specs(KW, nb)
    return pl.pallas_call(
        body, name=name, out_shape=jax.ShapeDtypeStruct((T, QW), BF16), grid=(nb,),
        in_specs=[pl.BlockSpec((CHUNK, QW), lambda n: (n, 0)), kp, kc, kp, kc,
                  pl.BlockSpec(memory_space=pltpu.SMEM), _valid_spec(grp)],
        out_specs=pl.BlockSpec((CHUNK, QW), lambda n: (n, 0)),
        compiler_params=pltpu.CompilerParams(dimension_semantics=("arbitrary",), vmem_limit_bytes=_vmem(8 << 20)),
    )(qr, kr, kr, vr, vr, sinks, _attn_valid(grp))


def _attn_bwd(qr, kr, vr, sinks, do, *, name):
    T, QW = qr.shape
    KW = kr.shape[1]
    HQ, HK = QW // HEAD_DIM, KW // HEAD_DIM
    grp = HQ // HK
    nb = T // CHUNK

    def body(q_ref, kp_ref, kc_ref, vp_ref, vc_ref, s_ref, do_ref, ok_ref,
             dq_ref, dkp_ref, dkc_ref, dvp_ref, dvc_ref, ds_ref):
        n = pl.program_id(0)
        valid = ok_ref[...] > 0.5
        lane = lax.broadcasted_iota(jnp.int32, (1, LANES), 1)
        dsink = jnp.zeros((1, LANES), F32)
        for kh in range(HK):
            ks = slice(kh * HEAD_DIM, (kh + 1) * HEAD_DIM)
            heads = list(range(kh * grp, (kh + 1) * grp))
            q = _group_rows(q_ref, heads)
            doh = _group_rows(do_ref, heads)
            kk = jnp.concatenate([kp_ref[:, ks], kc_ref[:, ks]], axis=0)
            vv = jnp.concatenate([vp_ref[:, ks], vc_ref[:, ks]], axis=0)
            p, ps = _attn_group_probs(q, kk, [s_ref[0, h] for h in heads], valid, grp)
            dp = lax.dot_general(doh, vv, _NT, preferred_element_type=F32)
            delta = jnp.sum(p * dp, axis=1, keepdims=True)
            ds = (p * (dp - delta)).astype(BF16)
            dv = lax.dot_general(p.astype(BF16), doh, _TN, preferred_element_type=F32)
            dk = lax.dot_general(ds, q, _TN, preferred_element_type=F32)
            dq = jnp.dot(ds, kk, preferred_element_type=F32)
            psd = ps * delta
            for g, h in enumerate(heads):
                dq_ref[:, h * HEAD_DIM:(h + 1) * HEAD_DIM] = dq[g * CHUNK:(g + 1) * CHUNK]
                dsink = dsink + jnp.where(
                    lane == h, -jnp.sum(psd[g * CHUNK:(g + 1) * CHUNK], axis=0, keepdims=True), 0.0)
            dkp_ref[:, ks] = dk[:CHUNK]
            dkc_ref[:, ks] = dk[CHUNK:]
            dvp_ref[:, ks] = dv[:CHUNK]
            dvc_ref[:, ks] = dv[CHUNK:]

        @pl.when(n == 0)
        def _():
            ds_ref[...] = dsink

        @pl.when(n > 0)
        def _():
            ds_ref[...] += dsink

    kp, kc = _kv_specs(KW, nb)
    qspec = pl.BlockSpec((CHUNK, QW), lambda n: (n, 0))
    kout = pl.BlockSpec((CHUNK, KW), lambda n: (n, 0))
    return pl.pallas_call(
        body, name=name,
        out_shape=[jax.ShapeDtypeStruct((T, QW), F32)] + [jax.ShapeDtypeStruct((T, KW), F32)] * 4
        + [jax.ShapeDtypeStruct((1, LANES), F32)],
        grid=(nb,),
        in_specs=[qspec, kp, kc, kp, kc, pl.BlockSpec(memory_space=pltpu.SMEM), qspec, _valid_spec(grp)],
        out_specs=[qspec, kout, kout, kout, kout, pl.BlockSpec((1, LANES), lambda n: (0, 0))],
        compiler_params=pltpu.CompilerParams(dimension_semantics=("arbitrary",), vmem_limit_bytes=_vmem(12 << 20)),
    )(qr, kr, kr, vr, vr, sinks, do, _attn_valid(grp))


def _rope_bwd(dq, dkp, dkc, dvp, dvc, ctab, stab, *, name):
    T, QW = dq.shape
    KW = dkp.shape[1]
    nb = T // CHUNK
    scale = HEAD_DIM ** -0.5
    width = QW + 2 * KW

    def body(dq_ref, dkc_ref, dkn_ref, dvc_ref, dvn_ref, c_ref, s_ref, o_ref, db_ref):
        n = pl.program_id(0)
        c = c_ref[...]
        s = s_ref[...]
        has_next = (n < nb - 1).astype(F32)
        dqv = dq_ref[...]
        dk = dkc_ref[...] + has_next * dkn_ref[...]
        dv = dvc_ref[...] + has_next * dvn_ref[...]
        dq_pre = (dqv * _wide(c, QW) + _swap8(dqv * _wide(s, QW))) * scale
        dk_pre = dk * _wide(c, KW) + _swap8(dk * _wide(s, KW))
        o_ref[:, :QW] = dq_pre.astype(BF16)
        o_ref[:, QW:QW + KW] = dk_pre.astype(BF16)
        o_ref[:, QW + KW:] = dv.astype(BF16)
        part = jnp.concatenate([jnp.sum(dq_pre, axis=0, keepdims=True), jnp.sum(dk_pre, axis=0, keepdims=True),
                                jnp.sum(dv, axis=0, keepdims=True)], axis=1)

        @pl.when(n == 0)
        def _():
            db_ref[...] = part

        @pl.when(n > 0)
        def _():
            db_ref[...] += part

    cur = lambda w: pl.BlockSpec((CHUNK, w), lambda n: (n, 0))
    nxt = lambda w: pl.BlockSpec((CHUNK, w), lambda n: (jnp.minimum(n + 1, nb - 1), 0))
    return pl.pallas_call(
        body, name=name,
        out_shape=[jax.ShapeDtypeStruct((T, width), BF16), jax.ShapeDtypeStruct((1, width), F32)],
        grid=(nb,),
        in_specs=[cur(QW), cur(KW), nxt(KW), cur(KW), nxt(KW), cur(LANES), cur(LANES)],
        out_specs=[cur(width), pl.BlockSpec((1, width), lambda n: (0, 0))],
        compiler_params=pltpu.CompilerParams(dimension_semantics=("arbitrary",), vmem_limit_bytes=_vmem(8 << 20)),
    )(dq, dkc, dkp, dvc, dvp, ctab, stab)


def _cast_block(w, l, axis, chip_arr, *, name):
    _, Ks, Ns = w.shape
    tk = _pick(Ks, (512, 352, 256, 128))
    nk = Ks // tk
    full = (Ks * N_CHIPS, Ns) if axis == 0 else (Ks, Ns * N_CHIPS)

    def body(p_ref, w_ref, o_ref):
        o_ref[...] = w_ref[...].astype(BF16)

    if axis == 0:
        out_spec = pl.BlockSpec((tk, Ns), lambda i, p: (p[0] * nk + i, 0))
    else:
        out_spec = pl.BlockSpec((tk, Ns), lambda i, p: (i, p[0]))
    grid_spec = pltpu.PrefetchScalarGridSpec(
        num_scalar_prefetch=1, grid=(nk,),
        in_specs=[pl.BlockSpec((None, tk, Ns), lambda i, p: (l, i, 0))], out_specs=out_spec)
    return pl.pallas_call(
        body, name=name, out_shape=jax.ShapeDtypeStruct(full, BF16), grid_spec=grid_spec,
        compiler_params=pltpu.CompilerParams(dimension_semantics=("arbitrary",),
                                             vmem_limit_bytes=_vmem(4 * tk * Ns * 6)),
    )(chip_arr, w)


def _adamw_math(w, g, m, v):
    m = ADAM_B1 * m + (1.0 - ADAM_B1) * g
    v = ADAM_B2 * v + (1.0 - ADAM_B2) * (g * g)
    m_hat = m / (1.0 - ADAM_B1 ** ADAM_STEP)
    v_hat = v / (1.0 - ADAM_B2 ** ADAM_STEP)
    delta = -ADAM_LR * (m_hat / (jnp.sqrt(v_hat) + ADAM_EPS) + ADAM_WD * w)
    return delta, m, v


def _adamw_layer(w, m, v, g, l, outs, *, name, after=()):
    _, K, N = w.shape
    tk = _pick(K, (512, 352, 256, 128)) if N <= 1024 else _pick(K, (256, 176, 128))
    n_after = len(after)

    def body(w_ref, m_ref, v_ref, g_ref, *rest):
        go_ref, d_ref, mo_ref, vo_ref = rest[4 + n_after:]
        gv = g_ref[...]
        d, mn, vn = _adamw_math(w_ref[...], gv, m_ref[...], v_ref[...])
        go_ref[...] = gv
        d_ref[...] = d
        mo_ref[...] = mn
        vo_ref[...] = vn

    layer = pl.BlockSpec((None, tk, N), lambda i: (l, i, 0))
    any_spec = pl.BlockSpec(memory_space=pl.ANY)
    sd = jax.ShapeDtypeStruct(w.shape, F32)
    return pl.pallas_call(
        body, name=name, out_shape=[sd, sd, sd, sd], grid=(K // tk,),
        in_specs=[layer, layer, layer, pl.BlockSpec((tk, N), lambda i: (i, 0))] + [any_spec] * (4 + n_after),
        out_specs=[layer] * 4, input_output_aliases={4: 0, 5: 1, 6: 2, 7: 3},
        compiler_params=pltpu.CompilerParams(dimension_semantics=("arbitrary",),
                                             vmem_limit_bytes=_vmem(2 * 8 * tk * N * 4 + 6 * tk * N * 4)),
    )(w, m, v, g, *outs, *after)


def _adamw_small(w, g, m, v, *, name):
    def body(w_ref, g_ref, m_ref, v_ref, d_ref, mo_ref, vo_ref):
        d, mn, vn = _adamw_math(w_ref[...], g_ref[...], m_ref[...], v_ref[...])
        d_ref[...] = d
        mo_ref[...] = mn
        vo_ref[...] = vn

    sd = jax.ShapeDtypeStruct(w.shape, F32)
    return pl.pallas_call(body, name=name, out_shape=[sd, sd, sd])(w, g, m, v)


def _my_place():
    return lax.axis_index("x"), lax.axis_index("y"), lax.axis_index("c")


def _peer_chips(x, y):
    return [(1 - x, y), (x, 1 - y), (1 - x, 1 - y)]


_HBM = pl.BlockSpec(memory_space=pltpu.HBM)
_SEM = pl.BlockSpec(memory_space=pltpu.SEMAPHORE)
_EFFECT = pltpu.SideEffectType.DATAFLOW_SIDE_EFFECTING


def _split_start(name, bufs, n_copies, make_copies, after):
    nb = len(bufs)

    def body(*refs):
        send_sems, recv_sems = refs[nb + 1], refs[nb + 2]
        token = refs[2 * nb + 3]
        sends, _ = make_copies(refs[:nb], send_sems, recv_sems)
        for cp in sends:
            cp.start()
        token[...] = jnp.zeros_like(token)

    res = pl.pallas_call(
        body, name=name,
        out_shape=(pltpu.SemaphoreType.DMA((n_copies,)), pltpu.SemaphoreType.DMA((n_copies,)),
                   *[pltpu.HBM(b.shape, b.dtype) for b in bufs], jax.ShapeDtypeStruct((8, LANES), F32)),
        in_specs=[_HBM] * nb + [pl.BlockSpec(memory_space=pl.ANY)],
        out_specs=(_SEM, _SEM, *[_HBM] * nb, pl.BlockSpec(memory_space=pltpu.VMEM)),
        input_output_aliases={k: 2 + k for k in range(nb)},
        compiler_params=pltpu.CompilerParams(has_side_effects=_EFFECT),
    )(*[pltpu.with_memory_space_constraint(b, pltpu.HBM) for b in bufs],
      after[0] if isinstance(after, (list, tuple)) else after)
    return res[0], res[1], list(res[2:2 + nb]), res[2 + nb]


def _split_wait(name, bufs, sems, make_copies, after):
    nb = len(bufs)
    after = list(after) if isinstance(after, (list, tuple)) else [after]

    def body(*refs):
        send_sems, recv_sems = refs[nb], refs[nb + 1]
        sends, recvs = make_copies(refs[:nb], send_sems, recv_sems)
        for cp in sends:
            cp.wait_send()
        for cp in recvs:
            cp.wait_recv()

    res = pl.pallas_call(
        body, name=name,
        out_shape=tuple(pltpu.HBM(b.shape, b.dtype) for b in bufs),
        in_specs=[_HBM] * nb + [_SEM, _SEM] + [pl.BlockSpec(memory_space=pl.ANY)] * len(after),
        out_specs=tuple([_HBM] * nb),
        input_output_aliases={k: k for k in range(nb)},
        compiler_params=pltpu.CompilerParams(has_side_effects=_EFFECT),
    )(*bufs, sems[0], sems[1], *after)
    return list(res)


def _remote(src, dst, send_sems, recv_sems, k, target):
    return pltpu.make_async_remote_copy(src_ref=src, dst_ref=dst, send_sem=send_sems.at[k],
                                        recv_sem=recv_sems.at[k], device_id=target, device_id_type=MESH)


def _ag_region(ref, axis, chip, half):
    K, N = ref.shape
    if axis == 0:
        hs = K // N_CHIPS // 2
        assert hs % 16 == 0
        return ref.at[pl.ds(pl.multiple_of((2 * chip + half) * hs, 16), hs), :]
    ns, hk = N // N_CHIPS, K // 2
    assert ns % LANES == 0 and hk % 16 == 0
    return ref.at[pl.ds(pl.multiple_of(half * hk, 16), hk), pl.ds(pl.multiple_of(chip * ns, LANES), ns)]


def _ag_copies(stage, axes):
    n = len(axes)

    def make(bufs, send_sems, recv_sems):
        x, y, c = _my_place()
        me = 2 * x + y
        sends, recvs = [], []
        for j, (px, py) in enumerate(_peer_chips(x, y)):
            other = 2 * px + py
            for w in range(n):
                k = j * n + w
                if stage == 1:
                    src, target = _ag_region(bufs[w], axes[w], me, c), (px, py, c)
                    land = _ag_region(bufs[w], axes[w], other, c)
                else:
                    src, target = _ag_region(bufs[w], axes[w], other, c), (x, y, 1 - c)
                    land = _ag_region(bufs[w], axes[w], other, 1 - c)
                sends.append(_remote(src, src, send_sems, recv_sems, k, target))
                recvs.append(_remote(land, land, send_sems, recv_sems, k, target))
        return sends, recvs

    return make


def _half_shape(shape, axis):
    K, N = shape
    return (K, N // 2) if axis == 0 else (K // 2, N)


def _core_half(ref, axis, half):
    K, N = ref.shape
    if axis == 0:
        return ref.at[:, pl.ds(pl.multiple_of(half * (N // 2), LANES), N // 2)]
    return ref.at[pl.ds(pl.multiple_of(half * (K // 2), 16), K // 2), :]


def _chip_block(ref, axis, chip):
    K, N = ref.shape
    if axis == 0:
        return ref.at[pl.ds(pl.multiple_of(chip * (K // N_CHIPS), 16), K // N_CHIPS), :]
    return ref.at[:, pl.ds(pl.multiple_of(chip * (N // N_CHIPS), LANES), N // N_CHIPS)]


def _rs_sibling_copies(axes):
    n = len(axes)

    def make(bufs, send_sems, recv_sems):
        x, y, c = _my_place()
        sends = [_remote(_core_half(bufs[w], axes[w], 1 - c), bufs[n + w], send_sems, recv_sems, w, (x, y, 1 - c))
                 for w in range(n)]
        recvs = [_remote(bufs[n + w], bufs[n + w], send_sems, recv_sems, w, (x, y, 1 - c)) for w in range(n)]
        return sends, recvs

    return make


def _rs_chip_copies(axes):
    n = len(axes)

    def make(bufs, send_sems, recv_sems):
        x, y, c = _my_place()
        sends, recvs = [], []
        for j, (px, py) in enumerate(_peer_chips(x, y)):
            for w in range(n):
                k = j * n + w
                sends.append(_remote(_chip_block(bufs[w], axes[w], 2 * px + py), bufs[n + w].at[j],
                                     send_sems, recv_sems, k, (px, py, c)))
                recvs.append(_remote(bufs[n + w].at[j], bufs[n + w].at[j], send_sems, recv_sems, k, (px, py, c)))
        return sends, recvs

    return make


def _rs_fill_copies(axes):
    n = len(axes)

    def make(bufs, send_sems, recv_sems):
        x, y, c = _my_place()
        sends = [_remote(_core_half(bufs[w], axes[w], c), _core_half(bufs[w], axes[w], c),
                         send_sems, recv_sems, w, (x, y, 1 - c)) for w in range(n)]
        recvs = [_remote(_core_half(bufs[w], axes[w], 1 - c), _core_half(bufs[w], axes[w], 1 - c),
                         send_sems, recv_sems, w, (x, y, 1 - c)) for w in range(n)]
        return sends, recvs

    return make


def _chip_sum(g, r, axis, place, *, name):
    hk, hn = r.shape
    bk, bn = (hk // N_CHIPS, hn) if axis == 0 else (hk, hn // N_CHIPS)
    tk = _pick(bk, (512, 352, 256, 128))
    nk = bk // tk

    def body(p_ref, g_ref, r_ref, b_ref, own_ref):
        s = g_ref[...].astype(F32) + r_ref[...].astype(F32)
        b_ref[...] = s.astype(BF16)

        @pl.when(pl.program_id(1) == p_ref[0])
        def _():
            own_ref[...] = s

    if axis == 0:
        g_spec = pl.BlockSpec((tk, bn), lambda i, j, p: (j * nk + i, p[1]))
        r_spec = pl.BlockSpec((tk, bn), lambda i, j, p: (j * nk + i, 0))
    else:
        g_spec = pl.BlockSpec((tk, bn), lambda i, j, p: (p[1] * nk + i, j))
        r_spec = pl.BlockSpec((tk, bn), lambda i, j, p: (i, j))
    grid_spec = pltpu.PrefetchScalarGridSpec(
        num_scalar_prefetch=1, grid=(nk, N_CHIPS), in_specs=[g_spec, r_spec],
        out_specs=[r_spec, pl.BlockSpec((tk, bn), lambda i, j, p: (i, 0))])
    return pl.pallas_call(
        body, name=name,
        out_shape=[jax.ShapeDtypeStruct(r.shape, BF16), jax.ShapeDtypeStruct((bk, bn), F32)],
        grid_spec=grid_spec,
        compiler_params=pltpu.CompilerParams(dimension_semantics=("arbitrary", "arbitrary"),
                                             vmem_limit_bytes=_vmem(2 * tk * bn * 10 + 3 * tk * bn * 4)),
    )(place, g, r)


def _final_sum(own, recv, axis, place, *, name):
    _, bk, bn = recv.shape
    tk = _pick(bk, (256, 176, 128))
    nk = bk // tk

    def body(p_ref, o_ref, r_ref, out_ref):
        out_ref[...] = ((o_ref[...] + r_ref[0].astype(F32)) + r_ref[1].astype(F32)) + r_ref[2].astype(F32)

    own_spec = pl.BlockSpec((tk, bn), lambda i, p: (i, 0))
    if axis == 0:
        out_shape, out_spec = (bk, 2 * bn), pl.BlockSpec((tk, bn), lambda i, p: (i, p[1]))
    else:
        out_shape, out_spec = (2 * bk, bn), pl.BlockSpec((tk, bn), lambda i, p: (p[1] * nk + i, 0))
    grid_spec = pltpu.PrefetchScalarGridSpec(
        num_scalar_prefetch=1, grid=(nk,),
        in_specs=[own_spec, pl.BlockSpec((3, tk, bn), lambda i, p: (0, i, 0))], out_specs=out_spec)
    return pl.pallas_call(
        body, name=name, out_shape=jax.ShapeDtypeStruct(out_shape, F32), grid_spec=grid_spec,
        compiler_params=pltpu.CompilerParams(dimension_semantics=("arbitrary",),
                                             vmem_limit_bytes=_vmem(2 * tk * bn * 14 + 4 * tk * bn * 4)),
    )(place, own, recv)


def _allreduce_small(p, after=()):
    n_after = len(after)

    def body(*refs):
        p_ref = refs[0]
        o_ref, r0, r1, r2, send_sems, recv_sems = refs[1 + n_after:]
        x, y, c = _my_place()
        o_ref[...] = p_ref[...]
        for s, (peer, rbuf) in enumerate([((x, y, 1 - c), r0), ((1 - x, y, c), r1), ((x, 1 - y, c), r2)]):
            cp = pltpu.make_async_remote_copy(src_ref=o_ref, dst_ref=rbuf, send_sem=send_sems.at[s],
                                              recv_sem=recv_sems.at[s], device_id=peer, device_id_type=MESH)
            cp.start()
            cp.wait()
            o_ref[...] = o_ref[...] + rbuf[...]

    vm = pl.BlockSpec(memory_space=pltpu.VMEM)
    return pl.pallas_call(
        body, name="allreduce_small", out_shape=jax.ShapeDtypeStruct(p.shape, F32),
        in_specs=[vm] + [pl.BlockSpec(memory_space=pl.ANY)] * n_after, out_specs=vm,
        scratch_shapes=[pltpu.VMEM(p.shape, F32)] * 3 + [pltpu.SemaphoreType.DMA((3,))] * 2,
        compiler_params=pltpu.CompilerParams(vmem_limit_bytes=_vmem(6 * _nbytes(p.shape, F32))),
    )(p, *after)


_BUTTERFLY = (lambda x, y, c: (x, y, 1 - c), lambda x, y, c: (1 - x, y, c), lambda x, y, c: (x, 1 - y, c))


def _swap_copies(stage):
    def make(bufs, send_sems, recv_sems):
        target = _BUTTERFLY[stage](*_my_place())
        return ([_remote(bufs[0], bufs[1], send_sems, recv_sems, 0, target)],
                [_remote(bufs[1], bufs[1], send_sems, recv_sems, 0, target)])

    return make


def _add(a, b, *, name):
    def body(a_ref, b_ref, o_ref):
        o_ref[...] = a_ref[...] + b_ref[...]

    return pl.pallas_call(body, name=name, out_shape=jax.ShapeDtypeStruct(a.shape, a.dtype))(a, b)


def _pack_rows(parts):
    rows, metas = [], []
    for a in parts:
        flat = a.reshape(-1)
        nrow = -(-flat.shape[0] // LANES)
        nrow = -(-nrow // 8) * 8
        flat = jnp.pad(flat, (0, nrow * LANES - flat.shape[0]))
        rows.append(flat.reshape(nrow, LANES))
        metas.append((a.shape, nrow))
    return jnp.concatenate(rows, axis=0), metas


def _unpack_rows(packed, metas):
    out, r0 = [], 0
    for shape, nrow in metas:
        size = int(np.prod(shape))
        out.append(packed[r0:r0 + nrow].reshape(-1)[:size].reshape(shape))
        r0 += nrow
    return out


def kernel(x, positions, pre_mix_g, post_mix_g, pre_ffn_g, post_ffn_g, a_w_in, a_b_in, a_ln_g, a_ln_b, a_w_s, a_b_s, a_w_out, b_w_qkv, b_b_qkv, b_sinks, b_w_o, ffn_w_gu, ffn_w_down, loss_target, m_pre_mix_g, m_post_mix_g, m_pre_ffn_g, m_post_ffn_g, m_a_w_in, m_a_b_in, m_a_ln_g, m_a_ln_b, m_a_w_s, m_a_b_s, m_a_w_out, m_b_w_qkv, m_b_b_qkv, m_b_sinks, m_b_w_o, m_ffn_w_gu, m_ffn_w_down, v_pre_mix_g, v_post_mix_g, v_pre_ffn_g, v_post_ffn_g, v_a_w_in, v_a_b_in, v_a_ln_g, v_a_ln_b, v_a_w_s, v_a_b_s, v_a_w_out, v_b_w_qkv, v_b_b_qkv, v_b_sinks, v_b_w_o, v_ffn_w_gu, v_ffn_w_down):
    depth, D = pre_mix_g.shape
    xi, yi, ci = _my_place()
    chip = 2 * xi + yi
    place = jnp.stack([chip, ci]).astype(jnp.int32)

    stacked = {"a_w_in": (a_w_in, m_a_w_in, v_a_w_in), "a_w_out": (a_w_out, m_a_w_out, v_a_w_out),
               "b_w_qkv": (b_w_qkv, m_b_w_qkv, v_b_w_qkv), "b_w_o": (b_w_o, m_b_w_o, v_b_w_o),
               "ffn_w_gu": (ffn_w_gu, m_ffn_w_gu, v_ffn_w_gu), "ffn_w_down": (ffn_w_down, m_ffn_w_down, v_ffn_w_down)}
    cut = {"a_w_in": 1, "a_w_out": 0, "b_w_qkv": 1, "b_w_o": 0, "ffn_w_gu": 1, "ffn_w_down": 0}

    def layer_keys(i):
        mix = [("a_w_in", i // 2), ("a_w_out", i // 2)] if i % 2 == 0 else [("b_w_qkv", i // 2), ("b_w_o", i // 2)]
        return mix + [("ffn_w_gu", i), ("ffn_w_down", i)]

    def dep(a, toks):
        for t in toks:
            a = a + t[:1, :1]
        return a

    W = {}
    for i in range(depth):
        for nm, l in layer_keys(i):
            W[(nm, l)] = _cast_block(stacked[nm][0], l, cut[nm], place, name=f"cast_{nm}_{l}")

    def gather(tag, keys, after):
        axes = [cut[nm] for nm, _ in keys]
        for stage in (1, 2):
            ss, rs, bufs, tok = _split_start(f"ag{stage}_start_{tag}", [W[k] for k in keys], 3 * len(keys),
                                             _ag_copies(stage, axes), after)
            after = yield tok
            bufs = _split_wait(f"ag{stage}_wait_{tag}", bufs, (ss, rs), _ag_copies(stage, axes), after)
            W.update(zip(keys, bufs))
        yield None

    nq = b_b_qkv.shape[1]
    bq_full = jnp.zeros((b_b_qkv.shape[0], N_CHIPS * nq), F32)
    bq_full = lax.dynamic_update_slice(bq_full, jnp.where(ci == 0, b_b_qkv, 0.0), (0, chip * nq))
    bq_packed, bq_meta = _pack_rows([bq_full])
    bq_gathered = _allreduce_small(bq_packed)
    b_qkv_full = _unpack_rows(bq_gathered, bq_meta)[0]

    first = gather("0m", layer_keys(0)[:2], bq_gathered)
    tok = next(first)
    tok = first.send([tok] + [W[k] for i in range(depth) for k in layer_keys(i)[2 if i == 0 else 0:]])
    first.send(tok)

    h = x[0]
    target = loss_target[0]
    ctab, stab = _rope_tables(positions[0])
    q_width = W[("b_w_o", 0)].shape[0]
    kv_width = N_KV_HEADS * HEAD_DIM
    row = lambda a, i: a[i:i + 1]
    gains = {"pre_mix": pre_mix_g[:, None], "post_mix": post_mix_g[:, None], "pre_ffn": pre_ffn_g[:, None],
             "post_ffn": post_ffn_g[:, None]}
    gain = lambda which, i: (gains[which], i)

    saved = []
    hn = None
    for i in range(depth):
        j = i // 2
        s = {"h": h}
        ffn_w = None
        if i == 0:
            ffn_w = gather("0f", layer_keys(0)[2:], W[("a_w_out", 0)])
            toks = [next(ffn_w)]
            nxt = gather("1", layer_keys(1), toks[0])
            toks.append(next(nxt))
            hn = _rms_fwd(h, gain("pre_mix", i), out_dtype=BF16, after=toks, name=f"rms_pre_mix_{i}")
        elif i + 1 < depth:
            nxt = gather(str(i + 1), layer_keys(i + 1), h)
            toks = [next(nxt)]
        else:
            toks = []
        s["hn"] = hn
        if i % 2 == 0:
            pre = _matmul(hn, W[("a_w_in", j)], mode="nn", bias=row(a_b_in, j), out_dtype=F32, after=toks,
                          name=f"gmlp_in_{i}")
            gated = _sgu_fwd(pre, row(a_ln_g, j), row(a_ln_b, j), a_w_s[j], a_b_s[j].T, name=f"sgu_fwd_{i}")
            mix = _matmul(gated, W[("a_w_out", j)], mode="nn", out_dtype=F32, name=f"gmlp_out_{i}")
            s.update(pre=pre, gated=gated)
        else:
            qkv = _matmul(hn, W[("b_w_qkv", j)], mode="nn", bias=row(b_qkv_full, j), out_dtype=F32, after=toks,
                          name=f"attn_qkv_{i}")
            qr, kr, vr = _rope_fwd(qkv, ctab, stab, q_width=q_width, kv_width=kv_width, name=f"rope_fwd_{i}")
            o = _attn_fwd(qr, kr, vr, row(b_sinks, j), name=f"attn_fwd_{i}")
            mix = _matmul(o, W[("b_w_o", j)], mode="nn", out_dtype=F32, name=f"attn_o_{i}")
            s.update(qr=qr, kr=kr, vr=vr, o=o)
        s["mix"] = mix
        toks = [ffn_w.send(mix)] if ffn_w else []
        h1, fn = _rms_res_norm(h, mix, gain("post_mix", i), gain("pre_ffn", i), after=toks, name=f"rms_post_mix_{i}")
        if ffn_w:
            ffn_w.send(h1)
        s["h1"] = h1
        g_pre, u_pre, act = _ffn_up(fn, W[("ffn_w_gu", i)][None], 0, name=f"ffn_up_{i}")
        f = _matmul(act, W[("ffn_w_down", i)], mode="nn", out_dtype=F32, name=f"ffn_down_{i}")
        if i + 1 < depth:
            toks = [nxt.send(f)]
            h, hn = _rms_res_norm(h1, f, gain("post_ffn", i), gain("pre_mix", i + 1), after=toks,
                                  name=f"rms_post_ffn_{i}")
            nxt.send(h)
        else:
            h = _rms_res(h1, f, gain("post_ffn", i), name=f"rms_post_ffn_{i}")
        s.update(fn=fn, g_pre=g_pre, u_pre=u_pre, act=act, f=f)
        saved.append(s)

    dh, df, loss_part, g_last = _loss_and_grad(h, target, saved[-1]["f"], gain("post_ffn", depth - 1), name="loss")

    big_out = {nm: tuple(lax.empty(w.shape, F32) for _ in range(4)) for nm, (w, _, _) in stacked.items()}

    def reduce_group(i, keys, grads):
        axes = [cut[nm] for nm, _ in keys]
        n = len(keys)
        lands = [lax.empty(_half_shape(g.shape, ax), BF16) for g, ax in zip(grads, axes)]
        ss, rs, bufs, tok = _split_start(f"rs_sibling_start_{i}", list(grads) + lands, n, _rs_sibling_copies(axes),
                                         place)
        after = yield tok
        bufs = _split_wait(f"rs_sibling_wait_{i}", bufs, (ss, rs), _rs_sibling_copies(axes), after)
        sums = [_chip_sum(bufs[w], bufs[n + w], axes[w], place, name=f"chip_sum_{keys[w][0]}_{keys[w][1]}")
                for w in range(n)]
        lands = [lax.empty((3,) + own.shape, BF16) for _, own in sums]
        ss, rs, bufs, tok = _split_start(f"rs_chip_start_{i}", [sb for sb, _ in sums] + lands, 3 * n,
                                         _rs_chip_copies(axes), place)
        after = yield tok
        bufs = _split_wait(f"rs_chip_wait_{i}", bufs, (ss, rs), _rs_chip_copies(axes), after)
        blocks = [_final_sum(sums[w][1], bufs[n + w], axes[w], place, name=f"final_sum_{keys[w][0]}_{keys[w][1]}")
                  for w in range(n)]
        ss, rs, bufs, tok = _split_start(f"rs_fill_start_{i}", blocks, n, _rs_fill_copies(axes), place)
        after = yield tok
        blocks = _split_wait(f"rs_fill_wait_{i}", bufs, (ss, rs), _rs_fill_copies(axes), after)
        updates.extend(zip(keys, blocks))
        yield None

    updates = []

    def adamw(items, after):
        for (nm, l), g in items:
            w, m, v = stacked[nm]
            big_out[nm] = tuple(_adamw_layer(w, m, v, g, l, big_out[nm], after=after, name=f"adamw_{nm}_{l}"))
        return [big_out[nm][1] for nm in dict.fromkeys(nm for (nm, _), _ in items)]

    reducing = []

    def advance(after, newest_only=False):
        toks = []
        for gen in (reducing[-1:] if newest_only else list(reducing)):
            tok = gen.send(after)
            if tok is None:
                reducing.remove(gen)
            else:
                toks.append(tok)
        return toks

    small = {}
    g_pre_mix, g_post_mix, g_pre_ffn, g_post_ffn = [None] * depth, [None] * depth, [None] * depth, [None] * depth
    g_post_ffn[depth - 1] = g_last
    toks = []
    early = []
    for i in reversed(range(depth)):
        j = i // 2
        s = saved[i]
        g_down = _matmul(s["act"], df, mode="tn", out_dtype=BF16, after=toks, name=f"ffn_down_dw_{i}")
        dg_, du_ = _ffn_down_dx(df, W[("ffn_w_down", i)][None], 0, s["g_pre"], s["u_pre"], g_down,
                                name=f"ffn_down_dx_{i}")
        g_gu = _matmul_pair(s["fn"], dg_, du_, mode="tn", out_dtype=BF16, name=f"ffn_gu_dw_{i}")
        dfn = _matmul_pair(dg_, W[("ffn_w_gu", i)], du_, mode="nt", out_dtype=F32, after=[g_gu],
                           name=f"ffn_gu_dx_{i}")
        toks = advance(dfn)
        if i == 0:
            gen = reduce_group("0f", layer_keys(0)[2:], [g_gu, g_down])
            toks.append(next(gen))
            reducing.append(gen)
        dh1, dmix, g_pre_ffn[i], g_post_mix[i] = _rms_bwd_chain(
            s["h1"], gain("pre_ffn", i), dfn, dh, s["mix"], gain("post_mix", i), after=toks,
            name=f"rms_ffn_mix_bwd_{i}")
        if i % 2 == 0:
            g_out = _matmul(s["gated"], dmix, mode="tn", out_dtype=BF16, name=f"gmlp_out_dw_{i}")
            dgated = _matmul(dmix, W[("a_w_out", j)], mode="nt", out_dtype=BF16, after=[g_out],
                             name=f"gmlp_out_dx_{i}")
            toks = advance(dgated, newest_only=True) if i == 0 else []
            dpre, dws, dbsT, dlng, dlnb, dbin = _sgu_bwd(s["pre"], dgated, dep(row(a_ln_g, j), toks), row(a_ln_b, j),
                                                         a_w_s[j], a_b_s[j].T, name=f"sgu_bwd_{i}")
            small[("a_w_s", j)] = dws
            small[("a_b_s", j)] = dbsT.T
            small[("a_ln_g", j)] = dlng
            small[("a_ln_b", j)] = dlnb
            small[("a_b_in", j)] = dbin
            g_in = _matmul(s["hn"], dpre, mode="tn", out_dtype=BF16, name=f"gmlp_in_dw_{i}")
            if i == 0:
                last = reduce_group("0m", layer_keys(0)[:2], [g_in, g_out])
                early = [next(last)]
            dhn = _matmul(dpre, W[("a_w_in", j)], mode="nt", out_dtype=F32, after=[g_in] + early,
                          name=f"gmlp_in_dx_{i}")
        else:
            g_out = _matmul(s["o"], dmix, mode="tn", out_dtype=BF16, name=f"attn_o_dw_{i}")
            do = _matmul(dmix, W[("b_w_o", j)], mode="nt", out_dtype=BF16, after=[g_out], name=f"attn_o_dx_{i}")
            dq, dkp, dkc, dvp, dvc, dsk = _attn_bwd(s["qr"], s["kr"], s["vr"], row(b_sinks, j), do,
                                                    name=f"attn_bwd_{i}")
            dqkv, dbq = _rope_bwd(dq, dkp, dkc, dvp, dvc, ctab, stab, name=f"rope_bwd_{i}")
            small[("b_sinks", j)] = dsk[:, :b_sinks.shape[1]]
            small[("b_b_qkv", j)] = dbq
            g_in = _matmul(s["hn"], dqkv, mode="tn", out_dtype=BF16, name=f"attn_qkv_dw_{i}")
            if i == 0:
                last = reduce_group("0m", layer_keys(0)[:2], [g_in, g_out])
                early = [next(last)]
            dhn = _matmul(dqkv, W[("b_w_qkv", j)], mode="nt", out_dtype=F32, after=[g_in] + early,
                          name=f"attn_qkv_dx_{i}")
        toks = advance(dhn)
        if i > 0:
            dh, df, g_pre_mix[i], g_post_ffn[i - 1] = _rms_bwd_chain(
                s["h"], gain("pre_mix", i), dhn, dh1, saved[i - 1]["f"], gain("post_ffn", i - 1), after=toks,
                name=f"rms_mix_ffn_bwd_{i}")
            gen = reduce_group(str(i), layer_keys(i), [g_in, g_out, g_gu, g_down])
            toks = [next(gen)] + advance(dh)
            reducing.append(gen)
        else:
            toks.append(last.send(dhn))
            dh, g_pre_mix[i] = _rms_bwd(s["h"], gain("pre_mix", i), dhn, dh1, out_dtype=F32, after=toks,
                                        name=f"rms_pre_mix_bwd_{i}")
            advance(dh)
    grad_x = dh[None]
    assert not reducing

    n_a, n_b = a_b_in.shape[0], b_sinks.shape[0]
    stack = lambda key, n: jnp.concatenate([small[(key, j)] for j in range(n)], axis=0)
    small_parts = [
        jnp.concatenate(g_pre_mix, axis=0), jnp.concatenate(g_post_mix, axis=0),
        jnp.concatenate(g_pre_ffn, axis=0), jnp.concatenate(g_post_ffn, axis=0),
        stack("a_b_in", n_a), stack("a_ln_g", n_a), stack("a_ln_b", n_a),
        jnp.stack([small[("a_w_s", j)] for j in range(n_a)]), jnp.stack([small[("a_b_s", j)] for j in range(n_a)]),
        stack("b_b_qkv", n_b), stack("b_sinks", n_b), loss_part,
    ]
    packed, metas = _pack_rows(small_parts)
    reduced = packed
    for stage in range(len(_BUTTERFLY)):
        ss, rs, bufs, tok = _split_start(f"butterfly_start_{stage}", [reduced, lax.empty(packed.shape, F32)], 1,
                                         _swap_copies(stage), place)
        done = adamw(updates[stage::len(_BUTTERFLY)], [tok])
        bufs = _split_wait(f"butterfly_wait_{stage}", bufs, (ss, rs), _swap_copies(stage), done or [tok])
        reduced = _add(bufs[0], bufs[1], name=f"butterfly_add_{stage}")
    updates = []
    while last.send(reduced) is not None:
        pass
    adamw(updates, [])
    red = _unpack_rows(reduced, metas)
    (gr_pre_mix, gr_post_mix, gr_pre_ffn, gr_post_ffn, gr_b_in, gr_ln_g, gr_ln_b, gr_w_s, gr_b_s,
     gr_b_qkv_full, gr_sinks, loss_sum) = red
    loss = loss_sum[0, 0]
    gr_b_qkv = lax.dynamic_slice(gr_b_qkv_full, (0, chip * nq), (gr_b_qkv_full.shape[0], nq))

    grads = {"pre_mix_g": gr_pre_mix, "post_mix_g": gr_post_mix, "pre_ffn_g": gr_pre_ffn, "post_ffn_g": gr_post_ffn,
             "a_b_in": gr_b_in, "a_ln_g": gr_ln_g, "a_ln_b": gr_ln_b, "a_w_s": gr_w_s, "a_b_s": gr_b_s,
             "b_b_qkv": gr_b_qkv, "b_sinks": gr_sinks}
    weights = {"pre_mix_g": (pre_mix_g, m_pre_mix_g, v_pre_mix_g), "post_mix_g": (post_mix_g, m_post_mix_g, v_post_mix_g),
               "pre_ffn_g": (pre_ffn_g, m_pre_ffn_g, v_pre_ffn_g), "post_ffn_g": (post_ffn_g, m_post_ffn_g, v_post_ffn_g),
               "a_b_in": (a_b_in, m_a_b_in, v_a_b_in), "a_ln_g": (a_ln_g, m_a_ln_g, v_a_ln_g),
               "a_ln_b": (a_ln_b, m_a_ln_b, v_a_ln_b), "a_w_s": (a_w_s, m_a_w_s, v_a_w_s), "a_b_s": (a_b_s, m_a_b_s, v_a_b_s),
               "b_b_qkv": (b_b_qkv, m_b_b_qkv, v_b_b_qkv), "b_sinks": (b_sinks, m_b_sinks, v_b_sinks)}
    order = ["pre_mix_g", "post_mix_g", "pre_ffn_g", "post_ffn_g", "a_w_in", "a_b_in", "a_ln_g", "a_ln_b", "a_w_s",
             "a_b_s", "a_w_out", "b_w_qkv", "b_b_qkv", "b_sinks", "b_w_o", "ffn_w_gu", "ffn_w_down"]
    deltas, new_m, new_v = {}, {}, {}
    for nm in order:
        if nm in big_out:
            grads[nm], deltas[nm], new_m[nm], new_v[nm] = big_out[nm]
        else:
            w, m, v = weights[nm]
            deltas[nm], new_m[nm], new_v[nm] = _adamw_small(w, grads[nm], m, v, name="adamw_" + nm)
    return (loss, grad_x, *[grads[nm] for nm in order], *[deltas[nm] for nm in order],
            *[new_m[nm] for nm in order], *[new_v[nm] for nm in order])
```

```python
import functools
import math

import jax
import jax.numpy as jnp
import numpy as np
from jax import lax
from jax.experimental import pallas as pl
from jax.experimental.pallas import tpu as pltpu

F32 = jnp.float32
BF16 = jnp.bfloat16
MESH = pl.DeviceIdType.MESH

HEAD_DIM = 64
N_KV_HEADS = 4
ROPE_DIM = 16
ROPE_THETA = 500000.0
CHUNK = 128
GMLP_GROUPS = 8
RMS_EPS = 1e-6
LN_EPS = 1e-5
NEG_INF = -1e30
ADAM_LR = 0.001
ADAM_B1 = 0.9
ADAM_B2 = 0.999
ADAM_EPS = 1e-08
ADAM_WD = 0.01
ADAM_STEP = 10

N_CHIPS = 4
LANES = 128
VMEM_CAP = 58 * 1024 * 1024


def _vmem(est_bytes):
    assert est_bytes < VMEM_CAP
    return VMEM_CAP


def _pick(n, cands):
    for c in cands:
        if c <= n and n % c == 0:
            return c
    return n


def _nbytes(shape, dtype):
    return int(np.prod(shape)) * jnp.dtype(dtype).itemsize


MATMUL_VMEM_BUDGET = 48 * 1024 * 1024
MXU_COLS = 256


def _halvings(n, unit):
    out, t = [], n
    while t % unit == 0 and t >= unit:
        out.append(t)
        if t % 2:
            break
        t //= 2
    return out


def _matmul_tiles(P, Q, R, a_bytes, b_bytes, o_bytes, full_addend, tp, tq, tr, repeat=1):
    step_us, bytes_per_us, flops_per_us = 0.85, 3.2e6, 9.0e8
    best = None
    for p in ([tp] if tp else _halvings(P, LANES)):
        for q in ([tq] if tq else _halvings(Q, LANES)):
            for r in ([tr] if tr else _halvings(R, LANES)):
                nk = R // r
                vm = 2 * (p * r * a_bytes + r * q * b_bytes + p * q * o_bytes + (p * q * 4 if full_addend else 0))
                vm += p * q * 4 * (2 if nk > 1 else 1)
                if vm > MATMUL_VMEM_BUDGET:
                    continue
                exposed = (p * r * a_bytes + r * q * b_bytes + p * q * o_bytes) / bytes_per_us
                mxu_us = repeat * 2.0 * P * R * (Q // q) * (-(-q // MXU_COLS) * MXU_COLS) / flops_per_us
                key = (repeat * (P // p) * (Q // q) * nk * step_us + exposed + mxu_us, nk, abs(p - q))
                if best is None or key < best[0]:
                    best = (key, (p, q, r))
    assert best is not None, (P, Q, R)
    return best[1]


def _matmul(a, b, *, mode, out_dtype, name, bias=None, after=()):
    if mode == "nn":
        (P, R), (R2, Q) = a.shape, b.shape
    elif mode == "nt":
        (P, R), (Q, R2) = a.shape, b.shape
    else:
        (R, P), (R2, Q) = a.shape, b.shape
    assert R == R2, (mode, a.shape, b.shape)
    tp, tq, tr = _matmul_tiles(P, Q, R, a.dtype.itemsize, b.dtype.itemsize, jnp.dtype(out_dtype).itemsize, False,
                               None, None, None)
    nk = R // tr
    dims = {"nn": (((1,), (0,)), ((), ())), "nt": (((1,), (1,)), ((), ())), "tn": (((0,), (0,)), ((), ()))}[mode]
    if mode == "nn":
        a_spec = pl.BlockSpec((tp, tr), lambda i, j, k: (i, k))
        b_spec = pl.BlockSpec((tr, tq), lambda i, j, k: (k, j))
    elif mode == "nt":
        a_spec = pl.BlockSpec((tp, tr), lambda i, j, k: (i, k))
        b_spec = pl.BlockSpec((tq, tr), lambda i, j, k: (j, k))
    else:
        a_spec = pl.BlockSpec((tr, tp), lambda i, j, k: (k, i))
        b_spec = pl.BlockSpec((tr, tq), lambda i, j, k: (k, j))
    in_specs = [a_spec, b_spec]
    args = [a, b]
    has_bias = bias is not None
    if has_bias:
        in_specs.append(pl.BlockSpec((1, tq), lambda i, j, k: (0, j)))
        args.append(bias)
    out_shape = jax.ShapeDtypeStruct((P, Q), out_dtype)
    out_spec = pl.BlockSpec((tp, tq), lambda i, j, k: (i, j))
    n_in = len(args) + len(after)
    in_specs += [pl.BlockSpec(memory_space=pl.ANY)] * len(after)
    args += list(after)

    def body(*refs):
        a_ref, b_ref = refs[0], refs[1]
        bias_ref = refs[2] if has_bias else None
        o_ref = refs[n_in]
        acc_ref = refs[n_in + 1] if nk > 1 else None
        part = lax.dot_general(a_ref[...], b_ref[...], dims, preferred_element_type=F32)

        def finish(acc):
            if has_bias:
                acc = acc + bias_ref[...]
            o_ref[...] = acc.astype(out_dtype)

        if nk == 1:
            finish(part)
        else:
            k = pl.program_id(2)

            @pl.when(k == 0)
            def _():
                acc_ref[...] = part

            @pl.when(k > 0)
            def _():
                acc_ref[...] += part

            @pl.when(k == nk - 1)
            def _():
                finish(acc_ref[...])

    est = 2 * (_nbytes((tp, tr), a.dtype) + _nbytes((tr, tq), b.dtype) + _nbytes((tp, tq), out_dtype)) + 3 * tp * tq * 4
    return pl.pallas_call(
        body, name=name, out_shape=out_shape,
        grid=(P // tp, Q // tq, nk),
        in_specs=in_specs, out_specs=out_spec,
        scratch_shapes=[pltpu.VMEM((tp, tq), F32)] if nk > 1 else [],
        compiler_params=pltpu.CompilerParams(
            dimension_semantics=("parallel", "parallel", "arbitrary"), vmem_limit_bytes=_vmem(est)),
    )(*args)


def _matmul_pair(a, b, pair, *, mode, out_dtype, name, after=()):
    if mode == "tn":
        (R, P), (R2, Q) = a.shape, b.shape
        assert R == R2 and pair.shape == b.shape
        tp, tq, tr = _matmul_tiles(P, Q, R, a.dtype.itemsize, 2 * b.dtype.itemsize,
                                   jnp.dtype(out_dtype).itemsize, False, None, None, None, repeat=2)
        nq, nk = Q // tq, R // tr
        grid, nk_total = (P // tp, 2 * nq, nk), nk
        a_spec = pl.BlockSpec((tr, tp), lambda i, j, k: (k, i))
        b_spec = pl.BlockSpec((tr, tq), lambda i, j, k: (jnp.where(j < nq, k, nk - 1), jnp.minimum(j, nq - 1)))
        p_spec = pl.BlockSpec((tr, tq), lambda i, j, k: (jnp.where(j >= nq, k, 0), jnp.maximum(j - nq, 0)))
        out_shape = (P, 2 * Q)
        dims = (((0,), (0,)), ((), ()))
    else:
        assert mode == "nt"
        (P, R), (Q, R2) = a.shape, b.shape
        assert R2 == 2 * R and pair.shape == a.shape
        tp, tq, tr = _matmul_tiles(P, Q, R, 2 * a.dtype.itemsize, b.dtype.itemsize,
                                   jnp.dtype(out_dtype).itemsize, False, None, None, None, repeat=2)
        nk = R // tr
        grid, nk_total = (P // tp, Q // tq, 2 * nk), 2 * nk
        a_spec = pl.BlockSpec((tp, tr), lambda i, j, k: (i, jnp.minimum(k, nk - 1)))
        p_spec = pl.BlockSpec((tp, tr), lambda i, j, k: (i, jnp.maximum(k - nk, 0)))
        b_spec = pl.BlockSpec((tq, tr), lambda i, j, k: (j, k))
        out_shape = (P, Q)
        dims = (((1,), (1,)), ((), ()))
    n_after = len(after)

    def body(a_ref, b_ref, p_ref, *rest):
        o_ref = rest[n_after]
        acc_ref = rest[n_after + 1] if nk_total > 1 else None
        j, k = pl.program_id(1), pl.program_id(2)

        def step(l_ref, r_ref):
            part = lax.dot_general(l_ref[...], r_ref[...], dims, preferred_element_type=F32)
            if nk_total == 1:
                o_ref[...] = part.astype(out_dtype)
                return

            @pl.when(k == 0)
            def _():
                acc_ref[...] = part

            @pl.when(k > 0)
            def _():
                acc_ref[...] += part

            @pl.when(k == nk_total - 1)
            def _():
                o_ref[...] = acc_ref[...].astype(out_dtype)

        first = (j < nq) if mode == "tn" else (k < nk)

        @pl.when(first)
        def _():
            step(a_ref, b_ref)

        @pl.when(jnp.logical_not(first))
        def _():
            step(a_ref if mode == "tn" else p_ref, p_ref if mode == "tn" else b_ref)

    n_a, n_b = (1, 2) if mode == "tn" else (2, 1)
    est = (2 * (n_a * _nbytes((tp, tr), a.dtype) + n_b * _nbytes((tr, tq), b.dtype) + _nbytes((tp, tq), out_dtype))
           + 2 * tp * tq * 4)
    return pl.pallas_call(
        body, name=name, out_shape=jax.ShapeDtypeStruct(out_shape, out_dtype), grid=grid,
        in_specs=[a_spec, b_spec, p_spec] + [pl.BlockSpec(memory_space=pl.ANY)] * n_after,
        out_specs=pl.BlockSpec((tp, tq), lambda i, j, k: (i, j)),
        scratch_shapes=[pltpu.VMEM((tp, tq), F32)] if nk_total > 1 else [],
        compiler_params=pltpu.CompilerParams(
            dimension_semantics=("parallel", "parallel", "arbitrary"), vmem_limit_bytes=_vmem(est)),
    )(a, b, pair, *after)


def _row_call(body, ins, outs, *, name, rows, tr, acc_outs=(), est=0, after=()):
    in_specs, args = [], []
    for arr, kind in ins:
        if kind == "row":
            in_specs.append(pl.BlockSpec((tr, arr.shape[1]), lambda i: (i, 0)))
        elif isinstance(arr, tuple):
            arr, layer = arr
            in_specs.append(pl.BlockSpec((None,) + arr.shape[1:], lambda i, layer=layer: (layer, 0, 0)))
        else:
            nd = arr.ndim
            in_specs.append(pl.BlockSpec(arr.shape, lambda i, nd=nd: (0,) * nd))
        args.append(arr)
    n_ins = len(args)
    in_specs += [pl.BlockSpec(memory_space=pl.ANY)] * len(after)
    args += list(after)

    def kernel_fn(*refs):
        body(*refs[:n_ins], *refs[n_ins + len(after):])

    out_shapes = [jax.ShapeDtypeStruct(s, d) for s, d in outs] + [jax.ShapeDtypeStruct(s, d) for s, d in acc_outs]
    out_specs = [pl.BlockSpec((tr, s[1]), lambda i: (i, 0)) for s, _ in outs]
    out_specs += [pl.BlockSpec(s, lambda i, nd=len(s): (0,) * nd) for s, _ in acc_outs]
    res = pl.pallas_call(
        kernel_fn, name=name, out_shape=out_shapes, grid=(rows // tr,), in_specs=in_specs, out_specs=out_specs,
        compiler_params=pltpu.CompilerParams(dimension_semantics=("arbitrary",), vmem_limit_bytes=_vmem(est)),
    )(*args)
    return res


def _rms_fwd(x, g, *, out_dtype, name, after=()):
    T, D = x.shape
    tr = _pick(T, (512, 256, 128))

    def body(x_ref, g_ref, o_ref):
        xv = x_ref[...]
        r = lax.rsqrt(jnp.mean(xv * xv, axis=-1, keepdims=True) + RMS_EPS)
        o_ref[...] = (xv * r * g_ref[...]).astype(out_dtype)

    return _row_call(body, [(x, "row"), (g, "full")], [((T, D), out_dtype)], name=name, rows=T, tr=tr,
                     est=8 * tr * D * 4, after=after)[0]


def _rms_res(h, y, g, *, name):
    T, D = h.shape
    tr = _pick(T, (512, 256, 128))

    def body(h_ref, y_ref, g_ref, o_ref):
        yv = y_ref[...]
        r = lax.rsqrt(jnp.mean(yv * yv, axis=-1, keepdims=True) + RMS_EPS)
        o_ref[...] = h_ref[...] + yv * r * g_ref[...]

    return _row_call(body, [(h, "row"), (y, "row"), (g, "full")], [((T, D), F32)], name=name, rows=T, tr=tr,
                     est=10 * tr * D * 4)[0]


def _rms_bwd(x, g, dy, dres, *, out_dtype, name, after=()):
    T, D = x.shape
    tr = _pick(T, (512, 256, 128))
    has_res = dres is not None

    def body(*refs):
        if has_res:
            x_ref, g_ref, dy_ref, dr_ref, dx_ref, dg_ref = refs
        else:
            x_ref, g_ref, dy_ref, dx_ref, dg_ref = refs
        xv = x_ref[...]
        r = lax.rsqrt(jnp.mean(xv * xv, axis=-1, keepdims=True) + RMS_EPS)
        xhat = xv * r
        dyv = dy_ref[...].astype(F32)
        dxn = dyv * g_ref[...]
        dx = r * (dxn - xhat * jnp.mean(dxn * xhat, axis=-1, keepdims=True))
        if has_res:
            dx = dx + dr_ref[...]
        dx_ref[...] = dx.astype(out_dtype)
        part = jnp.sum(dyv * xhat, axis=0, keepdims=True)

        @pl.when(pl.program_id(0) == 0)
        def _():
            dg_ref[...] = part

        @pl.when(pl.program_id(0) > 0)
        def _():
            dg_ref[...] += part

    ins = [(x, "row"), (g, "full"), (dy, "row")] + ([(dres, "row")] if has_res else [])
    dx, dg = _row_call(body, ins, [((T, D), out_dtype)], name=name, rows=T, tr=tr, acc_outs=[((1, D), F32)],
                       est=12 * tr * D * 4, after=after)
    return dx, dg


def _rms_res_norm(h, y, g_res, g_next, *, name, after=()):
    T, D = h.shape
    tr = _pick(T, (512, 256, 128))

    def body(h_ref, y_ref, g_ref, gn_ref, o_ref, n_ref):
        yv = y_ref[...]
        r = lax.rsqrt(jnp.mean(yv * yv, axis=-1, keepdims=True) + RMS_EPS)
        h2 = h_ref[...] + yv * r * g_ref[...]
        o_ref[...] = h2
        r2 = lax.rsqrt(jnp.mean(h2 * h2, axis=-1, keepdims=True) + RMS_EPS)
        n_ref[...] = (h2 * r2 * gn_ref[...]).astype(BF16)

    return _row_call(body, [(h, "row"), (y, "row"), (g_res, "full"), (g_next, "full")],
                     [((T, D), F32), ((T, D), BF16)], name=name, rows=T, tr=tr, est=12 * tr * D * 4, after=after)


def _rms_bwd_chain(x1, g1, dy1, dres, x2, g2, *, name, after=()):
    T, D = x1.shape
    tr = _pick(T, (512, 256, 128))

    def one(xv, gv, dyv):
        r = lax.rsqrt(jnp.mean(xv * xv, axis=-1, keepdims=True) + RMS_EPS)
        xhat = xv * r
        dxn = dyv * gv
        dx = r * (dxn - xhat * jnp.mean(dxn * xhat, axis=-1, keepdims=True))
        return dx, jnp.sum(dyv * xhat, axis=0, keepdims=True)

    def body(x1_ref, g1_ref, dy1_ref, dr_ref, x2_ref, g2_ref, d1_ref, d2_ref, dg1_ref, dg2_ref):
        dx1, p1 = one(x1_ref[...], g1_ref[...], dy1_ref[...].astype(F32))
        d1 = dx1 + dr_ref[...]
        d1_ref[...] = d1
        dx2, p2 = one(x2_ref[...], g2_ref[...], d1)
        d2_ref[...] = dx2.astype(BF16)

        @pl.when(pl.program_id(0) == 0)
        def _():
            dg1_ref[...] = p1
            dg2_ref[...] = p2

        @pl.when(pl.program_id(0) > 0)
        def _():
            dg1_ref[...] += p1
            dg2_ref[...] += p2

    ins = [(x1, "row"), (g1, "full"), (dy1, "row"), (dres, "row"), (x2, "row"), (g2, "full")]
    return _row_call(body, ins, [((T, D), F32), ((T, D), BF16)], name=name, rows=T, tr=tr,
                     acc_outs=[((1, D), F32), ((1, D), F32)], est=20 * tr * D * 4, after=after)


def _ffn_up(fn, w_gu, l, *, name):
    T, D = fn.shape
    H = w_gu.shape[2] // 2
    tp = _pick(T, (256, 128))
    tq = H
    nj = H // tq

    def body(a_ref, wg_ref, wu_ref, g_ref, u_ref, act_ref):
        a = a_ref[...]
        g = jnp.dot(a, wg_ref[...], preferred_element_type=F32)
        u = jnp.dot(a, wu_ref[...], preferred_element_type=F32)
        sg = jax.nn.sigmoid(g)
        silu = g * sg
        g_ref[...] = (u * (sg + silu * (1.0 - sg))).astype(BF16)
        u_ref[...] = silu.astype(BF16)
        act_ref[...] = (silu * u).astype(BF16)

    tile = pl.BlockSpec((tp, tq), lambda j, i: (i, j))
    est = 2 * (tp * D * 2 + 2 * D * tq * 2 + 3 * tp * tq * 2) + 4 * tp * tq * 4
    return pl.pallas_call(
        body, name=name,
        out_shape=[jax.ShapeDtypeStruct((T, H), BF16), jax.ShapeDtypeStruct((T, H), BF16),
                   jax.ShapeDtypeStruct((T, H), BF16)],
        grid=(nj, T // tp),
        in_specs=[pl.BlockSpec((tp, D), lambda j, i: (i, 0)),
                  pl.BlockSpec((None, D, tq), lambda j, i: (l, 0, j)),
                  pl.BlockSpec((None, D, tq), lambda j, i: (l, 0, j + nj))],
        out_specs=[tile, tile, tile],
        compiler_params=pltpu.CompilerParams(dimension_semantics=("parallel", "parallel"),
                                             vmem_limit_bytes=_vmem(est)),
    )(fn, w_gu, w_gu)


def _ffn_down_dx(df, w_down, l, g, u, after, *, name):
    T, D = df.shape
    H = w_down.shape[1]
    tp = _pick(T, (512, 256, 128))
    tq = H

    def body(a_ref, w_ref, g_ref, u_ref, _, dg_ref, du_ref):
        da = lax.dot_general(a_ref[...], w_ref[...], (((1,), (1,)), ((), ())), preferred_element_type=F32)
        dg_ref[...] = (da * g_ref[...].astype(F32)).astype(BF16)
        du_ref[...] = (da * u_ref[...].astype(F32)).astype(BF16)

    tile = pl.BlockSpec((tp, tq), lambda j, i: (i, j))
    est = 2 * (tp * D * 2 + tq * D * 2 + 4 * tp * tq * 2) + 3 * tp * tq * 4
    return pl.pallas_call(
        body, name=name,
        out_shape=[jax.ShapeDtypeStruct((T, H), BF16), jax.ShapeDtypeStruct((T, H), BF16)],
        grid=(H // tq, T // tp),
        in_specs=[pl.BlockSpec((tp, D), lambda j, i: (i, 0)),
                  pl.BlockSpec((None, tq, D), lambda j, i: (l, j, 0)), tile, tile,
                  pl.BlockSpec(memory_space=pl.ANY)],
        out_specs=[tile, tile],
        compiler_params=pltpu.CompilerParams(dimension_semantics=("parallel", "parallel"),
                                             vmem_limit_bytes=_vmem(est)),
    )(df, w_down, g, u, after)


def _loss_and_grad(y, target, x, g, *, name):
    T, D = y.shape
    tr = _pick(T, (512, 256, 128))

    def body(y_ref, t_ref, x_ref, g_ref, dy_ref, dx_ref, l_ref, dg_ref):
        e = y_ref[...] - t_ref[...]
        dy = e * (1.0 / D)
        dy_ref[...] = dy
        part = jnp.sum(jnp.sum(e * e, axis=1, keepdims=True), axis=0, keepdims=True) * (0.5 / D)
        xv = x_ref[...]
        r = lax.rsqrt(jnp.mean(xv * xv, axis=-1, keepdims=True) + RMS_EPS)
        xhat = xv * r
        dxn = dy * g_ref[...]
        dx_ref[...] = (r * (dxn - xhat * jnp.mean(dxn * xhat, axis=-1, keepdims=True))).astype(BF16)
        dg = jnp.sum(dy * xhat, axis=0, keepdims=True)

        @pl.when(pl.program_id(0) == 0)
        def _():
            l_ref[...] = part
            dg_ref[...] = dg

        @pl.when(pl.program_id(0) > 0)
        def _():
            l_ref[...] += part
            dg_ref[...] += dg

    dy, dx, l, dg = _row_call(body, [(y, "row"), (target, "row"), (x, "row"), (g, "full")],
                              [((T, D), F32), ((T, D), BF16)], name=name, rows=T, tr=tr,
                              acc_outs=[((1, 1), F32), ((1, D), F32)], est=14 * tr * D * 4)
    return dy, dx, l, dg


_SQRT_HALF = 0.7071067811865476
_INV_SQRT_2PI = 0.3989422804014327


def _gelu_parts(x):
    cdf = 0.5 * (1.0 + lax.erf(x * _SQRT_HALF))
    return cdf


def _sgu_common(pre, lng, lnb, W):
    cdf = _gelu_parts(pre)
    z = pre * cdf
    u = z[:, :W]
    v = z[:, W:]
    mu = jnp.mean(v, axis=-1, keepdims=True)
    vc = v - mu
    var = jnp.mean(vc * vc, axis=-1, keepdims=True)
    rstd = lax.rsqrt(var + LN_EPS)
    vhat = vc * rstd
    vn = vhat * lng + lnb
    return cdf, u, vhat, rstd, vn


def _causal_mask():
    t = lax.broadcasted_iota(jnp.int32, (CHUNK, CHUNK), 0)
    s = lax.broadcasted_iota(jnp.int32, (CHUNK, CHUNK), 1)
    return t >= s


def _sgu_fwd(pre, lng, lnb, ws, bsT, *, name):
    T, W2 = pre.shape
    W = W2 // 2
    G = ws.shape[0]
    gd = W // G

    def body(pre_ref, lng_ref, lnb_ref, ws_ref, bs_ref, o_ref):
        _, u, _, _, vn = _sgu_common(pre_ref[...], lng_ref[...], lnb_ref[...], W)
        vnb = vn.astype(BF16)
        causal = _causal_mask()
        for g in range(G):
            w = jnp.where(causal, ws_ref[g], 0.0).astype(BF16)
            sv = jnp.dot(w, vnb[:, g * gd:(g + 1) * gd], preferred_element_type=F32) + bs_ref[:, g:g + 1]
            o_ref[:, g * gd:(g + 1) * gd] = (u[:, g * gd:(g + 1) * gd] * sv).astype(BF16)

    return pl.pallas_call(
        body, name=name, out_shape=jax.ShapeDtypeStruct((T, W), BF16), grid=(T // CHUNK,),
        in_specs=[pl.BlockSpec((CHUNK, W2), lambda i: (i, 0)),
                  pl.BlockSpec((1, W), lambda i: (0, 0)), pl.BlockSpec((1, W), lambda i: (0, 0)),
                  pl.BlockSpec(ws.shape, lambda i: (0, 0, 0)), pl.BlockSpec(bsT.shape, lambda i: (0, 0))],
        out_specs=pl.BlockSpec((CHUNK, W), lambda i: (i, 0)),
        compiler_params=pltpu.CompilerParams(dimension_semantics=("arbitrary",),
                                             vmem_limit_bytes=_vmem(12 * CHUNK * W2 * 4)),
    )(pre, lng, lnb, ws, bsT)


def _sgu_bwd(pre, dgated, lng, lnb, ws, bsT, *, name):
    T, W2 = pre.shape
    W = W2 // 2
    G = ws.shape[0]
    gd = W // G

    def body(pre_ref, dgt_ref, lng_ref, lnb_ref, ws_ref, bs_ref,
             dpre_ref, dws_ref, dbs_ref, dlng_ref, dlnb_ref, dbin_ref):
        first = pl.program_id(0) == 0

        @pl.when(first)
        def _():
            dws_ref[...] = jnp.zeros_like(dws_ref)
            dbs_ref[...] = jnp.zeros_like(dbs_ref)
            dlng_ref[...] = jnp.zeros_like(dlng_ref)
            dlnb_ref[...] = jnp.zeros_like(dlnb_ref)
            dbin_ref[...] = jnp.zeros_like(dbin_ref)

        pre_v = pre_ref[...]
        lng_v = lng_ref[...]
        cdf, u, vhat, rstd, vn = _sgu_common(pre_v, lng_v, lnb_ref[...], W)
        vnb = vn.astype(BF16)
        dgt = dgt_ref[...].astype(F32)
        causal = _causal_mask()
        du_parts, dvn_parts = [], []
        for g in range(G):
            sl = slice(g * gd, (g + 1) * gd)
            w = jnp.where(causal, ws_ref[g], 0.0).astype(BF16)
            sv = jnp.dot(w, vnb[:, sl], preferred_element_type=F32) + bs_ref[:, g:g + 1]
            dgt_g = dgt[:, sl]
            du_parts.append(dgt_g * sv)
            dsv = dgt_g * u[:, sl]
            dsvb = dsv.astype(BF16)
            dvn_parts.append(lax.dot_general(w, dsvb, (((0,), (0,)), ((), ())), preferred_element_type=F32))
            dw = lax.dot_general(dsvb, vnb[:, sl], (((1,), (1,)), ((), ())), preferred_element_type=F32)
            dws_ref[g] += jnp.where(causal, dw, 0.0)
            dbs_ref[:, g:g + 1] += jnp.sum(dsv, axis=1, keepdims=True)
        du = jnp.concatenate(du_parts, axis=1)
        dvn = jnp.concatenate(dvn_parts, axis=1)
        dlng_ref[...] += jnp.sum(dvn * vhat, axis=0, keepdims=True)
        dlnb_ref[...] += jnp.sum(dvn, axis=0, keepdims=True)
        dvh = dvn * lng_v
        dv = rstd * (dvh - jnp.mean(dvh, axis=-1, keepdims=True)
                     - vhat * jnp.mean(dvh * vhat, axis=-1, keepdims=True))
        dz = jnp.concatenate([du, dv], axis=1)
        dgelu = cdf + pre_v * jnp.exp(-0.5 * pre_v * pre_v) * _INV_SQRT_2PI
        dpre = dz * dgelu
        dbin_ref[...] += jnp.sum(dpre, axis=0, keepdims=True)
        dpre_ref[...] = dpre.astype(BF16)

    full = lambda shape: pl.BlockSpec(shape, lambda i, nd=len(shape): (0,) * nd)
    return pl.pallas_call(
        body, name=name,
        out_shape=[jax.ShapeDtypeStruct((T, W2), BF16), jax.ShapeDtypeStruct(ws.shape, F32),
                   jax.ShapeDtypeStruct(bsT.shape, F32), jax.ShapeDtypeStruct((1, W), F32),
                   jax.ShapeDtypeStruct((1, W), F32), jax.ShapeDtypeStruct((1, W2), F32)],
        grid=(T // CHUNK,),
        in_specs=[pl.BlockSpec((CHUNK, W2), lambda i: (i, 0)), pl.BlockSpec((CHUNK, W), lambda i: (i, 0)),
                  full((1, W)), full((1, W)), full(ws.shape), full(bsT.shape)],
        out_specs=[pl.BlockSpec((CHUNK, W2), lambda i: (i, 0)), full(ws.shape), full(bsT.shape),
                   full((1, W)), full((1, W)), full((1, W2))],
        compiler_params=pltpu.CompilerParams(dimension_semantics=("arbitrary",),
                                             vmem_limit_bytes=_vmem(24 * CHUNK * W2 * 4)),
    )(pre, dgated, lng, lnb, ws, bsT)


def _rope_tables(positions):
    half = ROPE_DIM // 2
    inv_freq = ROPE_THETA ** (-jnp.arange(0, ROPE_DIM, 2, dtype=F32) / ROPE_DIM)
    ang = positions.astype(F32).reshape(-1, 1) * inv_freq
    cos, sin = jnp.cos(ang), jnp.sin(ang)
    T = ang.shape[0]
    rest = HEAD_DIM - ROPE_DIM
    c64 = jnp.concatenate([cos, cos, jnp.ones((T, rest), F32)], axis=1)
    s64 = jnp.concatenate([-sin, sin, jnp.zeros((T, rest), F32)], axis=1)
    del half
    return jnp.tile(c64, (1, LANES // HEAD_DIM)), jnp.tile(s64, (1, LANES // HEAD_DIM))


def _swap8(x):
    W = x.shape[1]
    half = ROPE_DIM // 2
    lane = lax.broadcasted_iota(jnp.int32, x.shape, 1) % HEAD_DIM
    return jnp.where(lane < half, pltpu.roll(x, W - half, axis=1),
                     jnp.where(lane < ROPE_DIM, pltpu.roll(x, half, axis=1), 0.0))


def _wide(tab, W):
    return jnp.concatenate([tab] * (W // LANES), axis=1) if W > LANES else tab


def _rope_fwd(qkv, ctab, stab, *, q_width, kv_width, name):
    T = qkv.shape[0]
    tr = _pick(T, (256, 128))
    scale = HEAD_DIM ** -0.5

    def body(x_ref, c_ref, s_ref, q_ref, k_ref, v_ref):
        c = c_ref[...]
        s = s_ref[...]
        q = x_ref[:, :q_width]
        k = x_ref[:, q_width:q_width + kv_width]
        q_ref[...] = ((q * _wide(c, q_width) + _swap8(q) * _wide(s, q_width)) * scale).astype(BF16)
        k_ref[...] = (k * _wide(c, kv_width) + _swap8(k) * _wide(s, kv_width)).astype(BF16)
        v_ref[...] = x_ref[:, q_width + kv_width:].astype(BF16)

    return _row_call(body, [(qkv, "row"), (ctab, "row"), (stab, "row")],
                     [((T, q_width), BF16), ((T, kv_width), BF16), ((T, kv_width), BF16)],
                     name=name, rows=T, tr=tr, est=10 * tr * qkv.shape[1] * 4)


_NT = (((1,), (1,)), ((), ()))
_TN = (((0,), (0,)), ((), ()))


def _group_rows(ref, heads):
    return jnp.concatenate([ref[:, h * HEAD_DIM:(h + 1) * HEAD_DIM] for h in heads], axis=0)


def _attn_valid(grp):
    qi = np.arange(grp * CHUNK)[:, None] % CHUNK
    sj = np.arange(2 * CHUNK)[None, :]
    cur = (sj >= CHUNK) & (sj - CHUNK <= qi)
    prev = (sj < CHUNK) & (sj > qi)
    return jnp.asarray(np.stack([cur, cur | prev]).astype(np.float32))


def _valid_spec(grp):
    return pl.BlockSpec((None, grp * CHUNK, 2 * CHUNK), lambda n: (jnp.minimum(n, 1), 0, 0))


def _attn_group_probs(q, kk, sinks, valid, grp):
    rows = grp * CHUNK
    s = lax.dot_general(q, kk, _NT, preferred_element_type=F32)
    s = jnp.where(valid, s, NEG_INF)
    r = lax.broadcasted_iota(jnp.int32, (rows, 1), 0)
    sink = jnp.full((rows, 1), sinks[grp - 1], F32)
    for g in range(grp - 2, -1, -1):
        sink = jnp.where(r < (g + 1) * CHUNK, sinks[g], sink)
    m = jnp.maximum(jnp.max(s, axis=1, keepdims=True), sink)
    p = jnp.exp(s - m)
    ps = jnp.exp(sink - m)
    inv = 1.0 / (jnp.sum(p, axis=1, keepdims=True) + ps)
    return p * inv, ps * inv


def _kv_specs(width, nb):
    prev = pl.BlockSpec((CHUNK, width), lambda n: (jnp.maximum(n - 1, 0), 0))
    cur = pl.BlockSpec((CHUNK, width), lambda n: (n, 0))
    return prev, cur


def _attn_fwd(qr, kr, vr, sinks, *, name):
    T, QW = qr.shape
    KW = kr.shape[1]
    HQ, HK = QW // HEAD_DIM, KW // HEAD_DIM
    grp = HQ // HK
    nb = T // CHUNK

    def body(q_ref, kp_ref, kc_ref, vp_ref, vc_ref, s_ref, ok_ref, o_ref):
        valid = ok_ref[...] > 0.5
        for kh in range(HK):
            ks = slice(kh * HEAD_DIM, (kh + 1) * HEAD_DIM)
            heads = list(range(kh * grp, (kh + 1) * grp))
            q = _group_rows(q_ref, heads)
            kk = jnp.concatenate([kp_ref[:, ks], kc_ref[:, ks]], axis=0)
            vv = jnp.concatenate([vp_ref[:, ks], vc_ref[:, ks]], axis=0)
            p, _ = _attn_group_probs(q, kk, [s_ref[0, h] for h in heads], valid, grp)
            o = jnp.dot(p.astype(BF16), vv, preferred_element_type=F32).astype(BF16)
            for g, h in enumerate(heads):
                o_ref[:, h * HEAD_DIM:(h + 1) * HEAD_DIM] = o[g * CHUNK:(g + 1) * CHUNK]

    kp, kc = _kv_specs(KW, nb)
    return pl.pallas_call(
        body, name=name, out_shape=jax.ShapeDtypeStruct((T, QW), BF16), grid=(nb,),
        in_specs=[pl.BlockSpec((CHUNK, QW), lambda n: (n, 0)), kp, kc, kp, kc,
                  pl.BlockSpec(memory_space=pltpu.SMEM), _valid_spec(grp)],
        out_specs=pl.BlockSpec((CHUNK, QW), lambda n: (n, 0)),
        compiler_params=pltpu.CompilerParams(dimension_semantics=("arbitrary",), vmem_limit_bytes=_vmem(8 << 20)),
    )(qr, kr, kr, vr, vr, sinks, _attn_valid(grp))


def _attn_bwd(qr, kr, vr, sinks, do, *, name):
    T, QW = qr.shape
    KW = kr.shape[1]
    HQ, HK = QW // HEAD_DIM, KW // HEAD_DIM
    grp = HQ // HK
    nb = T // CHUNK

    def body(q_ref, kp_ref, kc_ref, vp_ref, vc_ref, s_ref, do_ref, ok_ref,
             dq_ref, dkp_ref, dkc_ref, dvp_ref, dvc_ref, ds_ref):
        n = pl.program_id(0)
        valid = ok_ref[...] > 0.5
        lane = lax.broadcasted_iota(jnp.int32, (1, LANES), 1)
        dsink = jnp.zeros((1, LANES), F32)
        for kh in range(HK):
            ks = slice(kh * HEAD_DIM, (kh + 1) * HEAD_DIM)
            heads = list(range(kh * grp, (kh + 1) * grp))
            q = _group_rows(q_ref, heads)
            doh = _group_rows(do_ref, heads)
            kk = jnp.concatenate([kp_ref[:, ks], kc_ref[:, ks]], axis=0)
            vv = jnp.concatenate([vp_ref[:, ks], vc_ref[:, ks]], axis=0)
            p, ps = _attn_group_probs(q, kk, [s_ref[0, h] for h in heads], valid, grp)
            dp = lax.dot_general(doh, vv, _NT, preferred_element_type=F32)
            delta = jnp.sum(p * dp, axis=1, keepdims=True)
            ds = (p * (dp - delta)).astype(BF16)
            dv = lax.dot_general(p.astype(BF16), doh, _TN, preferred_element_type=F32)
            dk = lax.dot_general(ds, q, _TN, preferred_element_type=F32)
            dq = jnp.dot(ds, kk, preferred_element_type=F32)
            psd = ps * delta
            for g, h in enumerate(heads):
                dq_ref[:, h * HEAD_DIM:(h + 1) * HEAD_DIM] = dq[g * CHUNK:(g + 1) * CHUNK]
                dsink = dsink + jnp.where(
                    lane == h, -jnp.sum(psd[g * CHUNK:(g + 1) * CHUNK], axis=0, keepdims=True), 0.0)
            dkp_ref[:, ks] = dk[:CHUNK]
            dkc_ref[:, ks] = dk[CHUNK:]
            dvp_ref[:, ks] = dv[:CHUNK]
            dvc_ref[:, ks] = dv[CHUNK:]

        @pl.when(n == 0)
        def _():
            ds_ref[...] = dsink

        @pl.when(n > 0)
        def _():
            ds_ref[...] += dsink

    kp, kc = _kv_specs(KW, nb)
    qspec = pl.BlockSpec((CHUNK, QW), lambda n: (n, 0))
    kout = pl.BlockSpec((CHUNK, KW), lambda n: (n, 0))
    return pl.pallas_call(
        body, name=name,
        out_shape=[jax.ShapeDtypeStruct((T, QW), F32)] + [jax.ShapeDtypeStruct((T, KW), F32)] * 4
        + [jax.ShapeDtypeStruct((1, LANES), F32)],
        grid=(nb,),
        in_specs=[qspec, kp, kc, kp, kc, pl.BlockSpec(memory_space=pltpu.SMEM), qspec, _valid_spec(grp)],
        out_specs=[qspec, kout, kout, kout, kout, pl.BlockSpec((1, LANES), lambda n: (0, 0))],
        compiler_params=pltpu.CompilerParams(dimension_semantics=("arbitrary",), vmem_limit_bytes=_vmem(12 << 20)),
    )(qr, kr, kr, vr, vr, sinks, do, _attn_valid(grp))


def _rope_bwd(dq, dkp, dkc, dvp, dvc, ctab, stab, *, name):
    T, QW = dq.shape
    KW = dkp.shape[1]
    nb = T // CHUNK
    scale = HEAD_DIM ** -0.5
    width = QW + 2 * KW

    def body(dq_ref, dkc_ref, dkn_ref, dvc_ref, dvn_ref, c_ref, s_ref, o_ref, db_ref):
        n = pl.program_id(0)
        c = c_ref[...]
        s = s_ref[...]
        has_next = (n < nb - 1).astype(F32)
        dqv = dq_ref[...]
        dk = dkc_ref[...] + has_next * dkn_ref[...]
        dv = dvc_ref[...] + has_next * dvn_ref[...]
        dq_pre = (dqv * _wide(c, QW) + _swap8(dqv * _wide(s, QW))) * scale
        dk_pre = dk * _wide(c, KW) + _swap8(dk * _wide(s, KW))
        o_ref[:, :QW] = dq_pre.astype(BF16)
        o_ref[:, QW:QW + KW] = dk_pre.astype(BF16)
        o_ref[:, QW + KW:] = dv.astype(BF16)
        part = jnp.concatenate([jnp.sum(dq_pre, axis=0, keepdims=True), jnp.sum(dk_pre, axis=0, keepdims=True),
                                jnp.sum(dv, axis=0, keepdims=True)], axis=1)

        @pl.when(n == 0)
        def _():
            db_ref[...] = part

        @pl.when(n > 0)
        def _():
            db_ref[...] += part

    cur = lambda w: pl.BlockSpec((CHUNK, w), lambda n: (n, 0))
    nxt = lambda w: pl.BlockSpec((CHUNK, w), lambda n: (jnp.minimum(n + 1, nb - 1), 0))
    return pl.pallas_call(
        body, name=name,
        out_shape=[jax.ShapeDtypeStruct((T, width), BF16), jax.ShapeDtypeStruct((1, width), F32)],
        grid=(nb,),
        in_specs=[cur(QW), cur(KW), nxt(KW), cur(KW), nxt(KW), cur(LANES), cur(LANES)],
        out_specs=[cur(width), pl.BlockSpec((1, width), lambda n: (0, 0))],
        compiler_params=pltpu.CompilerParams(dimension_semantics=("arbitrary",), vmem_limit_bytes=_vmem(8 << 20)),
    )(dq, dkc, dkp, dvc, dvp, ctab, stab)


def _cast_block(w, l, axis, chip_arr, *, name):
    _, Ks, Ns = w.shape
    tk = _pick(Ks, (512, 352, 256, 128))
    nk = Ks // tk
    full = (Ks * N_CHIPS, Ns) if axis == 0 else (Ks, Ns * N_CHIPS)

    def body(p_ref, w_ref, o_ref):
        o_ref[...] = w_ref[...].astype(BF16)

    if axis == 0:
        out_spec = pl.BlockSpec((tk, Ns), lambda i, p: (p[0] * nk + i, 0))
    else:
        out_spec = pl.BlockSpec((tk, Ns), lambda i, p: (i, p[0]))
    grid_spec = pltpu.PrefetchScalarGridSpec(
        num_scalar_prefetch=1, grid=(nk,),
        in_specs=[pl.BlockSpec((None, tk, Ns), lambda i, p: (l, i, 0))], out_specs=out_spec)
    return pl.pallas_call(
        body, name=name, out_shape=jax.ShapeDtypeStruct(full, BF16), grid_spec=grid_spec,
        compiler_params=pltpu.CompilerParams(dimension_semantics=("arbitrary",),
                                             vmem_limit_bytes=_vmem(4 * tk * Ns * 6)),
    )(chip_arr, w)


def _adamw_math(w, g, m, v):
    m = ADAM_B1 * m + (1.0 - ADAM_B1) * g
    v = ADAM_B2 * v + (1.0 - ADAM_B2) * (g * g)
    m_hat = m / (1.0 - ADAM_B1 ** ADAM_STEP)
    v_hat = v / (1.0 - ADAM_B2 ** ADAM_STEP)
    delta = -ADAM_LR * (m_hat / (jnp.sqrt(v_hat) + ADAM_EPS) + ADAM_WD * w)
    return delta, m, v


def _adamw_layer(w, m, v, g, l, outs, *, name, after=()):
    _, K, N = w.shape
    tk = _pick(K, (512, 352, 256, 128)) if N <= 1024 else _pick(K, (256, 176, 128))
    n_after = len(after)

    def body(w_ref, m_ref, v_ref, g_ref, *rest):
        go_ref, d_ref, mo_ref, vo_ref = rest[4 + n_after:]
        gv = g_ref[...]
        d, mn, vn = _adamw_math(w_ref[...], gv, m_ref[...], v_ref[...])
        go_ref[...] = gv
        d_ref[...] = d
        mo_ref[...] = mn
        vo_ref[...] = vn

    layer = pl.BlockSpec((None, tk, N), lambda i: (l, i, 0))
    any_spec = pl.BlockSpec(memory_space=pl.ANY)
    sd = jax.ShapeDtypeStruct(w.shape, F32)
    return pl.pallas_call(
        body, name=name, out_shape=[sd, sd, sd, sd], grid=(K // tk,),
        in_specs=[layer, layer, layer, pl.BlockSpec((tk, N), lambda i: (i, 0))] + [any_spec] * (4 + n_after),
        out_specs=[layer] * 4, input_output_aliases={4: 0, 5: 1, 6: 2, 7: 3},
        compiler_params=pltpu.CompilerParams(dimension_semantics=("arbitrary",),
                                             vmem_limit_bytes=_vmem(2 * 8 * tk * N * 4 + 6 * tk * N * 4)),
    )(w, m, v, g, *outs, *after)


def _adamw_small(w, g, m, v, *, name):
    def body(w_ref, g_ref, m_ref, v_ref, d_ref, mo_ref, vo_ref):
        d, mn, vn = _adamw_math(w_ref[...], g_ref[...], m_ref[...], v_ref[...])
        d_ref[...] = d
        mo_ref[...] = mn
        vo_ref[...] = vn

    sd = jax.ShapeDtypeStruct(w.shape, F32)
    return pl.pallas_call(body, name=name, out_shape=[sd, sd, sd])(w, g, m, v)


def _my_place():
    return lax.axis_index("x"), lax.axis_index("y"), lax.axis_index("c")


def _peer_chips(x, y):
    return [(1 - x, y), (x, 1 - y), (1 - x, 1 - y)]


_HBM = pl.BlockSpec(memory_space=pltpu.HBM)
_SEM = pl.BlockSpec(memory_space=pltpu.SEMAPHORE)
_EFFECT = pltpu.SideEffectType.DATAFLOW_SIDE_EFFECTING


def _split_start(name, bufs, n_copies, make_copies, after):
    nb = len(bufs)

    def body(*refs):
        send_sems, recv_sems = refs[nb + 1], refs[nb + 2]
        token = refs[2 * nb + 3]
        sends, _ = make_copies(refs[:nb], send_sems, recv_sems)
        for cp in sends:
            cp.start()
        token[...] = jnp.zeros_like(token)

    res = pl.pallas_call(
        body, name=name,
        out_shape=(pltpu.SemaphoreType.DMA((n_copies,)), pltpu.SemaphoreType.DMA((n_copies,)),
                   *[pltpu.HBM(b.shape, b.dtype) for b in bufs], jax.ShapeDtypeStruct((8, LANES), F32)),
        in_specs=[_HBM] * nb + [pl.BlockSpec(memory_space=pl.ANY)],
        out_specs=(_SEM, _SEM, *[_HBM] * nb, pl.BlockSpec(memory_space=pltpu.VMEM)),
        input_output_aliases={k: 2 + k for k in range(nb)},
        compiler_params=pltpu.CompilerParams(has_side_effects=_EFFECT),
    )(*[pltpu.with_memory_space_constraint(b, pltpu.HBM) for b in bufs],
      after[0] if isinstance(after, (list, tuple)) else after)
    return res[0], res[1], list(res[2:2 + nb]), res[2 + nb]


def _split_wait(name, bufs, sems, make_copies, after):
    nb = len(bufs)
    after = list(after) if isinstance(after, (list, tuple)) else [after]

    def body(*refs):
        send_sems, recv_sems = refs[nb], refs[nb + 1]
        sends, recvs = make_copies(refs[:nb], send_sems, recv_sems)
        for cp in sends:
            cp.wait_send()
        for cp in recvs:
            cp.wait_recv()

    res = pl.pallas_call(
        body, name=name,
        out_shape=tuple(pltpu.HBM(b.shape, b.dtype) for b in bufs),
        in_specs=[_HBM] * nb + [_SEM, _SEM] + [pl.BlockSpec(memory_space=pl.ANY)] * len(after),
        out_specs=tuple([_HBM] * nb),
        input_output_aliases={k: k for k in range(nb)},
        compiler_params=pltpu.CompilerParams(has_side_effects=_EFFECT),
    )(*bufs, sems[0], sems[1], *after)
    return list(res)


def _remote(src, dst, send_sems, recv_sems, k, target):
    return pltpu.make_async_remote_copy(src_ref=src, dst_ref=dst, send_sem=send_sems.at[k],
                                        recv_sem=recv_sems.at[k], device_id=target, device_id_type=MESH)


def _ag_region(ref, axis, chip, half):
    K, N = ref.shape
    if axis == 0:
        hs = K // N_CHIPS // 2
        assert hs % 16 == 0
        return ref.at[pl.ds(pl.multiple_of((2 * chip + half) * hs, 16), hs), :]
    ns, hk = N // N_CHIPS, K // 2
    assert ns % LANES == 0 and hk % 16 == 0
    return ref.at[pl.ds(pl.multiple_of(half * hk, 16), hk), pl.ds(pl.multiple_of(chip * ns, LANES), ns)]


def _ag_copies(stage, axes):
    n = len(axes)

    def make(bufs, send_sems, recv_sems):
        x, y, c = _my_place()
        me = 2 * x + y
        sends, recvs = [], []
        for j, (px, py) in enumerate(_peer_chips(x, y)):
            other = 2 * px + py
            for w in range(n):
                k = j * n + w
                if stage == 1:
                    src, target = _ag_region(bufs[w], axes[w], me, c), (px, py, c)
                    land = _ag_region(bufs[w], axes[w], other, c)
                else:
                    src, target = _ag_region(bufs[w], axes[w], other, c), (x, y, 1 - c)
                    land = _ag_region(bufs[w], axes[w], other, 1 - c)
                sends.append(_remote(src, src, send_sems, recv_sems, k, target))
                recvs.append(_remote(land, land, send_sems, recv_sems, k, target))
        return sends, recvs

    return make


def _half_shape(shape, axis):
    K, N = shape
    return (K, N // 2) if axis == 0 else (K // 2, N)


def _core_half(ref, axis, half):
    K, N = ref.shape
    if axis == 0:
        return ref.at[:, pl.ds(pl.multiple_of(half * (N // 2), LANES), N // 2)]
    return ref.at[pl.ds(pl.multiple_of(half * (K // 2), 16), K // 2), :]


def _chip_block(ref, axis, chip):
    K, N = ref.shape
    if axis == 0:
        return ref.at[pl.ds(pl.multiple_of(chip * (K // N_CHIPS), 16), K // N_CHIPS), :]
    return ref.at[:, pl.ds(pl.multiple_of(chip * (N // N_CHIPS), LANES), N // N_CHIPS)]


def _rs_sibling_copies(axes):
    n = len(axes)

    def make(bufs, send_sems, recv_sems):
        x, y, c = _my_place()
        sends = [_remote(_core_half(bufs[w], axes[w], 1 - c), bufs[n + w], send_sems, recv_sems, w, (x, y, 1 - c))
                 for w in range(n)]
        recvs = [_remote(bufs[n + w], bufs[n + w], send_sems, recv_sems, w, (x, y, 1 - c)) for w in range(n)]
        return sends, recvs

    return make


def _rs_chip_copies(axes):
    n = len(axes)

    def make(bufs, send_sems, recv_sems):
        x, y, c = _my_place()
        sends, recvs = [], []
        for j, (px, py) in enumerate(_peer_chips(x, y)):
            for w in range(n):
                k = j * n + w
                sends.append(_remote(_chip_block(bufs[w], axes[w], 2 * px + py), bufs[n + w].at[j],
                                     send_sems, recv_sems, k, (px, py, c)))
                recvs.append(_remote(bufs[n + w].at[j], bufs[n + w].at[j], send_sems, recv_sems, k, (px, py, c)))
        return sends, recvs

    return make


def _rs_fill_copies(axes):
    n = len(axes)

    def make(bufs, send_sems, recv_sems):
        x, y, c = _my_place()
        sends = [_remote(_core_half(bufs[w], axes[w], c), _core_half(bufs[w], axes[w], c),
                         send_sems, recv_sems, w, (x, y, 1 - c)) for w in range(n)]
        recvs = [_remote(_core_half(bufs[w], axes[w], 1 - c), _core_half(bufs[w], axes[w], 1 - c),
                         send_sems, recv_sems, w, (x, y, 1 - c)) for w in range(n)]
        return sends, recvs

    return make


def _chip_sum(g, r, axis, place, *, name):
    hk, hn = r.shape
    bk, bn = (hk // N_CHIPS, hn) if axis == 0 else (hk, hn // N_CHIPS)
    tk = _pick(bk, (512, 352, 256, 128))
    nk = bk // tk

    def body(p_ref, g_ref, r_ref, b_ref, own_ref):
        s = g_ref[...].astype(F32) + r_ref[...].astype(F32)
        b_ref[...] = s.astype(BF16)

        @pl.when(pl.program_id(1) == p_ref[0])
        def _():
            own_ref[...] = s

    if axis == 0:
        g_spec = pl.BlockSpec((tk, bn), lambda i, j, p: (j * nk + i, p[1]))
        r_spec = pl.BlockSpec((tk, bn), lambda i, j, p: (j * nk + i, 0))
    else:
        g_spec = pl.BlockSpec((tk, bn), lambda i, j, p: (p[1] * nk + i, j))
        r_spec = pl.BlockSpec((tk, bn), lambda i, j, p: (i, j))
    grid_spec = pltpu.PrefetchScalarGridSpec(
        num_scalar_prefetch=1, grid=(nk, N_CHIPS), in_specs=[g_spec, r_spec],
        out_specs=[r_spec, pl.BlockSpec((tk, bn), lambda i, j, p: (i, 0))])
    return pl.pallas_call(
        body, name=name,
        out_shape=[jax.ShapeDtypeStruct(r.shape, BF16), jax.ShapeDtypeStruct((bk, bn), F32)],
        grid_spec=grid_spec,
        compiler_params=pltpu.CompilerParams(dimension_semantics=("arbitrary", "arbitrary"),
                                             vmem_limit_bytes=_vmem(2 * tk * bn * 10 + 3 * tk * bn * 4)),
    )(place, g, r)


def _final_sum(own, recv, axis, place, *, name):
    _, bk, bn = recv.shape
    tk = _pick(bk, (256, 176, 128))
    nk = bk // tk

    def body(p_ref, o_ref, r_ref, out_ref):
        out_ref[...] = ((o_ref[...] + r_ref[0].astype(F32)) + r_ref[1].astype(F32)) + r_ref[2].astype(F32)

    own_spec = pl.BlockSpec((tk, bn), lambda i, p: (i, 0))
    if axis == 0:
        out_shape, out_spec = (bk, 2 * bn), pl.BlockSpec((tk, bn), lambda i, p: (i, p[1]))
    else:
        out_shape, out_spec = (2 * bk, bn), pl.BlockSpec((tk, bn), lambda i, p: (p[1] * nk + i, 0))
    grid_spec = pltpu.PrefetchScalarGridSpec(
        num_scalar_prefetch=1, grid=(nk,),
        in_specs=[own_spec, pl.BlockSpec((3, tk, bn), lambda i, p: (0, i, 0))], out_specs=out_spec)
    return pl.pallas_call(
        body, name=name, out_shape=jax.ShapeDtypeStruct(out_shape, F32), grid_spec=grid_spec,
        compiler_params=pltpu.CompilerParams(dimension_semantics=("arbitrary",),
                                             vmem_limit_bytes=_vmem(2 * tk * bn * 14 + 4 * tk * bn * 4)),
    )(place, own, recv)


def _allreduce_small(p, after=()):
    n_after = len(after)

    def body(*refs):
        p_ref = refs[0]
        o_ref, r0, r1, r2, send_sems, recv_sems = refs[1 + n_after:]
        x, y, c = _my_place()
        o_ref[...] = p_ref[...]
        for s, (peer, rbuf) in enumerate([((x, y, 1 - c), r0), ((1 - x, y, c), r1), ((x, 1 - y, c), r2)]):
            cp = pltpu.make_async_remote_copy(src_ref=o_ref, dst_ref=rbuf, send_sem=send_sems.at[s],
                                              recv_sem=recv_sems.at[s], device_id=peer, device_id_type=MESH)
            cp.start()
            cp.wait()
            o_ref[...] = o_ref[...] + rbuf[...]

    vm = pl.BlockSpec(memory_space=pltpu.VMEM)
    return pl.pallas_call(
        body, name="allreduce_small", out_shape=jax.ShapeDtypeStruct(p.shape, F32),
        in_specs=[vm] + [pl.BlockSpec(memory_space=pl.ANY)] * n_after, out_specs=vm,
        scratch_shapes=[pltpu.VMEM(p.shape, F32)] * 3 + [pltpu.SemaphoreType.DMA((3,))] * 2,
        compiler_params=pltpu.CompilerParams(vmem_limit_bytes=_vmem(6 * _nbytes(p.shape, F32))),
    )(p, *after)


_BUTTERFLY = (lambda x, y, c: (x, y, 1 - c), lambda x, y, c: (1 - x, y, c), lambda x, y, c: (x, 1 - y, c))


def _swap_copies(stage):
    def make(bufs, send_sems, recv_sems):
        target = _BUTTERFLY[stage](*_my_place())
        return ([_remote(bufs[0], bufs[1], send_sems, recv_sems, 0, target)],
                [_remote(bufs[1], bufs[1], send_sems, recv_sems, 0, target)])

    return make


def _add(a, b, *, name):
    def body(a_ref, b_ref, o_ref):
        o_ref[...] = a_ref[...] + b_ref[...]

    return pl.pallas_call(body, name=name, out_shape=jax.ShapeDtypeStruct(a.shape, a.dtype))(a, b)


def _pack_rows(parts):
    rows, metas = [], []
    for a in parts:
        flat = a.reshape(-1)
        nrow = -(-flat.shape[0] // LANES)
        nrow = -(-nrow // 8) * 8
        flat = jnp.pad(flat, (0, nrow * LANES - flat.shape[0]))
        rows.append(flat.reshape(nrow, LANES))
        metas.append((a.shape, nrow))
    return jnp.concatenate(rows, axis=0), metas


def _unpack_rows(packed, metas):
    out, r0 = [], 0
    for shape, nrow in metas:
        size = int(np.prod(shape))
        out.append(packed[r0:r0 + nrow].reshape(-1)[:size].reshape(shape))
        r0 += nrow
    return out


def kernel(x, positions, pre_mix_g, post_mix_g, pre_ffn_g, post_ffn_g, a_w_in, a_b_in, a_ln_g, a_ln_b, a_w_s, a_b_s, a_w_out, b_w_qkv, b_b_qkv, b_sinks, b_w_o, ffn_w_gu, ffn_w_down, loss_target, m_pre_mix_g, m_post_mix_g, m_pre_ffn_g, m_post_ffn_g, m_a_w_in, m_a_b_in, m_a_ln_g, m_a_ln_b, m_a_w_s, m_a_b_s, m_a_w_out, m_b_w_qkv, m_b_b_qkv, m_b_sinks, m_b_w_o, m_ffn_w_gu, m_ffn_w_down, v_pre_mix_g, v_post_mix_g, v_pre_ffn_g, v_post_ffn_g, v_a_w_in, v_a_b_in, v_a_ln_g, v_a_ln_b, v_a_w_s, v_a_b_s, v_a_w_out, v_b_w_qkv, v_b_b_qkv, v_b_sinks, v_b_w_o, v_ffn_w_gu, v_ffn_w_down):
    depth, D = pre_mix_g.shape
    xi, yi, ci = _my_place()
    chip = 2 * xi + yi
    place = jnp.stack([chip, ci]).astype(jnp.int32)

    stacked = {"a_w_in": (a_w_in, m_a_w_in, v_a_w_in), "a_w_out": (a_w_out, m_a_w_out, v_a_w_out),
               "b_w_qkv": (b_w_qkv, m_b_w_qkv, v_b_w_qkv), "b_w_o": (b_w_o, m_b_w_o, v_b_w_o),
               "ffn_w_gu": (ffn_w_gu, m_ffn_w_gu, v_ffn_w_gu), "ffn_w_down": (ffn_w_down, m_ffn_w_down, v_ffn_w_down)}
    cut = {"a_w_in": 1, "a_w_out": 0, "b_w_qkv": 1, "b_w_o": 0, "ffn_w_gu": 1, "ffn_w_down": 0}

    def layer_keys(i):
        mix = [("a_w_in", i // 2), ("a_w_out", i // 2)] if i % 2 == 0 else [("b_w_qkv", i // 2), ("b_w_o", i // 2)]
        return mix + [("ffn_w_gu", i), ("ffn_w_down", i)]

    def dep(a, toks):
        for t in toks:
            a = a + t[:1, :1]
        return a

    W = {}
    for i in range(depth):
        for nm, l in layer_keys(i):
            W[(nm, l)] = _cast_block(stacked[nm][0], l, cut[nm], place, name=f"cast_{nm}_{l}")

    def gather(tag, keys, after):
        axes = [cut[nm] for nm, _ in keys]
        for stage in (1, 2):
            ss, rs, bufs, tok = _split_start(f"ag{stage}_start_{tag}", [W[k] for k in keys], 3 * len(keys),
                                             _ag_copies(stage, axes), after)
            after = yield tok
            bufs = _split_wait(f"ag{stage}_wait_{tag}", bufs, (ss, rs), _ag_copies(stage, axes), after)
            W.update(zip(keys, bufs))
        yield None

    nq = b_b_qkv.shape[1]
    bq_full = jnp.zeros((b_b_qkv.shape[0], N_CHIPS * nq), F32)
    bq_full = lax.dynamic_update_slice(bq_full, jnp.where(ci == 0, b_b_qkv, 0.0), (0, chip * nq))
    bq_packed, bq_meta = _pack_rows([bq_full])
    first = gather("0m", layer_keys(0)[:2], place)
    tok = next(first)
    bq_gathered = _allreduce_small(bq_packed, after=[tok])
    b_qkv_full = _unpack_rows(bq_gathered, bq_meta)[0]
    tok = first.send([tok, bq_gathered] + [W[k] for i in range(depth) for k in layer_keys(i)[2 if i == 0 else 0:]])
    first.send(tok)

    h = x[0]
    target = loss_target[0]
    ctab, stab = _rope_tables(positions[0])
    q_width = W[("b_w_o", 0)].shape[0]
    kv_width = N_KV_HEADS * HEAD_DIM
    row = lambda a, i: a[i:i + 1]
    gains = {"pre_mix": pre_mix_g[:, None], "post_mix": post_mix_g[:, None], "pre_ffn": pre_ffn_g[:, None],
             "post_ffn": post_ffn_g[:, None]}
    gain = lambda which, i: (gains[which], i)

    saved = []
    hn = None
    for i in range(depth):
        j = i // 2
        s = {"h": h}
        ffn_w = None
        if i == 0:
            ffn_w = gather("0f", layer_keys(0)[2:], W[("a_w_out", 0)])
            toks = [next(ffn_w)]
            nxt = gather("1", layer_keys(1), toks[0])
            toks.append(next(nxt))
            hn = _rms_fwd(h, gain("pre_mix", i), out_dtype=BF16, after=toks, name=f"rms_pre_mix_{i}")
        elif i + 1 < depth:
            nxt = gather(str(i + 1), layer_keys(i + 1), h)
            toks = [next(nxt)]
        else:
            toks = []
        s["hn"] = hn
        if i % 2 == 0:
            pre = _matmul(hn, W[("a_w_in", j)], mode="nn", bias=row(a_b_in, j), out_dtype=F32, after=toks,
                          name=f"gmlp_in_{i}")
            gated = _sgu_fwd(pre, row(a_ln_g, j), row(a_ln_b, j), a_w_s[j], a_b_s[j].T, name=f"sgu_fwd_{i}")
            mix = _matmul(gated, W[("a_w_out", j)], mode="nn", out_dtype=F32, name=f"gmlp_out_{i}")
            s.update(pre=pre, gated=gated)
        else:
            qkv = _matmul(hn, W[("b_w_qkv", j)], mode="nn", bias=row(b_qkv_full, j), out_dtype=F32, after=toks,
                          name=f"attn_qkv_{i}")
            qr, kr, vr = _rope_fwd(qkv, ctab, stab, q_width=q_width, kv_width=kv_width, name=f"rope_fwd_{i}")
            o = _attn_fwd(qr, kr, vr, row(b_sinks, j), name=f"attn_fwd_{i}")
            mix = _matmul(o, W[("b_w_o", j)], mode="nn", out_dtype=F32, name=f"attn_o_{i}")
            s.update(qr=qr, kr=kr, vr=vr, o=o)
        s["mix"] = mix
        toks = [ffn_w.send(mix)] if ffn_w else []
        h1, fn = _rms_res_norm(h, mix, gain("post_mix", i), gain("pre_ffn", i), after=toks, name=f"rms_post_mix_{i}")
        if ffn_w:
            ffn_w.send(h1)
        s["h1"] = h1
        g_pre, u_pre, act = _ffn_up(fn, W[("ffn_w_gu", i)][None], 0, name=f"ffn_up_{i}")
        f = _matmul(act, W[("ffn_w_down", i)], mode="nn", out_dtype=F32, name=f"ffn_down_{i}")
        if i + 1 < depth:
            toks = [nxt.send(f)]
            h, hn = _rms_res_norm(h1, f, gain("post_ffn", i), gain("pre_mix", i + 1), after=toks,
                                  name=f"rms_post_ffn_{i}")
            nxt.send(h)
        else:
            h = _rms_res(h1, f, gain("post_ffn", i), name=f"rms_post_ffn_{i}")
        s.update(fn=fn, g_pre=g_pre, u_pre=u_pre, act=act, f=f)
        saved.append(s)

    dh, df, loss_part, g_last = _loss_and_grad(h, target, saved[-1]["f"], gain("post_ffn", depth - 1), name="loss")

    big_out = {nm: tuple(lax.empty(w.shape, F32) for _ in range(4)) for nm, (w, _, _) in stacked.items()}

    def reduce_group(i, keys, grads):
        axes = [cut[nm] for nm, _ in keys]
        n = len(keys)
        lands = [lax.empty(_half_shape(g.shape, ax), BF16) for g, ax in zip(grads, axes)]
        ss, rs, bufs, tok = _split_start(f"rs_sibling_start_{i}", list(grads) + lands, n, _rs_sibling_copies(axes),
                                         place)
        after = yield tok
        bufs = _split_wait(f"rs_sibling_wait_{i}", bufs, (ss, rs), _rs_sibling_copies(axes), after)
        sums = [_chip_sum(bufs[w], bufs[n + w], axes[w], place, name=f"chip_sum_{keys[w][0]}_{keys[w][1]}")
                for w in range(n)]
        lands = [lax.empty((3,) + own.shape, BF16) for _, own in sums]
        ss, rs, bufs, tok = _split_start(f"rs_chip_start_{i}", [sb for sb, _ in sums] + lands, 3 * n,
                                         _rs_chip_copies(axes), place)
        after = yield tok
        bufs = _split_wait(f"rs_chip_wait_{i}", bufs, (ss, rs), _rs_chip_copies(axes), after)
        blocks = [_final_sum(sums[w][1], bufs[n + w], axes[w], place, name=f"final_sum_{keys[w][0]}_{keys[w][1]}")
                  for w in range(n)]
        ss, rs, bufs, tok = _split_start(f"rs_fill_start_{i}", blocks, n, _rs_fill_copies(axes), place)
        after = yield tok
        blocks = _split_wait(f"rs_fill_wait_{i}", bufs, (ss, rs), _rs_fill_copies(axes), after)
        updates.extend(zip(keys, blocks))
        yield None

    updates = []

    def adamw(items, after):
        for (nm, l), g in items:
            w, m, v = stacked[nm]
            big_out[nm] = tuple(_adamw_layer(w, m, v, g, l, big_out[nm], after=after, name=f"adamw_{nm}_{l}"))
        return [big_out[nm][1] for nm in dict.fromkeys(nm for (nm, _), _ in items)]

    reducing = []

    def advance(after, newest_only=False):
        toks = []
        for gen in (reducing[-1:] if newest_only else list(reducing)):
            tok = gen.send(after)
            if tok is None:
                reducing.remove(gen)
            else:
                toks.append(tok)
        return toks

    small = {}
    g_pre_mix, g_post_mix, g_pre_ffn, g_post_ffn = [None] * depth, [None] * depth, [None] * depth, [None] * depth
    g_post_ffn[depth - 1] = g_last
    toks = []
    early = []
    for i in reversed(range(depth)):
        j = i // 2
        s = saved[i]
        g_down = _matmul(s["act"], df, mode="tn", out_dtype=BF16, after=toks, name=f"ffn_down_dw_{i}")
        dg_, du_ = _ffn_down_dx(df, W[("ffn_w_down", i)][None], 0, s["g_pre"], s["u_pre"], g_down,
                                name=f"ffn_down_dx_{i}")
        g_gu = _matmul_pair(s["fn"], dg_, du_, mode="tn", out_dtype=BF16, name=f"ffn_gu_dw_{i}")
        dfn = _matmul_pair(dg_, W[("ffn_w_gu", i)], du_, mode="nt", out_dtype=F32, after=[g_gu],
                           name=f"ffn_gu_dx_{i}")
        toks = advance(dfn)
        if i == 0:
            gen = reduce_group("0f", layer_keys(0)[2:], [g_gu, g_down])
            toks.append(next(gen))
            reducing.append(gen)
        dh1, dmix, g_pre_ffn[i], g_post_mix[i] = _rms_bwd_chain(
            s["h1"], gain("pre_ffn", i), dfn, dh, s["mix"], gain("post_mix", i), after=toks,
            name=f"rms_ffn_mix_bwd_{i}")
        if i % 2 == 0:
            g_out = _matmul(s["gated"], dmix, mode="tn", out_dtype=BF16, name=f"gmlp_out_dw_{i}")
            dgated = _matmul(dmix, W[("a_w_out", j)], mode="nt", out_dtype=BF16, after=[g_out],
                             name=f"gmlp_out_dx_{i}")
            toks = advance(dgated, newest_only=True) if i == 0 else []
            dpre, dws, dbsT, dlng, dlnb, dbin = _sgu_bwd(s["pre"], dgated, dep(row(a_ln_g, j), toks), row(a_ln_b, j),
                                                         a_w_s[j], a_b_s[j].T, name=f"sgu_bwd_{i}")
            small[("a_w_s", j)] = dws
            small[("a_b_s", j)] = dbsT.T
            small[("a_ln_g", j)] = dlng
            small[("a_ln_b", j)] = dlnb
            small[("a_b_in", j)] = dbin
            g_in = _matmul(s["hn"], dpre, mode="tn", out_dtype=BF16, name=f"gmlp_in_dw_{i}")
            if i == 0:
                last = reduce_group("0m", layer_keys(0)[:2], [g_in, g_out])
                early = [next(last)]
            dhn = _matmul(dpre, W[("a_w_in", j)], mode="nt", out_dtype=F32, after=[g_in] + early,
                          name=f"gmlp_in_dx_{i}")
        else:
            g_out = _matmul(s["o"], dmix, mode="tn", out_dtype=BF16, name=f"attn_o_dw_{i}")
            do = _matmul(dmix, W[("b_w_o", j)], mode="nt", out_dtype=BF16, after=[g_out], name=f"attn_o_dx_{i}")
            dq, dkp, dkc, dvp, dvc, dsk = _attn_bwd(s["qr"], s["kr"], s["vr"], row(b_sinks, j), do,
                                                    name=f"attn_bwd_{i}")
            dqkv, dbq = _rope_bwd(dq, dkp, dkc, dvp, dvc, ctab, stab, name=f"rope_bwd_{i}")
            small[("b_sinks", j)] = dsk[:, :b_sinks.shape[1]]
            small[("b_b_qkv", j)] = dbq
            g_in = _matmul(s["hn"], dqkv, mode="tn", out_dtype=BF16, name=f"attn_qkv_dw_{i}")
            if i == 0:
                last = reduce_group("0m", layer_keys(0)[:2], [g_in, g_out])
                early = [next(last)]
            dhn = _matmul(dqkv, W[("b_w_qkv", j)], mode="nt", out_dtype=F32, after=[g_in] + early,
                          name=f"attn_qkv_dx_{i}")
        toks = advance(dhn)
        if i > 0:
            dh, df, g_pre_mix[i], g_post_ffn[i - 1] = _rms_bwd_chain(
                s["h"], gain("pre_mix", i), dhn, dh1, saved[i - 1]["f"], gain("post_ffn", i - 1), after=toks,
                name=f"rms_mix_ffn_bwd_{i}")
            gen = reduce_group(str(i), layer_keys(i), [g_in, g_out, g_gu, g_down])
            toks = [next(gen)] + advance(dh)
            reducing.append(gen)
        else:
            toks.append(last.send(dhn))
            dh, g_pre_mix[i] = _rms_bwd(s["h"], gain("pre_mix", i), dhn, dh1, out_dtype=F32, after=toks,
                                        name=f"rms_pre_mix_bwd_{i}")
            advance(dh)
    grad_x = dh[None]
    assert not reducing

    n_a, n_b = a_b_in.shape[0], b_sinks.shape[0]
    stack = lambda key, n: jnp.concatenate([small[(key, j)] for j in range(n)], axis=0)
    small_parts = [
        jnp.concatenate(g_pre_mix, axis=0), jnp.concatenate(g_post_mix, axis=0),
        jnp.concatenate(g_pre_ffn, axis=0), jnp.concatenate(g_post_ffn, axis=0),
        stack("a_b_in", n_a), stack("a_ln_g", n_a), stack("a_ln_b", n_a),
        jnp.stack([small[("a_w_s", j)] for j in range(n_a)]), jnp.stack([small[("a_b_s", j)] for j in range(n_a)]),
        stack("b_b_qkv", n_b), stack("b_sinks", n_b), loss_part,
    ]
    packed, metas = _pack_rows(small_parts)
    reduced = packed
    for stage in range(len(_BUTTERFLY)):
        ss, rs, bufs, tok = _split_start(f"butterfly_start_{stage}", [reduced, lax.empty(packed.shape, F32)], 1,
                                         _swap_copies(stage), place)
        done = adamw(updates[stage::len(_BUTTERFLY)], [tok])
        bufs = _split_wait(f"butterfly_wait_{stage}", bufs, (ss, rs), _swap_copies(stage), done or [tok])
        reduced = _add(bufs[0], bufs[1], name=f"butterfly_add_{stage}")
    updates = []
    while last.send(reduced) is not None:
        pass
    adamw(updates, [])
    red = _unpack_rows(reduced, metas)
    (gr_pre_mix, gr_post_mix, gr_pre_ffn, gr_post_ffn, gr_b_in, gr_ln_g, gr_ln_b, gr_w_s, gr_b_s,
     gr_b_qkv_full, gr_sinks, loss_sum) = red
    loss = loss_sum[0, 0]
    gr_b_qkv = lax.dynamic_slice(gr_b_qkv_full, (0, chip * nq), (gr_b_qkv_full.shape[0], nq))

    grads = {"pre_mix_g": gr_pre_mix, "post_mix_g": gr_post_mix, "pre_ffn_g": gr_pre_ffn, "post_ffn_g": gr_post_ffn,
             "a_b_in": gr_b_in, "a_ln_g": gr_ln_g, "a_ln_b": gr_ln_b, "a_w_s": gr_w_s, "a_b_s": gr_b_s,
             "b_b_qkv": gr_b_qkv, "b_sinks": gr_sinks}
    weights = {"pre_mix_g": (pre_mix_g, m_pre_mix_g, v_pre_mix_g), "post_mix_g": (post_mix_g, m_post_mix_g, v_post_mix_g),
               "pre_ffn_g": (pre_ffn_g, m_pre_ffn_g, v_pre_ffn_g), "post_ffn_g": (post_ffn_g, m_post_ffn_g, v_post_ffn_g),
               "a_b_in": (a_b_in, m_a_b_in, v_a_b_in), "a_ln_g": (a_ln_g, m_a_ln_g, v_a_ln_g),
               "a_ln_b": (a_ln_b, m_a_ln_b, v_a_ln_b), "a_w_s": (a_w_s, m_a_w_s, v_a_w_s), "a_b_s": (a_b_s, m_a_b_s, v_a_b_s),
               "b_b_qkv": (b_b_qkv, m_b_b_qkv, v_b_b_qkv), "b_sinks": (b_sinks, m_b_sinks, v_b_sinks)}
    order = ["pre_mix_g", "post_mix_g", "pre_ffn_g", "post_ffn_g", "a_w_in", "a_b_in", "a_ln_g", "a_ln_b", "a_w_s",
             "a_b_s", "a_w_out", "b_w_qkv", "b_b_qkv", "b_sinks", "b_w_o", "ffn_w_gu", "ffn_w_down"]
    deltas, new_m, new_v = {}, {}, {}
    for nm in order:
        if nm in big_out:
            grads[nm], deltas[nm], new_m[nm], new_v[nm] = big_out[nm]
        else:
            w, m, v = weights[nm]
            deltas[nm], new_m[nm], new_v[nm] = _adamw_small(w, grads[nm], m, v, name="adamw_" + nm)
    return (loss, grad_x, *[grads[nm] for nm in order], *[deltas[nm] for nm in order],
            *[new_m[nm] for nm in order], *[new_v[nm] for nm in order])
```

```python
import functools
import math

import jax
import jax.numpy as jnp
import numpy as np
from jax import lax
from jax.experimental import pallas as pl
from jax.experimental.pallas import tpu as pltpu

F32 = jnp.float32
BF16 = jnp.bfloat16
MESH = pl.DeviceIdType.MESH

HEAD_DIM = 64
N_KV_HEADS = 4
ROPE_DIM = 16
ROPE_THETA = 500000.0
CHUNK = 128
GMLP_GROUPS = 8
RMS_EPS = 1e-6
LN_EPS = 1e-5
NEG_INF = -1e30
ADAM_LR = 0.001
ADAM_B1 = 0.9
ADAM_B2 = 0.999
ADAM_EPS = 1e-08
ADAM_WD = 0.01
ADAM_STEP = 10

N_CHIPS = 4
LANES = 128
VMEM_CAP = 58 * 1024 * 1024


def _vmem(est_bytes):
    assert est_bytes < VMEM_CAP
    return VMEM_CAP


def _pick(n, cands):
    for c in cands:
        if c <= n and n % c == 0:
            return c
    return n


def _nbytes(shape, dtype):
    return int(np.prod(shape)) * jnp.dtype(dtype).itemsize


MATMUL_VMEM_BUDGET = 48 * 1024 * 1024
MXU_COLS = 256


def _halvings(n, unit):
    out, t = [], n
    while t % unit == 0 and t >= unit:
        out.append(t)
        if t % 2:
            break
        t //= 2
    return out


def _matmul_tiles(P, Q, R, a_bytes, b_bytes, o_bytes, full_addend, tp, tq, tr, repeat=1):
    step_us, bytes_per_us, flops_per_us = 0.85, 3.2e6, 9.0e8
    best = None
    for p in ([tp] if tp else _halvings(P, LANES)):
        for q in ([tq] if tq else _halvings(Q, LANES)):
            for r in ([tr] if tr else _halvings(R, LANES)):
                nk = R // r
                vm = 2 * (p * r * a_bytes + r * q * b_bytes + p * q * o_bytes + (p * q * 4 if full_addend else 0))
                vm += p * q * 4 * (2 if nk > 1 else 1)
                if vm > MATMUL_VMEM_BUDGET:
                    continue
                exposed = (p * r * a_bytes + r * q * b_bytes + p * q * o_bytes) / bytes_per_us
                mxu_us = repeat * 2.0 * P * R * (Q // q) * (-(-q // MXU_COLS) * MXU_COLS) / flops_per_us
                key = (repeat * (P // p) * (Q // q) * nk * step_us + exposed + mxu_us, nk, abs(p - q))
                if best is None or key < best[0]:
                    best = (key, (p, q, r))
    assert best is not None, (P, Q, R)
    return best[1]


def _matmul(a, b, *, mode, out_dtype, name, bias=None, after=()):
    if mode == "nn":
        (P, R), (R2, Q) = a.shape, b.shape
    elif mode == "nt":
        (P, R), (Q, R2) = a.shape, b.shape
    else:
        (R, P), (R2, Q) = a.shape, b.shape
    assert R == R2, (mode, a.shape, b.shape)
    tp, tq, tr = _matmul_tiles(P, Q, R, a.dtype.itemsize, b.dtype.itemsize, jnp.dtype(out_dtype).itemsize, False,
                               None, None, None)
    nk = R // tr
    dims = {"nn": (((1,), (0,)), ((), ())), "nt": (((1,), (1,)), ((), ())), "tn": (((0,), (0,)), ((), ()))}[mode]
    if mode == "nn":
        a_spec = pl.BlockSpec((tp, tr), lambda i, j, k: (i, k))
        b_spec = pl.BlockSpec((tr, tq), lambda i, j, k: (k, j))
    elif mode == "nt":
        a_spec = pl.BlockSpec((tp, tr), lambda i, j, k: (i, k))
        b_spec = pl.BlockSpec((tq, tr), lambda i, j, k: (j, k))
    else:
        a_spec = pl.BlockSpec((tr, tp), lambda i, j, k: (k, i))
        b_spec = pl.BlockSpec((tr, tq), lambda i, j, k: (k, j))
    in_specs = [a_spec, b_spec]
    args = [a, b]
    has_bias = bias is not None
    if has_bias:
        in_specs.append(pl.BlockSpec((1, tq), lambda i, j, k: (0, j)))
        args.append(bias)
    out_shape = jax.ShapeDtypeStruct((P, Q), out_dtype)
    out_spec = pl.BlockSpec((tp, tq), lambda i, j, k: (i, j))
    n_in = len(args) + len(after)
    in_specs += [pl.BlockSpec(memory_space=pl.ANY)] * len(after)
    args += list(after)

    def body(*refs):
        a_ref, b_ref = refs[0], refs[1]
        bias_ref = refs[2] if has_bias else None
        o_ref = refs[n_in]
        acc_ref = refs[n_in + 1] if nk > 1 else None
        part = lax.dot_general(a_ref[...], b_ref[...], dims, preferred_element_type=F32)

        def finish(acc):
            if has_bias:
                acc = acc + bias_ref[...]
            o_ref[...] = acc.astype(out_dtype)

        if nk == 1:
            finish(part)
        else:
            k = pl.program_id(2)

            @pl.when(k == 0)
            def _():
                acc_ref[...] = part

            @pl.when(k > 0)
            def _():
                acc_ref[...] += part

            @pl.when(k == nk - 1)
            def _():
                finish(acc_ref[...])

    est = 2 * (_nbytes((tp, tr), a.dtype) + _nbytes((tr, tq), b.dtype) + _nbytes((tp, tq), out_dtype)) + 3 * tp * tq * 4
    return pl.pallas_call(
        body, name=name, out_shape=out_shape,
        grid=(P // tp, Q // tq, nk),
        in_specs=in_specs, out_specs=out_spec,
        scratch_shapes=[pltpu.VMEM((tp, tq), F32)] if nk > 1 else [],
        compiler_params=pltpu.CompilerParams(
            dimension_semantics=("parallel", "parallel", "arbitrary"), vmem_limit_bytes=_vmem(est)),
    )(*args)


def _matmul_pair(a, b, pair, *, mode, out_dtype, name, after=()):
    if mode == "tn":
        (R, P), (R2, Q) = a.shape, b.shape
        assert R == R2 and pair.shape == b.shape
        tp, tq, tr = _matmul_tiles(P, Q, R, a.dtype.itemsize, 2 * b.dtype.itemsize,
                                   jnp.dtype(out_dtype).itemsize, False, None, None, None, repeat=2)
        nq, nk = Q // tq, R // tr
        grid, nk_total = (P // tp, 2 * nq, nk), nk
        a_spec = pl.BlockSpec((tr, tp), lambda i, j, k: (k, i))
        b_spec = pl.BlockSpec((tr, tq), lambda i, j, k: (jnp.where(j < nq, k, nk - 1), jnp.minimum(j, nq - 1)))
        p_spec = pl.BlockSpec((tr, tq), lambda i, j, k: (jnp.where(j >= nq, k, 0), jnp.maximum(j - nq, 0)))
        out_shape = (P, 2 * Q)
        dims = (((0,), (0,)), ((), ()))
    else:
        assert mode == "nt"
        (P, R), (Q, R2) = a.shape, b.shape
        assert R2 == 2 * R and pair.shape == a.shape
        tp, tq, tr = _matmul_tiles(P, Q, R, 2 * a.dtype.itemsize, b.dtype.itemsize,
                                   jnp.dtype(out_dtype).itemsize, False, None, None, None, repeat=2)
        nk = R // tr
        grid, nk_total = (P // tp, Q // tq, 2 * nk), 2 * nk
        a_spec = pl.BlockSpec((tp, tr), lambda i, j, k: (i, jnp.minimum(k, nk - 1)))
        p_spec = pl.BlockSpec((tp, tr), lambda i, j, k: (i, jnp.maximum(k - nk, 0)))
        b_spec = pl.BlockSpec((tq, tr), lambda i, j, k: (j, k))
        out_shape = (P, Q)
        dims = (((1,), (1,)), ((), ()))
    n_after = len(after)

    def body(a_ref, b_ref, p_ref, *rest):
        o_ref = rest[n_after]
        acc_ref = rest[n_after + 1] if nk_total > 1 else None
        j, k = pl.program_id(1), pl.program_id(2)

        def step(l_ref, r_ref):
            part = lax.dot_general(l_ref[...], r_ref[...], dims, preferred_element_type=F32)
            if nk_total == 1:
                o_ref[...] = part.astype(out_dtype)
                return

            @pl.when(k == 0)
            def _():
                acc_ref[...] = part

            @pl.when(k > 0)
            def _():
                acc_ref[...] += part

            @pl.when(k == nk_total - 1)
            def _():
                o_ref[...] = acc_ref[...].astype(out_dtype)

        first = (j < nq) if mode == "tn" else (k < nk)

        @pl.when(first)
        def _():
            step(a_ref, b_ref)

        @pl.when(jnp.logical_not(first))
        def _():
            step(a_ref if mode == "tn" else p_ref, p_ref if mode == "tn" else b_ref)

    n_a, n_b = (1, 2) if mode == "tn" else (2, 1)
    est = (2 * (n_a * _nbytes((tp, tr), a.dtype) + n_b * _nbytes((tr, tq), b.dtype) + _nbytes((tp, tq), out_dtype))
           + 2 * tp * tq * 4)
    return pl.pallas_call(
        body, name=name, out_shape=jax.ShapeDtypeStruct(out_shape, out_dtype), grid=grid,
        in_specs=[a_spec, b_spec, p_spec] + [pl.BlockSpec(memory_space=pl.ANY)] * n_after,
        out_specs=pl.BlockSpec((tp, tq), lambda i, j, k: (i, j)),
        scratch_shapes=[pltpu.VMEM((tp, tq), F32)] if nk_total > 1 else [],
        compiler_params=pltpu.CompilerParams(
            dimension_semantics=("parallel", "parallel", "arbitrary"), vmem_limit_bytes=_vmem(est)),
    )(a, b, pair, *after)


def _row_call(body, ins, outs, *, name, rows, tr, acc_outs=(), est=0, after=()):
    in_specs, args = [], []
    for arr, kind in ins:
        if kind == "row":
            in_specs.append(pl.BlockSpec((tr, arr.shape[1]), lambda i: (i, 0)))
        elif isinstance(arr, tuple):
            arr, layer = arr
            in_specs.append(pl.BlockSpec((None,) + arr.shape[1:], lambda i, layer=layer: (layer, 0, 0)))
        else:
            nd = arr.ndim
            in_specs.append(pl.BlockSpec(arr.shape, lambda i, nd=nd: (0,) * nd))
        args.append(arr)
    n_ins = len(args)
    in_specs += [pl.BlockSpec(memory_space=pl.ANY)] * len(after)
    args += list(after)

    def kernel_fn(*refs):
        body(*refs[:n_ins], *refs[n_ins + len(after):])

    out_shapes = [jax.ShapeDtypeStruct(s, d) for s, d in outs] + [jax.ShapeDtypeStruct(s, d) for s, d in acc_outs]
    out_specs = [pl.BlockSpec((tr, s[1]), lambda i: (i, 0)) for s, _ in outs]
    out_specs += [pl.BlockSpec(s, lambda i, nd=len(s): (0,) * nd) for s, _ in acc_outs]
    res = pl.pallas_call(
        kernel_fn, name=name, out_shape=out_shapes, grid=(rows // tr,), in_specs=in_specs, out_specs=out_specs,
        compiler_params=pltpu.CompilerParams(dimension_semantics=("arbitrary",), vmem_limit_bytes=_vmem(est)),
    )(*args)
    return res


def _rms_fwd(x, g, *, out_dtype, name, after=()):
    T, D = x.shape
    tr = _pick(T, (512, 256, 128))

    def body(x_ref, g_ref, o_ref):
        xv = x_ref[...]
        r = lax.rsqrt(jnp.mean(xv * xv, axis=-1, keepdims=True) + RMS_EPS)
        o_ref[...] = (xv * r * g_ref[...]).astype(out_dtype)

    return _row_call(body, [(x, "row"), (g, "full")], [((T, D), out_dtype)], name=name, rows=T, tr=tr,
                     est=8 * tr * D * 4, after=after)[0]


def _rms_res(h, y, g, *, name):
    T, D = h.shape
    tr = _pick(T, (512, 256, 128))

    def body(h_ref, y_ref, g_ref, o_ref):
        yv = y_ref[...]
        r = lax.rsqrt(jnp.mean(yv * yv, axis=-1, keepdims=True) + RMS_EPS)
        o_ref[...] = h_ref[...] + yv * r * g_ref[...]

    return _row_call(body, [(h, "row"), (y, "row"), (g, "full")], [((T, D), F32)], name=name, rows=T, tr=tr,
                     est=10 * tr * D * 4)[0]


def _rms_bwd(x, g, dy, dres, *, out_dtype, name, after=()):
    T, D = x.shape
    tr = _pick(T, (512, 256, 128))
    has_res = dres is not None

    def body(*refs):
        if has_res:
            x_ref, g_ref, dy_ref, dr_ref, dx_ref, dg_ref = refs
        else:
            x_ref, g_ref, dy_ref, dx_ref, dg_ref = refs
        xv = x_ref[...]
        r = lax.rsqrt(jnp.mean(xv * xv, axis=-1, keepdims=True) + RMS_EPS)
        xhat = xv * r
        dyv = dy_ref[...].astype(F32)
        dxn = dyv * g_ref[...]
        dx = r * (dxn - xhat * jnp.mean(dxn * xhat, axis=-1, keepdims=True))
        if has_res:
            dx = dx + dr_ref[...]
        dx_ref[...] = dx.astype(out_dtype)
        part = jnp.sum(dyv * xhat, axis=0, keepdims=True)

        @pl.when(pl.program_id(0) == 0)
        def _():
            dg_ref[...] = part

        @pl.when(pl.program_id(0) > 0)
        def _():
            dg_ref[...] += part

    ins = [(x, "row"), (g, "full"), (dy, "row")] + ([(dres, "row")] if has_res else [])
    dx, dg = _row_call(body, ins, [((T, D), out_dtype)], name=name, rows=T, tr=tr, acc_outs=[((1, D), F32)],
                       est=12 * tr * D * 4, after=after)
    return dx, dg


def _rms_res_norm(h, y, g_res, g_next, *, name, after=()):
    T, D = h.shape
    tr = _pick(T, (512, 256, 128))

    def body(h_ref, y_ref, g_ref, gn_ref, o_ref, n_ref):
        yv = y_ref[...]
        r = lax.rsqrt(jnp.mean(yv * yv, axis=-1, keepdims=True) + RMS_EPS)
        h2 = h_ref[...] + yv * r * g_ref[...]
        o_ref[...] = h2
        r2 = lax.rsqrt(jnp.mean(h2 * h2, axis=-1, keepdims=True) + RMS_EPS)
        n_ref[...] = (h2 * r2 * gn_ref[...]).astype(BF16)

    return _row_call(body, [(h, "row"), (y, "row"), (g_res, "full"), (g_next, "full")],
                     [((T, D), F32), ((T, D), BF16)], name=name, rows=T, tr=tr, est=12 * tr * D * 4, after=after)


def _rms_bwd_chain(x1, g1, dy1, dres, x2, g2, *, name, after=()):
    T, D = x1.shape
    tr = _pick(T, (512, 256, 128))

    def one(xv, gv, dyv):
        r = lax.rsqrt(jnp.mean(xv * xv, axis=-1, keepdims=True) + RMS_EPS)
        xhat = xv * r
        dxn = dyv * gv
        dx = r * (dxn - xhat * jnp.mean(dxn * xhat, axis=-1, keepdims=True))
        return dx, jnp.sum(dyv * xhat, axis=0, keepdims=True)

    def body(x1_ref, g1_ref, dy1_ref, dr_ref, x2_ref, g2_ref, d1_ref, d2_ref, dg1_ref, dg2_ref):
        dx1, p1 = one(x1_ref[...], g1_ref[...], dy1_ref[...].astype(F32))
        d1 = dx1 + dr_ref[...]
        d1_ref[...] = d1
        dx2, p2 = one(x2_ref[...], g2_ref[...], d1)
        d2_ref[...] = dx2.astype(BF16)

        @pl.when(pl.program_id(0) == 0)
        def _():
            dg1_ref[...] = p1
            dg2_ref[...] = p2

        @pl.when(pl.program_id(0) > 0)
        def _():
            dg1_ref[...] += p1
            dg2_ref[...] += p2

    ins = [(x1, "row"), (g1, "full"), (dy1, "row"), (dres, "row"), (x2, "row"), (g2, "full")]
    return _row_call(body, ins, [((T, D), F32), ((T, D), BF16)], name=name, rows=T, tr=tr,
                     acc_outs=[((1, D), F32), ((1, D), F32)], est=20 * tr * D * 4, after=after)


def _ffn_up(fn, w_gu, l, *, name):
    T, D = fn.shape
    H = w_gu.shape[2] // 2
    tp = _pick(T, (512, 256, 128))
    tq = H
    nj = H // tq

    def body(a_ref, wg_ref, wu_ref, g_ref, u_ref, act_ref):
        a = a_ref[...]
        g = jnp.dot(a, wg_ref[...], preferred_element_type=F32)
        u = jnp.dot(a, wu_ref[...], preferred_element_type=F32)
        sg = jax.nn.sigmoid(g)
        silu = g * sg
        g_ref[...] = (u * (sg + silu * (1.0 - sg))).astype(BF16)
        u_ref[...] = silu.astype(BF16)
        act_ref[...] = (silu * u).astype(BF16)

    tile = pl.BlockSpec((tp, tq), lambda j, i: (i, j))
    est = 2 * (tp * D * 2 + 3 * tp * tq * 2) + 2 * D * tq * 2 + 4 * tp * tq * 4
    return pl.pallas_call(
        body, name=name,
        out_shape=[jax.ShapeDtypeStruct((T, H), BF16), jax.ShapeDtypeStruct((T, H), BF16),
                   jax.ShapeDtypeStruct((T, H), BF16)],
        grid=(nj, T // tp),
        in_specs=[pl.BlockSpec((tp, D), lambda j, i: (i, 0)),
                  pl.BlockSpec((None, D, tq), lambda j, i: (l, 0, j), pipeline_mode=pl.Buffered(1)),
                  pl.BlockSpec((None, D, tq), lambda j, i: (l, 0, j + nj), pipeline_mode=pl.Buffered(1))],
        out_specs=[tile, tile, tile],
        compiler_params=pltpu.CompilerParams(dimension_semantics=("parallel", "parallel"),
                                             vmem_limit_bytes=_vmem(est)),
    )(fn, w_gu, w_gu)


def _ffn_down_dx(df, w_down, l, g, u, after, *, name):
    T, D = df.shape
    H = w_down.shape[1]
    tp = _pick(T, (512, 256, 128))
    tq = H

    def body(a_ref, w_ref, g_ref, u_ref, _, dg_ref, du_ref):
        da = lax.dot_general(a_ref[...], w_ref[...], (((1,), (1,)), ((), ())), preferred_element_type=F32)
        dg_ref[...] = (da * g_ref[...].astype(F32)).astype(BF16)
        du_ref[...] = (da * u_ref[...].astype(F32)).astype(BF16)

    tile = pl.BlockSpec((tp, tq), lambda j, i: (i, j))
    est = 2 * (tp * D * 2 + tq * D * 2 + 4 * tp * tq * 2) + 3 * tp * tq * 4
    return pl.pallas_call(
        body, name=name,
        out_shape=[jax.ShapeDtypeStruct((T, H), BF16), jax.ShapeDtypeStruct((T, H), BF16)],
        grid=(H // tq, T // tp),
        in_specs=[pl.BlockSpec((tp, D), lambda j, i: (i, 0)),
                  pl.BlockSpec((None, tq, D), lambda j, i: (l, j, 0)), tile, tile,
                  pl.BlockSpec(memory_space=pl.ANY)],
        out_specs=[tile, tile],
        compiler_params=pltpu.CompilerParams(dimension_semantics=("parallel", "parallel"),
                                             vmem_limit_bytes=_vmem(est)),
    )(df, w_down, g, u, after)


def _loss_and_grad(y, target, x, g, *, name):
    T, D = y.shape
    tr = _pick(T, (512, 256, 128))

    def body(y_ref, t_ref, x_ref, g_ref, dy_ref, dx_ref, l_ref, dg_ref):
        e = y_ref[...] - t_ref[...]
        dy = e * (1.0 / D)
        dy_ref[...] = dy
        part = jnp.sum(jnp.sum(e * e, axis=1, keepdims=True), axis=0, keepdims=True) * (0.5 / D)
        xv = x_ref[...]
        r = lax.rsqrt(jnp.mean(xv * xv, axis=-1, keepdims=True) + RMS_EPS)
        xhat = xv * r
        dxn = dy * g_ref[...]
        dx_ref[...] = (r * (dxn - xhat * jnp.mean(dxn * xhat, axis=-1, keepdims=True))).astype(BF16)
        dg = jnp.sum(dy * xhat, axis=0, keepdims=True)

        @pl.when(pl.program_id(0) == 0)
        def _():
            l_ref[...] = part
            dg_ref[...] = dg

        @pl.when(pl.program_id(0) > 0)
        def _():
            l_ref[...] += part
            dg_ref[...] += dg

    dy, dx, l, dg = _row_call(body, [(y, "row"), (target, "row"), (x, "row"), (g, "full")],
                              [((T, D), F32), ((T, D), BF16)], name=name, rows=T, tr=tr,
                              acc_outs=[((1, 1), F32), ((1, D), F32)], est=14 * tr * D * 4)
    return dy, dx, l, dg


_SQRT_HALF = 0.7071067811865476
_INV_SQRT_2PI = 0.3989422804014327


def _gelu_parts(x):
    cdf = 0.5 * (1.0 + lax.erf(x * _SQRT_HALF))
    return cdf


def _sgu_common(pre, lng, lnb, W):
    cdf = _gelu_parts(pre)
    z = pre * cdf
    u = z[:, :W]
    v = z[:, W:]
    mu = jnp.mean(v, axis=-1, keepdims=True)
    vc = v - mu
    var = jnp.mean(vc * vc, axis=-1, keepdims=True)
    rstd = lax.rsqrt(var + LN_EPS)
    vhat = vc * rstd
    vn = vhat * lng + lnb
    return cdf, u, vhat, rstd, vn


def _causal_mask():
    t = lax.broadcasted_iota(jnp.int32, (CHUNK, CHUNK), 0)
    s = lax.broadcasted_iota(jnp.int32, (CHUNK, CHUNK), 1)
    return t >= s


def _sgu_fwd(pre, lng, lnb, ws, bsT, *, name):
    T, W2 = pre.shape
    W = W2 // 2
    G = ws.shape[0]
    gd = W // G

    def body(pre_ref, lng_ref, lnb_ref, ws_ref, bs_ref, o_ref):
        _, u, _, _, vn = _sgu_common(pre_ref[...], lng_ref[...], lnb_ref[...], W)
        vnb = vn.astype(BF16)
        causal = _causal_mask()
        for g in range(G):
            w = jnp.where(causal, ws_ref[g], 0.0).astype(BF16)
            sv = jnp.dot(w, vnb[:, g * gd:(g + 1) * gd], preferred_element_type=F32) + bs_ref[:, g:g + 1]
            o_ref[:, g * gd:(g + 1) * gd] = (u[:, g * gd:(g + 1) * gd] * sv).astype(BF16)

    return pl.pallas_call(
        body, name=name, out_shape=jax.ShapeDtypeStruct((T, W), BF16), grid=(T // CHUNK,),
        in_specs=[pl.BlockSpec((CHUNK, W2), lambda i: (i, 0)),
                  pl.BlockSpec((1, W), lambda i: (0, 0)), pl.BlockSpec((1, W), lambda i: (0, 0)),
                  pl.BlockSpec(ws.shape, lambda i: (0, 0, 0)), pl.BlockSpec(bsT.shape, lambda i: (0, 0))],
        out_specs=pl.BlockSpec((CHUNK, W), lambda i: (i, 0)),
        compiler_params=pltpu.CompilerParams(dimension_semantics=("arbitrary",),
                                             vmem_limit_bytes=_vmem(12 * CHUNK * W2 * 4)),
    )(pre, lng, lnb, ws, bsT)


def _sgu_bwd(pre, dgated, lng, lnb, ws, bsT, *, name):
    T, W2 = pre.shape
    W = W2 // 2
    G = ws.shape[0]
    gd = W // G

    def body(pre_ref, dgt_ref, lng_ref, lnb_ref, ws_ref, bs_ref,
             dpre_ref, dws_ref, dbs_ref, dlng_ref, dlnb_ref, dbin_ref):
        first = pl.program_id(0) == 0

        @pl.when(first)
        def _():
            dws_ref[...] = jnp.zeros_like(dws_ref)
            dbs_ref[...] = jnp.zeros_like(dbs_ref)
            dlng_ref[...] = jnp.zeros_like(dlng_ref)
            dlnb_ref[...] = jnp.zeros_like(dlnb_ref)
            dbin_ref[...] = jnp.zeros_like(dbin_ref)

        pre_v = pre_ref[...]
        lng_v = lng_ref[...]
        cdf, u, vhat, rstd, vn = _sgu_common(pre_v, lng_v, lnb_ref[...], W)
        vnb = vn.astype(BF16)
        dgt = dgt_ref[...].astype(F32)
        causal = _causal_mask()
        du_parts, dvn_parts = [], []
        for g in range(G):
            sl = slice(g * gd, (g + 1) * gd)
            w = jnp.where(causal, ws_ref[g], 0.0).astype(BF16)
            sv = jnp.dot(w, vnb[:, sl], preferred_element_type=F32) + bs_ref[:, g:g + 1]
            dgt_g = dgt[:, sl]
            du_parts.append(dgt_g * sv)
            dsv = dgt_g * u[:, sl]
            dsvb = dsv.astype(BF16)
            dvn_parts.append(lax.dot_general(w, dsvb, (((0,), (0,)), ((), ())), preferred_element_type=F32))
            dw = lax.dot_general(dsvb, vnb[:, sl], (((1,), (1,)), ((), ())), preferred_element_type=F32)
            dws_ref[g] += jnp.where(causal, dw, 0.0)
            dbs_ref[:, g:g + 1] += jnp.sum(dsv, axis=1, keepdims=True)
        du = jnp.concatenate(du_parts, axis=1)
        dvn = jnp.concatenate(dvn_parts, axis=1)
        dlng_ref[...] += jnp.sum(dvn * vhat, axis=0, keepdims=True)
        dlnb_ref[...] += jnp.sum(dvn, axis=0, keepdims=True)
        dvh = dvn * lng_v
        dv = rstd * (dvh - jnp.mean(dvh, axis=-1, keepdims=True)
                     - vhat * jnp.mean(dvh * vhat, axis=-1, keepdims=True))
        dz = jnp.concatenate([du, dv], axis=1)
        dgelu = cdf + pre_v * jnp.exp(-0.5 * pre_v * pre_v) * _INV_SQRT_2PI
        dpre = dz * dgelu
        dbin_ref[...] += jnp.sum(dpre, axis=0, keepdims=True)
        dpre_ref[...] = dpre.astype(BF16)

    full = lambda shape: pl.BlockSpec(shape, lambda i, nd=len(shape): (0,) * nd)
    return pl.pallas_call(
        body, name=name,
        out_shape=[jax.ShapeDtypeStruct((T, W2), BF16), jax.ShapeDtypeStruct(ws.shape, F32),
                   jax.ShapeDtypeStruct(bsT.shape, F32), jax.ShapeDtypeStruct((1, W), F32),
                   jax.ShapeDtypeStruct((1, W), F32), jax.ShapeDtypeStruct((1, W2), F32)],
        grid=(T // CHUNK,),
        in_specs=[pl.BlockSpec((CHUNK, W2), lambda i: (i, 0)), pl.BlockSpec((CHUNK, W), lambda i: (i, 0)),
                  full((1, W)), full((1, W)), full(ws.shape), full(bsT.shape)],
        out_specs=[pl.BlockSpec((CHUNK, W2), lambda i: (i, 0)), full(ws.shape), full(bsT.shape),
                   full((1, W)), full((1, W)), full((1, W2))],
        compiler_params=pltpu.CompilerParams(dimension_semantics=("arbitrary",),
                                             vmem_limit_bytes=_vmem(24 * CHUNK * W2 * 4)),
    )(pre, dgated, lng, lnb, ws, bsT)


def _rope_tables(positions):
    half = ROPE_DIM // 2
    inv_freq = ROPE_THETA ** (-jnp.arange(0, ROPE_DIM, 2, dtype=F32) / ROPE_DIM)
    ang = positions.astype(F32).reshape(-1, 1) * inv_freq
    cos, sin = jnp.cos(ang), jnp.sin(ang)
    T = ang.shape[0]
    rest = HEAD_DIM - ROPE_DIM
    c64 = jnp.concatenate([cos, cos, jnp.ones((T, rest), F32)], axis=1)
    s64 = jnp.concatenate([-sin, sin, jnp.zeros((T, rest), F32)], axis=1)
    del half
    return jnp.tile(c64, (1, LANES // HEAD_DIM)), jnp.tile(s64, (1, LANES // HEAD_DIM))


def _swap8(x):
    W = x.shape[1]
    half = ROPE_DIM // 2
    lane = lax.broadcasted_iota(jnp.int32, x.shape, 1) % HEAD_DIM
    return jnp.where(lane < half, pltpu.roll(x, W - half, axis=1),
                     jnp.where(lane < ROPE_DIM, pltpu.roll(x, half, axis=1), 0.0))


def _wide(tab, W):
    return jnp.concatenate([tab] * (W // LANES), axis=1) if W > LANES else tab


def _rope_fwd(qkv, ctab, stab, *, q_width, kv_width, name):
    T = qkv.shape[0]
    tr = _pick(T, (256, 128))
    scale = HEAD_DIM ** -0.5

    def body(x_ref, c_ref, s_ref, q_ref, k_ref, v_ref):
        c = c_ref[...]
        s = s_ref[...]
        q = x_ref[:, :q_width]
        k = x_ref[:, q_width:q_width + kv_width]
        q_ref[...] = ((q * _wide(c, q_width) + _swap8(q) * _wide(s, q_width)) * scale).astype(BF16)
        k_ref[...] = (k * _wide(c, kv_width) + _swap8(k) * _wide(s, kv_width)).astype(BF16)
        v_ref[...] = x_ref[:, q_width + kv_width:].astype(BF16)

    return _row_call(body, [(qkv, "row"), (ctab, "row"), (stab, "row")],
                     [((T, q_width), BF16), ((T, kv_width), BF16), ((T, kv_width), BF16)],
                     name=name, rows=T, tr=tr, est=10 * tr * qkv.shape[1] * 4)


_NT = (((1,), (1,)), ((), ()))
_TN = (((0,), (0,)), ((), ()))


def _group_rows(ref, heads):
    return jnp.concatenate([ref[:, h * HEAD_DIM:(h + 1) * HEAD_DIM] for h in heads], axis=0)


def _attn_valid(grp):
    qi = np.arange(grp * CHUNK)[:, None] % CHUNK
    sj = np.arange(2 * CHUNK)[None, :]
    cur = (sj >= CHUNK) & (sj - CHUNK <= qi)
    prev = (sj < CHUNK) & (sj > qi)
    return jnp.asarray(np.stack([cur, cur | prev]).astype(np.float32))


def _valid_spec(grp):
    return pl.BlockSpec((None, grp * CHUNK, 2 * CHUNK), lambda n: (jnp.minimum(n, 1), 0, 0))


def _attn_group_probs(q, kk, sinks, valid, grp):
    rows = grp * CHUNK
    s = lax.dot_general(q, kk, _NT, preferred_element_type=F32)
    s = jnp.where(valid, s, NEG_INF)
    r = lax.broadcasted_iota(jnp.int32, (rows, 1), 0)
    sink = jnp.full((rows, 1), sinks[grp - 1], F32)
    for g in range(grp - 2, -1, -1):
        sink = jnp.where(r < (g + 1) * CHUNK, sinks[g], sink)
    m = jnp.maximum(jnp.max(s, axis=1, keepdims=True), sink)
    p = jnp.exp(s - m)
    ps = jnp.exp(sink - m)
    inv = 1.0 / (jnp.sum(p, axis=1, keepdims=True) + ps)
    return p * inv, ps * inv


def _kv_specs(width, nb):
    prev = pl.BlockSpec((CHUNK, width), lambda n: (jnp.maximum(n - 1, 0), 0))
    cur = pl.BlockSpec((CHUNK, width), lambda n: (n, 0))
    return prev, cur


def _attn_fwd(qr, kr, vr, sinks, *, name):
    T, QW = qr.shape
    KW = kr.shape[1]
    HQ, HK = QW // HEAD_DIM, KW // HEAD_DIM
    grp = HQ // HK
    nb = T // CHUNK

    def body(q_ref, kp_ref, kc_ref, vp_ref, vc_ref, s_ref, ok_ref, o_ref):
        valid = ok_ref[...] > 0.5
        for kh in range(HK):
            ks = slice(kh * HEAD_DIM, (kh + 1) * HEAD_DIM)
            heads = list(range(kh * grp, (kh + 1) * grp))
            q = _group_rows(q_ref, heads)
            kk = jnp.concatenate([kp_ref[:, ks], kc_ref[:, ks]], axis=0)
            vv = jnp.concatenate([vp_ref[:, ks], vc_ref[:, ks]], axis=0)
            p, _ = _attn_group_probs(q, kk, [s_ref[0, h] for h in heads], valid, grp)
            o = jnp.dot(p.astype(BF16), vv, preferred_element_type=F32).astype(BF16)
            for g, h in enumerate(heads):
                o_ref[:, h * HEAD_DIM:(h + 1) * HEAD_DIM] = o[g * CHUNK:(g + 1) * CHUNK]

    kp, kc = _kv_specs(KW, nb)
    return pl.pallas_call(
        body, name=name, out_shape=jax.ShapeDtypeStruct((T, QW), BF16), grid=(nb,),
        in_specs=[pl.BlockSpec((CHUNK, QW), lambda n: (n, 0)), kp, kc, kp, kc,
                  pl.BlockSpec(memory_space=pltpu.SMEM), _valid_spec(grp)],
        out_specs=pl.BlockSpec((CHUNK, QW), lambda n: (n, 0)),
        compiler_params=pltpu.CompilerParams(dimension_semantics=("arbitrary",), vmem_limit_bytes=_vmem(8 << 20)),
    )(qr, kr, kr, vr, vr, sinks, _attn_valid(grp))


def _attn_bwd(qr, kr, vr, sinks, do, *, name):
    T, QW = qr.shape
    KW = kr.shape[1]
    HQ, HK = QW // HEAD_DIM, KW // HEAD_DIM
    grp = HQ // HK
    nb = T // CHUNK

    def body(q_ref, kp_ref, kc_ref, vp_ref, vc_ref, s_ref, do_ref, ok_ref,
             dq_ref, dkp_ref, dkc_ref, dvp_ref, dvc_ref, ds_ref):
        n = pl.program_id(0)
        valid = ok_ref[...] > 0.5
        lane = lax.broadcasted_iota(jnp.int32, (1, LANES), 1)
        dsink = jnp.zeros((1, LANES), F32)
        for kh in range(HK):
            ks = slice(kh * HEAD_DIM, (kh + 1) * HEAD_DIM)
            heads = list(range(kh * grp, (kh + 1) * grp))
            q = _group_rows(q_ref, heads)
            doh = _group_rows(do_ref, heads)
            kk = jnp.concatenate([kp_ref[:, ks], kc_ref[:, ks]], axis=0)
            vv = jnp.concatenate([vp_ref[:, ks], vc_ref[:, ks]], axis=0)
            p, ps = _attn_group_probs(q, kk, [s_ref[0, h] for h in heads], valid, grp)
            dp = lax.dot_general(doh, vv, _NT, preferred_element_type=F32)
            delta = jnp.sum(p * dp, axis=1, keepdims=True)
            ds = (p * (dp - delta)).astype(BF16)
            dv = lax.dot_general(p.astype(BF16), doh, _TN, preferred_element_type=F32)
            dk = lax.dot_general(ds, q, _TN, preferred_element_type=F32)
            dq = jnp.dot(ds, kk, preferred_element_type=F32)
            psd = ps * delta
            for g, h in enumerate(heads):
                dq_ref[:, h * HEAD_DIM:(h + 1) * HEAD_DIM] = dq[g * CHUNK:(g + 1) * CHUNK]
                dsink = dsink + jnp.where(
                    lane == h, -jnp.sum(psd[g * CHUNK:(g + 1) * CHUNK], axis=0, keepdims=True), 0.0)
            dkp_ref[:, ks] = dk[:CHUNK]
            dkc_ref[:, ks] = dk[CHUNK:]
            dvp_ref[:, ks] = dv[:CHUNK]
            dvc_ref[:, ks] = dv[CHUNK:]

        @pl.when(n == 0)
        def _():
            ds_ref[...] = dsink

        @pl.when(n > 0)
        def _():
            ds_ref[...] += dsink

    kp, kc = _kv_specs(KW, nb)
    qspec = pl.BlockSpec((CHUNK, QW), lambda n: (n, 0))
    kout = pl.BlockSpec((CHUNK, KW), lambda n: (n, 0))
    return pl.pallas_call(
        body, name=name,
        out_shape=[jax.ShapeDtypeStruct((T, QW), F32)] + [jax.ShapeDtypeStruct((T, KW), F32)] * 4
        + [jax.ShapeDtypeStruct((1, LANES), F32)],
        grid=(nb,),
        in_specs=[qspec, kp, kc, kp, kc, pl.BlockSpec(memory_space=pltpu.SMEM), qspec, _valid_spec(grp)],
        out_specs=[qspec, kout, kout, kout, kout, pl.BlockSpec((1, LANES), lambda n: (0, 0))],
        compiler_params=pltpu.CompilerParams(dimension_semantics=("arbitrary",), vmem_limit_bytes=_vmem(12 << 20)),
    )(qr, kr, kr, vr, vr, sinks, do, _attn_valid(grp))


def _rope_bwd(dq, dkp, dkc, dvp, dvc, ctab, stab, *, name):
    T, QW = dq.shape
    KW = dkp.shape[1]
    nb = T // CHUNK
    scale = HEAD_DIM ** -0.5
    width = QW + 2 * KW

    def body(dq_ref, dkc_ref, dkn_ref, dvc_ref, dvn_ref, c_ref, s_ref, o_ref, db_ref):
        n = pl.program_id(0)
        c = c_ref[...]
        s = s_ref[...]
        has_next = (n < nb - 1).astype(F32)
        dqv = dq_ref[...]
        dk = dkc_ref[...] + has_next * dkn_ref[...]
        dv = dvc_ref[...] + has_next * dvn_ref[...]
        dq_pre = (dqv * _wide(c, QW) + _swap8(dqv * _wide(s, QW))) * scale
        dk_pre = dk * _wide(c, KW) + _swap8(dk * _wide(s, KW))
        o_ref[:, :QW] = dq_pre.astype(BF16)
        o_ref[:, QW:QW + KW] = dk_pre.astype(BF16)
        o_ref[:, QW + KW:] = dv.astype(BF16)
        part = jnp.concatenate([jnp.sum(dq_pre, axis=0, keepdims=True), jnp.sum(dk_pre, axis=0, keepdims=True),
                                jnp.sum(dv, axis=0, keepdims=True)], axis=1)

        @pl.when(n == 0)
        def _():
            db_ref[...] = part

        @pl.when(n > 0)
        def _():
            db_ref[...] += part

    cur = lambda w: pl.BlockSpec((CHUNK, w), lambda n: (n, 0))
    nxt = lambda w: pl.BlockSpec((CHUNK, w), lambda n: (jnp.minimum(n + 1, nb - 1), 0))
    return pl.pallas_call(
        body, name=name,
        out_shape=[jax.ShapeDtypeStruct((T, width), BF16), jax.ShapeDtypeStruct((1, width), F32)],
        grid=(nb,),
        in_specs=[cur(QW), cur(KW), nxt(KW), cur(KW), nxt(KW), cur(LANES), cur(LANES)],
        out_specs=[cur(width), pl.BlockSpec((1, width), lambda n: (0, 0))],
        compiler_params=pltpu.CompilerParams(dimension_semantics=("arbitrary",), vmem_limit_bytes=_vmem(8 << 20)),
    )(dq, dkc, dkp, dvc, dvp, ctab, stab)


def _cast_block(w, l, axis, chip_arr, *, name):
    _, Ks, Ns = w.shape
    tk = _pick(Ks, (512, 352, 256, 128))
    nk = Ks // tk
    full = (Ks * N_CHIPS, Ns) if axis == 0 else (Ks, Ns * N_CHIPS)

    def body(p_ref, w_ref, o_ref):
        o_ref[...] = w_ref[...].astype(BF16)

    if axis == 0:
        out_spec = pl.BlockSpec((tk, Ns), lambda i, p: (p[0] * nk + i, 0))
    else:
        out_spec = pl.BlockSpec((tk, Ns), lambda i, p: (i, p[0]))
    grid_spec = pltpu.PrefetchScalarGridSpec(
        num_scalar_prefetch=1, grid=(nk,),
        in_specs=[pl.BlockSpec((None, tk, Ns), lambda i, p: (l, i, 0))], out_specs=out_spec)
    return pl.pallas_call(
        body, name=name, out_shape=jax.ShapeDtypeStruct(full, BF16), grid_spec=grid_spec,
        compiler_params=pltpu.CompilerParams(dimension_semantics=("arbitrary",),
                                             vmem_limit_bytes=_vmem(4 * tk * Ns * 6)),
    )(chip_arr, w)


def _adamw_math(w, g, m, v):
    m = ADAM_B1 * m + (1.0 - ADAM_B1) * g
    v = ADAM_B2 * v + (1.0 - ADAM_B2) * (g * g)
    m_hat = m / (1.0 - ADAM_B1 ** ADAM_STEP)
    v_hat = v / (1.0 - ADAM_B2 ** ADAM_STEP)
    delta = -ADAM_LR * (m_hat / (jnp.sqrt(v_hat) + ADAM_EPS) + ADAM_WD * w)
    return delta, m, v


def _adamw_layer(w, m, v, g, l, outs, *, name, after=()):
    _, K, N = w.shape
    tk = _pick(K, (512, 352, 256, 128)) if N <= 1024 else _pick(K, (256, 176, 128))
    n_after = len(after)

    def body(w_ref, m_ref, v_ref, g_ref, *rest):
        go_ref, d_ref, mo_ref, vo_ref = rest[4 + n_after:]
        gv = g_ref[...]
        d, mn, vn = _adamw_math(w_ref[...], gv, m_ref[...], v_ref[...])
        go_ref[...] = gv
        d_ref[...] = d
        mo_ref[...] = mn
        vo_ref[...] = vn

    layer = pl.BlockSpec((None, tk, N), lambda i: (l, i, 0))
    any_spec = pl.BlockSpec(memory_space=pl.ANY)
    sd = jax.ShapeDtypeStruct(w.shape, F32)
    return pl.pallas_call(
        body, name=name, out_shape=[sd, sd, sd, sd], grid=(K // tk,),
        in_specs=[layer, layer, layer, pl.BlockSpec((tk, N), lambda i: (i, 0))] + [any_spec] * (4 + n_after),
        out_specs=[layer] * 4, input_output_aliases={4: 0, 5: 1, 6: 2, 7: 3},
        compiler_params=pltpu.CompilerParams(dimension_semantics=("arbitrary",),
                                             vmem_limit_bytes=_vmem(2 * 8 * tk * N * 4 + 6 * tk * N * 4)),
    )(w, m, v, g, *outs, *after)


def _adamw_small(w, g, m, v, *, name):
    def body(w_ref, g_ref, m_ref, v_ref, d_ref, mo_ref, vo_ref):
        d, mn, vn = _adamw_math(w_ref[...], g_ref[...], m_ref[...], v_ref[...])
        d_ref[...] = d
        mo_ref[...] = mn
        vo_ref[...] = vn

    sd = jax.ShapeDtypeStruct(w.shape, F32)
    return pl.pallas_call(body, name=name, out_shape=[sd, sd, sd])(w, g, m, v)


def _my_place():
    return lax.axis_index("x"), lax.axis_index("y"), lax.axis_index("c")


def _peer_chips(x, y):
    return [(1 - x, y), (x, 1 - y), (1 - x, 1 - y)]


_HBM = pl.BlockSpec(memory_space=pltpu.HBM)
_SEM = pl.BlockSpec(memory_space=pltpu.SEMAPHORE)
_EFFECT = pltpu.SideEffectType.DATAFLOW_SIDE_EFFECTING


def _split_start(name, bufs, n_copies, make_copies, after):
    nb = len(bufs)

    def body(*refs):
        send_sems, recv_sems = refs[nb + 1], refs[nb + 2]
        token = refs[2 * nb + 3]
        sends, _ = make_copies(refs[:nb], send_sems, recv_sems)
        for cp in sends:
            cp.start()
        token[...] = jnp.zeros_like(token)

    res = pl.pallas_call(
        body, name=name,
        out_shape=(pltpu.SemaphoreType.DMA((n_copies,)), pltpu.SemaphoreType.DMA((n_copies,)),
                   *[pltpu.HBM(b.shape, b.dtype) for b in bufs], jax.ShapeDtypeStruct((8, LANES), F32)),
        in_specs=[_HBM] * nb + [pl.BlockSpec(memory_space=pl.ANY)],
        out_specs=(_SEM, _SEM, *[_HBM] * nb, pl.BlockSpec(memory_space=pltpu.VMEM)),
        input_output_aliases={k: 2 + k for k in range(nb)},
        compiler_params=pltpu.CompilerParams(has_side_effects=_EFFECT),
    )(*[pltpu.with_memory_space_constraint(b, pltpu.HBM) for b in bufs],
      after[0] if isinstance(after, (list, tuple)) else after)
    return res[0], res[1], list(res[2:2 + nb]), res[2 + nb]


def _split_wait(name, bufs, sems, make_copies, after):
    nb = len(bufs)
    after = list(after) if isinstance(after, (list, tuple)) else [after]

    def body(*refs):
        send_sems, recv_sems = refs[nb], refs[nb + 1]
        sends, recvs = make_copies(refs[:nb], send_sems, recv_sems)
        for cp in sends:
            cp.wait_send()
        for cp in recvs:
            cp.wait_recv()

    res = pl.pallas_call(
        body, name=name,
        out_shape=tuple(pltpu.HBM(b.shape, b.dtype) for b in bufs),
        in_specs=[_HBM] * nb + [_SEM, _SEM] + [pl.BlockSpec(memory_space=pl.ANY)] * len(after),
        out_specs=tuple([_HBM] * nb),
        input_output_aliases={k: k for k in range(nb)},
        compiler_params=pltpu.CompilerParams(has_side_effects=_EFFECT),
    )(*bufs, sems[0], sems[1], *after)
    return list(res)


def _remote(src, dst, send_sems, recv_sems, k, target):
    return pltpu.make_async_remote_copy(src_ref=src, dst_ref=dst, send_sem=send_sems.at[k],
                                        recv_sem=recv_sems.at[k], device_id=target, device_id_type=MESH)


def _ag_region(ref, axis, chip, half):
    K, N = ref.shape
    if axis == 0:
        hs = K // N_CHIPS // 2
        assert hs % 16 == 0
        return ref.at[pl.ds(pl.multiple_of((2 * chip + half) * hs, 16), hs), :]
    ns, hk = N // N_CHIPS, K // 2
    assert ns % LANES == 0 and hk % 16 == 0
    return ref.at[pl.ds(pl.multiple_of(half * hk, 16), hk), pl.ds(pl.multiple_of(chip * ns, LANES), ns)]


def _ag_copies(stage, axes):
    n = len(axes)

    def make(bufs, send_sems, recv_sems):
        x, y, c = _my_place()
        me = 2 * x + y
        sends, recvs = [], []
        for j, (px, py) in enumerate(_peer_chips(x, y)):
            other = 2 * px + py
            for w in range(n):
                k = j * n + w
                if stage == 1:
                    src, target = _ag_region(bufs[w], axes[w], me, c), (px, py, c)
                    land = _ag_region(bufs[w], axes[w], other, c)
                else:
                    src, target = _ag_region(bufs[w], axes[w], other, c), (x, y, 1 - c)
                    land = _ag_region(bufs[w], axes[w], other, 1 - c)
                sends.append(_remote(src, src, send_sems, recv_sems, k, target))
                recvs.append(_remote(land, land, send_sems, recv_sems, k, target))
        return sends, recvs

    return make


def _half_shape(shape, axis):
    K, N = shape
    return (K, N // 2) if axis == 0 else (K // 2, N)


def _core_half(ref, axis, half):
    K, N = ref.shape
    if axis == 0:
        return ref.at[:, pl.ds(pl.multiple_of(half * (N // 2), LANES), N // 2)]
    return ref.at[pl.ds(pl.multiple_of(half * (K // 2), 16), K // 2), :]


def _chip_block(ref, axis, chip):
    K, N = ref.shape
    if axis == 0:
        return ref.at[pl.ds(pl.multiple_of(chip * (K // N_CHIPS), 16), K // N_CHIPS), :]
    return ref.at[:, pl.ds(pl.multiple_of(chip * (N // N_CHIPS), LANES), N // N_CHIPS)]


def _rs_sibling_copies(axes):
    n = len(axes)

    def make(bufs, send_sems, recv_sems):
        x, y, c = _my_place()
        sends = [_remote(_core_half(bufs[w], axes[w], 1 - c), bufs[n + w], send_sems, recv_sems, w, (x, y, 1 - c))
                 for w in range(n)]
        recvs = [_remote(bufs[n + w], bufs[n + w], send_sems, recv_sems, w, (x, y, 1 - c)) for w in range(n)]
        return sends, recvs

    return make


def _rs_chip_copies(axes):
    n = len(axes)

    def make(bufs, send_sems, recv_sems):
        x, y, c = _my_place()
        sends, recvs = [], []
        for j, (px, py) in enumerate(_peer_chips(x, y)):
            for w in range(n):
                k = j * n + w
                sends.append(_remote(_chip_block(bufs[w], axes[w], 2 * px + py), bufs[n + w].at[j],
                                     send_sems, recv_sems, k, (px, py, c)))
                recvs.append(_remote(bufs[n + w].at[j], bufs[n + w].at[j], send_sems, recv_sems, k, (px, py, c)))
        return sends, recvs

    return make


def _rs_fill_copies(axes):
    n = len(axes)

    def make(bufs, send_sems, recv_sems):
        x, y, c = _my_place()
        sends = [_remote(_core_half(bufs[w], axes[w], c), _core_half(bufs[w], axes[w], c),
                         send_sems, recv_sems, w, (x, y, 1 - c)) for w in range(n)]
        recvs = [_remote(_core_half(bufs[w], axes[w], 1 - c), _core_half(bufs[w], axes[w], 1 - c),
                         send_sems, recv_sems, w, (x, y, 1 - c)) for w in range(n)]
        return sends, recvs

    return make


def _chip_sum(g, r, axis, place, *, name):
    hk, hn = r.shape
    bk, bn = (hk // N_CHIPS, hn) if axis == 0 else (hk, hn // N_CHIPS)
    tk = _pick(bk, (512, 352, 256, 128))
    nk = bk // tk

    def body(p_ref, g_ref, r_ref, b_ref, own_ref):
        s = g_ref[...].astype(F32) + r_ref[...].astype(F32)
        b_ref[...] = s.astype(BF16)

        @pl.when(pl.program_id(1) == p_ref[0])
        def _():
            own_ref[...] = s

    if axis == 0:
        g_spec = pl.BlockSpec((tk, bn), lambda i, j, p: (j * nk + i, p[1]))
        r_spec = pl.BlockSpec((tk, bn), lambda i, j, p: (j * nk + i, 0))
    else:
        g_spec = pl.BlockSpec((tk, bn), lambda i, j, p: (p[1] * nk + i, j))
        r_spec = pl.BlockSpec((tk, bn), lambda i, j, p: (i, j))
    grid_spec = pltpu.PrefetchScalarGridSpec(
        num_scalar_prefetch=1, grid=(nk, N_CHIPS), in_specs=[g_spec, r_spec],
        out_specs=[r_spec, pl.BlockSpec((tk, bn), lambda i, j, p: (i, 0))])
    return pl.pallas_call(
        body, name=name,
        out_shape=[jax.ShapeDtypeStruct(r.shape, BF16), jax.ShapeDtypeStruct((bk, bn), F32)],
        grid_spec=grid_spec,
        compiler_params=pltpu.CompilerParams(dimension_semantics=("arbitrary", "arbitrary"),
                                             vmem_limit_bytes=_vmem(2 * tk * bn * 10 + 3 * tk * bn * 4)),
    )(place, g, r)


def _final_sum(own, recv, axis, place, *, name):
    _, bk, bn = recv.shape
    tk = _pick(bk, (256, 176, 128))
    nk = bk // tk

    def body(p_ref, o_ref, r_ref, out_ref):
        out_ref[...] = ((o_ref[...] + r_ref[0].astype(F32)) + r_ref[1].astype(F32)) + r_ref[2].astype(F32)

    own_spec = pl.BlockSpec((tk, bn), lambda i, p: (i, 0))
    if axis == 0:
        out_shape, out_spec = (bk, 2 * bn), pl.BlockSpec((tk, bn), lambda i, p: (i, p[1]))
    else:
        out_shape, out_spec = (2 * bk, bn), pl.BlockSpec((tk, bn), lambda i, p: (p[1] * nk + i, 0))
    grid_spec = pltpu.PrefetchScalarGridSpec(
        num_scalar_prefetch=1, grid=(nk,),
        in_specs=[own_spec, pl.BlockSpec((3, tk, bn), lambda i, p: (0, i, 0))], out_specs=out_spec)
    return pl.pallas_call(
        body, name=name, out_shape=jax.ShapeDtypeStruct(out_shape, F32), grid_spec=grid_spec,
        compiler_params=pltpu.CompilerParams(dimension_semantics=("arbitrary",),
                                             vmem_limit_bytes=_vmem(2 * tk * bn * 14 + 4 * tk * bn * 4)),
    )(place, own, recv)


def _allreduce_small(p, after=()):
    n_after = len(after)

    def body(*refs):
        p_ref = refs[0]
        o_ref, r0, r1, r2, send_sems, recv_sems = refs[1 + n_after:]
        x, y, c = _my_place()
        o_ref[...] = p_ref[...]
        for s, (peer, rbuf) in enumerate([((x, y, 1 - c), r0), ((1 - x, y, c), r1), ((x, 1 - y, c), r2)]):
            cp = pltpu.make_async_remote_copy(src_ref=o_ref, dst_ref=rbuf, send_sem=send_sems.at[s],
                                              recv_sem=recv_sems.at[s], device_id=peer, device_id_type=MESH)
            cp.start()
            cp.wait()
            o_ref[...] = o_ref[...] + rbuf[...]

    vm = pl.BlockSpec(memory_space=pltpu.VMEM)
    return pl.pallas_call(
        body, name="allreduce_small", out_shape=jax.ShapeDtypeStruct(p.shape, F32),
        in_specs=[vm] + [pl.BlockSpec(memory_space=pl.ANY)] * n_after, out_specs=vm,
        scratch_shapes=[pltpu.VMEM(p.shape, F32)] * 3 + [pltpu.SemaphoreType.DMA((3,))] * 2,
        compiler_params=pltpu.CompilerParams(vmem_limit_bytes=_vmem(6 * _nbytes(p.shape, F32))),
    )(p, *after)


_BUTTERFLY = (lambda x, y, c: (x, y, 1 - c), lambda x, y, c: (1 - x, y, c), lambda x, y, c: (x, 1 - y, c))


def _swap_copies(stage):
    def make(bufs, send_sems, recv_sems):
        target = _BUTTERFLY[stage](*_my_place())
        return ([_remote(bufs[0], bufs[1], send_sems, recv_sems, 0, target)],
                [_remote(bufs[1], bufs[1], send_sems, recv_sems, 0, target)])

    return make


def _add(a, b, *, name):
    def body(a_ref, b_ref, o_ref):
        o_ref[...] = a_ref[...] + b_ref[...]

    return pl.pallas_call(body, name=name, out_shape=jax.ShapeDtypeStruct(a.shape, a.dtype))(a, b)


def _pack_rows(parts):
    rows, metas = [], []
    for a in parts:
        flat = a.reshape(-1)
        nrow = -(-flat.shape[0] // LANES)
        nrow = -(-nrow // 8) * 8
        flat = jnp.pad(flat, (0, nrow * LANES - flat.shape[0]))
        rows.append(flat.reshape(nrow, LANES))
        metas.append((a.shape, nrow))
    return jnp.concatenate(rows, axis=0), metas


def _unpack_rows(packed, metas):
    out, r0 = [], 0
    for shape, nrow in metas:
        size = int(np.prod(shape))
        out.append(packed[r0:r0 + nrow].reshape(-1)[:size].reshape(shape))
        r0 += nrow
    return out


def kernel(x, positions, pre_mix_g, post_mix_g, pre_ffn_g, post_ffn_g, a_w_in, a_b_in, a_ln_g, a_ln_b, a_w_s, a_b_s, a_w_out, b_w_qkv, b_b_qkv, b_sinks, b_w_o, ffn_w_gu, ffn_w_down, loss_target, m_pre_mix_g, m_post_mix_g, m_pre_ffn_g, m_post_ffn_g, m_a_w_in, m_a_b_in, m_a_ln_g, m_a_ln_b, m_a_w_s, m_a_b_s, m_a_w_out, m_b_w_qkv, m_b_b_qkv, m_b_sinks, m_b_w_o, m_ffn_w_gu, m_ffn_w_down, v_pre_mix_g, v_post_mix_g, v_pre_ffn_g, v_post_ffn_g, v_a_w_in, v_a_b_in, v_a_ln_g, v_a_ln_b, v_a_w_s, v_a_b_s, v_a_w_out, v_b_w_qkv, v_b_b_qkv, v_b_sinks, v_b_w_o, v_ffn_w_gu, v_ffn_w_down):
    depth, D = pre_mix_g.shape
    xi, yi, ci = _my_place()
    chip = 2 * xi + yi
    place = jnp.stack([chip, ci]).astype(jnp.int32)

    stacked = {"a_w_in": (a_w_in, m_a_w_in, v_a_w_in), "a_w_out": (a_w_out, m_a_w_out, v_a_w_out),
               "b_w_qkv": (b_w_qkv, m_b_w_qkv, v_b_w_qkv), "b_w_o": (b_w_o, m_b_w_o, v_b_w_o),
               "ffn_w_gu": (ffn_w_gu, m_ffn_w_gu, v_ffn_w_gu), "ffn_w_down": (ffn_w_down, m_ffn_w_down, v_ffn_w_down)}
    cut = {"a_w_in": 1, "a_w_out": 0, "b_w_qkv": 1, "b_w_o": 0, "ffn_w_gu": 1, "ffn_w_down": 0}

    def layer_keys(i):
        mix = [("a_w_in", i // 2), ("a_w_out", i // 2)] if i % 2 == 0 else [("b_w_qkv", i // 2), ("b_w_o", i // 2)]
        return mix + [("ffn_w_gu", i), ("ffn_w_down", i)]

    def dep(a, toks):
        for t in toks:
            a = a + t[:1, :1]
        return a

    W = {}
    for i in range(depth):
        for nm, l in layer_keys(i):
            W[(nm, l)] = _cast_block(stacked[nm][0], l, cut[nm], place, name=f"cast_{nm}_{l}")

    def gather(tag, keys, after):
        axes = [cut[nm] for nm, _ in keys]
        for stage in (1, 2):
            ss, rs, bufs, tok = _split_start(f"ag{stage}_start_{tag}", [W[k] for k in keys], 3 * len(keys),
                                             _ag_copies(stage, axes), after)
            after = yield tok
            bufs = _split_wait(f"ag{stage}_wait_{tag}", bufs, (ss, rs), _ag_copies(stage, axes), after)
            W.update(zip(keys, bufs))
        yield None

    nq = b_b_qkv.shape[1]
    bq_full = jnp.zeros((b_b_qkv.shape[0], N_CHIPS * nq), F32)
    bq_full = lax.dynamic_update_slice(bq_full, jnp.where(ci == 0, b_b_qkv, 0.0), (0, chip * nq))
    bq_packed, bq_meta = _pack_rows([bq_full])
    bq_gathered = _allreduce_small(bq_packed)
    b_qkv_full = _unpack_rows(bq_gathered, bq_meta)[0]

    first = gather("0m", layer_keys(0)[:2], bq_gathered)
    tok = next(first)
    tok = first.send([tok] + [W[k] for i in range(depth) for k in layer_keys(i)[2 if i == 0 else 0:]])
    first.send(tok)

    h = x[0]
    target = loss_target[0]
    ctab, stab = _rope_tables(positions[0])
    q_width = W[("b_w_o", 0)].shape[0]
    kv_width = N_KV_HEADS * HEAD_DIM
    row = lambda a, i: a[i:i + 1]
    gains = {"pre_mix": pre_mix_g[:, None], "post_mix": post_mix_g[:, None], "pre_ffn": pre_ffn_g[:, None],
             "post_ffn": post_ffn_g[:, None]}
    gain = lambda which, i: (gains[which], i)

    saved = []
    hn = None
    for i in range(depth):
        j = i // 2
        s = {"h": h}
        ffn_w = None
        if i == 0:
            ffn_w = gather("0f", layer_keys(0)[2:], W[("a_w_out", 0)])
            toks = [next(ffn_w)]
            nxt = gather("1", layer_keys(1), toks[0])
            toks.append(next(nxt))
            hn = _rms_fwd(h, gain("pre_mix", i), out_dtype=BF16, after=toks, name=f"rms_pre_mix_{i}")
        elif i + 1 < depth:
            nxt = gather(str(i + 1), layer_keys(i + 1), h)
            toks = [next(nxt)]
        else:
            toks = []
        s["hn"] = hn
        if i % 2 == 0:
            pre = _matmul(hn, W[("a_w_in", j)], mode="nn", bias=row(a_b_in, j), out_dtype=F32, after=toks,
                          name=f"gmlp_in_{i}")
            gated = _sgu_fwd(pre, row(a_ln_g, j), row(a_ln_b, j), a_w_s[j], a_b_s[j].T, name=f"sgu_fwd_{i}")
            mix = _matmul(gated, W[("a_w_out", j)], mode="nn", out_dtype=F32, name=f"gmlp_out_{i}")
            s.update(pre=pre, gated=gated)
        else:
            qkv = _matmul(hn, W[("b_w_qkv", j)], mode="nn", bias=row(b_qkv_full, j), out_dtype=F32, after=toks,
                          name=f"attn_qkv_{i}")
            qr, kr, vr = _rope_fwd(qkv, ctab, stab, q_width=q_width, kv_width=kv_width, name=f"rope_fwd_{i}")
            o = _attn_fwd(qr, kr, vr, row(b_sinks, j), name=f"attn_fwd_{i}")
            mix = _matmul(o, W[("b_w_o", j)], mode="nn", out_dtype=F32, name=f"attn_o_{i}")
            s.update(qr=qr, kr=kr, vr=vr, o=o)
        s["mix"] = mix
        toks = [ffn_w.send(mix)] if ffn_w else []
        h1, fn = _rms_res_norm(h, mix, gain("post_mix", i), gain("pre_ffn", i), after=toks, name=f"rms_post_mix_{i}")
        if ffn_w:
            ffn_w.send(h1)
        s["h1"] = h1
        g_pre, u_pre, act = _ffn_up(fn, W[("ffn_w_gu", i)][None], 0, name=f"ffn_up_{i}")
        f = _matmul(act, W[("ffn_w_down", i)], mode="nn", out_dtype=F32, name=f"ffn_down_{i}")
        if i + 1 < depth:
            toks = [nxt.send(f)]
            h, hn = _rms_res_norm(h1, f, gain("post_ffn", i), gain("pre_mix", i + 1), after=toks,
                                  name=f"rms_post_ffn_{i}")
            nxt.send(h)
        else:
            h = _rms_res(h1, f, gain("post_ffn", i), name=f"rms_post_ffn_{i}")
        s.update(fn=fn, g_pre=g_pre, u_pre=u_pre, act=act, f=f)
        saved.append(s)

    dh, df, loss_part, g_last = _loss_and_grad(h, target, saved[-1]["f"], gain("post_ffn", depth - 1), name="loss")

    big_out = {nm: tuple(lax.empty(w.shape, F32) for _ in range(4)) for nm, (w, _, _) in stacked.items()}

    def reduce_group(i, keys, grads):
        axes = [cut[nm] for nm, _ in keys]
        n = len(keys)
        lands = [lax.empty(_half_shape(g.shape, ax), BF16) for g, ax in zip(grads, axes)]
        ss, rs, bufs, tok = _split_start(f"rs_sibling_start_{i}", list(grads) + lands, n, _rs_sibling_copies(axes),
                                         place)
        after = yield tok
        bufs = _split_wait(f"rs_sibling_wait_{i}", bufs, (ss, rs), _rs_sibling_copies(axes), after)
        sums = [_chip_sum(bufs[w], bufs[n + w], axes[w], place, name=f"chip_sum_{keys[w][0]}_{keys[w][1]}")
                for w in range(n)]
        lands = [lax.empty((3,) + own.shape, BF16) for _, own in sums]
        ss, rs, bufs, tok = _split_start(f"rs_chip_start_{i}", [sb for sb, _ in sums] + lands, 3 * n,
                                         _rs_chip_copies(axes), place)
        after = yield tok
        bufs = _split_wait(f"rs_chip_wait_{i}", bufs, (ss, rs), _rs_chip_copies(axes), after)
        blocks = [_final_sum(sums[w][1], bufs[n + w], axes[w], place, name=f"final_sum_{keys[w][0]}_{keys[w][1]}")
                  for w in range(n)]
        ss, rs, bufs, tok = _split_start(f"rs_fill_start_{i}", blocks, n, _rs_fill_copies(axes), place)
        after = yield tok
        blocks = _split_wait(f"rs_fill_wait_{i}", bufs, (ss, rs), _rs_fill_copies(axes), after)
        updates.extend(zip(keys, blocks))
        yield None

    updates = []

    def adamw(items, after):
        for (nm, l), g in items:
            w, m, v = stacked[nm]
            big_out[nm] = tuple(_adamw_layer(w, m, v, g, l, big_out[nm], after=after, name=f"adamw_{nm}_{l}"))
        return [big_out[nm][1] for nm in dict.fromkeys(nm for (nm, _), _ in items)]

    reducing = []

    def advance(after, newest_only=False):
        toks = []
        for gen in (reducing[-1:] if newest_only else list(reducing)):
            tok = gen.send(after)
            if tok is None:
                reducing.remove(gen)
            else:
                toks.append(tok)
        return toks

    small = {}
    g_pre_mix, g_post_mix, g_pre_ffn, g_post_ffn = [None] * depth, [None] * depth, [None] * depth, [None] * depth
    g_post_ffn[depth - 1] = g_last
    toks = []
    early = []
    for i in reversed(range(depth)):
        j = i // 2
        s = saved[i]
        g_down = _matmul(s["act"], df, mode="tn", out_dtype=BF16, after=toks, name=f"ffn_down_dw_{i}")
        dg_, du_ = _ffn_down_dx(df, W[("ffn_w_down", i)][None], 0, s["g_pre"], s["u_pre"], g_down,
                                name=f"ffn_down_dx_{i}")
        g_gu = _matmul_pair(s["fn"], dg_, du_, mode="tn", out_dtype=BF16, name=f"ffn_gu_dw_{i}")
        dfn = _matmul_pair(dg_, W[("ffn_w_gu", i)], du_, mode="nt", out_dtype=F32, after=[g_gu],
                           name=f"ffn_gu_dx_{i}")
        toks = advance(dfn)
        if i == 0:
            gen = reduce_group("0f", layer_keys(0)[2:], [g_gu, g_down])
            toks.append(next(gen))
            reducing.append(gen)
        dh1, dmix, g_pre_ffn[i], g_post_mix[i] = _rms_bwd_chain(
            s["h1"], gain("pre_ffn", i), dfn, dh, s["mix"], gain("post_mix", i), after=toks,
            name=f"rms_ffn_mix_bwd_{i}")
        if i % 2 == 0:
            g_out = _matmul(s["gated"], dmix, mode="tn", out_dtype=BF16, name=f"gmlp_out_dw_{i}")
            dgated = _matmul(dmix, W[("a_w_out", j)], mode="nt", out_dtype=BF16, after=[g_out],
                             name=f"gmlp_out_dx_{i}")
            toks = advance(dgated, newest_only=True) if i == 0 else []
            dpre, dws, dbsT, dlng, dlnb, dbin = _sgu_bwd(s["pre"], dgated, dep(row(a_ln_g, j), toks), row(a_ln_b, j),
                                                         a_w_s[j], a_b_s[j].T, name=f"sgu_bwd_{i}")
            small[("a_w_s", j)] = dws
            small[("a_b_s", j)] = dbsT.T
            small[("a_ln_g", j)] = dlng
            small[("a_ln_b", j)] = dlnb
            small[("a_b_in", j)] = dbin
            g_in = _matmul(s["hn"], dpre, mode="tn", out_dtype=BF16, name=f"gmlp_in_dw_{i}")
            if i == 0:
                last = reduce_group("0m", layer_keys(0)[:2], [g_in, g_out])
                early = [next(last)]
            dhn = _matmul(dpre, W[("a_w_in", j)], mode="nt", out_dtype=F32, after=[g_in] + early,
                          name=f"gmlp_in_dx_{i}")
        else:
            g_out = _matmul(s["o"], dmix, mode="tn", out_dtype=BF16, name=f"attn_o_dw_{i}")
            do = _matmul(dmix, W[("b_w_o", j)], mode="nt", out_dtype=BF16, after=[g_out], name=f"attn_o_dx_{i}")
            dq, dkp, dkc, dvp, dvc, dsk = _attn_bwd(s["qr"], s["kr"], s["vr"], row(b_sinks, j), do,
                                                    name=f"attn_bwd_{i}")
            dqkv, dbq = _rope_bwd(dq, dkp, dkc, dvp, dvc, ctab, stab, name=f"rope_bwd_{i}")
            small[("b_sinks", j)] = dsk[:, :b_sinks.shape[1]]
            small[("b_b_qkv", j)] = dbq
            g_in = _matmul(s["hn"], dqkv, mode="tn", out_dtype=BF16, name=f"attn_qkv_dw_{i}")
            if i == 0:
                last = reduce_group("0m", layer_keys(0)[:2], [g_in, g_out])
                early = [next(last)]
            dhn = _matmul(dqkv, W[("b_w_qkv", j)], mode="nt", out_dtype=F32, after=[g_in] + early,
                          name=f"attn_qkv_dx_{i}")
        toks = advance(dhn)
        if i > 0:
            dh, df, g_pre_mix[i], g_post_ffn[i - 1] = _rms_bwd_chain(
                s["h"], gain("pre_mix", i), dhn, dh1, saved[i - 1]["f"], gain("post_ffn", i - 1), after=toks,
                name=f"rms_mix_ffn_bwd_{i}")
            gen = reduce_group(str(i), layer_keys(i), [g_in, g_out, g_gu, g_down])
            toks = [next(gen)] + advance(dh)
            reducing.append(gen)
        else:
            toks.append(last.send(dhn))
            dh, g_pre_mix[i] = _rms_bwd(s["h"], gain("pre_mix", i), dhn, dh1, out_dtype=F32, after=toks,
                                        name=f"rms_pre_mix_bwd_{i}")
            advance(dh)
    grad_x = dh[None]
    assert not reducing

    n_a, n_b = a_b_in.shape[0], b_sinks.shape[0]
    stack = lambda key, n: jnp.concatenate([small[(key, j)] for j in range(n)], axis=0)
    small_parts = [
        jnp.concatenate(g_pre_mix, axis=0), jnp.concatenate(g_post_mix, axis=0),
        jnp.concatenate(g_pre_ffn, axis=0), jnp.concatenate(g_post_ffn, axis=0),
        stack("a_b_in", n_a), stack("a_ln_g", n_a), stack("a_ln_b", n_a),
        jnp.stack([small[("a_w_s", j)] for j in range(n_a)]), jnp.stack([small[("a_b_s", j)] for j in range(n_a)]),
        stack("b_b_qkv", n_b), stack("b_sinks", n_b), loss_part,
    ]
    packed, metas = _pack_rows(small_parts)
    reduced = packed
    for stage in range(len(_BUTTERFLY)):
        ss, rs, bufs, tok = _split_start(f"butterfly_start_{stage}", [reduced, lax.empty(packed.shape, F32)], 1,
                                         _swap_copies(stage), place)
        done = adamw(updates[stage::len(_BUTTERFLY)], [tok])
        bufs = _split_wait(f"butterfly_wait_{stage}", bufs, (ss, rs), _swap_copies(stage), done or [tok])
        reduced = _add(bufs[0], bufs[1], name=f"butterfly_add_{stage}")
    updates = []
    while last.send(reduced) is not None:
        pass
    adamw(updates, [])
    red = _unpack_rows(reduced, metas)
    (gr_pre_mix, gr_post_mix, gr_pre_ffn, gr_post_ffn, gr_b_in, gr_ln_g, gr_ln_b, gr_w_s, gr_b_s,
     gr_b_qkv_full, gr_sinks, loss_sum) = red
    loss = loss_sum[0, 0]
    gr_b_qkv = lax.dynamic_slice(gr_b_qkv_full, (0, chip * nq), (gr_b_qkv_full.shape[0], nq))

    grads = {"pre_mix_g": gr_pre_mix, "post_mix_g": gr_post_mix, "pre_ffn_g": gr_pre_ffn, "post_ffn_g": gr_post_ffn,
             "a_b_in": gr_b_in, "a_ln_g": gr_ln_g, "a_ln_b": gr_ln_b, "a_w_s": gr_w_s, "a_b_s": gr_b_s,
             "b_b_qkv": gr_b_qkv, "b_sinks": gr_sinks}
    weights = {"pre_mix_g": (pre_mix_g, m_pre_mix_g, v_pre_mix_g), "post_mix_g": (post_mix_g, m_post_mix_g, v_post_mix_g),
               "pre_ffn_g": (pre_ffn_g, m_pre_ffn_g, v_pre_ffn_g), "post_ffn_g": (post_ffn_g, m_post_ffn_g, v_post_ffn_g),
               "a_b_in": (a_b_in, m_a_b_in, v_a_b_in), "a_ln_g": (a_ln_g, m_a_ln_g, v_a_ln_g),
               "a_ln_b": (a_ln_b, m_a_ln_b, v_a_ln_b), "a_w_s": (a_w_s, m_a_w_s, v_a_w_s), "a_b_s": (a_b_s, m_a_b_s, v_a_b_s),
               "b_b_qkv": (b_b_qkv, m_b_b_qkv, v_b_b_qkv), "b_sinks": (b_sinks, m_b_sinks, v_b_sinks)}
    order = ["pre_mix_g", "post_mix_g", "pre_ffn_g", "post_ffn_g", "a_w_in", "a_b_in", "a_ln_g", "a_ln_b", "a_w_s",
             "a_b_s", "a_w_out", "b_w_qkv", "b_b_qkv", "b_sinks", "b_w_o", "ffn_w_gu", "ffn_w_down"]
    deltas, new_m, new_v = {}, {}, {}
    for nm in order:
        if nm in big_out:
            grads[nm], deltas[nm], new_m[nm], new_v[nm] = big_out[nm]
        else:
            w, m, v = weights[nm]
            deltas[nm], new_m[nm], new_v[nm] = _adamw_small(w, grads[nm], m, v, name="adamw_" + nm)
    return (loss, grad_x, *[grads[nm] for nm in order], *[deltas[nm] for nm in order],
            *[new_m[nm] for nm in order], *[new_v[nm] for nm in order])
```
